```python
import math
import jax, jax.numpy as jnp
from jax import lax
import numpy as np

D_MODEL = 1024
BATCH = 16
SEQ = 2048
DEPTH = 2

D_MIX = D_MODEL
D_GMLP = D_MIX // 2
D_MLA = D_MIX - D_GMLP
GMLP_GROUPS = 8
GMLP_GROUP_DIM = D_GMLP // GMLP_GROUPS
CHUNK = 128
MLA_HEADS = 8
QK_NOPE_DIM = 64
QK_ROPE_DIM = 32
V_HEAD_DIM = D_MLA // MLA_HEADS
Q_RANK = D_MODEL // 4
KV_RANK = D_MODEL // 8
ROPE_THETA = 10000.0
Q_BLOCK = 128
D_FF = 4 * D_MODEL
N_MOD = 6
EPS = 1e-6
D_IN = 2 * D_GMLP + Q_RANK + KV_RANK + QK_ROPE_DIM

kernel_name = "hybrid_gmlp_mla_adaln_block"


def rmsnorm(x, g):
    xf = x.astype(jnp.float32)
    y = xf * lax.rsqrt(jnp.mean(xf * xf, axis=-1, keepdims=True) + EPS)
    return (y * g.astype(jnp.float32)).astype(x.dtype)


def layernorm_noaffine(x):
    xf = x.astype(jnp.float32)
    mu = jnp.mean(xf, axis=-1, keepdims=True)
    d = xf - mu
    y = d * lax.rsqrt(jnp.mean(d * d, axis=-1, keepdims=True) + EPS)
    return y.astype(x.dtype)


def rope_tables(positions, dim):
    freqs = ROPE_THETA ** (-jnp.arange(0, dim, 2, dtype=jnp.float32) / dim)
    ang = positions.astype(jnp.float32)[..., None] * freqs
    return jnp.cos(ang), jnp.sin(ang)


def apply_rope(x, cos, sin):
    half = x.shape[-1] // 2
    x1, x2 = x[..., :half], x[..., half:]
    cos = cos.astype(x.dtype)
    sin = sin.astype(x.dtype)
    return jnp.concatenate([x1 * cos - x2 * sin, x1 * sin + x2 * cos], axis=-1)


def gmlp_mixer(u, v, w_s, b_s):
    B, S, G, Dg = u.shape
    n_chunks = S // CHUNK
    u = jax.nn.gelu(u)
    v = layernorm_noaffine(jax.nn.gelu(v))
    causal = jnp.tril(jnp.ones((CHUNK, CHUNK), dtype=bool))
    w = jnp.where(causal[None], w_s, 0.0)
    vc = v.reshape(B, n_chunks, CHUNK, G, Dg)
    mixed = jnp.einsum('gts,bcsgd->bctgd', w, vc) + b_s.T[None, None, :, :, None]
    return u * mixed.reshape(B, S, G, Dg)


def mla_mixer(q_lat, kv_lat, k_rope_raw, cos, sin, g_q, g_kv, w_uq, w_ukv):
    B, S, _ = q_lat.shape
    c_q = rmsnorm(q_lat, g_q)
    q = (c_q @ w_uq).reshape(B, S, MLA_HEADS, QK_NOPE_DIM + QK_ROPE_DIM)
    q_nope, q_rope = q[..., :QK_NOPE_DIM], q[..., QK_NOPE_DIM:]
    q_rope = apply_rope(q_rope, cos[:, :, None, :], sin[:, :, None, :])
    c_kv = rmsnorm(kv_lat, g_kv)
    kv = (c_kv @ w_ukv).reshape(B, S, MLA_HEADS, QK_NOPE_DIM + V_HEAD_DIM)
    k_nope, v = kv[..., :QK_NOPE_DIM], kv[..., QK_NOPE_DIM:]
    k_rope = apply_rope(k_rope_raw, cos, sin)
    scale = (QK_NOPE_DIM + QK_ROPE_DIM) ** -0.5
    outs = []
    for i in range(S // Q_BLOCK):
        q0 = i * Q_BLOCK
        kend = q0 + Q_BLOCK
        s = (jnp.einsum('bqhd,bkhd->bhqk', q_nope[:, q0:kend], k_nope[:, :kend])
             + jnp.einsum('bqhr,bkr->bhqk', q_rope[:, q0:kend], k_rope[:, :kend]))
        s = s.astype(jnp.float32) * scale
        qpos = q0 + jnp.arange(Q_BLOCK)
        kpos = jnp.arange(kend)
        s = jnp.where(kpos[None, :] <= qpos[:, None], s, -1e30)
        p = jax.nn.softmax(s, axis=-1).astype(v.dtype)
        outs.append(jnp.einsum('bhqk,bkhd->bqhd', p, v[:, :kend]))
    o = jnp.concatenate(outs, axis=1)
    return o.reshape(B, S, MLA_HEADS * V_HEAD_DIM)


def _fwd_setup_inputs(seed: int = 0) -> dict:
    key = jax.random.key(seed)
    ks = jax.random.split(key, 24)
    f32 = jnp.float32

    def nrm(k, shape, scale):
        return jax.random.normal(k, shape, f32) * scale

    def gain(k, shape):
        return 1.0 + 0.02 * jax.random.normal(k, shape, f32)

    x = jax.random.normal(ks[0], (BATCH, SEQ, D_MODEL), f32)
    c = jax.random.normal(ks[1], (BATCH, D_MODEL), f32)
    offset = jax.random.randint(ks[2], (BATCH, 1), 0, 1024, dtype=jnp.int32)
    positions = offset + jnp.arange(SEQ, dtype=jnp.int32)[None, :]
    return {
        "x": x,
        "c": c,
        "positions": positions,
        "w_ada": nrm(ks[3], (DEPTH, D_MODEL, N_MOD * D_MODEL), 0.02),
        "b_ada": nrm(ks[4], (DEPTH, N_MOD * D_MODEL), 0.01),
        "norm_mix_g": gain(ks[5], (DEPTH, D_MODEL)),
        "w_in": nrm(ks[6], (DEPTH, D_MODEL, D_IN), D_MODEL ** -0.5),
        "gmlp_ws": nrm(ks[7], (DEPTH, GMLP_GROUPS, CHUNK, CHUNK), CHUNK ** -0.5),
        "gmlp_bs": gain(ks[8], (DEPTH, GMLP_GROUPS, CHUNK)),
        "mla_q_norm_g": gain(ks[9], (DEPTH, Q_RANK)),
        "mla_kv_norm_g": gain(ks[10], (DEPTH, KV_RANK)),
        "mla_w_uq": nrm(ks[11], (DEPTH, Q_RANK, MLA_HEADS * (QK_NOPE_DIM + QK_ROPE_DIM)), Q_RANK ** -0.5),
        "mla_w_ukv": nrm(ks[12], (DEPTH, KV_RANK, MLA_HEADS * (QK_NOPE_DIM + V_HEAD_DIM)), KV_RANK ** -0.5),
        "out_norm_gmlp_g": gain(ks[13], (DEPTH, D_GMLP)),
        "out_norm_mla_g": gain(ks[14], (DEPTH, D_MLA)),
        "w_out": nrm(ks[15], (DEPTH, D_MIX, D_MODEL), D_MIX ** -0.5),
        "norm_ffn_g": gain(ks[16], (DEPTH, D_MODEL)),
        "w_ff1": nrm(ks[17], (DEPTH, D_MODEL, D_FF), D_MODEL ** -0.5),
        "w_ff2": nrm(ks[18], (DEPTH, D_FF, D_MODEL), D_FF ** -0.5),
        "final_norm_g": gain(ks[19], (D_MODEL,)),
    }


def _fwd_reference(x, c, positions, w_ada, b_ada, norm_mix_g, w_in, gmlp_ws, gmlp_bs,
              mla_q_norm_g, mla_kv_norm_g, mla_w_uq, mla_w_ukv, out_norm_gmlp_g,
              out_norm_mla_g, w_out, norm_ffn_g, w_ff1, w_ff2, final_norm_g):
    B, S, _ = x.shape
    cos, sin = rope_tables(positions, QK_ROPE_DIM)
    c_act = jax.nn.silu(c)
    split_pts = [D_GMLP, 2 * D_GMLP, 2 * D_GMLP + Q_RANK, 2 * D_GMLP + Q_RANK + KV_RANK]
    for l in range(DEPTH):
        mod = c_act @ w_ada[l] + b_ada[l]
        shift1, scale1, gate1, shift2, scale2, gate2 = jnp.split(mod[:, None, :], N_MOD, axis=-1)

        h = rmsnorm(x, norm_mix_g[l]) * (1.0 + scale1) + shift1
        z = h @ w_in[l]
        u, v, q_lat, kv_lat, k_rope_raw = jnp.split(z, split_pts, axis=-1)
        y_g = gmlp_mixer(u.reshape(B, S, GMLP_GROUPS, GMLP_GROUP_DIM),
                         v.reshape(B, S, GMLP_GROUPS, GMLP_GROUP_DIM),
                         gmlp_ws[l], gmlp_bs[l]).reshape(B, S, D_GMLP)
        y_a = mla_mixer(q_lat, kv_lat, k_rope_raw, cos, sin, mla_q_norm_g[l],
                        mla_kv_norm_g[l], mla_w_uq[l], mla_w_ukv[l])
        y = jnp.concatenate([rmsnorm(y_g, out_norm_gmlp_g[l]), rmsnorm(y_a, out_norm_mla_g[l])], axis=-1)
        x = x + gate1 * (y @ w_out[l])

        h = rmsnorm(x, norm_ffn_g[l]) * (1.0 + scale2) + shift2
        f = jnp.square(jax.nn.relu(h @ w_ff1[l])) @ w_ff2[l]
        x = x + gate2 * f
    return rmsnorm(x, final_norm_g)


import jax as _jax
import jax.numpy as _jnp

TWIN_FORMAT = 'train_step'
FWD_PARAMS = ['x', 'c', 'positions', 'w_ada', 'b_ada', 'norm_mix_g', 'w_in', 'gmlp_ws', 'gmlp_bs', 'mla_q_norm_g', 'mla_kv_norm_g', 'mla_w_uq', 'mla_w_ukv', 'out_norm_gmlp_g', 'out_norm_mla_g', 'w_out', 'norm_ffn_g', 'w_ff1', 'w_ff2', 'final_norm_g']
TWIN_WEIGHTS = ['w_ada', 'b_ada', 'norm_mix_g', 'w_in', 'gmlp_ws', 'gmlp_bs', 'mla_q_norm_g', 'mla_kv_norm_g', 'mla_w_uq', 'mla_w_ukv', 'out_norm_gmlp_g', 'out_norm_mla_g', 'w_out', 'norm_ffn_g', 'w_ff1', 'w_ff2', 'final_norm_g']
TWIN_DIFF_INPUT = 'x'
TWIN_INPUTS = ['x', 'c', 'positions', 'w_ada', 'b_ada', 'norm_mix_g', 'w_in', 'gmlp_ws', 'gmlp_bs', 'mla_q_norm_g', 'mla_kv_norm_g', 'mla_w_uq', 'mla_w_ukv', 'out_norm_gmlp_g', 'out_norm_mla_g', 'w_out', 'norm_ffn_g', 'w_ff1', 'w_ff2', 'final_norm_g', 'loss_target', 'm_w_ada', 'm_b_ada', 'm_norm_mix_g', 'm_w_in', 'm_gmlp_ws', 'm_gmlp_bs', 'm_mla_q_norm_g', 'm_mla_kv_norm_g', 'm_mla_w_uq', 'm_mla_w_ukv', 'm_out_norm_gmlp_g', 'm_out_norm_mla_g', 'm_w_out', 'm_norm_ffn_g', 'm_w_ff1', 'm_w_ff2', 'm_final_norm_g', 'v_w_ada', 'v_b_ada', 'v_norm_mix_g', 'v_w_in', 'v_gmlp_ws', 'v_gmlp_bs', 'v_mla_q_norm_g', 'v_mla_kv_norm_g', 'v_mla_w_uq', 'v_mla_w_ukv', 'v_out_norm_gmlp_g', 'v_out_norm_mla_g', 'v_w_out', 'v_norm_ffn_g', 'v_w_ff1', 'v_w_ff2', 'v_final_norm_g']
TWIN_OUTPUTS = ['loss', 'grad_x', 'grad_w_ada', 'grad_b_ada', 'grad_norm_mix_g', 'grad_w_in', 'grad_gmlp_ws', 'grad_gmlp_bs', 'grad_mla_q_norm_g', 'grad_mla_kv_norm_g', 'grad_mla_w_uq', 'grad_mla_w_ukv', 'grad_out_norm_gmlp_g', 'grad_out_norm_mla_g', 'grad_w_out', 'grad_norm_ffn_g', 'grad_w_ff1', 'grad_w_ff2', 'grad_final_norm_g', 'delta_w_ada', 'delta_b_ada', 'delta_norm_mix_g', 'delta_w_in', 'delta_gmlp_ws', 'delta_gmlp_bs', 'delta_mla_q_norm_g', 'delta_mla_kv_norm_g', 'delta_mla_w_uq', 'delta_mla_w_ukv', 'delta_out_norm_gmlp_g', 'delta_out_norm_mla_g', 'delta_w_out', 'delta_norm_ffn_g', 'delta_w_ff1', 'delta_w_ff2', 'delta_final_norm_g', 'new_m_w_ada', 'new_m_b_ada', 'new_m_norm_mix_g', 'new_m_w_in', 'new_m_gmlp_ws', 'new_m_gmlp_bs', 'new_m_mla_q_norm_g', 'new_m_mla_kv_norm_g', 'new_m_mla_w_uq', 'new_m_mla_w_ukv', 'new_m_out_norm_gmlp_g', 'new_m_out_norm_mla_g', 'new_m_w_out', 'new_m_norm_ffn_g', 'new_m_w_ff1', 'new_m_w_ff2', 'new_m_final_norm_g', 'new_v_w_ada', 'new_v_b_ada', 'new_v_norm_mix_g', 'new_v_w_in', 'new_v_gmlp_ws', 'new_v_gmlp_bs', 'new_v_mla_q_norm_g', 'new_v_mla_kv_norm_g', 'new_v_mla_w_uq', 'new_v_mla_w_ukv', 'new_v_out_norm_gmlp_g', 'new_v_out_norm_mla_g', 'new_v_w_out', 'new_v_norm_ffn_g', 'new_v_w_ff1', 'new_v_w_ff2', 'new_v_final_norm_g']
TWIN_LEAF_KINDS = {'loss': 'loss', 'grad_x': 'grad_x', 'grad_w_ada': 'grad_w', 'grad_b_ada': 'grad_w', 'grad_norm_mix_g': 'grad_w', 'grad_w_in': 'grad_w', 'grad_gmlp_ws': 'grad_w', 'grad_gmlp_bs': 'grad_w', 'grad_mla_q_norm_g': 'grad_w', 'grad_mla_kv_norm_g': 'grad_w', 'grad_mla_w_uq': 'grad_w', 'grad_mla_w_ukv': 'grad_w', 'grad_out_norm_gmlp_g': 'grad_w', 'grad_out_norm_mla_g': 'grad_w', 'grad_w_out': 'grad_w', 'grad_norm_ffn_g': 'grad_w', 'grad_w_ff1': 'grad_w', 'grad_w_ff2': 'grad_w', 'grad_final_norm_g': 'grad_w', 'delta_w_ada': 'delta_w', 'delta_b_ada': 'delta_w', 'delta_norm_mix_g': 'delta_w', 'delta_w_in': 'delta_w', 'delta_gmlp_ws': 'delta_w', 'delta_gmlp_bs': 'delta_w', 'delta_mla_q_norm_g': 'delta_w', 'delta_mla_kv_norm_g': 'delta_w', 'delta_mla_w_uq': 'delta_w', 'delta_mla_w_ukv': 'delta_w', 'delta_out_norm_gmlp_g': 'delta_w', 'delta_out_norm_mla_g': 'delta_w', 'delta_w_out': 'delta_w', 'delta_norm_ffn_g': 'delta_w', 'delta_w_ff1': 'delta_w', 'delta_w_ff2': 'delta_w', 'delta_final_norm_g': 'delta_w', 'new_m_w_ada': 'new_m', 'new_m_b_ada': 'new_m', 'new_m_norm_mix_g': 'new_m', 'new_m_w_in': 'new_m', 'new_m_gmlp_ws': 'new_m', 'new_m_gmlp_bs': 'new_m', 'new_m_mla_q_norm_g': 'new_m', 'new_m_mla_kv_norm_g': 'new_m', 'new_m_mla_w_uq': 'new_m', 'new_m_mla_w_ukv': 'new_m', 'new_m_out_norm_gmlp_g': 'new_m', 'new_m_out_norm_mla_g': 'new_m', 'new_m_w_out': 'new_m', 'new_m_norm_ffn_g': 'new_m', 'new_m_w_ff1': 'new_m', 'new_m_w_ff2': 'new_m', 'new_m_final_norm_g': 'new_m', 'new_v_w_ada': 'new_v', 'new_v_b_ada': 'new_v', 'new_v_norm_mix_g': 'new_v', 'new_v_w_in': 'new_v', 'new_v_gmlp_ws': 'new_v', 'new_v_gmlp_bs': 'new_v', 'new_v_mla_q_norm_g': 'new_v', 'new_v_mla_kv_norm_g': 'new_v', 'new_v_mla_w_uq': 'new_v', 'new_v_mla_w_ukv': 'new_v', 'new_v_out_norm_gmlp_g': 'new_v', 'new_v_out_norm_mla_g': 'new_v', 'new_v_w_out': 'new_v', 'new_v_norm_ffn_g': 'new_v', 'new_v_w_ff1': 'new_v', 'new_v_w_ff2': 'new_v', 'new_v_final_norm_g': 'new_v'}


def _forward(args):
    return _fwd_reference(*[args[k] for k in FWD_PARAMS])


def _output_shape():
    out = _jax.eval_shape(lambda: _forward(_fwd_setup_inputs(0)))
    return out.shape, out.dtype

N_MICROBATCH = 1
ADAM_LR = 0.001
ADAM_B1 = 0.9
ADAM_B2 = 0.999
ADAM_EPS = 1e-08
ADAM_WD = 0.01
ADAM_STEP = 10
PER_EXAMPLE_BATCH_AXIS = {'x': 0, 'c': 0, 'positions': 0, 'loss_target': 0}
SHARED_INPUTS = []
_WEIGHT_DTYPES = {'w_ada': _jnp.float32, 'b_ada': _jnp.float32, 'norm_mix_g': _jnp.float32, 'w_in': _jnp.float32, 'gmlp_ws': _jnp.float32, 'gmlp_bs': _jnp.float32, 'mla_q_norm_g': _jnp.float32, 'mla_kv_norm_g': _jnp.float32, 'mla_w_uq': _jnp.float32, 'mla_w_ukv': _jnp.float32, 'out_norm_gmlp_g': _jnp.float32, 'out_norm_mla_g': _jnp.float32, 'w_out': _jnp.float32, 'norm_ffn_g': _jnp.float32, 'w_ff1': _jnp.float32, 'w_ff2': _jnp.float32, 'final_norm_g': _jnp.float32}
MOMENT_SCALE = {'w_ada': 1.034025e-01, 'b_ada': 1.804504e-01, 'norm_mix_g': 6.038706e-02, 'w_in': 6.623186e-02, 'gmlp_ws': 2.386503e-02, 'gmlp_bs': 3.498244e-02, 'mla_q_norm_g': 3.279570e-02, 'mla_kv_norm_g': 1.574236e-01, 'mla_w_uq': 1.893423e-02, 'mla_w_ukv': 5.791160e-02, 'out_norm_gmlp_g': 6.170933e-02, 'out_norm_mla_g': 8.685356e-02, 'w_out': 7.307415e-02, 'norm_ffn_g': 9.968014e-02, 'w_ff1': 5.071878e-02, 'w_ff2': 9.810687e-02, 'final_norm_g': 3.333088e+01}


def _to_microbatches(a, axis):
    t = _jnp.moveaxis(a, axis, 0)
    t = t.reshape((N_MICROBATCH, t.shape[0] // N_MICROBATCH) + t.shape[1:])
    return _jnp.moveaxis(t, 1, axis + 1)


def setup_inputs(seed: int = 0) -> dict:
    inp = _fwd_setup_inputs(seed)
    key = _jax.random.fold_in(_jax.random.key(seed), 7919)
    shape, _ = _output_shape()
    out = dict(inp)
    out["loss_target"] = _jax.random.normal(_jax.random.fold_in(key, 0), shape, _jnp.float32)
    for i, name in enumerate(TWIN_WEIGHTS):
        w = inp[name].astype(_jnp.float32)
        if MOMENT_SCALE is None:
            s = _jnp.sqrt(_jnp.mean(_jnp.square(w)) + 1e-30)
        else:
            s = MOMENT_SCALE[name]
        km, kv = _jax.random.split(_jax.random.fold_in(key, i + 1))
        out[name] = w
        out["m_" + name] = s * _jax.random.normal(km, w.shape, _jnp.float32)
        out["v_" + name] = (s * s) * _jax.random.uniform(kv, w.shape, _jnp.float32, 0.5, 1.5)
    if N_MICROBATCH > 1:
        for name, axis in PER_EXAMPLE_BATCH_AXIS.items():
            out[name] = _to_microbatches(out[name], axis)
    return {'x': out['x'], 'c': out['c'], 'positions': out['positions'], 'w_ada': out['w_ada'], 'b_ada': out['b_ada'], 'norm_mix_g': out['norm_mix_g'], 'w_in': out['w_in'], 'gmlp_ws': out['gmlp_ws'], 'gmlp_bs': out['gmlp_bs'], 'mla_q_norm_g': out['mla_q_norm_g'], 'mla_kv_norm_g': out['mla_kv_norm_g'], 'mla_w_uq': out['mla_w_uq'], 'mla_w_ukv': out['mla_w_ukv'], 'out_norm_gmlp_g': out['out_norm_gmlp_g'], 'out_norm_mla_g': out['out_norm_mla_g'], 'w_out': out['w_out'], 'norm_ffn_g': out['norm_ffn_g'], 'w_ff1': out['w_ff1'], 'w_ff2': out['w_ff2'], 'final_norm_g': out['final_norm_g'], 'loss_target': out['loss_target'], 'm_w_ada': out['m_w_ada'], 'm_b_ada': out['m_b_ada'], 'm_norm_mix_g': out['m_norm_mix_g'], 'm_w_in': out['m_w_in'], 'm_gmlp_ws': out['m_gmlp_ws'], 'm_gmlp_bs': out['m_gmlp_bs'], 'm_mla_q_norm_g': out['m_mla_q_norm_g'], 'm_mla_kv_norm_g': out['m_mla_kv_norm_g'], 'm_mla_w_uq': out['m_mla_w_uq'], 'm_mla_w_ukv': out['m_mla_w_ukv'], 'm_out_norm_gmlp_g': out['m_out_norm_gmlp_g'], 'm_out_norm_mla_g': out['m_out_norm_mla_g'], 'm_w_out': out['m_w_out'], 'm_norm_ffn_g': out['m_norm_ffn_g'], 'm_w_ff1': out['m_w_ff1'], 'm_w_ff2': out['m_w_ff2'], 'm_final_norm_g': out['m_final_norm_g'], 'v_w_ada': out['v_w_ada'], 'v_b_ada': out['v_b_ada'], 'v_norm_mix_g': out['v_norm_mix_g'], 'v_w_in': out['v_w_in'], 'v_gmlp_ws': out['v_gmlp_ws'], 'v_gmlp_bs': out['v_gmlp_bs'], 'v_mla_q_norm_g': out['v_mla_q_norm_g'], 'v_mla_kv_norm_g': out['v_mla_kv_norm_g'], 'v_mla_w_uq': out['v_mla_w_uq'], 'v_mla_w_ukv': out['v_mla_w_ukv'], 'v_out_norm_gmlp_g': out['v_out_norm_gmlp_g'], 'v_out_norm_mla_g': out['v_out_norm_mla_g'], 'v_w_out': out['v_w_out'], 'v_norm_ffn_g': out['v_norm_ffn_g'], 'v_w_ff1': out['v_w_ff1'], 'v_w_ff2': out['v_w_ff2'], 'v_final_norm_g': out['v_final_norm_g']}


def _loss(weights, diff, rest, loss_target):
    with _jax.named_scope("forward"):
        args = {**rest, TWIN_DIFF_INPUT: diff, **{k: w.astype(_WEIGHT_DTYPES[k]) for k, w in weights.items()}}
        y = _forward(args)
    with _jax.named_scope("loss_head"):
        err = _jnp.square(y.astype(_jnp.float32) - loss_target)
        return 0.5 * _jnp.sum(_jnp.mean(err, axis=-1)) if err.ndim else 0.5 * err


def _adamw(w, g, m, v):
    m = ADAM_B1 * m + (1.0 - ADAM_B1) * g
    v = ADAM_B2 * v + (1.0 - ADAM_B2) * _jnp.square(g)
    m_hat = m / (1.0 - ADAM_B1 ** ADAM_STEP)
    v_hat = v / (1.0 - ADAM_B2 ** ADAM_STEP)
    delta = -ADAM_LR * (m_hat / (_jnp.sqrt(v_hat) + ADAM_EPS) + ADAM_WD * w)
    return delta, m, v


def reference(x, c, positions, w_ada, b_ada, norm_mix_g, w_in, gmlp_ws, gmlp_bs, mla_q_norm_g, mla_kv_norm_g, mla_w_uq, mla_w_ukv, out_norm_gmlp_g, out_norm_mla_g, w_out, norm_ffn_g, w_ff1, w_ff2, final_norm_g, loss_target, m_w_ada, m_b_ada, m_norm_mix_g, m_w_in, m_gmlp_ws, m_gmlp_bs, m_mla_q_norm_g, m_mla_kv_norm_g, m_mla_w_uq, m_mla_w_ukv, m_out_norm_gmlp_g, m_out_norm_mla_g, m_w_out, m_norm_ffn_g, m_w_ff1, m_w_ff2, m_final_norm_g, v_w_ada, v_b_ada, v_norm_mix_g, v_w_in, v_gmlp_ws, v_gmlp_bs, v_mla_q_norm_g, v_mla_kv_norm_g, v_mla_w_uq, v_mla_w_ukv, v_out_norm_gmlp_g, v_out_norm_mla_g, v_w_out, v_norm_ffn_g, v_w_ff1, v_w_ff2, v_final_norm_g):
    given = dict(x=x, c=c, positions=positions, w_ada=w_ada, b_ada=b_ada, norm_mix_g=norm_mix_g, w_in=w_in, gmlp_ws=gmlp_ws, gmlp_bs=gmlp_bs, mla_q_norm_g=mla_q_norm_g, mla_kv_norm_g=mla_kv_norm_g, mla_w_uq=mla_w_uq, mla_w_ukv=mla_w_ukv, out_norm_gmlp_g=out_norm_gmlp_g, out_norm_mla_g=out_norm_mla_g, w_out=w_out, norm_ffn_g=norm_ffn_g, w_ff1=w_ff1, w_ff2=w_ff2, final_norm_g=final_norm_g, loss_target=loss_target, m_w_ada=m_w_ada, m_b_ada=m_b_ada, m_norm_mix_g=m_norm_mix_g, m_w_in=m_w_in, m_gmlp_ws=m_gmlp_ws, m_gmlp_bs=m_gmlp_bs, m_mla_q_norm_g=m_mla_q_norm_g, m_mla_kv_norm_g=m_mla_kv_norm_g, m_mla_w_uq=m_mla_w_uq, m_mla_w_ukv=m_mla_w_ukv, m_out_norm_gmlp_g=m_out_norm_gmlp_g, m_out_norm_mla_g=m_out_norm_mla_g, m_w_out=m_w_out, m_norm_ffn_g=m_norm_ffn_g, m_w_ff1=m_w_ff1, m_w_ff2=m_w_ff2, m_final_norm_g=m_final_norm_g, v_w_ada=v_w_ada, v_b_ada=v_b_ada, v_norm_mix_g=v_norm_mix_g, v_w_in=v_w_in, v_gmlp_ws=v_gmlp_ws, v_gmlp_bs=v_gmlp_bs, v_mla_q_norm_g=v_mla_q_norm_g, v_mla_kv_norm_g=v_mla_kv_norm_g, v_mla_w_uq=v_mla_w_uq, v_mla_w_ukv=v_mla_w_ukv, v_out_norm_gmlp_g=v_out_norm_gmlp_g, v_out_norm_mla_g=v_out_norm_mla_g, v_w_out=v_w_out, v_norm_ffn_g=v_norm_ffn_g, v_w_ff1=v_w_ff1, v_w_ff2=v_w_ff2, v_final_norm_g=v_final_norm_g)
    weights = {n: given[n] for n in TWIN_WEIGHTS}
    shared = {n: given[n] for n in SHARED_INPUTS}
    per_example = {n: given[n] for n in ['x', 'c', 'positions']}
    grad_fn = _jax.value_and_grad(_loss, argnums=(0, 1))

    def one_microbatch(ex, loss_target):
        ex = dict(ex)
        diff = ex.pop(TWIN_DIFF_INPUT)
        return grad_fn(weights, diff, {**shared, **ex}, loss_target)

    if N_MICROBATCH == 1:
        loss, (grad_w, grad_x) = one_microbatch(per_example, given["loss_target"])
    else:
        def body(carry, xs):
            loss_sum, grad_sum = carry
            l_k, (gw_k, gx_k) = one_microbatch(xs[0], xs[1])
            with _jax.named_scope("update"):
                return (loss_sum + l_k, _jax.tree.map(_jnp.add, grad_sum, gw_k)), gx_k

        init = (_jnp.zeros((), _jnp.float32), _jax.tree.map(_jnp.zeros_like, weights))
        (loss, grad_w), grad_x = _jax.lax.scan(body, init, (per_example, given["loss_target"]))
    with _jax.named_scope("update"):
        delta_w, new_m, new_v = {}, {}, {}
        for n in TWIN_WEIGHTS:
            delta_w[n], new_m[n], new_v[n] = _adamw(weights[n], grad_w[n], given["m_" + n], given["v_" + n])
    return (loss, grad_x, *[grad_w[n] for n in TWIN_WEIGHTS], *[delta_w[n] for n in TWIN_WEIGHTS],
            *[new_m[n] for n in TWIN_WEIGHTS], *[new_v[n] for n in TWIN_WEIGHTS])
```

```python
import functools
import math

import jax
import jax.numpy as jnp
from jax import lax
from jax.experimental import pallas as pl
from jax.experimental.pallas import tpu as pltpu

F32 = jnp.float32
BF16 = jnp.bfloat16

D_MODEL = 1024
DEPTH = 2
D_GMLP = 512
GROUPS = 8
GROUP_DIM = 64
CHUNK = 128
HEADS = 8
NOPE = 64
ROPE = 32
HEAD_PAD = 128
Q_RANK = 256
KV_RANK = 128
D_FF = 4096
N_MOD = 6
MOD_ROWS = 8
EPS = 1e-6
ROPE_THETA = 10000.0
D_IN = 1440
D_IN_PAD = 1536
ATTN_SCALE = (NOPE + ROPE) ** -0.5
N_CHIPS = 4
N_DEV = 8

ADAM_LR = 0.001
ADAM_B1 = 0.9
ADAM_B2 = 0.999
ADAM_EPS = 1e-08
ADAM_WD = 0.01
ADAM_STEP = 10

VMEM_LIMIT = 48 * 1024 * 1024
FLAT_W = 1024
ROW_ALIGN = 512

NN = (((1,), (0,)), ((), ()))
NT = (((1,), (1,)), ((), ()))
TN = (((0,), (0,)), ((), ()))
MESH = pl.DeviceIdType.MESH

SHIFT1, SCALE1, GATE1, SHIFT2, SCALE2, GATE2 = range(6)

FSDP_SECTIONS = (("w_in", 360), ("w_uq", 48), ("w_ukv", 32), ("w_out", 256), ("w_ff1", 1024), ("w_ff2", 1024))


def _cparams(vmem=VMEM_LIMIT):
    return pltpu.CompilerParams(vmem_limit_bytes=vmem)


def _dot(a, b, dims=NN):
    return lax.dot_general(a, b, dims, preferred_element_type=F32)


def _iota(shape, axis):
    return lax.broadcasted_iota(jnp.int32, shape, axis)


def _gelu(x):
    k = math.sqrt(2.0 / math.pi)
    return 0.5 * x * (1.0 + jnp.tanh(k * (x + 0.044715 * (x * x * x))))


def _gelu_grad(x):
    k = math.sqrt(2.0 / math.pi)
    t = jnp.tanh(k * (x + 0.044715 * (x * x * x)))
    return 0.5 * (1.0 + t) + 0.5 * x * (1.0 - t * t) * (k * (1.0 + 3.0 * 0.044715 * (x * x)))


def _rms_fwd(x, g, n):
    r = lax.rsqrt(jnp.sum(x * x, axis=-1, keepdims=True) * (1.0 / n) + EPS)
    return x * r * g


def _rms_bwd(x, g, dy, n):
    r = lax.rsqrt(jnp.sum(x * x, axis=-1, keepdims=True) * (1.0 / n) + EPS)
    xh = x * r
    dxh = dy * g
    dx = r * (dxh - xh * (jnp.sum(dxh * xh, axis=-1, keepdims=True) * (1.0 / n)))
    dg = jnp.sum(dy * xh, axis=0, keepdims=True)
    return dx, dg


def _pick_rows(rows, limit):
    if rows <= limit:
        return rows
    for t in range(limit, 7, -8):
        if rows % t == 0:
            return t
    return rows


def _mm(a, b, *, dims, name, tm=512, tn=1024, tk=1024, out_dtypes=(F32,), epilogue=None,
        extras=(), extra_specs=(), a_fn=None):
    if dims == "tn":
        kk, m = a.shape
    else:
        m, kk = a.shape
    n = b.shape[0] if dims == "nt" else b.shape[1]
    tm, tn, tk = min(tm, m), min(tn, n), min(tk, kk)
    assert m % tm == 0 and n % tn == 0 and kk % tk == 0, (name, a.shape, b.shape, tm, tn, tk)
    ni, nj, nk = m // tm, n // tn, kk // tk
    if dims == "tn":
        a_spec = pl.BlockSpec((tk, tm), lambda i, j, k: (k, i))
    else:
        a_spec = pl.BlockSpec((tm, tk), lambda i, j, k: (i, k))
    if dims == "nt":
        b_spec = pl.BlockSpec((tn, tk), lambda i, j, k: (j, k))
    else:
        b_spec = pl.BlockSpec((tk, tn), lambda i, j, k: (k, j))
    o_spec = pl.BlockSpec((tm, tn), lambda i, j, k: (i, j))
    dn = {"nn": NN, "nt": NT, "tn": TN}[dims]
    n_ex, n_out = len(extras), len(out_dtypes)
    e_specs = [o_spec if s is None else s for s in (tuple(extra_specs) + (None,) * n_ex)[:n_ex]]

    def body(*refs):
        a_ref, b_ref = refs[0], refs[1]
        e_refs = refs[2:2 + n_ex]
        o_refs = refs[2 + n_ex:2 + n_ex + n_out]
        av = a_ref[...]
        if a_fn is not None:
            av = a_fn(av)
        part = _dot(av.astype(BF16), b_ref[...].astype(BF16), dn)

        def finish(acc):
            outs = (acc,) if epilogue is None else epilogue(acc, *[e[...] for e in e_refs])
            for o_ref, o in zip(o_refs, outs):
                o_ref[...] = o.astype(o_ref.dtype)

        if nk == 1:
            finish(part)
        else:
            acc_ref = refs[-1]
            k = pl.program_id(2)

            @pl.when(k == 0)
            def _():
                acc_ref[...] = part

            @pl.when(k > 0)
            def _():
                acc_ref[...] += part

            @pl.when(k == nk - 1)
            def _():
                finish(acc_ref[...])

    outs = pl.pallas_call(
        body, name=name, grid=(ni, nj, nk),
        in_specs=[a_spec, b_spec] + e_specs,
        out_specs=[o_spec] * n_out,
        out_shape=[jax.ShapeDtypeStruct((m, n), dt) for dt in out_dtypes],
        scratch_shapes=[pltpu.VMEM((tm, tn), F32)] if nk > 1 else [],
        compiler_params=_cparams(),
    )(a, b, *extras)
    return outs[0] if n_out == 1 else outs


def _mod_spec(tm, tn, seq):
    return pl.BlockSpec((1, MOD_ROWS, tn), lambda i, j, k: ((i * tm) // seq, 0, j))


def _normmod_fwd(x3, g, mod, shift_row, scale_row, *, name, tb=256):
    bsz, seq, d = x3.shape
    tb = min(tb, seq)

    def body(x_ref, g_ref, mod_ref, h_ref):
        m = mod_ref[0]
        nrm = _rms_fwd(x_ref[0], g_ref[...], d)
        h = nrm * (1.0 + m[scale_row:scale_row + 1, :]) + m[shift_row:shift_row + 1, :]
        h_ref[0] = h.astype(BF16)

    return pl.pallas_call(
        body, name=name, grid=(bsz, seq // tb),
        in_specs=[pl.BlockSpec((1, tb, d), lambda b, i: (b, i, 0)),
                  pl.BlockSpec((1, d), lambda b, i: (0, 0)),
                  pl.BlockSpec((1, MOD_ROWS, d), lambda b, i: (b, 0, 0))],
        out_specs=pl.BlockSpec((1, tb, d), lambda b, i: (b, i, 0)),
        out_shape=jax.ShapeDtypeStruct((bsz, seq, d), BF16),
        compiler_params=_cparams(),
    )(x3, g, mod)


def _pair_mean_exact(x, lo):
    s_lo = jnp.sum(jnp.where(lo, x, 0.0), axis=-1, keepdims=True)
    s_hi = jnp.sum(jnp.where(lo, 0.0, x), axis=-1, keepdims=True)
    return jnp.where(lo, s_lo, s_hi) * (1.0 / GROUP_DIM)


def _gmlp_pair_fwd(gv_p, w0, w1, bias, lo):
    mu = _pair_mean_exact(gv_p, lo)
    dlt = gv_p - mu
    var = _pair_mean_exact(dlt * dlt, lo)
    rstd = lax.rsqrt(var + EPS)
    vn = dlt * rstd
    vnb = vn.astype(BF16)
    mixed = jnp.where(lo, _dot(w0, vnb), _dot(w1, vnb)) + bias
    return vn, vnb, rstd, mixed


def _tril_bf16(w):
    t = w.shape[-1]
    return jnp.where(_iota((t, t), 1) <= _iota((t, t), 0), w, 0.0).astype(BF16)


def _gmlp_fwd(z3, ws, bexp, g_out, *, name):
    bsz, seq, _ = z3.shape
    nc = seq // CHUNK

    def body(u_ref, v_ref, ws_ref, b_ref, g_ref, y_ref):
        lo = _iota((CHUNK, 128), 1) < GROUP_DIM
        gu = _gelu(u_ref[0])
        gv = _gelu(v_ref[0])
        parts = []
        for p in range(GROUPS // 2):
            sl = slice(128 * p, 128 * p + 128)
            w0 = _tril_bf16(ws_ref[2 * p])
            w1 = _tril_bf16(ws_ref[2 * p + 1])
            _, _, _, mixed = _gmlp_pair_fwd(gv[:, sl], w0, w1, b_ref[p], lo)
            parts.append(gu[:, sl] * mixed)
        yg = jnp.concatenate(parts, axis=1)
        y_ref[0] = _rms_fwd(yg, g_ref[...], D_GMLP).astype(BF16)

    return pl.pallas_call(
        body, name=name, grid=(bsz, nc),
        in_specs=[pl.BlockSpec((1, CHUNK, D_GMLP), lambda b, i: (b, i, 0)),
                  pl.BlockSpec((1, CHUNK, D_GMLP), lambda b, i: (b, i, 1)),
                  pl.BlockSpec((GROUPS, CHUNK, CHUNK), lambda b, i: (0, 0, 0)),
                  pl.BlockSpec((GROUPS // 2, CHUNK, 128), lambda b, i: (0, 0, 0)),
                  pl.BlockSpec((1, D_GMLP), lambda b, i: (0, 0))],
        out_specs=pl.BlockSpec((1, CHUNK, D_GMLP), lambda b, i: (b, i, 0)),
        out_shape=jax.ShapeDtypeStruct((bsz, seq, D_GMLP), BF16),
        compiler_params=_cparams(),
    )(z3, z3, ws, bexp, g_out)


def _gmlp_bwd(z3, dyn3, ws, wst, bexp, g_out, *, name):
    bsz, seq, _ = z3.shape
    nc = seq // CHUNK
    npair = GROUPS // 2

    def body(u_ref, v_ref, dy_ref, ws_ref, wst_ref, b_ref, g_ref, duv_ref, dws_ref, dbs_ref, dg_ref, dbacc):
        first = jnp.logical_and(pl.program_id(0) == 0, pl.program_id(1) == 0)
        last = jnp.logical_and(pl.program_id(0) == bsz - 1, pl.program_id(1) == nc - 1)

        @pl.when(first)
        def _():
            dws_ref[...] = jnp.zeros_like(dws_ref)
            dg_ref[...] = jnp.zeros_like(dg_ref)
            dbacc[...] = jnp.zeros_like(dbacc)

        lo = _iota((CHUNK, 128), 1) < GROUP_DIM
        tril = _iota((CHUNK, CHUNK), 1) <= _iota((CHUNK, CHUNK), 0)
        u = u_ref[0]
        v = v_ref[0]
        gu = _gelu(u)
        gv = _gelu(v)
        fwd = []
        for p in range(npair):
            sl = slice(128 * p, 128 * p + 128)
            w0 = _tril_bf16(ws_ref[2 * p])
            w1 = _tril_bf16(ws_ref[2 * p + 1])
            fwd.append(_gmlp_pair_fwd(gv[:, sl], w0, w1, b_ref[p], lo))
        yg = jnp.concatenate([gu[:, 128 * p:128 * p + 128] * fwd[p][3] for p in range(npair)], axis=1)
        dyg, dg = _rms_bwd(yg, g_ref[...], dy_ref[0], D_GMLP)
        dg_ref[...] += dg
        du_parts, dv_parts = [], []
        for p in range(npair):
            sl = slice(128 * p, 128 * p + 128)
            vn, vnb, rstd, mixed = fwd[p]
            dyg_p = dyg[:, sl]
            dmixed = dyg_p * gu[:, sl]
            dbacc[p] += dmixed
            dm_lo = jnp.where(lo, dmixed, 0.0).astype(BF16)
            dm_hi = jnp.where(lo, 0.0, dmixed).astype(BF16)
            dws_ref[2 * p] += jnp.where(tril, _dot(dm_lo, vnb, NT), 0.0)
            dws_ref[2 * p + 1] += jnp.where(tril, _dot(dm_hi, vnb, NT), 0.0)
            dmb = dmixed.astype(BF16)
            dvn = jnp.where(lo, _dot(wst_ref[2 * p], dmb), _dot(wst_ref[2 * p + 1], dmb))
            dgv = rstd * (dvn - _pair_mean_exact(dvn, lo) - vn * _pair_mean_exact(dvn * vn, lo))
            dv_parts.append(dgv * _gelu_grad(v[:, sl]))
            du_parts.append(dyg_p * mixed * _gelu_grad(u[:, sl]))
        duv_ref[0] = jnp.concatenate(du_parts + dv_parts, axis=1).astype(BF16)

        @pl.when(last)
        def _():
            sel = jnp.where(_iota((8, 128), 0) == 0, (_iota((8, 128), 1) < GROUP_DIM).astype(F32),
                            jnp.where(_iota((8, 128), 0) == 1, (_iota((8, 128), 1) >= GROUP_DIM).astype(F32), 0.0))
            for p in range(npair):
                dbs_ref[p] = lax.dot_general(sel, dbacc[p], NT, precision=lax.Precision.HIGHEST,
                                             preferred_element_type=F32)

    duv, dws, dbs, dg = pl.pallas_call(
        body, name=name, grid=(bsz, nc),
        in_specs=[pl.BlockSpec((1, CHUNK, D_GMLP), lambda b, i: (b, i, 0)),
                  pl.BlockSpec((1, CHUNK, D_GMLP), lambda b, i: (b, i, 1)),
                  pl.BlockSpec((1, CHUNK, D_GMLP), lambda b, i: (b, i, 0)),
                  pl.BlockSpec((GROUPS, CHUNK, CHUNK), lambda b, i: (0, 0, 0)),
                  pl.BlockSpec((GROUPS, CHUNK, CHUNK), lambda b, i: (0, 0, 0)),
                  pl.BlockSpec((npair, CHUNK, 128), lambda b, i: (0, 0, 0)),
                  pl.BlockSpec((1, D_GMLP), lambda b, i: (0, 0))],
        out_specs=[pl.BlockSpec((1, CHUNK, 2 * D_GMLP), lambda b, i: (b, i, 0)),
                   pl.BlockSpec((GROUPS, CHUNK, CHUNK), lambda b, i: (0, 0, 0)),
                   pl.BlockSpec((npair, 8, CHUNK), lambda b, i: (0, 0, 0)),
                   pl.BlockSpec((1, D_GMLP), lambda b, i: (0, 0))],
        out_shape=[jax.ShapeDtypeStruct((bsz, seq, 2 * D_GMLP), BF16),
                   jax.ShapeDtypeStruct((GROUPS, CHUNK, CHUNK), F32),
                   jax.ShapeDtypeStruct((npair, 8, CHUNK), F32),
                   jax.ShapeDtypeStruct((1, D_GMLP), F32)],
        scratch_shapes=[pltpu.VMEM((npair, CHUNK, 128), F32)],
        compiler_params=_cparams(),
    )(z3, z3, dyn3, ws, wst, bexp, g_out)
    return duv, dws, dbs[:, :2, :].reshape(GROUPS, CHUNK), dg


def _partner(x):
    width = x.shape[-1]
    lane = _iota(x.shape, x.ndim - 1) % HEAD_PAD
    up = pltpu.roll(x, width - ROPE // 2, x.ndim - 1)
    down = pltpu.roll(x, ROPE // 2, x.ndim - 1)
    first = jnp.logical_and(lane >= NOPE, lane < NOPE + ROPE // 2)
    second = jnp.logical_and(lane >= NOPE + ROPE // 2, lane < NOPE + ROPE)
    return jnp.where(first, up, jnp.where(second, down, 0.0))


def _mla_prep_fwd(z3, g_q, g_kv, w_uq, w_ukv, ctab, stab, *, name, tb=256):
    bsz, seq, _ = z3.shape
    tb = min(tb, seq)
    hw = HEADS * HEAD_PAD

    def body(ql_ref, kvl_ref, krl_ref, gq_ref, gkv_ref, wuq_ref, wukv_ref, c_ref, s_ref, q_ref, kv_ref, kr_ref):
        cq = _rms_fwd(ql_ref[0], gq_ref[...], Q_RANK).astype(BF16)
        q = _dot(cq, wuq_ref[...])
        c1, s1 = c_ref[0], s_ref[0]
        c8, s8 = jnp.tile(c1, (1, HEADS)), jnp.tile(s1, (1, HEADS))
        q_ref[0] = (q * c8 + _partner(q) * s8).astype(BF16)
        ckv = _rms_fwd(kvl_ref[0], gkv_ref[...], KV_RANK).astype(BF16)
        kv_ref[0] = _dot(ckv, wukv_ref[...]).astype(BF16)
        kr = krl_ref[0]
        kr_ref[0] = (kr * c1 + _partner(kr) * s1).astype(BF16)

    return pl.pallas_call(
        body, name=name, grid=(bsz, seq // tb),
        in_specs=[pl.BlockSpec((1, tb, Q_RANK), lambda b, i: (b, i, 4)),
                  pl.BlockSpec((1, tb, KV_RANK), lambda b, i: (b, i, 10)),
                  pl.BlockSpec((1, tb, HEAD_PAD), lambda b, i: (b, i, 11)),
                  pl.BlockSpec((1, Q_RANK), lambda b, i: (0, 0)),
                  pl.BlockSpec((1, KV_RANK), lambda b, i: (0, 0)),
                  pl.BlockSpec((Q_RANK, hw), lambda b, i: (0, 0)),
                  pl.BlockSpec((KV_RANK, hw), lambda b, i: (0, 0)),
                  pl.BlockSpec((1, tb, HEAD_PAD), lambda b, i: (b, i, 0)),
                  pl.BlockSpec((1, tb, HEAD_PAD), lambda b, i: (b, i, 0))],
        out_specs=[pl.BlockSpec((1, tb, hw), lambda b, i: (b, i, 0)),
                   pl.BlockSpec((1, tb, hw), lambda b, i: (b, i, 0)),
                   pl.BlockSpec((1, tb, HEAD_PAD), lambda b, i: (b, i, 0))],
        out_shape=[jax.ShapeDtypeStruct((bsz, seq, hw), BF16),
                   jax.ShapeDtypeStruct((bsz, seq, hw), BF16),
                   jax.ShapeDtypeStruct((bsz, seq, HEAD_PAD), BF16)],
        compiler_params=_cparams(),
    )(z3, z3, z3, g_q, g_kv, w_uq, w_ukv, ctab, stab)


def _mla_prep_bwd(z3, dq3, dk3, dv3, g_q, g_kv, w_uq, w_ukv, ctab, stab, *, name, tb=256):
    bsz, seq, _ = z3.shape
    tb = min(tb, seq)
    hw = HEADS * HEAD_PAD
    nb = seq // tb

    def body(ql_ref, kvl_ref, dq_ref, dk_ref, dv_ref, gq_ref, gkv_ref, wuq_ref, wukv_ref, c_ref, s_ref,
             dz_ref, cq_ref, dqb_ref, ckv_ref, dkvb_ref, dgq_ref, dgkv_ref):
        @pl.when(jnp.logical_and(pl.program_id(0) == 0, pl.program_id(1) == 0))
        def _():
            dgq_ref[...] = jnp.zeros_like(dgq_ref)
            dgkv_ref[...] = jnp.zeros_like(dgkv_ref)

        c1, s1 = c_ref[0], s_ref[0]
        c8, s8 = jnp.tile(c1, (1, HEADS)), jnp.tile(s1, (1, HEADS))
        dqr = dq_ref[0]
        dqb = (dqr * c8 + _partner(dqr * s8)).astype(BF16)
        dqb_ref[0] = dqb
        ql = ql_ref[0]
        cq_ref[0] = _rms_fwd(ql, gq_ref[...], Q_RANK).astype(BF16)
        dql, dgq = _rms_bwd(ql, gq_ref[...], _dot(dqb, wuq_ref[...], NT), Q_RANK)
        dgq_ref[...] += dgq

        dk = dk_ref[0]
        lane = _iota((tb, hw), 1) % HEAD_PAD
        dkvb = jnp.where(lane < NOPE, dk, dv_ref[0]).astype(BF16)
        dkvb_ref[0] = dkvb
        kvl = kvl_ref[0]
        ckv_ref[0] = _rms_fwd(kvl, gkv_ref[...], KV_RANK).astype(BF16)
        dkvl, dgkv = _rms_bwd(kvl, gkv_ref[...], _dot(dkvb, wukv_ref[...], NT), KV_RANK)
        dgkv_ref[...] += dgkv

        dkr = dk[:, 0:HEAD_PAD]
        for h in range(1, HEADS):
            dkr = dkr + dk[:, HEAD_PAD * h:HEAD_PAD * (h + 1)]
        lane1 = _iota((tb, HEAD_PAD), 1)
        dkr = jnp.where(jnp.logical_and(lane1 >= NOPE, lane1 < NOPE + ROPE), dkr, 0.0)
        dkrl = dkr * c1 + _partner(dkr * s1)
        dz_ref[0] = jnp.concatenate([dql, dkvl, dkrl], axis=1).astype(BF16)

    return pl.pallas_call(
        body, name=name, grid=(bsz, nb),
        in_specs=[pl.BlockSpec((1, tb, Q_RANK), lambda b, i: (b, i, 4)),
                  pl.BlockSpec((1, tb, KV_RANK), lambda b, i: (b, i, 10)),
                  pl.BlockSpec((1, tb, hw), lambda b, i: (b, i, 0)),
                  pl.BlockSpec((1, tb, hw), lambda b, i: (b, i, 0)),
                  pl.BlockSpec((1, tb, hw), lambda b, i: (b, i, 0)),
                  pl.BlockSpec((1, Q_RANK), lambda b, i: (0, 0)),
                  pl.BlockSpec((1, KV_RANK), lambda b, i: (0, 0)),
                  pl.BlockSpec((Q_RANK, hw), lambda b, i: (0, 0)),
                  pl.BlockSpec((KV_RANK, hw), lambda b, i: (0, 0)),
                  pl.BlockSpec((1, tb, HEAD_PAD), lambda b, i: (b, i, 0)),
                  pl.BlockSpec((1, tb, HEAD_PAD), lambda b, i: (b, i, 0))],
        out_specs=[pl.BlockSpec((1, tb, 512), lambda b, i: (b, i, 0)),
                   pl.BlockSpec((1, tb, Q_RANK), lambda b, i: (b, i, 0)),
                   pl.BlockSpec((1, tb, hw), lambda b, i: (b, i, 0)),
                   pl.BlockSpec((1, tb, KV_RANK), lambda b, i: (b, i, 0)),
                   pl.BlockSpec((1, tb, hw), lambda b, i: (b, i, 0)),
                   pl.BlockSpec((1, Q_RANK), lambda b, i: (0, 0)),
                   pl.BlockSpec((1, KV_RANK), lambda b, i: (0, 0))],
        out_shape=[jax.ShapeDtypeStruct((bsz, seq, 512), BF16),
                   jax.ShapeDtypeStruct((bsz, seq, Q_RANK), BF16),
                   jax.ShapeDtypeStruct((bsz, seq, hw), BF16),
                   jax.ShapeDtypeStruct((bsz, seq, KV_RANK), BF16),
                   jax.ShapeDtypeStruct((bsz, seq, hw), BF16),
                   jax.ShapeDtypeStruct((1, Q_RANK), F32),
                   jax.ShapeDtypeStruct((1, KV_RANK), F32)],
        compiler_params=_cparams(),
    )(z3, z3, dq3, dk3, dv3, g_q, g_kv, w_uq, w_ukv, ctab, stab)


def _attn_specs(tq, seq):
    blk = pl.BlockSpec((1, tq, HEAD_PAD), lambda b, h, i: (b, i, h))
    full = pl.BlockSpec((1, seq, HEAD_PAD), lambda b, h, i: (b, 0, h))
    full0 = pl.BlockSpec((1, seq, HEAD_PAD), lambda b, h, i: (b, 0, 0))
    blk0 = pl.BlockSpec((1, tq, HEAD_PAD), lambda b, h, i: (b, i, 0))
    return blk, full, full0, blk0


def _attn_fwd(q3, kv3, kr3, *, name, tq=256):
    bsz, seq, hw = q3.shape
    tq = min(tq, seq)
    blk, full, full0, _ = _attn_specs(tq, seq)

    def body(q_ref, kv_ref, kr_ref, o_ref, lse_ref):
        i = pl.program_id(2)
        q = q_ref[0]
        is_nope = _iota((tq, HEAD_PAD), 1) < NOPE
        causal = _iota((tq, tq), 1) <= _iota((tq, tq), 0)

        def step(j, carry, diag):
            m, l, acc = carry
            st = pl.multiple_of(j * tq, tq)
            kvj = kv_ref[0, pl.ds(st, tq), :]
            kp = jnp.where(is_nope, kvj, kr_ref[0, pl.ds(st, tq), :])
            s = _dot(q, kp, NT) * ATTN_SCALE
            if diag:
                s = jnp.where(causal, s, -1e30)
            m_new = jnp.maximum(m, jnp.max(s, axis=1, keepdims=True))
            alpha = jnp.exp(m - m_new)
            p = jnp.exp(s - m_new)
            l = alpha * l + jnp.sum(p, axis=1, keepdims=True)
            acc = alpha * acc + _dot(p.astype(BF16), kvj)
            return m_new, l, acc

        init = (jnp.full((tq, 1), -1e30, F32), jnp.zeros((tq, 1), F32), jnp.zeros((tq, HEAD_PAD), F32))
        carry = lax.fori_loop(0, i, lambda j, c: step(j, c, False), init)
        m, l, acc = step(i, carry, True)
        o_ref[0] = jnp.where(is_nope, 0.0, acc / l)
        lse_ref[0] = jnp.broadcast_to(m + jnp.log(l), (tq, HEAD_PAD))

    return pl.pallas_call(
        body, name=name, grid=(bsz, HEADS, seq // tq),
        in_specs=[blk, full, full0],
        out_specs=[blk, blk],
        out_shape=[jax.ShapeDtypeStruct((bsz, seq, hw), F32), jax.ShapeDtypeStruct((bsz, seq, hw), F32)],
        compiler_params=_cparams(),
    )(q3, kv3, kr3)


def _attn_bwd_dq(q3, kv3, kr3, do3, lse3, dl3, *, name, tq=256):
    bsz, seq, hw = q3.shape
    tq = min(tq, seq)
    blk, full, full0, _ = _attn_specs(tq, seq)

    def body(q_ref, kv_ref, kr_ref, do_ref, lse_ref, dl_ref, dq_ref):
        i = pl.program_id(2)
        q = q_ref[0]
        is_nope = _iota((tq, HEAD_PAD), 1) < NOPE
        causal = _iota((tq, tq), 1) <= _iota((tq, tq), 0)
        do = jnp.where(is_nope, 0.0, do_ref[0]).astype(BF16)
        rep = tq // HEAD_PAD
        lse = jnp.tile(lse_ref[0], (1, rep))
        dl = jnp.tile(dl_ref[0], (1, rep))

        def step(j, dq, diag):
            st = pl.multiple_of(j * tq, tq)
            kvj = kv_ref[0, pl.ds(st, tq), :]
            kp = jnp.where(is_nope, kvj, kr_ref[0, pl.ds(st, tq), :])
            vj = jnp.where(is_nope, jnp.zeros_like(kvj), kvj)
            s = _dot(q, kp, NT) * ATTN_SCALE
            if diag:
                s = jnp.where(causal, s, -1e30)
            p = jnp.exp(s - lse)
            dp = _dot(do, vj, NT)
            ds = p * (dp - dl) * ATTN_SCALE
            return dq + _dot(ds.astype(BF16), kp)

        dq = lax.fori_loop(0, i, lambda j, c: step(j, c, False), jnp.zeros((tq, HEAD_PAD), F32))
        dq_ref[0] = step(i, dq, True)

    return pl.pallas_call(
        body, name=name, grid=(bsz, HEADS, seq // tq),
        in_specs=[blk, full, full0, blk, blk, blk],
        out_specs=blk,
        out_shape=jax.ShapeDtypeStruct((bsz, seq, hw), F32),
        compiler_params=_cparams(),
    )(q3, kv3, kr3, do3, lse3, dl3)


def _attn_bwd_dkv(q3, kv3, kr3, do3, lse3, dl3, *, name, tq=256):
    bsz, seq, hw = q3.shape
    tq = min(tq, seq)
    nq = seq // tq
    blk, full, _, blk0 = _attn_specs(tq, seq)

    def body(kv_ref, kr_ref, q_ref, do_ref, lse_ref, dl_ref, dk_ref, dv_ref):
        j = pl.program_id(2)
        is_nope = _iota((tq, HEAD_PAD), 1) < NOPE
        causal = _iota((tq, tq), 1) <= _iota((tq, tq), 0)
        kvj = kv_ref[0]
        kp = jnp.where(is_nope, kvj, kr_ref[0])
        vj = jnp.where(is_nope, jnp.zeros_like(kvj), kvj)
        rep = tq // HEAD_PAD

        def step(i, carry, diag):
            dk, dv = carry
            st = pl.multiple_of(i * tq, tq)
            qi = q_ref[0, pl.ds(st, tq), :]
            do = jnp.where(is_nope, 0.0, do_ref[0, pl.ds(st, tq), :]).astype(BF16)
            lse = jnp.tile(lse_ref[0, pl.ds(st, tq), :], (1, rep))
            dl = jnp.tile(dl_ref[0, pl.ds(st, tq), :], (1, rep))
            s = _dot(qi, kp, NT) * ATTN_SCALE
            if diag:
                s = jnp.where(causal, s, -1e30)
            p = jnp.exp(s - lse)
            dv = dv + _dot(p.astype(BF16), do, TN)
            dp = _dot(do, vj, NT)
            ds = p * (dp - dl) * ATTN_SCALE
            dk = dk + _dot(ds.astype(BF16), qi, TN)
            return dk, dv

        zero = jnp.zeros((tq, HEAD_PAD), F32)
        carry = step(j, (zero, zero), True)
        dk, dv = lax.fori_loop(j + 1, nq, lambda i, c: step(i, c, False), carry)
        dk_ref[0] = dk
        dv_ref[0] = dv

    return pl.pallas_call(
        body, name=name, grid=(bsz, HEADS, nq),
        in_specs=[blk, blk0, full, full, full, full],
        out_specs=[blk, blk],
        out_shape=[jax.ShapeDtypeStruct((bsz, seq, hw), F32), jax.ShapeDtypeStruct((bsz, seq, hw), F32)],
        compiler_params=_cparams(),
    )(kv3, kr3, q3, do3, lse3, dl3)


def _onorm_fwd(o3, g_pad, *, name, tb=256):
    bsz, seq, hw = o3.shape
    tb = min(tb, seq)

    def body(o_ref, g_ref, y_ref):
        y_ref[0] = _rms_fwd(o_ref[0], g_ref[...], HEADS * 64).astype(BF16)

    return pl.pallas_call(
        body, name=name, grid=(bsz, seq // tb),
        in_specs=[pl.BlockSpec((1, tb, hw), lambda b, i: (b, i, 0)), pl.BlockSpec((1, hw), lambda b, i: (0, 0))],
        out_specs=pl.BlockSpec((1, tb, hw), lambda b, i: (b, i, 0)),
        out_shape=jax.ShapeDtypeStruct((bsz, seq, hw), BF16),
        compiler_params=_cparams(),
    )(o3, g_pad)


def _onorm_bwd(o3, dy3, g_pad, *, name, tb=256):
    bsz, seq, hw = o3.shape
    tb = min(tb, seq)

    def body(o_ref, dy_ref, g_ref, do_ref, dl_ref, dg_ref):
        @pl.when(jnp.logical_and(pl.program_id(0) == 0, pl.program_id(1) == 0))
        def _():
            dg_ref[...] = jnp.zeros_like(dg_ref)

        o = o_ref[0]
        do, dg = _rms_bwd(o, g_ref[...], dy_ref[0], HEADS * 64)
        dg_ref[...] += dg
        do_ref[0] = do
        prod = do * o
        parts = []
        for h in range(HEADS):
            sh = jnp.sum(prod[:, HEAD_PAD * h:HEAD_PAD * (h + 1)], axis=1, keepdims=True)
            parts.append(jnp.broadcast_to(sh, (tb, HEAD_PAD)))
        dl_ref[0] = jnp.concatenate(parts, axis=1)

    return pl.pallas_call(
        body, name=name, grid=(bsz, seq // tb),
        in_specs=[pl.BlockSpec((1, tb, hw), lambda b, i: (b, i, 0)),
                  pl.BlockSpec((1, tb, hw), lambda b, i: (b, i, 0)),
                  pl.BlockSpec((1, hw), lambda b, i: (0, 0))],
        out_specs=[pl.BlockSpec((1, tb, hw), lambda b, i: (b, i, 0)),
                   pl.BlockSpec((1, tb, hw), lambda b, i: (b, i, 0)),
                   pl.BlockSpec((1, hw), lambda b, i: (0, 0))],
        out_shape=[jax.ShapeDtypeStruct((bsz, seq, hw), F32),
                   jax.ShapeDtypeStruct((bsz, seq, hw), F32),
                   jax.ShapeDtypeStruct((1, hw), F32)],
        compiler_params=_cparams(),
    )(o3, dy3, g_pad)


def _resnode_bwd(x3, g, *, name, target3=None, dh3=None, dres3=None, mod_nm=None, rows=None,
                 branch3=None, mod_gate=None, gate_row=None, tb=256):
    bsz, seq, d = x3.shape
    tb = min(tb, seq)
    final = target3 is not None
    has_branch = branch3 is not None
    row_spec = pl.BlockSpec((1, tb, d), lambda b, i: (b, i, 0))
    vec_spec = pl.BlockSpec((1, d), lambda b, i: (0, 0))
    mod_spec = pl.BlockSpec((1, MOD_ROWS, d), lambda b, i: (b, 0, 0))

    ins, in_specs = [x3, g], [row_spec, vec_spec]
    if final:
        ins += [target3]
        in_specs += [row_spec]
    else:
        ins += [dh3, dres3, mod_nm]
        in_specs += [row_spec, row_spec, mod_spec]
    if has_branch:
        ins += [branch3, mod_gate]
        in_specs += [row_spec, mod_spec]

    out_names = ["dx", "dg"]
    out_specs = [row_spec, vec_spec]
    out_shape = [jax.ShapeDtypeStruct((bsz, seq, d), F32), jax.ShapeDtypeStruct((1, d), F32)]
    if final:
        out_names += ["loss"]
        out_specs += [pl.BlockSpec((1, 128), lambda b, i: (0, 0))]
        out_shape += [jax.ShapeDtypeStruct((1, 128), F32)]
    else:
        out_names += ["dnm"]
        out_specs += [mod_spec]
        out_shape += [jax.ShapeDtypeStruct((bsz, MOD_ROWS, d), F32)]
    if has_branch:
        out_names += ["dbr", "dgate"]
        out_specs += [row_spec, mod_spec]
        out_shape += [jax.ShapeDtypeStruct((bsz, seq, d), BF16), jax.ShapeDtypeStruct((bsz, MOD_ROWS, d), F32)]
    n_in = len(ins)

    def body(*refs):
        r = dict(zip(["x", "g"] + (["t"] if final else ["dh", "dres", "nm"]) + (["br", "gm"] if has_branch else []),
                     refs[:n_in]))
        o = dict(zip(out_names, refs[n_in:]))
        b_first = pl.program_id(1) == 0
        first = jnp.logical_and(pl.program_id(0) == 0, b_first)
        rowid = _iota((MOD_ROWS, d), 0)

        @pl.when(first)
        def _():
            o["dg"][...] = jnp.zeros_like(o["dg"])
            if final:
                o["loss"][...] = jnp.zeros_like(o["loss"])

        @pl.when(b_first)
        def _():
            if not final:
                o["dnm"][...] = jnp.zeros_like(o["dnm"])
            if has_branch:
                o["dgate"][...] = jnp.zeros_like(o["dgate"])

        x = r["x"][0]
        gv = r["g"][...]
        if final:
            e = _rms_fwd(x, gv, d) - r["t"][0]
            sq = jnp.sum(jnp.sum(e * e, axis=1, keepdims=True), axis=0, keepdims=True)
            o["loss"][...] += jnp.broadcast_to(sq * (0.5 / d), (1, 128))
            dx, dg = _rms_bwd(x, gv, e * (1.0 / d), d)
        else:
            m = r["nm"][0]
            dh = r["dh"][0]
            scale = m[rows[1]:rows[1] + 1, :]
            rstd = lax.rsqrt(jnp.sum(x * x, axis=-1, keepdims=True) * (1.0 / d) + EPS)
            xh = x * rstd
            nrm = xh * gv
            dshift = jnp.sum(dh, axis=0, keepdims=True)
            dscale = jnp.sum(dh * nrm, axis=0, keepdims=True)
            o["dnm"][0] += jnp.where(rowid == 0, dshift, jnp.where(rowid == 1, dscale, 0.0))
            dn = dh * (1.0 + scale)
            dg = jnp.sum(dn * xh, axis=0, keepdims=True)
            dxh = dn * gv
            dx = rstd * (dxh - xh * (jnp.sum(dxh * xh, axis=-1, keepdims=True) * (1.0 / d))) + r["dres"][0]
        o["dg"][...] += dg
        o["dx"][0] = dx
        if has_branch:
            gate = r["gm"][0][gate_row:gate_row + 1, :]
            o["dbr"][0] = (gate * dx).astype(BF16)
            dgate = jnp.sum(dx * r["br"][0], axis=0, keepdims=True)
            o["dgate"][0] += jnp.where(rowid == 0, dgate, 0.0)

    outs = pl.pallas_call(
        body, name=name, grid=(bsz, seq // tb),
        in_specs=in_specs, out_specs=out_specs, out_shape=out_shape,
        compiler_params=_cparams(),
    )(*ins)
    return dict(zip(out_names, outs))


def _adamw(w, g, m, v, *, name):
    shape = w.shape
    cols = shape[-1]
    rows = w.size // cols
    tr = _pick_rows(rows, max(8, (256 * 1024) // cols // 8 * 8))
    c1 = 1.0 - ADAM_B1 ** ADAM_STEP
    c2 = 1.0 - ADAM_B2 ** ADAM_STEP

    def body(w_ref, g_ref, m_ref, v_ref, d_ref, nm_ref, nv_ref):
        gg = g_ref[...]
        nm = ADAM_B1 * m_ref[...] + (1.0 - ADAM_B1) * gg
        nv = ADAM_B2 * v_ref[...] + (1.0 - ADAM_B2) * (gg * gg)
        m_hat = nm / c1
        v_hat = nv / c2
        d_ref[...] = -ADAM_LR * (m_hat / (jnp.sqrt(v_hat) + ADAM_EPS) + ADAM_WD * w_ref[...])
        nm_ref[...] = nm
        nv_ref[...] = nv

    spec = pl.BlockSpec((tr, cols), lambda i: (i, 0))
    outs = pl.pallas_call(
        body, name=name, grid=(rows // tr,),
        in_specs=[spec] * 4, out_specs=[spec] * 3,
        out_shape=[jax.ShapeDtypeStruct((rows, cols), F32)] * 3,
        compiler_params=_cparams(),
    )(*[t.reshape(rows, cols) for t in (w, g, m, v)])
    return tuple(o.reshape(shape) for o in outs)


def _sum_leading(x, *, name, tr=256):
    n, rows, cols = x.shape
    tr = _pick_rows(rows, tr)

    def body(x_ref, o_ref):
        acc = x_ref[0]
        for k in range(1, n):
            acc = acc + x_ref[k]
        o_ref[...] = acc

    return pl.pallas_call(
        body, name=name, grid=(rows // tr,),
        in_specs=[pl.BlockSpec((n, tr, cols), lambda i: (0, i, 0))],
        out_specs=pl.BlockSpec((tr, cols), lambda i: (i, 0)),
        out_shape=jax.ShapeDtypeStruct((rows, cols), F32),
        compiler_params=_cparams(),
    )(x)


def _position():
    return lax.axis_index("x"), lax.axis_index("y"), lax.axis_index("c")


def _allgather8(x, *, name, own_half=False):
    shape = (x.shape[0] // 2,) + x.shape[1:] if own_half else x.shape
    half_rows = shape[0]

    def body(x_ref, out_ref, send_sems, recv_sems, local_sem):
        px, py, pc = _position()
        me, sibling = (px, py, pc), (px, py, 1 - pc)
        chips = [(1 - px, py), (px, 1 - py), (1 - px, 1 - py)]
        src_own = x_ref.at[pl.ds(pc * half_rows, half_rows)] if own_half else x_ref

        def slot(qx, qy, qc):
            return out_ref.at[4 * qx + 2 * qy + qc]

        def copy(k, block, to, src=None):
            return pltpu.make_async_remote_copy(
                src_ref=slot(*block) if src is None else src, dst_ref=slot(*block),
                send_sem=send_sems.at[k], recv_sem=recv_sems.at[k], device_id=to, device_id_type=MESH)

        mine = pltpu.make_async_copy(src_own, slot(*me), local_sem)
        mine.start()
        first = [copy(0, me, sibling, src=src_own)]
        first += [copy(1 + j, me, (*chip, pc), src=src_own) for j, chip in enumerate(chips)]
        for cp in first:
            cp.start()
        passed = [copy(4 + j, (*chip, pc), sibling) for j, chip in enumerate(chips)]
        for j, chip in enumerate(chips):
            copy(1 + j, (*chip, pc), me).wait_recv()
            passed[j].start()
        copy(0, sibling, me).wait_recv()
        for j, chip in enumerate(chips):
            copy(4 + j, (*chip, 1 - pc), me).wait_recv()
        for cp in first + passed:
            cp.wait_send()
        mine.wait()

    return pl.pallas_call(
        body, name=name,
        out_shape=jax.ShapeDtypeStruct((N_DEV,) + shape, x.dtype),
        in_specs=[pl.BlockSpec(memory_space=pl.ANY)],
        out_specs=pl.BlockSpec(memory_space=pl.ANY),
        scratch_shapes=[pltpu.SemaphoreType.DMA((7,)), pltpu.SemaphoreType.DMA((7,)), pltpu.SemaphoreType.DMA],
    )(x)


def _sibling_other_half(g):
    n, rows, w = g.shape
    hr = rows // 2

    def body(g_ref, out_ref, send_sem, recv_sem):
        px, py, pc = _position()
        cp = pltpu.make_async_remote_copy(
            src_ref=g_ref.at[:, pl.ds((1 - pc) * hr, hr), :], dst_ref=out_ref,
            send_sem=send_sem, recv_sem=recv_sem, device_id=(px, py, 1 - pc), device_id_type=MESH)
        cp.start()
        cp.wait()

    return pl.pallas_call(
        body, name="rs_sibling_exchange",
        out_shape=jax.ShapeDtypeStruct((n, hr, w), g.dtype),
        in_specs=[pl.BlockSpec(memory_space=pl.ANY)],
        out_specs=pl.BlockSpec(memory_space=pl.ANY),
        scratch_shapes=[pltpu.SemaphoreType.DMA, pltpu.SemaphoreType.DMA],
    )(g)


def _chip_exchange(sb):
    _, rows, w = sb.shape

    def body(sb_ref, out_ref, send_sems, recv_sems):
        px, py, pc = _position()
        peers = [(px, 1 - py, pc), (1 - px, py, pc), (1 - px, 1 - py, pc)]
        cps = [pltpu.make_async_remote_copy(
            src_ref=sb_ref.at[j], dst_ref=out_ref.at[j], send_sem=send_sems.at[j], recv_sem=recv_sems.at[j],
            device_id=peer, device_id_type=MESH) for j, peer in enumerate(peers)]
        for cp in cps:
            cp.start()
        for cp in cps:
            cp.wait()

    return pl.pallas_call(
        body, name="rs_chip_exchange",
        out_shape=jax.ShapeDtypeStruct(sb.shape, sb.dtype),
        in_specs=[pl.BlockSpec(memory_space=pl.ANY)],
        out_specs=pl.BlockSpec(memory_space=pl.ANY),
        scratch_shapes=[pltpu.SemaphoreType.DMA((3,)), pltpu.SemaphoreType.DMA((3,))],
    )(sb)


def _sibling_complete(half):
    hr, w = half.shape

    def body(h_ref, out_ref, send_sem, recv_sem, local_sem):
        px, py, pc = _position()
        rows = out_ref.at[pl.ds(pc * hr, hr)]
        mine = pltpu.make_async_copy(h_ref, rows, local_sem)
        mine.start()
        cp = pltpu.make_async_remote_copy(
            src_ref=h_ref, dst_ref=rows, send_sem=send_sem, recv_sem=recv_sem,
            device_id=(px, py, 1 - pc), device_id_type=MESH)
        cp.start()
        cp.wait()
        mine.wait()

    return pl.pallas_call(
        body, name="rs_sibling_complete",
        out_shape=jax.ShapeDtypeStruct((2 * hr, w), half.dtype),
        in_specs=[pl.BlockSpec(memory_space=pl.ANY)],
        out_specs=pl.BlockSpec(memory_space=pl.ANY),
        scratch_shapes=[pltpu.SemaphoreType.DMA, pltpu.SemaphoreType.DMA, pltpu.SemaphoreType.DMA],
    )(half)


def _rs_partial(g, recv, ids, *, tr=256):
    _, rows, w = g.shape
    hr = rows // 2
    nb = hr // tr

    def body(ids_ref, g_ref, r_ref, o_ref):
        o_ref[0] = (g_ref[0] + r_ref[0]).astype(BF16)

    grid_spec = pltpu.PrefetchScalarGridSpec(
        num_scalar_prefetch=1, grid=(3, nb),
        in_specs=[pl.BlockSpec((1, tr, w), lambda j, i, ids: (ids[1] ^ (j + 1), ids[0] * nb + i, 0)),
                  pl.BlockSpec((1, tr, w), lambda j, i, ids: (ids[1] ^ (j + 1), i, 0))],
        out_specs=pl.BlockSpec((1, tr, w), lambda j, i, ids: (j, i, 0)))
    return pl.pallas_call(
        body, name="rs_partial", grid_spec=grid_spec,
        out_shape=jax.ShapeDtypeStruct((3, hr, w), BF16),
        compiler_params=_cparams(),
    )(ids, g, recv)


def _rs_total(g, recv, got, ids, *, tr=256):
    _, rows, w = g.shape
    hr = rows // 2
    nb = hr // tr

    def body(ids_ref, g_ref, r_ref, got_ref, o_ref):
        acc = g_ref[0] + r_ref[0]
        for j in range(3):
            acc = acc + got_ref[j].astype(F32)
        o_ref[...] = acc

    grid_spec = pltpu.PrefetchScalarGridSpec(
        num_scalar_prefetch=1, grid=(nb,),
        in_specs=[pl.BlockSpec((1, tr, w), lambda i, ids: (ids[1], ids[0] * nb + i, 0)),
                  pl.BlockSpec((1, tr, w), lambda i, ids: (ids[1], i, 0)),
                  pl.BlockSpec((3, tr, w), lambda i, ids: (0, i, 0))],
        out_specs=pl.BlockSpec((tr, w), lambda i, ids: (i, 0)))
    return pl.pallas_call(
        body, name="rs_total", grid_spec=grid_spec,
        out_shape=jax.ShapeDtypeStruct((hr, w), F32),
        compiler_params=_cparams(),
    )(ids, g, recv, got)


def _reduce_scatter(g, ids):
    recv = _sibling_other_half(g)
    got = _chip_exchange(_rs_partial(g, recv, ids))
    return _sibling_complete(_rs_total(g, recv, got, ids))


def _flat_rows():
    per_layer = sum(r for _, r in FSDP_SECTIONS)
    used = DEPTH * per_layer
    return used, -(-used // ROW_ALIGN) * ROW_ALIGN


def _cols_to_chunks(full):
    rows, cols = full.shape
    t = full.reshape(rows, N_CHIPS, cols // N_CHIPS).transpose(1, 0, 2)
    return t.reshape(N_CHIPS, -1, FLAT_W)


def _chunks_to_cols(chunks, rows, cols):
    return chunks.reshape(N_CHIPS, rows, cols // N_CHIPS).transpose(1, 0, 2).reshape(rows, cols)


def _pad_heads(w, real):
    lead = w.shape[:-1]
    t = w.reshape(lead + (HEADS, real))
    t = jnp.pad(t, [(0, 0)] * len(lead) + [(0, 0), (0, HEAD_PAD - real)])
    return t.reshape(lead + (HEADS * HEAD_PAD,))


def _unpad_heads(w, real):
    lead = w.shape[:-1]
    return w.reshape(lead + (HEADS, HEAD_PAD))[..., :real].reshape(lead + (HEADS * real,))


def _pad_value_lanes(w, axis):
    w = jnp.moveaxis(w, axis, -1)
    lead = w.shape[:-1]
    t = w.reshape(lead + (HEADS, 64))
    t = jnp.pad(t, [(0, 0)] * len(lead) + [(0, 0), (HEAD_PAD - 64, 0)])
    return jnp.moveaxis(t.reshape(lead + (HEADS * HEAD_PAD,)), -1, axis)


def _unpad_value_lanes(w, axis):
    w = jnp.moveaxis(w, axis, -1)
    lead = w.shape[:-1]
    t = w.reshape(lead + (HEADS, HEAD_PAD))[..., HEAD_PAD - 64:]
    return jnp.moveaxis(t.reshape(lead + (HEADS * 64,)), -1, axis)


def _pad_w_in(w):
    z = jnp.zeros((w.shape[0], NOPE), w.dtype)
    z2 = jnp.zeros((w.shape[0], HEAD_PAD - NOPE - ROPE), w.dtype)
    return jnp.concatenate([w[:, :1408], z, w[:, 1408:], z2], axis=1)


def _unpad_w_in(w):
    return jnp.concatenate([w[:, :1408], w[:, 1408 + NOPE:1408 + NOPE + ROPE]], axis=1)


def _rope_tables(positions):
    freqs = ROPE_THETA ** (-jnp.arange(0, ROPE, 2, dtype=F32) / ROPE)
    ang = positions.astype(F32)[..., None] * freqs
    cos, sin = jnp.cos(ang), jnp.sin(ang)
    lead = cos.shape[:-1]
    ones = jnp.ones(lead + (NOPE,), F32)
    zeros_n = jnp.zeros(lead + (NOPE,), F32)
    zeros_p = jnp.zeros(lead + (HEAD_PAD - NOPE - ROPE,), F32)
    ctab = jnp.concatenate([ones, cos, cos, zeros_p], axis=-1)
    stab = jnp.concatenate([zeros_n, -sin, sin, zeros_p], axis=-1)
    return ctab, stab


def _layer_weights(p, l):
    ws = p["gmlp_ws"][l]
    tril = jnp.tril(jnp.ones((CHUNK, CHUNK), bool))
    bs = p["gmlp_bs"][l]
    bexp = jnp.repeat(bs.reshape(GROUPS // 2, 2, CHUNK).transpose(0, 2, 1), GROUP_DIM, axis=2)
    return dict(
        w_in=_pad_w_in(p["w_in"][l]),
        w_uq=_pad_heads(p["mla_w_uq"][l], NOPE + ROPE),
        w_ukv=p["mla_w_ukv"][l],
        w_out_a=_pad_value_lanes(p["w_out"][l][D_GMLP:], 0),
        w_out_g=p["w_out"][l][:D_GMLP],
        w_ff1=p["w_ff1"][l],
        w_ff2=p["w_ff2"][l],
        ws=ws,
        wst=jnp.where(tril[None], ws, 0.0).transpose(0, 2, 1).astype(BF16),
        bexp=bexp,
        g_mix=p["norm_mix_g"][l][None],
        g_ffn=p["norm_ffn_g"][l][None],
        g_q=p["mla_q_norm_g"][l][None],
        g_kv=p["mla_kv_norm_g"][l][None],
        g_og=p["out_norm_gmlp_g"][l][None],
        g_oa=_pad_value_lanes(p["out_norm_mla_g"][l], 0)[None],
    )


def _local_step(x3, target3, positions, mods, p):
    bsz, seq, d = x3.shape
    tok = bsz * seq
    tmt = min(512, seq)
    ctab, stab = _rope_tables(positions)
    lw = [_layer_weights(p, l) for l in range(DEPTH)]

    def flat(t):
        return t.reshape(tok, t.shape[-1])

    def cube(t):
        return t.reshape(bsz, seq, t.shape[-1])

    saved = []
    x = x3
    for l in range(DEPTH):
        w, mod = lw[l], mods[l]
        h1 = _normmod_fwd(x, w["g_mix"], mod, SHIFT1, SCALE1, name=f"l{l}_normmod1")
        z = cube(_mm(flat(h1), w["w_in"], dims="nn", name=f"l{l}_w_in", tm=tmt, tn=D_IN_PAD, tk=d))
        yg = _gmlp_fwd(z, w["ws"], w["bexp"], w["g_og"], name=f"l{l}_gmlp_fwd")
        q, kv, kr = _mla_prep_fwd(z, w["g_q"], w["g_kv"], w["w_uq"], w["w_ukv"], ctab, stab, name=f"l{l}_mla_prep")
        o, lse = _attn_fwd(q, kv, kr, name=f"l{l}_attn_fwd")
        ya = _onorm_fwd(o, w["g_oa"], name=f"l{l}_onorm_fwd")
        pg = _mm(flat(yg), w["w_out_g"], dims="nn", name=f"l{l}_w_out_g", tm=tmt, tn=d, tk=D_GMLP)

        def out_epi(acc, pgv, xv, gm):
            po = acc + pgv
            return po, xv + gm[0][GATE1:GATE1 + 1, :] * po

        po, x_mid = _mm(flat(ya), w["w_out_a"], dims="nn", name=f"l{l}_w_out_a", tm=tmt, tn=d, tk=d,
                        out_dtypes=(F32, F32), epilogue=out_epi, extras=(pg, flat(x), mod),
                        extra_specs=(None, None, _mod_spec(tmt, d, seq)))
        x_mid = cube(x_mid)
        h2 = _normmod_fwd(x_mid, w["g_ffn"], mod, SHIFT2, SCALE2, name=f"l{l}_normmod2")

        def act_epi(acc):
            r = jnp.maximum(acc, 0.0)
            return acc, r * r

        a, r = _mm(flat(h2), w["w_ff1"], dims="nn", name=f"l{l}_w_ff1", tm=tmt, tn=1024, tk=d,
                   out_dtypes=(F32, BF16), epilogue=act_epi)

        def ff2_epi(acc, xv, gm):
            return acc, xv + gm[0][GATE2:GATE2 + 1, :] * acc

        f, x_out = _mm(r, w["w_ff2"], dims="nn", name=f"l{l}_w_ff2", tm=tmt, tn=d, tk=1024,
                       out_dtypes=(F32, F32), epilogue=ff2_epi, extras=(flat(x_mid), mod),
                       extra_specs=(None, _mod_spec(tmt, d, seq)))
        saved.append(dict(x_in=x, h1=h1, z=z, q=q, kv=kv, kr=kr, o=o, lse=lse, ya=ya, yg=yg, po=cube(po),
                          x_mid=x_mid, h2=h2, a=a, r=r, f=cube(f)))
        x = cube(x_out)

    grads = [dict() for _ in range(DEPTH)]
    dmods = [None] * DEPTH
    top = DEPTH - 1
    node = _resnode_bwd(x, p["final_norm_g"][None], name="final_loss_bwd", target3=target3,
                        branch3=saved[top]["f"], mod_gate=mods[top], gate_row=GATE2)
    loss_part = node["loss"][0, 0]
    d_final_g = node["dg"][0]
    for l in range(DEPTH - 1, -1, -1):
        w, mod, s = lw[l], mods[l], saved[l]
        dx_out, dfb, dgate2 = node["dx"], flat(node["dbr"]), node["dgate"][:, 0]

        def dact_epi(acc, av):
            return (acc * (2.0 * jnp.maximum(av, 0.0)),)

        da = _mm(dfb, w["w_ff2"], dims="nt", name=f"l{l}_d_r", tm=tmt, tn=1024, tk=d,
                 out_dtypes=(BF16,), epilogue=dact_epi, extras=(s["a"],))
        grads[l]["w_ff2"] = _mm(s["r"], dfb, dims="tn", name=f"l{l}_dw_ff2", tm=1024, tn=d, tk=1024)
        grads[l]["w_ff1"] = _mm(flat(s["h2"]), da, dims="tn", name=f"l{l}_dw_ff1", tm=d, tn=1024, tk=1024)
        dh2 = _mm(da, w["w_ff1"], dims="nt", name=f"l{l}_d_h2", tm=tmt, tn=d, tk=1024)
        node = _resnode_bwd(s["x_mid"], w["g_ffn"], name=f"l{l}_resnode_ffn", dh3=cube(dh2), dres3=dx_out,
                            mod_nm=mod, rows=(SHIFT2, SCALE2), branch3=s["po"], mod_gate=mod, gate_row=GATE1)
        grads[l]["norm_ffn_g"] = node["dg"][0]
        dshift2, dscale2 = node["dnm"][:, 0], node["dnm"][:, 1]
        dx_mid, dpo, dgate1 = node["dx"], flat(node["dbr"]), node["dgate"][:, 0]

        dya = _mm(dpo, w["w_out_a"], dims="nt", name=f"l{l}_d_ya", tm=tmt, tn=d, tk=d)
        dyg = _mm(dpo, w["w_out_g"], dims="nt", name=f"l{l}_d_yg", tm=tmt, tn=D_GMLP, tk=d)
        dw_out_a = _mm(flat(s["ya"]), dpo, dims="tn", name=f"l{l}_dw_out_a", tm=d, tn=d, tk=1024)
        dw_out_g = _mm(flat(s["yg"]), dpo, dims="tn", name=f"l{l}_dw_out_g", tm=D_GMLP, tn=d, tk=1024)
        grads[l]["w_out"] = jnp.concatenate([dw_out_g, _unpad_value_lanes(dw_out_a, 0)], axis=0)

        duv, dws, dbs, dg_og = _gmlp_bwd(s["z"], cube(dyg), w["ws"], w["wst"], w["bexp"], w["g_og"],
                                         name=f"l{l}_gmlp_bwd")
        grads[l]["gmlp_ws"], grads[l]["gmlp_bs"], grads[l]["out_norm_gmlp_g"] = dws, dbs, dg_og[0]

        do, dl, dg_oa = _onorm_bwd(s["o"], cube(dya), w["g_oa"], name=f"l{l}_onorm_bwd")
        grads[l]["out_norm_mla_g"] = _unpad_value_lanes(dg_oa[0], 0)
        dq = _attn_bwd_dq(s["q"], s["kv"], s["kr"], do, s["lse"], dl, name=f"l{l}_attn_dq")
        dk, dv = _attn_bwd_dkv(s["q"], s["kv"], s["kr"], do, s["lse"], dl, name=f"l{l}_attn_dkv")
        dzm, cq, dqb, ckv, dkvb, dg_q, dg_kv = _mla_prep_bwd(
            s["z"], dq, dk, dv, w["g_q"], w["g_kv"], w["w_uq"], w["w_ukv"], ctab, stab, name=f"l{l}_mla_prep_bwd")
        grads[l]["mla_q_norm_g"], grads[l]["mla_kv_norm_g"] = dg_q[0], dg_kv[0]
        dw_uq = _mm(flat(cq), flat(dqb), dims="tn", name=f"l{l}_dw_uq", tm=Q_RANK, tn=1024, tk=1024)
        grads[l]["mla_w_uq"] = _unpad_heads(dw_uq, NOPE + ROPE)
        grads[l]["mla_w_ukv"] = _mm(flat(ckv), flat(dkvb), dims="tn", name=f"l{l}_dw_ukv", tm=KV_RANK, tn=1024, tk=1024)

        h1f = flat(s["h1"])
        dw_in_uv = _mm(h1f, flat(duv), dims="tn", name=f"l{l}_dw_in_uv", tm=d, tn=1024, tk=1024)
        dw_in_m = _mm(h1f, flat(dzm), dims="tn", name=f"l{l}_dw_in_m", tm=d, tn=512, tk=1024)
        grads[l]["w_in"] = _unpad_w_in(jnp.concatenate([dw_in_uv, dw_in_m], axis=1))
        dh1_uv = _mm(flat(duv), w["w_in"][:, :1024], dims="nt", name=f"l{l}_d_h1_uv", tm=tmt, tn=d, tk=1024)
        dh1 = _mm(flat(dzm), w["w_in"][:, 1024:], dims="nt", name=f"l{l}_d_h1", tm=tmt, tn=d, tk=512,
                  epilogue=lambda acc, prev: (acc + prev,), extras=(dh1_uv,))
        if l > 0:
            node = _resnode_bwd(s["x_in"], w["g_mix"], name=f"l{l}_resnode_mix", dh3=cube(dh1), dres3=dx_mid,
                                mod_nm=mod, rows=(SHIFT1, SCALE1), branch3=saved[l - 1]["f"],
                                mod_gate=mods[l - 1], gate_row=GATE2)
        else:
            node = _resnode_bwd(s["x_in"], w["g_mix"], name=f"l{l}_resnode_mix", dh3=cube(dh1), dres3=dx_mid,
                                mod_nm=mod, rows=(SHIFT1, SCALE1))
        grads[l]["norm_mix_g"] = node["dg"][0]
        dshift1, dscale1 = node["dnm"][:, 0], node["dnm"][:, 1]
        dmods[l] = jnp.stack([dshift1, dscale1, dgate1, dshift2, dscale2, dgate2], axis=1)
    return loss_part, node["dx"], grads, d_final_g, dmods


W_NAMES = ("w_ada", "b_ada", "norm_mix_g", "w_in", "gmlp_ws", "gmlp_bs", "mla_q_norm_g", "mla_kv_norm_g",
           "mla_w_uq", "mla_w_ukv", "out_norm_gmlp_g", "out_norm_mla_g", "w_out", "norm_ffn_g", "w_ff1", "w_ff2",
           "final_norm_g")
FLAT_KEY = {"w_in": "w_in", "w_uq": "mla_w_uq", "w_ukv": "mla_w_ukv", "w_out": "w_out", "w_ff1": "w_ff1",
            "w_ff2": "w_ff2"}
COL_SHARDED = ("w_in", "w_uq", "w_ukv", "w_ff1")
FULL_SHAPE = {"w_in": (D_MODEL, D_IN), "w_uq": (Q_RANK, HEADS * (NOPE + ROPE)), "w_ukv": (KV_RANK, HEADS * 128),
              "w_out": (D_MODEL, D_MODEL), "w_ff1": (D_MODEL, D_FF), "w_ff2": (D_FF, D_MODEL)}
SMALL_NAMES = ("norm_mix_g", "gmlp_ws", "gmlp_bs", "mla_q_norm_g", "mla_kv_norm_g", "out_norm_gmlp_g",
               "out_norm_mla_g", "norm_ffn_g", "final_norm_g")


def _silu(v):
    return v * (1.0 / (1.0 + jnp.exp(-v)))


def kernel(x, c, positions, w_ada, b_ada, norm_mix_g, w_in, gmlp_ws, gmlp_bs, mla_q_norm_g, mla_kv_norm_g, mla_w_uq, mla_w_ukv, out_norm_gmlp_g, out_norm_mla_g, w_out, norm_ffn_g, w_ff1, w_ff2, final_norm_g, loss_target, m_w_ada, m_b_ada, m_norm_mix_g, m_w_in, m_gmlp_ws, m_gmlp_bs, m_mla_q_norm_g, m_mla_kv_norm_g, m_mla_w_uq, m_mla_w_ukv, m_out_norm_gmlp_g, m_out_norm_mla_g, m_w_out, m_norm_ffn_g, m_w_ff1, m_w_ff2, m_final_norm_g, v_w_ada, v_b_ada, v_norm_mix_g, v_w_in, v_gmlp_ws, v_gmlp_bs, v_mla_q_norm_g, v_mla_kv_norm_g, v_mla_w_uq, v_mla_w_ukv, v_out_norm_gmlp_g, v_out_norm_mla_g, v_w_out, v_norm_ffn_g, v_w_ff1, v_w_ff2, v_final_norm_g):
    weights = dict(w_ada=w_ada, b_ada=b_ada, norm_mix_g=norm_mix_g, w_in=w_in, gmlp_ws=gmlp_ws, gmlp_bs=gmlp_bs,
                   mla_q_norm_g=mla_q_norm_g, mla_kv_norm_g=mla_kv_norm_g, mla_w_uq=mla_w_uq, mla_w_ukv=mla_w_ukv,
                   out_norm_gmlp_g=out_norm_gmlp_g, out_norm_mla_g=out_norm_mla_g, w_out=w_out,
                   norm_ffn_g=norm_ffn_g, w_ff1=w_ff1, w_ff2=w_ff2, final_norm_g=final_norm_g)
    mom_m = dict(zip(W_NAMES, (m_w_ada, m_b_ada, m_norm_mix_g, m_w_in, m_gmlp_ws, m_gmlp_bs, m_mla_q_norm_g,
                               m_mla_kv_norm_g, m_mla_w_uq, m_mla_w_ukv, m_out_norm_gmlp_g, m_out_norm_mla_g,
                               m_w_out, m_norm_ffn_g, m_w_ff1, m_w_ff2, m_final_norm_g)))
    mom_v = dict(zip(W_NAMES, (v_w_ada, v_b_ada, v_norm_mix_g, v_w_in, v_gmlp_ws, v_gmlp_bs, v_mla_q_norm_g,
                               v_mla_kv_norm_g, v_mla_w_uq, v_mla_w_ukv, v_out_norm_gmlp_g, v_out_norm_mla_g,
                               v_w_out, v_norm_ffn_g, v_w_ff1, v_w_ff2, v_final_norm_g)))
    bsz, seq, d = x.shape
    px, py, pc = _position()
    chip = 2 * px + py
    dev = 2 * chip + pc
    ids = jnp.stack([pc, chip]).astype(jnp.int32)
    n_ex = N_DEV * bsz
    ada_cols = w_ada.shape[-1]

    c_all = _allgather8(c.reshape(bsz * d // 128, 128), name="gather_c").reshape(n_ex, d)
    mod_parts = []
    for l in range(DEPTH):
        bias = lax.dynamic_slice(b_ada[l], (chip * ada_cols,), (ada_cols,))[None]
        mod_parts.append(_mm(c_all, w_ada[l], dims="nn", name=f"l{l}_mod", tm=n_ex, tn=ada_cols, tk=d,
                             epilogue=lambda acc, bv: (acc + bv,), extras=(bias,),
                             extra_specs=(pl.BlockSpec((1, ada_cols), lambda i, j, k: (0, j)),), a_fn=_silu))
    mod_g = _allgather8(jnp.concatenate(mod_parts, axis=0), name="gather_mod")
    mod_g = mod_g.reshape(N_CHIPS, 2, DEPTH, n_ex, ada_cols)[:, 0]
    mod_full = mod_g.transpose(1, 2, 0, 3).reshape(DEPTH, n_ex, N_CHIPS * ada_cols)
    mod_mine = lax.dynamic_slice(mod_full, (0, dev * bsz, 0), (DEPTH, bsz, N_MOD * d))
    mod_mine = jnp.pad(mod_mine.reshape(DEPTH, bsz, N_MOD, d), ((0, 0), (0, 0), (0, MOD_ROWS - N_MOD), (0, 0)))
    mods = [mod_mine[l] for l in range(DEPTH)]

    used_rows, flat_rows = _flat_rows()
    pieces = [weights[FLAT_KEY[nm]][l].reshape(-1, FLAT_W) for l in range(DEPTH) for nm, _ in FSDP_SECTIONS]
    pieces.append(jnp.zeros((flat_rows - used_rows, FLAT_W), F32))
    w_flat = jnp.concatenate(pieces, axis=0).astype(BF16)
    w_gath = _allgather8(w_flat, name="gather_weights", own_half=True).reshape(N_CHIPS, flat_rows, FLAT_W)
    full = {FLAT_KEY[nm]: [] for nm, _ in FSDP_SECTIONS}
    off = 0
    for l in range(DEPTH):
        for nm, nrows in FSDP_SECTIONS:
            sec = w_gath[:, off:off + nrows]
            off += nrows
            rows, cols = FULL_SHAPE[nm]
            full[FLAT_KEY[nm]].append(_chunks_to_cols(sec, rows, cols) if nm in COL_SHARDED
                                      else sec.reshape(rows, cols))
    p = dict(weights)
    p.update(full)

    loss_part, grad_x, grads, d_final_g, dmods = _local_step(x, loss_target, positions, mods, p)
    loss = lax.psum(loss_part, ("x", "y", "c"))

    gpieces = []
    for l in range(DEPTH):
        for nm, nrows in FSDP_SECTIONS:
            g = grads[l][FLAT_KEY[nm]]
            gpieces.append(_cols_to_chunks(g) if nm in COL_SHARDED else g.reshape(N_CHIPS, nrows, FLAT_W))
    gpieces.append(jnp.zeros((N_CHIPS, flat_rows - used_rows, FLAT_W), F32))
    g_shard = _reduce_scatter(jnp.concatenate(gpieces, axis=1), ids)
    grad = {}
    off = 0
    per = {FLAT_KEY[nm]: [] for nm, _ in FSDP_SECTIONS}
    for l in range(DEPTH):
        for nm, nrows in FSDP_SECTIONS:
            per[FLAT_KEY[nm]].append(g_shard[off:off + nrows].reshape(weights[FLAT_KEY[nm]].shape[1:]))
            off += nrows
    for key, parts in per.items():
        grad[key] = jnp.stack(parts, axis=0)

    small = {nm: (d_final_g if nm == "final_norm_g" else jnp.stack([grads[l][nm] for l in range(DEPTH)], axis=0))
             for nm in SMALL_NAMES}
    svec = jnp.concatenate([small[nm].reshape(-1) for nm in SMALL_NAMES])
    n_small = svec.shape[0]
    srows = -(-n_small // (8 * FLAT_W)) * 8
    svec = jnp.pad(svec, (0, srows * FLAT_W - n_small)).reshape(srows, FLAT_W)
    ssum = _sum_leading(_allgather8(svec, name="gather_small_grads"), name="sum_small_grads").reshape(-1)
    off = 0
    for nm in SMALL_NAMES:
        size = weights[nm].size
        grad[nm] = ssum[off:off + size].reshape(weights[nm].shape)
        off += size

    dmod = jnp.stack(dmods, axis=1).reshape(bsz * DEPTH * N_MOD, d)
    dmod_all = _allgather8(dmod, name="gather_dmod").reshape(n_ex, DEPTH, N_MOD * d)
    gw, gb = [], []
    for l in range(DEPTH):
        dm = dmod_all[:, l]
        dm_cols = lax.dynamic_slice(dm, (0, chip * ada_cols), (n_ex, ada_cols))
        gw.append(_mm(c_all, dm_cols, dims="tn", name=f"l{l}_dw_ada", tm=d, tn=ada_cols, tk=n_ex, a_fn=_silu))
        gb.append(_sum_leading(dm.reshape(n_ex, N_MOD * d // FLAT_W, FLAT_W), name=f"l{l}_db_ada").reshape(-1))
    grad["w_ada"] = jnp.stack(gw, axis=0)
    grad["b_ada"] = jnp.stack(gb, axis=0)

    delta, new_m, new_v = {}, {}, {}
    for nm in W_NAMES:
        delta[nm], new_m[nm], new_v[nm] = _adamw(weights[nm], grad[nm], mom_m[nm], mom_v[nm], name=f"adamw_{nm}")
    return (loss, grad_x, *[grad[nm] for nm in W_NAMES], *[delta[nm] for nm in W_NAMES],
            *[new_m[nm] for nm in W_NAMES], *[new_v[nm] for nm in W_NAMES])
```

```python
import functools
import math

import jax
import jax.numpy as jnp
from jax import lax
from jax.experimental import pallas as pl
from jax.experimental.pallas import tpu as pltpu

F32 = jnp.float32
BF16 = jnp.bfloat16

D_MODEL = 1024
DEPTH = 2
D_GMLP = 512
GROUPS = 8
GROUP_DIM = 64
CHUNK = 128
HEADS = 8
NOPE = 64
ROPE = 32
HEAD_PAD = 128
Q_RANK = 256
KV_RANK = 128
D_FF = 4096
N_MOD = 6
MOD_ROWS = 8
EPS = 1e-6
ROPE_THETA = 10000.0
D_IN = 1440
D_IN_PAD = 1536
ATTN_SCALE = (NOPE + ROPE) ** -0.5
N_CHIPS = 4
N_DEV = 8

ADAM_LR = 0.001
ADAM_B1 = 0.9
ADAM_B2 = 0.999
ADAM_EPS = 1e-08
ADAM_WD = 0.01
ADAM_STEP = 10

VMEM_LIMIT = 48 * 1024 * 1024
FLAT_W = 1024
ROW_ALIGN = 512

NN = (((1,), (0,)), ((), ()))
NT = (((1,), (1,)), ((), ()))
TN = (((0,), (0,)), ((), ()))
MESH = pl.DeviceIdType.MESH

SHIFT1, SCALE1, GATE1, SHIFT2, SCALE2, GATE2 = range(6)

FSDP_SECTIONS = (("w_in", 360), ("w_uq", 48), ("w_ukv", 32), ("w_out", 256), ("w_ff1", 1024), ("w_ff2", 1024))


def _cparams(vmem=VMEM_LIMIT):
    return pltpu.CompilerParams(vmem_limit_bytes=vmem)


def _dot(a, b, dims=NN):
    return lax.dot_general(a, b, dims, preferred_element_type=F32)


def _iota(shape, axis):
    return lax.broadcasted_iota(jnp.int32, shape, axis)


def _gelu(x):
    k = math.sqrt(2.0 / math.pi)
    return 0.5 * x * (1.0 + jnp.tanh(k * (x + 0.044715 * (x * x * x))))


def _gelu_grad(x):
    k = math.sqrt(2.0 / math.pi)
    t = jnp.tanh(k * (x + 0.044715 * (x * x * x)))
    return 0.5 * (1.0 + t) + 0.5 * x * (1.0 - t * t) * (k * (1.0 + 3.0 * 0.044715 * (x * x)))


def _rms_fwd(x, g, n):
    r = lax.rsqrt(jnp.sum(x * x, axis=-1, keepdims=True) * (1.0 / n) + EPS)
    return x * r * g


def _rms_bwd(x, g, dy, n):
    r = lax.rsqrt(jnp.sum(x * x, axis=-1, keepdims=True) * (1.0 / n) + EPS)
    xh = x * r
    dxh = dy * g
    dx = r * (dxh - xh * (jnp.sum(dxh * xh, axis=-1, keepdims=True) * (1.0 / n)))
    dg = jnp.sum(dy * xh, axis=0, keepdims=True)
    return dx, dg


def _pick_rows(rows, limit):
    if rows <= limit:
        return rows
    for t in range(limit, 7, -8):
        if rows % t == 0:
            return t
    return rows


def _mm(a, b, *, dims, name, tm=512, tn=1024, tk=1024, out_dtypes=(F32,), epilogue=None,
        extras=(), extra_specs=(), a_fn=None):
    if dims == "tn":
        kk, m = a.shape
    else:
        m, kk = a.shape
    n = b.shape[0] if dims == "nt" else b.shape[1]
    tm, tn, tk = min(tm, m), min(tn, n), min(tk, kk)
    assert m % tm == 0 and n % tn == 0 and kk % tk == 0, (name, a.shape, b.shape, tm, tn, tk)
    ni, nj, nk = m // tm, n // tn, kk // tk
    if dims == "tn":
        a_spec = pl.BlockSpec((tk, tm), lambda i, j, k: (k, i))
    else:
        a_spec = pl.BlockSpec((tm, tk), lambda i, j, k: (i, k))
    if dims == "nt":
        b_spec = pl.BlockSpec((tn, tk), lambda i, j, k: (j, k))
    else:
        b_spec = pl.BlockSpec((tk, tn), lambda i, j, k: (k, j))
    o_spec = pl.BlockSpec((tm, tn), lambda i, j, k: (i, j))
    dn = {"nn": NN, "nt": NT, "tn": TN}[dims]
    n_ex, n_out = len(extras), len(out_dtypes)
    e_specs = [o_spec if s is None else s for s in (tuple(extra_specs) + (None,) * n_ex)[:n_ex]]

    def body(*refs):
        a_ref, b_ref = refs[0], refs[1]
        e_refs = refs[2:2 + n_ex]
        o_refs = refs[2 + n_ex:2 + n_ex + n_out]
        av = a_ref[...]
        if a_fn is not None:
            av = a_fn(av)
        part = _dot(av.astype(BF16), b_ref[...].astype(BF16), dn)

        def finish(acc):
            outs = (acc,) if epilogue is None else epilogue(acc, *[e[...] for e in e_refs])
            for o_ref, o in zip(o_refs, outs):
                o_ref[...] = o.astype(o_ref.dtype)

        if nk == 1:
            finish(part)
        else:
            acc_ref = refs[-1]
            k = pl.program_id(2)

            @pl.when(k == 0)
            def _():
                acc_ref[...] = part

            @pl.when(k > 0)
            def _():
                acc_ref[...] += part

            @pl.when(k == nk - 1)
            def _():
                finish(acc_ref[...])

    outs = pl.pallas_call(
        body, name=name, grid=(ni, nj, nk),
        in_specs=[a_spec, b_spec] + e_specs,
        out_specs=[o_spec] * n_out,
        out_shape=[jax.ShapeDtypeStruct((m, n), dt) for dt in out_dtypes],
        scratch_shapes=[pltpu.VMEM((tm, tn), F32)] if nk > 1 else [],
        compiler_params=_cparams(),
    )(a, b, *extras)
    return outs[0] if n_out == 1 else outs


def _mod_spec(tm, tn, seq):
    return pl.BlockSpec((1, MOD_ROWS, tn), lambda i, j, k: ((i * tm) // seq, 0, j))


def _normmod_fwd(x3, g, mod, shift_row, scale_row, *, name, tb=256):
    bsz, seq, d = x3.shape
    tb = min(tb, seq)

    def body(x_ref, g_ref, mod_ref, h_ref):
        m = mod_ref[0]
        nrm = _rms_fwd(x_ref[0], g_ref[...], d)
        h = nrm * (1.0 + m[scale_row:scale_row + 1, :]) + m[shift_row:shift_row + 1, :]
        h_ref[0] = h.astype(BF16)

    return pl.pallas_call(
        body, name=name, grid=(bsz, seq // tb),
        in_specs=[pl.BlockSpec((1, tb, d), lambda b, i: (b, i, 0)),
                  pl.BlockSpec((1, d), lambda b, i: (0, 0)),
                  pl.BlockSpec((1, MOD_ROWS, d), lambda b, i: (b, 0, 0))],
        out_specs=pl.BlockSpec((1, tb, d), lambda b, i: (b, i, 0)),
        out_shape=jax.ShapeDtypeStruct((bsz, seq, d), BF16),
        compiler_params=_cparams(),
    )(x3, g, mod)


def _pair_mean_exact(x, lo):
    s_lo = jnp.sum(jnp.where(lo, x, 0.0), axis=-1, keepdims=True)
    s_hi = jnp.sum(jnp.where(lo, 0.0, x), axis=-1, keepdims=True)
    return jnp.where(lo, s_lo, s_hi) * (1.0 / GROUP_DIM)


def _gmlp_pair_fwd(gv_p, w0, w1, bias, lo):
    mu = _pair_mean_exact(gv_p, lo)
    dlt = gv_p - mu
    var = _pair_mean_exact(dlt * dlt, lo)
    rstd = lax.rsqrt(var + EPS)
    vn = dlt * rstd
    vnb = vn.astype(BF16)
    mixed = jnp.where(lo, _dot(w0, vnb), _dot(w1, vnb)) + bias
    return vn, vnb, rstd, mixed


def _tril_bf16(w):
    t = w.shape[-1]
    return jnp.where(_iota((t, t), 1) <= _iota((t, t), 0), w, 0.0).astype(BF16)


def _gmlp_fwd(z3, ws, bexp, g_out, *, name):
    bsz, seq, _ = z3.shape
    nc = seq // CHUNK

    def body(u_ref, v_ref, ws_ref, b_ref, g_ref, y_ref):
        lo = _iota((CHUNK, 128), 1) < GROUP_DIM
        gu = _gelu(u_ref[0])
        gv = _gelu(v_ref[0])
        parts = []
        for p in range(GROUPS // 2):
            sl = slice(128 * p, 128 * p + 128)
            w0 = _tril_bf16(ws_ref[2 * p])
            w1 = _tril_bf16(ws_ref[2 * p + 1])
            _, _, _, mixed = _gmlp_pair_fwd(gv[:, sl], w0, w1, b_ref[p], lo)
            parts.append(gu[:, sl] * mixed)
        yg = jnp.concatenate(parts, axis=1)
        y_ref[0] = _rms_fwd(yg, g_ref[...], D_GMLP).astype(BF16)

    return pl.pallas_call(
        body, name=name, grid=(bsz, nc),
        in_specs=[pl.BlockSpec((1, CHUNK, D_GMLP), lambda b, i: (b, i, 0)),
                  pl.BlockSpec((1, CHUNK, D_GMLP), lambda b, i: (b, i, 1)),
                  pl.BlockSpec((GROUPS, CHUNK, CHUNK), lambda b, i: (0, 0, 0)),
                  pl.BlockSpec((GROUPS // 2, CHUNK, 128), lambda b, i: (0, 0, 0)),
                  pl.BlockSpec((1, D_GMLP), lambda b, i: (0, 0))],
        out_specs=pl.BlockSpec((1, CHUNK, D_GMLP), lambda b, i: (b, i, 0)),
        out_shape=jax.ShapeDtypeStruct((bsz, seq, D_GMLP), BF16),
        compiler_params=_cparams(),
    )(z3, z3, ws, bexp, g_out)


def _gmlp_bwd(z3, dyn3, ws, wst, bexp, g_out, *, name):
    bsz, seq, _ = z3.shape
    nc = seq // CHUNK
    npair = GROUPS // 2

    def body(u_ref, v_ref, dy_ref, ws_ref, wst_ref, b_ref, g_ref, duv_ref, dws_ref, dbs_ref, dg_ref, dbacc):
        first = jnp.logical_and(pl.program_id(0) == 0, pl.program_id(1) == 0)
        last = jnp.logical_and(pl.program_id(0) == bsz - 1, pl.program_id(1) == nc - 1)

        @pl.when(first)
        def _():
            dws_ref[...] = jnp.zeros_like(dws_ref)
            dg_ref[...] = jnp.zeros_like(dg_ref)
            dbacc[...] = jnp.zeros_like(dbacc)

        lo = _iota((CHUNK, 128), 1) < GROUP_DIM
        tril = _iota((CHUNK, CHUNK), 1) <= _iota((CHUNK, CHUNK), 0)
        u = u_ref[0]
        v = v_ref[0]
        gu = _gelu(u)
        gv = _gelu(v)
        fwd = []
        for p in range(npair):
            sl = slice(128 * p, 128 * p + 128)
            w0 = _tril_bf16(ws_ref[2 * p])
            w1 = _tril_bf16(ws_ref[2 * p + 1])
            fwd.append(_gmlp_pair_fwd(gv[:, sl], w0, w1, b_ref[p], lo))
        yg = jnp.concatenate([gu[:, 128 * p:128 * p + 128] * fwd[p][3] for p in range(npair)], axis=1)
        dyg, dg = _rms_bwd(yg, g_ref[...], dy_ref[0], D_GMLP)
        dg_ref[...] += dg
        du_parts, dv_parts = [], []
        for p in range(npair):
            sl = slice(128 * p, 128 * p + 128)
            vn, vnb, rstd, mixed = fwd[p]
            dyg_p = dyg[:, sl]
            dmixed = dyg_p * gu[:, sl]
            dbacc[p] += dmixed
            dm_lo = jnp.where(lo, dmixed, 0.0).astype(BF16)
            dm_hi = jnp.where(lo, 0.0, dmixed).astype(BF16)
            dws_ref[2 * p] += jnp.where(tril, _dot(dm_lo, vnb, NT), 0.0)
            dws_ref[2 * p + 1] += jnp.where(tril, _dot(dm_hi, vnb, NT), 0.0)
            dmb = dmixed.astype(BF16)
            dvn = jnp.where(lo, _dot(wst_ref[2 * p], dmb), _dot(wst_ref[2 * p + 1], dmb))
            dgv = rstd * (dvn - _pair_mean_exact(dvn, lo) - vn * _pair_mean_exact(dvn * vn, lo))
            dv_parts.append(dgv * _gelu_grad(v[:, sl]))
            du_parts.append(dyg_p * mixed * _gelu_grad(u[:, sl]))
        duv_ref[0] = jnp.concatenate(du_parts + dv_parts, axis=1).astype(BF16)

        @pl.when(last)
        def _():
            sel = jnp.where(_iota((8, 128), 0) == 0, (_iota((8, 128), 1) < GROUP_DIM).astype(F32),
                            jnp.where(_iota((8, 128), 0) == 1, (_iota((8, 128), 1) >= GROUP_DIM).astype(F32), 0.0))
            for p in range(npair):
                dbs_ref[p] = lax.dot_general(sel, dbacc[p], NT, precision=lax.Precision.HIGHEST,
                                             preferred_element_type=F32)

    duv, dws, dbs, dg = pl.pallas_call(
        body, name=name, grid=(bsz, nc),
        in_specs=[pl.BlockSpec((1, CHUNK, D_GMLP), lambda b, i: (b, i, 0)),
                  pl.BlockSpec((1, CHUNK, D_GMLP), lambda b, i: (b, i, 1)),
                  pl.BlockSpec((1, CHUNK, D_GMLP), lambda b, i: (b, i, 0)),
                  pl.BlockSpec((GROUPS, CHUNK, CHUNK), lambda b, i: (0, 0, 0)),
                  pl.BlockSpec((GROUPS, CHUNK, CHUNK), lambda b, i: (0, 0, 0)),
                  pl.BlockSpec((npair, CHUNK, 128), lambda b, i: (0, 0, 0)),
                  pl.BlockSpec((1, D_GMLP), lambda b, i: (0, 0))],
        out_specs=[pl.BlockSpec((1, CHUNK, 2 * D_GMLP), lambda b, i: (b, i, 0)),
                   pl.BlockSpec((GROUPS, CHUNK, CHUNK), lambda b, i: (0, 0, 0)),
                   pl.BlockSpec((npair, 8, CHUNK), lambda b, i: (0, 0, 0)),
                   pl.BlockSpec((1, D_GMLP), lambda b, i: (0, 0))],
        out_shape=[jax.ShapeDtypeStruct((bsz, seq, 2 * D_GMLP), BF16),
                   jax.ShapeDtypeStruct((GROUPS, CHUNK, CHUNK), F32),
                   jax.ShapeDtypeStruct((npair, 8, CHUNK), F32),
                   jax.ShapeDtypeStruct((1, D_GMLP), F32)],
        scratch_shapes=[pltpu.VMEM((npair, CHUNK, 128), F32)],
        compiler_params=_cparams(),
    )(z3, z3, dyn3, ws, wst, bexp, g_out)
    return duv, dws, dbs[:, :2, :].reshape(GROUPS, CHUNK), dg


def _partner(x):
    width = x.shape[-1]
    lane = _iota(x.shape, x.ndim - 1) % HEAD_PAD
    up = pltpu.roll(x, width - ROPE // 2, x.ndim - 1)
    down = pltpu.roll(x, ROPE // 2, x.ndim - 1)
    first = jnp.logical_and(lane >= NOPE, lane < NOPE + ROPE // 2)
    second = jnp.logical_and(lane >= NOPE + ROPE // 2, lane < NOPE + ROPE)
    return jnp.where(first, up, jnp.where(second, down, 0.0))


def _mla_prep_fwd(z3, g_q, g_kv, w_uq, w_ukv, ctab, stab, *, name, tb=256):
    bsz, seq, _ = z3.shape
    tb = min(tb, seq)
    hw = HEADS * HEAD_PAD

    def body(ql_ref, kvl_ref, krl_ref, gq_ref, gkv_ref, wuq_ref, wukv_ref, c_ref, s_ref, q_ref, kv_ref, kr_ref):
        cq = _rms_fwd(ql_ref[0], gq_ref[...], Q_RANK).astype(BF16)
        q = _dot(cq, wuq_ref[...])
        c1, s1 = c_ref[0], s_ref[0]
        c8, s8 = jnp.tile(c1, (1, HEADS)), jnp.tile(s1, (1, HEADS))
        q_ref[0] = (q * c8 + _partner(q) * s8).astype(BF16)
        ckv = _rms_fwd(kvl_ref[0], gkv_ref[...], KV_RANK).astype(BF16)
        kv_ref[0] = _dot(ckv, wukv_ref[...]).astype(BF16)
        kr = krl_ref[0]
        kr_ref[0] = (kr * c1 + _partner(kr) * s1).astype(BF16)

    return pl.pallas_call(
        body, name=name, grid=(bsz, seq // tb),
        in_specs=[pl.BlockSpec((1, tb, Q_RANK), lambda b, i: (b, i, 4)),
                  pl.BlockSpec((1, tb, KV_RANK), lambda b, i: (b, i, 10)),
                  pl.BlockSpec((1, tb, HEAD_PAD), lambda b, i: (b, i, 11)),
                  pl.BlockSpec((1, Q_RANK), lambda b, i: (0, 0)),
                  pl.BlockSpec((1, KV_RANK), lambda b, i: (0, 0)),
                  pl.BlockSpec((Q_RANK, hw), lambda b, i: (0, 0)),
                  pl.BlockSpec((KV_RANK, hw), lambda b, i: (0, 0)),
                  pl.BlockSpec((1, tb, HEAD_PAD), lambda b, i: (b, i, 0)),
                  pl.BlockSpec((1, tb, HEAD_PAD), lambda b, i: (b, i, 0))],
        out_specs=[pl.BlockSpec((1, tb, hw), lambda b, i: (b, i, 0)),
                   pl.BlockSpec((1, tb, hw), lambda b, i: (b, i, 0)),
                   pl.BlockSpec((1, tb, HEAD_PAD), lambda b, i: (b, i, 0))],
        out_shape=[jax.ShapeDtypeStruct((bsz, seq, hw), BF16),
                   jax.ShapeDtypeStruct((bsz, seq, hw), BF16),
                   jax.ShapeDtypeStruct((bsz, seq, HEAD_PAD), BF16)],
        compiler_params=_cparams(),
    )(z3, z3, z3, g_q, g_kv, w_uq, w_ukv, ctab, stab)


def _mla_prep_bwd(z3, dq3, dk3, dv3, g_q, g_kv, w_uq, w_ukv, ctab, stab, *, name, tb=256):
    bsz, seq, _ = z3.shape
    tb = min(tb, seq)
    hw = HEADS * HEAD_PAD
    nb = seq // tb

    def body(ql_ref, kvl_ref, dq_ref, dk_ref, dv_ref, gq_ref, gkv_ref, wuq_ref, wukv_ref, c_ref, s_ref,
             dz_ref, cq_ref, dqb_ref, ckv_ref, dkvb_ref, dgq_ref, dgkv_ref):
        @pl.when(jnp.logical_and(pl.program_id(0) == 0, pl.program_id(1) == 0))
        def _():
            dgq_ref[...] = jnp.zeros_like(dgq_ref)
            dgkv_ref[...] = jnp.zeros_like(dgkv_ref)

        c1, s1 = c_ref[0], s_ref[0]
        c8, s8 = jnp.tile(c1, (1, HEADS)), jnp.tile(s1, (1, HEADS))
        dqr = dq_ref[0]
        dqb = (dqr * c8 + _partner(dqr * s8)).astype(BF16)
        dqb_ref[0] = dqb
        ql = ql_ref[0]
        cq_ref[0] = _rms_fwd(ql, gq_ref[...], Q_RANK).astype(BF16)
        dql, dgq = _rms_bwd(ql, gq_ref[...], _dot(dqb, wuq_ref[...], NT), Q_RANK)
        dgq_ref[...] += dgq

        dk = dk_ref[0]
        lane = _iota((tb, hw), 1) % HEAD_PAD
        dkvb = jnp.where(lane < NOPE, dk, dv_ref[0]).astype(BF16)
        dkvb_ref[0] = dkvb
        kvl = kvl_ref[0]
        ckv_ref[0] = _rms_fwd(kvl, gkv_ref[...], KV_RANK).astype(BF16)
        dkvl, dgkv = _rms_bwd(kvl, gkv_ref[...], _dot(dkvb, wukv_ref[...], NT), KV_RANK)
        dgkv_ref[...] += dgkv

        dkr = dk[:, 0:HEAD_PAD]
        for h in range(1, HEADS):
            dkr = dkr + dk[:, HEAD_PAD * h:HEAD_PAD * (h + 1)]
        lane1 = _iota((tb, HEAD_PAD), 1)
        dkr = jnp.where(jnp.logical_and(lane1 >= NOPE, lane1 < NOPE + ROPE), dkr, 0.0)
        dkrl = dkr * c1 + _partner(dkr * s1)
        dz_ref[0] = jnp.concatenate([dql, dkvl, dkrl], axis=1).astype(BF16)

    return pl.pallas_call(
        body, name=name, grid=(bsz, nb),
        in_specs=[pl.BlockSpec((1, tb, Q_RANK), lambda b, i: (b, i, 4)),
                  pl.BlockSpec((1, tb, KV_RANK), lambda b, i: (b, i, 10)),
                  pl.BlockSpec((1, tb, hw), lambda b, i: (b, i, 0)),
                  pl.BlockSpec((1, tb, hw), lambda b, i: (b, i, 0)),
                  pl.BlockSpec((1, tb, hw), lambda b, i: (b, i, 0)),
                  pl.BlockSpec((1, Q_RANK), lambda b, i: (0, 0)),
                  pl.BlockSpec((1, KV_RANK), lambda b, i: (0, 0)),
                  pl.BlockSpec((Q_RANK, hw), lambda b, i: (0, 0)),
                  pl.BlockSpec((KV_RANK, hw), lambda b, i: (0, 0)),
                  pl.BlockSpec((1, tb, HEAD_PAD), lambda b, i: (b, i, 0)),
                  pl.BlockSpec((1, tb, HEAD_PAD), lambda b, i: (b, i, 0))],
        out_specs=[pl.BlockSpec((1, tb, 512), lambda b, i: (b, i, 0)),
                   pl.BlockSpec((1, tb, Q_RANK), lambda b, i: (b, i, 0)),
                   pl.BlockSpec((1, tb, hw), lambda b, i: (b, i, 0)),
                   pl.BlockSpec((1, tb, KV_RANK), lambda b, i: (b, i, 0)),
                   pl.BlockSpec((1, tb, hw), lambda b, i: (b, i, 0)),
                   pl.BlockSpec((1, Q_RANK), lambda b, i: (0, 0)),
                   pl.BlockSpec((1, KV_RANK), lambda b, i: (0, 0))],
        out_shape=[jax.ShapeDtypeStruct((bsz, seq, 512), BF16),
                   jax.ShapeDtypeStruct((bsz, seq, Q_RANK), BF16),
                   jax.ShapeDtypeStruct((bsz, seq, hw), BF16),
                   jax.ShapeDtypeStruct((bsz, seq, KV_RANK), BF16),
                   jax.ShapeDtypeStruct((bsz, seq, hw), BF16),
                   jax.ShapeDtypeStruct((1, Q_RANK), F32),
                   jax.ShapeDtypeStruct((1, KV_RANK), F32)],
        compiler_params=_cparams(),
    )(z3, z3, dq3, dk3, dv3, g_q, g_kv, w_uq, w_ukv, ctab, stab)


ATTN_HEADS_PER_STEP = 4


def _attn_specs(tq, seq, hp):
    blk = pl.BlockSpec((1, tq, hp * HEAD_PAD), lambda b, h, i: (b, i, h))
    full = pl.BlockSpec((1, seq, hp * HEAD_PAD), lambda b, h, i: (b, 0, h))
    full0 = pl.BlockSpec((1, seq, HEAD_PAD), lambda b, h, i: (b, 0, 0))
    blk0 = pl.BlockSpec((1, tq, HEAD_PAD), lambda b, h, i: (b, i, 0))
    return blk, full, full0, blk0


def _head(h):
    return slice(HEAD_PAD * h, HEAD_PAD * (h + 1))


def _attn_fwd(q3, kv3, kr3, *, name, tq=256, hp=ATTN_HEADS_PER_STEP):
    bsz, seq, hw = q3.shape
    tq = min(tq, seq)
    blk, full, full0, _ = _attn_specs(tq, seq, hp)

    def body(q_ref, kv_ref, kr_ref, o_ref, lse_ref):
        i = pl.program_id(2)
        is_nope = _iota((tq, HEAD_PAD), 1) < NOPE
        causal = _iota((tq, tq), 1) <= _iota((tq, tq), 0)

        def step(j, carry, diag):
            st = pl.multiple_of(j * tq, tq)
            krj = kr_ref[0, pl.ds(st, tq), :]
            out = []
            for h in range(hp):
                m, l, acc = carry[h]
                kvj = kv_ref[0, pl.ds(st, tq), _head(h)]
                kp = jnp.where(is_nope, kvj, krj)
                s = _dot(q_ref[0, :, _head(h)], kp, NT) * ATTN_SCALE
                if diag:
                    s = jnp.where(causal, s, -1e30)
                m_new = jnp.maximum(m, jnp.max(s, axis=1, keepdims=True))
                alpha = jnp.exp(m - m_new)
                p = jnp.exp(s - m_new)
                l = alpha * l + jnp.sum(p, axis=1, keepdims=True)
                acc = alpha * acc + _dot(p.astype(BF16), kvj)
                out.append((m_new, l, acc))
            return tuple(out)

        init = tuple((jnp.full((tq, 1), -1e30, F32), jnp.zeros((tq, 1), F32), jnp.zeros((tq, HEAD_PAD), F32))
                     for _ in range(hp))
        carry = lax.fori_loop(0, i, lambda j, c: step(j, c, False), init)
        carry = step(i, carry, True)
        for h in range(hp):
            m, l, acc = carry[h]
            o_ref[0, :, _head(h)] = jnp.where(is_nope, 0.0, acc / l)
            lse_ref[0, :, _head(h)] = jnp.broadcast_to(m + jnp.log(l), (tq, HEAD_PAD))

    return pl.pallas_call(
        body, name=name, grid=(bsz, HEADS // hp, seq // tq),
        in_specs=[blk, full, full0],
        out_specs=[blk, blk],
        out_shape=[jax.ShapeDtypeStruct((bsz, seq, hw), F32), jax.ShapeDtypeStruct((bsz, seq, hw), F32)],
        compiler_params=_cparams(),
    )(q3, kv3, kr3)


def _attn_bwd_dq(q3, kv3, kr3, do3, lse3, dl3, *, name, tq=256, hp=ATTN_HEADS_PER_STEP):
    bsz, seq, hw = q3.shape
    tq = min(tq, seq)
    blk, full, full0, _ = _attn_specs(tq, seq, hp)
    rep = tq // HEAD_PAD

    def body(q_ref, kv_ref, kr_ref, do_ref, lse_ref, dl_ref, dq_ref):
        i = pl.program_id(2)
        is_nope = _iota((tq, HEAD_PAD), 1) < NOPE
        causal = _iota((tq, tq), 1) <= _iota((tq, tq), 0)

        def step(j, carry, diag):
            st = pl.multiple_of(j * tq, tq)
            krj = kr_ref[0, pl.ds(st, tq), :]
            out = []
            for h in range(hp):
                kvj = kv_ref[0, pl.ds(st, tq), _head(h)]
                kp = jnp.where(is_nope, kvj, krj)
                vj = jnp.where(is_nope, jnp.zeros_like(kvj), kvj)
                do = jnp.where(is_nope, 0.0, do_ref[0, :, _head(h)]).astype(BF16)
                s = _dot(q_ref[0, :, _head(h)], kp, NT) * ATTN_SCALE
                if diag:
                    s = jnp.where(causal, s, -1e30)
                p = jnp.exp(s - jnp.tile(lse_ref[0, :, _head(h)], (1, rep)))
                dp = _dot(do, vj, NT)
                ds = p * (dp - jnp.tile(dl_ref[0, :, _head(h)], (1, rep))) * ATTN_SCALE
                out.append(carry[h] + _dot(ds.astype(BF16), kp))
            return tuple(out)

        init = tuple(jnp.zeros((tq, HEAD_PAD), F32) for _ in range(hp))
        carry = lax.fori_loop(0, i, lambda j, c: step(j, c, False), init)
        carry = step(i, carry, True)
        for h in range(hp):
            dq_ref[0, :, _head(h)] = carry[h]

    return pl.pallas_call(
        body, name=name, grid=(bsz, HEADS // hp, seq // tq),
        in_specs=[blk, full, full0, blk, blk, blk],
        out_specs=blk,
        out_shape=jax.ShapeDtypeStruct((bsz, seq, hw), F32),
        compiler_params=_cparams(),
    )(q3, kv3, kr3, do3, lse3, dl3)


def _attn_bwd_dkv(q3, kv3, kr3, do3, lse3, dl3, *, name, tq=256, hp=ATTN_HEADS_PER_STEP):
    bsz, seq, hw = q3.shape
    tq = min(tq, seq)
    nq = seq // tq
    blk, full, _, blk0 = _attn_specs(tq, seq, hp)
    rep = tq // HEAD_PAD

    def body(kv_ref, kr_ref, q_ref, do_ref, lse_ref, dl_ref, dk_ref, dv_ref):
        j = pl.program_id(2)
        is_nope = _iota((tq, HEAD_PAD), 1) < NOPE
        causal = _iota((tq, tq), 1) <= _iota((tq, tq), 0)

        def step(i, carry, diag):
            st = pl.multiple_of(i * tq, tq)
            out = []
            for h in range(hp):
                dk, dv = carry[h]
                kvj = kv_ref[0, :, _head(h)]
                kp = jnp.where(is_nope, kvj, kr_ref[0])
                vj = jnp.where(is_nope, jnp.zeros_like(kvj), kvj)
                qi = q_ref[0, pl.ds(st, tq), _head(h)]
                do = jnp.where(is_nope, 0.0, do_ref[0, pl.ds(st, tq), _head(h)]).astype(BF16)
                s = _dot(qi, kp, NT) * ATTN_SCALE
                if diag:
                    s = jnp.where(causal, s, -1e30)
                p = jnp.exp(s - jnp.tile(lse_ref[0, pl.ds(st, tq), _head(h)], (1, rep)))
                dv = dv + _dot(p.astype(BF16), do, TN)
                dp = _dot(do, vj, NT)
                ds = p * (dp - jnp.tile(dl_ref[0, pl.ds(st, tq), _head(h)], (1, rep))) * ATTN_SCALE
                dk = dk + _dot(ds.astype(BF16), qi, TN)
                out.append((dk, dv))
            return tuple(out)

        zero = jnp.zeros((tq, HEAD_PAD), F32)
        carry = step(j, tuple((zero, zero) for _ in range(hp)), True)
        carry = lax.fori_loop(j + 1, nq, lambda i, c: step(i, c, False), carry)
        for h in range(hp):
            dk_ref[0, :, _head(h)] = carry[h][0]
            dv_ref[0, :, _head(h)] = carry[h][1]

    return pl.pallas_call(
        body, name=name, grid=(bsz, HEADS // hp, nq),
        in_specs=[blk, blk0, full, full, full, full],
        out_specs=[blk, blk],
        out_shape=[jax.ShapeDtypeStruct((bsz, seq, hw), F32), jax.ShapeDtypeStruct((bsz, seq, hw), F32)],
        compiler_params=_cparams(),
    )(kv3, kr3, q3, do3, lse3, dl3)


def _onorm_fwd(o3, g_pad, *, name, tb=256):
    bsz, seq, hw = o3.shape
    tb = min(tb, seq)

    def body(o_ref, g_ref, y_ref):
        y_ref[0] = _rms_fwd(o_ref[0], g_ref[...], HEADS * 64).astype(BF16)

    return pl.pallas_call(
        body, name=name, grid=(bsz, seq // tb),
        in_specs=[pl.BlockSpec((1, tb, hw), lambda b, i: (b, i, 0)), pl.BlockSpec((1, hw), lambda b, i: (0, 0))],
        out_specs=pl.BlockSpec((1, tb, hw), lambda b, i: (b, i, 0)),
        out_shape=jax.ShapeDtypeStruct((bsz, seq, hw), BF16),
        compiler_params=_cparams(),
    )(o3, g_pad)


def _onorm_bwd(o3, dy3, g_pad, *, name, tb=256):
    bsz, seq, hw = o3.shape
    tb = min(tb, seq)

    def body(o_ref, dy_ref, g_ref, do_ref, dl_ref, dg_ref):
        @pl.when(jnp.logical_and(pl.program_id(0) == 0, pl.program_id(1) == 0))
        def _():
            dg_ref[...] = jnp.zeros_like(dg_ref)

        o = o_ref[0]
        do, dg = _rms_bwd(o, g_ref[...], dy_ref[0], HEADS * 64)
        dg_ref[...] += dg
        do_ref[0] = do
        prod = do * o
        parts = []
        for h in range(HEADS):
            sh = jnp.sum(prod[:, HEAD_PAD * h:HEAD_PAD * (h + 1)], axis=1, keepdims=True)
            parts.append(jnp.broadcast_to(sh, (tb, HEAD_PAD)))
        dl_ref[0] = jnp.concatenate(parts, axis=1)

    return pl.pallas_call(
        body, name=name, grid=(bsz, seq // tb),
        in_specs=[pl.BlockSpec((1, tb, hw), lambda b, i: (b, i, 0)),
                  pl.BlockSpec((1, tb, hw), lambda b, i: (b, i, 0)),
                  pl.BlockSpec((1, hw), lambda b, i: (0, 0))],
        out_specs=[pl.BlockSpec((1, tb, hw), lambda b, i: (b, i, 0)),
                   pl.BlockSpec((1, tb, hw), lambda b, i: (b, i, 0)),
                   pl.BlockSpec((1, hw), lambda b, i: (0, 0))],
        out_shape=[jax.ShapeDtypeStruct((bsz, seq, hw), F32),
                   jax.ShapeDtypeStruct((bsz, seq, hw), F32),
                   jax.ShapeDtypeStruct((1, hw), F32)],
        compiler_params=_cparams(),
    )(o3, dy3, g_pad)


def _resnode_bwd(x3, g, *, name, target3=None, dh3=None, dres3=None, mod_nm=None, rows=None,
                 branch3=None, mod_gate=None, gate_row=None, tb=256):
    bsz, seq, d = x3.shape
    tb = min(tb, seq)
    final = target3 is not None
    has_branch = branch3 is not None
    row_spec = pl.BlockSpec((1, tb, d), lambda b, i: (b, i, 0))
    vec_spec = pl.BlockSpec((1, d), lambda b, i: (0, 0))
    mod_spec = pl.BlockSpec((1, MOD_ROWS, d), lambda b, i: (b, 0, 0))

    ins, in_specs = [x3, g], [row_spec, vec_spec]
    if final:
        ins += [target3]
        in_specs += [row_spec]
    else:
        ins += [dh3, dres3, mod_nm]
        in_specs += [row_spec, row_spec, mod_spec]
    if has_branch:
        ins += [branch3, mod_gate]
        in_specs += [row_spec, mod_spec]

    out_names = ["dx", "dg"]
    out_specs = [row_spec, vec_spec]
    out_shape = [jax.ShapeDtypeStruct((bsz, seq, d), F32), jax.ShapeDtypeStruct((1, d), F32)]
    if final:
        out_names += ["loss"]
        out_specs += [pl.BlockSpec((1, 128), lambda b, i: (0, 0))]
        out_shape += [jax.ShapeDtypeStruct((1, 128), F32)]
    else:
        out_names += ["dnm"]
        out_specs += [mod_spec]
        out_shape += [jax.ShapeDtypeStruct((bsz, MOD_ROWS, d), F32)]
    if has_branch:
        out_names += ["dbr", "dgate"]
        out_specs += [row_spec, mod_spec]
        out_shape += [jax.ShapeDtypeStruct((bsz, seq, d), BF16), jax.ShapeDtypeStruct((bsz, MOD_ROWS, d), F32)]
    n_in = len(ins)

    def body(*refs):
        r = dict(zip(["x", "g"] + (["t"] if final else ["dh", "dres", "nm"]) + (["br", "gm"] if has_branch else []),
                     refs[:n_in]))
        o = dict(zip(out_names, refs[n_in:]))
        b_first = pl.program_id(1) == 0
        first = jnp.logical_and(pl.program_id(0) == 0, b_first)
        rowid = _iota((MOD_ROWS, d), 0)

        @pl.when(first)
        def _():
            o["dg"][...] = jnp.zeros_like(o["dg"])
            if final:
                o["loss"][...] = jnp.zeros_like(o["loss"])

        @pl.when(b_first)
        def _():
            if not final:
                o["dnm"][...] = jnp.zeros_like(o["dnm"])
            if has_branch:
                o["dgate"][...] = jnp.zeros_like(o["dgate"])

        x = r["x"][0]
        gv = r["g"][...]
        if final:
            e = _rms_fwd(x, gv, d) - r["t"][0]
            sq = jnp.sum(jnp.sum(e * e, axis=1, keepdims=True), axis=0, keepdims=True)
            o["loss"][...] += jnp.broadcast_to(sq * (0.5 / d), (1, 128))
            dx, dg = _rms_bwd(x, gv, e * (1.0 / d), d)
        else:
            m = r["nm"][0]
            dh = r["dh"][0]
            scale = m[rows[1]:rows[1] + 1, :]
            rstd = lax.rsqrt(jnp.sum(x * x, axis=-1, keepdims=True) * (1.0 / d) + EPS)
            xh = x * rstd
            nrm = xh * gv
            dshift = jnp.sum(dh, axis=0, keepdims=True)
            dscale = jnp.sum(dh * nrm, axis=0, keepdims=True)
            o["dnm"][0] += jnp.where(rowid == 0, dshift, jnp.where(rowid == 1, dscale, 0.0))
            dn = dh * (1.0 + scale)
            dg = jnp.sum(dn * xh, axis=0, keepdims=True)
            dxh = dn * gv
            dx = rstd * (dxh - xh * (jnp.sum(dxh * xh, axis=-1, keepdims=True) * (1.0 / d))) + r["dres"][0]
        o["dg"][...] += dg
        o["dx"][0] = dx
        if has_branch:
            gate = r["gm"][0][gate_row:gate_row + 1, :]
            o["dbr"][0] = (gate * dx).astype(BF16)
            dgate = jnp.sum(dx * r["br"][0], axis=0, keepdims=True)
            o["dgate"][0] += jnp.where(rowid == 0, dgate, 0.0)

    outs = pl.pallas_call(
        body, name=name, grid=(bsz, seq // tb),
        in_specs=in_specs, out_specs=out_specs, out_shape=out_shape,
        compiler_params=_cparams(),
    )(*ins)
    return dict(zip(out_names, outs))


def _adamw(w, g, m, v, *, name):
    shape = w.shape
    cols = shape[-1]
    rows = w.size // cols
    tr = _pick_rows(rows, max(8, (256 * 1024) // cols // 8 * 8))
    c1 = 1.0 - ADAM_B1 ** ADAM_STEP
    c2 = 1.0 - ADAM_B2 ** ADAM_STEP

    def body(w_ref, g_ref, m_ref, v_ref, d_ref, nm_ref, nv_ref):
        gg = g_ref[...]
        nm = ADAM_B1 * m_ref[...] + (1.0 - ADAM_B1) * gg
        nv = ADAM_B2 * v_ref[...] + (1.0 - ADAM_B2) * (gg * gg)
        m_hat = nm / c1
        v_hat = nv / c2
        d_ref[...] = -ADAM_LR * (m_hat / (jnp.sqrt(v_hat) + ADAM_EPS) + ADAM_WD * w_ref[...])
        nm_ref[...] = nm
        nv_ref[...] = nv

    spec = pl.BlockSpec((tr, cols), lambda i: (i, 0))
    outs = pl.pallas_call(
        body, name=name, grid=(rows // tr,),
        in_specs=[spec] * 4, out_specs=[spec] * 3,
        out_shape=[jax.ShapeDtypeStruct((rows, cols), F32)] * 3,
        compiler_params=_cparams(),
    )(*[t.reshape(rows, cols) for t in (w, g, m, v)])
    return tuple(o.reshape(shape) for o in outs)


def _sum_leading(x, *, name, tr=256):
    n, rows, cols = x.shape
    tr = _pick_rows(rows, tr)

    def body(x_ref, o_ref):
        acc = x_ref[0]
        for k in range(1, n):
            acc = acc + x_ref[k]
        o_ref[...] = acc

    return pl.pallas_call(
        body, name=name, grid=(rows // tr,),
        in_specs=[pl.BlockSpec((n, tr, cols), lambda i: (0, i, 0))],
        out_specs=pl.BlockSpec((tr, cols), lambda i: (i, 0)),
        out_shape=jax.ShapeDtypeStruct((rows, cols), F32),
        compiler_params=_cparams(),
    )(x)


def _position():
    return lax.axis_index("x"), lax.axis_index("y"), lax.axis_index("c")


def _allgather8(x, *, name, own_half=False):
    shape = (x.shape[0] // 2,) + x.shape[1:] if own_half else x.shape
    half_rows = shape[0]

    def body(x_ref, out_ref, send_sems, recv_sems, local_sem):
        px, py, pc = _position()
        me, sibling = (px, py, pc), (px, py, 1 - pc)
        chips = [(1 - px, py), (px, 1 - py), (1 - px, 1 - py)]
        src_own = x_ref.at[pl.ds(pc * half_rows, half_rows)] if own_half else x_ref

        def slot(qx, qy, qc):
            return out_ref.at[4 * qx + 2 * qy + qc]

        def copy(k, block, to, src=None):
            return pltpu.make_async_remote_copy(
                src_ref=slot(*block) if src is None else src, dst_ref=slot(*block),
                send_sem=send_sems.at[k], recv_sem=recv_sems.at[k], device_id=to, device_id_type=MESH)

        mine = pltpu.make_async_copy(src_own, slot(*me), local_sem)
        mine.start()
        first = [copy(0, me, sibling, src=src_own)]
        first += [copy(1 + j, me, (*chip, pc), src=src_own) for j, chip in enumerate(chips)]
        for cp in first:
            cp.start()
        passed = [copy(4 + j, (*chip, pc), sibling) for j, chip in enumerate(chips)]
        for j, chip in enumerate(chips):
            copy(1 + j, (*chip, pc), me).wait_recv()
            passed[j].start()
        copy(0, sibling, me).wait_recv()
        for j, chip in enumerate(chips):
            copy(4 + j, (*chip, 1 - pc), me).wait_recv()
        for cp in first + passed:
            cp.wait_send()
        mine.wait()

    return pl.pallas_call(
        body, name=name,
        out_shape=jax.ShapeDtypeStruct((N_DEV,) + shape, x.dtype),
        in_specs=[pl.BlockSpec(memory_space=pl.ANY)],
        out_specs=pl.BlockSpec(memory_space=pl.ANY),
        scratch_shapes=[pltpu.SemaphoreType.DMA((7,)), pltpu.SemaphoreType.DMA((7,)), pltpu.SemaphoreType.DMA],
    )(x)


def _sibling_other_half(g):
    n, rows, w = g.shape
    hr = rows // 2

    def body(g_ref, out_ref, send_sem, recv_sem):
        px, py, pc = _position()
        cp = pltpu.make_async_remote_copy(
            src_ref=g_ref.at[:, pl.ds((1 - pc) * hr, hr), :], dst_ref=out_ref,
            send_sem=send_sem, recv_sem=recv_sem, device_id=(px, py, 1 - pc), device_id_type=MESH)
        cp.start()
        cp.wait()

    return pl.pallas_call(
        body, name="rs_sibling_exchange",
        out_shape=jax.ShapeDtypeStruct((n, hr, w), g.dtype),
        in_specs=[pl.BlockSpec(memory_space=pl.ANY)],
        out_specs=pl.BlockSpec(memory_space=pl.ANY),
        scratch_shapes=[pltpu.SemaphoreType.DMA, pltpu.SemaphoreType.DMA],
    )(g)


def _chip_exchange(sb):
    _, rows, w = sb.shape

    def body(sb_ref, out_ref, send_sems, recv_sems):
        px, py, pc = _position()
        peers = [(px, 1 - py, pc), (1 - px, py, pc), (1 - px, 1 - py, pc)]
        cps = [pltpu.make_async_remote_copy(
            src_ref=sb_ref.at[j], dst_ref=out_ref.at[j], send_sem=send_sems.at[j], recv_sem=recv_sems.at[j],
            device_id=peer, device_id_type=MESH) for j, peer in enumerate(peers)]
        for cp in cps:
            cp.start()
        for cp in cps:
            cp.wait()

    return pl.pallas_call(
        body, name="rs_chip_exchange",
        out_shape=jax.ShapeDtypeStruct(sb.shape, sb.dtype),
        in_specs=[pl.BlockSpec(memory_space=pl.ANY)],
        out_specs=pl.BlockSpec(memory_space=pl.ANY),
        scratch_shapes=[pltpu.SemaphoreType.DMA((3,)), pltpu.SemaphoreType.DMA((3,))],
    )(sb)


def _sibling_complete(half):
    hr, w = half.shape

    def body(h_ref, out_ref, send_sem, recv_sem, local_sem):
        px, py, pc = _position()
        rows = out_ref.at[pc]
        mine = pltpu.make_async_copy(h_ref, rows, local_sem)
        mine.start()
        cp = pltpu.make_async_remote_copy(
            src_ref=h_ref, dst_ref=rows, send_sem=send_sem, recv_sem=recv_sem,
            device_id=(px, py, 1 - pc), device_id_type=MESH)
        cp.start()
        cp.wait()
        mine.wait()

    return pl.pallas_call(
        body, name="rs_sibling_complete",
        out_shape=jax.ShapeDtypeStruct((2, hr, w), half.dtype),
        in_specs=[pl.BlockSpec(memory_space=pl.ANY)],
        out_specs=pl.BlockSpec(memory_space=pl.ANY),
        scratch_shapes=[pltpu.SemaphoreType.DMA, pltpu.SemaphoreType.DMA, pltpu.SemaphoreType.DMA],
    )(half).reshape(2 * hr, w)


def _rs_partial(g, recv, ids, *, tr=256):
    _, rows, w = g.shape
    hr = rows // 2
    nb = hr // tr

    def body(ids_ref, g_ref, r_ref, o_ref):
        o_ref[0] = (g_ref[0] + r_ref[0]).astype(BF16)

    grid_spec = pltpu.PrefetchScalarGridSpec(
        num_scalar_prefetch=1, grid=(3, nb),
        in_specs=[pl.BlockSpec((1, tr, w), lambda j, i, ids: (ids[1] ^ (j + 1), ids[0] * nb + i, 0)),
                  pl.BlockSpec((1, tr, w), lambda j, i, ids: (ids[1] ^ (j + 1), i, 0))],
        out_specs=pl.BlockSpec((1, tr, w), lambda j, i, ids: (j, i, 0)))
    return pl.pallas_call(
        body, name="rs_partial", grid_spec=grid_spec,
        out_shape=jax.ShapeDtypeStruct((3, hr, w), BF16),
        compiler_params=_cparams(),
    )(ids, g, recv)


def _rs_total(g, recv, got, ids, *, tr=256):
    _, rows, w = g.shape
    hr = rows // 2
    nb = hr // tr

    def body(ids_ref, g_ref, r_ref, got_ref, o_ref):
        acc = g_ref[0] + r_ref[0]
        for j in range(3):
            acc = acc + got_ref[j].astype(F32)
        o_ref[...] = acc

    grid_spec = pltpu.PrefetchScalarGridSpec(
        num_scalar_prefetch=1, grid=(nb,),
        in_specs=[pl.BlockSpec((1, tr, w), lambda i, ids: (ids[1], ids[0] * nb + i, 0)),
                  pl.BlockSpec((1, tr, w), lambda i, ids: (ids[1], i, 0)),
                  pl.BlockSpec((3, tr, w), lambda i, ids: (0, i, 0))],
        out_specs=pl.BlockSpec((tr, w), lambda i, ids: (i, 0)))
    return pl.pallas_call(
        body, name="rs_total", grid_spec=grid_spec,
        out_shape=jax.ShapeDtypeStruct((hr, w), F32),
        compiler_params=_cparams(),
    )(ids, g, recv, got)


def _reduce_scatter(g, ids):
    recv = _sibling_other_half(g)
    got = _chip_exchange(_rs_partial(g, recv, ids))
    return _sibling_complete(_rs_total(g, recv, got, ids))


def _flat_rows():
    per_layer = sum(r for _, r in FSDP_SECTIONS)
    used = DEPTH * per_layer
    return used, -(-used // ROW_ALIGN) * ROW_ALIGN


def _cols_to_chunks(full):
    rows, cols = full.shape
    t = full.reshape(rows, N_CHIPS, cols // N_CHIPS).transpose(1, 0, 2)
    return t.reshape(N_CHIPS, -1, FLAT_W)


def _chunks_to_cols(chunks, rows, cols):
    return chunks.reshape(N_CHIPS, rows, cols // N_CHIPS).transpose(1, 0, 2).reshape(rows, cols)


def _pad_heads(w, real):
    lead = w.shape[:-1]
    t = w.reshape(lead + (HEADS, real))
    t = jnp.pad(t, [(0, 0)] * len(lead) + [(0, 0), (0, HEAD_PAD - real)])
    return t.reshape(lead + (HEADS * HEAD_PAD,))


def _unpad_heads(w, real):
    lead = w.shape[:-1]
    return w.reshape(lead + (HEADS, HEAD_PAD))[..., :real].reshape(lead + (HEADS * real,))


def _pad_value_lanes(w, axis):
    w = jnp.moveaxis(w, axis, -1)
    lead = w.shape[:-1]
    t = w.reshape(lead + (HEADS, 64))
    t = jnp.pad(t, [(0, 0)] * len(lead) + [(0, 0), (HEAD_PAD - 64, 0)])
    return jnp.moveaxis(t.reshape(lead + (HEADS * HEAD_PAD,)), -1, axis)


def _unpad_value_lanes(w, axis):
    w = jnp.moveaxis(w, axis, -1)
    lead = w.shape[:-1]
    t = w.reshape(lead + (HEADS, HEAD_PAD))[..., HEAD_PAD - 64:]
    return jnp.moveaxis(t.reshape(lead + (HEADS * 64,)), -1, axis)


def _pad_w_in(w):
    z = jnp.zeros((w.shape[0], NOPE), w.dtype)
    z2 = jnp.zeros((w.shape[0], HEAD_PAD - NOPE - ROPE), w.dtype)
    return jnp.concatenate([w[:, :1408], z, w[:, 1408:], z2], axis=1)


def _unpad_w_in(w):
    return jnp.concatenate([w[:, :1408], w[:, 1408 + NOPE:1408 + NOPE + ROPE]], axis=1)


def _rope_tables(positions):
    freqs = ROPE_THETA ** (-jnp.arange(0, ROPE, 2, dtype=F32) / ROPE)
    ang = positions.astype(F32)[..., None] * freqs
    cos, sin = jnp.cos(ang), jnp.sin(ang)
    lead = cos.shape[:-1]
    ones = jnp.ones(lead + (NOPE,), F32)
    zeros_n = jnp.zeros(lead + (NOPE,), F32)
    zeros_p = jnp.zeros(lead + (HEAD_PAD - NOPE - ROPE,), F32)
    ctab = jnp.concatenate([ones, cos, cos, zeros_p], axis=-1)
    stab = jnp.concatenate([zeros_n, -sin, sin, zeros_p], axis=-1)
    return ctab, stab


def _layer_weights(p, l):
    ws = p["gmlp_ws"][l]
    tril = jnp.tril(jnp.ones((CHUNK, CHUNK), bool))
    bs = p["gmlp_bs"][l]
    bexp = jnp.repeat(bs.reshape(GROUPS // 2, 2, CHUNK).transpose(0, 2, 1), GROUP_DIM, axis=2)
    return dict(
        w_in=_pad_w_in(p["w_in"][l]),
        w_uq=_pad_heads(p["mla_w_uq"][l], NOPE + ROPE),
        w_ukv=p["mla_w_ukv"][l],
        w_out_a=_pad_value_lanes(p["w_out"][l][D_GMLP:], 0),
        w_out_g=p["w_out"][l][:D_GMLP],
        w_ff1=p["w_ff1"][l],
        w_ff2=p["w_ff2"][l],
        ws=ws,
        wst=jnp.where(tril[None], ws, 0.0).transpose(0, 2, 1).astype(BF16),
        bexp=bexp,
        g_mix=p["norm_mix_g"][l][None],
        g_ffn=p["norm_ffn_g"][l][None],
        g_q=p["mla_q_norm_g"][l][None],
        g_kv=p["mla_kv_norm_g"][l][None],
        g_og=p["out_norm_gmlp_g"][l][None],
        g_oa=_pad_value_lanes(p["out_norm_mla_g"][l], 0)[None],
    )


def _local_step(x3, target3, positions, mods, p):
    bsz, seq, d = x3.shape
    tok = bsz * seq
    tmt = min(512, seq)
    ctab, stab = _rope_tables(positions)
    lw = [_layer_weights(p, l) for l in range(DEPTH)]

    def flat(t):
        return t.reshape(tok, t.shape[-1])

    def cube(t):
        return t.reshape(bsz, seq, t.shape[-1])

    saved = []
    x = x3
    for l in range(DEPTH):
        w, mod = lw[l], mods[l]
        h1 = _normmod_fwd(x, w["g_mix"], mod, SHIFT1, SCALE1, name=f"l{l}_normmod1")
        z = cube(_mm(flat(h1), w["w_in"], dims="nn", name=f"l{l}_w_in", tm=tmt, tn=D_IN_PAD, tk=d))
        yg = _gmlp_fwd(z, w["ws"], w["bexp"], w["g_og"], name=f"l{l}_gmlp_fwd")
        q, kv, kr = _mla_prep_fwd(z, w["g_q"], w["g_kv"], w["w_uq"], w["w_ukv"], ctab, stab, name=f"l{l}_mla_prep")
        o, lse = _attn_fwd(q, kv, kr, name=f"l{l}_attn_fwd")
        ya = _onorm_fwd(o, w["g_oa"], name=f"l{l}_onorm_fwd")
        pg = _mm(flat(yg), w["w_out_g"], dims="nn", name=f"l{l}_w_out_g", tm=tmt, tn=d, tk=D_GMLP)

        def out_epi(acc, pgv, xv, gm):
            po = acc + pgv
            return po, xv + gm[0][GATE1:GATE1 + 1, :] * po

        po, x_mid = _mm(flat(ya), w["w_out_a"], dims="nn", name=f"l{l}_w_out_a", tm=tmt, tn=d, tk=d,
                        out_dtypes=(F32, F32), epilogue=out_epi, extras=(pg, flat(x), mod),
                        extra_specs=(None, None, _mod_spec(tmt, d, seq)))
        x_mid = cube(x_mid)
        h2 = _normmod_fwd(x_mid, w["g_ffn"], mod, SHIFT2, SCALE2, name=f"l{l}_normmod2")

        def act_epi(acc):
            r = jnp.maximum(acc, 0.0)
            return acc, r * r

        a, r = _mm(flat(h2), w["w_ff1"], dims="nn", name=f"l{l}_w_ff1", tm=tmt, tn=1024, tk=d,
                   out_dtypes=(F32, BF16), epilogue=act_epi)

        def ff2_epi(acc, xv, gm):
            return acc, xv + gm[0][GATE2:GATE2 + 1, :] * acc

        f, x_out = _mm(r, w["w_ff2"], dims="nn", name=f"l{l}_w_ff2", tm=tmt, tn=d, tk=1024,
                       out_dtypes=(F32, F32), epilogue=ff2_epi, extras=(flat(x_mid), mod),
                       extra_specs=(None, _mod_spec(tmt, d, seq)))
        saved.append(dict(x_in=x, h1=h1, z=z, q=q, kv=kv, kr=kr, o=o, lse=lse, ya=ya, yg=yg, po=cube(po),
                          x_mid=x_mid, h2=h2, a=a, r=r, f=cube(f)))
        x = cube(x_out)

    grads = [dict() for _ in range(DEPTH)]
    dmods = [None] * DEPTH
    top = DEPTH - 1
    node = _resnode_bwd(x, p["final_norm_g"][None], name="final_loss_bwd", target3=target3,
                        branch3=saved[top]["f"], mod_gate=mods[top], gate_row=GATE2)
    loss_part = node["loss"][0, 0]
    d_final_g = node["dg"][0]
    for l in range(DEPTH - 1, -1, -1):
        w, mod, s = lw[l], mods[l], saved[l]
        dx_out, dfb, dgate2 = node["dx"], flat(node["dbr"]), node["dgate"][:, 0]

        def dact_epi(acc, av):
            return (acc * (2.0 * jnp.maximum(av, 0.0)),)

        da = _mm(dfb, w["w_ff2"], dims="nt", name=f"l{l}_d_r", tm=tmt, tn=1024, tk=d,
                 out_dtypes=(BF16,), epilogue=dact_epi, extras=(s["a"],))
        grads[l]["w_ff2"] = _mm(s["r"], dfb, dims="tn", name=f"l{l}_dw_ff2", tm=1024, tn=d, tk=1024)
        grads[l]["w_ff1"] = _mm(flat(s["h2"]), da, dims="tn", name=f"l{l}_dw_ff1", tm=d, tn=1024, tk=1024)
        dh2 = _mm(da, w["w_ff1"], dims="nt", name=f"l{l}_d_h2", tm=tmt, tn=d, tk=1024)
        node = _resnode_bwd(s["x_mid"], w["g_ffn"], name=f"l{l}_resnode_ffn", dh3=cube(dh2), dres3=dx_out,
                            mod_nm=mod, rows=(SHIFT2, SCALE2), branch3=s["po"], mod_gate=mod, gate_row=GATE1)
        grads[l]["norm_ffn_g"] = node["dg"][0]
        dshift2, dscale2 = node["dnm"][:, 0], node["dnm"][:, 1]
        dx_mid, dpo, dgate1 = node["dx"], flat(node["dbr"]), node["dgate"][:, 0]

        dya = _mm(dpo, w["w_out_a"], dims="nt", name=f"l{l}_d_ya", tm=tmt, tn=d, tk=d)
        dyg = _mm(dpo, w["w_out_g"], dims="nt", name=f"l{l}_d_yg", tm=tmt, tn=D_GMLP, tk=d)
        dw_out_a = _mm(flat(s["ya"]), dpo, dims="tn", name=f"l{l}_dw_out_a", tm=d, tn=d, tk=1024)
        dw_out_g = _mm(flat(s["yg"]), dpo, dims="tn", name=f"l{l}_dw_out_g", tm=D_GMLP, tn=d, tk=1024)
        grads[l]["w_out"] = jnp.concatenate([dw_out_g, _unpad_value_lanes(dw_out_a, 0)], axis=0)

        duv, dws, dbs, dg_og = _gmlp_bwd(s["z"], cube(dyg), w["ws"], w["wst"], w["bexp"], w["g_og"],
                                         name=f"l{l}_gmlp_bwd")
        grads[l]["gmlp_ws"], grads[l]["gmlp_bs"], grads[l]["out_norm_gmlp_g"] = dws, dbs, dg_og[0]

        do, dl, dg_oa = _onorm_bwd(s["o"], cube(dya), w["g_oa"], name=f"l{l}_onorm_bwd")
        grads[l]["out_norm_mla_g"] = _unpad_value_lanes(dg_oa[0], 0)
        dq = _attn_bwd_dq(s["q"], s["kv"], s["kr"], do, s["lse"], dl, name=f"l{l}_attn_dq")
        dk, dv = _attn_bwd_dkv(s["q"], s["kv"], s["kr"], do, s["lse"], dl, name=f"l{l}_attn_dkv")
        dzm, cq, dqb, ckv, dkvb, dg_q, dg_kv = _mla_prep_bwd(
            s["z"], dq, dk, dv, w["g_q"], w["g_kv"], w["w_uq"], w["w_ukv"], ctab, stab, name=f"l{l}_mla_prep_bwd")
        grads[l]["mla_q_norm_g"], grads[l]["mla_kv_norm_g"] = dg_q[0], dg_kv[0]
        dw_uq = _mm(flat(cq), flat(dqb), dims="tn", name=f"l{l}_dw_uq", tm=Q_RANK, tn=1024, tk=1024)
        grads[l]["mla_w_uq"] = _unpad_heads(dw_uq, NOPE + ROPE)
        grads[l]["mla_w_ukv"] = _mm(flat(ckv), flat(dkvb), dims="tn", name=f"l{l}_dw_ukv", tm=KV_RANK, tn=1024, tk=1024)

        h1f = flat(s["h1"])
        dw_in_uv = _mm(h1f, flat(duv), dims="tn", name=f"l{l}_dw_in_uv", tm=d, tn=1024, tk=1024)
        dw_in_m = _mm(h1f, flat(dzm), dims="tn", name=f"l{l}_dw_in_m", tm=d, tn=512, tk=1024)
        grads[l]["w_in"] = _unpad_w_in(jnp.concatenate([dw_in_uv, dw_in_m], axis=1))
        dh1_uv = _mm(flat(duv), w["w_in"][:, :1024], dims="nt", name=f"l{l}_d_h1_uv", tm=tmt, tn=d, tk=1024)
        dh1 = _mm(flat(dzm), w["w_in"][:, 1024:], dims="nt", name=f"l{l}_d_h1", tm=tmt, tn=d, tk=512,
                  epilogue=lambda acc, prev: (acc + prev,), extras=(dh1_uv,))
        if l > 0:
            node = _resnode_bwd(s["x_in"], w["g_mix"], name=f"l{l}_resnode_mix", dh3=cube(dh1), dres3=dx_mid,
                                mod_nm=mod, rows=(SHIFT1, SCALE1), branch3=saved[l - 1]["f"],
                                mod_gate=mods[l - 1], gate_row=GATE2)
        else:
            node = _resnode_bwd(s["x_in"], w["g_mix"], name=f"l{l}_resnode_mix", dh3=cube(dh1), dres3=dx_mid,
                                mod_nm=mod, rows=(SHIFT1, SCALE1))
        grads[l]["norm_mix_g"] = node["dg"][0]
        dshift1, dscale1 = node["dnm"][:, 0], node["dnm"][:, 1]
        dmods[l] = jnp.stack([dshift1, dscale1, dgate1, dshift2, dscale2, dgate2], axis=1)
    return loss_part, node["dx"], grads, d_final_g, dmods


W_NAMES = ("w_ada", "b_ada", "norm_mix_g", "w_in", "gmlp_ws", "gmlp_bs", "mla_q_norm_g", "mla_kv_norm_g",
           "mla_w_uq", "mla_w_ukv", "out_norm_gmlp_g", "out_norm_mla_g", "w_out", "norm_ffn_g", "w_ff1", "w_ff2",
           "final_norm_g")
FLAT_KEY = {"w_in": "w_in", "w_uq": "mla_w_uq", "w_ukv": "mla_w_ukv", "w_out": "w_out", "w_ff1": "w_ff1",
            "w_ff2": "w_ff2"}
COL_SHARDED = ("w_in", "w_uq", "w_ukv", "w_ff1")
FULL_SHAPE = {"w_in": (D_MODEL, D_IN), "w_uq": (Q_RANK, HEADS * (NOPE + ROPE)), "w_ukv": (KV_RANK, HEADS * 128),
              "w_out": (D_MODEL, D_MODEL), "w_ff1": (D_MODEL, D_FF), "w_ff2": (D_FF, D_MODEL)}
SMALL_NAMES = ("norm_mix_g", "gmlp_ws", "gmlp_bs", "mla_q_norm_g", "mla_kv_norm_g", "out_norm_gmlp_g",
               "out_norm_mla_g", "norm_ffn_g", "final_norm_g")


def _silu(v):
    return v * (1.0 / (1.0 + jnp.exp(-v)))


def kernel(x, c, positions, w_ada, b_ada, norm_mix_g, w_in, gmlp_ws, gmlp_bs, mla_q_norm_g, mla_kv_norm_g, mla_w_uq, mla_w_ukv, out_norm_gmlp_g, out_norm_mla_g, w_out, norm_ffn_g, w_ff1, w_ff2, final_norm_g, loss_target, m_w_ada, m_b_ada, m_norm_mix_g, m_w_in, m_gmlp_ws, m_gmlp_bs, m_mla_q_norm_g, m_mla_kv_norm_g, m_mla_w_uq, m_mla_w_ukv, m_out_norm_gmlp_g, m_out_norm_mla_g, m_w_out, m_norm_ffn_g, m_w_ff1, m_w_ff2, m_final_norm_g, v_w_ada, v_b_ada, v_norm_mix_g, v_w_in, v_gmlp_ws, v_gmlp_bs, v_mla_q_norm_g, v_mla_kv_norm_g, v_mla_w_uq, v_mla_w_ukv, v_out_norm_gmlp_g, v_out_norm_mla_g, v_w_out, v_norm_ffn_g, v_w_ff1, v_w_ff2, v_final_norm_g):
    weights = dict(w_ada=w_ada, b_ada=b_ada, norm_mix_g=norm_mix_g, w_in=w_in, gmlp_ws=gmlp_ws, gmlp_bs=gmlp_bs,
                   mla_q_norm_g=mla_q_norm_g, mla_kv_norm_g=mla_kv_norm_g, mla_w_uq=mla_w_uq, mla_w_ukv=mla_w_ukv,
                   out_norm_gmlp_g=out_norm_gmlp_g, out_norm_mla_g=out_norm_mla_g, w_out=w_out,
                   norm_ffn_g=norm_ffn_g, w_ff1=w_ff1, w_ff2=w_ff2, final_norm_g=final_norm_g)
    mom_m = dict(zip(W_NAMES, (m_w_ada, m_b_ada, m_norm_mix_g, m_w_in, m_gmlp_ws, m_gmlp_bs, m_mla_q_norm_g,
                               m_mla_kv_norm_g, m_mla_w_uq, m_mla_w_ukv, m_out_norm_gmlp_g, m_out_norm_mla_g,
                               m_w_out, m_norm_ffn_g, m_w_ff1, m_w_ff2, m_final_norm_g)))
    mom_v = dict(zip(W_NAMES, (v_w_ada, v_b_ada, v_norm_mix_g, v_w_in, v_gmlp_ws, v_gmlp_bs, v_mla_q_norm_g,
                               v_mla_kv_norm_g, v_mla_w_uq, v_mla_w_ukv, v_out_norm_gmlp_g, v_out_norm_mla_g,
                               v_w_out, v_norm_ffn_g, v_w_ff1, v_w_ff2, v_final_norm_g)))
    bsz, seq, d = x.shape
    px, py, pc = _position()
    chip = 2 * px + py
    dev = 2 * chip + pc
    ids = jnp.stack([pc, chip]).astype(jnp.int32)
    n_ex = N_DEV * bsz
    ada_cols = w_ada.shape[-1]

    c_all = _allgather8(c.reshape(bsz * d // 128, 128), name="gather_c").reshape(n_ex, d)
    mod_parts = []
    for l in range(DEPTH):
        bias = lax.dynamic_slice(b_ada[l], (chip * ada_cols,), (ada_cols,))[None]
        mod_parts.append(_mm(c_all, w_ada[l], dims="nn", name=f"l{l}_mod", tm=n_ex, tn=ada_cols, tk=d,
                             epilogue=lambda acc, bv: (acc + bv,), extras=(bias,),
                             extra_specs=(pl.BlockSpec((1, ada_cols), lambda i, j, k: (0, j)),), a_fn=_silu))
    mod_g = _allgather8(jnp.concatenate(mod_parts, axis=0), name="gather_mod")
    mod_g = mod_g.reshape(N_CHIPS, 2, DEPTH, n_ex, ada_cols)[:, 0]
    mod_full = mod_g.transpose(1, 2, 0, 3).reshape(DEPTH, n_ex, N_CHIPS * ada_cols)
    mod_mine = lax.dynamic_slice(mod_full, (0, dev * bsz, 0), (DEPTH, bsz, N_MOD * d))
    mod_mine = jnp.pad(mod_mine.reshape(DEPTH, bsz, N_MOD, d), ((0, 0), (0, 0), (0, MOD_ROWS - N_MOD), (0, 0)))
    mods = [mod_mine[l] for l in range(DEPTH)]

    used_rows, flat_rows = _flat_rows()
    pieces = [weights[FLAT_KEY[nm]][l].reshape(-1, FLAT_W) for l in range(DEPTH) for nm, _ in FSDP_SECTIONS]
    pieces.append(jnp.zeros((flat_rows - used_rows, FLAT_W), F32))
    w_flat = jnp.concatenate(pieces, axis=0).astype(BF16)
    w_gath = _allgather8(w_flat, name="gather_weights", own_half=True).reshape(N_CHIPS, flat_rows, FLAT_W)
    full = {FLAT_KEY[nm]: [] for nm, _ in FSDP_SECTIONS}
    off = 0
    for l in range(DEPTH):
        for nm, nrows in FSDP_SECTIONS:
            sec = w_gath[:, off:off + nrows]
            off += nrows
            rows, cols = FULL_SHAPE[nm]
            full[FLAT_KEY[nm]].append(_chunks_to_cols(sec, rows, cols) if nm in COL_SHARDED
                                      else sec.reshape(rows, cols))
    p = dict(weights)
    p.update(full)

    loss_part, grad_x, grads, d_final_g, dmods = _local_step(x, loss_target, positions, mods, p)
    loss = lax.psum(loss_part, ("x", "y", "c"))

    gpieces = []
    for l in range(DEPTH):
        for nm, nrows in FSDP_SECTIONS:
            g = grads[l][FLAT_KEY[nm]]
            gpieces.append(_cols_to_chunks(g) if nm in COL_SHARDED else g.reshape(N_CHIPS, nrows, FLAT_W))
    gpieces.append(jnp.zeros((N_CHIPS, flat_rows - used_rows, FLAT_W), F32))
    g_shard = _reduce_scatter(jnp.concatenate(gpieces, axis=1), ids)
    grad = {}
    off = 0
    per = {FLAT_KEY[nm]: [] for nm, _ in FSDP_SECTIONS}
    for l in range(DEPTH):
        for nm, nrows in FSDP_SECTIONS:
            per[FLAT_KEY[nm]].append(g_shard[off:off + nrows].reshape(weights[FLAT_KEY[nm]].shape[1:]))
            off += nrows
    for key, parts in per.items():
        grad[key] = jnp.stack(parts, axis=0)

    small = {nm: (d_final_g if nm == "final_norm_g" else jnp.stack([grads[l][nm] for l in range(DEPTH)], axis=0))
             for nm in SMALL_NAMES}
    svec = jnp.concatenate([small[nm].reshape(-1) for nm in SMALL_NAMES])
    n_small = svec.shape[0]
    srows = -(-n_small // (8 * FLAT_W)) * 8
    svec = jnp.pad(svec, (0, srows * FLAT_W - n_small)).reshape(srows, FLAT_W)
    ssum = _sum_leading(_allgather8(svec, name="gather_small_grads"), name="sum_small_grads").reshape(-1)
    off = 0
    for nm in SMALL_NAMES:
        size = weights[nm].size
        grad[nm] = ssum[off:off + size].reshape(weights[nm].shape)
        off += size

    dmod = jnp.stack(dmods, axis=1).reshape(bsz * DEPTH * N_MOD, d)
    dmod_all = _allgather8(dmod, name="gather_dmod").reshape(n_ex, DEPTH, N_MOD * d)
    gw, gb = [], []
    for l in range(DEPTH):
        dm = dmod_all[:, l]
        dm_cols = lax.dynamic_slice(dm, (0, chip * ada_cols), (n_ex, ada_cols))
        gw.append(_mm(c_all, dm_cols, dims="tn", name=f"l{l}_dw_ada", tm=d, tn=ada_cols, tk=n_ex, a_fn=_silu))
        gb.append(_sum_leading(dm.reshape(n_ex, N_MOD * d // FLAT_W, FLAT_W), name=f"l{l}_db_ada").reshape(-1))
    grad["w_ada"] = jnp.stack(gw, axis=0)
    grad["b_ada"] = jnp.stack(gb, axis=0)

    delta, new_m, new_v = {}, {}, {}
    for nm in W_NAMES:
        delta[nm], new_m[nm], new_v[nm] = _adamw(weights[nm], grad[nm], mom_m[nm], mom_v[nm], name=f"adamw_{nm}")
    return (loss, grad_x, *[grad[nm] for nm in W_NAMES], *[delta[nm] for nm in W_NAMES],
            *[new_m[nm] for nm in W_NAMES], *[new_v[nm] for nm in W_NAMES])
```

```python
import functools
import math

import jax
import jax.numpy as jnp
from jax import lax
from jax.experimental import pallas as pl
from jax.experimental.pallas import tpu as pltpu

F32 = jnp.float32
BF16 = jnp.bfloat16

D_MODEL = 1024
DEPTH = 2
D_GMLP = 512
GROUPS = 8
GROUP_DIM = 64
CHUNK = 128
HEADS = 8
NOPE = 64
ROPE = 32
HEAD_PAD = 128
Q_RANK = 256
KV_RANK = 128
D_FF = 4096
N_MOD = 6
MOD_ROWS = 8
EPS = 1e-6
ROPE_THETA = 10000.0
D_IN = 1440
D_IN_PAD = 1536
ATTN_SCALE = (NOPE + ROPE) ** -0.5
LOG2E = math.log2(math.e)
SCALE_LOG2 = ATTN_SCALE * LOG2E
N_CHIPS = 4
N_DEV = 8

ADAM_LR = 0.001
ADAM_B1 = 0.9
ADAM_B2 = 0.999
ADAM_EPS = 1e-08
ADAM_WD = 0.01
ADAM_STEP = 10

VMEM_LIMIT = 48 * 1024 * 1024
FLAT_W = 1024
ROW_ALIGN = 512

NN = (((1,), (0,)), ((), ()))
NT = (((1,), (1,)), ((), ()))
TN = (((0,), (0,)), ((), ()))
MESH = pl.DeviceIdType.MESH

SHIFT1, SCALE1, GATE1, SHIFT2, SCALE2, GATE2 = range(6)

FSDP_SECTIONS = (("w_in", 360), ("w_uq", 48), ("w_ukv", 32), ("w_out", 256), ("w_ff1", 1024), ("w_ff2", 1024))


def _cparams(vmem=VMEM_LIMIT):
    return pltpu.CompilerParams(vmem_limit_bytes=vmem)


def _dot(a, b, dims=NN):
    return lax.dot_general(a, b, dims, preferred_element_type=F32)


def _iota(shape, axis):
    return lax.broadcasted_iota(jnp.int32, shape, axis)


def _gelu(x):
    k = math.sqrt(2.0 / math.pi)
    return 0.5 * x * (1.0 + jnp.tanh(k * (x + 0.044715 * (x * x * x))))


def _gelu_grad(x):
    k = math.sqrt(2.0 / math.pi)
    t = jnp.tanh(k * (x + 0.044715 * (x * x * x)))
    return 0.5 * (1.0 + t) + 0.5 * x * (1.0 - t * t) * (k * (1.0 + 3.0 * 0.044715 * (x * x)))


def _rms_fwd(x, g, n):
    r = lax.rsqrt(jnp.sum(x * x, axis=-1, keepdims=True) * (1.0 / n) + EPS)
    return x * r * g


def _rms_bwd(x, g, dy, n):
    r = lax.rsqrt(jnp.sum(x * x, axis=-1, keepdims=True) * (1.0 / n) + EPS)
    xh = x * r
    dxh = dy * g
    dx = r * (dxh - xh * (jnp.sum(dxh * xh, axis=-1, keepdims=True) * (1.0 / n)))
    dg = jnp.sum(dy * xh, axis=0, keepdims=True)
    return dx, dg


def _pick_rows(rows, limit):
    if rows <= limit:
        return rows
    for t in range(limit, 7, -8):
        if rows % t == 0:
            return t
    return rows


def _mm(a, b, *, dims, name, tm=512, tn=1024, tk=1024, out_dtypes=(F32,), epilogue=None,
        extras=(), extra_specs=(), a_fn=None, weights_outer=False):
    if dims == "tn":
        kk, m = a.shape
    else:
        m, kk = a.shape
    n = b.shape[0] if dims == "nt" else b.shape[1]
    tm, tn, tk = min(tm, m), min(tn, n), min(tk, kk)
    assert m % tm == 0 and n % tn == 0 and kk % tk == 0, (name, a.shape, b.shape, tm, tn, tk)
    ni, nj, nk = m // tm, n // tn, kk // tk

    def spec(shape, pick):
        if weights_outer:
            return pl.BlockSpec(shape, lambda j, i, k: pick(i, j, k))
        return pl.BlockSpec(shape, pick)

    if dims == "tn":
        a_spec = spec((tk, tm), lambda i, j, k: (k, i))
    else:
        a_spec = spec((tm, tk), lambda i, j, k: (i, k))
    if dims == "nt":
        b_spec = spec((tn, tk), lambda i, j, k: (j, k))
    else:
        b_spec = spec((tk, tn), lambda i, j, k: (k, j))
    o_spec = spec((tm, tn), lambda i, j, k: (i, j))
    assert not (weights_outer and extra_specs)
    dn = {"nn": NN, "nt": NT, "tn": TN}[dims]
    n_ex, n_out = len(extras), len(out_dtypes)
    e_specs = [o_spec if s is None else s for s in (tuple(extra_specs) + (None,) * n_ex)[:n_ex]]

    def body(*refs):
        a_ref, b_ref = refs[0], refs[1]
        e_refs = refs[2:2 + n_ex]
        o_refs = refs[2 + n_ex:2 + n_ex + n_out]
        av = a_ref[...]
        if a_fn is not None:
            av = a_fn(av)
        part = _dot(av.astype(BF16), b_ref[...].astype(BF16), dn)

        def finish(acc):
            outs = (acc,) if epilogue is None else epilogue(acc, *[e[...] for e in e_refs])
            for o_ref, o in zip(o_refs, outs):
                o_ref[...] = o.astype(o_ref.dtype)

        if nk == 1:
            finish(part)
        else:
            acc_ref = refs[-1]
            k = pl.program_id(2)

            @pl.when(k == 0)
            def _():
                acc_ref[...] = part

            @pl.when(k > 0)
            def _():
                acc_ref[...] += part

            @pl.when(k == nk - 1)
            def _():
                finish(acc_ref[...])

    outs = pl.pallas_call(
        body, name=name, grid=(nj, ni, nk) if weights_outer else (ni, nj, nk),
        in_specs=[a_spec, b_spec] + e_specs,
        out_specs=[o_spec] * n_out,
        out_shape=[jax.ShapeDtypeStruct((m, n), dt) for dt in out_dtypes],
        scratch_shapes=[pltpu.VMEM((tm, tn), F32)] if nk > 1 else [],
        compiler_params=_cparams(),
    )(a, b, *extras)
    return outs[0] if n_out == 1 else outs


def _mod_spec(tm, tn, seq):
    return pl.BlockSpec((1, MOD_ROWS, tn), lambda i, j, k: ((i * tm) // seq, 0, j))


def _normmod_fwd(x3, g, mod, shift_row, scale_row, *, name, tb=256):
    bsz, seq, d = x3.shape
    tb = min(tb, seq)

    def body(x_ref, g_ref, mod_ref, h_ref):
        m = mod_ref[0]
        nrm = _rms_fwd(x_ref[0], g_ref[...], d)
        h = nrm * (1.0 + m[scale_row:scale_row + 1, :]) + m[shift_row:shift_row + 1, :]
        h_ref[0] = h.astype(BF16)

    return pl.pallas_call(
        body, name=name, grid=(bsz, seq // tb),
        in_specs=[pl.BlockSpec((1, tb, d), lambda b, i: (b, i, 0)),
                  pl.BlockSpec((1, d), lambda b, i: (0, 0)),
                  pl.BlockSpec((1, MOD_ROWS, d), lambda b, i: (b, 0, 0))],
        out_specs=pl.BlockSpec((1, tb, d), lambda b, i: (b, i, 0)),
        out_shape=jax.ShapeDtypeStruct((bsz, seq, d), BF16),
        compiler_params=_cparams(),
    )(x3, g, mod)


def _pair_mean_exact(x, lo):
    s_lo = jnp.sum(jnp.where(lo, x, 0.0), axis=-1, keepdims=True)
    s_hi = jnp.sum(jnp.where(lo, 0.0, x), axis=-1, keepdims=True)
    return jnp.where(lo, s_lo, s_hi) * (1.0 / GROUP_DIM)


def _gmlp_pair_fwd(gv_p, w0, w1, bias, lo):
    mu = _pair_mean_exact(gv_p, lo)
    dlt = gv_p - mu
    var = _pair_mean_exact(dlt * dlt, lo)
    rstd = lax.rsqrt(var + EPS)
    vn = dlt * rstd
    vnb = vn.astype(BF16)
    mixed = jnp.where(lo, _dot(w0, vnb), _dot(w1, vnb)) + bias
    return vn, vnb, rstd, mixed


def _tril_bf16(w):
    t = w.shape[-1]
    return jnp.where(_iota((t, t), 1) <= _iota((t, t), 0), w, 0.0).astype(BF16)


def _gmlp_fwd(z3, ws, bexp, g_out, *, name):
    bsz, seq, _ = z3.shape
    nc = seq // CHUNK

    def body(u_ref, v_ref, ws_ref, b_ref, g_ref, y_ref):
        lo = _iota((CHUNK, 128), 1) < GROUP_DIM
        gu = _gelu(u_ref[0])
        gv = _gelu(v_ref[0])
        parts = []
        for p in range(GROUPS // 2):
            sl = slice(128 * p, 128 * p + 128)
            w0 = _tril_bf16(ws_ref[2 * p])
            w1 = _tril_bf16(ws_ref[2 * p + 1])
            _, _, _, mixed = _gmlp_pair_fwd(gv[:, sl], w0, w1, b_ref[p], lo)
            parts.append(gu[:, sl] * mixed)
        yg = jnp.concatenate(parts, axis=1)
        y_ref[0] = _rms_fwd(yg, g_ref[...], D_GMLP).astype(BF16)

    return pl.pallas_call(
        body, name=name, grid=(bsz, nc),
        in_specs=[pl.BlockSpec((1, CHUNK, D_GMLP), lambda b, i: (b, i, 0)),
                  pl.BlockSpec((1, CHUNK, D_GMLP), lambda b, i: (b, i, 1)),
                  pl.BlockSpec((GROUPS, CHUNK, CHUNK), lambda b, i: (0, 0, 0)),
                  pl.BlockSpec((GROUPS // 2, CHUNK, 128), lambda b, i: (0, 0, 0)),
                  pl.BlockSpec((1, D_GMLP), lambda b, i: (0, 0))],
        out_specs=pl.BlockSpec((1, CHUNK, D_GMLP), lambda b, i: (b, i, 0)),
        out_shape=jax.ShapeDtypeStruct((bsz, seq, D_GMLP), BF16),
        compiler_params=_cparams(),
    )(z3, z3, ws, bexp, g_out)


def _gmlp_bwd(z3, dyn3, ws, wst, bexp, g_out, *, name):
    bsz, seq, _ = z3.shape
    nc = seq // CHUNK
    npair = GROUPS // 2

    def body(u_ref, v_ref, dy_ref, ws_ref, wst_ref, b_ref, g_ref, duv_ref, dws_ref, dbs_ref, dg_ref, dbacc):
        first = jnp.logical_and(pl.program_id(0) == 0, pl.program_id(1) == 0)
        last = jnp.logical_and(pl.program_id(0) == bsz - 1, pl.program_id(1) == nc - 1)

        @pl.when(first)
        def _():
            dws_ref[...] = jnp.zeros_like(dws_ref)
            dg_ref[...] = jnp.zeros_like(dg_ref)
            dbacc[...] = jnp.zeros_like(dbacc)

        lo = _iota((CHUNK, 128), 1) < GROUP_DIM
        tril = _iota((CHUNK, CHUNK), 1) <= _iota((CHUNK, CHUNK), 0)
        u = u_ref[0]
        v = v_ref[0]
        gu = _gelu(u)
        gv = _gelu(v)
        fwd = []
        for p in range(npair):
            sl = slice(128 * p, 128 * p + 128)
            w0 = _tril_bf16(ws_ref[2 * p])
            w1 = _tril_bf16(ws_ref[2 * p + 1])
            fwd.append(_gmlp_pair_fwd(gv[:, sl], w0, w1, b_ref[p], lo))
        yg = jnp.concatenate([gu[:, 128 * p:128 * p + 128] * fwd[p][3] for p in range(npair)], axis=1)
        dyg, dg = _rms_bwd(yg, g_ref[...], dy_ref[0], D_GMLP)
        dg_ref[...] += dg
        du_parts, dv_parts = [], []
        for p in range(npair):
            sl = slice(128 * p, 128 * p + 128)
            vn, vnb, rstd, mixed = fwd[p]
            dyg_p = dyg[:, sl]
            dmixed = dyg_p * gu[:, sl]
            dbacc[p] += dmixed
            dm_lo = jnp.where(lo, dmixed, 0.0).astype(BF16)
            dm_hi = jnp.where(lo, 0.0, dmixed).astype(BF16)
            dws_ref[2 * p] += jnp.where(tril, _dot(dm_lo, vnb, NT), 0.0)
            dws_ref[2 * p + 1] += jnp.where(tril, _dot(dm_hi, vnb, NT), 0.0)
            dmb = dmixed.astype(BF16)
            dvn = jnp.where(lo, _dot(wst_ref[2 * p], dmb), _dot(wst_ref[2 * p + 1], dmb))
            dgv = rstd * (dvn - _pair_mean_exact(dvn, lo) - vn * _pair_mean_exact(dvn * vn, lo))
            dv_parts.append(dgv * _gelu_grad(v[:, sl]))
            du_parts.append(dyg_p * mixed * _gelu_grad(u[:, sl]))
        duv_ref[0] = jnp.concatenate(du_parts + dv_parts, axis=1).astype(BF16)

        @pl.when(last)
        def _():
            sel = jnp.where(_iota((8, 128), 0) == 0, (_iota((8, 128), 1) < GROUP_DIM).astype(F32),
                            jnp.where(_iota((8, 128), 0) == 1, (_iota((8, 128), 1) >= GROUP_DIM).astype(F32), 0.0))
            for p in range(npair):
                dbs_ref[p] = lax.dot_general(sel, dbacc[p], NT, precision=lax.Precision.HIGHEST,
                                             preferred_element_type=F32)

    duv, dws, dbs, dg = pl.pallas_call(
        body, name=name, grid=(bsz, nc),
        in_specs=[pl.BlockSpec((1, CHUNK, D_GMLP), lambda b, i: (b, i, 0)),
                  pl.BlockSpec((1, CHUNK, D_GMLP), lambda b, i: (b, i, 1)),
                  pl.BlockSpec((1, CHUNK, D_GMLP), lambda b, i: (b, i, 0)),
                  pl.BlockSpec((GROUPS, CHUNK, CHUNK), lambda b, i: (0, 0, 0)),
                  pl.BlockSpec((GROUPS, CHUNK, CHUNK), lambda b, i: (0, 0, 0)),
                  pl.BlockSpec((npair, CHUNK, 128), lambda b, i: (0, 0, 0)),
                  pl.BlockSpec((1, D_GMLP), lambda b, i: (0, 0))],
        out_specs=[pl.BlockSpec((1, CHUNK, 2 * D_GMLP), lambda b, i: (b, i, 0)),
                   pl.BlockSpec((GROUPS, CHUNK, CHUNK), lambda b, i: (0, 0, 0)),
                   pl.BlockSpec((npair, 8, CHUNK), lambda b, i: (0, 0, 0)),
                   pl.BlockSpec((1, D_GMLP), lambda b, i: (0, 0))],
        out_shape=[jax.ShapeDtypeStruct((bsz, seq, 2 * D_GMLP), BF16),
                   jax.ShapeDtypeStruct((GROUPS, CHUNK, CHUNK), F32),
                   jax.ShapeDtypeStruct((npair, 8, CHUNK), F32),
                   jax.ShapeDtypeStruct((1, D_GMLP), F32)],
        scratch_shapes=[pltpu.VMEM((npair, CHUNK, 128), F32)],
        compiler_params=_cparams(),
    )(z3, z3, dyn3, ws, wst, bexp, g_out)
    return duv, dws, dbs[:, :2, :].reshape(GROUPS, CHUNK), dg


def _partner(x):
    width = x.shape[-1]
    lane = _iota(x.shape, x.ndim - 1) % HEAD_PAD
    up = pltpu.roll(x, width - ROPE // 2, x.ndim - 1)
    down = pltpu.roll(x, ROPE // 2, x.ndim - 1)
    first = jnp.logical_and(lane >= NOPE, lane < NOPE + ROPE // 2)
    second = jnp.logical_and(lane >= NOPE + ROPE // 2, lane < NOPE + ROPE)
    return jnp.where(first, up, jnp.where(second, down, 0.0))


def _mla_prep_fwd(z3, g_q, g_kv, w_uq, w_ukv, ctab, stab, *, name, tb=256):
    bsz, seq, _ = z3.shape
    tb = min(tb, seq)
    hw = HEADS * HEAD_PAD

    def body(ql_ref, kvl_ref, krl_ref, gq_ref, gkv_ref, wuq_ref, wukv_ref, c_ref, s_ref, q_ref, kv_ref, kp_ref):
        cq = _rms_fwd(ql_ref[0], gq_ref[...], Q_RANK).astype(BF16)
        q = _dot(cq, wuq_ref[...])
        c1, s1 = c_ref[0], s_ref[0]
        c8, s8 = jnp.tile(c1, (1, HEADS)), jnp.tile(s1, (1, HEADS))
        q_ref[0] = (q * c8 + _partner(q) * s8).astype(BF16)
        ckv = _rms_fwd(kvl_ref[0], gkv_ref[...], KV_RANK).astype(BF16)
        kv = _dot(ckv, wukv_ref[...])
        kv_ref[0] = kv.astype(BF16)
        kr = krl_ref[0]
        kr = kr * c1 + _partner(kr) * s1
        lane = _iota((tb, hw), 1) % HEAD_PAD
        kp_ref[0] = jnp.where(lane < NOPE, kv, jnp.tile(kr, (1, HEADS))).astype(BF16)

    return pl.pallas_call(
        body, name=name, grid=(bsz, seq // tb),
        in_specs=[pl.BlockSpec((1, tb, Q_RANK), lambda b, i: (b, i, 4)),
                  pl.BlockSpec((1, tb, KV_RANK), lambda b, i: (b, i, 10)),
                  pl.BlockSpec((1, tb, HEAD_PAD), lambda b, i: (b, i, 11)),
                  pl.BlockSpec((1, Q_RANK), lambda b, i: (0, 0)),
                  pl.BlockSpec((1, KV_RANK), lambda b, i: (0, 0)),
                  pl.BlockSpec((Q_RANK, hw), lambda b, i: (0, 0)),
                  pl.BlockSpec((KV_RANK, hw), lambda b, i: (0, 0)),
                  pl.BlockSpec((1, tb, HEAD_PAD), lambda b, i: (b, i, 0)),
                  pl.BlockSpec((1, tb, HEAD_PAD), lambda b, i: (b, i, 0))],
        out_specs=[pl.BlockSpec((1, tb, hw), lambda b, i: (b, i, 0))] * 3,
        out_shape=[jax.ShapeDtypeStruct((bsz, seq, hw), BF16)] * 3,
        compiler_params=_cparams(),
    )(z3, z3, z3, g_q, g_kv, w_uq, w_ukv, ctab, stab)


def _mla_prep_bwd(z3, dq3, dk3, dv3, g_q, g_kv, w_uq, w_ukv, ctab, stab, *, name, tb=256):
    bsz, seq, _ = z3.shape
    tb = min(tb, seq)
    hw = HEADS * HEAD_PAD
    nb = seq // tb

    def body(ql_ref, kvl_ref, dq_ref, dk_ref, dv_ref, gq_ref, gkv_ref, wuq_ref, wukv_ref, c_ref, s_ref,
             dz_ref, cq_ref, dqb_ref, ckv_ref, dkvb_ref, dgq_ref, dgkv_ref):
        @pl.when(jnp.logical_and(pl.program_id(0) == 0, pl.program_id(1) == 0))
        def _():
            dgq_ref[...] = jnp.zeros_like(dgq_ref)
            dgkv_ref[...] = jnp.zeros_like(dgkv_ref)

        c1, s1 = c_ref[0], s_ref[0]
        c8, s8 = jnp.tile(c1, (1, HEADS)), jnp.tile(s1, (1, HEADS))
        dqr = dq_ref[0]
        dqb = (dqr * c8 + _partner(dqr * s8)).astype(BF16)
        dqb_ref[0] = dqb
        ql = ql_ref[0]
        cq_ref[0] = _rms_fwd(ql, gq_ref[...], Q_RANK).astype(BF16)
        dql, dgq = _rms_bwd(ql, gq_ref[...], _dot(dqb, wuq_ref[...], NT), Q_RANK)
        dgq_ref[...] += dgq

        dk = dk_ref[0]
        lane = _iota((tb, hw), 1) % HEAD_PAD
        dkvb = jnp.where(lane < NOPE, dk, dv_ref[0]).astype(BF16)
        dkvb_ref[0] = dkvb
        kvl = kvl_ref[0]
        ckv_ref[0] = _rms_fwd(kvl, gkv_ref[...], KV_RANK).astype(BF16)
        dkvl, dgkv = _rms_bwd(kvl, gkv_ref[...], _dot(dkvb, wukv_ref[...], NT), KV_RANK)
        dgkv_ref[...] += dgkv

        dkr = dk[:, 0:HEAD_PAD]
        for h in range(1, HEADS):
            dkr = dkr + dk[:, HEAD_PAD * h:HEAD_PAD * (h + 1)]
        lane1 = _iota((tb, HEAD_PAD), 1)
        dkr = jnp.where(jnp.logical_and(lane1 >= NOPE, lane1 < NOPE + ROPE), dkr, 0.0)
        dkrl = dkr * c1 + _partner(dkr * s1)
        dz_ref[0] = jnp.concatenate([dql, dkvl, dkrl], axis=1).astype(BF16)

    return pl.pallas_call(
        body, name=name, grid=(bsz, nb),
        in_specs=[pl.BlockSpec((1, tb, Q_RANK), lambda b, i: (b, i, 4)),
                  pl.BlockSpec((1, tb, KV_RANK), lambda b, i: (b, i, 10)),
                  pl.BlockSpec((1, tb, hw), lambda b, i: (b, i, 0)),
                  pl.BlockSpec((1, tb, hw), lambda b, i: (b, i, 0)),
                  pl.BlockSpec((1, tb, hw), lambda b, i: (b, i, 0)),
                  pl.BlockSpec((1, Q_RANK), lambda b, i: (0, 0)),
                  pl.BlockSpec((1, KV_RANK), lambda b, i: (0, 0)),
                  pl.BlockSpec((Q_RANK, hw), lambda b, i: (0, 0)),
                  pl.BlockSpec((KV_RANK, hw), lambda b, i: (0, 0)),
                  pl.BlockSpec((1, tb, HEAD_PAD), lambda b, i: (b, i, 0)),
                  pl.BlockSpec((1, tb, HEAD_PAD), lambda b, i: (b, i, 0))],
        out_specs=[pl.BlockSpec((1, tb, 512), lambda b, i: (b, i, 0)),
                   pl.BlockSpec((1, tb, Q_RANK), lambda b, i: (b, i, 0)),
                   pl.BlockSpec((1, tb, hw), lambda b, i: (b, i, 0)),
                   pl.BlockSpec((1, tb, KV_RANK), lambda b, i: (b, i, 0)),
                   pl.BlockSpec((1, tb, hw), lambda b, i: (b, i, 0)),
                   pl.BlockSpec((1, Q_RANK), lambda b, i: (0, 0)),
                   pl.BlockSpec((1, KV_RANK), lambda b, i: (0, 0))],
        out_shape=[jax.ShapeDtypeStruct((bsz, seq, 512), BF16),
                   jax.ShapeDtypeStruct((bsz, seq, Q_RANK), BF16),
                   jax.ShapeDtypeStruct((bsz, seq, hw), BF16),
                   jax.ShapeDtypeStruct((bsz, seq, KV_RANK), BF16),
                   jax.ShapeDtypeStruct((bsz, seq, hw), BF16),
                   jax.ShapeDtypeStruct((1, Q_RANK), F32),
                   jax.ShapeDtypeStruct((1, KV_RANK), F32)],
        compiler_params=_cparams(),
    )(z3, z3, dq3, dk3, dv3, g_q, g_kv, w_uq, w_ukv, ctab, stab)


ATTN_HEADS_PER_STEP = 2


def _attn_specs(tq, seq, hp):
    blk = pl.BlockSpec((1, tq, hp * HEAD_PAD), lambda b, h, i: (b, i, h))
    full = pl.BlockSpec((1, seq, hp * HEAD_PAD), lambda b, h, i: (b, 0, h))
    return blk, full


def _head(h):
    return slice(HEAD_PAD * h, HEAD_PAD * (h + 1))


def _attn_fwd(q3, kv3, kp3, *, name, tq=512, hp=ATTN_HEADS_PER_STEP):
    bsz, seq, hw = q3.shape
    tq = min(tq, seq)
    blk, full = _attn_specs(tq, seq, hp)

    def body(q_ref, kv_ref, kp_ref, o_ref, lse_ref):
        i = pl.program_id(2)
        is_nope = _iota((tq, HEAD_PAD), 1) < NOPE
        causal = _iota((tq, tq), 1) <= _iota((tq, tq), 0)

        def step(j, carry, diag):
            st = pl.multiple_of(j * tq, tq)
            out = []
            for h in range(hp):
                m, l, acc = carry[h]
                kvj = kv_ref[0, pl.ds(st, tq), _head(h)]
                s = _dot(q_ref[0, :, _head(h)], kp_ref[0, pl.ds(st, tq), _head(h)], NT) * SCALE_LOG2
                if diag:
                    s = jnp.where(causal, s, -1e30)
                m_new = jnp.maximum(m, jnp.max(s, axis=1, keepdims=True))
                alpha = jnp.exp2(m - m_new)
                p = jnp.exp2(s - m_new)
                l = alpha * l + jnp.sum(p, axis=1, keepdims=True)
                acc = alpha * acc + _dot(p.astype(BF16), kvj)
                out.append((m_new, l, acc))
            return tuple(out)

        init = tuple((jnp.full((tq, 1), -1e30, F32), jnp.zeros((tq, 1), F32), jnp.zeros((tq, HEAD_PAD), F32))
                     for _ in range(hp))
        carry = lax.fori_loop(0, i, lambda j, c: step(j, c, False), init)
        carry = step(i, carry, True)
        for h in range(hp):
            m, l, acc = carry[h]
            o_ref[0, :, _head(h)] = jnp.where(is_nope, 0.0, acc / l)
            lse_ref[0, :, _head(h)] = jnp.broadcast_to(m + jnp.log(l) * LOG2E, (tq, HEAD_PAD))

    return pl.pallas_call(
        body, name=name, grid=(bsz, HEADS // hp, seq // tq),
        in_specs=[blk, full, full],
        out_specs=[blk, blk],
        out_shape=[jax.ShapeDtypeStruct((bsz, seq, hw), F32), jax.ShapeDtypeStruct((bsz, seq, hw), F32)],
        compiler_params=_cparams(),
    )(q3, kv3, kp3)


def _attn_bwd_dq(q3, kv3, kp3, do3, lse3, dl3, *, name, tq=512, hp=ATTN_HEADS_PER_STEP):
    bsz, seq, hw = q3.shape
    tq = min(tq, seq)
    blk, full = _attn_specs(tq, seq, hp)
    rep = tq // HEAD_PAD

    def body(q_ref, kv_ref, kp_ref, do_ref, lse_ref, dl_ref, dq_ref):
        i = pl.program_id(2)
        causal = _iota((tq, tq), 1) <= _iota((tq, tq), 0)

        def step(j, carry, diag):
            st = pl.multiple_of(j * tq, tq)
            out = []
            for h in range(hp):
                kp = kp_ref[0, pl.ds(st, tq), _head(h)]
                s = _dot(q_ref[0, :, _head(h)], kp, NT) * SCALE_LOG2
                if diag:
                    s = jnp.where(causal, s, -1e30)
                p = jnp.exp2(s - jnp.tile(lse_ref[0, :, _head(h)], (1, rep)))
                dp = _dot(do_ref[0, :, _head(h)], kv_ref[0, pl.ds(st, tq), _head(h)], NT)
                ds = p * (dp - jnp.tile(dl_ref[0, :, _head(h)], (1, rep)))
                out.append(carry[h] + _dot(ds.astype(BF16), kp))
            return tuple(out)

        init = tuple(jnp.zeros((tq, HEAD_PAD), F32) for _ in range(hp))
        carry = lax.fori_loop(0, i, lambda j, c: step(j, c, False), init)
        carry = step(i, carry, True)
        for h in range(hp):
            dq_ref[0, :, _head(h)] = carry[h] * ATTN_SCALE

    return pl.pallas_call(
        body, name=name, grid=(bsz, HEADS // hp, seq // tq),
        in_specs=[blk, full, full, blk, blk, blk],
        out_specs=blk,
        out_shape=jax.ShapeDtypeStruct((bsz, seq, hw), F32),
        compiler_params=_cparams(),
    )(q3, kv3, kp3, do3, lse3, dl3)


def _attn_bwd_dkv(q3, kv3, kp3, do3, lse3, dl3, *, name, tq=512, hp=ATTN_HEADS_PER_STEP):
    bsz, seq, hw = q3.shape
    tq = min(tq, seq)
    nq = seq // tq
    blk, full = _attn_specs(tq, seq, hp)
    rep = tq // HEAD_PAD

    def body(kv_ref, kp_ref, q_ref, do_ref, lse_ref, dl_ref, dk_ref, dv_ref):
        j = pl.program_id(2)
        causal = _iota((tq, tq), 1) <= _iota((tq, tq), 0)

        def step(i, carry, diag):
            st = pl.multiple_of(i * tq, tq)
            out = []
            for h in range(hp):
                dk, dv = carry[h]
                qi = q_ref[0, pl.ds(st, tq), _head(h)]
                do = do_ref[0, pl.ds(st, tq), _head(h)]
                s = _dot(qi, kp_ref[0, :, _head(h)], NT) * SCALE_LOG2
                if diag:
                    s = jnp.where(causal, s, -1e30)
                p = jnp.exp2(s - jnp.tile(lse_ref[0, pl.ds(st, tq), _head(h)], (1, rep)))
                dv = dv + _dot(p.astype(BF16), do, TN)
                dp = _dot(do, kv_ref[0, :, _head(h)], NT)
                ds = p * (dp - jnp.tile(dl_ref[0, pl.ds(st, tq), _head(h)], (1, rep)))
                dk = dk + _dot(ds.astype(BF16), qi, TN)
                out.append((dk, dv))
            return tuple(out)

        zero = jnp.zeros((tq, HEAD_PAD), F32)
        carry = step(j, tuple((zero, zero) for _ in range(hp)), True)
        carry = lax.fori_loop(j + 1, nq, lambda i, c: step(i, c, False), carry)
        for h in range(hp):
            dk_ref[0, :, _head(h)] = carry[h][0] * ATTN_SCALE
            dv_ref[0, :, _head(h)] = carry[h][1]

    return pl.pallas_call(
        body, name=name, grid=(bsz, HEADS // hp, nq),
        in_specs=[blk, blk, full, full, full, full],
        out_specs=[blk, blk],
        out_shape=[jax.ShapeDtypeStruct((bsz, seq, hw), F32), jax.ShapeDtypeStruct((bsz, seq, hw), F32)],
        compiler_params=_cparams(),
    )(kv3, kp3, q3, do3, lse3, dl3)


def _onorm_fwd(o3, g_pad, *, name, tb=256):
    bsz, seq, hw = o3.shape
    tb = min(tb, seq)

    def body(o_ref, g_ref, y_ref):
        y_ref[0] = _rms_fwd(o_ref[0], g_ref[...], HEADS * 64).astype(BF16)

    return pl.pallas_call(
        body, name=name, grid=(bsz, seq // tb),
        in_specs=[pl.BlockSpec((1, tb, hw), lambda b, i: (b, i, 0)), pl.BlockSpec((1, hw), lambda b, i: (0, 0))],
        out_specs=pl.BlockSpec((1, tb, hw), lambda b, i: (b, i, 0)),
        out_shape=jax.ShapeDtypeStruct((bsz, seq, hw), BF16),
        compiler_params=_cparams(),
    )(o3, g_pad)


def _onorm_bwd(o3, dy3, g_pad, *, name, tb=256):
    bsz, seq, hw = o3.shape
    tb = min(tb, seq)

    def body(o_ref, dy_ref, g_ref, do_ref, dl_ref, dg_ref):
        @pl.when(jnp.logical_and(pl.program_id(0) == 0, pl.program_id(1) == 0))
        def _():
            dg_ref[...] = jnp.zeros_like(dg_ref)

        o = o_ref[0]
        do, dg = _rms_bwd(o, g_ref[...], dy_ref[0], HEADS * 64)
        dg_ref[...] += dg
        do_ref[0] = do.astype(BF16)
        prod = do * o
        parts = []
        for h in range(HEADS):
            sh = jnp.sum(prod[:, HEAD_PAD * h:HEAD_PAD * (h + 1)], axis=1, keepdims=True)
            parts.append(jnp.broadcast_to(sh, (tb, HEAD_PAD)))
        dl_ref[0] = jnp.concatenate(parts, axis=1)

    return pl.pallas_call(
        body, name=name, grid=(bsz, seq // tb),
        in_specs=[pl.BlockSpec((1, tb, hw), lambda b, i: (b, i, 0)),
                  pl.BlockSpec((1, tb, hw), lambda b, i: (b, i, 0)),
                  pl.BlockSpec((1, hw), lambda b, i: (0, 0))],
        out_specs=[pl.BlockSpec((1, tb, hw), lambda b, i: (b, i, 0)),
                   pl.BlockSpec((1, tb, hw), lambda b, i: (b, i, 0)),
                   pl.BlockSpec((1, hw), lambda b, i: (0, 0))],
        out_shape=[jax.ShapeDtypeStruct((bsz, seq, hw), BF16),
                   jax.ShapeDtypeStruct((bsz, seq, hw), F32),
                   jax.ShapeDtypeStruct((1, hw), F32)],
        compiler_params=_cparams(),
    )(o3, dy3, g_pad)


def _resnode_bwd(x3, g, *, name, target3=None, dh3=None, dres3=None, mod_nm=None, rows=None,
                 branch3=None, mod_gate=None, gate_row=None, tb=256):
    bsz, seq, d = x3.shape
    tb = min(tb, seq)
    final = target3 is not None
    has_branch = branch3 is not None
    row_spec = pl.BlockSpec((1, tb, d), lambda b, i: (b, i, 0))
    vec_spec = pl.BlockSpec((1, d), lambda b, i: (0, 0))
    mod_spec = pl.BlockSpec((1, MOD_ROWS, d), lambda b, i: (b, 0, 0))

    ins, in_specs = [x3, g], [row_spec, vec_spec]
    if final:
        ins += [target3]
        in_specs += [row_spec]
    else:
        ins += [dh3, dres3, mod_nm]
        in_specs += [row_spec, row_spec, mod_spec]
    if has_branch:
        ins += [branch3, mod_gate]
        in_specs += [row_spec, mod_spec]

    out_names = ["dx", "dg"]
    out_specs = [row_spec, vec_spec]
    out_shape = [jax.ShapeDtypeStruct((bsz, seq, d), F32), jax.ShapeDtypeStruct((1, d), F32)]
    if final:
        out_names += ["loss"]
        out_specs += [pl.BlockSpec((1, 128), lambda b, i: (0, 0))]
        out_shape += [jax.ShapeDtypeStruct((1, 128), F32)]
    else:
        out_names += ["dnm"]
        out_specs += [mod_spec]
        out_shape += [jax.ShapeDtypeStruct((bsz, MOD_ROWS, d), F32)]
    if has_branch:
        out_names += ["dbr", "dgate"]
        out_specs += [row_spec, mod_spec]
        out_shape += [jax.ShapeDtypeStruct((bsz, seq, d), BF16), jax.ShapeDtypeStruct((bsz, MOD_ROWS, d), F32)]
    n_in = len(ins)

    def body(*refs):
        r = dict(zip(["x", "g"] + (["t"] if final else ["dh", "dres", "nm"]) + (["br", "gm"] if has_branch else []),
                     refs[:n_in]))
        o = dict(zip(out_names, refs[n_in:]))
        b_first = pl.program_id(1) == 0
        first = jnp.logical_and(pl.program_id(0) == 0, b_first)
        rowid = _iota((MOD_ROWS, d), 0)

        @pl.when(first)
        def _():
            o["dg"][...] = jnp.zeros_like(o["dg"])
            if final:
                o["loss"][...] = jnp.zeros_like(o["loss"])

        @pl.when(b_first)
        def _():
            if not final:
                o["dnm"][...] = jnp.zeros_like(o["dnm"])
            if has_branch:
                o["dgate"][...] = jnp.zeros_like(o["dgate"])

        x = r["x"][0]
        gv = r["g"][...]
        if final:
            e = _rms_fwd(x, gv, d) - r["t"][0]
            sq = jnp.sum(jnp.sum(e * e, axis=1, keepdims=True), axis=0, keepdims=True)
            o["loss"][...] += jnp.broadcast_to(sq * (0.5 / d), (1, 128))
            dx, dg = _rms_bwd(x, gv, e * (1.0 / d), d)
        else:
            m = r["nm"][0]
            dh = r["dh"][0]
            scale = m[rows[1]:rows[1] + 1, :]
            rstd = lax.rsqrt(jnp.sum(x * x, axis=-1, keepdims=True) * (1.0 / d) + EPS)
            xh = x * rstd
            nrm = xh * gv
            dshift = jnp.sum(dh, axis=0, keepdims=True)
            dscale = jnp.sum(dh * nrm, axis=0, keepdims=True)
            o["dnm"][0] += jnp.where(rowid == 0, dshift, jnp.where(rowid == 1, dscale, 0.0))
            dn = dh * (1.0 + scale)
            dg = jnp.sum(dn * xh, axis=0, keepdims=True)
            dxh = dn * gv
            dx = rstd * (dxh - xh * (jnp.sum(dxh * xh, axis=-1, keepdims=True) * (1.0 / d))) + r["dres"][0]
        o["dg"][...] += dg
        o["dx"][0] = dx
        if has_branch:
            gate = r["gm"][0][gate_row:gate_row + 1, :]
            o["dbr"][0] = (gate * dx).astype(BF16)
            dgate = jnp.sum(dx * r["br"][0], axis=0, keepdims=True)
            o["dgate"][0] += jnp.where(rowid == 0, dgate, 0.0)

    outs = pl.pallas_call(
        body, name=name, grid=(bsz, seq // tb),
        in_specs=in_specs, out_specs=out_specs, out_shape=out_shape,
        compiler_params=_cparams(),
    )(*ins)
    return dict(zip(out_names, outs))


def _adamw(w, g, m, v, *, name):
    shape = w.shape
    cols = shape[-1]
    rows = w.size // cols
    tr = _pick_rows(rows, max(8, (256 * 1024) // cols // 8 * 8))
    c1 = 1.0 - ADAM_B1 ** ADAM_STEP
    c2 = 1.0 - ADAM_B2 ** ADAM_STEP

    def body(w_ref, g_ref, m_ref, v_ref, d_ref, nm_ref, nv_ref):
        gg = g_ref[...]
        nm = ADAM_B1 * m_ref[...] + (1.0 - ADAM_B1) * gg
        nv = ADAM_B2 * v_ref[...] + (1.0 - ADAM_B2) * (gg * gg)
        m_hat = nm / c1
        v_hat = nv / c2
        d_ref[...] = -ADAM_LR * (m_hat / (jnp.sqrt(v_hat) + ADAM_EPS) + ADAM_WD * w_ref[...])
        nm_ref[...] = nm
        nv_ref[...] = nv

    spec = pl.BlockSpec((tr, cols), lambda i: (i, 0))
    outs = pl.pallas_call(
        body, name=name, grid=(rows // tr,),
        in_specs=[spec] * 4, out_specs=[spec] * 3,
        out_shape=[jax.ShapeDtypeStruct((rows, cols), F32)] * 3,
        compiler_params=_cparams(),
    )(*[t.reshape(rows, cols) for t in (w, g, m, v)])
    return tuple(o.reshape(shape) for o in outs)


def _sum_leading(x, *, name, tr=256):
    n, rows, cols = x.shape
    tr = _pick_rows(rows, tr)

    def body(x_ref, o_ref):
        acc = x_ref[0]
        for k in range(1, n):
            acc = acc + x_ref[k]
        o_ref[...] = acc

    return pl.pallas_call(
        body, name=name, grid=(rows // tr,),
        in_specs=[pl.BlockSpec((n, tr, cols), lambda i: (0, i, 0))],
        out_specs=pl.BlockSpec((tr, cols), lambda i: (i, 0)),
        out_shape=jax.ShapeDtypeStruct((rows, cols), F32),
        compiler_params=_cparams(),
    )(x)


def _position():
    return lax.axis_index("x"), lax.axis_index("y"), lax.axis_index("c")


def _allgather8(x, *, name, own_half=False):
    shape = (x.shape[0] // 2,) + x.shape[1:] if own_half else x.shape
    half_rows = shape[0]

    def body(x_ref, out_ref, send_sems, recv_sems, local_sem):
        px, py, pc = _position()
        me, sibling = (px, py, pc), (px, py, 1 - pc)
        chips = [(1 - px, py), (px, 1 - py), (1 - px, 1 - py)]
        src_own = x_ref.at[pl.ds(pc * half_rows, half_rows)] if own_half else x_ref

        def slot(qx, qy, qc):
            return out_ref.at[4 * qx + 2 * qy + qc]

        def copy(k, block, to, src=None):
            return pltpu.make_async_remote_copy(
                src_ref=slot(*block) if src is None else src, dst_ref=slot(*block),
                send_sem=send_sems.at[k], recv_sem=recv_sems.at[k], device_id=to, device_id_type=MESH)

        mine = pltpu.make_async_copy(src_own, slot(*me), local_sem)
        mine.start()
        first = [copy(0, me, sibling, src=src_own)]
        first += [copy(1 + j, me, (*chip, pc), src=src_own) for j, chip in enumerate(chips)]
        for cp in first:
            cp.start()
        passed = [copy(4 + j, (*chip, pc), sibling) for j, chip in enumerate(chips)]
        for j, chip in enumerate(chips):
            copy(1 + j, (*chip, pc), me).wait_recv()
            passed[j].start()
        copy(0, sibling, me).wait_recv()
        for j, chip in enumerate(chips):
            copy(4 + j, (*chip, 1 - pc), me).wait_recv()
        for cp in first + passed:
            cp.wait_send()
        mine.wait()

    return pl.pallas_call(
        body, name=name,
        out_shape=jax.ShapeDtypeStruct((N_DEV,) + shape, x.dtype),
        in_specs=[pl.BlockSpec(memory_space=pl.ANY)],
        out_specs=pl.BlockSpec(memory_space=pl.ANY),
        scratch_shapes=[pltpu.SemaphoreType.DMA((7,)), pltpu.SemaphoreType.DMA((7,)), pltpu.SemaphoreType.DMA],
    )(x)


def _sibling_other_half(g):
    n, rows, w = g.shape
    hr = rows // 2

    def body(g_ref, out_ref, send_sem, recv_sem):
        px, py, pc = _position()
        cp = pltpu.make_async_remote_copy(
            src_ref=g_ref.at[:, pl.ds((1 - pc) * hr, hr), :], dst_ref=out_ref,
            send_sem=send_sem, recv_sem=recv_sem, device_id=(px, py, 1 - pc), device_id_type=MESH)
        cp.start()
        cp.wait()

    return pl.pallas_call(
        body, name="rs_sibling_exchange",
        out_shape=jax.ShapeDtypeStruct((n, hr, w), g.dtype),
        in_specs=[pl.BlockSpec(memory_space=pl.ANY)],
        out_specs=pl.BlockSpec(memory_space=pl.ANY),
        scratch_shapes=[pltpu.SemaphoreType.DMA, pltpu.SemaphoreType.DMA],
    )(g)


def _chip_exchange(sb):
    _, rows, w = sb.shape

    def body(sb_ref, out_ref, send_sems, recv_sems):
        px, py, pc = _position()
        peers = [(px, 1 - py, pc), (1 - px, py, pc), (1 - px, 1 - py, pc)]
        cps = [pltpu.make_async_remote_copy(
            src_ref=sb_ref.at[j], dst_ref=out_ref.at[j], send_sem=send_sems.at[j], recv_sem=recv_sems.at[j],
            device_id=peer, device_id_type=MESH) for j, peer in enumerate(peers)]
        for cp in cps:
            cp.start()
        for cp in cps:
            cp.wait()

    return pl.pallas_call(
        body, name="rs_chip_exchange",
        out_shape=jax.ShapeDtypeStruct(sb.shape, sb.dtype),
        in_specs=[pl.BlockSpec(memory_space=pl.ANY)],
        out_specs=pl.BlockSpec(memory_space=pl.ANY),
        scratch_shapes=[pltpu.SemaphoreType.DMA((3,)), pltpu.SemaphoreType.DMA((3,))],
    )(sb)


def _sibling_complete(buf):
    _, hr, w = buf.shape

    def body(b_ref, out_ref, send_sem, recv_sem):
        px, py, pc = _position()
        cp = pltpu.make_async_remote_copy(
            src_ref=out_ref.at[pc], dst_ref=out_ref.at[pc], send_sem=send_sem, recv_sem=recv_sem,
            device_id=(px, py, 1 - pc), device_id_type=MESH)
        cp.start()
        cp.wait()

    return pl.pallas_call(
        body, name="rs_sibling_complete",
        out_shape=jax.ShapeDtypeStruct((2, hr, w), buf.dtype),
        in_specs=[pl.BlockSpec(memory_space=pl.ANY)],
        out_specs=pl.BlockSpec(memory_space=pl.ANY),
        scratch_shapes=[pltpu.SemaphoreType.DMA, pltpu.SemaphoreType.DMA],
        input_output_aliases={0: 0},
    )(buf).reshape(2 * hr, w)


def _rs_partial(g, recv, ids, *, tr=256):
    _, rows, w = g.shape
    hr = rows // 2
    nb = hr // tr

    def body(ids_ref, g_ref, r_ref, o_ref):
        o_ref[0] = (g_ref[0] + r_ref[0]).astype(BF16)

    grid_spec = pltpu.PrefetchScalarGridSpec(
        num_scalar_prefetch=1, grid=(3, nb),
        in_specs=[pl.BlockSpec((1, tr, w), lambda j, i, ids: (ids[1] ^ (j + 1), ids[0] * nb + i, 0)),
                  pl.BlockSpec((1, tr, w), lambda j, i, ids: (ids[1] ^ (j + 1), i, 0))],
        out_specs=pl.BlockSpec((1, tr, w), lambda j, i, ids: (j, i, 0)))
    return pl.pallas_call(
        body, name="rs_partial", grid_spec=grid_spec,
        out_shape=jax.ShapeDtypeStruct((3, hr, w), BF16),
        compiler_params=_cparams(),
    )(ids, g, recv)


def _rs_total(g, recv, got, ids, *, tr=256):
    _, rows, w = g.shape
    hr = rows // 2
    nb = hr // tr

    def body(ids_ref, g_ref, r_ref, got_ref, o_ref):
        acc = g_ref[0] + r_ref[0]
        for j in range(3):
            acc = acc + got_ref[j].astype(F32)
        o_ref[0] = acc

    grid_spec = pltpu.PrefetchScalarGridSpec(
        num_scalar_prefetch=1, grid=(nb,),
        in_specs=[pl.BlockSpec((1, tr, w), lambda i, ids: (ids[1], ids[0] * nb + i, 0)),
                  pl.BlockSpec((1, tr, w), lambda i, ids: (ids[1], i, 0)),
                  pl.BlockSpec((3, tr, w), lambda i, ids: (0, i, 0))],
        out_specs=pl.BlockSpec((1, tr, w), lambda i, ids: (ids[0], i, 0)))
    return pl.pallas_call(
        body, name="rs_total", grid_spec=grid_spec,
        out_shape=jax.ShapeDtypeStruct((2, hr, w), F32),
        compiler_params=_cparams(),
    )(ids, g, recv, got)


def _reduce_scatter(g, ids):
    recv = _sibling_other_half(g)
    got = _chip_exchange(_rs_partial(g, recv, ids))
    return _sibling_complete(_rs_total(g, recv, got, ids))


def _flat_rows():
    per_layer = sum(r for _, r in FSDP_SECTIONS)
    used = DEPTH * per_layer
    return used, -(-used // ROW_ALIGN) * ROW_ALIGN


def _cols_to_chunks(full):
    rows, cols = full.shape
    t = full.reshape(rows, N_CHIPS, cols // N_CHIPS).transpose(1, 0, 2)
    return t.reshape(N_CHIPS, -1, FLAT_W)


def _chunks_to_cols(chunks, rows, cols):
    return chunks.reshape(N_CHIPS, rows, cols // N_CHIPS).transpose(1, 0, 2).reshape(rows, cols)


def _pad_heads(w, real):
    lead = w.shape[:-1]
    t = w.reshape(lead + (HEADS, real))
    t = jnp.pad(t, [(0, 0)] * len(lead) + [(0, 0), (0, HEAD_PAD - real)])
    return t.reshape(lead + (HEADS * HEAD_PAD,))


def _unpad_heads(w, real):
    lead = w.shape[:-1]
    return w.reshape(lead + (HEADS, HEAD_PAD))[..., :real].reshape(lead + (HEADS * real,))


def _pad_value_lanes(w, axis):
    w = jnp.moveaxis(w, axis, -1)
    lead = w.shape[:-1]
    t = w.reshape(lead + (HEADS, 64))
    t = jnp.pad(t, [(0, 0)] * len(lead) + [(0, 0), (HEAD_PAD - 64, 0)])
    return jnp.moveaxis(t.reshape(lead + (HEADS * HEAD_PAD,)), -1, axis)


def _unpad_value_lanes(w, axis):
    w = jnp.moveaxis(w, axis, -1)
    lead = w.shape[:-1]
    t = w.reshape(lead + (HEADS, HEAD_PAD))[..., HEAD_PAD - 64:]
    return jnp.moveaxis(t.reshape(lead + (HEADS * 64,)), -1, axis)


def _pad_w_in(w):
    z = jnp.zeros((w.shape[0], NOPE), w.dtype)
    z2 = jnp.zeros((w.shape[0], HEAD_PAD - NOPE - ROPE), w.dtype)
    return jnp.concatenate([w[:, :1408], z, w[:, 1408:], z2], axis=1)


def _unpad_w_in(w):
    return jnp.concatenate([w[:, :1408], w[:, 1408 + NOPE:1408 + NOPE + ROPE]], axis=1)


def _rope_tables(positions):
    freqs = ROPE_THETA ** (-jnp.arange(0, ROPE, 2, dtype=F32) / ROPE)
    ang = positions.astype(F32)[..., None] * freqs
    cos, sin = jnp.cos(ang), jnp.sin(ang)
    lead = cos.shape[:-1]
    ones = jnp.ones(lead + (NOPE,), F32)
    zeros_n = jnp.zeros(lead + (NOPE,), F32)
    zeros_p = jnp.zeros(lead + (HEAD_PAD - NOPE - ROPE,), F32)
    ctab = jnp.concatenate([ones, cos, cos, zeros_p], axis=-1)
    stab = jnp.concatenate([zeros_n, -sin, sin, zeros_p], axis=-1)
    return ctab, stab


def _layer_weights(p, l):
    ws = p["gmlp_ws"][l]
    tril = jnp.tril(jnp.ones((CHUNK, CHUNK), bool))
    bs = p["gmlp_bs"][l]
    bexp = jnp.repeat(bs.reshape(GROUPS // 2, 2, CHUNK).transpose(0, 2, 1), GROUP_DIM, axis=2)
    return dict(
        w_in=_pad_w_in(p["w_in"][l]),
        w_uq=_pad_heads(p["mla_w_uq"][l], NOPE + ROPE),
        w_ukv=p["mla_w_ukv"][l],
        w_out_a=_pad_value_lanes(p["w_out"][l][D_GMLP:], 0),
        w_out_g=p["w_out"][l][:D_GMLP],
        w_ff1=p["w_ff1"][l],
        w_ff2=p["w_ff2"][l],
        ws=ws,
        wst=jnp.where(tril[None], ws, 0.0).transpose(0, 2, 1).astype(BF16),
        bexp=bexp,
        g_mix=p["norm_mix_g"][l][None],
        g_ffn=p["norm_ffn_g"][l][None],
        g_q=p["mla_q_norm_g"][l][None],
        g_kv=p["mla_kv_norm_g"][l][None],
        g_og=p["out_norm_gmlp_g"][l][None],
        g_oa=_pad_value_lanes(p["out_norm_mla_g"][l], 0)[None],
    )


def _local_step(x3, target3, positions, mods, p):
    bsz, seq, d = x3.shape
    tok = bsz * seq
    tmt = min(512, seq)
    tmk = min(1024, seq)
    ctab, stab = _rope_tables(positions)
    lw = [_layer_weights(p, l) for l in range(DEPTH)]

    def flat(t):
        return t.reshape(tok, t.shape[-1])

    def cube(t):
        return t.reshape(bsz, seq, t.shape[-1])

    saved = []
    x = x3
    for l in range(DEPTH):
        w, mod = lw[l], mods[l]
        h1 = _normmod_fwd(x, w["g_mix"], mod, SHIFT1, SCALE1, name=f"l{l}_normmod1")
        z = cube(_mm(flat(h1), w["w_in"], dims="nn", name=f"l{l}_w_in", tm=tmt, tn=D_IN_PAD, tk=d))
        yg = _gmlp_fwd(z, w["ws"], w["bexp"], w["g_og"], name=f"l{l}_gmlp_fwd")
        q, kv, kp = _mla_prep_fwd(z, w["g_q"], w["g_kv"], w["w_uq"], w["w_ukv"], ctab, stab, name=f"l{l}_mla_prep")
        o, lse = _attn_fwd(q, kv, kp, name=f"l{l}_attn_fwd")
        ya = _onorm_fwd(o, w["g_oa"], name=f"l{l}_onorm_fwd")
        pg = _mm(flat(yg), w["w_out_g"], dims="nn", name=f"l{l}_w_out_g", tm=tmt, tn=d, tk=D_GMLP)

        def out_epi(acc, pgv, xv, gm):
            po = acc + pgv
            return po, xv + gm[0][GATE1:GATE1 + 1, :] * po

        po, x_mid = _mm(flat(ya), w["w_out_a"], dims="nn", name=f"l{l}_w_out_a", tm=tmt, tn=d, tk=d,
                        out_dtypes=(F32, F32), epilogue=out_epi, extras=(pg, flat(x), mod),
                        extra_specs=(None, None, _mod_spec(tmt, d, seq)))
        x_mid = cube(x_mid)
        h2 = _normmod_fwd(x_mid, w["g_ffn"], mod, SHIFT2, SCALE2, name=f"l{l}_normmod2")

        def act_epi(acc):
            r = jnp.maximum(acc, 0.0)
            return (r * r,)

        r = _mm(flat(h2), w["w_ff1"], dims="nn", name=f"l{l}_w_ff1", tm=tmt, tn=1024, tk=d,
                out_dtypes=(BF16,), epilogue=act_epi, weights_outer=True)

        def ff2_epi(acc, xv, gm):
            return acc, xv + gm[0][GATE2:GATE2 + 1, :] * acc

        f, x_out = _mm(r, w["w_ff2"], dims="nn", name=f"l{l}_w_ff2", tm=tmk, tn=d, tk=1024,
                       out_dtypes=(F32, F32), epilogue=ff2_epi, extras=(flat(x_mid), mod),
                       extra_specs=(None, _mod_spec(tmk, d, seq)))
        saved.append(dict(x_in=x, h1=h1, z=z, q=q, kv=kv, kp=kp, o=o, lse=lse, ya=ya, yg=yg, po=cube(po),
                          x_mid=x_mid, h2=h2, r=r, f=cube(f)))
        x = cube(x_out)

    grads = [dict() for _ in range(DEPTH)]
    dmods = [None] * DEPTH
    top = DEPTH - 1
    node = _resnode_bwd(x, p["final_norm_g"][None], name="final_loss_bwd", target3=target3,
                        branch3=saved[top]["f"], mod_gate=mods[top], gate_row=GATE2)
    loss_part = node["loss"][0, 0]
    d_final_g = node["dg"][0]
    for l in range(DEPTH - 1, -1, -1):
        w, mod, s = lw[l], mods[l], saved[l]
        dx_out, dfb, dgate2 = node["dx"], flat(node["dbr"]), node["dgate"][:, 0]

        def dact_epi(acc, rv):
            return (acc * (2.0 * jnp.sqrt(rv.astype(F32))),)

        da = _mm(dfb, w["w_ff2"], dims="nt", name=f"l{l}_d_r", tm=tmt, tn=1024, tk=d,
                 out_dtypes=(BF16,), epilogue=dact_epi, extras=(s["r"],), weights_outer=True)
        grads[l]["w_ff2"] = _mm(s["r"], dfb, dims="tn", name=f"l{l}_dw_ff2", tm=1024, tn=d, tk=1024)
        grads[l]["w_ff1"] = _mm(flat(s["h2"]), da, dims="tn", name=f"l{l}_dw_ff1", tm=d, tn=1024, tk=1024)
        dh2 = _mm(da, w["w_ff1"], dims="nt", name=f"l{l}_d_h2", tm=tmk, tn=d, tk=1024)
        node = _resnode_bwd(s["x_mid"], w["g_ffn"], name=f"l{l}_resnode_ffn", dh3=cube(dh2), dres3=dx_out,
                            mod_nm=mod, rows=(SHIFT2, SCALE2), branch3=s["po"], mod_gate=mod, gate_row=GATE1)
        grads[l]["norm_ffn_g"] = node["dg"][0]
        dshift2, dscale2 = node["dnm"][:, 0], node["dnm"][:, 1]
        dx_mid, dpo, dgate1 = node["dx"], flat(node["dbr"]), node["dgate"][:, 0]

        dya = _mm(dpo, w["w_out_a"], dims="nt", name=f"l{l}_d_ya", tm=tmt, tn=d, tk=d)
        dyg = _mm(dpo, w["w_out_g"], dims="nt", name=f"l{l}_d_yg", tm=tmt, tn=D_GMLP, tk=d)
        dw_out_a = _mm(flat(s["ya"]), dpo, dims="tn", name=f"l{l}_dw_out_a", tm=d, tn=d, tk=1024)
        dw_out_g = _mm(flat(s["yg"]), dpo, dims="tn", name=f"l{l}_dw_out_g", tm=D_GMLP, tn=d, tk=1024)
        grads[l]["w_out"] = jnp.concatenate([dw_out_g, _unpad_value_lanes(dw_out_a, 0)], axis=0)

        duv, dws, dbs, dg_og = _gmlp_bwd(s["z"], cube(dyg), w["ws"], w["wst"], w["bexp"], w["g_og"],
                                         name=f"l{l}_gmlp_bwd")
        grads[l]["gmlp_ws"], grads[l]["gmlp_bs"], grads[l]["out_norm_gmlp_g"] = dws, dbs, dg_og[0]

        do, dl, dg_oa = _onorm_bwd(s["o"], cube(dya), w["g_oa"], name=f"l{l}_onorm_bwd")
        grads[l]["out_norm_mla_g"] = _unpad_value_lanes(dg_oa[0], 0)
        dq = _attn_bwd_dq(s["q"], s["kv"], s["kp"], do, s["lse"], dl, name=f"l{l}_attn_dq")
        dk, dv = _attn_bwd_dkv(s["q"], s["kv"], s["kp"], do, s["lse"], dl, name=f"l{l}_attn_dkv")
        dzm, cq, dqb, ckv, dkvb, dg_q, dg_kv = _mla_prep_bwd(
            s["z"], dq, dk, dv, w["g_q"], w["g_kv"], w["w_uq"], w["w_ukv"], ctab, stab, name=f"l{l}_mla_prep_bwd")
        grads[l]["mla_q_norm_g"], grads[l]["mla_kv_norm_g"] = dg_q[0], dg_kv[0]
        dw_uq = _mm(flat(cq), flat(dqb), dims="tn", name=f"l{l}_dw_uq", tm=Q_RANK, tn=1024, tk=1024)
        grads[l]["mla_w_uq"] = _unpad_heads(dw_uq, NOPE + ROPE)
        grads[l]["mla_w_ukv"] = _mm(flat(ckv), flat(dkvb), dims="tn", name=f"l{l}_dw_ukv", tm=KV_RANK, tn=1024, tk=1024)

        h1f = flat(s["h1"])
        dw_in_uv = _mm(h1f, flat(duv), dims="tn", name=f"l{l}_dw_in_uv", tm=d, tn=1024, tk=1024)
        dw_in_m = _mm(h1f, flat(dzm), dims="tn", name=f"l{l}_dw_in_m", tm=d, tn=512, tk=1024)
        grads[l]["w_in"] = _unpad_w_in(jnp.concatenate([dw_in_uv, dw_in_m], axis=1))
        dh1_uv = _mm(flat(duv), w["w_in"][:, :1024], dims="nt", name=f"l{l}_d_h1_uv", tm=tmt, tn=d, tk=1024)
        dh1 = _mm(flat(dzm), w["w_in"][:, 1024:], dims="nt", name=f"l{l}_d_h1", tm=tmt, tn=d, tk=512,
                  epilogue=lambda acc, prev: (acc + prev,), extras=(dh1_uv,))
        if l > 0:
            node = _resnode_bwd(s["x_in"], w["g_mix"], name=f"l{l}_resnode_mix", dh3=cube(dh1), dres3=dx_mid,
                                mod_nm=mod, rows=(SHIFT1, SCALE1), branch3=saved[l - 1]["f"],
                                mod_gate=mods[l - 1], gate_row=GATE2)
        else:
            node = _resnode_bwd(s["x_in"], w["g_mix"], name=f"l{l}_resnode_mix", dh3=cube(dh1), dres3=dx_mid,
                                mod_nm=mod, rows=(SHIFT1, SCALE1))
        grads[l]["norm_mix_g"] = node["dg"][0]
        dshift1, dscale1 = node["dnm"][:, 0], node["dnm"][:, 1]
        dmods[l] = jnp.stack([dshift1, dscale1, dgate1, dshift2, dscale2, dgate2], axis=1)
    return loss_part, node["dx"], grads, d_final_g, dmods


W_NAMES = ("w_ada", "b_ada", "norm_mix_g", "w_in", "gmlp_ws", "gmlp_bs", "mla_q_norm_g", "mla_kv_norm_g",
           "mla_w_uq", "mla_w_ukv", "out_norm_gmlp_g", "out_norm_mla_g", "w_out", "norm_ffn_g", "w_ff1", "w_ff2",
           "final_norm_g")
FLAT_KEY = {"w_in": "w_in", "w_uq": "mla_w_uq", "w_ukv": "mla_w_ukv", "w_out": "w_out", "w_ff1": "w_ff1",
            "w_ff2": "w_ff2"}
COL_SHARDED = ("w_in", "w_uq", "w_ukv", "w_ff1")
FULL_SHAPE = {"w_in": (D_MODEL, D_IN), "w_uq": (Q_RANK, HEADS * (NOPE + ROPE)), "w_ukv": (KV_RANK, HEADS * 128),
              "w_out": (D_MODEL, D_MODEL), "w_ff1": (D_MODEL, D_FF), "w_ff2": (D_FF, D_MODEL)}
SMALL_NAMES = ("norm_mix_g", "gmlp_ws", "gmlp_bs", "mla_q_norm_g", "mla_kv_norm_g", "out_norm_gmlp_g",
               "out_norm_mla_g", "norm_ffn_g", "final_norm_g")


def _silu(v):
    return v * (1.0 / (1.0 + jnp.exp(-v)))


def kernel(x, c, positions, w_ada, b_ada, norm_mix_g, w_in, gmlp_ws, gmlp_bs, mla_q_norm_g, mla_kv_norm_g, mla_w_uq, mla_w_ukv, out_norm_gmlp_g, out_norm_mla_g, w_out, norm_ffn_g, w_ff1, w_ff2, final_norm_g, loss_target, m_w_ada, m_b_ada, m_norm_mix_g, m_w_in, m_gmlp_ws, m_gmlp_bs, m_mla_q_norm_g, m_mla_kv_norm_g, m_mla_w_uq, m_mla_w_ukv, m_out_norm_gmlp_g, m_out_norm_mla_g, m_w_out, m_norm_ffn_g, m_w_ff1, m_w_ff2, m_final_norm_g, v_w_ada, v_b_ada, v_norm_mix_g, v_w_in, v_gmlp_ws, v_gmlp_bs, v_mla_q_norm_g, v_mla_kv_norm_g, v_mla_w_uq, v_mla_w_ukv, v_out_norm_gmlp_g, v_out_norm_mla_g, v_w_out, v_norm_ffn_g, v_w_ff1, v_w_ff2, v_final_norm_g):
    weights = dict(w_ada=w_ada, b_ada=b_ada, norm_mix_g=norm_mix_g, w_in=w_in, gmlp_ws=gmlp_ws, gmlp_bs=gmlp_bs,
                   mla_q_norm_g=mla_q_norm_g, mla_kv_norm_g=mla_kv_norm_g, mla_w_uq=mla_w_uq, mla_w_ukv=mla_w_ukv,
                   out_norm_gmlp_g=out_norm_gmlp_g, out_norm_mla_g=out_norm_mla_g, w_out=w_out,
                   norm_ffn_g=norm_ffn_g, w_ff1=w_ff1, w_ff2=w_ff2, final_norm_g=final_norm_g)
    mom_m = dict(zip(W_NAMES, (m_w_ada, m_b_ada, m_norm_mix_g, m_w_in, m_gmlp_ws, m_gmlp_bs, m_mla_q_norm_g,
                               m_mla_kv_norm_g, m_mla_w_uq, m_mla_w_ukv, m_out_norm_gmlp_g, m_out_norm_mla_g,
                               m_w_out, m_norm_ffn_g, m_w_ff1, m_w_ff2, m_final_norm_g)))
    mom_v = dict(zip(W_NAMES, (v_w_ada, v_b_ada, v_norm_mix_g, v_w_in, v_gmlp_ws, v_gmlp_bs, v_mla_q_norm_g,
                               v_mla_kv_norm_g, v_mla_w_uq, v_mla_w_ukv, v_out_norm_gmlp_g, v_out_norm_mla_g,
                               v_w_out, v_norm_ffn_g, v_w_ff1, v_w_ff2, v_final_norm_g)))
    bsz, seq, d = x.shape
    px, py, pc = _position()
    chip = 2 * px + py
    dev = 2 * chip + pc
    ids = jnp.stack([pc, chip]).astype(jnp.int32)
    n_ex = N_DEV * bsz
    ada_cols = w_ada.shape[-1]

    c_all = _allgather8(c.reshape(bsz * d // 128, 128), name="gather_c").reshape(n_ex, d)
    mod_parts = []
    for l in range(DEPTH):
        bias = lax.dynamic_slice(b_ada[l], (chip * ada_cols,), (ada_cols,))[None]
        mod_parts.append(_mm(c_all, w_ada[l], dims="nn", name=f"l{l}_mod", tm=n_ex, tn=ada_cols, tk=d,
                             epilogue=lambda acc, bv: (acc + bv,), extras=(bias,),
                             extra_specs=(pl.BlockSpec((1, ada_cols), lambda i, j, k: (0, j)),), a_fn=_silu))
    mod_g = _allgather8(jnp.concatenate(mod_parts, axis=0), name="gather_mod")
    mod_g = mod_g.reshape(N_CHIPS, 2, DEPTH, n_ex, ada_cols)[:, 0]
    mod_full = mod_g.transpose(1, 2, 0, 3).reshape(DEPTH, n_ex, N_CHIPS * ada_cols)
    mod_mine = lax.dynamic_slice(mod_full, (0, dev * bsz, 0), (DEPTH, bsz, N_MOD * d))
    mod_mine = jnp.pad(mod_mine.reshape(DEPTH, bsz, N_MOD, d), ((0, 0), (0, 0), (0, MOD_ROWS - N_MOD), (0, 0)))
    mods = [mod_mine[l] for l in range(DEPTH)]

    used_rows, flat_rows = _flat_rows()
    pieces = [weights[FLAT_KEY[nm]][l].reshape(-1, FLAT_W) for l in range(DEPTH) for nm, _ in FSDP_SECTIONS]
    pieces.append(jnp.zeros((flat_rows - used_rows, FLAT_W), F32))
    w_flat = jnp.concatenate(pieces, axis=0).astype(BF16)
    w_gath = _allgather8(w_flat, name="gather_weights", own_half=True).reshape(N_CHIPS, flat_rows, FLAT_W)
    full = {FLAT_KEY[nm]: [] for nm, _ in FSDP_SECTIONS}
    off = 0
    for l in range(DEPTH):
        for nm, nrows in FSDP_SECTIONS:
            sec = w_gath[:, off:off + nrows]
            off += nrows
            rows, cols = FULL_SHAPE[nm]
            full[FLAT_KEY[nm]].append(_chunks_to_cols(sec, rows, cols) if nm in COL_SHARDED
                                      else sec.reshape(rows, cols))
    p = dict(weights)
    p.update(full)

    loss_part, grad_x, grads, d_final_g, dmods = _local_step(x, loss_target, positions, mods, p)
    loss = lax.psum(loss_part, ("x", "y", "c"))

    gpieces = []
    for l in range(DEPTH):
        for nm, nrows in FSDP_SECTIONS:
            g = grads[l][FLAT_KEY[nm]]
            gpieces.append(_cols_to_chunks(g) if nm in COL_SHARDED else g.reshape(N_CHIPS, nrows, FLAT_W))
    gpieces.append(jnp.zeros((N_CHIPS, flat_rows - used_rows, FLAT_W), F32))
    g_shard = _reduce_scatter(jnp.concatenate(gpieces, axis=1), ids)
    grad = {}
    off = 0
    per = {FLAT_KEY[nm]: [] for nm, _ in FSDP_SECTIONS}
    for l in range(DEPTH):
        for nm, nrows in FSDP_SECTIONS:
            per[FLAT_KEY[nm]].append(g_shard[off:off + nrows].reshape(weights[FLAT_KEY[nm]].shape[1:]))
            off += nrows
    for key, parts in per.items():
        grad[key] = jnp.stack(parts, axis=0)

    small = {nm: (d_final_g if nm == "final_norm_g" else jnp.stack([grads[l][nm] for l in range(DEPTH)], axis=0))
             for nm in SMALL_NAMES}
    svec = jnp.concatenate([small[nm].reshape(-1) for nm in SMALL_NAMES])
    n_small = svec.shape[0]
    srows = -(-n_small // (8 * FLAT_W)) * 8
    svec = jnp.pad(svec, (0, srows * FLAT_W - n_small)).reshape(srows, FLAT_W)
    ssum = _sum_leading(_allgather8(svec, name="gather_small_grads"), name="sum_small_grads").reshape(-1)
    off = 0
    for nm in SMALL_NAMES:
        size = weights[nm].size
        grad[nm] = ssum[off:off + size].reshape(weights[nm].shape)
        off += size

    dmod = jnp.stack(dmods, axis=1).reshape(bsz * DEPTH * N_MOD, d)
    dmod_all = _allgather8(dmod, name="gather_dmod").reshape(n_ex, DEPTH, N_MOD * d)
    gw, gb = [], []
    for l in range(DEPTH):
        dm = dmod_all[:, l]
        dm_cols = lax.dynamic_slice(dm, (0, chip * ada_cols), (n_ex, ada_cols))
        gw.append(_mm(c_all, dm_cols, dims="tn", name=f"l{l}_dw_ada", tm=d, tn=ada_cols, tk=n_ex, a_fn=_silu))
        gb.append(_sum_leading(dm.reshape(n_ex, N_MOD * d // FLAT_W, FLAT_W), name=f"l{l}_db_ada").reshape(-1))
    grad["w_ada"] = jnp.stack(gw, axis=0)
    grad["b_ada"] = jnp.stack(gb, axis=0)

    delta, new_m, new_v = {}, {}, {}
    for nm in W_NAMES:
        delta[nm], new_m[nm], new_v[nm] = _adamw(weights[nm], grad[nm], mom_m[nm], mom_v[nm], name=f"adamw_{nm}")
    return (loss, grad_x, *[grad[nm] for nm in W_NAMES], *[delta[nm] for nm in W_NAMES],
            *[new_m[nm] for nm in W_NAMES], *[new_v[nm] for nm in W_NAMES])
```

```python
import functools
import math

import jax
import jax.numpy as jnp
from jax import lax
from jax.experimental import pallas as pl
from jax.experimental.pallas import tpu as pltpu

F32 = jnp.float32
BF16 = jnp.bfloat16

D_MODEL = 1024
DEPTH = 2
D_GMLP = 512
GROUPS = 8
GROUP_DIM = 64
CHUNK = 128
HEADS = 8
NOPE = 64
ROPE = 32
HEAD_PAD = 128
Q_RANK = 256
KV_RANK = 128
D_FF = 4096
N_MOD = 6
MOD_ROWS = 8
EPS = 1e-6
ROPE_THETA = 10000.0
D_IN = 1440
D_IN_PAD = 1536
ATTN_SCALE = (NOPE + ROPE) ** -0.5
LOG2E = math.log2(math.e)
SCALE_LOG2 = ATTN_SCALE * LOG2E
N_CHIPS = 4
N_DEV = 8

ADAM_LR = 0.001
ADAM_B1 = 0.9
ADAM_B2 = 0.999
ADAM_EPS = 1e-08
ADAM_WD = 0.01
ADAM_STEP = 10

VMEM_LIMIT = 48 * 1024 * 1024
FLAT_W = 1024
ROW_ALIGN = 256

NN = (((1,), (0,)), ((), ()))
NT = (((1,), (1,)), ((), ()))
TN = (((0,), (0,)), ((), ()))
MESH = pl.DeviceIdType.MESH

SHIFT1, SCALE1, GATE1, SHIFT2, SCALE2, GATE2 = range(6)

FSDP_SECTIONS = (("w_in", 360), ("w_uq", 48), ("w_ukv", 32), ("w_out", 256), ("w_ff1", 1024), ("w_ff2", 1024))


def _cparams(vmem=VMEM_LIMIT):
    return pltpu.CompilerParams(vmem_limit_bytes=vmem)


def _dot(a, b, dims=NN):
    return lax.dot_general(a, b, dims, preferred_element_type=F32)


def _iota(shape, axis):
    return lax.broadcasted_iota(jnp.int32, shape, axis)


def _gelu(x):
    k = math.sqrt(2.0 / math.pi)
    return 0.5 * x * (1.0 + jnp.tanh(k * (x + 0.044715 * (x * x * x))))


def _gelu_grad(x):
    k = math.sqrt(2.0 / math.pi)
    t = jnp.tanh(k * (x + 0.044715 * (x * x * x)))
    return 0.5 * (1.0 + t) + 0.5 * x * (1.0 - t * t) * (k * (1.0 + 3.0 * 0.044715 * (x * x)))


def _rms_fwd(x, g, n):
    r = lax.rsqrt(jnp.sum(x * x, axis=-1, keepdims=True) * (1.0 / n) + EPS)
    return x * r * g


def _rms_bwd(x, g, dy, n):
    r = lax.rsqrt(jnp.sum(x * x, axis=-1, keepdims=True) * (1.0 / n) + EPS)
    xh = x * r
    dxh = dy * g
    dx = r * (dxh - xh * (jnp.sum(dxh * xh, axis=-1, keepdims=True) * (1.0 / n)))
    dg = jnp.sum(dy * xh, axis=0, keepdims=True)
    return dx, dg


def _pick_rows(rows, limit):
    if rows <= limit:
        return rows
    for t in range(limit, 7, -8):
        if rows % t == 0:
            return t
    return rows


def _mm(a, b, *, dims, name, tm=512, tn=1024, tk=1024, out_dtypes=(F32,), epilogue=None,
        extras=(), extra_specs=(), a_fn=None, weights_outer=False, side=None):
    if dims == "tn":
        kk, m = a.shape
    else:
        m, kk = a.shape
    n = b.shape[0] if dims == "nt" else b.shape[1]
    tm, tn, tk = min(tm, m), min(tn, n), min(tk, kk)
    assert m % tm == 0 and n % tn == 0 and kk % tk == 0, (name, a.shape, b.shape, tm, tn, tk)
    ni, nj, nk = m // tm, n // tn, kk // tk

    def spec(shape, pick):
        if weights_outer:
            return pl.BlockSpec(shape, lambda j, i, k: pick(i, j, k))
        return pl.BlockSpec(shape, pick)

    if dims == "tn":
        a_spec = spec((tk, tm), lambda i, j, k: (k, i))
    else:
        a_spec = spec((tm, tk), lambda i, j, k: (i, k))
    if dims == "nt":
        b_spec = spec((tn, tk), lambda i, j, k: (j, k))
    else:
        b_spec = spec((tk, tn), lambda i, j, k: (k, j))
    o_spec = spec((tm, tn), lambda i, j, k: (i, j))
    assert not (weights_outer and extra_specs)
    dn = {"nn": NN, "nt": NT, "tn": TN}[dims]
    n_ex, n_out = len(extras), len(out_dtypes)
    e_specs = [o_spec if s is None else s for s in (tuple(extra_specs) + (None,) * n_ex)[:n_ex]]

    def body(*refs):
        a_ref, b_ref = refs[0], refs[1]
        e_refs = refs[2:2 + n_ex]
        o_refs = refs[2 + n_ex:2 + n_ex + n_out]
        av = a_ref[...]
        if a_fn is not None:
            av = a_fn(av)
        part = _dot(av.astype(BF16), b_ref[...].astype(BF16), dn)

        def finish(acc):
            outs = (acc,) if epilogue is None else epilogue(acc, *[e[...] for e in e_refs])
            for o_ref, o in zip(o_refs, outs):
                o_ref[...] = o.astype(o_ref.dtype)

        if nk == 1:
            finish(part)
        else:
            acc_ref = refs[-1]
            k = pl.program_id(2)

            @pl.when(k == 0)
            def _():
                acc_ref[...] = part

            @pl.when(k > 0)
            def _():
                acc_ref[...] += part

            @pl.when(k == nk - 1)
            def _():
                finish(acc_ref[...])

    outs, side_outs = _hosted_call(
        body, name=name, grid=(nj, ni, nk) if weights_outer else (ni, nj, nk),
        in_specs=[a_spec, b_spec] + e_specs,
        out_specs=[o_spec] * n_out,
        out_shape=[jax.ShapeDtypeStruct((m, n), dt) for dt in out_dtypes],
        scratch_shapes=[pltpu.VMEM((tm, tn), F32)] if nk > 1 else [],
        args=(a, b, *extras), side=side)
    res = outs[0] if n_out == 1 else outs
    return res if side is None else (res, side_outs)


def _mod_spec(tm, tn, seq):
    return pl.BlockSpec((1, MOD_ROWS, tn), lambda i, j, k: ((i * tm) // seq, 0, j))


def _normmod_fwd(x3, g, mod, shift_row, scale_row, *, name, tb=256):
    bsz, seq, d = x3.shape
    tb = min(tb, seq)

    def body(x_ref, g_ref, mod_ref, h_ref):
        m = mod_ref[0]
        nrm = _rms_fwd(x_ref[0], g_ref[...], d)
        h = nrm * (1.0 + m[scale_row:scale_row + 1, :]) + m[shift_row:shift_row + 1, :]
        h_ref[0] = h.astype(BF16)

    return pl.pallas_call(
        body, name=name, grid=(bsz, seq // tb),
        in_specs=[pl.BlockSpec((1, tb, d), lambda b, i: (b, i, 0)),
                  pl.BlockSpec((1, d), lambda b, i: (0, 0)),
                  pl.BlockSpec((1, MOD_ROWS, d), lambda b, i: (b, 0, 0))],
        out_specs=pl.BlockSpec((1, tb, d), lambda b, i: (b, i, 0)),
        out_shape=jax.ShapeDtypeStruct((bsz, seq, d), BF16),
        compiler_params=_cparams(),
    )(x3, g, mod)


def _pair_mean_exact(x, lo):
    s_lo = jnp.sum(jnp.where(lo, x, 0.0), axis=-1, keepdims=True)
    s_hi = jnp.sum(jnp.where(lo, 0.0, x), axis=-1, keepdims=True)
    return jnp.where(lo, s_lo, s_hi) * (1.0 / GROUP_DIM)


def _gmlp_pair_fwd(gv_p, w0, w1, bias, lo):
    mu = _pair_mean_exact(gv_p, lo)
    dlt = gv_p - mu
    var = _pair_mean_exact(dlt * dlt, lo)
    rstd = lax.rsqrt(var + EPS)
    vn = dlt * rstd
    vnb = vn.astype(BF16)
    mixed = jnp.where(lo, _dot(w0, vnb), _dot(w1, vnb)) + bias
    return vn, vnb, rstd, mixed


def _tril_bf16(w):
    t = w.shape[-1]
    return jnp.where(_iota((t, t), 1) <= _iota((t, t), 0), w, 0.0).astype(BF16)


def _gmlp_fwd(z3, ws, bexp, g_out, *, name):
    bsz, seq, _ = z3.shape
    nc = seq // CHUNK

    def body(u_ref, v_ref, ws_ref, b_ref, g_ref, y_ref):
        lo = _iota((CHUNK, 128), 1) < GROUP_DIM
        gu = _gelu(u_ref[0])
        gv = _gelu(v_ref[0])
        parts = []
        for p in range(GROUPS // 2):
            sl = slice(128 * p, 128 * p + 128)
            w0 = _tril_bf16(ws_ref[2 * p])
            w1 = _tril_bf16(ws_ref[2 * p + 1])
            _, _, _, mixed = _gmlp_pair_fwd(gv[:, sl], w0, w1, b_ref[p], lo)
            parts.append(gu[:, sl] * mixed)
        yg = jnp.concatenate(parts, axis=1)
        y_ref[0] = _rms_fwd(yg, g_ref[...], D_GMLP).astype(BF16)

    return pl.pallas_call(
        body, name=name, grid=(bsz, nc),
        in_specs=[pl.BlockSpec((1, CHUNK, D_GMLP), lambda b, i: (b, i, 0)),
                  pl.BlockSpec((1, CHUNK, D_GMLP), lambda b, i: (b, i, 1)),
                  pl.BlockSpec((GROUPS, CHUNK, CHUNK), lambda b, i: (0, 0, 0)),
                  pl.BlockSpec((GROUPS // 2, CHUNK, 128), lambda b, i: (0, 0, 0)),
                  pl.BlockSpec((1, D_GMLP), lambda b, i: (0, 0))],
        out_specs=pl.BlockSpec((1, CHUNK, D_GMLP), lambda b, i: (b, i, 0)),
        out_shape=jax.ShapeDtypeStruct((bsz, seq, D_GMLP), BF16),
        compiler_params=_cparams(),
    )(z3, z3, ws, bexp, g_out)


def _gmlp_bwd(z3, dyn3, ws, wst, bexp, g_out, *, name):
    bsz, seq, _ = z3.shape
    nc = seq // CHUNK
    npair = GROUPS // 2

    def body(u_ref, v_ref, dy_ref, ws_ref, wst_ref, b_ref, g_ref, duv_ref, dws_ref, dbs_ref, dg_ref, dbacc):
        first = jnp.logical_and(pl.program_id(0) == 0, pl.program_id(1) == 0)
        last = jnp.logical_and(pl.program_id(0) == bsz - 1, pl.program_id(1) == nc - 1)

        @pl.when(first)
        def _():
            dws_ref[...] = jnp.zeros_like(dws_ref)
            dg_ref[...] = jnp.zeros_like(dg_ref)
            dbacc[...] = jnp.zeros_like(dbacc)

        lo = _iota((CHUNK, 128), 1) < GROUP_DIM
        tril = _iota((CHUNK, CHUNK), 1) <= _iota((CHUNK, CHUNK), 0)
        u = u_ref[0]
        v = v_ref[0]
        gu = _gelu(u)
        gv = _gelu(v)
        fwd = []
        for p in range(npair):
            sl = slice(128 * p, 128 * p + 128)
            w0 = _tril_bf16(ws_ref[2 * p])
            w1 = _tril_bf16(ws_ref[2 * p + 1])
            fwd.append(_gmlp_pair_fwd(gv[:, sl], w0, w1, b_ref[p], lo))
        yg = jnp.concatenate([gu[:, 128 * p:128 * p + 128] * fwd[p][3] for p in range(npair)], axis=1)
        dyg, dg = _rms_bwd(yg, g_ref[...], dy_ref[0], D_GMLP)
        dg_ref[...] += dg
        du_parts, dv_parts = [], []
        for p in range(npair):
            sl = slice(128 * p, 128 * p + 128)
            vn, vnb, rstd, mixed = fwd[p]
            dyg_p = dyg[:, sl]
            dmixed = dyg_p * gu[:, sl]
            dbacc[p] += dmixed
            dm_lo = jnp.where(lo, dmixed, 0.0).astype(BF16)
            dm_hi = jnp.where(lo, 0.0, dmixed).astype(BF16)
            dws_ref[2 * p] += jnp.where(tril, _dot(dm_lo, vnb, NT), 0.0)
            dws_ref[2 * p + 1] += jnp.where(tril, _dot(dm_hi, vnb, NT), 0.0)
            dmb = dmixed.astype(BF16)
            dvn = jnp.where(lo, _dot(wst_ref[2 * p], dmb), _dot(wst_ref[2 * p + 1], dmb))
            dgv = rstd * (dvn - _pair_mean_exact(dvn, lo) - vn * _pair_mean_exact(dvn * vn, lo))
            dv_parts.append(dgv * _gelu_grad(v[:, sl]))
            du_parts.append(dyg_p * mixed * _gelu_grad(u[:, sl]))
        duv_ref[0] = jnp.concatenate(du_parts + dv_parts, axis=1).astype(BF16)

        @pl.when(last)
        def _():
            sel = jnp.where(_iota((8, 128), 0) == 0, (_iota((8, 128), 1) < GROUP_DIM).astype(F32),
                            jnp.where(_iota((8, 128), 0) == 1, (_iota((8, 128), 1) >= GROUP_DIM).astype(F32), 0.0))
            for p in range(npair):
                dbs_ref[p] = lax.dot_general(sel, dbacc[p], NT, precision=lax.Precision.HIGHEST,
                                             preferred_element_type=F32)

    duv, dws, dbs, dg = pl.pallas_call(
        body, name=name, grid=(bsz, nc),
        in_specs=[pl.BlockSpec((1, CHUNK, D_GMLP), lambda b, i: (b, i, 0)),
                  pl.BlockSpec((1, CHUNK, D_GMLP), lambda b, i: (b, i, 1)),
                  pl.BlockSpec((1, CHUNK, D_GMLP), lambda b, i: (b, i, 0)),
                  pl.BlockSpec((GROUPS, CHUNK, CHUNK), lambda b, i: (0, 0, 0)),
                  pl.BlockSpec((GROUPS, CHUNK, CHUNK), lambda b, i: (0, 0, 0)),
                  pl.BlockSpec((npair, CHUNK, 128), lambda b, i: (0, 0, 0)),
                  pl.BlockSpec((1, D_GMLP), lambda b, i: (0, 0))],
        out_specs=[pl.BlockSpec((1, CHUNK, 2 * D_GMLP), lambda b, i: (b, i, 0)),
                   pl.BlockSpec((GROUPS, CHUNK, CHUNK), lambda b, i: (0, 0, 0)),
                   pl.BlockSpec((npair, 8, CHUNK), lambda b, i: (0, 0, 0)),
                   pl.BlockSpec((1, D_GMLP), lambda b, i: (0, 0))],
        out_shape=[jax.ShapeDtypeStruct((bsz, seq, 2 * D_GMLP), BF16),
                   jax.ShapeDtypeStruct((GROUPS, CHUNK, CHUNK), F32),
                   jax.ShapeDtypeStruct((npair, 8, CHUNK), F32),
                   jax.ShapeDtypeStruct((1, D_GMLP), F32)],
        scratch_shapes=[pltpu.VMEM((npair, CHUNK, 128), F32)],
        compiler_params=_cparams(),
    )(z3, z3, dyn3, ws, wst, bexp, g_out)
    return duv, dws, dbs[:, :2, :].reshape(GROUPS, CHUNK), dg


def _partner(x):
    width = x.shape[-1]
    lane = _iota(x.shape, x.ndim - 1) % HEAD_PAD
    up = pltpu.roll(x, width - ROPE // 2, x.ndim - 1)
    down = pltpu.roll(x, ROPE // 2, x.ndim - 1)
    first = jnp.logical_and(lane >= NOPE, lane < NOPE + ROPE // 2)
    second = jnp.logical_and(lane >= NOPE + ROPE // 2, lane < NOPE + ROPE)
    return jnp.where(first, up, jnp.where(second, down, 0.0))


def _mla_prep_fwd(z3, g_q, g_kv, w_uq, w_ukv, ctab, stab, *, name, tb=256):
    bsz, seq, _ = z3.shape
    tb = min(tb, seq)
    hw = HEADS * HEAD_PAD

    def body(ql_ref, kvl_ref, krl_ref, gq_ref, gkv_ref, wuq_ref, wukv_ref, c_ref, s_ref, q_ref, kv_ref, kp_ref):
        cq = _rms_fwd(ql_ref[0], gq_ref[...], Q_RANK).astype(BF16)
        q = _dot(cq, wuq_ref[...])
        c1, s1 = c_ref[0], s_ref[0]
        c8, s8 = jnp.tile(c1, (1, HEADS)), jnp.tile(s1, (1, HEADS))
        q_ref[0] = (q * c8 + _partner(q) * s8).astype(BF16)
        ckv = _rms_fwd(kvl_ref[0], gkv_ref[...], KV_RANK).astype(BF16)
        kv = _dot(ckv, wukv_ref[...])
        kv_ref[0] = kv.astype(BF16)
        kr = krl_ref[0]
        kr = kr * c1 + _partner(kr) * s1
        lane = _iota((tb, hw), 1) % HEAD_PAD
        kp_ref[0] = jnp.where(lane < NOPE, kv, jnp.tile(kr, (1, HEADS))).astype(BF16)

    return pl.pallas_call(
        body, name=name, grid=(bsz, seq // tb),
        in_specs=[pl.BlockSpec((1, tb, Q_RANK), lambda b, i: (b, i, 4)),
                  pl.BlockSpec((1, tb, KV_RANK), lambda b, i: (b, i, 10)),
                  pl.BlockSpec((1, tb, HEAD_PAD), lambda b, i: (b, i, 11)),
                  pl.BlockSpec((1, Q_RANK), lambda b, i: (0, 0)),
                  pl.BlockSpec((1, KV_RANK), lambda b, i: (0, 0)),
                  pl.BlockSpec((Q_RANK, hw), lambda b, i: (0, 0)),
                  pl.BlockSpec((KV_RANK, hw), lambda b, i: (0, 0)),
                  pl.BlockSpec((1, tb, HEAD_PAD), lambda b, i: (b, i, 0)),
                  pl.BlockSpec((1, tb, HEAD_PAD), lambda b, i: (b, i, 0))],
        out_specs=[pl.BlockSpec((1, tb, hw), lambda b, i: (b, i, 0))] * 3,
        out_shape=[jax.ShapeDtypeStruct((bsz, seq, hw), BF16)] * 3,
        compiler_params=_cparams(),
    )(z3, z3, z3, g_q, g_kv, w_uq, w_ukv, ctab, stab)


def _mla_prep_bwd(z3, dq3, dk3, dv3, g_q, g_kv, w_uq, w_ukv, ctab, stab, *, name, tb=256):
    bsz, seq, _ = z3.shape
    tb = min(tb, seq)
    hw = HEADS * HEAD_PAD
    nb = seq // tb

    def body(ql_ref, kvl_ref, dq_ref, dk_ref, dv_ref, gq_ref, gkv_ref, wuq_ref, wukv_ref, c_ref, s_ref,
             dz_ref, cq_ref, dqb_ref, ckv_ref, dkvb_ref, dgq_ref, dgkv_ref):
        @pl.when(jnp.logical_and(pl.program_id(0) == 0, pl.program_id(1) == 0))
        def _():
            dgq_ref[...] = jnp.zeros_like(dgq_ref)
            dgkv_ref[...] = jnp.zeros_like(dgkv_ref)

        c1, s1 = c_ref[0], s_ref[0]
        c8, s8 = jnp.tile(c1, (1, HEADS)), jnp.tile(s1, (1, HEADS))
        dqr = dq_ref[0]
        dqb = (dqr * c8 + _partner(dqr * s8)).astype(BF16)
        dqb_ref[0] = dqb
        ql = ql_ref[0]
        cq_ref[0] = _rms_fwd(ql, gq_ref[...], Q_RANK).astype(BF16)
        dql, dgq = _rms_bwd(ql, gq_ref[...], _dot(dqb, wuq_ref[...], NT), Q_RANK)
        dgq_ref[...] += dgq

        dk = dk_ref[0]
        lane = _iota((tb, hw), 1) % HEAD_PAD
        dkvb = jnp.where(lane < NOPE, dk, dv_ref[0]).astype(BF16)
        dkvb_ref[0] = dkvb
        kvl = kvl_ref[0]
        ckv_ref[0] = _rms_fwd(kvl, gkv_ref[...], KV_RANK).astype(BF16)
        dkvl, dgkv = _rms_bwd(kvl, gkv_ref[...], _dot(dkvb, wukv_ref[...], NT), KV_RANK)
        dgkv_ref[...] += dgkv

        dkr = dk[:, 0:HEAD_PAD]
        for h in range(1, HEADS):
            dkr = dkr + dk[:, HEAD_PAD * h:HEAD_PAD * (h + 1)]
        lane1 = _iota((tb, HEAD_PAD), 1)
        dkr = jnp.where(jnp.logical_and(lane1 >= NOPE, lane1 < NOPE + ROPE), dkr, 0.0)
        dkrl = dkr * c1 + _partner(dkr * s1)
        dz_ref[0] = jnp.concatenate([dql, dkvl, dkrl], axis=1).astype(BF16)

    return pl.pallas_call(
        body, name=name, grid=(bsz, nb),
        in_specs=[pl.BlockSpec((1, tb, Q_RANK), lambda b, i: (b, i, 4)),
                  pl.BlockSpec((1, tb, KV_RANK), lambda b, i: (b, i, 10)),
                  pl.BlockSpec((1, tb, hw), lambda b, i: (b, i, 0)),
                  pl.BlockSpec((1, tb, hw), lambda b, i: (b, i, 0)),
                  pl.BlockSpec((1, tb, hw), lambda b, i: (b, i, 0)),
                  pl.BlockSpec((1, Q_RANK), lambda b, i: (0, 0)),
                  pl.BlockSpec((1, KV_RANK), lambda b, i: (0, 0)),
                  pl.BlockSpec((Q_RANK, hw), lambda b, i: (0, 0)),
                  pl.BlockSpec((KV_RANK, hw), lambda b, i: (0, 0)),
                  pl.BlockSpec((1, tb, HEAD_PAD), lambda b, i: (b, i, 0)),
                  pl.BlockSpec((1, tb, HEAD_PAD), lambda b, i: (b, i, 0))],
        out_specs=[pl.BlockSpec((1, tb, 512), lambda b, i: (b, i, 0)),
                   pl.BlockSpec((1, tb, Q_RANK), lambda b, i: (b, i, 0)),
                   pl.BlockSpec((1, tb, hw), lambda b, i: (b, i, 0)),
                   pl.BlockSpec((1, tb, KV_RANK), lambda b, i: (b, i, 0)),
                   pl.BlockSpec((1, tb, hw), lambda b, i: (b, i, 0)),
                   pl.BlockSpec((1, Q_RANK), lambda b, i: (0, 0)),
                   pl.BlockSpec((1, KV_RANK), lambda b, i: (0, 0))],
        out_shape=[jax.ShapeDtypeStruct((bsz, seq, 512), BF16),
                   jax.ShapeDtypeStruct((bsz, seq, Q_RANK), BF16),
                   jax.ShapeDtypeStruct((bsz, seq, hw), BF16),
                   jax.ShapeDtypeStruct((bsz, seq, KV_RANK), BF16),
                   jax.ShapeDtypeStruct((bsz, seq, hw), BF16),
                   jax.ShapeDtypeStruct((1, Q_RANK), F32),
                   jax.ShapeDtypeStruct((1, KV_RANK), F32)],
        compiler_params=_cparams(),
    )(z3, z3, dq3, dk3, dv3, g_q, g_kv, w_uq, w_ukv, ctab, stab)


ATTN_HEADS_PER_STEP = 2


def _attn_specs(tq, seq, hp):
    blk = pl.BlockSpec((1, tq, hp * HEAD_PAD), lambda b, h, i: (b, i, h))
    full = pl.BlockSpec((1, seq, hp * HEAD_PAD), lambda b, h, i: (b, 0, h))
    return blk, full


def _head(h):
    return slice(HEAD_PAD * h, HEAD_PAD * (h + 1))


def _attn_fwd(q3, kv3, kp3, *, name, tq=512, hp=ATTN_HEADS_PER_STEP, side=None):
    bsz, seq, hw = q3.shape
    tq = min(tq, seq)
    blk, full = _attn_specs(tq, seq, hp)

    def body(q_ref, kv_ref, kp_ref, o_ref, lse_ref):
        i = pl.program_id(2)
        is_nope = _iota((tq, HEAD_PAD), 1) < NOPE
        causal = _iota((tq, tq), 1) <= _iota((tq, tq), 0)

        def step(j, carry, diag):
            st = pl.multiple_of(j * tq, tq)
            out = []
            for h in range(hp):
                m, l, acc = carry[h]
                kvj = kv_ref[0, pl.ds(st, tq), _head(h)]
                s = _dot(q_ref[0, :, _head(h)], kp_ref[0, pl.ds(st, tq), _head(h)], NT) * SCALE_LOG2
                if diag:
                    s = jnp.where(causal, s, -1e30)
                m_new = jnp.maximum(m, jnp.max(s, axis=1, keepdims=True))
                alpha = jnp.exp2(m - m_new)
                p = jnp.exp2(s - m_new)
                l = alpha * l + jnp.sum(p, axis=1, keepdims=True)
                acc = alpha * acc + _dot(p.astype(BF16), kvj)
                out.append((m_new, l, acc))
            return tuple(out)

        init = tuple((jnp.full((tq, 1), -1e30, F32), jnp.zeros((tq, 1), F32), jnp.zeros((tq, HEAD_PAD), F32))
                     for _ in range(hp))
        carry = lax.fori_loop(0, i, lambda j, c: step(j, c, False), init)
        carry = step(i, carry, True)
        for h in range(hp):
            m, l, acc = carry[h]
            o_ref[0, :, _head(h)] = jnp.where(is_nope, 0.0, acc / l)
            lse_ref[0, :, _head(h)] = jnp.broadcast_to(m + jnp.log(l) * LOG2E, (tq, HEAD_PAD))

    outs, side_outs = _hosted_call(
        body, name=name, grid=(bsz, HEADS // hp, seq // tq),
        in_specs=[blk, full, full],
        out_specs=[blk, blk],
        out_shape=[jax.ShapeDtypeStruct((bsz, seq, hw), F32), jax.ShapeDtypeStruct((bsz, seq, hw), F32)],
        args=(q3, kv3, kp3), side=side)
    return outs if side is None else (outs, side_outs)


def _attn_bwd_dq(q3, kv3, kp3, do3, lse3, dl3, *, name, tq=512, hp=ATTN_HEADS_PER_STEP):
    bsz, seq, hw = q3.shape
    tq = min(tq, seq)
    blk, full = _attn_specs(tq, seq, hp)
    rep = tq // HEAD_PAD

    def body(q_ref, kv_ref, kp_ref, do_ref, lse_ref, dl_ref, dq_ref):
        i = pl.program_id(2)
        causal = _iota((tq, tq), 1) <= _iota((tq, tq), 0)

        def step(j, carry, diag):
            st = pl.multiple_of(j * tq, tq)
            out = []
            for h in range(hp):
                kp = kp_ref[0, pl.ds(st, tq), _head(h)]
                s = _dot(q_ref[0, :, _head(h)], kp, NT) * SCALE_LOG2
                if diag:
                    s = jnp.where(causal, s, -1e30)
                p = jnp.exp2(s - jnp.tile(lse_ref[0, :, _head(h)], (1, rep)))
                dp = _dot(do_ref[0, :, _head(h)], kv_ref[0, pl.ds(st, tq), _head(h)], NT)
                ds = p * (dp - jnp.tile(dl_ref[0, :, _head(h)], (1, rep)))
                out.append(carry[h] + _dot(ds.astype(BF16), kp))
            return tuple(out)

        init = tuple(jnp.zeros((tq, HEAD_PAD), F32) for _ in range(hp))
        carry = lax.fori_loop(0, i, lambda j, c: step(j, c, False), init)
        carry = step(i, carry, True)
        for h in range(hp):
            dq_ref[0, :, _head(h)] = carry[h] * ATTN_SCALE

    return pl.pallas_call(
        body, name=name, grid=(bsz, HEADS // hp, seq // tq),
        in_specs=[blk, full, full, blk, blk, blk],
        out_specs=blk,
        out_shape=jax.ShapeDtypeStruct((bsz, seq, hw), F32),
        compiler_params=_cparams(),
    )(q3, kv3, kp3, do3, lse3, dl3)


def _attn_bwd_dkv(q3, kv3, kp3, do3, lse3, dl3, *, name, tq=512, hp=ATTN_HEADS_PER_STEP, side=None):
    bsz, seq, hw = q3.shape
    tq = min(tq, seq)
    nq = seq // tq
    blk, full = _attn_specs(tq, seq, hp)
    rep = tq // HEAD_PAD

    def body(kv_ref, kp_ref, q_ref, do_ref, lse_ref, dl_ref, dk_ref, dv_ref):
        j = pl.program_id(2)
        causal = _iota((tq, tq), 1) <= _iota((tq, tq), 0)

        def step(i, carry, diag):
            st = pl.multiple_of(i * tq, tq)
            out = []
            for h in range(hp):
                dk, dv = carry[h]
                qi = q_ref[0, pl.ds(st, tq), _head(h)]
                do = do_ref[0, pl.ds(st, tq), _head(h)]
                s = _dot(qi, kp_ref[0, :, _head(h)], NT) * SCALE_LOG2
                if diag:
                    s = jnp.where(causal, s, -1e30)
                p = jnp.exp2(s - jnp.tile(lse_ref[0, pl.ds(st, tq), _head(h)], (1, rep)))
                dv = dv + _dot(p.astype(BF16), do, TN)
                dp = _dot(do, kv_ref[0, :, _head(h)], NT)
                ds = p * (dp - jnp.tile(dl_ref[0, pl.ds(st, tq), _head(h)], (1, rep)))
                dk = dk + _dot(ds.astype(BF16), qi, TN)
                out.append((dk, dv))
            return tuple(out)

        zero = jnp.zeros((tq, HEAD_PAD), F32)
        carry = step(j, tuple((zero, zero) for _ in range(hp)), True)
        carry = lax.fori_loop(j + 1, nq, lambda i, c: step(i, c, False), carry)
        for h in range(hp):
            dk_ref[0, :, _head(h)] = carry[h][0] * ATTN_SCALE
            dv_ref[0, :, _head(h)] = carry[h][1]

    outs, side_outs = _hosted_call(
        body, name=name, grid=(bsz, HEADS // hp, nq),
        in_specs=[blk, blk, full, full, full, full],
        out_specs=[blk, blk],
        out_shape=[jax.ShapeDtypeStruct((bsz, seq, hw), F32), jax.ShapeDtypeStruct((bsz, seq, hw), F32)],
        args=(kv3, kp3, q3, do3, lse3, dl3), side=side)
    return outs if side is None else (outs, side_outs)


def _onorm_fwd(o3, g_pad, *, name, tb=256):
    bsz, seq, hw = o3.shape
    tb = min(tb, seq)

    def body(o_ref, g_ref, y_ref):
        y_ref[0] = _rms_fwd(o_ref[0], g_ref[...], HEADS * 64).astype(BF16)

    return pl.pallas_call(
        body, name=name, grid=(bsz, seq // tb),
        in_specs=[pl.BlockSpec((1, tb, hw), lambda b, i: (b, i, 0)), pl.BlockSpec((1, hw), lambda b, i: (0, 0))],
        out_specs=pl.BlockSpec((1, tb, hw), lambda b, i: (b, i, 0)),
        out_shape=jax.ShapeDtypeStruct((bsz, seq, hw), BF16),
        compiler_params=_cparams(),
    )(o3, g_pad)


def _onorm_bwd(o3, dy3, g_pad, *, name, tb=256):
    bsz, seq, hw = o3.shape
    tb = min(tb, seq)

    def body(o_ref, dy_ref, g_ref, do_ref, dl_ref, dg_ref):
        @pl.when(jnp.logical_and(pl.program_id(0) == 0, pl.program_id(1) == 0))
        def _():
            dg_ref[...] = jnp.zeros_like(dg_ref)

        o = o_ref[0]
        do, dg = _rms_bwd(o, g_ref[...], dy_ref[0], HEADS * 64)
        dg_ref[...] += dg
        do_ref[0] = do.astype(BF16)
        prod = do * o
        parts = []
        for h in range(HEADS):
            sh = jnp.sum(prod[:, HEAD_PAD * h:HEAD_PAD * (h + 1)], axis=1, keepdims=True)
            parts.append(jnp.broadcast_to(sh, (tb, HEAD_PAD)))
        dl_ref[0] = jnp.concatenate(parts, axis=1)

    return pl.pallas_call(
        body, name=name, grid=(bsz, seq // tb),
        in_specs=[pl.BlockSpec((1, tb, hw), lambda b, i: (b, i, 0)),
                  pl.BlockSpec((1, tb, hw), lambda b, i: (b, i, 0)),
                  pl.BlockSpec((1, hw), lambda b, i: (0, 0))],
        out_specs=[pl.BlockSpec((1, tb, hw), lambda b, i: (b, i, 0)),
                   pl.BlockSpec((1, tb, hw), lambda b, i: (b, i, 0)),
                   pl.BlockSpec((1, hw), lambda b, i: (0, 0))],
        out_shape=[jax.ShapeDtypeStruct((bsz, seq, hw), BF16),
                   jax.ShapeDtypeStruct((bsz, seq, hw), F32),
                   jax.ShapeDtypeStruct((1, hw), F32)],
        compiler_params=_cparams(),
    )(o3, dy3, g_pad)


def _resnode_bwd(x3, g, *, name, target3=None, dh3=None, dres3=None, mod_nm=None, rows=None,
                 branch3=None, mod_gate=None, gate_row=None, tb=256):
    bsz, seq, d = x3.shape
    tb = min(tb, seq)
    final = target3 is not None
    has_branch = branch3 is not None
    row_spec = pl.BlockSpec((1, tb, d), lambda b, i: (b, i, 0))
    vec_spec = pl.BlockSpec((1, d), lambda b, i: (0, 0))
    mod_spec = pl.BlockSpec((1, MOD_ROWS, d), lambda b, i: (b, 0, 0))

    ins, in_specs = [x3, g], [row_spec, vec_spec]
    if final:
        ins += [target3]
        in_specs += [row_spec]
    else:
        ins += [dh3, dres3, mod_nm]
        in_specs += [row_spec, row_spec, mod_spec]
    if has_branch:
        ins += [branch3, mod_gate]
        in_specs += [row_spec, mod_spec]

    out_names = ["dx", "dg"]
    out_specs = [row_spec, vec_spec]
    out_shape = [jax.ShapeDtypeStruct((bsz, seq, d), F32), jax.ShapeDtypeStruct((1, d), F32)]
    if final:
        out_names += ["loss"]
        out_specs += [pl.BlockSpec((1, 128), lambda b, i: (0, 0))]
        out_shape += [jax.ShapeDtypeStruct((1, 128), F32)]
    else:
        out_names += ["dnm"]
        out_specs += [mod_spec]
        out_shape += [jax.ShapeDtypeStruct((bsz, MOD_ROWS, d), F32)]
    if has_branch:
        out_names += ["dbr", "dgate"]
        out_specs += [row_spec, mod_spec]
        out_shape += [jax.ShapeDtypeStruct((bsz, seq, d), BF16), jax.ShapeDtypeStruct((bsz, MOD_ROWS, d), F32)]
    n_in = len(ins)

    def body(*refs):
        r = dict(zip(["x", "g"] + (["t"] if final else ["dh", "dres", "nm"]) + (["br", "gm"] if has_branch else []),
                     refs[:n_in]))
        o = dict(zip(out_names, refs[n_in:]))
        b_first = pl.program_id(1) == 0
        first = jnp.logical_and(pl.program_id(0) == 0, b_first)
        rowid = _iota((MOD_ROWS, d), 0)

        @pl.when(first)
        def _():
            o["dg"][...] = jnp.zeros_like(o["dg"])
            if final:
                o["loss"][...] = jnp.zeros_like(o["loss"])

        @pl.when(b_first)
        def _():
            if not final:
                o["dnm"][...] = jnp.zeros_like(o["dnm"])
            if has_branch:
                o["dgate"][...] = jnp.zeros_like(o["dgate"])

        x = r["x"][0]
        gv = r["g"][...]
        if final:
            e = _rms_fwd(x, gv, d) - r["t"][0]
            sq = jnp.sum(jnp.sum(e * e, axis=1, keepdims=True), axis=0, keepdims=True)
            o["loss"][...] += jnp.broadcast_to(sq * (0.5 / d), (1, 128))
            dx, dg = _rms_bwd(x, gv, e * (1.0 / d), d)
        else:
            m = r["nm"][0]
            dh = r["dh"][0]
            scale = m[rows[1]:rows[1] + 1, :]
            rstd = lax.rsqrt(jnp.sum(x * x, axis=-1, keepdims=True) * (1.0 / d) + EPS)
            xh = x * rstd
            nrm = xh * gv
            dshift = jnp.sum(dh, axis=0, keepdims=True)
            dscale = jnp.sum(dh * nrm, axis=0, keepdims=True)
            o["dnm"][0] += jnp.where(rowid == 0, dshift, jnp.where(rowid == 1, dscale, 0.0))
            dn = dh * (1.0 + scale)
            dg = jnp.sum(dn * xh, axis=0, keepdims=True)
            dxh = dn * gv
            dx = rstd * (dxh - xh * (jnp.sum(dxh * xh, axis=-1, keepdims=True) * (1.0 / d))) + r["dres"][0]
        o["dg"][...] += dg
        o["dx"][0] = dx
        if has_branch:
            gate = r["gm"][0][gate_row:gate_row + 1, :]
            o["dbr"][0] = (gate * dx).astype(BF16)
            dgate = jnp.sum(dx * r["br"][0], axis=0, keepdims=True)
            o["dgate"][0] += jnp.where(rowid == 0, dgate, 0.0)

    outs = pl.pallas_call(
        body, name=name, grid=(bsz, seq // tb),
        in_specs=in_specs, out_specs=out_specs, out_shape=out_shape,
        compiler_params=_cparams(),
    )(*ins)
    return dict(zip(out_names, outs))


def _adamw(w, g, m, v, *, name):
    shape = w.shape
    cols = shape[-1]
    rows = w.size // cols
    tr = _pick_rows(rows, max(8, (256 * 1024) // cols // 8 * 8))
    c1 = 1.0 - ADAM_B1 ** ADAM_STEP
    c2 = 1.0 - ADAM_B2 ** ADAM_STEP

    def body(w_ref, g_ref, m_ref, v_ref, d_ref, nm_ref, nv_ref):
        gg = g_ref[...]
        nm = ADAM_B1 * m_ref[...] + (1.0 - ADAM_B1) * gg
        nv = ADAM_B2 * v_ref[...] + (1.0 - ADAM_B2) * (gg * gg)
        m_hat = nm / c1
        v_hat = nv / c2
        d_ref[...] = -ADAM_LR * (m_hat / (jnp.sqrt(v_hat) + ADAM_EPS) + ADAM_WD * w_ref[...])
        nm_ref[...] = nm
        nv_ref[...] = nv

    spec = pl.BlockSpec((tr, cols), lambda i: (i, 0))
    outs = pl.pallas_call(
        body, name=name, grid=(rows // tr,),
        in_specs=[spec] * 4, out_specs=[spec] * 3,
        out_shape=[jax.ShapeDtypeStruct((rows, cols), F32)] * 3,
        compiler_params=_cparams(),
    )(*[t.reshape(rows, cols) for t in (w, g, m, v)])
    return tuple(o.reshape(shape) for o in outs)


def _sum_leading(x, *, name, tr=256):
    n, rows, cols = x.shape
    tr = _pick_rows(rows, tr)

    def body(x_ref, o_ref):
        acc = x_ref[0]
        for k in range(1, n):
            acc = acc + x_ref[k]
        o_ref[...] = acc

    return pl.pallas_call(
        body, name=name, grid=(rows // tr,),
        in_specs=[pl.BlockSpec((n, tr, cols), lambda i: (0, i, 0))],
        out_specs=pl.BlockSpec((tr, cols), lambda i: (i, 0)),
        out_shape=jax.ShapeDtypeStruct((rows, cols), F32),
        compiler_params=_cparams(),
    )(x)


def _position():
    return lax.axis_index("x"), lax.axis_index("y"), lax.axis_index("c")


def _allgather8(x, *, name):
    shape = x.shape

    def body(x_ref, out_ref, send_sems, recv_sems, local_sem):
        px, py, pc = _position()
        me, sibling = (px, py, pc), (px, py, 1 - pc)
        chips = [(1 - px, py), (px, 1 - py), (1 - px, 1 - py)]
        src_own = x_ref

        def slot(qx, qy, qc):
            return out_ref.at[4 * qx + 2 * qy + qc]

        def copy(k, block, to, src=None):
            return pltpu.make_async_remote_copy(
                src_ref=slot(*block) if src is None else src, dst_ref=slot(*block),
                send_sem=send_sems.at[k], recv_sem=recv_sems.at[k], device_id=to, device_id_type=MESH)

        mine = pltpu.make_async_copy(src_own, slot(*me), local_sem)
        mine.start()
        first = [copy(0, me, sibling, src=src_own)]
        first += [copy(1 + j, me, (*chip, pc), src=src_own) for j, chip in enumerate(chips)]
        for cp in first:
            cp.start()
        passed = [copy(4 + j, (*chip, pc), sibling) for j, chip in enumerate(chips)]
        for j, chip in enumerate(chips):
            copy(1 + j, (*chip, pc), me).wait_recv()
            passed[j].start()
        copy(0, sibling, me).wait_recv()
        for j, chip in enumerate(chips):
            copy(4 + j, (*chip, 1 - pc), me).wait_recv()
        for cp in first + passed:
            cp.wait_send()
        mine.wait()

    return pl.pallas_call(
        body, name=name,
        out_shape=jax.ShapeDtypeStruct((N_DEV,) + shape, x.dtype),
        in_specs=[pl.BlockSpec(memory_space=pl.ANY)],
        out_specs=pl.BlockSpec(memory_space=pl.ANY),
        scratch_shapes=[pltpu.SemaphoreType.DMA((7,)), pltpu.SemaphoreType.DMA((7,)), pltpu.SemaphoreType.DMA],
    )(x)


class _Exchange:
    def __init__(self, ins, out_shapes, n, build, aliases=None):
        self.ins, self.out_shapes, self.n, self.build = tuple(ins), tuple(out_shapes), n, build
        self.aliases = dict(aliases or {})

    def _descriptors(self, in_refs, out_refs, send_sems, recv_sems):
        sends, recvs = [], []
        for k, (src, dst, peer, landing) in enumerate(self.build(in_refs, out_refs)):
            sends.append(pltpu.make_async_remote_copy(
                src_ref=src, dst_ref=dst, send_sem=send_sems.at[k], recv_sem=recv_sems.at[k],
                device_id=peer, device_id_type=MESH))
            recvs.append(pltpu.make_async_remote_copy(
                src_ref=src, dst_ref=landing, send_sem=send_sems.at[k], recv_sem=recv_sems.at[k],
                device_id=peer, device_id_type=MESH))
        return sends, recvs

    def start(self, *refs):
        for cp in self._descriptors(*refs)[0]:
            cp.start()

    def finish(self, *refs):
        sends, recvs = self._descriptors(*refs)
        for cp in recvs:
            cp.wait_recv()
        for cp in sends:
            cp.wait_send()


ANY_SPEC = pl.BlockSpec(memory_space=pl.ANY)


def _hosted_call(body, *, name, grid, in_specs, out_specs, out_shape, args, scratch_shapes=(), side=None,
                 num_scalar_prefetch=0):
    in_specs, out_specs, out_shape = list(in_specs), list(out_specs), list(out_shape)
    n_in, n_out = len(in_specs) + num_scalar_prefetch, len(out_specs)
    kernel_body = body
    aliases = {}
    if side is not None:
        s_in, s_out = len(side.ins), len(side.out_shapes)
        aliases = {n_in + i: n_out + o for i, o in side.aliases.items()}

        def kernel_body(*refs):
            ins, s_ins = refs[:n_in], refs[n_in:n_in + s_in]
            outs = refs[n_in + s_in:n_in + s_in + n_out]
            s_outs = refs[n_in + s_in + n_out:n_in + s_in + n_out + s_out]
            scratch, sems = refs[n_in + s_in + n_out + s_out:-2], refs[-2:]
            first = functools.reduce(jnp.logical_and, [pl.program_id(a) == 0 for a in range(len(grid))])
            last = functools.reduce(jnp.logical_and, [pl.program_id(a) == g - 1 for a, g in enumerate(grid)])

            @pl.when(first)
            def _():
                side.start(s_ins, s_outs, *sems)

            body(*ins, *outs, *scratch)

            @pl.when(last)
            def _():
                side.finish(s_ins, s_outs, *sems)

        in_specs += [ANY_SPEC] * s_in
        out_specs += [ANY_SPEC] * s_out
        out_shape += list(side.out_shapes)
        scratch_shapes = list(scratch_shapes) + [pltpu.SemaphoreType.DMA((side.n,)),
                                                 pltpu.SemaphoreType.DMA((side.n,))]
        args = tuple(args) + side.ins
    if num_scalar_prefetch:
        grid_spec = pltpu.PrefetchScalarGridSpec(num_scalar_prefetch=num_scalar_prefetch, grid=grid,
                                                 in_specs=in_specs, out_specs=out_specs,
                                                 scratch_shapes=list(scratch_shapes))
        outs = pl.pallas_call(kernel_body, name=name, grid_spec=grid_spec, out_shape=out_shape,
                              input_output_aliases=aliases, compiler_params=_cparams())(*args)
    else:
        outs = pl.pallas_call(kernel_body, name=name, grid=grid, in_specs=in_specs, out_specs=out_specs,
                              out_shape=out_shape, scratch_shapes=list(scratch_shapes),
                              input_output_aliases=aliases, compiler_params=_cparams())(*args)
    return tuple(outs[:n_out]), tuple(outs[n_out:])


def _run_exchange(ex, *, name):
    s_in = len(ex.ins)

    def body(*refs):
        ins, outs, sems = refs[:s_in], refs[s_in:-2], refs[-2:]
        ex.start(ins, outs, *sems)
        ex.finish(ins, outs, *sems)

    outs = pl.pallas_call(
        body, name=name, out_shape=list(ex.out_shapes),
        in_specs=[ANY_SPEC] * s_in, out_specs=[ANY_SPEC] * len(ex.out_shapes),
        scratch_shapes=[pltpu.SemaphoreType.DMA((ex.n,)), pltpu.SemaphoreType.DMA((ex.n,))],
        input_output_aliases=ex.aliases,
    )(*ex.ins)
    return tuple(outs)


def _other_chips(px, py):
    return [(px, 1 - py), (1 - px, py), (1 - px, 1 - py)]


def _gather_spread(w_flat):
    rows, w = w_flat.shape
    hr = rows // 2

    def build(ins, outs):
        px, py, pc = _position()
        mine = ins[0].at[pl.ds(pc * hr, hr)]
        me = 4 * px + 2 * py + pc
        plan = [((px, py, 1 - pc), me ^ 1)]
        plan += [((qx, qy, pc), 4 * qx + 2 * qy + pc) for qx, qy in _other_chips(px, py)]
        return [(mine, outs[0].at[me], peer, outs[0].at[their]) for peer, their in plan]

    return _Exchange([w_flat], [jax.ShapeDtypeStruct((N_DEV, hr, w), w_flat.dtype)], 4, build)


def _gather_pass_on(gath):
    def build(ins, outs):
        px, py, pc = _position()
        out = []
        for qx, qy in _other_chips(px, py):
            blk = 4 * qx + 2 * qy + pc
            out.append((outs[0].at[blk], outs[0].at[blk], (px, py, 1 - pc), outs[0].at[blk ^ 1]))
        return out

    return _Exchange([gath], [jax.ShapeDtypeStruct(gath.shape, gath.dtype)], 3, build, aliases={0: 0})


def _rs_halves(g):
    n, rows, w = g.shape
    hr = rows // 2

    def build(ins, outs):
        px, py, pc = _position()
        return [(ins[0].at[:, pl.ds((1 - pc) * hr, hr), :], outs[0], (px, py, 1 - pc), outs[0])]

    return _Exchange([g], [jax.ShapeDtypeStruct((n, hr, w), g.dtype)], 1, build)


def _rs_chips(sb):
    def build(ins, outs):
        px, py, pc = _position()
        return [(ins[0].at[j], outs[0].at[j], (qx, qy, pc), outs[0].at[j])
                for j, (qx, qy) in enumerate(_other_chips(px, py))]

    return _Exchange([sb], [jax.ShapeDtypeStruct(sb.shape, sb.dtype)], 3, build)


def _rs_complete(buf):
    def build(ins, outs):
        px, py, pc = _position()
        return [(outs[0].at[pc], outs[0].at[pc], (px, py, 1 - pc), outs[0].at[1 - pc])]

    return _Exchange([buf], [jax.ShapeDtypeStruct(buf.shape, buf.dtype)], 1, build, aliases={0: 0})


def _rs_partial(g, recv, ids, *, name, tr=128):
    _, rows, w = g.shape
    hr = rows // 2
    nb = hr // tr

    def body(ids_ref, g_ref, r_ref, o_ref):
        o_ref[0] = (g_ref[0] + r_ref[0]).astype(BF16)

    grid_spec = pltpu.PrefetchScalarGridSpec(
        num_scalar_prefetch=1, grid=(3, nb),
        in_specs=[pl.BlockSpec((1, tr, w), lambda j, i, ids: (ids[1] ^ (j + 1), ids[0] * nb + i, 0)),
                  pl.BlockSpec((1, tr, w), lambda j, i, ids: (ids[1] ^ (j + 1), i, 0))],
        out_specs=pl.BlockSpec((1, tr, w), lambda j, i, ids: (j, i, 0)))
    return pl.pallas_call(
        body, name=name, grid_spec=grid_spec,
        out_shape=jax.ShapeDtypeStruct((3, hr, w), BF16),
        compiler_params=_cparams(),
    )(ids, g, recv)


def _rs_total(g, recv, got, ids, *, name, tr=128):
    _, rows, w = g.shape
    hr = rows // 2
    nb = hr // tr

    def body(ids_ref, g_ref, r_ref, got_ref, o_ref):
        acc = g_ref[0] + r_ref[0]
        for j in range(3):
            acc = acc + got_ref[j].astype(F32)
        o_ref[0] = acc

    grid_spec = pltpu.PrefetchScalarGridSpec(
        num_scalar_prefetch=1, grid=(nb,),
        in_specs=[pl.BlockSpec((1, tr, w), lambda i, ids: (ids[1], ids[0] * nb + i, 0)),
                  pl.BlockSpec((1, tr, w), lambda i, ids: (ids[1], i, 0)),
                  pl.BlockSpec((3, tr, w), lambda i, ids: (0, i, 0))],
        out_specs=pl.BlockSpec((1, tr, w), lambda i, ids: (ids[0], i, 0)))
    return pl.pallas_call(
        body, name=name, grid_spec=grid_spec,
        out_shape=jax.ShapeDtypeStruct((2, hr, w), F32),
        compiler_params=_cparams(),
    )(ids, g, recv, got)


class _ReduceScatter:
    def __init__(self, g, ids, tag):
        self.g, self.ids, self.tag, self.stage, self.result = g, ids, tag, 0, None

    def next_exchange(self):
        if self.stage == 0:
            return _rs_halves(self.g)
        if self.stage == 1:
            return _rs_chips(self.sb)
        return _rs_complete(self.buf)

    def done(self, outs):
        if self.stage == 0:
            self.recv = outs[0]
            self.sb = _rs_partial(self.g, self.recv, self.ids, name=f"{self.tag}_partial")
        elif self.stage == 1:
            self.buf = _rs_total(self.g, self.recv, outs[0], self.ids, name=f"{self.tag}_total")
        else:
            _, hr, w = outs[0].shape
            self.result = outs[0].reshape(2 * hr, w)
        self.stage += 1

    def finish_alone(self):
        names = ("halves", "chips", "complete")
        while self.stage < 3:
            self.done(_run_exchange(self.next_exchange(), name=f"{self.tag}_{names[self.stage]}"))
        return self.result


def _flat_rows():
    used = sum(r for _, r in FSDP_SECTIONS)
    return used, -(-used // ROW_ALIGN) * ROW_ALIGN


def _cols_to_chunks(full):
    rows, cols = full.shape
    t = full.reshape(rows, N_CHIPS, cols // N_CHIPS).transpose(1, 0, 2)
    return t.reshape(N_CHIPS, -1, FLAT_W)


def _chunks_to_cols(chunks, rows, cols):
    return chunks.reshape(N_CHIPS, rows, cols // N_CHIPS).transpose(1, 0, 2).reshape(rows, cols)


def _pad_heads(w, real):
    lead = w.shape[:-1]
    t = w.reshape(lead + (HEADS, real))
    t = jnp.pad(t, [(0, 0)] * len(lead) + [(0, 0), (0, HEAD_PAD - real)])
    return t.reshape(lead + (HEADS * HEAD_PAD,))


def _unpad_heads(w, real):
    lead = w.shape[:-1]
    return w.reshape(lead + (HEADS, HEAD_PAD))[..., :real].reshape(lead + (HEADS * real,))


def _pad_value_lanes(w, axis):
    w = jnp.moveaxis(w, axis, -1)
    lead = w.shape[:-1]
    t = w.reshape(lead + (HEADS, 64))
    t = jnp.pad(t, [(0, 0)] * len(lead) + [(0, 0), (HEAD_PAD - 64, 0)])
    return jnp.moveaxis(t.reshape(lead + (HEADS * HEAD_PAD,)), -1, axis)


def _unpad_value_lanes(w, axis):
    w = jnp.moveaxis(w, axis, -1)
    lead = w.shape[:-1]
    t = w.reshape(lead + (HEADS, HEAD_PAD))[..., HEAD_PAD - 64:]
    return jnp.moveaxis(t.reshape(lead + (HEADS * 64,)), -1, axis)


def _pad_w_in(w):
    z = jnp.zeros((w.shape[0], NOPE), w.dtype)
    z2 = jnp.zeros((w.shape[0], HEAD_PAD - NOPE - ROPE), w.dtype)
    return jnp.concatenate([w[:, :1408], z, w[:, 1408:], z2], axis=1)


def _unpad_w_in(w):
    return jnp.concatenate([w[:, :1408], w[:, 1408 + NOPE:1408 + NOPE + ROPE]], axis=1)


def _rope_tables(positions):
    freqs = ROPE_THETA ** (-jnp.arange(0, ROPE, 2, dtype=F32) / ROPE)
    ang = positions.astype(F32)[..., None] * freqs
    cos, sin = jnp.cos(ang), jnp.sin(ang)
    lead = cos.shape[:-1]
    ones = jnp.ones(lead + (NOPE,), F32)
    zeros_n = jnp.zeros(lead + (NOPE,), F32)
    zeros_p = jnp.zeros(lead + (HEAD_PAD - NOPE - ROPE,), F32)
    ctab = jnp.concatenate([ones, cos, cos, zeros_p], axis=-1)
    stab = jnp.concatenate([zeros_n, -sin, sin, zeros_p], axis=-1)
    return ctab, stab


def _layer_weights(full, p, l):
    ws = p["gmlp_ws"][l]
    tril = jnp.tril(jnp.ones((CHUNK, CHUNK), bool))
    bs = p["gmlp_bs"][l]
    bexp = jnp.repeat(bs.reshape(GROUPS // 2, 2, CHUNK).transpose(0, 2, 1), GROUP_DIM, axis=2)
    return dict(
        w_in=_pad_w_in(full["w_in"]),
        w_uq=_pad_heads(full["mla_w_uq"], NOPE + ROPE),
        w_ukv=full["mla_w_ukv"],
        w_out_a=_pad_value_lanes(full["w_out"][D_GMLP:], 0),
        w_out_g=full["w_out"][:D_GMLP],
        w_ff1=full["w_ff1"],
        w_ff2=full["w_ff2"],
        ws=ws,
        wst=jnp.where(tril[None], ws, 0.0).transpose(0, 2, 1).astype(BF16),
        bexp=bexp,
        g_mix=p["norm_mix_g"][l][None],
        g_ffn=p["norm_ffn_g"][l][None],
        g_q=p["mla_q_norm_g"][l][None],
        g_kv=p["mla_kv_norm_g"][l][None],
        g_og=p["out_norm_gmlp_g"][l][None],
        g_oa=_pad_value_lanes(p["out_norm_mla_g"][l], 0)[None],
    )


def _local_step(x3, target3, positions, mods, final_g, plan):
    bsz, seq, d = x3.shape
    tok = bsz * seq
    tmt = min(512, seq)
    tmk = min(1024, seq)
    ctab, stab = _rope_tables(positions)
    lw = [None] * DEPTH

    def flat(t):
        return t.reshape(tok, t.shape[-1])

    def cube(t):
        return t.reshape(bsz, seq, t.shape[-1])

    def carrying(l, tag, fn, *args, **kw):
        side = plan.host(l, tag)
        if side is None:
            return fn(*args, **kw)
        res, side_outs = fn(*args, side=side, **kw)
        plan.hosted(l, tag, side_outs)
        return res

    saved = []
    x = x3
    for l in range(DEPTH):
        lw[l] = plan.layer(l)
        w, mod = lw[l], mods[l]
        h1 = _normmod_fwd(x, w["g_mix"], mod, SHIFT1, SCALE1, name=f"l{l}_normmod1")
        z = cube(_mm(flat(h1), w["w_in"], dims="nn", name=f"l{l}_w_in", tm=tmt, tn=D_IN_PAD, tk=d))
        yg = _gmlp_fwd(z, w["ws"], w["bexp"], w["g_og"], name=f"l{l}_gmlp_fwd")
        q, kv, kp = _mla_prep_fwd(z, w["g_q"], w["g_kv"], w["w_uq"], w["w_ukv"], ctab, stab, name=f"l{l}_mla_prep")
        o, lse = carrying(l, "fwd_attn", _attn_fwd, q, kv, kp, name=f"l{l}_attn_fwd")
        ya = _onorm_fwd(o, w["g_oa"], name=f"l{l}_onorm_fwd")
        pg = _mm(flat(yg), w["w_out_g"], dims="nn", name=f"l{l}_w_out_g", tm=tmt, tn=d, tk=D_GMLP)

        def out_epi(acc, pgv, xv, gm):
            po = acc + pgv
            return po, xv + gm[0][GATE1:GATE1 + 1, :] * po

        po, x_mid = _mm(flat(ya), w["w_out_a"], dims="nn", name=f"l{l}_w_out_a", tm=tmt, tn=d, tk=d,
                        out_dtypes=(F32, F32), epilogue=out_epi, extras=(pg, flat(x), mod),
                        extra_specs=(None, None, _mod_spec(tmt, d, seq)))
        x_mid = cube(x_mid)
        h2 = _normmod_fwd(x_mid, w["g_ffn"], mod, SHIFT2, SCALE2, name=f"l{l}_normmod2")

        def act_epi(acc):
            r = jnp.maximum(acc, 0.0)
            return (r * r,)

        r = carrying(l, "fwd_ff1", _mm, flat(h2), w["w_ff1"], dims="nn", name=f"l{l}_w_ff1", tm=tmt, tn=1024,
                     tk=d, out_dtypes=(BF16,), epilogue=act_epi, weights_outer=True)

        def ff2_epi(acc, xv, gm):
            return acc, xv + gm[0][GATE2:GATE2 + 1, :] * acc

        f, x_out = _mm(r, w["w_ff2"], dims="nn", name=f"l{l}_w_ff2", tm=tmk, tn=d, tk=1024,
                       out_dtypes=(F32, F32), epilogue=ff2_epi, extras=(flat(x_mid), mod),
                       extra_specs=(None, _mod_spec(tmk, d, seq)))
        saved.append(dict(x_in=x, h1=h1, z=z, q=q, kv=kv, kp=kp, o=o, lse=lse, ya=ya, yg=yg, po=cube(po),
                          x_mid=x_mid, h2=h2, r=r, f=cube(f)))
        x = cube(x_out)

    grads = [dict() for _ in range(DEPTH)]
    dmods = [None] * DEPTH
    top = DEPTH - 1
    node = _resnode_bwd(x, final_g[None], name="final_loss_bwd", target3=target3,
                        branch3=saved[top]["f"], mod_gate=mods[top], gate_row=GATE2)
    loss_part = node["loss"][0, 0]
    d_final_g = node["dg"][0]
    for l in range(DEPTH - 1, -1, -1):
        w, mod, s = lw[l], mods[l], saved[l]
        dx_out, dfb, dgate2 = node["dx"], flat(node["dbr"]), node["dgate"][:, 0]

        def dact_epi(acc, rv):
            return (acc * (2.0 * jnp.sqrt(rv.astype(F32))),)

        da = carrying(l, "bwd_d_r", _mm, dfb, w["w_ff2"], dims="nt", name=f"l{l}_d_r", tm=tmt, tn=1024, tk=d,
                      out_dtypes=(BF16,), epilogue=dact_epi, extras=(s["r"],), weights_outer=True)
        grads[l]["w_ff2"] = _mm(s["r"], dfb, dims="tn", name=f"l{l}_dw_ff2", tm=1024, tn=d, tk=1024)
        grads[l]["w_ff1"] = _mm(flat(s["h2"]), da, dims="tn", name=f"l{l}_dw_ff1", tm=d, tn=1024, tk=1024)
        dh2 = _mm(da, w["w_ff1"], dims="nt", name=f"l{l}_d_h2", tm=tmk, tn=d, tk=1024)
        node = _resnode_bwd(s["x_mid"], w["g_ffn"], name=f"l{l}_resnode_ffn", dh3=cube(dh2), dres3=dx_out,
                            mod_nm=mod, rows=(SHIFT2, SCALE2), branch3=s["po"], mod_gate=mod, gate_row=GATE1)
        grads[l]["norm_ffn_g"] = node["dg"][0]
        dshift2, dscale2 = node["dnm"][:, 0], node["dnm"][:, 1]
        dx_mid, dpo, dgate1 = node["dx"], flat(node["dbr"]), node["dgate"][:, 0]

        dya = _mm(dpo, w["w_out_a"], dims="nt", name=f"l{l}_d_ya", tm=tmt, tn=d, tk=d)
        dyg = _mm(dpo, w["w_out_g"], dims="nt", name=f"l{l}_d_yg", tm=tmt, tn=D_GMLP, tk=d)
        dw_out_a = _mm(flat(s["ya"]), dpo, dims="tn", name=f"l{l}_dw_out_a", tm=d, tn=d, tk=1024)
        dw_out_g = _mm(flat(s["yg"]), dpo, dims="tn", name=f"l{l}_dw_out_g", tm=D_GMLP, tn=d, tk=1024)
        grads[l]["w_out"] = jnp.concatenate([dw_out_g, _unpad_value_lanes(dw_out_a, 0)], axis=0)

        duv, dws, dbs, dg_og = _gmlp_bwd(s["z"], cube(dyg), w["ws"], w["wst"], w["bexp"], w["g_og"],
                                         name=f"l{l}_gmlp_bwd")
        grads[l]["gmlp_ws"], grads[l]["gmlp_bs"], grads[l]["out_norm_gmlp_g"] = dws, dbs, dg_og[0]

        do, dl, dg_oa = _onorm_bwd(s["o"], cube(dya), w["g_oa"], name=f"l{l}_onorm_bwd")
        grads[l]["out_norm_mla_g"] = _unpad_value_lanes(dg_oa[0], 0)
        dq = _attn_bwd_dq(s["q"], s["kv"], s["kp"], do, s["lse"], dl, name=f"l{l}_attn_dq")
        dk, dv = carrying(l, "bwd_attn_dkv", _attn_bwd_dkv, s["q"], s["kv"], s["kp"], do, s["lse"], dl,
                          name=f"l{l}_attn_dkv")
        dzm, cq, dqb, ckv, dkvb, dg_q, dg_kv = _mla_prep_bwd(
            s["z"], dq, dk, dv, w["g_q"], w["g_kv"], w["w_uq"], w["w_ukv"], ctab, stab, name=f"l{l}_mla_prep_bwd")
        grads[l]["mla_q_norm_g"], grads[l]["mla_kv_norm_g"] = dg_q[0], dg_kv[0]
        dw_uq = _mm(flat(cq), flat(dqb), dims="tn", name=f"l{l}_dw_uq", tm=Q_RANK, tn=1024, tk=1024)
        grads[l]["mla_w_uq"] = _unpad_heads(dw_uq, NOPE + ROPE)
        grads[l]["mla_w_ukv"] = _mm(flat(ckv), flat(dkvb), dims="tn", name=f"l{l}_dw_ukv", tm=KV_RANK, tn=1024, tk=1024)

        h1f = flat(s["h1"])
        dw_in_uv = carrying(l, "bwd_dw_in", _mm, h1f, flat(duv), dims="tn", name=f"l{l}_dw_in_uv", tm=d, tn=1024,
                            tk=1024)
        dw_in_m = _mm(h1f, flat(dzm), dims="tn", name=f"l{l}_dw_in_m", tm=d, tn=512, tk=1024)
        grads[l]["w_in"] = _unpad_w_in(jnp.concatenate([dw_in_uv, dw_in_m], axis=1))
        dh1_uv = _mm(flat(duv), w["w_in"][:, :1024], dims="nt", name=f"l{l}_d_h1_uv", tm=tmt, tn=d, tk=1024)
        dh1 = _mm(flat(dzm), w["w_in"][:, 1024:], dims="nt", name=f"l{l}_d_h1", tm=tmt, tn=d, tk=512,
                  epilogue=lambda acc, prev: (acc + prev,), extras=(dh1_uv,))
        if l > 0:
            node = _resnode_bwd(s["x_in"], w["g_mix"], name=f"l{l}_resnode_mix", dh3=cube(dh1), dres3=dx_mid,
                                mod_nm=mod, rows=(SHIFT1, SCALE1), branch3=saved[l - 1]["f"],
                                mod_gate=mods[l - 1], gate_row=GATE2)
        else:
            node = _resnode_bwd(s["x_in"], w["g_mix"], name=f"l{l}_resnode_mix", dh3=cube(dh1), dres3=dx_mid,
                                mod_nm=mod, rows=(SHIFT1, SCALE1))
        grads[l]["norm_mix_g"] = node["dg"][0]
        dshift1, dscale1 = node["dnm"][:, 0], node["dnm"][:, 1]
        dmods[l] = jnp.stack([dshift1, dscale1, dgate1, dshift2, dscale2, dgate2], axis=1)
        plan.layer_grads(l, grads[l])
    return loss_part, node["dx"], d_final_g, dmods


W_NAMES = ("w_ada", "b_ada", "norm_mix_g", "w_in", "gmlp_ws", "gmlp_bs", "mla_q_norm_g", "mla_kv_norm_g",
           "mla_w_uq", "mla_w_ukv", "out_norm_gmlp_g", "out_norm_mla_g", "w_out", "norm_ffn_g", "w_ff1", "w_ff2",
           "final_norm_g")
FLAT_KEY = {"w_in": "w_in", "w_uq": "mla_w_uq", "w_ukv": "mla_w_ukv", "w_out": "w_out", "w_ff1": "w_ff1",
            "w_ff2": "w_ff2"}
COL_SHARDED = ("w_in", "w_uq", "w_ukv", "w_ff1")
FULL_SHAPE = {"w_in": (D_MODEL, D_IN), "w_uq": (Q_RANK, HEADS * (NOPE + ROPE)), "w_ukv": (KV_RANK, HEADS * 128),
              "w_out": (D_MODEL, D_MODEL), "w_ff1": (D_MODEL, D_FF), "w_ff2": (D_FF, D_MODEL)}
SMALL_NAMES = ("norm_mix_g", "gmlp_ws", "gmlp_bs", "mla_q_norm_g", "mla_kv_norm_g", "out_norm_gmlp_g",
               "out_norm_mla_g", "norm_ffn_g", "final_norm_g")


def _silu(v):
    return v * (1.0 / (1.0 + jnp.exp(-v)))


class _CommPlan:
    def __init__(self, weights, ids, dev, core):
        self.weights, self.ids, self.dev, self.core = weights, ids, dev, core
        self.used, self.rows = _flat_rows()
        self.flat = [self._flat_weights(l) for l in range(DEPTH)]
        self.lw, self.rs, self.grads, self.spread = {}, {}, {}, None
        (gath,) = _run_exchange(_gather_spread(self.flat[0]), name="l0_gather_spread")
        (gath,) = _run_exchange(_gather_pass_on(gath), name="l0_gather_pass_on")
        self._set_layer(0, gath)

    def _flat_weights(self, l):
        pieces = [self.weights[FLAT_KEY[nm]][l].reshape(-1, FLAT_W) for nm, _ in FSDP_SECTIONS]
        pieces.append(jnp.zeros((self.rows - self.used, FLAT_W), F32))
        return jnp.concatenate(pieces, axis=0).astype(BF16)

    def _set_layer(self, l, gath):
        hr = self.rows // 2
        mine = lax.dynamic_slice(self.flat[l], (self.core * hr, 0), (hr, FLAT_W))
        gath = lax.dynamic_update_slice(gath, mine[None], (self.dev, 0, 0))
        w_gath = gath.reshape(N_CHIPS, self.rows, FLAT_W)
        full, off = {}, 0
        for nm, nrows in FSDP_SECTIONS:
            sec = w_gath[:, off:off + nrows]
            off += nrows
            rows, cols = FULL_SHAPE[nm]
            full[FLAT_KEY[nm]] = _chunks_to_cols(sec, rows, cols) if nm in COL_SHARDED else sec.reshape(rows, cols)
        self.lw[l] = _layer_weights(full, self.weights, l)

    def layer(self, l):
        return self.lw[l]

    def host(self, l, tag):
        if tag == "fwd_attn" and l + 1 < DEPTH:
            return _gather_spread(self.flat[l + 1])
        if tag == "fwd_ff1" and l + 1 < DEPTH:
            return _gather_pass_on(self.spread)
        if tag.startswith("bwd_") and l + 1 in self.rs:
            return self.rs[l + 1].next_exchange()
        return None

    def hosted(self, l, tag, outs):
        if tag == "fwd_attn":
            self.spread = outs[0]
        elif tag == "fwd_ff1":
            self._set_layer(l + 1, outs[0])
        else:
            self.rs[l + 1].done(outs)

    def layer_grads(self, l, grads):
        self.grads[l] = grads
        pieces = []
        for nm, nrows in FSDP_SECTIONS:
            g = grads[FLAT_KEY[nm]]
            pieces.append(_cols_to_chunks(g) if nm in COL_SHARDED else g.reshape(N_CHIPS, nrows, FLAT_W))
        pieces.append(jnp.zeros((N_CHIPS, self.rows - self.used, FLAT_W), F32))
        self.rs[l] = _ReduceScatter(jnp.concatenate(pieces, axis=1), self.ids, f"l{l}_rs")
        if l == 0:
            self.rs[l].finish_alone()

    def sharded_grads(self):
        per = {FLAT_KEY[nm]: [] for nm, _ in FSDP_SECTIONS}
        for l in range(DEPTH):
            shard, off = self.rs[l].result, 0
            for nm, nrows in FSDP_SECTIONS:
                key = FLAT_KEY[nm]
                per[key].append(shard[off:off + nrows].reshape(self.weights[key].shape[1:]))
                off += nrows
        return {key: jnp.stack(parts, axis=0) for key, parts in per.items()}


def kernel(x, c, positions, w_ada, b_ada, norm_mix_g, w_in, gmlp_ws, gmlp_bs, mla_q_norm_g, mla_kv_norm_g, mla_w_uq, mla_w_ukv, out_norm_gmlp_g, out_norm_mla_g, w_out, norm_ffn_g, w_ff1, w_ff2, final_norm_g, loss_target, m_w_ada, m_b_ada, m_norm_mix_g, m_w_in, m_gmlp_ws, m_gmlp_bs, m_mla_q_norm_g, m_mla_kv_norm_g, m_mla_w_uq, m_mla_w_ukv, m_out_norm_gmlp_g, m_out_norm_mla_g, m_w_out, m_norm_ffn_g, m_w_ff1, m_w_ff2, m_final_norm_g, v_w_ada, v_b_ada, v_norm_mix_g, v_w_in, v_gmlp_ws, v_gmlp_bs, v_mla_q_norm_g, v_mla_kv_norm_g, v_mla_w_uq, v_mla_w_ukv, v_out_norm_gmlp_g, v_out_norm_mla_g, v_w_out, v_norm_ffn_g, v_w_ff1, v_w_ff2, v_final_norm_g):
    weights = dict(w_ada=w_ada, b_ada=b_ada, norm_mix_g=norm_mix_g, w_in=w_in, gmlp_ws=gmlp_ws, gmlp_bs=gmlp_bs,
                   mla_q_norm_g=mla_q_norm_g, mla_kv_norm_g=mla_kv_norm_g, mla_w_uq=mla_w_uq, mla_w_ukv=mla_w_ukv,
                   out_norm_gmlp_g=out_norm_gmlp_g, out_norm_mla_g=out_norm_mla_g, w_out=w_out,
                   norm_ffn_g=norm_ffn_g, w_ff1=w_ff1, w_ff2=w_ff2, final_norm_g=final_norm_g)
    mom_m = dict(zip(W_NAMES, (m_w_ada, m_b_ada, m_norm_mix_g, m_w_in, m_gmlp_ws, m_gmlp_bs, m_mla_q_norm_g,
                               m_mla_kv_norm_g, m_mla_w_uq, m_mla_w_ukv, m_out_norm_gmlp_g, m_out_norm_mla_g,
                               m_w_out, m_norm_ffn_g, m_w_ff1, m_w_ff2, m_final_norm_g)))
    mom_v = dict(zip(W_NAMES, (v_w_ada, v_b_ada, v_norm_mix_g, v_w_in, v_gmlp_ws, v_gmlp_bs, v_mla_q_norm_g,
                               v_mla_kv_norm_g, v_mla_w_uq, v_mla_w_ukv, v_out_norm_gmlp_g, v_out_norm_mla_g,
                               v_w_out, v_norm_ffn_g, v_w_ff1, v_w_ff2, v_final_norm_g)))
    bsz, seq, d = x.shape
    px, py, pc = _position()
    chip = 2 * px + py
    dev = 2 * chip + pc
    ids = jnp.stack([pc, chip]).astype(jnp.int32)
    n_ex = N_DEV * bsz
    ada_cols = w_ada.shape[-1]

    c_all = _allgather8(c.reshape(bsz * d // 128, 128), name="gather_c").reshape(n_ex, d)
    mod_parts = []
    for l in range(DEPTH):
        bias = lax.dynamic_slice(b_ada[l], (chip * ada_cols,), (ada_cols,))[None]
        mod_parts.append(_mm(c_all, w_ada[l], dims="nn", name=f"l{l}_mod", tm=n_ex, tn=ada_cols, tk=d,
                             epilogue=lambda acc, bv: (acc + bv,), extras=(bias,),
                             extra_specs=(pl.BlockSpec((1, ada_cols), lambda i, j, k: (0, j)),), a_fn=_silu))
    mod_g = _allgather8(jnp.concatenate(mod_parts, axis=0), name="gather_mod")
    mod_g = mod_g.reshape(N_CHIPS, 2, DEPTH, n_ex, ada_cols)[:, 0]
    mod_full = mod_g.transpose(1, 2, 0, 3).reshape(DEPTH, n_ex, N_CHIPS * ada_cols)
    mod_mine = lax.dynamic_slice(mod_full, (0, dev * bsz, 0), (DEPTH, bsz, N_MOD * d))
    mod_mine = jnp.pad(mod_mine.reshape(DEPTH, bsz, N_MOD, d), ((0, 0), (0, 0), (0, MOD_ROWS - N_MOD), (0, 0)))
    mods = [mod_mine[l] for l in range(DEPTH)]

    plan = _CommPlan(weights, ids, dev, pc)
    loss_part, grad_x, d_final_g, dmods = _local_step(x, loss_target, positions, mods, final_norm_g, plan)
    loss = lax.psum(loss_part, ("x", "y", "c"))
    grads = plan.grads
    grad = plan.sharded_grads()

    small = {nm: (d_final_g if nm == "final_norm_g" else jnp.stack([grads[l][nm] for l in range(DEPTH)], axis=0))
             for nm in SMALL_NAMES}
    svec = jnp.concatenate([small[nm].reshape(-1) for nm in SMALL_NAMES])
    n_small = svec.shape[0]
    srows = -(-n_small // (8 * FLAT_W)) * 8
    svec = jnp.pad(svec, (0, srows * FLAT_W - n_small)).reshape(srows, FLAT_W)
    ssum = _sum_leading(_allgather8(svec, name="gather_small_grads"), name="sum_small_grads").reshape(-1)
    off = 0
    for nm in SMALL_NAMES:
        size = weights[nm].size
        grad[nm] = ssum[off:off + size].reshape(weights[nm].shape)
        off += size

    dmod = jnp.stack(dmods, axis=1).reshape(bsz * DEPTH * N_MOD, d)
    dmod_all = _allgather8(dmod, name="gather_dmod").reshape(n_ex, DEPTH, N_MOD * d)
    gw, gb = [], []
    for l in range(DEPTH):
        dm = dmod_all[:, l]
        dm_cols = lax.dynamic_slice(dm, (0, chip * ada_cols), (n_ex, ada_cols))
        gw.append(_mm(c_all, dm_cols, dims="tn", name=f"l{l}_dw_ada", tm=d, tn=ada_cols, tk=n_ex, a_fn=_silu))
        gb.append(_sum_leading(dm.reshape(n_ex, N_MOD * d // FLAT_W, FLAT_W), name=f"l{l}_db_ada").reshape(-1))
    grad["w_ada"] = jnp.stack(gw, axis=0)
    grad["b_ada"] = jnp.stack(gb, axis=0)

    delta, new_m, new_v = {}, {}, {}
    for nm in W_NAMES:
        delta[nm], new_m[nm], new_v[nm] = _adamw(weights[nm], grad[nm], mom_m[nm], mom_v[nm], name=f"adamw_{nm}")
    return (loss, grad_x, *[grad[nm] for nm in W_NAMES], *[delta[nm] for nm in W_NAMES],
            *[new_m[nm] for nm in W_NAMES], *[new_v[nm] for nm in W_NAMES])
```

```python
import functools
import math

import jax
import jax.numpy as jnp
from jax import lax
from jax.experimental import pallas as pl
from jax.experimental.pallas import tpu as pltpu

F32 = jnp.float32
BF16 = jnp.bfloat16

D_MODEL = 1024
DEPTH = 2
D_GMLP = 512
GROUPS = 8
GROUP_DIM = 64
CHUNK = 128
HEADS = 8
NOPE = 64
ROPE = 32
HEAD_PAD = 128
Q_RANK = 256
KV_RANK = 128
D_FF = 4096
N_MOD = 6
MOD_ROWS = 8
EPS = 1e-6
ROPE_THETA = 10000.0
D_IN = 1440
D_IN_PAD = 1536
ATTN_SCALE = (NOPE + ROPE) ** -0.5
LOG2E = math.log2(math.e)
SCALE_LOG2 = ATTN_SCALE * LOG2E
N_CHIPS = 4
N_DEV = 8

ADAM_LR = 0.001
ADAM_B1 = 0.9
ADAM_B2 = 0.999
ADAM_EPS = 1e-08
ADAM_WD = 0.01
ADAM_STEP = 10

VMEM_LIMIT = 48 * 1024 * 1024
FLAT_W = 1024
ROW_ALIGN = 256

NN = (((1,), (0,)), ((), ()))
NT = (((1,), (1,)), ((), ()))
TN = (((0,), (0,)), ((), ()))
MESH = pl.DeviceIdType.MESH

SHIFT1, SCALE1, GATE1, SHIFT2, SCALE2, GATE2 = range(6)

FSDP_SECTIONS = (("w_out", 256), ("w_in", 360), ("w_uq", 48), ("w_ukv", 32))


def _cparams(vmem=VMEM_LIMIT):
    return pltpu.CompilerParams(vmem_limit_bytes=vmem)


def _dot(a, b, dims=NN):
    return lax.dot_general(a, b, dims, preferred_element_type=F32)


def _iota(shape, axis):
    return lax.broadcasted_iota(jnp.int32, shape, axis)


def _gelu(x):
    k = math.sqrt(2.0 / math.pi)
    return 0.5 * x * (1.0 + jnp.tanh(k * (x + 0.044715 * (x * x * x))))


def _gelu_grad(x):
    k = math.sqrt(2.0 / math.pi)
    t = jnp.tanh(k * (x + 0.044715 * (x * x * x)))
    return 0.5 * (1.0 + t) + 0.5 * x * (1.0 - t * t) * (k * (1.0 + 3.0 * 0.044715 * (x * x)))


def _rms_fwd(x, g, n):
    r = lax.rsqrt(jnp.sum(x * x, axis=-1, keepdims=True) * (1.0 / n) + EPS)
    return x * r * g


def _rms_bwd(x, g, dy, n):
    r = lax.rsqrt(jnp.sum(x * x, axis=-1, keepdims=True) * (1.0 / n) + EPS)
    xh = x * r
    dxh = dy * g
    dx = r * (dxh - xh * (jnp.sum(dxh * xh, axis=-1, keepdims=True) * (1.0 / n)))
    dg = jnp.sum(dy * xh, axis=0, keepdims=True)
    return dx, dg


def _pick_rows(rows, limit):
    if rows <= limit:
        return rows
    for t in range(limit, 7, -8):
        if rows % t == 0:
            return t
    return rows


def _mm(a, b, *, dims, name, tm=512, tn=1024, tk=1024, out_dtypes=(F32,), epilogue=None,
        extras=(), extra_specs=(), a_fn=None, weights_outer=False, side=None, b_block=None, n=None,
        out_into=None):
    if dims == "tn":
        kk, m = a.shape
    else:
        m, kk = a.shape
    if n is None:
        n = b.shape[0] if dims == "nt" else b.shape[1]
    tm, tn, tk = min(tm, m), min(tn, n), min(tk, kk)
    assert m % tm == 0 and n % tn == 0 and kk % tk == 0, (name, a.shape, b.shape, tm, tn, tk)
    ni, nj, nk = m // tm, n // tn, kk // tk

    def spec(shape, pick):
        if weights_outer:
            return pl.BlockSpec(shape, lambda j, i, k: pick(i, j, k))
        return pl.BlockSpec(shape, pick)

    if dims == "tn":
        a_spec = spec((tk, tm), lambda i, j, k: (k, i))
    else:
        a_spec = spec((tm, tk), lambda i, j, k: (i, k))
    if b_block is not None:
        b_spec = spec(*b_block)
    elif dims == "nt":
        b_spec = spec((tn, tk), lambda i, j, k: (j, k))
    else:
        b_spec = spec((tk, tn), lambda i, j, k: (k, j))
    o_spec = spec((tm, tn), lambda i, j, k: (i, j))
    out_shape = [jax.ShapeDtypeStruct((m, n), dt) for dt in out_dtypes]
    out_specs = [o_spec] * len(out_dtypes)
    prev, io_aliases = (), {}
    if out_into is not None:
        full_shape, block, index, before = out_into
        assert len(out_dtypes) == 1 and not extras
        out_shape = [jax.ShapeDtypeStruct(full_shape, out_dtypes[0])]
        out_specs = [spec(block, index)]
        if before is not None:
            prev, io_aliases = (before,), {2: 0}
    assert not (weights_outer and extra_specs)
    dn = {"nn": NN, "nt": NT, "tn": TN}[dims]
    n_ex, n_out = len(extras), len(out_dtypes)
    e_specs = [o_spec if s is None else s for s in (tuple(extra_specs) + (None,) * n_ex)[:n_ex]]

    n_prev = len(prev)

    def body(*refs):
        a_ref, b_ref = refs[0], refs[1]
        e_refs = refs[2 + n_prev:2 + n_prev + n_ex]
        o_refs = refs[2 + n_prev + n_ex:2 + n_prev + n_ex + n_out]
        av = a_ref[...]
        if a_fn is not None:
            av = a_fn(av)
        part = _dot(av.astype(BF16), b_ref[...].astype(BF16), dn)

        def finish(acc):
            outs = (acc,) if epilogue is None else epilogue(acc, *[e[...] for e in e_refs])
            for o_ref, o in zip(o_refs, outs):
                o_ref[...] = o.astype(o_ref.dtype)

        if nk == 1:
            finish(part)
        else:
            acc_ref = refs[-1]
            k = pl.program_id(2)

            @pl.when(k == 0)
            def _():
                acc_ref[...] = part

            @pl.when(k > 0)
            def _():
                acc_ref[...] += part

            @pl.when(k == nk - 1)
            def _():
                finish(acc_ref[...])

    outs, side_outs = _hosted_call(
        body, name=name, grid=(nj, ni, nk) if weights_outer else (ni, nj, nk),
        in_specs=[a_spec, b_spec] + [ANY_SPEC] * n_prev + e_specs,
        out_specs=out_specs, out_shape=out_shape,
        scratch_shapes=[pltpu.VMEM((tm, tn), F32)] if nk > 1 else [],
        args=(a, b, *prev, *extras), side=side, io_aliases=io_aliases)
    res = outs[0] if n_out == 1 else outs
    return res if side is None else (res, side_outs)


def _mod_spec(tm, tn, seq):
    return pl.BlockSpec((1, MOD_ROWS, tn), lambda i, j, k: ((i * tm) // seq, 0, j))


def _normmod_fwd(x3, g, mod, shift_row, scale_row, *, name, tb=256):
    bsz, seq, d = x3.shape
    tb = min(tb, seq)

    def body(x_ref, g_ref, mod_ref, h_ref):
        m = mod_ref[0]
        nrm = _rms_fwd(x_ref[0], g_ref[...], d)
        h = nrm * (1.0 + m[scale_row:scale_row + 1, :]) + m[shift_row:shift_row + 1, :]
        h_ref[0] = h.astype(BF16)

    return pl.pallas_call(
        body, name=name, grid=(bsz, seq // tb),
        in_specs=[pl.BlockSpec((1, tb, d), lambda b, i: (b, i, 0)),
                  pl.BlockSpec((1, d), lambda b, i: (0, 0)),
                  pl.BlockSpec((1, MOD_ROWS, d), lambda b, i: (b, 0, 0))],
        out_specs=pl.BlockSpec((1, tb, d), lambda b, i: (b, i, 0)),
        out_shape=jax.ShapeDtypeStruct((bsz, seq, d), BF16),
        compiler_params=_cparams(),
    )(x3, g, mod)


def _pair_mean_exact(x, lo):
    s_lo = jnp.sum(jnp.where(lo, x, 0.0), axis=-1, keepdims=True)
    s_hi = jnp.sum(jnp.where(lo, 0.0, x), axis=-1, keepdims=True)
    return jnp.where(lo, s_lo, s_hi) * (1.0 / GROUP_DIM)


def _gmlp_pair_fwd(gv_p, w0, w1, bias, lo):
    mu = _pair_mean_exact(gv_p, lo)
    dlt = gv_p - mu
    var = _pair_mean_exact(dlt * dlt, lo)
    rstd = lax.rsqrt(var + EPS)
    vn = dlt * rstd
    vnb = vn.astype(BF16)
    mixed = jnp.where(lo, _dot(w0, vnb), _dot(w1, vnb)) + bias
    return vn, vnb, rstd, mixed


def _tril_bf16(w):
    t = w.shape[-1]
    return jnp.where(_iota((t, t), 1) <= _iota((t, t), 0), w, 0.0).astype(BF16)


def _gmlp_fwd(z3, ws, bexp, g_out, *, name):
    bsz, seq, _ = z3.shape
    nc = seq // CHUNK

    def body(u_ref, v_ref, ws_ref, b_ref, g_ref, y_ref):
        lo = _iota((CHUNK, 128), 1) < GROUP_DIM
        gu = _gelu(u_ref[0])
        gv = _gelu(v_ref[0])
        parts = []
        for p in range(GROUPS // 2):
            sl = slice(128 * p, 128 * p + 128)
            w0 = _tril_bf16(ws_ref[2 * p])
            w1 = _tril_bf16(ws_ref[2 * p + 1])
            _, _, _, mixed = _gmlp_pair_fwd(gv[:, sl], w0, w1, b_ref[p], lo)
            parts.append(gu[:, sl] * mixed)
        yg = jnp.concatenate(parts, axis=1)
        y_ref[0] = _rms_fwd(yg, g_ref[...], D_GMLP).astype(BF16)

    return pl.pallas_call(
        body, name=name, grid=(bsz, nc),
        in_specs=[pl.BlockSpec((1, CHUNK, D_GMLP), lambda b, i: (b, i, 0)),
                  pl.BlockSpec((1, CHUNK, D_GMLP), lambda b, i: (b, i, 1)),
                  pl.BlockSpec((GROUPS, CHUNK, CHUNK), lambda b, i: (0, 0, 0)),
                  pl.BlockSpec((GROUPS // 2, CHUNK, 128), lambda b, i: (0, 0, 0)),
                  pl.BlockSpec((1, D_GMLP), lambda b, i: (0, 0))],
        out_specs=pl.BlockSpec((1, CHUNK, D_GMLP), lambda b, i: (b, i, 0)),
        out_shape=jax.ShapeDtypeStruct((bsz, seq, D_GMLP), BF16),
        compiler_params=_cparams(),
    )(z3, z3, ws, bexp, g_out)


def _gmlp_bwd(z3, dyn3, ws, wst, bexp, g_out, *, name):
    bsz, seq, _ = z3.shape
    nc = seq // CHUNK
    npair = GROUPS // 2

    def body(u_ref, v_ref, dy_ref, ws_ref, wst_ref, b_ref, g_ref, duv_ref, dws_ref, dbs_ref, dg_ref, dbacc):
        first = jnp.logical_and(pl.program_id(0) == 0, pl.program_id(1) == 0)
        last = jnp.logical_and(pl.program_id(0) == bsz - 1, pl.program_id(1) == nc - 1)

        @pl.when(first)
        def _():
            dws_ref[...] = jnp.zeros_like(dws_ref)
            dg_ref[...] = jnp.zeros_like(dg_ref)
            dbacc[...] = jnp.zeros_like(dbacc)

        lo = _iota((CHUNK, 128), 1) < GROUP_DIM
        tril = _iota((CHUNK, CHUNK), 1) <= _iota((CHUNK, CHUNK), 0)
        u = u_ref[0]
        v = v_ref[0]
        gu = _gelu(u)
        gv = _gelu(v)
        fwd = []
        for p in range(npair):
            sl = slice(128 * p, 128 * p + 128)
            w0 = _tril_bf16(ws_ref[2 * p])
            w1 = _tril_bf16(ws_ref[2 * p + 1])
            fwd.append(_gmlp_pair_fwd(gv[:, sl], w0, w1, b_ref[p], lo))
        yg = jnp.concatenate([gu[:, 128 * p:128 * p + 128] * fwd[p][3] for p in range(npair)], axis=1)
        dyg, dg = _rms_bwd(yg, g_ref[...], dy_ref[0], D_GMLP)
        dg_ref[...] += dg
        du_parts, dv_parts = [], []
        for p in range(npair):
            sl = slice(128 * p, 128 * p + 128)
            vn, vnb, rstd, mixed = fwd[p]
            dyg_p = dyg[:, sl]
            dmixed = dyg_p * gu[:, sl]
            dbacc[p] += dmixed
            dm_lo = jnp.where(lo, dmixed, 0.0).astype(BF16)
            dm_hi = jnp.where(lo, 0.0, dmixed).astype(BF16)
            dws_ref[2 * p] += jnp.where(tril, _dot(dm_lo, vnb, NT), 0.0)
            dws_ref[2 * p + 1] += jnp.where(tril, _dot(dm_hi, vnb, NT), 0.0)
            dmb = dmixed.astype(BF16)
            dvn = jnp.where(lo, _dot(wst_ref[2 * p], dmb), _dot(wst_ref[2 * p + 1], dmb))
            dgv = rstd * (dvn - _pair_mean_exact(dvn, lo) - vn * _pair_mean_exact(dvn * vn, lo))
            dv_parts.append(dgv * _gelu_grad(v[:, sl]))
            du_parts.append(dyg_p * mixed * _gelu_grad(u[:, sl]))
        duv_ref[0] = jnp.concatenate(du_parts + dv_parts, axis=1).astype(BF16)

        @pl.when(last)
        def _():
            sel = jnp.where(_iota((8, 128), 0) == 0, (_iota((8, 128), 1) < GROUP_DIM).astype(F32),
                            jnp.where(_iota((8, 128), 0) == 1, (_iota((8, 128), 1) >= GROUP_DIM).astype(F32), 0.0))
            for p in range(npair):
                dbs_ref[p] = lax.dot_general(sel, dbacc[p], NT, precision=lax.Precision.HIGHEST,
                                             preferred_element_type=F32)

    duv, dws, dbs, dg = pl.pallas_call(
        body, name=name, grid=(bsz, nc),
        in_specs=[pl.BlockSpec((1, CHUNK, D_GMLP), lambda b, i: (b, i, 0)),
                  pl.BlockSpec((1, CHUNK, D_GMLP), lambda b, i: (b, i, 1)),
                  pl.BlockSpec((1, CHUNK, D_GMLP), lambda b, i: (b, i, 0)),
                  pl.BlockSpec((GROUPS, CHUNK, CHUNK), lambda b, i: (0, 0, 0)),
                  pl.BlockSpec((GROUPS, CHUNK, CHUNK), lambda b, i: (0, 0, 0)),
                  pl.BlockSpec((npair, CHUNK, 128), lambda b, i: (0, 0, 0)),
                  pl.BlockSpec((1, D_GMLP), lambda b, i: (0, 0))],
        out_specs=[pl.BlockSpec((1, CHUNK, 2 * D_GMLP), lambda b, i: (b, i, 0)),
                   pl.BlockSpec((GROUPS, CHUNK, CHUNK), lambda b, i: (0, 0, 0)),
                   pl.BlockSpec((npair, 8, CHUNK), lambda b, i: (0, 0, 0)),
                   pl.BlockSpec((1, D_GMLP), lambda b, i: (0, 0))],
        out_shape=[jax.ShapeDtypeStruct((bsz, seq, 2 * D_GMLP), BF16),
                   jax.ShapeDtypeStruct((GROUPS, CHUNK, CHUNK), F32),
                   jax.ShapeDtypeStruct((npair, 8, CHUNK), F32),
                   jax.ShapeDtypeStruct((1, D_GMLP), F32)],
        scratch_shapes=[pltpu.VMEM((npair, CHUNK, 128), F32)],
        compiler_params=_cparams(),
    )(z3, z3, dyn3, ws, wst, bexp, g_out)
    return duv, dws, dbs[:, :2, :].reshape(GROUPS, CHUNK), dg


def _partner(x):
    width = x.shape[-1]
    lane = _iota(x.shape, x.ndim - 1) % HEAD_PAD
    up = pltpu.roll(x, width - ROPE // 2, x.ndim - 1)
    down = pltpu.roll(x, ROPE // 2, x.ndim - 1)
    first = jnp.logical_and(lane >= NOPE, lane < NOPE + ROPE // 2)
    second = jnp.logical_and(lane >= NOPE + ROPE // 2, lane < NOPE + ROPE)
    return jnp.where(first, up, jnp.where(second, down, 0.0))


def _mla_prep_fwd(z3, g_q, g_kv, w_uq, w_ukv, ctab, stab, *, name, tb=256):
    bsz, seq, _ = z3.shape
    tb = min(tb, seq)
    hw = HEADS * HEAD_PAD

    def body(ql_ref, kvl_ref, krl_ref, gq_ref, gkv_ref, wuq_ref, wukv_ref, c_ref, s_ref, q_ref, kv_ref, kp_ref):
        cq = _rms_fwd(ql_ref[0], gq_ref[...], Q_RANK).astype(BF16)
        q = _dot(cq, wuq_ref[...])
        c1, s1 = c_ref[0], s_ref[0]
        c8, s8 = jnp.tile(c1, (1, HEADS)), jnp.tile(s1, (1, HEADS))
        q_ref[0] = (q * c8 + _partner(q) * s8).astype(BF16)
        ckv = _rms_fwd(kvl_ref[0], gkv_ref[...], KV_RANK).astype(BF16)
        kv = _dot(ckv, wukv_ref[...])
        kv_ref[0] = kv.astype(BF16)
        kr = krl_ref[0]
        kr = kr * c1 + _partner(kr) * s1
        lane = _iota((tb, hw), 1) % HEAD_PAD
        kp_ref[0] = jnp.where(lane < NOPE, kv, jnp.tile(kr, (1, HEADS))).astype(BF16)

    return pl.pallas_call(
        body, name=name, grid=(bsz, seq // tb),
        in_specs=[pl.BlockSpec((1, tb, Q_RANK), lambda b, i: (b, i, 4)),
                  pl.BlockSpec((1, tb, KV_RANK), lambda b, i: (b, i, 10)),
                  pl.BlockSpec((1, tb, HEAD_PAD), lambda b, i: (b, i, 11)),
                  pl.BlockSpec((1, Q_RANK), lambda b, i: (0, 0)),
                  pl.BlockSpec((1, KV_RANK), lambda b, i: (0, 0)),
                  pl.BlockSpec((Q_RANK, hw), lambda b, i: (0, 0)),
                  pl.BlockSpec((KV_RANK, hw), lambda b, i: (0, 0)),
                  pl.BlockSpec((1, tb, HEAD_PAD), lambda b, i: (b, i, 0)),
                  pl.BlockSpec((1, tb, HEAD_PAD), lambda b, i: (b, i, 0))],
        out_specs=[pl.BlockSpec((1, tb, hw), lambda b, i: (b, i, 0))] * 3,
        out_shape=[jax.ShapeDtypeStruct((bsz, seq, hw), BF16)] * 3,
        compiler_params=_cparams(),
    )(z3, z3, z3, g_q, g_kv, w_uq, w_ukv, ctab, stab)


def _mla_prep_bwd(z3, dq3, dk3, dv3, g_q, g_kv, w_uq, w_ukv, ctab, stab, *, name, tb=256):
    bsz, seq, _ = z3.shape
    tb = min(tb, seq)
    hw = HEADS * HEAD_PAD
    nb = seq // tb

    def body(ql_ref, kvl_ref, dq_ref, dk_ref, dv_ref, gq_ref, gkv_ref, wuq_ref, wukv_ref, c_ref, s_ref,
             dz_ref, cq_ref, dqb_ref, ckv_ref, dkvb_ref, dgq_ref, dgkv_ref):
        @pl.when(jnp.logical_and(pl.program_id(0) == 0, pl.program_id(1) == 0))
        def _():
            dgq_ref[...] = jnp.zeros_like(dgq_ref)
            dgkv_ref[...] = jnp.zeros_like(dgkv_ref)

        c1, s1 = c_ref[0], s_ref[0]
        c8, s8 = jnp.tile(c1, (1, HEADS)), jnp.tile(s1, (1, HEADS))
        dqr = dq_ref[0]
        dqb = (dqr * c8 + _partner(dqr * s8)).astype(BF16)
        dqb_ref[0] = dqb
        ql = ql_ref[0]
        cq_ref[0] = _rms_fwd(ql, gq_ref[...], Q_RANK).astype(BF16)
        dql, dgq = _rms_bwd(ql, gq_ref[...], _dot(dqb, wuq_ref[...], NT), Q_RANK)
        dgq_ref[...] += dgq

        dk = dk_ref[0]
        lane = _iota((tb, hw), 1) % HEAD_PAD
        dkvb = jnp.where(lane < NOPE, dk, dv_ref[0]).astype(BF16)
        dkvb_ref[0] = dkvb
        kvl = kvl_ref[0]
        ckv_ref[0] = _rms_fwd(kvl, gkv_ref[...], KV_RANK).astype(BF16)
        dkvl, dgkv = _rms_bwd(kvl, gkv_ref[...], _dot(dkvb, wukv_ref[...], NT), KV_RANK)
        dgkv_ref[...] += dgkv

        dkr = dk[:, 0:HEAD_PAD]
        for h in range(1, HEADS):
            dkr = dkr + dk[:, HEAD_PAD * h:HEAD_PAD * (h + 1)]
        lane1 = _iota((tb, HEAD_PAD), 1)
        dkr = jnp.where(jnp.logical_and(lane1 >= NOPE, lane1 < NOPE + ROPE), dkr, 0.0)
        dkrl = dkr * c1 + _partner(dkr * s1)
        dz_ref[0] = jnp.concatenate([dql, dkvl, dkrl], axis=1).astype(BF16)

    return pl.pallas_call(
        body, name=name, grid=(bsz, nb),
        in_specs=[pl.BlockSpec((1, tb, Q_RANK), lambda b, i: (b, i, 4)),
                  pl.BlockSpec((1, tb, KV_RANK), lambda b, i: (b, i, 10)),
                  pl.BlockSpec((1, tb, hw), lambda b, i: (b, i, 0)),
                  pl.BlockSpec((1, tb, hw), lambda b, i: (b, i, 0)),
                  pl.BlockSpec((1, tb, hw), lambda b, i: (b, i, 0)),
                  pl.BlockSpec((1, Q_RANK), lambda b, i: (0, 0)),
                  pl.BlockSpec((1, KV_RANK), lambda b, i: (0, 0)),
                  pl.BlockSpec((Q_RANK, hw), lambda b, i: (0, 0)),
                  pl.BlockSpec((KV_RANK, hw), lambda b, i: (0, 0)),
                  pl.BlockSpec((1, tb, HEAD_PAD), lambda b, i: (b, i, 0)),
                  pl.BlockSpec((1, tb, HEAD_PAD), lambda b, i: (b, i, 0))],
        out_specs=[pl.BlockSpec((1, tb, 512), lambda b, i: (b, i, 0)),
                   pl.BlockSpec((1, tb, Q_RANK), lambda b, i: (b, i, 0)),
                   pl.BlockSpec((1, tb, hw), lambda b, i: (b, i, 0)),
                   pl.BlockSpec((1, tb, KV_RANK), lambda b, i: (b, i, 0)),
                   pl.BlockSpec((1, tb, hw), lambda b, i: (b, i, 0)),
                   pl.BlockSpec((1, Q_RANK), lambda b, i: (0, 0)),
                   pl.BlockSpec((1, KV_RANK), lambda b, i: (0, 0))],
        out_shape=[jax.ShapeDtypeStruct((bsz, seq, 512), BF16),
                   jax.ShapeDtypeStruct((bsz, seq, Q_RANK), BF16),
                   jax.ShapeDtypeStruct((bsz, seq, hw), BF16),
                   jax.ShapeDtypeStruct((bsz, seq, KV_RANK), BF16),
                   jax.ShapeDtypeStruct((bsz, seq, hw), BF16),
                   jax.ShapeDtypeStruct((1, Q_RANK), F32),
                   jax.ShapeDtypeStruct((1, KV_RANK), F32)],
        compiler_params=_cparams(),
    )(z3, z3, dq3, dk3, dv3, g_q, g_kv, w_uq, w_ukv, ctab, stab)


ATTN_HEADS_PER_STEP = 2


def _attn_specs(tq, seq, hp):
    blk = pl.BlockSpec((1, tq, hp * HEAD_PAD), lambda b, h, i: (b, i, h))
    full = pl.BlockSpec((1, seq, hp * HEAD_PAD), lambda b, h, i: (b, 0, h))
    return blk, full


def _head(h):
    return slice(HEAD_PAD * h, HEAD_PAD * (h + 1))


def _attn_fwd(q3, kv3, kp3, *, name, tq=512, hp=ATTN_HEADS_PER_STEP, side=None):
    bsz, seq, hw = q3.shape
    tq = min(tq, seq)
    blk, full = _attn_specs(tq, seq, hp)

    def body(q_ref, kv_ref, kp_ref, o_ref, lse_ref):
        i = pl.program_id(2)
        is_nope = _iota((tq, HEAD_PAD), 1) < NOPE
        causal = _iota((tq, tq), 1) <= _iota((tq, tq), 0)

        def step(j, carry, diag):
            st = pl.multiple_of(j * tq, tq)
            out = []
            for h in range(hp):
                m, l, acc = carry[h]
                kvj = kv_ref[0, pl.ds(st, tq), _head(h)]
                s = _dot(q_ref[0, :, _head(h)], kp_ref[0, pl.ds(st, tq), _head(h)], NT) * SCALE_LOG2
                if diag:
                    s = jnp.where(causal, s, -1e30)
                m_new = jnp.maximum(m, jnp.max(s, axis=1, keepdims=True))
                alpha = jnp.exp2(m - m_new)
                p = jnp.exp2(s - m_new)
                l = alpha * l + jnp.sum(p, axis=1, keepdims=True)
                acc = alpha * acc + _dot(p.astype(BF16), kvj)
                out.append((m_new, l, acc))
            return tuple(out)

        init = tuple((jnp.full((tq, 1), -1e30, F32), jnp.zeros((tq, 1), F32), jnp.zeros((tq, HEAD_PAD), F32))
                     for _ in range(hp))
        carry = lax.fori_loop(0, i, lambda j, c: step(j, c, False), init)
        carry = step(i, carry, True)
        for h in range(hp):
            m, l, acc = carry[h]
            o_ref[0, :, _head(h)] = jnp.where(is_nope, 0.0, acc / l)
            lse_ref[0, :, _head(h)] = jnp.broadcast_to(m + jnp.log(l) * LOG2E, (tq, HEAD_PAD))

    outs, side_outs = _hosted_call(
        body, name=name, grid=(bsz, HEADS // hp, seq // tq),
        in_specs=[blk, full, full],
        out_specs=[blk, blk],
        out_shape=[jax.ShapeDtypeStruct((bsz, seq, hw), F32), jax.ShapeDtypeStruct((bsz, seq, hw), F32)],
        args=(q3, kv3, kp3), side=side)
    return outs if side is None else (outs, side_outs)


def _attn_bwd_dq(q3, kv3, kp3, do3, lse3, dl3, *, name, tq=512, hp=ATTN_HEADS_PER_STEP):
    bsz, seq, hw = q3.shape
    tq = min(tq, seq)
    blk, full = _attn_specs(tq, seq, hp)
    rep = tq // HEAD_PAD

    def body(q_ref, kv_ref, kp_ref, do_ref, lse_ref, dl_ref, dq_ref):
        i = pl.program_id(2)
        causal = _iota((tq, tq), 1) <= _iota((tq, tq), 0)

        def step(j, carry, diag):
            st = pl.multiple_of(j * tq, tq)
            out = []
            for h in range(hp):
                kp = kp_ref[0, pl.ds(st, tq), _head(h)]
                s = _dot(q_ref[0, :, _head(h)], kp, NT) * SCALE_LOG2
                if diag:
                    s = jnp.where(causal, s, -1e30)
                p = jnp.exp2(s - jnp.tile(lse_ref[0, :, _head(h)], (1, rep)))
                dp = _dot(do_ref[0, :, _head(h)], kv_ref[0, pl.ds(st, tq), _head(h)], NT)
                ds = p * (dp - jnp.tile(dl_ref[0, :, _head(h)], (1, rep)))
                out.append(carry[h] + _dot(ds.astype(BF16), kp))
            return tuple(out)

        init = tuple(jnp.zeros((tq, HEAD_PAD), F32) for _ in range(hp))
        carry = lax.fori_loop(0, i, lambda j, c: step(j, c, False), init)
        carry = step(i, carry, True)
        for h in range(hp):
            dq_ref[0, :, _head(h)] = carry[h] * ATTN_SCALE

    return pl.pallas_call(
        body, name=name, grid=(bsz, HEADS // hp, seq // tq),
        in_specs=[blk, full, full, blk, blk, blk],
        out_specs=blk,
        out_shape=jax.ShapeDtypeStruct((bsz, seq, hw), F32),
        compiler_params=_cparams(),
    )(q3, kv3, kp3, do3, lse3, dl3)


def _attn_bwd_dkv(q3, kv3, kp3, do3, lse3, dl3, *, name, tq=512, hp=ATTN_HEADS_PER_STEP, side=None):
    bsz, seq, hw = q3.shape
    tq = min(tq, seq)
    nq = seq // tq
    blk, full = _attn_specs(tq, seq, hp)
    rep = tq // HEAD_PAD

    def body(kv_ref, kp_ref, q_ref, do_ref, lse_ref, dl_ref, dk_ref, dv_ref):
        j = pl.program_id(2)
        causal = _iota((tq, tq), 1) <= _iota((tq, tq), 0)

        def step(i, carry, diag):
            st = pl.multiple_of(i * tq, tq)
            out = []
            for h in range(hp):
                dk, dv = carry[h]
                qi = q_ref[0, pl.ds(st, tq), _head(h)]
                do = do_ref[0, pl.ds(st, tq), _head(h)]
                s = _dot(qi, kp_ref[0, :, _head(h)], NT) * SCALE_LOG2
                if diag:
                    s = jnp.where(causal, s, -1e30)
                p = jnp.exp2(s - jnp.tile(lse_ref[0, pl.ds(st, tq), _head(h)], (1, rep)))
                dv = dv + _dot(p.astype(BF16), do, TN)
                dp = _dot(do, kv_ref[0, :, _head(h)], NT)
                ds = p * (dp - jnp.tile(dl_ref[0, pl.ds(st, tq), _head(h)], (1, rep)))
                dk = dk + _dot(ds.astype(BF16), qi, TN)
                out.append((dk, dv))
            return tuple(out)

        zero = jnp.zeros((tq, HEAD_PAD), F32)
        carry = step(j, tuple((zero, zero) for _ in range(hp)), True)
        carry = lax.fori_loop(j + 1, nq, lambda i, c: step(i, c, False), carry)
        for h in range(hp):
            dk_ref[0, :, _head(h)] = carry[h][0] * ATTN_SCALE
            dv_ref[0, :, _head(h)] = carry[h][1]

    outs, side_outs = _hosted_call(
        body, name=name, grid=(bsz, HEADS // hp, nq),
        in_specs=[blk, blk, full, full, full, full],
        out_specs=[blk, blk],
        out_shape=[jax.ShapeDtypeStruct((bsz, seq, hw), F32), jax.ShapeDtypeStruct((bsz, seq, hw), F32)],
        args=(kv3, kp3, q3, do3, lse3, dl3), side=side)
    return outs if side is None else (outs, side_outs)


def _onorm_fwd(o3, g_pad, *, name, tb=256):
    bsz, seq, hw = o3.shape
    tb = min(tb, seq)

    def body(o_ref, g_ref, y_ref):
        y_ref[0] = _rms_fwd(o_ref[0], g_ref[...], HEADS * 64).astype(BF16)

    return pl.pallas_call(
        body, name=name, grid=(bsz, seq // tb),
        in_specs=[pl.BlockSpec((1, tb, hw), lambda b, i: (b, i, 0)), pl.BlockSpec((1, hw), lambda b, i: (0, 0))],
        out_specs=pl.BlockSpec((1, tb, hw), lambda b, i: (b, i, 0)),
        out_shape=jax.ShapeDtypeStruct((bsz, seq, hw), BF16),
        compiler_params=_cparams(),
    )(o3, g_pad)


def _onorm_bwd(o3, dy3, g_pad, *, name, tb=256):
    bsz, seq, hw = o3.shape
    tb = min(tb, seq)

    def body(o_ref, dy_ref, g_ref, do_ref, dl_ref, dg_ref):
        @pl.when(jnp.logical_and(pl.program_id(0) == 0, pl.program_id(1) == 0))
        def _():
            dg_ref[...] = jnp.zeros_like(dg_ref)

        o = o_ref[0]
        do, dg = _rms_bwd(o, g_ref[...], dy_ref[0], HEADS * 64)
        dg_ref[...] += dg
        do_ref[0] = do.astype(BF16)
        prod = do * o
        parts = []
        for h in range(HEADS):
            sh = jnp.sum(prod[:, HEAD_PAD * h:HEAD_PAD * (h + 1)], axis=1, keepdims=True)
            parts.append(jnp.broadcast_to(sh, (tb, HEAD_PAD)))
        dl_ref[0] = jnp.concatenate(parts, axis=1)

    return pl.pallas_call(
        body, name=name, grid=(bsz, seq // tb),
        in_specs=[pl.BlockSpec((1, tb, hw), lambda b, i: (b, i, 0)),
                  pl.BlockSpec((1, tb, hw), lambda b, i: (b, i, 0)),
                  pl.BlockSpec((1, hw), lambda b, i: (0, 0))],
        out_specs=[pl.BlockSpec((1, tb, hw), lambda b, i: (b, i, 0)),
                   pl.BlockSpec((1, tb, hw), lambda b, i: (b, i, 0)),
                   pl.BlockSpec((1, hw), lambda b, i: (0, 0))],
        out_shape=[jax.ShapeDtypeStruct((bsz, seq, hw), BF16),
                   jax.ShapeDtypeStruct((bsz, seq, hw), F32),
                   jax.ShapeDtypeStruct((1, hw), F32)],
        compiler_params=_cparams(),
    )(o3, dy3, g_pad)


def _resnode_bwd(x3, g, *, name, target3=None, dh3=None, dres3=None, mod_nm=None, rows=None,
                 branch3=None, mod_gate=None, gate_row=None, tb=256):
    bsz, seq, d = x3.shape
    tb = min(tb, seq)
    final = target3 is not None
    has_branch = branch3 is not None
    row_spec = pl.BlockSpec((1, tb, d), lambda b, i: (b, i, 0))
    vec_spec = pl.BlockSpec((1, d), lambda b, i: (0, 0))
    mod_spec = pl.BlockSpec((1, MOD_ROWS, d), lambda b, i: (b, 0, 0))

    ins, in_specs = [x3, g], [row_spec, vec_spec]
    if final:
        ins += [target3]
        in_specs += [row_spec]
    else:
        ins += [dh3, dres3, mod_nm]
        in_specs += [row_spec, row_spec, mod_spec]
    if has_branch:
        ins += [branch3, mod_gate]
        in_specs += [row_spec, mod_spec]

    out_names = ["dx", "dg"]
    out_specs = [row_spec, vec_spec]
    out_shape = [jax.ShapeDtypeStruct((bsz, seq, d), F32), jax.ShapeDtypeStruct((1, d), F32)]
    if final:
        out_names += ["loss"]
        out_specs += [pl.BlockSpec((1, 128), lambda b, i: (0, 0))]
        out_shape += [jax.ShapeDtypeStruct((1, 128), F32)]
    else:
        out_names += ["dnm"]
        out_specs += [mod_spec]
        out_shape += [jax.ShapeDtypeStruct((bsz, MOD_ROWS, d), F32)]
    if has_branch:
        out_names += ["dbr", "dgate"]
        out_specs += [row_spec, mod_spec]
        out_shape += [jax.ShapeDtypeStruct((bsz, seq, d), BF16), jax.ShapeDtypeStruct((bsz, MOD_ROWS, d), F32)]
    n_in = len(ins)

    def body(*refs):
        r = dict(zip(["x", "g"] + (["t"] if final else ["dh", "dres", "nm"]) + (["br", "gm"] if has_branch else []),
                     refs[:n_in]))
        o = dict(zip(out_names, refs[n_in:]))
        b_first = pl.program_id(1) == 0
        first = jnp.logical_and(pl.program_id(0) == 0, b_first)
        rowid = _iota((MOD_ROWS, d), 0)

        @pl.when(first)
        def _():
            o["dg"][...] = jnp.zeros_like(o["dg"])
            if final:
                o["loss"][...] = jnp.zeros_like(o["loss"])

        @pl.when(b_first)
        def _():
            if not final:
                o["dnm"][...] = jnp.zeros_like(o["dnm"])
            if has_branch:
                o["dgate"][...] = jnp.zeros_like(o["dgate"])

        x = r["x"][0]
        gv = r["g"][...]
        if final:
            e = _rms_fwd(x, gv, d) - r["t"][0]
            sq = jnp.sum(jnp.sum(e * e, axis=1, keepdims=True), axis=0, keepdims=True)
            o["loss"][...] += jnp.broadcast_to(sq * (0.5 / d), (1, 128))
            dx, dg = _rms_bwd(x, gv, e * (1.0 / d), d)
        else:
            m = r["nm"][0]
            dh = r["dh"][0]
            scale = m[rows[1]:rows[1] + 1, :]
            rstd = lax.rsqrt(jnp.sum(x * x, axis=-1, keepdims=True) * (1.0 / d) + EPS)
            xh = x * rstd
            nrm = xh * gv
            dshift = jnp.sum(dh, axis=0, keepdims=True)
            dscale = jnp.sum(dh * nrm, axis=0, keepdims=True)
            o["dnm"][0] += jnp.where(rowid == 0, dshift, jnp.where(rowid == 1, dscale, 0.0))
            dn = dh * (1.0 + scale)
            dg = jnp.sum(dn * xh, axis=0, keepdims=True)
            dxh = dn * gv
            dx = rstd * (dxh - xh * (jnp.sum(dxh * xh, axis=-1, keepdims=True) * (1.0 / d))) + r["dres"][0]
        o["dg"][...] += dg
        o["dx"][0] = dx
        if has_branch:
            gate = r["gm"][0][gate_row:gate_row + 1, :]
            o["dbr"][0] = (gate * dx).astype(BF16)
            dgate = jnp.sum(dx * r["br"][0], axis=0, keepdims=True)
            o["dgate"][0] += jnp.where(rowid == 0, dgate, 0.0)

    outs = pl.pallas_call(
        body, name=name, grid=(bsz, seq // tb),
        in_specs=in_specs, out_specs=out_specs, out_shape=out_shape,
        compiler_params=_cparams(),
    )(*ins)
    return dict(zip(out_names, outs))


def _adamw(w, g, m, v, *, name):
    shape = w.shape
    cols = shape[-1]
    rows = w.size // cols
    tr = _pick_rows(rows, max(8, (256 * 1024) // cols // 8 * 8))

    def body(w_ref, g_ref, m_ref, v_ref, d_ref, nm_ref, nv_ref):
        d_ref[...], nm_ref[...], nv_ref[...] = _adamw_math(w_ref[...], g_ref[...], m_ref[...], v_ref[...])

    spec = pl.BlockSpec((tr, cols), lambda i: (i, 0))
    outs = pl.pallas_call(
        body, name=name, grid=(rows // tr,),
        in_specs=[spec] * 4, out_specs=[spec] * 3,
        out_shape=[jax.ShapeDtypeStruct((rows, cols), F32)] * 3,
        compiler_params=_cparams(),
    )(*[t.reshape(rows, cols) for t in (w, g, m, v)])
    return tuple(o.reshape(shape) for o in outs)


def _adamw_math(w, g, m, v):
    c1 = 1.0 - ADAM_B1 ** ADAM_STEP
    c2 = 1.0 - ADAM_B2 ** ADAM_STEP
    nm = ADAM_B1 * m + (1.0 - ADAM_B1) * g
    nv = ADAM_B2 * v + (1.0 - ADAM_B2) * (g * g)
    delta = -ADAM_LR * ((nm / c1) / (jnp.sqrt(nv / c2) + ADAM_EPS) + ADAM_WD * w)
    return delta, nm, nv


def _adamw_layers(w, m, v, bufs, row_off, *, name, tr=256):
    depth, rows, cols = w.shape
    tr = min(tr, rows)
    assert rows % tr == 0 and row_off % tr == 0

    outs = None
    for l in range(depth):
        def body(w_ref, g_ref, m_ref, v_ref, *rest):
            go_ref, d_ref, nm_ref, nv_ref = rest[-4:]
            g = g_ref[...]
            go_ref[...] = g
            d_ref[...], nm_ref[...], nv_ref[...] = _adamw_math(w_ref[...], g, m_ref[...], v_ref[...])

        layer = pl.BlockSpec((None, tr, cols), lambda i, l=l: (l, i, 0))
        prev = () if outs is None else tuple(outs)
        outs = pl.pallas_call(
            body, name=f"{name}_l{l}", grid=(rows // tr,),
            in_specs=[layer, pl.BlockSpec((tr, cols), lambda i: (row_off // tr + i, 0)), layer, layer]
            + [ANY_SPEC] * len(prev),
            out_specs=[layer] * 4,
            out_shape=[jax.ShapeDtypeStruct(w.shape, F32)] * 4,
            input_output_aliases={4 + k: k for k in range(len(prev))},
            compiler_params=_cparams(),
        )(w, bufs[l], m, v, *prev)
    return tuple(outs)


def _sum_leading(x, *, name, tr=256):
    n, rows, cols = x.shape
    tr = _pick_rows(rows, tr)

    def body(x_ref, o_ref):
        acc = x_ref[0]
        for k in range(1, n):
            acc = acc + x_ref[k]
        o_ref[...] = acc

    return pl.pallas_call(
        body, name=name, grid=(rows // tr,),
        in_specs=[pl.BlockSpec((n, tr, cols), lambda i: (0, i, 0))],
        out_specs=pl.BlockSpec((tr, cols), lambda i: (i, 0)),
        out_shape=jax.ShapeDtypeStruct((rows, cols), F32),
        compiler_params=_cparams(),
    )(x)


def _position():
    return lax.axis_index("x"), lax.axis_index("y"), lax.axis_index("c")


def _allgather8(x, *, name):
    shape = x.shape

    def body(x_ref, out_ref, send_sems, recv_sems, local_sem):
        px, py, pc = _position()
        me, sibling = (px, py, pc), (px, py, 1 - pc)
        chips = [(1 - px, py), (px, 1 - py), (1 - px, 1 - py)]
        src_own = x_ref

        def slot(qx, qy, qc):
            return out_ref.at[4 * qx + 2 * qy + qc]

        def copy(k, block, to, src=None):
            return pltpu.make_async_remote_copy(
                src_ref=slot(*block) if src is None else src, dst_ref=slot(*block),
                send_sem=send_sems.at[k], recv_sem=recv_sems.at[k], device_id=to, device_id_type=MESH)

        mine = pltpu.make_async_copy(src_own, slot(*me), local_sem)
        mine.start()
        first = [copy(0, me, sibling, src=src_own)]
        first += [copy(1 + j, me, (*chip, pc), src=src_own) for j, chip in enumerate(chips)]
        for cp in first:
            cp.start()
        passed = [copy(4 + j, (*chip, pc), sibling) for j, chip in enumerate(chips)]
        for j, chip in enumerate(chips):
            copy(1 + j, (*chip, pc), me).wait_recv()
            passed[j].start()
        copy(0, sibling, me).wait_recv()
        for j, chip in enumerate(chips):
            copy(4 + j, (*chip, 1 - pc), me).wait_recv()
        for cp in first + passed:
            cp.wait_send()
        mine.wait()

    return pl.pallas_call(
        body, name=name,
        out_shape=jax.ShapeDtypeStruct((N_DEV,) + shape, x.dtype),
        in_specs=[pl.BlockSpec(memory_space=pl.ANY)],
        out_specs=pl.BlockSpec(memory_space=pl.ANY),
        scratch_shapes=[pltpu.SemaphoreType.DMA((7,)), pltpu.SemaphoreType.DMA((7,)), pltpu.SemaphoreType.DMA],
    )(x)


class _Exchange:
    def __init__(self, ins, out_shapes, n, build, aliases=None):
        self.ins, self.out_shapes, self.n, self.build = tuple(ins), tuple(out_shapes), n, build
        self.aliases = dict(aliases or {})

    def _descriptors(self, in_refs, out_refs, send_sems, recv_sems):
        sends, recvs = [], []
        for k, (src, dst, peer, landing) in enumerate(self.build(in_refs, out_refs)):
            sends.append(pltpu.make_async_remote_copy(
                src_ref=src, dst_ref=dst, send_sem=send_sems.at[k], recv_sem=recv_sems.at[k],
                device_id=peer, device_id_type=MESH))
            recvs.append(pltpu.make_async_remote_copy(
                src_ref=src, dst_ref=landing, send_sem=send_sems.at[k], recv_sem=recv_sems.at[k],
                device_id=peer, device_id_type=MESH))
        return sends, recvs

    def start(self, *refs):
        for cp in self._descriptors(*refs)[0]:
            cp.start()

    def finish(self, *refs):
        sends, recvs = self._descriptors(*refs)
        for cp in recvs:
            cp.wait_recv()
        for cp in sends:
            cp.wait_send()


ANY_SPEC = pl.BlockSpec(memory_space=pl.ANY)


def _hosted_call(body, *, name, grid, in_specs, out_specs, out_shape, args, scratch_shapes=(), side=None,
                 num_scalar_prefetch=0, io_aliases=None):
    in_specs, out_specs, out_shape = list(in_specs), list(out_specs), list(out_shape)
    n_in, n_out = len(in_specs) + num_scalar_prefetch, len(out_specs)
    kernel_body = body
    aliases = dict(io_aliases or {})
    if side is not None:
        s_in, s_out = len(side.ins), len(side.out_shapes)
        aliases.update({n_in + i: n_out + o for i, o in side.aliases.items()})

        def kernel_body(*refs):
            ins, s_ins = refs[:n_in], refs[n_in:n_in + s_in]
            outs = refs[n_in + s_in:n_in + s_in + n_out]
            s_outs = refs[n_in + s_in + n_out:n_in + s_in + n_out + s_out]
            scratch, sems = refs[n_in + s_in + n_out + s_out:-2], refs[-2:]
            first = functools.reduce(jnp.logical_and, [pl.program_id(a) == 0 for a in range(len(grid))])
            last = functools.reduce(jnp.logical_and, [pl.program_id(a) == g - 1 for a, g in enumerate(grid)])

            @pl.when(first)
            def _():
                side.start(s_ins, s_outs, *sems)

            body(*ins, *outs, *scratch)

            @pl.when(last)
            def _():
                side.finish(s_ins, s_outs, *sems)

        in_specs += [ANY_SPEC] * s_in
        out_specs += [ANY_SPEC] * s_out
        out_shape += list(side.out_shapes)
        scratch_shapes = list(scratch_shapes) + [pltpu.SemaphoreType.DMA((side.n,)),
                                                 pltpu.SemaphoreType.DMA((side.n,))]
        args = tuple(args) + side.ins
    if num_scalar_prefetch:
        grid_spec = pltpu.PrefetchScalarGridSpec(num_scalar_prefetch=num_scalar_prefetch, grid=grid,
                                                 in_specs=in_specs, out_specs=out_specs,
                                                 scratch_shapes=list(scratch_shapes))
        outs = pl.pallas_call(kernel_body, name=name, grid_spec=grid_spec, out_shape=out_shape,
                              input_output_aliases=aliases, compiler_params=_cparams())(*args)
    else:
        outs = pl.pallas_call(kernel_body, name=name, grid=grid, in_specs=in_specs, out_specs=out_specs,
                              out_shape=out_shape, scratch_shapes=list(scratch_shapes),
                              input_output_aliases=aliases, compiler_params=_cparams())(*args)
    return tuple(outs[:n_out]), tuple(outs[n_out:])


def _run_exchange(ex, *, name):
    s_in = len(ex.ins)

    def body(*refs):
        ins, outs, sems = refs[:s_in], refs[s_in:-2], refs[-2:]
        ex.start(ins, outs, *sems)
        ex.finish(ins, outs, *sems)

    outs = pl.pallas_call(
        body, name=name, out_shape=list(ex.out_shapes),
        in_specs=[ANY_SPEC] * s_in, out_specs=[ANY_SPEC] * len(ex.out_shapes),
        scratch_shapes=[pltpu.SemaphoreType.DMA((ex.n,)), pltpu.SemaphoreType.DMA((ex.n,))],
        input_output_aliases=ex.aliases,
    )(*ex.ins)
    return tuple(outs)


def _other_chips(px, py):
    return [(px, 1 - py), (1 - px, py), (1 - px, 1 - py)]


def _gather_spread(w_flat):
    rows, w = w_flat.shape
    hr = rows // 2

    def build(ins, outs):
        px, py, pc = _position()
        mine = ins[0].at[pl.ds(pc * hr, hr)]
        me = 4 * px + 2 * py + pc
        plan = [((px, py, 1 - pc), me ^ 1)]
        plan += [((qx, qy, pc), 4 * qx + 2 * qy + pc) for qx, qy in _other_chips(px, py)]
        return [(mine, outs[0].at[me], peer, outs[0].at[their]) for peer, their in plan]

    return _Exchange([w_flat], [jax.ShapeDtypeStruct((N_DEV, hr, w), w_flat.dtype)], 4, build)


def _gather_pass_on(gath):
    def build(ins, outs):
        px, py, pc = _position()
        out = []
        for qx, qy in _other_chips(px, py):
            blk = 4 * qx + 2 * qy + pc
            out.append((outs[0].at[blk], outs[0].at[blk], (px, py, 1 - pc), outs[0].at[blk ^ 1]))
        return out

    return _Exchange([gath], [jax.ShapeDtypeStruct(gath.shape, gath.dtype)], 3, build, aliases={0: 0})


def _rs_halves(g):
    n, rows, w = g.shape
    hr = rows // 2

    def build(ins, outs):
        px, py, pc = _position()
        return [(ins[0].at[:, pl.ds((1 - pc) * hr, hr), :], outs[0], (px, py, 1 - pc), outs[0])]

    return _Exchange([g], [jax.ShapeDtypeStruct((n, hr, w), g.dtype)], 1, build)


def _rs_chips(sb):
    def build(ins, outs):
        px, py, pc = _position()
        return [(ins[0].at[j], outs[0].at[j], (qx, qy, pc), outs[0].at[j])
                for j, (qx, qy) in enumerate(_other_chips(px, py))]

    return _Exchange([sb], [jax.ShapeDtypeStruct(sb.shape, sb.dtype)], 3, build)


def _rs_complete(buf):
    def build(ins, outs):
        px, py, pc = _position()
        return [(outs[0].at[pc], outs[0].at[pc], (px, py, 1 - pc), outs[0].at[1 - pc])]

    return _Exchange([buf], [jax.ShapeDtypeStruct(buf.shape, buf.dtype)], 1, build, aliases={0: 0})


def _rs_partial(g, recv, ids, *, name, tr=128):
    _, rows, w = g.shape
    hr = rows // 2
    nb = hr // tr

    def body(ids_ref, g_ref, r_ref, o_ref):
        o_ref[0] = (g_ref[0] + r_ref[0]).astype(BF16)

    grid_spec = pltpu.PrefetchScalarGridSpec(
        num_scalar_prefetch=1, grid=(3, nb),
        in_specs=[pl.BlockSpec((1, tr, w), lambda j, i, ids: (ids[1] ^ (j + 1), ids[0] * nb + i, 0)),
                  pl.BlockSpec((1, tr, w), lambda j, i, ids: (ids[1] ^ (j + 1), i, 0))],
        out_specs=pl.BlockSpec((1, tr, w), lambda j, i, ids: (j, i, 0)))
    return pl.pallas_call(
        body, name=name, grid_spec=grid_spec,
        out_shape=jax.ShapeDtypeStruct((3, hr, w), BF16),
        compiler_params=_cparams(),
    )(ids, g, recv)


def _rs_total(g, recv, got, ids, *, name, tr=128):
    _, rows, w = g.shape
    hr = rows // 2
    nb = hr // tr

    def body(ids_ref, g_ref, r_ref, got_ref, o_ref):
        acc = g_ref[0] + r_ref[0]
        for j in range(3):
            acc = acc + got_ref[j].astype(F32)
        o_ref[0] = acc

    grid_spec = pltpu.PrefetchScalarGridSpec(
        num_scalar_prefetch=1, grid=(nb,),
        in_specs=[pl.BlockSpec((1, tr, w), lambda i, ids: (ids[1], ids[0] * nb + i, 0)),
                  pl.BlockSpec((1, tr, w), lambda i, ids: (ids[1], i, 0)),
                  pl.BlockSpec((3, tr, w), lambda i, ids: (0, i, 0))],
        out_specs=pl.BlockSpec((1, tr, w), lambda i, ids: (ids[0], i, 0)))
    return pl.pallas_call(
        body, name=name, grid_spec=grid_spec,
        out_shape=jax.ShapeDtypeStruct((2, hr, w), F32),
        compiler_params=_cparams(),
    )(ids, g, recv, got)


class _ReduceScatter:
    def __init__(self, g, ids, tag):
        self.g, self.ids, self.tag, self.stage, self.result = g, ids, tag, 0, None

    def next_exchange(self):
        if self.stage == 0:
            return _rs_halves(self.g)
        if self.stage == 1:
            return _rs_chips(self.sb)
        return _rs_complete(self.buf)

    def done(self, outs):
        if self.stage == 0:
            self.recv = outs[0]
            self.sb = _rs_partial(self.g, self.recv, self.ids, name=f"{self.tag}_partial")
        elif self.stage == 1:
            self.buf = _rs_total(self.g, self.recv, outs[0], self.ids, name=f"{self.tag}_total")
        else:
            _, hr, w = outs[0].shape
            self.result = outs[0].reshape(2 * hr, w)
        self.stage += 1

    def finish_alone(self):
        names = ("halves", "chips", "complete")
        while self.stage < 3:
            self.done(_run_exchange(self.next_exchange(), name=f"{self.tag}_{names[self.stage]}"))
        return self.result


def _flat_rows():
    used = sum(r for _, r in FSDP_SECTIONS)
    return used, -(-used // ROW_ALIGN) * ROW_ALIGN


def _cols_to_chunks(full):
    rows, cols = full.shape
    t = full.reshape(rows, N_CHIPS, cols // N_CHIPS).transpose(1, 0, 2)
    return t.reshape(N_CHIPS, -1, FLAT_W)


def _chunks_to_cols(chunks, rows, cols):
    return chunks.reshape(N_CHIPS, rows, cols // N_CHIPS).transpose(1, 0, 2).reshape(rows, cols)


def _pad_heads(w, real):
    lead = w.shape[:-1]
    t = w.reshape(lead + (HEADS, real))
    t = jnp.pad(t, [(0, 0)] * len(lead) + [(0, 0), (0, HEAD_PAD - real)])
    return t.reshape(lead + (HEADS * HEAD_PAD,))


def _unpad_heads(w, real):
    lead = w.shape[:-1]
    return w.reshape(lead + (HEADS, HEAD_PAD))[..., :real].reshape(lead + (HEADS * real,))


def _pad_value_lanes(w, axis):
    w = jnp.moveaxis(w, axis, -1)
    lead = w.shape[:-1]
    t = w.reshape(lead + (HEADS, 64))
    t = jnp.pad(t, [(0, 0)] * len(lead) + [(0, 0), (HEAD_PAD - 64, 0)])
    return jnp.moveaxis(t.reshape(lead + (HEADS * HEAD_PAD,)), -1, axis)


def _unpad_value_lanes(w, axis):
    w = jnp.moveaxis(w, axis, -1)
    lead = w.shape[:-1]
    t = w.reshape(lead + (HEADS, HEAD_PAD))[..., HEAD_PAD - 64:]
    return jnp.moveaxis(t.reshape(lead + (HEADS * 64,)), -1, axis)


def _pad_w_in(w):
    z = jnp.zeros((w.shape[0], NOPE), w.dtype)
    z2 = jnp.zeros((w.shape[0], HEAD_PAD - NOPE - ROPE), w.dtype)
    return jnp.concatenate([w[:, :1408], z, w[:, 1408:], z2], axis=1)


def _unpad_w_in(w):
    return jnp.concatenate([w[:, :1408], w[:, 1408 + NOPE:1408 + NOPE + ROPE]], axis=1)


def _rope_tables(positions):
    freqs = ROPE_THETA ** (-jnp.arange(0, ROPE, 2, dtype=F32) / ROPE)
    ang = positions.astype(F32)[..., None] * freqs
    cos, sin = jnp.cos(ang), jnp.sin(ang)
    lead = cos.shape[:-1]
    ones = jnp.ones(lead + (NOPE,), F32)
    zeros_n = jnp.zeros(lead + (NOPE,), F32)
    zeros_p = jnp.zeros(lead + (HEAD_PAD - NOPE - ROPE,), F32)
    ctab = jnp.concatenate([ones, cos, cos, zeros_p], axis=-1)
    stab = jnp.concatenate([zeros_n, -sin, sin, zeros_p], axis=-1)
    return ctab, stab


def _layer_weights(full, p, l):
    ws = p["gmlp_ws"][l]
    tril = jnp.tril(jnp.ones((CHUNK, CHUNK), bool))
    bs = p["gmlp_bs"][l]
    bexp = jnp.repeat(bs.reshape(GROUPS // 2, 2, CHUNK).transpose(0, 2, 1), GROUP_DIM, axis=2)
    return dict(
        w_in=_pad_w_in(full["w_in"]),
        w_uq=_pad_heads(full["mla_w_uq"], NOPE + ROPE),
        w_ukv=full["mla_w_ukv"],
        w_out_a=_pad_value_lanes(full["w_out"][D_GMLP:], 0),
        w_out_g=full["w_out"][:D_GMLP],
        ws=ws,
        wst=jnp.where(tril[None], ws, 0.0).transpose(0, 2, 1).astype(BF16),
        bexp=bexp,
        g_mix=p["norm_mix_g"][l][None],
        g_ffn=p["norm_ffn_g"][l][None],
        g_q=p["mla_q_norm_g"][l][None],
        g_kv=p["mla_kv_norm_g"][l][None],
        g_og=p["out_norm_gmlp_g"][l][None],
        g_oa=_pad_value_lanes(p["out_norm_mla_g"][l], 0)[None],
    )


def _local_step(x3, target3, positions, mods, final_g, plan):
    bsz, seq, d = x3.shape
    tok = bsz * seq
    tmt = min(512, seq)
    tmk = min(1024, seq)
    chunk = (None, None, FLAT_W, FLAT_W)
    ff_grad_shape = (N_CHIPS, 2 * FLAT_W, FLAT_W)
    ctab, stab = _rope_tables(positions)
    lw = [None] * DEPTH

    def flat(t):
        return t.reshape(tok, t.shape[-1])

    def cube(t):
        return t.reshape(bsz, seq, t.shape[-1])

    def carrying(l, tag, fn, *args, **kw):
        side = plan.host(l, tag)
        if side is None:
            return fn(*args, **kw)
        res, side_outs = fn(*args, side=side, **kw)
        plan.hosted(l, tag, side_outs)
        return res

    saved = []
    x = x3
    for l in range(DEPTH):
        lw[l] = plan.layer(l)
        w, mod = lw[l], mods[l]
        h1 = _normmod_fwd(x, w["g_mix"], mod, SHIFT1, SCALE1, name=f"l{l}_normmod1")
        z = cube(_mm(flat(h1), w["w_in"], dims="nn", name=f"l{l}_w_in", tm=tmt, tn=D_IN_PAD, tk=d))
        yg = _gmlp_fwd(z, w["ws"], w["bexp"], w["g_og"], name=f"l{l}_gmlp_fwd")
        q, kv, kp = _mla_prep_fwd(z, w["g_q"], w["g_kv"], w["w_uq"], w["w_ukv"], ctab, stab, name=f"l{l}_mla_prep")
        o, lse = carrying(l, "fwd_attn", _attn_fwd, q, kv, kp, name=f"l{l}_attn_fwd")
        ya = _onorm_fwd(o, w["g_oa"], name=f"l{l}_onorm_fwd")
        pg = _mm(flat(yg), w["w_out_g"], dims="nn", name=f"l{l}_w_out_g", tm=tmt, tn=d, tk=D_GMLP)

        def out_epi(acc, pgv, xv, gm):
            po = acc + pgv
            return po, xv + gm[0][GATE1:GATE1 + 1, :] * po

        po, x_mid = carrying(l, "fwd_out_a", _mm, flat(ya), w["w_out_a"], dims="nn", name=f"l{l}_w_out_a",
                             tm=tmt, tn=d, tk=d, out_dtypes=(F32, F32), epilogue=out_epi,
                             extras=(pg, flat(x), mod), extra_specs=(None, None, _mod_spec(tmt, d, seq)))
        x_mid = cube(x_mid)
        h2 = _normmod_fwd(x_mid, w["g_ffn"], mod, SHIFT2, SCALE2, name=f"l{l}_normmod2")

        def act_epi(acc):
            r = jnp.maximum(acc, 0.0)
            return (r * r,)

        r = carrying(l, "fwd_ff1", _mm, flat(h2), w["ff"], dims="nn", name=f"l{l}_w_ff1", tm=tmt, tn=FLAT_W,
                     tk=d, out_dtypes=(BF16,), epilogue=act_epi, weights_outer=True, n=D_FF,
                     b_block=(chunk, lambda i, j, k: (j, 0, 0, 0)))

        def ff2_epi(acc, xv, gm):
            return acc, xv + gm[0][GATE2:GATE2 + 1, :] * acc

        f, x_out = carrying(l, "fwd_ff2", _mm, r, w["ff"], dims="nn", name=f"l{l}_w_ff2", tm=tmk, tn=d, tk=FLAT_W,
                            out_dtypes=(F32, F32), epilogue=ff2_epi, extras=(flat(x_mid), mod),
                            extra_specs=(None, _mod_spec(tmk, d, seq)), n=d,
                            b_block=(chunk, lambda i, j, k: (k, 1, 0, 0)))
        saved.append(dict(x_in=x, h1=h1, z=z, q=q, kv=kv, kp=kp, o=o, lse=lse, ya=ya, yg=yg, po=cube(po),
                          x_mid=x_mid, h2=h2, r=r, f=cube(f)))
        x = cube(x_out)

    grads = [dict() for _ in range(DEPTH)]
    dmods = [None] * DEPTH
    top = DEPTH - 1
    node = _resnode_bwd(x, final_g[None], name="final_loss_bwd", target3=target3,
                        branch3=saved[top]["f"], mod_gate=mods[top], gate_row=GATE2)
    loss_part = node["loss"][0, 0]
    d_final_g = node["dg"][0]
    for l in range(DEPTH - 1, -1, -1):
        w, mod, s = lw[l], mods[l], saved[l]
        dx_out, dfb, dgate2 = node["dx"], flat(node["dbr"]), node["dgate"][:, 0]

        def dact_epi(acc, rv):
            return (acc * (2.0 * jnp.sqrt(rv.astype(F32))),)

        da = carrying(l, "bwd_d_r", _mm, dfb, w["ff"], dims="nt", name=f"l{l}_d_r", tm=tmt, tn=FLAT_W, tk=d,
                      out_dtypes=(BF16,), epilogue=dact_epi, extras=(s["r"],), weights_outer=True, n=D_FF,
                      b_block=(chunk, lambda i, j, k: (j, 1, 0, 0)))
        g_ff = carrying(l, "bwd_dw_ff2", _mm, s["r"], dfb, dims="tn", name=f"l{l}_dw_ff2", tm=FLAT_W, tn=d,
                        tk=1024, out_into=(ff_grad_shape, (None, FLAT_W, FLAT_W), lambda i, j, k: (i, 1, 0), None))
        g_ff = carrying(l, "bwd_dw_ff1", _mm, flat(s["h2"]), da, dims="tn", name=f"l{l}_dw_ff1", tm=d, tn=FLAT_W,
                        tk=1024, out_into=(ff_grad_shape, (None, FLAT_W, FLAT_W), lambda i, j, k: (j, 0, 0), g_ff))
        plan.ff_grads(l, g_ff)
        dh2 = carrying(l, "bwd_d_h2", _mm, da, w["ff"], dims="nt", name=f"l{l}_d_h2", tm=tmk, tn=d, tk=FLAT_W,
                       n=d, b_block=(chunk, lambda i, j, k: (k, 0, 0, 0)))
        node = _resnode_bwd(s["x_mid"], w["g_ffn"], name=f"l{l}_resnode_ffn", dh3=cube(dh2), dres3=dx_out,
                            mod_nm=mod, rows=(SHIFT2, SCALE2), branch3=s["po"], mod_gate=mod, gate_row=GATE1)
        grads[l]["norm_ffn_g"] = node["dg"][0]
        dshift2, dscale2 = node["dnm"][:, 0], node["dnm"][:, 1]
        dx_mid, dpo, dgate1 = node["dx"], flat(node["dbr"]), node["dgate"][:, 0]

        dya = _mm(dpo, w["w_out_a"], dims="nt", name=f"l{l}_d_ya", tm=tmt, tn=d, tk=d)
        dyg = _mm(dpo, w["w_out_g"], dims="nt", name=f"l{l}_d_yg", tm=tmt, tn=D_GMLP, tk=d)
        dw_out_a = _mm(flat(s["ya"]), dpo, dims="tn", name=f"l{l}_dw_out_a", tm=d, tn=d, tk=1024)
        dw_out_g = _mm(flat(s["yg"]), dpo, dims="tn", name=f"l{l}_dw_out_g", tm=D_GMLP, tn=d, tk=1024)
        grads[l]["w_out"] = jnp.concatenate([dw_out_g, _unpad_value_lanes(dw_out_a, 0)], axis=0)

        duv, dws, dbs, dg_og = _gmlp_bwd(s["z"], cube(dyg), w["ws"], w["wst"], w["bexp"], w["g_og"],
                                         name=f"l{l}_gmlp_bwd")
        grads[l]["gmlp_ws"], grads[l]["gmlp_bs"], grads[l]["out_norm_gmlp_g"] = dws, dbs, dg_og[0]

        do, dl, dg_oa = _onorm_bwd(s["o"], cube(dya), w["g_oa"], name=f"l{l}_onorm_bwd")
        grads[l]["out_norm_mla_g"] = _unpad_value_lanes(dg_oa[0], 0)
        dq = _attn_bwd_dq(s["q"], s["kv"], s["kp"], do, s["lse"], dl, name=f"l{l}_attn_dq")
        dk, dv = carrying(l, "bwd_attn_dkv", _attn_bwd_dkv, s["q"], s["kv"], s["kp"], do, s["lse"], dl,
                          name=f"l{l}_attn_dkv")
        dzm, cq, dqb, ckv, dkvb, dg_q, dg_kv = _mla_prep_bwd(
            s["z"], dq, dk, dv, w["g_q"], w["g_kv"], w["w_uq"], w["w_ukv"], ctab, stab, name=f"l{l}_mla_prep_bwd")
        grads[l]["mla_q_norm_g"], grads[l]["mla_kv_norm_g"] = dg_q[0], dg_kv[0]
        dw_uq = carrying(l, "bwd_dw_uq", _mm, flat(cq), flat(dqb), dims="tn", name=f"l{l}_dw_uq", tm=Q_RANK,
                         tn=1024, tk=1024)
        grads[l]["mla_w_uq"] = _unpad_heads(dw_uq, NOPE + ROPE)
        grads[l]["mla_w_ukv"] = _mm(flat(ckv), flat(dkvb), dims="tn", name=f"l{l}_dw_ukv", tm=KV_RANK, tn=1024, tk=1024)

        h1f = flat(s["h1"])
        dw_in_uv = _mm(h1f, flat(duv), dims="tn", name=f"l{l}_dw_in_uv", tm=d, tn=1024, tk=1024)
        dw_in_m = _mm(h1f, flat(dzm), dims="tn", name=f"l{l}_dw_in_m", tm=d, tn=512, tk=1024)
        grads[l]["w_in"] = _unpad_w_in(jnp.concatenate([dw_in_uv, dw_in_m], axis=1))
        dh1_uv = _mm(flat(duv), w["w_in"][:, :1024], dims="nt", name=f"l{l}_d_h1_uv", tm=tmt, tn=d, tk=1024)
        dh1 = _mm(flat(dzm), w["w_in"][:, 1024:], dims="nt", name=f"l{l}_d_h1", tm=tmt, tn=d, tk=512,
                  epilogue=lambda acc, prev: (acc + prev,), extras=(dh1_uv,))
        if l > 0:
            node = _resnode_bwd(s["x_in"], w["g_mix"], name=f"l{l}_resnode_mix", dh3=cube(dh1), dres3=dx_mid,
                                mod_nm=mod, rows=(SHIFT1, SCALE1), branch3=saved[l - 1]["f"],
                                mod_gate=mods[l - 1], gate_row=GATE2)
        else:
            node = _resnode_bwd(s["x_in"], w["g_mix"], name=f"l{l}_resnode_mix", dh3=cube(dh1), dres3=dx_mid,
                                mod_nm=mod, rows=(SHIFT1, SCALE1))
        grads[l]["norm_mix_g"] = node["dg"][0]
        dshift1, dscale1 = node["dnm"][:, 0], node["dnm"][:, 1]
        dmods[l] = jnp.stack([dshift1, dscale1, dgate1, dshift2, dscale2, dgate2], axis=1)
        plan.layer_grads(l, grads[l])
    return loss_part, node["dx"], d_final_g, dmods


W_NAMES = ("w_ada", "b_ada", "norm_mix_g", "w_in", "gmlp_ws", "gmlp_bs", "mla_q_norm_g", "mla_kv_norm_g",
           "mla_w_uq", "mla_w_ukv", "out_norm_gmlp_g", "out_norm_mla_g", "w_out", "norm_ffn_g", "w_ff1", "w_ff2",
           "final_norm_g")
FLAT_KEY = {"w_in": "w_in", "w_uq": "mla_w_uq", "w_ukv": "mla_w_ukv", "w_out": "w_out", "w_ff1": "w_ff1",
            "w_ff2": "w_ff2"}
COL_SHARDED = ("w_in", "w_uq", "w_ukv", "w_ff1")
FULL_SHAPE = {"w_in": (D_MODEL, D_IN), "w_uq": (Q_RANK, HEADS * (NOPE + ROPE)), "w_ukv": (KV_RANK, HEADS * 128),
              "w_out": (D_MODEL, D_MODEL), "w_ff1": (D_MODEL, D_FF), "w_ff2": (D_FF, D_MODEL)}
SMALL_NAMES = ("norm_mix_g", "gmlp_ws", "gmlp_bs", "mla_q_norm_g", "mla_kv_norm_g", "out_norm_gmlp_g",
               "out_norm_mla_g", "norm_ffn_g", "final_norm_g")


def _silu(v):
    return v * (1.0 / (1.0 + jnp.exp(-v)))


class _CommPlan:
    FWD = {"fwd_attn": ("ff", 0, "spread"), "fwd_out_a": ("ff", 0, "pass"),
           "fwd_ff1": ("mix", 1, "spread"), "fwd_ff2": ("mix", 1, "pass")}
    BWD = {"bwd_d_r": ("mix", 1), "bwd_dw_ff2": ("mix", 1), "bwd_dw_ff1": ("mix", 1),
           "bwd_d_h2": ("ff", 0), "bwd_attn_dkv": ("ff", 0), "bwd_dw_uq": ("ff", 0)}

    def __init__(self, weights, ids, dev, core):
        self.weights, self.ids, self.dev, self.core = weights, ids, dev, core
        self.used, self.rows = _flat_rows()
        self.flat = {("mix", l): self._flat_mix(l) for l in range(DEPTH)}
        self.flat.update({("ff", l): jnp.concatenate([weights["w_ff1"][l], weights["w_ff2"][l]], axis=0).astype(BF16)
                          for l in range(DEPTH)})
        self.lw, self.rs, self.grads, self.spread = {}, {}, {}, {}
        (gath,) = _run_exchange(_gather_spread(self.flat["mix", 0]), name="l0_mix_gather_spread")
        (gath,) = _run_exchange(_gather_pass_on(gath), name="l0_mix_gather_pass_on")
        self._arrived("mix", 0, gath)

    def _flat_mix(self, l):
        pieces = [self.weights[FLAT_KEY[nm]][l].reshape(-1, FLAT_W) for nm, _ in FSDP_SECTIONS]
        pieces.append(jnp.zeros((self.rows - self.used, FLAT_W), F32))
        return jnp.concatenate(pieces, axis=0).astype(BF16)

    def _arrived(self, group, l, gath):
        flat = self.flat[group, l]
        hr = flat.shape[0] // 2
        mine = lax.dynamic_slice(flat, (self.core * hr, 0), (hr, FLAT_W))
        gath = lax.dynamic_update_slice(gath, mine[None], (self.dev, 0, 0))
        if group == "ff":
            self.lw[l]["ff"] = gath.reshape(N_CHIPS, 2, hr, FLAT_W)
            return
        w_gath = gath.reshape(N_CHIPS, self.rows, FLAT_W)
        full, off = {}, 0
        for nm, nrows in FSDP_SECTIONS:
            sec = w_gath[:, off:off + nrows]
            off += nrows
            rows, cols = FULL_SHAPE[nm]
            full[FLAT_KEY[nm]] = _chunks_to_cols(sec, rows, cols) if nm in COL_SHARDED else sec.reshape(rows, cols)
        self.lw[l] = _layer_weights(full, self.weights, l)

    def layer(self, l):
        return self.lw[l]

    def host(self, l, tag):
        if tag in self.FWD:
            group, ahead, what = self.FWD[tag]
            if l + ahead >= DEPTH:
                return None
            return _gather_spread(self.flat[group, l + ahead]) if what == "spread" else _gather_pass_on(self.spread[group])
        group, ahead = self.BWD[tag]
        rs = self.rs.get((group, l + ahead))
        return None if rs is None else rs.next_exchange()

    def hosted(self, l, tag, outs):
        if tag in self.FWD:
            group, ahead, what = self.FWD[tag]
            if what == "spread":
                self.spread[group] = outs[0]
            else:
                self._arrived(group, l + ahead, outs[0])
        else:
            group, ahead = self.BWD[tag]
            self.rs[group, l + ahead].done(outs)

    def ff_grads(self, l, g_ff):
        self.rs["ff", l] = _ReduceScatter(g_ff, self.ids, f"l{l}_ff_rs")

    def layer_grads(self, l, grads):
        self.grads[l] = grads
        pieces = []
        for nm, nrows in FSDP_SECTIONS:
            g = grads[FLAT_KEY[nm]]
            pieces.append(_cols_to_chunks(g) if nm in COL_SHARDED else g.reshape(N_CHIPS, nrows, FLAT_W))
        pieces.append(jnp.zeros((N_CHIPS, self.rows - self.used, FLAT_W), F32))
        self.rs["mix", l] = _ReduceScatter(jnp.concatenate(pieces, axis=1), self.ids, f"l{l}_mix_rs")
        if l == 0:
            self.rs["mix", l].finish_alone()

    def mix_grads(self):
        per = {FLAT_KEY[nm]: [] for nm, _ in FSDP_SECTIONS}
        for l in range(DEPTH):
            shard, off = self.rs["mix", l].result, 0
            for nm, nrows in FSDP_SECTIONS:
                key = FLAT_KEY[nm]
                per[key].append(shard[off:off + nrows].reshape(self.weights[key].shape[1:]))
                off += nrows
        return {key: jnp.stack(parts, axis=0) for key, parts in per.items()}

    def ff_shards(self):
        return [self.rs["ff", l].result for l in range(DEPTH)]


def kernel(x, c, positions, w_ada, b_ada, norm_mix_g, w_in, gmlp_ws, gmlp_bs, mla_q_norm_g, mla_kv_norm_g, mla_w_uq, mla_w_ukv, out_norm_gmlp_g, out_norm_mla_g, w_out, norm_ffn_g, w_ff1, w_ff2, final_norm_g, loss_target, m_w_ada, m_b_ada, m_norm_mix_g, m_w_in, m_gmlp_ws, m_gmlp_bs, m_mla_q_norm_g, m_mla_kv_norm_g, m_mla_w_uq, m_mla_w_ukv, m_out_norm_gmlp_g, m_out_norm_mla_g, m_w_out, m_norm_ffn_g, m_w_ff1, m_w_ff2, m_final_norm_g, v_w_ada, v_b_ada, v_norm_mix_g, v_w_in, v_gmlp_ws, v_gmlp_bs, v_mla_q_norm_g, v_mla_kv_norm_g, v_mla_w_uq, v_mla_w_ukv, v_out_norm_gmlp_g, v_out_norm_mla_g, v_w_out, v_norm_ffn_g, v_w_ff1, v_w_ff2, v_final_norm_g):
    weights = dict(w_ada=w_ada, b_ada=b_ada, norm_mix_g=norm_mix_g, w_in=w_in, gmlp_ws=gmlp_ws, gmlp_bs=gmlp_bs,
                   mla_q_norm_g=mla_q_norm_g, mla_kv_norm_g=mla_kv_norm_g, mla_w_uq=mla_w_uq, mla_w_ukv=mla_w_ukv,
                   out_norm_gmlp_g=out_norm_gmlp_g, out_norm_mla_g=out_norm_mla_g, w_out=w_out,
                   norm_ffn_g=norm_ffn_g, w_ff1=w_ff1, w_ff2=w_ff2, final_norm_g=final_norm_g)
    mom_m = dict(zip(W_NAMES, (m_w_ada, m_b_ada, m_norm_mix_g, m_w_in, m_gmlp_ws, m_gmlp_bs, m_mla_q_norm_g,
                               m_mla_kv_norm_g, m_mla_w_uq, m_mla_w_ukv, m_out_norm_gmlp_g, m_out_norm_mla_g,
                               m_w_out, m_norm_ffn_g, m_w_ff1, m_w_ff2, m_final_norm_g)))
    mom_v = dict(zip(W_NAMES, (v_w_ada, v_b_ada, v_norm_mix_g, v_w_in, v_gmlp_ws, v_gmlp_bs, v_mla_q_norm_g,
                               v_mla_kv_norm_g, v_mla_w_uq, v_mla_w_ukv, v_out_norm_gmlp_g, v_out_norm_mla_g,
                               v_w_out, v_norm_ffn_g, v_w_ff1, v_w_ff2, v_final_norm_g)))
    bsz, seq, d = x.shape
    px, py, pc = _position()
    chip = 2 * px + py
    dev = 2 * chip + pc
    ids = jnp.stack([pc, chip]).astype(jnp.int32)
    n_ex = N_DEV * bsz
    ada_cols = w_ada.shape[-1]

    c_all = _allgather8(c.reshape(bsz * d // 128, 128), name="gather_c").reshape(n_ex, d)
    mod_parts = []
    for l in range(DEPTH):
        bias = lax.dynamic_slice(b_ada[l], (chip * ada_cols,), (ada_cols,))[None]
        mod_parts.append(_mm(c_all, w_ada[l], dims="nn", name=f"l{l}_mod", tm=n_ex, tn=ada_cols, tk=d,
                             epilogue=lambda acc, bv: (acc + bv,), extras=(bias,),
                             extra_specs=(pl.BlockSpec((1, ada_cols), lambda i, j, k: (0, j)),), a_fn=_silu))
    mod_g = _allgather8(jnp.concatenate(mod_parts, axis=0), name="gather_mod")
    mod_g = mod_g.reshape(N_CHIPS, 2, DEPTH, n_ex, ada_cols)[:, 0]
    mod_full = mod_g.transpose(1, 2, 0, 3).reshape(DEPTH, n_ex, N_CHIPS * ada_cols)
    mod_mine = lax.dynamic_slice(mod_full, (0, dev * bsz, 0), (DEPTH, bsz, N_MOD * d))
    mod_mine = jnp.pad(mod_mine.reshape(DEPTH, bsz, N_MOD, d), ((0, 0), (0, 0), (0, MOD_ROWS - N_MOD), (0, 0)))
    mods = [mod_mine[l] for l in range(DEPTH)]

    plan = _CommPlan(weights, ids, dev, pc)
    loss_part, grad_x, d_final_g, dmods = _local_step(x, loss_target, positions, mods, final_norm_g, plan)
    grads = plan.grads
    grad = plan.mix_grads()

    small = {nm: (d_final_g if nm == "final_norm_g" else jnp.stack([grads[l][nm] for l in range(DEPTH)], axis=0))
             for nm in SMALL_NAMES}
    svec = jnp.concatenate([small[nm].reshape(-1) for nm in SMALL_NAMES] + [loss_part[None]])
    n_small = svec.shape[0]
    srows = -(-n_small // (8 * FLAT_W)) * 8
    svec = jnp.pad(svec, (0, srows * FLAT_W - n_small)).reshape(srows, FLAT_W)
    ssum = _sum_leading(_allgather8(svec, name="gather_small_grads"), name="sum_small_grads").reshape(-1)
    off = 0
    for nm in SMALL_NAMES:
        size = weights[nm].size
        grad[nm] = ssum[off:off + size].reshape(weights[nm].shape)
        off += size
    loss = ssum[off]

    dmod = jnp.stack(dmods, axis=1).reshape(bsz * DEPTH * N_MOD, d)
    dmod_all = _allgather8(dmod, name="gather_dmod").reshape(n_ex, DEPTH, N_MOD * d)
    gw, gb = [], []
    for l in range(DEPTH):
        dm = dmod_all[:, l]
        dm_cols = lax.dynamic_slice(dm, (0, chip * ada_cols), (n_ex, ada_cols))
        gw.append(_mm(c_all, dm_cols, dims="tn", name=f"l{l}_dw_ada", tm=d, tn=ada_cols, tk=n_ex, a_fn=_silu))
        gb.append(_sum_leading(dm.reshape(n_ex, N_MOD * d // FLAT_W, FLAT_W), name=f"l{l}_db_ada").reshape(-1))
    grad["w_ada"] = jnp.stack(gw, axis=0)
    grad["b_ada"] = jnp.stack(gb, axis=0)

    delta, new_m, new_v = {}, {}, {}
    ff_bufs = plan.ff_shards()
    for nm, row_off in (("w_ff1", 0), ("w_ff2", FLAT_W)):
        grad[nm], delta[nm], new_m[nm], new_v[nm] = _adamw_layers(
            weights[nm], mom_m[nm], mom_v[nm], ff_bufs, row_off, name=f"adamw_{nm}")
    for nm in W_NAMES:
        if nm not in delta:
            delta[nm], new_m[nm], new_v[nm] = _adamw(weights[nm], grad[nm], mom_m[nm], mom_v[nm],
                                                     name=f"adamw_{nm}")
    return (loss, grad_x, *[grad[nm] for nm in W_NAMES], *[delta[nm] for nm in W_NAMES],
            *[new_m[nm] for nm in W_NAMES], *[new_v[nm] for nm in W_NAMES])
```

```python
import functools
import math

import jax
import jax.numpy as jnp
from jax import lax
from jax.experimental import pallas as pl
from jax.experimental.pallas import tpu as pltpu

F32 = jnp.float32
BF16 = jnp.bfloat16

D_MODEL = 1024
DEPTH = 2
D_GMLP = 512
GROUPS = 8
GROUP_DIM = 64
CHUNK = 128
HEADS = 8
NOPE = 64
ROPE = 32
HEAD_PAD = 128
Q_RANK = 256
KV_RANK = 128
D_FF = 4096
N_MOD = 6
MOD_ROWS = 8
EPS = 1e-6
ROPE_THETA = 10000.0
D_IN = 1440
D_IN_PAD = 1536
ATTN_SCALE = (NOPE + ROPE) ** -0.5
LOG2E = math.log2(math.e)
SCALE_LOG2 = ATTN_SCALE * LOG2E
N_CHIPS = 4
N_DEV = 8

ADAM_LR = 0.001
ADAM_B1 = 0.9
ADAM_B2 = 0.999
ADAM_EPS = 1e-08
ADAM_WD = 0.01
ADAM_STEP = 10

VMEM_LIMIT = 48 * 1024 * 1024
FLAT_W = 1024
ROW_ALIGN = 256

NN = (((1,), (0,)), ((), ()))
NT = (((1,), (1,)), ((), ()))
TN = (((0,), (0,)), ((), ()))
MESH = pl.DeviceIdType.MESH

SHIFT1, SCALE1, GATE1, SHIFT2, SCALE2, GATE2 = range(6)

FSDP_SECTIONS = (("w_out", 256), ("w_in", 360), ("w_uq", 48), ("w_ukv", 32))


def _cparams(vmem=VMEM_LIMIT):
    return pltpu.CompilerParams(vmem_limit_bytes=vmem)


def _dot(a, b, dims=NN):
    return lax.dot_general(a, b, dims, preferred_element_type=F32)


def _iota(shape, axis):
    return lax.broadcasted_iota(jnp.int32, shape, axis)


def _gelu(x):
    k = math.sqrt(2.0 / math.pi)
    return 0.5 * x * (1.0 + jnp.tanh(k * (x + 0.044715 * (x * x * x))))


def _gelu_grad(x):
    k = math.sqrt(2.0 / math.pi)
    t = jnp.tanh(k * (x + 0.044715 * (x * x * x)))
    return 0.5 * (1.0 + t) + 0.5 * x * (1.0 - t * t) * (k * (1.0 + 3.0 * 0.044715 * (x * x)))


def _rms_fwd(x, g, n):
    r = lax.rsqrt(jnp.sum(x * x, axis=-1, keepdims=True) * (1.0 / n) + EPS)
    return x * r * g


def _rms_bwd(x, g, dy, n):
    r = lax.rsqrt(jnp.sum(x * x, axis=-1, keepdims=True) * (1.0 / n) + EPS)
    xh = x * r
    dxh = dy * g
    dx = r * (dxh - xh * (jnp.sum(dxh * xh, axis=-1, keepdims=True) * (1.0 / n)))
    dg = jnp.sum(dy * xh, axis=0, keepdims=True)
    return dx, dg


def _pick_rows(rows, limit):
    if rows <= limit:
        return rows
    for t in range(limit, 7, -8):
        if rows % t == 0:
            return t
    return rows


def _mm(a, b, *, dims, name, tm=512, tn=1024, tk=1024, out_dtypes=(F32,), epilogue=None,
        extras=(), extra_specs=(), a_fn=None, weights_outer=False, side=None, b_block=None, n=None,
        out_into=None):
    if dims == "tn":
        kk, m = a.shape
    else:
        m, kk = a.shape
    if n is None:
        n = b.shape[0] if dims == "nt" else b.shape[1]
    tm, tn, tk = min(tm, m), min(tn, n), min(tk, kk)
    assert m % tm == 0 and n % tn == 0 and kk % tk == 0, (name, a.shape, b.shape, tm, tn, tk)
    ni, nj, nk = m // tm, n // tn, kk // tk

    def spec(shape, pick):
        if weights_outer:
            return pl.BlockSpec(shape, lambda j, i, k: pick(i, j, k))
        return pl.BlockSpec(shape, pick)

    if dims == "tn":
        a_spec = spec((tk, tm), lambda i, j, k: (k, i))
    else:
        a_spec = spec((tm, tk), lambda i, j, k: (i, k))
    if b_block is not None:
        b_spec = spec(*b_block)
    elif dims == "nt":
        b_spec = spec((tn, tk), lambda i, j, k: (j, k))
    else:
        b_spec = spec((tk, tn), lambda i, j, k: (k, j))
    o_spec = spec((tm, tn), lambda i, j, k: (i, j))
    out_shape = [jax.ShapeDtypeStruct((m, n), dt) for dt in out_dtypes]
    out_specs = [o_spec] * len(out_dtypes)
    prev, io_aliases = (), {}
    if out_into is not None:
        full_shape, block, index, before = out_into
        assert len(out_dtypes) == 1 and not extras
        out_shape = [jax.ShapeDtypeStruct(full_shape, out_dtypes[0])]
        out_specs = [spec(block, index)]
        if before is not None:
            prev, io_aliases = (before,), {2: 0}
    assert not (weights_outer and extra_specs)
    dn = {"nn": NN, "nt": NT, "tn": TN}[dims]
    n_ex, n_out = len(extras), len(out_dtypes)
    e_specs = [o_spec if s is None else s for s in (tuple(extra_specs) + (None,) * n_ex)[:n_ex]]

    n_prev = len(prev)

    def body(*refs):
        a_ref, b_ref = refs[0], refs[1]
        e_refs = refs[2 + n_prev:2 + n_prev + n_ex]
        o_refs = refs[2 + n_prev + n_ex:2 + n_prev + n_ex + n_out]
        av = a_ref[...]
        if a_fn is not None:
            av = a_fn(av)
        part = _dot(av.astype(BF16), b_ref[...].astype(BF16), dn)

        def finish(acc):
            outs = (acc,) if epilogue is None else epilogue(acc, *[e[...] for e in e_refs])
            for o_ref, o in zip(o_refs, outs):
                o_ref[...] = o.astype(o_ref.dtype)

        if nk == 1:
            finish(part)
        else:
            acc_ref = refs[-1]
            k = pl.program_id(2)

            @pl.when(k == 0)
            def _():
                acc_ref[...] = part

            @pl.when(k > 0)
            def _():
                acc_ref[...] += part

            @pl.when(k == nk - 1)
            def _():
                finish(acc_ref[...])

    outs, side_outs = _hosted_call(
        body, name=name, grid=(nj, ni, nk) if weights_outer else (ni, nj, nk),
        in_specs=[a_spec, b_spec] + [ANY_SPEC] * n_prev + e_specs,
        out_specs=out_specs, out_shape=out_shape,
        scratch_shapes=[pltpu.VMEM((tm, tn), F32)] if nk > 1 else [],
        args=(a, b, *prev, *extras), side=side, io_aliases=io_aliases)
    res = outs[0] if n_out == 1 else outs
    return res if side is None else (res, side_outs)


def _mod_spec(tm, tn, seq):
    return pl.BlockSpec((1, MOD_ROWS, tn), lambda i, j, k: ((i * tm) // seq, 0, j))


def _normmod_fwd(x3, g, mod, shift_row, scale_row, *, name, tb=256):
    bsz, seq, d = x3.shape
    tb = min(tb, seq)

    def body(x_ref, g_ref, mod_ref, h_ref):
        m = mod_ref[0]
        nrm = _rms_fwd(x_ref[0], g_ref[...], d)
        h = nrm * (1.0 + m[scale_row:scale_row + 1, :]) + m[shift_row:shift_row + 1, :]
        h_ref[0] = h.astype(BF16)

    return pl.pallas_call(
        body, name=name, grid=(bsz, seq // tb),
        in_specs=[pl.BlockSpec((1, tb, d), lambda b, i: (b, i, 0)),
                  pl.BlockSpec((1, d), lambda b, i: (0, 0)),
                  pl.BlockSpec((1, MOD_ROWS, d), lambda b, i: (b, 0, 0))],
        out_specs=pl.BlockSpec((1, tb, d), lambda b, i: (b, i, 0)),
        out_shape=jax.ShapeDtypeStruct((bsz, seq, d), BF16),
        compiler_params=_cparams(),
    )(x3, g, mod)


def _pair_mean_exact(x, lo):
    s_lo = jnp.sum(jnp.where(lo, x, 0.0), axis=-1, keepdims=True)
    s_hi = jnp.sum(jnp.where(lo, 0.0, x), axis=-1, keepdims=True)
    return jnp.where(lo, s_lo, s_hi) * (1.0 / GROUP_DIM)


def _gmlp_pair_fwd(gv_p, w0, w1, bias, lo):
    mu = _pair_mean_exact(gv_p, lo)
    dlt = gv_p - mu
    var = _pair_mean_exact(dlt * dlt, lo)
    rstd = lax.rsqrt(var + EPS)
    vn = dlt * rstd
    vnb = vn.astype(BF16)
    mixed = jnp.where(lo, _dot(w0, vnb), _dot(w1, vnb)) + bias
    return vn, vnb, rstd, mixed


def _tril_bf16(w):
    t = w.shape[-1]
    return jnp.where(_iota((t, t), 1) <= _iota((t, t), 0), w, 0.0).astype(BF16)


def _gmlp_fwd(z3, ws, bexp, g_out, *, name):
    bsz, seq, _ = z3.shape
    nc = seq // CHUNK

    def body(u_ref, v_ref, ws_ref, b_ref, g_ref, y_ref):
        lo = _iota((CHUNK, 128), 1) < GROUP_DIM
        gu = _gelu(u_ref[0])
        gv = _gelu(v_ref[0])
        parts = []
        for p in range(GROUPS // 2):
            sl = slice(128 * p, 128 * p + 128)
            w0 = _tril_bf16(ws_ref[2 * p])
            w1 = _tril_bf16(ws_ref[2 * p + 1])
            _, _, _, mixed = _gmlp_pair_fwd(gv[:, sl], w0, w1, b_ref[p], lo)
            parts.append(gu[:, sl] * mixed)
        yg = jnp.concatenate(parts, axis=1)
        y_ref[0] = _rms_fwd(yg, g_ref[...], D_GMLP).astype(BF16)

    return pl.pallas_call(
        body, name=name, grid=(bsz, nc),
        in_specs=[pl.BlockSpec((1, CHUNK, D_GMLP), lambda b, i: (b, i, 0)),
                  pl.BlockSpec((1, CHUNK, D_GMLP), lambda b, i: (b, i, 1)),
                  pl.BlockSpec((GROUPS, CHUNK, CHUNK), lambda b, i: (0, 0, 0)),
                  pl.BlockSpec((GROUPS // 2, CHUNK, 128), lambda b, i: (0, 0, 0)),
                  pl.BlockSpec((1, D_GMLP), lambda b, i: (0, 0))],
        out_specs=pl.BlockSpec((1, CHUNK, D_GMLP), lambda b, i: (b, i, 0)),
        out_shape=jax.ShapeDtypeStruct((bsz, seq, D_GMLP), BF16),
        compiler_params=_cparams(),
    )(z3, z3, ws, bexp, g_out)


def _gmlp_bwd(z3, dyn3, ws, wst, bexp, g_out, *, name):
    bsz, seq, _ = z3.shape
    nc = seq // CHUNK
    npair = GROUPS // 2

    def body(u_ref, v_ref, dy_ref, ws_ref, wst_ref, b_ref, g_ref, duv_ref, dws_ref, dbs_ref, dg_ref, dbacc):
        first = jnp.logical_and(pl.program_id(0) == 0, pl.program_id(1) == 0)
        last = jnp.logical_and(pl.program_id(0) == bsz - 1, pl.program_id(1) == nc - 1)

        @pl.when(first)
        def _():
            dws_ref[...] = jnp.zeros_like(dws_ref)
            dg_ref[...] = jnp.zeros_like(dg_ref)
            dbacc[...] = jnp.zeros_like(dbacc)

        lo = _iota((CHUNK, 128), 1) < GROUP_DIM
        tril = _iota((CHUNK, CHUNK), 1) <= _iota((CHUNK, CHUNK), 0)
        u = u_ref[0]
        v = v_ref[0]
        gu = _gelu(u)
        gv = _gelu(v)
        fwd = []
        for p in range(npair):
            sl = slice(128 * p, 128 * p + 128)
            w0 = _tril_bf16(ws_ref[2 * p])
            w1 = _tril_bf16(ws_ref[2 * p + 1])
            fwd.append(_gmlp_pair_fwd(gv[:, sl], w0, w1, b_ref[p], lo))
        yg = jnp.concatenate([gu[:, 128 * p:128 * p + 128] * fwd[p][3] for p in range(npair)], axis=1)
        dyg, dg = _rms_bwd(yg, g_ref[...], dy_ref[0], D_GMLP)
        dg_ref[...] += dg
        du_parts, dv_parts = [], []
        for p in range(npair):
            sl = slice(128 * p, 128 * p + 128)
            vn, vnb, rstd, mixed = fwd[p]
            dyg_p = dyg[:, sl]
            dmixed = dyg_p * gu[:, sl]
            dbacc[p] += dmixed
            dm_lo = jnp.where(lo, dmixed, 0.0).astype(BF16)
            dm_hi = jnp.where(lo, 0.0, dmixed).astype(BF16)
            dws_ref[2 * p] += jnp.where(tril, _dot(dm_lo, vnb, NT), 0.0)
            dws_ref[2 * p + 1] += jnp.where(tril, _dot(dm_hi, vnb, NT), 0.0)
            dmb = dmixed.astype(BF16)
            dvn = jnp.where(lo, _dot(wst_ref[2 * p], dmb), _dot(wst_ref[2 * p + 1], dmb))
            dgv = rstd * (dvn - _pair_mean_exact(dvn, lo) - vn * _pair_mean_exact(dvn * vn, lo))
            dv_parts.append(dgv * _gelu_grad(v[:, sl]))
            du_parts.append(dyg_p * mixed * _gelu_grad(u[:, sl]))
        duv_ref[0] = jnp.concatenate(du_parts + dv_parts, axis=1).astype(BF16)

        @pl.when(last)
        def _():
            sel = jnp.where(_iota((8, 128), 0) == 0, (_iota((8, 128), 1) < GROUP_DIM).astype(F32),
                            jnp.where(_iota((8, 128), 0) == 1, (_iota((8, 128), 1) >= GROUP_DIM).astype(F32), 0.0))
            for p in range(npair):
                dbs_ref[p] = lax.dot_general(sel, dbacc[p], NT, precision=lax.Precision.HIGHEST,
                                             preferred_element_type=F32)

    duv, dws, dbs, dg = pl.pallas_call(
        body, name=name, grid=(bsz, nc),
        in_specs=[pl.BlockSpec((1, CHUNK, D_GMLP), lambda b, i: (b, i, 0)),
                  pl.BlockSpec((1, CHUNK, D_GMLP), lambda b, i: (b, i, 1)),
                  pl.BlockSpec((1, CHUNK, D_GMLP), lambda b, i: (b, i, 0)),
                  pl.BlockSpec((GROUPS, CHUNK, CHUNK), lambda b, i: (0, 0, 0)),
                  pl.BlockSpec((GROUPS, CHUNK, CHUNK), lambda b, i: (0, 0, 0)),
                  pl.BlockSpec((npair, CHUNK, 128), lambda b, i: (0, 0, 0)),
                  pl.BlockSpec((1, D_GMLP), lambda b, i: (0, 0))],
        out_specs=[pl.BlockSpec((1, CHUNK, 2 * D_GMLP), lambda b, i: (b, i, 0)),
                   pl.BlockSpec((GROUPS, CHUNK, CHUNK), lambda b, i: (0, 0, 0)),
                   pl.BlockSpec((npair, 8, CHUNK), lambda b, i: (0, 0, 0)),
                   pl.BlockSpec((1, D_GMLP), lambda b, i: (0, 0))],
        out_shape=[jax.ShapeDtypeStruct((bsz, seq, 2 * D_GMLP), BF16),
                   jax.ShapeDtypeStruct((GROUPS, CHUNK, CHUNK), F32),
                   jax.ShapeDtypeStruct((npair, 8, CHUNK), F32),
                   jax.ShapeDtypeStruct((1, D_GMLP), F32)],
        scratch_shapes=[pltpu.VMEM((npair, CHUNK, 128), F32)],
        compiler_params=_cparams(),
    )(z3, z3, dyn3, ws, wst, bexp, g_out)
    return duv, dws, dbs[:, :2, :].reshape(GROUPS, CHUNK), dg


def _partner(x):
    width = x.shape[-1]
    lane = _iota(x.shape, x.ndim - 1) % HEAD_PAD
    up = pltpu.roll(x, width - ROPE // 2, x.ndim - 1)
    down = pltpu.roll(x, ROPE // 2, x.ndim - 1)
    first = jnp.logical_and(lane >= NOPE, lane < NOPE + ROPE // 2)
    second = jnp.logical_and(lane >= NOPE + ROPE // 2, lane < NOPE + ROPE)
    return jnp.where(first, up, jnp.where(second, down, 0.0))


def _mla_prep_fwd(z3, g_q, g_kv, w_uq, w_ukv, ctab, stab, *, name, tb=256):
    bsz, seq, _ = z3.shape
    tb = min(tb, seq)
    hw = HEADS * HEAD_PAD

    def body(ql_ref, kvl_ref, krl_ref, gq_ref, gkv_ref, wuq_ref, wukv_ref, c_ref, s_ref, q_ref, kv_ref, kp_ref):
        cq = _rms_fwd(ql_ref[0], gq_ref[...], Q_RANK).astype(BF16)
        q = _dot(cq, wuq_ref[...])
        c1, s1 = c_ref[0], s_ref[0]
        c8, s8 = jnp.tile(c1, (1, HEADS)), jnp.tile(s1, (1, HEADS))
        q_ref[0] = (q * c8 + _partner(q) * s8).astype(BF16)
        ckv = _rms_fwd(kvl_ref[0], gkv_ref[...], KV_RANK).astype(BF16)
        kv = _dot(ckv, wukv_ref[...])
        kv_ref[0] = kv.astype(BF16)
        kr = krl_ref[0]
        kr = kr * c1 + _partner(kr) * s1
        lane = _iota((tb, hw), 1) % HEAD_PAD
        kp_ref[0] = jnp.where(lane < NOPE, kv, jnp.tile(kr, (1, HEADS))).astype(BF16)

    return pl.pallas_call(
        body, name=name, grid=(bsz, seq // tb),
        in_specs=[pl.BlockSpec((1, tb, Q_RANK), lambda b, i: (b, i, 4)),
                  pl.BlockSpec((1, tb, KV_RANK), lambda b, i: (b, i, 10)),
                  pl.BlockSpec((1, tb, HEAD_PAD), lambda b, i: (b, i, 11)),
                  pl.BlockSpec((1, Q_RANK), lambda b, i: (0, 0)),
                  pl.BlockSpec((1, KV_RANK), lambda b, i: (0, 0)),
                  pl.BlockSpec((Q_RANK, hw), lambda b, i: (0, 0)),
                  pl.BlockSpec((KV_RANK, hw), lambda b, i: (0, 0)),
                  pl.BlockSpec((1, tb, HEAD_PAD), lambda b, i: (b, i, 0)),
                  pl.BlockSpec((1, tb, HEAD_PAD), lambda b, i: (b, i, 0))],
        out_specs=[pl.BlockSpec((1, tb, hw), lambda b, i: (b, i, 0))] * 3,
        out_shape=[jax.ShapeDtypeStruct((bsz, seq, hw), BF16)] * 3,
        compiler_params=_cparams(),
    )(z3, z3, z3, g_q, g_kv, w_uq, w_ukv, ctab, stab)


def _mla_prep_bwd(z3, dq3, dk3, dv3, g_q, g_kv, w_uq, w_ukv, ctab, stab, *, name, tb=256):
    bsz, seq, _ = z3.shape
    tb = min(tb, seq)
    hw = HEADS * HEAD_PAD
    nb = seq // tb

    def body(ql_ref, kvl_ref, dq_ref, dk_ref, dv_ref, gq_ref, gkv_ref, wuq_ref, wukv_ref, c_ref, s_ref,
             dz_ref, cq_ref, dqb_ref, ckv_ref, dkvb_ref, dgq_ref, dgkv_ref):
        @pl.when(jnp.logical_and(pl.program_id(0) == 0, pl.program_id(1) == 0))
        def _():
            dgq_ref[...] = jnp.zeros_like(dgq_ref)
            dgkv_ref[...] = jnp.zeros_like(dgkv_ref)

        c1, s1 = c_ref[0], s_ref[0]
        c8, s8 = jnp.tile(c1, (1, HEADS)), jnp.tile(s1, (1, HEADS))
        dqr = dq_ref[0]
        dqb = (dqr * c8 + _partner(dqr * s8)).astype(BF16)
        dqb_ref[0] = dqb
        ql = ql_ref[0]
        cq_ref[0] = _rms_fwd(ql, gq_ref[...], Q_RANK).astype(BF16)
        dql, dgq = _rms_bwd(ql, gq_ref[...], _dot(dqb, wuq_ref[...], NT), Q_RANK)
        dgq_ref[...] += dgq

        dk = dk_ref[0]
        lane = _iota((tb, hw), 1) % HEAD_PAD
        dkvb = jnp.where(lane < NOPE, dk, dv_ref[0]).astype(BF16)
        dkvb_ref[0] = dkvb
        kvl = kvl_ref[0]
        ckv_ref[0] = _rms_fwd(kvl, gkv_ref[...], KV_RANK).astype(BF16)
        dkvl, dgkv = _rms_bwd(kvl, gkv_ref[...], _dot(dkvb, wukv_ref[...], NT), KV_RANK)
        dgkv_ref[...] += dgkv

        dkr = dk[:, 0:HEAD_PAD].astype(F32)
        for h in range(1, HEADS):
            dkr = dkr + dk[:, HEAD_PAD * h:HEAD_PAD * (h + 1)].astype(F32)
        lane1 = _iota((tb, HEAD_PAD), 1)
        dkr = jnp.where(jnp.logical_and(lane1 >= NOPE, lane1 < NOPE + ROPE), dkr, 0.0)
        dkrl = dkr * c1 + _partner(dkr * s1)
        dz_ref[0] = jnp.concatenate([dql, dkvl, dkrl], axis=1).astype(BF16)

    return pl.pallas_call(
        body, name=name, grid=(bsz, nb),
        in_specs=[pl.BlockSpec((1, tb, Q_RANK), lambda b, i: (b, i, 4)),
                  pl.BlockSpec((1, tb, KV_RANK), lambda b, i: (b, i, 10)),
                  pl.BlockSpec((1, tb, hw), lambda b, i: (b, i, 0)),
                  pl.BlockSpec((1, tb, hw), lambda b, i: (b, i, 0)),
                  pl.BlockSpec((1, tb, hw), lambda b, i: (b, i, 0)),
                  pl.BlockSpec((1, Q_RANK), lambda b, i: (0, 0)),
                  pl.BlockSpec((1, KV_RANK), lambda b, i: (0, 0)),
                  pl.BlockSpec((Q_RANK, hw), lambda b, i: (0, 0)),
                  pl.BlockSpec((KV_RANK, hw), lambda b, i: (0, 0)),
                  pl.BlockSpec((1, tb, HEAD_PAD), lambda b, i: (b, i, 0)),
                  pl.BlockSpec((1, tb, HEAD_PAD), lambda b, i: (b, i, 0))],
        out_specs=[pl.BlockSpec((1, tb, 512), lambda b, i: (b, i, 0)),
                   pl.BlockSpec((1, tb, Q_RANK), lambda b, i: (b, i, 0)),
                   pl.BlockSpec((1, tb, hw), lambda b, i: (b, i, 0)),
                   pl.BlockSpec((1, tb, KV_RANK), lambda b, i: (b, i, 0)),
                   pl.BlockSpec((1, tb, hw), lambda b, i: (b, i, 0)),
                   pl.BlockSpec((1, Q_RANK), lambda b, i: (0, 0)),
                   pl.BlockSpec((1, KV_RANK), lambda b, i: (0, 0))],
        out_shape=[jax.ShapeDtypeStruct((bsz, seq, 512), BF16),
                   jax.ShapeDtypeStruct((bsz, seq, Q_RANK), BF16),
                   jax.ShapeDtypeStruct((bsz, seq, hw), BF16),
                   jax.ShapeDtypeStruct((bsz, seq, KV_RANK), BF16),
                   jax.ShapeDtypeStruct((bsz, seq, hw), BF16),
                   jax.ShapeDtypeStruct((1, Q_RANK), F32),
                   jax.ShapeDtypeStruct((1, KV_RANK), F32)],
        compiler_params=_cparams(),
    )(z3, z3, dq3, dk3, dv3, g_q, g_kv, w_uq, w_ukv, ctab, stab)


ATTN_HEADS_PER_STEP = 2


def _attn_specs(tq, seq, hp):
    blk = pl.BlockSpec((1, tq, hp * HEAD_PAD), lambda b, h, i: (b, i, h))
    full = pl.BlockSpec((1, seq, hp * HEAD_PAD), lambda b, h, i: (b, 0, h))
    return blk, full


def _head(h):
    return slice(HEAD_PAD * h, HEAD_PAD * (h + 1))


def _attn_fwd(q3, kv3, kp3, *, name, tq=512, hp=ATTN_HEADS_PER_STEP, side=None):
    bsz, seq, hw = q3.shape
    tq = min(tq, seq)
    blk, full = _attn_specs(tq, seq, hp)

    def body(q_ref, kv_ref, kp_ref, o_ref, lse_ref):
        i = pl.program_id(2)
        is_nope = _iota((tq, HEAD_PAD), 1) < NOPE
        causal = _iota((tq, tq), 1) <= _iota((tq, tq), 0)

        def step(j, carry, diag):
            st = pl.multiple_of(j * tq, tq)
            out = []
            for h in range(hp):
                m, l, acc = carry[h]
                kvj = kv_ref[0, pl.ds(st, tq), _head(h)]
                s = _dot(q_ref[0, :, _head(h)], kp_ref[0, pl.ds(st, tq), _head(h)], NT) * SCALE_LOG2
                if diag:
                    s = jnp.where(causal, s, -1e30)
                m_new = jnp.maximum(m, jnp.max(s, axis=1, keepdims=True))
                alpha = jnp.exp2(m - m_new)
                p = jnp.exp2(s - m_new)
                l = alpha * l + jnp.sum(p, axis=1, keepdims=True)
                acc = alpha * acc + _dot(p.astype(BF16), kvj)
                out.append((m_new, l, acc))
            return tuple(out)

        init = tuple((jnp.full((tq, 1), -1e30, F32), jnp.zeros((tq, 1), F32), jnp.zeros((tq, HEAD_PAD), F32))
                     for _ in range(hp))
        carry = lax.fori_loop(0, i, lambda j, c: step(j, c, False), init)
        carry = step(i, carry, True)
        for h in range(hp):
            m, l, acc = carry[h]
            o_ref[0, :, _head(h)] = jnp.where(is_nope, 0.0, acc / l)
            lse_ref[0, :, _head(h)] = jnp.broadcast_to(m + jnp.log(l) * LOG2E, (tq, HEAD_PAD))

    outs, side_outs = _hosted_call(
        body, name=name, grid=(bsz, HEADS // hp, seq // tq),
        in_specs=[blk, full, full],
        out_specs=[blk, blk],
        out_shape=[jax.ShapeDtypeStruct((bsz, seq, hw), F32), jax.ShapeDtypeStruct((bsz, seq, hw), F32)],
        args=(q3, kv3, kp3), side=side)
    return outs if side is None else (outs, side_outs)


def _attn_bwd(q3, kv3, kp3, do3, lse3, dl3, *, name, tq=512, hp=ATTN_HEADS_PER_STEP, side=None):
    bsz, seq, hw = q3.shape
    tq = min(tq, seq)
    nq = seq // tq
    blk, full = _attn_specs(tq, seq, hp)
    rep = tq // HEAD_PAD

    def body(kv_ref, kp_ref, q_ref, do_ref, lse_ref, dl_ref, dq_ref, dk_ref, dv_ref):
        j = pl.program_id(2)
        causal = _iota((tq, tq), 1) <= _iota((tq, tq), 0)

        @pl.when(j == 0)
        def _():
            dq_ref[...] = jnp.zeros_like(dq_ref)

        def step(i, carry, diag):
            st = pl.multiple_of(i * tq, tq)
            out = []
            for h in range(hp):
                dk, dv = carry[h]
                qi = q_ref[0, pl.ds(st, tq), _head(h)]
                do = do_ref[0, pl.ds(st, tq), _head(h)]
                kp = kp_ref[0, :, _head(h)]
                s = _dot(qi, kp, NT) * SCALE_LOG2
                if diag:
                    s = jnp.where(causal, s, -1e30)
                p = jnp.exp2(s - jnp.tile(lse_ref[0, pl.ds(st, tq), _head(h)], (1, rep)))
                dv = dv + _dot(p.astype(BF16), do, TN)
                dp = _dot(do, kv_ref[0, :, _head(h)], NT)
                ds = (p * (dp - jnp.tile(dl_ref[0, pl.ds(st, tq), _head(h)], (1, rep)))).astype(BF16)
                dk = dk + _dot(ds, qi, TN)
                dq_ref[0, pl.ds(st, tq), _head(h)] += _dot(ds, kp)
                out.append((dk, dv))
            return tuple(out)

        zero = jnp.zeros((tq, HEAD_PAD), F32)
        carry = step(j, tuple((zero, zero) for _ in range(hp)), True)
        carry = lax.fori_loop(j + 1, nq, lambda i, c: step(i, c, False), carry)
        for h in range(hp):
            dk_ref[0, :, _head(h)] = (carry[h][0] * ATTN_SCALE).astype(BF16)
            dv_ref[0, :, _head(h)] = carry[h][1].astype(BF16)

        @pl.when(j == nq - 1)
        def _():
            dq_ref[...] = dq_ref[...] * ATTN_SCALE

    outs, side_outs = _hosted_call(
        body, name=name, grid=(bsz, HEADS // hp, nq),
        in_specs=[blk, blk, full, full, full, full],
        out_specs=[full, blk, blk],
        out_shape=[jax.ShapeDtypeStruct((bsz, seq, hw), F32)] + [jax.ShapeDtypeStruct((bsz, seq, hw), BF16)] * 2,
        args=(kv3, kp3, q3, do3, lse3, dl3), side=side)
    return outs if side is None else (outs, side_outs)


def _onorm_fwd(o3, g_pad, *, name, tb=256):
    bsz, seq, hw = o3.shape
    tb = min(tb, seq)

    def body(o_ref, g_ref, y_ref):
        y_ref[0] = _rms_fwd(o_ref[0], g_ref[...], HEADS * 64).astype(BF16)

    return pl.pallas_call(
        body, name=name, grid=(bsz, seq // tb),
        in_specs=[pl.BlockSpec((1, tb, hw), lambda b, i: (b, i, 0)), pl.BlockSpec((1, hw), lambda b, i: (0, 0))],
        out_specs=pl.BlockSpec((1, tb, hw), lambda b, i: (b, i, 0)),
        out_shape=jax.ShapeDtypeStruct((bsz, seq, hw), BF16),
        compiler_params=_cparams(),
    )(o3, g_pad)


def _onorm_bwd(o3, dy3, g_pad, *, name, tb=256):
    bsz, seq, hw = o3.shape
    tb = min(tb, seq)

    def body(o_ref, dy_ref, g_ref, do_ref, dl_ref, dg_ref):
        @pl.when(jnp.logical_and(pl.program_id(0) == 0, pl.program_id(1) == 0))
        def _():
            dg_ref[...] = jnp.zeros_like(dg_ref)

        o = o_ref[0]
        do, dg = _rms_bwd(o, g_ref[...], dy_ref[0], HEADS * 64)
        dg_ref[...] += dg
        do_ref[0] = do.astype(BF16)
        prod = do * o
        parts = []
        for h in range(HEADS):
            sh = jnp.sum(prod[:, HEAD_PAD * h:HEAD_PAD * (h + 1)], axis=1, keepdims=True)
            parts.append(jnp.broadcast_to(sh, (tb, HEAD_PAD)))
        dl_ref[0] = jnp.concatenate(parts, axis=1)

    return pl.pallas_call(
        body, name=name, grid=(bsz, seq // tb),
        in_specs=[pl.BlockSpec((1, tb, hw), lambda b, i: (b, i, 0)),
                  pl.BlockSpec((1, tb, hw), lambda b, i: (b, i, 0)),
                  pl.BlockSpec((1, hw), lambda b, i: (0, 0))],
        out_specs=[pl.BlockSpec((1, tb, hw), lambda b, i: (b, i, 0)),
                   pl.BlockSpec((1, tb, hw), lambda b, i: (b, i, 0)),
                   pl.BlockSpec((1, hw), lambda b, i: (0, 0))],
        out_shape=[jax.ShapeDtypeStruct((bsz, seq, hw), BF16),
                   jax.ShapeDtypeStruct((bsz, seq, hw), F32),
                   jax.ShapeDtypeStruct((1, hw), F32)],
        compiler_params=_cparams(),
    )(o3, dy3, g_pad)


def _resnode_bwd(x3, g, *, name, target3=None, dh3=None, dres3=None, mod_nm=None, rows=None,
                 branch3=None, mod_gate=None, gate_row=None, tb=256):
    bsz, seq, d = x3.shape
    tb = min(tb, seq)
    final = target3 is not None
    has_branch = branch3 is not None
    row_spec = pl.BlockSpec((1, tb, d), lambda b, i: (b, i, 0))
    vec_spec = pl.BlockSpec((1, d), lambda b, i: (0, 0))
    mod_spec = pl.BlockSpec((1, MOD_ROWS, d), lambda b, i: (b, 0, 0))

    ins, in_specs = [x3, g], [row_spec, vec_spec]
    if final:
        ins += [target3]
        in_specs += [row_spec]
    else:
        ins += [dh3, dres3, mod_nm]
        in_specs += [row_spec, row_spec, mod_spec]
    if has_branch:
        ins += [branch3, mod_gate]
        in_specs += [row_spec, mod_spec]

    out_names = ["dx", "dg"]
    out_specs = [row_spec, vec_spec]
    out_shape = [jax.ShapeDtypeStruct((bsz, seq, d), F32), jax.ShapeDtypeStruct((1, d), F32)]
    if final:
        out_names += ["loss"]
        out_specs += [pl.BlockSpec((1, 128), lambda b, i: (0, 0))]
        out_shape += [jax.ShapeDtypeStruct((1, 128), F32)]
    else:
        out_names += ["dnm"]
        out_specs += [mod_spec]
        out_shape += [jax.ShapeDtypeStruct((bsz, MOD_ROWS, d), F32)]
    if has_branch:
        out_names += ["dbr", "dgate"]
        out_specs += [row_spec, mod_spec]
        out_shape += [jax.ShapeDtypeStruct((bsz, seq, d), BF16), jax.ShapeDtypeStruct((bsz, MOD_ROWS, d), F32)]
    n_in = len(ins)

    def body(*refs):
        r = dict(zip(["x", "g"] + (["t"] if final else ["dh", "dres", "nm"]) + (["br", "gm"] if has_branch else []),
                     refs[:n_in]))
        o = dict(zip(out_names, refs[n_in:]))
        b_first = pl.program_id(1) == 0
        first = jnp.logical_and(pl.program_id(0) == 0, b_first)
        rowid = _iota((MOD_ROWS, d), 0)

        @pl.when(first)
        def _():
            o["dg"][...] = jnp.zeros_like(o["dg"])
            if final:
                o["loss"][...] = jnp.zeros_like(o["loss"])

        @pl.when(b_first)
        def _():
            if not final:
                o["dnm"][...] = jnp.zeros_like(o["dnm"])
            if has_branch:
                o["dgate"][...] = jnp.zeros_like(o["dgate"])

        x = r["x"][0]
        gv = r["g"][...]
        if final:
            e = _rms_fwd(x, gv, d) - r["t"][0]
            sq = jnp.sum(jnp.sum(e * e, axis=1, keepdims=True), axis=0, keepdims=True)
            o["loss"][...] += jnp.broadcast_to(sq * (0.5 / d), (1, 128))
            dx, dg = _rms_bwd(x, gv, e * (1.0 / d), d)
        else:
            m = r["nm"][0]
            dh = r["dh"][0]
            scale = m[rows[1]:rows[1] + 1, :]
            rstd = lax.rsqrt(jnp.sum(x * x, axis=-1, keepdims=True) * (1.0 / d) + EPS)
            xh = x * rstd
            nrm = xh * gv
            dshift = jnp.sum(dh, axis=0, keepdims=True)
            dscale = jnp.sum(dh * nrm, axis=0, keepdims=True)
            o["dnm"][0] += jnp.where(rowid == 0, dshift, jnp.where(rowid == 1, dscale, 0.0))
            dn = dh * (1.0 + scale)
            dg = jnp.sum(dn * xh, axis=0, keepdims=True)
            dxh = dn * gv
            dx = rstd * (dxh - xh * (jnp.sum(dxh * xh, axis=-1, keepdims=True) * (1.0 / d))) + r["dres"][0]
        o["dg"][...] += dg
        o["dx"][0] = dx
        if has_branch:
            gate = r["gm"][0][gate_row:gate_row + 1, :]
            o["dbr"][0] = (gate * dx).astype(BF16)
            dgate = jnp.sum(dx * r["br"][0], axis=0, keepdims=True)
            o["dgate"][0] += jnp.where(rowid == 0, dgate, 0.0)

    outs = pl.pallas_call(
        body, name=name, grid=(bsz, seq // tb),
        in_specs=in_specs, out_specs=out_specs, out_shape=out_shape,
        compiler_params=_cparams(),
    )(*ins)
    return dict(zip(out_names, outs))


def _adamw(w, g, m, v, *, name):
    shape = w.shape
    cols = shape[-1]
    rows = w.size // cols
    tr = _pick_rows(rows, max(8, (256 * 1024) // cols // 8 * 8))

    def body(w_ref, g_ref, m_ref, v_ref, d_ref, nm_ref, nv_ref):
        d_ref[...], nm_ref[...], nv_ref[...] = _adamw_math(w_ref[...], g_ref[...], m_ref[...], v_ref[...])

    spec = pl.BlockSpec((tr, cols), lambda i: (i, 0))
    outs = pl.pallas_call(
        body, name=name, grid=(rows // tr,),
        in_specs=[spec] * 4, out_specs=[spec] * 3,
        out_shape=[jax.ShapeDtypeStruct((rows, cols), F32)] * 3,
        compiler_params=_cparams(),
    )(*[t.reshape(rows, cols) for t in (w, g, m, v)])
    return tuple(o.reshape(shape) for o in outs)


def _adamw_math(w, g, m, v):
    c1 = 1.0 - ADAM_B1 ** ADAM_STEP
    c2 = 1.0 - ADAM_B2 ** ADAM_STEP
    nm = ADAM_B1 * m + (1.0 - ADAM_B1) * g
    nv = ADAM_B2 * v + (1.0 - ADAM_B2) * (g * g)
    delta = -ADAM_LR * ((nm / c1) / (jnp.sqrt(nv / c2) + ADAM_EPS) + ADAM_WD * w)
    return delta, nm, nv


def _adamw_layers(w, m, v, bufs, row_off, *, name, tr=256):
    depth, rows, cols = w.shape
    tr = min(tr, rows)
    assert rows % tr == 0 and row_off % tr == 0

    outs = None
    for l in range(depth):
        def body(w_ref, g_ref, m_ref, v_ref, *rest):
            go_ref, d_ref, nm_ref, nv_ref = rest[-4:]
            g = g_ref[...]
            go_ref[...] = g
            d_ref[...], nm_ref[...], nv_ref[...] = _adamw_math(w_ref[...], g, m_ref[...], v_ref[...])

        layer = pl.BlockSpec((None, tr, cols), lambda i, l=l: (l, i, 0))
        prev = () if outs is None else tuple(outs)
        outs = pl.pallas_call(
            body, name=f"{name}_l{l}", grid=(rows // tr,),
            in_specs=[layer, pl.BlockSpec((tr, cols), lambda i: (row_off // tr + i, 0)), layer, layer]
            + [ANY_SPEC] * len(prev),
            out_specs=[layer] * 4,
            out_shape=[jax.ShapeDtypeStruct(w.shape, F32)] * 4,
            input_output_aliases={4 + k: k for k in range(len(prev))},
            compiler_params=_cparams(),
        )(w, bufs[l], m, v, *prev)
    return tuple(outs)


def _sum_leading(x, *, name, tr=256):
    n, rows, cols = x.shape
    tr = _pick_rows(rows, tr)

    def body(x_ref, o_ref):
        acc = x_ref[0]
        for k in range(1, n):
            acc = acc + x_ref[k]
        o_ref[...] = acc

    return pl.pallas_call(
        body, name=name, grid=(rows // tr,),
        in_specs=[pl.BlockSpec((n, tr, cols), lambda i: (0, i, 0))],
        out_specs=pl.BlockSpec((tr, cols), lambda i: (i, 0)),
        out_shape=jax.ShapeDtypeStruct((rows, cols), F32),
        compiler_params=_cparams(),
    )(x)


def _position():
    return lax.axis_index("x"), lax.axis_index("y"), lax.axis_index("c")


def _allgather8(x, *, name):
    shape = x.shape

    def body(x_ref, out_ref, send_sems, recv_sems, local_sem):
        px, py, pc = _position()
        me, sibling = (px, py, pc), (px, py, 1 - pc)
        chips = [(1 - px, py), (px, 1 - py), (1 - px, 1 - py)]
        src_own = x_ref

        def slot(qx, qy, qc):
            return out_ref.at[4 * qx + 2 * qy + qc]

        def copy(k, block, to, src=None):
            return pltpu.make_async_remote_copy(
                src_ref=slot(*block) if src is None else src, dst_ref=slot(*block),
                send_sem=send_sems.at[k], recv_sem=recv_sems.at[k], device_id=to, device_id_type=MESH)

        mine = pltpu.make_async_copy(src_own, slot(*me), local_sem)
        mine.start()
        first = [copy(0, me, sibling, src=src_own)]
        first += [copy(1 + j, me, (*chip, pc), src=src_own) for j, chip in enumerate(chips)]
        for cp in first:
            cp.start()
        passed = [copy(4 + j, (*chip, pc), sibling) for j, chip in enumerate(chips)]
        for j, chip in enumerate(chips):
            copy(1 + j, (*chip, pc), me).wait_recv()
            passed[j].start()
        copy(0, sibling, me).wait_recv()
        for j, chip in enumerate(chips):
            copy(4 + j, (*chip, 1 - pc), me).wait_recv()
        for cp in first + passed:
            cp.wait_send()
        mine.wait()

    return pl.pallas_call(
        body, name=name,
        out_shape=jax.ShapeDtypeStruct((N_DEV,) + shape, x.dtype),
        in_specs=[pl.BlockSpec(memory_space=pl.ANY)],
        out_specs=pl.BlockSpec(memory_space=pl.ANY),
        scratch_shapes=[pltpu.SemaphoreType.DMA((7,)), pltpu.SemaphoreType.DMA((7,)), pltpu.SemaphoreType.DMA],
    )(x)


class _Exchange:
    def __init__(self, ins, out_shapes, n, build, aliases=None):
        self.ins, self.out_shapes, self.n, self.build = tuple(ins), tuple(out_shapes), n, build
        self.aliases = dict(aliases or {})

    def _descriptors(self, in_refs, out_refs, send_sems, recv_sems):
        sends, recvs = [], []
        for k, (src, dst, peer, landing) in enumerate(self.build(in_refs, out_refs)):
            sends.append(pltpu.make_async_remote_copy(
                src_ref=src, dst_ref=dst, send_sem=send_sems.at[k], recv_sem=recv_sems.at[k],
                device_id=peer, device_id_type=MESH))
            recvs.append(pltpu.make_async_remote_copy(
                src_ref=src, dst_ref=landing, send_sem=send_sems.at[k], recv_sem=recv_sems.at[k],
                device_id=peer, device_id_type=MESH))
        return sends, recvs

    def start(self, *refs):
        for cp in self._descriptors(*refs)[0]:
            cp.start()

    def finish(self, *refs):
        sends, recvs = self._descriptors(*refs)
        for cp in recvs:
            cp.wait_recv()
        for cp in sends:
            cp.wait_send()


ANY_SPEC = pl.BlockSpec(memory_space=pl.ANY)


def _hosted_call(body, *, name, grid, in_specs, out_specs, out_shape, args, scratch_shapes=(), side=None,
                 num_scalar_prefetch=0, io_aliases=None):
    in_specs, out_specs, out_shape = list(in_specs), list(out_specs), list(out_shape)
    n_in, n_out = len(in_specs) + num_scalar_prefetch, len(out_specs)
    kernel_body = body
    aliases = dict(io_aliases or {})
    if side is not None:
        s_in, s_out = len(side.ins), len(side.out_shapes)
        aliases.update({n_in + i: n_out + o for i, o in side.aliases.items()})

        def kernel_body(*refs):
            ins, s_ins = refs[:n_in], refs[n_in:n_in + s_in]
            outs = refs[n_in + s_in:n_in + s_in + n_out]
            s_outs = refs[n_in + s_in + n_out:n_in + s_in + n_out + s_out]
            scratch, sems = refs[n_in + s_in + n_out + s_out:-2], refs[-2:]
            first = functools.reduce(jnp.logical_and, [pl.program_id(a) == 0 for a in range(len(grid))])
            last = functools.reduce(jnp.logical_and, [pl.program_id(a) == g - 1 for a, g in enumerate(grid)])

            @pl.when(first)
            def _():
                side.start(s_ins, s_outs, *sems)

            body(*ins, *outs, *scratch)

            @pl.when(last)
            def _():
                side.finish(s_ins, s_outs, *sems)

        in_specs += [ANY_SPEC] * s_in
        out_specs += [ANY_SPEC] * s_out
        out_shape += list(side.out_shapes)
        scratch_shapes = list(scratch_shapes) + [pltpu.SemaphoreType.DMA((side.n,)),
                                                 pltpu.SemaphoreType.DMA((side.n,))]
        args = tuple(args) + side.ins
    if num_scalar_prefetch:
        grid_spec = pltpu.PrefetchScalarGridSpec(num_scalar_prefetch=num_scalar_prefetch, grid=grid,
                                                 in_specs=in_specs, out_specs=out_specs,
                                                 scratch_shapes=list(scratch_shapes))
        outs = pl.pallas_call(kernel_body, name=name, grid_spec=grid_spec, out_shape=out_shape,
                              input_output_aliases=aliases, compiler_params=_cparams())(*args)
    else:
        outs = pl.pallas_call(kernel_body, name=name, grid=grid, in_specs=in_specs, out_specs=out_specs,
                              out_shape=out_shape, scratch_shapes=list(scratch_shapes),
                              input_output_aliases=aliases, compiler_params=_cparams())(*args)
    return tuple(outs[:n_out]), tuple(outs[n_out:])


def _run_exchange(ex, *, name):
    s_in = len(ex.ins)

    def body(*refs):
        ins, outs, sems = refs[:s_in], refs[s_in:-2], refs[-2:]
        ex.start(ins, outs, *sems)
        ex.finish(ins, outs, *sems)

    outs = pl.pallas_call(
        body, name=name, out_shape=list(ex.out_shapes),
        in_specs=[ANY_SPEC] * s_in, out_specs=[ANY_SPEC] * len(ex.out_shapes),
        scratch_shapes=[pltpu.SemaphoreType.DMA((ex.n,)), pltpu.SemaphoreType.DMA((ex.n,))],
        input_output_aliases=ex.aliases,
    )(*ex.ins)
    return tuple(outs)


def _other_chips(px, py):
    return [(px, 1 - py), (1 - px, py), (1 - px, 1 - py)]


def _gather_spread(w_flat):
    rows, w = w_flat.shape
    hr = rows // 2

    def build(ins, outs):
        px, py, pc = _position()
        mine = ins[0].at[pl.ds(pc * hr, hr)]
        me = 4 * px + 2 * py + pc
        plan = [((px, py, 1 - pc), me ^ 1)]
        plan += [((qx, qy, pc), 4 * qx + 2 * qy + pc) for qx, qy in _other_chips(px, py)]
        return [(mine, outs[0].at[me], peer, outs[0].at[their]) for peer, their in plan]

    return _Exchange([w_flat], [jax.ShapeDtypeStruct((N_DEV, hr, w), w_flat.dtype)], 4, build)


def _gather_pass_on(gath):
    def build(ins, outs):
        px, py, pc = _position()
        out = []
        for qx, qy in _other_chips(px, py):
            blk = 4 * qx + 2 * qy + pc
            out.append((outs[0].at[blk], outs[0].at[blk], (px, py, 1 - pc), outs[0].at[blk ^ 1]))
        return out

    return _Exchange([gath], [jax.ShapeDtypeStruct(gath.shape, gath.dtype)], 3, build, aliases={0: 0})


def _rs_halves(g):
    n, rows, w = g.shape
    hr = rows // 2

    def build(ins, outs):
        px, py, pc = _position()
        return [(ins[0].at[:, pl.ds((1 - pc) * hr, hr), :], outs[0], (px, py, 1 - pc), outs[0])]

    return _Exchange([g], [jax.ShapeDtypeStruct((n, hr, w), g.dtype)], 1, build)


def _rs_chips(sb):
    def build(ins, outs):
        px, py, pc = _position()
        return [(ins[0].at[j], outs[0].at[j], (qx, qy, pc), outs[0].at[j])
                for j, (qx, qy) in enumerate(_other_chips(px, py))]

    return _Exchange([sb], [jax.ShapeDtypeStruct(sb.shape, sb.dtype)], 3, build)


def _rs_complete(buf):
    def build(ins, outs):
        px, py, pc = _position()
        return [(outs[0].at[pc], outs[0].at[pc], (px, py, 1 - pc), outs[0].at[1 - pc])]

    return _Exchange([buf], [jax.ShapeDtypeStruct(buf.shape, buf.dtype)], 1, build, aliases={0: 0})


def _rs_partial(g, recv, ids, *, name, tr=128):
    _, rows, w = g.shape
    hr = rows // 2
    nb = hr // tr

    def body(ids_ref, g_ref, r_ref, o_ref):
        o_ref[0] = (g_ref[0] + r_ref[0]).astype(BF16)

    grid_spec = pltpu.PrefetchScalarGridSpec(
        num_scalar_prefetch=1, grid=(3, nb),
        in_specs=[pl.BlockSpec((1, tr, w), lambda j, i, ids: (ids[1] ^ (j + 1), ids[0] * nb + i, 0)),
                  pl.BlockSpec((1, tr, w), lambda j, i, ids: (ids[1] ^ (j + 1), i, 0))],
        out_specs=pl.BlockSpec((1, tr, w), lambda j, i, ids: (j, i, 0)))
    return pl.pallas_call(
        body, name=name, grid_spec=grid_spec,
        out_shape=jax.ShapeDtypeStruct((3, hr, w), BF16),
        compiler_params=_cparams(),
    )(ids, g, recv)


def _rs_total(g, recv, got, ids, *, name, tr=128):
    _, rows, w = g.shape
    hr = rows // 2
    nb = hr // tr

    def body(ids_ref, g_ref, r_ref, got_ref, o_ref):
        acc = g_ref[0] + r_ref[0]
        for j in range(3):
            acc = acc + got_ref[j].astype(F32)
        o_ref[0] = acc

    grid_spec = pltpu.PrefetchScalarGridSpec(
        num_scalar_prefetch=1, grid=(nb,),
        in_specs=[pl.BlockSpec((1, tr, w), lambda i, ids: (ids[1], ids[0] * nb + i, 0)),
                  pl.BlockSpec((1, tr, w), lambda i, ids: (ids[1], i, 0)),
                  pl.BlockSpec((3, tr, w), lambda i, ids: (0, i, 0))],
        out_specs=pl.BlockSpec((1, tr, w), lambda i, ids: (ids[0], i, 0)))
    return pl.pallas_call(
        body, name=name, grid_spec=grid_spec,
        out_shape=jax.ShapeDtypeStruct((2, hr, w), F32),
        compiler_params=_cparams(),
    )(ids, g, recv, got)


class _ReduceScatter:
    def __init__(self, g, ids, tag):
        self.g, self.ids, self.tag, self.stage, self.result = g, ids, tag, 0, None

    def next_exchange(self):
        if self.stage == 0:
            return _rs_halves(self.g)
        if self.stage == 1:
            return _rs_chips(self.sb)
        return _rs_complete(self.buf)

    def done(self, outs):
        if self.stage == 0:
            self.recv = outs[0]
            self.sb = _rs_partial(self.g, self.recv, self.ids, name=f"{self.tag}_partial")
        elif self.stage == 1:
            self.buf = _rs_total(self.g, self.recv, outs[0], self.ids, name=f"{self.tag}_total")
        else:
            _, hr, w = outs[0].shape
            self.result = outs[0].reshape(2 * hr, w)
        self.stage += 1

    def finish_alone(self):
        names = ("halves", "chips", "complete")
        while self.stage < 3:
            self.done(_run_exchange(self.next_exchange(), name=f"{self.tag}_{names[self.stage]}"))
        return self.result


def _flat_rows():
    used = sum(r for _, r in FSDP_SECTIONS)
    return used, -(-used // ROW_ALIGN) * ROW_ALIGN


def _cols_to_chunks(full):
    rows, cols = full.shape
    t = full.reshape(rows, N_CHIPS, cols // N_CHIPS).transpose(1, 0, 2)
    return t.reshape(N_CHIPS, -1, FLAT_W)


def _chunks_to_cols(chunks, rows, cols):
    return chunks.reshape(N_CHIPS, rows, cols // N_CHIPS).transpose(1, 0, 2).reshape(rows, cols)


def _pad_heads(w, real):
    lead = w.shape[:-1]
    t = w.reshape(lead + (HEADS, real))
    t = jnp.pad(t, [(0, 0)] * len(lead) + [(0, 0), (0, HEAD_PAD - real)])
    return t.reshape(lead + (HEADS * HEAD_PAD,))


def _unpad_heads(w, real):
    lead = w.shape[:-1]
    return w.reshape(lead + (HEADS, HEAD_PAD))[..., :real].reshape(lead + (HEADS * real,))


def _pad_value_lanes(w, axis):
    w = jnp.moveaxis(w, axis, -1)
    lead = w.shape[:-1]
    t = w.reshape(lead + (HEADS, 64))
    t = jnp.pad(t, [(0, 0)] * len(lead) + [(0, 0), (HEAD_PAD - 64, 0)])
    return jnp.moveaxis(t.reshape(lead + (HEADS * HEAD_PAD,)), -1, axis)


def _unpad_value_lanes(w, axis):
    w = jnp.moveaxis(w, axis, -1)
    lead = w.shape[:-1]
    t = w.reshape(lead + (HEADS, HEAD_PAD))[..., HEAD_PAD - 64:]
    return jnp.moveaxis(t.reshape(lead + (HEADS * 64,)), -1, axis)


def _pad_w_in(w):
    z = jnp.zeros((w.shape[0], NOPE), w.dtype)
    z2 = jnp.zeros((w.shape[0], HEAD_PAD - NOPE - ROPE), w.dtype)
    return jnp.concatenate([w[:, :1408], z, w[:, 1408:], z2], axis=1)


def _unpad_w_in(w):
    return jnp.concatenate([w[:, :1408], w[:, 1408 + NOPE:1408 + NOPE + ROPE]], axis=1)


def _rope_tables(positions):
    freqs = ROPE_THETA ** (-jnp.arange(0, ROPE, 2, dtype=F32) / ROPE)
    ang = positions.astype(F32)[..., None] * freqs
    cos, sin = jnp.cos(ang), jnp.sin(ang)
    lead = cos.shape[:-1]
    ones = jnp.ones(lead + (NOPE,), F32)
    zeros_n = jnp.zeros(lead + (NOPE,), F32)
    zeros_p = jnp.zeros(lead + (HEAD_PAD - NOPE - ROPE,), F32)
    ctab = jnp.concatenate([ones, cos, cos, zeros_p], axis=-1)
    stab = jnp.concatenate([zeros_n, -sin, sin, zeros_p], axis=-1)
    return ctab, stab


def _layer_weights(full, p, l):
    ws = p["gmlp_ws"][l]
    tril = jnp.tril(jnp.ones((CHUNK, CHUNK), bool))
    bs = p["gmlp_bs"][l]
    bexp = jnp.repeat(bs.reshape(GROUPS // 2, 2, CHUNK).transpose(0, 2, 1), GROUP_DIM, axis=2)
    return dict(
        w_in=_pad_w_in(full["w_in"]),
        w_uq=_pad_heads(full["mla_w_uq"], NOPE + ROPE),
        w_ukv=full["mla_w_ukv"],
        w_out_a=_pad_value_lanes(full["w_out"][D_GMLP:], 0),
        w_out_g=full["w_out"][:D_GMLP],
        ws=ws,
        wst=jnp.where(tril[None], ws, 0.0).transpose(0, 2, 1).astype(BF16),
        bexp=bexp,
        g_mix=p["norm_mix_g"][l][None],
        g_ffn=p["norm_ffn_g"][l][None],
        g_q=p["mla_q_norm_g"][l][None],
        g_kv=p["mla_kv_norm_g"][l][None],
        g_og=p["out_norm_gmlp_g"][l][None],
        g_oa=_pad_value_lanes(p["out_norm_mla_g"][l], 0)[None],
    )


def _local_step(x3, target3, positions, mods, final_g, plan):
    bsz, seq, d = x3.shape
    tok = bsz * seq
    tmt = min(512, seq)
    tmk = min(1024, seq)
    chunk = (None, None, FLAT_W, FLAT_W)
    ff_grad_shape = (N_CHIPS, 2 * FLAT_W, FLAT_W)
    ctab, stab = _rope_tables(positions)
    lw = [None] * DEPTH

    def flat(t):
        return t.reshape(tok, t.shape[-1])

    def cube(t):
        return t.reshape(bsz, seq, t.shape[-1])

    def carrying(l, tag, fn, *args, **kw):
        side = plan.host(l, tag)
        if side is None:
            return fn(*args, **kw)
        res, side_outs = fn(*args, side=side, **kw)
        plan.hosted(l, tag, side_outs)
        return res

    saved = []
    x = x3
    for l in range(DEPTH):
        lw[l] = plan.layer(l)
        w, mod = lw[l], mods[l]
        h1 = _normmod_fwd(x, w["g_mix"], mod, SHIFT1, SCALE1, name=f"l{l}_normmod1")
        z = cube(_mm(flat(h1), w["w_in"], dims="nn", name=f"l{l}_w_in", tm=tmt, tn=D_IN_PAD, tk=d))
        yg = _gmlp_fwd(z, w["ws"], w["bexp"], w["g_og"], name=f"l{l}_gmlp_fwd")
        q, kv, kp = _mla_prep_fwd(z, w["g_q"], w["g_kv"], w["w_uq"], w["w_ukv"], ctab, stab, name=f"l{l}_mla_prep")
        o, lse = carrying(l, "fwd_attn", _attn_fwd, q, kv, kp, name=f"l{l}_attn_fwd")
        ya = _onorm_fwd(o, w["g_oa"], name=f"l{l}_onorm_fwd")
        pg = _mm(flat(yg), w["w_out_g"], dims="nn", name=f"l{l}_w_out_g", tm=tmt, tn=d, tk=D_GMLP)

        def out_epi(acc, pgv, xv, gm):
            po = acc + pgv
            return po, xv + gm[0][GATE1:GATE1 + 1, :] * po

        po, x_mid = carrying(l, "fwd_out_a", _mm, flat(ya), w["w_out_a"], dims="nn", name=f"l{l}_w_out_a",
                             tm=tmt, tn=d, tk=d, out_dtypes=(BF16, F32), epilogue=out_epi,
                             extras=(pg, flat(x), mod), extra_specs=(None, None, _mod_spec(tmt, d, seq)))
        x_mid = cube(x_mid)
        h2 = _normmod_fwd(x_mid, w["g_ffn"], mod, SHIFT2, SCALE2, name=f"l{l}_normmod2")

        def act_epi(acc):
            r = jnp.maximum(acc, 0.0)
            return (r * r,)

        r = carrying(l, "fwd_ff1", _mm, flat(h2), w["ff"], dims="nn", name=f"l{l}_w_ff1", tm=tmt, tn=FLAT_W,
                     tk=d, out_dtypes=(BF16,), epilogue=act_epi, weights_outer=True, n=D_FF,
                     b_block=(chunk, lambda i, j, k: (j, 0, 0, 0)))

        def ff2_epi(acc, xv, gm):
            return acc, xv + gm[0][GATE2:GATE2 + 1, :] * acc

        f, x_out = carrying(l, "fwd_ff2", _mm, r, w["ff"], dims="nn", name=f"l{l}_w_ff2", tm=tmk, tn=d, tk=FLAT_W,
                            out_dtypes=(BF16, F32), epilogue=ff2_epi, extras=(flat(x_mid), mod),
                            extra_specs=(None, _mod_spec(tmk, d, seq)), n=d,
                            b_block=(chunk, lambda i, j, k: (k, 1, 0, 0)))
        saved.append(dict(x_in=x, h1=h1, z=z, q=q, kv=kv, kp=kp, o=o, lse=lse, ya=ya, yg=yg, po=cube(po),
                          x_mid=x_mid, h2=h2, r=r, f=cube(f)))
        x = cube(x_out)

    grads = [dict() for _ in range(DEPTH)]
    dmods = [None] * DEPTH
    top = DEPTH - 1
    node = _resnode_bwd(x, final_g[None], name="final_loss_bwd", target3=target3,
                        branch3=saved[top]["f"], mod_gate=mods[top], gate_row=GATE2)
    loss_part = node["loss"][0, 0]
    d_final_g = node["dg"][0]
    for l in range(DEPTH - 1, -1, -1):
        w, mod, s = lw[l], mods[l], saved[l]
        dx_out, dfb, dgate2 = node["dx"], flat(node["dbr"]), node["dgate"][:, 0]

        def dact_epi(acc, rv):
            return (acc * (2.0 * jnp.sqrt(rv.astype(F32))),)

        da = carrying(l, "bwd_d_r", _mm, dfb, w["ff"], dims="nt", name=f"l{l}_d_r", tm=tmt, tn=FLAT_W, tk=d,
                      out_dtypes=(BF16,), epilogue=dact_epi, extras=(s["r"],), weights_outer=True, n=D_FF,
                      b_block=(chunk, lambda i, j, k: (j, 1, 0, 0)))
        g_ff = carrying(l, "bwd_dw_ff2", _mm, s["r"], dfb, dims="tn", name=f"l{l}_dw_ff2", tm=FLAT_W, tn=d,
                        tk=1024, out_into=(ff_grad_shape, (None, FLAT_W, FLAT_W), lambda i, j, k: (i, 1, 0), None))
        g_ff = carrying(l, "bwd_dw_ff1", _mm, flat(s["h2"]), da, dims="tn", name=f"l{l}_dw_ff1", tm=d, tn=FLAT_W,
                        tk=1024, out_into=(ff_grad_shape, (None, FLAT_W, FLAT_W), lambda i, j, k: (j, 0, 0), g_ff))
        plan.ff_grads(l, g_ff)
        dh2 = carrying(l, "bwd_d_h2", _mm, da, w["ff"], dims="nt", name=f"l{l}_d_h2", tm=tmk, tn=d, tk=FLAT_W,
                       n=d, b_block=(chunk, lambda i, j, k: (k, 0, 0, 0)))
        node = _resnode_bwd(s["x_mid"], w["g_ffn"], name=f"l{l}_resnode_ffn", dh3=cube(dh2), dres3=dx_out,
                            mod_nm=mod, rows=(SHIFT2, SCALE2), branch3=s["po"], mod_gate=mod, gate_row=GATE1)
        grads[l]["norm_ffn_g"] = node["dg"][0]
        dshift2, dscale2 = node["dnm"][:, 0], node["dnm"][:, 1]
        dx_mid, dpo, dgate1 = node["dx"], flat(node["dbr"]), node["dgate"][:, 0]

        dya = _mm(dpo, w["w_out_a"], dims="nt", name=f"l{l}_d_ya", tm=tmt, tn=d, tk=d)
        dyg = _mm(dpo, w["w_out_g"], dims="nt", name=f"l{l}_d_yg", tm=tmt, tn=D_GMLP, tk=d)
        dw_out_a = _mm(flat(s["ya"]), dpo, dims="tn", name=f"l{l}_dw_out_a", tm=d, tn=d, tk=1024)
        dw_out_g = _mm(flat(s["yg"]), dpo, dims="tn", name=f"l{l}_dw_out_g", tm=D_GMLP, tn=d, tk=1024)
        grads[l]["w_out"] = jnp.concatenate([dw_out_g, _unpad_value_lanes(dw_out_a, 0)], axis=0)

        duv, dws, dbs, dg_og = _gmlp_bwd(s["z"], cube(dyg), w["ws"], w["wst"], w["bexp"], w["g_og"],
                                         name=f"l{l}_gmlp_bwd")
        grads[l]["gmlp_ws"], grads[l]["gmlp_bs"], grads[l]["out_norm_gmlp_g"] = dws, dbs, dg_og[0]

        do, dl, dg_oa = _onorm_bwd(s["o"], cube(dya), w["g_oa"], name=f"l{l}_onorm_bwd")
        grads[l]["out_norm_mla_g"] = _unpad_value_lanes(dg_oa[0], 0)
        dq, dk, dv = carrying(l, "bwd_attn_dkv", _attn_bwd, s["q"], s["kv"], s["kp"], do, s["lse"], dl,
                              name=f"l{l}_attn_bwd")
        dzm, cq, dqb, ckv, dkvb, dg_q, dg_kv = _mla_prep_bwd(
            s["z"], dq, dk, dv, w["g_q"], w["g_kv"], w["w_uq"], w["w_ukv"], ctab, stab, name=f"l{l}_mla_prep_bwd")
        grads[l]["mla_q_norm_g"], grads[l]["mla_kv_norm_g"] = dg_q[0], dg_kv[0]
        dw_uq = carrying(l, "bwd_dw_uq", _mm, flat(cq), flat(dqb), dims="tn", name=f"l{l}_dw_uq", tm=Q_RANK,
                         tn=1024, tk=1024)
        grads[l]["mla_w_uq"] = _unpad_heads(dw_uq, NOPE + ROPE)
        grads[l]["mla_w_ukv"] = _mm(flat(ckv), flat(dkvb), dims="tn", name=f"l{l}_dw_ukv", tm=KV_RANK, tn=1024, tk=1024)

        h1f = flat(s["h1"])
        dw_in_uv = _mm(h1f, flat(duv), dims="tn", name=f"l{l}_dw_in_uv", tm=d, tn=1024, tk=1024)
        dw_in_m = _mm(h1f, flat(dzm), dims="tn", name=f"l{l}_dw_in_m", tm=d, tn=512, tk=1024)
        grads[l]["w_in"] = _unpad_w_in(jnp.concatenate([dw_in_uv, dw_in_m], axis=1))
        dh1_uv = _mm(flat(duv), w["w_in"][:, :1024], dims="nt", name=f"l{l}_d_h1_uv", tm=tmt, tn=d, tk=1024)
        dh1 = _mm(flat(dzm), w["w_in"][:, 1024:], dims="nt", name=f"l{l}_d_h1", tm=tmt, tn=d, tk=512,
                  epilogue=lambda acc, prev: (acc + prev,), extras=(dh1_uv,))
        if l > 0:
            node = _resnode_bwd(s["x_in"], w["g_mix"], name=f"l{l}_resnode_mix", dh3=cube(dh1), dres3=dx_mid,
                                mod_nm=mod, rows=(SHIFT1, SCALE1), branch3=saved[l - 1]["f"],
                                mod_gate=mods[l - 1], gate_row=GATE2)
        else:
            node = _resnode_bwd(s["x_in"], w["g_mix"], name=f"l{l}_resnode_mix", dh3=cube(dh1), dres3=dx_mid,
                                mod_nm=mod, rows=(SHIFT1, SCALE1))
        grads[l]["norm_mix_g"] = node["dg"][0]
        dshift1, dscale1 = node["dnm"][:, 0], node["dnm"][:, 1]
        dmods[l] = jnp.stack([dshift1, dscale1, dgate1, dshift2, dscale2, dgate2], axis=1)
        plan.layer_grads(l, grads[l])
    return loss_part, node["dx"], d_final_g, dmods


W_NAMES = ("w_ada", "b_ada", "norm_mix_g", "w_in", "gmlp_ws", "gmlp_bs", "mla_q_norm_g", "mla_kv_norm_g",
           "mla_w_uq", "mla_w_ukv", "out_norm_gmlp_g", "out_norm_mla_g", "w_out", "norm_ffn_g", "w_ff1", "w_ff2",
           "final_norm_g")
FLAT_KEY = {"w_in": "w_in", "w_uq": "mla_w_uq", "w_ukv": "mla_w_ukv", "w_out": "w_out", "w_ff1": "w_ff1",
            "w_ff2": "w_ff2"}
COL_SHARDED = ("w_in", "w_uq", "w_ukv", "w_ff1")
FULL_SHAPE = {"w_in": (D_MODEL, D_IN), "w_uq": (Q_RANK, HEADS * (NOPE + ROPE)), "w_ukv": (KV_RANK, HEADS * 128),
              "w_out": (D_MODEL, D_MODEL), "w_ff1": (D_MODEL, D_FF), "w_ff2": (D_FF, D_MODEL)}
SMALL_NAMES = ("norm_mix_g", "gmlp_ws", "gmlp_bs", "mla_q_norm_g", "mla_kv_norm_g", "out_norm_gmlp_g",
               "out_norm_mla_g", "norm_ffn_g", "final_norm_g")


def _silu(v):
    return v * (1.0 / (1.0 + jnp.exp(-v)))


class _CommPlan:
    FWD = {"fwd_attn": ("ff", 0, "spread"), "fwd_out_a": ("ff", 0, "pass"),
           "fwd_ff1": ("mix", 1, "spread"), "fwd_ff2": ("mix", 1, "pass")}
    BWD = {"bwd_d_r": ("mix", 1), "bwd_dw_ff2": ("mix", 1), "bwd_dw_ff1": ("mix", 1),
           "bwd_d_h2": ("ff", 0), "bwd_attn_dkv": ("ff", 0), "bwd_dw_uq": ("ff", 0)}

    def __init__(self, weights, ids, dev, core):
        self.weights, self.ids, self.dev, self.core = weights, ids, dev, core
        self.used, self.rows = _flat_rows()
        self.flat = {("mix", l): self._flat_mix(l) for l in range(DEPTH)}
        self.flat.update({("ff", l): jnp.concatenate([weights["w_ff1"][l], weights["w_ff2"][l]], axis=0).astype(BF16)
                          for l in range(DEPTH)})
        self.lw, self.rs, self.grads, self.spread = {}, {}, {}, {}
        (gath,) = _run_exchange(_gather_spread(self.flat["mix", 0]), name="l0_mix_gather_spread")
        (gath,) = _run_exchange(_gather_pass_on(gath), name="l0_mix_gather_pass_on")
        self._arrived("mix", 0, gath)

    def _flat_mix(self, l):
        pieces = [self.weights[FLAT_KEY[nm]][l].reshape(-1, FLAT_W) for nm, _ in FSDP_SECTIONS]
        pieces.append(jnp.zeros((self.rows - self.used, FLAT_W), F32))
        return jnp.concatenate(pieces, axis=0).astype(BF16)

    def _arrived(self, group, l, gath):
        flat = self.flat[group, l]
        hr = flat.shape[0] // 2
        mine = lax.dynamic_slice(flat, (self.core * hr, 0), (hr, FLAT_W))
        gath = lax.dynamic_update_slice(gath, mine[None], (self.dev, 0, 0))
        if group == "ff":
            self.lw[l]["ff"] = gath.reshape(N_CHIPS, 2, hr, FLAT_W)
            return
        w_gath = gath.reshape(N_CHIPS, self.rows, FLAT_W)
        full, off = {}, 0
        for nm, nrows in FSDP_SECTIONS:
            sec = w_gath[:, off:off + nrows]
            off += nrows
            rows, cols = FULL_SHAPE[nm]
            full[FLAT_KEY[nm]] = _chunks_to_cols(sec, rows, cols) if nm in COL_SHARDED else sec.reshape(rows, cols)
        self.lw[l] = _layer_weights(full, self.weights, l)

    def layer(self, l):
        return self.lw[l]

    def host(self, l, tag):
        if tag in self.FWD:
            group, ahead, what = self.FWD[tag]
            if l + ahead >= DEPTH:
                return None
            return _gather_spread(self.flat[group, l + ahead]) if what == "spread" else _gather_pass_on(self.spread[group])
        group, ahead = self.BWD[tag]
        rs = self.rs.get((group, l + ahead))
        return None if rs is None else rs.next_exchange()

    def hosted(self, l, tag, outs):
        if tag in self.FWD:
            group, ahead, what = self.FWD[tag]
            if what == "spread":
                self.spread[group] = outs[0]
            else:
                self._arrived(group, l + ahead, outs[0])
        else:
            group, ahead = self.BWD[tag]
            self.rs[group, l + ahead].done(outs)

    def ff_grads(self, l, g_ff):
        self.rs["ff", l] = _ReduceScatter(g_ff, self.ids, f"l{l}_ff_rs")

    def layer_grads(self, l, grads):
        self.grads[l] = grads
        pieces = []
        for nm, nrows in FSDP_SECTIONS:
            g = grads[FLAT_KEY[nm]]
            pieces.append(_cols_to_chunks(g) if nm in COL_SHARDED else g.reshape(N_CHIPS, nrows, FLAT_W))
        pieces.append(jnp.zeros((N_CHIPS, self.rows - self.used, FLAT_W), F32))
        self.rs["mix", l] = _ReduceScatter(jnp.concatenate(pieces, axis=1), self.ids, f"l{l}_mix_rs")
        if l == 0:
            self.rs["mix", l].finish_alone()

    def mix_grads(self):
        per = {FLAT_KEY[nm]: [] for nm, _ in FSDP_SECTIONS}
        for l in range(DEPTH):
            shard, off = self.rs["mix", l].result, 0
            for nm, nrows in FSDP_SECTIONS:
                key = FLAT_KEY[nm]
                per[key].append(shard[off:off + nrows].reshape(self.weights[key].shape[1:]))
                off += nrows
        return {key: jnp.stack(parts, axis=0) for key, parts in per.items()}

    def ff_shards(self):
        return [self.rs["ff", l].result for l in range(DEPTH)]


def kernel(x, c, positions, w_ada, b_ada, norm_mix_g, w_in, gmlp_ws, gmlp_bs, mla_q_norm_g, mla_kv_norm_g, mla_w_uq, mla_w_ukv, out_norm_gmlp_g, out_norm_mla_g, w_out, norm_ffn_g, w_ff1, w_ff2, final_norm_g, loss_target, m_w_ada, m_b_ada, m_norm_mix_g, m_w_in, m_gmlp_ws, m_gmlp_bs, m_mla_q_norm_g, m_mla_kv_norm_g, m_mla_w_uq, m_mla_w_ukv, m_out_norm_gmlp_g, m_out_norm_mla_g, m_w_out, m_norm_ffn_g, m_w_ff1, m_w_ff2, m_final_norm_g, v_w_ada, v_b_ada, v_norm_mix_g, v_w_in, v_gmlp_ws, v_gmlp_bs, v_mla_q_norm_g, v_mla_kv_norm_g, v_mla_w_uq, v_mla_w_ukv, v_out_norm_gmlp_g, v_out_norm_mla_g, v_w_out, v_norm_ffn_g, v_w_ff1, v_w_ff2, v_final_norm_g):
    weights = dict(w_ada=w_ada, b_ada=b_ada, norm_mix_g=norm_mix_g, w_in=w_in, gmlp_ws=gmlp_ws, gmlp_bs=gmlp_bs,
                   mla_q_norm_g=mla_q_norm_g, mla_kv_norm_g=mla_kv_norm_g, mla_w_uq=mla_w_uq, mla_w_ukv=mla_w_ukv,
                   out_norm_gmlp_g=out_norm_gmlp_g, out_norm_mla_g=out_norm_mla_g, w_out=w_out,
                   norm_ffn_g=norm_ffn_g, w_ff1=w_ff1, w_ff2=w_ff2, final_norm_g=final_norm_g)
    mom_m = dict(zip(W_NAMES, (m_w_ada, m_b_ada, m_norm_mix_g, m_w_in, m_gmlp_ws, m_gmlp_bs, m_mla_q_norm_g,
                               m_mla_kv_norm_g, m_mla_w_uq, m_mla_w_ukv, m_out_norm_gmlp_g, m_out_norm_mla_g,
                               m_w_out, m_norm_ffn_g, m_w_ff1, m_w_ff2, m_final_norm_g)))
    mom_v = dict(zip(W_NAMES, (v_w_ada, v_b_ada, v_norm_mix_g, v_w_in, v_gmlp_ws, v_gmlp_bs, v_mla_q_norm_g,
                               v_mla_kv_norm_g, v_mla_w_uq, v_mla_w_ukv, v_out_norm_gmlp_g, v_out_norm_mla_g,
                               v_w_out, v_norm_ffn_g, v_w_ff1, v_w_ff2, v_final_norm_g)))
    bsz, seq, d = x.shape
    px, py, pc = _position()
    chip = 2 * px + py
    dev = 2 * chip + pc
    ids = jnp.stack([pc, chip]).astype(jnp.int32)
    n_ex = N_DEV * bsz
    ada_cols = w_ada.shape[-1]

    c_all = _allgather8(c.reshape(bsz * d // 128, 128), name="gather_c").reshape(n_ex, d)
    mod_parts = []
    for l in range(DEPTH):
        bias = lax.dynamic_slice(b_ada[l], (chip * ada_cols,), (ada_cols,))[None]
        mod_parts.append(_mm(c_all, w_ada[l], dims="nn", name=f"l{l}_mod", tm=n_ex, tn=ada_cols, tk=d,
                             epilogue=lambda acc, bv: (acc + bv,), extras=(bias,),
                             extra_specs=(pl.BlockSpec((1, ada_cols), lambda i, j, k: (0, j)),), a_fn=_silu))
    mod_g = _allgather8(jnp.concatenate(mod_parts, axis=0), name="gather_mod")
    mod_g = mod_g.reshape(N_CHIPS, 2, DEPTH, n_ex, ada_cols)[:, 0]
    mod_full = mod_g.transpose(1, 2, 0, 3).reshape(DEPTH, n_ex, N_CHIPS * ada_cols)
    mod_mine = lax.dynamic_slice(mod_full, (0, dev * bsz, 0), (DEPTH, bsz, N_MOD * d))
    mod_mine = jnp.pad(mod_mine.reshape(DEPTH, bsz, N_MOD, d), ((0, 0), (0, 0), (0, MOD_ROWS - N_MOD), (0, 0)))
    mods = [mod_mine[l] for l in range(DEPTH)]

    plan = _CommPlan(weights, ids, dev, pc)
    loss_part, grad_x, d_final_g, dmods = _local_step(x, loss_target, positions, mods, final_norm_g, plan)
    grads = plan.grads
    grad = plan.mix_grads()

    small = {nm: (d_final_g if nm == "final_norm_g" else jnp.stack([grads[l][nm] for l in range(DEPTH)], axis=0))
             for nm in SMALL_NAMES}
    svec = jnp.concatenate([small[nm].reshape(-1) for nm in SMALL_NAMES] + [loss_part[None]])
    n_small = svec.shape[0]
    srows = -(-n_small // (8 * FLAT_W)) * 8
    svec = jnp.pad(svec, (0, srows * FLAT_W - n_small)).reshape(srows, FLAT_W)
    ssum = _sum_leading(_allgather8(svec, name="gather_small_grads"), name="sum_small_grads").reshape(-1)
    off = 0
    for nm in SMALL_NAMES:
        size = weights[nm].size
        grad[nm] = ssum[off:off + size].reshape(weights[nm].shape)
        off += size
    loss = ssum[off]

    dmod = jnp.stack(dmods, axis=1).reshape(bsz * DEPTH * N_MOD, d)
    dmod_all = _allgather8(dmod, name="gather_dmod").reshape(n_ex, DEPTH, N_MOD * d)
    gw, gb = [], []
    for l in range(DEPTH):
        dm = dmod_all[:, l]
        dm_cols = lax.dynamic_slice(dm, (0, chip * ada_cols), (n_ex, ada_cols))
        gw.append(_mm(c_all, dm_cols, dims="tn", name=f"l{l}_dw_ada", tm=d, tn=ada_cols, tk=n_ex, a_fn=_silu))
        gb.append(_sum_leading(dm.reshape(n_ex, N_MOD * d // FLAT_W, FLAT_W), name=f"l{l}_db_ada").reshape(-1))
    grad["w_ada"] = jnp.stack(gw, axis=0)
    grad["b_ada"] = jnp.stack(gb, axis=0)

    delta, new_m, new_v = {}, {}, {}
    ff_bufs = plan.ff_shards()
    for nm, row_off in (("w_ff1", 0), ("w_ff2", FLAT_W)):
        grad[nm], delta[nm], new_m[nm], new_v[nm] = _adamw_layers(
            weights[nm], mom_m[nm], mom_v[nm], ff_bufs, row_off, name=f"adamw_{nm}")
    for nm in W_NAMES:
        if nm not in delta:
            delta[nm], new_m[nm], new_v[nm] = _adamw(weights[nm], grad[nm], mom_m[nm], mom_v[nm],
                                                     name=f"adamw_{nm}")
    return (loss, grad_x, *[grad[nm] for nm in W_NAMES], *[delta[nm] for nm in W_NAMES],
            *[new_m[nm] for nm in W_NAMES], *[new_v[nm] for nm in W_NAMES])
```

```python
import functools
import math

import jax
import jax.numpy as jnp
from jax import lax
from jax.experimental import pallas as pl
from jax.experimental.pallas import tpu as pltpu

F32 = jnp.float32
BF16 = jnp.bfloat16

D_MODEL = 1024
DEPTH = 2
D_GMLP = 512
GROUPS = 8
GROUP_DIM = 64
CHUNK = 128
HEADS = 8
NOPE = 64
ROPE = 32
HEAD_PAD = 128
Q_RANK = 256
KV_RANK = 128
D_FF = 4096
N_MOD = 6
MOD_ROWS = 8
EPS = 1e-6
ROPE_THETA = 10000.0
D_IN = 1440
D_IN_PAD = 1536
ATTN_SCALE = (NOPE + ROPE) ** -0.5
LOG2E = math.log2(math.e)
SCALE_LOG2 = ATTN_SCALE * LOG2E
N_CHIPS = 4
N_DEV = 8

ADAM_LR = 0.001
ADAM_B1 = 0.9
ADAM_B2 = 0.999
ADAM_EPS = 1e-08
ADAM_WD = 0.01
ADAM_STEP = 10

VMEM_LIMIT = 48 * 1024 * 1024
FLAT_W = 1024
ROW_ALIGN = 256

NN = (((1,), (0,)), ((), ()))
NT = (((1,), (1,)), ((), ()))
TN = (((0,), (0,)), ((), ()))
MESH = pl.DeviceIdType.MESH

SHIFT1, SCALE1, GATE1, SHIFT2, SCALE2, GATE2 = range(6)

FSDP_SECTIONS = (("w_out", 256), ("w_in", 360), ("w_uq", 48), ("w_ukv", 32))


def _cparams(vmem=VMEM_LIMIT):
    return pltpu.CompilerParams(vmem_limit_bytes=vmem)


def _dot(a, b, dims=NN):
    return lax.dot_general(a, b, dims, preferred_element_type=F32)


def _iota(shape, axis):
    return lax.broadcasted_iota(jnp.int32, shape, axis)


def _gelu(x):
    k = math.sqrt(2.0 / math.pi)
    return 0.5 * x * (1.0 + jnp.tanh(k * (x + 0.044715 * (x * x * x))))


def _gelu_grad(x):
    k = math.sqrt(2.0 / math.pi)
    t = jnp.tanh(k * (x + 0.044715 * (x * x * x)))
    return 0.5 * (1.0 + t) + 0.5 * x * (1.0 - t * t) * (k * (1.0 + 3.0 * 0.044715 * (x * x)))


def _rms_fwd(x, g, n):
    r = lax.rsqrt(jnp.sum(x * x, axis=-1, keepdims=True) * (1.0 / n) + EPS)
    return x * r * g


def _rms_bwd(x, g, dy, n):
    r = lax.rsqrt(jnp.sum(x * x, axis=-1, keepdims=True) * (1.0 / n) + EPS)
    xh = x * r
    dxh = dy * g
    dx = r * (dxh - xh * (jnp.sum(dxh * xh, axis=-1, keepdims=True) * (1.0 / n)))
    dg = jnp.sum(dy * xh, axis=0, keepdims=True)
    return dx, dg


def _pick_rows(rows, limit):
    if rows <= limit:
        return rows
    for t in range(limit, 7, -8):
        if rows % t == 0:
            return t
    return rows


def _mm(a, b, *, dims, name, tm=512, tn=1024, tk=1024, out_dtypes=(F32,), epilogue=None,
        extras=(), extra_specs=(), a_fn=None, weights_outer=False, side=None, b_block=None, n=None,
        out_into=None):
    if dims == "tn":
        kk, m = a.shape
    else:
        m, kk = a.shape
    if n is None:
        n = b.shape[0] if dims == "nt" else b.shape[1]
    tm, tn, tk = min(tm, m), min(tn, n), min(tk, kk)
    assert m % tm == 0 and n % tn == 0 and kk % tk == 0, (name, a.shape, b.shape, tm, tn, tk)
    ni, nj, nk = m // tm, n // tn, kk // tk

    def spec(shape, pick):
        if weights_outer:
            return pl.BlockSpec(shape, lambda j, i, k: pick(i, j, k))
        return pl.BlockSpec(shape, pick)

    if dims == "tn":
        a_spec = spec((tk, tm), lambda i, j, k: (k, i))
    else:
        a_spec = spec((tm, tk), lambda i, j, k: (i, k))
    if b_block is not None:
        b_spec = spec(*b_block)
    elif dims == "nt":
        b_spec = spec((tn, tk), lambda i, j, k: (j, k))
    else:
        b_spec = spec((tk, tn), lambda i, j, k: (k, j))
    o_spec = spec((tm, tn), lambda i, j, k: (i, j))
    out_shape = [jax.ShapeDtypeStruct((m, n), dt) for dt in out_dtypes]
    out_specs = [o_spec] * len(out_dtypes)
    prev, io_aliases = (), {}
    if out_into is not None:
        full_shape, block, index, before = out_into
        assert len(out_dtypes) == 1 and not extras
        out_shape = [jax.ShapeDtypeStruct(full_shape, out_dtypes[0])]
        out_specs = [spec(block, index)]
        if before is not None:
            prev, io_aliases = (before,), {2: 0}
    assert not (weights_outer and extra_specs)
    dn = {"nn": NN, "nt": NT, "tn": TN}[dims]
    n_ex, n_out = len(extras), len(out_dtypes)
    e_specs = [o_spec if s is None else s for s in (tuple(extra_specs) + (None,) * n_ex)[:n_ex]]

    n_prev = len(prev)

    def body(*refs):
        a_ref, b_ref = refs[0], refs[1]
        e_refs = refs[2 + n_prev:2 + n_prev + n_ex]
        o_refs = refs[2 + n_prev + n_ex:2 + n_prev + n_ex + n_out]
        av = a_ref[...]
        if a_fn is not None:
            av = a_fn(av)
        part = _dot(av.astype(BF16), b_ref[...].astype(BF16), dn)

        def finish(acc):
            outs = (acc,) if epilogue is None else epilogue(acc, *[e[...] for e in e_refs])
            for o_ref, o in zip(o_refs, outs):
                o_ref[...] = o.astype(o_ref.dtype)

        if nk == 1:
            finish(part)
        else:
            acc_ref = refs[-1]
            k = pl.program_id(2)

            @pl.when(k == 0)
            def _():
                acc_ref[...] = part

            @pl.when(k > 0)
            def _():
                acc_ref[...] += part

            @pl.when(k == nk - 1)
            def _():
                finish(acc_ref[...])

    outs, side_outs = _hosted_call(
        body, name=name, grid=(nj, ni, nk) if weights_outer else (ni, nj, nk),
        in_specs=[a_spec, b_spec] + [ANY_SPEC] * n_prev + e_specs,
        out_specs=out_specs, out_shape=out_shape,
        scratch_shapes=[pltpu.VMEM((tm, tn), F32)] if nk > 1 else [],
        args=(a, b, *prev, *extras), side=side, io_aliases=io_aliases)
    res = outs[0] if n_out == 1 else outs
    return res if side is None else (res, side_outs)


def _mod_spec(tm, tn, seq):
    return pl.BlockSpec((1, MOD_ROWS, tn), lambda i, j, k: ((i * tm) // seq, 0, j))


def _normmod_fwd(x3, g, mod, shift_row, scale_row, *, name, tb=256):
    bsz, seq, d = x3.shape
    tb = min(tb, seq)

    def body(x_ref, g_ref, mod_ref, h_ref):
        m = mod_ref[0]
        nrm = _rms_fwd(x_ref[0], g_ref[...], d)
        h = nrm * (1.0 + m[scale_row:scale_row + 1, :]) + m[shift_row:shift_row + 1, :]
        h_ref[0] = h.astype(BF16)

    return pl.pallas_call(
        body, name=name, grid=(bsz, seq // tb),
        in_specs=[pl.BlockSpec((1, tb, d), lambda b, i: (b, i, 0)),
                  pl.BlockSpec((1, d), lambda b, i: (0, 0)),
                  pl.BlockSpec((1, MOD_ROWS, d), lambda b, i: (b, 0, 0))],
        out_specs=pl.BlockSpec((1, tb, d), lambda b, i: (b, i, 0)),
        out_shape=jax.ShapeDtypeStruct((bsz, seq, d), BF16),
        compiler_params=_cparams(),
    )(x3, g, mod)


def _pair_mean_exact(x, lo):
    s_lo = jnp.sum(jnp.where(lo, x, 0.0), axis=-1, keepdims=True)
    s_hi = jnp.sum(jnp.where(lo, 0.0, x), axis=-1, keepdims=True)
    return jnp.where(lo, s_lo, s_hi) * (1.0 / GROUP_DIM)


def _gmlp_pair_fwd(gv_p, w0, w1, bias, lo):
    mu = _pair_mean_exact(gv_p, lo)
    dlt = gv_p - mu
    var = _pair_mean_exact(dlt * dlt, lo)
    rstd = lax.rsqrt(var + EPS)
    vn = dlt * rstd
    vnb = vn.astype(BF16)
    mixed = jnp.where(lo, _dot(w0, vnb), _dot(w1, vnb)) + bias
    return vn, vnb, rstd, mixed


def _tril_bf16(w):
    t = w.shape[-1]
    return jnp.where(_iota((t, t), 1) <= _iota((t, t), 0), w, 0.0).astype(BF16)


def _gmlp_fwd(z3, ws, bexp, g_out, *, name):
    bsz, seq, _ = z3.shape
    nc = seq // CHUNK

    def body(u_ref, v_ref, ws_ref, b_ref, g_ref, y_ref):
        lo = _iota((CHUNK, 128), 1) < GROUP_DIM
        gu = _gelu(u_ref[0].astype(F32))
        gv = _gelu(v_ref[0].astype(F32))
        parts = []
        for p in range(GROUPS // 2):
            sl = slice(128 * p, 128 * p + 128)
            w0 = _tril_bf16(ws_ref[2 * p])
            w1 = _tril_bf16(ws_ref[2 * p + 1])
            _, _, _, mixed = _gmlp_pair_fwd(gv[:, sl], w0, w1, b_ref[p], lo)
            parts.append(gu[:, sl] * mixed)
        yg = jnp.concatenate(parts, axis=1)
        y_ref[0] = _rms_fwd(yg, g_ref[...], D_GMLP).astype(BF16)

    return pl.pallas_call(
        body, name=name, grid=(bsz, nc),
        in_specs=[pl.BlockSpec((1, CHUNK, D_GMLP), lambda b, i: (b, i, 0)),
                  pl.BlockSpec((1, CHUNK, D_GMLP), lambda b, i: (b, i, 1)),
                  pl.BlockSpec((GROUPS, CHUNK, CHUNK), lambda b, i: (0, 0, 0)),
                  pl.BlockSpec((GROUPS // 2, CHUNK, 128), lambda b, i: (0, 0, 0)),
                  pl.BlockSpec((1, D_GMLP), lambda b, i: (0, 0))],
        out_specs=pl.BlockSpec((1, CHUNK, D_GMLP), lambda b, i: (b, i, 0)),
        out_shape=jax.ShapeDtypeStruct((bsz, seq, D_GMLP), BF16),
        compiler_params=_cparams(),
    )(z3, z3, ws, bexp, g_out)


def _gmlp_bwd(z3, dyn3, ws, wst, bexp, g_out, *, name):
    bsz, seq, _ = z3.shape
    nc = seq // CHUNK
    npair = GROUPS // 2

    def body(u_ref, v_ref, dy_ref, ws_ref, wst_ref, b_ref, g_ref, duv_ref, dws_ref, dbs_ref, dg_ref, dbacc):
        first = jnp.logical_and(pl.program_id(0) == 0, pl.program_id(1) == 0)
        last = jnp.logical_and(pl.program_id(0) == bsz - 1, pl.program_id(1) == nc - 1)

        @pl.when(first)
        def _():
            dws_ref[...] = jnp.zeros_like(dws_ref)
            dg_ref[...] = jnp.zeros_like(dg_ref)
            dbacc[...] = jnp.zeros_like(dbacc)

        lo = _iota((CHUNK, 128), 1) < GROUP_DIM
        tril = _iota((CHUNK, CHUNK), 1) <= _iota((CHUNK, CHUNK), 0)
        u = u_ref[0].astype(F32)
        v = v_ref[0].astype(F32)
        gu = _gelu(u)
        gv = _gelu(v)
        fwd = []
        for p in range(npair):
            sl = slice(128 * p, 128 * p + 128)
            w0 = _tril_bf16(ws_ref[2 * p])
            w1 = _tril_bf16(ws_ref[2 * p + 1])
            fwd.append(_gmlp_pair_fwd(gv[:, sl], w0, w1, b_ref[p], lo))
        yg = jnp.concatenate([gu[:, 128 * p:128 * p + 128] * fwd[p][3] for p in range(npair)], axis=1)
        dyg, dg = _rms_bwd(yg, g_ref[...], dy_ref[0].astype(F32), D_GMLP)
        dg_ref[...] += dg
        du_parts, dv_parts = [], []
        for p in range(npair):
            sl = slice(128 * p, 128 * p + 128)
            vn, vnb, rstd, mixed = fwd[p]
            dyg_p = dyg[:, sl]
            dmixed = dyg_p * gu[:, sl]
            dbacc[p] += dmixed
            dm_lo = jnp.where(lo, dmixed, 0.0).astype(BF16)
            dm_hi = jnp.where(lo, 0.0, dmixed).astype(BF16)
            dws_ref[2 * p] += jnp.where(tril, _dot(dm_lo, vnb, NT), 0.0)
            dws_ref[2 * p + 1] += jnp.where(tril, _dot(dm_hi, vnb, NT), 0.0)
            dmb = dmixed.astype(BF16)
            dvn = jnp.where(lo, _dot(wst_ref[2 * p], dmb), _dot(wst_ref[2 * p + 1], dmb))
            dgv = rstd * (dvn - _pair_mean_exact(dvn, lo) - vn * _pair_mean_exact(dvn * vn, lo))
            dv_parts.append(dgv * _gelu_grad(v[:, sl]))
            du_parts.append(dyg_p * mixed * _gelu_grad(u[:, sl]))
        duv_ref[0] = jnp.concatenate(du_parts + dv_parts, axis=1).astype(BF16)

        @pl.when(last)
        def _():
            sel = jnp.where(_iota((8, 128), 0) == 0, (_iota((8, 128), 1) < GROUP_DIM).astype(F32),
                            jnp.where(_iota((8, 128), 0) == 1, (_iota((8, 128), 1) >= GROUP_DIM).astype(F32), 0.0))
            for p in range(npair):
                dbs_ref[p] = lax.dot_general(sel, dbacc[p], NT, precision=lax.Precision.HIGHEST,
                                             preferred_element_type=F32)

    duv, dws, dbs, dg = pl.pallas_call(
        body, name=name, grid=(bsz, nc),
        in_specs=[pl.BlockSpec((1, CHUNK, D_GMLP), lambda b, i: (b, i, 0)),
                  pl.BlockSpec((1, CHUNK, D_GMLP), lambda b, i: (b, i, 1)),
                  pl.BlockSpec((1, CHUNK, D_GMLP), lambda b, i: (b, i, 0)),
                  pl.BlockSpec((GROUPS, CHUNK, CHUNK), lambda b, i: (0, 0, 0)),
                  pl.BlockSpec((GROUPS, CHUNK, CHUNK), lambda b, i: (0, 0, 0)),
                  pl.BlockSpec((npair, CHUNK, 128), lambda b, i: (0, 0, 0)),
                  pl.BlockSpec((1, D_GMLP), lambda b, i: (0, 0))],
        out_specs=[pl.BlockSpec((1, CHUNK, 2 * D_GMLP), lambda b, i: (b, i, 0)),
                   pl.BlockSpec((GROUPS, CHUNK, CHUNK), lambda b, i: (0, 0, 0)),
                   pl.BlockSpec((npair, 8, CHUNK), lambda b, i: (0, 0, 0)),
                   pl.BlockSpec((1, D_GMLP), lambda b, i: (0, 0))],
        out_shape=[jax.ShapeDtypeStruct((bsz, seq, 2 * D_GMLP), BF16),
                   jax.ShapeDtypeStruct((GROUPS, CHUNK, CHUNK), F32),
                   jax.ShapeDtypeStruct((npair, 8, CHUNK), F32),
                   jax.ShapeDtypeStruct((1, D_GMLP), F32)],
        scratch_shapes=[pltpu.VMEM((npair, CHUNK, 128), F32)],
        compiler_params=_cparams(),
    )(z3, z3, dyn3, ws, wst, bexp, g_out)
    return duv, dws, dbs[:, :2, :].reshape(GROUPS, CHUNK), dg


def _partner(x):
    width = x.shape[-1]
    lane = _iota(x.shape, x.ndim - 1) % HEAD_PAD
    up = pltpu.roll(x, width - ROPE // 2, x.ndim - 1)
    down = pltpu.roll(x, ROPE // 2, x.ndim - 1)
    first = jnp.logical_and(lane >= NOPE, lane < NOPE + ROPE // 2)
    second = jnp.logical_and(lane >= NOPE + ROPE // 2, lane < NOPE + ROPE)
    return jnp.where(first, up, jnp.where(second, down, 0.0))


def _mla_prep_fwd(z3, g_q, g_kv, w_uq, w_ukv, ctab, stab, *, name, tb=256):
    bsz, seq, _ = z3.shape
    tb = min(tb, seq)
    hw = HEADS * HEAD_PAD

    def body(ql_ref, kvl_ref, krl_ref, gq_ref, gkv_ref, wuq_ref, wukv_ref, c_ref, s_ref, q_ref, kv_ref, kp_ref):
        cq = _rms_fwd(ql_ref[0].astype(F32), gq_ref[...], Q_RANK).astype(BF16)
        q = _dot(cq, wuq_ref[...])
        c1, s1 = c_ref[0], s_ref[0]
        c8, s8 = jnp.tile(c1, (1, HEADS)), jnp.tile(s1, (1, HEADS))
        q_ref[0] = (q * c8 + _partner(q) * s8).astype(BF16)
        ckv = _rms_fwd(kvl_ref[0].astype(F32), gkv_ref[...], KV_RANK).astype(BF16)
        kv = _dot(ckv, wukv_ref[...])
        kv_ref[0] = kv.astype(BF16)
        kr = krl_ref[0].astype(F32)
        kr = kr * c1 + _partner(kr) * s1
        lane = _iota((tb, hw), 1) % HEAD_PAD
        kp_ref[0] = jnp.where(lane < NOPE, kv, jnp.tile(kr, (1, HEADS))).astype(BF16)

    return pl.pallas_call(
        body, name=name, grid=(bsz, seq // tb),
        in_specs=[pl.BlockSpec((1, tb, Q_RANK), lambda b, i: (b, i, 4)),
                  pl.BlockSpec((1, tb, KV_RANK), lambda b, i: (b, i, 10)),
                  pl.BlockSpec((1, tb, HEAD_PAD), lambda b, i: (b, i, 11)),
                  pl.BlockSpec((1, Q_RANK), lambda b, i: (0, 0)),
                  pl.BlockSpec((1, KV_RANK), lambda b, i: (0, 0)),
                  pl.BlockSpec((Q_RANK, hw), lambda b, i: (0, 0)),
                  pl.BlockSpec((KV_RANK, hw), lambda b, i: (0, 0)),
                  pl.BlockSpec((1, tb, HEAD_PAD), lambda b, i: (b, i, 0)),
                  pl.BlockSpec((1, tb, HEAD_PAD), lambda b, i: (b, i, 0))],
        out_specs=[pl.BlockSpec((1, tb, hw), lambda b, i: (b, i, 0))] * 3,
        out_shape=[jax.ShapeDtypeStruct((bsz, seq, hw), BF16)] * 3,
        compiler_params=_cparams(),
    )(z3, z3, z3, g_q, g_kv, w_uq, w_ukv, ctab, stab)


def _mla_prep_bwd(z3, dq3, dk3, dv3, g_q, g_kv, w_uq, w_ukv, ctab, stab, *, name, tb=256):
    bsz, seq, _ = z3.shape
    tb = min(tb, seq)
    hw = HEADS * HEAD_PAD
    nb = seq // tb

    def body(ql_ref, kvl_ref, dq_ref, dk_ref, dv_ref, gq_ref, gkv_ref, wuq_ref, wukv_ref, c_ref, s_ref,
             dz_ref, cq_ref, dqb_ref, ckv_ref, dkvb_ref, dgq_ref, dgkv_ref):
        @pl.when(jnp.logical_and(pl.program_id(0) == 0, pl.program_id(1) == 0))
        def _():
            dgq_ref[...] = jnp.zeros_like(dgq_ref)
            dgkv_ref[...] = jnp.zeros_like(dgkv_ref)

        c1, s1 = c_ref[0], s_ref[0]
        c8, s8 = jnp.tile(c1, (1, HEADS)), jnp.tile(s1, (1, HEADS))
        dqr = dq_ref[0]
        dqb = (dqr * c8 + _partner(dqr * s8)).astype(BF16)
        dqb_ref[0] = dqb
        ql = ql_ref[0].astype(F32)
        cq_ref[0] = _rms_fwd(ql, gq_ref[...], Q_RANK).astype(BF16)
        dql, dgq = _rms_bwd(ql, gq_ref[...], _dot(dqb, wuq_ref[...], NT), Q_RANK)
        dgq_ref[...] += dgq

        dk = dk_ref[0]
        lane = _iota((tb, hw), 1) % HEAD_PAD
        dkvb = jnp.where(lane < NOPE, dk, dv_ref[0]).astype(BF16)
        dkvb_ref[0] = dkvb
        kvl = kvl_ref[0].astype(F32)
        ckv_ref[0] = _rms_fwd(kvl, gkv_ref[...], KV_RANK).astype(BF16)
        dkvl, dgkv = _rms_bwd(kvl, gkv_ref[...], _dot(dkvb, wukv_ref[...], NT), KV_RANK)
        dgkv_ref[...] += dgkv

        dkr = dk[:, 0:HEAD_PAD].astype(F32)
        for h in range(1, HEADS):
            dkr = dkr + dk[:, HEAD_PAD * h:HEAD_PAD * (h + 1)].astype(F32)
        lane1 = _iota((tb, HEAD_PAD), 1)
        dkr = jnp.where(jnp.logical_and(lane1 >= NOPE, lane1 < NOPE + ROPE), dkr, 0.0)
        dkrl = dkr * c1 + _partner(dkr * s1)
        dz_ref[0] = jnp.concatenate([dql, dkvl, dkrl], axis=1).astype(BF16)

    return pl.pallas_call(
        body, name=name, grid=(bsz, nb),
        in_specs=[pl.BlockSpec((1, tb, Q_RANK), lambda b, i: (b, i, 4)),
                  pl.BlockSpec((1, tb, KV_RANK), lambda b, i: (b, i, 10)),
                  pl.BlockSpec((1, tb, hw), lambda b, i: (b, i, 0)),
                  pl.BlockSpec((1, tb, hw), lambda b, i: (b, i, 0)),
                  pl.BlockSpec((1, tb, hw), lambda b, i: (b, i, 0)),
                  pl.BlockSpec((1, Q_RANK), lambda b, i: (0, 0)),
                  pl.BlockSpec((1, KV_RANK), lambda b, i: (0, 0)),
                  pl.BlockSpec((Q_RANK, hw), lambda b, i: (0, 0)),
                  pl.BlockSpec((KV_RANK, hw), lambda b, i: (0, 0)),
                  pl.BlockSpec((1, tb, HEAD_PAD), lambda b, i: (b, i, 0)),
                  pl.BlockSpec((1, tb, HEAD_PAD), lambda b, i: (b, i, 0))],
        out_specs=[pl.BlockSpec((1, tb, 512), lambda b, i: (b, i, 0)),
                   pl.BlockSpec((1, tb, Q_RANK), lambda b, i: (b, i, 0)),
                   pl.BlockSpec((1, tb, hw), lambda b, i: (b, i, 0)),
                   pl.BlockSpec((1, tb, KV_RANK), lambda b, i: (b, i, 0)),
                   pl.BlockSpec((1, tb, hw), lambda b, i: (b, i, 0)),
                   pl.BlockSpec((1, Q_RANK), lambda b, i: (0, 0)),
                   pl.BlockSpec((1, KV_RANK), lambda b, i: (0, 0))],
        out_shape=[jax.ShapeDtypeStruct((bsz, seq, 512), BF16),
                   jax.ShapeDtypeStruct((bsz, seq, Q_RANK), BF16),
                   jax.ShapeDtypeStruct((bsz, seq, hw), BF16),
                   jax.ShapeDtypeStruct((bsz, seq, KV_RANK), BF16),
                   jax.ShapeDtypeStruct((bsz, seq, hw), BF16),
                   jax.ShapeDtypeStruct((1, Q_RANK), F32),
                   jax.ShapeDtypeStruct((1, KV_RANK), F32)],
        compiler_params=_cparams(),
    )(z3, z3, dq3, dk3, dv3, g_q, g_kv, w_uq, w_ukv, ctab, stab)


ATTN_HEADS_PER_STEP = 2


def _attn_specs(tq, seq, hp):
    blk = pl.BlockSpec((1, tq, hp * HEAD_PAD), lambda b, h, i: (b, i, h))
    full = pl.BlockSpec((1, seq, hp * HEAD_PAD), lambda b, h, i: (b, 0, h))
    return blk, full


def _head(h):
    return slice(HEAD_PAD * h, HEAD_PAD * (h + 1))


def _attn_fwd(q3, kv3, kp3, *, name, tq=512, hp=ATTN_HEADS_PER_STEP, side=None):
    bsz, seq, hw = q3.shape
    tq = min(tq, seq)
    blk, full = _attn_specs(tq, seq, hp)

    def body(q_ref, kv_ref, kp_ref, o_ref, lse_ref):
        i = pl.program_id(2)
        is_nope = _iota((tq, HEAD_PAD), 1) < NOPE
        causal = _iota((tq, tq), 1) <= _iota((tq, tq), 0)

        def step(j, carry, diag):
            st = pl.multiple_of(j * tq, tq)
            out = []
            for h in range(hp):
                m, l, acc = carry[h]
                kvj = kv_ref[0, pl.ds(st, tq), _head(h)]
                s = _dot(q_ref[0, :, _head(h)], kp_ref[0, pl.ds(st, tq), _head(h)], NT) * SCALE_LOG2
                if diag:
                    s = jnp.where(causal, s, -1e30)
                m_new = jnp.maximum(m, jnp.max(s, axis=1, keepdims=True))
                alpha = jnp.exp2(m - m_new)
                p = jnp.exp2(s - m_new)
                l = alpha * l + jnp.sum(p, axis=1, keepdims=True)
                acc = alpha * acc + _dot(p.astype(BF16), kvj)
                out.append((m_new, l, acc))
            return tuple(out)

        init = tuple((jnp.full((tq, 1), -1e30, F32), jnp.zeros((tq, 1), F32), jnp.zeros((tq, HEAD_PAD), F32))
                     for _ in range(hp))
        carry = lax.fori_loop(0, i, lambda j, c: step(j, c, False), init)
        carry = step(i, carry, True)
        for h in range(hp):
            m, l, acc = carry[h]
            o_ref[0, :, _head(h)] = jnp.where(is_nope, 0.0, acc / l).astype(BF16)
            lse_ref[0, :, _head(h)] = jnp.broadcast_to(m + jnp.log(l) * LOG2E, (tq, HEAD_PAD))

    outs, side_outs = _hosted_call(
        body, name=name, grid=(bsz, HEADS // hp, seq // tq),
        in_specs=[blk, full, full],
        out_specs=[blk, blk],
        out_shape=[jax.ShapeDtypeStruct((bsz, seq, hw), BF16), jax.ShapeDtypeStruct((bsz, seq, hw), F32)],
        args=(q3, kv3, kp3), side=side)
    return outs if side is None else (outs, side_outs)


def _attn_bwd(q3, kv3, kp3, do3, lse3, dl3, *, name, tq=512, hp=ATTN_HEADS_PER_STEP, side=None):
    bsz, seq, hw = q3.shape
    tq = min(tq, seq)
    nq = seq // tq
    blk, full = _attn_specs(tq, seq, hp)
    rep = tq // HEAD_PAD

    def body(kv_ref, kp_ref, q_ref, do_ref, lse_ref, dl_ref, dq_ref, dk_ref, dv_ref):
        j = pl.program_id(2)
        causal = _iota((tq, tq), 1) <= _iota((tq, tq), 0)

        @pl.when(j == 0)
        def _():
            dq_ref[...] = jnp.zeros_like(dq_ref)

        def step(i, carry, diag):
            st = pl.multiple_of(i * tq, tq)
            out = []
            for h in range(hp):
                dk, dv = carry[h]
                qi = q_ref[0, pl.ds(st, tq), _head(h)]
                do = do_ref[0, pl.ds(st, tq), _head(h)]
                kp = kp_ref[0, :, _head(h)]
                s = _dot(qi, kp, NT) * SCALE_LOG2
                if diag:
                    s = jnp.where(causal, s, -1e30)
                p = jnp.exp2(s - jnp.tile(lse_ref[0, pl.ds(st, tq), _head(h)], (1, rep)))
                dv = dv + _dot(p.astype(BF16), do, TN)
                dp = _dot(do, kv_ref[0, :, _head(h)], NT)
                ds = (p * (dp - jnp.tile(dl_ref[0, pl.ds(st, tq), _head(h)], (1, rep)))).astype(BF16)
                dk = dk + _dot(ds, qi, TN)
                dq_ref[0, pl.ds(st, tq), _head(h)] += _dot(ds, kp)
                out.append((dk, dv))
            return tuple(out)

        zero = jnp.zeros((tq, HEAD_PAD), F32)
        carry = step(j, tuple((zero, zero) for _ in range(hp)), True)
        carry = lax.fori_loop(j + 1, nq, lambda i, c: step(i, c, False), carry)
        for h in range(hp):
            dk_ref[0, :, _head(h)] = (carry[h][0] * ATTN_SCALE).astype(BF16)
            dv_ref[0, :, _head(h)] = carry[h][1].astype(BF16)

        @pl.when(j == nq - 1)
        def _():
            dq_ref[...] = dq_ref[...] * ATTN_SCALE

    outs, side_outs = _hosted_call(
        body, name=name, grid=(bsz, HEADS // hp, nq),
        in_specs=[blk, blk, full, full, full, full],
        out_specs=[full, blk, blk],
        out_shape=[jax.ShapeDtypeStruct((bsz, seq, hw), F32)] + [jax.ShapeDtypeStruct((bsz, seq, hw), BF16)] * 2,
        args=(kv3, kp3, q3, do3, lse3, dl3), side=side)
    return outs if side is None else (outs, side_outs)


def _onorm_fwd(o3, g_pad, *, name, tb=256):
    bsz, seq, hw = o3.shape
    tb = min(tb, seq)

    def body(o_ref, g_ref, y_ref):
        y_ref[0] = _rms_fwd(o_ref[0].astype(F32), g_ref[...], HEADS * 64).astype(BF16)

    return pl.pallas_call(
        body, name=name, grid=(bsz, seq // tb),
        in_specs=[pl.BlockSpec((1, tb, hw), lambda b, i: (b, i, 0)), pl.BlockSpec((1, hw), lambda b, i: (0, 0))],
        out_specs=pl.BlockSpec((1, tb, hw), lambda b, i: (b, i, 0)),
        out_shape=jax.ShapeDtypeStruct((bsz, seq, hw), BF16),
        compiler_params=_cparams(),
    )(o3, g_pad)


def _onorm_bwd(o3, dy3, g_pad, *, name, tb=256):
    bsz, seq, hw = o3.shape
    tb = min(tb, seq)

    def body(o_ref, dy_ref, g_ref, do_ref, dl_ref, dg_ref):
        @pl.when(jnp.logical_and(pl.program_id(0) == 0, pl.program_id(1) == 0))
        def _():
            dg_ref[...] = jnp.zeros_like(dg_ref)

        o = o_ref[0].astype(F32)
        do, dg = _rms_bwd(o, g_ref[...], dy_ref[0].astype(F32), HEADS * 64)
        dg_ref[...] += dg
        do_ref[0] = do.astype(BF16)
        prod = do * o
        parts = []
        for h in range(HEADS):
            sh = jnp.sum(prod[:, HEAD_PAD * h:HEAD_PAD * (h + 1)], axis=1, keepdims=True)
            parts.append(jnp.broadcast_to(sh, (tb, HEAD_PAD)))
        dl_ref[0] = jnp.concatenate(parts, axis=1)

    return pl.pallas_call(
        body, name=name, grid=(bsz, seq // tb),
        in_specs=[pl.BlockSpec((1, tb, hw), lambda b, i: (b, i, 0)),
                  pl.BlockSpec((1, tb, hw), lambda b, i: (b, i, 0)),
                  pl.BlockSpec((1, hw), lambda b, i: (0, 0))],
        out_specs=[pl.BlockSpec((1, tb, hw), lambda b, i: (b, i, 0)),
                   pl.BlockSpec((1, tb, hw), lambda b, i: (b, i, 0)),
                   pl.BlockSpec((1, hw), lambda b, i: (0, 0))],
        out_shape=[jax.ShapeDtypeStruct((bsz, seq, hw), BF16),
                   jax.ShapeDtypeStruct((bsz, seq, hw), F32),
                   jax.ShapeDtypeStruct((1, hw), F32)],
        compiler_params=_cparams(),
    )(o3, dy3, g_pad)


def _resnode_bwd(x3, g, *, name, target3=None, dh3=None, dres3=None, mod_nm=None, rows=None,
                 branch3=None, mod_gate=None, gate_row=None, tb=256):
    bsz, seq, d = x3.shape
    tb = min(tb, seq)
    final = target3 is not None
    has_branch = branch3 is not None
    row_spec = pl.BlockSpec((1, tb, d), lambda b, i: (b, i, 0))
    vec_spec = pl.BlockSpec((1, d), lambda b, i: (0, 0))
    mod_spec = pl.BlockSpec((1, MOD_ROWS, d), lambda b, i: (b, 0, 0))

    ins, in_specs = [x3, g], [row_spec, vec_spec]
    if final:
        ins += [target3]
        in_specs += [row_spec]
    else:
        ins += [dh3, dres3, mod_nm]
        in_specs += [row_spec, row_spec, mod_spec]
    if has_branch:
        ins += [branch3, mod_gate]
        in_specs += [row_spec, mod_spec]

    out_names = ["dx", "dg"]
    out_specs = [row_spec, vec_spec]
    out_shape = [jax.ShapeDtypeStruct((bsz, seq, d), F32), jax.ShapeDtypeStruct((1, d), F32)]
    if final:
        out_names += ["loss"]
        out_specs += [pl.BlockSpec((1, 128), lambda b, i: (0, 0))]
        out_shape += [jax.ShapeDtypeStruct((1, 128), F32)]
    else:
        out_names += ["dnm"]
        out_specs += [mod_spec]
        out_shape += [jax.ShapeDtypeStruct((bsz, MOD_ROWS, d), F32)]
    if has_branch:
        out_names += ["dbr", "dgate"]
        out_specs += [row_spec, mod_spec]
        out_shape += [jax.ShapeDtypeStruct((bsz, seq, d), BF16), jax.ShapeDtypeStruct((bsz, MOD_ROWS, d), F32)]
    n_in = len(ins)

    def body(*refs):
        r = dict(zip(["x", "g"] + (["t"] if final else ["dh", "dres", "nm"]) + (["br", "gm"] if has_branch else []),
                     refs[:n_in]))
        o = dict(zip(out_names, refs[n_in:]))
        b_first = pl.program_id(1) == 0
        first = jnp.logical_and(pl.program_id(0) == 0, b_first)
        rowid = _iota((MOD_ROWS, d), 0)

        @pl.when(first)
        def _():
            o["dg"][...] = jnp.zeros_like(o["dg"])
            if final:
                o["loss"][...] = jnp.zeros_like(o["loss"])

        @pl.when(b_first)
        def _():
            if not final:
                o["dnm"][...] = jnp.zeros_like(o["dnm"])
            if has_branch:
                o["dgate"][...] = jnp.zeros_like(o["dgate"])

        x = r["x"][0]
        gv = r["g"][...]
        if final:
            e = _rms_fwd(x, gv, d) - r["t"][0]
            sq = jnp.sum(jnp.sum(e * e, axis=1, keepdims=True), axis=0, keepdims=True)
            o["loss"][...] += jnp.broadcast_to(sq * (0.5 / d), (1, 128))
            dx, dg = _rms_bwd(x, gv, e * (1.0 / d), d)
        else:
            m = r["nm"][0]
            dh = r["dh"][0].astype(F32)
            scale = m[rows[1]:rows[1] + 1, :]
            rstd = lax.rsqrt(jnp.sum(x * x, axis=-1, keepdims=True) * (1.0 / d) + EPS)
            xh = x * rstd
            nrm = xh * gv
            dshift = jnp.sum(dh, axis=0, keepdims=True)
            dscale = jnp.sum(dh * nrm, axis=0, keepdims=True)
            o["dnm"][0] += jnp.where(rowid == 0, dshift, jnp.where(rowid == 1, dscale, 0.0))
            dn = dh * (1.0 + scale)
            dg = jnp.sum(dn * xh, axis=0, keepdims=True)
            dxh = dn * gv
            dx = rstd * (dxh - xh * (jnp.sum(dxh * xh, axis=-1, keepdims=True) * (1.0 / d))) + r["dres"][0]
        o["dg"][...] += dg
        o["dx"][0] = dx
        if has_branch:
            gate = r["gm"][0][gate_row:gate_row + 1, :]
            o["dbr"][0] = (gate * dx).astype(BF16)
            dgate = jnp.sum(dx * r["br"][0], axis=0, keepdims=True)
            o["dgate"][0] += jnp.where(rowid == 0, dgate, 0.0)

    outs = pl.pallas_call(
        body, name=name, grid=(bsz, seq // tb),
        in_specs=in_specs, out_specs=out_specs, out_shape=out_shape,
        compiler_params=_cparams(),
    )(*ins)
    return dict(zip(out_names, outs))


def _adamw(w, g, m, v, *, name):
    shape = w.shape
    cols = shape[-1]
    rows = w.size // cols
    tr = _pick_rows(rows, max(8, (256 * 1024) // cols // 8 * 8))

    def body(w_ref, g_ref, m_ref, v_ref, d_ref, nm_ref, nv_ref):
        d_ref[...], nm_ref[...], nv_ref[...] = _adamw_math(w_ref[...], g_ref[...], m_ref[...], v_ref[...])

    spec = pl.BlockSpec((tr, cols), lambda i: (i, 0))
    outs = pl.pallas_call(
        body, name=name, grid=(rows // tr,),
        in_specs=[spec] * 4, out_specs=[spec] * 3,
        out_shape=[jax.ShapeDtypeStruct((rows, cols), F32)] * 3,
        compiler_params=_cparams(),
    )(*[t.reshape(rows, cols) for t in (w, g, m, v)])
    return tuple(o.reshape(shape) for o in outs)


def _adamw_math(w, g, m, v):
    c1 = 1.0 - ADAM_B1 ** ADAM_STEP
    c2 = 1.0 - ADAM_B2 ** ADAM_STEP
    nm = ADAM_B1 * m + (1.0 - ADAM_B1) * g
    nv = ADAM_B2 * v + (1.0 - ADAM_B2) * (g * g)
    delta = -ADAM_LR * ((nm / c1) / (jnp.sqrt(nv / c2) + ADAM_EPS) + ADAM_WD * w)
    return delta, nm, nv


def _adamw_layers(w, m, v, bufs, row_off, *, name, tr=256):
    depth, rows, cols = w.shape
    tr = min(tr, rows)
    assert rows % tr == 0 and row_off % tr == 0

    outs = None
    for l in range(depth):
        def body(w_ref, g_ref, m_ref, v_ref, *rest):
            go_ref, d_ref, nm_ref, nv_ref = rest[-4:]
            g = g_ref[...]
            go_ref[...] = g
            d_ref[...], nm_ref[...], nv_ref[...] = _adamw_math(w_ref[...], g, m_ref[...], v_ref[...])

        layer = pl.BlockSpec((None, tr, cols), lambda i, l=l: (l, i, 0))
        prev = () if outs is None else tuple(outs)
        outs = pl.pallas_call(
            body, name=f"{name}_l{l}", grid=(rows // tr,),
            in_specs=[layer, pl.BlockSpec((tr, cols), lambda i: (row_off // tr + i, 0)), layer, layer]
            + [ANY_SPEC] * len(prev),
            out_specs=[layer] * 4,
            out_shape=[jax.ShapeDtypeStruct(w.shape, F32)] * 4,
            input_output_aliases={4 + k: k for k in range(len(prev))},
            compiler_params=_cparams(),
        )(w, bufs[l], m, v, *prev)
    return tuple(outs)


def _sum_leading(x, *, name, tr=256):
    n, rows, cols = x.shape
    tr = _pick_rows(rows, tr)

    def body(x_ref, o_ref):
        acc = x_ref[0]
        for k in range(1, n):
            acc = acc + x_ref[k]
        o_ref[...] = acc

    return pl.pallas_call(
        body, name=name, grid=(rows // tr,),
        in_specs=[pl.BlockSpec((n, tr, cols), lambda i: (0, i, 0))],
        out_specs=pl.BlockSpec((tr, cols), lambda i: (i, 0)),
        out_shape=jax.ShapeDtypeStruct((rows, cols), F32),
        compiler_params=_cparams(),
    )(x)


def _position():
    return lax.axis_index("x"), lax.axis_index("y"), lax.axis_index("c")


def _allgather8(x, *, name):
    shape = x.shape

    def body(x_ref, out_ref, send_sems, recv_sems, local_sem):
        px, py, pc = _position()
        me, sibling = (px, py, pc), (px, py, 1 - pc)
        chips = [(1 - px, py), (px, 1 - py), (1 - px, 1 - py)]
        src_own = x_ref

        def slot(qx, qy, qc):
            return out_ref.at[4 * qx + 2 * qy + qc]

        def copy(k, block, to, src=None):
            return pltpu.make_async_remote_copy(
                src_ref=slot(*block) if src is None else src, dst_ref=slot(*block),
                send_sem=send_sems.at[k], recv_sem=recv_sems.at[k], device_id=to, device_id_type=MESH)

        mine = pltpu.make_async_copy(src_own, slot(*me), local_sem)
        mine.start()
        first = [copy(0, me, sibling, src=src_own)]
        first += [copy(1 + j, me, (*chip, pc), src=src_own) for j, chip in enumerate(chips)]
        for cp in first:
            cp.start()
        passed = [copy(4 + j, (*chip, pc), sibling) for j, chip in enumerate(chips)]
        for j, chip in enumerate(chips):
            copy(1 + j, (*chip, pc), me).wait_recv()
            passed[j].start()
        copy(0, sibling, me).wait_recv()
        for j, chip in enumerate(chips):
            copy(4 + j, (*chip, 1 - pc), me).wait_recv()
        for cp in first + passed:
            cp.wait_send()
        mine.wait()

    return pl.pallas_call(
        body, name=name,
        out_shape=jax.ShapeDtypeStruct((N_DEV,) + shape, x.dtype),
        in_specs=[pl.BlockSpec(memory_space=pl.ANY)],
        out_specs=pl.BlockSpec(memory_space=pl.ANY),
        scratch_shapes=[pltpu.SemaphoreType.DMA((7,)), pltpu.SemaphoreType.DMA((7,)), pltpu.SemaphoreType.DMA],
    )(x)


class _Exchange:
    def __init__(self, ins, out_shapes, n, build, aliases=None):
        self.ins, self.out_shapes, self.n, self.build = tuple(ins), tuple(out_shapes), n, build
        self.aliases = dict(aliases or {})

    def _descriptors(self, in_refs, out_refs, send_sems, recv_sems):
        sends, recvs = [], []
        for k, (src, dst, peer, landing) in enumerate(self.build(in_refs, out_refs)):
            sends.append(pltpu.make_async_remote_copy(
                src_ref=src, dst_ref=dst, send_sem=send_sems.at[k], recv_sem=recv_sems.at[k],
                device_id=peer, device_id_type=MESH))
            recvs.append(pltpu.make_async_remote_copy(
                src_ref=src, dst_ref=landing, send_sem=send_sems.at[k], recv_sem=recv_sems.at[k],
                device_id=peer, device_id_type=MESH))
        return sends, recvs

    def start(self, *refs):
        for cp in self._descriptors(*refs)[0]:
            cp.start()

    def finish(self, *refs):
        sends, recvs = self._descriptors(*refs)
        for cp in recvs:
            cp.wait_recv()
        for cp in sends:
            cp.wait_send()


ANY_SPEC = pl.BlockSpec(memory_space=pl.ANY)


def _hosted_call(body, *, name, grid, in_specs, out_specs, out_shape, args, scratch_shapes=(), side=None,
                 num_scalar_prefetch=0, io_aliases=None):
    in_specs, out_specs, out_shape = list(in_specs), list(out_specs), list(out_shape)
    n_in, n_out = len(in_specs) + num_scalar_prefetch, len(out_specs)
    kernel_body = body
    aliases = dict(io_aliases or {})
    if side is not None:
        s_in, s_out = len(side.ins), len(side.out_shapes)
        aliases.update({n_in + i: n_out + o for i, o in side.aliases.items()})

        def kernel_body(*refs):
            ins, s_ins = refs[:n_in], refs[n_in:n_in + s_in]
            outs = refs[n_in + s_in:n_in + s_in + n_out]
            s_outs = refs[n_in + s_in + n_out:n_in + s_in + n_out + s_out]
            scratch, sems = refs[n_in + s_in + n_out + s_out:-2], refs[-2:]
            first = functools.reduce(jnp.logical_and, [pl.program_id(a) == 0 for a in range(len(grid))])
            last = functools.reduce(jnp.logical_and, [pl.program_id(a) == g - 1 for a, g in enumerate(grid)])

            @pl.when(first)
            def _():
                side.start(s_ins, s_outs, *sems)

            body(*ins, *outs, *scratch)

            @pl.when(last)
            def _():
                side.finish(s_ins, s_outs, *sems)

        in_specs += [ANY_SPEC] * s_in
        out_specs += [ANY_SPEC] * s_out
        out_shape += list(side.out_shapes)
        scratch_shapes = list(scratch_shapes) + [pltpu.SemaphoreType.DMA((side.n,)),
                                                 pltpu.SemaphoreType.DMA((side.n,))]
        args = tuple(args) + side.ins
    if num_scalar_prefetch:
        grid_spec = pltpu.PrefetchScalarGridSpec(num_scalar_prefetch=num_scalar_prefetch, grid=grid,
                                                 in_specs=in_specs, out_specs=out_specs,
                                                 scratch_shapes=list(scratch_shapes))
        outs = pl.pallas_call(kernel_body, name=name, grid_spec=grid_spec, out_shape=out_shape,
                              input_output_aliases=aliases, compiler_params=_cparams())(*args)
    else:
        outs = pl.pallas_call(kernel_body, name=name, grid=grid, in_specs=in_specs, out_specs=out_specs,
                              out_shape=out_shape, scratch_shapes=list(scratch_shapes),
                              input_output_aliases=aliases, compiler_params=_cparams())(*args)
    return tuple(outs[:n_out]), tuple(outs[n_out:])


def _run_exchange(ex, *, name):
    s_in = len(ex.ins)

    def body(*refs):
        ins, outs, sems = refs[:s_in], refs[s_in:-2], refs[-2:]
        ex.start(ins, outs, *sems)
        ex.finish(ins, outs, *sems)

    outs = pl.pallas_call(
        body, name=name, out_shape=list(ex.out_shapes),
        in_specs=[ANY_SPEC] * s_in, out_specs=[ANY_SPEC] * len(ex.out_shapes),
        scratch_shapes=[pltpu.SemaphoreType.DMA((ex.n,)), pltpu.SemaphoreType.DMA((ex.n,))],
        input_output_aliases=ex.aliases,
    )(*ex.ins)
    return tuple(outs)


def _other_chips(px, py):
    return [(px, 1 - py), (1 - px, py), (1 - px, 1 - py)]


def _gather_spread(w_flat):
    rows, w = w_flat.shape
    hr = rows // 2

    def build(ins, outs):
        px, py, pc = _position()
        mine = ins[0].at[pl.ds(pc * hr, hr)]
        me = 4 * px + 2 * py + pc
        plan = [((px, py, 1 - pc), me ^ 1)]
        plan += [((qx, qy, pc), 4 * qx + 2 * qy + pc) for qx, qy in _other_chips(px, py)]
        return [(mine, outs[0].at[me], peer, outs[0].at[their]) for peer, their in plan]

    return _Exchange([w_flat], [jax.ShapeDtypeStruct((N_DEV, hr, w), w_flat.dtype)], 4, build)


def _gather_pass_on(gath):
    def build(ins, outs):
        px, py, pc = _position()
        out = []
        for qx, qy in _other_chips(px, py):
            blk = 4 * qx + 2 * qy + pc
            out.append((outs[0].at[blk], outs[0].at[blk], (px, py, 1 - pc), outs[0].at[blk ^ 1]))
        return out

    return _Exchange([gath], [jax.ShapeDtypeStruct(gath.shape, gath.dtype)], 3, build, aliases={0: 0})


def _rs_halves(g):
    n, rows, w = g.shape
    hr = rows // 2

    def build(ins, outs):
        px, py, pc = _position()
        return [(ins[0].at[:, pl.ds((1 - pc) * hr, hr), :], outs[0], (px, py, 1 - pc), outs[0])]

    return _Exchange([g], [jax.ShapeDtypeStruct((n, hr, w), g.dtype)], 1, build)


def _rs_chips(sb):
    def build(ins, outs):
        px, py, pc = _position()
        return [(ins[0].at[j], outs[0].at[j], (qx, qy, pc), outs[0].at[j])
                for j, (qx, qy) in enumerate(_other_chips(px, py))]

    return _Exchange([sb], [jax.ShapeDtypeStruct(sb.shape, sb.dtype)], 3, build)


def _rs_complete(buf):
    def build(ins, outs):
        px, py, pc = _position()
        return [(outs[0].at[pc], outs[0].at[pc], (px, py, 1 - pc), outs[0].at[1 - pc])]

    return _Exchange([buf], [jax.ShapeDtypeStruct(buf.shape, buf.dtype)], 1, build, aliases={0: 0})


def _rs_partial(g, recv, ids, *, name, tr=128):
    _, rows, w = g.shape
    hr = rows // 2
    nb = hr // tr

    def body(ids_ref, g_ref, r_ref, o_ref):
        o_ref[0] = (g_ref[0] + r_ref[0]).astype(BF16)

    grid_spec = pltpu.PrefetchScalarGridSpec(
        num_scalar_prefetch=1, grid=(3, nb),
        in_specs=[pl.BlockSpec((1, tr, w), lambda j, i, ids: (ids[1] ^ (j + 1), ids[0] * nb + i, 0)),
                  pl.BlockSpec((1, tr, w), lambda j, i, ids: (ids[1] ^ (j + 1), i, 0))],
        out_specs=pl.BlockSpec((1, tr, w), lambda j, i, ids: (j, i, 0)))
    return pl.pallas_call(
        body, name=name, grid_spec=grid_spec,
        out_shape=jax.ShapeDtypeStruct((3, hr, w), BF16),
        compiler_params=_cparams(),
    )(ids, g, recv)


def _rs_total(g, recv, got, ids, *, name, tr=128):
    _, rows, w = g.shape
    hr = rows // 2
    nb = hr // tr

    def body(ids_ref, g_ref, r_ref, got_ref, o_ref):
        acc = g_ref[0] + r_ref[0]
        for j in range(3):
            acc = acc + got_ref[j].astype(F32)
        o_ref[0] = acc

    grid_spec = pltpu.PrefetchScalarGridSpec(
        num_scalar_prefetch=1, grid=(nb,),
        in_specs=[pl.BlockSpec((1, tr, w), lambda i, ids: (ids[1], ids[0] * nb + i, 0)),
                  pl.BlockSpec((1, tr, w), lambda i, ids: (ids[1], i, 0)),
                  pl.BlockSpec((3, tr, w), lambda i, ids: (0, i, 0))],
        out_specs=pl.BlockSpec((1, tr, w), lambda i, ids: (ids[0], i, 0)))
    return pl.pallas_call(
        body, name=name, grid_spec=grid_spec,
        out_shape=jax.ShapeDtypeStruct((2, hr, w), F32),
        compiler_params=_cparams(),
    )(ids, g, recv, got)


class _ReduceScatter:
    def __init__(self, g, ids, tag):
        self.g, self.ids, self.tag, self.stage, self.result = g, ids, tag, 0, None

    def next_exchange(self):
        if self.stage == 0:
            return _rs_halves(self.g)
        if self.stage == 1:
            return _rs_chips(self.sb)
        return _rs_complete(self.buf)

    def done(self, outs):
        if self.stage == 0:
            self.recv = outs[0]
            self.sb = _rs_partial(self.g, self.recv, self.ids, name=f"{self.tag}_partial")
        elif self.stage == 1:
            self.buf = _rs_total(self.g, self.recv, outs[0], self.ids, name=f"{self.tag}_total")
        else:
            _, hr, w = outs[0].shape
            self.result = outs[0].reshape(2 * hr, w)
        self.stage += 1

    def finish_alone(self):
        names = ("halves", "chips", "complete")
        while self.stage < 3:
            self.done(_run_exchange(self.next_exchange(), name=f"{self.tag}_{names[self.stage]}"))
        return self.result


def _flat_rows():
    used = sum(r for _, r in FSDP_SECTIONS)
    return used, -(-used // ROW_ALIGN) * ROW_ALIGN


def _cols_to_chunks(full):
    rows, cols = full.shape
    t = full.reshape(rows, N_CHIPS, cols // N_CHIPS).transpose(1, 0, 2)
    return t.reshape(N_CHIPS, -1, FLAT_W)


def _chunks_to_cols(chunks, rows, cols):
    return chunks.reshape(N_CHIPS, rows, cols // N_CHIPS).transpose(1, 0, 2).reshape(rows, cols)


def _pad_heads(w, real):
    lead = w.shape[:-1]
    t = w.reshape(lead + (HEADS, real))
    t = jnp.pad(t, [(0, 0)] * len(lead) + [(0, 0), (0, HEAD_PAD - real)])
    return t.reshape(lead + (HEADS * HEAD_PAD,))


def _unpad_heads(w, real):
    lead = w.shape[:-1]
    return w.reshape(lead + (HEADS, HEAD_PAD))[..., :real].reshape(lead + (HEADS * real,))


def _pad_value_lanes(w, axis):
    w = jnp.moveaxis(w, axis, -1)
    lead = w.shape[:-1]
    t = w.reshape(lead + (HEADS, 64))
    t = jnp.pad(t, [(0, 0)] * len(lead) + [(0, 0), (HEAD_PAD - 64, 0)])
    return jnp.moveaxis(t.reshape(lead + (HEADS * HEAD_PAD,)), -1, axis)


def _unpad_value_lanes(w, axis):
    w = jnp.moveaxis(w, axis, -1)
    lead = w.shape[:-1]
    t = w.reshape(lead + (HEADS, HEAD_PAD))[..., HEAD_PAD - 64:]
    return jnp.moveaxis(t.reshape(lead + (HEADS * 64,)), -1, axis)


def _pad_w_in(w):
    z = jnp.zeros((w.shape[0], NOPE), w.dtype)
    z2 = jnp.zeros((w.shape[0], HEAD_PAD - NOPE - ROPE), w.dtype)
    return jnp.concatenate([w[:, :1408], z, w[:, 1408:], z2], axis=1)


def _unpad_w_in(w):
    return jnp.concatenate([w[:, :1408], w[:, 1408 + NOPE:1408 + NOPE + ROPE]], axis=1)


def _rope_tables(positions):
    freqs = ROPE_THETA ** (-jnp.arange(0, ROPE, 2, dtype=F32) / ROPE)
    ang = positions.astype(F32)[..., None] * freqs
    cos, sin = jnp.cos(ang), jnp.sin(ang)
    lead = cos.shape[:-1]
    ones = jnp.ones(lead + (NOPE,), F32)
    zeros_n = jnp.zeros(lead + (NOPE,), F32)
    zeros_p = jnp.zeros(lead + (HEAD_PAD - NOPE - ROPE,), F32)
    ctab = jnp.concatenate([ones, cos, cos, zeros_p], axis=-1)
    stab = jnp.concatenate([zeros_n, -sin, sin, zeros_p], axis=-1)
    return ctab, stab


def _layer_weights(full, p, l):
    ws = p["gmlp_ws"][l]
    tril = jnp.tril(jnp.ones((CHUNK, CHUNK), bool))
    bs = p["gmlp_bs"][l]
    bexp = jnp.repeat(bs.reshape(GROUPS // 2, 2, CHUNK).transpose(0, 2, 1), GROUP_DIM, axis=2)
    return dict(
        w_in=_pad_w_in(full["w_in"]),
        w_uq=_pad_heads(full["mla_w_uq"], NOPE + ROPE),
        w_ukv=full["mla_w_ukv"],
        w_out_a=_pad_value_lanes(full["w_out"][D_GMLP:], 0),
        w_out_g=full["w_out"][:D_GMLP],
        ws=ws,
        wst=jnp.where(tril[None], ws, 0.0).transpose(0, 2, 1).astype(BF16),
        bexp=bexp,
        g_mix=p["norm_mix_g"][l][None],
        g_ffn=p["norm_ffn_g"][l][None],
        g_q=p["mla_q_norm_g"][l][None],
        g_kv=p["mla_kv_norm_g"][l][None],
        g_og=p["out_norm_gmlp_g"][l][None],
        g_oa=_pad_value_lanes(p["out_norm_mla_g"][l], 0)[None],
    )


def _local_step(x3, target3, positions, mods, final_g, plan):
    bsz, seq, d = x3.shape
    tok = bsz * seq
    tmt = min(512, seq)
    tmk = min(1024, seq)
    chunk = (None, None, FLAT_W, FLAT_W)
    ff_grad_shape = (N_CHIPS, 2 * FLAT_W, FLAT_W)
    ctab, stab = _rope_tables(positions)
    lw = [None] * DEPTH

    def flat(t):
        return t.reshape(tok, t.shape[-1])

    def cube(t):
        return t.reshape(bsz, seq, t.shape[-1])

    def carrying(l, tag, fn, *args, **kw):
        side = plan.host(l, tag)
        if side is None:
            return fn(*args, **kw)
        res, side_outs = fn(*args, side=side, **kw)
        plan.hosted(l, tag, side_outs)
        return res

    saved = []
    x = x3
    for l in range(DEPTH):
        lw[l] = plan.layer(l)
        w, mod = lw[l], mods[l]
        h1 = _normmod_fwd(x, w["g_mix"], mod, SHIFT1, SCALE1, name=f"l{l}_normmod1")
        z = cube(_mm(flat(h1), w["w_in"], dims="nn", name=f"l{l}_w_in", tm=tmt, tn=D_IN_PAD, tk=d,
                     out_dtypes=(BF16,)))
        yg = _gmlp_fwd(z, w["ws"], w["bexp"], w["g_og"], name=f"l{l}_gmlp_fwd")
        q, kv, kp = _mla_prep_fwd(z, w["g_q"], w["g_kv"], w["w_uq"], w["w_ukv"], ctab, stab, name=f"l{l}_mla_prep")
        o, lse = carrying(l, "fwd_attn", _attn_fwd, q, kv, kp, name=f"l{l}_attn_fwd")
        ya = _onorm_fwd(o, w["g_oa"], name=f"l{l}_onorm_fwd")
        pg = _mm(flat(yg), w["w_out_g"], dims="nn", name=f"l{l}_w_out_g", tm=tmt, tn=d, tk=D_GMLP)

        def out_epi(acc, pgv, xv, gm):
            po = acc + pgv
            return po, xv + gm[0][GATE1:GATE1 + 1, :] * po

        po, x_mid = carrying(l, "fwd_out_a", _mm, flat(ya), w["w_out_a"], dims="nn", name=f"l{l}_w_out_a",
                             tm=tmt, tn=d, tk=d, out_dtypes=(BF16, F32), epilogue=out_epi,
                             extras=(pg, flat(x), mod), extra_specs=(None, None, _mod_spec(tmt, d, seq)))
        x_mid = cube(x_mid)
        h2 = _normmod_fwd(x_mid, w["g_ffn"], mod, SHIFT2, SCALE2, name=f"l{l}_normmod2")

        def act_epi(acc):
            r = jnp.maximum(acc, 0.0)
            return (r * r,)

        r = carrying(l, "fwd_ff1", _mm, flat(h2), w["ff"], dims="nn", name=f"l{l}_w_ff1", tm=tmt, tn=FLAT_W,
                     tk=d, out_dtypes=(BF16,), epilogue=act_epi, weights_outer=True, n=D_FF,
                     b_block=(chunk, lambda i, j, k: (j, 0, 0, 0)))

        def ff2_epi(acc, xv, gm):
            return acc, xv + gm[0][GATE2:GATE2 + 1, :] * acc

        f, x_out = carrying(l, "fwd_ff2", _mm, r, w["ff"], dims="nn", name=f"l{l}_w_ff2", tm=tmk, tn=d, tk=FLAT_W,
                            out_dtypes=(BF16, F32), epilogue=ff2_epi, extras=(flat(x_mid), mod),
                            extra_specs=(None, _mod_spec(tmk, d, seq)), n=d,
                            b_block=(chunk, lambda i, j, k: (k, 1, 0, 0)))
        saved.append(dict(x_in=x, h1=h1, z=z, q=q, kv=kv, kp=kp, o=o, lse=lse, ya=ya, yg=yg, po=cube(po),
                          x_mid=x_mid, h2=h2, r=r, f=cube(f)))
        x = cube(x_out)

    grads = [dict() for _ in range(DEPTH)]
    dmods = [None] * DEPTH
    top = DEPTH - 1
    node = _resnode_bwd(x, final_g[None], name="final_loss_bwd", target3=target3,
                        branch3=saved[top]["f"], mod_gate=mods[top], gate_row=GATE2)
    loss_part = node["loss"][0, 0]
    d_final_g = node["dg"][0]
    for l in range(DEPTH - 1, -1, -1):
        w, mod, s = lw[l], mods[l], saved[l]
        dx_out, dfb, dgate2 = node["dx"], flat(node["dbr"]), node["dgate"][:, 0]

        def dact_epi(acc, rv):
            return (acc * (2.0 * jnp.sqrt(rv.astype(F32))),)

        da = carrying(l, "bwd_d_r", _mm, dfb, w["ff"], dims="nt", name=f"l{l}_d_r", tm=tmt, tn=FLAT_W, tk=d,
                      out_dtypes=(BF16,), epilogue=dact_epi, extras=(s["r"],), weights_outer=True, n=D_FF,
                      b_block=(chunk, lambda i, j, k: (j, 1, 0, 0)))
        g_ff = carrying(l, "bwd_dw_ff2", _mm, s["r"], dfb, dims="tn", name=f"l{l}_dw_ff2", tm=FLAT_W, tn=d,
                        tk=1024, out_into=(ff_grad_shape, (None, FLAT_W, FLAT_W), lambda i, j, k: (i, 1, 0), None))
        g_ff = carrying(l, "bwd_dw_ff1", _mm, flat(s["h2"]), da, dims="tn", name=f"l{l}_dw_ff1", tm=d, tn=FLAT_W,
                        tk=1024, out_into=(ff_grad_shape, (None, FLAT_W, FLAT_W), lambda i, j, k: (j, 0, 0), g_ff))
        plan.ff_grads(l, g_ff)
        dh2 = carrying(l, "bwd_d_h2", _mm, da, w["ff"], dims="nt", name=f"l{l}_d_h2", tm=tmk, tn=d, tk=FLAT_W,
                       n=d, b_block=(chunk, lambda i, j, k: (k, 0, 0, 0)), out_dtypes=(BF16,))
        node = _resnode_bwd(s["x_mid"], w["g_ffn"], name=f"l{l}_resnode_ffn", dh3=cube(dh2), dres3=dx_out,
                            mod_nm=mod, rows=(SHIFT2, SCALE2), branch3=s["po"], mod_gate=mod, gate_row=GATE1)
        grads[l]["norm_ffn_g"] = node["dg"][0]
        dshift2, dscale2 = node["dnm"][:, 0], node["dnm"][:, 1]
        dx_mid, dpo, dgate1 = node["dx"], flat(node["dbr"]), node["dgate"][:, 0]

        dya = _mm(dpo, w["w_out_a"], dims="nt", name=f"l{l}_d_ya", tm=tmt, tn=d, tk=d, out_dtypes=(BF16,))
        dyg = _mm(dpo, w["w_out_g"], dims="nt", name=f"l{l}_d_yg", tm=tmt, tn=D_GMLP, tk=d, out_dtypes=(BF16,))
        dw_out_a = _mm(flat(s["ya"]), dpo, dims="tn", name=f"l{l}_dw_out_a", tm=d, tn=d, tk=1024)
        dw_out_g = _mm(flat(s["yg"]), dpo, dims="tn", name=f"l{l}_dw_out_g", tm=D_GMLP, tn=d, tk=1024)
        grads[l]["w_out"] = jnp.concatenate([dw_out_g, _unpad_value_lanes(dw_out_a, 0)], axis=0)

        duv, dws, dbs, dg_og = _gmlp_bwd(s["z"], cube(dyg), w["ws"], w["wst"], w["bexp"], w["g_og"],
                                         name=f"l{l}_gmlp_bwd")
        grads[l]["gmlp_ws"], grads[l]["gmlp_bs"], grads[l]["out_norm_gmlp_g"] = dws, dbs, dg_og[0]

        do, dl, dg_oa = _onorm_bwd(s["o"], cube(dya), w["g_oa"], name=f"l{l}_onorm_bwd")
        grads[l]["out_norm_mla_g"] = _unpad_value_lanes(dg_oa[0], 0)
        dq, dk, dv = carrying(l, "bwd_attn_dkv", _attn_bwd, s["q"], s["kv"], s["kp"], do, s["lse"], dl,
                              name=f"l{l}_attn_bwd")
        dzm, cq, dqb, ckv, dkvb, dg_q, dg_kv = _mla_prep_bwd(
            s["z"], dq, dk, dv, w["g_q"], w["g_kv"], w["w_uq"], w["w_ukv"], ctab, stab, name=f"l{l}_mla_prep_bwd")
        grads[l]["mla_q_norm_g"], grads[l]["mla_kv_norm_g"] = dg_q[0], dg_kv[0]
        dw_uq = carrying(l, "bwd_dw_uq", _mm, flat(cq), flat(dqb), dims="tn", name=f"l{l}_dw_uq", tm=Q_RANK,
                         tn=1024, tk=1024)
        grads[l]["mla_w_uq"] = _unpad_heads(dw_uq, NOPE + ROPE)
        grads[l]["mla_w_ukv"] = _mm(flat(ckv), flat(dkvb), dims="tn", name=f"l{l}_dw_ukv", tm=KV_RANK, tn=1024, tk=1024)

        h1f = flat(s["h1"])
        dw_in_uv = _mm(h1f, flat(duv), dims="tn", name=f"l{l}_dw_in_uv", tm=d, tn=1024, tk=1024)
        dw_in_m = _mm(h1f, flat(dzm), dims="tn", name=f"l{l}_dw_in_m", tm=d, tn=512, tk=1024)
        grads[l]["w_in"] = _unpad_w_in(jnp.concatenate([dw_in_uv, dw_in_m], axis=1))
        dh1_uv = _mm(flat(duv), w["w_in"][:, :1024], dims="nt", name=f"l{l}_d_h1_uv", tm=tmt, tn=d, tk=1024,
                     out_dtypes=(BF16,))
        dh1 = _mm(flat(dzm), w["w_in"][:, 1024:], dims="nt", name=f"l{l}_d_h1", tm=tmt, tn=d, tk=512,
                  epilogue=lambda acc, prev: (acc + prev,), extras=(dh1_uv,), out_dtypes=(BF16,))
        if l > 0:
            node = _resnode_bwd(s["x_in"], w["g_mix"], name=f"l{l}_resnode_mix", dh3=cube(dh1), dres3=dx_mid,
                                mod_nm=mod, rows=(SHIFT1, SCALE1), branch3=saved[l - 1]["f"],
                                mod_gate=mods[l - 1], gate_row=GATE2)
        else:
            node = _resnode_bwd(s["x_in"], w["g_mix"], name=f"l{l}_resnode_mix", dh3=cube(dh1), dres3=dx_mid,
                                mod_nm=mod, rows=(SHIFT1, SCALE1))
        grads[l]["norm_mix_g"] = node["dg"][0]
        dshift1, dscale1 = node["dnm"][:, 0], node["dnm"][:, 1]
        dmods[l] = jnp.stack([dshift1, dscale1, dgate1, dshift2, dscale2, dgate2], axis=1)
        plan.layer_grads(l, grads[l])
    return loss_part, node["dx"], d_final_g, dmods


W_NAMES = ("w_ada", "b_ada", "norm_mix_g", "w_in", "gmlp_ws", "gmlp_bs", "mla_q_norm_g", "mla_kv_norm_g",
           "mla_w_uq", "mla_w_ukv", "out_norm_gmlp_g", "out_norm_mla_g", "w_out", "norm_ffn_g", "w_ff1", "w_ff2",
           "final_norm_g")
FLAT_KEY = {"w_in": "w_in", "w_uq": "mla_w_uq", "w_ukv": "mla_w_ukv", "w_out": "w_out", "w_ff1": "w_ff1",
            "w_ff2": "w_ff2"}
COL_SHARDED = ("w_in", "w_uq", "w_ukv", "w_ff1")
FULL_SHAPE = {"w_in": (D_MODEL, D_IN), "w_uq": (Q_RANK, HEADS * (NOPE + ROPE)), "w_ukv": (KV_RANK, HEADS * 128),
              "w_out": (D_MODEL, D_MODEL), "w_ff1": (D_MODEL, D_FF), "w_ff2": (D_FF, D_MODEL)}
SMALL_NAMES = ("norm_mix_g", "gmlp_ws", "gmlp_bs", "mla_q_norm_g", "mla_kv_norm_g", "out_norm_gmlp_g",
               "out_norm_mla_g", "norm_ffn_g", "final_norm_g")


def _silu(v):
    return v * (1.0 / (1.0 + jnp.exp(-v)))


class _CommPlan:
    FWD = {"fwd_attn": ("ff", 0, "spread"), "fwd_out_a": ("ff", 0, "pass"),
           "fwd_ff1": ("mix", 1, "spread"), "fwd_ff2": ("mix", 1, "pass")}
    BWD = {"bwd_d_r": ("mix", 1), "bwd_dw_ff2": ("mix", 1), "bwd_dw_ff1": ("mix", 1),
           "bwd_d_h2": ("ff", 0), "bwd_attn_dkv": ("ff", 0), "bwd_dw_uq": ("ff", 0)}

    def __init__(self, weights, ids, dev, core):
        self.weights, self.ids, self.dev, self.core = weights, ids, dev, core
        self.used, self.rows = _flat_rows()
        self.flat = {("mix", l): self._flat_mix(l) for l in range(DEPTH)}
        self.flat.update({("ff", l): jnp.concatenate([weights["w_ff1"][l], weights["w_ff2"][l]], axis=0).astype(BF16)
                          for l in range(DEPTH)})
        self.lw, self.rs, self.grads, self.spread = {}, {}, {}, {}
        (gath,) = _run_exchange(_gather_spread(self.flat["mix", 0]), name="l0_mix_gather_spread")
        (gath,) = _run_exchange(_gather_pass_on(gath), name="l0_mix_gather_pass_on")
        self._arrived("mix", 0, gath)

    def _flat_mix(self, l):
        pieces = [self.weights[FLAT_KEY[nm]][l].reshape(-1, FLAT_W) for nm, _ in FSDP_SECTIONS]
        pieces.append(jnp.zeros((self.rows - self.used, FLAT_W), F32))
        return jnp.concatenate(pieces, axis=0).astype(BF16)

    def _arrived(self, group, l, gath):
        flat = self.flat[group, l]
        hr = flat.shape[0] // 2
        mine = lax.dynamic_slice(flat, (self.core * hr, 0), (hr, FLAT_W))
        gath = lax.dynamic_update_slice(gath, mine[None], (self.dev, 0, 0))
        if group == "ff":
            self.lw[l]["ff"] = gath.reshape(N_CHIPS, 2, hr, FLAT_W)
            return
        w_gath = gath.reshape(N_CHIPS, self.rows, FLAT_W)
        full, off = {}, 0
        for nm, nrows in FSDP_SECTIONS:
            sec = w_gath[:, off:off + nrows]
            off += nrows
            rows, cols = FULL_SHAPE[nm]
            full[FLAT_KEY[nm]] = _chunks_to_cols(sec, rows, cols) if nm in COL_SHARDED else sec.reshape(rows, cols)
        self.lw[l] = _layer_weights(full, self.weights, l)

    def layer(self, l):
        return self.lw[l]

    def host(self, l, tag):
        if tag in self.FWD:
            group, ahead, what = self.FWD[tag]
            if l + ahead >= DEPTH:
                return None
            return _gather_spread(self.flat[group, l + ahead]) if what == "spread" else _gather_pass_on(self.spread[group])
        group, ahead = self.BWD[tag]
        rs = self.rs.get((group, l + ahead))
        return None if rs is None else rs.next_exchange()

    def hosted(self, l, tag, outs):
        if tag in self.FWD:
            group, ahead, what = self.FWD[tag]
            if what == "spread":
                self.spread[group] = outs[0]
            else:
                self._arrived(group, l + ahead, outs[0])
        else:
            group, ahead = self.BWD[tag]
            self.rs[group, l + ahead].done(outs)

    def ff_grads(self, l, g_ff):
        self.rs["ff", l] = _ReduceScatter(g_ff, self.ids, f"l{l}_ff_rs")

    def layer_grads(self, l, grads):
        self.grads[l] = grads
        pieces = []
        for nm, nrows in FSDP_SECTIONS:
            g = grads[FLAT_KEY[nm]]
            pieces.append(_cols_to_chunks(g) if nm in COL_SHARDED else g.reshape(N_CHIPS, nrows, FLAT_W))
        pieces.append(jnp.zeros((N_CHIPS, self.rows - self.used, FLAT_W), F32))
        self.rs["mix", l] = _ReduceScatter(jnp.concatenate(pieces, axis=1), self.ids, f"l{l}_mix_rs")
        if l == 0:
            self.rs["mix", l].finish_alone()

    def mix_grads(self):
        per = {FLAT_KEY[nm]: [] for nm, _ in FSDP_SECTIONS}
        for l in range(DEPTH):
            shard, off = self.rs["mix", l].result, 0
            for nm, nrows in FSDP_SECTIONS:
                key = FLAT_KEY[nm]
                per[key].append(shard[off:off + nrows].reshape(self.weights[key].shape[1:]))
                off += nrows
        return {key: jnp.stack(parts, axis=0) for key, parts in per.items()}

    def ff_shards(self):
        return [self.rs["ff", l].result for l in range(DEPTH)]


def kernel(x, c, positions, w_ada, b_ada, norm_mix_g, w_in, gmlp_ws, gmlp_bs, mla_q_norm_g, mla_kv_norm_g, mla_w_uq, mla_w_ukv, out_norm_gmlp_g, out_norm_mla_g, w_out, norm_ffn_g, w_ff1, w_ff2, final_norm_g, loss_target, m_w_ada, m_b_ada, m_norm_mix_g, m_w_in, m_gmlp_ws, m_gmlp_bs, m_mla_q_norm_g, m_mla_kv_norm_g, m_mla_w_uq, m_mla_w_ukv, m_out_norm_gmlp_g, m_out_norm_mla_g, m_w_out, m_norm_ffn_g, m_w_ff1, m_w_ff2, m_final_norm_g, v_w_ada, v_b_ada, v_norm_mix_g, v_w_in, v_gmlp_ws, v_gmlp_bs, v_mla_q_norm_g, v_mla_kv_norm_g, v_mla_w_uq, v_mla_w_ukv, v_out_norm_gmlp_g, v_out_norm_mla_g, v_w_out, v_norm_ffn_g, v_w_ff1, v_w_ff2, v_final_norm_g):
    weights = dict(w_ada=w_ada, b_ada=b_ada, norm_mix_g=norm_mix_g, w_in=w_in, gmlp_ws=gmlp_ws, gmlp_bs=gmlp_bs,
                   mla_q_norm_g=mla_q_norm_g, mla_kv_norm_g=mla_kv_norm_g, mla_w_uq=mla_w_uq, mla_w_ukv=mla_w_ukv,
                   out_norm_gmlp_g=out_norm_gmlp_g, out_norm_mla_g=out_norm_mla_g, w_out=w_out,
                   norm_ffn_g=norm_ffn_g, w_ff1=w_ff1, w_ff2=w_ff2, final_norm_g=final_norm_g)
    mom_m = dict(zip(W_NAMES, (m_w_ada, m_b_ada, m_norm_mix_g, m_w_in, m_gmlp_ws, m_gmlp_bs, m_mla_q_norm_g,
                               m_mla_kv_norm_g, m_mla_w_uq, m_mla_w_ukv, m_out_norm_gmlp_g, m_out_norm_mla_g,
                               m_w_out, m_norm_ffn_g, m_w_ff1, m_w_ff2, m_final_norm_g)))
    mom_v = dict(zip(W_NAMES, (v_w_ada, v_b_ada, v_norm_mix_g, v_w_in, v_gmlp_ws, v_gmlp_bs, v_mla_q_norm_g,
                               v_mla_kv_norm_g, v_mla_w_uq, v_mla_w_ukv, v_out_norm_gmlp_g, v_out_norm_mla_g,
                               v_w_out, v_norm_ffn_g, v_w_ff1, v_w_ff2, v_final_norm_g)))
    bsz, seq, d = x.shape
    px, py, pc = _position()
    chip = 2 * px + py
    dev = 2 * chip + pc
    ids = jnp.stack([pc, chip]).astype(jnp.int32)
    n_ex = N_DEV * bsz
    ada_cols = w_ada.shape[-1]

    c_all = _allgather8(c.reshape(bsz * d // 128, 128), name="gather_c").reshape(n_ex, d)
    mod_parts = []
    for l in range(DEPTH):
        bias = lax.dynamic_slice(b_ada[l], (chip * ada_cols,), (ada_cols,))[None]
        mod_parts.append(_mm(c_all, w_ada[l], dims="nn", name=f"l{l}_mod", tm=n_ex, tn=ada_cols, tk=d,
                             epilogue=lambda acc, bv: (acc + bv,), extras=(bias,),
                             extra_specs=(pl.BlockSpec((1, ada_cols), lambda i, j, k: (0, j)),), a_fn=_silu))
    mod_g = _allgather8(jnp.concatenate(mod_parts, axis=0), name="gather_mod")
    mod_g = mod_g.reshape(N_CHIPS, 2, DEPTH, n_ex, ada_cols)[:, 0]
    mod_full = mod_g.transpose(1, 2, 0, 3).reshape(DEPTH, n_ex, N_CHIPS * ada_cols)
    mod_mine = lax.dynamic_slice(mod_full, (0, dev * bsz, 0), (DEPTH, bsz, N_MOD * d))
    mod_mine = jnp.pad(mod_mine.reshape(DEPTH, bsz, N_MOD, d), ((0, 0), (0, 0), (0, MOD_ROWS - N_MOD), (0, 0)))
    mods = [mod_mine[l] for l in range(DEPTH)]

    plan = _CommPlan(weights, ids, dev, pc)
    loss_part, grad_x, d_final_g, dmods = _local_step(x, loss_target, positions, mods, final_norm_g, plan)
    grads = plan.grads
    grad = plan.mix_grads()

    small = {nm: (d_final_g if nm == "final_norm_g" else jnp.stack([grads[l][nm] for l in range(DEPTH)], axis=0))
             for nm in SMALL_NAMES}
    svec = jnp.concatenate([small[nm].reshape(-1) for nm in SMALL_NAMES] + [loss_part[None]])
    n_small = svec.shape[0]
    srows = -(-n_small // (8 * FLAT_W)) * 8
    svec = jnp.pad(svec, (0, srows * FLAT_W - n_small)).reshape(srows, FLAT_W)
    ssum = _sum_leading(_allgather8(svec, name="gather_small_grads"), name="sum_small_grads").reshape(-1)
    off = 0
    for nm in SMALL_NAMES:
        size = weights[nm].size
        grad[nm] = ssum[off:off + size].reshape(weights[nm].shape)
        off += size
    loss = ssum[off]

    dmod = jnp.stack(dmods, axis=1).reshape(bsz * DEPTH * N_MOD, d)
    dmod_all = _allgather8(dmod, name="gather_dmod").reshape(n_ex, DEPTH, N_MOD * d)
    gw, gb = [], []
    for l in range(DEPTH):
        dm = dmod_all[:, l]
        dm_cols = lax.dynamic_slice(dm, (0, chip * ada_cols), (n_ex, ada_cols))
        gw.append(_mm(c_all, dm_cols, dims="tn", name=f"l{l}_dw_ada", tm=d, tn=ada_cols, tk=n_ex, a_fn=_silu))
        gb.append(_sum_leading(dm.reshape(n_ex, N_MOD * d // FLAT_W, FLAT_W), name=f"l{l}_db_ada").reshape(-1))
    grad["w_ada"] = jnp.stack(gw, axis=0)
    grad["b_ada"] = jnp.stack(gb, axis=0)

    delta, new_m, new_v = {}, {}, {}
    ff_bufs = plan.ff_shards()
    for nm, row_off in (("w_ff1", 0), ("w_ff2", FLAT_W)):
        grad[nm], delta[nm], new_m[nm], new_v[nm] = _adamw_layers(
            weights[nm], mom_m[nm], mom_v[nm], ff_bufs, row_off, name=f"adamw_{nm}")
    for nm in W_NAMES:
        if nm not in delta:
            delta[nm], new_m[nm], new_v[nm] = _adamw(weights[nm], grad[nm], mom_m[nm], mom_v[nm],
                                                     name=f"adamw_{nm}")
    return (loss, grad_x, *[grad[nm] for nm in W_NAMES], *[delta[nm] for nm in W_NAMES],
            *[new_m[nm] for nm in W_NAMES], *[new_v[nm] for nm in W_NAMES])
```

```python
import functools
import math

import jax
import jax.numpy as jnp
from jax import lax
from jax.experimental import pallas as pl
from jax.experimental.pallas import tpu as pltpu

F32 = jnp.float32
BF16 = jnp.bfloat16

D_MODEL = 1024
DEPTH = 2
D_GMLP = 512
GROUPS = 8
GROUP_DIM = 64
CHUNK = 128
HEADS = 8
NOPE = 64
ROPE = 32
HEAD_PAD = 128
Q_RANK = 256
KV_RANK = 128
D_FF = 4096
N_MOD = 6
MOD_ROWS = 8
EPS = 1e-6
ROPE_THETA = 10000.0
D_IN = 1440
D_IN_PAD = 1536
ATTN_SCALE = (NOPE + ROPE) ** -0.5
LOG2E = math.log2(math.e)
SCALE_LOG2 = ATTN_SCALE * LOG2E
N_CHIPS = 4
N_DEV = 8

ADAM_LR = 0.001
ADAM_B1 = 0.9
ADAM_B2 = 0.999
ADAM_EPS = 1e-08
ADAM_WD = 0.01
ADAM_STEP = 10

VMEM_LIMIT = 48 * 1024 * 1024
FLAT_W = 1024
ROW_ALIGN = 256

NN = (((1,), (0,)), ((), ()))
NT = (((1,), (1,)), ((), ()))
TN = (((0,), (0,)), ((), ()))
MESH = pl.DeviceIdType.MESH

SHIFT1, SCALE1, GATE1, SHIFT2, SCALE2, GATE2 = range(6)

FSDP_SECTIONS = (("w_out", 256), ("w_in", 360), ("w_uq", 48), ("w_ukv", 32))


def _cparams(vmem=VMEM_LIMIT):
    return pltpu.CompilerParams(vmem_limit_bytes=vmem)


def _dot(a, b, dims=NN):
    return lax.dot_general(a, b, dims, preferred_element_type=F32)


def _iota(shape, axis):
    return lax.broadcasted_iota(jnp.int32, shape, axis)


def _gelu(x):
    k = math.sqrt(2.0 / math.pi)
    return 0.5 * x * (1.0 + jnp.tanh(k * (x + 0.044715 * (x * x * x))))


def _gelu_grad(x):
    k = math.sqrt(2.0 / math.pi)
    t = jnp.tanh(k * (x + 0.044715 * (x * x * x)))
    return 0.5 * (1.0 + t) + 0.5 * x * (1.0 - t * t) * (k * (1.0 + 3.0 * 0.044715 * (x * x)))


def _rms_fwd(x, g, n):
    r = lax.rsqrt(jnp.sum(x * x, axis=-1, keepdims=True) * (1.0 / n) + EPS)
    return x * r * g


def _rms_bwd(x, g, dy, n):
    r = lax.rsqrt(jnp.sum(x * x, axis=-1, keepdims=True) * (1.0 / n) + EPS)
    xh = x * r
    dxh = dy * g
    dx = r * (dxh - xh * (jnp.sum(dxh * xh, axis=-1, keepdims=True) * (1.0 / n)))
    dg = jnp.sum(dy * xh, axis=0, keepdims=True)
    return dx, dg


def _pick_rows(rows, limit):
    if rows <= limit:
        return rows
    for t in range(limit, 7, -8):
        if rows % t == 0:
            return t
    return rows


def _mm(a, b, *, dims, name, tm=512, tn=1024, tk=1024, out_dtypes=(F32,), epilogue=None,
        extras=(), extra_specs=(), a_fn=None, weights_outer=False, side=None, b_block=None, n=None,
        out_into=None):
    if dims == "tn":
        kk, m = a.shape
    else:
        m, kk = a.shape
    if n is None:
        n = b.shape[0] if dims == "nt" else b.shape[1]
    tm, tn, tk = min(tm, m), min(tn, n), min(tk, kk)
    assert m % tm == 0 and n % tn == 0 and kk % tk == 0, (name, a.shape, b.shape, tm, tn, tk)
    ni, nj, nk = m // tm, n // tn, kk // tk

    def spec(shape, pick):
        if weights_outer:
            return pl.BlockSpec(shape, lambda j, i, k: pick(i, j, k))
        return pl.BlockSpec(shape, pick)

    if dims == "tn":
        a_spec = spec((tk, tm), lambda i, j, k: (k, i))
    else:
        a_spec = spec((tm, tk), lambda i, j, k: (i, k))
    if b_block is not None:
        b_spec = spec(*b_block)
    elif dims == "nt":
        b_spec = spec((tn, tk), lambda i, j, k: (j, k))
    else:
        b_spec = spec((tk, tn), lambda i, j, k: (k, j))
    o_spec = spec((tm, tn), lambda i, j, k: (i, j))
    out_shape = [jax.ShapeDtypeStruct((m, n), dt) for dt in out_dtypes]
    out_specs = [o_spec] * len(out_dtypes)
    prev, io_aliases = (), {}
    if out_into is not None:
        full_shape, block, index, before = out_into
        assert len(out_dtypes) == 1 and not extras
        out_shape = [jax.ShapeDtypeStruct(full_shape, out_dtypes[0])]
        out_specs = [spec(block, index)]
        if before is not None:
            prev, io_aliases = (before,), {2: 0}
    assert not (weights_outer and extra_specs)
    dn = {"nn": NN, "nt": NT, "tn": TN}[dims]
    n_ex, n_out = len(extras), len(out_dtypes)
    e_specs = [o_spec if s is None else s for s in (tuple(extra_specs) + (None,) * n_ex)[:n_ex]]

    n_prev = len(prev)

    def body(*refs):
        a_ref, b_ref = refs[0], refs[1]
        e_refs = refs[2 + n_prev:2 + n_prev + n_ex]
        o_refs = refs[2 + n_prev + n_ex:2 + n_prev + n_ex + n_out]
        av = a_ref[...]
        if a_fn is not None:
            av = a_fn(av)
        part = _dot(av.astype(BF16), b_ref[...].astype(BF16), dn)

        def finish(acc):
            outs = (acc,) if epilogue is None else epilogue(acc, *[e[...] for e in e_refs])
            for o_ref, o in zip(o_refs, outs):
                o_ref[...] = o.astype(o_ref.dtype)

        if nk == 1:
            finish(part)
        else:
            acc_ref = refs[-1]
            k = pl.program_id(2)

            @pl.when(k == 0)
            def _():
                acc_ref[...] = part

            @pl.when(k > 0)
            def _():
                acc_ref[...] += part

            @pl.when(k == nk - 1)
            def _():
                finish(acc_ref[...])

    outs, side_outs = _hosted_call(
        body, name=name, grid=(nj, ni, nk) if weights_outer else (ni, nj, nk),
        in_specs=[a_spec, b_spec] + [ANY_SPEC] * n_prev + e_specs,
        out_specs=out_specs, out_shape=out_shape,
        scratch_shapes=[pltpu.VMEM((tm, tn), F32)] if nk > 1 else [],
        args=(a, b, *prev, *extras), side=side, io_aliases=io_aliases)
    res = outs[0] if n_out == 1 else outs
    return res if side is None else (res, side_outs)


def _mod_spec(tm, tn, seq):
    return pl.BlockSpec((1, MOD_ROWS, tn), lambda i, j, k: ((i * tm) // seq, 0, j))


def _normmod_fwd(x3, g, mod, shift_row, scale_row, *, name, tb=256):
    bsz, seq, d = x3.shape
    tb = min(tb, seq)

    def body(x_ref, g_ref, mod_ref, h_ref):
        m = mod_ref[0]
        nrm = _rms_fwd(x_ref[0], g_ref[...], d)
        h = nrm * (1.0 + m[scale_row:scale_row + 1, :]) + m[shift_row:shift_row + 1, :]
        h_ref[0] = h.astype(BF16)

    return pl.pallas_call(
        body, name=name, grid=(bsz, seq // tb),
        in_specs=[pl.BlockSpec((1, tb, d), lambda b, i: (b, i, 0)),
                  pl.BlockSpec((1, d), lambda b, i: (0, 0)),
                  pl.BlockSpec((1, MOD_ROWS, d), lambda b, i: (b, 0, 0))],
        out_specs=pl.BlockSpec((1, tb, d), lambda b, i: (b, i, 0)),
        out_shape=jax.ShapeDtypeStruct((bsz, seq, d), BF16),
        compiler_params=_cparams(),
    )(x3, g, mod)


def _pair_mean_exact(x, lo):
    s_lo = jnp.sum(jnp.where(lo, x, 0.0), axis=-1, keepdims=True)
    s_hi = jnp.sum(jnp.where(lo, 0.0, x), axis=-1, keepdims=True)
    return jnp.where(lo, s_lo, s_hi) * (1.0 / GROUP_DIM)


def _gmlp_pair_fwd(gv_p, w0, w1, bias, lo):
    mu = _pair_mean_exact(gv_p, lo)
    dlt = gv_p - mu
    var = _pair_mean_exact(dlt * dlt, lo)
    rstd = lax.rsqrt(var + EPS)
    vn = dlt * rstd
    vnb = vn.astype(BF16)
    mixed = jnp.where(lo, _dot(w0, vnb), _dot(w1, vnb)) + bias
    return vn, vnb, rstd, mixed


def _tril_bf16(w):
    t = w.shape[-1]
    return jnp.where(_iota((t, t), 1) <= _iota((t, t), 0), w, 0.0).astype(BF16)


def _gmlp_fwd(z3, ws, bexp, g_out, *, name):
    bsz, seq, _ = z3.shape
    nc = seq // CHUNK

    def body(u_ref, v_ref, ws_ref, b_ref, g_ref, y_ref):
        lo = _iota((CHUNK, 128), 1) < GROUP_DIM
        gu = _gelu(u_ref[0].astype(F32))
        gv = _gelu(v_ref[0].astype(F32))
        parts = []
        for p in range(GROUPS // 2):
            sl = slice(128 * p, 128 * p + 128)
            w0 = _tril_bf16(ws_ref[2 * p])
            w1 = _tril_bf16(ws_ref[2 * p + 1])
            _, _, _, mixed = _gmlp_pair_fwd(gv[:, sl], w0, w1, b_ref[p], lo)
            parts.append(gu[:, sl] * mixed)
        yg = jnp.concatenate(parts, axis=1)
        y_ref[0] = _rms_fwd(yg, g_ref[...], D_GMLP).astype(BF16)

    return pl.pallas_call(
        body, name=name, grid=(bsz, nc),
        in_specs=[pl.BlockSpec((1, CHUNK, D_GMLP), lambda b, i: (b, i, 0)),
                  pl.BlockSpec((1, CHUNK, D_GMLP), lambda b, i: (b, i, 1)),
                  pl.BlockSpec((GROUPS, CHUNK, CHUNK), lambda b, i: (0, 0, 0)),
                  pl.BlockSpec((GROUPS // 2, CHUNK, 128), lambda b, i: (0, 0, 0)),
                  pl.BlockSpec((1, D_GMLP), lambda b, i: (0, 0))],
        out_specs=pl.BlockSpec((1, CHUNK, D_GMLP), lambda b, i: (b, i, 0)),
        out_shape=jax.ShapeDtypeStruct((bsz, seq, D_GMLP), BF16),
        compiler_params=_cparams(),
    )(z3, z3, ws, bexp, g_out)


def _gmlp_bwd(z3, dyn3, ws, wst, bexp, g_out, *, name, dy_col):
    bsz, seq, _ = z3.shape
    nc = seq // CHUNK
    npair = GROUPS // 2

    def body(u_ref, v_ref, dy_ref, ws_ref, wst_ref, b_ref, g_ref, duv_ref, dws_ref, dbs_ref, dg_ref, dbacc):
        first = jnp.logical_and(pl.program_id(0) == 0, pl.program_id(1) == 0)
        last = jnp.logical_and(pl.program_id(0) == bsz - 1, pl.program_id(1) == nc - 1)

        @pl.when(first)
        def _():
            dws_ref[...] = jnp.zeros_like(dws_ref)
            dg_ref[...] = jnp.zeros_like(dg_ref)
            dbacc[...] = jnp.zeros_like(dbacc)

        lo = _iota((CHUNK, 128), 1) < GROUP_DIM
        tril = _iota((CHUNK, CHUNK), 1) <= _iota((CHUNK, CHUNK), 0)
        u = u_ref[0].astype(F32)
        v = v_ref[0].astype(F32)
        gu = _gelu(u)
        gv = _gelu(v)
        fwd = []
        for p in range(npair):
            sl = slice(128 * p, 128 * p + 128)
            w0 = _tril_bf16(ws_ref[2 * p])
            w1 = _tril_bf16(ws_ref[2 * p + 1])
            fwd.append(_gmlp_pair_fwd(gv[:, sl], w0, w1, b_ref[p], lo))
        yg = jnp.concatenate([gu[:, 128 * p:128 * p + 128] * fwd[p][3] for p in range(npair)], axis=1)
        dyg, dg = _rms_bwd(yg, g_ref[...], dy_ref[0].astype(F32), D_GMLP)
        dg_ref[...] += dg
        du_parts, dv_parts = [], []
        for p in range(npair):
            sl = slice(128 * p, 128 * p + 128)
            vn, vnb, rstd, mixed = fwd[p]
            dyg_p = dyg[:, sl]
            dmixed = dyg_p * gu[:, sl]
            dbacc[p] += dmixed
            dm_lo = jnp.where(lo, dmixed, 0.0).astype(BF16)
            dm_hi = jnp.where(lo, 0.0, dmixed).astype(BF16)
            dws_ref[2 * p] += jnp.where(tril, _dot(dm_lo, vnb, NT), 0.0)
            dws_ref[2 * p + 1] += jnp.where(tril, _dot(dm_hi, vnb, NT), 0.0)
            dmb = dmixed.astype(BF16)
            dvn = jnp.where(lo, _dot(wst_ref[2 * p], dmb), _dot(wst_ref[2 * p + 1], dmb))
            dgv = rstd * (dvn - _pair_mean_exact(dvn, lo) - vn * _pair_mean_exact(dvn * vn, lo))
            dv_parts.append(dgv * _gelu_grad(v[:, sl]))
            du_parts.append(dyg_p * mixed * _gelu_grad(u[:, sl]))
        duv_ref[0] = jnp.concatenate(du_parts + dv_parts, axis=1).astype(BF16)

        @pl.when(last)
        def _():
            sel = jnp.where(_iota((8, 128), 0) == 0, (_iota((8, 128), 1) < GROUP_DIM).astype(F32),
                            jnp.where(_iota((8, 128), 0) == 1, (_iota((8, 128), 1) >= GROUP_DIM).astype(F32), 0.0))
            for p in range(npair):
                dbs_ref[p] = lax.dot_general(sel, dbacc[p], NT, precision=lax.Precision.HIGHEST,
                                             preferred_element_type=F32)

    duv, dws, dbs, dg = pl.pallas_call(
        body, name=name, grid=(bsz, nc),
        in_specs=[pl.BlockSpec((1, CHUNK, D_GMLP), lambda b, i: (b, i, 0)),
                  pl.BlockSpec((1, CHUNK, D_GMLP), lambda b, i: (b, i, 1)),
                  pl.BlockSpec((1, CHUNK, D_GMLP), lambda b, i: (b, i, dy_col)),
                  pl.BlockSpec((GROUPS, CHUNK, CHUNK), lambda b, i: (0, 0, 0)),
                  pl.BlockSpec((GROUPS, CHUNK, CHUNK), lambda b, i: (0, 0, 0)),
                  pl.BlockSpec((npair, CHUNK, 128), lambda b, i: (0, 0, 0)),
                  pl.BlockSpec((1, D_GMLP), lambda b, i: (0, 0))],
        out_specs=[pl.BlockSpec((1, CHUNK, 2 * D_GMLP), lambda b, i: (b, i, 0)),
                   pl.BlockSpec((GROUPS, CHUNK, CHUNK), lambda b, i: (0, 0, 0)),
                   pl.BlockSpec((npair, 8, CHUNK), lambda b, i: (0, 0, 0)),
                   pl.BlockSpec((1, D_GMLP), lambda b, i: (0, 0))],
        out_shape=[jax.ShapeDtypeStruct((bsz, seq, D_IN_PAD), BF16),
                   jax.ShapeDtypeStruct((GROUPS, CHUNK, CHUNK), F32),
                   jax.ShapeDtypeStruct((npair, 8, CHUNK), F32),
                   jax.ShapeDtypeStruct((1, D_GMLP), F32)],
        scratch_shapes=[pltpu.VMEM((npair, CHUNK, 128), F32)],
        compiler_params=_cparams(),
    )(z3, z3, dyn3, ws, wst, bexp, g_out)
    return duv, dws, dbs[:, :2, :].reshape(GROUPS, CHUNK), dg


def _partner(x):
    width = x.shape[-1]
    lane = _iota(x.shape, x.ndim - 1) % HEAD_PAD
    up = pltpu.roll(x, width - ROPE // 2, x.ndim - 1)
    down = pltpu.roll(x, ROPE // 2, x.ndim - 1)
    first = jnp.logical_and(lane >= NOPE, lane < NOPE + ROPE // 2)
    second = jnp.logical_and(lane >= NOPE + ROPE // 2, lane < NOPE + ROPE)
    return jnp.where(first, up, jnp.where(second, down, 0.0))


def _mla_prep_fwd(z3, g_q, g_kv, w_uq, w_ukv, ctab, stab, *, name, tb=256):
    bsz, seq, _ = z3.shape
    tb = min(tb, seq)
    hw = HEADS * HEAD_PAD

    def body(ql_ref, kvl_ref, krl_ref, gq_ref, gkv_ref, wuq_ref, wukv_ref, c_ref, s_ref, q_ref, kv_ref, kp_ref):
        cq = _rms_fwd(ql_ref[0].astype(F32), gq_ref[...], Q_RANK).astype(BF16)
        q = _dot(cq, wuq_ref[...])
        c1, s1 = c_ref[0], s_ref[0]
        c8, s8 = jnp.tile(c1, (1, HEADS)), jnp.tile(s1, (1, HEADS))
        q_ref[0] = (q * c8 + _partner(q) * s8).astype(BF16)
        ckv = _rms_fwd(kvl_ref[0].astype(F32), gkv_ref[...], KV_RANK).astype(BF16)
        kv = _dot(ckv, wukv_ref[...])
        kv_ref[0] = kv.astype(BF16)
        kr = krl_ref[0].astype(F32)
        kr = kr * c1 + _partner(kr) * s1
        lane = _iota((tb, hw), 1) % HEAD_PAD
        kp_ref[0] = jnp.where(lane < NOPE, kv, jnp.tile(kr, (1, HEADS))).astype(BF16)

    return pl.pallas_call(
        body, name=name, grid=(bsz, seq // tb),
        in_specs=[pl.BlockSpec((1, tb, Q_RANK), lambda b, i: (b, i, 4)),
                  pl.BlockSpec((1, tb, KV_RANK), lambda b, i: (b, i, 10)),
                  pl.BlockSpec((1, tb, HEAD_PAD), lambda b, i: (b, i, 11)),
                  pl.BlockSpec((1, Q_RANK), lambda b, i: (0, 0)),
                  pl.BlockSpec((1, KV_RANK), lambda b, i: (0, 0)),
                  pl.BlockSpec((Q_RANK, hw), lambda b, i: (0, 0)),
                  pl.BlockSpec((KV_RANK, hw), lambda b, i: (0, 0)),
                  pl.BlockSpec((1, tb, HEAD_PAD), lambda b, i: (b, i, 0)),
                  pl.BlockSpec((1, tb, HEAD_PAD), lambda b, i: (b, i, 0))],
        out_specs=[pl.BlockSpec((1, tb, hw), lambda b, i: (b, i, 0))] * 3,
        out_shape=[jax.ShapeDtypeStruct((bsz, seq, hw), BF16)] * 3,
        compiler_params=_cparams(),
    )(z3, z3, z3, g_q, g_kv, w_uq, w_ukv, ctab, stab)


def _mla_prep_bwd(z3, dz3, dq3, dk3, dv3, g_q, g_kv, w_uq, w_ukv, ctab, stab, *, name, tb=256):
    bsz, seq, _ = z3.shape
    tb = min(tb, seq)
    hw = HEADS * HEAD_PAD
    nb = seq // tb

    def body(ql_ref, kvl_ref, dq_ref, dk_ref, dv_ref, gq_ref, gkv_ref, wuq_ref, wukv_ref, c_ref, s_ref, dz_in,
             dz_ref, cq_ref, dqb_ref, ckv_ref, dkvb_ref, dgq_ref, dgkv_ref):
        @pl.when(jnp.logical_and(pl.program_id(0) == 0, pl.program_id(1) == 0))
        def _():
            dgq_ref[...] = jnp.zeros_like(dgq_ref)
            dgkv_ref[...] = jnp.zeros_like(dgkv_ref)

        c1, s1 = c_ref[0], s_ref[0]
        c8, s8 = jnp.tile(c1, (1, HEADS)), jnp.tile(s1, (1, HEADS))
        dqr = dq_ref[0]
        dqb = (dqr * c8 + _partner(dqr * s8)).astype(BF16)
        dqb_ref[0] = dqb
        ql = ql_ref[0].astype(F32)
        cq_ref[0] = _rms_fwd(ql, gq_ref[...], Q_RANK).astype(BF16)
        dql, dgq = _rms_bwd(ql, gq_ref[...], _dot(dqb, wuq_ref[...], NT), Q_RANK)
        dgq_ref[...] += dgq

        dk = dk_ref[0]
        lane = _iota((tb, hw), 1) % HEAD_PAD
        dkvb = jnp.where(lane < NOPE, dk, dv_ref[0]).astype(BF16)
        dkvb_ref[0] = dkvb
        kvl = kvl_ref[0].astype(F32)
        ckv_ref[0] = _rms_fwd(kvl, gkv_ref[...], KV_RANK).astype(BF16)
        dkvl, dgkv = _rms_bwd(kvl, gkv_ref[...], _dot(dkvb, wukv_ref[...], NT), KV_RANK)
        dgkv_ref[...] += dgkv

        dkr = dk[:, 0:HEAD_PAD].astype(F32)
        for h in range(1, HEADS):
            dkr = dkr + dk[:, HEAD_PAD * h:HEAD_PAD * (h + 1)].astype(F32)
        lane1 = _iota((tb, HEAD_PAD), 1)
        dkr = jnp.where(jnp.logical_and(lane1 >= NOPE, lane1 < NOPE + ROPE), dkr, 0.0)
        dkrl = dkr * c1 + _partner(dkr * s1)
        dz_ref[0] = jnp.concatenate([dql, dkvl, dkrl], axis=1).astype(BF16)

    return pl.pallas_call(
        body, name=name, grid=(bsz, nb),
        in_specs=[pl.BlockSpec((1, tb, Q_RANK), lambda b, i: (b, i, 4)),
                  pl.BlockSpec((1, tb, KV_RANK), lambda b, i: (b, i, 10)),
                  pl.BlockSpec((1, tb, hw), lambda b, i: (b, i, 0)),
                  pl.BlockSpec((1, tb, hw), lambda b, i: (b, i, 0)),
                  pl.BlockSpec((1, tb, hw), lambda b, i: (b, i, 0)),
                  pl.BlockSpec((1, Q_RANK), lambda b, i: (0, 0)),
                  pl.BlockSpec((1, KV_RANK), lambda b, i: (0, 0)),
                  pl.BlockSpec((Q_RANK, hw), lambda b, i: (0, 0)),
                  pl.BlockSpec((KV_RANK, hw), lambda b, i: (0, 0)),
                  pl.BlockSpec((1, tb, HEAD_PAD), lambda b, i: (b, i, 0)),
                  pl.BlockSpec((1, tb, HEAD_PAD), lambda b, i: (b, i, 0)),
                  ANY_SPEC],
        out_specs=[pl.BlockSpec((1, tb, 512), lambda b, i: (b, i, 2)),
                   pl.BlockSpec((1, tb, Q_RANK), lambda b, i: (b, i, 0)),
                   pl.BlockSpec((1, tb, hw), lambda b, i: (b, i, 0)),
                   pl.BlockSpec((1, tb, KV_RANK), lambda b, i: (b, i, 0)),
                   pl.BlockSpec((1, tb, hw), lambda b, i: (b, i, 0)),
                   pl.BlockSpec((1, Q_RANK), lambda b, i: (0, 0)),
                   pl.BlockSpec((1, KV_RANK), lambda b, i: (0, 0))],
        out_shape=[jax.ShapeDtypeStruct((bsz, seq, D_IN_PAD), BF16),
                   jax.ShapeDtypeStruct((bsz, seq, Q_RANK), BF16),
                   jax.ShapeDtypeStruct((bsz, seq, hw), BF16),
                   jax.ShapeDtypeStruct((bsz, seq, KV_RANK), BF16),
                   jax.ShapeDtypeStruct((bsz, seq, hw), BF16),
                   jax.ShapeDtypeStruct((1, Q_RANK), F32),
                   jax.ShapeDtypeStruct((1, KV_RANK), F32)],
        input_output_aliases={11: 0},
        compiler_params=_cparams(),
    )(z3, z3, dq3, dk3, dv3, g_q, g_kv, w_uq, w_ukv, ctab, stab, dz3)


ATTN_HEADS_PER_STEP = 2


def _attn_specs(tq, seq, hp):
    blk = pl.BlockSpec((1, tq, hp * HEAD_PAD), lambda b, h, i: (b, i, h))
    full = pl.BlockSpec((1, seq, hp * HEAD_PAD), lambda b, h, i: (b, 0, h))
    return blk, full


def _head(h):
    return slice(HEAD_PAD * h, HEAD_PAD * (h + 1))


def _attn_fwd(q3, kv3, kp3, *, name, tq=512, hp=ATTN_HEADS_PER_STEP, side=None):
    bsz, seq, hw = q3.shape
    tq = min(tq, seq)
    blk, full = _attn_specs(tq, seq, hp)

    def body(q_ref, kv_ref, kp_ref, o_ref, lse_ref):
        i = pl.program_id(2)
        is_nope = _iota((tq, HEAD_PAD), 1) < NOPE
        causal = _iota((tq, tq), 1) <= _iota((tq, tq), 0)

        def step(j, carry, diag):
            st = pl.multiple_of(j * tq, tq)
            out = []
            for h in range(hp):
                m, l, acc = carry[h]
                kvj = kv_ref[0, pl.ds(st, tq), _head(h)]
                s = _dot(q_ref[0, :, _head(h)], kp_ref[0, pl.ds(st, tq), _head(h)], NT) * SCALE_LOG2
                if diag:
                    s = jnp.where(causal, s, -1e30)
                m_new = jnp.maximum(m, jnp.max(s, axis=1, keepdims=True))
                alpha = jnp.exp2(m - m_new)
                p = jnp.exp2(s - m_new)
                l = alpha * l + jnp.sum(p, axis=1, keepdims=True)
                acc = alpha * acc + _dot(p.astype(BF16), kvj)
                out.append((m_new, l, acc))
            return tuple(out)

        init = tuple((jnp.full((tq, 1), -1e30, F32), jnp.zeros((tq, 1), F32), jnp.zeros((tq, HEAD_PAD), F32))
                     for _ in range(hp))
        carry = lax.fori_loop(0, i, lambda j, c: step(j, c, False), init)
        carry = step(i, carry, True)
        for h in range(hp):
            m, l, acc = carry[h]
            o_ref[0, :, _head(h)] = jnp.where(is_nope, 0.0, acc / l).astype(BF16)
            lse_ref[0, :, _head(h)] = jnp.broadcast_to(m + jnp.log(l) * LOG2E, (tq, HEAD_PAD))

    outs, side_outs = _hosted_call(
        body, name=name, grid=(bsz, HEADS // hp, seq // tq),
        in_specs=[blk, full, full],
        out_specs=[blk, blk],
        out_shape=[jax.ShapeDtypeStruct((bsz, seq, hw), BF16), jax.ShapeDtypeStruct((bsz, seq, hw), F32)],
        args=(q3, kv3, kp3), side=side)
    return outs if side is None else (outs, side_outs)


def _attn_bwd(q3, kv3, kp3, do3, lse3, dl3, *, name, tq=512, hp=ATTN_HEADS_PER_STEP, side=None):
    bsz, seq, hw = q3.shape
    tq = min(tq, seq)
    nq = seq // tq
    blk, full = _attn_specs(tq, seq, hp)
    rep = tq // HEAD_PAD

    def body(kv_ref, kp_ref, q_ref, do_ref, lse_ref, dl_ref, dq_ref, dk_ref, dv_ref):
        j = pl.program_id(2)
        causal = _iota((tq, tq), 1) <= _iota((tq, tq), 0)

        @pl.when(j == 0)
        def _():
            dq_ref[...] = jnp.zeros_like(dq_ref)

        def step(i, carry, diag):
            st = pl.multiple_of(i * tq, tq)
            out = []
            for h in range(hp):
                dk, dv = carry[h]
                qi = q_ref[0, pl.ds(st, tq), _head(h)]
                do = do_ref[0, pl.ds(st, tq), _head(h)]
                kp = kp_ref[0, :, _head(h)]
                s = _dot(qi, kp, NT) * SCALE_LOG2
                if diag:
                    s = jnp.where(causal, s, -1e30)
                p = jnp.exp2(s - jnp.tile(lse_ref[0, pl.ds(st, tq), _head(h)], (1, rep)))
                dv = dv + _dot(p.astype(BF16), do, TN)
                dp = _dot(do, kv_ref[0, :, _head(h)], NT)
                ds = (p * (dp - jnp.tile(dl_ref[0, pl.ds(st, tq), _head(h)], (1, rep)))).astype(BF16)
                dk = dk + _dot(ds, qi, TN)
                dq_ref[0, pl.ds(st, tq), _head(h)] += _dot(ds, kp)
                out.append((dk, dv))
            return tuple(out)

        zero = jnp.zeros((tq, HEAD_PAD), F32)
        carry = step(j, tuple((zero, zero) for _ in range(hp)), True)
        carry = lax.fori_loop(j + 1, nq, lambda i, c: step(i, c, False), carry)
        for h in range(hp):
            dk_ref[0, :, _head(h)] = (carry[h][0] * ATTN_SCALE).astype(BF16)
            dv_ref[0, :, _head(h)] = carry[h][1].astype(BF16)

        @pl.when(j == nq - 1)
        def _():
            dq_ref[...] = dq_ref[...] * ATTN_SCALE

    outs, side_outs = _hosted_call(
        body, name=name, grid=(bsz, HEADS // hp, nq),
        in_specs=[blk, blk, full, full, full, full],
        out_specs=[full, blk, blk],
        out_shape=[jax.ShapeDtypeStruct((bsz, seq, hw), F32)] + [jax.ShapeDtypeStruct((bsz, seq, hw), BF16)] * 2,
        args=(kv3, kp3, q3, do3, lse3, dl3), side=side)
    return outs if side is None else (outs, side_outs)


def _onorm_fwd(o3, yg3, g_pad, *, name, tb=256):
    bsz, seq, hw = o3.shape
    wg = yg3.shape[-1]
    tb = min(tb, seq)

    def body(o_ref, yg_ref, g_ref, y_ref):
        ya = _rms_fwd(o_ref[0].astype(F32), g_ref[...], HEADS * 64).astype(BF16)
        y_ref[0] = jnp.concatenate([ya, yg_ref[0]], axis=1)

    return pl.pallas_call(
        body, name=name, grid=(bsz, seq // tb),
        in_specs=[pl.BlockSpec((1, tb, hw), lambda b, i: (b, i, 0)),
                  pl.BlockSpec((1, tb, wg), lambda b, i: (b, i, 0)),
                  pl.BlockSpec((1, hw), lambda b, i: (0, 0))],
        out_specs=pl.BlockSpec((1, tb, hw + wg), lambda b, i: (b, i, 0)),
        out_shape=jax.ShapeDtypeStruct((bsz, seq, hw + wg), BF16),
        compiler_params=_cparams(),
    )(o3, yg3, g_pad)


def _onorm_bwd(o3, dy3, g_pad, *, name, tb=256):
    bsz, seq, hw = o3.shape
    tb = min(tb, seq)

    def body(o_ref, dy_ref, g_ref, do_ref, dl_ref, dg_ref):
        @pl.when(jnp.logical_and(pl.program_id(0) == 0, pl.program_id(1) == 0))
        def _():
            dg_ref[...] = jnp.zeros_like(dg_ref)

        o = o_ref[0].astype(F32)
        do, dg = _rms_bwd(o, g_ref[...], dy_ref[0].astype(F32), HEADS * 64)
        dg_ref[...] += dg
        do_ref[0] = do.astype(BF16)
        prod = do * o
        parts = []
        for h in range(HEADS):
            sh = jnp.sum(prod[:, HEAD_PAD * h:HEAD_PAD * (h + 1)], axis=1, keepdims=True)
            parts.append(jnp.broadcast_to(sh, (tb, HEAD_PAD)))
        dl_ref[0] = jnp.concatenate(parts, axis=1)

    return pl.pallas_call(
        body, name=name, grid=(bsz, seq // tb),
        in_specs=[pl.BlockSpec((1, tb, hw), lambda b, i: (b, i, 0)),
                  pl.BlockSpec((1, tb, hw), lambda b, i: (b, i, 0)),
                  pl.BlockSpec((1, hw), lambda b, i: (0, 0))],
        out_specs=[pl.BlockSpec((1, tb, hw), lambda b, i: (b, i, 0)),
                   pl.BlockSpec((1, tb, hw), lambda b, i: (b, i, 0)),
                   pl.BlockSpec((1, hw), lambda b, i: (0, 0))],
        out_shape=[jax.ShapeDtypeStruct((bsz, seq, hw), BF16),
                   jax.ShapeDtypeStruct((bsz, seq, hw), F32),
                   jax.ShapeDtypeStruct((1, hw), F32)],
        compiler_params=_cparams(),
    )(o3, dy3, g_pad)


def _resnode_bwd(x3, g, *, name, target3=None, dh3=None, dres3=None, mod_nm=None, rows=None,
                 branch3=None, mod_gate=None, gate_row=None, tb=256):
    bsz, seq, d = x3.shape
    tb = min(tb, seq)
    final = target3 is not None
    has_branch = branch3 is not None
    row_spec = pl.BlockSpec((1, tb, d), lambda b, i: (b, i, 0))
    vec_spec = pl.BlockSpec((1, d), lambda b, i: (0, 0))
    mod_spec = pl.BlockSpec((1, MOD_ROWS, d), lambda b, i: (b, 0, 0))

    ins, in_specs = [x3, g], [row_spec, vec_spec]
    if final:
        ins += [target3]
        in_specs += [row_spec]
    else:
        ins += [dh3, dres3, mod_nm]
        in_specs += [row_spec, row_spec, mod_spec]
    if has_branch:
        ins += [branch3, mod_gate]
        in_specs += [row_spec, mod_spec]

    out_names = ["dx", "dg"]
    out_specs = [row_spec, vec_spec]
    out_shape = [jax.ShapeDtypeStruct((bsz, seq, d), F32), jax.ShapeDtypeStruct((1, d), F32)]
    if final:
        out_names += ["loss"]
        out_specs += [pl.BlockSpec((1, 128), lambda b, i: (0, 0))]
        out_shape += [jax.ShapeDtypeStruct((1, 128), F32)]
    else:
        out_names += ["dnm"]
        out_specs += [mod_spec]
        out_shape += [jax.ShapeDtypeStruct((bsz, MOD_ROWS, d), F32)]
    if has_branch:
        out_names += ["dbr", "dgate"]
        out_specs += [row_spec, mod_spec]
        out_shape += [jax.ShapeDtypeStruct((bsz, seq, d), BF16), jax.ShapeDtypeStruct((bsz, MOD_ROWS, d), F32)]
    n_in = len(ins)

    def body(*refs):
        r = dict(zip(["x", "g"] + (["t"] if final else ["dh", "dres", "nm"]) + (["br", "gm"] if has_branch else []),
                     refs[:n_in]))
        o = dict(zip(out_names, refs[n_in:]))
        b_first = pl.program_id(1) == 0
        first = jnp.logical_and(pl.program_id(0) == 0, b_first)
        rowid = _iota((MOD_ROWS, d), 0)

        @pl.when(first)
        def _():
            o["dg"][...] = jnp.zeros_like(o["dg"])
            if final:
                o["loss"][...] = jnp.zeros_like(o["loss"])

        @pl.when(b_first)
        def _():
            if not final:
                o["dnm"][...] = jnp.zeros_like(o["dnm"])
            if has_branch:
                o["dgate"][...] = jnp.zeros_like(o["dgate"])

        x = r["x"][0]
        gv = r["g"][...]
        if final:
            e = _rms_fwd(x, gv, d) - r["t"][0]
            sq = jnp.sum(jnp.sum(e * e, axis=1, keepdims=True), axis=0, keepdims=True)
            o["loss"][...] += jnp.broadcast_to(sq * (0.5 / d), (1, 128))
            dx, dg = _rms_bwd(x, gv, e * (1.0 / d), d)
        else:
            m = r["nm"][0]
            dh = r["dh"][0].astype(F32)
            scale = m[rows[1]:rows[1] + 1, :]
            rstd = lax.rsqrt(jnp.sum(x * x, axis=-1, keepdims=True) * (1.0 / d) + EPS)
            xh = x * rstd
            nrm = xh * gv
            dshift = jnp.sum(dh, axis=0, keepdims=True)
            dscale = jnp.sum(dh * nrm, axis=0, keepdims=True)
            o["dnm"][0] += jnp.where(rowid == 0, dshift, jnp.where(rowid == 1, dscale, 0.0))
            dn = dh * (1.0 + scale)
            dg = jnp.sum(dn * xh, axis=0, keepdims=True)
            dxh = dn * gv
            dx = rstd * (dxh - xh * (jnp.sum(dxh * xh, axis=-1, keepdims=True) * (1.0 / d))) + r["dres"][0]
        o["dg"][...] += dg
        o["dx"][0] = dx
        if has_branch:
            gate = r["gm"][0][gate_row:gate_row + 1, :]
            o["dbr"][0] = (gate * dx).astype(BF16)
            dgate = jnp.sum(dx * r["br"][0], axis=0, keepdims=True)
            o["dgate"][0] += jnp.where(rowid == 0, dgate, 0.0)

    outs = pl.pallas_call(
        body, name=name, grid=(bsz, seq // tb),
        in_specs=in_specs, out_specs=out_specs, out_shape=out_shape,
        compiler_params=_cparams(),
    )(*ins)
    return dict(zip(out_names, outs))


def _adamw(w, g, m, v, *, name):
    shape = w.shape
    cols = shape[-1]
    rows = w.size // cols
    tr = _pick_rows(rows, max(8, (256 * 1024) // cols // 8 * 8))

    def body(w_ref, g_ref, m_ref, v_ref, d_ref, nm_ref, nv_ref):
        d_ref[...], nm_ref[...], nv_ref[...] = _adamw_math(w_ref[...], g_ref[...], m_ref[...], v_ref[...])

    spec = pl.BlockSpec((tr, cols), lambda i: (i, 0))
    outs = pl.pallas_call(
        body, name=name, grid=(rows // tr,),
        in_specs=[spec] * 4, out_specs=[spec] * 3,
        out_shape=[jax.ShapeDtypeStruct((rows, cols), F32)] * 3,
        compiler_params=_cparams(),
    )(*[t.reshape(rows, cols) for t in (w, g, m, v)])
    return tuple(o.reshape(shape) for o in outs)


def _adamw_math(w, g, m, v):
    c1 = 1.0 - ADAM_B1 ** ADAM_STEP
    c2 = 1.0 - ADAM_B2 ** ADAM_STEP
    nm = ADAM_B1 * m + (1.0 - ADAM_B1) * g
    nv = ADAM_B2 * v + (1.0 - ADAM_B2) * (g * g)
    delta = -ADAM_LR * ((nm / c1) / (jnp.sqrt(nv / c2) + ADAM_EPS) + ADAM_WD * w)
    return delta, nm, nv


def _adamw_layers(w, m, v, bufs, row_off, *, name, tr=256):
    depth, rows, cols = w.shape
    tr = min(tr, rows)
    assert rows % tr == 0 and row_off % tr == 0

    outs = None
    for l in range(depth):
        def body(w_ref, g_ref, m_ref, v_ref, *rest):
            go_ref, d_ref, nm_ref, nv_ref = rest[-4:]
            g = g_ref[...]
            go_ref[...] = g
            d_ref[...], nm_ref[...], nv_ref[...] = _adamw_math(w_ref[...], g, m_ref[...], v_ref[...])

        layer = pl.BlockSpec((None, tr, cols), lambda i, l=l: (l, i, 0))
        prev = () if outs is None else tuple(outs)
        outs = pl.pallas_call(
            body, name=f"{name}_l{l}", grid=(rows // tr,),
            in_specs=[layer, pl.BlockSpec((tr, cols), lambda i: (row_off // tr + i, 0)), layer, layer]
            + [ANY_SPEC] * len(prev),
            out_specs=[layer] * 4,
            out_shape=[jax.ShapeDtypeStruct(w.shape, F32)] * 4,
            input_output_aliases={4 + k: k for k in range(len(prev))},
            compiler_params=_cparams(),
        )(w, bufs[l], m, v, *prev)
    return tuple(outs)


def _sum_leading(x, *, name, tr=256):
    n, rows, cols = x.shape
    tr = _pick_rows(rows, tr)

    def body(x_ref, o_ref):
        acc = x_ref[0]
        for k in range(1, n):
            acc = acc + x_ref[k]
        o_ref[...] = acc

    return pl.pallas_call(
        body, name=name, grid=(rows // tr,),
        in_specs=[pl.BlockSpec((n, tr, cols), lambda i: (0, i, 0))],
        out_specs=pl.BlockSpec((tr, cols), lambda i: (i, 0)),
        out_shape=jax.ShapeDtypeStruct((rows, cols), F32),
        compiler_params=_cparams(),
    )(x)


def _position():
    return lax.axis_index("x"), lax.axis_index("y"), lax.axis_index("c")


def _allgather8(x, *, name):
    shape = x.shape

    def body(x_ref, out_ref, send_sems, recv_sems, local_sem):
        px, py, pc = _position()
        me, sibling = (px, py, pc), (px, py, 1 - pc)
        chips = [(1 - px, py), (px, 1 - py), (1 - px, 1 - py)]
        src_own = x_ref

        def slot(qx, qy, qc):
            return out_ref.at[4 * qx + 2 * qy + qc]

        def copy(k, block, to, src=None):
            return pltpu.make_async_remote_copy(
                src_ref=slot(*block) if src is None else src, dst_ref=slot(*block),
                send_sem=send_sems.at[k], recv_sem=recv_sems.at[k], device_id=to, device_id_type=MESH)

        mine = pltpu.make_async_copy(src_own, slot(*me), local_sem)
        mine.start()
        first = [copy(0, me, sibling, src=src_own)]
        first += [copy(1 + j, me, (*chip, pc), src=src_own) for j, chip in enumerate(chips)]
        for cp in first:
            cp.start()
        passed = [copy(4 + j, (*chip, pc), sibling) for j, chip in enumerate(chips)]
        for j, chip in enumerate(chips):
            copy(1 + j, (*chip, pc), me).wait_recv()
            passed[j].start()
        copy(0, sibling, me).wait_recv()
        for j, chip in enumerate(chips):
            copy(4 + j, (*chip, 1 - pc), me).wait_recv()
        for cp in first + passed:
            cp.wait_send()
        mine.wait()

    return pl.pallas_call(
        body, name=name,
        out_shape=jax.ShapeDtypeStruct((N_DEV,) + shape, x.dtype),
        in_specs=[pl.BlockSpec(memory_space=pl.ANY)],
        out_specs=pl.BlockSpec(memory_space=pl.ANY),
        scratch_shapes=[pltpu.SemaphoreType.DMA((7,)), pltpu.SemaphoreType.DMA((7,)), pltpu.SemaphoreType.DMA],
    )(x)


class _Exchange:
    def __init__(self, ins, out_shapes, n, build, aliases=None):
        self.ins, self.out_shapes, self.n, self.build = tuple(ins), tuple(out_shapes), n, build
        self.aliases = dict(aliases or {})

    def _descriptors(self, in_refs, out_refs, send_sems, recv_sems):
        sends, recvs = [], []
        for k, (src, dst, peer, landing) in enumerate(self.build(in_refs, out_refs)):
            sends.append(pltpu.make_async_remote_copy(
                src_ref=src, dst_ref=dst, send_sem=send_sems.at[k], recv_sem=recv_sems.at[k],
                device_id=peer, device_id_type=MESH))
            recvs.append(pltpu.make_async_remote_copy(
                src_ref=src, dst_ref=landing, send_sem=send_sems.at[k], recv_sem=recv_sems.at[k],
                device_id=peer, device_id_type=MESH))
        return sends, recvs

    def start(self, *refs):
        for cp in self._descriptors(*refs)[0]:
            cp.start()

    def finish(self, *refs):
        sends, recvs = self._descriptors(*refs)
        for cp in recvs:
            cp.wait_recv()
        for cp in sends:
            cp.wait_send()


ANY_SPEC = pl.BlockSpec(memory_space=pl.ANY)


def _hosted_call(body, *, name, grid, in_specs, out_specs, out_shape, args, scratch_shapes=(), side=None,
                 num_scalar_prefetch=0, io_aliases=None):
    in_specs, out_specs, out_shape = list(in_specs), list(out_specs), list(out_shape)
    n_in, n_out = len(in_specs) + num_scalar_prefetch, len(out_specs)
    kernel_body = body
    aliases = dict(io_aliases or {})
    if side is not None:
        s_in, s_out = len(side.ins), len(side.out_shapes)
        aliases.update({n_in + i: n_out + o for i, o in side.aliases.items()})

        def kernel_body(*refs):
            ins, s_ins = refs[:n_in], refs[n_in:n_in + s_in]
            outs = refs[n_in + s_in:n_in + s_in + n_out]
            s_outs = refs[n_in + s_in + n_out:n_in + s_in + n_out + s_out]
            scratch, sems = refs[n_in + s_in + n_out + s_out:-2], refs[-2:]
            first = functools.reduce(jnp.logical_and, [pl.program_id(a) == 0 for a in range(len(grid))])
            last = functools.reduce(jnp.logical_and, [pl.program_id(a) == g - 1 for a, g in enumerate(grid)])

            @pl.when(first)
            def _():
                side.start(s_ins, s_outs, *sems)

            body(*ins, *outs, *scratch)

            @pl.when(last)
            def _():
                side.finish(s_ins, s_outs, *sems)

        in_specs += [ANY_SPEC] * s_in
        out_specs += [ANY_SPEC] * s_out
        out_shape += list(side.out_shapes)
        scratch_shapes = list(scratch_shapes) + [pltpu.SemaphoreType.DMA((side.n,)),
                                                 pltpu.SemaphoreType.DMA((side.n,))]
        args = tuple(args) + side.ins
    if num_scalar_prefetch:
        grid_spec = pltpu.PrefetchScalarGridSpec(num_scalar_prefetch=num_scalar_prefetch, grid=grid,
                                                 in_specs=in_specs, out_specs=out_specs,
                                                 scratch_shapes=list(scratch_shapes))
        outs = pl.pallas_call(kernel_body, name=name, grid_spec=grid_spec, out_shape=out_shape,
                              input_output_aliases=aliases, compiler_params=_cparams())(*args)
    else:
        outs = pl.pallas_call(kernel_body, name=name, grid=grid, in_specs=in_specs, out_specs=out_specs,
                              out_shape=out_shape, scratch_shapes=list(scratch_shapes),
                              input_output_aliases=aliases, compiler_params=_cparams())(*args)
    return tuple(outs[:n_out]), tuple(outs[n_out:])


def _run_exchange(ex, *, name):
    s_in = len(ex.ins)

    def body(*refs):
        ins, outs, sems = refs[:s_in], refs[s_in:-2], refs[-2:]
        ex.start(ins, outs, *sems)
        ex.finish(ins, outs, *sems)

    outs = pl.pallas_call(
        body, name=name, out_shape=list(ex.out_shapes),
        in_specs=[ANY_SPEC] * s_in, out_specs=[ANY_SPEC] * len(ex.out_shapes),
        scratch_shapes=[pltpu.SemaphoreType.DMA((ex.n,)), pltpu.SemaphoreType.DMA((ex.n,))],
        input_output_aliases=ex.aliases,
    )(*ex.ins)
    return tuple(outs)


def _other_chips(px, py):
    return [(px, 1 - py), (1 - px, py), (1 - px, 1 - py)]


def _gather_spread(w_flat):
    rows, w = w_flat.shape
    hr = rows // 2

    def build(ins, outs):
        px, py, pc = _position()
        mine = ins[0].at[pl.ds(pc * hr, hr)]
        me = 4 * px + 2 * py + pc
        plan = [((px, py, 1 - pc), me ^ 1)]
        plan += [((qx, qy, pc), 4 * qx + 2 * qy + pc) for qx, qy in _other_chips(px, py)]
        return [(mine, outs[0].at[me], peer, outs[0].at[their]) for peer, their in plan]

    return _Exchange([w_flat], [jax.ShapeDtypeStruct((N_DEV, hr, w), w_flat.dtype)], 4, build)


def _gather_pass_on(gath):
    def build(ins, outs):
        px, py, pc = _position()
        out = []
        for qx, qy in _other_chips(px, py):
            blk = 4 * qx + 2 * qy + pc
            out.append((outs[0].at[blk], outs[0].at[blk], (px, py, 1 - pc), outs[0].at[blk ^ 1]))
        return out

    return _Exchange([gath], [jax.ShapeDtypeStruct(gath.shape, gath.dtype)], 3, build, aliases={0: 0})


def _rs_halves(g):
    n, rows, w = g.shape
    hr = rows // 2

    def build(ins, outs):
        px, py, pc = _position()
        return [(ins[0].at[:, pl.ds((1 - pc) * hr, hr), :], outs[0], (px, py, 1 - pc), outs[0])]

    return _Exchange([g], [jax.ShapeDtypeStruct((n, hr, w), g.dtype)], 1, build)


def _rs_chips(sb):
    def build(ins, outs):
        px, py, pc = _position()
        return [(ins[0].at[j], outs[0].at[j], (qx, qy, pc), outs[0].at[j])
                for j, (qx, qy) in enumerate(_other_chips(px, py))]

    return _Exchange([sb], [jax.ShapeDtypeStruct(sb.shape, sb.dtype)], 3, build)


def _rs_complete(buf):
    def build(ins, outs):
        px, py, pc = _position()
        return [(outs[0].at[pc], outs[0].at[pc], (px, py, 1 - pc), outs[0].at[1 - pc])]

    return _Exchange([buf], [jax.ShapeDtypeStruct(buf.shape, buf.dtype)], 1, build, aliases={0: 0})


def _rs_partial(g, recv, ids, *, name, tr=128):
    _, rows, w = g.shape
    hr = rows // 2
    nb = hr // tr

    def body(ids_ref, g_ref, r_ref, o_ref):
        o_ref[0] = (g_ref[0] + r_ref[0]).astype(BF16)

    grid_spec = pltpu.PrefetchScalarGridSpec(
        num_scalar_prefetch=1, grid=(3, nb),
        in_specs=[pl.BlockSpec((1, tr, w), lambda j, i, ids: (ids[1] ^ (j + 1), ids[0] * nb + i, 0)),
                  pl.BlockSpec((1, tr, w), lambda j, i, ids: (ids[1] ^ (j + 1), i, 0))],
        out_specs=pl.BlockSpec((1, tr, w), lambda j, i, ids: (j, i, 0)))
    return pl.pallas_call(
        body, name=name, grid_spec=grid_spec,
        out_shape=jax.ShapeDtypeStruct((3, hr, w), BF16),
        compiler_params=_cparams(),
    )(ids, g, recv)


def _rs_total(g, recv, got, ids, *, name, tr=128):
    _, rows, w = g.shape
    hr = rows // 2
    nb = hr // tr

    def body(ids_ref, g_ref, r_ref, got_ref, o_ref):
        acc = g_ref[0] + r_ref[0]
        for j in range(3):
            acc = acc + got_ref[j].astype(F32)
        o_ref[0] = acc

    grid_spec = pltpu.PrefetchScalarGridSpec(
        num_scalar_prefetch=1, grid=(nb,),
        in_specs=[pl.BlockSpec((1, tr, w), lambda i, ids: (ids[1], ids[0] * nb + i, 0)),
                  pl.BlockSpec((1, tr, w), lambda i, ids: (ids[1], i, 0)),
                  pl.BlockSpec((3, tr, w), lambda i, ids: (0, i, 0))],
        out_specs=pl.BlockSpec((1, tr, w), lambda i, ids: (ids[0], i, 0)))
    return pl.pallas_call(
        body, name=name, grid_spec=grid_spec,
        out_shape=jax.ShapeDtypeStruct((2, hr, w), F32),
        compiler_params=_cparams(),
    )(ids, g, recv, got)


class _ReduceScatter:
    def __init__(self, g, ids, tag):
        self.g, self.ids, self.tag, self.stage, self.result = g, ids, tag, 0, None

    def next_exchange(self):
        if self.stage == 0:
            return _rs_halves(self.g)
        if self.stage == 1:
            return _rs_chips(self.sb)
        return _rs_complete(self.buf)

    def done(self, outs):
        if self.stage == 0:
            self.recv = outs[0]
            hr = self.recv.shape[1]
            self.tr = max(t for t in range(16, 513, 16) if hr % t == 0)
            self.sb = _rs_partial(self.g, self.recv, self.ids, name=f"{self.tag}_partial", tr=self.tr)
        elif self.stage == 1:
            self.buf = _rs_total(self.g, self.recv, outs[0], self.ids, name=f"{self.tag}_total", tr=self.tr)
        else:
            _, hr, w = outs[0].shape
            self.result = outs[0].reshape(2 * hr, w)
        self.stage += 1

    def finish_alone(self):
        names = ("halves", "chips", "complete")
        while self.stage < 3:
            self.done(_run_exchange(self.next_exchange(), name=f"{self.tag}_{names[self.stage]}"))
        return self.result


def _flat_rows():
    used = sum(r for _, r in FSDP_SECTIONS)
    return used, -(-used // ROW_ALIGN) * ROW_ALIGN


def _cols_to_chunks(full):
    rows, cols = full.shape
    t = full.reshape(rows, N_CHIPS, cols // N_CHIPS).transpose(1, 0, 2)
    return t.reshape(N_CHIPS, -1, FLAT_W)


def _chunks_to_cols(chunks, rows, cols):
    return chunks.reshape(N_CHIPS, rows, cols // N_CHIPS).transpose(1, 0, 2).reshape(rows, cols)


def _pad_heads(w, real):
    lead = w.shape[:-1]
    t = w.reshape(lead + (HEADS, real))
    t = jnp.pad(t, [(0, 0)] * len(lead) + [(0, 0), (0, HEAD_PAD - real)])
    return t.reshape(lead + (HEADS * HEAD_PAD,))


def _unpad_heads(w, real):
    lead = w.shape[:-1]
    return w.reshape(lead + (HEADS, HEAD_PAD))[..., :real].reshape(lead + (HEADS * real,))


def _pad_value_lanes(w, axis):
    w = jnp.moveaxis(w, axis, -1)
    lead = w.shape[:-1]
    t = w.reshape(lead + (HEADS, 64))
    t = jnp.pad(t, [(0, 0)] * len(lead) + [(0, 0), (HEAD_PAD - 64, 0)])
    return jnp.moveaxis(t.reshape(lead + (HEADS * HEAD_PAD,)), -1, axis)


def _unpad_value_lanes(w, axis):
    w = jnp.moveaxis(w, axis, -1)
    lead = w.shape[:-1]
    t = w.reshape(lead + (HEADS, HEAD_PAD))[..., HEAD_PAD - 64:]
    return jnp.moveaxis(t.reshape(lead + (HEADS * 64,)), -1, axis)


def _pad_w_in(w):
    z = jnp.zeros((w.shape[0], NOPE), w.dtype)
    z2 = jnp.zeros((w.shape[0], HEAD_PAD - NOPE - ROPE), w.dtype)
    return jnp.concatenate([w[:, :1408], z, w[:, 1408:], z2], axis=1)


def _unpad_w_in(w):
    return jnp.concatenate([w[:, :1408], w[:, 1408 + NOPE:1408 + NOPE + ROPE]], axis=1)


def _rope_tables(positions):
    freqs = ROPE_THETA ** (-jnp.arange(0, ROPE, 2, dtype=F32) / ROPE)
    ang = positions.astype(F32)[..., None] * freqs
    cos, sin = jnp.cos(ang), jnp.sin(ang)
    lead = cos.shape[:-1]
    ones = jnp.ones(lead + (NOPE,), F32)
    zeros_n = jnp.zeros(lead + (NOPE,), F32)
    zeros_p = jnp.zeros(lead + (HEAD_PAD - NOPE - ROPE,), F32)
    ctab = jnp.concatenate([ones, cos, cos, zeros_p], axis=-1)
    stab = jnp.concatenate([zeros_n, -sin, sin, zeros_p], axis=-1)
    return ctab, stab


def _layer_weights(full, p, l):
    ws = p["gmlp_ws"][l]
    tril = jnp.tril(jnp.ones((CHUNK, CHUNK), bool))
    bs = p["gmlp_bs"][l]
    bexp = jnp.repeat(bs.reshape(GROUPS // 2, 2, CHUNK).transpose(0, 2, 1), GROUP_DIM, axis=2)
    return dict(
        w_in=_pad_w_in(full["w_in"]),
        w_uq=_pad_heads(full["mla_w_uq"], NOPE + ROPE),
        w_ukv=full["mla_w_ukv"],
        w_out=jnp.concatenate([_pad_value_lanes(full["w_out"][D_GMLP:], 0), full["w_out"][:D_GMLP]], axis=0),
        ws=ws,
        wst=jnp.where(tril[None], ws, 0.0).transpose(0, 2, 1).astype(BF16),
        bexp=bexp,
        g_mix=p["norm_mix_g"][l][None],
        g_ffn=p["norm_ffn_g"][l][None],
        g_q=p["mla_q_norm_g"][l][None],
        g_kv=p["mla_kv_norm_g"][l][None],
        g_og=p["out_norm_gmlp_g"][l][None],
        g_oa=_pad_value_lanes(p["out_norm_mla_g"][l], 0)[None],
    )


def _local_step(x3, target3, positions, mods, final_g, plan):
    bsz, seq, d = x3.shape
    tok = bsz * seq
    tmt = min(512, seq)
    tmk = min(1024, seq)
    chunk = (None, None, FLAT_W, FLAT_W)
    ff_grad_shape = (N_CHIPS, 2 * FLAT_W, FLAT_W)
    ctab, stab = _rope_tables(positions)
    lw = [None] * DEPTH

    def flat(t):
        return t.reshape(tok, t.shape[-1])

    def cube(t):
        return t.reshape(bsz, seq, t.shape[-1])

    def carrying(l, tag, fn, *args, **kw):
        side = plan.host(l, tag)
        if side is None:
            return fn(*args, **kw)
        res, side_outs = fn(*args, side=side, **kw)
        plan.hosted(l, tag, side_outs)
        return res

    saved = []
    x = x3
    for l in range(DEPTH):
        lw[l] = plan.layer(l)
        w, mod = lw[l], mods[l]
        h1 = _normmod_fwd(x, w["g_mix"], mod, SHIFT1, SCALE1, name=f"l{l}_normmod1")
        z = cube(_mm(flat(h1), w["w_in"], dims="nn", name=f"l{l}_w_in", tm=tmt, tn=D_IN_PAD, tk=d,
                     out_dtypes=(BF16,)))
        yg = _gmlp_fwd(z, w["ws"], w["bexp"], w["g_og"], name=f"l{l}_gmlp_fwd")
        q, kv, kp = _mla_prep_fwd(z, w["g_q"], w["g_kv"], w["w_uq"], w["w_ukv"], ctab, stab, name=f"l{l}_mla_prep")
        o, lse = carrying(l, "fwd_attn", _attn_fwd, q, kv, kp, name=f"l{l}_attn_fwd")
        y = _onorm_fwd(o, yg, w["g_oa"], name=f"l{l}_onorm_fwd")

        def out_epi(po, xv, gm):
            return po, xv + gm[0][GATE1:GATE1 + 1, :] * po

        po, x_mid = carrying(l, "fwd_out_a", _mm, flat(y), w["w_out"], dims="nn", name=f"l{l}_w_out",
                             tm=tmt, tn=d, tk=y.shape[-1], out_dtypes=(BF16, F32), epilogue=out_epi,
                             extras=(flat(x), mod), extra_specs=(None, _mod_spec(tmt, d, seq)))
        x_mid = cube(x_mid)
        h2 = _normmod_fwd(x_mid, w["g_ffn"], mod, SHIFT2, SCALE2, name=f"l{l}_normmod2")

        def act_epi(acc):
            r = jnp.maximum(acc, 0.0)
            return (r * r,)

        r = carrying(l, "fwd_ff1", _mm, flat(h2), w["ff"], dims="nn", name=f"l{l}_w_ff1", tm=tmt, tn=FLAT_W,
                     tk=d, out_dtypes=(BF16,), epilogue=act_epi, weights_outer=True, n=D_FF,
                     b_block=(chunk, lambda i, j, k: (j, 0, 0, 0)))

        def ff2_epi(acc, xv, gm):
            return acc, xv + gm[0][GATE2:GATE2 + 1, :] * acc

        f, x_out = carrying(l, "fwd_ff2", _mm, r, w["ff"], dims="nn", name=f"l{l}_w_ff2", tm=tmk, tn=d, tk=FLAT_W,
                            out_dtypes=(BF16, F32), epilogue=ff2_epi, extras=(flat(x_mid), mod),
                            extra_specs=(None, _mod_spec(tmk, d, seq)), n=d,
                            b_block=(chunk, lambda i, j, k: (k, 1, 0, 0)))
        saved.append(dict(x_in=x, h1=h1, z=z, q=q, kv=kv, kp=kp, o=o, lse=lse, y=y, po=cube(po),
                          x_mid=x_mid, h2=h2, r=r, f=cube(f)))
        x = cube(x_out)

    grads = [dict() for _ in range(DEPTH)]
    dmods = [None] * DEPTH
    top = DEPTH - 1
    node = _resnode_bwd(x, final_g[None], name="final_loss_bwd", target3=target3,
                        branch3=saved[top]["f"], mod_gate=mods[top], gate_row=GATE2)
    loss_part = node["loss"][0, 0]
    d_final_g = node["dg"][0]
    for l in range(DEPTH - 1, -1, -1):
        w, mod, s = lw[l], mods[l], saved[l]
        dx_out, dfb, dgate2 = node["dx"], flat(node["dbr"]), node["dgate"][:, 0]

        def dact_epi(acc, rv):
            return (acc * (2.0 * jnp.sqrt(rv.astype(F32))),)

        da = carrying(l, "bwd_d_r", _mm, dfb, w["ff"], dims="nt", name=f"l{l}_d_r", tm=tmt, tn=FLAT_W, tk=d,
                      out_dtypes=(BF16,), epilogue=dact_epi, extras=(s["r"],), weights_outer=True, n=D_FF,
                      b_block=(chunk, lambda i, j, k: (j, 1, 0, 0)))
        g_ff = carrying(l, "bwd_dw_ff2", _mm, s["r"], dfb, dims="tn", name=f"l{l}_dw_ff2", tm=FLAT_W, tn=d,
                        tk=1024, out_into=(ff_grad_shape, (None, FLAT_W, FLAT_W), lambda i, j, k: (i, 1, 0), None))
        g_ff = carrying(l, "bwd_dw_ff1", _mm, flat(s["h2"]), da, dims="tn", name=f"l{l}_dw_ff1", tm=d, tn=FLAT_W,
                        tk=1024, out_into=(ff_grad_shape, (None, FLAT_W, FLAT_W), lambda i, j, k: (j, 0, 0), g_ff))
        plan.ff_grads(l, g_ff)
        dh2 = carrying(l, "bwd_d_h2", _mm, da, w["ff"], dims="nt", name=f"l{l}_d_h2", tm=tmk, tn=d, tk=FLAT_W,
                       n=d, b_block=(chunk, lambda i, j, k: (k, 0, 0, 0)), out_dtypes=(BF16,))
        node = _resnode_bwd(s["x_mid"], w["g_ffn"], name=f"l{l}_resnode_ffn", dh3=cube(dh2), dres3=dx_out,
                            mod_nm=mod, rows=(SHIFT2, SCALE2), branch3=s["po"], mod_gate=mod, gate_row=GATE1)
        grads[l]["norm_ffn_g"] = node["dg"][0]
        dshift2, dscale2 = node["dnm"][:, 0], node["dnm"][:, 1]
        dx_mid, dpo, dgate1 = node["dx"], flat(node["dbr"]), node["dgate"][:, 0]

        wy = s["y"].shape[-1]
        dy = cube(_mm(dpo, w["w_out"], dims="nt", name=f"l{l}_d_y", tm=tmt, tn=wy, tk=d, out_dtypes=(BF16,)))
        dw_out = _mm(flat(s["y"]), dpo, dims="tn", name=f"l{l}_dw_out", tm=wy // 3, tn=d, tk=1024)
        hw = HEADS * HEAD_PAD
        grads[l]["w_out"] = jnp.concatenate([dw_out[hw:], _unpad_value_lanes(dw_out[:hw], 0)], axis=0)

        dz, dws, dbs, dg_og = _gmlp_bwd(s["z"], dy, w["ws"], w["wst"], w["bexp"], w["g_og"],
                                        name=f"l{l}_gmlp_bwd", dy_col=hw // D_GMLP)
        grads[l]["gmlp_ws"], grads[l]["gmlp_bs"], grads[l]["out_norm_gmlp_g"] = dws, dbs, dg_og[0]

        do, dl, dg_oa = _onorm_bwd(s["o"], dy, w["g_oa"], name=f"l{l}_onorm_bwd")
        grads[l]["out_norm_mla_g"] = _unpad_value_lanes(dg_oa[0], 0)
        dq, dk, dv = carrying(l, "bwd_attn_dkv", _attn_bwd, s["q"], s["kv"], s["kp"], do, s["lse"], dl,
                              name=f"l{l}_attn_bwd")
        dz, cq, dqb, ckv, dkvb, dg_q, dg_kv = _mla_prep_bwd(
            s["z"], dz, dq, dk, dv, w["g_q"], w["g_kv"], w["w_uq"], w["w_ukv"], ctab, stab,
            name=f"l{l}_mla_prep_bwd")
        grads[l]["mla_q_norm_g"], grads[l]["mla_kv_norm_g"] = dg_q[0], dg_kv[0]
        dw_uq = carrying(l, "bwd_dw_uq", _mm, flat(cq), flat(dqb), dims="tn", name=f"l{l}_dw_uq", tm=Q_RANK,
                         tn=1024, tk=1024)
        grads[l]["mla_w_uq"] = _unpad_heads(dw_uq, NOPE + ROPE)
        grads[l]["mla_w_ukv"] = _mm(flat(ckv), flat(dkvb), dims="tn", name=f"l{l}_dw_ukv", tm=KV_RANK, tn=1024, tk=1024)

        grads[l]["w_in"] = _unpad_w_in(_mm(flat(s["h1"]), flat(dz), dims="tn", name=f"l{l}_dw_in", tm=d,
                                           tn=D_IN_PAD // 2, tk=1024))
        dh1 = _mm(flat(dz), w["w_in"], dims="nt", name=f"l{l}_d_h1", tm=tmt, tn=d, tk=D_IN_PAD, out_dtypes=(BF16,))
        if l > 0:
            node = _resnode_bwd(s["x_in"], w["g_mix"], name=f"l{l}_resnode_mix", dh3=cube(dh1), dres3=dx_mid,
                                mod_nm=mod, rows=(SHIFT1, SCALE1), branch3=saved[l - 1]["f"],
                                mod_gate=mods[l - 1], gate_row=GATE2)
        else:
            node = _resnode_bwd(s["x_in"], w["g_mix"], name=f"l{l}_resnode_mix", dh3=cube(dh1), dres3=dx_mid,
                                mod_nm=mod, rows=(SHIFT1, SCALE1))
        grads[l]["norm_mix_g"] = node["dg"][0]
        dshift1, dscale1 = node["dnm"][:, 0], node["dnm"][:, 1]
        dmods[l] = jnp.stack([dshift1, dscale1, dgate1, dshift2, dscale2, dgate2], axis=1)
        plan.layer_grads(l, grads[l])
    return loss_part, node["dx"], d_final_g, dmods


W_NAMES = ("w_ada", "b_ada", "norm_mix_g", "w_in", "gmlp_ws", "gmlp_bs", "mla_q_norm_g", "mla_kv_norm_g",
           "mla_w_uq", "mla_w_ukv", "out_norm_gmlp_g", "out_norm_mla_g", "w_out", "norm_ffn_g", "w_ff1", "w_ff2",
           "final_norm_g")
FLAT_KEY = {"w_in": "w_in", "w_uq": "mla_w_uq", "w_ukv": "mla_w_ukv", "w_out": "w_out", "w_ff1": "w_ff1",
            "w_ff2": "w_ff2"}
COL_SHARDED = ("w_in", "w_uq", "w_ukv", "w_ff1")
FULL_SHAPE = {"w_in": (D_MODEL, D_IN), "w_uq": (Q_RANK, HEADS * (NOPE + ROPE)), "w_ukv": (KV_RANK, HEADS * 128),
              "w_out": (D_MODEL, D_MODEL), "w_ff1": (D_MODEL, D_FF), "w_ff2": (D_FF, D_MODEL)}
SMALL_NAMES = ("norm_mix_g", "gmlp_ws", "gmlp_bs", "mla_q_norm_g", "mla_kv_norm_g", "out_norm_gmlp_g",
               "out_norm_mla_g", "norm_ffn_g", "final_norm_g")


def _silu(v):
    return v * (1.0 / (1.0 + jnp.exp(-v)))


class _CommPlan:
    FWD = {"fwd_attn": ("ff", 0, "spread"), "fwd_out_a": ("ff", 0, "pass"),
           "fwd_ff1": ("mix", 1, "spread"), "fwd_ff2": ("mix", 1, "pass")}
    BWD = {"bwd_d_r": ("mix", 1), "bwd_dw_ff2": ("mix", 1), "bwd_dw_ff1": ("mix", 1),
           "bwd_d_h2": ("ff", 0), "bwd_attn_dkv": ("ff", 0), "bwd_dw_uq": ("ff", 0)}

    def __init__(self, weights, ids, dev, core):
        self.weights, self.ids, self.dev, self.core = weights, ids, dev, core
        self.used, self.rows = _flat_rows()
        self.flat = {("mix", l): self._flat_mix(l) for l in range(DEPTH)}
        self.flat.update({("ff", l): jnp.concatenate([weights["w_ff1"][l], weights["w_ff2"][l]], axis=0).astype(BF16)
                          for l in range(DEPTH)})
        self.lw, self.rs, self.grads, self.spread = {}, {}, {}, {}
        (gath,) = _run_exchange(_gather_spread(self.flat["mix", 0]), name="l0_mix_gather_spread")
        (gath,) = _run_exchange(_gather_pass_on(gath), name="l0_mix_gather_pass_on")
        self._arrived("mix", 0, gath)

    def _flat_mix(self, l):
        pieces = [self.weights[FLAT_KEY[nm]][l].reshape(-1, FLAT_W) for nm, _ in FSDP_SECTIONS]
        pieces.append(jnp.zeros((self.rows - self.used, FLAT_W), F32))
        return jnp.concatenate(pieces, axis=0).astype(BF16)

    def _arrived(self, group, l, gath):
        flat = self.flat[group, l]
        hr = flat.shape[0] // 2
        mine = lax.dynamic_slice(flat, (self.core * hr, 0), (hr, FLAT_W))
        gath = lax.dynamic_update_slice(gath, mine[None], (self.dev, 0, 0))
        if group == "ff":
            self.lw[l]["ff"] = gath.reshape(N_CHIPS, 2, hr, FLAT_W)
            return
        w_gath = gath.reshape(N_CHIPS, self.rows, FLAT_W)
        full, off = {}, 0
        for nm, nrows in FSDP_SECTIONS:
            sec = w_gath[:, off:off + nrows]
            off += nrows
            rows, cols = FULL_SHAPE[nm]
            full[FLAT_KEY[nm]] = _chunks_to_cols(sec, rows, cols) if nm in COL_SHARDED else sec.reshape(rows, cols)
        self.lw[l] = _layer_weights(full, self.weights, l)

    def layer(self, l):
        return self.lw[l]

    def host(self, l, tag):
        if tag in self.FWD:
            group, ahead, what = self.FWD[tag]
            if l + ahead >= DEPTH:
                return None
            return _gather_spread(self.flat[group, l + ahead]) if what == "spread" else _gather_pass_on(self.spread[group])
        group, ahead = self.BWD[tag]
        rs = self.rs.get((group, l + ahead))
        return None if rs is None else rs.next_exchange()

    def hosted(self, l, tag, outs):
        if tag in self.FWD:
            group, ahead, what = self.FWD[tag]
            if what == "spread":
                self.spread[group] = outs[0]
            else:
                self._arrived(group, l + ahead, outs[0])
        else:
            group, ahead = self.BWD[tag]
            self.rs[group, l + ahead].done(outs)

    def ff_grads(self, l, g_ff):
        self.rs["ff", l] = _ReduceScatter(g_ff, self.ids, f"l{l}_ff_rs")

    def layer_grads(self, l, grads):
        self.grads[l] = grads
        pieces = []
        for nm, nrows in FSDP_SECTIONS:
            g = grads[FLAT_KEY[nm]]
            pieces.append(_cols_to_chunks(g) if nm in COL_SHARDED else g.reshape(N_CHIPS, nrows, FLAT_W))
        pieces.append(jnp.zeros((N_CHIPS, self.rows - self.used, FLAT_W), F32))
        self.rs["mix", l] = _ReduceScatter(jnp.concatenate(pieces, axis=1), self.ids, f"l{l}_mix_rs")
        if l == 0:
            self.rs["mix", l].finish_alone()

    def mix_grads(self):
        per = {FLAT_KEY[nm]: [] for nm, _ in FSDP_SECTIONS}
        for l in range(DEPTH):
            shard, off = self.rs["mix", l].result, 0
            for nm, nrows in FSDP_SECTIONS:
                key = FLAT_KEY[nm]
                per[key].append(shard[off:off + nrows].reshape(self.weights[key].shape[1:]))
                off += nrows
        return {key: jnp.stack(parts, axis=0) for key, parts in per.items()}

    def ff_shards(self):
        return [self.rs["ff", l].result for l in range(DEPTH)]


def kernel(x, c, positions, w_ada, b_ada, norm_mix_g, w_in, gmlp_ws, gmlp_bs, mla_q_norm_g, mla_kv_norm_g, mla_w_uq, mla_w_ukv, out_norm_gmlp_g, out_norm_mla_g, w_out, norm_ffn_g, w_ff1, w_ff2, final_norm_g, loss_target, m_w_ada, m_b_ada, m_norm_mix_g, m_w_in, m_gmlp_ws, m_gmlp_bs, m_mla_q_norm_g, m_mla_kv_norm_g, m_mla_w_uq, m_mla_w_ukv, m_out_norm_gmlp_g, m_out_norm_mla_g, m_w_out, m_norm_ffn_g, m_w_ff1, m_w_ff2, m_final_norm_g, v_w_ada, v_b_ada, v_norm_mix_g, v_w_in, v_gmlp_ws, v_gmlp_bs, v_mla_q_norm_g, v_mla_kv_norm_g, v_mla_w_uq, v_mla_w_ukv, v_out_norm_gmlp_g, v_out_norm_mla_g, v_w_out, v_norm_ffn_g, v_w_ff1, v_w_ff2, v_final_norm_g):
    weights = dict(w_ada=w_ada, b_ada=b_ada, norm_mix_g=norm_mix_g, w_in=w_in, gmlp_ws=gmlp_ws, gmlp_bs=gmlp_bs,
                   mla_q_norm_g=mla_q_norm_g, mla_kv_norm_g=mla_kv_norm_g, mla_w_uq=mla_w_uq, mla_w_ukv=mla_w_ukv,
                   out_norm_gmlp_g=out_norm_gmlp_g, out_norm_mla_g=out_norm_mla_g, w_out=w_out,
                   norm_ffn_g=norm_ffn_g, w_ff1=w_ff1, w_ff2=w_ff2, final_norm_g=final_norm_g)
    mom_m = dict(zip(W_NAMES, (m_w_ada, m_b_ada, m_norm_mix_g, m_w_in, m_gmlp_ws, m_gmlp_bs, m_mla_q_norm_g,
                               m_mla_kv_norm_g, m_mla_w_uq, m_mla_w_ukv, m_out_norm_gmlp_g, m_out_norm_mla_g,
                               m_w_out, m_norm_ffn_g, m_w_ff1, m_w_ff2, m_final_norm_g)))
    mom_v = dict(zip(W_NAMES, (v_w_ada, v_b_ada, v_norm_mix_g, v_w_in, v_gmlp_ws, v_gmlp_bs, v_mla_q_norm_g,
                               v_mla_kv_norm_g, v_mla_w_uq, v_mla_w_ukv, v_out_norm_gmlp_g, v_out_norm_mla_g,
                               v_w_out, v_norm_ffn_g, v_w_ff1, v_w_ff2, v_final_norm_g)))
    bsz, seq, d = x.shape
    px, py, pc = _position()
    chip = 2 * px + py
    dev = 2 * chip + pc
    ids = jnp.stack([pc, chip]).astype(jnp.int32)
    n_ex = N_DEV * bsz
    ada_cols = w_ada.shape[-1]

    c_all = _allgather8(c.reshape(bsz * d // 128, 128), name="gather_c").reshape(n_ex, d)
    mod_parts = []
    for l in range(DEPTH):
        bias = lax.dynamic_slice(b_ada[l], (chip * ada_cols,), (ada_cols,))[None]
        mod_parts.append(_mm(c_all, w_ada[l], dims="nn", name=f"l{l}_mod", tm=n_ex, tn=ada_cols, tk=d,
                             epilogue=lambda acc, bv: (acc + bv,), extras=(bias,),
                             extra_specs=(pl.BlockSpec((1, ada_cols), lambda i, j, k: (0, j)),), a_fn=_silu))
    mod_g = _allgather8(jnp.concatenate(mod_parts, axis=0), name="gather_mod")
    mod_g = mod_g.reshape(N_CHIPS, 2, DEPTH, n_ex, ada_cols)[:, 0]
    mod_full = mod_g.transpose(1, 2, 0, 3).reshape(DEPTH, n_ex, N_CHIPS * ada_cols)
    mod_mine = lax.dynamic_slice(mod_full, (0, dev * bsz, 0), (DEPTH, bsz, N_MOD * d))
    mod_mine = jnp.pad(mod_mine.reshape(DEPTH, bsz, N_MOD, d), ((0, 0), (0, 0), (0, MOD_ROWS - N_MOD), (0, 0)))
    mods = [mod_mine[l] for l in range(DEPTH)]

    plan = _CommPlan(weights, ids, dev, pc)
    loss_part, grad_x, d_final_g, dmods = _local_step(x, loss_target, positions, mods, final_norm_g, plan)
    grads = plan.grads
    grad = plan.mix_grads()

    small = {nm: (d_final_g if nm == "final_norm_g" else jnp.stack([grads[l][nm] for l in range(DEPTH)], axis=0))
             for nm in SMALL_NAMES}
    svec = jnp.concatenate([small[nm].reshape(-1) for nm in SMALL_NAMES] + [loss_part[None]])
    n_small = svec.shape[0]
    srows = -(-n_small // (8 * FLAT_W)) * 8
    svec = jnp.pad(svec, (0, srows * FLAT_W - n_small)).reshape(srows, FLAT_W)
    ssum = _sum_leading(_allgather8(svec, name="gather_small_grads"), name="sum_small_grads").reshape(-1)
    off = 0
    for nm in SMALL_NAMES:
        size = weights[nm].size
        grad[nm] = ssum[off:off + size].reshape(weights[nm].shape)
        off += size
    loss = ssum[off]

    dmod = jnp.stack(dmods, axis=1).reshape(bsz * DEPTH * N_MOD, d)
    dmod_all = _allgather8(dmod, name="gather_dmod").reshape(n_ex, DEPTH, N_MOD * d)
    gw, gb = [], []
    for l in range(DEPTH):
        dm = dmod_all[:, l]
        dm_cols = lax.dynamic_slice(dm, (0, chip * ada_cols), (n_ex, ada_cols))
        gw.append(_mm(c_all, dm_cols, dims="tn", name=f"l{l}_dw_ada", tm=d, tn=ada_cols, tk=n_ex, a_fn=_silu))
        gb.append(_sum_leading(dm.reshape(n_ex, N_MOD * d // FLAT_W, FLAT_W), name=f"l{l}_db_ada").reshape(-1))
    grad["w_ada"] = jnp.stack(gw, axis=0)
    grad["b_ada"] = jnp.stack(gb, axis=0)

    delta, new_m, new_v = {}, {}, {}
    ff_bufs = plan.ff_shards()
    for nm, row_off in (("w_ff1", 0), ("w_ff2", FLAT_W)):
        grad[nm], delta[nm], new_m[nm], new_v[nm] = _adamw_layers(
            weights[nm], mom_m[nm], mom_v[nm], ff_bufs, row_off, name=f"adamw_{nm}")
    for nm in W_NAMES:
        if nm not in delta:
            delta[nm], new_m[nm], new_v[nm] = _adamw(weights[nm], grad[nm], mom_m[nm], mom_v[nm],
                                                     name=f"adamw_{nm}")
    return (loss, grad_x, *[grad[nm] for nm in W_NAMES], *[delta[nm] for nm in W_NAMES],
            *[new_m[nm] for nm in W_NAMES], *[new_v[nm] for nm in W_NAMES])
```

```python
import functools
import math

import jax
import jax.numpy as jnp
from jax import lax
from jax.experimental import pallas as pl
from jax.experimental.pallas import tpu as pltpu

F32 = jnp.float32
BF16 = jnp.bfloat16

D_MODEL = 1024
DEPTH = 2
D_GMLP = 512
GROUPS = 8
GROUP_DIM = 64
CHUNK = 128
HEADS = 8
NOPE = 64
ROPE = 32
HEAD_PAD = 128
Q_RANK = 256
KV_RANK = 128
D_FF = 4096
N_MOD = 6
MOD_ROWS = 8
EPS = 1e-6
ROPE_THETA = 10000.0
D_IN = 1440
D_IN_PAD = 1536
ATTN_SCALE = (NOPE + ROPE) ** -0.5
LOG2E = math.log2(math.e)
SCALE_LOG2 = ATTN_SCALE * LOG2E
N_CHIPS = 4
N_DEV = 8

ADAM_LR = 0.001
ADAM_B1 = 0.9
ADAM_B2 = 0.999
ADAM_EPS = 1e-08
ADAM_WD = 0.01
ADAM_STEP = 10

VMEM_LIMIT = 48 * 1024 * 1024
FLAT_W = 1024
ROW_ALIGN = 256

NN = (((1,), (0,)), ((), ()))
NT = (((1,), (1,)), ((), ()))
TN = (((0,), (0,)), ((), ()))
MESH = pl.DeviceIdType.MESH

SHIFT1, SCALE1, GATE1, SHIFT2, SCALE2, GATE2 = range(6)

FSDP_SECTIONS = (("w_out", 256), ("w_in", 360), ("w_uq", 48), ("w_ukv", 32))


def _cparams(vmem=VMEM_LIMIT):
    return pltpu.CompilerParams(vmem_limit_bytes=vmem)


def _dot(a, b, dims=NN):
    return lax.dot_general(a, b, dims, preferred_element_type=F32)


def _iota(shape, axis):
    return lax.broadcasted_iota(jnp.int32, shape, axis)


def _gelu(x):
    k = math.sqrt(2.0 / math.pi)
    return 0.5 * x * (1.0 + jnp.tanh(k * (x + 0.044715 * (x * x * x))))


def _gelu_grad(x):
    k = math.sqrt(2.0 / math.pi)
    t = jnp.tanh(k * (x + 0.044715 * (x * x * x)))
    return 0.5 * (1.0 + t) + 0.5 * x * (1.0 - t * t) * (k * (1.0 + 3.0 * 0.044715 * (x * x)))


def _rms_fwd(x, g, n):
    r = lax.rsqrt(jnp.sum(x * x, axis=-1, keepdims=True) * (1.0 / n) + EPS)
    return x * r * g


def _rms_bwd(x, g, dy, n):
    r = lax.rsqrt(jnp.sum(x * x, axis=-1, keepdims=True) * (1.0 / n) + EPS)
    xh = x * r
    dxh = dy * g
    dx = r * (dxh - xh * (jnp.sum(dxh * xh, axis=-1, keepdims=True) * (1.0 / n)))
    dg = jnp.sum(dy * xh, axis=0, keepdims=True)
    return dx, dg


def _pick_rows(rows, limit):
    if rows <= limit:
        return rows
    for t in range(limit, 7, -8):
        if rows % t == 0:
            return t
    return rows


def _mm(a, b, *, dims, name, tm=512, tn=1024, tk=1024, out_dtypes=(F32,), epilogue=None,
        extras=(), extra_specs=(), a_fn=None, weights_outer=False, side=None, b_block=None, n=None,
        out_into=None):
    if dims == "tn":
        kk, m = a.shape
    else:
        m, kk = a.shape
    if n is None:
        n = b.shape[0] if dims == "nt" else b.shape[1]
    tm, tn, tk = min(tm, m), min(tn, n), min(tk, kk)
    assert m % tm == 0 and n % tn == 0 and kk % tk == 0, (name, a.shape, b.shape, tm, tn, tk)
    ni, nj, nk = m // tm, n // tn, kk // tk

    def spec(shape, pick):
        if weights_outer:
            return pl.BlockSpec(shape, lambda j, i, k: pick(i, j, k))
        return pl.BlockSpec(shape, pick)

    if dims == "tn":
        a_spec = spec((tk, tm), lambda i, j, k: (k, i))
    else:
        a_spec = spec((tm, tk), lambda i, j, k: (i, k))
    if b_block is not None:
        b_spec = spec(*b_block)
    elif dims == "nt":
        b_spec = spec((tn, tk), lambda i, j, k: (j, k))
    else:
        b_spec = spec((tk, tn), lambda i, j, k: (k, j))
    o_spec = spec((tm, tn), lambda i, j, k: (i, j))
    out_shape = [jax.ShapeDtypeStruct((m, n), dt) for dt in out_dtypes]
    out_specs = [o_spec] * len(out_dtypes)
    prev, io_aliases = (), {}
    if out_into is not None:
        full_shape, block, index, before = out_into
        assert len(out_dtypes) == 1 and not extras
        out_shape = [jax.ShapeDtypeStruct(full_shape, out_dtypes[0])]
        out_specs = [spec(block, index)]
        if before is not None:
            prev, io_aliases = (before,), {2: 0}
    assert not (weights_outer and extra_specs)
    dn = {"nn": NN, "nt": NT, "tn": TN}[dims]
    n_ex, n_out = len(extras), len(out_dtypes)
    e_specs = [o_spec if s is None else s for s in (tuple(extra_specs) + (None,) * n_ex)[:n_ex]]

    n_prev = len(prev)

    def body(*refs):
        a_ref, b_ref = refs[0], refs[1]
        e_refs = refs[2 + n_prev:2 + n_prev + n_ex]
        o_refs = refs[2 + n_prev + n_ex:2 + n_prev + n_ex + n_out]
        av = a_ref[...]
        if a_fn is not None:
            av = a_fn(av)
        part = _dot(av.astype(BF16), b_ref[...].astype(BF16), dn)

        def finish(acc):
            outs = (acc,) if epilogue is None else epilogue(acc, *[e[...] for e in e_refs])
            for o_ref, o in zip(o_refs, outs):
                o_ref[...] = o.astype(o_ref.dtype)

        if nk == 1:
            finish(part)
        else:
            acc_ref = refs[-1]
            k = pl.program_id(2)

            @pl.when(k == 0)
            def _():
                acc_ref[...] = part

            @pl.when(k > 0)
            def _():
                acc_ref[...] += part

            @pl.when(k == nk - 1)
            def _():
                finish(acc_ref[...])

    outs, side_outs = _hosted_call(
        body, name=name, grid=(nj, ni, nk) if weights_outer else (ni, nj, nk),
        in_specs=[a_spec, b_spec] + [ANY_SPEC] * n_prev + e_specs,
        out_specs=out_specs, out_shape=out_shape,
        scratch_shapes=[pltpu.VMEM((tm, tn), F32)] if nk > 1 else [],
        args=(a, b, *prev, *extras), side=side, io_aliases=io_aliases)
    res = outs[0] if n_out == 1 else outs
    return res if side is None else (res, side_outs)


def _mod_spec(tm, tn, seq):
    return pl.BlockSpec((1, MOD_ROWS, tn), lambda i, j, k: ((i * tm) // seq, 0, j))


def _normmod_fwd(x3, g, mod, shift_row, scale_row, *, name, tb=256):
    bsz, seq, d = x3.shape
    tb = min(tb, seq)

    def body(x_ref, g_ref, mod_ref, h_ref):
        m = mod_ref[0]
        nrm = _rms_fwd(x_ref[0], g_ref[...], d)
        h = nrm * (1.0 + m[scale_row:scale_row + 1, :]) + m[shift_row:shift_row + 1, :]
        h_ref[0] = h.astype(BF16)

    return pl.pallas_call(
        body, name=name, grid=(bsz, seq // tb),
        in_specs=[pl.BlockSpec((1, tb, d), lambda b, i: (b, i, 0)),
                  pl.BlockSpec((1, d), lambda b, i: (0, 0)),
                  pl.BlockSpec((1, MOD_ROWS, d), lambda b, i: (b, 0, 0))],
        out_specs=pl.BlockSpec((1, tb, d), lambda b, i: (b, i, 0)),
        out_shape=jax.ShapeDtypeStruct((bsz, seq, d), BF16),
        compiler_params=_cparams(),
    )(x3, g, mod)


def _pair_mean_exact(x, lo):
    s_lo = jnp.sum(jnp.where(lo, x, 0.0), axis=-1, keepdims=True)
    s_hi = jnp.sum(jnp.where(lo, 0.0, x), axis=-1, keepdims=True)
    return jnp.where(lo, s_lo, s_hi) * (1.0 / GROUP_DIM)


def _gmlp_pair_fwd(gv_p, w0, w1, bias, lo):
    mu = _pair_mean_exact(gv_p, lo)
    dlt = gv_p - mu
    var = _pair_mean_exact(dlt * dlt, lo)
    rstd = lax.rsqrt(var + EPS)
    vn = dlt * rstd
    vnb = vn.astype(BF16)
    mixed = jnp.where(lo, _dot(w0, vnb), _dot(w1, vnb)) + bias
    return vn, vnb, rstd, mixed


def _tril_bf16(w):
    t = w.shape[-1]
    return jnp.where(_iota((t, t), 1) <= _iota((t, t), 0), w, 0.0).astype(BF16)


def _gmlp_fwd(z3, ws, bexp, g_out, *, name):
    bsz, seq, _ = z3.shape
    nc = seq // CHUNK

    def body(u_ref, v_ref, ws_ref, b_ref, g_ref, y_ref):
        lo = _iota((CHUNK, 128), 1) < GROUP_DIM
        gu = _gelu(u_ref[0].astype(F32))
        gv = _gelu(v_ref[0].astype(F32))
        parts = []
        for p in range(GROUPS // 2):
            sl = slice(128 * p, 128 * p + 128)
            w0 = _tril_bf16(ws_ref[2 * p])
            w1 = _tril_bf16(ws_ref[2 * p + 1])
            _, _, _, mixed = _gmlp_pair_fwd(gv[:, sl], w0, w1, b_ref[p], lo)
            parts.append(gu[:, sl] * mixed)
        yg = jnp.concatenate(parts, axis=1)
        y_ref[0] = _rms_fwd(yg, g_ref[...], D_GMLP).astype(BF16)

    return pl.pallas_call(
        body, name=name, grid=(bsz, nc),
        in_specs=[pl.BlockSpec((1, CHUNK, D_GMLP), lambda b, i: (b, i, 0)),
                  pl.BlockSpec((1, CHUNK, D_GMLP), lambda b, i: (b, i, 1)),
                  pl.BlockSpec((GROUPS, CHUNK, CHUNK), lambda b, i: (0, 0, 0)),
                  pl.BlockSpec((GROUPS // 2, CHUNK, 128), lambda b, i: (0, 0, 0)),
                  pl.BlockSpec((1, D_GMLP), lambda b, i: (0, 0))],
        out_specs=pl.BlockSpec((1, CHUNK, D_GMLP), lambda b, i: (b, i, 0)),
        out_shape=jax.ShapeDtypeStruct((bsz, seq, D_GMLP), BF16),
        compiler_params=_cparams(),
    )(z3, z3, ws, bexp, g_out)


def _gmlp_bwd(z3, dyn3, ws, wst, bexp, g_out, *, name, dy_col):
    bsz, seq, _ = z3.shape
    nc = seq // CHUNK
    npair = GROUPS // 2

    def body(u_ref, v_ref, dy_ref, ws_ref, wst_ref, b_ref, g_ref, duv_ref, dws_ref, dbs_ref, dg_ref, dbacc):
        first = jnp.logical_and(pl.program_id(0) == 0, pl.program_id(1) == 0)
        last = jnp.logical_and(pl.program_id(0) == bsz - 1, pl.program_id(1) == nc - 1)

        @pl.when(first)
        def _():
            dws_ref[...] = jnp.zeros_like(dws_ref)
            dg_ref[...] = jnp.zeros_like(dg_ref)
            dbacc[...] = jnp.zeros_like(dbacc)

        lo = _iota((CHUNK, 128), 1) < GROUP_DIM
        tril = _iota((CHUNK, CHUNK), 1) <= _iota((CHUNK, CHUNK), 0)
        u = u_ref[0].astype(F32)
        v = v_ref[0].astype(F32)
        gu = _gelu(u)
        gv = _gelu(v)
        fwd = []
        for p in range(npair):
            sl = slice(128 * p, 128 * p + 128)
            w0 = _tril_bf16(ws_ref[2 * p])
            w1 = _tril_bf16(ws_ref[2 * p + 1])
            fwd.append(_gmlp_pair_fwd(gv[:, sl], w0, w1, b_ref[p], lo))
        yg = jnp.concatenate([gu[:, 128 * p:128 * p + 128] * fwd[p][3] for p in range(npair)], axis=1)
        dyg, dg = _rms_bwd(yg, g_ref[...], dy_ref[0].astype(F32), D_GMLP)
        dg_ref[...] += dg
        du_parts, dv_parts = [], []
        for p in range(npair):
            sl = slice(128 * p, 128 * p + 128)
            vn, vnb, rstd, mixed = fwd[p]
            dyg_p = dyg[:, sl]
            dmixed = dyg_p * gu[:, sl]
            dbacc[p] += dmixed
            dm_lo = jnp.where(lo, dmixed, 0.0).astype(BF16)
            dm_hi = jnp.where(lo, 0.0, dmixed).astype(BF16)
            dws_ref[2 * p] += jnp.where(tril, _dot(dm_lo, vnb, NT), 0.0)
            dws_ref[2 * p + 1] += jnp.where(tril, _dot(dm_hi, vnb, NT), 0.0)
            dmb = dmixed.astype(BF16)
            dvn = jnp.where(lo, _dot(wst_ref[2 * p], dmb), _dot(wst_ref[2 * p + 1], dmb))
            dgv = rstd * (dvn - _pair_mean_exact(dvn, lo) - vn * _pair_mean_exact(dvn * vn, lo))
            dv_parts.append(dgv * _gelu_grad(v[:, sl]))
            du_parts.append(dyg_p * mixed * _gelu_grad(u[:, sl]))
        duv_ref[0] = jnp.concatenate(du_parts + dv_parts, axis=1).astype(BF16)

        @pl.when(last)
        def _():
            sel = jnp.where(_iota((8, 128), 0) == 0, (_iota((8, 128), 1) < GROUP_DIM).astype(F32),
                            jnp.where(_iota((8, 128), 0) == 1, (_iota((8, 128), 1) >= GROUP_DIM).astype(F32), 0.0))
            for p in range(npair):
                dbs_ref[p] = lax.dot_general(sel, dbacc[p], NT, precision=lax.Precision.HIGHEST,
                                             preferred_element_type=F32)

    duv, dws, dbs, dg = pl.pallas_call(
        body, name=name, grid=(bsz, nc),
        in_specs=[pl.BlockSpec((1, CHUNK, D_GMLP), lambda b, i: (b, i, 0)),
                  pl.BlockSpec((1, CHUNK, D_GMLP), lambda b, i: (b, i, 1)),
                  pl.BlockSpec((1, CHUNK, D_GMLP), lambda b, i: (b, i, dy_col)),
                  pl.BlockSpec((GROUPS, CHUNK, CHUNK), lambda b, i: (0, 0, 0)),
                  pl.BlockSpec((GROUPS, CHUNK, CHUNK), lambda b, i: (0, 0, 0)),
                  pl.BlockSpec((npair, CHUNK, 128), lambda b, i: (0, 0, 0)),
                  pl.BlockSpec((1, D_GMLP), lambda b, i: (0, 0))],
        out_specs=[pl.BlockSpec((1, CHUNK, 2 * D_GMLP), lambda b, i: (b, i, 0)),
                   pl.BlockSpec((GROUPS, CHUNK, CHUNK), lambda b, i: (0, 0, 0)),
                   pl.BlockSpec((npair, 8, CHUNK), lambda b, i: (0, 0, 0)),
                   pl.BlockSpec((1, D_GMLP), lambda b, i: (0, 0))],
        out_shape=[jax.ShapeDtypeStruct((bsz, seq, D_IN_PAD), BF16),
                   jax.ShapeDtypeStruct((GROUPS, CHUNK, CHUNK), F32),
                   jax.ShapeDtypeStruct((npair, 8, CHUNK), F32),
                   jax.ShapeDtypeStruct((1, D_GMLP), F32)],
        scratch_shapes=[pltpu.VMEM((npair, CHUNK, 128), F32)],
        compiler_params=_cparams(),
    )(z3, z3, dyn3, ws, wst, bexp, g_out)
    return duv, dws, dbs[:, :2, :].reshape(GROUPS, CHUNK), dg


def _partner(x):
    width = x.shape[-1]
    lane = _iota(x.shape, x.ndim - 1) % HEAD_PAD
    up = pltpu.roll(x, width - ROPE // 2, x.ndim - 1)
    down = pltpu.roll(x, ROPE // 2, x.ndim - 1)
    first = jnp.logical_and(lane >= NOPE, lane < NOPE + ROPE // 2)
    second = jnp.logical_and(lane >= NOPE + ROPE // 2, lane < NOPE + ROPE)
    return jnp.where(first, up, jnp.where(second, down, 0.0))


def _mla_prep_fwd(z3, g_q, g_kv, w_uq, w_ukv, ctab, stab, *, name, tb=256):
    bsz, seq, _ = z3.shape
    tb = min(tb, seq)
    hw = HEADS * HEAD_PAD

    def body(ql_ref, kvl_ref, krl_ref, gq_ref, gkv_ref, wuq_ref, wukv_ref, c_ref, s_ref, q_ref, kv_ref, kp_ref):
        cq = _rms_fwd(ql_ref[0].astype(F32), gq_ref[...], Q_RANK).astype(BF16)
        q = _dot(cq, wuq_ref[...])
        c1, s1 = c_ref[0], s_ref[0]
        c8, s8 = jnp.tile(c1, (1, HEADS)), jnp.tile(s1, (1, HEADS))
        q_ref[0] = (q * c8 + _partner(q) * s8).astype(BF16)
        ckv = _rms_fwd(kvl_ref[0].astype(F32), gkv_ref[...], KV_RANK).astype(BF16)
        kv = _dot(ckv, wukv_ref[...])
        kv_ref[0] = kv.astype(BF16)
        kr = krl_ref[0].astype(F32)
        kr = kr * c1 + _partner(kr) * s1
        lane = _iota((tb, hw), 1) % HEAD_PAD
        kp_ref[0] = jnp.where(lane < NOPE, kv, jnp.tile(kr, (1, HEADS))).astype(BF16)

    return pl.pallas_call(
        body, name=name, grid=(bsz, seq // tb),
        in_specs=[pl.BlockSpec((1, tb, Q_RANK), lambda b, i: (b, i, 4)),
                  pl.BlockSpec((1, tb, KV_RANK), lambda b, i: (b, i, 10)),
                  pl.BlockSpec((1, tb, HEAD_PAD), lambda b, i: (b, i, 11)),
                  pl.BlockSpec((1, Q_RANK), lambda b, i: (0, 0)),
                  pl.BlockSpec((1, KV_RANK), lambda b, i: (0, 0)),
                  pl.BlockSpec((Q_RANK, hw), lambda b, i: (0, 0)),
                  pl.BlockSpec((KV_RANK, hw), lambda b, i: (0, 0)),
                  pl.BlockSpec((1, tb, HEAD_PAD), lambda b, i: (b, i, 0)),
                  pl.BlockSpec((1, tb, HEAD_PAD), lambda b, i: (b, i, 0))],
        out_specs=[pl.BlockSpec((1, tb, hw), lambda b, i: (b, i, 0))] * 3,
        out_shape=[jax.ShapeDtypeStruct((bsz, seq, hw), BF16)] * 3,
        compiler_params=_cparams(),
    )(z3, z3, z3, g_q, g_kv, w_uq, w_ukv, ctab, stab)


def _mla_prep_bwd(z3, dz3, dq3, dk3, dv3, g_q, g_kv, w_uq, w_ukv, ctab, stab, *, name, tb=256):
    bsz, seq, _ = z3.shape
    tb = min(tb, seq)
    hw = HEADS * HEAD_PAD
    nb = seq // tb

    def body(ql_ref, kvl_ref, dq_ref, dk_ref, dv_ref, gq_ref, gkv_ref, wuq_ref, wukv_ref, c_ref, s_ref, dz_in,
             dz_ref, cq_ref, dqb_ref, ckv_ref, dkvb_ref, dgq_ref, dgkv_ref):
        @pl.when(jnp.logical_and(pl.program_id(0) == 0, pl.program_id(1) == 0))
        def _():
            dgq_ref[...] = jnp.zeros_like(dgq_ref)
            dgkv_ref[...] = jnp.zeros_like(dgkv_ref)

        c1, s1 = c_ref[0], s_ref[0]
        c8, s8 = jnp.tile(c1, (1, HEADS)), jnp.tile(s1, (1, HEADS))
        dqr = dq_ref[0]
        dqb = (dqr * c8 + _partner(dqr * s8)).astype(BF16)
        dqb_ref[0] = dqb
        ql = ql_ref[0].astype(F32)
        cq_ref[0] = _rms_fwd(ql, gq_ref[...], Q_RANK).astype(BF16)
        dql, dgq = _rms_bwd(ql, gq_ref[...], _dot(dqb, wuq_ref[...], NT), Q_RANK)
        dgq_ref[...] += dgq

        dk = dk_ref[0]
        lane = _iota((tb, hw), 1) % HEAD_PAD
        dkvb = jnp.where(lane < NOPE, dk, dv_ref[0]).astype(BF16)
        dkvb_ref[0] = dkvb
        kvl = kvl_ref[0].astype(F32)
        ckv_ref[0] = _rms_fwd(kvl, gkv_ref[...], KV_RANK).astype(BF16)
        dkvl, dgkv = _rms_bwd(kvl, gkv_ref[...], _dot(dkvb, wukv_ref[...], NT), KV_RANK)
        dgkv_ref[...] += dgkv

        dkr = dk[:, 0:HEAD_PAD].astype(F32)
        for h in range(1, HEADS):
            dkr = dkr + dk[:, HEAD_PAD * h:HEAD_PAD * (h + 1)].astype(F32)
        lane1 = _iota((tb, HEAD_PAD), 1)
        dkr = jnp.where(jnp.logical_and(lane1 >= NOPE, lane1 < NOPE + ROPE), dkr, 0.0)
        dkrl = dkr * c1 + _partner(dkr * s1)
        dz_ref[0] = jnp.concatenate([dql, dkvl, dkrl], axis=1).astype(BF16)

    return pl.pallas_call(
        body, name=name, grid=(bsz, nb),
        in_specs=[pl.BlockSpec((1, tb, Q_RANK), lambda b, i: (b, i, 4)),
                  pl.BlockSpec((1, tb, KV_RANK), lambda b, i: (b, i, 10)),
                  pl.BlockSpec((1, tb, hw), lambda b, i: (b, i, 0)),
                  pl.BlockSpec((1, tb, hw), lambda b, i: (b, i, 0)),
                  pl.BlockSpec((1, tb, hw), lambda b, i: (b, i, 0)),
                  pl.BlockSpec((1, Q_RANK), lambda b, i: (0, 0)),
                  pl.BlockSpec((1, KV_RANK), lambda b, i: (0, 0)),
                  pl.BlockSpec((Q_RANK, hw), lambda b, i: (0, 0)),
                  pl.BlockSpec((KV_RANK, hw), lambda b, i: (0, 0)),
                  pl.BlockSpec((1, tb, HEAD_PAD), lambda b, i: (b, i, 0)),
                  pl.BlockSpec((1, tb, HEAD_PAD), lambda b, i: (b, i, 0)),
                  ANY_SPEC],
        out_specs=[pl.BlockSpec((1, tb, 512), lambda b, i: (b, i, 2)),
                   pl.BlockSpec((1, tb, Q_RANK), lambda b, i: (b, i, 0)),
                   pl.BlockSpec((1, tb, hw), lambda b, i: (b, i, 0)),
                   pl.BlockSpec((1, tb, KV_RANK), lambda b, i: (b, i, 0)),
                   pl.BlockSpec((1, tb, hw), lambda b, i: (b, i, 0)),
                   pl.BlockSpec((1, Q_RANK), lambda b, i: (0, 0)),
                   pl.BlockSpec((1, KV_RANK), lambda b, i: (0, 0))],
        out_shape=[jax.ShapeDtypeStruct((bsz, seq, D_IN_PAD), BF16),
                   jax.ShapeDtypeStruct((bsz, seq, Q_RANK), BF16),
                   jax.ShapeDtypeStruct((bsz, seq, hw), BF16),
                   jax.ShapeDtypeStruct((bsz, seq, KV_RANK), BF16),
                   jax.ShapeDtypeStruct((bsz, seq, hw), BF16),
                   jax.ShapeDtypeStruct((1, Q_RANK), F32),
                   jax.ShapeDtypeStruct((1, KV_RANK), F32)],
        input_output_aliases={11: 0},
        compiler_params=_cparams(),
    )(z3, z3, dq3, dk3, dv3, g_q, g_kv, w_uq, w_ukv, ctab, stab, dz3)


ATTN_HEADS_PER_STEP = 2


def _attn_specs(tq, seq, hp):
    blk = pl.BlockSpec((1, tq, hp * HEAD_PAD), lambda b, h, i: (b, i, h))
    full = pl.BlockSpec((1, seq, hp * HEAD_PAD), lambda b, h, i: (b, 0, h))
    return blk, full


def _head(h):
    return slice(HEAD_PAD * h, HEAD_PAD * (h + 1))


def _attn_fwd(q3, kv3, kp3, *, name, tq=512, hp=ATTN_HEADS_PER_STEP, side=None):
    bsz, seq, hw = q3.shape
    tq = min(tq, seq)
    blk, full = _attn_specs(tq, seq, hp)

    def body(q_ref, kv_ref, kp_ref, o_ref, lse_ref):
        i = pl.program_id(2)
        is_nope = _iota((tq, HEAD_PAD), 1) < NOPE
        causal = _iota((tq, tq), 1) <= _iota((tq, tq), 0)

        def step(j, carry, diag):
            st = pl.multiple_of(j * tq, tq)
            out = []
            for h in range(hp):
                m, l, acc = carry[h]
                kvj = kv_ref[0, pl.ds(st, tq), _head(h)]
                s = _dot(q_ref[0, :, _head(h)], kp_ref[0, pl.ds(st, tq), _head(h)], NT) * SCALE_LOG2
                if diag:
                    s = jnp.where(causal, s, -1e30)
                m_new = jnp.maximum(m, jnp.max(s, axis=1, keepdims=True))
                alpha = jnp.exp2(m - m_new)
                p = jnp.exp2(s - m_new)
                l = alpha * l + jnp.sum(p, axis=1, keepdims=True)
                acc = alpha * acc + _dot(p.astype(BF16), kvj)
                out.append((m_new, l, acc))
            return tuple(out)

        init = tuple((jnp.full((tq, 1), -1e30, F32), jnp.zeros((tq, 1), F32), jnp.zeros((tq, HEAD_PAD), F32))
                     for _ in range(hp))
        carry = lax.fori_loop(0, i, lambda j, c: step(j, c, False), init)
        carry = step(i, carry, True)
        for h in range(hp):
            m, l, acc = carry[h]
            o_ref[0, :, _head(h)] = jnp.where(is_nope, 0.0, acc / l).astype(BF16)
            lse_ref[0, :, _head(h)] = jnp.broadcast_to(m + jnp.log(l) * LOG2E, (tq, HEAD_PAD))

    outs, side_outs = _hosted_call(
        body, name=name, grid=(bsz, HEADS // hp, seq // tq),
        in_specs=[blk, full, full],
        out_specs=[blk, blk],
        out_shape=[jax.ShapeDtypeStruct((bsz, seq, hw), BF16), jax.ShapeDtypeStruct((bsz, seq, hw), F32)],
        args=(q3, kv3, kp3), side=side)
    return outs if side is None else (outs, side_outs)


def _attn_bwd(q3, kv3, kp3, do3, lse3, dl3, *, name, tq=512, hp=ATTN_HEADS_PER_STEP, side=None):
    bsz, seq, hw = q3.shape
    tq = min(tq, seq)
    nq = seq // tq
    blk, full = _attn_specs(tq, seq, hp)
    rep = tq // HEAD_PAD

    def body(kv_ref, kp_ref, q_ref, do_ref, lse_ref, dl_ref, dq_ref, dk_ref, dv_ref):
        j = pl.program_id(2)
        causal = _iota((tq, tq), 1) <= _iota((tq, tq), 0)

        @pl.when(j == 0)
        def _():
            dq_ref[...] = jnp.zeros_like(dq_ref)

        def step(i, carry, diag):
            st = pl.multiple_of(i * tq, tq)
            out = []
            for h in range(hp):
                dk, dv = carry[h]
                qi = q_ref[0, pl.ds(st, tq), _head(h)]
                do = do_ref[0, pl.ds(st, tq), _head(h)]
                kp = kp_ref[0, :, _head(h)]
                s = _dot(qi, kp, NT) * SCALE_LOG2
                if diag:
                    s = jnp.where(causal, s, -1e30)
                p = jnp.exp2(s - jnp.tile(lse_ref[0, pl.ds(st, tq), _head(h)], (1, rep)))
                dv = dv + _dot(p.astype(BF16), do, TN)
                dp = _dot(do, kv_ref[0, :, _head(h)], NT)
                ds = (p * (dp - jnp.tile(dl_ref[0, pl.ds(st, tq), _head(h)], (1, rep)))).astype(BF16)
                dk = dk + _dot(ds, qi, TN)
                dq_ref[0, pl.ds(st, tq), _head(h)] += _dot(ds, kp)
                out.append((dk, dv))
            return tuple(out)

        zero = jnp.zeros((tq, HEAD_PAD), F32)
        carry = step(j, tuple((zero, zero) for _ in range(hp)), True)
        carry = lax.fori_loop(j + 1, nq, lambda i, c: step(i, c, False), carry)
        for h in range(hp):
            dk_ref[0, :, _head(h)] = (carry[h][0] * ATTN_SCALE).astype(BF16)
            dv_ref[0, :, _head(h)] = carry[h][1].astype(BF16)

        @pl.when(j == nq - 1)
        def _():
            dq_ref[...] = dq_ref[...] * ATTN_SCALE

    outs, side_outs = _hosted_call(
        body, name=name, grid=(bsz, HEADS // hp, nq),
        in_specs=[blk, blk, full, full, full, full],
        out_specs=[full, blk, blk],
        out_shape=[jax.ShapeDtypeStruct((bsz, seq, hw), F32)] + [jax.ShapeDtypeStruct((bsz, seq, hw), BF16)] * 2,
        args=(kv3, kp3, q3, do3, lse3, dl3), side=side)
    return outs if side is None else (outs, side_outs)


def _onorm_fwd(o3, yg3, g_pad, *, name, tb=256):
    bsz, seq, hw = o3.shape
    wg = yg3.shape[-1]
    tb = min(tb, seq)

    def body(o_ref, yg_ref, g_ref, y_ref):
        ya = _rms_fwd(o_ref[0].astype(F32), g_ref[...], HEADS * 64).astype(BF16)
        y_ref[0] = jnp.concatenate([ya, yg_ref[0]], axis=1)

    return pl.pallas_call(
        body, name=name, grid=(bsz, seq // tb),
        in_specs=[pl.BlockSpec((1, tb, hw), lambda b, i: (b, i, 0)),
                  pl.BlockSpec((1, tb, wg), lambda b, i: (b, i, 0)),
                  pl.BlockSpec((1, hw), lambda b, i: (0, 0))],
        out_specs=pl.BlockSpec((1, tb, hw + wg), lambda b, i: (b, i, 0)),
        out_shape=jax.ShapeDtypeStruct((bsz, seq, hw + wg), BF16),
        compiler_params=_cparams(),
    )(o3, yg3, g_pad)


def _onorm_bwd(o3, dy3, g_pad, *, name, tb=256):
    bsz, seq, hw = o3.shape
    tb = min(tb, seq)

    def body(o_ref, dy_ref, g_ref, do_ref, dl_ref, dg_ref):
        @pl.when(jnp.logical_and(pl.program_id(0) == 0, pl.program_id(1) == 0))
        def _():
            dg_ref[...] = jnp.zeros_like(dg_ref)

        o = o_ref[0].astype(F32)
        do, dg = _rms_bwd(o, g_ref[...], dy_ref[0].astype(F32), HEADS * 64)
        dg_ref[...] += dg
        do_ref[0] = do.astype(BF16)
        prod = do * o
        parts = []
        for h in range(HEADS):
            sh = jnp.sum(prod[:, HEAD_PAD * h:HEAD_PAD * (h + 1)], axis=1, keepdims=True)
            parts.append(jnp.broadcast_to(sh, (tb, HEAD_PAD)))
        dl_ref[0] = jnp.concatenate(parts, axis=1)

    return pl.pallas_call(
        body, name=name, grid=(bsz, seq // tb),
        in_specs=[pl.BlockSpec((1, tb, hw), lambda b, i: (b, i, 0)),
                  pl.BlockSpec((1, tb, hw), lambda b, i: (b, i, 0)),
                  pl.BlockSpec((1, hw), lambda b, i: (0, 0))],
        out_specs=[pl.BlockSpec((1, tb, hw), lambda b, i: (b, i, 0)),
                   pl.BlockSpec((1, tb, hw), lambda b, i: (b, i, 0)),
                   pl.BlockSpec((1, hw), lambda b, i: (0, 0))],
        out_shape=[jax.ShapeDtypeStruct((bsz, seq, hw), BF16),
                   jax.ShapeDtypeStruct((bsz, seq, hw), F32),
                   jax.ShapeDtypeStruct((1, hw), F32)],
        compiler_params=_cparams(),
    )(o3, dy3, g_pad)


def _resnode_bwd(x3, g, *, name, target3=None, dh3=None, dres3=None, mod_nm=None, rows=None,
                 branch3=None, mod_gate=None, gate_row=None, tb=256, side=None):
    bsz, seq, d = x3.shape
    tb = min(tb, seq)
    final = target3 is not None
    has_branch = branch3 is not None
    row_spec = pl.BlockSpec((1, tb, d), lambda b, i: (b, i, 0))
    vec_spec = pl.BlockSpec((1, d), lambda b, i: (0, 0))
    mod_spec = pl.BlockSpec((1, MOD_ROWS, d), lambda b, i: (b, 0, 0))

    ins, in_specs = [x3, g], [row_spec, vec_spec]
    if final:
        ins += [target3]
        in_specs += [row_spec]
    else:
        ins += [dh3, dres3, mod_nm]
        in_specs += [row_spec, row_spec, mod_spec]
    if has_branch:
        ins += [branch3, mod_gate]
        in_specs += [row_spec, mod_spec]

    out_names = ["dx", "dg"]
    out_specs = [row_spec, vec_spec]
    out_shape = [jax.ShapeDtypeStruct((bsz, seq, d), F32), jax.ShapeDtypeStruct((1, d), F32)]
    if final:
        out_names += ["loss"]
        out_specs += [pl.BlockSpec((1, 128), lambda b, i: (0, 0))]
        out_shape += [jax.ShapeDtypeStruct((1, 128), F32)]
    else:
        out_names += ["dnm"]
        out_specs += [mod_spec]
        out_shape += [jax.ShapeDtypeStruct((bsz, MOD_ROWS, d), F32)]
    if has_branch:
        out_names += ["dbr", "dgate"]
        out_specs += [row_spec, mod_spec]
        out_shape += [jax.ShapeDtypeStruct((bsz, seq, d), BF16), jax.ShapeDtypeStruct((bsz, MOD_ROWS, d), F32)]
    n_in = len(ins)

    def body(*refs):
        r = dict(zip(["x", "g"] + (["t"] if final else ["dh", "dres", "nm"]) + (["br", "gm"] if has_branch else []),
                     refs[:n_in]))
        o = dict(zip(out_names, refs[n_in:]))
        b_first = pl.program_id(1) == 0
        first = jnp.logical_and(pl.program_id(0) == 0, b_first)
        rowid = _iota((MOD_ROWS, d), 0)

        @pl.when(first)
        def _():
            o["dg"][...] = jnp.zeros_like(o["dg"])
            if final:
                o["loss"][...] = jnp.zeros_like(o["loss"])

        @pl.when(b_first)
        def _():
            if not final:
                o["dnm"][...] = jnp.zeros_like(o["dnm"])
            if has_branch:
                o["dgate"][...] = jnp.zeros_like(o["dgate"])

        x = r["x"][0]
        gv = r["g"][...]
        if final:
            e = _rms_fwd(x, gv, d) - r["t"][0]
            sq = jnp.sum(jnp.sum(e * e, axis=1, keepdims=True), axis=0, keepdims=True)
            o["loss"][...] += jnp.broadcast_to(sq * (0.5 / d), (1, 128))
            dx, dg = _rms_bwd(x, gv, e * (1.0 / d), d)
        else:
            m = r["nm"][0]
            dh = r["dh"][0].astype(F32)
            scale = m[rows[1]:rows[1] + 1, :]
            rstd = lax.rsqrt(jnp.sum(x * x, axis=-1, keepdims=True) * (1.0 / d) + EPS)
            xh = x * rstd
            nrm = xh * gv
            dshift = jnp.sum(dh, axis=0, keepdims=True)
            dscale = jnp.sum(dh * nrm, axis=0, keepdims=True)
            o["dnm"][0] += jnp.where(rowid == 0, dshift, jnp.where(rowid == 1, dscale, 0.0))
            dn = dh * (1.0 + scale)
            dg = jnp.sum(dn * xh, axis=0, keepdims=True)
            dxh = dn * gv
            dx = rstd * (dxh - xh * (jnp.sum(dxh * xh, axis=-1, keepdims=True) * (1.0 / d))) + r["dres"][0]
        o["dg"][...] += dg
        o["dx"][0] = dx
        if has_branch:
            gate = r["gm"][0][gate_row:gate_row + 1, :]
            o["dbr"][0] = (gate * dx).astype(BF16)
            dgate = jnp.sum(dx * r["br"][0], axis=0, keepdims=True)
            o["dgate"][0] += jnp.where(rowid == 0, dgate, 0.0)

    outs, side_outs = _hosted_call(
        body, name=name, grid=(bsz, seq // tb),
        in_specs=in_specs, out_specs=out_specs, out_shape=out_shape, args=tuple(ins), side=side)
    res = dict(zip(out_names, outs))
    return res if side is None else (res, side_outs)


def _adamw(w, g, m, v, *, name):
    shape = w.shape
    cols = shape[-1]
    rows = w.size // cols
    tr = _pick_rows(rows, max(8, (256 * 1024) // cols // 8 * 8))

    def body(w_ref, g_ref, m_ref, v_ref, d_ref, nm_ref, nv_ref):
        d_ref[...], nm_ref[...], nv_ref[...] = _adamw_math(w_ref[...], g_ref[...], m_ref[...], v_ref[...])

    spec = pl.BlockSpec((tr, cols), lambda i: (i, 0))
    outs = pl.pallas_call(
        body, name=name, grid=(rows // tr,),
        in_specs=[spec] * 4, out_specs=[spec] * 3,
        out_shape=[jax.ShapeDtypeStruct((rows, cols), F32)] * 3,
        compiler_params=_cparams(),
    )(*[t.reshape(rows, cols) for t in (w, g, m, v)])
    return tuple(o.reshape(shape) for o in outs)


def _adamw_math(w, g, m, v):
    c1 = 1.0 - ADAM_B1 ** ADAM_STEP
    c2 = 1.0 - ADAM_B2 ** ADAM_STEP
    nm = ADAM_B1 * m + (1.0 - ADAM_B1) * g
    nv = ADAM_B2 * v + (1.0 - ADAM_B2) * (g * g)
    delta = -ADAM_LR * ((nm / c1) / (jnp.sqrt(nv / c2) + ADAM_EPS) + ADAM_WD * w)
    return delta, nm, nv


def _adamw_layers(w, m, v, bufs, row_off, *, name, tr=256):
    depth, rows, cols = w.shape
    tr = min(tr, rows)
    assert rows % tr == 0 and row_off % tr == 0

    outs = None
    for l in range(depth):
        def body(w_ref, g_ref, m_ref, v_ref, *rest):
            go_ref, d_ref, nm_ref, nv_ref = rest[-4:]
            g = g_ref[...]
            go_ref[...] = g
            d_ref[...], nm_ref[...], nv_ref[...] = _adamw_math(w_ref[...], g, m_ref[...], v_ref[...])

        layer = pl.BlockSpec((None, tr, cols), lambda i, l=l: (l, i, 0))
        prev = () if outs is None else tuple(outs)
        outs = pl.pallas_call(
            body, name=f"{name}_l{l}", grid=(rows // tr,),
            in_specs=[layer, pl.BlockSpec((tr, cols), lambda i: (row_off // tr + i, 0)), layer, layer]
            + [ANY_SPEC] * len(prev),
            out_specs=[layer] * 4,
            out_shape=[jax.ShapeDtypeStruct(w.shape, F32)] * 4,
            input_output_aliases={4 + k: k for k in range(len(prev))},
            compiler_params=_cparams(),
        )(w, bufs[l], m, v, *prev)
    return tuple(outs)


def _sum_leading(x, *, name, tr=256):
    n, rows, cols = x.shape
    tr = _pick_rows(rows, tr)

    def body(x_ref, o_ref):
        acc = x_ref[0]
        for k in range(1, n):
            acc = acc + x_ref[k]
        o_ref[...] = acc

    return pl.pallas_call(
        body, name=name, grid=(rows // tr,),
        in_specs=[pl.BlockSpec((n, tr, cols), lambda i: (0, i, 0))],
        out_specs=pl.BlockSpec((tr, cols), lambda i: (i, 0)),
        out_shape=jax.ShapeDtypeStruct((rows, cols), F32),
        compiler_params=_cparams(),
    )(x)


def _position():
    return lax.axis_index("x"), lax.axis_index("y"), lax.axis_index("c")


def _allgather8(x, *, name):
    shape = x.shape

    def body(x_ref, out_ref, send_sems, recv_sems, local_sem):
        px, py, pc = _position()
        me, sibling = (px, py, pc), (px, py, 1 - pc)
        chips = [(1 - px, py), (px, 1 - py), (1 - px, 1 - py)]
        src_own = x_ref

        def slot(qx, qy, qc):
            return out_ref.at[4 * qx + 2 * qy + qc]

        def copy(k, block, to, src=None):
            return pltpu.make_async_remote_copy(
                src_ref=slot(*block) if src is None else src, dst_ref=slot(*block),
                send_sem=send_sems.at[k], recv_sem=recv_sems.at[k], device_id=to, device_id_type=MESH)

        mine = pltpu.make_async_copy(src_own, slot(*me), local_sem)
        mine.start()
        first = [copy(0, me, sibling, src=src_own)]
        first += [copy(1 + j, me, (*chip, pc), src=src_own) for j, chip in enumerate(chips)]
        for cp in first:
            cp.start()
        passed = [copy(4 + j, (*chip, pc), sibling) for j, chip in enumerate(chips)]
        for j, chip in enumerate(chips):
            copy(1 + j, (*chip, pc), me).wait_recv()
            passed[j].start()
        copy(0, sibling, me).wait_recv()
        for j, chip in enumerate(chips):
            copy(4 + j, (*chip, 1 - pc), me).wait_recv()
        for cp in first + passed:
            cp.wait_send()
        mine.wait()

    return pl.pallas_call(
        body, name=name,
        out_shape=jax.ShapeDtypeStruct((N_DEV,) + shape, x.dtype),
        in_specs=[pl.BlockSpec(memory_space=pl.ANY)],
        out_specs=pl.BlockSpec(memory_space=pl.ANY),
        scratch_shapes=[pltpu.SemaphoreType.DMA((7,)), pltpu.SemaphoreType.DMA((7,)), pltpu.SemaphoreType.DMA],
    )(x)


class _Exchange:
    def __init__(self, ins, out_shapes, n, build, aliases=None):
        self.ins, self.out_shapes, self.n, self.build = tuple(ins), tuple(out_shapes), n, build
        self.aliases = dict(aliases or {})

    def _descriptors(self, in_refs, out_refs, send_sems, recv_sems):
        sends, recvs = [], []
        for k, (src, dst, peer, landing) in enumerate(self.build(in_refs, out_refs)):
            sends.append(pltpu.make_async_remote_copy(
                src_ref=src, dst_ref=dst, send_sem=send_sems.at[k], recv_sem=recv_sems.at[k],
                device_id=peer, device_id_type=MESH))
            recvs.append(pltpu.make_async_remote_copy(
                src_ref=src, dst_ref=landing, send_sem=send_sems.at[k], recv_sem=recv_sems.at[k],
                device_id=peer, device_id_type=MESH))
        return sends, recvs

    def start(self, *refs):
        for cp in self._descriptors(*refs)[0]:
            cp.start()

    def finish(self, *refs):
        sends, recvs = self._descriptors(*refs)
        for cp in recvs:
            cp.wait_recv()
        for cp in sends:
            cp.wait_send()


ANY_SPEC = pl.BlockSpec(memory_space=pl.ANY)


def _hosted_call(body, *, name, grid, in_specs, out_specs, out_shape, args, scratch_shapes=(), side=None,
                 num_scalar_prefetch=0, io_aliases=None):
    in_specs, out_specs, out_shape = list(in_specs), list(out_specs), list(out_shape)
    n_in, n_out = len(in_specs) + num_scalar_prefetch, len(out_specs)
    kernel_body = body
    aliases = dict(io_aliases or {})
    if side is not None:
        s_in, s_out = len(side.ins), len(side.out_shapes)
        aliases.update({n_in + i: n_out + o for i, o in side.aliases.items()})

        def kernel_body(*refs):
            ins, s_ins = refs[:n_in], refs[n_in:n_in + s_in]
            outs = refs[n_in + s_in:n_in + s_in + n_out]
            s_outs = refs[n_in + s_in + n_out:n_in + s_in + n_out + s_out]
            scratch, sems = refs[n_in + s_in + n_out + s_out:-2], refs[-2:]
            first = functools.reduce(jnp.logical_and, [pl.program_id(a) == 0 for a in range(len(grid))])
            last = functools.reduce(jnp.logical_and, [pl.program_id(a) == g - 1 for a, g in enumerate(grid)])

            @pl.when(first)
            def _():
                side.start(s_ins, s_outs, *sems)

            body(*ins, *outs, *scratch)

            @pl.when(last)
            def _():
                side.finish(s_ins, s_outs, *sems)

        in_specs += [ANY_SPEC] * s_in
        out_specs += [ANY_SPEC] * s_out
        out_shape += list(side.out_shapes)
        scratch_shapes = list(scratch_shapes) + [pltpu.SemaphoreType.DMA((side.n,)),
                                                 pltpu.SemaphoreType.DMA((side.n,))]
        args = tuple(args) + side.ins
    if num_scalar_prefetch:
        grid_spec = pltpu.PrefetchScalarGridSpec(num_scalar_prefetch=num_scalar_prefetch, grid=grid,
                                                 in_specs=in_specs, out_specs=out_specs,
                                                 scratch_shapes=list(scratch_shapes))
        outs = pl.pallas_call(kernel_body, name=name, grid_spec=grid_spec, out_shape=out_shape,
                              input_output_aliases=aliases, compiler_params=_cparams())(*args)
    else:
        outs = pl.pallas_call(kernel_body, name=name, grid=grid, in_specs=in_specs, out_specs=out_specs,
                              out_shape=out_shape, scratch_shapes=list(scratch_shapes),
                              input_output_aliases=aliases, compiler_params=_cparams())(*args)
    return tuple(outs[:n_out]), tuple(outs[n_out:])


def _run_exchange(ex, *, name):
    s_in = len(ex.ins)

    def body(*refs):
        ins, outs, sems = refs[:s_in], refs[s_in:-2], refs[-2:]
        ex.start(ins, outs, *sems)
        ex.finish(ins, outs, *sems)

    outs = pl.pallas_call(
        body, name=name, out_shape=list(ex.out_shapes),
        in_specs=[ANY_SPEC] * s_in, out_specs=[ANY_SPEC] * len(ex.out_shapes),
        scratch_shapes=[pltpu.SemaphoreType.DMA((ex.n,)), pltpu.SemaphoreType.DMA((ex.n,))],
        input_output_aliases=ex.aliases,
    )(*ex.ins)
    return tuple(outs)


def _other_chips(px, py):
    return [(px, 1 - py), (1 - px, py), (1 - px, 1 - py)]


def _gather_spread(w_flat):
    rows, w = w_flat.shape
    hr = rows // 2

    def build(ins, outs):
        px, py, pc = _position()
        mine = ins[0].at[pl.ds(pc * hr, hr)]
        me = 4 * px + 2 * py + pc
        plan = [((px, py, 1 - pc), me ^ 1)]
        plan += [((qx, qy, pc), 4 * qx + 2 * qy + pc) for qx, qy in _other_chips(px, py)]
        return [(mine, outs[0].at[me], peer, outs[0].at[their]) for peer, their in plan]

    return _Exchange([w_flat], [jax.ShapeDtypeStruct((N_DEV, hr, w), w_flat.dtype)], 4, build)


def _gather_pass_on(gath):
    def build(ins, outs):
        px, py, pc = _position()
        out = []
        for qx, qy in _other_chips(px, py):
            blk = 4 * qx + 2 * qy + pc
            out.append((outs[0].at[blk], outs[0].at[blk], (px, py, 1 - pc), outs[0].at[blk ^ 1]))
        return out

    return _Exchange([gath], [jax.ShapeDtypeStruct(gath.shape, gath.dtype)], 3, build, aliases={0: 0})


def _rs_halves(g):
    n, rows, w = g.shape
    hr = rows // 2

    def build(ins, outs):
        px, py, pc = _position()
        return [(ins[0].at[:, pl.ds((1 - pc) * hr, hr), :], outs[0], (px, py, 1 - pc), outs[0])]

    return _Exchange([g], [jax.ShapeDtypeStruct((n, hr, w), g.dtype)], 1, build)


def _rs_chips(sb):
    def build(ins, outs):
        px, py, pc = _position()
        return [(ins[0].at[j], outs[0].at[j], (qx, qy, pc), outs[0].at[j])
                for j, (qx, qy) in enumerate(_other_chips(px, py))]

    return _Exchange([sb], [jax.ShapeDtypeStruct(sb.shape, sb.dtype)], 3, build)


def _rs_complete(buf):
    def build(ins, outs):
        px, py, pc = _position()
        return [(outs[0].at[pc], outs[0].at[pc], (px, py, 1 - pc), outs[0].at[1 - pc])]

    return _Exchange([buf], [jax.ShapeDtypeStruct(buf.shape, buf.dtype)], 1, build, aliases={0: 0})


def _rs_partial(g, recv, ids, *, name, tr=128):
    _, rows, w = g.shape
    hr = rows // 2
    nb = hr // tr

    def body(ids_ref, g_ref, r_ref, o_ref):
        o_ref[0] = (g_ref[0] + r_ref[0]).astype(BF16)

    grid_spec = pltpu.PrefetchScalarGridSpec(
        num_scalar_prefetch=1, grid=(3, nb),
        in_specs=[pl.BlockSpec((1, tr, w), lambda j, i, ids: (ids[1] ^ (j + 1), ids[0] * nb + i, 0)),
                  pl.BlockSpec((1, tr, w), lambda j, i, ids: (ids[1] ^ (j + 1), i, 0))],
        out_specs=pl.BlockSpec((1, tr, w), lambda j, i, ids: (j, i, 0)))
    return pl.pallas_call(
        body, name=name, grid_spec=grid_spec,
        out_shape=jax.ShapeDtypeStruct((3, hr, w), BF16),
        compiler_params=_cparams(),
    )(ids, g, recv)


def _rs_total(g, recv, got, ids, *, name, tr=128):
    _, rows, w = g.shape
    hr = rows // 2
    nb = hr // tr

    def body(ids_ref, g_ref, r_ref, got_ref, o_ref):
        acc = g_ref[0] + r_ref[0]
        for j in range(3):
            acc = acc + got_ref[j].astype(F32)
        o_ref[0] = acc

    grid_spec = pltpu.PrefetchScalarGridSpec(
        num_scalar_prefetch=1, grid=(nb,),
        in_specs=[pl.BlockSpec((1, tr, w), lambda i, ids: (ids[1], ids[0] * nb + i, 0)),
                  pl.BlockSpec((1, tr, w), lambda i, ids: (ids[1], i, 0)),
                  pl.BlockSpec((3, tr, w), lambda i, ids: (0, i, 0))],
        out_specs=pl.BlockSpec((1, tr, w), lambda i, ids: (ids[0], i, 0)))
    return pl.pallas_call(
        body, name=name, grid_spec=grid_spec,
        out_shape=jax.ShapeDtypeStruct((2, hr, w), F32),
        compiler_params=_cparams(),
    )(ids, g, recv, got)


class _ReduceScatter:
    def __init__(self, g, ids, tag):
        self.g, self.ids, self.tag, self.stage, self.result = g, ids, tag, 0, None

    def next_exchange(self):
        if self.stage == 0:
            return _rs_halves(self.g)
        if self.stage == 1:
            return _rs_chips(self.sb)
        return _rs_complete(self.buf)

    def done(self, outs):
        if self.stage == 0:
            self.recv = outs[0]
            hr = self.recv.shape[1]
            self.tr = max(t for t in range(16, 513, 16) if hr % t == 0)
            self.sb = _rs_partial(self.g, self.recv, self.ids, name=f"{self.tag}_partial", tr=self.tr)
        elif self.stage == 1:
            self.buf = _rs_total(self.g, self.recv, outs[0], self.ids, name=f"{self.tag}_total", tr=self.tr)
        else:
            _, hr, w = outs[0].shape
            self.result = outs[0].reshape(2 * hr, w)
        self.stage += 1

    def finish_alone(self):
        names = ("halves", "chips", "complete")
        while self.stage < 3:
            self.done(_run_exchange(self.next_exchange(), name=f"{self.tag}_{names[self.stage]}"))
        return self.result


def _flat_rows():
    used = sum(r for _, r in FSDP_SECTIONS)
    return used, -(-used // ROW_ALIGN) * ROW_ALIGN


def _cols_to_chunks(full):
    rows, cols = full.shape
    t = full.reshape(rows, N_CHIPS, cols // N_CHIPS).transpose(1, 0, 2)
    return t.reshape(N_CHIPS, -1, FLAT_W)


def _chunks_to_cols(chunks, rows, cols):
    return chunks.reshape(N_CHIPS, rows, cols // N_CHIPS).transpose(1, 0, 2).reshape(rows, cols)


def _pad_heads(w, real):
    lead = w.shape[:-1]
    t = w.reshape(lead + (HEADS, real))
    t = jnp.pad(t, [(0, 0)] * len(lead) + [(0, 0), (0, HEAD_PAD - real)])
    return t.reshape(lead + (HEADS * HEAD_PAD,))


def _unpad_heads(w, real):
    lead = w.shape[:-1]
    return w.reshape(lead + (HEADS, HEAD_PAD))[..., :real].reshape(lead + (HEADS * real,))


def _pad_value_lanes(w, axis):
    w = jnp.moveaxis(w, axis, -1)
    lead = w.shape[:-1]
    t = w.reshape(lead + (HEADS, 64))
    t = jnp.pad(t, [(0, 0)] * len(lead) + [(0, 0), (HEAD_PAD - 64, 0)])
    return jnp.moveaxis(t.reshape(lead + (HEADS * HEAD_PAD,)), -1, axis)


def _unpad_value_lanes(w, axis):
    w = jnp.moveaxis(w, axis, -1)
    lead = w.shape[:-1]
    t = w.reshape(lead + (HEADS, HEAD_PAD))[..., HEAD_PAD - 64:]
    return jnp.moveaxis(t.reshape(lead + (HEADS * 64,)), -1, axis)


def _pad_w_in_t(wt):
    z = jnp.zeros((NOPE, wt.shape[1]), wt.dtype)
    z2 = jnp.zeros((HEAD_PAD - NOPE - ROPE, wt.shape[1]), wt.dtype)
    return jnp.concatenate([wt[:1408], z, wt[1408:], z2], axis=0)


def _unpad_w_in_t(wt):
    return jnp.concatenate([wt[:1408], wt[1408 + NOPE:1408 + NOPE + ROPE]], axis=0)


def _rope_tables(positions):
    freqs = ROPE_THETA ** (-jnp.arange(0, ROPE, 2, dtype=F32) / ROPE)
    ang = positions.astype(F32)[..., None] * freqs
    cos, sin = jnp.cos(ang), jnp.sin(ang)
    lead = cos.shape[:-1]
    ones = jnp.ones(lead + (NOPE,), F32)
    zeros_n = jnp.zeros(lead + (NOPE,), F32)
    zeros_p = jnp.zeros(lead + (HEAD_PAD - NOPE - ROPE,), F32)
    ctab = jnp.concatenate([ones, cos, cos, zeros_p], axis=-1)
    stab = jnp.concatenate([zeros_n, -sin, sin, zeros_p], axis=-1)
    return ctab, stab


def _layer_weights(full, p, l):
    ws = p["gmlp_ws"][l]
    tril = jnp.tril(jnp.ones((CHUNK, CHUNK), bool))
    bs = p["gmlp_bs"][l]
    bexp = jnp.repeat(bs.reshape(GROUPS // 2, 2, CHUNK).transpose(0, 2, 1), GROUP_DIM, axis=2)
    return dict(
        w_in_t=_pad_w_in_t(full["w_in_t"]),
        w_uq=_pad_heads(full["mla_w_uq"], NOPE + ROPE),
        w_ukv=full["mla_w_ukv"],
        w_out=jnp.concatenate([_pad_value_lanes(full["w_out"][D_GMLP:], 0), full["w_out"][:D_GMLP]], axis=0),
        ws=ws,
        wst=jnp.where(tril[None], ws, 0.0).transpose(0, 2, 1).astype(BF16),
        bexp=bexp,
        g_mix=p["norm_mix_g"][l][None],
        g_ffn=p["norm_ffn_g"][l][None],
        g_q=p["mla_q_norm_g"][l][None],
        g_kv=p["mla_kv_norm_g"][l][None],
        g_og=p["out_norm_gmlp_g"][l][None],
        g_oa=_pad_value_lanes(p["out_norm_mla_g"][l], 0)[None],
    )


def _local_step(x3, target3, positions, mods, final_g, plan):
    bsz, seq, d = x3.shape
    tok = bsz * seq
    tmt = min(512, seq)
    tmk = min(1024, seq)
    chunk = (None, None, FLAT_W, FLAT_W)
    ff_grad_shape = (N_CHIPS, 2 * FLAT_W, FLAT_W)
    ctab, stab = _rope_tables(positions)
    lw = [None] * DEPTH

    def flat(t):
        return t.reshape(tok, t.shape[-1])

    def cube(t):
        return t.reshape(bsz, seq, t.shape[-1])

    def carrying(l, tag, fn, *args, **kw):
        side = plan.host(l, tag)
        if side is None:
            return fn(*args, **kw)
        res, side_outs = fn(*args, side=side, **kw)
        plan.hosted(l, tag, side_outs)
        return res

    saved = []
    x = x3
    for l in range(DEPTH):
        lw[l] = plan.layer(l)
        w, mod = lw[l], mods[l]
        h1 = _normmod_fwd(x, w["g_mix"], mod, SHIFT1, SCALE1, name=f"l{l}_normmod1")
        z = cube(_mm(flat(h1), w["w_in_t"], dims="nt", name=f"l{l}_w_in", tm=tmt, tn=D_IN_PAD, tk=d,
                     out_dtypes=(BF16,)))
        yg = _gmlp_fwd(z, w["ws"], w["bexp"], w["g_og"], name=f"l{l}_gmlp_fwd")
        q, kv, kp = _mla_prep_fwd(z, w["g_q"], w["g_kv"], w["w_uq"], w["w_ukv"], ctab, stab, name=f"l{l}_mla_prep")
        o, lse = carrying(l, "fwd_attn", _attn_fwd, q, kv, kp, name=f"l{l}_attn_fwd")
        y = _onorm_fwd(o, yg, w["g_oa"], name=f"l{l}_onorm_fwd")

        def out_epi(po, xv, gm):
            return po, xv + gm[0][GATE1:GATE1 + 1, :] * po

        po, x_mid = carrying(l, "fwd_out_a", _mm, flat(y), w["w_out"], dims="nn", name=f"l{l}_w_out",
                             tm=tmt, tn=d, tk=y.shape[-1], out_dtypes=(BF16, F32), epilogue=out_epi,
                             extras=(flat(x), mod), extra_specs=(None, _mod_spec(tmt, d, seq)))
        x_mid = cube(x_mid)
        h2 = _normmod_fwd(x_mid, w["g_ffn"], mod, SHIFT2, SCALE2, name=f"l{l}_normmod2")

        def act_epi(acc):
            r = jnp.maximum(acc, 0.0)
            return (r * r,)

        r = carrying(l, "fwd_ff1", _mm, flat(h2), w["ff"], dims="nn", name=f"l{l}_w_ff1", tm=tmt, tn=FLAT_W,
                     tk=d, out_dtypes=(BF16,), epilogue=act_epi, weights_outer=True, n=D_FF,
                     b_block=(chunk, lambda i, j, k: (j, 0, 0, 0)))

        def ff2_epi(acc, xv, gm):
            return acc, xv + gm[0][GATE2:GATE2 + 1, :] * acc

        f, x_out = carrying(l, "fwd_ff2", _mm, r, w["ff"], dims="nn", name=f"l{l}_w_ff2", tm=tmk, tn=d, tk=FLAT_W,
                            out_dtypes=(BF16, F32), epilogue=ff2_epi, extras=(flat(x_mid), mod),
                            extra_specs=(None, _mod_spec(tmk, d, seq)), n=d,
                            b_block=(chunk, lambda i, j, k: (k, 1, 0, 0)))
        saved.append(dict(x_in=x, h1=h1, z=z, q=q, kv=kv, kp=kp, o=o, lse=lse, y=y, po=cube(po),
                          x_mid=x_mid, h2=h2, r=r, f=cube(f)))
        x = cube(x_out)

    grads = [dict() for _ in range(DEPTH)]
    dmods = [None] * DEPTH
    top = DEPTH - 1
    node = _resnode_bwd(x, final_g[None], name="final_loss_bwd", target3=target3,
                        branch3=saved[top]["f"], mod_gate=mods[top], gate_row=GATE2)
    loss_part = node["loss"][0, 0]
    d_final_g = node["dg"][0]
    for l in range(DEPTH - 1, -1, -1):
        w, mod, s = lw[l], mods[l], saved[l]
        dx_out, dfb, dgate2 = node["dx"], flat(node["dbr"]), node["dgate"][:, 0]

        def dact_epi(acc, rv):
            return (acc * (2.0 * jnp.sqrt(rv.astype(F32))),)

        da = carrying(l, "bwd_d_r", _mm, dfb, w["ff"], dims="nt", name=f"l{l}_d_r", tm=tmt, tn=FLAT_W, tk=d,
                      out_dtypes=(BF16,), epilogue=dact_epi, extras=(s["r"],), weights_outer=True, n=D_FF,
                      b_block=(chunk, lambda i, j, k: (j, 1, 0, 0)))
        g_ff = carrying(l, "bwd_dw_ff2", _mm, s["r"], dfb, dims="tn", name=f"l{l}_dw_ff2", tm=FLAT_W, tn=d,
                        tk=1024, out_into=(ff_grad_shape, (None, FLAT_W, FLAT_W), lambda i, j, k: (i, 1, 0), None))
        g_ff = carrying(l, "bwd_dw_ff1", _mm, flat(s["h2"]), da, dims="tn", name=f"l{l}_dw_ff1", tm=d, tn=FLAT_W,
                        tk=1024, out_into=(ff_grad_shape, (None, FLAT_W, FLAT_W), lambda i, j, k: (j, 0, 0), g_ff))
        plan.ff_grads(l, g_ff)
        dh2 = carrying(l, "bwd_d_h2", _mm, da, w["ff"], dims="nt", name=f"l{l}_d_h2", tm=tmk, tn=d, tk=FLAT_W,
                       n=d, b_block=(chunk, lambda i, j, k: (k, 0, 0, 0)), out_dtypes=(BF16,))
        node = _resnode_bwd(s["x_mid"], w["g_ffn"], name=f"l{l}_resnode_ffn", dh3=cube(dh2), dres3=dx_out,
                            mod_nm=mod, rows=(SHIFT2, SCALE2), branch3=s["po"], mod_gate=mod, gate_row=GATE1)
        grads[l]["norm_ffn_g"] = node["dg"][0]
        dshift2, dscale2 = node["dnm"][:, 0], node["dnm"][:, 1]
        dx_mid, dpo, dgate1 = node["dx"], flat(node["dbr"]), node["dgate"][:, 0]

        wy = s["y"].shape[-1]
        dy = cube(_mm(dpo, w["w_out"], dims="nt", name=f"l{l}_d_y", tm=tmt, tn=wy, tk=d, out_dtypes=(BF16,)))
        dw_out = _mm(flat(s["y"]), dpo, dims="tn", name=f"l{l}_dw_out", tm=wy // 3, tn=d, tk=1024)
        hw = HEADS * HEAD_PAD
        grads[l]["w_out"] = jnp.concatenate([dw_out[hw:], _unpad_value_lanes(dw_out[:hw], 0)], axis=0)

        dz, dws, dbs, dg_og = _gmlp_bwd(s["z"], dy, w["ws"], w["wst"], w["bexp"], w["g_og"],
                                        name=f"l{l}_gmlp_bwd", dy_col=hw // D_GMLP)
        grads[l]["gmlp_ws"], grads[l]["gmlp_bs"], grads[l]["out_norm_gmlp_g"] = dws, dbs, dg_og[0]

        do, dl, dg_oa = _onorm_bwd(s["o"], dy, w["g_oa"], name=f"l{l}_onorm_bwd")
        grads[l]["out_norm_mla_g"] = _unpad_value_lanes(dg_oa[0], 0)
        dq, dk, dv = carrying(l, "bwd_attn_dkv", _attn_bwd, s["q"], s["kv"], s["kp"], do, s["lse"], dl,
                              name=f"l{l}_attn_bwd")
        dz, cq, dqb, ckv, dkvb, dg_q, dg_kv = _mla_prep_bwd(
            s["z"], dz, dq, dk, dv, w["g_q"], w["g_kv"], w["w_uq"], w["w_ukv"], ctab, stab,
            name=f"l{l}_mla_prep_bwd")
        grads[l]["mla_q_norm_g"], grads[l]["mla_kv_norm_g"] = dg_q[0], dg_kv[0]
        dw_uq = carrying(l, "bwd_dw_uq", _mm, flat(cq), flat(dqb), dims="tn", name=f"l{l}_dw_uq", tm=Q_RANK,
                         tn=1024, tk=1024)
        grads[l]["mla_w_uq"] = _unpad_heads(dw_uq, NOPE + ROPE)
        grads[l]["mla_w_ukv"] = _mm(flat(ckv), flat(dkvb), dims="tn", name=f"l{l}_dw_ukv", tm=KV_RANK, tn=1024, tk=1024)

        grads[l]["w_in_t"] = _unpad_w_in_t(_mm(flat(dz), flat(s["h1"]), dims="tn", name=f"l{l}_dw_in",
                                               tm=D_IN_PAD // 2, tn=d, tk=1024))
        plan.layer_grads(l, grads[l])
        dh1 = carrying(l, "bwd_d_h1", _mm, flat(dz), w["w_in_t"], dims="nn", name=f"l{l}_d_h1", tm=tmt, tn=d,
                       tk=D_IN_PAD, out_dtypes=(BF16,))
        below = dict(branch3=saved[l - 1]["f"], mod_gate=mods[l - 1], gate_row=GATE2) if l > 0 else {}
        node = carrying(l, "bwd_resnode_mix", _resnode_bwd, s["x_in"], w["g_mix"], name=f"l{l}_resnode_mix",
                        dh3=cube(dh1), dres3=dx_mid, mod_nm=mod, rows=(SHIFT1, SCALE1), **below)
        grads[l]["norm_mix_g"] = node["dg"][0]
        dshift1, dscale1 = node["dnm"][:, 0], node["dnm"][:, 1]
        dmods[l] = jnp.stack([dshift1, dscale1, dgate1, dshift2, dscale2, dgate2], axis=1)
        plan.layer_done(l)
    return loss_part, node["dx"], d_final_g, dmods


W_NAMES = ("w_ada", "b_ada", "norm_mix_g", "w_in", "gmlp_ws", "gmlp_bs", "mla_q_norm_g", "mla_kv_norm_g",
           "mla_w_uq", "mla_w_ukv", "out_norm_gmlp_g", "out_norm_mla_g", "w_out", "norm_ffn_g", "w_ff1", "w_ff2",
           "final_norm_g")
FLAT_KEY = {"w_in": "w_in", "w_uq": "mla_w_uq", "w_ukv": "mla_w_ukv", "w_out": "w_out", "w_ff1": "w_ff1",
            "w_ff2": "w_ff2"}
COL_SHARDED = ("w_in", "w_uq", "w_ukv", "w_ff1")
FULL_SHAPE = {"w_in": (D_MODEL, D_IN), "w_uq": (Q_RANK, HEADS * (NOPE + ROPE)), "w_ukv": (KV_RANK, HEADS * 128),
              "w_out": (D_MODEL, D_MODEL), "w_ff1": (D_MODEL, D_FF), "w_ff2": (D_FF, D_MODEL)}
SMALL_NAMES = ("norm_mix_g", "gmlp_ws", "gmlp_bs", "mla_q_norm_g", "mla_kv_norm_g", "out_norm_gmlp_g",
               "out_norm_mla_g", "norm_ffn_g", "final_norm_g")


def _silu(v):
    return v * (1.0 / (1.0 + jnp.exp(-v)))


class _CommPlan:
    FWD = {"fwd_attn": ("ff", 0, "spread"), "fwd_out_a": ("ff", 0, "pass"),
           "fwd_ff1": ("mix", 1, "spread"), "fwd_ff2": ("mix", 1, "pass")}
    BWD = {"bwd_d_r": ("mix", 1), "bwd_dw_ff2": ("mix", 1), "bwd_dw_ff1": ("mix", 1),
           "bwd_d_h2": ("ff", 0), "bwd_attn_dkv": ("ff", 0), "bwd_dw_uq": ("ff", 0)}
    BWD_LAST = {"bwd_d_h1": ("mix", 0), "bwd_resnode_mix": ("mix", 0)}

    def __init__(self, weights, ids, dev, core):
        self.weights, self.ids, self.dev, self.core = weights, ids, dev, core
        self.used, self.rows = _flat_rows()
        self.flat = {("mix", l): self._flat_mix(l) for l in range(DEPTH)}
        self.flat.update({("ff", l): jnp.concatenate([weights["w_ff1"][l], weights["w_ff2"][l]], axis=0).astype(BF16)
                          for l in range(DEPTH)})
        self.lw, self.rs, self.grads, self.spread = {}, {}, {}, {}
        (gath,) = _run_exchange(_gather_spread(self.flat["mix", 0]), name="l0_mix_gather_spread")
        (gath,) = _run_exchange(_gather_pass_on(gath), name="l0_mix_gather_pass_on")
        self._arrived("mix", 0, gath)

    def _flat_mix(self, l):
        pieces = []
        for nm, _ in FSDP_SECTIONS:
            shard = self.weights[FLAT_KEY[nm]][l]
            pieces.append(shard.T if nm == "w_in" else shard.reshape(-1, FLAT_W))
        pieces.append(jnp.zeros((self.rows - self.used, FLAT_W), F32))
        return jnp.concatenate(pieces, axis=0).astype(BF16)

    def _arrived(self, group, l, gath):
        flat = self.flat[group, l]
        hr = flat.shape[0] // 2
        mine = lax.dynamic_slice(flat, (self.core * hr, 0), (hr, FLAT_W))
        gath = lax.dynamic_update_slice(gath, mine[None], (self.dev, 0, 0))
        if group == "ff":
            self.lw[l]["ff"] = gath.reshape(N_CHIPS, 2, hr, FLAT_W)
            return
        w_gath = gath.reshape(N_CHIPS, self.rows, FLAT_W)
        full, off = {}, 0
        for nm, nrows in FSDP_SECTIONS:
            sec = w_gath[:, off:off + nrows]
            off += nrows
            rows, cols = FULL_SHAPE[nm]
            if nm == "w_in":
                full["w_in_t"] = sec.reshape(cols, rows)
            else:
                full[FLAT_KEY[nm]] = (_chunks_to_cols(sec, rows, cols) if nm in COL_SHARDED
                                      else sec.reshape(rows, cols))
        self.lw[l] = _layer_weights(full, self.weights, l)

    def layer(self, l):
        return self.lw[l]

    def host(self, l, tag):
        if tag in self.FWD:
            group, ahead, what = self.FWD[tag]
            if l + ahead >= DEPTH:
                return None
            return _gather_spread(self.flat[group, l + ahead]) if what == "spread" else _gather_pass_on(self.spread[group])
        rs = self._rs_for(l, tag)
        return None if rs is None or rs.stage > 2 else rs.next_exchange()

    def _rs_for(self, l, tag):
        if tag in self.BWD_LAST:
            return self.rs.get(self.BWD_LAST[tag]) if l == 0 else None
        group, ahead = self.BWD[tag]
        return self.rs.get((group, l + ahead))

    def hosted(self, l, tag, outs):
        if tag in self.FWD:
            group, ahead, what = self.FWD[tag]
            if what == "spread":
                self.spread[group] = outs[0]
            else:
                self._arrived(group, l + ahead, outs[0])
        else:
            self._rs_for(l, tag).done(outs)

    def ff_grads(self, l, g_ff):
        self.rs["ff", l] = _ReduceScatter(g_ff, self.ids, f"l{l}_ff_rs")

    def layer_grads(self, l, grads):
        self.grads[l] = grads
        pieces = []
        for nm, nrows in FSDP_SECTIONS:
            if nm == "w_in":
                pieces.append(grads["w_in_t"].reshape(N_CHIPS, nrows, FLAT_W))
                continue
            g = grads[FLAT_KEY[nm]]
            pieces.append(_cols_to_chunks(g) if nm in COL_SHARDED else g.reshape(N_CHIPS, nrows, FLAT_W))
        pieces.append(jnp.zeros((N_CHIPS, self.rows - self.used, FLAT_W), F32))
        self.rs["mix", l] = _ReduceScatter(jnp.concatenate(pieces, axis=1), self.ids, f"l{l}_mix_rs")

    def layer_done(self, l):
        if l == 0:
            self.rs["mix", 0].finish_alone()

    def mix_grads(self):
        per = {FLAT_KEY[nm]: [] for nm, _ in FSDP_SECTIONS}
        for l in range(DEPTH):
            shard, off = self.rs["mix", l].result, 0
            for nm, nrows in FSDP_SECTIONS:
                key = FLAT_KEY[nm]
                sec = shard[off:off + nrows]
                per[key].append(sec.T if nm == "w_in" else sec.reshape(self.weights[key].shape[1:]))
                off += nrows
        return {key: jnp.stack(parts, axis=0) for key, parts in per.items()}

    def ff_shards(self):
        return [self.rs["ff", l].result for l in range(DEPTH)]


def kernel(x, c, positions, w_ada, b_ada, norm_mix_g, w_in, gmlp_ws, gmlp_bs, mla_q_norm_g, mla_kv_norm_g, mla_w_uq, mla_w_ukv, out_norm_gmlp_g, out_norm_mla_g, w_out, norm_ffn_g, w_ff1, w_ff2, final_norm_g, loss_target, m_w_ada, m_b_ada, m_norm_mix_g, m_w_in, m_gmlp_ws, m_gmlp_bs, m_mla_q_norm_g, m_mla_kv_norm_g, m_mla_w_uq, m_mla_w_ukv, m_out_norm_gmlp_g, m_out_norm_mla_g, m_w_out, m_norm_ffn_g, m_w_ff1, m_w_ff2, m_final_norm_g, v_w_ada, v_b_ada, v_norm_mix_g, v_w_in, v_gmlp_ws, v_gmlp_bs, v_mla_q_norm_g, v_mla_kv_norm_g, v_mla_w_uq, v_mla_w_ukv, v_out_norm_gmlp_g, v_out_norm_mla_g, v_w_out, v_norm_ffn_g, v_w_ff1, v_w_ff2, v_final_norm_g):
    weights = dict(w_ada=w_ada, b_ada=b_ada, norm_mix_g=norm_mix_g, w_in=w_in, gmlp_ws=gmlp_ws, gmlp_bs=gmlp_bs,
                   mla_q_norm_g=mla_q_norm_g, mla_kv_norm_g=mla_kv_norm_g, mla_w_uq=mla_w_uq, mla_w_ukv=mla_w_ukv,
                   out_norm_gmlp_g=out_norm_gmlp_g, out_norm_mla_g=out_norm_mla_g, w_out=w_out,
                   norm_ffn_g=norm_ffn_g, w_ff1=w_ff1, w_ff2=w_ff2, final_norm_g=final_norm_g)
    mom_m = dict(zip(W_NAMES, (m_w_ada, m_b_ada, m_norm_mix_g, m_w_in, m_gmlp_ws, m_gmlp_bs, m_mla_q_norm_g,
                               m_mla_kv_norm_g, m_mla_w_uq, m_mla_w_ukv, m_out_norm_gmlp_g, m_out_norm_mla_g,
                               m_w_out, m_norm_ffn_g, m_w_ff1, m_w_ff2, m_final_norm_g)))
    mom_v = dict(zip(W_NAMES, (v_w_ada, v_b_ada, v_norm_mix_g, v_w_in, v_gmlp_ws, v_gmlp_bs, v_mla_q_norm_g,
                               v_mla_kv_norm_g, v_mla_w_uq, v_mla_w_ukv, v_out_norm_gmlp_g, v_out_norm_mla_g,
                               v_w_out, v_norm_ffn_g, v_w_ff1, v_w_ff2, v_final_norm_g)))
    bsz, seq, d = x.shape
    px, py, pc = _position()
    chip = 2 * px + py
    dev = 2 * chip + pc
    ids = jnp.stack([pc, chip]).astype(jnp.int32)
    n_ex = N_DEV * bsz
    ada_cols = w_ada.shape[-1]

    c_all = _allgather8(c.reshape(bsz * d // 128, 128), name="gather_c").reshape(n_ex, d)
    mod_parts = []
    for l in range(DEPTH):
        bias = lax.dynamic_slice(b_ada[l], (chip * ada_cols,), (ada_cols,))[None]
        mod_parts.append(_mm(c_all, w_ada[l], dims="nn", name=f"l{l}_mod", tm=n_ex, tn=ada_cols, tk=d,
                             epilogue=lambda acc, bv: (acc + bv,), extras=(bias,),
                             extra_specs=(pl.BlockSpec((1, ada_cols), lambda i, j, k: (0, j)),), a_fn=_silu))
    mod_g = _allgather8(jnp.concatenate(mod_parts, axis=0), name="gather_mod")
    mod_g = mod_g.reshape(N_CHIPS, 2, DEPTH, n_ex, ada_cols)[:, 0]
    mod_full = mod_g.transpose(1, 2, 0, 3).reshape(DEPTH, n_ex, N_CHIPS * ada_cols)
    mod_mine = lax.dynamic_slice(mod_full, (0, dev * bsz, 0), (DEPTH, bsz, N_MOD * d))
    mod_mine = jnp.pad(mod_mine.reshape(DEPTH, bsz, N_MOD, d), ((0, 0), (0, 0), (0, MOD_ROWS - N_MOD), (0, 0)))
    mods = [mod_mine[l] for l in range(DEPTH)]

    plan = _CommPlan(weights, ids, dev, pc)
    loss_part, grad_x, d_final_g, dmods = _local_step(x, loss_target, positions, mods, final_norm_g, plan)
    grads = plan.grads
    grad = plan.mix_grads()

    small = {nm: (d_final_g if nm == "final_norm_g" else jnp.stack([grads[l][nm] for l in range(DEPTH)], axis=0))
             for nm in SMALL_NAMES}
    svec = jnp.concatenate([small[nm].reshape(-1) for nm in SMALL_NAMES] + [loss_part[None]])
    n_small = svec.shape[0]
    srows = -(-n_small // (8 * FLAT_W)) * 8
    svec = jnp.pad(svec, (0, srows * FLAT_W - n_small)).reshape(srows, FLAT_W)
    ssum = _sum_leading(_allgather8(svec, name="gather_small_grads"), name="sum_small_grads").reshape(-1)
    off = 0
    for nm in SMALL_NAMES:
        size = weights[nm].size
        grad[nm] = ssum[off:off + size].reshape(weights[nm].shape)
        off += size
    loss = ssum[off]

    dmod = jnp.stack(dmods, axis=1).reshape(bsz * DEPTH * N_MOD, d)
    dmod_all = _allgather8(dmod, name="gather_dmod").reshape(n_ex, DEPTH, N_MOD * d)
    gw, gb = [], []
    for l in range(DEPTH):
        dm = dmod_all[:, l]
        dm_cols = lax.dynamic_slice(dm, (0, chip * ada_cols), (n_ex, ada_cols))
        gw.append(_mm(c_all, dm_cols, dims="tn", name=f"l{l}_dw_ada", tm=d, tn=ada_cols, tk=n_ex, a_fn=_silu))
        gb.append(_sum_leading(dm.reshape(n_ex, N_MOD * d // FLAT_W, FLAT_W), name=f"l{l}_db_ada").reshape(-1))
    grad["w_ada"] = jnp.stack(gw, axis=0)
    grad["b_ada"] = jnp.stack(gb, axis=0)

    delta, new_m, new_v = {}, {}, {}
    ff_bufs = plan.ff_shards()
    for nm, row_off in (("w_ff1", 0), ("w_ff2", FLAT_W)):
        grad[nm], delta[nm], new_m[nm], new_v[nm] = _adamw_layers(
            weights[nm], mom_m[nm], mom_v[nm], ff_bufs, row_off, name=f"adamw_{nm}")
    for nm in W_NAMES:
        if nm not in delta:
            delta[nm], new_m[nm], new_v[nm] = _adamw(weights[nm], grad[nm], mom_m[nm], mom_v[nm],
                                                     name=f"adamw_{nm}")
    return (loss, grad_x, *[grad[nm] for nm in W_NAMES], *[delta[nm] for nm in W_NAMES],
            *[new_m[nm] for nm in W_NAMES], *[new_v[nm] for nm in W_NAMES])
```

```python
import functools
import math

import jax
import jax.numpy as jnp
from jax import lax
from jax.experimental import pallas as pl
from jax.experimental.pallas import tpu as pltpu

F32 = jnp.float32
BF16 = jnp.bfloat16

D_MODEL = 1024
DEPTH = 2
D_GMLP = 512
GROUPS = 8
GROUP_DIM = 64
CHUNK = 128
HEADS = 8
NOPE = 64
ROPE = 32
HEAD_PAD = 128
Q_RANK = 256
KV_RANK = 128
D_FF = 4096
N_MOD = 6
MOD_ROWS = 8
EPS = 1e-6
ROPE_THETA = 10000.0
D_IN = 1440
D_IN_PAD = 1536
ATTN_SCALE = (NOPE + ROPE) ** -0.5
LOG2E = math.log2(math.e)
SCALE_LOG2 = ATTN_SCALE * LOG2E
N_CHIPS = 4
N_DEV = 8

ADAM_LR = 0.001
ADAM_B1 = 0.9
ADAM_B2 = 0.999
ADAM_EPS = 1e-08
ADAM_WD = 0.01
ADAM_STEP = 10

VMEM_LIMIT = 48 * 1024 * 1024
FLAT_W = 1024
ROW_ALIGN = 256

NN = (((1,), (0,)), ((), ()))
NT = (((1,), (1,)), ((), ()))
TN = (((0,), (0,)), ((), ()))
MESH = pl.DeviceIdType.MESH

SHIFT1, SCALE1, GATE1, SHIFT2, SCALE2, GATE2 = range(6)

FSDP_SECTIONS = (("w_out", 256), ("w_in", 360), ("w_uq", 48), ("w_ukv", 32))


def _cparams(vmem=VMEM_LIMIT):
    return pltpu.CompilerParams(vmem_limit_bytes=vmem)


def _dot(a, b, dims=NN):
    return lax.dot_general(a, b, dims, preferred_element_type=F32)


def _iota(shape, axis):
    return lax.broadcasted_iota(jnp.int32, shape, axis)


def _gelu(x):
    k = math.sqrt(2.0 / math.pi)
    return 0.5 * x * (1.0 + jnp.tanh(k * (x + 0.044715 * (x * x * x))))


def _gelu_grad(x):
    k = math.sqrt(2.0 / math.pi)
    t = jnp.tanh(k * (x + 0.044715 * (x * x * x)))
    return 0.5 * (1.0 + t) + 0.5 * x * (1.0 - t * t) * (k * (1.0 + 3.0 * 0.044715 * (x * x)))


def _rms_fwd(x, g, n):
    r = lax.rsqrt(jnp.sum(x * x, axis=-1, keepdims=True) * (1.0 / n) + EPS)
    return x * r * g


def _rms_bwd(x, g, dy, n):
    r = lax.rsqrt(jnp.sum(x * x, axis=-1, keepdims=True) * (1.0 / n) + EPS)
    xh = x * r
    dxh = dy * g
    dx = r * (dxh - xh * (jnp.sum(dxh * xh, axis=-1, keepdims=True) * (1.0 / n)))
    dg = jnp.sum(dy * xh, axis=0, keepdims=True)
    return dx, dg


def _pick_rows(rows, limit):
    if rows <= limit:
        return rows
    for t in range(limit, 7, -8):
        if rows % t == 0:
            return t
    return rows


def _mm(a, b, *, dims, name, tm=512, tn=1024, tk=1024, out_dtypes=(F32,), epilogue=None,
        extras=(), extra_specs=(), a_fn=None, weights_outer=False, side=None, b_block=None, n=None,
        out_into=None):
    if dims == "tn":
        kk, m = a.shape
    else:
        m, kk = a.shape
    if n is None:
        n = b.shape[0] if dims == "nt" else b.shape[1]
    tm, tn, tk = min(tm, m), min(tn, n), min(tk, kk)
    assert m % tm == 0 and n % tn == 0 and kk % tk == 0, (name, a.shape, b.shape, tm, tn, tk)
    ni, nj, nk = m // tm, n // tn, kk // tk

    def spec(shape, pick):
        if weights_outer:
            return pl.BlockSpec(shape, lambda j, i, k: pick(i, j, k))
        return pl.BlockSpec(shape, pick)

    if dims == "tn":
        a_spec = spec((tk, tm), lambda i, j, k: (k, i))
    else:
        a_spec = spec((tm, tk), lambda i, j, k: (i, k))
    if b_block is not None:
        b_spec = spec(*b_block)
    elif dims == "nt":
        b_spec = spec((tn, tk), lambda i, j, k: (j, k))
    else:
        b_spec = spec((tk, tn), lambda i, j, k: (k, j))
    o_spec = spec((tm, tn), lambda i, j, k: (i, j))
    out_shape = [jax.ShapeDtypeStruct((m, n), dt) for dt in out_dtypes]
    out_specs = [o_spec] * len(out_dtypes)
    prev, io_aliases = (), {}
    if out_into is not None:
        full_shape, block, index, before = out_into
        assert len(out_dtypes) == 1 and not extras
        out_shape = [jax.ShapeDtypeStruct(full_shape, out_dtypes[0])]
        out_specs = [spec(block, index)]
        if before is not None:
            prev, io_aliases = (before,), {2: 0}
    assert not (weights_outer and extra_specs)
    dn = {"nn": NN, "nt": NT, "tn": TN}[dims]
    n_ex, n_out = len(extras), len(out_dtypes)
    e_specs = [o_spec if s is None else s for s in (tuple(extra_specs) + (None,) * n_ex)[:n_ex]]

    n_prev = len(prev)

    def body(*refs):
        a_ref, b_ref = refs[0], refs[1]
        e_refs = refs[2 + n_prev:2 + n_prev + n_ex]
        o_refs = refs[2 + n_prev + n_ex:2 + n_prev + n_ex + n_out]
        av = a_ref[...]
        if a_fn is not None:
            av = a_fn(av)
        part = _dot(av.astype(BF16), b_ref[...].astype(BF16), dn)

        def finish(acc):
            outs = (acc,) if epilogue is None else epilogue(acc, *[e[...] for e in e_refs])
            for o_ref, o in zip(o_refs, outs):
                o_ref[...] = o.astype(o_ref.dtype)

        if nk == 1:
            finish(part)
        else:
            acc_ref = refs[-1]
            k = pl.program_id(2)

            @pl.when(k == 0)
            def _():
                acc_ref[...] = part

            @pl.when(k > 0)
            def _():
                acc_ref[...] += part

            @pl.when(k == nk - 1)
            def _():
                finish(acc_ref[...])

    outs, side_outs = _hosted_call(
        body, name=name, grid=(nj, ni, nk) if weights_outer else (ni, nj, nk),
        in_specs=[a_spec, b_spec] + [ANY_SPEC] * n_prev + e_specs,
        out_specs=out_specs, out_shape=out_shape,
        scratch_shapes=[pltpu.VMEM((tm, tn), F32)] if nk > 1 else [],
        args=(a, b, *prev, *extras), side=side, io_aliases=io_aliases)
    res = outs[0] if n_out == 1 else outs
    return res if side is None else (res, side_outs)


def _mod_spec(tm, tn, seq):
    return pl.BlockSpec((1, MOD_ROWS, tn), lambda i, j, k: ((i * tm) // seq, 0, j))


def _normmod_fwd(x3, g, mod, shift_row, scale_row, *, name, tb=256):
    bsz, seq, d = x3.shape
    tb = min(tb, seq)

    def body(x_ref, g_ref, mod_ref, h_ref):
        m = mod_ref[0]
        nrm = _rms_fwd(x_ref[0], g_ref[...], d)
        h = nrm * (1.0 + m[scale_row:scale_row + 1, :]) + m[shift_row:shift_row + 1, :]
        h_ref[0] = h.astype(BF16)

    return pl.pallas_call(
        body, name=name, grid=(bsz, seq // tb),
        in_specs=[pl.BlockSpec((1, tb, d), lambda b, i: (b, i, 0)),
                  pl.BlockSpec((1, d), lambda b, i: (0, 0)),
                  pl.BlockSpec((1, MOD_ROWS, d), lambda b, i: (b, 0, 0))],
        out_specs=pl.BlockSpec((1, tb, d), lambda b, i: (b, i, 0)),
        out_shape=jax.ShapeDtypeStruct((bsz, seq, d), BF16),
        compiler_params=_cparams(),
    )(x3, g, mod)


def _pair_mean_exact(x, lo):
    s_lo = jnp.sum(jnp.where(lo, x, 0.0), axis=-1, keepdims=True)
    s_hi = jnp.sum(jnp.where(lo, 0.0, x), axis=-1, keepdims=True)
    return jnp.where(lo, s_lo, s_hi) * (1.0 / GROUP_DIM)


def _gmlp_pair_fwd(gv_p, w0, w1, bias, lo):
    mu = _pair_mean_exact(gv_p, lo)
    dlt = gv_p - mu
    var = _pair_mean_exact(dlt * dlt, lo)
    rstd = lax.rsqrt(var + EPS)
    vn = dlt * rstd
    vnb = vn.astype(BF16)
    mixed = jnp.where(lo, _dot(w0, vnb), _dot(w1, vnb)) + bias
    return vn, vnb, rstd, mixed


def _tril_bf16(w):
    t = w.shape[-1]
    return jnp.where(_iota((t, t), 1) <= _iota((t, t), 0), w, 0.0).astype(BF16)


def _gmlp_fwd(z3, ws, bexp, g_out, *, name):
    bsz, seq, _ = z3.shape
    nc = seq // CHUNK

    def body(u_ref, v_ref, ws_ref, b_ref, g_ref, y_ref):
        lo = _iota((CHUNK, 128), 1) < GROUP_DIM
        gu = _gelu(u_ref[0].astype(F32))
        gv = _gelu(v_ref[0].astype(F32))
        parts = []
        for p in range(GROUPS // 2):
            sl = slice(128 * p, 128 * p + 128)
            w0 = _tril_bf16(ws_ref[2 * p])
            w1 = _tril_bf16(ws_ref[2 * p + 1])
            _, _, _, mixed = _gmlp_pair_fwd(gv[:, sl], w0, w1, b_ref[p], lo)
            parts.append(gu[:, sl] * mixed)
        yg = jnp.concatenate(parts, axis=1)
        y_ref[0] = _rms_fwd(yg, g_ref[...], D_GMLP).astype(BF16)

    return pl.pallas_call(
        body, name=name, grid=(bsz, nc),
        in_specs=[pl.BlockSpec((1, CHUNK, D_GMLP), lambda b, i: (b, i, 0)),
                  pl.BlockSpec((1, CHUNK, D_GMLP), lambda b, i: (b, i, 1)),
                  pl.BlockSpec((GROUPS, CHUNK, CHUNK), lambda b, i: (0, 0, 0)),
                  pl.BlockSpec((GROUPS // 2, CHUNK, 128), lambda b, i: (0, 0, 0)),
                  pl.BlockSpec((1, D_GMLP), lambda b, i: (0, 0))],
        out_specs=pl.BlockSpec((1, CHUNK, D_GMLP), lambda b, i: (b, i, 0)),
        out_shape=jax.ShapeDtypeStruct((bsz, seq, D_GMLP), BF16),
        compiler_params=_cparams(),
    )(z3, z3, ws, bexp, g_out)


def _gmlp_bwd(z3, dyn3, ws, wst, bexp, g_out, *, name, dy_col):
    bsz, seq, _ = z3.shape
    nc = seq // CHUNK
    npair = GROUPS // 2

    def body(u_ref, v_ref, dy_ref, ws_ref, wst_ref, b_ref, g_ref, duv_ref, dws_ref, dbs_ref, dg_ref, dbacc):
        first = jnp.logical_and(pl.program_id(0) == 0, pl.program_id(1) == 0)
        last = jnp.logical_and(pl.program_id(0) == bsz - 1, pl.program_id(1) == nc - 1)

        @pl.when(first)
        def _():
            dws_ref[...] = jnp.zeros_like(dws_ref)
            dg_ref[...] = jnp.zeros_like(dg_ref)
            dbacc[...] = jnp.zeros_like(dbacc)

        lo = _iota((CHUNK, 128), 1) < GROUP_DIM
        tril = _iota((CHUNK, CHUNK), 1) <= _iota((CHUNK, CHUNK), 0)
        u = u_ref[0].astype(F32)
        v = v_ref[0].astype(F32)
        gu = _gelu(u)
        gv = _gelu(v)
        fwd = []
        for p in range(npair):
            sl = slice(128 * p, 128 * p + 128)
            w0 = _tril_bf16(ws_ref[2 * p])
            w1 = _tril_bf16(ws_ref[2 * p + 1])
            fwd.append(_gmlp_pair_fwd(gv[:, sl], w0, w1, b_ref[p], lo))
        yg = jnp.concatenate([gu[:, 128 * p:128 * p + 128] * fwd[p][3] for p in range(npair)], axis=1)
        dyg, dg = _rms_bwd(yg, g_ref[...], dy_ref[0].astype(F32), D_GMLP)
        dg_ref[...] += dg
        du_parts, dv_parts = [], []
        for p in range(npair):
            sl = slice(128 * p, 128 * p + 128)
            vn, vnb, rstd, mixed = fwd[p]
            dyg_p = dyg[:, sl]
            dmixed = dyg_p * gu[:, sl]
            dbacc[p] += dmixed
            dm_lo = jnp.where(lo, dmixed, 0.0).astype(BF16)
            dm_hi = jnp.where(lo, 0.0, dmixed).astype(BF16)
            dws_ref[2 * p] += jnp.where(tril, _dot(dm_lo, vnb, NT), 0.0)
            dws_ref[2 * p + 1] += jnp.where(tril, _dot(dm_hi, vnb, NT), 0.0)
            dmb = dmixed.astype(BF16)
            dvn = jnp.where(lo, _dot(wst_ref[2 * p], dmb), _dot(wst_ref[2 * p + 1], dmb))
            dgv = rstd * (dvn - _pair_mean_exact(dvn, lo) - vn * _pair_mean_exact(dvn * vn, lo))
            dv_parts.append(dgv * _gelu_grad(v[:, sl]))
            du_parts.append(dyg_p * mixed * _gelu_grad(u[:, sl]))
        duv_ref[0] = jnp.concatenate(du_parts + dv_parts, axis=1).astype(BF16)

        @pl.when(last)
        def _():
            sel = jnp.where(_iota((8, 128), 0) == 0, (_iota((8, 128), 1) < GROUP_DIM).astype(F32),
                            jnp.where(_iota((8, 128), 0) == 1, (_iota((8, 128), 1) >= GROUP_DIM).astype(F32), 0.0))
            for p in range(npair):
                dbs_ref[p] = lax.dot_general(sel, dbacc[p], NT, precision=lax.Precision.HIGHEST,
                                             preferred_element_type=F32)

    duv, dws, dbs, dg = pl.pallas_call(
        body, name=name, grid=(bsz, nc),
        in_specs=[pl.BlockSpec((1, CHUNK, D_GMLP), lambda b, i: (b, i, 0)),
                  pl.BlockSpec((1, CHUNK, D_GMLP), lambda b, i: (b, i, 1)),
                  pl.BlockSpec((1, CHUNK, D_GMLP), lambda b, i: (b, i, dy_col)),
                  pl.BlockSpec((GROUPS, CHUNK, CHUNK), lambda b, i: (0, 0, 0)),
                  pl.BlockSpec((GROUPS, CHUNK, CHUNK), lambda b, i: (0, 0, 0)),
                  pl.BlockSpec((npair, CHUNK, 128), lambda b, i: (0, 0, 0)),
                  pl.BlockSpec((1, D_GMLP), lambda b, i: (0, 0))],
        out_specs=[pl.BlockSpec((1, CHUNK, 2 * D_GMLP), lambda b, i: (b, i, 0)),
                   pl.BlockSpec((GROUPS, CHUNK, CHUNK), lambda b, i: (0, 0, 0)),
                   pl.BlockSpec((npair, 8, CHUNK), lambda b, i: (0, 0, 0)),
                   pl.BlockSpec((1, D_GMLP), lambda b, i: (0, 0))],
        out_shape=[jax.ShapeDtypeStruct((bsz, seq, D_IN_PAD), BF16),
                   jax.ShapeDtypeStruct((GROUPS, CHUNK, CHUNK), F32),
                   jax.ShapeDtypeStruct((npair, 8, CHUNK), F32),
                   jax.ShapeDtypeStruct((1, D_GMLP), F32)],
        scratch_shapes=[pltpu.VMEM((npair, CHUNK, 128), F32)],
        compiler_params=_cparams(),
    )(z3, z3, dyn3, ws, wst, bexp, g_out)
    return duv, dws, dbs[:, :2, :].reshape(GROUPS, CHUNK), dg


def _partner(x):
    width = x.shape[-1]
    lane = _iota(x.shape, x.ndim - 1) % HEAD_PAD
    up = pltpu.roll(x, width - ROPE // 2, x.ndim - 1)
    down = pltpu.roll(x, ROPE // 2, x.ndim - 1)
    first = jnp.logical_and(lane >= NOPE, lane < NOPE + ROPE // 2)
    second = jnp.logical_and(lane >= NOPE + ROPE // 2, lane < NOPE + ROPE)
    return jnp.where(first, up, jnp.where(second, down, 0.0))


def _mla_prep_fwd(z3, g_q, g_kv, w_uq, w_ukv, ctab, stab, *, name, tb=256):
    bsz, seq, _ = z3.shape
    tb = min(tb, seq)
    hw = HEADS * HEAD_PAD

    def body(ql_ref, kvl_ref, krl_ref, gq_ref, gkv_ref, wuq_ref, wukv_ref, c_ref, s_ref, q_ref, kv_ref, kp_ref):
        cq = _rms_fwd(ql_ref[0].astype(F32), gq_ref[...], Q_RANK).astype(BF16)
        q = _dot(cq, wuq_ref[...])
        c1, s1 = c_ref[0], s_ref[0]
        c8, s8 = jnp.tile(c1, (1, HEADS)), jnp.tile(s1, (1, HEADS))
        q_ref[0] = ((q * c8 + _partner(q) * s8) * SCALE_LOG2).astype(BF16)
        ckv = _rms_fwd(kvl_ref[0].astype(F32), gkv_ref[...], KV_RANK).astype(BF16)
        kv = _dot(ckv, wukv_ref[...])
        kv_ref[0] = kv.astype(BF16)
        kr = krl_ref[0].astype(F32)
        kr = kr * c1 + _partner(kr) * s1
        lane = _iota((tb, hw), 1) % HEAD_PAD
        kp_ref[0] = jnp.where(lane < NOPE, kv, jnp.tile(kr, (1, HEADS))).astype(BF16)

    return pl.pallas_call(
        body, name=name, grid=(bsz, seq // tb),
        in_specs=[pl.BlockSpec((1, tb, Q_RANK), lambda b, i: (b, i, 4)),
                  pl.BlockSpec((1, tb, KV_RANK), lambda b, i: (b, i, 10)),
                  pl.BlockSpec((1, tb, HEAD_PAD), lambda b, i: (b, i, 11)),
                  pl.BlockSpec((1, Q_RANK), lambda b, i: (0, 0)),
                  pl.BlockSpec((1, KV_RANK), lambda b, i: (0, 0)),
                  pl.BlockSpec((Q_RANK, hw), lambda b, i: (0, 0)),
                  pl.BlockSpec((KV_RANK, hw), lambda b, i: (0, 0)),
                  pl.BlockSpec((1, tb, HEAD_PAD), lambda b, i: (b, i, 0)),
                  pl.BlockSpec((1, tb, HEAD_PAD), lambda b, i: (b, i, 0))],
        out_specs=[pl.BlockSpec((1, tb, hw), lambda b, i: (b, i, 0))] * 3,
        out_shape=[jax.ShapeDtypeStruct((bsz, seq, hw), BF16)] * 3,
        compiler_params=_cparams(),
    )(z3, z3, z3, g_q, g_kv, w_uq, w_ukv, ctab, stab)


def _mla_prep_bwd(z3, dz3, dq3, dk3, dv3, g_q, g_kv, w_uq, w_ukv, ctab, stab, *, name, tb=256):
    bsz, seq, _ = z3.shape
    tb = min(tb, seq)
    hw = HEADS * HEAD_PAD
    nb = seq // tb

    def body(ql_ref, kvl_ref, dq_ref, dk_ref, dv_ref, gq_ref, gkv_ref, wuq_ref, wukv_ref, c_ref, s_ref, dz_in,
             dz_ref, cq_ref, dqb_ref, ckv_ref, dkvb_ref, dgq_ref, dgkv_ref):
        @pl.when(jnp.logical_and(pl.program_id(0) == 0, pl.program_id(1) == 0))
        def _():
            dgq_ref[...] = jnp.zeros_like(dgq_ref)
            dgkv_ref[...] = jnp.zeros_like(dgkv_ref)

        c1, s1 = c_ref[0], s_ref[0]
        c8, s8 = jnp.tile(c1, (1, HEADS)), jnp.tile(s1, (1, HEADS))
        dqr = dq_ref[0]
        dqb = (dqr * c8 + _partner(dqr * s8)).astype(BF16)
        dqb_ref[0] = dqb
        ql = ql_ref[0].astype(F32)
        cq_ref[0] = _rms_fwd(ql, gq_ref[...], Q_RANK).astype(BF16)
        dql, dgq = _rms_bwd(ql, gq_ref[...], _dot(dqb, wuq_ref[...], NT), Q_RANK)
        dgq_ref[...] += dgq

        dk = dk_ref[0]
        lane = _iota((tb, hw), 1) % HEAD_PAD
        dkvb = jnp.where(lane < NOPE, dk, dv_ref[0]).astype(BF16)
        dkvb_ref[0] = dkvb
        kvl = kvl_ref[0].astype(F32)
        ckv_ref[0] = _rms_fwd(kvl, gkv_ref[...], KV_RANK).astype(BF16)
        dkvl, dgkv = _rms_bwd(kvl, gkv_ref[...], _dot(dkvb, wukv_ref[...], NT), KV_RANK)
        dgkv_ref[...] += dgkv

        dkr = dk[:, 0:HEAD_PAD].astype(F32)
        for h in range(1, HEADS):
            dkr = dkr + dk[:, HEAD_PAD * h:HEAD_PAD * (h + 1)].astype(F32)
        lane1 = _iota((tb, HEAD_PAD), 1)
        dkr = jnp.where(jnp.logical_and(lane1 >= NOPE, lane1 < NOPE + ROPE), dkr, 0.0)
        dkrl = dkr * c1 + _partner(dkr * s1)
        dz_ref[0] = jnp.concatenate([dql, dkvl, dkrl], axis=1).astype(BF16)

    return pl.pallas_call(
        body, name=name, grid=(bsz, nb),
        in_specs=[pl.BlockSpec((1, tb, Q_RANK), lambda b, i: (b, i, 4)),
                  pl.BlockSpec((1, tb, KV_RANK), lambda b, i: (b, i, 10)),
                  pl.BlockSpec((1, tb, hw), lambda b, i: (b, i, 0)),
                  pl.BlockSpec((1, tb, hw), lambda b, i: (b, i, 0)),
                  pl.BlockSpec((1, tb, hw), lambda b, i: (b, i, 0)),
                  pl.BlockSpec((1, Q_RANK), lambda b, i: (0, 0)),
                  pl.BlockSpec((1, KV_RANK), lambda b, i: (0, 0)),
                  pl.BlockSpec((Q_RANK, hw), lambda b, i: (0, 0)),
                  pl.BlockSpec((KV_RANK, hw), lambda b, i: (0, 0)),
                  pl.BlockSpec((1, tb, HEAD_PAD), lambda b, i: (b, i, 0)),
                  pl.BlockSpec((1, tb, HEAD_PAD), lambda b, i: (b, i, 0)),
                  ANY_SPEC],
        out_specs=[pl.BlockSpec((1, tb, 512), lambda b, i: (b, i, 2)),
                   pl.BlockSpec((1, tb, Q_RANK), lambda b, i: (b, i, 0)),
                   pl.BlockSpec((1, tb, hw), lambda b, i: (b, i, 0)),
                   pl.BlockSpec((1, tb, KV_RANK), lambda b, i: (b, i, 0)),
                   pl.BlockSpec((1, tb, hw), lambda b, i: (b, i, 0)),
                   pl.BlockSpec((1, Q_RANK), lambda b, i: (0, 0)),
                   pl.BlockSpec((1, KV_RANK), lambda b, i: (0, 0))],
        out_shape=[jax.ShapeDtypeStruct((bsz, seq, D_IN_PAD), BF16),
                   jax.ShapeDtypeStruct((bsz, seq, Q_RANK), BF16),
                   jax.ShapeDtypeStruct((bsz, seq, hw), BF16),
                   jax.ShapeDtypeStruct((bsz, seq, KV_RANK), BF16),
                   jax.ShapeDtypeStruct((bsz, seq, hw), BF16),
                   jax.ShapeDtypeStruct((1, Q_RANK), F32),
                   jax.ShapeDtypeStruct((1, KV_RANK), F32)],
        input_output_aliases={11: 0},
        compiler_params=_cparams(),
    )(z3, z3, dq3, dk3, dv3, g_q, g_kv, w_uq, w_ukv, ctab, stab, dz3)


ATTN_HEADS_PER_STEP = 2


def _attn_specs(tq, seq, hp):
    blk = pl.BlockSpec((1, tq, hp * HEAD_PAD), lambda b, h, i: (b, i, h))
    full = pl.BlockSpec((1, seq, hp * HEAD_PAD), lambda b, h, i: (b, 0, h))
    return blk, full


def _head(h):
    return slice(HEAD_PAD * h, HEAD_PAD * (h + 1))


def _attn_fwd(q3, kv3, kp3, *, name, tq=512, hp=ATTN_HEADS_PER_STEP, side=None):
    bsz, seq, hw = q3.shape
    tq = min(tq, seq)
    blk, full = _attn_specs(tq, seq, hp)

    def body(q_ref, kv_ref, kp_ref, o_ref, lse_ref):
        i = pl.program_id(2)
        is_nope = _iota((tq, HEAD_PAD), 1) < NOPE
        causal = _iota((tq, tq), 1) <= _iota((tq, tq), 0)

        def step(j, carry, diag):
            st = pl.multiple_of(j * tq, tq)
            out = []
            for h in range(hp):
                m, l, acc = carry[h]
                kvj = kv_ref[0, pl.ds(st, tq), _head(h)]
                s = _dot(q_ref[0, :, _head(h)], kp_ref[0, pl.ds(st, tq), _head(h)], NT)
                if diag:
                    s = jnp.where(causal, s, -1e30)
                m_new = jnp.maximum(m, jnp.max(s, axis=1, keepdims=True))
                alpha = jnp.exp2(m - m_new)
                p = jnp.exp2(s - m_new)
                l = alpha * l + jnp.sum(p, axis=1, keepdims=True)
                acc = alpha * acc + _dot(p.astype(BF16), kvj)
                out.append((m_new, l, acc))
            return tuple(out)

        init = tuple((jnp.full((tq, 1), -1e30, F32), jnp.zeros((tq, 1), F32), jnp.zeros((tq, HEAD_PAD), F32))
                     for _ in range(hp))
        carry = lax.fori_loop(0, i, lambda j, c: step(j, c, False), init)
        carry = step(i, carry, True)
        for h in range(hp):
            m, l, acc = carry[h]
            o_ref[0, :, _head(h)] = jnp.where(is_nope, 0.0, acc / l).astype(BF16)
            lse_ref[0, :, _head(h)] = jnp.broadcast_to(m + jnp.log(l) * LOG2E, (tq, HEAD_PAD))

    outs, side_outs = _hosted_call(
        body, name=name, grid=(bsz, HEADS // hp, seq // tq),
        in_specs=[blk, full, full],
        out_specs=[blk, blk],
        out_shape=[jax.ShapeDtypeStruct((bsz, seq, hw), BF16), jax.ShapeDtypeStruct((bsz, seq, hw), F32)],
        args=(q3, kv3, kp3), side=side)
    return outs if side is None else (outs, side_outs)


def _attn_bwd(q3, kv3, kp3, do3, lse3, dl3, *, name, tq=512, hp=ATTN_HEADS_PER_STEP, side=None):
    bsz, seq, hw = q3.shape
    tq = min(tq, seq)
    nq = seq // tq
    blk, full = _attn_specs(tq, seq, hp)
    rep = tq // HEAD_PAD

    def body(kv_ref, kp_ref, q_ref, do_ref, lse_ref, dl_ref, dq_ref, dk_ref, dv_ref):
        j = pl.program_id(2)
        causal = _iota((tq, tq), 1) <= _iota((tq, tq), 0)

        @pl.when(j == 0)
        def _():
            dq_ref[...] = jnp.zeros_like(dq_ref)

        def step(i, carry, diag):
            st = pl.multiple_of(i * tq, tq)
            out = []
            for h in range(hp):
                dk, dv = carry[h]
                qi = q_ref[0, pl.ds(st, tq), _head(h)]
                do = do_ref[0, pl.ds(st, tq), _head(h)]
                kp = kp_ref[0, :, _head(h)]
                s = _dot(qi, kp, NT)
                if diag:
                    s = jnp.where(causal, s, -1e30)
                p = jnp.exp2(s - jnp.tile(lse_ref[0, pl.ds(st, tq), _head(h)], (1, rep)))
                dv = dv + _dot(p.astype(BF16), do, TN)
                dp = _dot(do, kv_ref[0, :, _head(h)], NT)
                ds = (p * (dp - jnp.tile(dl_ref[0, pl.ds(st, tq), _head(h)], (1, rep)))).astype(BF16)
                dk = dk + _dot(ds, qi, TN)
                dq_ref[0, pl.ds(st, tq), _head(h)] += _dot(ds, kp)
                out.append((dk, dv))
            return tuple(out)

        zero = jnp.zeros((tq, HEAD_PAD), F32)
        carry = step(j, tuple((zero, zero) for _ in range(hp)), True)
        carry = lax.fori_loop(j + 1, nq, lambda i, c: step(i, c, False), carry)
        for h in range(hp):
            dk_ref[0, :, _head(h)] = (carry[h][0] * (1.0 / LOG2E)).astype(BF16)
            dv_ref[0, :, _head(h)] = carry[h][1].astype(BF16)

        @pl.when(j == nq - 1)
        def _():
            dq_ref[...] = dq_ref[...] * ATTN_SCALE

    outs, side_outs = _hosted_call(
        body, name=name, grid=(bsz, HEADS // hp, nq),
        in_specs=[blk, blk, full, full, full, full],
        out_specs=[full, blk, blk],
        out_shape=[jax.ShapeDtypeStruct((bsz, seq, hw), F32)] + [jax.ShapeDtypeStruct((bsz, seq, hw), BF16)] * 2,
        args=(kv3, kp3, q3, do3, lse3, dl3), side=side)
    return outs if side is None else (outs, side_outs)


def _onorm_fwd(o3, yg3, g_pad, *, name, tb=256):
    bsz, seq, hw = o3.shape
    wg = yg3.shape[-1]
    tb = min(tb, seq)

    def body(o_ref, yg_ref, g_ref, y_ref):
        ya = _rms_fwd(o_ref[0].astype(F32), g_ref[...], HEADS * 64).astype(BF16)
        y_ref[0] = jnp.concatenate([ya, yg_ref[0]], axis=1)

    return pl.pallas_call(
        body, name=name, grid=(bsz, seq // tb),
        in_specs=[pl.BlockSpec((1, tb, hw), lambda b, i: (b, i, 0)),
                  pl.BlockSpec((1, tb, wg), lambda b, i: (b, i, 0)),
                  pl.BlockSpec((1, hw), lambda b, i: (0, 0))],
        out_specs=pl.BlockSpec((1, tb, hw + wg), lambda b, i: (b, i, 0)),
        out_shape=jax.ShapeDtypeStruct((bsz, seq, hw + wg), BF16),
        compiler_params=_cparams(),
    )(o3, yg3, g_pad)


def _onorm_bwd(o3, dy3, g_pad, *, name, tb=256):
    bsz, seq, hw = o3.shape
    tb = min(tb, seq)

    def body(o_ref, dy_ref, g_ref, do_ref, dl_ref, dg_ref):
        @pl.when(jnp.logical_and(pl.program_id(0) == 0, pl.program_id(1) == 0))
        def _():
            dg_ref[...] = jnp.zeros_like(dg_ref)

        o = o_ref[0].astype(F32)
        do, dg = _rms_bwd(o, g_ref[...], dy_ref[0].astype(F32), HEADS * 64)
        dg_ref[...] += dg
        do_ref[0] = do.astype(BF16)
        prod = do * o
        parts = []
        for h in range(HEADS):
            sh = jnp.sum(prod[:, HEAD_PAD * h:HEAD_PAD * (h + 1)], axis=1, keepdims=True)
            parts.append(jnp.broadcast_to(sh, (tb, HEAD_PAD)))
        dl_ref[0] = jnp.concatenate(parts, axis=1)

    return pl.pallas_call(
        body, name=name, grid=(bsz, seq // tb),
        in_specs=[pl.BlockSpec((1, tb, hw), lambda b, i: (b, i, 0)),
                  pl.BlockSpec((1, tb, hw), lambda b, i: (b, i, 0)),
                  pl.BlockSpec((1, hw), lambda b, i: (0, 0))],
        out_specs=[pl.BlockSpec((1, tb, hw), lambda b, i: (b, i, 0)),
                   pl.BlockSpec((1, tb, hw), lambda b, i: (b, i, 0)),
                   pl.BlockSpec((1, hw), lambda b, i: (0, 0))],
        out_shape=[jax.ShapeDtypeStruct((bsz, seq, hw), BF16),
                   jax.ShapeDtypeStruct((bsz, seq, hw), F32),
                   jax.ShapeDtypeStruct((1, hw), F32)],
        compiler_params=_cparams(),
    )(o3, dy3, g_pad)


def _resnode_bwd(x3, g, *, name, target3=None, dh3=None, dres3=None, mod_nm=None, rows=None,
                 branch3=None, mod_gate=None, gate_row=None, tb=256, side=None):
    bsz, seq, d = x3.shape
    tb = min(tb, seq)
    final = target3 is not None
    has_branch = branch3 is not None
    row_spec = pl.BlockSpec((1, tb, d), lambda b, i: (b, i, 0))
    vec_spec = pl.BlockSpec((1, d), lambda b, i: (0, 0))
    mod_spec = pl.BlockSpec((1, MOD_ROWS, d), lambda b, i: (b, 0, 0))

    ins, in_specs = [x3, g], [row_spec, vec_spec]
    if final:
        ins += [target3]
        in_specs += [row_spec]
    else:
        ins += [dh3, dres3, mod_nm]
        in_specs += [row_spec, row_spec, mod_spec]
    if has_branch:
        ins += [branch3, mod_gate]
        in_specs += [row_spec, mod_spec]

    out_names = ["dx", "dg"]
    out_specs = [row_spec, vec_spec]
    out_shape = [jax.ShapeDtypeStruct((bsz, seq, d), F32), jax.ShapeDtypeStruct((1, d), F32)]
    if final:
        out_names += ["loss"]
        out_specs += [pl.BlockSpec((1, 128), lambda b, i: (0, 0))]
        out_shape += [jax.ShapeDtypeStruct((1, 128), F32)]
    else:
        out_names += ["dnm"]
        out_specs += [mod_spec]
        out_shape += [jax.ShapeDtypeStruct((bsz, MOD_ROWS, d), F32)]
    if has_branch:
        out_names += ["dbr", "dgate"]
        out_specs += [row_spec, mod_spec]
        out_shape += [jax.ShapeDtypeStruct((bsz, seq, d), BF16), jax.ShapeDtypeStruct((bsz, MOD_ROWS, d), F32)]
    n_in = len(ins)

    def body(*refs):
        r = dict(zip(["x", "g"] + (["t"] if final else ["dh", "dres", "nm"]) + (["br", "gm"] if has_branch else []),
                     refs[:n_in]))
        o = dict(zip(out_names, refs[n_in:]))
        b_first = pl.program_id(1) == 0
        first = jnp.logical_and(pl.program_id(0) == 0, b_first)
        rowid = _iota((MOD_ROWS, d), 0)

        @pl.when(first)
        def _():
            o["dg"][...] = jnp.zeros_like(o["dg"])
            if final:
                o["loss"][...] = jnp.zeros_like(o["loss"])

        @pl.when(b_first)
        def _():
            if not final:
                o["dnm"][...] = jnp.zeros_like(o["dnm"])
            if has_branch:
                o["dgate"][...] = jnp.zeros_like(o["dgate"])

        x = r["x"][0]
        gv = r["g"][...]
        if final:
            e = _rms_fwd(x, gv, d) - r["t"][0]
            sq = jnp.sum(jnp.sum(e * e, axis=1, keepdims=True), axis=0, keepdims=True)
            o["loss"][...] += jnp.broadcast_to(sq * (0.5 / d), (1, 128))
            dx, dg = _rms_bwd(x, gv, e * (1.0 / d), d)
        else:
            m = r["nm"][0]
            dh = r["dh"][0].astype(F32)
            scale = m[rows[1]:rows[1] + 1, :]
            rstd = lax.rsqrt(jnp.sum(x * x, axis=-1, keepdims=True) * (1.0 / d) + EPS)
            xh = x * rstd
            nrm = xh * gv
            dshift = jnp.sum(dh, axis=0, keepdims=True)
            dscale = jnp.sum(dh * nrm, axis=0, keepdims=True)
            o["dnm"][0] += jnp.where(rowid == 0, dshift, jnp.where(rowid == 1, dscale, 0.0))
            dn = dh * (1.0 + scale)
            dg = jnp.sum(dn * xh, axis=0, keepdims=True)
            dxh = dn * gv
            dx = rstd * (dxh - xh * (jnp.sum(dxh * xh, axis=-1, keepdims=True) * (1.0 / d))) + r["dres"][0]
        o["dg"][...] += dg
        o["dx"][0] = dx
        if has_branch:
            gate = r["gm"][0][gate_row:gate_row + 1, :]
            o["dbr"][0] = (gate * dx).astype(BF16)
            dgate = jnp.sum(dx * r["br"][0], axis=0, keepdims=True)
            o["dgate"][0] += jnp.where(rowid == 0, dgate, 0.0)

    outs, side_outs = _hosted_call(
        body, name=name, grid=(bsz, seq // tb),
        in_specs=in_specs, out_specs=out_specs, out_shape=out_shape, args=tuple(ins), side=side)
    res = dict(zip(out_names, outs))
    return res if side is None else (res, side_outs)


def _adamw(w, g, m, v, *, name):
    shape = w.shape
    cols = shape[-1]
    rows = w.size // cols
    tr = _pick_rows(rows, max(8, (256 * 1024) // cols // 8 * 8))

    def body(w_ref, g_ref, m_ref, v_ref, d_ref, nm_ref, nv_ref):
        d_ref[...], nm_ref[...], nv_ref[...] = _adamw_math(w_ref[...], g_ref[...], m_ref[...], v_ref[...])

    spec = pl.BlockSpec((tr, cols), lambda i: (i, 0))
    outs = pl.pallas_call(
        body, name=name, grid=(rows // tr,),
        in_specs=[spec] * 4, out_specs=[spec] * 3,
        out_shape=[jax.ShapeDtypeStruct((rows, cols), F32)] * 3,
        compiler_params=_cparams(),
    )(*[t.reshape(rows, cols) for t in (w, g, m, v)])
    return tuple(o.reshape(shape) for o in outs)


def _adamw_math(w, g, m, v):
    c1 = 1.0 - ADAM_B1 ** ADAM_STEP
    c2 = 1.0 - ADAM_B2 ** ADAM_STEP
    nm = ADAM_B1 * m + (1.0 - ADAM_B1) * g
    nv = ADAM_B2 * v + (1.0 - ADAM_B2) * (g * g)
    delta = -ADAM_LR * ((nm / c1) / (jnp.sqrt(nv / c2) + ADAM_EPS) + ADAM_WD * w)
    return delta, nm, nv


def _adamw_layers(w, m, v, bufs, row_off, *, name, tr=256):
    depth, rows, cols = w.shape
    tr = min(tr, rows)
    assert rows % tr == 0 and row_off % tr == 0

    outs = None
    for l in range(depth):
        def body(w_ref, g_ref, m_ref, v_ref, *rest):
            go_ref, d_ref, nm_ref, nv_ref = rest[-4:]
            g = g_ref[...]
            go_ref[...] = g
            d_ref[...], nm_ref[...], nv_ref[...] = _adamw_math(w_ref[...], g, m_ref[...], v_ref[...])

        layer = pl.BlockSpec((None, tr, cols), lambda i, l=l: (l, i, 0))
        prev = () if outs is None else tuple(outs)
        outs = pl.pallas_call(
            body, name=f"{name}_l{l}", grid=(rows // tr,),
            in_specs=[layer, pl.BlockSpec((tr, cols), lambda i: (row_off // tr + i, 0)), layer, layer]
            + [ANY_SPEC] * len(prev),
            out_specs=[layer] * 4,
            out_shape=[jax.ShapeDtypeStruct(w.shape, F32)] * 4,
            input_output_aliases={4 + k: k for k in range(len(prev))},
            compiler_params=_cparams(),
        )(w, bufs[l], m, v, *prev)
    return tuple(outs)


def _sum_leading(x, *, name, tr=256):
    n, rows, cols = x.shape
    tr = _pick_rows(rows, tr)

    def body(x_ref, o_ref):
        acc = x_ref[0]
        for k in range(1, n):
            acc = acc + x_ref[k]
        o_ref[...] = acc

    return pl.pallas_call(
        body, name=name, grid=(rows // tr,),
        in_specs=[pl.BlockSpec((n, tr, cols), lambda i: (0, i, 0))],
        out_specs=pl.BlockSpec((tr, cols), lambda i: (i, 0)),
        out_shape=jax.ShapeDtypeStruct((rows, cols), F32),
        compiler_params=_cparams(),
    )(x)


def _position():
    return lax.axis_index("x"), lax.axis_index("y"), lax.axis_index("c")


def _allgather8(x, *, name):
    shape = x.shape

    def body(x_ref, out_ref, send_sems, recv_sems, local_sem):
        px, py, pc = _position()
        me, sibling = (px, py, pc), (px, py, 1 - pc)
        chips = [(1 - px, py), (px, 1 - py), (1 - px, 1 - py)]
        src_own = x_ref

        def slot(qx, qy, qc):
            return out_ref.at[4 * qx + 2 * qy + qc]

        def copy(k, block, to, src=None):
            return pltpu.make_async_remote_copy(
                src_ref=slot(*block) if src is None else src, dst_ref=slot(*block),
                send_sem=send_sems.at[k], recv_sem=recv_sems.at[k], device_id=to, device_id_type=MESH)

        mine = pltpu.make_async_copy(src_own, slot(*me), local_sem)
        mine.start()
        first = [copy(0, me, sibling, src=src_own)]
        first += [copy(1 + j, me, (*chip, pc), src=src_own) for j, chip in enumerate(chips)]
        for cp in first:
            cp.start()
        passed = [copy(4 + j, (*chip, pc), sibling) for j, chip in enumerate(chips)]
        for j, chip in enumerate(chips):
            copy(1 + j, (*chip, pc), me).wait_recv()
            passed[j].start()
        copy(0, sibling, me).wait_recv()
        for j, chip in enumerate(chips):
            copy(4 + j, (*chip, 1 - pc), me).wait_recv()
        for cp in first + passed:
            cp.wait_send()
        mine.wait()

    return pl.pallas_call(
        body, name=name,
        out_shape=jax.ShapeDtypeStruct((N_DEV,) + shape, x.dtype),
        in_specs=[pl.BlockSpec(memory_space=pl.ANY)],
        out_specs=pl.BlockSpec(memory_space=pl.ANY),
        scratch_shapes=[pltpu.SemaphoreType.DMA((7,)), pltpu.SemaphoreType.DMA((7,)), pltpu.SemaphoreType.DMA],
    )(x)


class _Exchange:
    def __init__(self, ins, out_shapes, n, build, aliases=None):
        self.ins, self.out_shapes, self.n, self.build = tuple(ins), tuple(out_shapes), n, build
        self.aliases = dict(aliases or {})

    def _descriptors(self, in_refs, out_refs, send_sems, recv_sems):
        sends, recvs = [], []
        for k, (src, dst, peer, landing) in enumerate(self.build(in_refs, out_refs)):
            sends.append(pltpu.make_async_remote_copy(
                src_ref=src, dst_ref=dst, send_sem=send_sems.at[k], recv_sem=recv_sems.at[k],
                device_id=peer, device_id_type=MESH))
            recvs.append(pltpu.make_async_remote_copy(
                src_ref=src, dst_ref=landing, send_sem=send_sems.at[k], recv_sem=recv_sems.at[k],
                device_id=peer, device_id_type=MESH))
        return sends, recvs

    def start(self, *refs):
        for cp in self._descriptors(*refs)[0]:
            cp.start()

    def finish(self, *refs):
        sends, recvs = self._descriptors(*refs)
        for cp in recvs:
            cp.wait_recv()
        for cp in sends:
            cp.wait_send()


ANY_SPEC = pl.BlockSpec(memory_space=pl.ANY)


def _hosted_call(body, *, name, grid, in_specs, out_specs, out_shape, args, scratch_shapes=(), side=None,
                 num_scalar_prefetch=0, io_aliases=None):
    in_specs, out_specs, out_shape = list(in_specs), list(out_specs), list(out_shape)
    n_in, n_out = len(in_specs) + num_scalar_prefetch, len(out_specs)
    kernel_body = body
    aliases = dict(io_aliases or {})
    if side is not None:
        s_in, s_out = len(side.ins), len(side.out_shapes)
        aliases.update({n_in + i: n_out + o for i, o in side.aliases.items()})

        def kernel_body(*refs):
            ins, s_ins = refs[:n_in], refs[n_in:n_in + s_in]
            outs = refs[n_in + s_in:n_in + s_in + n_out]
            s_outs = refs[n_in + s_in + n_out:n_in + s_in + n_out + s_out]
            scratch, sems = refs[n_in + s_in + n_out + s_out:-2], refs[-2:]
            first = functools.reduce(jnp.logical_and, [pl.program_id(a) == 0 for a in range(len(grid))])
            last = functools.reduce(jnp.logical_and, [pl.program_id(a) == g - 1 for a, g in enumerate(grid)])

            @pl.when(first)
            def _():
                side.start(s_ins, s_outs, *sems)

            body(*ins, *outs, *scratch)

            @pl.when(last)
            def _():
                side.finish(s_ins, s_outs, *sems)

        in_specs += [ANY_SPEC] * s_in
        out_specs += [ANY_SPEC] * s_out
        out_shape += list(side.out_shapes)
        scratch_shapes = list(scratch_shapes) + [pltpu.SemaphoreType.DMA((side.n,)),
                                                 pltpu.SemaphoreType.DMA((side.n,))]
        args = tuple(args) + side.ins
    if num_scalar_prefetch:
        grid_spec = pltpu.PrefetchScalarGridSpec(num_scalar_prefetch=num_scalar_prefetch, grid=grid,
                                                 in_specs=in_specs, out_specs=out_specs,
                                                 scratch_shapes=list(scratch_shapes))
        outs = pl.pallas_call(kernel_body, name=name, grid_spec=grid_spec, out_shape=out_shape,
                              input_output_aliases=aliases, compiler_params=_cparams())(*args)
    else:
        outs = pl.pallas_call(kernel_body, name=name, grid=grid, in_specs=in_specs, out_specs=out_specs,
                              out_shape=out_shape, scratch_shapes=list(scratch_shapes),
                              input_output_aliases=aliases, compiler_params=_cparams())(*args)
    return tuple(outs[:n_out]), tuple(outs[n_out:])


def _run_exchange(ex, *, name):
    s_in = len(ex.ins)

    def body(*refs):
        ins, outs, sems = refs[:s_in], refs[s_in:-2], refs[-2:]
        ex.start(ins, outs, *sems)
        ex.finish(ins, outs, *sems)

    outs = pl.pallas_call(
        body, name=name, out_shape=list(ex.out_shapes),
        in_specs=[ANY_SPEC] * s_in, out_specs=[ANY_SPEC] * len(ex.out_shapes),
        scratch_shapes=[pltpu.SemaphoreType.DMA((ex.n,)), pltpu.SemaphoreType.DMA((ex.n,))],
        input_output_aliases=ex.aliases,
    )(*ex.ins)
    return tuple(outs)


def _other_chips(px, py):
    return [(px, 1 - py), (1 - px, py), (1 - px, 1 - py)]


def _gather_spread(w_flat, halves=True):
    rows, w = w_flat.shape
    hr = rows // 2 if halves else rows

    def build(ins, outs):
        px, py, pc = _position()
        mine = ins[0].at[pl.ds(pc * hr, hr)] if halves else ins[0]
        me = 4 * px + 2 * py + pc
        plan = [((px, py, 1 - pc), me ^ 1)]
        plan += [((qx, qy, pc), 4 * qx + 2 * qy + pc) for qx, qy in _other_chips(px, py)]
        return [(mine, outs[0].at[me], peer, outs[0].at[their]) for peer, their in plan]

    return _Exchange([w_flat], [jax.ShapeDtypeStruct((N_DEV, hr, w), w_flat.dtype)], 4, build)


def _gather_pass_on(gath):
    def build(ins, outs):
        px, py, pc = _position()
        out = []
        for qx, qy in _other_chips(px, py):
            blk = 4 * qx + 2 * qy + pc
            out.append((outs[0].at[blk], outs[0].at[blk], (px, py, 1 - pc), outs[0].at[blk ^ 1]))
        return out

    return _Exchange([gath], [jax.ShapeDtypeStruct(gath.shape, gath.dtype)], 3, build, aliases={0: 0})


def _rs_halves(g):
    n, rows, w = g.shape
    hr = rows // 2

    def build(ins, outs):
        px, py, pc = _position()
        return [(ins[0].at[:, pl.ds((1 - pc) * hr, hr), :], outs[0], (px, py, 1 - pc), outs[0])]

    return _Exchange([g], [jax.ShapeDtypeStruct((n, hr, w), g.dtype)], 1, build)


def _rs_chips(sb):
    def build(ins, outs):
        px, py, pc = _position()
        return [(ins[0].at[j], outs[0].at[j], (qx, qy, pc), outs[0].at[j])
                for j, (qx, qy) in enumerate(_other_chips(px, py))]

    return _Exchange([sb], [jax.ShapeDtypeStruct(sb.shape, sb.dtype)], 3, build)


def _rs_complete(buf):
    def build(ins, outs):
        px, py, pc = _position()
        return [(outs[0].at[pc], outs[0].at[pc], (px, py, 1 - pc), outs[0].at[1 - pc])]

    return _Exchange([buf], [jax.ShapeDtypeStruct(buf.shape, buf.dtype)], 1, build, aliases={0: 0})


def _rs_partial(g, recv, ids, *, name, tr=128):
    _, rows, w = g.shape
    hr = rows // 2
    nb = hr // tr

    def body(ids_ref, g_ref, r_ref, o_ref):
        o_ref[0] = (g_ref[0] + r_ref[0]).astype(BF16)

    grid_spec = pltpu.PrefetchScalarGridSpec(
        num_scalar_prefetch=1, grid=(3, nb),
        in_specs=[pl.BlockSpec((1, tr, w), lambda j, i, ids: (ids[1] ^ (j + 1), ids[0] * nb + i, 0)),
                  pl.BlockSpec((1, tr, w), lambda j, i, ids: (ids[1] ^ (j + 1), i, 0))],
        out_specs=pl.BlockSpec((1, tr, w), lambda j, i, ids: (j, i, 0)))
    return pl.pallas_call(
        body, name=name, grid_spec=grid_spec,
        out_shape=jax.ShapeDtypeStruct((3, hr, w), BF16),
        compiler_params=_cparams(),
    )(ids, g, recv)


def _rs_total(g, recv, got, ids, *, name, tr=128):
    _, rows, w = g.shape
    hr = rows // 2
    nb = hr // tr

    def body(ids_ref, g_ref, r_ref, got_ref, o_ref):
        acc = g_ref[0] + r_ref[0]
        for j in range(3):
            acc = acc + got_ref[j].astype(F32)
        o_ref[0] = acc

    grid_spec = pltpu.PrefetchScalarGridSpec(
        num_scalar_prefetch=1, grid=(nb,),
        in_specs=[pl.BlockSpec((1, tr, w), lambda i, ids: (ids[1], ids[0] * nb + i, 0)),
                  pl.BlockSpec((1, tr, w), lambda i, ids: (ids[1], i, 0)),
                  pl.BlockSpec((3, tr, w), lambda i, ids: (0, i, 0))],
        out_specs=pl.BlockSpec((1, tr, w), lambda i, ids: (ids[0], i, 0)))
    return pl.pallas_call(
        body, name=name, grid_spec=grid_spec,
        out_shape=jax.ShapeDtypeStruct((2, hr, w), F32),
        compiler_params=_cparams(),
    )(ids, g, recv, got)


class _ReduceScatter:
    def __init__(self, g, ids, tag):
        self.g, self.ids, self.tag, self.stage, self.result = g, ids, tag, 0, None

    def next_exchange(self):
        if self.stage == 0:
            return _rs_halves(self.g)
        if self.stage == 1:
            return _rs_chips(self.sb)
        return _rs_complete(self.buf)

    def done(self, outs):
        if self.stage == 0:
            self.recv = outs[0]
            hr = self.recv.shape[1]
            self.tr = max(t for t in range(16, 513, 16) if hr % t == 0)
            self.sb = _rs_partial(self.g, self.recv, self.ids, name=f"{self.tag}_partial", tr=self.tr)
        elif self.stage == 1:
            self.buf = _rs_total(self.g, self.recv, outs[0], self.ids, name=f"{self.tag}_total", tr=self.tr)
        else:
            _, hr, w = outs[0].shape
            self.result = outs[0].reshape(2 * hr, w)
        self.stage += 1

    def finish_alone(self):
        names = ("halves", "chips", "complete")
        while self.stage < 3:
            self.done(_run_exchange(self.next_exchange(), name=f"{self.tag}_{names[self.stage]}"))
        return self.result


def _flat_rows():
    used = sum(r for _, r in FSDP_SECTIONS)
    return used, -(-used // ROW_ALIGN) * ROW_ALIGN


def _cols_to_chunks(full):
    rows, cols = full.shape
    t = full.reshape(rows, N_CHIPS, cols // N_CHIPS).transpose(1, 0, 2)
    return t.reshape(N_CHIPS, -1, FLAT_W)


def _chunks_to_cols(chunks, rows, cols):
    return chunks.reshape(N_CHIPS, rows, cols // N_CHIPS).transpose(1, 0, 2).reshape(rows, cols)


def _pad_heads(w, real):
    lead = w.shape[:-1]
    t = w.reshape(lead + (HEADS, real))
    t = jnp.pad(t, [(0, 0)] * len(lead) + [(0, 0), (0, HEAD_PAD - real)])
    return t.reshape(lead + (HEADS * HEAD_PAD,))


def _unpad_heads(w, real):
    lead = w.shape[:-1]
    return w.reshape(lead + (HEADS, HEAD_PAD))[..., :real].reshape(lead + (HEADS * real,))


def _pad_value_lanes(w, axis):
    w = jnp.moveaxis(w, axis, -1)
    lead = w.shape[:-1]
    t = w.reshape(lead + (HEADS, 64))
    t = jnp.pad(t, [(0, 0)] * len(lead) + [(0, 0), (HEAD_PAD - 64, 0)])
    return jnp.moveaxis(t.reshape(lead + (HEADS * HEAD_PAD,)), -1, axis)


def _unpad_value_lanes(w, axis):
    w = jnp.moveaxis(w, axis, -1)
    lead = w.shape[:-1]
    t = w.reshape(lead + (HEADS, HEAD_PAD))[..., HEAD_PAD - 64:]
    return jnp.moveaxis(t.reshape(lead + (HEADS * 64,)), -1, axis)


def _pad_w_in_t(wt):
    z = jnp.zeros((NOPE, wt.shape[1]), wt.dtype)
    z2 = jnp.zeros((HEAD_PAD - NOPE - ROPE, wt.shape[1]), wt.dtype)
    return jnp.concatenate([wt[:1408], z, wt[1408:], z2], axis=0)


def _unpad_w_in_t(wt):
    return jnp.concatenate([wt[:1408], wt[1408 + NOPE:1408 + NOPE + ROPE]], axis=0)


def _rope_tables(positions):
    freqs = ROPE_THETA ** (-jnp.arange(0, ROPE, 2, dtype=F32) / ROPE)
    ang = positions.astype(F32)[..., None] * freqs
    cos, sin = jnp.cos(ang), jnp.sin(ang)
    lead = cos.shape[:-1]
    ones = jnp.ones(lead + (NOPE,), F32)
    zeros_n = jnp.zeros(lead + (NOPE,), F32)
    zeros_p = jnp.zeros(lead + (HEAD_PAD - NOPE - ROPE,), F32)
    ctab = jnp.concatenate([ones, cos, cos, zeros_p], axis=-1)
    stab = jnp.concatenate([zeros_n, -sin, sin, zeros_p], axis=-1)
    return ctab, stab


def _layer_weights(full, p, l):
    ws = p["gmlp_ws"][l]
    tril = jnp.tril(jnp.ones((CHUNK, CHUNK), bool))
    bs = p["gmlp_bs"][l]
    bexp = jnp.repeat(bs.reshape(GROUPS // 2, 2, CHUNK).transpose(0, 2, 1), GROUP_DIM, axis=2)
    return dict(
        w_in_t=_pad_w_in_t(full["w_in_t"]),
        w_uq=_pad_heads(full["mla_w_uq"], NOPE + ROPE),
        w_ukv=full["mla_w_ukv"],
        w_out=jnp.concatenate([_pad_value_lanes(full["w_out"][D_GMLP:], 0), full["w_out"][:D_GMLP]], axis=0),
        ws=ws,
        wst=jnp.where(tril[None], ws, 0.0).transpose(0, 2, 1).astype(BF16),
        bexp=bexp,
        g_mix=p["norm_mix_g"][l][None],
        g_ffn=p["norm_ffn_g"][l][None],
        g_q=p["mla_q_norm_g"][l][None],
        g_kv=p["mla_kv_norm_g"][l][None],
        g_og=p["out_norm_gmlp_g"][l][None],
        g_oa=_pad_value_lanes(p["out_norm_mla_g"][l], 0)[None],
    )


def _local_step(x3, target3, positions, mods, final_g, plan):
    bsz, seq, d = x3.shape
    tok = bsz * seq
    tmt = min(512, seq)
    tmk = min(1024, seq)
    tmw = min(2048, tok)
    chunk = (None, None, FLAT_W, FLAT_W)
    ff_grad_shape = (N_CHIPS, 2 * FLAT_W, FLAT_W)
    ctab, stab = _rope_tables(positions)
    lw = [None] * DEPTH

    def flat(t):
        return t.reshape(tok, t.shape[-1])

    def cube(t):
        return t.reshape(bsz, seq, t.shape[-1])

    def carrying(l, tag, fn, *args, **kw):
        side = plan.host(l, tag)
        if side is None:
            return fn(*args, **kw)
        res, side_outs = fn(*args, side=side, **kw)
        plan.hosted(l, tag, side_outs)
        return res

    saved = []
    x = x3
    for l in range(DEPTH):
        lw[l] = plan.layer(l)
        w, mod = lw[l], mods[l]
        h1 = _normmod_fwd(x, w["g_mix"], mod, SHIFT1, SCALE1, name=f"l{l}_normmod1")
        z = cube(_mm(flat(h1), w["w_in_t"], dims="nt", name=f"l{l}_w_in", tm=tmt, tn=D_IN_PAD, tk=d,
                     out_dtypes=(BF16,)))
        yg = _gmlp_fwd(z, w["ws"], w["bexp"], w["g_og"], name=f"l{l}_gmlp_fwd")
        q, kv, kp = _mla_prep_fwd(z, w["g_q"], w["g_kv"], w["w_uq"], w["w_ukv"], ctab, stab, name=f"l{l}_mla_prep")
        o, lse = carrying(l, "fwd_attn", _attn_fwd, q, kv, kp, name=f"l{l}_attn_fwd")
        y = _onorm_fwd(o, yg, w["g_oa"], name=f"l{l}_onorm_fwd")

        def out_epi(po, xv, gm):
            return po, xv + gm[0][GATE1:GATE1 + 1, :] * po

        po, x_mid = carrying(l, "fwd_out_a", _mm, flat(y), w["w_out"], dims="nn", name=f"l{l}_w_out",
                             tm=tmt, tn=d, tk=y.shape[-1], out_dtypes=(BF16, F32), epilogue=out_epi,
                             extras=(flat(x), mod), extra_specs=(None, _mod_spec(tmt, d, seq)))
        x_mid = cube(x_mid)
        h2 = _normmod_fwd(x_mid, w["g_ffn"], mod, SHIFT2, SCALE2, name=f"l{l}_normmod2")

        def act_epi(acc):
            r = jnp.maximum(acc, 0.0)
            return (r * r,)

        r = carrying(l, "fwd_ff1", _mm, flat(h2), w["ff"], dims="nn", name=f"l{l}_w_ff1", tm=tmw, tn=FLAT_W,
                     tk=d, out_dtypes=(BF16,), epilogue=act_epi, weights_outer=True, n=D_FF,
                     b_block=(chunk, lambda i, j, k: (j, 0, 0, 0)))

        def ff2_epi(acc, xv, gm):
            return acc, xv + gm[0][GATE2:GATE2 + 1, :] * acc

        f, x_out = carrying(l, "fwd_ff2", _mm, r, w["ff"], dims="nn", name=f"l{l}_w_ff2", tm=tmk, tn=d, tk=FLAT_W,
                            out_dtypes=(BF16, F32), epilogue=ff2_epi, extras=(flat(x_mid), mod),
                            extra_specs=(None, _mod_spec(tmk, d, seq)), n=d,
                            b_block=(chunk, lambda i, j, k: (k, 1, 0, 0)))
        saved.append(dict(x_in=x, h1=h1, z=z, q=q, kv=kv, kp=kp, o=o, lse=lse, y=y, po=cube(po),
                          x_mid=x_mid, h2=h2, r=r, f=cube(f)))
        x = cube(x_out)

    grads = [dict() for _ in range(DEPTH)]
    dmods = [None] * DEPTH
    top = DEPTH - 1
    node = _resnode_bwd(x, final_g[None], name="final_loss_bwd", target3=target3,
                        branch3=saved[top]["f"], mod_gate=mods[top], gate_row=GATE2)
    loss_part = node["loss"][0, 0]
    d_final_g = node["dg"][0]
    plan.scalars(loss_part, d_final_g)
    for l in range(DEPTH - 1, -1, -1):
        w, mod, s = lw[l], mods[l], saved[l]
        dx_out, dfb, dgate2 = node["dx"], flat(node["dbr"]), node["dgate"][:, 0]

        def dact_epi(acc, rv):
            return (acc * (2.0 * jnp.sqrt(rv.astype(F32))),)

        da = carrying(l, "bwd_d_r", _mm, dfb, w["ff"], dims="nt", name=f"l{l}_d_r", tm=tmw, tn=FLAT_W, tk=d,
                      out_dtypes=(BF16,), epilogue=dact_epi, extras=(s["r"],), weights_outer=True, n=D_FF,
                      b_block=(chunk, lambda i, j, k: (j, 1, 0, 0)))
        g_ff = carrying(l, "bwd_dw_ff2", _mm, s["r"], dfb, dims="tn", name=f"l{l}_dw_ff2", tm=FLAT_W, tn=d,
                        tk=1024, out_into=(ff_grad_shape, (None, FLAT_W, FLAT_W), lambda i, j, k: (i, 1, 0), None))
        g_ff = carrying(l, "bwd_dw_ff1", _mm, flat(s["h2"]), da, dims="tn", name=f"l{l}_dw_ff1", tm=d, tn=FLAT_W,
                        tk=1024, out_into=(ff_grad_shape, (None, FLAT_W, FLAT_W), lambda i, j, k: (j, 0, 0), g_ff))
        plan.ff_grads(l, g_ff)
        dh2 = carrying(l, "bwd_d_h2", _mm, da, w["ff"], dims="nt", name=f"l{l}_d_h2", tm=tmk, tn=d, tk=FLAT_W,
                       n=d, b_block=(chunk, lambda i, j, k: (k, 0, 0, 0)), out_dtypes=(BF16,))
        node = _resnode_bwd(s["x_mid"], w["g_ffn"], name=f"l{l}_resnode_ffn", dh3=cube(dh2), dres3=dx_out,
                            mod_nm=mod, rows=(SHIFT2, SCALE2), branch3=s["po"], mod_gate=mod, gate_row=GATE1)
        grads[l]["norm_ffn_g"] = node["dg"][0]
        dshift2, dscale2 = node["dnm"][:, 0], node["dnm"][:, 1]
        dx_mid, dpo, dgate1 = node["dx"], flat(node["dbr"]), node["dgate"][:, 0]

        wy = s["y"].shape[-1]
        dy = cube(carrying(l, "bwd_d_y", _mm, dpo, w["w_out"], dims="nt", name=f"l{l}_d_y", tm=tmt, tn=wy, tk=d,
                           out_dtypes=(BF16,)))
        dw_out = carrying(l, "bwd_dw_out", _mm, flat(s["y"]), dpo, dims="tn", name=f"l{l}_dw_out", tm=wy // 3,
                          tn=d, tk=1024)
        hw = HEADS * HEAD_PAD
        grads[l]["w_out"] = jnp.concatenate([dw_out[hw:], _unpad_value_lanes(dw_out[:hw], 0)], axis=0)

        dz, dws, dbs, dg_og = _gmlp_bwd(s["z"], dy, w["ws"], w["wst"], w["bexp"], w["g_og"],
                                        name=f"l{l}_gmlp_bwd", dy_col=hw // D_GMLP)
        grads[l]["gmlp_ws"], grads[l]["gmlp_bs"], grads[l]["out_norm_gmlp_g"] = dws, dbs, dg_og[0]

        do, dl, dg_oa = _onorm_bwd(s["o"], dy, w["g_oa"], name=f"l{l}_onorm_bwd")
        grads[l]["out_norm_mla_g"] = _unpad_value_lanes(dg_oa[0], 0)
        dq, dk, dv = carrying(l, "bwd_attn_dkv", _attn_bwd, s["q"], s["kv"], s["kp"], do, s["lse"], dl,
                              name=f"l{l}_attn_bwd")
        dz, cq, dqb, ckv, dkvb, dg_q, dg_kv = _mla_prep_bwd(
            s["z"], dz, dq, dk, dv, w["g_q"], w["g_kv"], w["w_uq"], w["w_ukv"], ctab, stab,
            name=f"l{l}_mla_prep_bwd")
        grads[l]["mla_q_norm_g"], grads[l]["mla_kv_norm_g"] = dg_q[0], dg_kv[0]
        dw_uq = carrying(l, "bwd_dw_uq", _mm, flat(cq), flat(dqb), dims="tn", name=f"l{l}_dw_uq", tm=Q_RANK,
                         tn=1024, tk=1024)
        grads[l]["mla_w_uq"] = _unpad_heads(dw_uq, NOPE + ROPE)
        grads[l]["mla_w_ukv"] = _mm(flat(ckv), flat(dkvb), dims="tn", name=f"l{l}_dw_ukv", tm=KV_RANK, tn=1024, tk=1024)

        grads[l]["w_in_t"] = _unpad_w_in_t(_mm(flat(dz), flat(s["h1"]), dims="tn", name=f"l{l}_dw_in",
                                               tm=D_IN_PAD // 2, tn=d, tk=1024))
        plan.layer_grads(l, grads[l])
        dh1 = carrying(l, "bwd_d_h1", _mm, flat(dz), w["w_in_t"], dims="nn", name=f"l{l}_d_h1", tm=tmt, tn=d,
                       tk=D_IN_PAD, out_dtypes=(BF16,))
        below = dict(branch3=saved[l - 1]["f"], mod_gate=mods[l - 1], gate_row=GATE2) if l > 0 else {}
        node = carrying(l, "bwd_resnode_mix", _resnode_bwd, s["x_in"], w["g_mix"], name=f"l{l}_resnode_mix",
                        dh3=cube(dh1), dres3=dx_mid, mod_nm=mod, rows=(SHIFT1, SCALE1), **below)
        grads[l]["norm_mix_g"] = node["dg"][0]
        dshift1, dscale1 = node["dnm"][:, 0], node["dnm"][:, 1]
        dmods[l] = jnp.stack([dshift1, dscale1, dgate1, dshift2, dscale2, dgate2], axis=1)
        plan.layer_done(l)
    return node["dx"], dmods


W_NAMES = ("w_ada", "b_ada", "norm_mix_g", "w_in", "gmlp_ws", "gmlp_bs", "mla_q_norm_g", "mla_kv_norm_g",
           "mla_w_uq", "mla_w_ukv", "out_norm_gmlp_g", "out_norm_mla_g", "w_out", "norm_ffn_g", "w_ff1", "w_ff2",
           "final_norm_g")
FLAT_KEY = {"w_in": "w_in", "w_uq": "mla_w_uq", "w_ukv": "mla_w_ukv", "w_out": "w_out", "w_ff1": "w_ff1",
            "w_ff2": "w_ff2"}
COL_SHARDED = ("w_in", "w_uq", "w_ukv", "w_ff1")
FULL_SHAPE = {"w_in": (D_MODEL, D_IN), "w_uq": (Q_RANK, HEADS * (NOPE + ROPE)), "w_ukv": (KV_RANK, HEADS * 128),
              "w_out": (D_MODEL, D_MODEL), "w_ff1": (D_MODEL, D_FF), "w_ff2": (D_FF, D_MODEL)}
SMALL_LAYER_NAMES = ("norm_mix_g", "gmlp_ws", "gmlp_bs", "mla_q_norm_g", "mla_kv_norm_g", "out_norm_gmlp_g",
                     "out_norm_mla_g", "norm_ffn_g")


def _silu(v):
    return v * (1.0 / (1.0 + jnp.exp(-v)))


class _CommPlan:
    FWD = {"fwd_attn": ("ff", 0, "spread"), "fwd_out_a": ("ff", 0, "pass"),
           "fwd_ff1": ("mix", 1, "spread"), "fwd_ff2": ("mix", 1, "pass")}
    BWD = {"bwd_d_r": ("mix", 1), "bwd_dw_ff2": ("mix", 1), "bwd_dw_ff1": ("mix", 1),
           "bwd_d_h2": ("ff", 0), "bwd_attn_dkv": ("ff", 0), "bwd_dw_uq": ("ff", 0)}
    BWD_LAST = {"bwd_d_h1": ("mix", 0), "bwd_resnode_mix": ("mix", 0)}
    SMALL = {"bwd_d_y": "spread", "bwd_dw_out": "pass"}

    def __init__(self, weights, ids, dev, core):
        self.weights, self.ids, self.dev, self.core = weights, ids, dev, core
        self.used, self.rows = _flat_rows()
        self.flat = {("mix", l): self._flat_mix(l) for l in range(DEPTH)}
        self.flat.update({("ff", l): jnp.concatenate([weights["w_ff1"][l], weights["w_ff2"][l]], axis=0).astype(BF16)
                          for l in range(DEPTH)})
        self.lw, self.rs, self.grads, self.spread = {}, {}, {}, {}
        self.small_vec, self.small_sum, self.small_spread, self.extra = {}, {}, None, {}
        (gath,) = _run_exchange(_gather_spread(self.flat["mix", 0]), name="l0_mix_gather_spread")
        (gath,) = _run_exchange(_gather_pass_on(gath), name="l0_mix_gather_pass_on")
        self._arrived("mix", 0, gath)

    def _flat_mix(self, l):
        pieces = []
        for nm, _ in FSDP_SECTIONS:
            shard = self.weights[FLAT_KEY[nm]][l]
            pieces.append(shard.T if nm == "w_in" else shard.reshape(-1, FLAT_W))
        pieces.append(jnp.zeros((self.rows - self.used, FLAT_W), F32))
        return jnp.concatenate(pieces, axis=0).astype(BF16)

    def _arrived(self, group, l, gath):
        flat = self.flat[group, l]
        hr = flat.shape[0] // 2
        mine = lax.dynamic_slice(flat, (self.core * hr, 0), (hr, FLAT_W))
        gath = lax.dynamic_update_slice(gath, mine[None], (self.dev, 0, 0))
        if group == "ff":
            self.lw[l]["ff"] = gath.reshape(N_CHIPS, 2, hr, FLAT_W)
            return
        w_gath = gath.reshape(N_CHIPS, self.rows, FLAT_W)
        full, off = {}, 0
        for nm, nrows in FSDP_SECTIONS:
            sec = w_gath[:, off:off + nrows]
            off += nrows
            rows, cols = FULL_SHAPE[nm]
            if nm == "w_in":
                full["w_in_t"] = sec.reshape(cols, rows)
            else:
                full[FLAT_KEY[nm]] = (_chunks_to_cols(sec, rows, cols) if nm in COL_SHARDED
                                      else sec.reshape(rows, cols))
        self.lw[l] = _layer_weights(full, self.weights, l)

    def layer(self, l):
        return self.lw[l]

    def host(self, l, tag):
        if tag in self.FWD:
            group, ahead, what = self.FWD[tag]
            if l + ahead >= DEPTH:
                return None
            return _gather_spread(self.flat[group, l + ahead]) if what == "spread" else _gather_pass_on(self.spread[group])
        if tag in self.SMALL:
            if l + 1 not in self.small_vec:
                return None
            if self.SMALL[tag] == "spread":
                return _gather_spread(self.small_vec[l + 1], halves=False)
            return _gather_pass_on(self.small_spread)
        rs = self._rs_for(l, tag)
        return None if rs is None or rs.stage > 2 else rs.next_exchange()

    def _rs_for(self, l, tag):
        if tag in self.BWD_LAST:
            return self.rs.get(self.BWD_LAST[tag]) if l == 0 else None
        group, ahead = self.BWD[tag]
        return self.rs.get((group, l + ahead))

    def hosted(self, l, tag, outs):
        if tag in self.FWD:
            group, ahead, what = self.FWD[tag]
            if what == "spread":
                self.spread[group] = outs[0]
            else:
                self._arrived(group, l + ahead, outs[0])
        elif tag in self.SMALL:
            if self.SMALL[tag] == "spread":
                self.small_spread = outs[0]
            else:
                self._small_arrived(l + 1, outs[0])
        else:
            self._rs_for(l, tag).done(outs)

    def ff_grads(self, l, g_ff):
        self.rs["ff", l] = _ReduceScatter(g_ff, self.ids, f"l{l}_ff_rs")

    def layer_grads(self, l, grads):
        self.grads[l] = grads
        pieces = []
        for nm, nrows in FSDP_SECTIONS:
            if nm == "w_in":
                pieces.append(grads["w_in_t"].reshape(N_CHIPS, nrows, FLAT_W))
                continue
            g = grads[FLAT_KEY[nm]]
            pieces.append(_cols_to_chunks(g) if nm in COL_SHARDED else g.reshape(N_CHIPS, nrows, FLAT_W))
        pieces.append(jnp.zeros((N_CHIPS, self.rows - self.used, FLAT_W), F32))
        self.rs["mix", l] = _ReduceScatter(jnp.concatenate(pieces, axis=1), self.ids, f"l{l}_mix_rs")

    def scalars(self, loss_part, d_final_g):
        self.extra = {0: [loss_part[None]]}
        self.extra.setdefault(DEPTH - 1, []).insert(0, d_final_g)

    def layer_done(self, l):
        if l == 0:
            self.rs["mix", 0].finish_alone()
        parts = [self.grads[l][nm].reshape(-1) for nm in SMALL_LAYER_NAMES] + self.extra.get(l, [])
        vec = jnp.concatenate(parts)
        rows = -(-vec.shape[0] // (8 * FLAT_W)) * 8
        self.small_vec[l] = jnp.pad(vec, (0, rows * FLAT_W - vec.shape[0])).reshape(rows, FLAT_W)
        if l == 0:
            (gath,) = _run_exchange(_gather_spread(self.small_vec[0], halves=False), name="l0_small_spread")
            self._small_arrived(0, _run_exchange(_gather_pass_on(gath), name="l0_small_pass_on")[0])

    def _small_arrived(self, l, gath):
        gath = lax.dynamic_update_slice(gath, self.small_vec[l][None], (self.dev, 0, 0))
        self.small_sum[l] = _sum_leading(gath, name=f"l{l}_small_sum").reshape(-1)

    def small_grads(self):
        out = {nm: [] for nm in SMALL_LAYER_NAMES}
        for l in range(DEPTH):
            off = 0
            for nm in SMALL_LAYER_NAMES:
                size = self.weights[nm][l].size
                out[nm].append(self.small_sum[l][off:off + size].reshape(self.weights[nm].shape[1:]))
                off += size
            if l == DEPTH - 1:
                final = self.small_sum[l][off:off + self.weights["final_norm_g"].size]
                off += final.shape[0]
            if l == 0:
                loss = self.small_sum[l][off]
        res = {nm: jnp.stack(parts, axis=0) for nm, parts in out.items()}
        res["final_norm_g"] = final
        return loss, res

    def mix_grads(self):
        per = {FLAT_KEY[nm]: [] for nm, _ in FSDP_SECTIONS}
        for l in range(DEPTH):
            shard, off = self.rs["mix", l].result, 0
            for nm, nrows in FSDP_SECTIONS:
                key = FLAT_KEY[nm]
                sec = shard[off:off + nrows]
                per[key].append(sec.T if nm == "w_in" else sec.reshape(self.weights[key].shape[1:]))
                off += nrows
        return {key: jnp.stack(parts, axis=0) for key, parts in per.items()}

    def ff_shards(self):
        return [self.rs["ff", l].result for l in range(DEPTH)]


def kernel(x, c, positions, w_ada, b_ada, norm_mix_g, w_in, gmlp_ws, gmlp_bs, mla_q_norm_g, mla_kv_norm_g, mla_w_uq, mla_w_ukv, out_norm_gmlp_g, out_norm_mla_g, w_out, norm_ffn_g, w_ff1, w_ff2, final_norm_g, loss_target, m_w_ada, m_b_ada, m_norm_mix_g, m_w_in, m_gmlp_ws, m_gmlp_bs, m_mla_q_norm_g, m_mla_kv_norm_g, m_mla_w_uq, m_mla_w_ukv, m_out_norm_gmlp_g, m_out_norm_mla_g, m_w_out, m_norm_ffn_g, m_w_ff1, m_w_ff2, m_final_norm_g, v_w_ada, v_b_ada, v_norm_mix_g, v_w_in, v_gmlp_ws, v_gmlp_bs, v_mla_q_norm_g, v_mla_kv_norm_g, v_mla_w_uq, v_mla_w_ukv, v_out_norm_gmlp_g, v_out_norm_mla_g, v_w_out, v_norm_ffn_g, v_w_ff1, v_w_ff2, v_final_norm_g):
    weights = dict(w_ada=w_ada, b_ada=b_ada, norm_mix_g=norm_mix_g, w_in=w_in, gmlp_ws=gmlp_ws, gmlp_bs=gmlp_bs,
                   mla_q_norm_g=mla_q_norm_g, mla_kv_norm_g=mla_kv_norm_g, mla_w_uq=mla_w_uq, mla_w_ukv=mla_w_ukv,
                   out_norm_gmlp_g=out_norm_gmlp_g, out_norm_mla_g=out_norm_mla_g, w_out=w_out,
                   norm_ffn_g=norm_ffn_g, w_ff1=w_ff1, w_ff2=w_ff2, final_norm_g=final_norm_g)
    mom_m = dict(zip(W_NAMES, (m_w_ada, m_b_ada, m_norm_mix_g, m_w_in, m_gmlp_ws, m_gmlp_bs, m_mla_q_norm_g,
                               m_mla_kv_norm_g, m_mla_w_uq, m_mla_w_ukv, m_out_norm_gmlp_g, m_out_norm_mla_g,
                               m_w_out, m_norm_ffn_g, m_w_ff1, m_w_ff2, m_final_norm_g)))
    mom_v = dict(zip(W_NAMES, (v_w_ada, v_b_ada, v_norm_mix_g, v_w_in, v_gmlp_ws, v_gmlp_bs, v_mla_q_norm_g,
                               v_mla_kv_norm_g, v_mla_w_uq, v_mla_w_ukv, v_out_norm_gmlp_g, v_out_norm_mla_g,
                               v_w_out, v_norm_ffn_g, v_w_ff1, v_w_ff2, v_final_norm_g)))
    bsz, seq, d = x.shape
    px, py, pc = _position()
    chip = 2 * px + py
    dev = 2 * chip + pc
    ids = jnp.stack([pc, chip]).astype(jnp.int32)
    n_ex = N_DEV * bsz
    ada_cols = w_ada.shape[-1]

    c_all = _allgather8(c.reshape(bsz * d // 128, 128), name="gather_c").reshape(n_ex, d)
    mod_parts = []
    for l in range(DEPTH):
        bias = lax.dynamic_slice(b_ada[l], (chip * ada_cols,), (ada_cols,))[None]
        mod_parts.append(_mm(c_all, w_ada, dims="nn", name=f"l{l}_mod", tm=n_ex, tn=ada_cols, tk=d, n=ada_cols,
                             b_block=((None, d, ada_cols), lambda i, j, k, l=l: (l, k, j)),
                             epilogue=lambda acc, bv: (acc + bv,), extras=(bias,),
                             extra_specs=(pl.BlockSpec((1, ada_cols), lambda i, j, k: (0, j)),), a_fn=_silu))
    mod_g = _allgather8(jnp.concatenate(mod_parts, axis=0), name="gather_mod")
    mod_g = mod_g.reshape(N_CHIPS, 2, DEPTH, n_ex, ada_cols)[:, 0]
    mod_full = mod_g.transpose(1, 2, 0, 3).reshape(DEPTH, n_ex, N_CHIPS * ada_cols)
    mod_mine = lax.dynamic_slice(mod_full, (0, dev * bsz, 0), (DEPTH, bsz, N_MOD * d))
    mod_mine = jnp.pad(mod_mine.reshape(DEPTH, bsz, N_MOD, d), ((0, 0), (0, 0), (0, MOD_ROWS - N_MOD), (0, 0)))
    mods = [mod_mine[l] for l in range(DEPTH)]

    plan = _CommPlan(weights, ids, dev, pc)
    grad_x, dmods = _local_step(x, loss_target, positions, mods, final_norm_g, plan)
    grad = plan.mix_grads()

    loss, small = plan.small_grads()
    grad.update(small)

    dmod = jnp.stack(dmods, axis=1).reshape(bsz * DEPTH * N_MOD, d)
    dmod_all = _allgather8(dmod, name="gather_dmod").reshape(n_ex, DEPTH, N_MOD * d)
    gw, gb = [], []
    for l in range(DEPTH):
        dm = dmod_all[:, l]
        dm_cols = lax.dynamic_slice(dm, (0, chip * ada_cols), (n_ex, ada_cols))
        gw.append(_mm(c_all, dm_cols, dims="tn", name=f"l{l}_dw_ada", tm=d, tn=ada_cols, tk=n_ex, a_fn=_silu,
                      out_into=(w_ada.shape, (None, d, ada_cols), lambda i, j, k, l=l: (l, i, j),
                                gw[-1] if gw else None)))
        gb.append(_sum_leading(dm.reshape(n_ex, N_MOD * d // FLAT_W, FLAT_W), name=f"l{l}_db_ada").reshape(-1))
    grad["w_ada"] = gw[-1]
    grad["b_ada"] = jnp.stack(gb, axis=0)

    delta, new_m, new_v = {}, {}, {}
    ff_bufs = plan.ff_shards()
    for nm, row_off in (("w_ff1", 0), ("w_ff2", FLAT_W)):
        grad[nm], delta[nm], new_m[nm], new_v[nm] = _adamw_layers(
            weights[nm], mom_m[nm], mom_v[nm], ff_bufs, row_off, name=f"adamw_{nm}")
    for nm in W_NAMES:
        if nm not in delta:
            delta[nm], new_m[nm], new_v[nm] = _adamw(weights[nm], grad[nm], mom_m[nm], mom_v[nm],
                                                     name=f"adamw_{nm}")
    return (loss, grad_x, *[grad[nm] for nm in W_NAMES], *[delta[nm] for nm in W_NAMES],
            *[new_m[nm] for nm in W_NAMES], *[new_v[nm] for nm in W_NAMES])
```

```python
import functools
import math

import jax
import jax.numpy as jnp
from jax import lax
from jax.experimental import pallas as pl
from jax.experimental.pallas import tpu as pltpu

F32 = jnp.float32
BF16 = jnp.bfloat16

D_MODEL = 1024
DEPTH = 2
D_GMLP = 512
GROUPS = 8
GROUP_DIM = 64
CHUNK = 128
HEADS = 8
NOPE = 64
ROPE = 32
HEAD_PAD = 128
Q_RANK = 256
KV_RANK = 128
D_FF = 4096
N_MOD = 6
MOD_ROWS = 8
EPS = 1e-6
ROPE_THETA = 10000.0
D_IN = 1440
D_IN_PAD = 1536
ATTN_SCALE = (NOPE + ROPE) ** -0.5
LOG2E = math.log2(math.e)
SCALE_LOG2 = ATTN_SCALE * LOG2E
N_CHIPS = 4
N_DEV = 8

ADAM_LR = 0.001
ADAM_B1 = 0.9
ADAM_B2 = 0.999
ADAM_EPS = 1e-08
ADAM_WD = 0.01
ADAM_STEP = 10

VMEM_LIMIT = 48 * 1024 * 1024
FLAT_W = 1024
ROW_ALIGN = 256

NN = (((1,), (0,)), ((), ()))
NT = (((1,), (1,)), ((), ()))
TN = (((0,), (0,)), ((), ()))
MESH = pl.DeviceIdType.MESH

SHIFT1, SCALE1, GATE1, SHIFT2, SCALE2, GATE2 = range(6)

FSDP_SECTIONS = (("w_out", 256), ("w_in", 360), ("w_uq", 48), ("w_ukv", 32))


def _cparams(vmem=VMEM_LIMIT):
    return pltpu.CompilerParams(vmem_limit_bytes=vmem)


def _dot(a, b, dims=NN):
    return lax.dot_general(a, b, dims, preferred_element_type=F32)


def _iota(shape, axis):
    return lax.broadcasted_iota(jnp.int32, shape, axis)


def _gelu(x):
    k = math.sqrt(2.0 / math.pi)
    return 0.5 * x * (1.0 + jnp.tanh(k * (x + 0.044715 * (x * x * x))))


def _gelu_and_grad(x):
    k = math.sqrt(2.0 / math.pi)
    x2 = x * x
    t = jnp.tanh(k * (x + 0.044715 * (x2 * x)))
    half = 0.5 * (1.0 + t)
    return x * half, half + 0.5 * x * (1.0 - t * t) * (k * (1.0 + 3.0 * 0.044715 * x2))


def _rms_fwd(x, g, n):
    r = lax.rsqrt(jnp.sum(x * x, axis=-1, keepdims=True) * (1.0 / n) + EPS)
    return x * r * g


def _rms_bwd(x, g, dy, n):
    r = lax.rsqrt(jnp.sum(x * x, axis=-1, keepdims=True) * (1.0 / n) + EPS)
    xh = x * r
    dxh = dy * g
    dx = r * (dxh - xh * (jnp.sum(dxh * xh, axis=-1, keepdims=True) * (1.0 / n)))
    dg = jnp.sum(dy * xh, axis=0, keepdims=True)
    return dx, dg


def _pick_rows(rows, limit):
    if rows <= limit:
        return rows
    for t in range(limit, 7, -8):
        if rows % t == 0:
            return t
    return rows


def _mm(a, b, *, dims, name, tm=512, tn=1024, tk=1024, out_dtypes=(F32,), epilogue=None,
        extras=(), extra_specs=(), a_fn=None, weights_outer=False, side=None, b_block=None, n=None,
        out_into=None):
    if dims == "tn":
        kk, m = a.shape
    else:
        m, kk = a.shape
    if n is None:
        n = b.shape[0] if dims == "nt" else b.shape[1]
    tm, tn, tk = min(tm, m), min(tn, n), min(tk, kk)
    assert m % tm == 0 and n % tn == 0 and kk % tk == 0, (name, a.shape, b.shape, tm, tn, tk)
    ni, nj, nk = m // tm, n // tn, kk // tk

    def spec(shape, pick):
        if weights_outer:
            return pl.BlockSpec(shape, lambda j, i, k: pick(i, j, k))
        return pl.BlockSpec(shape, pick)

    if dims == "tn":
        a_spec = spec((tk, tm), lambda i, j, k: (k, i))
    else:
        a_spec = spec((tm, tk), lambda i, j, k: (i, k))
    if b_block is not None:
        b_spec = spec(*b_block)
    elif dims == "nt":
        b_spec = spec((tn, tk), lambda i, j, k: (j, k))
    else:
        b_spec = spec((tk, tn), lambda i, j, k: (k, j))
    o_spec = spec((tm, tn), lambda i, j, k: (i, j))
    out_shape = [jax.ShapeDtypeStruct((m, n), dt) for dt in out_dtypes]
    out_specs = [o_spec] * len(out_dtypes)
    prev, io_aliases = (), {}
    if out_into is not None:
        full_shape, block, index, before = out_into
        assert len(out_dtypes) == 1 and not extras
        out_shape = [jax.ShapeDtypeStruct(full_shape, out_dtypes[0])]
        out_specs = [spec(block, index)]
        if before is not None:
            prev, io_aliases = (before,), {2: 0}
    assert not (weights_outer and extra_specs)
    dn = {"nn": NN, "nt": NT, "tn": TN}[dims]
    n_ex, n_out = len(extras), len(out_dtypes)
    e_specs = [o_spec if s is None else s for s in (tuple(extra_specs) + (None,) * n_ex)[:n_ex]]

    n_prev = len(prev)

    def body(*refs):
        a_ref, b_ref = refs[0], refs[1]
        e_refs = refs[2 + n_prev:2 + n_prev + n_ex]
        o_refs = refs[2 + n_prev + n_ex:2 + n_prev + n_ex + n_out]
        av = a_ref[...]
        if a_fn is not None:
            av = a_fn(av)
        part = _dot(av.astype(BF16), b_ref[...].astype(BF16), dn)

        def finish(acc):
            outs = (acc,) if epilogue is None else epilogue(acc, *[e[...] for e in e_refs])
            for o_ref, o in zip(o_refs, outs):
                o_ref[...] = o.astype(o_ref.dtype)

        if nk == 1:
            finish(part)
        else:
            acc_ref = refs[-1]
            k = pl.program_id(2)

            @pl.when(k == 0)
            def _():
                acc_ref[...] = part

            @pl.when(k > 0)
            def _():
                acc_ref[...] += part

            @pl.when(k == nk - 1)
            def _():
                finish(acc_ref[...])

    outs, side_outs = _hosted_call(
        body, name=name, grid=(nj, ni, nk) if weights_outer else (ni, nj, nk),
        in_specs=[a_spec, b_spec] + [ANY_SPEC] * n_prev + e_specs,
        out_specs=out_specs, out_shape=out_shape,
        scratch_shapes=[pltpu.VMEM((tm, tn), F32)] if nk > 1 else [],
        args=(a, b, *prev, *extras), side=side, io_aliases=io_aliases)
    res = outs[0] if n_out == 1 else outs
    return res if side is None else (res, side_outs)


def _mod_spec(tm, tn, seq):
    return pl.BlockSpec((1, MOD_ROWS, tn), lambda i, j, k: ((i * tm) // seq, 0, j))


def _normmod_fwd(x3, g, mod, shift_row, scale_row, *, name, tb=512, side=None):
    bsz, seq, d = x3.shape
    tb = min(tb, seq)

    def body(x_ref, g_ref, mod_ref, h_ref):
        m = mod_ref[0]
        nrm = _rms_fwd(x_ref[0], g_ref[...], d)
        h = nrm * (1.0 + m[scale_row:scale_row + 1, :]) + m[shift_row:shift_row + 1, :]
        h_ref[0] = h.astype(BF16)

    outs, side_outs = _hosted_call(
        body, name=name, grid=(bsz, seq // tb),
        in_specs=[pl.BlockSpec((1, tb, d), lambda b, i: (b, i, 0)),
                  pl.BlockSpec((1, d), lambda b, i: (0, 0)),
                  pl.BlockSpec((1, MOD_ROWS, d), lambda b, i: (b, 0, 0))],
        out_specs=[pl.BlockSpec((1, tb, d), lambda b, i: (b, i, 0))],
        out_shape=[jax.ShapeDtypeStruct((bsz, seq, d), BF16)],
        args=(x3, g, mod), side=side)
    return outs[0] if side is None else (outs[0], side_outs)


def _pair_mean_exact(x, lo):
    s_lo = jnp.sum(jnp.where(lo, x, 0.0), axis=-1, keepdims=True)
    s_hi = jnp.sum(jnp.where(lo, 0.0, x), axis=-1, keepdims=True)
    return jnp.where(lo, s_lo, s_hi) * (1.0 / GROUP_DIM)


def _gmlp_pair_fwd(gv_p, w0, w1, bias, lo):
    mu = _pair_mean_exact(gv_p, lo)
    dlt = gv_p - mu
    var = _pair_mean_exact(dlt * dlt, lo)
    rstd = lax.rsqrt(var + EPS)
    vn = dlt * rstd
    vnb = vn.astype(BF16)
    mixed = jnp.where(lo, _dot(w0, vnb), _dot(w1, vnb)) + bias
    return vn, vnb, rstd, mixed


def _tril_bf16(w):
    t = w.shape[-1]
    return jnp.where(_iota((t, t), 1) <= _iota((t, t), 0), w, 0.0).astype(BF16)


def _gmlp_fwd(z3, ws, bexp, g_out, *, name):
    bsz, seq, _ = z3.shape
    nc = seq // CHUNK

    def body(u_ref, v_ref, ws_ref, b_ref, g_ref, y_ref):
        lo = _iota((CHUNK, 128), 1) < GROUP_DIM
        gu = _gelu(u_ref[0].astype(F32))
        gv = _gelu(v_ref[0].astype(F32))
        parts = []
        for p in range(GROUPS // 2):
            sl = slice(128 * p, 128 * p + 128)
            w0 = _tril_bf16(ws_ref[2 * p])
            w1 = _tril_bf16(ws_ref[2 * p + 1])
            _, _, _, mixed = _gmlp_pair_fwd(gv[:, sl], w0, w1, b_ref[p], lo)
            parts.append(gu[:, sl] * mixed)
        yg = jnp.concatenate(parts, axis=1)
        y_ref[0] = _rms_fwd(yg, g_ref[...], D_GMLP).astype(BF16)

    return pl.pallas_call(
        body, name=name, grid=(bsz, nc),
        in_specs=[pl.BlockSpec((1, CHUNK, D_GMLP), lambda b, i: (b, i, 0)),
                  pl.BlockSpec((1, CHUNK, D_GMLP), lambda b, i: (b, i, 1)),
                  pl.BlockSpec((GROUPS, CHUNK, CHUNK), lambda b, i: (0, 0, 0)),
                  pl.BlockSpec((GROUPS // 2, CHUNK, 128), lambda b, i: (0, 0, 0)),
                  pl.BlockSpec((1, D_GMLP), lambda b, i: (0, 0))],
        out_specs=pl.BlockSpec((1, CHUNK, D_GMLP), lambda b, i: (b, i, 0)),
        out_shape=jax.ShapeDtypeStruct((bsz, seq, D_GMLP), BF16),
        compiler_params=_cparams(),
    )(z3, z3, ws, bexp, g_out)


def _gmlp_bwd(z3, dyn3, ws, wst, bexp, g_out, *, name, dy_col):
    bsz, seq, _ = z3.shape
    nc = seq // CHUNK
    npair = GROUPS // 2

    def body(u_ref, v_ref, dy_ref, ws_ref, wst_ref, b_ref, g_ref, duv_ref, dws_ref, dbs_ref, dg_ref, dbacc):
        first = jnp.logical_and(pl.program_id(0) == 0, pl.program_id(1) == 0)
        last = jnp.logical_and(pl.program_id(0) == bsz - 1, pl.program_id(1) == nc - 1)

        @pl.when(first)
        def _():
            dws_ref[...] = jnp.zeros_like(dws_ref)
            dg_ref[...] = jnp.zeros_like(dg_ref)
            dbacc[...] = jnp.zeros_like(dbacc)

        lo = _iota((CHUNK, 128), 1) < GROUP_DIM
        tril = _iota((CHUNK, CHUNK), 1) <= _iota((CHUNK, CHUNK), 0)
        u = u_ref[0].astype(F32)
        v = v_ref[0].astype(F32)
        gu, dgu = _gelu_and_grad(u)
        gv, dgv_dv = _gelu_and_grad(v)
        fwd = []
        for p in range(npair):
            sl = slice(128 * p, 128 * p + 128)
            w0 = _tril_bf16(ws_ref[2 * p])
            w1 = _tril_bf16(ws_ref[2 * p + 1])
            fwd.append(_gmlp_pair_fwd(gv[:, sl], w0, w1, b_ref[p], lo))
        yg = jnp.concatenate([gu[:, 128 * p:128 * p + 128] * fwd[p][3] for p in range(npair)], axis=1)
        dyg, dg = _rms_bwd(yg, g_ref[...], dy_ref[0].astype(F32), D_GMLP)
        dg_ref[...] += dg
        du_parts, dv_parts = [], []
        for p in range(npair):
            sl = slice(128 * p, 128 * p + 128)
            vn, vnb, rstd, mixed = fwd[p]
            dyg_p = dyg[:, sl]
            dmixed = dyg_p * gu[:, sl]
            dbacc[p] += dmixed
            dm_lo = jnp.where(lo, dmixed, 0.0).astype(BF16)
            dm_hi = jnp.where(lo, 0.0, dmixed).astype(BF16)
            dws_ref[2 * p] += jnp.where(tril, _dot(dm_lo, vnb, NT), 0.0)
            dws_ref[2 * p + 1] += jnp.where(tril, _dot(dm_hi, vnb, NT), 0.0)
            dmb = dmixed.astype(BF16)
            dvn = jnp.where(lo, _dot(wst_ref[2 * p], dmb), _dot(wst_ref[2 * p + 1], dmb))
            dgv = rstd * (dvn - _pair_mean_exact(dvn, lo) - vn * _pair_mean_exact(dvn * vn, lo))
            dv_parts.append(dgv * dgv_dv[:, sl])
            du_parts.append(dyg_p * mixed * dgu[:, sl])
        duv_ref[0] = jnp.concatenate(du_parts + dv_parts, axis=1).astype(BF16)

        @pl.when(last)
        def _():
            sel = jnp.where(_iota((8, 128), 0) == 0, (_iota((8, 128), 1) < GROUP_DIM).astype(F32),
                            jnp.where(_iota((8, 128), 0) == 1, (_iota((8, 128), 1) >= GROUP_DIM).astype(F32), 0.0))
            for p in range(npair):
                dbs_ref[p] = lax.dot_general(sel, dbacc[p], NT, precision=lax.Precision.HIGHEST,
                                             preferred_element_type=F32)

    duv, dws, dbs, dg = pl.pallas_call(
        body, name=name, grid=(bsz, nc),
        in_specs=[pl.BlockSpec((1, CHUNK, D_GMLP), lambda b, i: (b, i, 0)),
                  pl.BlockSpec((1, CHUNK, D_GMLP), lambda b, i: (b, i, 1)),
                  pl.BlockSpec((1, CHUNK, D_GMLP), lambda b, i: (b, i, dy_col)),
                  pl.BlockSpec((GROUPS, CHUNK, CHUNK), lambda b, i: (0, 0, 0)),
                  pl.BlockSpec((GROUPS, CHUNK, CHUNK), lambda b, i: (0, 0, 0)),
                  pl.BlockSpec((npair, CHUNK, 128), lambda b, i: (0, 0, 0)),
                  pl.BlockSpec((1, D_GMLP), lambda b, i: (0, 0))],
        out_specs=[pl.BlockSpec((1, CHUNK, 2 * D_GMLP), lambda b, i: (b, i, 0)),
                   pl.BlockSpec((GROUPS, CHUNK, CHUNK), lambda b, i: (0, 0, 0)),
                   pl.BlockSpec((npair, 8, CHUNK), lambda b, i: (0, 0, 0)),
                   pl.BlockSpec((1, D_GMLP), lambda b, i: (0, 0))],
        out_shape=[jax.ShapeDtypeStruct((bsz, seq, D_IN_PAD), BF16),
                   jax.ShapeDtypeStruct((GROUPS, CHUNK, CHUNK), F32),
                   jax.ShapeDtypeStruct((npair, 8, CHUNK), F32),
                   jax.ShapeDtypeStruct((1, D_GMLP), F32)],
        scratch_shapes=[pltpu.VMEM((npair, CHUNK, 128), F32)],
        compiler_params=_cparams(),
    )(z3, z3, dyn3, ws, wst, bexp, g_out)
    return duv, dws, dbs[:, :2, :].reshape(GROUPS, CHUNK), dg


def _partner(x):
    width = x.shape[-1]
    lane = _iota(x.shape, x.ndim - 1) % HEAD_PAD
    up = pltpu.roll(x, width - ROPE // 2, x.ndim - 1)
    down = pltpu.roll(x, ROPE // 2, x.ndim - 1)
    first = jnp.logical_and(lane >= NOPE, lane < NOPE + ROPE // 2)
    second = jnp.logical_and(lane >= NOPE + ROPE // 2, lane < NOPE + ROPE)
    return jnp.where(first, up, jnp.where(second, down, 0.0))


def _mla_prep_fwd(z3, g_q, g_kv, w_uq, w_ukv, ctab, stab, *, name, tb=256):
    bsz, seq, _ = z3.shape
    tb = min(tb, seq)
    hw = HEADS * HEAD_PAD

    def body(ql_ref, kvl_ref, krl_ref, gq_ref, gkv_ref, wuq_ref, wukv_ref, c_ref, s_ref, q_ref, kv_ref, kp_ref):
        cq = _rms_fwd(ql_ref[0].astype(F32), gq_ref[...], Q_RANK).astype(BF16)
        q = _dot(cq, wuq_ref[...])
        c1, s1 = c_ref[0], s_ref[0]
        c8, s8 = jnp.tile(c1, (1, HEADS)), jnp.tile(s1, (1, HEADS))
        q_ref[0] = ((q * c8 + _partner(q) * s8) * SCALE_LOG2).astype(BF16)
        ckv = _rms_fwd(kvl_ref[0].astype(F32), gkv_ref[...], KV_RANK).astype(BF16)
        kv = _dot(ckv, wukv_ref[...])
        kv_ref[0] = kv.astype(BF16)
        kr = krl_ref[0].astype(F32)
        kr = kr * c1 + _partner(kr) * s1
        lane = _iota((tb, hw), 1) % HEAD_PAD
        kp_ref[0] = jnp.where(lane < NOPE, kv, jnp.tile(kr, (1, HEADS))).astype(BF16)

    return pl.pallas_call(
        body, name=name, grid=(bsz, seq // tb),
        in_specs=[pl.BlockSpec((1, tb, Q_RANK), lambda b, i: (b, i, 4)),
                  pl.BlockSpec((1, tb, KV_RANK), lambda b, i: (b, i, 10)),
                  pl.BlockSpec((1, tb, HEAD_PAD), lambda b, i: (b, i, 11)),
                  pl.BlockSpec((1, Q_RANK), lambda b, i: (0, 0)),
                  pl.BlockSpec((1, KV_RANK), lambda b, i: (0, 0)),
                  pl.BlockSpec((Q_RANK, hw), lambda b, i: (0, 0)),
                  pl.BlockSpec((KV_RANK, hw), lambda b, i: (0, 0)),
                  pl.BlockSpec((1, tb, HEAD_PAD), lambda b, i: (b, i, 0)),
                  pl.BlockSpec((1, tb, HEAD_PAD), lambda b, i: (b, i, 0))],
        out_specs=[pl.BlockSpec((1, tb, hw), lambda b, i: (b, i, 0))] * 3,
        out_shape=[jax.ShapeDtypeStruct((bsz, seq, hw), BF16)] * 3,
        compiler_params=_cparams(),
    )(z3, z3, z3, g_q, g_kv, w_uq, w_ukv, ctab, stab)


def _mla_prep_bwd(z3, dz3, dq3, dk3, dv3, g_q, g_kv, w_uq, w_ukv, ctab, stab, *, name, tb=256):
    bsz, seq, _ = z3.shape
    tb = min(tb, seq)
    hw = HEADS * HEAD_PAD
    nb = seq // tb

    def body(ql_ref, kvl_ref, dq_ref, dk_ref, dv_ref, gq_ref, gkv_ref, wuq_ref, wukv_ref, c_ref, s_ref, dz_in,
             dz_ref, cq_ref, dqb_ref, ckv_ref, dkvb_ref, dgq_ref, dgkv_ref):
        @pl.when(jnp.logical_and(pl.program_id(0) == 0, pl.program_id(1) == 0))
        def _():
            dgq_ref[...] = jnp.zeros_like(dgq_ref)
            dgkv_ref[...] = jnp.zeros_like(dgkv_ref)

        c1, s1 = c_ref[0], s_ref[0]
        c8, s8 = jnp.tile(c1, (1, HEADS)), jnp.tile(s1, (1, HEADS))
        dqr = dq_ref[0]
        dqb = (dqr * c8 + _partner(dqr * s8)).astype(BF16)
        dqb_ref[0] = dqb
        ql = ql_ref[0].astype(F32)
        cq_ref[0] = _rms_fwd(ql, gq_ref[...], Q_RANK).astype(BF16)
        dql, dgq = _rms_bwd(ql, gq_ref[...], _dot(dqb, wuq_ref[...], NT), Q_RANK)
        dgq_ref[...] += dgq

        dk = dk_ref[0]
        lane = _iota((tb, hw), 1) % HEAD_PAD
        dkvb = jnp.where(lane < NOPE, dk, dv_ref[0]).astype(BF16)
        dkvb_ref[0] = dkvb
        kvl = kvl_ref[0].astype(F32)
        ckv_ref[0] = _rms_fwd(kvl, gkv_ref[...], KV_RANK).astype(BF16)
        dkvl, dgkv = _rms_bwd(kvl, gkv_ref[...], _dot(dkvb, wukv_ref[...], NT), KV_RANK)
        dgkv_ref[...] += dgkv

        dkr = dk[:, 0:HEAD_PAD].astype(F32)
        for h in range(1, HEADS):
            dkr = dkr + dk[:, HEAD_PAD * h:HEAD_PAD * (h + 1)].astype(F32)
        lane1 = _iota((tb, HEAD_PAD), 1)
        dkr = jnp.where(jnp.logical_and(lane1 >= NOPE, lane1 < NOPE + ROPE), dkr, 0.0)
        dkrl = dkr * c1 + _partner(dkr * s1)
        dz_ref[0] = jnp.concatenate([dql, dkvl, dkrl], axis=1).astype(BF16)

    return pl.pallas_call(
        body, name=name, grid=(bsz, nb),
        in_specs=[pl.BlockSpec((1, tb, Q_RANK), lambda b, i: (b, i, 4)),
                  pl.BlockSpec((1, tb, KV_RANK), lambda b, i: (b, i, 10)),
                  pl.BlockSpec((1, tb, hw), lambda b, i: (b, i, 0)),
                  pl.BlockSpec((1, tb, hw), lambda b, i: (b, i, 0)),
                  pl.BlockSpec((1, tb, hw), lambda b, i: (b, i, 0)),
                  pl.BlockSpec((1, Q_RANK), lambda b, i: (0, 0)),
                  pl.BlockSpec((1, KV_RANK), lambda b, i: (0, 0)),
                  pl.BlockSpec((Q_RANK, hw), lambda b, i: (0, 0)),
                  pl.BlockSpec((KV_RANK, hw), lambda b, i: (0, 0)),
                  pl.BlockSpec((1, tb, HEAD_PAD), lambda b, i: (b, i, 0)),
                  pl.BlockSpec((1, tb, HEAD_PAD), lambda b, i: (b, i, 0)),
                  ANY_SPEC],
        out_specs=[pl.BlockSpec((1, tb, 512), lambda b, i: (b, i, 2)),
                   pl.BlockSpec((1, tb, Q_RANK), lambda b, i: (b, i, 0)),
                   pl.BlockSpec((1, tb, hw), lambda b, i: (b, i, 0)),
                   pl.BlockSpec((1, tb, KV_RANK), lambda b, i: (b, i, 0)),
                   pl.BlockSpec((1, tb, hw), lambda b, i: (b, i, 0)),
                   pl.BlockSpec((1, Q_RANK), lambda b, i: (0, 0)),
                   pl.BlockSpec((1, KV_RANK), lambda b, i: (0, 0))],
        out_shape=[jax.ShapeDtypeStruct((bsz, seq, D_IN_PAD), BF16),
                   jax.ShapeDtypeStruct((bsz, seq, Q_RANK), BF16),
                   jax.ShapeDtypeStruct((bsz, seq, hw), BF16),
                   jax.ShapeDtypeStruct((bsz, seq, KV_RANK), BF16),
                   jax.ShapeDtypeStruct((bsz, seq, hw), BF16),
                   jax.ShapeDtypeStruct((1, Q_RANK), F32),
                   jax.ShapeDtypeStruct((1, KV_RANK), F32)],
        input_output_aliases={11: 0},
        compiler_params=_cparams(),
    )(z3, z3, dq3, dk3, dv3, g_q, g_kv, w_uq, w_ukv, ctab, stab, dz3)


ATTN_HEADS_PER_STEP = 2


def _attn_specs(tq, seq, hp):
    blk = pl.BlockSpec((1, tq, hp * HEAD_PAD), lambda b, h, i: (b, i, h))
    full = pl.BlockSpec((1, seq, hp * HEAD_PAD), lambda b, h, i: (b, 0, h))
    return blk, full


def _head(h):
    return slice(HEAD_PAD * h, HEAD_PAD * (h + 1))


def _attn_fwd(q3, kv3, kp3, *, name, tq=512, hp=ATTN_HEADS_PER_STEP, side=None):
    bsz, seq, hw = q3.shape
    tq = min(tq, seq)
    blk, full = _attn_specs(tq, seq, hp)

    def body(q_ref, kv_ref, kp_ref, o_ref, lse_ref):
        i = pl.program_id(2)
        is_nope = _iota((tq, HEAD_PAD), 1) < NOPE
        causal = _iota((tq, tq), 1) <= _iota((tq, tq), 0)

        def step(j, carry, diag):
            st = pl.multiple_of(j * tq, tq)
            out = []
            for h in range(hp):
                m, l, acc = carry[h]
                kvj = kv_ref[0, pl.ds(st, tq), _head(h)]
                s = _dot(q_ref[0, :, _head(h)], kp_ref[0, pl.ds(st, tq), _head(h)], NT)
                if diag:
                    s = jnp.where(causal, s, -1e30)
                m_new = jnp.maximum(m, jnp.max(s, axis=1, keepdims=True))
                alpha = jnp.exp2(m - m_new)
                p = jnp.exp2(s - m_new)
                l = alpha * l + jnp.sum(p, axis=1, keepdims=True)
                acc = alpha * acc + _dot(p.astype(BF16), kvj)
                out.append((m_new, l, acc))
            return tuple(out)

        init = tuple((jnp.full((tq, 1), -1e30, F32), jnp.zeros((tq, 1), F32), jnp.zeros((tq, HEAD_PAD), F32))
                     for _ in range(hp))
        carry = lax.fori_loop(0, i, lambda j, c: step(j, c, False), init)
        carry = step(i, carry, True)
        for h in range(hp):
            m, l, acc = carry[h]
            o_ref[0, :, _head(h)] = jnp.where(is_nope, 0.0, acc / l).astype(BF16)
            lse_ref[0, :, _head(h)] = jnp.broadcast_to(m + jnp.log(l) * LOG2E, (tq, HEAD_PAD))

    outs, side_outs = _hosted_call(
        body, name=name, grid=(bsz, HEADS // hp, seq // tq),
        in_specs=[blk, full, full],
        out_specs=[blk, blk],
        out_shape=[jax.ShapeDtypeStruct((bsz, seq, hw), BF16), jax.ShapeDtypeStruct((bsz, seq, hw), F32)],
        args=(q3, kv3, kp3), side=side)
    return outs if side is None else (outs, side_outs)


def _attn_bwd(q3, kv3, kp3, do3, lse3, dl3, *, name, tq=512, hp=ATTN_HEADS_PER_STEP, side=None):
    bsz, seq, hw = q3.shape
    tq = min(tq, seq)
    nq = seq // tq
    blk, full = _attn_specs(tq, seq, hp)
    rep = tq // HEAD_PAD

    def body(kv_ref, kp_ref, q_ref, do_ref, lse_ref, dl_ref, dq_ref, dk_ref, dv_ref):
        j = pl.program_id(2)
        causal = _iota((tq, tq), 1) <= _iota((tq, tq), 0)

        @pl.when(j == 0)
        def _():
            dq_ref[...] = jnp.zeros_like(dq_ref)

        def step(i, carry, diag):
            st = pl.multiple_of(i * tq, tq)
            out = []
            for h in range(hp):
                dk, dv = carry[h]
                qi = q_ref[0, pl.ds(st, tq), _head(h)]
                do = do_ref[0, pl.ds(st, tq), _head(h)]
                kp = kp_ref[0, :, _head(h)]
                s = _dot(qi, kp, NT)
                if diag:
                    s = jnp.where(causal, s, -1e30)
                p = jnp.exp2(s - jnp.tile(lse_ref[0, pl.ds(st, tq), _head(h)], (1, rep)))
                dv = dv + _dot(p.astype(BF16), do, TN)
                dp = _dot(do, kv_ref[0, :, _head(h)], NT)
                ds = (p * (dp - jnp.tile(dl_ref[0, pl.ds(st, tq), _head(h)], (1, rep)))).astype(BF16)
                dk = dk + _dot(ds, qi, TN)
                dq_ref[0, pl.ds(st, tq), _head(h)] += _dot(ds, kp)
                out.append((dk, dv))
            return tuple(out)

        zero = jnp.zeros((tq, HEAD_PAD), F32)
        carry = step(j, tuple((zero, zero) for _ in range(hp)), True)
        carry = lax.fori_loop(j + 1, nq, lambda i, c: step(i, c, False), carry)
        for h in range(hp):
            dk_ref[0, :, _head(h)] = (carry[h][0] * (1.0 / LOG2E)).astype(BF16)
            dv_ref[0, :, _head(h)] = carry[h][1].astype(BF16)

        @pl.when(j == nq - 1)
        def _():
            dq_ref[...] = dq_ref[...] * ATTN_SCALE

    outs, side_outs = _hosted_call(
        body, name=name, grid=(bsz, HEADS // hp, nq),
        in_specs=[blk, blk, full, full, full, full],
        out_specs=[full, blk, blk],
        out_shape=[jax.ShapeDtypeStruct((bsz, seq, hw), F32)] + [jax.ShapeDtypeStruct((bsz, seq, hw), BF16)] * 2,
        args=(kv3, kp3, q3, do3, lse3, dl3), side=side)
    return outs if side is None else (outs, side_outs)


def _onorm_fwd(o3, yg3, g_pad, *, name, tb=512):
    bsz, seq, hw = o3.shape
    wg = yg3.shape[-1]
    tb = min(tb, seq)

    def body(o_ref, yg_ref, g_ref, y_ref):
        ya = _rms_fwd(o_ref[0].astype(F32), g_ref[...], HEADS * 64).astype(BF16)
        y_ref[0] = jnp.concatenate([ya, yg_ref[0]], axis=1)

    return pl.pallas_call(
        body, name=name, grid=(bsz, seq // tb),
        in_specs=[pl.BlockSpec((1, tb, hw), lambda b, i: (b, i, 0)),
                  pl.BlockSpec((1, tb, wg), lambda b, i: (b, i, 0)),
                  pl.BlockSpec((1, hw), lambda b, i: (0, 0))],
        out_specs=pl.BlockSpec((1, tb, hw + wg), lambda b, i: (b, i, 0)),
        out_shape=jax.ShapeDtypeStruct((bsz, seq, hw + wg), BF16),
        compiler_params=_cparams(),
    )(o3, yg3, g_pad)


def _onorm_bwd(o3, dy3, g_pad, *, name, tb=512):
    bsz, seq, hw = o3.shape
    tb = min(tb, seq)

    def body(o_ref, dy_ref, g_ref, do_ref, dl_ref, dg_ref):
        @pl.when(jnp.logical_and(pl.program_id(0) == 0, pl.program_id(1) == 0))
        def _():
            dg_ref[...] = jnp.zeros_like(dg_ref)

        o = o_ref[0].astype(F32)
        do, dg = _rms_bwd(o, g_ref[...], dy_ref[0].astype(F32), HEADS * 64)
        dg_ref[...] += dg
        do_ref[0] = do.astype(BF16)
        prod = do * o
        parts = []
        for h in range(HEADS):
            sh = jnp.sum(prod[:, HEAD_PAD * h:HEAD_PAD * (h + 1)], axis=1, keepdims=True)
            parts.append(jnp.broadcast_to(sh, (tb, HEAD_PAD)))
        dl_ref[0] = jnp.concatenate(parts, axis=1)

    return pl.pallas_call(
        body, name=name, grid=(bsz, seq // tb),
        in_specs=[pl.BlockSpec((1, tb, hw), lambda b, i: (b, i, 0)),
                  pl.BlockSpec((1, tb, hw), lambda b, i: (b, i, 0)),
                  pl.BlockSpec((1, hw), lambda b, i: (0, 0))],
        out_specs=[pl.BlockSpec((1, tb, hw), lambda b, i: (b, i, 0)),
                   pl.BlockSpec((1, tb, hw), lambda b, i: (b, i, 0)),
                   pl.BlockSpec((1, hw), lambda b, i: (0, 0))],
        out_shape=[jax.ShapeDtypeStruct((bsz, seq, hw), BF16),
                   jax.ShapeDtypeStruct((bsz, seq, hw), F32),
                   jax.ShapeDtypeStruct((1, hw), F32)],
        compiler_params=_cparams(),
    )(o3, dy3, g_pad)


def _resnode_bwd(x3, g, *, name, target3=None, dh3=None, dres3=None, mod_nm=None, rows=None,
                 branch3=None, mod_gate=None, gate_row=None, tb=512, side=None):
    bsz, seq, d = x3.shape
    tb = min(tb, seq)
    final = target3 is not None
    has_branch = branch3 is not None
    row_spec = pl.BlockSpec((1, tb, d), lambda b, i: (b, i, 0))
    vec_spec = pl.BlockSpec((1, d), lambda b, i: (0, 0))
    mod_spec = pl.BlockSpec((1, MOD_ROWS, d), lambda b, i: (b, 0, 0))

    ins, in_specs = [x3, g], [row_spec, vec_spec]
    if final:
        ins += [target3]
        in_specs += [row_spec]
    else:
        ins += [dh3, dres3, mod_nm]
        in_specs += [row_spec, row_spec, mod_spec]
    if has_branch:
        ins += [branch3, mod_gate]
        in_specs += [row_spec, mod_spec]

    out_names = ["dx", "dg"]
    out_specs = [row_spec, vec_spec]
    out_shape = [jax.ShapeDtypeStruct((bsz, seq, d), F32), jax.ShapeDtypeStruct((1, d), F32)]
    if final:
        out_names += ["loss"]
        out_specs += [pl.BlockSpec((1, 128), lambda b, i: (0, 0))]
        out_shape += [jax.ShapeDtypeStruct((1, 128), F32)]
    else:
        out_names += ["dnm"]
        out_specs += [mod_spec]
        out_shape += [jax.ShapeDtypeStruct((bsz, MOD_ROWS, d), F32)]
    if has_branch:
        out_names += ["dbr", "dgate"]
        out_specs += [row_spec, mod_spec]
        out_shape += [jax.ShapeDtypeStruct((bsz, seq, d), BF16), jax.ShapeDtypeStruct((bsz, MOD_ROWS, d), F32)]
    n_in = len(ins)

    def body(*refs):
        r = dict(zip(["x", "g"] + (["t"] if final else ["dh", "dres", "nm"]) + (["br", "gm"] if has_branch else []),
                     refs[:n_in]))
        o = dict(zip(out_names, refs[n_in:]))
        b_first = pl.program_id(1) == 0
        first = jnp.logical_and(pl.program_id(0) == 0, b_first)
        rowid = _iota((MOD_ROWS, d), 0)

        @pl.when(first)
        def _():
            o["dg"][...] = jnp.zeros_like(o["dg"])
            if final:
                o["loss"][...] = jnp.zeros_like(o["loss"])

        @pl.when(b_first)
        def _():
            if not final:
                o["dnm"][...] = jnp.zeros_like(o["dnm"])
            if has_branch:
                o["dgate"][...] = jnp.zeros_like(o["dgate"])

        x = r["x"][0]
        gv = r["g"][...]
        if final:
            e = _rms_fwd(x, gv, d) - r["t"][0]
            sq = jnp.sum(jnp.sum(e * e, axis=1, keepdims=True), axis=0, keepdims=True)
            o["loss"][...] += jnp.broadcast_to(sq * (0.5 / d), (1, 128))
            dx, dg = _rms_bwd(x, gv, e * (1.0 / d), d)
        else:
            m = r["nm"][0]
            dh = r["dh"][0].astype(F32)
            scale = m[rows[1]:rows[1] + 1, :]
            rstd = lax.rsqrt(jnp.sum(x * x, axis=-1, keepdims=True) * (1.0 / d) + EPS)
            xh = x * rstd
            nrm = xh * gv
            dshift = jnp.sum(dh, axis=0, keepdims=True)
            dscale = jnp.sum(dh * nrm, axis=0, keepdims=True)
            o["dnm"][0] += jnp.where(rowid == 0, dshift, jnp.where(rowid == 1, dscale, 0.0))
            dn = dh * (1.0 + scale)
            dg = jnp.sum(dn * xh, axis=0, keepdims=True)
            dxh = dn * gv
            dx = rstd * (dxh - xh * (jnp.sum(dxh * xh, axis=-1, keepdims=True) * (1.0 / d))) + r["dres"][0]
        o["dg"][...] += dg
        o["dx"][0] = dx
        if has_branch:
            gate = r["gm"][0][gate_row:gate_row + 1, :]
            o["dbr"][0] = (gate * dx).astype(BF16)
            dgate = jnp.sum(dx * r["br"][0], axis=0, keepdims=True)
            o["dgate"][0] += jnp.where(rowid == 0, dgate, 0.0)

    outs, side_outs = _hosted_call(
        body, name=name, grid=(bsz, seq // tb),
        in_specs=in_specs, out_specs=out_specs, out_shape=out_shape, args=tuple(ins), side=side)
    res = dict(zip(out_names, outs))
    return res if side is None else (res, side_outs)


def _adamw(w, g, m, v, *, name):
    shape = w.shape
    cols = shape[-1]
    rows = w.size // cols
    tr = _pick_rows(rows, max(8, (256 * 1024) // cols // 8 * 8))

    def body(w_ref, g_ref, m_ref, v_ref, d_ref, nm_ref, nv_ref):
        d_ref[...], nm_ref[...], nv_ref[...] = _adamw_math(w_ref[...], g_ref[...], m_ref[...], v_ref[...])

    spec = pl.BlockSpec((tr, cols), lambda i: (i, 0))
    outs = pl.pallas_call(
        body, name=name, grid=(rows // tr,),
        in_specs=[spec] * 4, out_specs=[spec] * 3,
        out_shape=[jax.ShapeDtypeStruct((rows, cols), F32)] * 3,
        compiler_params=_cparams(),
    )(*[t.reshape(rows, cols) for t in (w, g, m, v)])
    return tuple(o.reshape(shape) for o in outs)


def _adamw_math(w, g, m, v):
    c1 = 1.0 - ADAM_B1 ** ADAM_STEP
    c2 = 1.0 - ADAM_B2 ** ADAM_STEP
    nm = ADAM_B1 * m + (1.0 - ADAM_B1) * g
    nv = ADAM_B2 * v + (1.0 - ADAM_B2) * (g * g)
    delta = -ADAM_LR * ((nm / c1) / (jnp.sqrt(nv / c2) + ADAM_EPS) + ADAM_WD * w)
    return delta, nm, nv


def _adamw_layers(w, m, v, bufs, row_off, *, name, tr=256):
    depth, rows, cols = w.shape
    tr = min(tr, rows)
    assert rows % tr == 0 and row_off % tr == 0

    outs = None
    for l in range(depth):
        def body(w_ref, g_ref, m_ref, v_ref, *rest):
            go_ref, d_ref, nm_ref, nv_ref = rest[-4:]
            g = g_ref[...]
            go_ref[...] = g
            d_ref[...], nm_ref[...], nv_ref[...] = _adamw_math(w_ref[...], g, m_ref[...], v_ref[...])

        layer = pl.BlockSpec((None, tr, cols), lambda i, l=l: (l, i, 0))
        prev = () if outs is None else tuple(outs)
        outs = pl.pallas_call(
            body, name=f"{name}_l{l}", grid=(rows // tr,),
            in_specs=[layer, pl.BlockSpec((tr, cols), lambda i: (row_off // tr + i, 0)), layer, layer]
            + [ANY_SPEC] * len(prev),
            out_specs=[layer] * 4,
            out_shape=[jax.ShapeDtypeStruct(w.shape, F32)] * 4,
            input_output_aliases={4 + k: k for k in range(len(prev))},
            compiler_params=_cparams(),
        )(w, bufs[l], m, v, *prev)
    return tuple(outs)


def _sum_leading(x, *, name, tr=256):
    n, rows, cols = x.shape
    tr = _pick_rows(rows, tr)

    def body(x_ref, o_ref):
        acc = x_ref[0]
        for k in range(1, n):
            acc = acc + x_ref[k]
        o_ref[...] = acc

    return pl.pallas_call(
        body, name=name, grid=(rows // tr,),
        in_specs=[pl.BlockSpec((n, tr, cols), lambda i: (0, i, 0))],
        out_specs=pl.BlockSpec((tr, cols), lambda i: (i, 0)),
        out_shape=jax.ShapeDtypeStruct((rows, cols), F32),
        compiler_params=_cparams(),
    )(x)


def _position():
    return lax.axis_index("x"), lax.axis_index("y"), lax.axis_index("c")


def _allgather8(x, *, name):
    shape = x.shape

    def body(x_ref, out_ref, send_sems, recv_sems, local_sem):
        px, py, pc = _position()
        me, sibling = (px, py, pc), (px, py, 1 - pc)
        chips = [(1 - px, py), (px, 1 - py), (1 - px, 1 - py)]
        src_own = x_ref

        def slot(qx, qy, qc):
            return out_ref.at[4 * qx + 2 * qy + qc]

        def copy(k, block, to, src=None):
            return pltpu.make_async_remote_copy(
                src_ref=slot(*block) if src is None else src, dst_ref=slot(*block),
                send_sem=send_sems.at[k], recv_sem=recv_sems.at[k], device_id=to, device_id_type=MESH)

        mine = pltpu.make_async_copy(src_own, slot(*me), local_sem)
        mine.start()
        first = [copy(0, me, sibling, src=src_own)]
        first += [copy(1 + j, me, (*chip, pc), src=src_own) for j, chip in enumerate(chips)]
        for cp in first:
            cp.start()
        passed = [copy(4 + j, (*chip, pc), sibling) for j, chip in enumerate(chips)]
        for j, chip in enumerate(chips):
            copy(1 + j, (*chip, pc), me).wait_recv()
            passed[j].start()
        copy(0, sibling, me).wait_recv()
        for j, chip in enumerate(chips):
            copy(4 + j, (*chip, 1 - pc), me).wait_recv()
        for cp in first + passed:
            cp.wait_send()
        mine.wait()

    return pl.pallas_call(
        body, name=name,
        out_shape=jax.ShapeDtypeStruct((N_DEV,) + shape, x.dtype),
        in_specs=[pl.BlockSpec(memory_space=pl.ANY)],
        out_specs=pl.BlockSpec(memory_space=pl.ANY),
        scratch_shapes=[pltpu.SemaphoreType.DMA((7,)), pltpu.SemaphoreType.DMA((7,)), pltpu.SemaphoreType.DMA],
    )(x)


class _Exchange:
    def __init__(self, ins, out_shapes, n, build, aliases=None):
        self.ins, self.out_shapes, self.n, self.build = tuple(ins), tuple(out_shapes), n, build
        self.aliases = dict(aliases or {})

    def _descriptors(self, in_refs, out_refs, send_sems, recv_sems):
        sends, recvs = [], []
        for k, (src, dst, peer, landing) in enumerate(self.build(in_refs, out_refs)):
            sends.append(pltpu.make_async_remote_copy(
                src_ref=src, dst_ref=dst, send_sem=send_sems.at[k], recv_sem=recv_sems.at[k],
                device_id=peer, device_id_type=MESH))
            recvs.append(pltpu.make_async_remote_copy(
                src_ref=src, dst_ref=landing, send_sem=send_sems.at[k], recv_sem=recv_sems.at[k],
                device_id=peer, device_id_type=MESH))
        return sends, recvs

    def start(self, *refs):
        for cp in self._descriptors(*refs)[0]:
            cp.start()

    def finish(self, *refs):
        sends, recvs = self._descriptors(*refs)
        for cp in recvs:
            cp.wait_recv()
        for cp in sends:
            cp.wait_send()


ANY_SPEC = pl.BlockSpec(memory_space=pl.ANY)


def _hosted_call(body, *, name, grid, in_specs, out_specs, out_shape, args, scratch_shapes=(), side=None,
                 num_scalar_prefetch=0, io_aliases=None):
    in_specs, out_specs, out_shape = list(in_specs), list(out_specs), list(out_shape)
    n_in, n_out = len(in_specs) + num_scalar_prefetch, len(out_specs)
    kernel_body = body
    aliases = dict(io_aliases or {})
    if side is not None:
        s_in, s_out = len(side.ins), len(side.out_shapes)
        aliases.update({n_in + i: n_out + o for i, o in side.aliases.items()})

        def kernel_body(*refs):
            ins, s_ins = refs[:n_in], refs[n_in:n_in + s_in]
            outs = refs[n_in + s_in:n_in + s_in + n_out]
            s_outs = refs[n_in + s_in + n_out:n_in + s_in + n_out + s_out]
            scratch, sems = refs[n_in + s_in + n_out + s_out:-2], refs[-2:]
            first = functools.reduce(jnp.logical_and, [pl.program_id(a) == 0 for a in range(len(grid))])
            last = functools.reduce(jnp.logical_and, [pl.program_id(a) == g - 1 for a, g in enumerate(grid)])

            @pl.when(first)
            def _():
                side.start(s_ins, s_outs, *sems)

            body(*ins, *outs, *scratch)

            @pl.when(last)
            def _():
                side.finish(s_ins, s_outs, *sems)

        in_specs += [ANY_SPEC] * s_in
        out_specs += [ANY_SPEC] * s_out
        out_shape += list(side.out_shapes)
        scratch_shapes = list(scratch_shapes) + [pltpu.SemaphoreType.DMA((side.n,)),
                                                 pltpu.SemaphoreType.DMA((side.n,))]
        args = tuple(args) + side.ins
    if num_scalar_prefetch:
        grid_spec = pltpu.PrefetchScalarGridSpec(num_scalar_prefetch=num_scalar_prefetch, grid=grid,
                                                 in_specs=in_specs, out_specs=out_specs,
                                                 scratch_shapes=list(scratch_shapes))
        outs = pl.pallas_call(kernel_body, name=name, grid_spec=grid_spec, out_shape=out_shape,
                              input_output_aliases=aliases, compiler_params=_cparams())(*args)
    else:
        outs = pl.pallas_call(kernel_body, name=name, grid=grid, in_specs=in_specs, out_specs=out_specs,
                              out_shape=out_shape, scratch_shapes=list(scratch_shapes),
                              input_output_aliases=aliases, compiler_params=_cparams())(*args)
    return tuple(outs[:n_out]), tuple(outs[n_out:])


def _run_exchange(ex, *, name):
    s_in = len(ex.ins)

    def body(*refs):
        ins, outs, sems = refs[:s_in], refs[s_in:-2], refs[-2:]
        ex.start(ins, outs, *sems)
        ex.finish(ins, outs, *sems)

    outs = pl.pallas_call(
        body, name=name, out_shape=list(ex.out_shapes),
        in_specs=[ANY_SPEC] * s_in, out_specs=[ANY_SPEC] * len(ex.out_shapes),
        scratch_shapes=[pltpu.SemaphoreType.DMA((ex.n,)), pltpu.SemaphoreType.DMA((ex.n,))],
        input_output_aliases=ex.aliases,
    )(*ex.ins)
    return tuple(outs)


def _other_chips(px, py):
    return [(px, 1 - py), (1 - px, py), (1 - px, 1 - py)]


def _gather_spread(w_flat, halves=True):
    rows, w = w_flat.shape
    hr = rows // 2 if halves else rows

    def build(ins, outs):
        px, py, pc = _position()
        mine = ins[0].at[pl.ds(pc * hr, hr)] if halves else ins[0]
        me = 4 * px + 2 * py + pc
        plan = [((px, py, 1 - pc), me ^ 1)]
        plan += [((qx, qy, pc), 4 * qx + 2 * qy + pc) for qx, qy in _other_chips(px, py)]
        return [(mine, outs[0].at[me], peer, outs[0].at[their]) for peer, their in plan]

    return _Exchange([w_flat], [jax.ShapeDtypeStruct((N_DEV, hr, w), w_flat.dtype)], 4, build)


def _gather_pass_on(gath):
    def build(ins, outs):
        px, py, pc = _position()
        out = []
        for qx, qy in _other_chips(px, py):
            blk = 4 * qx + 2 * qy + pc
            out.append((outs[0].at[blk], outs[0].at[blk], (px, py, 1 - pc), outs[0].at[blk ^ 1]))
        return out

    return _Exchange([gath], [jax.ShapeDtypeStruct(gath.shape, gath.dtype)], 3, build, aliases={0: 0})


def _rs_halves(g):
    n, rows, w = g.shape
    hr = rows // 2

    def build(ins, outs):
        px, py, pc = _position()
        return [(ins[0].at[:, pl.ds((1 - pc) * hr, hr), :], outs[0], (px, py, 1 - pc), outs[0])]

    return _Exchange([g], [jax.ShapeDtypeStruct((n, hr, w), g.dtype)], 1, build)


def _rs_chips(sb):
    def build(ins, outs):
        px, py, pc = _position()
        return [(ins[0].at[j], outs[0].at[j], (qx, qy, pc), outs[0].at[j])
                for j, (qx, qy) in enumerate(_other_chips(px, py))]

    return _Exchange([sb], [jax.ShapeDtypeStruct(sb.shape, sb.dtype)], 3, build)


def _rs_complete(buf):
    def build(ins, outs):
        px, py, pc = _position()
        return [(outs[0].at[pc], outs[0].at[pc], (px, py, 1 - pc), outs[0].at[1 - pc])]

    return _Exchange([buf], [jax.ShapeDtypeStruct(buf.shape, buf.dtype)], 1, build, aliases={0: 0})


def _rs_partial(g, recv, ids, *, name, tr=128):
    _, rows, w = g.shape
    hr = rows // 2
    nb = hr // tr

    def body(ids_ref, g_ref, r_ref, o_ref):
        o_ref[0] = (g_ref[0] + r_ref[0]).astype(BF16)

    grid_spec = pltpu.PrefetchScalarGridSpec(
        num_scalar_prefetch=1, grid=(3, nb),
        in_specs=[pl.BlockSpec((1, tr, w), lambda j, i, ids: (ids[1] ^ (j + 1), ids[0] * nb + i, 0)),
                  pl.BlockSpec((1, tr, w), lambda j, i, ids: (ids[1] ^ (j + 1), i, 0))],
        out_specs=pl.BlockSpec((1, tr, w), lambda j, i, ids: (j, i, 0)))
    return pl.pallas_call(
        body, name=name, grid_spec=grid_spec,
        out_shape=jax.ShapeDtypeStruct((3, hr, w), BF16),
        compiler_params=_cparams(),
    )(ids, g, recv)


def _rs_total(g, recv, got, ids, *, name, tr=128):
    _, rows, w = g.shape
    hr = rows // 2
    nb = hr // tr

    def body(ids_ref, g_ref, r_ref, got_ref, o_ref):
        acc = g_ref[0] + r_ref[0]
        for j in range(3):
            acc = acc + got_ref[j].astype(F32)
        o_ref[0] = acc

    grid_spec = pltpu.PrefetchScalarGridSpec(
        num_scalar_prefetch=1, grid=(nb,),
        in_specs=[pl.BlockSpec((1, tr, w), lambda i, ids: (ids[1], ids[0] * nb + i, 0)),
                  pl.BlockSpec((1, tr, w), lambda i, ids: (ids[1], i, 0)),
                  pl.BlockSpec((3, tr, w), lambda i, ids: (0, i, 0))],
        out_specs=pl.BlockSpec((1, tr, w), lambda i, ids: (ids[0], i, 0)))
    return pl.pallas_call(
        body, name=name, grid_spec=grid_spec,
        out_shape=jax.ShapeDtypeStruct((2, hr, w), F32),
        compiler_params=_cparams(),
    )(ids, g, recv, got)


class _ReduceScatter:
    def __init__(self, g, ids, tag):
        self.g, self.ids, self.tag, self.stage, self.result = g, ids, tag, 0, None

    def next_exchange(self):
        if self.stage == 0:
            return _rs_halves(self.g)
        if self.stage == 1:
            return _rs_chips(self.sb)
        return _rs_complete(self.buf)

    def done(self, outs):
        if self.stage == 0:
            self.recv = outs[0]
            hr = self.recv.shape[1]
            self.tr = max(t for t in range(16, 513, 16) if hr % t == 0)
            self.sb = _rs_partial(self.g, self.recv, self.ids, name=f"{self.tag}_partial", tr=self.tr)
        elif self.stage == 1:
            self.buf = _rs_total(self.g, self.recv, outs[0], self.ids, name=f"{self.tag}_total", tr=self.tr)
        else:
            _, hr, w = outs[0].shape
            self.result = outs[0].reshape(2 * hr, w)
        self.stage += 1

    def finish_alone(self):
        names = ("halves", "chips", "complete")
        while self.stage < 3:
            self.done(_run_exchange(self.next_exchange(), name=f"{self.tag}_{names[self.stage]}"))
        return self.result


def _flat_rows():
    used = sum(r for _, r in FSDP_SECTIONS)
    return used, -(-used // ROW_ALIGN) * ROW_ALIGN


def _cols_to_chunks(full):
    rows, cols = full.shape
    t = full.reshape(rows, N_CHIPS, cols // N_CHIPS).transpose(1, 0, 2)
    return t.reshape(N_CHIPS, -1, FLAT_W)


def _chunks_to_cols(chunks, rows, cols):
    return chunks.reshape(N_CHIPS, rows, cols // N_CHIPS).transpose(1, 0, 2).reshape(rows, cols)


def _pad_heads(w, real):
    lead = w.shape[:-1]
    t = w.reshape(lead + (HEADS, real))
    t = jnp.pad(t, [(0, 0)] * len(lead) + [(0, 0), (0, HEAD_PAD - real)])
    return t.reshape(lead + (HEADS * HEAD_PAD,))


def _unpad_heads(w, real):
    lead = w.shape[:-1]
    return w.reshape(lead + (HEADS, HEAD_PAD))[..., :real].reshape(lead + (HEADS * real,))


def _pad_value_lanes(w, axis):
    w = jnp.moveaxis(w, axis, -1)
    lead = w.shape[:-1]
    t = w.reshape(lead + (HEADS, 64))
    t = jnp.pad(t, [(0, 0)] * len(lead) + [(0, 0), (HEAD_PAD - 64, 0)])
    return jnp.moveaxis(t.reshape(lead + (HEADS * HEAD_PAD,)), -1, axis)


def _unpad_value_lanes(w, axis):
    w = jnp.moveaxis(w, axis, -1)
    lead = w.shape[:-1]
    t = w.reshape(lead + (HEADS, HEAD_PAD))[..., HEAD_PAD - 64:]
    return jnp.moveaxis(t.reshape(lead + (HEADS * 64,)), -1, axis)


def _pad_w_in_t(wt):
    z = jnp.zeros((NOPE, wt.shape[1]), wt.dtype)
    z2 = jnp.zeros((HEAD_PAD - NOPE - ROPE, wt.shape[1]), wt.dtype)
    return jnp.concatenate([wt[:1408], z, wt[1408:], z2], axis=0)


def _unpad_w_in_t(wt):
    return jnp.concatenate([wt[:1408], wt[1408 + NOPE:1408 + NOPE + ROPE]], axis=0)


def _rope_tables(positions):
    freqs = ROPE_THETA ** (-jnp.arange(0, ROPE, 2, dtype=F32) / ROPE)
    ang = positions.astype(F32)[..., None] * freqs
    cos, sin = jnp.cos(ang), jnp.sin(ang)
    lead = cos.shape[:-1]
    ones = jnp.ones(lead + (NOPE,), F32)
    zeros_n = jnp.zeros(lead + (NOPE,), F32)
    zeros_p = jnp.zeros(lead + (HEAD_PAD - NOPE - ROPE,), F32)
    ctab = jnp.concatenate([ones, cos, cos, zeros_p], axis=-1)
    stab = jnp.concatenate([zeros_n, -sin, sin, zeros_p], axis=-1)
    return ctab, stab


def _mix_weights(full):
    return dict(
        w_in_t=_pad_w_in_t(full["w_in_t"]),
        w_uq=_pad_heads(full["mla_w_uq"], NOPE + ROPE),
        w_ukv=full["mla_w_ukv"],
        w_out=jnp.concatenate([_pad_value_lanes(full["w_out"][D_GMLP:], 0), full["w_out"][:D_GMLP]], axis=0),
    )


def _small_weights(p, l):
    ws = p["gmlp_ws"][l]
    tril = jnp.tril(jnp.ones((CHUNK, CHUNK), bool))
    bs = p["gmlp_bs"][l]
    bexp = jnp.repeat(bs.reshape(GROUPS // 2, 2, CHUNK).transpose(0, 2, 1), GROUP_DIM, axis=2)
    return dict(
        ws=ws,
        wst=jnp.where(tril[None], ws, 0.0).transpose(0, 2, 1).astype(BF16),
        bexp=bexp,
        g_mix=p["norm_mix_g"][l][None],
        g_ffn=p["norm_ffn_g"][l][None],
        g_q=p["mla_q_norm_g"][l][None],
        g_kv=p["mla_kv_norm_g"][l][None],
        g_og=p["out_norm_gmlp_g"][l][None],
        g_oa=_pad_value_lanes(p["out_norm_mla_g"][l], 0)[None],
    )


def _local_step(x3, target3, positions, mods, final_g, plan):
    bsz, seq, d = x3.shape
    tok = bsz * seq
    tmt = min(512, seq)
    tmk = min(1024, seq)
    tmw = min(2048, tok)
    chunk = (None, None, FLAT_W, FLAT_W)
    ff_grad_shape = (N_CHIPS, 2 * FLAT_W, FLAT_W)
    ctab, stab = _rope_tables(positions)
    lw = [None] * DEPTH

    def flat(t):
        return t.reshape(tok, t.shape[-1])

    def cube(t):
        return t.reshape(bsz, seq, t.shape[-1])

    def carrying(l, tag, fn, *args, **kw):
        side = plan.host(l, tag)
        if side is None:
            return fn(*args, **kw)
        res, side_outs = fn(*args, side=side, **kw)
        plan.hosted(l, tag, side_outs)
        return res

    saved = []
    x = x3
    for l in range(DEPTH):
        lw[l] = plan.layer(l)
        w, mod = lw[l], mods[l]
        h1 = carrying(l, "fwd_normmod1", _normmod_fwd, x, w["g_mix"], mod, SHIFT1, SCALE1, name=f"l{l}_normmod1")
        z = cube(_mm(flat(h1), w["w_in_t"], dims="nt", name=f"l{l}_w_in", tm=tmt, tn=D_IN_PAD, tk=d,
                     out_dtypes=(BF16,)))
        yg = _gmlp_fwd(z, w["ws"], w["bexp"], w["g_og"], name=f"l{l}_gmlp_fwd")
        q, kv, kp = _mla_prep_fwd(z, w["g_q"], w["g_kv"], w["w_uq"], w["w_ukv"], ctab, stab, name=f"l{l}_mla_prep")
        o, lse = carrying(l, "fwd_attn", _attn_fwd, q, kv, kp, name=f"l{l}_attn_fwd")
        y = _onorm_fwd(o, yg, w["g_oa"], name=f"l{l}_onorm_fwd")

        def out_epi(po, xv, gm):
            return po, xv + gm[0][GATE1:GATE1 + 1, :] * po

        po, x_mid = carrying(l, "fwd_out_a", _mm, flat(y), w["w_out"], dims="nn", name=f"l{l}_w_out",
                             tm=tmt, tn=d, tk=y.shape[-1], out_dtypes=(BF16, F32), epilogue=out_epi,
                             extras=(flat(x), mod), extra_specs=(None, _mod_spec(tmt, d, seq)))
        x_mid = cube(x_mid)
        h2 = _normmod_fwd(x_mid, w["g_ffn"], mod, SHIFT2, SCALE2, name=f"l{l}_normmod2")

        def act_epi(acc):
            r = jnp.maximum(acc, 0.0)
            return (r * r,)

        r = carrying(l, "fwd_ff1", _mm, flat(h2), w["ff"], dims="nn", name=f"l{l}_w_ff1", tm=tmw, tn=FLAT_W,
                     tk=d, out_dtypes=(BF16,), epilogue=act_epi, weights_outer=True, n=D_FF,
                     b_block=(chunk, lambda i, j, k: (j, 0, 0, 0)))

        def ff2_epi(acc, xv, gm):
            return acc, xv + gm[0][GATE2:GATE2 + 1, :] * acc

        f, x_out = carrying(l, "fwd_ff2", _mm, r, w["ff"], dims="nn", name=f"l{l}_w_ff2", tm=tmk, tn=d, tk=FLAT_W,
                            out_dtypes=(BF16, F32), epilogue=ff2_epi, extras=(flat(x_mid), mod),
                            extra_specs=(None, _mod_spec(tmk, d, seq)), n=d,
                            b_block=(chunk, lambda i, j, k: (k, 1, 0, 0)))
        saved.append(dict(x_in=x, h1=h1, z=z, q=q, kv=kv, kp=kp, o=o, lse=lse, y=y, po=cube(po),
                          x_mid=x_mid, h2=h2, r=r, f=cube(f)))
        x = cube(x_out)

    grads = [dict() for _ in range(DEPTH)]
    dmods = [None] * DEPTH
    top = DEPTH - 1
    node = _resnode_bwd(x, final_g[None], name="final_loss_bwd", target3=target3,
                        branch3=saved[top]["f"], mod_gate=mods[top], gate_row=GATE2)
    loss_part = node["loss"][0, 0]
    d_final_g = node["dg"][0]
    plan.scalars(loss_part, d_final_g)
    for l in range(DEPTH - 1, -1, -1):
        w, mod, s = lw[l], mods[l], saved[l]
        dx_out, dfb, dgate2 = node["dx"], flat(node["dbr"]), node["dgate"][:, 0]

        def dact_epi(acc, rv):
            return (acc * (2.0 * jnp.sqrt(rv.astype(F32))),)

        da = carrying(l, "bwd_d_r", _mm, dfb, w["ff"], dims="nt", name=f"l{l}_d_r", tm=tmw, tn=FLAT_W, tk=d,
                      out_dtypes=(BF16,), epilogue=dact_epi, extras=(s["r"],), weights_outer=True, n=D_FF,
                      b_block=(chunk, lambda i, j, k: (j, 1, 0, 0)))
        g_ff = carrying(l, "bwd_dw_ff2", _mm, s["r"], dfb, dims="tn", name=f"l{l}_dw_ff2", tm=FLAT_W, tn=d,
                        tk=1024, out_into=(ff_grad_shape, (None, FLAT_W, FLAT_W), lambda i, j, k: (i, 1, 0), None))
        g_ff = carrying(l, "bwd_dw_ff1", _mm, flat(s["h2"]), da, dims="tn", name=f"l{l}_dw_ff1", tm=d, tn=FLAT_W,
                        tk=1024, out_into=(ff_grad_shape, (None, FLAT_W, FLAT_W), lambda i, j, k: (j, 0, 0), g_ff))
        plan.ff_grads(l, g_ff)
        dh2 = carrying(l, "bwd_d_h2", _mm, da, w["ff"], dims="nt", name=f"l{l}_d_h2", tm=tmk, tn=d, tk=FLAT_W,
                       n=d, b_block=(chunk, lambda i, j, k: (k, 0, 0, 0)), out_dtypes=(BF16,))
        node = _resnode_bwd(s["x_mid"], w["g_ffn"], name=f"l{l}_resnode_ffn", dh3=cube(dh2), dres3=dx_out,
                            mod_nm=mod, rows=(SHIFT2, SCALE2), branch3=s["po"], mod_gate=mod, gate_row=GATE1)
        grads[l]["norm_ffn_g"] = node["dg"][0]
        dshift2, dscale2 = node["dnm"][:, 0], node["dnm"][:, 1]
        dx_mid, dpo, dgate1 = node["dx"], flat(node["dbr"]), node["dgate"][:, 0]

        wy = s["y"].shape[-1]
        dy = cube(carrying(l, "bwd_d_y", _mm, dpo, w["w_out"], dims="nt", name=f"l{l}_d_y", tm=tmt, tn=wy, tk=d,
                           out_dtypes=(BF16,)))
        dw_out = carrying(l, "bwd_dw_out", _mm, flat(s["y"]), dpo, dims="tn", name=f"l{l}_dw_out", tm=wy // 3,
                          tn=d, tk=1024)
        hw = HEADS * HEAD_PAD
        grads[l]["w_out"] = jnp.concatenate([dw_out[hw:], _unpad_value_lanes(dw_out[:hw], 0)], axis=0)

        dz, dws, dbs, dg_og = _gmlp_bwd(s["z"], dy, w["ws"], w["wst"], w["bexp"], w["g_og"],
                                        name=f"l{l}_gmlp_bwd", dy_col=hw // D_GMLP)
        grads[l]["gmlp_ws"], grads[l]["gmlp_bs"], grads[l]["out_norm_gmlp_g"] = dws, dbs, dg_og[0]

        do, dl, dg_oa = _onorm_bwd(s["o"], dy, w["g_oa"], name=f"l{l}_onorm_bwd")
        grads[l]["out_norm_mla_g"] = _unpad_value_lanes(dg_oa[0], 0)
        dq, dk, dv = carrying(l, "bwd_attn_dkv", _attn_bwd, s["q"], s["kv"], s["kp"], do, s["lse"], dl,
                              name=f"l{l}_attn_bwd")
        dz, cq, dqb, ckv, dkvb, dg_q, dg_kv = _mla_prep_bwd(
            s["z"], dz, dq, dk, dv, w["g_q"], w["g_kv"], w["w_uq"], w["w_ukv"], ctab, stab,
            name=f"l{l}_mla_prep_bwd")
        grads[l]["mla_q_norm_g"], grads[l]["mla_kv_norm_g"] = dg_q[0], dg_kv[0]
        dw_uq = carrying(l, "bwd_dw_uq", _mm, flat(cq), flat(dqb), dims="tn", name=f"l{l}_dw_uq", tm=Q_RANK,
                         tn=1024, tk=1024)
        grads[l]["mla_w_uq"] = _unpad_heads(dw_uq, NOPE + ROPE)
        grads[l]["mla_w_ukv"] = _mm(flat(ckv), flat(dkvb), dims="tn", name=f"l{l}_dw_ukv", tm=KV_RANK, tn=1024, tk=1024)

        grads[l]["w_in_t"] = _unpad_w_in_t(_mm(flat(dz), flat(s["h1"]), dims="tn", name=f"l{l}_dw_in",
                                               tm=D_IN_PAD // 2, tn=d, tk=1024))
        plan.layer_grads(l, grads[l])
        dh1 = carrying(l, "bwd_d_h1", _mm, flat(dz), w["w_in_t"], dims="nn", name=f"l{l}_d_h1", tm=tmt, tn=d,
                       tk=D_IN_PAD, out_dtypes=(BF16,))
        below = dict(branch3=saved[l - 1]["f"], mod_gate=mods[l - 1], gate_row=GATE2) if l > 0 else {}
        node = carrying(l, "bwd_resnode_mix", _resnode_bwd, s["x_in"], w["g_mix"], name=f"l{l}_resnode_mix",
                        dh3=cube(dh1), dres3=dx_mid, mod_nm=mod, rows=(SHIFT1, SCALE1), **below)
        grads[l]["norm_mix_g"] = node["dg"][0]
        dshift1, dscale1 = node["dnm"][:, 0], node["dnm"][:, 1]
        dmods[l] = jnp.stack([dshift1, dscale1, dgate1, dshift2, dscale2, dgate2], axis=1)
        plan.layer_done(l)
    return node["dx"], dmods


W_NAMES = ("w_ada", "b_ada", "norm_mix_g", "w_in", "gmlp_ws", "gmlp_bs", "mla_q_norm_g", "mla_kv_norm_g",
           "mla_w_uq", "mla_w_ukv", "out_norm_gmlp_g", "out_norm_mla_g", "w_out", "norm_ffn_g", "w_ff1", "w_ff2",
           "final_norm_g")
FLAT_KEY = {"w_in": "w_in", "w_uq": "mla_w_uq", "w_ukv": "mla_w_ukv", "w_out": "w_out", "w_ff1": "w_ff1",
            "w_ff2": "w_ff2"}
COL_SHARDED = ("w_in", "w_uq", "w_ukv", "w_ff1")
FULL_SHAPE = {"w_in": (D_MODEL, D_IN), "w_uq": (Q_RANK, HEADS * (NOPE + ROPE)), "w_ukv": (KV_RANK, HEADS * 128),
              "w_out": (D_MODEL, D_MODEL), "w_ff1": (D_MODEL, D_FF), "w_ff2": (D_FF, D_MODEL)}
SMALL_LAYER_NAMES = ("norm_mix_g", "gmlp_ws", "gmlp_bs", "mla_q_norm_g", "mla_kv_norm_g", "out_norm_gmlp_g",
                     "out_norm_mla_g", "norm_ffn_g")


def _silu(v):
    return v * (1.0 / (1.0 + jnp.exp(-v)))


class _CommPlan:
    FWD = {"fwd_attn": ("ff", 0, "spread"), "fwd_out_a": ("ff", 0, "pass"),
           "fwd_ff1": ("mix", 1, "spread"), "fwd_ff2": ("mix", 1, "pass")}
    BWD = {"bwd_d_r": ("mix", 1), "bwd_dw_ff2": ("mix", 1), "bwd_dw_ff1": ("mix", 1),
           "bwd_d_h2": ("ff", 0), "bwd_attn_dkv": ("ff", 0), "bwd_dw_uq": ("ff", 0)}
    BWD_LAST = {"bwd_d_h1": ("mix", 0), "bwd_resnode_mix": ("mix", 0)}
    SMALL = {"bwd_d_y": "spread", "bwd_dw_out": "pass"}

    def __init__(self, weights, ids, dev, core):
        self.weights, self.ids, self.dev, self.core = weights, ids, dev, core
        self.used, self.rows = _flat_rows()
        self.flat = {("mix", l): self._flat_mix(l) for l in range(DEPTH)}
        self.flat.update({("ff", l): jnp.concatenate([weights["w_ff1"][l], weights["w_ff2"][l]], axis=0).astype(BF16)
                          for l in range(DEPTH)})
        self.lw, self.rs, self.grads, self.spread = {}, {}, {}, {}
        self.small_vec, self.small_sum, self.small_spread, self.extra = {}, {}, None, {}
        self.lw = {l: _small_weights(weights, l) for l in range(DEPTH)}

    def _flat_mix(self, l):
        pieces = []
        for nm, _ in FSDP_SECTIONS:
            shard = self.weights[FLAT_KEY[nm]][l]
            pieces.append(shard.T if nm == "w_in" else shard.reshape(-1, FLAT_W))
        pieces.append(jnp.zeros((self.rows - self.used, FLAT_W), F32))
        return jnp.concatenate(pieces, axis=0).astype(BF16)

    def _arrived(self, group, l, gath):
        flat = self.flat[group, l]
        hr = flat.shape[0] // 2
        mine = lax.dynamic_slice(flat, (self.core * hr, 0), (hr, FLAT_W))
        gath = lax.dynamic_update_slice(gath, mine[None], (self.dev, 0, 0))
        if group == "ff":
            self.lw[l]["ff"] = gath.reshape(N_CHIPS, 2, hr, FLAT_W)
            return
        w_gath = gath.reshape(N_CHIPS, self.rows, FLAT_W)
        full, off = {}, 0
        for nm, nrows in FSDP_SECTIONS:
            sec = w_gath[:, off:off + nrows]
            off += nrows
            rows, cols = FULL_SHAPE[nm]
            if nm == "w_in":
                full["w_in_t"] = sec.reshape(cols, rows)
            else:
                full[FLAT_KEY[nm]] = (_chunks_to_cols(sec, rows, cols) if nm in COL_SHARDED
                                      else sec.reshape(rows, cols))
        self.lw[l].update(_mix_weights(full))

    def layer(self, l):
        return self.lw[l]

    def host(self, l, tag):
        if tag == "fwd_normmod1":
            return _gather_spread(self.flat["mix", 0]) if l == 0 else None
        if tag in self.FWD:
            group, ahead, what = self.FWD[tag]
            if l + ahead >= DEPTH:
                return None
            return _gather_spread(self.flat[group, l + ahead]) if what == "spread" else _gather_pass_on(self.spread[group])
        if tag in self.SMALL:
            if l + 1 not in self.small_vec:
                return None
            if self.SMALL[tag] == "spread":
                return _gather_spread(self.small_vec[l + 1], halves=False)
            return _gather_pass_on(self.small_spread)
        rs = self._rs_for(l, tag)
        return None if rs is None or rs.stage > 2 else rs.next_exchange()

    def _rs_for(self, l, tag):
        if tag in self.BWD_LAST:
            return self.rs.get(self.BWD_LAST[tag]) if l == 0 else None
        group, ahead = self.BWD[tag]
        return self.rs.get((group, l + ahead))

    def hosted(self, l, tag, outs):
        if tag == "fwd_normmod1":
            self._arrived("mix", 0, _run_exchange(_gather_pass_on(outs[0]), name="l0_mix_gather_pass_on")[0])
        elif tag in self.FWD:
            group, ahead, what = self.FWD[tag]
            if what == "spread":
                self.spread[group] = outs[0]
            else:
                self._arrived(group, l + ahead, outs[0])
        elif tag in self.SMALL:
            if self.SMALL[tag] == "spread":
                self.small_spread = outs[0]
            else:
                self._small_arrived(l + 1, outs[0])
        else:
            self._rs_for(l, tag).done(outs)

    def ff_grads(self, l, g_ff):
        self.rs["ff", l] = _ReduceScatter(g_ff, self.ids, f"l{l}_ff_rs")

    def layer_grads(self, l, grads):
        self.grads[l] = grads
        pieces = []
        for nm, nrows in FSDP_SECTIONS:
            if nm == "w_in":
                pieces.append(grads["w_in_t"].reshape(N_CHIPS, nrows, FLAT_W))
                continue
            g = grads[FLAT_KEY[nm]]
            pieces.append(_cols_to_chunks(g) if nm in COL_SHARDED else g.reshape(N_CHIPS, nrows, FLAT_W))
        pieces.append(jnp.zeros((N_CHIPS, self.rows - self.used, FLAT_W), F32))
        self.rs["mix", l] = _ReduceScatter(jnp.concatenate(pieces, axis=1), self.ids, f"l{l}_mix_rs")

    def scalars(self, loss_part, d_final_g):
        self.extra = {0: [loss_part[None]]}
        self.extra.setdefault(DEPTH - 1, []).insert(0, d_final_g)

    def layer_done(self, l):
        if l == 0:
            self.rs["mix", 0].finish_alone()
        parts = [self.grads[l][nm].reshape(-1) for nm in SMALL_LAYER_NAMES] + self.extra.get(l, [])
        vec = jnp.concatenate(parts)
        rows = -(-vec.shape[0] // (8 * FLAT_W)) * 8
        self.small_vec[l] = jnp.pad(vec, (0, rows * FLAT_W - vec.shape[0])).reshape(rows, FLAT_W)
        if l == 0:
            (gath,) = _run_exchange(_gather_spread(self.small_vec[0], halves=False), name="l0_small_spread")
            self._small_arrived(0, _run_exchange(_gather_pass_on(gath), name="l0_small_pass_on")[0])

    def _small_arrived(self, l, gath):
        gath = lax.dynamic_update_slice(gath, self.small_vec[l][None], (self.dev, 0, 0))
        self.small_sum[l] = _sum_leading(gath, name=f"l{l}_small_sum").reshape(-1)

    def small_grads(self):
        out = {nm: [] for nm in SMALL_LAYER_NAMES}
        for l in range(DEPTH):
            off = 0
            for nm in SMALL_LAYER_NAMES:
                size = self.weights[nm][l].size
                out[nm].append(self.small_sum[l][off:off + size].reshape(self.weights[nm].shape[1:]))
                off += size
            if l == DEPTH - 1:
                final = self.small_sum[l][off:off + self.weights["final_norm_g"].size]
                off += final.shape[0]
            if l == 0:
                loss = self.small_sum[l][off]
        res = {nm: jnp.stack(parts, axis=0) for nm, parts in out.items()}
        res["final_norm_g"] = final
        return loss, res

    def mix_grads(self):
        per = {FLAT_KEY[nm]: [] for nm, _ in FSDP_SECTIONS}
        for l in range(DEPTH):
            shard, off = self.rs["mix", l].result, 0
            for nm, nrows in FSDP_SECTIONS:
                key = FLAT_KEY[nm]
                sec = shard[off:off + nrows]
                per[key].append(sec.T if nm == "w_in" else sec.reshape(self.weights[key].shape[1:]))
                off += nrows
        return {key: jnp.stack(parts, axis=0) for key, parts in per.items()}

    def ff_shards(self):
        return [self.rs["ff", l].result for l in range(DEPTH)]


def kernel(x, c, positions, w_ada, b_ada, norm_mix_g, w_in, gmlp_ws, gmlp_bs, mla_q_norm_g, mla_kv_norm_g, mla_w_uq, mla_w_ukv, out_norm_gmlp_g, out_norm_mla_g, w_out, norm_ffn_g, w_ff1, w_ff2, final_norm_g, loss_target, m_w_ada, m_b_ada, m_norm_mix_g, m_w_in, m_gmlp_ws, m_gmlp_bs, m_mla_q_norm_g, m_mla_kv_norm_g, m_mla_w_uq, m_mla_w_ukv, m_out_norm_gmlp_g, m_out_norm_mla_g, m_w_out, m_norm_ffn_g, m_w_ff1, m_w_ff2, m_final_norm_g, v_w_ada, v_b_ada, v_norm_mix_g, v_w_in, v_gmlp_ws, v_gmlp_bs, v_mla_q_norm_g, v_mla_kv_norm_g, v_mla_w_uq, v_mla_w_ukv, v_out_norm_gmlp_g, v_out_norm_mla_g, v_w_out, v_norm_ffn_g, v_w_ff1, v_w_ff2, v_final_norm_g):
    weights = dict(w_ada=w_ada, b_ada=b_ada, norm_mix_g=norm_mix_g, w_in=w_in, gmlp_ws=gmlp_ws, gmlp_bs=gmlp_bs,
                   mla_q_norm_g=mla_q_norm_g, mla_kv_norm_g=mla_kv_norm_g, mla_w_uq=mla_w_uq, mla_w_ukv=mla_w_ukv,
                   out_norm_gmlp_g=out_norm_gmlp_g, out_norm_mla_g=out_norm_mla_g, w_out=w_out,
                   norm_ffn_g=norm_ffn_g, w_ff1=w_ff1, w_ff2=w_ff2, final_norm_g=final_norm_g)
    mom_m = dict(zip(W_NAMES, (m_w_ada, m_b_ada, m_norm_mix_g, m_w_in, m_gmlp_ws, m_gmlp_bs, m_mla_q_norm_g,
                               m_mla_kv_norm_g, m_mla_w_uq, m_mla_w_ukv, m_out_norm_gmlp_g, m_out_norm_mla_g,
                               m_w_out, m_norm_ffn_g, m_w_ff1, m_w_ff2, m_final_norm_g)))
    mom_v = dict(zip(W_NAMES, (v_w_ada, v_b_ada, v_norm_mix_g, v_w_in, v_gmlp_ws, v_gmlp_bs, v_mla_q_norm_g,
                               v_mla_kv_norm_g, v_mla_w_uq, v_mla_w_ukv, v_out_norm_gmlp_g, v_out_norm_mla_g,
                               v_w_out, v_norm_ffn_g, v_w_ff1, v_w_ff2, v_final_norm_g)))
    bsz, seq, d = x.shape
    px, py, pc = _position()
    chip = 2 * px + py
    dev = 2 * chip + pc
    ids = jnp.stack([pc, chip]).astype(jnp.int32)
    n_ex = N_DEV * bsz
    ada_cols = w_ada.shape[-1]

    c_all = _allgather8(c.reshape(bsz * d // 128, 128), name="gather_c").reshape(n_ex, d)
    mod_parts = []
    for l in range(DEPTH):
        bias = lax.dynamic_slice(b_ada[l], (chip * ada_cols,), (ada_cols,))[None]
        mod_parts.append(_mm(c_all, w_ada, dims="nn", name=f"l{l}_mod", tm=n_ex, tn=ada_cols, tk=d, n=ada_cols,
                             b_block=((None, d, ada_cols), lambda i, j, k, l=l: (l, k, j)),
                             epilogue=lambda acc, bv: (acc + bv,), extras=(bias,),
                             extra_specs=(pl.BlockSpec((1, ada_cols), lambda i, j, k: (0, j)),), a_fn=_silu))
    mod_g = _allgather8(jnp.concatenate(mod_parts, axis=0), name="gather_mod")
    mod_g = mod_g.reshape(N_CHIPS, 2, DEPTH, n_ex, ada_cols)[:, 0]
    mod_full = mod_g.transpose(1, 2, 0, 3).reshape(DEPTH, n_ex, N_CHIPS * ada_cols)
    mod_mine = lax.dynamic_slice(mod_full, (0, dev * bsz, 0), (DEPTH, bsz, N_MOD * d))
    mod_mine = jnp.pad(mod_mine.reshape(DEPTH, bsz, N_MOD, d), ((0, 0), (0, 0), (0, MOD_ROWS - N_MOD), (0, 0)))
    mods = [mod_mine[l] for l in range(DEPTH)]

    plan = _CommPlan(weights, ids, dev, pc)
    grad_x, dmods = _local_step(x, loss_target, positions, mods, final_norm_g, plan)
    grad = plan.mix_grads()

    loss, small = plan.small_grads()
    grad.update(small)

    dmod = jnp.stack(dmods, axis=1).reshape(bsz * DEPTH * N_MOD, d)
    dmod_all = _allgather8(dmod, name="gather_dmod").reshape(n_ex, DEPTH, N_MOD * d)
    gw, gb = [], []
    for l in range(DEPTH):
        dm = dmod_all[:, l]
        dm_cols = lax.dynamic_slice(dm, (0, chip * ada_cols), (n_ex, ada_cols))
        gw.append(_mm(c_all, dm_cols, dims="tn", name=f"l{l}_dw_ada", tm=d, tn=ada_cols, tk=n_ex, a_fn=_silu,
                      out_into=(w_ada.shape, (None, d, ada_cols), lambda i, j, k, l=l: (l, i, j),
                                gw[-1] if gw else None)))
        gb.append(_sum_leading(dm.reshape(n_ex, N_MOD * d // FLAT_W, FLAT_W), name=f"l{l}_db_ada").reshape(-1))
    grad["w_ada"] = gw[-1]
    grad["b_ada"] = jnp.stack(gb, axis=0)

    delta, new_m, new_v = {}, {}, {}
    ff_bufs = plan.ff_shards()
    for nm, row_off in (("w_ff1", 0), ("w_ff2", FLAT_W)):
        grad[nm], delta[nm], new_m[nm], new_v[nm] = _adamw_layers(
            weights[nm], mom_m[nm], mom_v[nm], ff_bufs, row_off, name=f"adamw_{nm}")
    for nm in W_NAMES:
        if nm not in delta:
            delta[nm], new_m[nm], new_v[nm] = _adamw(weights[nm], grad[nm], mom_m[nm], mom_v[nm],
                                                     name=f"adamw_{nm}")
    return (loss, grad_x, *[grad[nm] for nm in W_NAMES], *[delta[nm] for nm in W_NAMES],
            *[new_m[nm] for nm in W_NAMES], *[new_v[nm] for nm in W_NAMES])
```

```python
import functools
import math

import jax
import jax.numpy as jnp
from jax import lax
from jax.experimental import pallas as pl
from jax.experimental.pallas import tpu as pltpu

F32 = jnp.float32
BF16 = jnp.bfloat16

D_MODEL = 1024
DEPTH = 2
D_GMLP = 512
GROUPS = 8
GROUP_DIM = 64
CHUNK = 128
HEADS = 8
NOPE = 64
ROPE = 32
HEAD_PAD = 128
Q_RANK = 256
KV_RANK = 128
D_FF = 4096
N_MOD = 6
MOD_ROWS = 8
EPS = 1e-6
ROPE_THETA = 10000.0
D_IN = 1440
D_IN_PAD = 1536
ATTN_SCALE = (NOPE + ROPE) ** -0.5
LOG2E = math.log2(math.e)
SCALE_LOG2 = ATTN_SCALE * LOG2E
N_CHIPS = 4
N_DEV = 8

ADAM_LR = 0.001
ADAM_B1 = 0.9
ADAM_B2 = 0.999
ADAM_EPS = 1e-08
ADAM_WD = 0.01
ADAM_STEP = 10

VMEM_LIMIT = 48 * 1024 * 1024
FLAT_W = 1024
ROW_ALIGN = 256

NN = (((1,), (0,)), ((), ()))
NT = (((1,), (1,)), ((), ()))
TN = (((0,), (0,)), ((), ()))
MESH = pl.DeviceIdType.MESH

SHIFT1, SCALE1, GATE1, SHIFT2, SCALE2, GATE2 = range(6)

FSDP_SECTIONS = (("w_out", 256), ("w_in", 360), ("w_uq", 48), ("w_ukv", 32))


def _cparams(vmem=VMEM_LIMIT):
    return pltpu.CompilerParams(vmem_limit_bytes=vmem)


def _dot(a, b, dims=NN):
    return lax.dot_general(a, b, dims, preferred_element_type=F32)


def _iota(shape, axis):
    return lax.broadcasted_iota(jnp.int32, shape, axis)


def _gelu(x):
    k = math.sqrt(2.0 / math.pi)
    return 0.5 * x * (1.0 + jnp.tanh(k * (x + 0.044715 * (x * x * x))))


def _gelu_and_grad(x):
    k = math.sqrt(2.0 / math.pi)
    x2 = x * x
    t = jnp.tanh(k * (x + 0.044715 * (x2 * x)))
    half = 0.5 * (1.0 + t)
    return x * half, half + 0.5 * x * (1.0 - t * t) * (k * (1.0 + 3.0 * 0.044715 * x2))


def _rms_fwd(x, g, n):
    r = lax.rsqrt(jnp.sum(x * x, axis=-1, keepdims=True) * (1.0 / n) + EPS)
    return x * r * g


def _rms_bwd(x, g, dy, n):
    r = lax.rsqrt(jnp.sum(x * x, axis=-1, keepdims=True) * (1.0 / n) + EPS)
    xh = x * r
    dxh = dy * g
    dx = r * (dxh - xh * (jnp.sum(dxh * xh, axis=-1, keepdims=True) * (1.0 / n)))
    dg = jnp.sum(dy * xh, axis=0, keepdims=True)
    return dx, dg


def _pick_rows(rows, limit):
    if rows <= limit:
        return rows
    for t in range(limit, 7, -8):
        if rows % t == 0:
            return t
    return rows


def _mm(a, b, *, dims, name, tm=512, tn=1024, tk=1024, out_dtypes=(F32,), epilogue=None,
        extras=(), extra_specs=(), a_fn=None, weights_outer=False, side=None, b_block=None, n=None,
        out_into=None):
    if dims == "tn":
        kk, m = a.shape
    else:
        m, kk = a.shape
    if n is None:
        n = b.shape[0] if dims == "nt" else b.shape[1]
    tm, tn, tk = min(tm, m), min(tn, n), min(tk, kk)
    assert m % tm == 0 and n % tn == 0 and kk % tk == 0, (name, a.shape, b.shape, tm, tn, tk)
    ni, nj, nk = m // tm, n // tn, kk // tk

    def spec(shape, pick):
        if weights_outer:
            return pl.BlockSpec(shape, lambda j, i, k: pick(i, j, k))
        return pl.BlockSpec(shape, pick)

    if dims == "tn":
        a_spec = spec((tk, tm), lambda i, j, k: (k, i))
    else:
        a_spec = spec((tm, tk), lambda i, j, k: (i, k))
    if b_block is not None:
        b_spec = spec(*b_block)
    elif dims == "nt":
        b_spec = spec((tn, tk), lambda i, j, k: (j, k))
    else:
        b_spec = spec((tk, tn), lambda i, j, k: (k, j))
    o_spec = spec((tm, tn), lambda i, j, k: (i, j))
    out_shape = [jax.ShapeDtypeStruct((m, n), dt) for dt in out_dtypes]
    out_specs = [o_spec] * len(out_dtypes)
    prev, io_aliases = (), {}
    if out_into is not None:
        full_shape, block, index, before = out_into
        assert len(out_dtypes) == 1 and not extras
        out_shape = [jax.ShapeDtypeStruct(full_shape, out_dtypes[0])]
        out_specs = [spec(block, index)]
        if before is not None:
            prev, io_aliases = (before,), {2: 0}
    assert not (weights_outer and extra_specs)
    dn = {"nn": NN, "nt": NT, "tn": TN}[dims]
    n_ex, n_out = len(extras), len(out_dtypes)
    e_specs = [o_spec if s is None else s for s in (tuple(extra_specs) + (None,) * n_ex)[:n_ex]]

    n_prev = len(prev)

    def body(*refs):
        a_ref, b_ref = refs[0], refs[1]
        e_refs = refs[2 + n_prev:2 + n_prev + n_ex]
        o_refs = refs[2 + n_prev + n_ex:2 + n_prev + n_ex + n_out]
        av = a_ref[...]
        if a_fn is not None:
            av = a_fn(av)
        part = _dot(av.astype(BF16), b_ref[...].astype(BF16), dn)

        def finish(acc):
            outs = (acc,) if epilogue is None else epilogue(acc, *[e[...] for e in e_refs])
            for o_ref, o in zip(o_refs, outs):
                o_ref[...] = o.astype(o_ref.dtype)

        if nk == 1:
            finish(part)
        else:
            acc_ref = refs[-1]
            k = pl.program_id(2)

            @pl.when(k == 0)
            def _():
                acc_ref[...] = part

            @pl.when(k > 0)
            def _():
                acc_ref[...] += part

            @pl.when(k == nk - 1)
            def _():
                finish(acc_ref[...])

    outs, side_outs = _hosted_call(
        body, name=name, grid=(nj, ni, nk) if weights_outer else (ni, nj, nk),
        in_specs=[a_spec, b_spec] + [ANY_SPEC] * n_prev + e_specs,
        out_specs=out_specs, out_shape=out_shape,
        scratch_shapes=[pltpu.VMEM((tm, tn), F32)] if nk > 1 else [],
        args=(a, b, *prev, *extras), side=side, io_aliases=io_aliases)
    res = outs[0] if n_out == 1 else outs
    return res if side is None else (res, side_outs)


def _mod_spec(tm, tn, seq):
    return pl.BlockSpec((1, MOD_ROWS, tn), lambda i, j, k: ((i * tm) // seq, 0, j))


def _normmod_fwd(x3, g, mod, shift_row, scale_row, *, name, tb=512, side=None):
    bsz, seq, d = x3.shape
    tb = min(tb, seq)

    def body(x_ref, g_ref, mod_ref, h_ref):
        m = mod_ref[0]
        nrm = _rms_fwd(x_ref[0], g_ref[...], d)
        h = nrm * (1.0 + m[scale_row:scale_row + 1, :]) + m[shift_row:shift_row + 1, :]
        h_ref[0] = h.astype(BF16)

    outs, side_outs = _hosted_call(
        body, name=name, grid=(bsz, seq // tb),
        in_specs=[pl.BlockSpec((1, tb, d), lambda b, i: (b, i, 0)),
                  pl.BlockSpec((1, d), lambda b, i: (0, 0)),
                  pl.BlockSpec((1, MOD_ROWS, d), lambda b, i: (b, 0, 0))],
        out_specs=[pl.BlockSpec((1, tb, d), lambda b, i: (b, i, 0))],
        out_shape=[jax.ShapeDtypeStruct((bsz, seq, d), BF16)],
        args=(x3, g, mod), side=side)
    return outs[0] if side is None else (outs[0], side_outs)


def _pair_mean_exact(x, lo):
    s_lo = jnp.sum(jnp.where(lo, x, 0.0), axis=-1, keepdims=True)
    s_hi = jnp.sum(jnp.where(lo, 0.0, x), axis=-1, keepdims=True)
    return jnp.where(lo, s_lo, s_hi) * (1.0 / GROUP_DIM)


def _gmlp_pair_fwd(gv_p, w0, w1, bias, lo):
    mu = _pair_mean_exact(gv_p, lo)
    dlt = gv_p - mu
    var = _pair_mean_exact(dlt * dlt, lo)
    rstd = lax.rsqrt(var + EPS)
    vn = dlt * rstd
    vnb = vn.astype(BF16)
    mixed = jnp.where(lo, _dot(w0, vnb), _dot(w1, vnb)) + bias
    return vn, vnb, rstd, mixed


def _tril_bf16(w):
    t = w.shape[-1]
    return jnp.where(_iota((t, t), 1) <= _iota((t, t), 0), w, 0.0).astype(BF16)


def _gmlp_fwd(z3, ws, bexp, g_out, *, name):
    bsz, seq, _ = z3.shape
    nc = seq // CHUNK

    def body(u_ref, v_ref, ws_ref, b_ref, g_ref, y_ref):
        lo = _iota((CHUNK, 128), 1) < GROUP_DIM
        gu = _gelu(u_ref[0].astype(F32))
        gv = _gelu(v_ref[0].astype(F32))
        parts = []
        for p in range(GROUPS // 2):
            sl = slice(128 * p, 128 * p + 128)
            w0 = _tril_bf16(ws_ref[2 * p])
            w1 = _tril_bf16(ws_ref[2 * p + 1])
            _, _, _, mixed = _gmlp_pair_fwd(gv[:, sl], w0, w1, b_ref[p], lo)
            parts.append(gu[:, sl] * mixed)
        yg = jnp.concatenate(parts, axis=1)
        y_ref[0] = _rms_fwd(yg, g_ref[...], D_GMLP).astype(BF16)

    return pl.pallas_call(
        body, name=name, grid=(bsz, nc),
        in_specs=[pl.BlockSpec((1, CHUNK, D_GMLP), lambda b, i: (b, i, 0)),
                  pl.BlockSpec((1, CHUNK, D_GMLP), lambda b, i: (b, i, 1)),
                  pl.BlockSpec((GROUPS, CHUNK, CHUNK), lambda b, i: (0, 0, 0)),
                  pl.BlockSpec((GROUPS // 2, CHUNK, 128), lambda b, i: (0, 0, 0)),
                  pl.BlockSpec((1, D_GMLP), lambda b, i: (0, 0))],
        out_specs=pl.BlockSpec((1, CHUNK, D_GMLP), lambda b, i: (b, i, 0)),
        out_shape=jax.ShapeDtypeStruct((bsz, seq, D_GMLP), BF16),
        compiler_params=_cparams(),
    )(z3, z3, ws, bexp, g_out)


def _gmlp_bwd(z3, dyn3, ws, wst, bexp, g_out, *, name, dy_col):
    bsz, seq, _ = z3.shape
    nc = seq // CHUNK
    npair = GROUPS // 2

    def body(u_ref, v_ref, dy_ref, ws_ref, wst_ref, b_ref, g_ref, duv_ref, dws_ref, dbs_ref, dg_ref, dbacc):
        first = jnp.logical_and(pl.program_id(0) == 0, pl.program_id(1) == 0)
        last = jnp.logical_and(pl.program_id(0) == bsz - 1, pl.program_id(1) == nc - 1)

        @pl.when(first)
        def _():
            dws_ref[...] = jnp.zeros_like(dws_ref)
            dg_ref[...] = jnp.zeros_like(dg_ref)
            dbacc[...] = jnp.zeros_like(dbacc)

        lo = _iota((CHUNK, 128), 1) < GROUP_DIM
        tril = _iota((CHUNK, CHUNK), 1) <= _iota((CHUNK, CHUNK), 0)
        u = u_ref[0].astype(F32)
        v = v_ref[0].astype(F32)
        gu, dgu = _gelu_and_grad(u)
        gv, dgv_dv = _gelu_and_grad(v)
        fwd = []
        for p in range(npair):
            sl = slice(128 * p, 128 * p + 128)
            w0 = _tril_bf16(ws_ref[2 * p])
            w1 = _tril_bf16(ws_ref[2 * p + 1])
            fwd.append(_gmlp_pair_fwd(gv[:, sl], w0, w1, b_ref[p], lo))
        yg = jnp.concatenate([gu[:, 128 * p:128 * p + 128] * fwd[p][3] for p in range(npair)], axis=1)
        dyg, dg = _rms_bwd(yg, g_ref[...], dy_ref[0].astype(F32), D_GMLP)
        dg_ref[...] += dg
        du_parts, dv_parts = [], []
        for p in range(npair):
            sl = slice(128 * p, 128 * p + 128)
            vn, vnb, rstd, mixed = fwd[p]
            dyg_p = dyg[:, sl]
            dmixed = dyg_p * gu[:, sl]
            dbacc[p] += dmixed
            dm_lo = jnp.where(lo, dmixed, 0.0).astype(BF16)
            dm_hi = jnp.where(lo, 0.0, dmixed).astype(BF16)
            dws_ref[2 * p] += jnp.where(tril, _dot(dm_lo, vnb, NT), 0.0)
            dws_ref[2 * p + 1] += jnp.where(tril, _dot(dm_hi, vnb, NT), 0.0)
            dmb = dmixed.astype(BF16)
            dvn = jnp.where(lo, _dot(wst_ref[2 * p], dmb), _dot(wst_ref[2 * p + 1], dmb))
            dgv = rstd * (dvn - _pair_mean_exact(dvn, lo) - vn * _pair_mean_exact(dvn * vn, lo))
            dv_parts.append(dgv * dgv_dv[:, sl])
            du_parts.append(dyg_p * mixed * dgu[:, sl])
        duv_ref[0] = jnp.concatenate(du_parts + dv_parts, axis=1).astype(BF16)

        @pl.when(last)
        def _():
            sel = jnp.where(_iota((8, 128), 0) == 0, (_iota((8, 128), 1) < GROUP_DIM).astype(F32),
                            jnp.where(_iota((8, 128), 0) == 1, (_iota((8, 128), 1) >= GROUP_DIM).astype(F32), 0.0))
            for p in range(npair):
                dbs_ref[p] = lax.dot_general(sel, dbacc[p], NT, precision=lax.Precision.HIGHEST,
                                             preferred_element_type=F32)

    duv, dws, dbs, dg = pl.pallas_call(
        body, name=name, grid=(bsz, nc),
        in_specs=[pl.BlockSpec((1, CHUNK, D_GMLP), lambda b, i: (b, i, 0)),
                  pl.BlockSpec((1, CHUNK, D_GMLP), lambda b, i: (b, i, 1)),
                  pl.BlockSpec((1, CHUNK, D_GMLP), lambda b, i: (b, i, dy_col)),
                  pl.BlockSpec((GROUPS, CHUNK, CHUNK), lambda b, i: (0, 0, 0)),
                  pl.BlockSpec((GROUPS, CHUNK, CHUNK), lambda b, i: (0, 0, 0)),
                  pl.BlockSpec((npair, CHUNK, 128), lambda b, i: (0, 0, 0)),
                  pl.BlockSpec((1, D_GMLP), lambda b, i: (0, 0))],
        out_specs=[pl.BlockSpec((1, CHUNK, 2 * D_GMLP), lambda b, i: (b, i, 0)),
                   pl.BlockSpec((GROUPS, CHUNK, CHUNK), lambda b, i: (0, 0, 0)),
                   pl.BlockSpec((npair, 8, CHUNK), lambda b, i: (0, 0, 0)),
                   pl.BlockSpec((1, D_GMLP), lambda b, i: (0, 0))],
        out_shape=[jax.ShapeDtypeStruct((bsz, seq, D_IN_PAD), BF16),
                   jax.ShapeDtypeStruct((GROUPS, CHUNK, CHUNK), F32),
                   jax.ShapeDtypeStruct((npair, 8, CHUNK), F32),
                   jax.ShapeDtypeStruct((1, D_GMLP), F32)],
        scratch_shapes=[pltpu.VMEM((npair, CHUNK, 128), F32)],
        compiler_params=_cparams(),
    )(z3, z3, dyn3, ws, wst, bexp, g_out)
    return duv, dws, dbs[:, :2, :].reshape(GROUPS, CHUNK), dg


def _partner(x):
    width = x.shape[-1]
    lane = _iota(x.shape, x.ndim - 1) % HEAD_PAD
    up = pltpu.roll(x, width - ROPE // 2, x.ndim - 1)
    down = pltpu.roll(x, ROPE // 2, x.ndim - 1)
    first = jnp.logical_and(lane >= NOPE, lane < NOPE + ROPE // 2)
    second = jnp.logical_and(lane >= NOPE + ROPE // 2, lane < NOPE + ROPE)
    return jnp.where(first, up, jnp.where(second, down, 0.0))


def _mla_prep_fwd(z3, g_q, g_kv, w_uq, w_ukv, ctab, stab, *, name, tb=256):
    bsz, seq, _ = z3.shape
    tb = min(tb, seq)
    hw = HEADS * HEAD_PAD

    def body(ql_ref, kvl_ref, krl_ref, gq_ref, gkv_ref, wuq_ref, wukv_ref, c_ref, s_ref, q_ref, kv_ref, kp_ref):
        cq = _rms_fwd(ql_ref[0].astype(F32), gq_ref[...], Q_RANK).astype(BF16)
        q = _dot(cq, wuq_ref[...])
        c1, s1 = c_ref[0], s_ref[0]
        c8, s8 = jnp.tile(c1, (1, HEADS)), jnp.tile(s1, (1, HEADS))
        q_ref[0] = ((q * c8 + _partner(q) * s8) * SCALE_LOG2).astype(BF16)
        ckv = _rms_fwd(kvl_ref[0].astype(F32), gkv_ref[...], KV_RANK).astype(BF16)
        kv = _dot(ckv, wukv_ref[...])
        kv_ref[0] = kv.astype(BF16)
        kr = krl_ref[0].astype(F32)
        kr = kr * c1 + _partner(kr) * s1
        lane = _iota((tb, hw), 1) % HEAD_PAD
        kp_ref[0] = jnp.where(lane < NOPE, kv, jnp.tile(kr, (1, HEADS))).astype(BF16)

    return pl.pallas_call(
        body, name=name, grid=(bsz, seq // tb),
        in_specs=[pl.BlockSpec((1, tb, Q_RANK), lambda b, i: (b, i, 4)),
                  pl.BlockSpec((1, tb, KV_RANK), lambda b, i: (b, i, 10)),
                  pl.BlockSpec((1, tb, HEAD_PAD), lambda b, i: (b, i, 11)),
                  pl.BlockSpec((1, Q_RANK), lambda b, i: (0, 0)),
                  pl.BlockSpec((1, KV_RANK), lambda b, i: (0, 0)),
                  pl.BlockSpec((Q_RANK, hw), lambda b, i: (0, 0)),
                  pl.BlockSpec((KV_RANK, hw), lambda b, i: (0, 0)),
                  pl.BlockSpec((1, tb, HEAD_PAD), lambda b, i: (b, i, 0)),
                  pl.BlockSpec((1, tb, HEAD_PAD), lambda b, i: (b, i, 0))],
        out_specs=[pl.BlockSpec((1, tb, hw), lambda b, i: (b, i, 0))] * 3,
        out_shape=[jax.ShapeDtypeStruct((bsz, seq, hw), BF16)] * 3,
        compiler_params=_cparams(),
    )(z3, z3, z3, g_q, g_kv, w_uq, w_ukv, ctab, stab)


def _mla_prep_bwd(z3, dz3, dq3, dk3, dv3, g_q, g_kv, w_uq, w_ukv, ctab, stab, *, name, tb=256):
    bsz, seq, _ = z3.shape
    tb = min(tb, seq)
    hw = HEADS * HEAD_PAD
    nb = seq // tb

    def body(ql_ref, kvl_ref, dq_ref, dk_ref, dv_ref, gq_ref, gkv_ref, wuq_ref, wukv_ref, c_ref, s_ref, dz_in,
             dz_ref, cq_ref, dqb_ref, ckv_ref, dkvb_ref, dgq_ref, dgkv_ref):
        @pl.when(jnp.logical_and(pl.program_id(0) == 0, pl.program_id(1) == 0))
        def _():
            dgq_ref[...] = jnp.zeros_like(dgq_ref)
            dgkv_ref[...] = jnp.zeros_like(dgkv_ref)

        c1, s1 = c_ref[0], s_ref[0]
        c8, s8 = jnp.tile(c1, (1, HEADS)), jnp.tile(s1, (1, HEADS))
        dqr = dq_ref[0]
        dqb = (dqr * c8 + _partner(dqr * s8)).astype(BF16)
        dqb_ref[0] = dqb
        ql = ql_ref[0].astype(F32)
        cq_ref[0] = _rms_fwd(ql, gq_ref[...], Q_RANK).astype(BF16)
        dql, dgq = _rms_bwd(ql, gq_ref[...], _dot(dqb, wuq_ref[...], NT), Q_RANK)
        dgq_ref[...] += dgq

        dk = dk_ref[0]
        lane = _iota((tb, hw), 1) % HEAD_PAD
        dkvb = jnp.where(lane < NOPE, dk, dv_ref[0]).astype(BF16)
        dkvb_ref[0] = dkvb
        kvl = kvl_ref[0].astype(F32)
        ckv_ref[0] = _rms_fwd(kvl, gkv_ref[...], KV_RANK).astype(BF16)
        dkvl, dgkv = _rms_bwd(kvl, gkv_ref[...], _dot(dkvb, wukv_ref[...], NT), KV_RANK)
        dgkv_ref[...] += dgkv

        dkr = dk[:, 0:HEAD_PAD].astype(F32)
        for h in range(1, HEADS):
            dkr = dkr + dk[:, HEAD_PAD * h:HEAD_PAD * (h + 1)].astype(F32)
        lane1 = _iota((tb, HEAD_PAD), 1)
        dkr = jnp.where(jnp.logical_and(lane1 >= NOPE, lane1 < NOPE + ROPE), dkr, 0.0)
        dkrl = dkr * c1 + _partner(dkr * s1)
        dz_ref[0] = jnp.concatenate([dql, dkvl, dkrl], axis=1).astype(BF16)

    return pl.pallas_call(
        body, name=name, grid=(bsz, nb),
        in_specs=[pl.BlockSpec((1, tb, Q_RANK), lambda b, i: (b, i, 4)),
                  pl.BlockSpec((1, tb, KV_RANK), lambda b, i: (b, i, 10)),
                  pl.BlockSpec((1, tb, hw), lambda b, i: (b, i, 0)),
                  pl.BlockSpec((1, tb, hw), lambda b, i: (b, i, 0)),
                  pl.BlockSpec((1, tb, hw), lambda b, i: (b, i, 0)),
                  pl.BlockSpec((1, Q_RANK), lambda b, i: (0, 0)),
                  pl.BlockSpec((1, KV_RANK), lambda b, i: (0, 0)),
                  pl.BlockSpec((Q_RANK, hw), lambda b, i: (0, 0)),
                  pl.BlockSpec((KV_RANK, hw), lambda b, i: (0, 0)),
                  pl.BlockSpec((1, tb, HEAD_PAD), lambda b, i: (b, i, 0)),
                  pl.BlockSpec((1, tb, HEAD_PAD), lambda b, i: (b, i, 0)),
                  ANY_SPEC],
        out_specs=[pl.BlockSpec((1, tb, 512), lambda b, i: (b, i, 2)),
                   pl.BlockSpec((1, tb, Q_RANK), lambda b, i: (b, i, 0)),
                   pl.BlockSpec((1, tb, hw), lambda b, i: (b, i, 0)),
                   pl.BlockSpec((1, tb, KV_RANK), lambda b, i: (b, i, 0)),
                   pl.BlockSpec((1, tb, hw), lambda b, i: (b, i, 0)),
                   pl.BlockSpec((1, Q_RANK), lambda b, i: (0, 0)),
                   pl.BlockSpec((1, KV_RANK), lambda b, i: (0, 0))],
        out_shape=[jax.ShapeDtypeStruct((bsz, seq, D_IN_PAD), BF16),
                   jax.ShapeDtypeStruct((bsz, seq, Q_RANK), BF16),
                   jax.ShapeDtypeStruct((bsz, seq, hw), BF16),
                   jax.ShapeDtypeStruct((bsz, seq, KV_RANK), BF16),
                   jax.ShapeDtypeStruct((bsz, seq, hw), BF16),
                   jax.ShapeDtypeStruct((1, Q_RANK), F32),
                   jax.ShapeDtypeStruct((1, KV_RANK), F32)],
        input_output_aliases={11: 0},
        compiler_params=_cparams(),
    )(z3, z3, dq3, dk3, dv3, g_q, g_kv, w_uq, w_ukv, ctab, stab, dz3)


ATTN_HEADS_PER_STEP = 4


def _attn_specs(tq, seq, hp):
    blk = pl.BlockSpec((1, tq, hp * HEAD_PAD), lambda b, h, i: (b, i, h))
    full = pl.BlockSpec((1, seq, hp * HEAD_PAD), lambda b, h, i: (b, 0, h))
    return blk, full


def _head(h):
    return slice(HEAD_PAD * h, HEAD_PAD * (h + 1))


def _attn_fwd(q3, kv3, kp3, *, name, tq=512, hp=ATTN_HEADS_PER_STEP, side=None):
    bsz, seq, hw = q3.shape
    tq = min(tq, seq)
    blk, full = _attn_specs(tq, seq, hp)

    def body(q_ref, kv_ref, kp_ref, o_ref, lse_ref):
        i = pl.program_id(2)
        is_nope = _iota((tq, HEAD_PAD), 1) < NOPE
        causal = _iota((tq, tq), 1) <= _iota((tq, tq), 0)

        def step(j, carry, diag):
            st = pl.multiple_of(j * tq, tq)
            out = []
            for h in range(hp):
                m, l, acc = carry[h]
                kvj = kv_ref[0, pl.ds(st, tq), _head(h)]
                s = _dot(q_ref[0, :, _head(h)], kp_ref[0, pl.ds(st, tq), _head(h)], NT)
                if diag:
                    s = jnp.where(causal, s, -1e30)
                m_new = jnp.maximum(m, jnp.max(s, axis=1, keepdims=True))
                alpha = jnp.exp2(m - m_new)
                p = jnp.exp2(s - m_new)
                l = alpha * l + jnp.sum(p, axis=1, keepdims=True)
                acc = alpha * acc + _dot(p.astype(BF16), kvj)
                out.append((m_new, l, acc))
            return tuple(out)

        init = tuple((jnp.full((tq, 1), -1e30, F32), jnp.zeros((tq, 1), F32), jnp.zeros((tq, HEAD_PAD), F32))
                     for _ in range(hp))
        carry = lax.fori_loop(0, i, lambda j, c: step(j, c, False), init)
        carry = step(i, carry, True)
        for h in range(hp):
            m, l, acc = carry[h]
            o_ref[0, :, _head(h)] = jnp.where(is_nope, 0.0, acc / l).astype(BF16)
            lse_ref[0, :, _head(h)] = jnp.broadcast_to(m + jnp.log(l) * LOG2E, (tq, HEAD_PAD))

    outs, side_outs = _hosted_call(
        body, name=name, grid=(bsz, HEADS // hp, seq // tq),
        in_specs=[blk, full, full],
        out_specs=[blk, blk],
        out_shape=[jax.ShapeDtypeStruct((bsz, seq, hw), BF16), jax.ShapeDtypeStruct((bsz, seq, hw), F32)],
        args=(q3, kv3, kp3), side=side)
    return outs if side is None else (outs, side_outs)


def _attn_bwd(q3, kv3, kp3, do3, lse3, dl3, *, name, tq=512, hp=ATTN_HEADS_PER_STEP, side=None):
    bsz, seq, hw = q3.shape
    tq = min(tq, seq)
    nq = seq // tq
    blk, full = _attn_specs(tq, seq, hp)
    rep = tq // HEAD_PAD

    def body(kv_ref, kp_ref, q_ref, do_ref, lse_ref, dl_ref, dq_ref, dk_ref, dv_ref):
        j = pl.program_id(2)
        causal = _iota((tq, tq), 1) <= _iota((tq, tq), 0)

        @pl.when(j == 0)
        def _():
            dq_ref[...] = jnp.zeros_like(dq_ref)

        def step(i, carry, diag):
            st = pl.multiple_of(i * tq, tq)
            out = []
            for h in range(hp):
                dk, dv = carry[h]
                qi = q_ref[0, pl.ds(st, tq), _head(h)]
                do = do_ref[0, pl.ds(st, tq), _head(h)]
                kp = kp_ref[0, :, _head(h)]
                s = _dot(qi, kp, NT)
                if diag:
                    s = jnp.where(causal, s, -1e30)
                p = jnp.exp2(s - jnp.tile(lse_ref[0, pl.ds(st, tq), _head(h)], (1, rep)))
                dv = dv + _dot(p.astype(BF16), do, TN)
                dp = _dot(do, kv_ref[0, :, _head(h)], NT)
                ds = (p * (dp - jnp.tile(dl_ref[0, pl.ds(st, tq), _head(h)], (1, rep)))).astype(BF16)
                dk = dk + _dot(ds, qi, TN)
                dq_ref[0, pl.ds(st, tq), _head(h)] += _dot(ds, kp)
                out.append((dk, dv))
            return tuple(out)

        zero = jnp.zeros((tq, HEAD_PAD), F32)
        carry = step(j, tuple((zero, zero) for _ in range(hp)), True)
        carry = lax.fori_loop(j + 1, nq, lambda i, c: step(i, c, False), carry)
        for h in range(hp):
            dk_ref[0, :, _head(h)] = (carry[h][0] * (1.0 / LOG2E)).astype(BF16)
            dv_ref[0, :, _head(h)] = carry[h][1].astype(BF16)

        @pl.when(j == nq - 1)
        def _():
            dq_ref[...] = dq_ref[...] * ATTN_SCALE

    outs, side_outs = _hosted_call(
        body, name=name, grid=(bsz, HEADS // hp, nq),
        in_specs=[blk, blk, full, full, full, full],
        out_specs=[full, blk, blk],
        out_shape=[jax.ShapeDtypeStruct((bsz, seq, hw), F32)] + [jax.ShapeDtypeStruct((bsz, seq, hw), BF16)] * 2,
        args=(kv3, kp3, q3, do3, lse3, dl3), side=side)
    return outs if side is None else (outs, side_outs)


def _onorm_fwd(o3, yg3, g_pad, *, name, tb=512):
    bsz, seq, hw = o3.shape
    wg = yg3.shape[-1]
    tb = min(tb, seq)

    def body(o_ref, yg_ref, g_ref, y_ref):
        ya = _rms_fwd(o_ref[0].astype(F32), g_ref[...], HEADS * 64).astype(BF16)
        y_ref[0] = jnp.concatenate([ya, yg_ref[0]], axis=1)

    return pl.pallas_call(
        body, name=name, grid=(bsz, seq // tb),
        in_specs=[pl.BlockSpec((1, tb, hw), lambda b, i: (b, i, 0)),
                  pl.BlockSpec((1, tb, wg), lambda b, i: (b, i, 0)),
                  pl.BlockSpec((1, hw), lambda b, i: (0, 0))],
        out_specs=pl.BlockSpec((1, tb, hw + wg), lambda b, i: (b, i, 0)),
        out_shape=jax.ShapeDtypeStruct((bsz, seq, hw + wg), BF16),
        compiler_params=_cparams(),
    )(o3, yg3, g_pad)


def _onorm_bwd(o3, dy3, g_pad, *, name, tb=512):
    bsz, seq, hw = o3.shape
    tb = min(tb, seq)

    def body(o_ref, dy_ref, g_ref, do_ref, dl_ref, dg_ref):
        @pl.when(jnp.logical_and(pl.program_id(0) == 0, pl.program_id(1) == 0))
        def _():
            dg_ref[...] = jnp.zeros_like(dg_ref)

        o = o_ref[0].astype(F32)
        do, dg = _rms_bwd(o, g_ref[...], dy_ref[0].astype(F32), HEADS * 64)
        dg_ref[...] += dg
        do_ref[0] = do.astype(BF16)
        prod = do * o
        parts = []
        for h in range(HEADS):
            sh = jnp.sum(prod[:, HEAD_PAD * h:HEAD_PAD * (h + 1)], axis=1, keepdims=True)
            parts.append(jnp.broadcast_to(sh, (tb, HEAD_PAD)))
        dl_ref[0] = jnp.concatenate(parts, axis=1)

    return pl.pallas_call(
        body, name=name, grid=(bsz, seq // tb),
        in_specs=[pl.BlockSpec((1, tb, hw), lambda b, i: (b, i, 0)),
                  pl.BlockSpec((1, tb, hw), lambda b, i: (b, i, 0)),
                  pl.BlockSpec((1, hw), lambda b, i: (0, 0))],
        out_specs=[pl.BlockSpec((1, tb, hw), lambda b, i: (b, i, 0)),
                   pl.BlockSpec((1, tb, hw), lambda b, i: (b, i, 0)),
                   pl.BlockSpec((1, hw), lambda b, i: (0, 0))],
        out_shape=[jax.ShapeDtypeStruct((bsz, seq, hw), BF16),
                   jax.ShapeDtypeStruct((bsz, seq, hw), F32),
                   jax.ShapeDtypeStruct((1, hw), F32)],
        compiler_params=_cparams(),
    )(o3, dy3, g_pad)


def _resnode_bwd(x3, g, *, name, target3=None, dh3=None, dres3=None, mod_nm=None, rows=None,
                 branch3=None, mod_gate=None, gate_row=None, tb=512, side=None):
    bsz, seq, d = x3.shape
    tb = min(tb, seq)
    final = target3 is not None
    has_branch = branch3 is not None
    row_spec = pl.BlockSpec((1, tb, d), lambda b, i: (b, i, 0))
    vec_spec = pl.BlockSpec((1, d), lambda b, i: (0, 0))
    mod_spec = pl.BlockSpec((1, MOD_ROWS, d), lambda b, i: (b, 0, 0))

    ins, in_specs = [x3, g], [row_spec, vec_spec]
    if final:
        ins += [target3]
        in_specs += [row_spec]
    else:
        ins += [dh3, dres3, mod_nm]
        in_specs += [row_spec, row_spec, mod_spec]
    if has_branch:
        ins += [branch3, mod_gate]
        in_specs += [row_spec, mod_spec]

    out_names = ["dx", "dg"]
    out_specs = [row_spec, vec_spec]
    out_shape = [jax.ShapeDtypeStruct((bsz, seq, d), F32), jax.ShapeDtypeStruct((1, d), F32)]
    if final:
        out_names += ["loss"]
        out_specs += [pl.BlockSpec((1, 128), lambda b, i: (0, 0))]
        out_shape += [jax.ShapeDtypeStruct((1, 128), F32)]
    else:
        out_names += ["dnm"]
        out_specs += [mod_spec]
        out_shape += [jax.ShapeDtypeStruct((bsz, MOD_ROWS, d), F32)]
    if has_branch:
        out_names += ["dbr", "dgate"]
        out_specs += [row_spec, mod_spec]
        out_shape += [jax.ShapeDtypeStruct((bsz, seq, d), BF16), jax.ShapeDtypeStruct((bsz, MOD_ROWS, d), F32)]
    n_in = len(ins)

    def body(*refs):
        r = dict(zip(["x", "g"] + (["t"] if final else ["dh", "dres", "nm"]) + (["br", "gm"] if has_branch else []),
                     refs[:n_in]))
        o = dict(zip(out_names, refs[n_in:]))
        b_first = pl.program_id(1) == 0
        first = jnp.logical_and(pl.program_id(0) == 0, b_first)
        rowid = _iota((MOD_ROWS, d), 0)

        @pl.when(first)
        def _():
            o["dg"][...] = jnp.zeros_like(o["dg"])
            if final:
                o["loss"][...] = jnp.zeros_like(o["loss"])

        @pl.when(b_first)
        def _():
            if not final:
                o["dnm"][...] = jnp.zeros_like(o["dnm"])
            if has_branch:
                o["dgate"][...] = jnp.zeros_like(o["dgate"])

        x = r["x"][0]
        gv = r["g"][...]
        if final:
            e = _rms_fwd(x, gv, d) - r["t"][0]
            sq = jnp.sum(jnp.sum(e * e, axis=1, keepdims=True), axis=0, keepdims=True)
            o["loss"][...] += jnp.broadcast_to(sq * (0.5 / d), (1, 128))
            dx, dg = _rms_bwd(x, gv, e * (1.0 / d), d)
        else:
            m = r["nm"][0]
            dh = r["dh"][0].astype(F32)
            scale = m[rows[1]:rows[1] + 1, :]
            rstd = lax.rsqrt(jnp.sum(x * x, axis=-1, keepdims=True) * (1.0 / d) + EPS)
            xh = x * rstd
            nrm = xh * gv
            dshift = jnp.sum(dh, axis=0, keepdims=True)
            dscale = jnp.sum(dh * nrm, axis=0, keepdims=True)
            o["dnm"][0] += jnp.where(rowid == 0, dshift, jnp.where(rowid == 1, dscale, 0.0))
            dn = dh * (1.0 + scale)
            dg = jnp.sum(dn * xh, axis=0, keepdims=True)
            dxh = dn * gv
            dx = rstd * (dxh - xh * (jnp.sum(dxh * xh, axis=-1, keepdims=True) * (1.0 / d))) + r["dres"][0]
        o["dg"][...] += dg
        o["dx"][0] = dx
        if has_branch:
            gate = r["gm"][0][gate_row:gate_row + 1, :]
            o["dbr"][0] = (gate * dx).astype(BF16)
            dgate = jnp.sum(dx * r["br"][0], axis=0, keepdims=True)
            o["dgate"][0] += jnp.where(rowid == 0, dgate, 0.0)

    outs, side_outs = _hosted_call(
        body, name=name, grid=(bsz, seq // tb),
        in_specs=in_specs, out_specs=out_specs, out_shape=out_shape, args=tuple(ins), side=side)
    res = dict(zip(out_names, outs))
    return res if side is None else (res, side_outs)


def _adamw(w, g, m, v, *, name):
    shape = w.shape
    cols = shape[-1]
    rows = w.size // cols
    tr = _pick_rows(rows, max(8, (256 * 1024) // cols // 8 * 8))

    def body(w_ref, g_ref, m_ref, v_ref, d_ref, nm_ref, nv_ref):
        d_ref[...], nm_ref[...], nv_ref[...] = _adamw_math(w_ref[...], g_ref[...], m_ref[...], v_ref[...])

    spec = pl.BlockSpec((tr, cols), lambda i: (i, 0))
    outs = pl.pallas_call(
        body, name=name, grid=(rows // tr,),
        in_specs=[spec] * 4, out_specs=[spec] * 3,
        out_shape=[jax.ShapeDtypeStruct((rows, cols), F32)] * 3,
        compiler_params=_cparams(),
    )(*[t.reshape(rows, cols) for t in (w, g, m, v)])
    return tuple(o.reshape(shape) for o in outs)


def _adamw_math(w, g, m, v):
    c1 = 1.0 - ADAM_B1 ** ADAM_STEP
    c2 = 1.0 - ADAM_B2 ** ADAM_STEP
    nm = ADAM_B1 * m + (1.0 - ADAM_B1) * g
    nv = ADAM_B2 * v + (1.0 - ADAM_B2) * (g * g)
    delta = -ADAM_LR * ((nm / c1) / (jnp.sqrt(nv / c2) + ADAM_EPS) + ADAM_WD * w)
    return delta, nm, nv


def _adamw_layers(w, m, v, bufs, row_off, *, name, tr=256):
    depth, rows, cols = w.shape
    tr = min(tr, rows)
    assert rows % tr == 0 and row_off % tr == 0

    outs = None
    for l in range(depth):
        def body(w_ref, g_ref, m_ref, v_ref, *rest):
            go_ref, d_ref, nm_ref, nv_ref = rest[-4:]
            g = g_ref[...]
            go_ref[...] = g
            d_ref[...], nm_ref[...], nv_ref[...] = _adamw_math(w_ref[...], g, m_ref[...], v_ref[...])

        layer = pl.BlockSpec((None, tr, cols), lambda i, l=l: (l, i, 0))
        prev = () if outs is None else tuple(outs)
        outs = pl.pallas_call(
            body, name=f"{name}_l{l}", grid=(rows // tr,),
            in_specs=[layer, pl.BlockSpec((tr, cols), lambda i: (row_off // tr + i, 0)), layer, layer]
            + [ANY_SPEC] * len(prev),
            out_specs=[layer] * 4,
            out_shape=[jax.ShapeDtypeStruct(w.shape, F32)] * 4,
            input_output_aliases={4 + k: k for k in range(len(prev))},
            compiler_params=_cparams(),
        )(w, bufs[l], m, v, *prev)
    return tuple(outs)


def _sum_leading(x, *, name, tr=256):
    n, rows, cols = x.shape
    tr = _pick_rows(rows, tr)

    def body(x_ref, o_ref):
        acc = x_ref[0]
        for k in range(1, n):
            acc = acc + x_ref[k]
        o_ref[...] = acc

    return pl.pallas_call(
        body, name=name, grid=(rows // tr,),
        in_specs=[pl.BlockSpec((n, tr, cols), lambda i: (0, i, 0))],
        out_specs=pl.BlockSpec((tr, cols), lambda i: (i, 0)),
        out_shape=jax.ShapeDtypeStruct((rows, cols), F32),
        compiler_params=_cparams(),
    )(x)


def _position():
    return lax.axis_index("x"), lax.axis_index("y"), lax.axis_index("c")


def _allgather8(x, *, name):
    shape = x.shape

    def body(x_ref, out_ref, send_sems, recv_sems, local_sem):
        px, py, pc = _position()
        me, sibling = (px, py, pc), (px, py, 1 - pc)
        chips = [(1 - px, py), (px, 1 - py), (1 - px, 1 - py)]
        src_own = x_ref

        def slot(qx, qy, qc):
            return out_ref.at[4 * qx + 2 * qy + qc]

        def copy(k, block, to, src=None):
            return pltpu.make_async_remote_copy(
                src_ref=slot(*block) if src is None else src, dst_ref=slot(*block),
                send_sem=send_sems.at[k], recv_sem=recv_sems.at[k], device_id=to, device_id_type=MESH)

        mine = pltpu.make_async_copy(src_own, slot(*me), local_sem)
        mine.start()
        first = [copy(0, me, sibling, src=src_own)]
        first += [copy(1 + j, me, (*chip, pc), src=src_own) for j, chip in enumerate(chips)]
        for cp in first:
            cp.start()
        passed = [copy(4 + j, (*chip, pc), sibling) for j, chip in enumerate(chips)]
        for j, chip in enumerate(chips):
            copy(1 + j, (*chip, pc), me).wait_recv()
            passed[j].start()
        copy(0, sibling, me).wait_recv()
        for j, chip in enumerate(chips):
            copy(4 + j, (*chip, 1 - pc), me).wait_recv()
        for cp in first + passed:
            cp.wait_send()
        mine.wait()

    return pl.pallas_call(
        body, name=name,
        out_shape=jax.ShapeDtypeStruct((N_DEV,) + shape, x.dtype),
        in_specs=[pl.BlockSpec(memory_space=pl.ANY)],
        out_specs=pl.BlockSpec(memory_space=pl.ANY),
        scratch_shapes=[pltpu.SemaphoreType.DMA((7,)), pltpu.SemaphoreType.DMA((7,)), pltpu.SemaphoreType.DMA],
    )(x)


class _Exchange:
    def __init__(self, ins, out_shapes, n, build, aliases=None):
        self.ins, self.out_shapes, self.n, self.build = tuple(ins), tuple(out_shapes), n, build
        self.aliases = dict(aliases or {})

    def _descriptors(self, in_refs, out_refs, send_sems, recv_sems):
        sends, recvs = [], []
        for k, (src, dst, peer, landing) in enumerate(self.build(in_refs, out_refs)):
            sends.append(pltpu.make_async_remote_copy(
                src_ref=src, dst_ref=dst, send_sem=send_sems.at[k], recv_sem=recv_sems.at[k],
                device_id=peer, device_id_type=MESH))
            recvs.append(pltpu.make_async_remote_copy(
                src_ref=src, dst_ref=landing, send_sem=send_sems.at[k], recv_sem=recv_sems.at[k],
                device_id=peer, device_id_type=MESH))
        return sends, recvs

    def start(self, *refs):
        for cp in self._descriptors(*refs)[0]:
            cp.start()

    def finish(self, *refs):
        sends, recvs = self._descriptors(*refs)
        for cp in recvs:
            cp.wait_recv()
        for cp in sends:
            cp.wait_send()


ANY_SPEC = pl.BlockSpec(memory_space=pl.ANY)


def _hosted_call(body, *, name, grid, in_specs, out_specs, out_shape, args, scratch_shapes=(), side=None,
                 num_scalar_prefetch=0, io_aliases=None):
    in_specs, out_specs, out_shape = list(in_specs), list(out_specs), list(out_shape)
    n_in, n_out = len(in_specs) + num_scalar_prefetch, len(out_specs)
    kernel_body = body
    aliases = dict(io_aliases or {})
    if side is not None:
        s_in, s_out = len(side.ins), len(side.out_shapes)
        aliases.update({n_in + i: n_out + o for i, o in side.aliases.items()})

        def kernel_body(*refs):
            ins, s_ins = refs[:n_in], refs[n_in:n_in + s_in]
            outs = refs[n_in + s_in:n_in + s_in + n_out]
            s_outs = refs[n_in + s_in + n_out:n_in + s_in + n_out + s_out]
            scratch, sems = refs[n_in + s_in + n_out + s_out:-2], refs[-2:]
            first = functools.reduce(jnp.logical_and, [pl.program_id(a) == 0 for a in range(len(grid))])
            last = functools.reduce(jnp.logical_and, [pl.program_id(a) == g - 1 for a, g in enumerate(grid)])

            @pl.when(first)
            def _():
                side.start(s_ins, s_outs, *sems)

            body(*ins, *outs, *scratch)

            @pl.when(last)
            def _():
                side.finish(s_ins, s_outs, *sems)

        in_specs += [ANY_SPEC] * s_in
        out_specs += [ANY_SPEC] * s_out
        out_shape += list(side.out_shapes)
        scratch_shapes = list(scratch_shapes) + [pltpu.SemaphoreType.DMA((side.n,)),
                                                 pltpu.SemaphoreType.DMA((side.n,))]
        args = tuple(args) + side.ins
    if num_scalar_prefetch:
        grid_spec = pltpu.PrefetchScalarGridSpec(num_scalar_prefetch=num_scalar_prefetch, grid=grid,
                                                 in_specs=in_specs, out_specs=out_specs,
                                                 scratch_shapes=list(scratch_shapes))
        outs = pl.pallas_call(kernel_body, name=name, grid_spec=grid_spec, out_shape=out_shape,
                              input_output_aliases=aliases, compiler_params=_cparams())(*args)
    else:
        outs = pl.pallas_call(kernel_body, name=name, grid=grid, in_specs=in_specs, out_specs=out_specs,
                              out_shape=out_shape, scratch_shapes=list(scratch_shapes),
                              input_output_aliases=aliases, compiler_params=_cparams())(*args)
    return tuple(outs[:n_out]), tuple(outs[n_out:])


def _run_exchange(ex, *, name):
    s_in = len(ex.ins)

    def body(*refs):
        ins, outs, sems = refs[:s_in], refs[s_in:-2], refs[-2:]
        ex.start(ins, outs, *sems)
        ex.finish(ins, outs, *sems)

    outs = pl.pallas_call(
        body, name=name, out_shape=list(ex.out_shapes),
        in_specs=[ANY_SPEC] * s_in, out_specs=[ANY_SPEC] * len(ex.out_shapes),
        scratch_shapes=[pltpu.SemaphoreType.DMA((ex.n,)), pltpu.SemaphoreType.DMA((ex.n,))],
        input_output_aliases=ex.aliases,
    )(*ex.ins)
    return tuple(outs)


def _other_chips(px, py):
    return [(px, 1 - py), (1 - px, py), (1 - px, 1 - py)]


def _gather_spread(w_flat, halves=True):
    rows, w = w_flat.shape
    hr = rows // 2 if halves else rows

    def build(ins, outs):
        px, py, pc = _position()
        mine = ins[0].at[pl.ds(pc * hr, hr)] if halves else ins[0]
        me = 4 * px + 2 * py + pc
        plan = [((px, py, 1 - pc), me ^ 1)]
        plan += [((qx, qy, pc), 4 * qx + 2 * qy + pc) for qx, qy in _other_chips(px, py)]
        return [(mine, outs[0].at[me], peer, outs[0].at[their]) for peer, their in plan]

    return _Exchange([w_flat], [jax.ShapeDtypeStruct((N_DEV, hr, w), w_flat.dtype)], 4, build)


def _gather_pass_on(gath):
    def build(ins, outs):
        px, py, pc = _position()
        out = []
        for qx, qy in _other_chips(px, py):
            blk = 4 * qx + 2 * qy + pc
            out.append((outs[0].at[blk], outs[0].at[blk], (px, py, 1 - pc), outs[0].at[blk ^ 1]))
        return out

    return _Exchange([gath], [jax.ShapeDtypeStruct(gath.shape, gath.dtype)], 3, build, aliases={0: 0})


def _rs_halves(g):
    n, rows, w = g.shape
    hr = rows // 2

    def build(ins, outs):
        px, py, pc = _position()
        return [(ins[0].at[:, pl.ds((1 - pc) * hr, hr), :], outs[0], (px, py, 1 - pc), outs[0])]

    return _Exchange([g], [jax.ShapeDtypeStruct((n, hr, w), g.dtype)], 1, build)


def _rs_chips(sb):
    def build(ins, outs):
        px, py, pc = _position()
        return [(ins[0].at[j], outs[0].at[j], (qx, qy, pc), outs[0].at[j])
                for j, (qx, qy) in enumerate(_other_chips(px, py))]

    return _Exchange([sb], [jax.ShapeDtypeStruct(sb.shape, sb.dtype)], 3, build)


def _rs_complete(buf):
    def build(ins, outs):
        px, py, pc = _position()
        return [(outs[0].at[pc], outs[0].at[pc], (px, py, 1 - pc), outs[0].at[1 - pc])]

    return _Exchange([buf], [jax.ShapeDtypeStruct(buf.shape, buf.dtype)], 1, build, aliases={0: 0})


def _rs_partial(g, recv, ids, *, name, tr=128):
    _, rows, w = g.shape
    hr = rows // 2
    nb = hr // tr

    def body(ids_ref, g_ref, r_ref, o_ref):
        o_ref[0] = (g_ref[0] + r_ref[0]).astype(BF16)

    grid_spec = pltpu.PrefetchScalarGridSpec(
        num_scalar_prefetch=1, grid=(3, nb),
        in_specs=[pl.BlockSpec((1, tr, w), lambda j, i, ids: (ids[1] ^ (j + 1), ids[0] * nb + i, 0)),
                  pl.BlockSpec((1, tr, w), lambda j, i, ids: (ids[1] ^ (j + 1), i, 0))],
        out_specs=pl.BlockSpec((1, tr, w), lambda j, i, ids: (j, i, 0)))
    return pl.pallas_call(
        body, name=name, grid_spec=grid_spec,
        out_shape=jax.ShapeDtypeStruct((3, hr, w), BF16),
        compiler_params=_cparams(),
    )(ids, g, recv)


def _rs_total(g, recv, got, ids, *, name, tr=128):
    _, rows, w = g.shape
    hr = rows // 2
    nb = hr // tr

    def body(ids_ref, g_ref, r_ref, got_ref, o_ref):
        acc = g_ref[0] + r_ref[0]
        for j in range(3):
            acc = acc + got_ref[j].astype(F32)
        o_ref[0] = acc

    grid_spec = pltpu.PrefetchScalarGridSpec(
        num_scalar_prefetch=1, grid=(nb,),
        in_specs=[pl.BlockSpec((1, tr, w), lambda i, ids: (ids[1], ids[0] * nb + i, 0)),
                  pl.BlockSpec((1, tr, w), lambda i, ids: (ids[1], i, 0)),
                  pl.BlockSpec((3, tr, w), lambda i, ids: (0, i, 0))],
        out_specs=pl.BlockSpec((1, tr, w), lambda i, ids: (ids[0], i, 0)))
    return pl.pallas_call(
        body, name=name, grid_spec=grid_spec,
        out_shape=jax.ShapeDtypeStruct((2, hr, w), F32),
        compiler_params=_cparams(),
    )(ids, g, recv, got)


class _ReduceScatter:
    def __init__(self, g, ids, tag):
        self.g, self.ids, self.tag, self.stage, self.result = g, ids, tag, 0, None

    def next_exchange(self):
        if self.stage == 0:
            return _rs_halves(self.g)
        if self.stage == 1:
            return _rs_chips(self.sb)
        return _rs_complete(self.buf)

    def done(self, outs):
        if self.stage == 0:
            self.recv = outs[0]
            hr = self.recv.shape[1]
            self.tr = max(t for t in range(16, 513, 16) if hr % t == 0)
            self.sb = _rs_partial(self.g, self.recv, self.ids, name=f"{self.tag}_partial", tr=self.tr)
        elif self.stage == 1:
            self.buf = _rs_total(self.g, self.recv, outs[0], self.ids, name=f"{self.tag}_total", tr=self.tr)
        else:
            _, hr, w = outs[0].shape
            self.result = outs[0].reshape(2 * hr, w)
        self.stage += 1

    def finish_alone(self):
        names = ("halves", "chips", "complete")
        while self.stage < 3:
            self.done(_run_exchange(self.next_exchange(), name=f"{self.tag}_{names[self.stage]}"))
        return self.result


def _flat_rows():
    used = sum(r for _, r in FSDP_SECTIONS)
    return used, -(-used // ROW_ALIGN) * ROW_ALIGN


def _cols_to_chunks(full):
    rows, cols = full.shape
    t = full.reshape(rows, N_CHIPS, cols // N_CHIPS).transpose(1, 0, 2)
    return t.reshape(N_CHIPS, -1, FLAT_W)


def _chunks_to_cols(chunks, rows, cols):
    return chunks.reshape(N_CHIPS, rows, cols // N_CHIPS).transpose(1, 0, 2).reshape(rows, cols)


def _pad_heads(w, real):
    lead = w.shape[:-1]
    t = w.reshape(lead + (HEADS, real))
    t = jnp.pad(t, [(0, 0)] * len(lead) + [(0, 0), (0, HEAD_PAD - real)])
    return t.reshape(lead + (HEADS * HEAD_PAD,))


def _unpad_heads(w, real):
    lead = w.shape[:-1]
    return w.reshape(lead + (HEADS, HEAD_PAD))[..., :real].reshape(lead + (HEADS * real,))


def _pad_value_lanes(w, axis):
    w = jnp.moveaxis(w, axis, -1)
    lead = w.shape[:-1]
    t = w.reshape(lead + (HEADS, 64))
    t = jnp.pad(t, [(0, 0)] * len(lead) + [(0, 0), (HEAD_PAD - 64, 0)])
    return jnp.moveaxis(t.reshape(lead + (HEADS * HEAD_PAD,)), -1, axis)


def _unpad_value_lanes(w, axis):
    w = jnp.moveaxis(w, axis, -1)
    lead = w.shape[:-1]
    t = w.reshape(lead + (HEADS, HEAD_PAD))[..., HEAD_PAD - 64:]
    return jnp.moveaxis(t.reshape(lead + (HEADS * 64,)), -1, axis)


def _pad_w_in_t(wt):
    z = jnp.zeros((NOPE, wt.shape[1]), wt.dtype)
    z2 = jnp.zeros((HEAD_PAD - NOPE - ROPE, wt.shape[1]), wt.dtype)
    return jnp.concatenate([wt[:1408], z, wt[1408:], z2], axis=0)


def _unpad_w_in_t(wt):
    return jnp.concatenate([wt[:1408], wt[1408 + NOPE:1408 + NOPE + ROPE]], axis=0)


def _rope_tables(positions):
    freqs = ROPE_THETA ** (-jnp.arange(0, ROPE, 2, dtype=F32) / ROPE)
    ang = positions.astype(F32)[..., None] * freqs
    cos, sin = jnp.cos(ang), jnp.sin(ang)
    lead = cos.shape[:-1]
    ones = jnp.ones(lead + (NOPE,), F32)
    zeros_n = jnp.zeros(lead + (NOPE,), F32)
    zeros_p = jnp.zeros(lead + (HEAD_PAD - NOPE - ROPE,), F32)
    ctab = jnp.concatenate([ones, cos, cos, zeros_p], axis=-1)
    stab = jnp.concatenate([zeros_n, -sin, sin, zeros_p], axis=-1)
    return ctab, stab


def _mix_weights(full):
    return dict(
        w_in_t=_pad_w_in_t(full["w_in_t"]),
        w_uq=_pad_heads(full["mla_w_uq"], NOPE + ROPE),
        w_ukv=full["mla_w_ukv"],
        w_out=jnp.concatenate([_pad_value_lanes(full["w_out"][D_GMLP:], 0), full["w_out"][:D_GMLP]], axis=0),
    )


def _small_weights(p, l):
    ws = p["gmlp_ws"][l]
    tril = jnp.tril(jnp.ones((CHUNK, CHUNK), bool))
    bs = p["gmlp_bs"][l]
    bexp = jnp.repeat(bs.reshape(GROUPS // 2, 2, CHUNK).transpose(0, 2, 1), GROUP_DIM, axis=2)
    return dict(
        ws=ws,
        wst=jnp.where(tril[None], ws, 0.0).transpose(0, 2, 1).astype(BF16),
        bexp=bexp,
        g_mix=p["norm_mix_g"][l][None],
        g_ffn=p["norm_ffn_g"][l][None],
        g_q=p["mla_q_norm_g"][l][None],
        g_kv=p["mla_kv_norm_g"][l][None],
        g_og=p["out_norm_gmlp_g"][l][None],
        g_oa=_pad_value_lanes(p["out_norm_mla_g"][l], 0)[None],
    )


def _local_step(x3, target3, positions, mods, final_g, plan):
    bsz, seq, d = x3.shape
    tok = bsz * seq
    tmt = min(512, seq)
    tmk = min(1024, seq)
    tmw = min(2048, tok)
    chunk = (None, None, FLAT_W, FLAT_W)
    ff_grad_shape = (N_CHIPS, 2 * FLAT_W, FLAT_W)
    ctab, stab = _rope_tables(positions)
    lw = [None] * DEPTH

    def flat(t):
        return t.reshape(tok, t.shape[-1])

    def cube(t):
        return t.reshape(bsz, seq, t.shape[-1])

    def carrying(l, tag, fn, *args, **kw):
        side = plan.host(l, tag)
        if side is None:
            return fn(*args, **kw)
        res, side_outs = fn(*args, side=side, **kw)
        plan.hosted(l, tag, side_outs)
        return res

    saved = []
    x = x3
    for l in range(DEPTH):
        lw[l] = plan.layer(l)
        w, mod = lw[l], mods[l]
        h1 = carrying(l, "fwd_normmod1", _normmod_fwd, x, w["g_mix"], mod, SHIFT1, SCALE1, name=f"l{l}_normmod1")
        z = cube(_mm(flat(h1), w["w_in_t"], dims="nt", name=f"l{l}_w_in", tm=tmt, tn=D_IN_PAD, tk=d,
                     out_dtypes=(BF16,)))
        yg = _gmlp_fwd(z, w["ws"], w["bexp"], w["g_og"], name=f"l{l}_gmlp_fwd")
        q, kv, kp = _mla_prep_fwd(z, w["g_q"], w["g_kv"], w["w_uq"], w["w_ukv"], ctab, stab, name=f"l{l}_mla_prep")
        o, lse = carrying(l, "fwd_attn", _attn_fwd, q, kv, kp, name=f"l{l}_attn_fwd")
        y = _onorm_fwd(o, yg, w["g_oa"], name=f"l{l}_onorm_fwd")

        def out_epi(po, xv, gm):
            return po, xv + gm[0][GATE1:GATE1 + 1, :] * po

        po, x_mid = carrying(l, "fwd_out_a", _mm, flat(y), w["w_out"], dims="nn", name=f"l{l}_w_out",
                             tm=tmt, tn=d, tk=y.shape[-1], out_dtypes=(BF16, F32), epilogue=out_epi,
                             extras=(flat(x), mod), extra_specs=(None, _mod_spec(tmt, d, seq)))
        x_mid = cube(x_mid)
        h2 = _normmod_fwd(x_mid, w["g_ffn"], mod, SHIFT2, SCALE2, name=f"l{l}_normmod2")

        def act_epi(acc):
            r = jnp.maximum(acc, 0.0)
            return (r * r,)

        r = carrying(l, "fwd_ff1", _mm, flat(h2), w["ff"], dims="nn", name=f"l{l}_w_ff1", tm=tmw, tn=FLAT_W,
                     tk=d, out_dtypes=(BF16,), epilogue=act_epi, weights_outer=True, n=D_FF,
                     b_block=(chunk, lambda i, j, k: (j, 0, 0, 0)))

        def ff2_epi(acc, xv, gm):
            return acc, xv + gm[0][GATE2:GATE2 + 1, :] * acc

        f, x_out = carrying(l, "fwd_ff2", _mm, r, w["ff"], dims="nn", name=f"l{l}_w_ff2", tm=tmk, tn=d, tk=FLAT_W,
                            out_dtypes=(BF16, F32), epilogue=ff2_epi, extras=(flat(x_mid), mod),
                            extra_specs=(None, _mod_spec(tmk, d, seq)), n=d,
                            b_block=(chunk, lambda i, j, k: (k, 1, 0, 0)))
        saved.append(dict(x_in=x, h1=h1, z=z, q=q, kv=kv, kp=kp, o=o, lse=lse, y=y, po=cube(po),
                          x_mid=x_mid, h2=h2, r=r, f=cube(f)))
        x = cube(x_out)

    grads = [dict() for _ in range(DEPTH)]
    dmods = [None] * DEPTH
    top = DEPTH - 1
    node = _resnode_bwd(x, final_g[None], name="final_loss_bwd", target3=target3,
                        branch3=saved[top]["f"], mod_gate=mods[top], gate_row=GATE2)
    loss_part = node["loss"][0, 0]
    d_final_g = node["dg"][0]
    plan.scalars(loss_part, d_final_g)
    for l in range(DEPTH - 1, -1, -1):
        w, mod, s = lw[l], mods[l], saved[l]
        dx_out, dfb, dgate2 = node["dx"], flat(node["dbr"]), node["dgate"][:, 0]

        def dact_epi(acc, rv):
            return (acc * (2.0 * jnp.sqrt(rv.astype(F32))),)

        da = carrying(l, "bwd_d_r", _mm, dfb, w["ff"], dims="nt", name=f"l{l}_d_r", tm=tmw, tn=FLAT_W, tk=d,
                      out_dtypes=(BF16,), epilogue=dact_epi, extras=(s["r"],), weights_outer=True, n=D_FF,
                      b_block=(chunk, lambda i, j, k: (j, 1, 0, 0)))
        g_ff = carrying(l, "bwd_dw_ff2", _mm, s["r"], dfb, dims="tn", name=f"l{l}_dw_ff2", tm=FLAT_W, tn=d,
                        tk=1024, out_into=(ff_grad_shape, (None, FLAT_W, FLAT_W), lambda i, j, k: (i, 1, 0), None))
        g_ff = carrying(l, "bwd_dw_ff1", _mm, flat(s["h2"]), da, dims="tn", name=f"l{l}_dw_ff1", tm=d, tn=FLAT_W,
                        tk=1024, out_into=(ff_grad_shape, (None, FLAT_W, FLAT_W), lambda i, j, k: (j, 0, 0), g_ff))
        plan.ff_grads(l, g_ff)
        dh2 = carrying(l, "bwd_d_h2", _mm, da, w["ff"], dims="nt", name=f"l{l}_d_h2", tm=tmk, tn=d, tk=FLAT_W,
                       n=d, b_block=(chunk, lambda i, j, k: (k, 0, 0, 0)), out_dtypes=(BF16,))
        node = _resnode_bwd(s["x_mid"], w["g_ffn"], name=f"l{l}_resnode_ffn", dh3=cube(dh2), dres3=dx_out,
                            mod_nm=mod, rows=(SHIFT2, SCALE2), branch3=s["po"], mod_gate=mod, gate_row=GATE1)
        grads[l]["norm_ffn_g"] = node["dg"][0]
        dshift2, dscale2 = node["dnm"][:, 0], node["dnm"][:, 1]
        dx_mid, dpo, dgate1 = node["dx"], flat(node["dbr"]), node["dgate"][:, 0]

        wy = s["y"].shape[-1]
        dy = cube(carrying(l, "bwd_d_y", _mm, dpo, w["w_out"], dims="nt", name=f"l{l}_d_y", tm=tmt, tn=wy, tk=d,
                           out_dtypes=(BF16,)))
        dw_out = carrying(l, "bwd_dw_out", _mm, flat(s["y"]), dpo, dims="tn", name=f"l{l}_dw_out", tm=wy // 3,
                          tn=d, tk=1024)
        hw = HEADS * HEAD_PAD
        grads[l]["w_out"] = jnp.concatenate([dw_out[hw:], _unpad_value_lanes(dw_out[:hw], 0)], axis=0)

        dz, dws, dbs, dg_og = _gmlp_bwd(s["z"], dy, w["ws"], w["wst"], w["bexp"], w["g_og"],
                                        name=f"l{l}_gmlp_bwd", dy_col=hw // D_GMLP)
        grads[l]["gmlp_ws"], grads[l]["gmlp_bs"], grads[l]["out_norm_gmlp_g"] = dws, dbs, dg_og[0]

        do, dl, dg_oa = _onorm_bwd(s["o"], dy, w["g_oa"], name=f"l{l}_onorm_bwd")
        grads[l]["out_norm_mla_g"] = _unpad_value_lanes(dg_oa[0], 0)
        dq, dk, dv = carrying(l, "bwd_attn_dkv", _attn_bwd, s["q"], s["kv"], s["kp"], do, s["lse"], dl,
                              name=f"l{l}_attn_bwd")
        dz, cq, dqb, ckv, dkvb, dg_q, dg_kv = _mla_prep_bwd(
            s["z"], dz, dq, dk, dv, w["g_q"], w["g_kv"], w["w_uq"], w["w_ukv"], ctab, stab,
            name=f"l{l}_mla_prep_bwd")
        grads[l]["mla_q_norm_g"], grads[l]["mla_kv_norm_g"] = dg_q[0], dg_kv[0]
        dw_uq = carrying(l, "bwd_dw_uq", _mm, flat(cq), flat(dqb), dims="tn", name=f"l{l}_dw_uq", tm=Q_RANK,
                         tn=1024, tk=1024)
        grads[l]["mla_w_uq"] = _unpad_heads(dw_uq, NOPE + ROPE)
        grads[l]["mla_w_ukv"] = _mm(flat(ckv), flat(dkvb), dims="tn", name=f"l{l}_dw_ukv", tm=KV_RANK, tn=1024, tk=1024)

        grads[l]["w_in_t"] = _unpad_w_in_t(_mm(flat(dz), flat(s["h1"]), dims="tn", name=f"l{l}_dw_in",
                                               tm=D_IN_PAD // 2, tn=d, tk=1024))
        plan.layer_grads(l, grads[l])
        dh1 = carrying(l, "bwd_d_h1", _mm, flat(dz), w["w_in_t"], dims="nn", name=f"l{l}_d_h1", tm=tmt, tn=d,
                       tk=D_IN_PAD, out_dtypes=(BF16,))
        below = dict(branch3=saved[l - 1]["f"], mod_gate=mods[l - 1], gate_row=GATE2) if l > 0 else {}
        node = carrying(l, "bwd_resnode_mix", _resnode_bwd, s["x_in"], w["g_mix"], name=f"l{l}_resnode_mix",
                        dh3=cube(dh1), dres3=dx_mid, mod_nm=mod, rows=(SHIFT1, SCALE1), **below)
        grads[l]["norm_mix_g"] = node["dg"][0]
        dshift1, dscale1 = node["dnm"][:, 0], node["dnm"][:, 1]
        dmods[l] = jnp.stack([dshift1, dscale1, dgate1, dshift2, dscale2, dgate2], axis=1)
        plan.layer_done(l)
    return node["dx"], dmods


W_NAMES = ("w_ada", "b_ada", "norm_mix_g", "w_in", "gmlp_ws", "gmlp_bs", "mla_q_norm_g", "mla_kv_norm_g",
           "mla_w_uq", "mla_w_ukv", "out_norm_gmlp_g", "out_norm_mla_g", "w_out", "norm_ffn_g", "w_ff1", "w_ff2",
           "final_norm_g")
FLAT_KEY = {"w_in": "w_in", "w_uq": "mla_w_uq", "w_ukv": "mla_w_ukv", "w_out": "w_out", "w_ff1": "w_ff1",
            "w_ff2": "w_ff2"}
COL_SHARDED = ("w_in", "w_uq", "w_ukv", "w_ff1")
FULL_SHAPE = {"w_in": (D_MODEL, D_IN), "w_uq": (Q_RANK, HEADS * (NOPE + ROPE)), "w_ukv": (KV_RANK, HEADS * 128),
              "w_out": (D_MODEL, D_MODEL), "w_ff1": (D_MODEL, D_FF), "w_ff2": (D_FF, D_MODEL)}
SMALL_LAYER_NAMES = ("norm_mix_g", "gmlp_ws", "gmlp_bs", "mla_q_norm_g", "mla_kv_norm_g", "out_norm_gmlp_g",
                     "out_norm_mla_g", "norm_ffn_g")


def _silu(v):
    return v * (1.0 / (1.0 + jnp.exp(-v)))


class _CommPlan:
    FWD = {"fwd_attn": ("ff", 0, "spread"), "fwd_out_a": ("ff", 0, "pass"),
           "fwd_ff1": ("mix", 1, "spread"), "fwd_ff2": ("mix", 1, "pass")}
    BWD = {"bwd_d_r": ("mix", 1), "bwd_dw_ff2": ("mix", 1), "bwd_dw_ff1": ("mix", 1),
           "bwd_d_h2": ("ff", 0), "bwd_attn_dkv": ("ff", 0), "bwd_dw_uq": ("ff", 0)}
    BWD_LAST = {"bwd_d_h1": ("mix", 0), "bwd_resnode_mix": ("mix", 0)}
    SMALL = {"bwd_d_y": "spread", "bwd_dw_out": "pass"}

    def __init__(self, weights, ids, dev, core):
        self.weights, self.ids, self.dev, self.core = weights, ids, dev, core
        self.used, self.rows = _flat_rows()
        self.flat = {("mix", l): self._flat_mix(l) for l in range(DEPTH)}
        self.flat.update({("ff", l): jnp.concatenate([weights["w_ff1"][l], weights["w_ff2"][l]], axis=0).astype(BF16)
                          for l in range(DEPTH)})
        self.lw, self.rs, self.grads, self.spread = {}, {}, {}, {}
        self.small_vec, self.small_sum, self.small_spread, self.extra = {}, {}, None, {}
        self.lw = {l: _small_weights(weights, l) for l in range(DEPTH)}

    def _flat_mix(self, l):
        pieces = []
        for nm, _ in FSDP_SECTIONS:
            shard = self.weights[FLAT_KEY[nm]][l]
            pieces.append(shard.T if nm == "w_in" else shard.reshape(-1, FLAT_W))
        pieces.append(jnp.zeros((self.rows - self.used, FLAT_W), F32))
        return jnp.concatenate(pieces, axis=0).astype(BF16)

    def _arrived(self, group, l, gath):
        flat = self.flat[group, l]
        hr = flat.shape[0] // 2
        mine = lax.dynamic_slice(flat, (self.core * hr, 0), (hr, FLAT_W))
        gath = lax.dynamic_update_slice(gath, mine[None], (self.dev, 0, 0))
        if group == "ff":
            self.lw[l]["ff"] = gath.reshape(N_CHIPS, 2, hr, FLAT_W)
            return
        w_gath = gath.reshape(N_CHIPS, self.rows, FLAT_W)
        full, off = {}, 0
        for nm, nrows in FSDP_SECTIONS:
            sec = w_gath[:, off:off + nrows]
            off += nrows
            rows, cols = FULL_SHAPE[nm]
            if nm == "w_in":
                full["w_in_t"] = sec.reshape(cols, rows)
            else:
                full[FLAT_KEY[nm]] = (_chunks_to_cols(sec, rows, cols) if nm in COL_SHARDED
                                      else sec.reshape(rows, cols))
        self.lw[l].update(_mix_weights(full))

    def layer(self, l):
        return self.lw[l]

    def host(self, l, tag):
        if tag == "fwd_normmod1":
            return _gather_spread(self.flat["mix", 0]) if l == 0 else None
        if tag in self.FWD:
            group, ahead, what = self.FWD[tag]
            if l + ahead >= DEPTH:
                return None
            return _gather_spread(self.flat[group, l + ahead]) if what == "spread" else _gather_pass_on(self.spread[group])
        if tag in self.SMALL:
            if l + 1 not in self.small_vec:
                return None
            if self.SMALL[tag] == "spread":
                return _gather_spread(self.small_vec[l + 1], halves=False)
            return _gather_pass_on(self.small_spread)
        rs = self._rs_for(l, tag)
        return None if rs is None or rs.stage > 2 else rs.next_exchange()

    def _rs_for(self, l, tag):
        if tag in self.BWD_LAST:
            return self.rs.get(self.BWD_LAST[tag]) if l == 0 else None
        group, ahead = self.BWD[tag]
        return self.rs.get((group, l + ahead))

    def hosted(self, l, tag, outs):
        if tag == "fwd_normmod1":
            self._arrived("mix", 0, _run_exchange(_gather_pass_on(outs[0]), name="l0_mix_gather_pass_on")[0])
        elif tag in self.FWD:
            group, ahead, what = self.FWD[tag]
            if what == "spread":
                self.spread[group] = outs[0]
            else:
                self._arrived(group, l + ahead, outs[0])
        elif tag in self.SMALL:
            if self.SMALL[tag] == "spread":
                self.small_spread = outs[0]
            else:
                self._small_arrived(l + 1, outs[0])
        else:
            self._rs_for(l, tag).done(outs)

    def ff_grads(self, l, g_ff):
        self.rs["ff", l] = _ReduceScatter(g_ff, self.ids, f"l{l}_ff_rs")

    def layer_grads(self, l, grads):
        self.grads[l] = grads
        pieces = []
        for nm, nrows in FSDP_SECTIONS:
            if nm == "w_in":
                pieces.append(grads["w_in_t"].reshape(N_CHIPS, nrows, FLAT_W))
                continue
            g = grads[FLAT_KEY[nm]]
            pieces.append(_cols_to_chunks(g) if nm in COL_SHARDED else g.reshape(N_CHIPS, nrows, FLAT_W))
        pieces.append(jnp.zeros((N_CHIPS, self.rows - self.used, FLAT_W), F32))
        self.rs["mix", l] = _ReduceScatter(jnp.concatenate(pieces, axis=1), self.ids, f"l{l}_mix_rs")

    def scalars(self, loss_part, d_final_g):
        self.extra = {0: [loss_part[None]]}
        self.extra.setdefault(DEPTH - 1, []).insert(0, d_final_g)

    def layer_done(self, l):
        if l == 0:
            self.rs["mix", 0].finish_alone()
        parts = [self.grads[l][nm].reshape(-1) for nm in SMALL_LAYER_NAMES] + self.extra.get(l, [])
        vec = jnp.concatenate(parts)
        rows = -(-vec.shape[0] // (8 * FLAT_W)) * 8
        self.small_vec[l] = jnp.pad(vec, (0, rows * FLAT_W - vec.shape[0])).reshape(rows, FLAT_W)
        if l == 0:
            (gath,) = _run_exchange(_gather_spread(self.small_vec[0], halves=False), name="l0_small_spread")
            self._small_arrived(0, _run_exchange(_gather_pass_on(gath), name="l0_small_pass_on")[0])

    def _small_arrived(self, l, gath):
        gath = lax.dynamic_update_slice(gath, self.small_vec[l][None], (self.dev, 0, 0))
        self.small_sum[l] = _sum_leading(gath, name=f"l{l}_small_sum").reshape(-1)

    def small_grads(self):
        out = {nm: [] for nm in SMALL_LAYER_NAMES}
        for l in range(DEPTH):
            off = 0
            for nm in SMALL_LAYER_NAMES:
                size = self.weights[nm][l].size
                out[nm].append(self.small_sum[l][off:off + size].reshape(self.weights[nm].shape[1:]))
                off += size
            if l == DEPTH - 1:
                final = self.small_sum[l][off:off + self.weights["final_norm_g"].size]
                off += final.shape[0]
            if l == 0:
                loss = self.small_sum[l][off]
        res = {nm: jnp.stack(parts, axis=0) for nm, parts in out.items()}
        res["final_norm_g"] = final
        return loss, res

    def mix_grads(self):
        per = {FLAT_KEY[nm]: [] for nm, _ in FSDP_SECTIONS}
        for l in range(DEPTH):
            shard, off = self.rs["mix", l].result, 0
            for nm, nrows in FSDP_SECTIONS:
                key = FLAT_KEY[nm]
                sec = shard[off:off + nrows]
                per[key].append(sec.T if nm == "w_in" else sec.reshape(self.weights[key].shape[1:]))
                off += nrows
        return {key: jnp.stack(parts, axis=0) for key, parts in per.items()}

    def ff_shards(self):
        return [self.rs["ff", l].result for l in range(DEPTH)]


def kernel(x, c, positions, w_ada, b_ada, norm_mix_g, w_in, gmlp_ws, gmlp_bs, mla_q_norm_g, mla_kv_norm_g, mla_w_uq, mla_w_ukv, out_norm_gmlp_g, out_norm_mla_g, w_out, norm_ffn_g, w_ff1, w_ff2, final_norm_g, loss_target, m_w_ada, m_b_ada, m_norm_mix_g, m_w_in, m_gmlp_ws, m_gmlp_bs, m_mla_q_norm_g, m_mla_kv_norm_g, m_mla_w_uq, m_mla_w_ukv, m_out_norm_gmlp_g, m_out_norm_mla_g, m_w_out, m_norm_ffn_g, m_w_ff1, m_w_ff2, m_final_norm_g, v_w_ada, v_b_ada, v_norm_mix_g, v_w_in, v_gmlp_ws, v_gmlp_bs, v_mla_q_norm_g, v_mla_kv_norm_g, v_mla_w_uq, v_mla_w_ukv, v_out_norm_gmlp_g, v_out_norm_mla_g, v_w_out, v_norm_ffn_g, v_w_ff1, v_w_ff2, v_final_norm_g):
    weights = dict(w_ada=w_ada, b_ada=b_ada, norm_mix_g=norm_mix_g, w_in=w_in, gmlp_ws=gmlp_ws, gmlp_bs=gmlp_bs,
                   mla_q_norm_g=mla_q_norm_g, mla_kv_norm_g=mla_kv_norm_g, mla_w_uq=mla_w_uq, mla_w_ukv=mla_w_ukv,
                   out_norm_gmlp_g=out_norm_gmlp_g, out_norm_mla_g=out_norm_mla_g, w_out=w_out,
                   norm_ffn_g=norm_ffn_g, w_ff1=w_ff1, w_ff2=w_ff2, final_norm_g=final_norm_g)
    mom_m = dict(zip(W_NAMES, (m_w_ada, m_b_ada, m_norm_mix_g, m_w_in, m_gmlp_ws, m_gmlp_bs, m_mla_q_norm_g,
                               m_mla_kv_norm_g, m_mla_w_uq, m_mla_w_ukv, m_out_norm_gmlp_g, m_out_norm_mla_g,
                               m_w_out, m_norm_ffn_g, m_w_ff1, m_w_ff2, m_final_norm_g)))
    mom_v = dict(zip(W_NAMES, (v_w_ada, v_b_ada, v_norm_mix_g, v_w_in, v_gmlp_ws, v_gmlp_bs, v_mla_q_norm_g,
                               v_mla_kv_norm_g, v_mla_w_uq, v_mla_w_ukv, v_out_norm_gmlp_g, v_out_norm_mla_g,
                               v_w_out, v_norm_ffn_g, v_w_ff1, v_w_ff2, v_final_norm_g)))
    bsz, seq, d = x.shape
    px, py, pc = _position()
    chip = 2 * px + py
    dev = 2 * chip + pc
    ids = jnp.stack([pc, chip]).astype(jnp.int32)
    n_ex = N_DEV * bsz
    ada_cols = w_ada.shape[-1]

    c_all = _allgather8(c.reshape(bsz * d // 128, 128), name="gather_c").reshape(n_ex, d)
    mod_parts = []
    for l in range(DEPTH):
        bias = lax.dynamic_slice(b_ada[l], (chip * ada_cols,), (ada_cols,))[None]
        mod_parts.append(_mm(c_all, w_ada, dims="nn", name=f"l{l}_mod", tm=n_ex, tn=ada_cols, tk=d, n=ada_cols,
                             b_block=((None, d, ada_cols), lambda i, j, k, l=l: (l, k, j)),
                             epilogue=lambda acc, bv: (acc + bv,), extras=(bias,),
                             extra_specs=(pl.BlockSpec((1, ada_cols), lambda i, j, k: (0, j)),), a_fn=_silu))
    mod_g = _allgather8(jnp.concatenate(mod_parts, axis=0), name="gather_mod")
    mod_g = mod_g.reshape(N_CHIPS, 2, DEPTH, n_ex, ada_cols)[:, 0]
    mod_full = mod_g.transpose(1, 2, 0, 3).reshape(DEPTH, n_ex, N_CHIPS * ada_cols)
    mod_mine = lax.dynamic_slice(mod_full, (0, dev * bsz, 0), (DEPTH, bsz, N_MOD * d))
    mod_mine = jnp.pad(mod_mine.reshape(DEPTH, bsz, N_MOD, d), ((0, 0), (0, 0), (0, MOD_ROWS - N_MOD), (0, 0)))
    mods = [mod_mine[l] for l in range(DEPTH)]

    plan = _CommPlan(weights, ids, dev, pc)
    grad_x, dmods = _local_step(x, loss_target, positions, mods, final_norm_g, plan)
    grad = plan.mix_grads()

    loss, small = plan.small_grads()
    grad.update(small)

    dmod = jnp.stack(dmods, axis=1).reshape(bsz * DEPTH * N_MOD, d)
    dmod_all = _allgather8(dmod, name="gather_dmod").reshape(n_ex, DEPTH, N_MOD * d)
    gw, gb = [], []
    for l in range(DEPTH):
        dm = dmod_all[:, l]
        dm_cols = lax.dynamic_slice(dm, (0, chip * ada_cols), (n_ex, ada_cols))
        gw.append(_mm(c_all, dm_cols, dims="tn", name=f"l{l}_dw_ada", tm=d, tn=ada_cols, tk=n_ex, a_fn=_silu,
                      out_into=(w_ada.shape, (None, d, ada_cols), lambda i, j, k, l=l: (l, i, j),
                                gw[-1] if gw else None)))
        gb.append(_sum_leading(dm.reshape(n_ex, N_MOD * d // FLAT_W, FLAT_W), name=f"l{l}_db_ada").reshape(-1))
    grad["w_ada"] = gw[-1]
    grad["b_ada"] = jnp.stack(gb, axis=0)

    delta, new_m, new_v = {}, {}, {}
    ff_bufs = plan.ff_shards()
    for nm, row_off in (("w_ff1", 0), ("w_ff2", FLAT_W)):
        grad[nm], delta[nm], new_m[nm], new_v[nm] = _adamw_layers(
            weights[nm], mom_m[nm], mom_v[nm], ff_bufs, row_off, name=f"adamw_{nm}")
    for nm in W_NAMES:
        if nm not in delta:
            delta[nm], new_m[nm], new_v[nm] = _adamw(weights[nm], grad[nm], mom_m[nm], mom_v[nm],
                                                     name=f"adamw_{nm}")
    return (loss, grad_x, *[grad[nm] for nm in W_NAMES], *[delta[nm] for nm in W_NAMES],
            *[new_m[nm] for nm in W_NAMES], *[new_v[nm] for nm in W_NAMES])
```

```python
import functools
import math

import jax
import jax.numpy as jnp
from jax import lax
from jax.experimental import pallas as pl
from jax.experimental.pallas import tpu as pltpu

F32 = jnp.float32
BF16 = jnp.bfloat16

D_MODEL = 1024
DEPTH = 2
D_GMLP = 512
GROUPS = 8
GROUP_DIM = 64
CHUNK = 128
HEADS = 8
NOPE = 64
ROPE = 32
HEAD_PAD = 128
Q_RANK = 256
KV_RANK = 128
D_FF = 4096
N_MOD = 6
MOD_ROWS = 8
EPS = 1e-6
ROPE_THETA = 10000.0
D_IN = 1440
D_IN_PAD = 1536
ATTN_SCALE = (NOPE + ROPE) ** -0.5
LOG2E = math.log2(math.e)
SCALE_LOG2 = ATTN_SCALE * LOG2E
N_CHIPS = 4
N_DEV = 8

ADAM_LR = 0.001
ADAM_B1 = 0.9
ADAM_B2 = 0.999
ADAM_EPS = 1e-08
ADAM_WD = 0.01
ADAM_STEP = 10

VMEM_LIMIT = 48 * 1024 * 1024
FLAT_W = 1024
ROW_ALIGN = 256

NN = (((1,), (0,)), ((), ()))
NT = (((1,), (1,)), ((), ()))
TN = (((0,), (0,)), ((), ()))
MESH = pl.DeviceIdType.MESH

SHIFT1, SCALE1, GATE1, SHIFT2, SCALE2, GATE2 = range(6)

FSDP_SECTIONS = (("w_out", 256), ("w_in", 360), ("w_uq", 48), ("w_ukv", 32))


def _cparams(vmem=VMEM_LIMIT):
    return pltpu.CompilerParams(vmem_limit_bytes=vmem)


def _dot(a, b, dims=NN):
    return lax.dot_general(a, b, dims, preferred_element_type=F32)


def _iota(shape, axis):
    return lax.broadcasted_iota(jnp.int32, shape, axis)


def _gelu(x):
    k = math.sqrt(2.0 / math.pi)
    return 0.5 * x * (1.0 + jnp.tanh(k * (x + 0.044715 * (x * x * x))))


def _gelu_and_grad(x):
    k = math.sqrt(2.0 / math.pi)
    x2 = x * x
    t = jnp.tanh(k * (x + 0.044715 * (x2 * x)))
    half = 0.5 * (1.0 + t)
    return x * half, half + 0.5 * x * (1.0 - t * t) * (k * (1.0 + 3.0 * 0.044715 * x2))


def _rms_fwd(x, g, n):
    r = lax.rsqrt(jnp.sum(x * x, axis=-1, keepdims=True) * (1.0 / n) + EPS)
    return x * r * g


def _rms_bwd(x, g, dy, n):
    r = lax.rsqrt(jnp.sum(x * x, axis=-1, keepdims=True) * (1.0 / n) + EPS)
    xh = x * r
    dxh = dy * g
    dx = r * (dxh - xh * (jnp.sum(dxh * xh, axis=-1, keepdims=True) * (1.0 / n)))
    dg = jnp.sum(dy * xh, axis=0, keepdims=True)
    return dx, dg


def _pick_rows(rows, limit):
    if rows <= limit:
        return rows
    for t in range(limit, 7, -8):
        if rows % t == 0:
            return t
    return rows


def _mm(a, b, *, dims, name, tm=512, tn=1024, tk=1024, out_dtypes=(F32,), epilogue=None,
        extras=(), extra_specs=(), a_fn=None, weights_outer=False, side=None, b_block=None, n=None,
        out_into=None):
    if dims == "tn":
        kk, m = a.shape
    else:
        m, kk = a.shape
    if n is None:
        n = b.shape[0] if dims == "nt" else b.shape[1]
    tm, tn, tk = min(tm, m), min(tn, n), min(tk, kk)
    assert m % tm == 0 and n % tn == 0 and kk % tk == 0, (name, a.shape, b.shape, tm, tn, tk)
    ni, nj, nk = m // tm, n // tn, kk // tk

    def spec(shape, pick):
        if weights_outer:
            return pl.BlockSpec(shape, lambda j, i, k: pick(i, j, k))
        return pl.BlockSpec(shape, pick)

    if dims == "tn":
        a_spec = spec((tk, tm), lambda i, j, k: (k, i))
    else:
        a_spec = spec((tm, tk), lambda i, j, k: (i, k))
    if b_block is not None:
        b_spec = spec(*b_block)
    elif dims == "nt":
        b_spec = spec((tn, tk), lambda i, j, k: (j, k))
    else:
        b_spec = spec((tk, tn), lambda i, j, k: (k, j))
    o_spec = spec((tm, tn), lambda i, j, k: (i, j))
    out_shape = [jax.ShapeDtypeStruct((m, n), dt) for dt in out_dtypes]
    out_specs = [o_spec] * len(out_dtypes)
    prev, io_aliases = (), {}
    if out_into is not None:
        full_shape, block, index, before = out_into
        assert len(out_dtypes) == 1 and not extras
        out_shape = [jax.ShapeDtypeStruct(full_shape, out_dtypes[0])]
        out_specs = [spec(block, index)]
        if before is not None:
            prev, io_aliases = (before,), {2: 0}
    assert not (weights_outer and extra_specs)
    dn = {"nn": NN, "nt": NT, "tn": TN}[dims]
    n_ex, n_out = len(extras), len(out_dtypes)
    e_specs = [o_spec if s is None else s for s in (tuple(extra_specs) + (None,) * n_ex)[:n_ex]]

    n_prev = len(prev)

    def body(*refs):
        a_ref, b_ref = refs[0], refs[1]
        e_refs = refs[2 + n_prev:2 + n_prev + n_ex]
        o_refs = refs[2 + n_prev + n_ex:2 + n_prev + n_ex + n_out]
        av = a_ref[...]
        if a_fn is not None:
            av = a_fn(av)
        part = _dot(av.astype(BF16), b_ref[...].astype(BF16), dn)

        def finish(acc):
            outs = (acc,) if epilogue is None else epilogue(acc, *[e[...] for e in e_refs])
            for o_ref, o in zip(o_refs, outs):
                o_ref[...] = o.astype(o_ref.dtype)

        if nk == 1:
            finish(part)
        else:
            acc_ref = refs[-1]
            k = pl.program_id(2)

            @pl.when(k == 0)
            def _():
                acc_ref[...] = part

            @pl.when(k > 0)
            def _():
                acc_ref[...] += part

            @pl.when(k == nk - 1)
            def _():
                finish(acc_ref[...])

    outs, side_outs = _hosted_call(
        body, name=name, grid=(nj, ni, nk) if weights_outer else (ni, nj, nk),
        in_specs=[a_spec, b_spec] + [ANY_SPEC] * n_prev + e_specs,
        out_specs=out_specs, out_shape=out_shape,
        scratch_shapes=[pltpu.VMEM((tm, tn), F32)] if nk > 1 else [],
        args=(a, b, *prev, *extras), side=side, io_aliases=io_aliases)
    res = outs[0] if n_out == 1 else outs
    return res if side is None else (res, side_outs)


def _mod_spec(tm, tn, seq):
    return pl.BlockSpec((1, MOD_ROWS, tn), lambda i, j, k: ((i * tm) // seq, 0, j))


def _normmod_fwd(x3, g, mod, shift_row, scale_row, *, name, tb=512, side=None):
    bsz, seq, d = x3.shape
    tb = min(tb, seq)

    def body(x_ref, g_ref, mod_ref, h_ref):
        m = mod_ref[0]
        nrm = _rms_fwd(x_ref[0], g_ref[...], d)
        h = nrm * (1.0 + m[scale_row:scale_row + 1, :]) + m[shift_row:shift_row + 1, :]
        h_ref[0] = h.astype(BF16)

    outs, side_outs = _hosted_call(
        body, name=name, grid=(bsz, seq // tb),
        in_specs=[pl.BlockSpec((1, tb, d), lambda b, i: (b, i, 0)),
                  pl.BlockSpec((1, d), lambda b, i: (0, 0)),
                  pl.BlockSpec((1, MOD_ROWS, d), lambda b, i: (b, 0, 0))],
        out_specs=[pl.BlockSpec((1, tb, d), lambda b, i: (b, i, 0))],
        out_shape=[jax.ShapeDtypeStruct((bsz, seq, d), BF16)],
        args=(x3, g, mod), side=side)
    return outs[0] if side is None else (outs[0], side_outs)


def _pair_mean_exact(x, lo):
    s_lo = jnp.sum(jnp.where(lo, x, 0.0), axis=-1, keepdims=True)
    s_hi = jnp.sum(jnp.where(lo, 0.0, x), axis=-1, keepdims=True)
    return jnp.where(lo, s_lo, s_hi) * (1.0 / GROUP_DIM)


def _gmlp_pair_fwd(gv_p, w0, w1, bias, lo):
    mu = _pair_mean_exact(gv_p, lo)
    dlt = gv_p - mu
    var = _pair_mean_exact(dlt * dlt, lo)
    rstd = lax.rsqrt(var + EPS)
    vn = dlt * rstd
    vnb = vn.astype(BF16)
    mixed = jnp.where(lo, _dot(w0, vnb), _dot(w1, vnb)) + bias
    return vn, vnb, rstd, mixed


def _tril_bf16(w):
    t = w.shape[-1]
    return jnp.where(_iota((t, t), 1) <= _iota((t, t), 0), w, 0.0).astype(BF16)


def _gmlp_fwd(z3, ws, bexp, g_out, *, name):
    bsz, seq, _ = z3.shape
    nc = seq // CHUNK

    def body(u_ref, v_ref, ws_ref, b_ref, g_ref, y_ref):
        lo = _iota((CHUNK, 128), 1) < GROUP_DIM
        gu = _gelu(u_ref[0].astype(F32))
        gv = _gelu(v_ref[0].astype(F32))
        parts = []
        for p in range(GROUPS // 2):
            sl = slice(128 * p, 128 * p + 128)
            w0 = _tril_bf16(ws_ref[2 * p])
            w1 = _tril_bf16(ws_ref[2 * p + 1])
            _, _, _, mixed = _gmlp_pair_fwd(gv[:, sl], w0, w1, b_ref[p], lo)
            parts.append(gu[:, sl] * mixed)
        yg = jnp.concatenate(parts, axis=1)
        y_ref[0] = _rms_fwd(yg, g_ref[...], D_GMLP).astype(BF16)

    return pl.pallas_call(
        body, name=name, grid=(bsz, nc),
        in_specs=[pl.BlockSpec((1, CHUNK, D_GMLP), lambda b, i: (b, i, 0)),
                  pl.BlockSpec((1, CHUNK, D_GMLP), lambda b, i: (b, i, 1)),
                  pl.BlockSpec((GROUPS, CHUNK, CHUNK), lambda b, i: (0, 0, 0)),
                  pl.BlockSpec((GROUPS // 2, CHUNK, 128), lambda b, i: (0, 0, 0)),
                  pl.BlockSpec((1, D_GMLP), lambda b, i: (0, 0))],
        out_specs=pl.BlockSpec((1, CHUNK, D_GMLP), lambda b, i: (b, i, 0)),
        out_shape=jax.ShapeDtypeStruct((bsz, seq, D_GMLP), BF16),
        compiler_params=_cparams(),
    )(z3, z3, ws, bexp, g_out)


def _gmlp_bwd(z3, dyn3, ws, wst, bexp, g_out, *, name, dy_col):
    bsz, seq, _ = z3.shape
    nc = seq // CHUNK
    npair = GROUPS // 2

    def body(u_ref, v_ref, dy_ref, ws_ref, wst_ref, b_ref, g_ref, duv_ref, dws_ref, dbs_ref, dg_ref, dbacc):
        first = jnp.logical_and(pl.program_id(0) == 0, pl.program_id(1) == 0)
        last = jnp.logical_and(pl.program_id(0) == bsz - 1, pl.program_id(1) == nc - 1)

        @pl.when(first)
        def _():
            dws_ref[...] = jnp.zeros_like(dws_ref)
            dg_ref[...] = jnp.zeros_like(dg_ref)
            dbacc[...] = jnp.zeros_like(dbacc)

        lo = _iota((CHUNK, 128), 1) < GROUP_DIM
        tril = _iota((CHUNK, CHUNK), 1) <= _iota((CHUNK, CHUNK), 0)
        u = u_ref[0].astype(F32)
        v = v_ref[0].astype(F32)
        gu, dgu = _gelu_and_grad(u)
        gv, dgv_dv = _gelu_and_grad(v)
        fwd = []
        for p in range(npair):
            sl = slice(128 * p, 128 * p + 128)
            w0 = _tril_bf16(ws_ref[2 * p])
            w1 = _tril_bf16(ws_ref[2 * p + 1])
            fwd.append(_gmlp_pair_fwd(gv[:, sl], w0, w1, b_ref[p], lo))
        yg = jnp.concatenate([gu[:, 128 * p:128 * p + 128] * fwd[p][3] for p in range(npair)], axis=1)
        dyg, dg = _rms_bwd(yg, g_ref[...], dy_ref[0].astype(F32), D_GMLP)
        dg_ref[...] += dg
        du_parts, dv_parts = [], []
        for p in range(npair):
            sl = slice(128 * p, 128 * p + 128)
            vn, vnb, rstd, mixed = fwd[p]
            dyg_p = dyg[:, sl]
            dmixed = dyg_p * gu[:, sl]
            dbacc[p] += dmixed
            dm_lo = jnp.where(lo, dmixed, 0.0).astype(BF16)
            dm_hi = jnp.where(lo, 0.0, dmixed).astype(BF16)
            dws_ref[2 * p] += jnp.where(tril, _dot(dm_lo, vnb, NT), 0.0)
            dws_ref[2 * p + 1] += jnp.where(tril, _dot(dm_hi, vnb, NT), 0.0)
            dmb = dmixed.astype(BF16)
            dvn = jnp.where(lo, _dot(wst_ref[2 * p], dmb), _dot(wst_ref[2 * p + 1], dmb))
            dgv = rstd * (dvn - _pair_mean_exact(dvn, lo) - vn * _pair_mean_exact(dvn * vn, lo))
            dv_parts.append(dgv * dgv_dv[:, sl])
            du_parts.append(dyg_p * mixed * dgu[:, sl])
        duv_ref[0] = jnp.concatenate(du_parts + dv_parts, axis=1).astype(BF16)

        @pl.when(last)
        def _():
            sel = jnp.where(_iota((8, 128), 0) == 0, (_iota((8, 128), 1) < GROUP_DIM).astype(F32),
                            jnp.where(_iota((8, 128), 0) == 1, (_iota((8, 128), 1) >= GROUP_DIM).astype(F32), 0.0))
            for p in range(npair):
                dbs_ref[p] = lax.dot_general(sel, dbacc[p], NT, precision=lax.Precision.HIGHEST,
                                             preferred_element_type=F32)

    duv, dws, dbs, dg = pl.pallas_call(
        body, name=name, grid=(bsz, nc),
        in_specs=[pl.BlockSpec((1, CHUNK, D_GMLP), lambda b, i: (b, i, 0)),
                  pl.BlockSpec((1, CHUNK, D_GMLP), lambda b, i: (b, i, 1)),
                  pl.BlockSpec((1, CHUNK, D_GMLP), lambda b, i: (b, i, dy_col)),
                  pl.BlockSpec((GROUPS, CHUNK, CHUNK), lambda b, i: (0, 0, 0)),
                  pl.BlockSpec((GROUPS, CHUNK, CHUNK), lambda b, i: (0, 0, 0)),
                  pl.BlockSpec((npair, CHUNK, 128), lambda b, i: (0, 0, 0)),
                  pl.BlockSpec((1, D_GMLP), lambda b, i: (0, 0))],
        out_specs=[pl.BlockSpec((1, CHUNK, 2 * D_GMLP), lambda b, i: (b, i, 0)),
                   pl.BlockSpec((GROUPS, CHUNK, CHUNK), lambda b, i: (0, 0, 0)),
                   pl.BlockSpec((npair, 8, CHUNK), lambda b, i: (0, 0, 0)),
                   pl.BlockSpec((1, D_GMLP), lambda b, i: (0, 0))],
        out_shape=[jax.ShapeDtypeStruct((bsz, seq, D_IN_PAD), BF16),
                   jax.ShapeDtypeStruct((GROUPS, CHUNK, CHUNK), F32),
                   jax.ShapeDtypeStruct((npair, 8, CHUNK), F32),
                   jax.ShapeDtypeStruct((1, D_GMLP), F32)],
        scratch_shapes=[pltpu.VMEM((npair, CHUNK, 128), F32)],
        compiler_params=_cparams(),
    )(z3, z3, dyn3, ws, wst, bexp, g_out)
    return duv, dws, dbs[:, :2, :].reshape(GROUPS, CHUNK), dg


def _partner(x):
    width = x.shape[-1]
    lane = _iota(x.shape, x.ndim - 1) % HEAD_PAD
    up = pltpu.roll(x, width - ROPE // 2, x.ndim - 1)
    down = pltpu.roll(x, ROPE // 2, x.ndim - 1)
    first = jnp.logical_and(lane >= NOPE, lane < NOPE + ROPE // 2)
    second = jnp.logical_and(lane >= NOPE + ROPE // 2, lane < NOPE + ROPE)
    return jnp.where(first, up, jnp.where(second, down, 0.0))


def _mla_prep_fwd(z3, g_q, g_kv, w_uq, w_ukv, ctab, stab, *, name, tb=256):
    bsz, seq, _ = z3.shape
    tb = min(tb, seq)
    hw = HEADS * HEAD_PAD

    def body(ql_ref, kvl_ref, krl_ref, gq_ref, gkv_ref, wuq_ref, wukv_ref, c_ref, s_ref, q_ref, kv_ref, kp_ref):
        cq = _rms_fwd(ql_ref[0].astype(F32), gq_ref[...], Q_RANK).astype(BF16)
        q = _dot(cq, wuq_ref[...])
        c1, s1 = c_ref[0], s_ref[0]
        c8, s8 = jnp.tile(c1, (1, HEADS)), jnp.tile(s1, (1, HEADS))
        q_ref[0] = ((q * c8 + _partner(q) * s8) * SCALE_LOG2).astype(BF16)
        ckv = _rms_fwd(kvl_ref[0].astype(F32), gkv_ref[...], KV_RANK).astype(BF16)
        kv = _dot(ckv, wukv_ref[...])
        kv_ref[0] = kv.astype(BF16)
        kr = krl_ref[0].astype(F32)
        kr = kr * c1 + _partner(kr) * s1
        lane = _iota((tb, hw), 1) % HEAD_PAD
        kp_ref[0] = jnp.where(lane < NOPE, kv, jnp.tile(kr, (1, HEADS))).astype(BF16)

    return pl.pallas_call(
        body, name=name, grid=(bsz, seq // tb),
        in_specs=[pl.BlockSpec((1, tb, Q_RANK), lambda b, i: (b, i, 4)),
                  pl.BlockSpec((1, tb, KV_RANK), lambda b, i: (b, i, 10)),
                  pl.BlockSpec((1, tb, HEAD_PAD), lambda b, i: (b, i, 11)),
                  pl.BlockSpec((1, Q_RANK), lambda b, i: (0, 0)),
                  pl.BlockSpec((1, KV_RANK), lambda b, i: (0, 0)),
                  pl.BlockSpec((Q_RANK, hw), lambda b, i: (0, 0)),
                  pl.BlockSpec((KV_RANK, hw), lambda b, i: (0, 0)),
                  pl.BlockSpec((1, tb, HEAD_PAD), lambda b, i: (b, i, 0)),
                  pl.BlockSpec((1, tb, HEAD_PAD), lambda b, i: (b, i, 0))],
        out_specs=[pl.BlockSpec((1, tb, hw), lambda b, i: (b, i, 0))] * 3,
        out_shape=[jax.ShapeDtypeStruct((bsz, seq, hw), BF16)] * 3,
        compiler_params=_cparams(),
    )(z3, z3, z3, g_q, g_kv, w_uq, w_ukv, ctab, stab)


def _mla_prep_bwd(z3, dz3, dq3, dk3, dv3, g_q, g_kv, w_uq, w_ukv, ctab, stab, *, name, tb=256):
    bsz, seq, _ = z3.shape
    tb = min(tb, seq)
    hw = HEADS * HEAD_PAD
    nb = seq // tb

    def body(ql_ref, kvl_ref, dq_ref, dk_ref, dv_ref, gq_ref, gkv_ref, wuq_ref, wukv_ref, c_ref, s_ref, dz_in,
             dz_ref, cq_ref, dqb_ref, ckv_ref, dkvb_ref, dgq_ref, dgkv_ref):
        @pl.when(jnp.logical_and(pl.program_id(0) == 0, pl.program_id(1) == 0))
        def _():
            dgq_ref[...] = jnp.zeros_like(dgq_ref)
            dgkv_ref[...] = jnp.zeros_like(dgkv_ref)

        c1, s1 = c_ref[0], s_ref[0]
        c8, s8 = jnp.tile(c1, (1, HEADS)), jnp.tile(s1, (1, HEADS))
        dqr = dq_ref[0]
        dqb = (dqr * c8 + _partner(dqr * s8)).astype(BF16)
        dqb_ref[0] = dqb
        ql = ql_ref[0].astype(F32)
        cq_ref[0] = _rms_fwd(ql, gq_ref[...], Q_RANK).astype(BF16)
        dql, dgq = _rms_bwd(ql, gq_ref[...], _dot(dqb, wuq_ref[...], NT), Q_RANK)
        dgq_ref[...] += dgq

        dk = dk_ref[0]
        lane = _iota((tb, hw), 1) % HEAD_PAD
        dkvb = jnp.where(lane < NOPE, dk, dv_ref[0]).astype(BF16)
        dkvb_ref[0] = dkvb
        kvl = kvl_ref[0].astype(F32)
        ckv_ref[0] = _rms_fwd(kvl, gkv_ref[...], KV_RANK).astype(BF16)
        dkvl, dgkv = _rms_bwd(kvl, gkv_ref[...], _dot(dkvb, wukv_ref[...], NT), KV_RANK)
        dgkv_ref[...] += dgkv

        dkr = dk[:, 0:HEAD_PAD].astype(F32)
        for h in range(1, HEADS):
            dkr = dkr + dk[:, HEAD_PAD * h:HEAD_PAD * (h + 1)].astype(F32)
        lane1 = _iota((tb, HEAD_PAD), 1)
        dkr = jnp.where(jnp.logical_and(lane1 >= NOPE, lane1 < NOPE + ROPE), dkr, 0.0)
        dkrl = dkr * c1 + _partner(dkr * s1)
        dz_ref[0] = jnp.concatenate([dql, dkvl, dkrl], axis=1).astype(BF16)

    return pl.pallas_call(
        body, name=name, grid=(bsz, nb),
        in_specs=[pl.BlockSpec((1, tb, Q_RANK), lambda b, i: (b, i, 4)),
                  pl.BlockSpec((1, tb, KV_RANK), lambda b, i: (b, i, 10)),
                  pl.BlockSpec((1, tb, hw), lambda b, i: (b, i, 0)),
                  pl.BlockSpec((1, tb, hw), lambda b, i: (b, i, 0)),
                  pl.BlockSpec((1, tb, hw), lambda b, i: (b, i, 0)),
                  pl.BlockSpec((1, Q_RANK), lambda b, i: (0, 0)),
                  pl.BlockSpec((1, KV_RANK), lambda b, i: (0, 0)),
                  pl.BlockSpec((Q_RANK, hw), lambda b, i: (0, 0)),
                  pl.BlockSpec((KV_RANK, hw), lambda b, i: (0, 0)),
                  pl.BlockSpec((1, tb, HEAD_PAD), lambda b, i: (b, i, 0)),
                  pl.BlockSpec((1, tb, HEAD_PAD), lambda b, i: (b, i, 0)),
                  ANY_SPEC],
        out_specs=[pl.BlockSpec((1, tb, 512), lambda b, i: (b, i, 2)),
                   pl.BlockSpec((1, tb, Q_RANK), lambda b, i: (b, i, 0)),
                   pl.BlockSpec((1, tb, hw), lambda b, i: (b, i, 0)),
                   pl.BlockSpec((1, tb, KV_RANK), lambda b, i: (b, i, 0)),
                   pl.BlockSpec((1, tb, hw), lambda b, i: (b, i, 0)),
                   pl.BlockSpec((1, Q_RANK), lambda b, i: (0, 0)),
                   pl.BlockSpec((1, KV_RANK), lambda b, i: (0, 0))],
        out_shape=[jax.ShapeDtypeStruct((bsz, seq, D_IN_PAD), BF16),
                   jax.ShapeDtypeStruct((bsz, seq, Q_RANK), BF16),
                   jax.ShapeDtypeStruct((bsz, seq, hw), BF16),
                   jax.ShapeDtypeStruct((bsz, seq, KV_RANK), BF16),
                   jax.ShapeDtypeStruct((bsz, seq, hw), BF16),
                   jax.ShapeDtypeStruct((1, Q_RANK), F32),
                   jax.ShapeDtypeStruct((1, KV_RANK), F32)],
        input_output_aliases={11: 0},
        compiler_params=_cparams(),
    )(z3, z3, dq3, dk3, dv3, g_q, g_kv, w_uq, w_ukv, ctab, stab, dz3)


ATTN_HEADS_PER_STEP = 4


def _attn_specs(tq, seq, hp):
    blk = pl.BlockSpec((1, tq, hp * HEAD_PAD), lambda b, h, i: (b, i, h))
    full = pl.BlockSpec((1, seq, hp * HEAD_PAD), lambda b, h, i: (b, 0, h))
    return blk, full


def _head(h):
    return slice(HEAD_PAD * h, HEAD_PAD * (h + 1))


def _attn_fwd(q3, kv3, kp3, *, name, tq=512, hp=ATTN_HEADS_PER_STEP, side=None):
    bsz, seq, hw = q3.shape
    tq = min(tq, seq)
    blk, full = _attn_specs(tq, seq, hp)

    def body(q_ref, kv_ref, kp_ref, o_ref, lse_ref):
        i = pl.program_id(2)
        is_nope = _iota((tq, HEAD_PAD), 1) < NOPE
        causal = _iota((tq, tq), 1) <= _iota((tq, tq), 0)

        def step(j, carry, diag):
            st = pl.multiple_of(j * tq, tq)
            out = []
            for h in range(hp):
                m, l, acc = carry[h]
                kvj = kv_ref[0, pl.ds(st, tq), _head(h)]
                s = _dot(q_ref[0, :, _head(h)], kp_ref[0, pl.ds(st, tq), _head(h)], NT)
                if diag:
                    s = jnp.where(causal, s, -1e30)
                m_new = jnp.maximum(m, jnp.max(s, axis=1, keepdims=True))
                alpha = jnp.exp2(m - m_new)
                p = jnp.exp2(s - m_new)
                l = alpha * l + jnp.sum(p, axis=1, keepdims=True)
                acc = alpha * acc + _dot(p.astype(BF16), kvj)
                out.append((m_new, l, acc))
            return tuple(out)

        init = tuple((jnp.full((tq, 1), -1e30, F32), jnp.zeros((tq, 1), F32), jnp.zeros((tq, HEAD_PAD), F32))
                     for _ in range(hp))
        carry = lax.fori_loop(0, i, lambda j, c: step(j, c, False), init)
        carry = step(i, carry, True)
        for h in range(hp):
            m, l, acc = carry[h]
            o_ref[0, :, _head(h)] = jnp.where(is_nope, 0.0, acc / l).astype(BF16)
            lse_ref[0, :, _head(h)] = jnp.broadcast_to(m + jnp.log(l) * LOG2E, (tq, HEAD_PAD))

    outs, side_outs = _hosted_call(
        body, name=name, grid=(bsz, HEADS // hp, seq // tq),
        in_specs=[blk, full, full],
        out_specs=[blk, blk],
        out_shape=[jax.ShapeDtypeStruct((bsz, seq, hw), BF16), jax.ShapeDtypeStruct((bsz, seq, hw), F32)],
        args=(q3, kv3, kp3), side=side)
    return outs if side is None else (outs, side_outs)


def _attn_bwd(q3, kv3, kp3, do3, lse3, dl3, *, name, tq=512, hp=ATTN_HEADS_PER_STEP, side=None):
    bsz, seq, hw = q3.shape
    tq = min(tq, seq)
    nq = seq // tq
    blk, full = _attn_specs(tq, seq, hp)
    rep = tq // HEAD_PAD

    def body(kv_ref, kp_ref, q_ref, do_ref, lse_ref, dl_ref, dq_ref, dk_ref, dv_ref):
        j = pl.program_id(2)
        causal = _iota((tq, tq), 1) <= _iota((tq, tq), 0)

        @pl.when(j == 0)
        def _():
            dq_ref[...] = jnp.zeros_like(dq_ref)

        def step(i, carry, diag):
            st = pl.multiple_of(i * tq, tq)
            out = []
            for h in range(hp):
                dk, dv = carry[h]
                qi = q_ref[0, pl.ds(st, tq), _head(h)]
                do = do_ref[0, pl.ds(st, tq), _head(h)]
                kp = kp_ref[0, :, _head(h)]
                s = _dot(qi, kp, NT)
                if diag:
                    s = jnp.where(causal, s, -1e30)
                p = jnp.exp2(s - jnp.tile(lse_ref[0, pl.ds(st, tq), _head(h)], (1, rep)))
                dv = dv + _dot(p.astype(BF16), do, TN)
                dp = _dot(do, kv_ref[0, :, _head(h)], NT)
                ds = (p * (dp - jnp.tile(dl_ref[0, pl.ds(st, tq), _head(h)], (1, rep)))).astype(BF16)
                dk = dk + _dot(ds, qi, TN)
                dq_ref[0, pl.ds(st, tq), _head(h)] += _dot(ds, kp)
                out.append((dk, dv))
            return tuple(out)

        zero = jnp.zeros((tq, HEAD_PAD), F32)
        carry = step(j, tuple((zero, zero) for _ in range(hp)), True)
        carry = lax.fori_loop(j + 1, nq, lambda i, c: step(i, c, False), carry)
        for h in range(hp):
            dk_ref[0, :, _head(h)] = (carry[h][0] * (1.0 / LOG2E)).astype(BF16)
            dv_ref[0, :, _head(h)] = carry[h][1].astype(BF16)

        @pl.when(j == nq - 1)
        def _():
            dq_ref[...] = dq_ref[...] * ATTN_SCALE

    outs, side_outs = _hosted_call(
        body, name=name, grid=(bsz, HEADS // hp, nq),
        in_specs=[blk, blk, full, full, full, full],
        out_specs=[full, blk, blk],
        out_shape=[jax.ShapeDtypeStruct((bsz, seq, hw), F32)] + [jax.ShapeDtypeStruct((bsz, seq, hw), BF16)] * 2,
        args=(kv3, kp3, q3, do3, lse3, dl3), side=side)
    return outs if side is None else (outs, side_outs)


def _onorm_fwd(o3, yg3, g_pad, *, name, tb=512):
    bsz, seq, hw = o3.shape
    wg = yg3.shape[-1]
    tb = min(tb, seq)

    def body(o_ref, yg_ref, g_ref, y_ref):
        ya = _rms_fwd(o_ref[0].astype(F32), g_ref[...], HEADS * 64).astype(BF16)
        y_ref[0] = jnp.concatenate([ya, yg_ref[0]], axis=1)

    return pl.pallas_call(
        body, name=name, grid=(bsz, seq // tb),
        in_specs=[pl.BlockSpec((1, tb, hw), lambda b, i: (b, i, 0)),
                  pl.BlockSpec((1, tb, wg), lambda b, i: (b, i, 0)),
                  pl.BlockSpec((1, hw), lambda b, i: (0, 0))],
        out_specs=pl.BlockSpec((1, tb, hw + wg), lambda b, i: (b, i, 0)),
        out_shape=jax.ShapeDtypeStruct((bsz, seq, hw + wg), BF16),
        compiler_params=_cparams(),
    )(o3, yg3, g_pad)


def _onorm_bwd(o3, dy3, g_pad, *, name, tb=512):
    bsz, seq, hw = o3.shape
    tb = min(tb, seq)

    def body(o_ref, dy_ref, g_ref, do_ref, dl_ref, dg_ref):
        @pl.when(jnp.logical_and(pl.program_id(0) == 0, pl.program_id(1) == 0))
        def _():
            dg_ref[...] = jnp.zeros_like(dg_ref)

        o = o_ref[0].astype(F32)
        do, dg = _rms_bwd(o, g_ref[...], dy_ref[0].astype(F32), HEADS * 64)
        dg_ref[...] += dg
        do_ref[0] = do.astype(BF16)
        prod = do * o
        parts = []
        for h in range(HEADS):
            sh = jnp.sum(prod[:, HEAD_PAD * h:HEAD_PAD * (h + 1)], axis=1, keepdims=True)
            parts.append(jnp.broadcast_to(sh, (tb, HEAD_PAD)))
        dl_ref[0] = jnp.concatenate(parts, axis=1)

    return pl.pallas_call(
        body, name=name, grid=(bsz, seq // tb),
        in_specs=[pl.BlockSpec((1, tb, hw), lambda b, i: (b, i, 0)),
                  pl.BlockSpec((1, tb, hw), lambda b, i: (b, i, 0)),
                  pl.BlockSpec((1, hw), lambda b, i: (0, 0))],
        out_specs=[pl.BlockSpec((1, tb, hw), lambda b, i: (b, i, 0)),
                   pl.BlockSpec((1, tb, hw), lambda b, i: (b, i, 0)),
                   pl.BlockSpec((1, hw), lambda b, i: (0, 0))],
        out_shape=[jax.ShapeDtypeStruct((bsz, seq, hw), BF16),
                   jax.ShapeDtypeStruct((bsz, seq, hw), F32),
                   jax.ShapeDtypeStruct((1, hw), F32)],
        compiler_params=_cparams(),
    )(o3, dy3, g_pad)


def _resnode_bwd(x3, g, *, name, target3=None, dh3=None, dres3=None, mod_nm=None, rows=None,
                 branch3=None, mod_gate=None, gate_row=None, tb=512, side=None):
    bsz, seq, d = x3.shape
    tb = min(tb, seq)
    final = target3 is not None
    has_branch = branch3 is not None
    row_spec = pl.BlockSpec((1, tb, d), lambda b, i: (b, i, 0))
    vec_spec = pl.BlockSpec((1, d), lambda b, i: (0, 0))
    mod_spec = pl.BlockSpec((1, MOD_ROWS, d), lambda b, i: (b, 0, 0))

    ins, in_specs = [x3, g], [row_spec, vec_spec]
    if final:
        ins += [target3]
        in_specs += [row_spec]
    else:
        ins += [dh3, dres3, mod_nm]
        in_specs += [row_spec, row_spec, mod_spec]
    if has_branch:
        ins += [branch3, mod_gate]
        in_specs += [row_spec, mod_spec]

    out_names = ["dx", "dg"]
    out_specs = [row_spec, vec_spec]
    out_shape = [jax.ShapeDtypeStruct((bsz, seq, d), F32), jax.ShapeDtypeStruct((1, d), F32)]
    if final:
        out_names += ["loss"]
        out_specs += [pl.BlockSpec((1, 128), lambda b, i: (0, 0))]
        out_shape += [jax.ShapeDtypeStruct((1, 128), F32)]
    else:
        out_names += ["dnm"]
        out_specs += [mod_spec]
        out_shape += [jax.ShapeDtypeStruct((bsz, MOD_ROWS, d), F32)]
    if has_branch:
        out_names += ["dbr", "dgate"]
        out_specs += [row_spec, mod_spec]
        out_shape += [jax.ShapeDtypeStruct((bsz, seq, d), BF16), jax.ShapeDtypeStruct((bsz, MOD_ROWS, d), F32)]
    n_in = len(ins)

    def body(*refs):
        r = dict(zip(["x", "g"] + (["t"] if final else ["dh", "dres", "nm"]) + (["br", "gm"] if has_branch else []),
                     refs[:n_in]))
        o = dict(zip(out_names, refs[n_in:]))
        b_first = pl.program_id(1) == 0
        first = jnp.logical_and(pl.program_id(0) == 0, b_first)
        rowid = _iota((MOD_ROWS, d), 0)

        @pl.when(first)
        def _():
            o["dg"][...] = jnp.zeros_like(o["dg"])
            if final:
                o["loss"][...] = jnp.zeros_like(o["loss"])

        @pl.when(b_first)
        def _():
            if not final:
                o["dnm"][...] = jnp.zeros_like(o["dnm"])
            if has_branch:
                o["dgate"][...] = jnp.zeros_like(o["dgate"])

        x = r["x"][0]
        gv = r["g"][...]
        if final:
            e = _rms_fwd(x, gv, d) - r["t"][0]
            sq = jnp.sum(jnp.sum(e * e, axis=1, keepdims=True), axis=0, keepdims=True)
            o["loss"][...] += jnp.broadcast_to(sq * (0.5 / d), (1, 128))
            dx, dg = _rms_bwd(x, gv, e * (1.0 / d), d)
        else:
            m = r["nm"][0]
            dh = r["dh"][0].astype(F32)
            scale = m[rows[1]:rows[1] + 1, :]
            rstd = lax.rsqrt(jnp.sum(x * x, axis=-1, keepdims=True) * (1.0 / d) + EPS)
            xh = x * rstd
            nrm = xh * gv
            dshift = jnp.sum(dh, axis=0, keepdims=True)
            dscale = jnp.sum(dh * nrm, axis=0, keepdims=True)
            o["dnm"][0] += jnp.where(rowid == 0, dshift, jnp.where(rowid == 1, dscale, 0.0))
            dn = dh * (1.0 + scale)
            dg = jnp.sum(dn * xh, axis=0, keepdims=True)
            dxh = dn * gv
            dx = rstd * (dxh - xh * (jnp.sum(dxh * xh, axis=-1, keepdims=True) * (1.0 / d))) + r["dres"][0]
        o["dg"][...] += dg
        o["dx"][0] = dx
        if has_branch:
            gate = r["gm"][0][gate_row:gate_row + 1, :]
            o["dbr"][0] = (gate * dx).astype(BF16)
            dgate = jnp.sum(dx * r["br"][0], axis=0, keepdims=True)
            o["dgate"][0] += jnp.where(rowid == 0, dgate, 0.0)

    outs, side_outs = _hosted_call(
        body, name=name, grid=(bsz, seq // tb),
        in_specs=in_specs, out_specs=out_specs, out_shape=out_shape, args=tuple(ins), side=side)
    res = dict(zip(out_names, outs))
    return res if side is None else (res, side_outs)


def _adamw(w, g, m, v, *, name):
    shape = w.shape
    cols = shape[-1]
    rows = w.size // cols
    tr = _pick_rows(rows, max(8, (256 * 1024) // cols // 8 * 8))

    def body(w_ref, g_ref, m_ref, v_ref, d_ref, nm_ref, nv_ref):
        d_ref[...], nm_ref[...], nv_ref[...] = _adamw_math(w_ref[...], g_ref[...], m_ref[...], v_ref[...])

    if w.ndim == 3 and shape[1] % 8 == 0:
        tr3 = _pick_rows(shape[1], max(8, (256 * 1024) // cols // 8 * 8))
        spec3 = pl.BlockSpec((None, tr3, cols), lambda l, i: (l, i, 0))
        return tuple(pl.pallas_call(
            body, name=name, grid=(shape[0], shape[1] // tr3),
            in_specs=[spec3] * 4, out_specs=[spec3] * 3,
            out_shape=[jax.ShapeDtypeStruct(shape, F32)] * 3,
            compiler_params=_cparams(),
        )(w, g, m, v))
    spec = pl.BlockSpec((tr, cols), lambda i: (i, 0))
    outs = pl.pallas_call(
        body, name=name, grid=(rows // tr,),
        in_specs=[spec] * 4, out_specs=[spec] * 3,
        out_shape=[jax.ShapeDtypeStruct((rows, cols), F32)] * 3,
        compiler_params=_cparams(),
    )(*[t.reshape(rows, cols) for t in (w, g, m, v)])
    return tuple(o.reshape(shape) for o in outs)


def _adamw_math(w, g, m, v):
    c1 = 1.0 - ADAM_B1 ** ADAM_STEP
    c2 = 1.0 - ADAM_B2 ** ADAM_STEP
    nm = ADAM_B1 * m + (1.0 - ADAM_B1) * g
    nv = ADAM_B2 * v + (1.0 - ADAM_B2) * (g * g)
    delta = -ADAM_LR * ((nm / c1) / (jnp.sqrt(nv / c2) + ADAM_EPS) + ADAM_WD * w)
    return delta, nm, nv


def _adamw_layers(w, m, v, bufs, row_off, *, name, tr=256):
    depth, rows, cols = w.shape
    tr = min(tr, rows)
    assert rows % tr == 0 and row_off % tr == 0

    outs = None
    for l in range(depth):
        def body(w_ref, g_ref, m_ref, v_ref, *rest):
            go_ref, d_ref, nm_ref, nv_ref = rest[-4:]
            g = g_ref[...]
            go_ref[...] = g
            d_ref[...], nm_ref[...], nv_ref[...] = _adamw_math(w_ref[...], g, m_ref[...], v_ref[...])

        layer = pl.BlockSpec((None, tr, cols), lambda i, l=l: (l, i, 0))
        prev = () if outs is None else tuple(outs)
        outs = pl.pallas_call(
            body, name=f"{name}_l{l}", grid=(rows // tr,),
            in_specs=[layer, pl.BlockSpec((tr, cols), lambda i: (row_off // tr + i, 0)), layer, layer]
            + [ANY_SPEC] * len(prev),
            out_specs=[layer] * 4,
            out_shape=[jax.ShapeDtypeStruct(w.shape, F32)] * 4,
            input_output_aliases={4 + k: k for k in range(len(prev))},
            compiler_params=_cparams(),
        )(w, bufs[l], m, v, *prev)
    return tuple(outs)


def _sum_leading(x, *, name, tr=256):
    n, rows, cols = x.shape
    tr = _pick_rows(rows, tr)

    def body(x_ref, o_ref):
        acc = x_ref[0]
        for k in range(1, n):
            acc = acc + x_ref[k]
        o_ref[...] = acc

    return pl.pallas_call(
        body, name=name, grid=(rows // tr,),
        in_specs=[pl.BlockSpec((n, tr, cols), lambda i: (0, i, 0))],
        out_specs=pl.BlockSpec((tr, cols), lambda i: (i, 0)),
        out_shape=jax.ShapeDtypeStruct((rows, cols), F32),
        compiler_params=_cparams(),
    )(x)


def _position():
    return lax.axis_index("x"), lax.axis_index("y"), lax.axis_index("c")


def _allgather8(x, *, name):
    shape = x.shape

    def body(x_ref, out_ref, send_sems, recv_sems, local_sem):
        px, py, pc = _position()
        me, sibling = (px, py, pc), (px, py, 1 - pc)
        chips = [(1 - px, py), (px, 1 - py), (1 - px, 1 - py)]
        src_own = x_ref

        def slot(qx, qy, qc):
            return out_ref.at[4 * qx + 2 * qy + qc]

        def copy(k, block, to, src=None):
            return pltpu.make_async_remote_copy(
                src_ref=slot(*block) if src is None else src, dst_ref=slot(*block),
                send_sem=send_sems.at[k], recv_sem=recv_sems.at[k], device_id=to, device_id_type=MESH)

        mine = pltpu.make_async_copy(src_own, slot(*me), local_sem)
        mine.start()
        first = [copy(0, me, sibling, src=src_own)]
        first += [copy(1 + j, me, (*chip, pc), src=src_own) for j, chip in enumerate(chips)]
        for cp in first:
            cp.start()
        passed = [copy(4 + j, (*chip, pc), sibling) for j, chip in enumerate(chips)]
        for j, chip in enumerate(chips):
            copy(1 + j, (*chip, pc), me).wait_recv()
            passed[j].start()
        copy(0, sibling, me).wait_recv()
        for j, chip in enumerate(chips):
            copy(4 + j, (*chip, 1 - pc), me).wait_recv()
        for cp in first + passed:
            cp.wait_send()
        mine.wait()

    return pl.pallas_call(
        body, name=name,
        out_shape=jax.ShapeDtypeStruct((N_DEV,) + shape, x.dtype),
        in_specs=[pl.BlockSpec(memory_space=pl.ANY)],
        out_specs=pl.BlockSpec(memory_space=pl.ANY),
        scratch_shapes=[pltpu.SemaphoreType.DMA((7,)), pltpu.SemaphoreType.DMA((7,)), pltpu.SemaphoreType.DMA],
    )(x)


class _Exchange:
    def __init__(self, ins, out_shapes, n, build, aliases=None):
        self.ins, self.out_shapes, self.n, self.build = tuple(ins), tuple(out_shapes), n, build
        self.aliases = dict(aliases or {})

    def _descriptors(self, in_refs, out_refs, send_sems, recv_sems):
        sends, recvs = [], []
        for k, (src, dst, peer, landing) in enumerate(self.build(in_refs, out_refs)):
            sends.append(pltpu.make_async_remote_copy(
                src_ref=src, dst_ref=dst, send_sem=send_sems.at[k], recv_sem=recv_sems.at[k],
                device_id=peer, device_id_type=MESH))
            recvs.append(pltpu.make_async_remote_copy(
                src_ref=src, dst_ref=landing, send_sem=send_sems.at[k], recv_sem=recv_sems.at[k],
                device_id=peer, device_id_type=MESH))
        return sends, recvs

    def start(self, *refs):
        for cp in self._descriptors(*refs)[0]:
            cp.start()

    def finish(self, *refs):
        sends, recvs = self._descriptors(*refs)
        for cp in recvs:
            cp.wait_recv()
        for cp in sends:
            cp.wait_send()


ANY_SPEC = pl.BlockSpec(memory_space=pl.ANY)


def _hosted_call(body, *, name, grid, in_specs, out_specs, out_shape, args, scratch_shapes=(), side=None,
                 num_scalar_prefetch=0, io_aliases=None):
    in_specs, out_specs, out_shape = list(in_specs), list(out_specs), list(out_shape)
    n_in, n_out = len(in_specs) + num_scalar_prefetch, len(out_specs)
    kernel_body = body
    aliases = dict(io_aliases or {})
    if side is not None:
        s_in, s_out = len(side.ins), len(side.out_shapes)
        aliases.update({n_in + i: n_out + o for i, o in side.aliases.items()})

        def kernel_body(*refs):
            ins, s_ins = refs[:n_in], refs[n_in:n_in + s_in]
            outs = refs[n_in + s_in:n_in + s_in + n_out]
            s_outs = refs[n_in + s_in + n_out:n_in + s_in + n_out + s_out]
            scratch, sems = refs[n_in + s_in + n_out + s_out:-2], refs[-2:]
            first = functools.reduce(jnp.logical_and, [pl.program_id(a) == 0 for a in range(len(grid))])
            last = functools.reduce(jnp.logical_and, [pl.program_id(a) == g - 1 for a, g in enumerate(grid)])

            @pl.when(first)
            def _():
                side.start(s_ins, s_outs, *sems)

            body(*ins, *outs, *scratch)

            @pl.when(last)
            def _():
                side.finish(s_ins, s_outs, *sems)

        in_specs += [ANY_SPEC] * s_in
        out_specs += [ANY_SPEC] * s_out
        out_shape += list(side.out_shapes)
        scratch_shapes = list(scratch_shapes) + [pltpu.SemaphoreType.DMA((side.n,)),
                                                 pltpu.SemaphoreType.DMA((side.n,))]
        args = tuple(args) + side.ins
    if num_scalar_prefetch:
        grid_spec = pltpu.PrefetchScalarGridSpec(num_scalar_prefetch=num_scalar_prefetch, grid=grid,
                                                 in_specs=in_specs, out_specs=out_specs,
                                                 scratch_shapes=list(scratch_shapes))
        outs = pl.pallas_call(kernel_body, name=name, grid_spec=grid_spec, out_shape=out_shape,
                              input_output_aliases=aliases, compiler_params=_cparams())(*args)
    else:
        outs = pl.pallas_call(kernel_body, name=name, grid=grid, in_specs=in_specs, out_specs=out_specs,
                              out_shape=out_shape, scratch_shapes=list(scratch_shapes),
                              input_output_aliases=aliases, compiler_params=_cparams())(*args)
    return tuple(outs[:n_out]), tuple(outs[n_out:])


def _run_exchange(ex, *, name):
    s_in = len(ex.ins)

    def body(*refs):
        ins, outs, sems = refs[:s_in], refs[s_in:-2], refs[-2:]
        ex.start(ins, outs, *sems)
        ex.finish(ins, outs, *sems)

    outs = pl.pallas_call(
        body, name=name, out_shape=list(ex.out_shapes),
        in_specs=[ANY_SPEC] * s_in, out_specs=[ANY_SPEC] * len(ex.out_shapes),
        scratch_shapes=[pltpu.SemaphoreType.DMA((ex.n,)), pltpu.SemaphoreType.DMA((ex.n,))],
        input_output_aliases=ex.aliases,
    )(*ex.ins)
    return tuple(outs)


def _other_chips(px, py):
    return [(px, 1 - py), (1 - px, py), (1 - px, 1 - py)]


def _gather_spread(w_flat, halves=True):
    rows, w = w_flat.shape
    hr = rows // 2 if halves else rows

    def build(ins, outs):
        px, py, pc = _position()
        mine = ins[0].at[pl.ds(pc * hr, hr)] if halves else ins[0]
        me = 4 * px + 2 * py + pc
        plan = [((px, py, 1 - pc), me ^ 1)]
        plan += [((qx, qy, pc), 4 * qx + 2 * qy + pc) for qx, qy in _other_chips(px, py)]
        return [(mine, outs[0].at[me], peer, outs[0].at[their]) for peer, their in plan]

    return _Exchange([w_flat], [jax.ShapeDtypeStruct((N_DEV, hr, w), w_flat.dtype)], 4, build)


def _gather_pass_on(gath):
    def build(ins, outs):
        px, py, pc = _position()
        out = []
        for qx, qy in _other_chips(px, py):
            blk = 4 * qx + 2 * qy + pc
            out.append((outs[0].at[blk], outs[0].at[blk], (px, py, 1 - pc), outs[0].at[blk ^ 1]))
        return out

    return _Exchange([gath], [jax.ShapeDtypeStruct(gath.shape, gath.dtype)], 3, build, aliases={0: 0})


def _rs_halves(g):
    n, rows, w = g.shape
    hr = rows // 2

    def build(ins, outs):
        px, py, pc = _position()
        return [(ins[0].at[:, pl.ds((1 - pc) * hr, hr), :], outs[0], (px, py, 1 - pc), outs[0])]

    return _Exchange([g], [jax.ShapeDtypeStruct((n, hr, w), g.dtype)], 1, build)


def _rs_chips(sb):
    def build(ins, outs):
        px, py, pc = _position()
        return [(ins[0].at[j], outs[0].at[j], (qx, qy, pc), outs[0].at[j])
                for j, (qx, qy) in enumerate(_other_chips(px, py))]

    return _Exchange([sb], [jax.ShapeDtypeStruct(sb.shape, sb.dtype)], 3, build)


def _rs_complete(buf):
    def build(ins, outs):
        px, py, pc = _position()
        return [(outs[0].at[pc], outs[0].at[pc], (px, py, 1 - pc), outs[0].at[1 - pc])]

    return _Exchange([buf], [jax.ShapeDtypeStruct(buf.shape, buf.dtype)], 1, build, aliases={0: 0})


def _rs_partial(g, recv, ids, *, name, tr=128):
    _, rows, w = g.shape
    hr = rows // 2
    nb = hr // tr

    def body(ids_ref, g_ref, r_ref, o_ref):
        o_ref[0] = (g_ref[0] + r_ref[0]).astype(BF16)

    grid_spec = pltpu.PrefetchScalarGridSpec(
        num_scalar_prefetch=1, grid=(3, nb),
        in_specs=[pl.BlockSpec((1, tr, w), lambda j, i, ids: (ids[1] ^ (j + 1), ids[0] * nb + i, 0)),
                  pl.BlockSpec((1, tr, w), lambda j, i, ids: (ids[1] ^ (j + 1), i, 0))],
        out_specs=pl.BlockSpec((1, tr, w), lambda j, i, ids: (j, i, 0)))
    return pl.pallas_call(
        body, name=name, grid_spec=grid_spec,
        out_shape=jax.ShapeDtypeStruct((3, hr, w), BF16),
        compiler_params=_cparams(),
    )(ids, g, recv)


def _rs_total(g, recv, got, ids, *, name, tr=128):
    _, rows, w = g.shape
    hr = rows // 2
    nb = hr // tr

    def body(ids_ref, g_ref, r_ref, got_ref, o_ref):
        acc = g_ref[0] + r_ref[0]
        for j in range(3):
            acc = acc + got_ref[j].astype(F32)
        o_ref[0] = acc

    grid_spec = pltpu.PrefetchScalarGridSpec(
        num_scalar_prefetch=1, grid=(nb,),
        in_specs=[pl.BlockSpec((1, tr, w), lambda i, ids: (ids[1], ids[0] * nb + i, 0)),
                  pl.BlockSpec((1, tr, w), lambda i, ids: (ids[1], i, 0)),
                  pl.BlockSpec((3, tr, w), lambda i, ids: (0, i, 0))],
        out_specs=pl.BlockSpec((1, tr, w), lambda i, ids: (ids[0], i, 0)))
    return pl.pallas_call(
        body, name=name, grid_spec=grid_spec,
        out_shape=jax.ShapeDtypeStruct((2, hr, w), F32),
        compiler_params=_cparams(),
    )(ids, g, recv, got)


class _ReduceScatter:
    def __init__(self, g, ids, tag):
        self.g, self.ids, self.tag, self.stage, self.result = g, ids, tag, 0, None

    def next_exchange(self):
        if self.stage == 0:
            return _rs_halves(self.g)
        if self.stage == 1:
            return _rs_chips(self.sb)
        return _rs_complete(self.buf)

    def done(self, outs):
        if self.stage == 0:
            self.recv = outs[0]
            hr = self.recv.shape[1]
            self.tr = max(t for t in range(16, 513, 16) if hr % t == 0)
            self.sb = _rs_partial(self.g, self.recv, self.ids, name=f"{self.tag}_partial", tr=self.tr)
        elif self.stage == 1:
            self.buf = _rs_total(self.g, self.recv, outs[0], self.ids, name=f"{self.tag}_total", tr=self.tr)
        else:
            _, hr, w = outs[0].shape
            self.result = outs[0].reshape(2 * hr, w)
        self.stage += 1

    def finish_alone(self):
        names = ("halves", "chips", "complete")
        while self.stage < 3:
            self.done(_run_exchange(self.next_exchange(), name=f"{self.tag}_{names[self.stage]}"))
        return self.result


def _flat_rows():
    used = sum(r for _, r in FSDP_SECTIONS)
    return used, -(-used // ROW_ALIGN) * ROW_ALIGN


def _cols_to_chunks(full):
    rows, cols = full.shape
    t = full.reshape(rows, N_CHIPS, cols // N_CHIPS).transpose(1, 0, 2)
    return t.reshape(N_CHIPS, -1, FLAT_W)


def _chunks_to_cols(chunks, rows, cols):
    return chunks.reshape(N_CHIPS, rows, cols // N_CHIPS).transpose(1, 0, 2).reshape(rows, cols)


def _pad_heads(w, real):
    lead = w.shape[:-1]
    t = w.reshape(lead + (HEADS, real))
    t = jnp.pad(t, [(0, 0)] * len(lead) + [(0, 0), (0, HEAD_PAD - real)])
    return t.reshape(lead + (HEADS * HEAD_PAD,))


def _unpad_heads(w, real):
    lead = w.shape[:-1]
    return w.reshape(lead + (HEADS, HEAD_PAD))[..., :real].reshape(lead + (HEADS * real,))


def _pad_value_lanes(w, axis):
    w = jnp.moveaxis(w, axis, -1)
    lead = w.shape[:-1]
    t = w.reshape(lead + (HEADS, 64))
    t = jnp.pad(t, [(0, 0)] * len(lead) + [(0, 0), (HEAD_PAD - 64, 0)])
    return jnp.moveaxis(t.reshape(lead + (HEADS * HEAD_PAD,)), -1, axis)


def _unpad_value_lanes(w, axis):
    w = jnp.moveaxis(w, axis, -1)
    lead = w.shape[:-1]
    t = w.reshape(lead + (HEADS, HEAD_PAD))[..., HEAD_PAD - 64:]
    return jnp.moveaxis(t.reshape(lead + (HEADS * 64,)), -1, axis)


def _pad_w_in_t(wt):
    z = jnp.zeros((NOPE, wt.shape[1]), wt.dtype)
    z2 = jnp.zeros((HEAD_PAD - NOPE - ROPE, wt.shape[1]), wt.dtype)
    return jnp.concatenate([wt[:1408], z, wt[1408:], z2], axis=0)


def _unpad_w_in_t(wt):
    return jnp.concatenate([wt[:1408], wt[1408 + NOPE:1408 + NOPE + ROPE]], axis=0)


def _rope_tables(positions):
    freqs = ROPE_THETA ** (-jnp.arange(0, ROPE, 2, dtype=F32) / ROPE)
    ang = positions.astype(F32)[..., None] * freqs
    cos, sin = jnp.cos(ang), jnp.sin(ang)
    lead = cos.shape[:-1]
    ones = jnp.ones(lead + (NOPE,), F32)
    zeros_n = jnp.zeros(lead + (NOPE,), F32)
    zeros_p = jnp.zeros(lead + (HEAD_PAD - NOPE - ROPE,), F32)
    ctab = jnp.concatenate([ones, cos, cos, zeros_p], axis=-1)
    stab = jnp.concatenate([zeros_n, -sin, sin, zeros_p], axis=-1)
    return ctab, stab


def _mix_weights(full):
    return dict(
        w_in_t=_pad_w_in_t(full["w_in_t"]),
        w_uq=_pad_heads(full["mla_w_uq"], NOPE + ROPE),
        w_ukv=full["mla_w_ukv"],
        w_out=jnp.concatenate([_pad_value_lanes(full["w_out"][D_GMLP:], 0), full["w_out"][:D_GMLP]], axis=0),
    )


def _small_weights(p, l):
    ws = p["gmlp_ws"][l]
    tril = jnp.tril(jnp.ones((CHUNK, CHUNK), bool))
    bs = p["gmlp_bs"][l]
    bexp = jnp.repeat(bs.reshape(GROUPS // 2, 2, CHUNK).transpose(0, 2, 1), GROUP_DIM, axis=2)
    return dict(
        ws=ws,
        wst=jnp.where(tril[None], ws, 0.0).transpose(0, 2, 1).astype(BF16),
        bexp=bexp,
        g_mix=p["norm_mix_g"][l][None],
        g_ffn=p["norm_ffn_g"][l][None],
        g_q=p["mla_q_norm_g"][l][None],
        g_kv=p["mla_kv_norm_g"][l][None],
        g_og=p["out_norm_gmlp_g"][l][None],
        g_oa=_pad_value_lanes(p["out_norm_mla_g"][l], 0)[None],
    )


def _local_step(x3, target3, positions, mods, final_g, plan):
    bsz, seq, d = x3.shape
    tok = bsz * seq
    tmt = min(512, seq)
    tmk = min(1024, seq)
    tmw = min(2048, tok)
    chunk = (None, None, FLAT_W, FLAT_W)
    ff_grad_shape = (N_CHIPS, 2 * FLAT_W, FLAT_W)
    ctab, stab = _rope_tables(positions)
    lw = [None] * DEPTH

    def flat(t):
        return t.reshape(tok, t.shape[-1])

    def cube(t):
        return t.reshape(bsz, seq, t.shape[-1])

    def carrying(l, tag, fn, *args, **kw):
        side = plan.host(l, tag)
        if side is None:
            return fn(*args, **kw)
        res, side_outs = fn(*args, side=side, **kw)
        plan.hosted(l, tag, side_outs)
        return res

    saved = []
    x = x3
    for l in range(DEPTH):
        lw[l] = plan.layer(l)
        w, mod = lw[l], mods[l]
        if l == 0:
            h1 = carrying(l, "fwd_normmod1", _normmod_fwd, x, w["g_mix"], mod, SHIFT1, SCALE1,
                          name=f"l{l}_normmod1")
        else:
            h1 = h1_next
        z = cube(_mm(flat(h1), w["w_in_t"], dims="nt", name=f"l{l}_w_in", tm=tmt, tn=D_IN_PAD, tk=d,
                     out_dtypes=(BF16,)))
        yg = _gmlp_fwd(z, w["ws"], w["bexp"], w["g_og"], name=f"l{l}_gmlp_fwd")
        q, kv, kp = _mla_prep_fwd(z, w["g_q"], w["g_kv"], w["w_uq"], w["w_ukv"], ctab, stab, name=f"l{l}_mla_prep")
        o, lse = carrying(l, "fwd_attn", _attn_fwd, q, kv, kp, name=f"l{l}_attn_fwd")
        y = _onorm_fwd(o, yg, w["g_oa"], name=f"l{l}_onorm_fwd")

        def normmod(xv, gv, gm, shift_row, scale_row):
            m = gm[0]
            return _rms_fwd(xv, gv, d) * (1.0 + m[scale_row:scale_row + 1, :]) + m[shift_row:shift_row + 1, :]

        def out_epi(po, xv, gm, gf):
            x_new = xv + gm[0][GATE1:GATE1 + 1, :] * po
            return po, x_new, normmod(x_new, gf, gm, SHIFT2, SCALE2)

        vec_spec = pl.BlockSpec((1, d), lambda i, j, k: (0, j))
        po, x_mid, h2 = carrying(l, "fwd_out_a", _mm, flat(y), w["w_out"], dims="nn", name=f"l{l}_w_out",
                                 tm=tmt, tn=d, tk=y.shape[-1], out_dtypes=(BF16, F32, BF16), epilogue=out_epi,
                                 extras=(flat(x), mod, w["g_ffn"]),
                                 extra_specs=(None, _mod_spec(tmt, d, seq), vec_spec))
        x_mid, h2 = cube(x_mid), cube(h2)

        def act_epi(acc):
            r = jnp.maximum(acc, 0.0)
            return (r * r,)

        r = carrying(l, "fwd_ff1", _mm, flat(h2), w["ff"], dims="nn", name=f"l{l}_w_ff1", tm=tmw, tn=FLAT_W,
                     tk=d, out_dtypes=(BF16,), epilogue=act_epi, weights_outer=True, n=D_FF,
                     b_block=(chunk, lambda i, j, k: (j, 0, 0, 0)))

        more = l + 1 < DEPTH

        def ff2_epi(acc, xv, gm, *nxt):
            x_new = xv + gm[0][GATE2:GATE2 + 1, :] * acc
            return (acc, x_new) + ((normmod(x_new, nxt[1], nxt[0], SHIFT1, SCALE1),) if more else ())

        mod_spec = _mod_spec(tmk, d, seq)
        outs = carrying(l, "fwd_ff2", _mm, r, w["ff"], dims="nn", name=f"l{l}_w_ff2", tm=tmk, tn=d, tk=FLAT_W,
                        out_dtypes=(BF16, F32) + ((BF16,) if more else ()), epilogue=ff2_epi,
                        extras=(flat(x_mid), mod) + ((mods[l + 1], plan.layer(l + 1)["g_mix"]) if more else ()),
                        extra_specs=(None, mod_spec) + ((mod_spec, vec_spec) if more else ()), n=d,
                        b_block=(chunk, lambda i, j, k: (k, 1, 0, 0)))
        f, x_out = outs[0], outs[1]
        h1_next = cube(outs[2]) if more else None
        saved.append(dict(x_in=x, h1=h1, z=z, q=q, kv=kv, kp=kp, o=o, lse=lse, y=y, po=cube(po),
                          x_mid=x_mid, h2=h2, r=r, f=cube(f)))
        x = cube(x_out)

    grads = [dict() for _ in range(DEPTH)]
    dmods = [None] * DEPTH
    top = DEPTH - 1
    node = _resnode_bwd(x, final_g[None], name="final_loss_bwd", target3=target3,
                        branch3=saved[top]["f"], mod_gate=mods[top], gate_row=GATE2)
    loss_part = node["loss"][0, 0]
    d_final_g = node["dg"][0]
    plan.scalars(loss_part, d_final_g)
    for l in range(DEPTH - 1, -1, -1):
        w, mod, s = lw[l], mods[l], saved[l]
        dx_out, dfb, dgate2 = node["dx"], flat(node["dbr"]), node["dgate"][:, 0]

        def dact_epi(acc, rv):
            return (acc * (2.0 * jnp.sqrt(rv.astype(F32))),)

        da = carrying(l, "bwd_d_r", _mm, dfb, w["ff"], dims="nt", name=f"l{l}_d_r", tm=tmw, tn=FLAT_W, tk=d,
                      out_dtypes=(BF16,), epilogue=dact_epi, extras=(s["r"],), weights_outer=True, n=D_FF,
                      b_block=(chunk, lambda i, j, k: (j, 1, 0, 0)))
        g_ff = carrying(l, "bwd_dw_ff2", _mm, s["r"], dfb, dims="tn", name=f"l{l}_dw_ff2", tm=FLAT_W, tn=d,
                        tk=1024, out_into=(ff_grad_shape, (None, FLAT_W, FLAT_W), lambda i, j, k: (i, 1, 0), None))
        g_ff = carrying(l, "bwd_dw_ff1", _mm, flat(s["h2"]), da, dims="tn", name=f"l{l}_dw_ff1", tm=d, tn=FLAT_W,
                        tk=1024, out_into=(ff_grad_shape, (None, FLAT_W, FLAT_W), lambda i, j, k: (j, 0, 0), g_ff))
        plan.ff_grads(l, g_ff)
        dh2 = carrying(l, "bwd_d_h2", _mm, da, w["ff"], dims="nt", name=f"l{l}_d_h2", tm=tmk, tn=d, tk=FLAT_W,
                       n=d, b_block=(chunk, lambda i, j, k: (k, 0, 0, 0)), out_dtypes=(BF16,))
        node = _resnode_bwd(s["x_mid"], w["g_ffn"], name=f"l{l}_resnode_ffn", dh3=cube(dh2), dres3=dx_out,
                            mod_nm=mod, rows=(SHIFT2, SCALE2), branch3=s["po"], mod_gate=mod, gate_row=GATE1)
        grads[l]["norm_ffn_g"] = node["dg"][0]
        dshift2, dscale2 = node["dnm"][:, 0], node["dnm"][:, 1]
        dx_mid, dpo, dgate1 = node["dx"], flat(node["dbr"]), node["dgate"][:, 0]

        wy = s["y"].shape[-1]
        dy = cube(carrying(l, "bwd_d_y", _mm, dpo, w["w_out"], dims="nt", name=f"l{l}_d_y", tm=tmt, tn=wy, tk=d,
                           out_dtypes=(BF16,)))
        dw_out = carrying(l, "bwd_dw_out", _mm, flat(s["y"]), dpo, dims="tn", name=f"l{l}_dw_out", tm=wy // 3,
                          tn=d, tk=1024)
        hw = HEADS * HEAD_PAD
        grads[l]["w_out"] = jnp.concatenate([dw_out[hw:], _unpad_value_lanes(dw_out[:hw], 0)], axis=0)

        dz, dws, dbs, dg_og = _gmlp_bwd(s["z"], dy, w["ws"], w["wst"], w["bexp"], w["g_og"],
                                        name=f"l{l}_gmlp_bwd", dy_col=hw // D_GMLP)
        grads[l]["gmlp_ws"], grads[l]["gmlp_bs"], grads[l]["out_norm_gmlp_g"] = dws, dbs, dg_og[0]

        do, dl, dg_oa = _onorm_bwd(s["o"], dy, w["g_oa"], name=f"l{l}_onorm_bwd")
        grads[l]["out_norm_mla_g"] = _unpad_value_lanes(dg_oa[0], 0)
        dq, dk, dv = carrying(l, "bwd_attn_dkv", _attn_bwd, s["q"], s["kv"], s["kp"], do, s["lse"], dl,
                              name=f"l{l}_attn_bwd")
        dz, cq, dqb, ckv, dkvb, dg_q, dg_kv = _mla_prep_bwd(
            s["z"], dz, dq, dk, dv, w["g_q"], w["g_kv"], w["w_uq"], w["w_ukv"], ctab, stab,
            name=f"l{l}_mla_prep_bwd")
        grads[l]["mla_q_norm_g"], grads[l]["mla_kv_norm_g"] = dg_q[0], dg_kv[0]
        dw_uq = carrying(l, "bwd_dw_uq", _mm, flat(cq), flat(dqb), dims="tn", name=f"l{l}_dw_uq", tm=Q_RANK,
                         tn=1024, tk=1024)
        grads[l]["mla_w_uq"] = _unpad_heads(dw_uq, NOPE + ROPE)
        grads[l]["mla_w_ukv"] = _mm(flat(ckv), flat(dkvb), dims="tn", name=f"l{l}_dw_ukv", tm=KV_RANK, tn=1024, tk=1024)

        grads[l]["w_in_t"] = _unpad_w_in_t(_mm(flat(dz), flat(s["h1"]), dims="tn", name=f"l{l}_dw_in",
                                               tm=D_IN_PAD // 2, tn=d, tk=1024))
        plan.layer_grads(l, grads[l])
        dh1 = carrying(l, "bwd_d_h1", _mm, flat(dz), w["w_in_t"], dims="nn", name=f"l{l}_d_h1", tm=tmt, tn=d,
                       tk=D_IN_PAD, out_dtypes=(BF16,))
        below = dict(branch3=saved[l - 1]["f"], mod_gate=mods[l - 1], gate_row=GATE2) if l > 0 else {}
        node = carrying(l, "bwd_resnode_mix", _resnode_bwd, s["x_in"], w["g_mix"], name=f"l{l}_resnode_mix",
                        dh3=cube(dh1), dres3=dx_mid, mod_nm=mod, rows=(SHIFT1, SCALE1), **below)
        grads[l]["norm_mix_g"] = node["dg"][0]
        dshift1, dscale1 = node["dnm"][:, 0], node["dnm"][:, 1]
        dmods[l] = jnp.stack([dshift1, dscale1, dgate1, dshift2, dscale2, dgate2], axis=1)
        plan.layer_done(l)
    return node["dx"], dmods


W_NAMES = ("w_ada", "b_ada", "norm_mix_g", "w_in", "gmlp_ws", "gmlp_bs", "mla_q_norm_g", "mla_kv_norm_g",
           "mla_w_uq", "mla_w_ukv", "out_norm_gmlp_g", "out_norm_mla_g", "w_out", "norm_ffn_g", "w_ff1", "w_ff2",
           "final_norm_g")
FLAT_KEY = {"w_in": "w_in", "w_uq": "mla_w_uq", "w_ukv": "mla_w_ukv", "w_out": "w_out", "w_ff1": "w_ff1",
            "w_ff2": "w_ff2"}
COL_SHARDED = ("w_in", "w_uq", "w_ukv", "w_ff1")
FULL_SHAPE = {"w_in": (D_MODEL, D_IN), "w_uq": (Q_RANK, HEADS * (NOPE + ROPE)), "w_ukv": (KV_RANK, HEADS * 128),
              "w_out": (D_MODEL, D_MODEL), "w_ff1": (D_MODEL, D_FF), "w_ff2": (D_FF, D_MODEL)}
SMALL_LAYER_NAMES = ("norm_mix_g", "gmlp_ws", "gmlp_bs", "mla_q_norm_g", "mla_kv_norm_g", "out_norm_gmlp_g",
                     "out_norm_mla_g", "norm_ffn_g")


def _silu(v):
    return v * (1.0 / (1.0 + jnp.exp(-v)))


class _CommPlan:
    FWD = {"fwd_attn": ("ff", 0, "spread"), "fwd_out_a": ("ff", 0, "pass"),
           "fwd_ff1": ("mix", 1, "spread"), "fwd_ff2": ("mix", 1, "pass")}
    BWD = {"bwd_d_r": ("mix", 1), "bwd_dw_ff2": ("mix", 1), "bwd_dw_ff1": ("mix", 1),
           "bwd_d_h2": ("ff", 0), "bwd_attn_dkv": ("ff", 0), "bwd_dw_uq": ("ff", 0)}
    BWD_LAST = {"bwd_d_h1": ("mix", 0), "bwd_resnode_mix": ("mix", 0)}
    SMALL = {"bwd_d_y": "spread", "bwd_dw_out": "pass"}

    def __init__(self, weights, ids, dev, core):
        self.weights, self.ids, self.dev, self.core = weights, ids, dev, core
        self.used, self.rows = _flat_rows()
        self.flat = {("mix", l): self._flat_mix(l) for l in range(DEPTH)}
        self.flat.update({("ff", l): jnp.concatenate([weights["w_ff1"][l], weights["w_ff2"][l]], axis=0).astype(BF16)
                          for l in range(DEPTH)})
        self.lw, self.rs, self.grads, self.spread = {}, {}, {}, {}
        self.small_vec, self.small_sum, self.small_spread, self.extra = {}, {}, None, {}
        self.lw = {l: _small_weights(weights, l) for l in range(DEPTH)}

    def _flat_mix(self, l):
        pieces = []
        for nm, _ in FSDP_SECTIONS:
            shard = self.weights[FLAT_KEY[nm]][l]
            pieces.append(shard.T if nm == "w_in" else shard.reshape(-1, FLAT_W))
        pieces.append(jnp.zeros((self.rows - self.used, FLAT_W), F32))
        return jnp.concatenate(pieces, axis=0).astype(BF16)

    def _arrived(self, group, l, gath):
        flat = self.flat[group, l]
        hr = flat.shape[0] // 2
        mine = lax.dynamic_slice(flat, (self.core * hr, 0), (hr, FLAT_W))
        gath = lax.dynamic_update_slice(gath, mine[None], (self.dev, 0, 0))
        if group == "ff":
            self.lw[l]["ff"] = gath.reshape(N_CHIPS, 2, hr, FLAT_W)
            return
        w_gath = gath.reshape(N_CHIPS, self.rows, FLAT_W)
        full, off = {}, 0
        for nm, nrows in FSDP_SECTIONS:
            sec = w_gath[:, off:off + nrows]
            off += nrows
            rows, cols = FULL_SHAPE[nm]
            if nm == "w_in":
                full["w_in_t"] = sec.reshape(cols, rows)
            else:
                full[FLAT_KEY[nm]] = (_chunks_to_cols(sec, rows, cols) if nm in COL_SHARDED
                                      else sec.reshape(rows, cols))
        self.lw[l].update(_mix_weights(full))

    def layer(self, l):
        return self.lw[l]

    def host(self, l, tag):
        if tag == "fwd_normmod1":
            return _gather_spread(self.flat["mix", 0]) if l == 0 else None
        if tag in self.FWD:
            group, ahead, what = self.FWD[tag]
            if l + ahead >= DEPTH:
                return None
            return _gather_spread(self.flat[group, l + ahead]) if what == "spread" else _gather_pass_on(self.spread[group])
        if tag in self.SMALL:
            if l + 1 not in self.small_vec:
                return None
            if self.SMALL[tag] == "spread":
                return _gather_spread(self.small_vec[l + 1], halves=False)
            return _gather_pass_on(self.small_spread)
        rs = self._rs_for(l, tag)
        return None if rs is None or rs.stage > 2 else rs.next_exchange()

    def _rs_for(self, l, tag):
        if tag in self.BWD_LAST:
            return self.rs.get(self.BWD_LAST[tag]) if l == 0 else None
        group, ahead = self.BWD[tag]
        return self.rs.get((group, l + ahead))

    def hosted(self, l, tag, outs):
        if tag == "fwd_normmod1":
            self._arrived("mix", 0, _run_exchange(_gather_pass_on(outs[0]), name="l0_mix_gather_pass_on")[0])
        elif tag in self.FWD:
            group, ahead, what = self.FWD[tag]
            if what == "spread":
                self.spread[group] = outs[0]
            else:
                self._arrived(group, l + ahead, outs[0])
        elif tag in self.SMALL:
            if self.SMALL[tag] == "spread":
                self.small_spread = outs[0]
            else:
                self._small_arrived(l + 1, outs[0])
        else:
            self._rs_for(l, tag).done(outs)

    def ff_grads(self, l, g_ff):
        self.rs["ff", l] = _ReduceScatter(g_ff, self.ids, f"l{l}_ff_rs")

    def layer_grads(self, l, grads):
        self.grads[l] = grads
        pieces = []
        for nm, nrows in FSDP_SECTIONS:
            if nm == "w_in":
                pieces.append(grads["w_in_t"].reshape(N_CHIPS, nrows, FLAT_W))
                continue
            g = grads[FLAT_KEY[nm]]
            pieces.append(_cols_to_chunks(g) if nm in COL_SHARDED else g.reshape(N_CHIPS, nrows, FLAT_W))
        pieces.append(jnp.zeros((N_CHIPS, self.rows - self.used, FLAT_W), F32))
        self.rs["mix", l] = _ReduceScatter(jnp.concatenate(pieces, axis=1), self.ids, f"l{l}_mix_rs")

    def scalars(self, loss_part, d_final_g):
        self.extra = {0: [loss_part[None]]}
        self.extra.setdefault(DEPTH - 1, []).insert(0, d_final_g)

    def layer_done(self, l):
        if l == 0:
            self.rs["mix", 0].finish_alone()
        parts = [self.grads[l][nm].reshape(-1) for nm in SMALL_LAYER_NAMES] + self.extra.get(l, [])
        vec = jnp.concatenate(parts)
        rows = -(-vec.shape[0] // (8 * FLAT_W)) * 8
        self.small_vec[l] = jnp.pad(vec, (0, rows * FLAT_W - vec.shape[0])).reshape(rows, FLAT_W)
        if l == 0:
            (gath,) = _run_exchange(_gather_spread(self.small_vec[0], halves=False), name="l0_small_spread")
            self._small_arrived(0, _run_exchange(_gather_pass_on(gath), name="l0_small_pass_on")[0])

    def _small_arrived(self, l, gath):
        gath = lax.dynamic_update_slice(gath, self.small_vec[l][None], (self.dev, 0, 0))
        self.small_sum[l] = _sum_leading(gath, name=f"l{l}_small_sum").reshape(-1)

    def small_grads(self):
        out = {nm: [] for nm in SMALL_LAYER_NAMES}
        for l in range(DEPTH):
            off = 0
            for nm in SMALL_LAYER_NAMES:
                size = self.weights[nm][l].size
                out[nm].append(self.small_sum[l][off:off + size].reshape(self.weights[nm].shape[1:]))
                off += size
            if l == DEPTH - 1:
                final = self.small_sum[l][off:off + self.weights["final_norm_g"].size]
                off += final.shape[0]
            if l == 0:
                loss = self.small_sum[l][off]
        res = {nm: jnp.stack(parts, axis=0) for nm, parts in out.items()}
        res["final_norm_g"] = final
        return loss, res

    def mix_grads(self):
        per = {FLAT_KEY[nm]: [] for nm, _ in FSDP_SECTIONS}
        for l in range(DEPTH):
            shard, off = self.rs["mix", l].result, 0
            for nm, nrows in FSDP_SECTIONS:
                key = FLAT_KEY[nm]
                sec = shard[off:off + nrows]
                per[key].append(sec.T if nm == "w_in" else sec.reshape(self.weights[key].shape[1:]))
                off += nrows
        return {key: jnp.stack(parts, axis=0) for key, parts in per.items()}

    def ff_shards(self):
        return [self.rs["ff", l].result for l in range(DEPTH)]


def kernel(x, c, positions, w_ada, b_ada, norm_mix_g, w_in, gmlp_ws, gmlp_bs, mla_q_norm_g, mla_kv_norm_g, mla_w_uq, mla_w_ukv, out_norm_gmlp_g, out_norm_mla_g, w_out, norm_ffn_g, w_ff1, w_ff2, final_norm_g, loss_target, m_w_ada, m_b_ada, m_norm_mix_g, m_w_in, m_gmlp_ws, m_gmlp_bs, m_mla_q_norm_g, m_mla_kv_norm_g, m_mla_w_uq, m_mla_w_ukv, m_out_norm_gmlp_g, m_out_norm_mla_g, m_w_out, m_norm_ffn_g, m_w_ff1, m_w_ff2, m_final_norm_g, v_w_ada, v_b_ada, v_norm_mix_g, v_w_in, v_gmlp_ws, v_gmlp_bs, v_mla_q_norm_g, v_mla_kv_norm_g, v_mla_w_uq, v_mla_w_ukv, v_out_norm_gmlp_g, v_out_norm_mla_g, v_w_out, v_norm_ffn_g, v_w_ff1, v_w_ff2, v_final_norm_g):
    weights = dict(w_ada=w_ada, b_ada=b_ada, norm_mix_g=norm_mix_g, w_in=w_in, gmlp_ws=gmlp_ws, gmlp_bs=gmlp_bs,
                   mla_q_norm_g=mla_q_norm_g, mla_kv_norm_g=mla_kv_norm_g, mla_w_uq=mla_w_uq, mla_w_ukv=mla_w_ukv,
                   out_norm_gmlp_g=out_norm_gmlp_g, out_norm_mla_g=out_norm_mla_g, w_out=w_out,
                   norm_ffn_g=norm_ffn_g, w_ff1=w_ff1, w_ff2=w_ff2, final_norm_g=final_norm_g)
    mom_m = dict(zip(W_NAMES, (m_w_ada, m_b_ada, m_norm_mix_g, m_w_in, m_gmlp_ws, m_gmlp_bs, m_mla_q_norm_g,
                               m_mla_kv_norm_g, m_mla_w_uq, m_mla_w_ukv, m_out_norm_gmlp_g, m_out_norm_mla_g,
                               m_w_out, m_norm_ffn_g, m_w_ff1, m_w_ff2, m_final_norm_g)))
    mom_v = dict(zip(W_NAMES, (v_w_ada, v_b_ada, v_norm_mix_g, v_w_in, v_gmlp_ws, v_gmlp_bs, v_mla_q_norm_g,
                               v_mla_kv_norm_g, v_mla_w_uq, v_mla_w_ukv, v_out_norm_gmlp_g, v_out_norm_mla_g,
                               v_w_out, v_norm_ffn_g, v_w_ff1, v_w_ff2, v_final_norm_g)))
    bsz, seq, d = x.shape
    px, py, pc = _position()
    chip = 2 * px + py
    dev = 2 * chip + pc
    ids = jnp.stack([pc, chip]).astype(jnp.int32)
    n_ex = N_DEV * bsz
    ada_cols = w_ada.shape[-1]

    c_all = _allgather8(c.reshape(bsz * d // 128, 128), name="gather_c").reshape(n_ex, d)
    mod_parts = []
    for l in range(DEPTH):
        bias = lax.dynamic_slice(b_ada[l], (chip * ada_cols,), (ada_cols,))[None]
        mod_parts.append(_mm(c_all, w_ada, dims="nn", name=f"l{l}_mod", tm=n_ex, tn=ada_cols, tk=d, n=ada_cols,
                             b_block=((None, d, ada_cols), lambda i, j, k, l=l: (l, k, j)),
                             epilogue=lambda acc, bv: (acc + bv,), extras=(bias,),
                             extra_specs=(pl.BlockSpec((1, ada_cols), lambda i, j, k: (0, j)),), a_fn=_silu))
    mod_g = _allgather8(jnp.concatenate(mod_parts, axis=0), name="gather_mod")
    mod_g = mod_g.reshape(N_CHIPS, 2, DEPTH, n_ex, ada_cols)[:, 0]
    mod_full = mod_g.transpose(1, 2, 0, 3).reshape(DEPTH, n_ex, N_CHIPS * ada_cols)
    mod_mine = lax.dynamic_slice(mod_full, (0, dev * bsz, 0), (DEPTH, bsz, N_MOD * d))
    mod_mine = jnp.pad(mod_mine.reshape(DEPTH, bsz, N_MOD, d), ((0, 0), (0, 0), (0, MOD_ROWS - N_MOD), (0, 0)))
    mods = [mod_mine[l] for l in range(DEPTH)]

    plan = _CommPlan(weights, ids, dev, pc)
    grad_x, dmods = _local_step(x, loss_target, positions, mods, final_norm_g, plan)
    grad = plan.mix_grads()

    loss, small = plan.small_grads()
    grad.update(small)

    dmod = jnp.stack(dmods, axis=1).reshape(bsz * DEPTH * N_MOD, d)
    dmod_all = _allgather8(dmod, name="gather_dmod").reshape(n_ex, DEPTH, N_MOD * d)
    gw, gb = [], []
    for l in range(DEPTH):
        dm = dmod_all[:, l]
        dm_cols = lax.dynamic_slice(dm, (0, chip * ada_cols), (n_ex, ada_cols))
        gw.append(_mm(c_all, dm_cols, dims="tn", name=f"l{l}_dw_ada", tm=d, tn=ada_cols, tk=n_ex, a_fn=_silu,
                      out_into=(w_ada.shape, (None, d, ada_cols), lambda i, j, k, l=l: (l, i, j),
                                gw[-1] if gw else None)))
        gb.append(_sum_leading(dm.reshape(n_ex, N_MOD * d // FLAT_W, FLAT_W), name=f"l{l}_db_ada").reshape(-1))
    grad["w_ada"] = gw[-1]
    grad["b_ada"] = jnp.stack(gb, axis=0)

    delta, new_m, new_v = {}, {}, {}
    ff_bufs = plan.ff_shards()
    for nm, row_off in (("w_ff1", 0), ("w_ff2", FLAT_W)):
        grad[nm], delta[nm], new_m[nm], new_v[nm] = _adamw_layers(
            weights[nm], mom_m[nm], mom_v[nm], ff_bufs, row_off, name=f"adamw_{nm}")
    for nm in W_NAMES:
        if nm not in delta:
            delta[nm], new_m[nm], new_v[nm] = _adamw(weights[nm], grad[nm], mom_m[nm], mom_v[nm],
                                                     name=f"adamw_{nm}")
    return (loss, grad_x, *[grad[nm] for nm in W_NAMES], *[delta[nm] for nm in W_NAMES],
            *[new_m[nm] for nm in W_NAMES], *[new_v[nm] for nm in W_NAMES])
```

```python
import functools
import math

import jax
import jax.numpy as jnp
from jax import lax
from jax.experimental import pallas as pl
from jax.experimental.pallas import tpu as pltpu

F32 = jnp.float32
BF16 = jnp.bfloat16

D_MODEL = 1024
DEPTH = 2
D_GMLP = 512
GROUPS = 8
GROUP_DIM = 64
CHUNK = 128
HEADS = 8
NOPE = 64
ROPE = 32
HEAD_PAD = 128
Q_RANK = 256
KV_RANK = 128
D_FF = 4096
N_MOD = 6
MOD_ROWS = 8
EPS = 1e-6
ROPE_THETA = 10000.0
D_IN = 1440
D_IN_PAD = 1536
ATTN_SCALE = (NOPE + ROPE) ** -0.5
LOG2E = math.log2(math.e)
SCALE_LOG2 = ATTN_SCALE * LOG2E
N_CHIPS = 4
N_DEV = 8

ADAM_LR = 0.001
ADAM_B1 = 0.9
ADAM_B2 = 0.999
ADAM_EPS = 1e-08
ADAM_WD = 0.01
ADAM_STEP = 10

VMEM_LIMIT = 48 * 1024 * 1024
FLAT_W = 1024
ROW_ALIGN = 256

NN = (((1,), (0,)), ((), ()))
NT = (((1,), (1,)), ((), ()))
TN = (((0,), (0,)), ((), ()))
MESH = pl.DeviceIdType.MESH

SHIFT1, SCALE1, GATE1, SHIFT2, SCALE2, GATE2 = range(6)

FSDP_SECTIONS = (("w_out", 256), ("w_in", 360), ("w_uq", 48), ("w_ukv", 32))


def _cparams(vmem=VMEM_LIMIT):
    return pltpu.CompilerParams(vmem_limit_bytes=vmem)


def _dot(a, b, dims=NN):
    return lax.dot_general(a, b, dims, preferred_element_type=F32)


def _iota(shape, axis):
    return lax.broadcasted_iota(jnp.int32, shape, axis)


def _gelu(x):
    k = math.sqrt(2.0 / math.pi)
    return 0.5 * x * (1.0 + jnp.tanh(k * (x + 0.044715 * (x * x * x))))


def _gelu_and_grad(x):
    k = math.sqrt(2.0 / math.pi)
    x2 = x * x
    t = jnp.tanh(k * (x + 0.044715 * (x2 * x)))
    half = 0.5 * (1.0 + t)
    return x * half, half + 0.5 * x * (1.0 - t * t) * (k * (1.0 + 3.0 * 0.044715 * x2))


def _rms_fwd(x, g, n):
    r = lax.rsqrt(jnp.sum(x * x, axis=-1, keepdims=True) * (1.0 / n) + EPS)
    return x * r * g


def _rms_bwd(x, g, dy, n):
    r = lax.rsqrt(jnp.sum(x * x, axis=-1, keepdims=True) * (1.0 / n) + EPS)
    xh = x * r
    dxh = dy * g
    dx = r * (dxh - xh * (jnp.sum(dxh * xh, axis=-1, keepdims=True) * (1.0 / n)))
    dg = jnp.sum(dy * xh, axis=0, keepdims=True)
    return dx, dg


def _pick_rows(rows, limit):
    if rows <= limit:
        return rows
    for t in range(limit, 7, -8):
        if rows % t == 0:
            return t
    return rows


def _mm(a, b, *, dims, name, tm=512, tn=1024, tk=1024, out_dtypes=(F32,), epilogue=None,
        extras=(), extra_specs=(), a_fn=None, weights_outer=False, side=None, b_block=None, n=None,
        out_into=None):
    if dims == "tn":
        kk, m = a.shape
    else:
        m, kk = a.shape
    if n is None:
        n = b.shape[0] if dims == "nt" else b.shape[1]
    tm, tn, tk = min(tm, m), min(tn, n), min(tk, kk)
    assert m % tm == 0 and n % tn == 0 and kk % tk == 0, (name, a.shape, b.shape, tm, tn, tk)
    ni, nj, nk = m // tm, n // tn, kk // tk

    def spec(shape, pick):
        if weights_outer:
            return pl.BlockSpec(shape, lambda j, i, k: pick(i, j, k))
        return pl.BlockSpec(shape, pick)

    if dims == "tn":
        a_spec = spec((tk, tm), lambda i, j, k: (k, i))
    else:
        a_spec = spec((tm, tk), lambda i, j, k: (i, k))
    if b_block is not None:
        b_spec = spec(*b_block)
    elif dims == "nt":
        b_spec = spec((tn, tk), lambda i, j, k: (j, k))
    else:
        b_spec = spec((tk, tn), lambda i, j, k: (k, j))
    o_spec = spec((tm, tn), lambda i, j, k: (i, j))
    out_shape = [jax.ShapeDtypeStruct((m, n), dt) for dt in out_dtypes]
    out_specs = [o_spec] * len(out_dtypes)
    prev, io_aliases = (), {}
    if out_into is not None:
        full_shape, block, index, before = out_into
        assert len(out_dtypes) == 1 and not extras
        out_shape = [jax.ShapeDtypeStruct(full_shape, out_dtypes[0])]
        out_specs = [spec(block, index)]
        if before is not None:
            prev, io_aliases = (before,), {2: 0}
    assert not (weights_outer and extra_specs)
    dn = {"nn": NN, "nt": NT, "tn": TN}[dims]
    n_ex, n_out = len(extras), len(out_dtypes)
    e_specs = [o_spec if s is None else s for s in (tuple(extra_specs) + (None,) * n_ex)[:n_ex]]

    n_prev = len(prev)

    def body(*refs):
        a_ref, b_ref = refs[0], refs[1]
        e_refs = refs[2 + n_prev:2 + n_prev + n_ex]
        o_refs = refs[2 + n_prev + n_ex:2 + n_prev + n_ex + n_out]
        av = a_ref[...]
        if a_fn is not None:
            av = a_fn(av)
        part = _dot(av.astype(BF16), b_ref[...].astype(BF16), dn)

        def finish(acc):
            outs = (acc,) if epilogue is None else epilogue(acc, *[e[...] for e in e_refs])
            for o_ref, o in zip(o_refs, outs):
                o_ref[...] = o.astype(o_ref.dtype)

        if nk == 1:
            finish(part)
        else:
            acc_ref = refs[-1]
            k = pl.program_id(2)

            @pl.when(k == 0)
            def _():
                acc_ref[...] = part

            @pl.when(k > 0)
            def _():
                acc_ref[...] += part

            @pl.when(k == nk - 1)
            def _():
                finish(acc_ref[...])

    outs, side_outs = _hosted_call(
        body, name=name, grid=(nj, ni, nk) if weights_outer else (ni, nj, nk),
        in_specs=[a_spec, b_spec] + [ANY_SPEC] * n_prev + e_specs,
        out_specs=out_specs, out_shape=out_shape,
        scratch_shapes=[pltpu.VMEM((tm, tn), F32)] if nk > 1 else [],
        args=(a, b, *prev, *extras), side=side, io_aliases=io_aliases)
    res = outs[0] if n_out == 1 else outs
    return res if side is None else (res, side_outs)


def _mod_spec(tm, tn, seq):
    return pl.BlockSpec((1, MOD_ROWS, tn), lambda i, j, k: ((i * tm) // seq, 0, j))


def _normmod_fwd(x3, g, mod, shift_row, scale_row, *, name, tb=512, side=None):
    bsz, seq, d = x3.shape
    tb = min(tb, seq)

    def body(x_ref, g_ref, mod_ref, h_ref):
        m = mod_ref[0]
        nrm = _rms_fwd(x_ref[0], g_ref[...], d)
        h = nrm * (1.0 + m[scale_row:scale_row + 1, :]) + m[shift_row:shift_row + 1, :]
        h_ref[0] = h.astype(BF16)

    outs, side_outs = _hosted_call(
        body, name=name, grid=(bsz, seq // tb),
        in_specs=[pl.BlockSpec((1, tb, d), lambda b, i: (b, i, 0)),
                  pl.BlockSpec((1, d), lambda b, i: (0, 0)),
                  pl.BlockSpec((1, MOD_ROWS, d), lambda b, i: (b, 0, 0))],
        out_specs=[pl.BlockSpec((1, tb, d), lambda b, i: (b, i, 0))],
        out_shape=[jax.ShapeDtypeStruct((bsz, seq, d), BF16)],
        args=(x3, g, mod), side=side)
    return outs[0] if side is None else (outs[0], side_outs)


def _pair_mean_exact(x, lo):
    s_lo = jnp.sum(jnp.where(lo, x, 0.0), axis=-1, keepdims=True)
    s_hi = jnp.sum(jnp.where(lo, 0.0, x), axis=-1, keepdims=True)
    return jnp.where(lo, s_lo, s_hi) * (1.0 / GROUP_DIM)


def _gmlp_pair_fwd(gv_p, w0, w1, bias, lo):
    mu = _pair_mean_exact(gv_p, lo)
    dlt = gv_p - mu
    var = _pair_mean_exact(dlt * dlt, lo)
    rstd = lax.rsqrt(var + EPS)
    vn = dlt * rstd
    vnb = vn.astype(BF16)
    mixed = jnp.where(lo, _dot(w0, vnb), _dot(w1, vnb)) + bias
    return vn, vnb, rstd, mixed


def _tril_bf16(w):
    t = w.shape[-1]
    return jnp.where(_iota((t, t), 1) <= _iota((t, t), 0), w, 0.0).astype(BF16)


def _gmlp_fwd(z3, ws, bexp, g_out, *, name):
    bsz, seq, _ = z3.shape
    nc = seq // CHUNK

    def body(u_ref, v_ref, ws_ref, b_ref, g_ref, y_ref):
        lo = _iota((CHUNK, 128), 1) < GROUP_DIM
        gu = _gelu(u_ref[0].astype(F32))
        gv = _gelu(v_ref[0].astype(F32))
        parts = []
        for p in range(GROUPS // 2):
            sl = slice(128 * p, 128 * p + 128)
            w0 = _tril_bf16(ws_ref[2 * p])
            w1 = _tril_bf16(ws_ref[2 * p + 1])
            _, _, _, mixed = _gmlp_pair_fwd(gv[:, sl], w0, w1, b_ref[p], lo)
            parts.append(gu[:, sl] * mixed)
        yg = jnp.concatenate(parts, axis=1)
        y_ref[0] = _rms_fwd(yg, g_ref[...], D_GMLP).astype(BF16)

    return pl.pallas_call(
        body, name=name, grid=(bsz, nc),
        in_specs=[pl.BlockSpec((1, CHUNK, D_GMLP), lambda b, i: (b, i, 0)),
                  pl.BlockSpec((1, CHUNK, D_GMLP), lambda b, i: (b, i, 1)),
                  pl.BlockSpec((GROUPS, CHUNK, CHUNK), lambda b, i: (0, 0, 0)),
                  pl.BlockSpec((GROUPS // 2, CHUNK, 128), lambda b, i: (0, 0, 0)),
                  pl.BlockSpec((1, D_GMLP), lambda b, i: (0, 0))],
        out_specs=pl.BlockSpec((1, CHUNK, D_GMLP), lambda b, i: (b, i, 0)),
        out_shape=jax.ShapeDtypeStruct((bsz, seq, D_GMLP), BF16),
        compiler_params=_cparams(),
    )(z3, z3, ws, bexp, g_out)


def _gmlp_bwd(z3, dyn3, ws, wst, bexp, g_out, *, name, dy_col):
    bsz, seq, _ = z3.shape
    nc = seq // CHUNK
    npair = GROUPS // 2

    def body(u_ref, v_ref, dy_ref, ws_ref, wst_ref, b_ref, g_ref, duv_ref, dws_ref, dbs_ref, dg_ref, dbacc):
        first = jnp.logical_and(pl.program_id(0) == 0, pl.program_id(1) == 0)
        last = jnp.logical_and(pl.program_id(0) == bsz - 1, pl.program_id(1) == nc - 1)

        @pl.when(first)
        def _():
            dws_ref[...] = jnp.zeros_like(dws_ref)
            dg_ref[...] = jnp.zeros_like(dg_ref)
            dbacc[...] = jnp.zeros_like(dbacc)

        lo = _iota((CHUNK, 128), 1) < GROUP_DIM
        tril = _iota((CHUNK, CHUNK), 1) <= _iota((CHUNK, CHUNK), 0)
        u = u_ref[0].astype(F32)
        v = v_ref[0].astype(F32)
        gu, dgu = _gelu_and_grad(u)
        gv, dgv_dv = _gelu_and_grad(v)
        fwd = []
        for p in range(npair):
            sl = slice(128 * p, 128 * p + 128)
            w0 = _tril_bf16(ws_ref[2 * p])
            w1 = _tril_bf16(ws_ref[2 * p + 1])
            fwd.append(_gmlp_pair_fwd(gv[:, sl], w0, w1, b_ref[p], lo))
        yg = jnp.concatenate([gu[:, 128 * p:128 * p + 128] * fwd[p][3] for p in range(npair)], axis=1)
        dyg, dg = _rms_bwd(yg, g_ref[...], dy_ref[0].astype(F32), D_GMLP)
        dg_ref[...] += dg
        du_parts, dv_parts = [], []
        for p in range(npair):
            sl = slice(128 * p, 128 * p + 128)
            vn, vnb, rstd, mixed = fwd[p]
            dyg_p = dyg[:, sl]
            dmixed = dyg_p * gu[:, sl]
            dbacc[p] += dmixed
            dm_lo = jnp.where(lo, dmixed, 0.0).astype(BF16)
            dm_hi = jnp.where(lo, 0.0, dmixed).astype(BF16)
            dws_ref[2 * p] += jnp.where(tril, _dot(dm_lo, vnb, NT), 0.0)
            dws_ref[2 * p + 1] += jnp.where(tril, _dot(dm_hi, vnb, NT), 0.0)
            dmb = dmixed.astype(BF16)
            dvn = jnp.where(lo, _dot(wst_ref[2 * p], dmb), _dot(wst_ref[2 * p + 1], dmb))
            dgv = rstd * (dvn - _pair_mean_exact(dvn, lo) - vn * _pair_mean_exact(dvn * vn, lo))
            dv_parts.append(dgv * dgv_dv[:, sl])
            du_parts.append(dyg_p * mixed * dgu[:, sl])
        duv_ref[0] = jnp.concatenate(du_parts + dv_parts, axis=1).astype(BF16)

        @pl.when(last)
        def _():
            sel = jnp.where(_iota((8, 128), 0) == 0, (_iota((8, 128), 1) < GROUP_DIM).astype(F32),
                            jnp.where(_iota((8, 128), 0) == 1, (_iota((8, 128), 1) >= GROUP_DIM).astype(F32), 0.0))
            for p in range(npair):
                dbs_ref[p] = lax.dot_general(sel, dbacc[p], NT, precision=lax.Precision.HIGHEST,
                                             preferred_element_type=F32)

    duv, dws, dbs, dg = pl.pallas_call(
        body, name=name, grid=(bsz, nc),
        in_specs=[pl.BlockSpec((1, CHUNK, D_GMLP), lambda b, i: (b, i, 0)),
                  pl.BlockSpec((1, CHUNK, D_GMLP), lambda b, i: (b, i, 1)),
                  pl.BlockSpec((1, CHUNK, D_GMLP), lambda b, i: (b, i, dy_col)),
                  pl.BlockSpec((GROUPS, CHUNK, CHUNK), lambda b, i: (0, 0, 0)),
                  pl.BlockSpec((GROUPS, CHUNK, CHUNK), lambda b, i: (0, 0, 0)),
                  pl.BlockSpec((npair, CHUNK, 128), lambda b, i: (0, 0, 0)),
                  pl.BlockSpec((1, D_GMLP), lambda b, i: (0, 0))],
        out_specs=[pl.BlockSpec((1, CHUNK, 2 * D_GMLP), lambda b, i: (b, i, 0)),
                   pl.BlockSpec((GROUPS, CHUNK, CHUNK), lambda b, i: (0, 0, 0)),
                   pl.BlockSpec((npair, 8, CHUNK), lambda b, i: (0, 0, 0)),
                   pl.BlockSpec((1, D_GMLP), lambda b, i: (0, 0))],
        out_shape=[jax.ShapeDtypeStruct((bsz, seq, D_IN_PAD), BF16),
                   jax.ShapeDtypeStruct((GROUPS, CHUNK, CHUNK), F32),
                   jax.ShapeDtypeStruct((npair, 8, CHUNK), F32),
                   jax.ShapeDtypeStruct((1, D_GMLP), F32)],
        scratch_shapes=[pltpu.VMEM((npair, CHUNK, 128), F32)],
        compiler_params=_cparams(),
    )(z3, z3, dyn3, ws, wst, bexp, g_out)
    return duv, dws, dbs[:, :2, :].reshape(GROUPS, CHUNK), dg


def _partner(x):
    width = x.shape[-1]
    lane = _iota(x.shape, x.ndim - 1) % HEAD_PAD
    up = pltpu.roll(x, width - ROPE // 2, x.ndim - 1)
    down = pltpu.roll(x, ROPE // 2, x.ndim - 1)
    first = jnp.logical_and(lane >= NOPE, lane < NOPE + ROPE // 2)
    second = jnp.logical_and(lane >= NOPE + ROPE // 2, lane < NOPE + ROPE)
    return jnp.where(first, up, jnp.where(second, down, 0.0))


def _mla_prep_fwd(z3, g_q, g_kv, w_uq, w_ukv, ctab, stab, *, name, tb=256):
    bsz, seq, _ = z3.shape
    tb = min(tb, seq)
    hw = HEADS * HEAD_PAD

    def body(ql_ref, kvl_ref, krl_ref, gq_ref, gkv_ref, wuq_ref, wukv_ref, c_ref, s_ref, q_ref, kv_ref, kp_ref):
        cq = _rms_fwd(ql_ref[0].astype(F32), gq_ref[...], Q_RANK).astype(BF16)
        q = _dot(cq, wuq_ref[...])
        c1, s1 = c_ref[0], s_ref[0]
        c8, s8 = jnp.tile(c1, (1, HEADS)), jnp.tile(s1, (1, HEADS))
        q_ref[0] = ((q * c8 + _partner(q) * s8) * SCALE_LOG2).astype(BF16)
        ckv = _rms_fwd(kvl_ref[0].astype(F32), gkv_ref[...], KV_RANK).astype(BF16)
        kv = _dot(ckv, wukv_ref[...])
        kv_ref[0] = kv.astype(BF16)
        kr = krl_ref[0].astype(F32)
        kr = kr * c1 + _partner(kr) * s1
        lane = _iota((tb, hw), 1) % HEAD_PAD
        kp_ref[0] = jnp.where(lane < NOPE, kv, jnp.tile(kr, (1, HEADS))).astype(BF16)

    return pl.pallas_call(
        body, name=name, grid=(bsz, seq // tb),
        in_specs=[pl.BlockSpec((1, tb, Q_RANK), lambda b, i: (b, i, 4)),
                  pl.BlockSpec((1, tb, KV_RANK), lambda b, i: (b, i, 10)),
                  pl.BlockSpec((1, tb, HEAD_PAD), lambda b, i: (b, i, 11)),
                  pl.BlockSpec((1, Q_RANK), lambda b, i: (0, 0)),
                  pl.BlockSpec((1, KV_RANK), lambda b, i: (0, 0)),
                  pl.BlockSpec((Q_RANK, hw), lambda b, i: (0, 0)),
                  pl.BlockSpec((KV_RANK, hw), lambda b, i: (0, 0)),
                  pl.BlockSpec((1, tb, HEAD_PAD), lambda b, i: (b, i, 0)),
                  pl.BlockSpec((1, tb, HEAD_PAD), lambda b, i: (b, i, 0))],
        out_specs=[pl.BlockSpec((1, tb, hw), lambda b, i: (b, i, 0))] * 3,
        out_shape=[jax.ShapeDtypeStruct((bsz, seq, hw), BF16)] * 3,
        compiler_params=_cparams(),
    )(z3, z3, z3, g_q, g_kv, w_uq, w_ukv, ctab, stab)


def _mla_prep_bwd(z3, dz3, dq3, dk3, dv3, g_q, g_kv, w_uq, w_ukv, ctab, stab, *, name, tb=256):
    bsz, seq, _ = z3.shape
    tb = min(tb, seq)
    hw = HEADS * HEAD_PAD
    nb = seq // tb

    def body(ql_ref, kvl_ref, dq_ref, dk_ref, dv_ref, gq_ref, gkv_ref, wuq_ref, wukv_ref, c_ref, s_ref, dz_in,
             dz_ref, cq_ref, dqb_ref, ckv_ref, dkvb_ref, dgq_ref, dgkv_ref):
        @pl.when(jnp.logical_and(pl.program_id(0) == 0, pl.program_id(1) == 0))
        def _():
            dgq_ref[...] = jnp.zeros_like(dgq_ref)
            dgkv_ref[...] = jnp.zeros_like(dgkv_ref)

        c1, s1 = c_ref[0], s_ref[0]
        c8, s8 = jnp.tile(c1, (1, HEADS)), jnp.tile(s1, (1, HEADS))
        dqr = dq_ref[0]
        dqb = (dqr * c8 + _partner(dqr * s8)).astype(BF16)
        dqb_ref[0] = dqb
        ql = ql_ref[0].astype(F32)
        cq_ref[0] = _rms_fwd(ql, gq_ref[...], Q_RANK).astype(BF16)
        dql, dgq = _rms_bwd(ql, gq_ref[...], _dot(dqb, wuq_ref[...], NT), Q_RANK)
        dgq_ref[...] += dgq

        dk = dk_ref[0]
        lane = _iota((tb, hw), 1) % HEAD_PAD
        dkvb = jnp.where(lane < NOPE, dk, dv_ref[0]).astype(BF16)
        dkvb_ref[0] = dkvb
        kvl = kvl_ref[0].astype(F32)
        ckv_ref[0] = _rms_fwd(kvl, gkv_ref[...], KV_RANK).astype(BF16)
        dkvl, dgkv = _rms_bwd(kvl, gkv_ref[...], _dot(dkvb, wukv_ref[...], NT), KV_RANK)
        dgkv_ref[...] += dgkv

        dkr = dk[:, 0:HEAD_PAD].astype(F32)
        for h in range(1, HEADS):
            dkr = dkr + dk[:, HEAD_PAD * h:HEAD_PAD * (h + 1)].astype(F32)
        lane1 = _iota((tb, HEAD_PAD), 1)
        dkr = jnp.where(jnp.logical_and(lane1 >= NOPE, lane1 < NOPE + ROPE), dkr, 0.0)
        dkrl = dkr * c1 + _partner(dkr * s1)
        dz_ref[0] = jnp.concatenate([dql, dkvl, dkrl], axis=1).astype(BF16)

    return pl.pallas_call(
        body, name=name, grid=(bsz, nb),
        in_specs=[pl.BlockSpec((1, tb, Q_RANK), lambda b, i: (b, i, 4)),
                  pl.BlockSpec((1, tb, KV_RANK), lambda b, i: (b, i, 10)),
                  pl.BlockSpec((1, tb, hw), lambda b, i: (b, i, 0)),
                  pl.BlockSpec((1, tb, hw), lambda b, i: (b, i, 0)),
                  pl.BlockSpec((1, tb, hw), lambda b, i: (b, i, 0)),
                  pl.BlockSpec((1, Q_RANK), lambda b, i: (0, 0)),
                  pl.BlockSpec((1, KV_RANK), lambda b, i: (0, 0)),
                  pl.BlockSpec((Q_RANK, hw), lambda b, i: (0, 0)),
                  pl.BlockSpec((KV_RANK, hw), lambda b, i: (0, 0)),
                  pl.BlockSpec((1, tb, HEAD_PAD), lambda b, i: (b, i, 0)),
                  pl.BlockSpec((1, tb, HEAD_PAD), lambda b, i: (b, i, 0)),
                  ANY_SPEC],
        out_specs=[pl.BlockSpec((1, tb, 512), lambda b, i: (b, i, 2)),
                   pl.BlockSpec((1, tb, Q_RANK), lambda b, i: (b, i, 0)),
                   pl.BlockSpec((1, tb, hw), lambda b, i: (b, i, 0)),
                   pl.BlockSpec((1, tb, KV_RANK), lambda b, i: (b, i, 0)),
                   pl.BlockSpec((1, tb, hw), lambda b, i: (b, i, 0)),
                   pl.BlockSpec((1, Q_RANK), lambda b, i: (0, 0)),
                   pl.BlockSpec((1, KV_RANK), lambda b, i: (0, 0))],
        out_shape=[jax.ShapeDtypeStruct((bsz, seq, D_IN_PAD), BF16),
                   jax.ShapeDtypeStruct((bsz, seq, Q_RANK), BF16),
                   jax.ShapeDtypeStruct((bsz, seq, hw), BF16),
                   jax.ShapeDtypeStruct((bsz, seq, KV_RANK), BF16),
                   jax.ShapeDtypeStruct((bsz, seq, hw), BF16),
                   jax.ShapeDtypeStruct((1, Q_RANK), F32),
                   jax.ShapeDtypeStruct((1, KV_RANK), F32)],
        input_output_aliases={11: 0},
        compiler_params=_cparams(),
    )(z3, z3, dq3, dk3, dv3, g_q, g_kv, w_uq, w_ukv, ctab, stab, dz3)


ATTN_HEADS_PER_STEP = 4


def _attn_specs(tq, seq, hp):
    blk = pl.BlockSpec((1, tq, hp * HEAD_PAD), lambda b, h, i: (b, i, h))
    full = pl.BlockSpec((1, seq, hp * HEAD_PAD), lambda b, h, i: (b, 0, h))
    return blk, full


def _head(h):
    return slice(HEAD_PAD * h, HEAD_PAD * (h + 1))


def _attn_fwd(q3, kv3, kp3, *, name, tq=512, hp=ATTN_HEADS_PER_STEP, side=None):
    bsz, seq, hw = q3.shape
    tq = min(tq, seq)
    blk, full = _attn_specs(tq, seq, hp)

    def body(q_ref, kv_ref, kp_ref, o_ref, lse_ref):
        i = pl.program_id(2)

        def update(state, q, kp, kv, mask=None):
            m, l, acc = state
            s = _dot(q, kp, NT)
            if mask is not None:
                s = jnp.where(mask, s, -1e30)
            m_new = jnp.maximum(m, jnp.max(s, axis=1, keepdims=True))
            alpha = jnp.exp2(m - m_new)
            p = jnp.exp2(s - m_new)
            return m_new, alpha * l + jnp.sum(p, axis=1, keepdims=True), alpha * acc + _dot(p.astype(BF16), kv)

        def step(j, carry):
            st = pl.multiple_of(j * tq, tq)
            return tuple(update(carry[h], q_ref[0, :, _head(h)], kp_ref[0, pl.ds(st, tq), _head(h)],
                                kv_ref[0, pl.ds(st, tq), _head(h)]) for h in range(hp))

        init = tuple((jnp.full((tq, 1), -1e30, F32), jnp.zeros((tq, 1), F32), jnp.zeros((tq, HEAD_PAD), F32))
                     for _ in range(hp))
        carry = lax.fori_loop(0, i, step, init)

        st = pl.multiple_of(i * tq, tq)
        is_nope = _iota((tq, HEAD_PAD), 1) < NOPE
        causal = _iota((tq, tq), 1) <= _iota((tq, tq), 0)
        for h in range(hp):
            m, l, acc = update(carry[h], q_ref[0, :, _head(h)], kp_ref[0, pl.ds(st, tq), _head(h)],
                               kv_ref[0, pl.ds(st, tq), _head(h)], causal)
            o_ref[0, :, _head(h)] = jnp.where(is_nope, 0.0, acc / l).astype(BF16)
            lse_ref[0, :, _head(h)] = jnp.broadcast_to(m + jnp.log(l) * LOG2E, (tq, HEAD_PAD))

    outs, side_outs = _hosted_call(
        body, name=name, grid=(bsz, HEADS // hp, seq // tq),
        in_specs=[blk, full, full],
        out_specs=[blk, blk],
        out_shape=[jax.ShapeDtypeStruct((bsz, seq, hw), BF16), jax.ShapeDtypeStruct((bsz, seq, hw), F32)],
        args=(q3, kv3, kp3), side=side)
    return outs if side is None else (outs, side_outs)


def _attn_bwd(q3, kv3, kp3, do3, lse3, dl3, *, name, tq=512, hp=ATTN_HEADS_PER_STEP, side=None):
    bsz, seq, hw = q3.shape
    tq = min(tq, seq)
    nq = seq // tq
    blk, full = _attn_specs(tq, seq, hp)

    def body(kv_ref, kp_ref, q_ref, do_ref, lse_ref, dl_ref, dq_ref, dk_ref, dv_ref):
        j = pl.program_id(2)

        @pl.when(j == 0)
        def _():
            dq_ref[...] = jnp.zeros_like(dq_ref)

        def pair(h, row0, nrows, nkeys, mask=None):
            row0 = pl.multiple_of(row0, nrows)
            qi = q_ref[0, pl.ds(row0, nrows), _head(h)]
            do = do_ref[0, pl.ds(row0, nrows), _head(h)]
            kp = kp_ref[0, :nkeys, _head(h)]
            s = _dot(qi, kp, NT)
            if mask is not None:
                s = jnp.where(mask, s, -1e30)
            wide = nkeys // HEAD_PAD
            p = jnp.exp2(s - jnp.tile(lse_ref[0, pl.ds(row0, nrows), _head(h)], (1, wide)))
            dv = _dot(p.astype(BF16), do, TN)
            dp = _dot(do, kv_ref[0, :nkeys, _head(h)], NT)
            ds = (p * (dp - jnp.tile(dl_ref[0, pl.ds(row0, nrows), _head(h)], (1, wide)))).astype(BF16)
            dq_ref[0, pl.ds(row0, nrows), _head(h)] += _dot(ds, kp)
            return _dot(ds, qi, TN), dv

        def step(i, carry):
            st = pl.multiple_of(i * tq, tq)
            out = []
            for h in range(hp):
                dk, dv = pair(h, st, tq, tq)
                out.append((carry[h][0] + dk, carry[h][1] + dv))
            return tuple(out)

        causal = _iota((tq, tq), 1) <= _iota((tq, tq), 0)
        carry = tuple(pair(h, pl.multiple_of(j * tq, tq), tq, tq, causal) for h in range(hp))
        carry = lax.fori_loop(j + 1, nq, step, carry)
        for h in range(hp):
            dk_ref[0, :, _head(h)] = (carry[h][0] * (1.0 / LOG2E)).astype(BF16)
            dv_ref[0, :, _head(h)] = carry[h][1].astype(BF16)

        @pl.when(j == nq - 1)
        def _():
            dq_ref[...] = dq_ref[...] * ATTN_SCALE

    outs, side_outs = _hosted_call(
        body, name=name, grid=(bsz, HEADS // hp, nq),
        in_specs=[blk, blk, full, full, full, full],
        out_specs=[full, blk, blk],
        out_shape=[jax.ShapeDtypeStruct((bsz, seq, hw), F32)] + [jax.ShapeDtypeStruct((bsz, seq, hw), BF16)] * 2,
        args=(kv3, kp3, q3, do3, lse3, dl3), side=side)
    return outs if side is None else (outs, side_outs)


def _onorm_fwd(o3, yg3, g_pad, *, name, tb=512):
    bsz, seq, hw = o3.shape
    wg = yg3.shape[-1]
    tb = min(tb, seq)

    def body(o_ref, yg_ref, g_ref, y_ref):
        ya = _rms_fwd(o_ref[0].astype(F32), g_ref[...], HEADS * 64).astype(BF16)
        y_ref[0] = jnp.concatenate([ya, yg_ref[0]], axis=1)

    return pl.pallas_call(
        body, name=name, grid=(bsz, seq // tb),
        in_specs=[pl.BlockSpec((1, tb, hw), lambda b, i: (b, i, 0)),
                  pl.BlockSpec((1, tb, wg), lambda b, i: (b, i, 0)),
                  pl.BlockSpec((1, hw), lambda b, i: (0, 0))],
        out_specs=pl.BlockSpec((1, tb, hw + wg), lambda b, i: (b, i, 0)),
        out_shape=jax.ShapeDtypeStruct((bsz, seq, hw + wg), BF16),
        compiler_params=_cparams(),
    )(o3, yg3, g_pad)


def _onorm_bwd(o3, dy3, g_pad, *, name, tb=512):
    bsz, seq, hw = o3.shape
    tb = min(tb, seq)

    def body(o_ref, dy_ref, g_ref, do_ref, dl_ref, dg_ref):
        @pl.when(jnp.logical_and(pl.program_id(0) == 0, pl.program_id(1) == 0))
        def _():
            dg_ref[...] = jnp.zeros_like(dg_ref)

        o = o_ref[0].astype(F32)
        do, dg = _rms_bwd(o, g_ref[...], dy_ref[0].astype(F32), HEADS * 64)
        dg_ref[...] += dg
        do_ref[0] = do.astype(BF16)
        prod = do * o
        parts = []
        for h in range(HEADS):
            sh = jnp.sum(prod[:, HEAD_PAD * h:HEAD_PAD * (h + 1)], axis=1, keepdims=True)
            parts.append(jnp.broadcast_to(sh, (tb, HEAD_PAD)))
        dl_ref[0] = jnp.concatenate(parts, axis=1)

    return pl.pallas_call(
        body, name=name, grid=(bsz, seq // tb),
        in_specs=[pl.BlockSpec((1, tb, hw), lambda b, i: (b, i, 0)),
                  pl.BlockSpec((1, tb, hw), lambda b, i: (b, i, 0)),
                  pl.BlockSpec((1, hw), lambda b, i: (0, 0))],
        out_specs=[pl.BlockSpec((1, tb, hw), lambda b, i: (b, i, 0)),
                   pl.BlockSpec((1, tb, hw), lambda b, i: (b, i, 0)),
                   pl.BlockSpec((1, hw), lambda b, i: (0, 0))],
        out_shape=[jax.ShapeDtypeStruct((bsz, seq, hw), BF16),
                   jax.ShapeDtypeStruct((bsz, seq, hw), F32),
                   jax.ShapeDtypeStruct((1, hw), F32)],
        compiler_params=_cparams(),
    )(o3, dy3, g_pad)


def _resnode_bwd(x3, g, *, name, target3=None, dh3=None, dres3=None, mod_nm=None, rows=None,
                 branch3=None, mod_gate=None, gate_row=None, tb=512, side=None):
    bsz, seq, d = x3.shape
    tb = min(tb, seq)
    final = target3 is not None
    has_branch = branch3 is not None
    row_spec = pl.BlockSpec((1, tb, d), lambda b, i: (b, i, 0))
    vec_spec = pl.BlockSpec((1, d), lambda b, i: (0, 0))
    mod_spec = pl.BlockSpec((1, MOD_ROWS, d), lambda b, i: (b, 0, 0))

    ins, in_specs = [x3, g], [row_spec, vec_spec]
    if final:
        ins += [target3]
        in_specs += [row_spec]
    else:
        ins += [dh3, dres3, mod_nm]
        in_specs += [row_spec, row_spec, mod_spec]
    if has_branch:
        ins += [branch3, mod_gate]
        in_specs += [row_spec, mod_spec]

    out_names = ["dx", "dg"]
    out_specs = [row_spec, vec_spec]
    out_shape = [jax.ShapeDtypeStruct((bsz, seq, d), F32), jax.ShapeDtypeStruct((1, d), F32)]
    if final:
        out_names += ["loss"]
        out_specs += [pl.BlockSpec((1, 128), lambda b, i: (0, 0))]
        out_shape += [jax.ShapeDtypeStruct((1, 128), F32)]
    else:
        out_names += ["dnm"]
        out_specs += [mod_spec]
        out_shape += [jax.ShapeDtypeStruct((bsz, MOD_ROWS, d), F32)]
    if has_branch:
        out_names += ["dbr", "dgate"]
        out_specs += [row_spec, mod_spec]
        out_shape += [jax.ShapeDtypeStruct((bsz, seq, d), BF16), jax.ShapeDtypeStruct((bsz, MOD_ROWS, d), F32)]
    n_in = len(ins)

    def body(*refs):
        r = dict(zip(["x", "g"] + (["t"] if final else ["dh", "dres", "nm"]) + (["br", "gm"] if has_branch else []),
                     refs[:n_in]))
        o = dict(zip(out_names, refs[n_in:]))
        b_first = pl.program_id(1) == 0
        first = jnp.logical_and(pl.program_id(0) == 0, b_first)
        rowid = _iota((MOD_ROWS, d), 0)

        @pl.when(first)
        def _():
            o["dg"][...] = jnp.zeros_like(o["dg"])
            if final:
                o["loss"][...] = jnp.zeros_like(o["loss"])

        @pl.when(b_first)
        def _():
            if not final:
                o["dnm"][...] = jnp.zeros_like(o["dnm"])
            if has_branch:
                o["dgate"][...] = jnp.zeros_like(o["dgate"])

        x = r["x"][0]
        gv = r["g"][...]
        if final:
            e = _rms_fwd(x, gv, d) - r["t"][0]
            sq = jnp.sum(jnp.sum(e * e, axis=1, keepdims=True), axis=0, keepdims=True)
            o["loss"][...] += jnp.broadcast_to(sq * (0.5 / d), (1, 128))
            dx, dg = _rms_bwd(x, gv, e * (1.0 / d), d)
        else:
            m = r["nm"][0]
            dh = r["dh"][0].astype(F32)
            scale = m[rows[1]:rows[1] + 1, :]
            rstd = lax.rsqrt(jnp.sum(x * x, axis=-1, keepdims=True) * (1.0 / d) + EPS)
            xh = x * rstd
            nrm = xh * gv
            dshift = jnp.sum(dh, axis=0, keepdims=True)
            dscale = jnp.sum(dh * nrm, axis=0, keepdims=True)
            o["dnm"][0] += jnp.where(rowid == 0, dshift, jnp.where(rowid == 1, dscale, 0.0))
            dn = dh * (1.0 + scale)
            dg = jnp.sum(dn * xh, axis=0, keepdims=True)
            dxh = dn * gv
            dx = rstd * (dxh - xh * (jnp.sum(dxh * xh, axis=-1, keepdims=True) * (1.0 / d))) + r["dres"][0]
        o["dg"][...] += dg
        o["dx"][0] = dx
        if has_branch:
            gate = r["gm"][0][gate_row:gate_row + 1, :]
            o["dbr"][0] = (gate * dx).astype(BF16)
            dgate = jnp.sum(dx * r["br"][0], axis=0, keepdims=True)
            o["dgate"][0] += jnp.where(rowid == 0, dgate, 0.0)

    outs, side_outs = _hosted_call(
        body, name=name, grid=(bsz, seq // tb),
        in_specs=in_specs, out_specs=out_specs, out_shape=out_shape, args=tuple(ins), side=side)
    res = dict(zip(out_names, outs))
    return res if side is None else (res, side_outs)


def _adamw(w, g, m, v, *, name):
    shape = w.shape
    cols = shape[-1]
    rows = w.size // cols
    tr = _pick_rows(rows, max(8, (256 * 1024) // cols // 8 * 8))

    def body(w_ref, g_ref, m_ref, v_ref, d_ref, nm_ref, nv_ref):
        d_ref[...], nm_ref[...], nv_ref[...] = _adamw_math(w_ref[...], g_ref[...], m_ref[...], v_ref[...])

    if w.ndim == 3 and shape[1] % 8 == 0:
        tr3 = _pick_rows(shape[1], max(8, (256 * 1024) // cols // 8 * 8))
        spec3 = pl.BlockSpec((None, tr3, cols), lambda l, i: (l, i, 0))
        return tuple(pl.pallas_call(
            body, name=name, grid=(shape[0], shape[1] // tr3),
            in_specs=[spec3] * 4, out_specs=[spec3] * 3,
            out_shape=[jax.ShapeDtypeStruct(shape, F32)] * 3,
            compiler_params=_cparams(),
        )(w, g, m, v))
    spec = pl.BlockSpec((tr, cols), lambda i: (i, 0))
    outs = pl.pallas_call(
        body, name=name, grid=(rows // tr,),
        in_specs=[spec] * 4, out_specs=[spec] * 3,
        out_shape=[jax.ShapeDtypeStruct((rows, cols), F32)] * 3,
        compiler_params=_cparams(),
    )(*[t.reshape(rows, cols) for t in (w, g, m, v)])
    return tuple(o.reshape(shape) for o in outs)


def _adamw_math(w, g, m, v):
    c1 = 1.0 - ADAM_B1 ** ADAM_STEP
    c2 = 1.0 - ADAM_B2 ** ADAM_STEP
    nm = ADAM_B1 * m + (1.0 - ADAM_B1) * g
    nv = ADAM_B2 * v + (1.0 - ADAM_B2) * (g * g)
    delta = -ADAM_LR * ((nm / c1) / (jnp.sqrt(nv / c2) + ADAM_EPS) + ADAM_WD * w)
    return delta, nm, nv


def _adamw_layers(w, m, v, bufs, row_off, *, name, tr=256):
    depth, rows, cols = w.shape
    tr = min(tr, rows)
    assert rows % tr == 0 and row_off % tr == 0

    outs = None
    for l in range(depth):
        def body(w_ref, g_ref, m_ref, v_ref, *rest):
            go_ref, d_ref, nm_ref, nv_ref = rest[-4:]
            g = g_ref[...]
            go_ref[...] = g
            d_ref[...], nm_ref[...], nv_ref[...] = _adamw_math(w_ref[...], g, m_ref[...], v_ref[...])

        layer = pl.BlockSpec((None, tr, cols), lambda i, l=l: (l, i, 0))
        prev = () if outs is None else tuple(outs)
        outs = pl.pallas_call(
            body, name=f"{name}_l{l}", grid=(rows // tr,),
            in_specs=[layer, pl.BlockSpec((tr, cols), lambda i: (row_off // tr + i, 0)), layer, layer]
            + [ANY_SPEC] * len(prev),
            out_specs=[layer] * 4,
            out_shape=[jax.ShapeDtypeStruct(w.shape, F32)] * 4,
            input_output_aliases={4 + k: k for k in range(len(prev))},
            compiler_params=_cparams(),
        )(w, bufs[l], m, v, *prev)
    return tuple(outs)


def _sum_leading(x, *, name, tr=256):
    n, rows, cols = x.shape
    tr = _pick_rows(rows, tr)

    def body(x_ref, o_ref):
        acc = x_ref[0]
        for k in range(1, n):
            acc = acc + x_ref[k]
        o_ref[...] = acc

    return pl.pallas_call(
        body, name=name, grid=(rows // tr,),
        in_specs=[pl.BlockSpec((n, tr, cols), lambda i: (0, i, 0))],
        out_specs=pl.BlockSpec((tr, cols), lambda i: (i, 0)),
        out_shape=jax.ShapeDtypeStruct((rows, cols), F32),
        compiler_params=_cparams(),
    )(x)


def _position():
    return lax.axis_index("x"), lax.axis_index("y"), lax.axis_index("c")


def _allgather8(x, *, name):
    shape = x.shape

    def body(x_ref, out_ref, send_sems, recv_sems, local_sem):
        px, py, pc = _position()
        me, sibling = (px, py, pc), (px, py, 1 - pc)
        chips = [(1 - px, py), (px, 1 - py), (1 - px, 1 - py)]
        src_own = x_ref

        def slot(qx, qy, qc):
            return out_ref.at[4 * qx + 2 * qy + qc]

        def copy(k, block, to, src=None):
            return pltpu.make_async_remote_copy(
                src_ref=slot(*block) if src is None else src, dst_ref=slot(*block),
                send_sem=send_sems.at[k], recv_sem=recv_sems.at[k], device_id=to, device_id_type=MESH)

        mine = pltpu.make_async_copy(src_own, slot(*me), local_sem)
        mine.start()
        first = [copy(0, me, sibling, src=src_own)]
        first += [copy(1 + j, me, (*chip, pc), src=src_own) for j, chip in enumerate(chips)]
        for cp in first:
            cp.start()
        passed = [copy(4 + j, (*chip, pc), sibling) for j, chip in enumerate(chips)]
        for j, chip in enumerate(chips):
            copy(1 + j, (*chip, pc), me).wait_recv()
            passed[j].start()
        copy(0, sibling, me).wait_recv()
        for j, chip in enumerate(chips):
            copy(4 + j, (*chip, 1 - pc), me).wait_recv()
        for cp in first + passed:
            cp.wait_send()
        mine.wait()

    return pl.pallas_call(
        body, name=name,
        out_shape=jax.ShapeDtypeStruct((N_DEV,) + shape, x.dtype),
        in_specs=[pl.BlockSpec(memory_space=pl.ANY)],
        out_specs=pl.BlockSpec(memory_space=pl.ANY),
        scratch_shapes=[pltpu.SemaphoreType.DMA((7,)), pltpu.SemaphoreType.DMA((7,)), pltpu.SemaphoreType.DMA],
    )(x)


class _Exchange:
    def __init__(self, ins, out_shapes, n, build, aliases=None):
        self.ins, self.out_shapes, self.n, self.build = tuple(ins), tuple(out_shapes), n, build
        self.aliases = dict(aliases or {})

    def _descriptors(self, in_refs, out_refs, send_sems, recv_sems):
        sends, recvs = [], []
        for k, (src, dst, peer, landing) in enumerate(self.build(in_refs, out_refs)):
            sends.append(pltpu.make_async_remote_copy(
                src_ref=src, dst_ref=dst, send_sem=send_sems.at[k], recv_sem=recv_sems.at[k],
                device_id=peer, device_id_type=MESH))
            recvs.append(pltpu.make_async_remote_copy(
                src_ref=src, dst_ref=landing, send_sem=send_sems.at[k], recv_sem=recv_sems.at[k],
                device_id=peer, device_id_type=MESH))
        return sends, recvs

    def start(self, *refs):
        for cp in self._descriptors(*refs)[0]:
            cp.start()

    def finish(self, *refs):
        sends, recvs = self._descriptors(*refs)
        for cp in recvs:
            cp.wait_recv()
        for cp in sends:
            cp.wait_send()


ANY_SPEC = pl.BlockSpec(memory_space=pl.ANY)


def _hosted_call(body, *, name, grid, in_specs, out_specs, out_shape, args, scratch_shapes=(), side=None,
                 num_scalar_prefetch=0, io_aliases=None):
    in_specs, out_specs, out_shape = list(in_specs), list(out_specs), list(out_shape)
    n_in, n_out = len(in_specs) + num_scalar_prefetch, len(out_specs)
    kernel_body = body
    aliases = dict(io_aliases or {})
    if side is not None:
        s_in, s_out = len(side.ins), len(side.out_shapes)
        aliases.update({n_in + i: n_out + o for i, o in side.aliases.items()})

        def kernel_body(*refs):
            ins, s_ins = refs[:n_in], refs[n_in:n_in + s_in]
            outs = refs[n_in + s_in:n_in + s_in + n_out]
            s_outs = refs[n_in + s_in + n_out:n_in + s_in + n_out + s_out]
            scratch, sems = refs[n_in + s_in + n_out + s_out:-2], refs[-2:]
            first = functools.reduce(jnp.logical_and, [pl.program_id(a) == 0 for a in range(len(grid))])
            last = functools.reduce(jnp.logical_and, [pl.program_id(a) == g - 1 for a, g in enumerate(grid)])

            @pl.when(first)
            def _():
                side.start(s_ins, s_outs, *sems)

            body(*ins, *outs, *scratch)

            @pl.when(last)
            def _():
                side.finish(s_ins, s_outs, *sems)

        in_specs += [ANY_SPEC] * s_in
        out_specs += [ANY_SPEC] * s_out
        out_shape += list(side.out_shapes)
        scratch_shapes = list(scratch_shapes) + [pltpu.SemaphoreType.DMA((side.n,)),
                                                 pltpu.SemaphoreType.DMA((side.n,))]
        args = tuple(args) + side.ins
    if num_scalar_prefetch:
        grid_spec = pltpu.PrefetchScalarGridSpec(num_scalar_prefetch=num_scalar_prefetch, grid=grid,
                                                 in_specs=in_specs, out_specs=out_specs,
                                                 scratch_shapes=list(scratch_shapes))
        outs = pl.pallas_call(kernel_body, name=name, grid_spec=grid_spec, out_shape=out_shape,
                              input_output_aliases=aliases, compiler_params=_cparams())(*args)
    else:
        outs = pl.pallas_call(kernel_body, name=name, grid=grid, in_specs=in_specs, out_specs=out_specs,
                              out_shape=out_shape, scratch_shapes=list(scratch_shapes),
                              input_output_aliases=aliases, compiler_params=_cparams())(*args)
    return tuple(outs[:n_out]), tuple(outs[n_out:])


def _run_exchange(ex, *, name):
    s_in = len(ex.ins)

    def body(*refs):
        ins, outs, sems = refs[:s_in], refs[s_in:-2], refs[-2:]
        ex.start(ins, outs, *sems)
        ex.finish(ins, outs, *sems)

    outs = pl.pallas_call(
        body, name=name, out_shape=list(ex.out_shapes),
        in_specs=[ANY_SPEC] * s_in, out_specs=[ANY_SPEC] * len(ex.out_shapes),
        scratch_shapes=[pltpu.SemaphoreType.DMA((ex.n,)), pltpu.SemaphoreType.DMA((ex.n,))],
        input_output_aliases=ex.aliases,
    )(*ex.ins)
    return tuple(outs)


def _other_chips(px, py):
    return [(px, 1 - py), (1 - px, py), (1 - px, 1 - py)]


def _gather_spread(w_flat, halves=True):
    rows, w = w_flat.shape
    hr = rows // 2 if halves else rows

    def build(ins, outs):
        px, py, pc = _position()
        mine = ins[0].at[pl.ds(pc * hr, hr)] if halves else ins[0]
        me = 4 * px + 2 * py + pc
        plan = [((px, py, 1 - pc), me ^ 1)]
        plan += [((qx, qy, pc), 4 * qx + 2 * qy + pc) for qx, qy in _other_chips(px, py)]
        return [(mine, outs[0].at[me], peer, outs[0].at[their]) for peer, their in plan]

    return _Exchange([w_flat], [jax.ShapeDtypeStruct((N_DEV, hr, w), w_flat.dtype)], 4, build)


def _gather_pass_on(gath):
    def build(ins, outs):
        px, py, pc = _position()
        out = []
        for qx, qy in _other_chips(px, py):
            blk = 4 * qx + 2 * qy + pc
            out.append((outs[0].at[blk], outs[0].at[blk], (px, py, 1 - pc), outs[0].at[blk ^ 1]))
        return out

    return _Exchange([gath], [jax.ShapeDtypeStruct(gath.shape, gath.dtype)], 3, build, aliases={0: 0})


def _rs_halves(g):
    n, rows, w = g.shape
    hr = rows // 2

    def build(ins, outs):
        px, py, pc = _position()
        return [(ins[0].at[:, pl.ds((1 - pc) * hr, hr), :], outs[0], (px, py, 1 - pc), outs[0])]

    return _Exchange([g], [jax.ShapeDtypeStruct((n, hr, w), g.dtype)], 1, build)


def _rs_chips(sb):
    def build(ins, outs):
        px, py, pc = _position()
        return [(ins[0].at[j], outs[0].at[j], (qx, qy, pc), outs[0].at[j])
                for j, (qx, qy) in enumerate(_other_chips(px, py))]

    return _Exchange([sb], [jax.ShapeDtypeStruct(sb.shape, sb.dtype)], 3, build)


def _rs_complete(buf):
    def build(ins, outs):
        px, py, pc = _position()
        return [(outs[0].at[pc], outs[0].at[pc], (px, py, 1 - pc), outs[0].at[1 - pc])]

    return _Exchange([buf], [jax.ShapeDtypeStruct(buf.shape, buf.dtype)], 1, build, aliases={0: 0})


def _rs_partial(g, recv, ids, *, name, tr=128):
    _, rows, w = g.shape
    hr = rows // 2
    nb = hr // tr

    def body(ids_ref, g_ref, r_ref, o_ref):
        o_ref[0] = (g_ref[0] + r_ref[0]).astype(BF16)

    grid_spec = pltpu.PrefetchScalarGridSpec(
        num_scalar_prefetch=1, grid=(3, nb),
        in_specs=[pl.BlockSpec((1, tr, w), lambda j, i, ids: (ids[1] ^ (j + 1), ids[0] * nb + i, 0)),
                  pl.BlockSpec((1, tr, w), lambda j, i, ids: (ids[1] ^ (j + 1), i, 0))],
        out_specs=pl.BlockSpec((1, tr, w), lambda j, i, ids: (j, i, 0)))
    return pl.pallas_call(
        body, name=name, grid_spec=grid_spec,
        out_shape=jax.ShapeDtypeStruct((3, hr, w), BF16),
        compiler_params=_cparams(),
    )(ids, g, recv)


def _rs_total(g, recv, got, ids, *, name, tr=128):
    _, rows, w = g.shape
    hr = rows // 2
    nb = hr // tr

    def body(ids_ref, g_ref, r_ref, got_ref, o_ref):
        acc = g_ref[0] + r_ref[0]
        for j in range(3):
            acc = acc + got_ref[j].astype(F32)
        o_ref[0] = acc

    grid_spec = pltpu.PrefetchScalarGridSpec(
        num_scalar_prefetch=1, grid=(nb,),
        in_specs=[pl.BlockSpec((1, tr, w), lambda i, ids: (ids[1], ids[0] * nb + i, 0)),
                  pl.BlockSpec((1, tr, w), lambda i, ids: (ids[1], i, 0)),
                  pl.BlockSpec((3, tr, w), lambda i, ids: (0, i, 0))],
        out_specs=pl.BlockSpec((1, tr, w), lambda i, ids: (ids[0], i, 0)))
    return pl.pallas_call(
        body, name=name, grid_spec=grid_spec,
        out_shape=jax.ShapeDtypeStruct((2, hr, w), F32),
        compiler_params=_cparams(),
    )(ids, g, recv, got)


class _ReduceScatter:
    def __init__(self, g, ids, tag):
        self.g, self.ids, self.tag, self.stage, self.result = g, ids, tag, 0, None

    def next_exchange(self):
        if self.stage == 0:
            return _rs_halves(self.g)
        if self.stage == 1:
            return _rs_chips(self.sb)
        return _rs_complete(self.buf)

    def done(self, outs):
        if self.stage == 0:
            self.recv = outs[0]
            hr = self.recv.shape[1]
            self.tr = max(t for t in range(16, 513, 16) if hr % t == 0)
            self.sb = _rs_partial(self.g, self.recv, self.ids, name=f"{self.tag}_partial", tr=self.tr)
        elif self.stage == 1:
            self.buf = _rs_total(self.g, self.recv, outs[0], self.ids, name=f"{self.tag}_total", tr=self.tr)
        else:
            _, hr, w = outs[0].shape
            self.result = outs[0].reshape(2 * hr, w)
        self.stage += 1

    def finish_alone(self):
        names = ("halves", "chips", "complete")
        while self.stage < 3:
            self.done(_run_exchange(self.next_exchange(), name=f"{self.tag}_{names[self.stage]}"))
        return self.result


def _flat_rows():
    used = sum(r for _, r in FSDP_SECTIONS)
    return used, -(-used // ROW_ALIGN) * ROW_ALIGN


def _cols_to_chunks(full):
    rows, cols = full.shape
    t = full.reshape(rows, N_CHIPS, cols // N_CHIPS).transpose(1, 0, 2)
    return t.reshape(N_CHIPS, -1, FLAT_W)


def _chunks_to_cols(chunks, rows, cols):
    return chunks.reshape(N_CHIPS, rows, cols // N_CHIPS).transpose(1, 0, 2).reshape(rows, cols)


def _pad_heads(w, real):
    lead = w.shape[:-1]
    t = w.reshape(lead + (HEADS, real))
    t = jnp.pad(t, [(0, 0)] * len(lead) + [(0, 0), (0, HEAD_PAD - real)])
    return t.reshape(lead + (HEADS * HEAD_PAD,))


def _unpad_heads(w, real):
    lead = w.shape[:-1]
    return w.reshape(lead + (HEADS, HEAD_PAD))[..., :real].reshape(lead + (HEADS * real,))


def _pad_value_lanes(w, axis):
    w = jnp.moveaxis(w, axis, -1)
    lead = w.shape[:-1]
    t = w.reshape(lead + (HEADS, 64))
    t = jnp.pad(t, [(0, 0)] * len(lead) + [(0, 0), (HEAD_PAD - 64, 0)])
    return jnp.moveaxis(t.reshape(lead + (HEADS * HEAD_PAD,)), -1, axis)


def _unpad_value_lanes(w, axis):
    w = jnp.moveaxis(w, axis, -1)
    lead = w.shape[:-1]
    t = w.reshape(lead + (HEADS, HEAD_PAD))[..., HEAD_PAD - 64:]
    return jnp.moveaxis(t.reshape(lead + (HEADS * 64,)), -1, axis)


def _pad_w_in_t(wt):
    z = jnp.zeros((NOPE, wt.shape[1]), wt.dtype)
    z2 = jnp.zeros((HEAD_PAD - NOPE - ROPE, wt.shape[1]), wt.dtype)
    return jnp.concatenate([wt[:1408], z, wt[1408:], z2], axis=0)


def _unpad_w_in_t(wt):
    return jnp.concatenate([wt[:1408], wt[1408 + NOPE:1408 + NOPE + ROPE]], axis=0)


def _rope_tables(positions):
    freqs = ROPE_THETA ** (-jnp.arange(0, ROPE, 2, dtype=F32) / ROPE)
    ang = positions.astype(F32)[..., None] * freqs
    cos, sin = jnp.cos(ang), jnp.sin(ang)
    lead = cos.shape[:-1]
    ones = jnp.ones(lead + (NOPE,), F32)
    zeros_n = jnp.zeros(lead + (NOPE,), F32)
    zeros_p = jnp.zeros(lead + (HEAD_PAD - NOPE - ROPE,), F32)
    ctab = jnp.concatenate([ones, cos, cos, zeros_p], axis=-1)
    stab = jnp.concatenate([zeros_n, -sin, sin, zeros_p], axis=-1)
    return ctab, stab


def _mix_weights(full):
    return dict(
        w_in_t=_pad_w_in_t(full["w_in_t"]),
        w_uq=_pad_heads(full["mla_w_uq"], NOPE + ROPE),
        w_ukv=full["mla_w_ukv"],
        w_out=jnp.concatenate([_pad_value_lanes(full["w_out"][D_GMLP:], 0), full["w_out"][:D_GMLP]], axis=0),
    )


def _small_weights(p, l):
    ws = p["gmlp_ws"][l]
    tril = jnp.tril(jnp.ones((CHUNK, CHUNK), bool))
    bs = p["gmlp_bs"][l]
    bexp = jnp.repeat(bs.reshape(GROUPS // 2, 2, CHUNK).transpose(0, 2, 1), GROUP_DIM, axis=2)
    return dict(
        ws=ws,
        wst=jnp.where(tril[None], ws, 0.0).transpose(0, 2, 1).astype(BF16),
        bexp=bexp,
        g_mix=p["norm_mix_g"][l][None],
        g_ffn=p["norm_ffn_g"][l][None],
        g_q=p["mla_q_norm_g"][l][None],
        g_kv=p["mla_kv_norm_g"][l][None],
        g_og=p["out_norm_gmlp_g"][l][None],
        g_oa=_pad_value_lanes(p["out_norm_mla_g"][l], 0)[None],
    )


def _local_step(x3, target3, positions, mods, final_g, plan):
    bsz, seq, d = x3.shape
    tok = bsz * seq
    tmt = min(512, seq)
    tmk = min(1024, seq)
    tmw = min(2048, tok)
    chunk = (None, None, FLAT_W, FLAT_W)
    ff_grad_shape = (N_CHIPS, 2 * FLAT_W, FLAT_W)
    ctab, stab = _rope_tables(positions)
    lw = [None] * DEPTH

    def flat(t):
        return t.reshape(tok, t.shape[-1])

    def cube(t):
        return t.reshape(bsz, seq, t.shape[-1])

    def carrying(l, tag, fn, *args, **kw):
        side = plan.host(l, tag)
        if side is None:
            return fn(*args, **kw)
        res, side_outs = fn(*args, side=side, **kw)
        plan.hosted(l, tag, side_outs)
        return res

    saved = []
    x = x3
    for l in range(DEPTH):
        lw[l] = plan.layer(l)
        w, mod = lw[l], mods[l]
        if l == 0:
            h1 = carrying(l, "fwd_normmod1", _normmod_fwd, x, w["g_mix"], mod, SHIFT1, SCALE1,
                          name=f"l{l}_normmod1")
        else:
            h1 = h1_next
        z = cube(_mm(flat(h1), w["w_in_t"], dims="nt", name=f"l{l}_w_in", tm=tmt, tn=D_IN_PAD, tk=d,
                     out_dtypes=(BF16,)))
        yg = _gmlp_fwd(z, w["ws"], w["bexp"], w["g_og"], name=f"l{l}_gmlp_fwd")
        q, kv, kp = _mla_prep_fwd(z, w["g_q"], w["g_kv"], w["w_uq"], w["w_ukv"], ctab, stab, name=f"l{l}_mla_prep")
        o, lse = carrying(l, "fwd_attn", _attn_fwd, q, kv, kp, name=f"l{l}_attn_fwd")
        y = _onorm_fwd(o, yg, w["g_oa"], name=f"l{l}_onorm_fwd")

        def normmod(xv, gv, gm, shift_row, scale_row):
            m = gm[0]
            return _rms_fwd(xv, gv, d) * (1.0 + m[scale_row:scale_row + 1, :]) + m[shift_row:shift_row + 1, :]

        def out_epi(po, xv, gm, gf):
            x_new = xv + gm[0][GATE1:GATE1 + 1, :] * po
            return po, x_new, normmod(x_new, gf, gm, SHIFT2, SCALE2)

        vec_spec = pl.BlockSpec((1, d), lambda i, j, k: (0, j))
        po, x_mid, h2 = carrying(l, "fwd_out_a", _mm, flat(y), w["w_out"], dims="nn", name=f"l{l}_w_out",
                                 tm=tmt, tn=d, tk=y.shape[-1], out_dtypes=(BF16, F32, BF16), epilogue=out_epi,
                                 extras=(flat(x), mod, w["g_ffn"]),
                                 extra_specs=(None, _mod_spec(tmt, d, seq), vec_spec))
        x_mid, h2 = cube(x_mid), cube(h2)

        def act_epi(acc):
            r = jnp.maximum(acc, 0.0)
            return (r * r,)

        r = carrying(l, "fwd_ff1", _mm, flat(h2), w["ff"], dims="nn", name=f"l{l}_w_ff1", tm=tmw, tn=FLAT_W,
                     tk=d, out_dtypes=(BF16,), epilogue=act_epi, weights_outer=True, n=D_FF,
                     b_block=(chunk, lambda i, j, k: (j, 0, 0, 0)))

        more = l + 1 < DEPTH

        def ff2_epi(acc, xv, gm, *nxt):
            x_new = xv + gm[0][GATE2:GATE2 + 1, :] * acc
            return (acc, x_new) + ((normmod(x_new, nxt[1], nxt[0], SHIFT1, SCALE1),) if more else ())

        mod_spec = _mod_spec(tmk, d, seq)
        outs = carrying(l, "fwd_ff2", _mm, r, w["ff"], dims="nn", name=f"l{l}_w_ff2", tm=tmk, tn=d, tk=FLAT_W,
                        out_dtypes=(BF16, F32) + ((BF16,) if more else ()), epilogue=ff2_epi,
                        extras=(flat(x_mid), mod) + ((mods[l + 1], plan.layer(l + 1)["g_mix"]) if more else ()),
                        extra_specs=(None, mod_spec) + ((mod_spec, vec_spec) if more else ()), n=d,
                        b_block=(chunk, lambda i, j, k: (k, 1, 0, 0)))
        f, x_out = outs[0], outs[1]
        h1_next = cube(outs[2]) if more else None
        saved.append(dict(x_in=x, h1=h1, z=z, q=q, kv=kv, kp=kp, o=o, lse=lse, y=y, po=cube(po),
                          x_mid=x_mid, h2=h2, r=r, f=cube(f)))
        x = cube(x_out)

    grads = [dict() for _ in range(DEPTH)]
    dmods = [None] * DEPTH
    top = DEPTH - 1
    node = _resnode_bwd(x, final_g[None], name="final_loss_bwd", target3=target3,
                        branch3=saved[top]["f"], mod_gate=mods[top], gate_row=GATE2)
    loss_part = node["loss"][0, 0]
    d_final_g = node["dg"][0]
    plan.scalars(loss_part, d_final_g)
    for l in range(DEPTH - 1, -1, -1):
        w, mod, s = lw[l], mods[l], saved[l]
        dx_out, dfb, dgate2 = node["dx"], flat(node["dbr"]), node["dgate"][:, 0]

        def dact_epi(acc, rv):
            return (acc * (2.0 * jnp.sqrt(rv.astype(F32))),)

        da = carrying(l, "bwd_d_r", _mm, dfb, w["ff"], dims="nt", name=f"l{l}_d_r", tm=tmw, tn=FLAT_W, tk=d,
                      out_dtypes=(BF16,), epilogue=dact_epi, extras=(s["r"],), weights_outer=True, n=D_FF,
                      b_block=(chunk, lambda i, j, k: (j, 1, 0, 0)))
        g_ff = carrying(l, "bwd_dw_ff2", _mm, s["r"], dfb, dims="tn", name=f"l{l}_dw_ff2", tm=FLAT_W, tn=d,
                        tk=2048, out_into=(ff_grad_shape, (None, FLAT_W, FLAT_W), lambda i, j, k: (i, 1, 0), None))
        g_ff = carrying(l, "bwd_dw_ff1", _mm, flat(s["h2"]), da, dims="tn", name=f"l{l}_dw_ff1", tm=d, tn=FLAT_W,
                        tk=2048, out_into=(ff_grad_shape, (None, FLAT_W, FLAT_W), lambda i, j, k: (j, 0, 0), g_ff))
        plan.ff_grads(l, g_ff)
        dh2 = carrying(l, "bwd_d_h2", _mm, da, w["ff"], dims="nt", name=f"l{l}_d_h2", tm=tmk, tn=d, tk=FLAT_W,
                       n=d, b_block=(chunk, lambda i, j, k: (k, 0, 0, 0)), out_dtypes=(BF16,))
        node = _resnode_bwd(s["x_mid"], w["g_ffn"], name=f"l{l}_resnode_ffn", dh3=cube(dh2), dres3=dx_out,
                            mod_nm=mod, rows=(SHIFT2, SCALE2), branch3=s["po"], mod_gate=mod, gate_row=GATE1)
        grads[l]["norm_ffn_g"] = node["dg"][0]
        dshift2, dscale2 = node["dnm"][:, 0], node["dnm"][:, 1]
        dx_mid, dpo, dgate1 = node["dx"], flat(node["dbr"]), node["dgate"][:, 0]

        wy = s["y"].shape[-1]
        dy = cube(carrying(l, "bwd_d_y", _mm, dpo, w["w_out"], dims="nt", name=f"l{l}_d_y", tm=tmt, tn=wy, tk=d,
                           out_dtypes=(BF16,)))
        dw_out = carrying(l, "bwd_dw_out", _mm, flat(s["y"]), dpo, dims="tn", name=f"l{l}_dw_out", tm=wy // 3,
                          tn=d, tk=1024)
        hw = HEADS * HEAD_PAD
        grads[l]["w_out"] = jnp.concatenate([dw_out[hw:], _unpad_value_lanes(dw_out[:hw], 0)], axis=0)

        dz, dws, dbs, dg_og = _gmlp_bwd(s["z"], dy, w["ws"], w["wst"], w["bexp"], w["g_og"],
                                        name=f"l{l}_gmlp_bwd", dy_col=hw // D_GMLP)
        grads[l]["gmlp_ws"], grads[l]["gmlp_bs"], grads[l]["out_norm_gmlp_g"] = dws, dbs, dg_og[0]

        do, dl, dg_oa = _onorm_bwd(s["o"], dy, w["g_oa"], name=f"l{l}_onorm_bwd")
        grads[l]["out_norm_mla_g"] = _unpad_value_lanes(dg_oa[0], 0)
        dq, dk, dv = carrying(l, "bwd_attn_dkv", _attn_bwd, s["q"], s["kv"], s["kp"], do, s["lse"], dl,
                              name=f"l{l}_attn_bwd")
        dz, cq, dqb, ckv, dkvb, dg_q, dg_kv = _mla_prep_bwd(
            s["z"], dz, dq, dk, dv, w["g_q"], w["g_kv"], w["w_uq"], w["w_ukv"], ctab, stab,
            name=f"l{l}_mla_prep_bwd")
        grads[l]["mla_q_norm_g"], grads[l]["mla_kv_norm_g"] = dg_q[0], dg_kv[0]
        dw_uq = carrying(l, "bwd_dw_uq", _mm, flat(cq), flat(dqb), dims="tn", name=f"l{l}_dw_uq", tm=Q_RANK,
                         tn=1024, tk=1024)
        grads[l]["mla_w_uq"] = _unpad_heads(dw_uq, NOPE + ROPE)
        grads[l]["mla_w_ukv"] = _mm(flat(ckv), flat(dkvb), dims="tn", name=f"l{l}_dw_ukv", tm=KV_RANK, tn=1024, tk=1024)

        grads[l]["w_in_t"] = _unpad_w_in_t(_mm(flat(dz), flat(s["h1"]), dims="tn", name=f"l{l}_dw_in",
                                               tm=D_IN_PAD // 2, tn=d, tk=1024))
        plan.layer_grads(l, grads[l])
        dh1 = carrying(l, "bwd_d_h1", _mm, flat(dz), w["w_in_t"], dims="nn", name=f"l{l}_d_h1", tm=tmt, tn=d,
                       tk=D_IN_PAD, out_dtypes=(BF16,))
        below = dict(branch3=saved[l - 1]["f"], mod_gate=mods[l - 1], gate_row=GATE2) if l > 0 else {}
        node = carrying(l, "bwd_resnode_mix", _resnode_bwd, s["x_in"], w["g_mix"], name=f"l{l}_resnode_mix",
                        dh3=cube(dh1), dres3=dx_mid, mod_nm=mod, rows=(SHIFT1, SCALE1), **below)
        grads[l]["norm_mix_g"] = node["dg"][0]
        dshift1, dscale1 = node["dnm"][:, 0], node["dnm"][:, 1]
        dmods[l] = jnp.stack([dshift1, dscale1, dgate1, dshift2, dscale2, dgate2], axis=1)
        plan.layer_done(l)
    return node["dx"], dmods


W_NAMES = ("w_ada", "b_ada", "norm_mix_g", "w_in", "gmlp_ws", "gmlp_bs", "mla_q_norm_g", "mla_kv_norm_g",
           "mla_w_uq", "mla_w_ukv", "out_norm_gmlp_g", "out_norm_mla_g", "w_out", "norm_ffn_g", "w_ff1", "w_ff2",
           "final_norm_g")
FLAT_KEY = {"w_in": "w_in", "w_uq": "mla_w_uq", "w_ukv": "mla_w_ukv", "w_out": "w_out", "w_ff1": "w_ff1",
            "w_ff2": "w_ff2"}
COL_SHARDED = ("w_in", "w_uq", "w_ukv", "w_ff1")
FULL_SHAPE = {"w_in": (D_MODEL, D_IN), "w_uq": (Q_RANK, HEADS * (NOPE + ROPE)), "w_ukv": (KV_RANK, HEADS * 128),
              "w_out": (D_MODEL, D_MODEL), "w_ff1": (D_MODEL, D_FF), "w_ff2": (D_FF, D_MODEL)}
SMALL_LAYER_NAMES = ("norm_mix_g", "gmlp_ws", "gmlp_bs", "mla_q_norm_g", "mla_kv_norm_g", "out_norm_gmlp_g",
                     "out_norm_mla_g", "norm_ffn_g")


def _silu(v):
    return v * (1.0 / (1.0 + jnp.exp(-v)))


class _CommPlan:
    FWD = {"fwd_attn": ("ff", 0, "spread"), "fwd_out_a": ("ff", 0, "pass"),
           "fwd_ff1": ("mix", 1, "spread"), "fwd_ff2": ("mix", 1, "pass")}
    BWD = {"bwd_d_r": ("mix", 1), "bwd_dw_ff2": ("mix", 1), "bwd_dw_ff1": ("mix", 1),
           "bwd_d_h2": ("ff", 0), "bwd_attn_dkv": ("ff", 0), "bwd_dw_uq": ("ff", 0)}
    BWD_LAST = {"bwd_d_h1": ("mix", 0), "bwd_resnode_mix": ("mix", 0)}
    SMALL = {"bwd_d_y": "spread", "bwd_dw_out": "pass"}

    def __init__(self, weights, ids, dev, core):
        self.weights, self.ids, self.dev, self.core = weights, ids, dev, core
        self.used, self.rows = _flat_rows()
        self.flat = {("mix", l): self._flat_mix(l) for l in range(DEPTH)}
        self.flat.update({("ff", l): jnp.concatenate([weights["w_ff1"][l], weights["w_ff2"][l]], axis=0).astype(BF16)
                          for l in range(DEPTH)})
        self.lw, self.rs, self.grads, self.spread = {}, {}, {}, {}
        self.small_vec, self.small_sum, self.small_spread, self.extra = {}, {}, None, {}
        self.lw = {l: _small_weights(weights, l) for l in range(DEPTH)}

    def _flat_mix(self, l):
        pieces = []
        for nm, _ in FSDP_SECTIONS:
            shard = self.weights[FLAT_KEY[nm]][l]
            pieces.append(shard.T if nm == "w_in" else shard.reshape(-1, FLAT_W))
        pieces.append(jnp.zeros((self.rows - self.used, FLAT_W), F32))
        return jnp.concatenate(pieces, axis=0).astype(BF16)

    def _arrived(self, group, l, gath):
        flat = self.flat[group, l]
        hr = flat.shape[0] // 2
        mine = lax.dynamic_slice(flat, (self.core * hr, 0), (hr, FLAT_W))
        gath = lax.dynamic_update_slice(gath, mine[None], (self.dev, 0, 0))
        if group == "ff":
            self.lw[l]["ff"] = gath.reshape(N_CHIPS, 2, hr, FLAT_W)
            return
        w_gath = gath.reshape(N_CHIPS, self.rows, FLAT_W)
        full, off = {}, 0
        for nm, nrows in FSDP_SECTIONS:
            sec = w_gath[:, off:off + nrows]
            off += nrows
            rows, cols = FULL_SHAPE[nm]
            if nm == "w_in":
                full["w_in_t"] = sec.reshape(cols, rows)
            else:
                full[FLAT_KEY[nm]] = (_chunks_to_cols(sec, rows, cols) if nm in COL_SHARDED
                                      else sec.reshape(rows, cols))
        self.lw[l].update(_mix_weights(full))

    def layer(self, l):
        return self.lw[l]

    def host(self, l, tag):
        if tag == "fwd_normmod1":
            return _gather_spread(self.flat["mix", 0]) if l == 0 else None
        if tag in self.FWD:
            group, ahead, what = self.FWD[tag]
            if l + ahead >= DEPTH:
                return None
            return _gather_spread(self.flat[group, l + ahead]) if what == "spread" else _gather_pass_on(self.spread[group])
        if tag in self.SMALL:
            if l + 1 not in self.small_vec:
                return None
            if self.SMALL[tag] == "spread":
                return _gather_spread(self.small_vec[l + 1], halves=False)
            return _gather_pass_on(self.small_spread)
        rs = self._rs_for(l, tag)
        return None if rs is None or rs.stage > 2 else rs.next_exchange()

    def _rs_for(self, l, tag):
        if tag in self.BWD_LAST:
            return self.rs.get(self.BWD_LAST[tag]) if l == 0 else None
        group, ahead = self.BWD[tag]
        return self.rs.get((group, l + ahead))

    def hosted(self, l, tag, outs):
        if tag == "fwd_normmod1":
            self._arrived("mix", 0, _run_exchange(_gather_pass_on(outs[0]), name="l0_mix_gather_pass_on")[0])
        elif tag in self.FWD:
            group, ahead, what = self.FWD[tag]
            if what == "spread":
                self.spread[group] = outs[0]
            else:
                self._arrived(group, l + ahead, outs[0])
        elif tag in self.SMALL:
            if self.SMALL[tag] == "spread":
                self.small_spread = outs[0]
            else:
                self._small_arrived(l + 1, outs[0])
        else:
            self._rs_for(l, tag).done(outs)

    def ff_grads(self, l, g_ff):
        self.rs["ff", l] = _ReduceScatter(g_ff, self.ids, f"l{l}_ff_rs")

    def layer_grads(self, l, grads):
        self.grads[l] = grads
        pieces = []
        for nm, nrows in FSDP_SECTIONS:
            if nm == "w_in":
                pieces.append(grads["w_in_t"].reshape(N_CHIPS, nrows, FLAT_W))
                continue
            g = grads[FLAT_KEY[nm]]
            pieces.append(_cols_to_chunks(g) if nm in COL_SHARDED else g.reshape(N_CHIPS, nrows, FLAT_W))
        pieces.append(jnp.zeros((N_CHIPS, self.rows - self.used, FLAT_W), F32))
        self.rs["mix", l] = _ReduceScatter(jnp.concatenate(pieces, axis=1), self.ids, f"l{l}_mix_rs")

    def scalars(self, loss_part, d_final_g):
        self.extra = {0: [loss_part[None]]}
        self.extra.setdefault(DEPTH - 1, []).insert(0, d_final_g)

    def layer_done(self, l):
        if l == 0:
            self.rs["mix", 0].finish_alone()
        parts = [self.grads[l][nm].reshape(-1) for nm in SMALL_LAYER_NAMES] + self.extra.get(l, [])
        vec = jnp.concatenate(parts)
        rows = -(-vec.shape[0] // (8 * FLAT_W)) * 8
        self.small_vec[l] = jnp.pad(vec, (0, rows * FLAT_W - vec.shape[0])).reshape(rows, FLAT_W)
        if l == 0:
            (gath,) = _run_exchange(_gather_spread(self.small_vec[0], halves=False), name="l0_small_spread")
            self._small_arrived(0, _run_exchange(_gather_pass_on(gath), name="l0_small_pass_on")[0])

    def _small_arrived(self, l, gath):
        gath = lax.dynamic_update_slice(gath, self.small_vec[l][None], (self.dev, 0, 0))
        self.small_sum[l] = _sum_leading(gath, name=f"l{l}_small_sum").reshape(-1)

    def small_grads(self):
        out = {nm: [] for nm in SMALL_LAYER_NAMES}
        for l in range(DEPTH):
            off = 0
            for nm in SMALL_LAYER_NAMES:
                size = self.weights[nm][l].size
                out[nm].append(self.small_sum[l][off:off + size].reshape(self.weights[nm].shape[1:]))
                off += size
            if l == DEPTH - 1:
                final = self.small_sum[l][off:off + self.weights["final_norm_g"].size]
                off += final.shape[0]
            if l == 0:
                loss = self.small_sum[l][off]
        res = {nm: jnp.stack(parts, axis=0) for nm, parts in out.items()}
        res["final_norm_g"] = final
        return loss, res

    def mix_grads(self):
        per = {FLAT_KEY[nm]: [] for nm, _ in FSDP_SECTIONS}
        for l in range(DEPTH):
            shard, off = self.rs["mix", l].result, 0
            for nm, nrows in FSDP_SECTIONS:
                key = FLAT_KEY[nm]
                sec = shard[off:off + nrows]
                per[key].append(sec.T if nm == "w_in" else sec.reshape(self.weights[key].shape[1:]))
                off += nrows
        return {key: jnp.stack(parts, axis=0) for key, parts in per.items()}

    def ff_shards(self):
        return [self.rs["ff", l].result for l in range(DEPTH)]


def kernel(x, c, positions, w_ada, b_ada, norm_mix_g, w_in, gmlp_ws, gmlp_bs, mla_q_norm_g, mla_kv_norm_g, mla_w_uq, mla_w_ukv, out_norm_gmlp_g, out_norm_mla_g, w_out, norm_ffn_g, w_ff1, w_ff2, final_norm_g, loss_target, m_w_ada, m_b_ada, m_norm_mix_g, m_w_in, m_gmlp_ws, m_gmlp_bs, m_mla_q_norm_g, m_mla_kv_norm_g, m_mla_w_uq, m_mla_w_ukv, m_out_norm_gmlp_g, m_out_norm_mla_g, m_w_out, m_norm_ffn_g, m_w_ff1, m_w_ff2, m_final_norm_g, v_w_ada, v_b_ada, v_norm_mix_g, v_w_in, v_gmlp_ws, v_gmlp_bs, v_mla_q_norm_g, v_mla_kv_norm_g, v_mla_w_uq, v_mla_w_ukv, v_out_norm_gmlp_g, v_out_norm_mla_g, v_w_out, v_norm_ffn_g, v_w_ff1, v_w_ff2, v_final_norm_g):
    weights = dict(w_ada=w_ada, b_ada=b_ada, norm_mix_g=norm_mix_g, w_in=w_in, gmlp_ws=gmlp_ws, gmlp_bs=gmlp_bs,
                   mla_q_norm_g=mla_q_norm_g, mla_kv_norm_g=mla_kv_norm_g, mla_w_uq=mla_w_uq, mla_w_ukv=mla_w_ukv,
                   out_norm_gmlp_g=out_norm_gmlp_g, out_norm_mla_g=out_norm_mla_g, w_out=w_out,
                   norm_ffn_g=norm_ffn_g, w_ff1=w_ff1, w_ff2=w_ff2, final_norm_g=final_norm_g)
    mom_m = dict(zip(W_NAMES, (m_w_ada, m_b_ada, m_norm_mix_g, m_w_in, m_gmlp_ws, m_gmlp_bs, m_mla_q_norm_g,
                               m_mla_kv_norm_g, m_mla_w_uq, m_mla_w_ukv, m_out_norm_gmlp_g, m_out_norm_mla_g,
                               m_w_out, m_norm_ffn_g, m_w_ff1, m_w_ff2, m_final_norm_g)))
    mom_v = dict(zip(W_NAMES, (v_w_ada, v_b_ada, v_norm_mix_g, v_w_in, v_gmlp_ws, v_gmlp_bs, v_mla_q_norm_g,
                               v_mla_kv_norm_g, v_mla_w_uq, v_mla_w_ukv, v_out_norm_gmlp_g, v_out_norm_mla_g,
                               v_w_out, v_norm_ffn_g, v_w_ff1, v_w_ff2, v_final_norm_g)))
    bsz, seq, d = x.shape
    px, py, pc = _position()
    chip = 2 * px + py
    dev = 2 * chip + pc
    ids = jnp.stack([pc, chip]).astype(jnp.int32)
    n_ex = N_DEV * bsz
    ada_cols = w_ada.shape[-1]

    c_all = _allgather8(c.reshape(bsz * d // 128, 128), name="gather_c").reshape(n_ex, d)
    mod_parts = []
    for l in range(DEPTH):
        bias = lax.dynamic_slice(b_ada[l], (chip * ada_cols,), (ada_cols,))[None]
        mod_parts.append(_mm(c_all, w_ada, dims="nn", name=f"l{l}_mod", tm=n_ex, tn=ada_cols, tk=d, n=ada_cols,
                             b_block=((None, d, ada_cols), lambda i, j, k, l=l: (l, k, j)),
                             epilogue=lambda acc, bv: (acc + bv,), extras=(bias,),
                             extra_specs=(pl.BlockSpec((1, ada_cols), lambda i, j, k: (0, j)),), a_fn=_silu))
    mod_g = _allgather8(jnp.concatenate(mod_parts, axis=0), name="gather_mod")
    mod_g = mod_g.reshape(N_CHIPS, 2, DEPTH, n_ex, ada_cols)[:, 0]
    mod_full = mod_g.transpose(1, 2, 0, 3).reshape(DEPTH, n_ex, N_CHIPS * ada_cols)
    mod_mine = lax.dynamic_slice(mod_full, (0, dev * bsz, 0), (DEPTH, bsz, N_MOD * d))
    mod_mine = jnp.pad(mod_mine.reshape(DEPTH, bsz, N_MOD, d), ((0, 0), (0, 0), (0, MOD_ROWS - N_MOD), (0, 0)))
    mods = [mod_mine[l] for l in range(DEPTH)]

    plan = _CommPlan(weights, ids, dev, pc)
    grad_x, dmods = _local_step(x, loss_target, positions, mods, final_norm_g, plan)
    grad = plan.mix_grads()

    loss, small = plan.small_grads()
    grad.update(small)

    dmod = jnp.stack(dmods, axis=1).reshape(bsz * DEPTH * N_MOD, d)
    dmod_all = _allgather8(dmod, name="gather_dmod").reshape(n_ex, DEPTH, N_MOD * d)
    gw, gb = [], []
    for l in range(DEPTH):
        dm = dmod_all[:, l]
        dm_cols = lax.dynamic_slice(dm, (0, chip * ada_cols), (n_ex, ada_cols))
        gw.append(_mm(c_all, dm_cols, dims="tn", name=f"l{l}_dw_ada", tm=d, tn=ada_cols, tk=n_ex, a_fn=_silu,
                      out_into=(w_ada.shape, (None, d, ada_cols), lambda i, j, k, l=l: (l, i, j),
                                gw[-1] if gw else None)))
        gb.append(_sum_leading(dm.reshape(n_ex, N_MOD * d // FLAT_W, FLAT_W), name=f"l{l}_db_ada").reshape(-1))
    grad["w_ada"] = gw[-1]
    grad["b_ada"] = jnp.stack(gb, axis=0)

    delta, new_m, new_v = {}, {}, {}
    ff_bufs = plan.ff_shards()
    for nm, row_off in (("w_ff1", 0), ("w_ff2", FLAT_W)):
        grad[nm], delta[nm], new_m[nm], new_v[nm] = _adamw_layers(
            weights[nm], mom_m[nm], mom_v[nm], ff_bufs, row_off, name=f"adamw_{nm}")
    for nm in W_NAMES:
        if nm not in delta:
            delta[nm], new_m[nm], new_v[nm] = _adamw(weights[nm], grad[nm], mom_m[nm], mom_v[nm],
                                                     name=f"adamw_{nm}")
    return (loss, grad_x, *[grad[nm] for nm in W_NAMES], *[delta[nm] for nm in W_NAMES],
            *[new_m[nm] for nm in W_NAMES], *[new_v[nm] for nm in W_NAMES])
```

```python
import functools
import math

import jax
import jax.numpy as jnp
from jax import lax
from jax.experimental import pallas as pl
from jax.experimental.pallas import tpu as pltpu

F32 = jnp.float32
BF16 = jnp.bfloat16

D_MODEL = 1024
DEPTH = 2
D_GMLP = 512
GROUPS = 8
GROUP_DIM = 64
CHUNK = 128
HEADS = 8
NOPE = 64
ROPE = 32
HEAD_PAD = 128
Q_RANK = 256
KV_RANK = 128
D_FF = 4096
N_MOD = 6
MOD_ROWS = 8
EPS = 1e-6
ROPE_THETA = 10000.0
D_IN = 1440
D_IN_PAD = 1536
ATTN_SCALE = (NOPE + ROPE) ** -0.5
LOG2E = math.log2(math.e)
SCALE_LOG2 = ATTN_SCALE * LOG2E
N_CHIPS = 4
N_DEV = 8

ADAM_LR = 0.001
ADAM_B1 = 0.9
ADAM_B2 = 0.999
ADAM_EPS = 1e-08
ADAM_WD = 0.01
ADAM_STEP = 10

VMEM_LIMIT = 48 * 1024 * 1024
FLAT_W = 1024
ROW_ALIGN = 256

NN = (((1,), (0,)), ((), ()))
NT = (((1,), (1,)), ((), ()))
TN = (((0,), (0,)), ((), ()))
MESH = pl.DeviceIdType.MESH

SHIFT1, SCALE1, GATE1, SHIFT2, SCALE2, GATE2 = range(6)

FSDP_SECTIONS = (("w_out", 256), ("w_in", 360), ("w_uq", 48), ("w_ukv", 32))


def _cparams(vmem=VMEM_LIMIT):
    return pltpu.CompilerParams(vmem_limit_bytes=vmem)


def _dot(a, b, dims=NN):
    return lax.dot_general(a, b, dims, preferred_element_type=F32)


def _iota(shape, axis):
    return lax.broadcasted_iota(jnp.int32, shape, axis)


def _gelu(x):
    k = math.sqrt(2.0 / math.pi)
    return 0.5 * x * (1.0 + jnp.tanh(k * (x + 0.044715 * (x * x * x))))


def _gelu_and_grad(x):
    k = math.sqrt(2.0 / math.pi)
    x2 = x * x
    t = jnp.tanh(k * (x + 0.044715 * (x2 * x)))
    half = 0.5 * (1.0 + t)
    return x * half, half + 0.5 * x * (1.0 - t * t) * (k * (1.0 + 3.0 * 0.044715 * x2))


def _rms_fwd(x, g, n):
    r = lax.rsqrt(jnp.sum(x * x, axis=-1, keepdims=True) * (1.0 / n) + EPS)
    return x * r * g


def _rms_bwd(x, g, dy, n):
    r = lax.rsqrt(jnp.sum(x * x, axis=-1, keepdims=True) * (1.0 / n) + EPS)
    xh = x * r
    dxh = dy * g
    dx = r * (dxh - xh * (jnp.sum(dxh * xh, axis=-1, keepdims=True) * (1.0 / n)))
    dg = jnp.sum(dy * xh, axis=0, keepdims=True)
    return dx, dg


def _pick_rows(rows, limit):
    if rows <= limit:
        return rows
    for t in range(limit, 7, -8):
        if rows % t == 0:
            return t
    return rows


def _mm(a, b, *, dims, name, tm=512, tn=1024, tk=1024, out_dtypes=(F32,), epilogue=None,
        extras=(), extra_specs=(), a_fn=None, weights_outer=False, side=None, b_block=None, n=None,
        out_into=None):
    if dims == "tn":
        kk, m = a.shape
    else:
        m, kk = a.shape
    if n is None:
        n = b.shape[0] if dims == "nt" else b.shape[1]
    tm, tn, tk = min(tm, m), min(tn, n), min(tk, kk)
    assert m % tm == 0 and n % tn == 0 and kk % tk == 0, (name, a.shape, b.shape, tm, tn, tk)
    ni, nj, nk = m // tm, n // tn, kk // tk

    def spec(shape, pick):
        if weights_outer:
            return pl.BlockSpec(shape, lambda j, i, k: pick(i, j, k))
        return pl.BlockSpec(shape, pick)

    if dims == "tn":
        a_spec = spec((tk, tm), lambda i, j, k: (k, i))
    else:
        a_spec = spec((tm, tk), lambda i, j, k: (i, k))
    if b_block is not None:
        b_spec = spec(*b_block)
    elif dims == "nt":
        b_spec = spec((tn, tk), lambda i, j, k: (j, k))
    else:
        b_spec = spec((tk, tn), lambda i, j, k: (k, j))
    o_spec = spec((tm, tn), lambda i, j, k: (i, j))
    out_shape = [jax.ShapeDtypeStruct((m, n), dt) for dt in out_dtypes]
    out_specs = [o_spec] * len(out_dtypes)
    prev, io_aliases = (), {}
    if out_into is not None:
        full_shape, block, index, before = out_into
        assert len(out_dtypes) == 1 and not extras
        out_shape = [jax.ShapeDtypeStruct(full_shape, out_dtypes[0])]
        out_specs = [spec(block, index)]
        if before is not None:
            prev, io_aliases = (before,), {2: 0}
    assert not (weights_outer and extra_specs)
    dn = {"nn": NN, "nt": NT, "tn": TN}[dims]
    n_ex, n_out = len(extras), len(out_dtypes)
    e_specs = [o_spec if s is None else s for s in (tuple(extra_specs) + (None,) * n_ex)[:n_ex]]

    n_prev = len(prev)

    def body(*refs):
        a_ref, b_ref = refs[0], refs[1]
        e_refs = refs[2 + n_prev:2 + n_prev + n_ex]
        o_refs = refs[2 + n_prev + n_ex:2 + n_prev + n_ex + n_out]
        av = a_ref[...]
        if a_fn is not None:
            av = a_fn(av)
        bv = b_ref[...]
        if bv.ndim == 3:
            if dims == "nt":
                bv = jnp.concatenate([bv[c] for c in range(bv.shape[0])], axis=1)
            else:
                bv = bv.reshape(-1, bv.shape[-1])
        part = _dot(av.astype(BF16), bv.astype(BF16), dn)

        def finish(acc):
            outs = (acc,) if epilogue is None else epilogue(acc, *[e[...] for e in e_refs])
            for o_ref, o in zip(o_refs, outs):
                o_ref[...] = o.astype(o_ref.dtype)

        if nk == 1:
            finish(part)
        else:
            acc_ref = refs[-1]
            k = pl.program_id(2)

            @pl.when(k == 0)
            def _():
                acc_ref[...] = part

            @pl.when(k > 0)
            def _():
                acc_ref[...] += part

            @pl.when(k == nk - 1)
            def _():
                finish(acc_ref[...])

    outs, side_outs = _hosted_call(
        body, name=name, grid=(nj, ni, nk) if weights_outer else (ni, nj, nk),
        in_specs=[a_spec, b_spec] + [ANY_SPEC] * n_prev + e_specs,
        out_specs=out_specs, out_shape=out_shape,
        scratch_shapes=[pltpu.VMEM((tm, tn), F32)] if nk > 1 else [],
        args=(a, b, *prev, *extras), side=side, io_aliases=io_aliases)
    res = outs[0] if n_out == 1 else outs
    return res if side is None else (res, side_outs)


def _mod_spec(tm, tn, seq):
    return pl.BlockSpec((1, MOD_ROWS, tn), lambda i, j, k: ((i * tm) // seq, 0, j))


def _normmod_fwd(x3, g, mod, shift_row, scale_row, *, name, tb=512, side=None):
    bsz, seq, d = x3.shape
    tb = min(tb, seq)

    def body(x_ref, g_ref, mod_ref, h_ref):
        m = mod_ref[0]
        nrm = _rms_fwd(x_ref[0], g_ref[...], d)
        h = nrm * (1.0 + m[scale_row:scale_row + 1, :]) + m[shift_row:shift_row + 1, :]
        h_ref[0] = h.astype(BF16)

    outs, side_outs = _hosted_call(
        body, name=name, grid=(bsz, seq // tb),
        in_specs=[pl.BlockSpec((1, tb, d), lambda b, i: (b, i, 0)),
                  pl.BlockSpec((1, d), lambda b, i: (0, 0)),
                  pl.BlockSpec((1, MOD_ROWS, d), lambda b, i: (b, 0, 0))],
        out_specs=[pl.BlockSpec((1, tb, d), lambda b, i: (b, i, 0))],
        out_shape=[jax.ShapeDtypeStruct((bsz, seq, d), BF16)],
        args=(x3, g, mod), side=side)
    return outs[0] if side is None else (outs[0], side_outs)


def _pair_mean_exact(x, lo):
    s_lo = jnp.sum(jnp.where(lo, x, 0.0), axis=-1, keepdims=True)
    s_hi = jnp.sum(jnp.where(lo, 0.0, x), axis=-1, keepdims=True)
    return jnp.where(lo, s_lo, s_hi) * (1.0 / GROUP_DIM)


def _gmlp_pair_fwd(gv_p, w0, w1, bias, lo):
    mu = _pair_mean_exact(gv_p, lo)
    dlt = gv_p - mu
    var = _pair_mean_exact(dlt * dlt, lo)
    rstd = lax.rsqrt(var + EPS)
    vn = dlt * rstd
    vnb = vn.astype(BF16)
    mixed = jnp.where(lo, _dot(w0, vnb), _dot(w1, vnb)) + bias
    return vn, vnb, rstd, mixed


def _tril_bf16(w):
    t = w.shape[-1]
    return jnp.where(_iota((t, t), 1) <= _iota((t, t), 0), w, 0.0).astype(BF16)


def _gmlp_fwd(z3, ws, bexp, g_out, *, name):
    bsz, seq, _ = z3.shape
    nc = seq // CHUNK

    def body(u_ref, v_ref, ws_ref, b_ref, g_ref, y_ref):
        lo = _iota((CHUNK, 128), 1) < GROUP_DIM
        gu = _gelu(u_ref[0].astype(F32))
        gv = _gelu(v_ref[0].astype(F32))
        parts = []
        for p in range(GROUPS // 2):
            sl = slice(128 * p, 128 * p + 128)
            w0 = _tril_bf16(ws_ref[2 * p])
            w1 = _tril_bf16(ws_ref[2 * p + 1])
            _, _, _, mixed = _gmlp_pair_fwd(gv[:, sl], w0, w1, b_ref[p], lo)
            parts.append(gu[:, sl] * mixed)
        yg = jnp.concatenate(parts, axis=1)
        y_ref[0] = _rms_fwd(yg, g_ref[...], D_GMLP).astype(BF16)

    return pl.pallas_call(
        body, name=name, grid=(bsz, nc),
        in_specs=[pl.BlockSpec((1, CHUNK, D_GMLP), lambda b, i: (b, i, 0)),
                  pl.BlockSpec((1, CHUNK, D_GMLP), lambda b, i: (b, i, 1)),
                  pl.BlockSpec((GROUPS, CHUNK, CHUNK), lambda b, i: (0, 0, 0)),
                  pl.BlockSpec((GROUPS // 2, CHUNK, 128), lambda b, i: (0, 0, 0)),
                  pl.BlockSpec((1, D_GMLP), lambda b, i: (0, 0))],
        out_specs=pl.BlockSpec((1, CHUNK, D_GMLP), lambda b, i: (b, i, 0)),
        out_shape=jax.ShapeDtypeStruct((bsz, seq, D_GMLP), BF16),
        compiler_params=_cparams(),
    )(z3, z3, ws, bexp, g_out)


def _gmlp_bwd(z3, dyn3, ws, wst, bexp, g_out, *, name, dy_col):
    bsz, seq, _ = z3.shape
    nc = seq // CHUNK
    npair = GROUPS // 2

    def body(u_ref, v_ref, dy_ref, ws_ref, wst_ref, b_ref, g_ref, duv_ref, dws_ref, dbs_ref, dg_ref, dbacc):
        first = jnp.logical_and(pl.program_id(0) == 0, pl.program_id(1) == 0)
        last = jnp.logical_and(pl.program_id(0) == bsz - 1, pl.program_id(1) == nc - 1)

        @pl.when(first)
        def _():
            dws_ref[...] = jnp.zeros_like(dws_ref)
            dg_ref[...] = jnp.zeros_like(dg_ref)
            dbacc[...] = jnp.zeros_like(dbacc)

        lo = _iota((CHUNK, 128), 1) < GROUP_DIM
        tril = _iota((CHUNK, CHUNK), 1) <= _iota((CHUNK, CHUNK), 0)
        u = u_ref[0].astype(F32)
        v = v_ref[0].astype(F32)
        gu, dgu = _gelu_and_grad(u)
        gv, dgv_dv = _gelu_and_grad(v)
        fwd = []
        for p in range(npair):
            sl = slice(128 * p, 128 * p + 128)
            w0 = _tril_bf16(ws_ref[2 * p])
            w1 = _tril_bf16(ws_ref[2 * p + 1])
            fwd.append(_gmlp_pair_fwd(gv[:, sl], w0, w1, b_ref[p], lo))
        yg = jnp.concatenate([gu[:, 128 * p:128 * p + 128] * fwd[p][3] for p in range(npair)], axis=1)
        dyg, dg = _rms_bwd(yg, g_ref[...], dy_ref[0].astype(F32), D_GMLP)
        dg_ref[...] += dg
        du_parts, dv_parts = [], []
        for p in range(npair):
            sl = slice(128 * p, 128 * p + 128)
            vn, vnb, rstd, mixed = fwd[p]
            dyg_p = dyg[:, sl]
            dmixed = dyg_p * gu[:, sl]
            dbacc[p] += dmixed
            dm_lo = jnp.where(lo, dmixed, 0.0).astype(BF16)
            dm_hi = jnp.where(lo, 0.0, dmixed).astype(BF16)
            dws_ref[2 * p] += jnp.where(tril, _dot(dm_lo, vnb, NT), 0.0)
            dws_ref[2 * p + 1] += jnp.where(tril, _dot(dm_hi, vnb, NT), 0.0)
            dmb = dmixed.astype(BF16)
            dvn = jnp.where(lo, _dot(wst_ref[2 * p], dmb), _dot(wst_ref[2 * p + 1], dmb))
            dgv = rstd * (dvn - _pair_mean_exact(dvn, lo) - vn * _pair_mean_exact(dvn * vn, lo))
            dv_parts.append(dgv * dgv_dv[:, sl])
            du_parts.append(dyg_p * mixed * dgu[:, sl])
        duv_ref[0] = jnp.concatenate(du_parts + dv_parts, axis=1).astype(BF16)

        @pl.when(last)
        def _():
            sel = jnp.where(_iota((8, 128), 0) == 0, (_iota((8, 128), 1) < GROUP_DIM).astype(F32),
                            jnp.where(_iota((8, 128), 0) == 1, (_iota((8, 128), 1) >= GROUP_DIM).astype(F32), 0.0))
            for p in range(npair):
                dbs_ref[p] = lax.dot_general(sel, dbacc[p], NT, precision=lax.Precision.HIGHEST,
                                             preferred_element_type=F32)

    duv, dws, dbs, dg = pl.pallas_call(
        body, name=name, grid=(bsz, nc),
        in_specs=[pl.BlockSpec((1, CHUNK, D_GMLP), lambda b, i: (b, i, 0)),
                  pl.BlockSpec((1, CHUNK, D_GMLP), lambda b, i: (b, i, 1)),
                  pl.BlockSpec((1, CHUNK, D_GMLP), lambda b, i: (b, i, dy_col)),
                  pl.BlockSpec((GROUPS, CHUNK, CHUNK), lambda b, i: (0, 0, 0)),
                  pl.BlockSpec((GROUPS, CHUNK, CHUNK), lambda b, i: (0, 0, 0)),
                  pl.BlockSpec((npair, CHUNK, 128), lambda b, i: (0, 0, 0)),
                  pl.BlockSpec((1, D_GMLP), lambda b, i: (0, 0))],
        out_specs=[pl.BlockSpec((1, CHUNK, 2 * D_GMLP), lambda b, i: (b, i, 0)),
                   pl.BlockSpec((GROUPS, CHUNK, CHUNK), lambda b, i: (0, 0, 0)),
                   pl.BlockSpec((npair, 8, CHUNK), lambda b, i: (0, 0, 0)),
                   pl.BlockSpec((1, D_GMLP), lambda b, i: (0, 0))],
        out_shape=[jax.ShapeDtypeStruct((bsz, seq, D_IN_PAD), BF16),
                   jax.ShapeDtypeStruct((GROUPS, CHUNK, CHUNK), F32),
                   jax.ShapeDtypeStruct((npair, 8, CHUNK), F32),
                   jax.ShapeDtypeStruct((1, D_GMLP), F32)],
        scratch_shapes=[pltpu.VMEM((npair, CHUNK, 128), F32)],
        compiler_params=_cparams(),
    )(z3, z3, dyn3, ws, wst, bexp, g_out)
    return duv, dws, dbs[:, :2, :].reshape(GROUPS, CHUNK), dg


def _partner(x):
    width = x.shape[-1]
    lane = _iota(x.shape, x.ndim - 1) % HEAD_PAD
    up = pltpu.roll(x, width - ROPE // 2, x.ndim - 1)
    down = pltpu.roll(x, ROPE // 2, x.ndim - 1)
    first = jnp.logical_and(lane >= NOPE, lane < NOPE + ROPE // 2)
    second = jnp.logical_and(lane >= NOPE + ROPE // 2, lane < NOPE + ROPE)
    return jnp.where(first, up, jnp.where(second, down, 0.0))


def _mla_prep_fwd(z3, g_q, g_kv, w_uq, w_ukv, ctab, stab, *, name, tb=256):
    bsz, seq, _ = z3.shape
    tb = min(tb, seq)
    hw = HEADS * HEAD_PAD

    def body(ql_ref, kvl_ref, krl_ref, gq_ref, gkv_ref, wuq_ref, wukv_ref, c_ref, s_ref, q_ref, kv_ref, kp_ref):
        cq = _rms_fwd(ql_ref[0].astype(F32), gq_ref[...], Q_RANK).astype(BF16)
        q = _dot(cq, wuq_ref[...])
        c1, s1 = c_ref[0], s_ref[0]
        c8, s8 = jnp.tile(c1, (1, HEADS)), jnp.tile(s1, (1, HEADS))
        q_ref[0] = ((q * c8 + _partner(q) * s8) * SCALE_LOG2).astype(BF16)
        ckv = _rms_fwd(kvl_ref[0].astype(F32), gkv_ref[...], KV_RANK).astype(BF16)
        kv = _dot(ckv, wukv_ref[...])
        kv_ref[0] = kv.astype(BF16)
        kr = krl_ref[0].astype(F32)
        kr = kr * c1 + _partner(kr) * s1
        lane = _iota((tb, hw), 1) % HEAD_PAD
        kp_ref[0] = jnp.where(lane < NOPE, kv, jnp.tile(kr, (1, HEADS))).astype(BF16)

    return pl.pallas_call(
        body, name=name, grid=(bsz, seq // tb),
        in_specs=[pl.BlockSpec((1, tb, Q_RANK), lambda b, i: (b, i, 4)),
                  pl.BlockSpec((1, tb, KV_RANK), lambda b, i: (b, i, 10)),
                  pl.BlockSpec((1, tb, HEAD_PAD), lambda b, i: (b, i, 11)),
                  pl.BlockSpec((1, Q_RANK), lambda b, i: (0, 0)),
                  pl.BlockSpec((1, KV_RANK), lambda b, i: (0, 0)),
                  pl.BlockSpec((Q_RANK, hw), lambda b, i: (0, 0)),
                  pl.BlockSpec((KV_RANK, hw), lambda b, i: (0, 0)),
                  pl.BlockSpec((1, tb, HEAD_PAD), lambda b, i: (b, i, 0)),
                  pl.BlockSpec((1, tb, HEAD_PAD), lambda b, i: (b, i, 0))],
        out_specs=[pl.BlockSpec((1, tb, hw), lambda b, i: (b, i, 0))] * 3,
        out_shape=[jax.ShapeDtypeStruct((bsz, seq, hw), BF16)] * 3,
        compiler_params=_cparams(),
    )(z3, z3, z3, g_q, g_kv, w_uq, w_ukv, ctab, stab)


def _mla_prep_bwd(z3, dz3, dq3, dk3, dv3, g_q, g_kv, w_uq, w_ukv, ctab, stab, *, name, tb=256):
    bsz, seq, _ = z3.shape
    tb = min(tb, seq)
    hw = HEADS * HEAD_PAD
    nb = seq // tb

    def body(ql_ref, kvl_ref, dq_ref, dk_ref, dv_ref, gq_ref, gkv_ref, wuq_ref, wukv_ref, c_ref, s_ref, dz_in,
             dz_ref, cq_ref, dqb_ref, ckv_ref, dkvb_ref, dgq_ref, dgkv_ref):
        @pl.when(jnp.logical_and(pl.program_id(0) == 0, pl.program_id(1) == 0))
        def _():
            dgq_ref[...] = jnp.zeros_like(dgq_ref)
            dgkv_ref[...] = jnp.zeros_like(dgkv_ref)

        c1, s1 = c_ref[0], s_ref[0]
        c8, s8 = jnp.tile(c1, (1, HEADS)), jnp.tile(s1, (1, HEADS))
        dqr = dq_ref[0]
        dqb = (dqr * c8 + _partner(dqr * s8)).astype(BF16)
        dqb_ref[0] = dqb
        ql = ql_ref[0].astype(F32)
        cq_ref[0] = _rms_fwd(ql, gq_ref[...], Q_RANK).astype(BF16)
        dql, dgq = _rms_bwd(ql, gq_ref[...], _dot(dqb, wuq_ref[...], NT), Q_RANK)
        dgq_ref[...] += dgq

        dk = dk_ref[0]
        lane = _iota((tb, hw), 1) % HEAD_PAD
        dkvb = jnp.where(lane < NOPE, dk, dv_ref[0]).astype(BF16)
        dkvb_ref[0] = dkvb
        kvl = kvl_ref[0].astype(F32)
        ckv_ref[0] = _rms_fwd(kvl, gkv_ref[...], KV_RANK).astype(BF16)
        dkvl, dgkv = _rms_bwd(kvl, gkv_ref[...], _dot(dkvb, wukv_ref[...], NT), KV_RANK)
        dgkv_ref[...] += dgkv

        dkr = dk[:, 0:HEAD_PAD].astype(F32)
        for h in range(1, HEADS):
            dkr = dkr + dk[:, HEAD_PAD * h:HEAD_PAD * (h + 1)].astype(F32)
        lane1 = _iota((tb, HEAD_PAD), 1)
        dkr = jnp.where(jnp.logical_and(lane1 >= NOPE, lane1 < NOPE + ROPE), dkr, 0.0)
        dkrl = dkr * c1 + _partner(dkr * s1)
        dz_ref[0] = jnp.concatenate([dql, dkvl, dkrl], axis=1).astype(BF16)

    return pl.pallas_call(
        body, name=name, grid=(bsz, nb),
        in_specs=[pl.BlockSpec((1, tb, Q_RANK), lambda b, i: (b, i, 4)),
                  pl.BlockSpec((1, tb, KV_RANK), lambda b, i: (b, i, 10)),
                  pl.BlockSpec((1, tb, hw), lambda b, i: (b, i, 0)),
                  pl.BlockSpec((1, tb, hw), lambda b, i: (b, i, 0)),
                  pl.BlockSpec((1, tb, hw), lambda b, i: (b, i, 0)),
                  pl.BlockSpec((1, Q_RANK), lambda b, i: (0, 0)),
                  pl.BlockSpec((1, KV_RANK), lambda b, i: (0, 0)),
                  pl.BlockSpec((Q_RANK, hw), lambda b, i: (0, 0)),
                  pl.BlockSpec((KV_RANK, hw), lambda b, i: (0, 0)),
                  pl.BlockSpec((1, tb, HEAD_PAD), lambda b, i: (b, i, 0)),
                  pl.BlockSpec((1, tb, HEAD_PAD), lambda b, i: (b, i, 0)),
                  ANY_SPEC],
        out_specs=[pl.BlockSpec((1, tb, 512), lambda b, i: (b, i, 2)),
                   pl.BlockSpec((1, tb, Q_RANK), lambda b, i: (b, i, 0)),
                   pl.BlockSpec((1, tb, hw), lambda b, i: (b, i, 0)),
                   pl.BlockSpec((1, tb, KV_RANK), lambda b, i: (b, i, 0)),
                   pl.BlockSpec((1, tb, hw), lambda b, i: (b, i, 0)),
                   pl.BlockSpec((1, Q_RANK), lambda b, i: (0, 0)),
                   pl.BlockSpec((1, KV_RANK), lambda b, i: (0, 0))],
        out_shape=[jax.ShapeDtypeStruct((bsz, seq, D_IN_PAD), BF16),
                   jax.ShapeDtypeStruct((bsz, seq, Q_RANK), BF16),
                   jax.ShapeDtypeStruct((bsz, seq, hw), BF16),
                   jax.ShapeDtypeStruct((bsz, seq, KV_RANK), BF16),
                   jax.ShapeDtypeStruct((bsz, seq, hw), BF16),
                   jax.ShapeDtypeStruct((1, Q_RANK), F32),
                   jax.ShapeDtypeStruct((1, KV_RANK), F32)],
        input_output_aliases={11: 0},
        compiler_params=_cparams(),
    )(z3, z3, dq3, dk3, dv3, g_q, g_kv, w_uq, w_ukv, ctab, stab, dz3)


ATTN_HEADS_PER_STEP = 4


def _attn_specs(tq, seq, hp):
    blk = pl.BlockSpec((1, tq, hp * HEAD_PAD), lambda b, h, i: (b, i, h))
    full = pl.BlockSpec((1, seq, hp * HEAD_PAD), lambda b, h, i: (b, 0, h))
    return blk, full


def _head(h):
    return slice(HEAD_PAD * h, HEAD_PAD * (h + 1))


def _attn_fwd(q3, kv3, kp3, *, name, tq=512, hp=ATTN_HEADS_PER_STEP, side=None):
    bsz, seq, hw = q3.shape
    tq = min(tq, seq)
    blk, full = _attn_specs(tq, seq, hp)

    def body(q_ref, kv_ref, kp_ref, o_ref, lse_ref):
        i = pl.program_id(2)

        def update(state, q, kp, kv, mask=None):
            m, l, acc = state
            s = _dot(q, kp, NT)
            if mask is not None:
                s = jnp.where(mask, s, -1e30)
            m_new = jnp.maximum(m, jnp.max(s, axis=1, keepdims=True))
            alpha = jnp.exp2(m - m_new)
            p = jnp.exp2(s - m_new)
            return m_new, alpha * l + jnp.sum(p, axis=1, keepdims=True), alpha * acc + _dot(p.astype(BF16), kv)

        def step(j, carry):
            st = pl.multiple_of(j * tq, tq)
            return tuple(update(carry[h], q_ref[0, :, _head(h)], kp_ref[0, pl.ds(st, tq), _head(h)],
                                kv_ref[0, pl.ds(st, tq), _head(h)]) for h in range(hp))

        init = tuple((jnp.full((tq, 1), -1e30, F32), jnp.zeros((tq, 1), F32), jnp.zeros((tq, HEAD_PAD), F32))
                     for _ in range(hp))
        carry = lax.fori_loop(0, i, step, init)

        st = pl.multiple_of(i * tq, tq)
        is_nope = _iota((tq, HEAD_PAD), 1) < NOPE
        causal = _iota((tq, tq), 1) <= _iota((tq, tq), 0)
        for h in range(hp):
            m, l, acc = update(carry[h], q_ref[0, :, _head(h)], kp_ref[0, pl.ds(st, tq), _head(h)],
                               kv_ref[0, pl.ds(st, tq), _head(h)], causal)
            o_ref[0, :, _head(h)] = jnp.where(is_nope, 0.0, acc / l).astype(BF16)
            lse_ref[0, :, _head(h)] = jnp.broadcast_to(m + jnp.log(l) * LOG2E, (tq, HEAD_PAD))

    outs, side_outs = _hosted_call(
        body, name=name, grid=(bsz, HEADS // hp, seq // tq),
        in_specs=[blk, full, full],
        out_specs=[blk, blk],
        out_shape=[jax.ShapeDtypeStruct((bsz, seq, hw), BF16), jax.ShapeDtypeStruct((bsz, seq, hw), F32)],
        args=(q3, kv3, kp3), side=side)
    return outs if side is None else (outs, side_outs)


def _attn_bwd(q3, kv3, kp3, do3, lse3, dl3, *, name, tq=512, hp=ATTN_HEADS_PER_STEP, side=None):
    bsz, seq, hw = q3.shape
    tq = min(tq, seq)
    nq = seq // tq
    blk, full = _attn_specs(tq, seq, hp)

    def body(kv_ref, kp_ref, q_ref, do_ref, lse_ref, dl_ref, dq_ref, dk_ref, dv_ref):
        j = pl.program_id(2)

        @pl.when(j == 0)
        def _():
            dq_ref[...] = jnp.zeros_like(dq_ref)

        def pair(h, row0, nrows, nkeys, mask=None):
            row0 = pl.multiple_of(row0, nrows)
            qi = q_ref[0, pl.ds(row0, nrows), _head(h)]
            do = do_ref[0, pl.ds(row0, nrows), _head(h)]
            kp = kp_ref[0, :nkeys, _head(h)]
            s = _dot(qi, kp, NT)
            if mask is not None:
                s = jnp.where(mask, s, -1e30)
            wide = nkeys // HEAD_PAD
            p = jnp.exp2(s - jnp.tile(lse_ref[0, pl.ds(row0, nrows), _head(h)], (1, wide)))
            dv = _dot(p.astype(BF16), do, TN)
            dp = _dot(do, kv_ref[0, :nkeys, _head(h)], NT)
            ds = (p * (dp - jnp.tile(dl_ref[0, pl.ds(row0, nrows), _head(h)], (1, wide)))).astype(BF16)
            dq_ref[0, pl.ds(row0, nrows), _head(h)] += _dot(ds, kp)
            return _dot(ds, qi, TN), dv

        def step(i, carry):
            st = pl.multiple_of(i * tq, tq)
            out = []
            for h in range(hp):
                dk, dv = pair(h, st, tq, tq)
                out.append((carry[h][0] + dk, carry[h][1] + dv))
            return tuple(out)

        causal = _iota((tq, tq), 1) <= _iota((tq, tq), 0)
        carry = tuple(pair(h, pl.multiple_of(j * tq, tq), tq, tq, causal) for h in range(hp))
        carry = lax.fori_loop(j + 1, nq, step, carry)
        for h in range(hp):
            dk_ref[0, :, _head(h)] = (carry[h][0] * (1.0 / LOG2E)).astype(BF16)
            dv_ref[0, :, _head(h)] = carry[h][1].astype(BF16)

        @pl.when(j == nq - 1)
        def _():
            dq_ref[...] = dq_ref[...] * ATTN_SCALE

    outs, side_outs = _hosted_call(
        body, name=name, grid=(bsz, HEADS // hp, nq),
        in_specs=[blk, blk, full, full, full, full],
        out_specs=[full, blk, blk],
        out_shape=[jax.ShapeDtypeStruct((bsz, seq, hw), F32)] + [jax.ShapeDtypeStruct((bsz, seq, hw), BF16)] * 2,
        args=(kv3, kp3, q3, do3, lse3, dl3), side=side)
    return outs if side is None else (outs, side_outs)


def _onorm_fwd(o3, yg3, g_pad, *, name, tb=512):
    bsz, seq, hw = o3.shape
    wg = yg3.shape[-1]
    tb = min(tb, seq)

    def body(o_ref, yg_ref, g_ref, y_ref):
        ya = _rms_fwd(o_ref[0].astype(F32), g_ref[...], HEADS * 64).astype(BF16)
        y_ref[0] = jnp.concatenate([ya, yg_ref[0]], axis=1)

    return pl.pallas_call(
        body, name=name, grid=(bsz, seq // tb),
        in_specs=[pl.BlockSpec((1, tb, hw), lambda b, i: (b, i, 0)),
                  pl.BlockSpec((1, tb, wg), lambda b, i: (b, i, 0)),
                  pl.BlockSpec((1, hw), lambda b, i: (0, 0))],
        out_specs=pl.BlockSpec((1, tb, hw + wg), lambda b, i: (b, i, 0)),
        out_shape=jax.ShapeDtypeStruct((bsz, seq, hw + wg), BF16),
        compiler_params=_cparams(),
    )(o3, yg3, g_pad)


def _onorm_bwd(o3, dy3, g_pad, *, name, tb=512):
    bsz, seq, hw = o3.shape
    tb = min(tb, seq)

    def body(o_ref, dy_ref, g_ref, do_ref, dl_ref, dg_ref):
        @pl.when(jnp.logical_and(pl.program_id(0) == 0, pl.program_id(1) == 0))
        def _():
            dg_ref[...] = jnp.zeros_like(dg_ref)

        o = o_ref[0].astype(F32)
        do, dg = _rms_bwd(o, g_ref[...], dy_ref[0].astype(F32), HEADS * 64)
        dg_ref[...] += dg
        do_ref[0] = do.astype(BF16)
        prod = do * o
        parts = []
        for h in range(HEADS):
            sh = jnp.sum(prod[:, HEAD_PAD * h:HEAD_PAD * (h + 1)], axis=1, keepdims=True)
            parts.append(jnp.broadcast_to(sh, (tb, HEAD_PAD)))
        dl_ref[0] = jnp.concatenate(parts, axis=1)

    return pl.pallas_call(
        body, name=name, grid=(bsz, seq // tb),
        in_specs=[pl.BlockSpec((1, tb, hw), lambda b, i: (b, i, 0)),
                  pl.BlockSpec((1, tb, hw), lambda b, i: (b, i, 0)),
                  pl.BlockSpec((1, hw), lambda b, i: (0, 0))],
        out_specs=[pl.BlockSpec((1, tb, hw), lambda b, i: (b, i, 0)),
                   pl.BlockSpec((1, tb, hw), lambda b, i: (b, i, 0)),
                   pl.BlockSpec((1, hw), lambda b, i: (0, 0))],
        out_shape=[jax.ShapeDtypeStruct((bsz, seq, hw), BF16),
                   jax.ShapeDtypeStruct((bsz, seq, hw), F32),
                   jax.ShapeDtypeStruct((1, hw), F32)],
        compiler_params=_cparams(),
    )(o3, dy3, g_pad)


def _resnode_bwd(x3, g, *, name, target3=None, dh3=None, dres3=None, mod_nm=None, rows=None,
                 branch3=None, mod_gate=None, gate_row=None, tb=512, side=None):
    bsz, seq, d = x3.shape
    tb = min(tb, seq)
    final = target3 is not None
    has_branch = branch3 is not None
    row_spec = pl.BlockSpec((1, tb, d), lambda b, i: (b, i, 0))
    vec_spec = pl.BlockSpec((1, d), lambda b, i: (0, 0))
    mod_spec = pl.BlockSpec((1, MOD_ROWS, d), lambda b, i: (b, 0, 0))

    ins, in_specs = [x3, g], [row_spec, vec_spec]
    if final:
        ins += [target3]
        in_specs += [row_spec]
    else:
        ins += [dh3, dres3, mod_nm]
        in_specs += [row_spec, row_spec, mod_spec]
    if has_branch:
        ins += [branch3, mod_gate]
        in_specs += [row_spec, mod_spec]

    out_names = ["dx", "dg"]
    out_specs = [row_spec, vec_spec]
    out_shape = [jax.ShapeDtypeStruct((bsz, seq, d), F32), jax.ShapeDtypeStruct((1, d), F32)]
    if final:
        out_names += ["loss"]
        out_specs += [pl.BlockSpec((1, 128), lambda b, i: (0, 0))]
        out_shape += [jax.ShapeDtypeStruct((1, 128), F32)]
    else:
        out_names += ["dnm"]
        out_specs += [mod_spec]
        out_shape += [jax.ShapeDtypeStruct((bsz, MOD_ROWS, d), F32)]
    if has_branch:
        out_names += ["dbr", "dgate"]
        out_specs += [row_spec, mod_spec]
        out_shape += [jax.ShapeDtypeStruct((bsz, seq, d), BF16), jax.ShapeDtypeStruct((bsz, MOD_ROWS, d), F32)]
    n_in = len(ins)

    def body(*refs):
        r = dict(zip(["x", "g"] + (["t"] if final else ["dh", "dres", "nm"]) + (["br", "gm"] if has_branch else []),
                     refs[:n_in]))
        o = dict(zip(out_names, refs[n_in:]))
        b_first = pl.program_id(1) == 0
        first = jnp.logical_and(pl.program_id(0) == 0, b_first)
        rowid = _iota((MOD_ROWS, d), 0)

        @pl.when(first)
        def _():
            o["dg"][...] = jnp.zeros_like(o["dg"])
            if final:
                o["loss"][...] = jnp.zeros_like(o["loss"])

        @pl.when(b_first)
        def _():
            if not final:
                o["dnm"][...] = jnp.zeros_like(o["dnm"])
            if has_branch:
                o["dgate"][...] = jnp.zeros_like(o["dgate"])

        x = r["x"][0]
        gv = r["g"][...]
        if final:
            e = _rms_fwd(x, gv, d) - r["t"][0]
            sq = jnp.sum(jnp.sum(e * e, axis=1, keepdims=True), axis=0, keepdims=True)
            o["loss"][...] += jnp.broadcast_to(sq * (0.5 / d), (1, 128))
            dx, dg = _rms_bwd(x, gv, e * (1.0 / d), d)
        else:
            m = r["nm"][0]
            dh = r["dh"][0].astype(F32)
            scale = m[rows[1]:rows[1] + 1, :]
            rstd = lax.rsqrt(jnp.sum(x * x, axis=-1, keepdims=True) * (1.0 / d) + EPS)
            xh = x * rstd
            nrm = xh * gv
            dshift = jnp.sum(dh, axis=0, keepdims=True)
            dscale = jnp.sum(dh * nrm, axis=0, keepdims=True)
            o["dnm"][0] += jnp.where(rowid == 0, dshift, jnp.where(rowid == 1, dscale, 0.0))
            dn = dh * (1.0 + scale)
            dg = jnp.sum(dn * xh, axis=0, keepdims=True)
            dxh = dn * gv
            dx = rstd * (dxh - xh * (jnp.sum(dxh * xh, axis=-1, keepdims=True) * (1.0 / d))) + r["dres"][0]
        o["dg"][...] += dg
        o["dx"][0] = dx
        if has_branch:
            gate = r["gm"][0][gate_row:gate_row + 1, :]
            o["dbr"][0] = (gate * dx).astype(BF16)
            dgate = jnp.sum(dx * r["br"][0], axis=0, keepdims=True)
            o["dgate"][0] += jnp.where(rowid == 0, dgate, 0.0)

    outs, side_outs = _hosted_call(
        body, name=name, grid=(bsz, seq // tb),
        in_specs=in_specs, out_specs=out_specs, out_shape=out_shape, args=tuple(ins), side=side)
    res = dict(zip(out_names, outs))
    return res if side is None else (res, side_outs)


def _adamw(w, g, m, v, *, name):
    shape = w.shape
    cols = shape[-1]
    rows = w.size // cols
    tr = _pick_rows(rows, max(8, (256 * 1024) // cols // 8 * 8))

    def body(w_ref, g_ref, m_ref, v_ref, d_ref, nm_ref, nv_ref):
        d_ref[...], nm_ref[...], nv_ref[...] = _adamw_math(w_ref[...], g_ref[...], m_ref[...], v_ref[...])

    if w.ndim == 3 and shape[1] % 8 == 0:
        tr3 = _pick_rows(shape[1], max(8, (256 * 1024) // cols // 8 * 8))
        spec3 = pl.BlockSpec((None, tr3, cols), lambda l, i: (l, i, 0))
        return tuple(pl.pallas_call(
            body, name=name, grid=(shape[0], shape[1] // tr3),
            in_specs=[spec3] * 4, out_specs=[spec3] * 3,
            out_shape=[jax.ShapeDtypeStruct(shape, F32)] * 3,
            compiler_params=_cparams(),
        )(w, g, m, v))
    spec = pl.BlockSpec((tr, cols), lambda i: (i, 0))
    outs = pl.pallas_call(
        body, name=name, grid=(rows // tr,),
        in_specs=[spec] * 4, out_specs=[spec] * 3,
        out_shape=[jax.ShapeDtypeStruct((rows, cols), F32)] * 3,
        compiler_params=_cparams(),
    )(*[t.reshape(rows, cols) for t in (w, g, m, v)])
    return tuple(o.reshape(shape) for o in outs)


def _adamw_math(w, g, m, v):
    c1 = 1.0 - ADAM_B1 ** ADAM_STEP
    c2 = 1.0 - ADAM_B2 ** ADAM_STEP
    nm = ADAM_B1 * m + (1.0 - ADAM_B1) * g
    nv = ADAM_B2 * v + (1.0 - ADAM_B2) * (g * g)
    delta = -ADAM_LR * ((nm / c1) / (jnp.sqrt(nv / c2) + ADAM_EPS) + ADAM_WD * w)
    return delta, nm, nv


def _adamw_layers(w, m, v, bufs, row_off, *, name, tr=256):
    depth, rows, cols = w.shape
    tr = min(tr, rows)
    assert rows % tr == 0 and row_off % tr == 0

    outs = None
    for l in range(depth):
        def body(w_ref, g_ref, m_ref, v_ref, *rest):
            go_ref, d_ref, nm_ref, nv_ref = rest[-4:]
            g = g_ref[...]
            go_ref[...] = g
            d_ref[...], nm_ref[...], nv_ref[...] = _adamw_math(w_ref[...], g, m_ref[...], v_ref[...])

        layer = pl.BlockSpec((None, tr, cols), lambda i, l=l: (l, i, 0))
        prev = () if outs is None else tuple(outs)
        outs = pl.pallas_call(
            body, name=f"{name}_l{l}", grid=(rows // tr,),
            in_specs=[layer, pl.BlockSpec((tr, cols), lambda i: (row_off // tr + i, 0)), layer, layer]
            + [ANY_SPEC] * len(prev),
            out_specs=[layer] * 4,
            out_shape=[jax.ShapeDtypeStruct(w.shape, F32)] * 4,
            input_output_aliases={4 + k: k for k in range(len(prev))},
            compiler_params=_cparams(),
        )(w, bufs[l], m, v, *prev)
    return tuple(outs)


def _sum_leading(x, *, name, tr=256):
    n, rows, cols = x.shape
    tr = _pick_rows(rows, tr)

    def body(x_ref, o_ref):
        acc = x_ref[0]
        for k in range(1, n):
            acc = acc + x_ref[k]
        o_ref[...] = acc

    return pl.pallas_call(
        body, name=name, grid=(rows // tr,),
        in_specs=[pl.BlockSpec((n, tr, cols), lambda i: (0, i, 0))],
        out_specs=pl.BlockSpec((tr, cols), lambda i: (i, 0)),
        out_shape=jax.ShapeDtypeStruct((rows, cols), F32),
        compiler_params=_cparams(),
    )(x)


def _position():
    return lax.axis_index("x"), lax.axis_index("y"), lax.axis_index("c")


def _allgather8(x, *, name):
    shape = x.shape

    def body(x_ref, out_ref, send_sems, recv_sems, local_sem):
        px, py, pc = _position()
        me, sibling = (px, py, pc), (px, py, 1 - pc)
        chips = [(1 - px, py), (px, 1 - py), (1 - px, 1 - py)]
        src_own = x_ref

        def slot(qx, qy, qc):
            return out_ref.at[4 * qx + 2 * qy + qc]

        def copy(k, block, to, src=None):
            return pltpu.make_async_remote_copy(
                src_ref=slot(*block) if src is None else src, dst_ref=slot(*block),
                send_sem=send_sems.at[k], recv_sem=recv_sems.at[k], device_id=to, device_id_type=MESH)

        mine = pltpu.make_async_copy(src_own, slot(*me), local_sem)
        mine.start()
        first = [copy(0, me, sibling, src=src_own)]
        first += [copy(1 + j, me, (*chip, pc), src=src_own) for j, chip in enumerate(chips)]
        for cp in first:
            cp.start()
        passed = [copy(4 + j, (*chip, pc), sibling) for j, chip in enumerate(chips)]
        for j, chip in enumerate(chips):
            copy(1 + j, (*chip, pc), me).wait_recv()
            passed[j].start()
        copy(0, sibling, me).wait_recv()
        for j, chip in enumerate(chips):
            copy(4 + j, (*chip, 1 - pc), me).wait_recv()
        for cp in first + passed:
            cp.wait_send()
        mine.wait()

    return pl.pallas_call(
        body, name=name,
        out_shape=jax.ShapeDtypeStruct((N_DEV,) + shape, x.dtype),
        in_specs=[pl.BlockSpec(memory_space=pl.ANY)],
        out_specs=pl.BlockSpec(memory_space=pl.ANY),
        scratch_shapes=[pltpu.SemaphoreType.DMA((7,)), pltpu.SemaphoreType.DMA((7,)), pltpu.SemaphoreType.DMA],
    )(x)


class _Exchange:
    def __init__(self, ins, out_shapes, n, build, aliases=None):
        self.ins, self.out_shapes, self.n, self.build = tuple(ins), tuple(out_shapes), n, build
        self.aliases = dict(aliases or {})

    def _descriptors(self, in_refs, out_refs, send_sems, recv_sems):
        sends, recvs = [], []
        for k, (src, dst, peer, landing) in enumerate(self.build(in_refs, out_refs)):
            sends.append(pltpu.make_async_remote_copy(
                src_ref=src, dst_ref=dst, send_sem=send_sems.at[k], recv_sem=recv_sems.at[k],
                device_id=peer, device_id_type=MESH))
            recvs.append(pltpu.make_async_remote_copy(
                src_ref=src, dst_ref=landing, send_sem=send_sems.at[k], recv_sem=recv_sems.at[k],
                device_id=peer, device_id_type=MESH))
        return sends, recvs

    def start(self, *refs):
        for cp in self._descriptors(*refs)[0]:
            cp.start()

    def finish(self, *refs):
        sends, recvs = self._descriptors(*refs)
        for cp in recvs:
            cp.wait_recv()
        for cp in sends:
            cp.wait_send()


ANY_SPEC = pl.BlockSpec(memory_space=pl.ANY)


def _hosted_call(body, *, name, grid, in_specs, out_specs, out_shape, args, scratch_shapes=(), side=None,
                 num_scalar_prefetch=0, io_aliases=None):
    in_specs, out_specs, out_shape = list(in_specs), list(out_specs), list(out_shape)
    n_in, n_out = len(in_specs) + num_scalar_prefetch, len(out_specs)
    kernel_body = body
    aliases = dict(io_aliases or {})
    if side is not None:
        s_in, s_out = len(side.ins), len(side.out_shapes)
        aliases.update({n_in + i: n_out + o for i, o in side.aliases.items()})

        def kernel_body(*refs):
            ins, s_ins = refs[:n_in], refs[n_in:n_in + s_in]
            outs = refs[n_in + s_in:n_in + s_in + n_out]
            s_outs = refs[n_in + s_in + n_out:n_in + s_in + n_out + s_out]
            scratch, sems = refs[n_in + s_in + n_out + s_out:-2], refs[-2:]
            first = functools.reduce(jnp.logical_and, [pl.program_id(a) == 0 for a in range(len(grid))])
            last = functools.reduce(jnp.logical_and, [pl.program_id(a) == g - 1 for a, g in enumerate(grid)])

            @pl.when(first)
            def _():
                side.start(s_ins, s_outs, *sems)

            body(*ins, *outs, *scratch)

            @pl.when(last)
            def _():
                side.finish(s_ins, s_outs, *sems)

        in_specs += [ANY_SPEC] * s_in
        out_specs += [ANY_SPEC] * s_out
        out_shape += list(side.out_shapes)
        scratch_shapes = list(scratch_shapes) + [pltpu.SemaphoreType.DMA((side.n,)),
                                                 pltpu.SemaphoreType.DMA((side.n,))]
        args = tuple(args) + side.ins
    if num_scalar_prefetch:
        grid_spec = pltpu.PrefetchScalarGridSpec(num_scalar_prefetch=num_scalar_prefetch, grid=grid,
                                                 in_specs=in_specs, out_specs=out_specs,
                                                 scratch_shapes=list(scratch_shapes))
        outs = pl.pallas_call(kernel_body, name=name, grid_spec=grid_spec, out_shape=out_shape,
                              input_output_aliases=aliases, compiler_params=_cparams())(*args)
    else:
        outs = pl.pallas_call(kernel_body, name=name, grid=grid, in_specs=in_specs, out_specs=out_specs,
                              out_shape=out_shape, scratch_shapes=list(scratch_shapes),
                              input_output_aliases=aliases, compiler_params=_cparams())(*args)
    return tuple(outs[:n_out]), tuple(outs[n_out:])


def _run_exchange(ex, *, name):
    s_in = len(ex.ins)

    def body(*refs):
        ins, outs, sems = refs[:s_in], refs[s_in:-2], refs[-2:]
        ex.start(ins, outs, *sems)
        ex.finish(ins, outs, *sems)

    outs = pl.pallas_call(
        body, name=name, out_shape=list(ex.out_shapes),
        in_specs=[ANY_SPEC] * s_in, out_specs=[ANY_SPEC] * len(ex.out_shapes),
        scratch_shapes=[pltpu.SemaphoreType.DMA((ex.n,)), pltpu.SemaphoreType.DMA((ex.n,))],
        input_output_aliases=ex.aliases,
    )(*ex.ins)
    return tuple(outs)


def _other_chips(px, py):
    return [(px, 1 - py), (1 - px, py), (1 - px, 1 - py)]


def _gather_spread(w_flat, halves=True):
    rows, w = w_flat.shape
    hr = rows // 2 if halves else rows

    def build(ins, outs):
        px, py, pc = _position()
        mine = ins[0].at[pl.ds(pc * hr, hr)] if halves else ins[0]
        me = 4 * px + 2 * py + pc
        plan = [((px, py, 1 - pc), me ^ 1)]
        plan += [((qx, qy, pc), 4 * qx + 2 * qy + pc) for qx, qy in _other_chips(px, py)]
        return [(mine, outs[0].at[me], peer, outs[0].at[their]) for peer, their in plan]

    return _Exchange([w_flat], [jax.ShapeDtypeStruct((N_DEV, hr, w), w_flat.dtype)], 4, build)


def _gather_pass_on(gath):
    def build(ins, outs):
        px, py, pc = _position()
        out = []
        for qx, qy in _other_chips(px, py):
            blk = 4 * qx + 2 * qy + pc
            out.append((outs[0].at[blk], outs[0].at[blk], (px, py, 1 - pc), outs[0].at[blk ^ 1]))
        return out

    return _Exchange([gath], [jax.ShapeDtypeStruct(gath.shape, gath.dtype)], 3, build, aliases={0: 0})


def _rs_halves(g):
    n, rows, w = g.shape
    hr = rows // 2

    def build(ins, outs):
        px, py, pc = _position()
        return [(ins[0].at[:, pl.ds((1 - pc) * hr, hr), :], outs[0], (px, py, 1 - pc), outs[0])]

    return _Exchange([g], [jax.ShapeDtypeStruct((n, hr, w), g.dtype)], 1, build)


def _rs_chips(sb):
    def build(ins, outs):
        px, py, pc = _position()
        return [(ins[0].at[j], outs[0].at[j], (qx, qy, pc), outs[0].at[j])
                for j, (qx, qy) in enumerate(_other_chips(px, py))]

    return _Exchange([sb], [jax.ShapeDtypeStruct(sb.shape, sb.dtype)], 3, build)


def _rs_complete(buf):
    def build(ins, outs):
        px, py, pc = _position()
        return [(outs[0].at[pc], outs[0].at[pc], (px, py, 1 - pc), outs[0].at[1 - pc])]

    return _Exchange([buf], [jax.ShapeDtypeStruct(buf.shape, buf.dtype)], 1, build, aliases={0: 0})


def _rs_partial(g, recv, ids, *, name, tr=128):
    _, rows, w = g.shape
    hr = rows // 2
    nb = hr // tr

    def body(ids_ref, g_ref, r_ref, o_ref):
        o_ref[0] = (g_ref[0] + r_ref[0]).astype(BF16)

    grid_spec = pltpu.PrefetchScalarGridSpec(
        num_scalar_prefetch=1, grid=(3, nb),
        in_specs=[pl.BlockSpec((1, tr, w), lambda j, i, ids: (ids[1] ^ (j + 1), ids[0] * nb + i, 0)),
                  pl.BlockSpec((1, tr, w), lambda j, i, ids: (ids[1] ^ (j + 1), i, 0))],
        out_specs=pl.BlockSpec((1, tr, w), lambda j, i, ids: (j, i, 0)))
    return pl.pallas_call(
        body, name=name, grid_spec=grid_spec,
        out_shape=jax.ShapeDtypeStruct((3, hr, w), BF16),
        compiler_params=_cparams(),
    )(ids, g, recv)


def _rs_total(g, recv, got, ids, *, name, tr=128):
    _, rows, w = g.shape
    hr = rows // 2
    nb = hr // tr

    def body(ids_ref, g_ref, r_ref, got_ref, o_ref):
        acc = g_ref[0] + r_ref[0]
        for j in range(3):
            acc = acc + got_ref[j].astype(F32)
        o_ref[0] = acc

    grid_spec = pltpu.PrefetchScalarGridSpec(
        num_scalar_prefetch=1, grid=(nb,),
        in_specs=[pl.BlockSpec((1, tr, w), lambda i, ids: (ids[1], ids[0] * nb + i, 0)),
                  pl.BlockSpec((1, tr, w), lambda i, ids: (ids[1], i, 0)),
                  pl.BlockSpec((3, tr, w), lambda i, ids: (0, i, 0))],
        out_specs=pl.BlockSpec((1, tr, w), lambda i, ids: (ids[0], i, 0)))
    return pl.pallas_call(
        body, name=name, grid_spec=grid_spec,
        out_shape=jax.ShapeDtypeStruct((2, hr, w), F32),
        compiler_params=_cparams(),
    )(ids, g, recv, got)


class _ReduceScatter:
    def __init__(self, g, ids, tag):
        self.g, self.ids, self.tag, self.stage, self.result = g, ids, tag, 0, None

    def next_exchange(self):
        if self.stage == 0:
            return _rs_halves(self.g)
        if self.stage == 1:
            return _rs_chips(self.sb)
        return _rs_complete(self.buf)

    def done(self, outs):
        if self.stage == 0:
            self.recv = outs[0]
            hr = self.recv.shape[1]
            self.tr = max(t for t in range(16, 513, 16) if hr % t == 0)
            self.sb = _rs_partial(self.g, self.recv, self.ids, name=f"{self.tag}_partial", tr=self.tr)
        elif self.stage == 1:
            self.buf = _rs_total(self.g, self.recv, outs[0], self.ids, name=f"{self.tag}_total", tr=self.tr)
        else:
            _, hr, w = outs[0].shape
            self.result = outs[0].reshape(2 * hr, w)
        self.stage += 1

    def finish_alone(self):
        names = ("halves", "chips", "complete")
        while self.stage < 3:
            self.done(_run_exchange(self.next_exchange(), name=f"{self.tag}_{names[self.stage]}"))
        return self.result


def _flat_rows():
    used = sum(r for _, r in FSDP_SECTIONS)
    return used, -(-used // ROW_ALIGN) * ROW_ALIGN


def _cols_to_chunks(full):
    rows, cols = full.shape
    t = full.reshape(rows, N_CHIPS, cols // N_CHIPS).transpose(1, 0, 2)
    return t.reshape(N_CHIPS, -1, FLAT_W)


def _chunks_to_cols(chunks, rows, cols):
    return chunks.reshape(N_CHIPS, rows, cols // N_CHIPS).transpose(1, 0, 2).reshape(rows, cols)


def _pad_heads(w, real):
    lead = w.shape[:-1]
    t = w.reshape(lead + (HEADS, real))
    t = jnp.pad(t, [(0, 0)] * len(lead) + [(0, 0), (0, HEAD_PAD - real)])
    return t.reshape(lead + (HEADS * HEAD_PAD,))


def _unpad_heads(w, real):
    lead = w.shape[:-1]
    return w.reshape(lead + (HEADS, HEAD_PAD))[..., :real].reshape(lead + (HEADS * real,))


def _pad_value_lanes(w, axis):
    w = jnp.moveaxis(w, axis, -1)
    lead = w.shape[:-1]
    t = w.reshape(lead + (HEADS, 64))
    t = jnp.pad(t, [(0, 0)] * len(lead) + [(0, 0), (HEAD_PAD - 64, 0)])
    return jnp.moveaxis(t.reshape(lead + (HEADS * HEAD_PAD,)), -1, axis)


def _unpad_value_lanes(w, axis):
    w = jnp.moveaxis(w, axis, -1)
    lead = w.shape[:-1]
    t = w.reshape(lead + (HEADS, HEAD_PAD))[..., HEAD_PAD - 64:]
    return jnp.moveaxis(t.reshape(lead + (HEADS * 64,)), -1, axis)


def _pad_w_in_t(wt):
    z = jnp.zeros((NOPE, wt.shape[1]), wt.dtype)
    z2 = jnp.zeros((HEAD_PAD - NOPE - ROPE, wt.shape[1]), wt.dtype)
    return jnp.concatenate([wt[:1408], z, wt[1408:], z2], axis=0)


def _unpad_w_in_t(wt):
    return jnp.concatenate([wt[:1408], wt[1408 + NOPE:1408 + NOPE + ROPE]], axis=0)


def _rope_tables(positions):
    freqs = ROPE_THETA ** (-jnp.arange(0, ROPE, 2, dtype=F32) / ROPE)
    ang = positions.astype(F32)[..., None] * freqs
    cos, sin = jnp.cos(ang), jnp.sin(ang)
    lead = cos.shape[:-1]
    ones = jnp.ones(lead + (NOPE,), F32)
    zeros_n = jnp.zeros(lead + (NOPE,), F32)
    zeros_p = jnp.zeros(lead + (HEAD_PAD - NOPE - ROPE,), F32)
    ctab = jnp.concatenate([ones, cos, cos, zeros_p], axis=-1)
    stab = jnp.concatenate([zeros_n, -sin, sin, zeros_p], axis=-1)
    return ctab, stab


def _mix_weights(full):
    return dict(
        w_in_t=_pad_w_in_t(full["w_in_t"]),
        w_uq=_pad_heads(full["mla_w_uq"], NOPE + ROPE),
        w_ukv=full["mla_w_ukv"],
        w_out=jnp.concatenate([_pad_value_lanes(full["w_out"][D_GMLP:], 0), full["w_out"][:D_GMLP]], axis=0),
    )


def _small_weights(p, l):
    ws = p["gmlp_ws"][l]
    tril = jnp.tril(jnp.ones((CHUNK, CHUNK), bool))
    bs = p["gmlp_bs"][l]
    bexp = jnp.repeat(bs.reshape(GROUPS // 2, 2, CHUNK).transpose(0, 2, 1), GROUP_DIM, axis=2)
    return dict(
        ws=ws,
        wst=jnp.where(tril[None], ws, 0.0).transpose(0, 2, 1).astype(BF16),
        bexp=bexp,
        g_mix=p["norm_mix_g"][l][None],
        g_ffn=p["norm_ffn_g"][l][None],
        g_q=p["mla_q_norm_g"][l][None],
        g_kv=p["mla_kv_norm_g"][l][None],
        g_og=p["out_norm_gmlp_g"][l][None],
        g_oa=_pad_value_lanes(p["out_norm_mla_g"][l], 0)[None],
    )


def _local_step(x3, target3, positions, mods, final_g, plan):
    bsz, seq, d = x3.shape
    tok = bsz * seq
    tmt = min(512, seq)
    tmk = min(1024, seq)
    tmw = min(2048, tok)
    chunk = (None, None, FLAT_W, FLAT_W)
    chunk2 = (2, None, FLAT_W, FLAT_W)
    ff_grad_shape = (N_CHIPS, 2 * FLAT_W, FLAT_W)
    ctab, stab = _rope_tables(positions)
    lw = [None] * DEPTH

    def flat(t):
        return t.reshape(tok, t.shape[-1])

    def cube(t):
        return t.reshape(bsz, seq, t.shape[-1])

    def carrying(l, tag, fn, *args, **kw):
        side = plan.host(l, tag)
        if side is None:
            return fn(*args, **kw)
        res, side_outs = fn(*args, side=side, **kw)
        plan.hosted(l, tag, side_outs)
        return res

    saved = []
    x = x3
    for l in range(DEPTH):
        lw[l] = plan.layer(l)
        w, mod = lw[l], mods[l]
        if l == 0:
            h1 = carrying(l, "fwd_normmod1", _normmod_fwd, x, w["g_mix"], mod, SHIFT1, SCALE1,
                          name=f"l{l}_normmod1")
        else:
            h1 = h1_next
        z = cube(_mm(flat(h1), w["w_in_t"], dims="nt", name=f"l{l}_w_in", tm=tmt, tn=D_IN_PAD, tk=d,
                     out_dtypes=(BF16,)))
        yg = _gmlp_fwd(z, w["ws"], w["bexp"], w["g_og"], name=f"l{l}_gmlp_fwd")
        q, kv, kp = _mla_prep_fwd(z, w["g_q"], w["g_kv"], w["w_uq"], w["w_ukv"], ctab, stab, name=f"l{l}_mla_prep")
        o, lse = carrying(l, "fwd_attn", _attn_fwd, q, kv, kp, name=f"l{l}_attn_fwd")
        y = _onorm_fwd(o, yg, w["g_oa"], name=f"l{l}_onorm_fwd")

        def normmod(xv, gv, gm, shift_row, scale_row):
            m = gm[0]
            return _rms_fwd(xv, gv, d) * (1.0 + m[scale_row:scale_row + 1, :]) + m[shift_row:shift_row + 1, :]

        def out_epi(po, xv, gm, gf):
            x_new = xv + gm[0][GATE1:GATE1 + 1, :] * po
            return po, x_new, normmod(x_new, gf, gm, SHIFT2, SCALE2)

        vec_spec = pl.BlockSpec((1, d), lambda i, j, k: (0, j))
        po, x_mid, h2 = carrying(l, "fwd_out_a", _mm, flat(y), w["w_out"], dims="nn", name=f"l{l}_w_out",
                                 tm=tmt, tn=d, tk=y.shape[-1], out_dtypes=(BF16, F32, BF16), epilogue=out_epi,
                                 extras=(flat(x), mod, w["g_ffn"]),
                                 extra_specs=(None, _mod_spec(tmt, d, seq), vec_spec))
        x_mid, h2 = cube(x_mid), cube(h2)

        def act_epi(acc):
            r = jnp.maximum(acc, 0.0)
            return (r * r,)

        r = carrying(l, "fwd_ff1", _mm, flat(h2), w["ff"], dims="nn", name=f"l{l}_w_ff1", tm=tmw, tn=FLAT_W,
                     tk=d, out_dtypes=(BF16,), epilogue=act_epi, weights_outer=True, n=D_FF,
                     b_block=(chunk, lambda i, j, k: (j, 0, 0, 0)))

        more = l + 1 < DEPTH

        def ff2_epi(acc, xv, gm, *nxt):
            x_new = xv + gm[0][GATE2:GATE2 + 1, :] * acc
            return (acc, x_new) + ((normmod(x_new, nxt[1], nxt[0], SHIFT1, SCALE1),) if more else ())

        mod_spec = _mod_spec(tmt, d, seq)
        outs = carrying(l, "fwd_ff2", _mm, r, w["ff"], dims="nn", name=f"l{l}_w_ff2", tm=tmt, tn=d, tk=2 * FLAT_W,
                        out_dtypes=(BF16, F32) + ((BF16,) if more else ()), epilogue=ff2_epi,
                        extras=(flat(x_mid), mod) + ((mods[l + 1], plan.layer(l + 1)["g_mix"]) if more else ()),
                        extra_specs=(None, mod_spec) + ((mod_spec, vec_spec) if more else ()), n=d,
                        b_block=(chunk2, lambda i, j, k: (k, 1, 0, 0)))
        f, x_out = outs[0], outs[1]
        h1_next = cube(outs[2]) if more else None
        saved.append(dict(x_in=x, h1=h1, z=z, q=q, kv=kv, kp=kp, o=o, lse=lse, y=y, po=cube(po),
                          x_mid=x_mid, h2=h2, r=r, f=cube(f)))
        x = cube(x_out)

    grads = [dict() for _ in range(DEPTH)]
    dmods = [None] * DEPTH
    top = DEPTH - 1
    node = _resnode_bwd(x, final_g[None], name="final_loss_bwd", target3=target3,
                        branch3=saved[top]["f"], mod_gate=mods[top], gate_row=GATE2)
    loss_part = node["loss"][0, 0]
    d_final_g = node["dg"][0]
    plan.scalars(loss_part, d_final_g)
    for l in range(DEPTH - 1, -1, -1):
        w, mod, s = lw[l], mods[l], saved[l]
        dx_out, dfb, dgate2 = node["dx"], flat(node["dbr"]), node["dgate"][:, 0]

        def dact_epi(acc, rv):
            return (acc * (2.0 * jnp.sqrt(rv.astype(F32))),)

        da = carrying(l, "bwd_d_r", _mm, dfb, w["ff"], dims="nt", name=f"l{l}_d_r", tm=tmw, tn=FLAT_W, tk=d,
                      out_dtypes=(BF16,), epilogue=dact_epi, extras=(s["r"],), weights_outer=True, n=D_FF,
                      b_block=(chunk, lambda i, j, k: (j, 1, 0, 0)))
        g_ff = carrying(l, "bwd_dw_ff2", _mm, s["r"], dfb, dims="tn", name=f"l{l}_dw_ff2", tm=FLAT_W, tn=d,
                        tk=2048, out_into=(ff_grad_shape, (None, FLAT_W, FLAT_W), lambda i, j, k: (i, 1, 0), None))
        g_ff = carrying(l, "bwd_dw_ff1", _mm, flat(s["h2"]), da, dims="tn", name=f"l{l}_dw_ff1", tm=d, tn=FLAT_W,
                        tk=2048, out_into=(ff_grad_shape, (None, FLAT_W, FLAT_W), lambda i, j, k: (j, 0, 0), g_ff))
        plan.ff_grads(l, g_ff)
        dh2 = carrying(l, "bwd_d_h2", _mm, da, w["ff"], dims="nt", name=f"l{l}_d_h2", tm=tmk, tn=d, tk=2 * FLAT_W,
                       n=d, b_block=(chunk2, lambda i, j, k: (k, 0, 0, 0)), out_dtypes=(BF16,))
        node = _resnode_bwd(s["x_mid"], w["g_ffn"], name=f"l{l}_resnode_ffn", dh3=cube(dh2), dres3=dx_out,
                            mod_nm=mod, rows=(SHIFT2, SCALE2), branch3=s["po"], mod_gate=mod, gate_row=GATE1)
        grads[l]["norm_ffn_g"] = node["dg"][0]
        dshift2, dscale2 = node["dnm"][:, 0], node["dnm"][:, 1]
        dx_mid, dpo, dgate1 = node["dx"], flat(node["dbr"]), node["dgate"][:, 0]

        wy = s["y"].shape[-1]
        dy = cube(carrying(l, "bwd_d_y", _mm, dpo, w["w_out"], dims="nt", name=f"l{l}_d_y", tm=tmt, tn=wy, tk=d,
                           out_dtypes=(BF16,)))
        dw_out = carrying(l, "bwd_dw_out", _mm, flat(s["y"]), dpo, dims="tn", name=f"l{l}_dw_out", tm=wy // 3,
                          tn=d, tk=2048)
        hw = HEADS * HEAD_PAD
        grads[l]["w_out"] = jnp.concatenate([dw_out[hw:], _unpad_value_lanes(dw_out[:hw], 0)], axis=0)

        dz, dws, dbs, dg_og = _gmlp_bwd(s["z"], dy, w["ws"], w["wst"], w["bexp"], w["g_og"],
                                        name=f"l{l}_gmlp_bwd", dy_col=hw // D_GMLP)
        grads[l]["gmlp_ws"], grads[l]["gmlp_bs"], grads[l]["out_norm_gmlp_g"] = dws, dbs, dg_og[0]

        do, dl, dg_oa = _onorm_bwd(s["o"], dy, w["g_oa"], name=f"l{l}_onorm_bwd")
        grads[l]["out_norm_mla_g"] = _unpad_value_lanes(dg_oa[0], 0)
        dq, dk, dv = carrying(l, "bwd_attn_dkv", _attn_bwd, s["q"], s["kv"], s["kp"], do, s["lse"], dl,
                              name=f"l{l}_attn_bwd")
        dz, cq, dqb, ckv, dkvb, dg_q, dg_kv = _mla_prep_bwd(
            s["z"], dz, dq, dk, dv, w["g_q"], w["g_kv"], w["w_uq"], w["w_ukv"], ctab, stab,
            name=f"l{l}_mla_prep_bwd")
        grads[l]["mla_q_norm_g"], grads[l]["mla_kv_norm_g"] = dg_q[0], dg_kv[0]
        dw_uq = carrying(l, "bwd_dw_uq", _mm, flat(cq), flat(dqb), dims="tn", name=f"l{l}_dw_uq", tm=Q_RANK,
                         tn=1024, tk=4096)
        grads[l]["mla_w_uq"] = _unpad_heads(dw_uq, NOPE + ROPE)
        grads[l]["mla_w_ukv"] = _mm(flat(ckv), flat(dkvb), dims="tn", name=f"l{l}_dw_ukv", tm=KV_RANK, tn=1024, tk=4096)

        grads[l]["w_in_t"] = _unpad_w_in_t(_mm(flat(dz), flat(s["h1"]), dims="tn", name=f"l{l}_dw_in",
                                               tm=D_IN_PAD // 2, tn=d, tk=2048))
        plan.layer_grads(l, grads[l])
        dh1 = carrying(l, "bwd_d_h1", _mm, flat(dz), w["w_in_t"], dims="nn", name=f"l{l}_d_h1", tm=tmt, tn=d,
                       tk=D_IN_PAD, out_dtypes=(BF16,))
        below = dict(branch3=saved[l - 1]["f"], mod_gate=mods[l - 1], gate_row=GATE2) if l > 0 else {}
        node = carrying(l, "bwd_resnode_mix", _resnode_bwd, s["x_in"], w["g_mix"], name=f"l{l}_resnode_mix",
                        dh3=cube(dh1), dres3=dx_mid, mod_nm=mod, rows=(SHIFT1, SCALE1), **below)
        grads[l]["norm_mix_g"] = node["dg"][0]
        dshift1, dscale1 = node["dnm"][:, 0], node["dnm"][:, 1]
        dmods[l] = jnp.stack([dshift1, dscale1, dgate1, dshift2, dscale2, dgate2], axis=1)
        plan.layer_done(l)
    return node["dx"], dmods


W_NAMES = ("w_ada", "b_ada", "norm_mix_g", "w_in", "gmlp_ws", "gmlp_bs", "mla_q_norm_g", "mla_kv_norm_g",
           "mla_w_uq", "mla_w_ukv", "out_norm_gmlp_g", "out_norm_mla_g", "w_out", "norm_ffn_g", "w_ff1", "w_ff2",
           "final_norm_g")
FLAT_KEY = {"w_in": "w_in", "w_uq": "mla_w_uq", "w_ukv": "mla_w_ukv", "w_out": "w_out", "w_ff1": "w_ff1",
            "w_ff2": "w_ff2"}
COL_SHARDED = ("w_in", "w_uq", "w_ukv", "w_ff1")
FULL_SHAPE = {"w_in": (D_MODEL, D_IN), "w_uq": (Q_RANK, HEADS * (NOPE + ROPE)), "w_ukv": (KV_RANK, HEADS * 128),
              "w_out": (D_MODEL, D_MODEL), "w_ff1": (D_MODEL, D_FF), "w_ff2": (D_FF, D_MODEL)}
SMALL_LAYER_NAMES = ("norm_mix_g", "gmlp_ws", "gmlp_bs", "mla_q_norm_g", "mla_kv_norm_g", "out_norm_gmlp_g",
                     "out_norm_mla_g", "norm_ffn_g")


def _silu(v):
    return v * (1.0 / (1.0 + jnp.exp(-v)))


class _CommPlan:
    FWD = {"fwd_attn": ("ff", 0, "spread"), "fwd_out_a": ("ff", 0, "pass"),
           "fwd_ff1": ("mix", 1, "spread"), "fwd_ff2": ("mix", 1, "pass")}
    BWD = {"bwd_d_r": ("mix", 1), "bwd_dw_ff2": ("mix", 1), "bwd_dw_ff1": ("mix", 1),
           "bwd_d_h2": ("ff", 0), "bwd_attn_dkv": ("ff", 0), "bwd_dw_uq": ("ff", 0)}
    BWD_LAST = {"bwd_d_h1": ("mix", 0), "bwd_resnode_mix": ("mix", 0)}
    SMALL = {"bwd_d_y": "spread", "bwd_dw_out": "pass"}

    def __init__(self, weights, ids, dev, core):
        self.weights, self.ids, self.dev, self.core = weights, ids, dev, core
        self.used, self.rows = _flat_rows()
        self.flat = {("mix", l): self._flat_mix(l) for l in range(DEPTH)}
        self.flat.update({("ff", l): jnp.concatenate([weights["w_ff1"][l], weights["w_ff2"][l]], axis=0).astype(BF16)
                          for l in range(DEPTH)})
        self.lw, self.rs, self.grads, self.spread = {}, {}, {}, {}
        self.small_vec, self.small_sum, self.small_spread, self.extra = {}, {}, None, {}
        self.lw = {l: _small_weights(weights, l) for l in range(DEPTH)}

    def _flat_mix(self, l):
        pieces = []
        for nm, _ in FSDP_SECTIONS:
            shard = self.weights[FLAT_KEY[nm]][l]
            pieces.append(shard.T if nm == "w_in" else shard.reshape(-1, FLAT_W))
        pieces.append(jnp.zeros((self.rows - self.used, FLAT_W), F32))
        return jnp.concatenate(pieces, axis=0).astype(BF16)

    def _arrived(self, group, l, gath):
        flat = self.flat[group, l]
        hr = flat.shape[0] // 2
        mine = lax.dynamic_slice(flat, (self.core * hr, 0), (hr, FLAT_W))
        gath = lax.dynamic_update_slice(gath, mine[None], (self.dev, 0, 0))
        if group == "ff":
            self.lw[l]["ff"] = gath.reshape(N_CHIPS, 2, hr, FLAT_W)
            return
        w_gath = gath.reshape(N_CHIPS, self.rows, FLAT_W)
        full, off = {}, 0
        for nm, nrows in FSDP_SECTIONS:
            sec = w_gath[:, off:off + nrows]
            off += nrows
            rows, cols = FULL_SHAPE[nm]
            if nm == "w_in":
                full["w_in_t"] = sec.reshape(cols, rows)
            else:
                full[FLAT_KEY[nm]] = (_chunks_to_cols(sec, rows, cols) if nm in COL_SHARDED
                                      else sec.reshape(rows, cols))
        self.lw[l].update(_mix_weights(full))

    def layer(self, l):
        return self.lw[l]

    def host(self, l, tag):
        if tag == "fwd_normmod1":
            return _gather_spread(self.flat["mix", 0]) if l == 0 else None
        if tag in self.FWD:
            group, ahead, what = self.FWD[tag]
            if l + ahead >= DEPTH:
                return None
            return _gather_spread(self.flat[group, l + ahead]) if what == "spread" else _gather_pass_on(self.spread[group])
        if tag in self.SMALL:
            if l + 1 not in self.small_vec:
                return None
            if self.SMALL[tag] == "spread":
                return _gather_spread(self.small_vec[l + 1], halves=False)
            return _gather_pass_on(self.small_spread)
        rs = self._rs_for(l, tag)
        return None if rs is None or rs.stage > 2 else rs.next_exchange()

    def _rs_for(self, l, tag):
        if tag in self.BWD_LAST:
            return self.rs.get(self.BWD_LAST[tag]) if l == 0 else None
        group, ahead = self.BWD[tag]
        return self.rs.get((group, l + ahead))

    def hosted(self, l, tag, outs):
        if tag == "fwd_normmod1":
            self._arrived("mix", 0, _run_exchange(_gather_pass_on(outs[0]), name="l0_mix_gather_pass_on")[0])
        elif tag in self.FWD:
            group, ahead, what = self.FWD[tag]
            if what == "spread":
                self.spread[group] = outs[0]
            else:
                self._arrived(group, l + ahead, outs[0])
        elif tag in self.SMALL:
            if self.SMALL[tag] == "spread":
                self.small_spread = outs[0]
            else:
                self._small_arrived(l + 1, outs[0])
        else:
            self._rs_for(l, tag).done(outs)

    def ff_grads(self, l, g_ff):
        self.rs["ff", l] = _ReduceScatter(g_ff, self.ids, f"l{l}_ff_rs")

    def layer_grads(self, l, grads):
        self.grads[l] = grads
        pieces = []
        for nm, nrows in FSDP_SECTIONS:
            if nm == "w_in":
                pieces.append(grads["w_in_t"].reshape(N_CHIPS, nrows, FLAT_W))
                continue
            g = grads[FLAT_KEY[nm]]
            pieces.append(_cols_to_chunks(g) if nm in COL_SHARDED else g.reshape(N_CHIPS, nrows, FLAT_W))
        pieces.append(jnp.zeros((N_CHIPS, self.rows - self.used, FLAT_W), F32))
        self.rs["mix", l] = _ReduceScatter(jnp.concatenate(pieces, axis=1), self.ids, f"l{l}_mix_rs")

    def scalars(self, loss_part, d_final_g):
        self.extra = {0: [loss_part[None]]}
        self.extra.setdefault(DEPTH - 1, []).insert(0, d_final_g)

    def layer_done(self, l):
        if l == 0:
            self.rs["mix", 0].finish_alone()
        parts = [self.grads[l][nm].reshape(-1) for nm in SMALL_LAYER_NAMES] + self.extra.get(l, [])
        vec = jnp.concatenate(parts)
        rows = -(-vec.shape[0] // (8 * FLAT_W)) * 8
        self.small_vec[l] = jnp.pad(vec, (0, rows * FLAT_W - vec.shape[0])).reshape(rows, FLAT_W)
        if l == 0:
            (gath,) = _run_exchange(_gather_spread(self.small_vec[0], halves=False), name="l0_small_spread")
            self._small_arrived(0, _run_exchange(_gather_pass_on(gath), name="l0_small_pass_on")[0])

    def _small_arrived(self, l, gath):
        gath = lax.dynamic_update_slice(gath, self.small_vec[l][None], (self.dev, 0, 0))
        self.small_sum[l] = _sum_leading(gath, name=f"l{l}_small_sum").reshape(-1)

    def small_grads(self):
        out = {nm: [] for nm in SMALL_LAYER_NAMES}
        for l in range(DEPTH):
            off = 0
            for nm in SMALL_LAYER_NAMES:
                size = self.weights[nm][l].size
                out[nm].append(self.small_sum[l][off:off + size].reshape(self.weights[nm].shape[1:]))
                off += size
            if l == DEPTH - 1:
                final = self.small_sum[l][off:off + self.weights["final_norm_g"].size]
                off += final.shape[0]
            if l == 0:
                loss = self.small_sum[l][off]
        res = {nm: jnp.stack(parts, axis=0) for nm, parts in out.items()}
        res["final_norm_g"] = final
        return loss, res

    def mix_grads(self):
        per = {FLAT_KEY[nm]: [] for nm, _ in FSDP_SECTIONS}
        for l in range(DEPTH):
            shard, off = self.rs["mix", l].result, 0
            for nm, nrows in FSDP_SECTIONS:
                key = FLAT_KEY[nm]
                sec = shard[off:off + nrows]
                per[key].append(sec.T if nm == "w_in" else sec.reshape(self.weights[key].shape[1:]))
                off += nrows
        return {key: jnp.stack(parts, axis=0) for key, parts in per.items()}

    def ff_shards(self):
        return [self.rs["ff", l].result for l in range(DEPTH)]


def kernel(x, c, positions, w_ada, b_ada, norm_mix_g, w_in, gmlp_ws, gmlp_bs, mla_q_norm_g, mla_kv_norm_g, mla_w_uq, mla_w_ukv, out_norm_gmlp_g, out_norm_mla_g, w_out, norm_ffn_g, w_ff1, w_ff2, final_norm_g, loss_target, m_w_ada, m_b_ada, m_norm_mix_g, m_w_in, m_gmlp_ws, m_gmlp_bs, m_mla_q_norm_g, m_mla_kv_norm_g, m_mla_w_uq, m_mla_w_ukv, m_out_norm_gmlp_g, m_out_norm_mla_g, m_w_out, m_norm_ffn_g, m_w_ff1, m_w_ff2, m_final_norm_g, v_w_ada, v_b_ada, v_norm_mix_g, v_w_in, v_gmlp_ws, v_gmlp_bs, v_mla_q_norm_g, v_mla_kv_norm_g, v_mla_w_uq, v_mla_w_ukv, v_out_norm_gmlp_g, v_out_norm_mla_g, v_w_out, v_norm_ffn_g, v_w_ff1, v_w_ff2, v_final_norm_g):
    weights = dict(w_ada=w_ada, b_ada=b_ada, norm_mix_g=norm_mix_g, w_in=w_in, gmlp_ws=gmlp_ws, gmlp_bs=gmlp_bs,
                   mla_q_norm_g=mla_q_norm_g, mla_kv_norm_g=mla_kv_norm_g, mla_w_uq=mla_w_uq, mla_w_ukv=mla_w_ukv,
                   out_norm_gmlp_g=out_norm_gmlp_g, out_norm_mla_g=out_norm_mla_g, w_out=w_out,
                   norm_ffn_g=norm_ffn_g, w_ff1=w_ff1, w_ff2=w_ff2, final_norm_g=final_norm_g)
    mom_m = dict(zip(W_NAMES, (m_w_ada, m_b_ada, m_norm_mix_g, m_w_in, m_gmlp_ws, m_gmlp_bs, m_mla_q_norm_g,
                               m_mla_kv_norm_g, m_mla_w_uq, m_mla_w_ukv, m_out_norm_gmlp_g, m_out_norm_mla_g,
                               m_w_out, m_norm_ffn_g, m_w_ff1, m_w_ff2, m_final_norm_g)))
    mom_v = dict(zip(W_NAMES, (v_w_ada, v_b_ada, v_norm_mix_g, v_w_in, v_gmlp_ws, v_gmlp_bs, v_mla_q_norm_g,
                               v_mla_kv_norm_g, v_mla_w_uq, v_mla_w_ukv, v_out_norm_gmlp_g, v_out_norm_mla_g,
                               v_w_out, v_norm_ffn_g, v_w_ff1, v_w_ff2, v_final_norm_g)))
    bsz, seq, d = x.shape
    px, py, pc = _position()
    chip = 2 * px + py
    dev = 2 * chip + pc
    ids = jnp.stack([pc, chip]).astype(jnp.int32)
    n_ex = N_DEV * bsz
    ada_cols = w_ada.shape[-1]

    c_all = _allgather8(c.reshape(bsz * d // 128, 128), name="gather_c").reshape(n_ex, d)
    mod_parts = []
    for l in range(DEPTH):
        bias = lax.dynamic_slice(b_ada[l], (chip * ada_cols,), (ada_cols,))[None]
        mod_parts.append(_mm(c_all, w_ada, dims="nn", name=f"l{l}_mod", tm=n_ex, tn=ada_cols, tk=d, n=ada_cols,
                             b_block=((None, d, ada_cols), lambda i, j, k, l=l: (l, k, j)),
                             epilogue=lambda acc, bv: (acc + bv,), extras=(bias,),
                             extra_specs=(pl.BlockSpec((1, ada_cols), lambda i, j, k: (0, j)),), a_fn=_silu))
    mod_g = _allgather8(jnp.concatenate(mod_parts, axis=0), name="gather_mod")
    mod_g = mod_g.reshape(N_CHIPS, 2, DEPTH, n_ex, ada_cols)[:, 0]
    mod_full = mod_g.transpose(1, 2, 0, 3).reshape(DEPTH, n_ex, N_CHIPS * ada_cols)
    mod_mine = lax.dynamic_slice(mod_full, (0, dev * bsz, 0), (DEPTH, bsz, N_MOD * d))
    mod_mine = jnp.pad(mod_mine.reshape(DEPTH, bsz, N_MOD, d), ((0, 0), (0, 0), (0, MOD_ROWS - N_MOD), (0, 0)))
    mods = [mod_mine[l] for l in range(DEPTH)]

    plan = _CommPlan(weights, ids, dev, pc)
    grad_x, dmods = _local_step(x, loss_target, positions, mods, final_norm_g, plan)
    grad = plan.mix_grads()

    loss, small = plan.small_grads()
    grad.update(small)

    dmod = jnp.stack(dmods, axis=1).reshape(bsz * DEPTH * N_MOD, d)
    dmod_all = _allgather8(dmod, name="gather_dmod").reshape(n_ex, DEPTH, N_MOD * d)
    gw, gb = [], []
    for l in range(DEPTH):
        dm = dmod_all[:, l]
        dm_cols = lax.dynamic_slice(dm, (0, chip * ada_cols), (n_ex, ada_cols))
        gw.append(_mm(c_all, dm_cols, dims="tn", name=f"l{l}_dw_ada", tm=d, tn=ada_cols, tk=n_ex, a_fn=_silu,
                      out_into=(w_ada.shape, (None, d, ada_cols), lambda i, j, k, l=l: (l, i, j),
                                gw[-1] if gw else None)))
        gb.append(_sum_leading(dm.reshape(n_ex, N_MOD * d // FLAT_W, FLAT_W), name=f"l{l}_db_ada").reshape(-1))
    grad["w_ada"] = gw[-1]
    grad["b_ada"] = jnp.stack(gb, axis=0)

    delta, new_m, new_v = {}, {}, {}
    ff_bufs = plan.ff_shards()
    for nm, row_off in (("w_ff1", 0), ("w_ff2", FLAT_W)):
        grad[nm], delta[nm], new_m[nm], new_v[nm] = _adamw_layers(
            weights[nm], mom_m[nm], mom_v[nm], ff_bufs, row_off, name=f"adamw_{nm}")
    for nm in W_NAMES:
        if nm not in delta:
            delta[nm], new_m[nm], new_v[nm] = _adamw(weights[nm], grad[nm], mom_m[nm], mom_v[nm],
                                                     name=f"adamw_{nm}")
    return (loss, grad_x, *[grad[nm] for nm in W_NAMES], *[delta[nm] for nm in W_NAMES],
            *[new_m[nm] for nm in W_NAMES], *[new_v[nm] for nm in W_NAMES])
```

```python
import functools
import math

import jax
import jax.numpy as jnp
from jax import lax
from jax.experimental import pallas as pl
from jax.experimental.pallas import tpu as pltpu

F32 = jnp.float32
BF16 = jnp.bfloat16

D_MODEL = 1024
DEPTH = 2
D_GMLP = 512
GROUPS = 8
GROUP_DIM = 64
CHUNK = 128
HEADS = 8
NOPE = 64
ROPE = 32
HEAD_PAD = 128
Q_RANK = 256
KV_RANK = 128
D_FF = 4096
N_MOD = 6
MOD_ROWS = 8
EPS = 1e-6
ROPE_THETA = 10000.0
D_IN = 1440
D_IN_PAD = 1536
ATTN_SCALE = (NOPE + ROPE) ** -0.5
LOG2E = math.log2(math.e)
SCALE_LOG2 = ATTN_SCALE * LOG2E
N_CHIPS = 4
N_DEV = 8

ADAM_LR = 0.001
ADAM_B1 = 0.9
ADAM_B2 = 0.999
ADAM_EPS = 1e-08
ADAM_WD = 0.01
ADAM_STEP = 10

VMEM_LIMIT = 48 * 1024 * 1024
FLAT_W = 1024
ROW_ALIGN = 256

NN = (((1,), (0,)), ((), ()))
NT = (((1,), (1,)), ((), ()))
TN = (((0,), (0,)), ((), ()))
MESH = pl.DeviceIdType.MESH

SHIFT1, SCALE1, GATE1, SHIFT2, SCALE2, GATE2 = range(6)

FSDP_SECTIONS = (("w_out", 256), ("w_in", 360), ("w_uq", 48), ("w_ukv", 32))


def _cparams(vmem=VMEM_LIMIT):
    return pltpu.CompilerParams(vmem_limit_bytes=vmem)


def _dot(a, b, dims=NN):
    return lax.dot_general(a, b, dims, preferred_element_type=F32)


def _iota(shape, axis):
    return lax.broadcasted_iota(jnp.int32, shape, axis)


def _gelu(x):
    k = math.sqrt(2.0 / math.pi)
    return 0.5 * x * (1.0 + jnp.tanh(k * (x + 0.044715 * (x * x * x))))


def _gelu_and_grad(x):
    k = math.sqrt(2.0 / math.pi)
    x2 = x * x
    t = jnp.tanh(k * (x + 0.044715 * (x2 * x)))
    half = 0.5 * (1.0 + t)
    return x * half, half + 0.5 * x * (1.0 - t * t) * (k * (1.0 + 3.0 * 0.044715 * x2))


def _rms_fwd(x, g, n):
    r = lax.rsqrt(jnp.sum(x * x, axis=-1, keepdims=True) * (1.0 / n) + EPS)
    return x * r * g


def _rms_bwd(x, g, dy, n):
    r = lax.rsqrt(jnp.sum(x * x, axis=-1, keepdims=True) * (1.0 / n) + EPS)
    xh = x * r
    dxh = dy * g
    dx = r * (dxh - xh * (jnp.sum(dxh * xh, axis=-1, keepdims=True) * (1.0 / n)))
    dg = jnp.sum(dy * xh, axis=0, keepdims=True)
    return dx, dg


def _pick_rows(rows, limit):
    if rows <= limit:
        return rows
    for t in range(limit, 7, -8):
        if rows % t == 0:
            return t
    return rows


def _mm(a, b, *, dims, name, tm=512, tn=1024, tk=1024, out_dtypes=(F32,), epilogue=None,
        extras=(), extra_specs=(), a_fn=None, weights_outer=False, side=None, b_block=None, n=None,
        out_into=None):
    if dims == "tn":
        kk, m = a.shape
    else:
        m, kk = a.shape
    if n is None:
        n = b.shape[0] if dims == "nt" else b.shape[1]
    tm, tn, tk = min(tm, m), min(tn, n), min(tk, kk)
    assert m % tm == 0 and n % tn == 0 and kk % tk == 0, (name, a.shape, b.shape, tm, tn, tk)
    ni, nj, nk = m // tm, n // tn, kk // tk

    def spec(shape, pick):
        if weights_outer:
            return pl.BlockSpec(shape, lambda j, i, k: pick(i, j, k))
        return pl.BlockSpec(shape, pick)

    if dims == "tn":
        a_spec = spec((tk, tm), lambda i, j, k: (k, i))
    else:
        a_spec = spec((tm, tk), lambda i, j, k: (i, k))
    if b_block is not None:
        b_spec = spec(*b_block)
    elif dims == "nt":
        b_spec = spec((tn, tk), lambda i, j, k: (j, k))
    else:
        b_spec = spec((tk, tn), lambda i, j, k: (k, j))
    o_spec = spec((tm, tn), lambda i, j, k: (i, j))
    out_shape = [jax.ShapeDtypeStruct((m, n), dt) for dt in out_dtypes]
    out_specs = [o_spec] * len(out_dtypes)
    prev, io_aliases = (), {}
    if out_into is not None:
        full_shape, block, index, before = out_into
        assert len(out_dtypes) == 1 and not extras
        out_shape = [jax.ShapeDtypeStruct(full_shape, out_dtypes[0])]
        out_specs = [spec(block, index)]
        if before is not None:
            prev, io_aliases = (before,), {2: 0}
    assert not (weights_outer and extra_specs)
    dn = {"nn": NN, "nt": NT, "tn": TN}[dims]
    n_ex, n_out = len(extras), len(out_dtypes)
    e_specs = [o_spec if s is None else s for s in (tuple(extra_specs) + (None,) * n_ex)[:n_ex]]

    n_prev = len(prev)

    def body(*refs):
        a_ref, b_ref = refs[0], refs[1]
        e_refs = refs[2 + n_prev:2 + n_prev + n_ex]
        o_refs = refs[2 + n_prev + n_ex:2 + n_prev + n_ex + n_out]
        av = a_ref[...]
        if a_fn is not None:
            av = a_fn(av)
        bv = b_ref[...]
        if bv.ndim == 3:
            if dims == "nt":
                bv = jnp.concatenate([bv[c] for c in range(bv.shape[0])], axis=1)
            else:
                bv = bv.reshape(-1, bv.shape[-1])
        part = _dot(av.astype(BF16), bv.astype(BF16), dn)

        def finish(acc):
            outs = (acc,) if epilogue is None else epilogue(acc, *[e[...] for e in e_refs])
            for o_ref, o in zip(o_refs, outs):
                o_ref[...] = o.astype(o_ref.dtype)

        if nk == 1:
            finish(part)
        else:
            acc_ref = refs[-1]
            k = pl.program_id(2)

            @pl.when(k == 0)
            def _():
                acc_ref[...] = part

            @pl.when(k > 0)
            def _():
                acc_ref[...] += part

            @pl.when(k == nk - 1)
            def _():
                finish(acc_ref[...])

    outs, side_outs = _hosted_call(
        body, name=name, grid=(nj, ni, nk) if weights_outer else (ni, nj, nk),
        in_specs=[a_spec, b_spec] + [ANY_SPEC] * n_prev + e_specs,
        out_specs=out_specs, out_shape=out_shape,
        scratch_shapes=[pltpu.VMEM((tm, tn), F32)] if nk > 1 else [],
        args=(a, b, *prev, *extras), side=side, io_aliases=io_aliases)
    res = outs[0] if n_out == 1 else outs
    return res if side is None else (res, side_outs)


def _mod_spec(tm, tn, seq):
    return pl.BlockSpec((1, MOD_ROWS, tn), lambda i, j, k: ((i * tm) // seq, 0, j))


def _normmod_fwd(x3, g, mod, shift_row, scale_row, *, name, tb=512, side=None):
    bsz, seq, d = x3.shape
    tb = min(tb, seq)

    def body(x_ref, g_ref, mod_ref, h_ref):
        m = mod_ref[0]
        nrm = _rms_fwd(x_ref[0], g_ref[...], d)
        h = nrm * (1.0 + m[scale_row:scale_row + 1, :]) + m[shift_row:shift_row + 1, :]
        h_ref[0] = h.astype(BF16)

    outs, side_outs = _hosted_call(
        body, name=name, grid=(bsz, seq // tb),
        in_specs=[pl.BlockSpec((1, tb, d), lambda b, i: (b, i, 0)),
                  pl.BlockSpec((1, d), lambda b, i: (0, 0)),
                  pl.BlockSpec((1, MOD_ROWS, d), lambda b, i: (b, 0, 0))],
        out_specs=[pl.BlockSpec((1, tb, d), lambda b, i: (b, i, 0))],
        out_shape=[jax.ShapeDtypeStruct((bsz, seq, d), BF16)],
        args=(x3, g, mod), side=side)
    return outs[0] if side is None else (outs[0], side_outs)


def _pair_mean_exact(x, lo):
    s_lo = jnp.sum(jnp.where(lo, x, 0.0), axis=-1, keepdims=True)
    s_hi = jnp.sum(jnp.where(lo, 0.0, x), axis=-1, keepdims=True)
    return jnp.where(lo, s_lo, s_hi) * (1.0 / GROUP_DIM)


def _gmlp_pair_fwd(gv_p, w0, w1, bias, lo):
    mu = _pair_mean_exact(gv_p, lo)
    dlt = gv_p - mu
    var = _pair_mean_exact(dlt * dlt, lo)
    rstd = lax.rsqrt(var + EPS)
    vn = dlt * rstd
    vnb = vn.astype(BF16)
    mixed = jnp.where(lo, _dot(w0, vnb), _dot(w1, vnb)) + bias
    return vn, vnb, rstd, mixed


def _tril_bf16(w):
    t = w.shape[-1]
    return jnp.where(_iota((t, t), 1) <= _iota((t, t), 0), w, 0.0).astype(BF16)


def _gmlp_fwd(z3, ws, bexp, g_out, *, name):
    bsz, seq, _ = z3.shape
    nc = seq // CHUNK

    def body(u_ref, v_ref, ws_ref, b_ref, g_ref, y_ref):
        lo = _iota((CHUNK, 128), 1) < GROUP_DIM
        gu = _gelu(u_ref[0].astype(F32))
        gv = _gelu(v_ref[0].astype(F32))
        parts = []
        for p in range(GROUPS // 2):
            sl = slice(128 * p, 128 * p + 128)
            w0 = _tril_bf16(ws_ref[2 * p])
            w1 = _tril_bf16(ws_ref[2 * p + 1])
            _, _, _, mixed = _gmlp_pair_fwd(gv[:, sl], w0, w1, b_ref[p], lo)
            parts.append(gu[:, sl] * mixed)
        yg = jnp.concatenate(parts, axis=1)
        y_ref[0] = _rms_fwd(yg, g_ref[...], D_GMLP).astype(BF16)

    return pl.pallas_call(
        body, name=name, grid=(bsz, nc),
        in_specs=[pl.BlockSpec((1, CHUNK, D_GMLP), lambda b, i: (b, i, 0)),
                  pl.BlockSpec((1, CHUNK, D_GMLP), lambda b, i: (b, i, 1)),
                  pl.BlockSpec((GROUPS, CHUNK, CHUNK), lambda b, i: (0, 0, 0)),
                  pl.BlockSpec((GROUPS // 2, CHUNK, 128), lambda b, i: (0, 0, 0)),
                  pl.BlockSpec((1, D_GMLP), lambda b, i: (0, 0))],
        out_specs=pl.BlockSpec((1, CHUNK, D_GMLP), lambda b, i: (b, i, 0)),
        out_shape=jax.ShapeDtypeStruct((bsz, seq, D_GMLP), BF16),
        compiler_params=_cparams(),
    )(z3, z3, ws, bexp, g_out)


def _gmlp_bwd(z3, dyn3, ws, wst, bexp, g_out, *, name, dy_col):
    bsz, seq, _ = z3.shape
    nc = seq // CHUNK
    npair = GROUPS // 2

    def body(u_ref, v_ref, dy_ref, ws_ref, wst_ref, b_ref, g_ref, duv_ref, dws_ref, dbs_ref, dg_ref, dbacc):
        first = jnp.logical_and(pl.program_id(0) == 0, pl.program_id(1) == 0)
        last = jnp.logical_and(pl.program_id(0) == bsz - 1, pl.program_id(1) == nc - 1)

        @pl.when(first)
        def _():
            dws_ref[...] = jnp.zeros_like(dws_ref)
            dg_ref[...] = jnp.zeros_like(dg_ref)
            dbacc[...] = jnp.zeros_like(dbacc)

        lo = _iota((CHUNK, 128), 1) < GROUP_DIM
        tril = _iota((CHUNK, CHUNK), 1) <= _iota((CHUNK, CHUNK), 0)
        u = u_ref[0].astype(F32)
        v = v_ref[0].astype(F32)
        gu, dgu = _gelu_and_grad(u)
        gv, dgv_dv = _gelu_and_grad(v)
        fwd = []
        for p in range(npair):
            sl = slice(128 * p, 128 * p + 128)
            w0 = _tril_bf16(ws_ref[2 * p])
            w1 = _tril_bf16(ws_ref[2 * p + 1])
            fwd.append(_gmlp_pair_fwd(gv[:, sl], w0, w1, b_ref[p], lo))
        yg = jnp.concatenate([gu[:, 128 * p:128 * p + 128] * fwd[p][3] for p in range(npair)], axis=1)
        dyg, dg = _rms_bwd(yg, g_ref[...], dy_ref[0].astype(F32), D_GMLP)
        dg_ref[...] += dg
        du_parts, dv_parts = [], []
        for p in range(npair):
            sl = slice(128 * p, 128 * p + 128)
            vn, vnb, rstd, mixed = fwd[p]
            dyg_p = dyg[:, sl]
            dmixed = dyg_p * gu[:, sl]
            dbacc[p] += dmixed
            dm_lo = jnp.where(lo, dmixed, 0.0).astype(BF16)
            dm_hi = jnp.where(lo, 0.0, dmixed).astype(BF16)
            dws_ref[2 * p] += jnp.where(tril, _dot(dm_lo, vnb, NT), 0.0)
            dws_ref[2 * p + 1] += jnp.where(tril, _dot(dm_hi, vnb, NT), 0.0)
            dmb = dmixed.astype(BF16)
            dvn = jnp.where(lo, _dot(wst_ref[2 * p], dmb), _dot(wst_ref[2 * p + 1], dmb))
            dgv = rstd * (dvn - _pair_mean_exact(dvn, lo) - vn * _pair_mean_exact(dvn * vn, lo))
            dv_parts.append(dgv * dgv_dv[:, sl])
            du_parts.append(dyg_p * mixed * dgu[:, sl])
        duv_ref[0] = jnp.concatenate(du_parts + dv_parts, axis=1).astype(BF16)

        @pl.when(last)
        def _():
            sel = jnp.where(_iota((8, 128), 0) == 0, (_iota((8, 128), 1) < GROUP_DIM).astype(F32),
                            jnp.where(_iota((8, 128), 0) == 1, (_iota((8, 128), 1) >= GROUP_DIM).astype(F32), 0.0))
            for p in range(npair):
                dbs_ref[p] = lax.dot_general(sel, dbacc[p], NT, precision=lax.Precision.HIGHEST,
                                             preferred_element_type=F32)

    duv, dws, dbs, dg = pl.pallas_call(
        body, name=name, grid=(bsz, nc),
        in_specs=[pl.BlockSpec((1, CHUNK, D_GMLP), lambda b, i: (b, i, 0)),
                  pl.BlockSpec((1, CHUNK, D_GMLP), lambda b, i: (b, i, 1)),
                  pl.BlockSpec((1, CHUNK, D_GMLP), lambda b, i: (b, i, dy_col)),
                  pl.BlockSpec((GROUPS, CHUNK, CHUNK), lambda b, i: (0, 0, 0)),
                  pl.BlockSpec((GROUPS, CHUNK, CHUNK), lambda b, i: (0, 0, 0)),
                  pl.BlockSpec((npair, CHUNK, 128), lambda b, i: (0, 0, 0)),
                  pl.BlockSpec((1, D_GMLP), lambda b, i: (0, 0))],
        out_specs=[pl.BlockSpec((1, CHUNK, 2 * D_GMLP), lambda b, i: (b, i, 0)),
                   pl.BlockSpec((GROUPS, CHUNK, CHUNK), lambda b, i: (0, 0, 0)),
                   pl.BlockSpec((npair, 8, CHUNK), lambda b, i: (0, 0, 0)),
                   pl.BlockSpec((1, D_GMLP), lambda b, i: (0, 0))],
        out_shape=[jax.ShapeDtypeStruct((bsz, seq, D_IN_PAD), BF16),
                   jax.ShapeDtypeStruct((GROUPS, CHUNK, CHUNK), F32),
                   jax.ShapeDtypeStruct((npair, 8, CHUNK), F32),
                   jax.ShapeDtypeStruct((1, D_GMLP), F32)],
        scratch_shapes=[pltpu.VMEM((npair, CHUNK, 128), F32)],
        compiler_params=_cparams(),
    )(z3, z3, dyn3, ws, wst, bexp, g_out)
    return duv, dws, dbs[:, :2, :].reshape(GROUPS, CHUNK), dg


def _partner(x):
    width = x.shape[-1]
    lane = _iota(x.shape, x.ndim - 1) % HEAD_PAD
    up = pltpu.roll(x, width - ROPE // 2, x.ndim - 1)
    down = pltpu.roll(x, ROPE // 2, x.ndim - 1)
    first = jnp.logical_and(lane >= NOPE, lane < NOPE + ROPE // 2)
    second = jnp.logical_and(lane >= NOPE + ROPE // 2, lane < NOPE + ROPE)
    return jnp.where(first, up, jnp.where(second, down, 0.0))


def _mla_prep_fwd(z3, g_q, g_kv, w_uq, w_ukv, ctab, stab, *, name, tb=256):
    bsz, seq, _ = z3.shape
    tb = min(tb, seq)
    hw = HEADS * HEAD_PAD

    def body(ql_ref, kvl_ref, krl_ref, gq_ref, gkv_ref, wuq_ref, wukv_ref, c_ref, s_ref, q_ref, kv_ref, kp_ref):
        cq = _rms_fwd(ql_ref[0].astype(F32), gq_ref[...], Q_RANK).astype(BF16)
        q = _dot(cq, wuq_ref[...])
        c1, s1 = c_ref[0], s_ref[0]
        c8, s8 = jnp.tile(c1, (1, HEADS)), jnp.tile(s1, (1, HEADS))
        q_ref[0] = ((q * c8 + _partner(q) * s8) * SCALE_LOG2).astype(BF16)
        ckv = _rms_fwd(kvl_ref[0].astype(F32), gkv_ref[...], KV_RANK).astype(BF16)
        kv = _dot(ckv, wukv_ref[...])
        kv_ref[0] = kv.astype(BF16)
        kr = krl_ref[0].astype(F32)
        kr = kr * c1 + _partner(kr) * s1
        lane = _iota((tb, hw), 1) % HEAD_PAD
        kp_ref[0] = jnp.where(lane < NOPE, kv, jnp.tile(kr, (1, HEADS))).astype(BF16)

    return pl.pallas_call(
        body, name=name, grid=(bsz, seq // tb),
        in_specs=[pl.BlockSpec((1, tb, Q_RANK), lambda b, i: (b, i, 4)),
                  pl.BlockSpec((1, tb, KV_RANK), lambda b, i: (b, i, 10)),
                  pl.BlockSpec((1, tb, HEAD_PAD), lambda b, i: (b, i, 11)),
                  pl.BlockSpec((1, Q_RANK), lambda b, i: (0, 0)),
                  pl.BlockSpec((1, KV_RANK), lambda b, i: (0, 0)),
                  pl.BlockSpec((Q_RANK, hw), lambda b, i: (0, 0)),
                  pl.BlockSpec((KV_RANK, hw), lambda b, i: (0, 0)),
                  pl.BlockSpec((1, tb, HEAD_PAD), lambda b, i: (b, i, 0)),
                  pl.BlockSpec((1, tb, HEAD_PAD), lambda b, i: (b, i, 0))],
        out_specs=[pl.BlockSpec((1, tb, hw), lambda b, i: (b, i, 0))] * 3,
        out_shape=[jax.ShapeDtypeStruct((bsz, seq, hw), BF16)] * 3,
        compiler_params=_cparams(),
    )(z3, z3, z3, g_q, g_kv, w_uq, w_ukv, ctab, stab)


def _mla_prep_bwd(z3, dz3, dq3, dk3, dv3, g_q, g_kv, w_uq, w_ukv, ctab, stab, *, name, tb=256):
    bsz, seq, _ = z3.shape
    tb = min(tb, seq)
    hw = HEADS * HEAD_PAD
    nb = seq // tb

    def body(ql_ref, kvl_ref, dq_ref, dk_ref, dv_ref, gq_ref, gkv_ref, wuq_ref, wukv_ref, c_ref, s_ref, dz_in,
             dz_ref, cq_ref, dqb_ref, ckv_ref, dkvb_ref, dgq_ref, dgkv_ref):
        @pl.when(jnp.logical_and(pl.program_id(0) == 0, pl.program_id(1) == 0))
        def _():
            dgq_ref[...] = jnp.zeros_like(dgq_ref)
            dgkv_ref[...] = jnp.zeros_like(dgkv_ref)

        c1, s1 = c_ref[0], s_ref[0]
        c8, s8 = jnp.tile(c1, (1, HEADS)), jnp.tile(s1, (1, HEADS))
        dqr = dq_ref[0]
        dqb = (dqr * c8 + _partner(dqr * s8)).astype(BF16)
        dqb_ref[0] = dqb
        ql = ql_ref[0].astype(F32)
        cq_ref[0] = _rms_fwd(ql, gq_ref[...], Q_RANK).astype(BF16)
        dql, dgq = _rms_bwd(ql, gq_ref[...], _dot(dqb, wuq_ref[...], NT), Q_RANK)
        dgq_ref[...] += dgq

        dk = dk_ref[0]
        lane = _iota((tb, hw), 1) % HEAD_PAD
        dkvb = jnp.where(lane < NOPE, dk, dv_ref[0]).astype(BF16)
        dkvb_ref[0] = dkvb
        kvl = kvl_ref[0].astype(F32)
        ckv_ref[0] = _rms_fwd(kvl, gkv_ref[...], KV_RANK).astype(BF16)
        dkvl, dgkv = _rms_bwd(kvl, gkv_ref[...], _dot(dkvb, wukv_ref[...], NT), KV_RANK)
        dgkv_ref[...] += dgkv

        dkr = dk[:, 0:HEAD_PAD].astype(F32)
        for h in range(1, HEADS):
            dkr = dkr + dk[:, HEAD_PAD * h:HEAD_PAD * (h + 1)].astype(F32)
        lane1 = _iota((tb, HEAD_PAD), 1)
        dkr = jnp.where(jnp.logical_and(lane1 >= NOPE, lane1 < NOPE + ROPE), dkr, 0.0)
        dkrl = dkr * c1 + _partner(dkr * s1)
        dz_ref[0] = jnp.concatenate([dql, dkvl, dkrl], axis=1).astype(BF16)

    return pl.pallas_call(
        body, name=name, grid=(bsz, nb),
        in_specs=[pl.BlockSpec((1, tb, Q_RANK), lambda b, i: (b, i, 4)),
                  pl.BlockSpec((1, tb, KV_RANK), lambda b, i: (b, i, 10)),
                  pl.BlockSpec((1, tb, hw), lambda b, i: (b, i, 0)),
                  pl.BlockSpec((1, tb, hw), lambda b, i: (b, i, 0)),
                  pl.BlockSpec((1, tb, hw), lambda b, i: (b, i, 0)),
                  pl.BlockSpec((1, Q_RANK), lambda b, i: (0, 0)),
                  pl.BlockSpec((1, KV_RANK), lambda b, i: (0, 0)),
                  pl.BlockSpec((Q_RANK, hw), lambda b, i: (0, 0)),
                  pl.BlockSpec((KV_RANK, hw), lambda b, i: (0, 0)),
                  pl.BlockSpec((1, tb, HEAD_PAD), lambda b, i: (b, i, 0)),
                  pl.BlockSpec((1, tb, HEAD_PAD), lambda b, i: (b, i, 0)),
                  ANY_SPEC],
        out_specs=[pl.BlockSpec((1, tb, 512), lambda b, i: (b, i, 2)),
                   pl.BlockSpec((1, tb, Q_RANK), lambda b, i: (b, i, 0)),
                   pl.BlockSpec((1, tb, hw), lambda b, i: (b, i, 0)),
                   pl.BlockSpec((1, tb, KV_RANK), lambda b, i: (b, i, 0)),
                   pl.BlockSpec((1, tb, hw), lambda b, i: (b, i, 0)),
                   pl.BlockSpec((1, Q_RANK), lambda b, i: (0, 0)),
                   pl.BlockSpec((1, KV_RANK), lambda b, i: (0, 0))],
        out_shape=[jax.ShapeDtypeStruct((bsz, seq, D_IN_PAD), BF16),
                   jax.ShapeDtypeStruct((bsz, seq, Q_RANK), BF16),
                   jax.ShapeDtypeStruct((bsz, seq, hw), BF16),
                   jax.ShapeDtypeStruct((bsz, seq, KV_RANK), BF16),
                   jax.ShapeDtypeStruct((bsz, seq, hw), BF16),
                   jax.ShapeDtypeStruct((1, Q_RANK), F32),
                   jax.ShapeDtypeStruct((1, KV_RANK), F32)],
        input_output_aliases={11: 0},
        compiler_params=_cparams(),
    )(z3, z3, dq3, dk3, dv3, g_q, g_kv, w_uq, w_ukv, ctab, stab, dz3)


ATTN_HEADS_PER_STEP = 4


def _attn_specs(tq, seq, hp):
    blk = pl.BlockSpec((1, tq, hp * HEAD_PAD), lambda b, h, i: (b, i, h))
    full = pl.BlockSpec((1, seq, hp * HEAD_PAD), lambda b, h, i: (b, 0, h))
    return blk, full


def _head(h):
    return slice(HEAD_PAD * h, HEAD_PAD * (h + 1))


def _attn_fwd(q3, kv3, kp3, *, name, tq=512, hp=ATTN_HEADS_PER_STEP, side=None):
    bsz, seq, hw = q3.shape
    tq = min(tq, seq)
    blk, full = _attn_specs(tq, seq, hp)

    def body(q_ref, kv_ref, kp_ref, o_ref, lse_ref):
        i = pl.program_id(2)

        def update(state, q, kp, kv, mask=None):
            m, l, acc = state
            s = _dot(q, kp, NT)
            if mask is not None:
                s = jnp.where(mask, s, -1e30)
            m_new = jnp.maximum(m, jnp.max(s, axis=1, keepdims=True))
            alpha = jnp.exp2(m - m_new)
            p = jnp.exp2(s - m_new)
            return m_new, alpha * l + jnp.sum(p, axis=1, keepdims=True), alpha * acc + _dot(p.astype(BF16), kv)

        def step(j, carry):
            st = pl.multiple_of(j * tq, tq)
            return tuple(update(carry[h], q_ref[0, :, _head(h)], kp_ref[0, pl.ds(st, tq), _head(h)],
                                kv_ref[0, pl.ds(st, tq), _head(h)]) for h in range(hp))

        init = tuple((jnp.full((tq, 1), -1e30, F32), jnp.zeros((tq, 1), F32), jnp.zeros((tq, HEAD_PAD), F32))
                     for _ in range(hp))
        carry = lax.fori_loop(0, i, step, init)

        st = pl.multiple_of(i * tq, tq)
        is_nope = _iota((tq, HEAD_PAD), 1) < NOPE
        causal = _iota((tq, tq), 1) <= _iota((tq, tq), 0)
        for h in range(hp):
            m, l, acc = update(carry[h], q_ref[0, :, _head(h)], kp_ref[0, pl.ds(st, tq), _head(h)],
                               kv_ref[0, pl.ds(st, tq), _head(h)], causal)
            o_ref[0, :, _head(h)] = jnp.where(is_nope, 0.0, acc / l).astype(BF16)
            lse_ref[0, :, _head(h)] = jnp.broadcast_to(m + jnp.log(l) * LOG2E, (tq, HEAD_PAD))

    outs, side_outs = _hosted_call(
        body, name=name, grid=(bsz, HEADS // hp, seq // tq),
        in_specs=[blk, full, full],
        out_specs=[blk, blk],
        out_shape=[jax.ShapeDtypeStruct((bsz, seq, hw), BF16), jax.ShapeDtypeStruct((bsz, seq, hw), F32)],
        args=(q3, kv3, kp3), side=side)
    return outs if side is None else (outs, side_outs)


def _attn_bwd(q3, kv3, kp3, do3, lse3, dl3, *, name, tq=512, hp=ATTN_HEADS_PER_STEP, side=None):
    bsz, seq, hw = q3.shape
    tq = min(tq, seq)
    nq = seq // tq
    blk, full = _attn_specs(tq, seq, hp)

    def body(kv_ref, kp_ref, q_ref, do_ref, lse_ref, dl_ref, dq_ref, dk_ref, dv_ref):
        j = pl.program_id(2)

        @pl.when(j == 0)
        def _():
            dq_ref[...] = jnp.zeros_like(dq_ref)

        def pair(h, row0, nrows, nkeys, mask=None):
            row0 = pl.multiple_of(row0, nrows)
            qi = q_ref[0, pl.ds(row0, nrows), _head(h)]
            do = do_ref[0, pl.ds(row0, nrows), _head(h)]
            kp = kp_ref[0, :nkeys, _head(h)]
            s = _dot(qi, kp, NT)
            if mask is not None:
                s = jnp.where(mask, s, -1e30)
            wide = nkeys // HEAD_PAD
            p = jnp.exp2(s - jnp.tile(lse_ref[0, pl.ds(row0, nrows), _head(h)], (1, wide)))
            dv = _dot(p.astype(BF16), do, TN)
            dp = _dot(do, kv_ref[0, :nkeys, _head(h)], NT)
            ds = (p * (dp - jnp.tile(dl_ref[0, pl.ds(row0, nrows), _head(h)], (1, wide)))).astype(BF16)
            dq_ref[0, pl.ds(row0, nrows), _head(h)] += _dot(ds, kp)
            return _dot(ds, qi, TN), dv

        def step(i, carry):
            st = pl.multiple_of(i * tq, tq)
            out = []
            for h in range(hp):
                dk, dv = pair(h, st, tq, tq)
                out.append((carry[h][0] + dk, carry[h][1] + dv))
            return tuple(out)

        causal = _iota((tq, tq), 1) <= _iota((tq, tq), 0)
        carry = tuple(pair(h, pl.multiple_of(j * tq, tq), tq, tq, causal) for h in range(hp))
        carry = lax.fori_loop(j + 1, nq, step, carry)
        for h in range(hp):
            dk_ref[0, :, _head(h)] = (carry[h][0] * (1.0 / LOG2E)).astype(BF16)
            dv_ref[0, :, _head(h)] = carry[h][1].astype(BF16)

        @pl.when(j == nq - 1)
        def _():
            dq_ref[...] = dq_ref[...] * ATTN_SCALE

    outs, side_outs = _hosted_call(
        body, name=name, grid=(bsz, HEADS // hp, nq),
        in_specs=[blk, blk, full, full, full, full],
        out_specs=[full, blk, blk],
        out_shape=[jax.ShapeDtypeStruct((bsz, seq, hw), F32)] + [jax.ShapeDtypeStruct((bsz, seq, hw), BF16)] * 2,
        args=(kv3, kp3, q3, do3, lse3, dl3), side=side)
    return outs if side is None else (outs, side_outs)


def _onorm_fwd(o3, yg3, g_pad, *, name, tb=512):
    bsz, seq, hw = o3.shape
    wg = yg3.shape[-1]
    tb = min(tb, seq)

    def body(o_ref, yg_ref, g_ref, y_ref):
        ya = _rms_fwd(o_ref[0].astype(F32), g_ref[...], HEADS * 64).astype(BF16)
        y_ref[0] = jnp.concatenate([ya, yg_ref[0]], axis=1)

    return pl.pallas_call(
        body, name=name, grid=(bsz, seq // tb),
        in_specs=[pl.BlockSpec((1, tb, hw), lambda b, i: (b, i, 0)),
                  pl.BlockSpec((1, tb, wg), lambda b, i: (b, i, 0)),
                  pl.BlockSpec((1, hw), lambda b, i: (0, 0))],
        out_specs=pl.BlockSpec((1, tb, hw + wg), lambda b, i: (b, i, 0)),
        out_shape=jax.ShapeDtypeStruct((bsz, seq, hw + wg), BF16),
        compiler_params=_cparams(),
    )(o3, yg3, g_pad)


def _onorm_bwd(o3, dy3, g_pad, *, name, tb=512):
    bsz, seq, hw = o3.shape
    tb = min(tb, seq)

    def body(o_ref, dy_ref, g_ref, do_ref, dl_ref, dg_ref):
        @pl.when(jnp.logical_and(pl.program_id(0) == 0, pl.program_id(1) == 0))
        def _():
            dg_ref[...] = jnp.zeros_like(dg_ref)

        o = o_ref[0].astype(F32)
        do, dg = _rms_bwd(o, g_ref[...], dy_ref[0].astype(F32), HEADS * 64)
        dg_ref[...] += dg
        do_ref[0] = do.astype(BF16)
        prod = do * o
        parts = []
        for h in range(HEADS):
            sh = jnp.sum(prod[:, HEAD_PAD * h:HEAD_PAD * (h + 1)], axis=1, keepdims=True)
            parts.append(jnp.broadcast_to(sh, (tb, HEAD_PAD)))
        dl_ref[0] = jnp.concatenate(parts, axis=1)

    return pl.pallas_call(
        body, name=name, grid=(bsz, seq // tb),
        in_specs=[pl.BlockSpec((1, tb, hw), lambda b, i: (b, i, 0)),
                  pl.BlockSpec((1, tb, hw), lambda b, i: (b, i, 0)),
                  pl.BlockSpec((1, hw), lambda b, i: (0, 0))],
        out_specs=[pl.BlockSpec((1, tb, hw), lambda b, i: (b, i, 0)),
                   pl.BlockSpec((1, tb, hw), lambda b, i: (b, i, 0)),
                   pl.BlockSpec((1, hw), lambda b, i: (0, 0))],
        out_shape=[jax.ShapeDtypeStruct((bsz, seq, hw), BF16),
                   jax.ShapeDtypeStruct((bsz, seq, hw), F32),
                   jax.ShapeDtypeStruct((1, hw), F32)],
        compiler_params=_cparams(),
    )(o3, dy3, g_pad)


def _resnode_bwd(x3, g, *, name, target3=None, dh3=None, dres3=None, mod_nm=None, rows=None,
                 branch3=None, mod_gate=None, gate_row=None, tb=512, side=None):
    bsz, seq, d = x3.shape
    tb = min(tb, seq)
    final = target3 is not None
    has_branch = branch3 is not None
    row_spec = pl.BlockSpec((1, tb, d), lambda b, i: (b, i, 0))
    vec_spec = pl.BlockSpec((1, d), lambda b, i: (0, 0))
    mod_spec = pl.BlockSpec((1, MOD_ROWS, d), lambda b, i: (b, 0, 0))

    ins, in_specs = [x3, g], [row_spec, vec_spec]
    if final:
        ins += [target3]
        in_specs += [row_spec]
    else:
        ins += [dh3, dres3, mod_nm]
        in_specs += [row_spec, row_spec, mod_spec]
    if has_branch:
        ins += [branch3, mod_gate]
        in_specs += [row_spec, mod_spec]

    out_names = ["dx", "dg"]
    out_specs = [row_spec, vec_spec]
    out_shape = [jax.ShapeDtypeStruct((bsz, seq, d), F32), jax.ShapeDtypeStruct((1, d), F32)]
    if final:
        out_names += ["loss"]
        out_specs += [pl.BlockSpec((1, 128), lambda b, i: (0, 0))]
        out_shape += [jax.ShapeDtypeStruct((1, 128), F32)]
    else:
        out_names += ["dnm"]
        out_specs += [mod_spec]
        out_shape += [jax.ShapeDtypeStruct((bsz, MOD_ROWS, d), F32)]
    if has_branch:
        out_names += ["dbr", "dgate"]
        out_specs += [row_spec, mod_spec]
        out_shape += [jax.ShapeDtypeStruct((bsz, seq, d), BF16), jax.ShapeDtypeStruct((bsz, MOD_ROWS, d), F32)]
    n_in = len(ins)

    def body(*refs):
        r = dict(zip(["x", "g"] + (["t"] if final else ["dh", "dres", "nm"]) + (["br", "gm"] if has_branch else []),
                     refs[:n_in]))
        o = dict(zip(out_names, refs[n_in:]))
        b_first = pl.program_id(1) == 0
        first = jnp.logical_and(pl.program_id(0) == 0, b_first)
        rowid = _iota((MOD_ROWS, d), 0)

        @pl.when(first)
        def _():
            o["dg"][...] = jnp.zeros_like(o["dg"])
            if final:
                o["loss"][...] = jnp.zeros_like(o["loss"])

        @pl.when(b_first)
        def _():
            if not final:
                o["dnm"][...] = jnp.zeros_like(o["dnm"])
            if has_branch:
                o["dgate"][...] = jnp.zeros_like(o["dgate"])

        x = r["x"][0]
        gv = r["g"][...]
        if final:
            e = _rms_fwd(x, gv, d) - r["t"][0]
            sq = jnp.sum(jnp.sum(e * e, axis=1, keepdims=True), axis=0, keepdims=True)
            o["loss"][...] += jnp.broadcast_to(sq * (0.5 / d), (1, 128))
            dx, dg = _rms_bwd(x, gv, e * (1.0 / d), d)
        else:
            m = r["nm"][0]
            dh = r["dh"][0].astype(F32)
            scale = m[rows[1]:rows[1] + 1, :]
            rstd = lax.rsqrt(jnp.sum(x * x, axis=-1, keepdims=True) * (1.0 / d) + EPS)
            xh = x * rstd
            nrm = xh * gv
            dshift = jnp.sum(dh, axis=0, keepdims=True)
            dscale = jnp.sum(dh * nrm, axis=0, keepdims=True)
            o["dnm"][0] += jnp.where(rowid == 0, dshift, jnp.where(rowid == 1, dscale, 0.0))
            dn = dh * (1.0 + scale)
            dg = jnp.sum(dn * xh, axis=0, keepdims=True)
            dxh = dn * gv
            dx = rstd * (dxh - xh * (jnp.sum(dxh * xh, axis=-1, keepdims=True) * (1.0 / d))) + r["dres"][0]
        o["dg"][...] += dg
        o["dx"][0] = dx
        if has_branch:
            gate = r["gm"][0][gate_row:gate_row + 1, :]
            o["dbr"][0] = (gate * dx).astype(BF16)
            dgate = jnp.sum(dx * r["br"][0], axis=0, keepdims=True)
            o["dgate"][0] += jnp.where(rowid == 0, dgate, 0.0)

    outs, side_outs = _hosted_call(
        body, name=name, grid=(bsz, seq // tb),
        in_specs=in_specs, out_specs=out_specs, out_shape=out_shape, args=tuple(ins), side=side)
    res = dict(zip(out_names, outs))
    return res if side is None else (res, side_outs)


def _adamw(w, g, m, v, *, name):
    shape = w.shape
    cols = shape[-1]
    rows = w.size // cols
    tr = _pick_rows(rows, max(8, (256 * 1024) // cols // 8 * 8))

    def body(w_ref, g_ref, m_ref, v_ref, d_ref, nm_ref, nv_ref):
        d_ref[...], nm_ref[...], nv_ref[...] = _adamw_math(w_ref[...], g_ref[...], m_ref[...], v_ref[...])

    if w.ndim == 3 and shape[1] % 8 == 0:
        tr3 = _pick_rows(shape[1], max(8, (256 * 1024) // cols // 8 * 8))
        spec3 = pl.BlockSpec((None, tr3, cols), lambda l, i: (l, i, 0))
        return tuple(pl.pallas_call(
            body, name=name, grid=(shape[0], shape[1] // tr3),
            in_specs=[spec3] * 4, out_specs=[spec3] * 3,
            out_shape=[jax.ShapeDtypeStruct(shape, F32)] * 3,
            compiler_params=_cparams(),
        )(w, g, m, v))
    spec = pl.BlockSpec((tr, cols), lambda i: (i, 0))
    outs = pl.pallas_call(
        body, name=name, grid=(rows // tr,),
        in_specs=[spec] * 4, out_specs=[spec] * 3,
        out_shape=[jax.ShapeDtypeStruct((rows, cols), F32)] * 3,
        compiler_params=_cparams(),
    )(*[t.reshape(rows, cols) for t in (w, g, m, v)])
    return tuple(o.reshape(shape) for o in outs)


def _adamw_math(w, g, m, v):
    c1 = 1.0 - ADAM_B1 ** ADAM_STEP
    c2 = 1.0 - ADAM_B2 ** ADAM_STEP
    nm = ADAM_B1 * m + (1.0 - ADAM_B1) * g
    nv = ADAM_B2 * v + (1.0 - ADAM_B2) * (g * g)
    delta = -ADAM_LR * ((nm / c1) / (jnp.sqrt(nv / c2) + ADAM_EPS) + ADAM_WD * w)
    return delta, nm, nv


def _adamw_layers(w, m, v, bufs, row_off, *, name, tr=256):
    depth, rows, cols = w.shape
    tr = min(tr, rows)
    assert rows % tr == 0 and row_off % tr == 0

    outs = None
    for l in range(depth):
        def body(w_ref, g_ref, m_ref, v_ref, *rest):
            go_ref, d_ref, nm_ref, nv_ref = rest[-4:]
            g = g_ref[...]
            go_ref[...] = g
            d_ref[...], nm_ref[...], nv_ref[...] = _adamw_math(w_ref[...], g, m_ref[...], v_ref[...])

        layer = pl.BlockSpec((None, tr, cols), lambda i, l=l: (l, i, 0))
        prev = () if outs is None else tuple(outs)
        outs = pl.pallas_call(
            body, name=f"{name}_l{l}", grid=(rows // tr,),
            in_specs=[layer, pl.BlockSpec((tr, cols), lambda i: (row_off // tr + i, 0)), layer, layer]
            + [ANY_SPEC] * len(prev),
            out_specs=[layer] * 4,
            out_shape=[jax.ShapeDtypeStruct(w.shape, F32)] * 4,
            input_output_aliases={4 + k: k for k in range(len(prev))},
            compiler_params=_cparams(),
        )(w, bufs[l], m, v, *prev)
    return tuple(outs)


def _sum_leading(x, *, name, tr=256):
    n, rows, cols = x.shape
    tr = _pick_rows(rows, tr)

    def body(x_ref, o_ref):
        acc = x_ref[0]
        for k in range(1, n):
            acc = acc + x_ref[k]
        o_ref[...] = acc

    return pl.pallas_call(
        body, name=name, grid=(rows // tr,),
        in_specs=[pl.BlockSpec((n, tr, cols), lambda i: (0, i, 0))],
        out_specs=pl.BlockSpec((tr, cols), lambda i: (i, 0)),
        out_shape=jax.ShapeDtypeStruct((rows, cols), F32),
        compiler_params=_cparams(),
    )(x)


def _position():
    return lax.axis_index("x"), lax.axis_index("y"), lax.axis_index("c")


def _allgather8(x, *, name):
    shape = x.shape

    def body(x_ref, out_ref, send_sems, recv_sems, local_sem):
        px, py, pc = _position()
        me, sibling = (px, py, pc), (px, py, 1 - pc)
        chips = [(1 - px, py), (px, 1 - py), (1 - px, 1 - py)]
        src_own = x_ref

        def slot(qx, qy, qc):
            return out_ref.at[4 * qx + 2 * qy + qc]

        def copy(k, block, to, src=None):
            return pltpu.make_async_remote_copy(
                src_ref=slot(*block) if src is None else src, dst_ref=slot(*block),
                send_sem=send_sems.at[k], recv_sem=recv_sems.at[k], device_id=to, device_id_type=MESH)

        mine = pltpu.make_async_copy(src_own, slot(*me), local_sem)
        mine.start()
        first = [copy(0, me, sibling, src=src_own)]
        first += [copy(1 + j, me, (*chip, pc), src=src_own) for j, chip in enumerate(chips)]
        for cp in first:
            cp.start()
        passed = [copy(4 + j, (*chip, pc), sibling) for j, chip in enumerate(chips)]
        for j, chip in enumerate(chips):
            copy(1 + j, (*chip, pc), me).wait_recv()
            passed[j].start()
        copy(0, sibling, me).wait_recv()
        for j, chip in enumerate(chips):
            copy(4 + j, (*chip, 1 - pc), me).wait_recv()
        for cp in first + passed:
            cp.wait_send()
        mine.wait()

    return pl.pallas_call(
        body, name=name,
        out_shape=jax.ShapeDtypeStruct((N_DEV,) + shape, x.dtype),
        in_specs=[pl.BlockSpec(memory_space=pl.ANY)],
        out_specs=pl.BlockSpec(memory_space=pl.ANY),
        scratch_shapes=[pltpu.SemaphoreType.DMA((7,)), pltpu.SemaphoreType.DMA((7,)), pltpu.SemaphoreType.DMA],
    )(x)


class _Exchange:
    def __init__(self, ins, out_shapes, n, build, aliases=None):
        self.ins, self.out_shapes, self.n, self.build = tuple(ins), tuple(out_shapes), n, build
        self.aliases = dict(aliases or {})

    def _descriptors(self, in_refs, out_refs, send_sems, recv_sems):
        sends, recvs = [], []
        for k, (src, dst, peer, landing) in enumerate(self.build(in_refs, out_refs)):
            sends.append(pltpu.make_async_remote_copy(
                src_ref=src, dst_ref=dst, send_sem=send_sems.at[k], recv_sem=recv_sems.at[k],
                device_id=peer, device_id_type=MESH))
            recvs.append(pltpu.make_async_remote_copy(
                src_ref=src, dst_ref=landing, send_sem=send_sems.at[k], recv_sem=recv_sems.at[k],
                device_id=peer, device_id_type=MESH))
        return sends, recvs

    def start(self, *refs):
        for cp in self._descriptors(*refs)[0]:
            cp.start()

    def finish(self, *refs):
        sends, recvs = self._descriptors(*refs)
        for cp in recvs:
            cp.wait_recv()
        for cp in sends:
            cp.wait_send()


ANY_SPEC = pl.BlockSpec(memory_space=pl.ANY)


def _hosted_call(body, *, name, grid, in_specs, out_specs, out_shape, args, scratch_shapes=(), side=None,
                 num_scalar_prefetch=0, io_aliases=None):
    in_specs, out_specs, out_shape = list(in_specs), list(out_specs), list(out_shape)
    n_in, n_out = len(in_specs) + num_scalar_prefetch, len(out_specs)
    kernel_body = body
    aliases = dict(io_aliases or {})
    if side is not None:
        s_in, s_out = len(side.ins), len(side.out_shapes)
        aliases.update({n_in + i: n_out + o for i, o in side.aliases.items()})

        def kernel_body(*refs):
            ins, s_ins = refs[:n_in], refs[n_in:n_in + s_in]
            outs = refs[n_in + s_in:n_in + s_in + n_out]
            s_outs = refs[n_in + s_in + n_out:n_in + s_in + n_out + s_out]
            scratch, sems = refs[n_in + s_in + n_out + s_out:-2], refs[-2:]
            first = functools.reduce(jnp.logical_and, [pl.program_id(a) == 0 for a in range(len(grid))])
            last = functools.reduce(jnp.logical_and, [pl.program_id(a) == g - 1 for a, g in enumerate(grid)])

            @pl.when(first)
            def _():
                side.start(s_ins, s_outs, *sems)

            body(*ins, *outs, *scratch)

            @pl.when(last)
            def _():
                side.finish(s_ins, s_outs, *sems)

        in_specs += [ANY_SPEC] * s_in
        out_specs += [ANY_SPEC] * s_out
        out_shape += list(side.out_shapes)
        scratch_shapes = list(scratch_shapes) + [pltpu.SemaphoreType.DMA((side.n,)),
                                                 pltpu.SemaphoreType.DMA((side.n,))]
        args = tuple(args) + side.ins
    if num_scalar_prefetch:
        grid_spec = pltpu.PrefetchScalarGridSpec(num_scalar_prefetch=num_scalar_prefetch, grid=grid,
                                                 in_specs=in_specs, out_specs=out_specs,
                                                 scratch_shapes=list(scratch_shapes))
        outs = pl.pallas_call(kernel_body, name=name, grid_spec=grid_spec, out_shape=out_shape,
                              input_output_aliases=aliases, compiler_params=_cparams())(*args)
    else:
        outs = pl.pallas_call(kernel_body, name=name, grid=grid, in_specs=in_specs, out_specs=out_specs,
                              out_shape=out_shape, scratch_shapes=list(scratch_shapes),
                              input_output_aliases=aliases, compiler_params=_cparams())(*args)
    return tuple(outs[:n_out]), tuple(outs[n_out:])


def _run_exchange(ex, *, name):
    s_in = len(ex.ins)

    def body(*refs):
        ins, outs, sems = refs[:s_in], refs[s_in:-2], refs[-2:]
        ex.start(ins, outs, *sems)
        ex.finish(ins, outs, *sems)

    outs = pl.pallas_call(
        body, name=name, out_shape=list(ex.out_shapes),
        in_specs=[ANY_SPEC] * s_in, out_specs=[ANY_SPEC] * len(ex.out_shapes),
        scratch_shapes=[pltpu.SemaphoreType.DMA((ex.n,)), pltpu.SemaphoreType.DMA((ex.n,))],
        input_output_aliases=ex.aliases,
    )(*ex.ins)
    return tuple(outs)


def _other_chips(px, py):
    return [(px, 1 - py), (1 - px, py), (1 - px, 1 - py)]


def _gather_spread(w_flat, halves=True):
    rows, w = w_flat.shape
    hr = rows // 2 if halves else rows

    def build(ins, outs):
        px, py, pc = _position()
        mine = ins[0].at[pl.ds(pc * hr, hr)] if halves else ins[0]
        me = 4 * px + 2 * py + pc
        plan = [((px, py, 1 - pc), me ^ 1)]
        plan += [((qx, qy, pc), 4 * qx + 2 * qy + pc) for qx, qy in _other_chips(px, py)]
        return [(mine, outs[0].at[me], peer, outs[0].at[their]) for peer, their in plan]

    return _Exchange([w_flat], [jax.ShapeDtypeStruct((N_DEV, hr, w), w_flat.dtype)], 4, build)


def _gather_pass_on(gath):
    def build(ins, outs):
        px, py, pc = _position()
        out = []
        for qx, qy in _other_chips(px, py):
            blk = 4 * qx + 2 * qy + pc
            out.append((outs[0].at[blk], outs[0].at[blk], (px, py, 1 - pc), outs[0].at[blk ^ 1]))
        return out

    return _Exchange([gath], [jax.ShapeDtypeStruct(gath.shape, gath.dtype)], 3, build, aliases={0: 0})


def _rs_halves(g):
    n, rows, w = g.shape
    hr = rows // 2

    def build(ins, outs):
        px, py, pc = _position()
        return [(ins[0].at[:, pl.ds((1 - pc) * hr, hr), :], outs[0], (px, py, 1 - pc), outs[0])]

    return _Exchange([g], [jax.ShapeDtypeStruct((n, hr, w), g.dtype)], 1, build)


def _rs_chips(sb):
    def build(ins, outs):
        px, py, pc = _position()
        return [(ins[0].at[j], outs[0].at[j], (qx, qy, pc), outs[0].at[j])
                for j, (qx, qy) in enumerate(_other_chips(px, py))]

    return _Exchange([sb], [jax.ShapeDtypeStruct(sb.shape, sb.dtype)], 3, build)


def _rs_complete(buf):
    def build(ins, outs):
        px, py, pc = _position()
        return [(outs[0].at[pc], outs[0].at[pc], (px, py, 1 - pc), outs[0].at[1 - pc])]

    return _Exchange([buf], [jax.ShapeDtypeStruct(buf.shape, buf.dtype)], 1, build, aliases={0: 0})


def _rs_partial(g, recv, ids, *, name, tr=128):
    _, rows, w = g.shape
    hr = rows // 2
    nb = hr // tr

    def body(ids_ref, g_ref, r_ref, o_ref):
        o_ref[0] = (g_ref[0] + r_ref[0]).astype(BF16)

    grid_spec = pltpu.PrefetchScalarGridSpec(
        num_scalar_prefetch=1, grid=(3, nb),
        in_specs=[pl.BlockSpec((1, tr, w), lambda j, i, ids: (ids[1] ^ (j + 1), ids[0] * nb + i, 0)),
                  pl.BlockSpec((1, tr, w), lambda j, i, ids: (ids[1] ^ (j + 1), i, 0))],
        out_specs=pl.BlockSpec((1, tr, w), lambda j, i, ids: (j, i, 0)))
    return pl.pallas_call(
        body, name=name, grid_spec=grid_spec,
        out_shape=jax.ShapeDtypeStruct((3, hr, w), BF16),
        compiler_params=_cparams(),
    )(ids, g, recv)


def _rs_total(g, recv, got, ids, *, name, tr=128):
    _, rows, w = g.shape
    hr = rows // 2
    nb = hr // tr

    def body(ids_ref, g_ref, r_ref, got_ref, o_ref):
        acc = g_ref[0] + r_ref[0]
        for j in range(3):
            acc = acc + got_ref[j].astype(F32)
        o_ref[0] = acc

    grid_spec = pltpu.PrefetchScalarGridSpec(
        num_scalar_prefetch=1, grid=(nb,),
        in_specs=[pl.BlockSpec((1, tr, w), lambda i, ids: (ids[1], ids[0] * nb + i, 0)),
                  pl.BlockSpec((1, tr, w), lambda i, ids: (ids[1], i, 0)),
                  pl.BlockSpec((3, tr, w), lambda i, ids: (0, i, 0))],
        out_specs=pl.BlockSpec((1, tr, w), lambda i, ids: (ids[0], i, 0)))
    return pl.pallas_call(
        body, name=name, grid_spec=grid_spec,
        out_shape=jax.ShapeDtypeStruct((2, hr, w), F32),
        compiler_params=_cparams(),
    )(ids, g, recv, got)


class _ReduceScatter:
    def __init__(self, g, ids, tag):
        self.g, self.ids, self.tag, self.stage, self.result = g, ids, tag, 0, None

    def next_exchange(self):
        if self.stage == 0:
            return _rs_halves(self.g)
        if self.stage == 1:
            return _rs_chips(self.sb)
        return _rs_complete(self.buf)

    def done(self, outs):
        if self.stage == 0:
            self.recv = outs[0]
            hr = self.recv.shape[1]
            self.tr = max(t for t in range(16, 513, 16) if hr % t == 0)
            self.sb = _rs_partial(self.g, self.recv, self.ids, name=f"{self.tag}_partial", tr=self.tr)
        elif self.stage == 1:
            self.buf = _rs_total(self.g, self.recv, outs[0], self.ids, name=f"{self.tag}_total", tr=self.tr)
        else:
            _, hr, w = outs[0].shape
            self.result = outs[0].reshape(2 * hr, w)
        self.stage += 1

    def finish_alone(self):
        names = ("halves", "chips", "complete")
        while self.stage < 3:
            self.done(_run_exchange(self.next_exchange(), name=f"{self.tag}_{names[self.stage]}"))
        return self.result


def _flat_rows():
    used = sum(r for _, r in FSDP_SECTIONS)
    return used, -(-used // ROW_ALIGN) * ROW_ALIGN


def _cols_to_chunks(full):
    rows, cols = full.shape
    t = full.reshape(rows, N_CHIPS, cols // N_CHIPS).transpose(1, 0, 2)
    return t.reshape(N_CHIPS, -1, FLAT_W)


def _chunks_to_cols(chunks, rows, cols):
    return chunks.reshape(N_CHIPS, rows, cols // N_CHIPS).transpose(1, 0, 2).reshape(rows, cols)


def _pad_heads(w, real):
    lead = w.shape[:-1]
    t = w.reshape(lead + (HEADS, real))
    t = jnp.pad(t, [(0, 0)] * len(lead) + [(0, 0), (0, HEAD_PAD - real)])
    return t.reshape(lead + (HEADS * HEAD_PAD,))


def _unpad_heads(w, real):
    lead = w.shape[:-1]
    return w.reshape(lead + (HEADS, HEAD_PAD))[..., :real].reshape(lead + (HEADS * real,))


def _pad_value_lanes(w, axis):
    w = jnp.moveaxis(w, axis, -1)
    lead = w.shape[:-1]
    t = w.reshape(lead + (HEADS, 64))
    t = jnp.pad(t, [(0, 0)] * len(lead) + [(0, 0), (HEAD_PAD - 64, 0)])
    return jnp.moveaxis(t.reshape(lead + (HEADS * HEAD_PAD,)), -1, axis)


def _unpad_value_lanes(w, axis):
    w = jnp.moveaxis(w, axis, -1)
    lead = w.shape[:-1]
    t = w.reshape(lead + (HEADS, HEAD_PAD))[..., HEAD_PAD - 64:]
    return jnp.moveaxis(t.reshape(lead + (HEADS * 64,)), -1, axis)


def _pad_w_in_t(wt):
    z = jnp.zeros((NOPE, wt.shape[1]), wt.dtype)
    z2 = jnp.zeros((HEAD_PAD - NOPE - ROPE, wt.shape[1]), wt.dtype)
    return jnp.concatenate([wt[:1408], z, wt[1408:], z2], axis=0)


def _unpad_w_in_t(wt):
    return jnp.concatenate([wt[:1408], wt[1408 + NOPE:1408 + NOPE + ROPE]], axis=0)


def _rope_tables(positions):
    freqs = ROPE_THETA ** (-jnp.arange(0, ROPE, 2, dtype=F32) / ROPE)
    ang = positions.astype(F32)[..., None] * freqs
    cos, sin = jnp.cos(ang), jnp.sin(ang)
    lead = cos.shape[:-1]
    ones = jnp.ones(lead + (NOPE,), F32)
    zeros_n = jnp.zeros(lead + (NOPE,), F32)
    zeros_p = jnp.zeros(lead + (HEAD_PAD - NOPE - ROPE,), F32)
    ctab = jnp.concatenate([ones, cos, cos, zeros_p], axis=-1)
    stab = jnp.concatenate([zeros_n, -sin, sin, zeros_p], axis=-1)
    return ctab, stab


def _mix_weights(full):
    return dict(
        w_in_t=_pad_w_in_t(full["w_in_t"]),
        w_uq=_pad_heads(full["mla_w_uq"], NOPE + ROPE),
        w_ukv=full["mla_w_ukv"],
        w_out=jnp.concatenate([_pad_value_lanes(full["w_out"][D_GMLP:], 0), full["w_out"][:D_GMLP]], axis=0),
    )


def _small_weights(p, l):
    ws = p["gmlp_ws"][l]
    tril = jnp.tril(jnp.ones((CHUNK, CHUNK), bool))
    bs = p["gmlp_bs"][l]
    bexp = jnp.repeat(bs.reshape(GROUPS // 2, 2, CHUNK).transpose(0, 2, 1), GROUP_DIM, axis=2)
    return dict(
        ws=ws,
        wst=jnp.where(tril[None], ws, 0.0).transpose(0, 2, 1).astype(BF16),
        bexp=bexp,
        g_mix=p["norm_mix_g"][l][None],
        g_ffn=p["norm_ffn_g"][l][None],
        g_q=p["mla_q_norm_g"][l][None],
        g_kv=p["mla_kv_norm_g"][l][None],
        g_og=p["out_norm_gmlp_g"][l][None],
        g_oa=_pad_value_lanes(p["out_norm_mla_g"][l], 0)[None],
    )


def _local_step(x3, target3, positions, mods, final_g, plan):
    bsz, seq, d = x3.shape
    tok = bsz * seq
    tmt = min(512, seq)
    tmk = min(1024, seq)
    tmw = min(2048, tok)
    chunk = (None, None, FLAT_W, FLAT_W)
    chunk2 = (2, None, FLAT_W, FLAT_W)
    ff_grad_shape = (N_CHIPS, 2 * FLAT_W, FLAT_W)
    ctab, stab = _rope_tables(positions)
    lw = [None] * DEPTH

    def flat(t):
        return t.reshape(tok, t.shape[-1])

    def cube(t):
        return t.reshape(bsz, seq, t.shape[-1])

    def carrying(l, tag, fn, *args, **kw):
        side = plan.host(l, tag)
        if side is None:
            return fn(*args, **kw)
        res, side_outs = fn(*args, side=side, **kw)
        plan.hosted(l, tag, side_outs)
        return res

    saved = []
    x = x3
    for l in range(DEPTH):
        lw[l] = plan.layer(l)
        w, mod = lw[l], mods[l]
        if l == 0:
            h1 = carrying(l, "fwd_normmod1", _normmod_fwd, x, w["g_mix"], mod, SHIFT1, SCALE1,
                          name=f"l{l}_normmod1")
        else:
            h1 = h1_next
        z = cube(_mm(flat(h1), w["w_in_t"], dims="nt", name=f"l{l}_w_in", tm=tmt, tn=D_IN_PAD, tk=d,
                     out_dtypes=(BF16,)))
        yg = _gmlp_fwd(z, w["ws"], w["bexp"], w["g_og"], name=f"l{l}_gmlp_fwd")
        q, kv, kp = _mla_prep_fwd(z, w["g_q"], w["g_kv"], w["w_uq"], w["w_ukv"], ctab, stab, name=f"l{l}_mla_prep")
        o, lse = carrying(l, "fwd_attn", _attn_fwd, q, kv, kp, name=f"l{l}_attn_fwd")
        y = _onorm_fwd(o, yg, w["g_oa"], name=f"l{l}_onorm_fwd")

        def normmod(xv, gv, gm, shift_row, scale_row):
            m = gm[0]
            return _rms_fwd(xv, gv, d) * (1.0 + m[scale_row:scale_row + 1, :]) + m[shift_row:shift_row + 1, :]

        def out_epi(po, xv, gm, gf):
            x_new = xv + gm[0][GATE1:GATE1 + 1, :] * po
            return po, x_new, normmod(x_new, gf, gm, SHIFT2, SCALE2)

        vec_spec = pl.BlockSpec((1, d), lambda i, j, k: (0, j))
        po, x_mid, h2 = carrying(l, "fwd_out_a", _mm, flat(y), w["w_out"], dims="nn", name=f"l{l}_w_out",
                                 tm=tmt, tn=d, tk=y.shape[-1], out_dtypes=(BF16, F32, BF16), epilogue=out_epi,
                                 extras=(flat(x), mod, w["g_ffn"]),
                                 extra_specs=(None, _mod_spec(tmt, d, seq), vec_spec))
        x_mid, h2 = cube(x_mid), cube(h2)

        def act_epi(acc):
            r = jnp.maximum(acc, 0.0)
            return (r * r,)

        r = carrying(l, "fwd_ff1", _mm, flat(h2), w["ff"], dims="nn", name=f"l{l}_w_ff1", tm=tmw, tn=FLAT_W,
                     tk=d, out_dtypes=(BF16,), epilogue=act_epi, weights_outer=True, n=D_FF,
                     b_block=(chunk, lambda i, j, k: (j, 0, 0, 0)))

        more = l + 1 < DEPTH

        def ff2_epi(acc, xv, gm, *nxt):
            x_new = xv + gm[0][GATE2:GATE2 + 1, :] * acc
            return (acc, x_new) + ((normmod(x_new, nxt[1], nxt[0], SHIFT1, SCALE1),) if more else ())

        mod_spec = _mod_spec(tmt, d, seq)
        outs = carrying(l, "fwd_ff2", _mm, r, w["ff"], dims="nn", name=f"l{l}_w_ff2", tm=tmt, tn=d, tk=2 * FLAT_W,
                        out_dtypes=(BF16, F32) + ((BF16,) if more else ()), epilogue=ff2_epi,
                        extras=(flat(x_mid), mod) + ((mods[l + 1], plan.layer(l + 1)["g_mix"]) if more else ()),
                        extra_specs=(None, mod_spec) + ((mod_spec, vec_spec) if more else ()), n=d,
                        b_block=(chunk2, lambda i, j, k: (k, 1, 0, 0)))
        f, x_out = outs[0], outs[1]
        h1_next = cube(outs[2]) if more else None
        saved.append(dict(x_in=x, h1=h1, z=z, q=q, kv=kv, kp=kp, o=o, lse=lse, y=y, po=cube(po),
                          x_mid=x_mid, h2=h2, r=r, f=cube(f)))
        x = cube(x_out)

    grads = [dict() for _ in range(DEPTH)]
    dmods = [None] * DEPTH
    top = DEPTH - 1
    node = _resnode_bwd(x, final_g[None], name="final_loss_bwd", target3=target3,
                        branch3=saved[top]["f"], mod_gate=mods[top], gate_row=GATE2)
    loss_part = node["loss"][0, 0]
    d_final_g = node["dg"][0]
    plan.scalars(loss_part, d_final_g)
    for l in range(DEPTH - 1, -1, -1):
        w, mod, s = lw[l], mods[l], saved[l]
        dx_out, dfb, dgate2 = node["dx"], flat(node["dbr"]), node["dgate"][:, 0]

        def dact_epi(acc, rv):
            return (acc * (2.0 * jnp.sqrt(rv.astype(F32))),)

        da = carrying(l, "bwd_d_r", _mm, dfb, w["ff"], dims="nt", name=f"l{l}_d_r", tm=tmw, tn=FLAT_W, tk=d,
                      out_dtypes=(BF16,), epilogue=dact_epi, extras=(s["r"],), weights_outer=True, n=D_FF,
                      b_block=(chunk, lambda i, j, k: (j, 1, 0, 0)))
        g_ff = carrying(l, "bwd_dw_ff2", _mm, s["r"], dfb, dims="tn", name=f"l{l}_dw_ff2", tm=FLAT_W, tn=d,
                        tk=2048, out_into=(ff_grad_shape, (None, FLAT_W, FLAT_W), lambda i, j, k: (i, 1, 0), None))
        g_ff = carrying(l, "bwd_dw_ff1", _mm, flat(s["h2"]), da, dims="tn", name=f"l{l}_dw_ff1", tm=d, tn=FLAT_W,
                        tk=2048, out_into=(ff_grad_shape, (None, FLAT_W, FLAT_W), lambda i, j, k: (j, 0, 0), g_ff))
        plan.ff_grads(l, g_ff)
        dh2 = carrying(l, "bwd_d_h2", _mm, da, w["ff"], dims="nt", name=f"l{l}_d_h2", tm=tmk, tn=d, tk=2 * FLAT_W,
                       n=d, b_block=(chunk2, lambda i, j, k: (k, 0, 0, 0)), out_dtypes=(BF16,))
        node = carrying(l, "bwd_resnode_ffn", _resnode_bwd, s["x_mid"], w["g_ffn"], name=f"l{l}_resnode_ffn",
                        dh3=cube(dh2), dres3=dx_out, mod_nm=mod, rows=(SHIFT2, SCALE2), branch3=s["po"],
                        mod_gate=mod, gate_row=GATE1)
        grads[l]["norm_ffn_g"] = node["dg"][0]
        dshift2, dscale2 = node["dnm"][:, 0], node["dnm"][:, 1]
        dx_mid, dpo, dgate1 = node["dx"], flat(node["dbr"]), node["dgate"][:, 0]

        wy = s["y"].shape[-1]
        dy = cube(carrying(l, "bwd_d_y", _mm, dpo, w["w_out"], dims="nt", name=f"l{l}_d_y", tm=tmt, tn=wy, tk=d,
                           out_dtypes=(BF16,)))
        dw_out = _mm(flat(s["y"]), dpo, dims="tn", name=f"l{l}_dw_out", tm=wy // 3, tn=d, tk=2048)
        hw = HEADS * HEAD_PAD
        grads[l]["w_out"] = jnp.concatenate([dw_out[hw:], _unpad_value_lanes(dw_out[:hw], 0)], axis=0)

        dz, dws, dbs, dg_og = _gmlp_bwd(s["z"], dy, w["ws"], w["wst"], w["bexp"], w["g_og"],
                                        name=f"l{l}_gmlp_bwd", dy_col=hw // D_GMLP)
        grads[l]["gmlp_ws"], grads[l]["gmlp_bs"], grads[l]["out_norm_gmlp_g"] = dws, dbs, dg_og[0]

        do, dl, dg_oa = _onorm_bwd(s["o"], dy, w["g_oa"], name=f"l{l}_onorm_bwd")
        grads[l]["out_norm_mla_g"] = _unpad_value_lanes(dg_oa[0], 0)
        dq, dk, dv = carrying(l, "bwd_attn_dkv", _attn_bwd, s["q"], s["kv"], s["kp"], do, s["lse"], dl,
                              name=f"l{l}_attn_bwd")
        dz, cq, dqb, ckv, dkvb, dg_q, dg_kv = _mla_prep_bwd(
            s["z"], dz, dq, dk, dv, w["g_q"], w["g_kv"], w["w_uq"], w["w_ukv"], ctab, stab,
            name=f"l{l}_mla_prep_bwd")
        grads[l]["mla_q_norm_g"], grads[l]["mla_kv_norm_g"] = dg_q[0], dg_kv[0]
        dw_uq = carrying(l, "bwd_dw_uq", _mm, flat(cq), flat(dqb), dims="tn", name=f"l{l}_dw_uq", tm=Q_RANK,
                         tn=1024, tk=4096)
        grads[l]["mla_w_uq"] = _unpad_heads(dw_uq, NOPE + ROPE)
        grads[l]["mla_w_ukv"] = _mm(flat(ckv), flat(dkvb), dims="tn", name=f"l{l}_dw_ukv", tm=KV_RANK, tn=1024, tk=4096)

        grads[l]["w_in_t"] = _unpad_w_in_t(_mm(flat(dz), flat(s["h1"]), dims="tn", name=f"l{l}_dw_in",
                                               tm=D_IN_PAD // 2, tn=d, tk=2048))
        plan.layer_grads(l, grads[l])
        dh1 = carrying(l, "bwd_d_h1", _mm, flat(dz), w["w_in_t"], dims="nn", name=f"l{l}_d_h1", tm=tmt, tn=d,
                       tk=D_IN_PAD, out_dtypes=(BF16,))
        below = dict(branch3=saved[l - 1]["f"], mod_gate=mods[l - 1], gate_row=GATE2) if l > 0 else {}
        node = carrying(l, "bwd_resnode_mix", _resnode_bwd, s["x_in"], w["g_mix"], name=f"l{l}_resnode_mix",
                        dh3=cube(dh1), dres3=dx_mid, mod_nm=mod, rows=(SHIFT1, SCALE1), **below)
        grads[l]["norm_mix_g"] = node["dg"][0]
        dshift1, dscale1 = node["dnm"][:, 0], node["dnm"][:, 1]
        dmods[l] = jnp.stack([dshift1, dscale1, dgate1, dshift2, dscale2, dgate2], axis=1)
        plan.layer_done(l)
    return node["dx"], dmods


W_NAMES = ("w_ada", "b_ada", "norm_mix_g", "w_in", "gmlp_ws", "gmlp_bs", "mla_q_norm_g", "mla_kv_norm_g",
           "mla_w_uq", "mla_w_ukv", "out_norm_gmlp_g", "out_norm_mla_g", "w_out", "norm_ffn_g", "w_ff1", "w_ff2",
           "final_norm_g")
FLAT_KEY = {"w_in": "w_in", "w_uq": "mla_w_uq", "w_ukv": "mla_w_ukv", "w_out": "w_out", "w_ff1": "w_ff1",
            "w_ff2": "w_ff2"}
COL_SHARDED = ("w_in", "w_uq", "w_ukv", "w_ff1")
FULL_SHAPE = {"w_in": (D_MODEL, D_IN), "w_uq": (Q_RANK, HEADS * (NOPE + ROPE)), "w_ukv": (KV_RANK, HEADS * 128),
              "w_out": (D_MODEL, D_MODEL), "w_ff1": (D_MODEL, D_FF), "w_ff2": (D_FF, D_MODEL)}
SMALL_LAYER_NAMES = ("norm_mix_g", "gmlp_ws", "gmlp_bs", "mla_q_norm_g", "mla_kv_norm_g", "out_norm_gmlp_g",
                     "out_norm_mla_g", "norm_ffn_g")


def _silu(v):
    return v * (1.0 / (1.0 + jnp.exp(-v)))


class _CommPlan:
    FWD = {"fwd_attn": ("ff", 0, "spread"), "fwd_out_a": ("ff", 0, "pass"),
           "fwd_ff1": ("mix", 1, "spread"), "fwd_ff2": ("mix", 1, "pass")}
    BWD = {"bwd_d_r": ("mix", 1), "bwd_dw_ff2": ("mix", 1), "bwd_dw_ff1": ("mix", 1),
           "bwd_d_h2": ("ff", 0), "bwd_attn_dkv": ("ff", 0), "bwd_dw_uq": ("ff", 0)}
    BWD_LAST = {"bwd_d_h1": ("mix", 0), "bwd_resnode_mix": ("mix", 0)}
    SMALL = {"bwd_resnode_ffn": "spread", "bwd_d_y": "pass"}

    def __init__(self, weights, ids, dev, core):
        self.weights, self.ids, self.dev, self.core = weights, ids, dev, core
        self.used, self.rows = _flat_rows()
        self.flat = {("mix", l): self._flat_mix(l) for l in range(DEPTH)}
        self.flat.update({("ff", l): jnp.concatenate([weights["w_ff1"][l], weights["w_ff2"][l]], axis=0).astype(BF16)
                          for l in range(DEPTH)})
        self.lw, self.rs, self.grads, self.spread = {}, {}, {}, {}
        self.small_vec, self.small_sum, self.small_spread, self.extra = {}, {}, None, {}
        self.lw = {l: _small_weights(weights, l) for l in range(DEPTH)}

    def _flat_mix(self, l):
        pieces = []
        for nm, _ in FSDP_SECTIONS:
            shard = self.weights[FLAT_KEY[nm]][l]
            pieces.append(shard.T if nm == "w_in" else shard.reshape(-1, FLAT_W))
        pieces.append(jnp.zeros((self.rows - self.used, FLAT_W), F32))
        return jnp.concatenate(pieces, axis=0).astype(BF16)

    def _arrived(self, group, l, gath):
        flat = self.flat[group, l]
        hr = flat.shape[0] // 2
        mine = lax.dynamic_slice(flat, (self.core * hr, 0), (hr, FLAT_W))
        gath = lax.dynamic_update_slice(gath, mine[None], (self.dev, 0, 0))
        if group == "ff":
            self.lw[l]["ff"] = gath.reshape(N_CHIPS, 2, hr, FLAT_W)
            return
        w_gath = gath.reshape(N_CHIPS, self.rows, FLAT_W)
        full, off = {}, 0
        for nm, nrows in FSDP_SECTIONS:
            sec = w_gath[:, off:off + nrows]
            off += nrows
            rows, cols = FULL_SHAPE[nm]
            if nm == "w_in":
                full["w_in_t"] = sec.reshape(cols, rows)
            else:
                full[FLAT_KEY[nm]] = (_chunks_to_cols(sec, rows, cols) if nm in COL_SHARDED
                                      else sec.reshape(rows, cols))
        self.lw[l].update(_mix_weights(full))

    def layer(self, l):
        return self.lw[l]

    def host(self, l, tag):
        if tag == "fwd_normmod1":
            return _gather_spread(self.flat["mix", 0]) if l == 0 else None
        if tag in self.FWD:
            group, ahead, what = self.FWD[tag]
            if l + ahead >= DEPTH:
                return None
            return _gather_spread(self.flat[group, l + ahead]) if what == "spread" else _gather_pass_on(self.spread[group])
        if tag in self.SMALL:
            if l + 1 not in self.small_vec:
                return None
            if self.SMALL[tag] == "spread":
                return _gather_spread(self.small_vec[l + 1], halves=False)
            return _gather_pass_on(self.small_spread)
        rs = self._rs_for(l, tag)
        return None if rs is None or rs.stage > 2 else rs.next_exchange()

    def _rs_for(self, l, tag):
        if tag in self.BWD_LAST:
            return self.rs.get(self.BWD_LAST[tag]) if l == 0 else None
        group, ahead = self.BWD[tag]
        return self.rs.get((group, l + ahead))

    def hosted(self, l, tag, outs):
        if tag == "fwd_normmod1":
            self._arrived("mix", 0, _run_exchange(_gather_pass_on(outs[0]), name="l0_mix_gather_pass_on")[0])
        elif tag in self.FWD:
            group, ahead, what = self.FWD[tag]
            if what == "spread":
                self.spread[group] = outs[0]
            else:
                self._arrived(group, l + ahead, outs[0])
        elif tag in self.SMALL:
            if self.SMALL[tag] == "spread":
                self.small_spread = outs[0]
            else:
                self._small_arrived(l + 1, outs[0])
        else:
            self._rs_for(l, tag).done(outs)

    def ff_grads(self, l, g_ff):
        self.rs["ff", l] = _ReduceScatter(g_ff, self.ids, f"l{l}_ff_rs")

    def layer_grads(self, l, grads):
        self.grads[l] = grads
        pieces = []
        for nm, nrows in FSDP_SECTIONS:
            if nm == "w_in":
                pieces.append(grads["w_in_t"].reshape(N_CHIPS, nrows, FLAT_W))
                continue
            g = grads[FLAT_KEY[nm]]
            pieces.append(_cols_to_chunks(g) if nm in COL_SHARDED else g.reshape(N_CHIPS, nrows, FLAT_W))
        pieces.append(jnp.zeros((N_CHIPS, self.rows - self.used, FLAT_W), F32))
        self.rs["mix", l] = _ReduceScatter(jnp.concatenate(pieces, axis=1), self.ids, f"l{l}_mix_rs")

    def scalars(self, loss_part, d_final_g):
        self.extra = {0: [loss_part[None]]}
        self.extra.setdefault(DEPTH - 1, []).insert(0, d_final_g)

    def layer_done(self, l):
        if l == 0:
            self.rs["mix", 0].finish_alone()
        parts = [self.grads[l][nm].reshape(-1) for nm in SMALL_LAYER_NAMES] + self.extra.get(l, [])
        vec = jnp.concatenate(parts)
        rows = -(-vec.shape[0] // (8 * FLAT_W)) * 8
        self.small_vec[l] = jnp.pad(vec, (0, rows * FLAT_W - vec.shape[0])).reshape(rows, FLAT_W)
        if l == 0:
            (gath,) = _run_exchange(_gather_spread(self.small_vec[0], halves=False), name="l0_small_spread")
            self._small_arrived(0, _run_exchange(_gather_pass_on(gath), name="l0_small_pass_on")[0])

    def _small_arrived(self, l, gath):
        gath = lax.dynamic_update_slice(gath, self.small_vec[l][None], (self.dev, 0, 0))
        self.small_sum[l] = _sum_leading(gath, name=f"l{l}_small_sum").reshape(-1)

    def small_grads(self):
        out = {nm: [] for nm in SMALL_LAYER_NAMES}
        for l in range(DEPTH):
            off = 0
            for nm in SMALL_LAYER_NAMES:
                size = self.weights[nm][l].size
                out[nm].append(self.small_sum[l][off:off + size].reshape(self.weights[nm].shape[1:]))
                off += size
            if l == DEPTH - 1:
                final = self.small_sum[l][off:off + self.weights["final_norm_g"].size]
                off += final.shape[0]
            if l == 0:
                loss = self.small_sum[l][off]
        res = {nm: jnp.stack(parts, axis=0) for nm, parts in out.items()}
        res["final_norm_g"] = final
        return loss, res

    def mix_grads(self):
        per = {FLAT_KEY[nm]: [] for nm, _ in FSDP_SECTIONS}
        for l in range(DEPTH):
            shard, off = self.rs["mix", l].result, 0
            for nm, nrows in FSDP_SECTIONS:
                key = FLAT_KEY[nm]
                sec = shard[off:off + nrows]
                per[key].append(sec.T if nm == "w_in" else sec.reshape(self.weights[key].shape[1:]))
                off += nrows
        return {key: jnp.stack(parts, axis=0) for key, parts in per.items()}

    def ff_shards(self):
        return [self.rs["ff", l].result for l in range(DEPTH)]


def kernel(x, c, positions, w_ada, b_ada, norm_mix_g, w_in, gmlp_ws, gmlp_bs, mla_q_norm_g, mla_kv_norm_g, mla_w_uq, mla_w_ukv, out_norm_gmlp_g, out_norm_mla_g, w_out, norm_ffn_g, w_ff1, w_ff2, final_norm_g, loss_target, m_w_ada, m_b_ada, m_norm_mix_g, m_w_in, m_gmlp_ws, m_gmlp_bs, m_mla_q_norm_g, m_mla_kv_norm_g, m_mla_w_uq, m_mla_w_ukv, m_out_norm_gmlp_g, m_out_norm_mla_g, m_w_out, m_norm_ffn_g, m_w_ff1, m_w_ff2, m_final_norm_g, v_w_ada, v_b_ada, v_norm_mix_g, v_w_in, v_gmlp_ws, v_gmlp_bs, v_mla_q_norm_g, v_mla_kv_norm_g, v_mla_w_uq, v_mla_w_ukv, v_out_norm_gmlp_g, v_out_norm_mla_g, v_w_out, v_norm_ffn_g, v_w_ff1, v_w_ff2, v_final_norm_g):
    weights = dict(w_ada=w_ada, b_ada=b_ada, norm_mix_g=norm_mix_g, w_in=w_in, gmlp_ws=gmlp_ws, gmlp_bs=gmlp_bs,
                   mla_q_norm_g=mla_q_norm_g, mla_kv_norm_g=mla_kv_norm_g, mla_w_uq=mla_w_uq, mla_w_ukv=mla_w_ukv,
                   out_norm_gmlp_g=out_norm_gmlp_g, out_norm_mla_g=out_norm_mla_g, w_out=w_out,
                   norm_ffn_g=norm_ffn_g, w_ff1=w_ff1, w_ff2=w_ff2, final_norm_g=final_norm_g)
    mom_m = dict(zip(W_NAMES, (m_w_ada, m_b_ada, m_norm_mix_g, m_w_in, m_gmlp_ws, m_gmlp_bs, m_mla_q_norm_g,
                               m_mla_kv_norm_g, m_mla_w_uq, m_mla_w_ukv, m_out_norm_gmlp_g, m_out_norm_mla_g,
                               m_w_out, m_norm_ffn_g, m_w_ff1, m_w_ff2, m_final_norm_g)))
    mom_v = dict(zip(W_NAMES, (v_w_ada, v_b_ada, v_norm_mix_g, v_w_in, v_gmlp_ws, v_gmlp_bs, v_mla_q_norm_g,
                               v_mla_kv_norm_g, v_mla_w_uq, v_mla_w_ukv, v_out_norm_gmlp_g, v_out_norm_mla_g,
                               v_w_out, v_norm_ffn_g, v_w_ff1, v_w_ff2, v_final_norm_g)))
    bsz, seq, d = x.shape
    px, py, pc = _position()
    chip = 2 * px + py
    dev = 2 * chip + pc
    ids = jnp.stack([pc, chip]).astype(jnp.int32)
    n_ex = N_DEV * bsz
    ada_cols = w_ada.shape[-1]

    c_all = _allgather8(c.reshape(bsz * d // 128, 128), name="gather_c").reshape(n_ex, d)
    mod_parts = []
    for l in range(DEPTH):
        bias = lax.dynamic_slice(b_ada[l], (chip * ada_cols,), (ada_cols,))[None]
        mod_parts.append(_mm(c_all, w_ada, dims="nn", name=f"l{l}_mod", tm=n_ex, tn=ada_cols, tk=d, n=ada_cols,
                             b_block=((None, d, ada_cols), lambda i, j, k, l=l: (l, k, j)),
                             epilogue=lambda acc, bv: (acc + bv,), extras=(bias,),
                             extra_specs=(pl.BlockSpec((1, ada_cols), lambda i, j, k: (0, j)),), a_fn=_silu))
    mod_g = _allgather8(jnp.concatenate(mod_parts, axis=0), name="gather_mod")
    mod_g = mod_g.reshape(N_CHIPS, 2, DEPTH, n_ex, ada_cols)[:, 0]
    mod_full = mod_g.transpose(1, 2, 0, 3).reshape(DEPTH, n_ex, N_CHIPS * ada_cols)
    mod_mine = lax.dynamic_slice(mod_full, (0, dev * bsz, 0), (DEPTH, bsz, N_MOD * d))
    mod_mine = jnp.pad(mod_mine.reshape(DEPTH, bsz, N_MOD, d), ((0, 0), (0, 0), (0, MOD_ROWS - N_MOD), (0, 0)))
    mods = [mod_mine[l] for l in range(DEPTH)]

    plan = _CommPlan(weights, ids, dev, pc)
    grad_x, dmods = _local_step(x, loss_target, positions, mods, final_norm_g, plan)
    grad = plan.mix_grads()

    loss, small = plan.small_grads()
    grad.update(small)

    dmod = jnp.stack(dmods, axis=1).reshape(bsz * DEPTH * N_MOD, d)
    dmod_all = _allgather8(dmod, name="gather_dmod").reshape(n_ex, DEPTH, N_MOD * d)
    gw, gb = [], []
    for l in range(DEPTH):
        dm = dmod_all[:, l]
        dm_cols = lax.dynamic_slice(dm, (0, chip * ada_cols), (n_ex, ada_cols))
        gw.append(_mm(c_all, dm_cols, dims="tn", name=f"l{l}_dw_ada", tm=d, tn=ada_cols, tk=n_ex, a_fn=_silu,
                      out_into=(w_ada.shape, (None, d, ada_cols), lambda i, j, k, l=l: (l, i, j),
                                gw[-1] if gw else None)))
        gb.append(_sum_leading(dm.reshape(n_ex, N_MOD * d // FLAT_W, FLAT_W), name=f"l{l}_db_ada").reshape(-1))
    grad["w_ada"] = gw[-1]
    grad["b_ada"] = jnp.stack(gb, axis=0)

    delta, new_m, new_v = {}, {}, {}
    ff_bufs = plan.ff_shards()
    for nm, row_off in (("w_ff1", 0), ("w_ff2", FLAT_W)):
        grad[nm], delta[nm], new_m[nm], new_v[nm] = _adamw_layers(
            weights[nm], mom_m[nm], mom_v[nm], ff_bufs, row_off, name=f"adamw_{nm}")
    for nm in W_NAMES:
        if nm not in delta:
            delta[nm], new_m[nm], new_v[nm] = _adamw(weights[nm], grad[nm], mom_m[nm], mom_v[nm],
                                                     name=f"adamw_{nm}")
    return (loss, grad_x, *[grad[nm] for nm in W_NAMES], *[delta[nm] for nm in W_NAMES],
            *[new_m[nm] for nm in W_NAMES], *[new_v[nm] for nm in W_NAMES])
```

```python
import functools
import math

import jax
import jax.numpy as jnp
from jax import lax
from jax.experimental import pallas as pl
from jax.experimental.pallas import tpu as pltpu

F32 = jnp.float32
BF16 = jnp.bfloat16

D_MODEL = 1024
DEPTH = 2
D_GMLP = 512
GROUPS = 8
GROUP_DIM = 64
CHUNK = 128
HEADS = 8
NOPE = 64
ROPE = 32
HEAD_PAD = 128
Q_RANK = 256
KV_RANK = 128
D_FF = 4096
N_MOD = 6
MOD_ROWS = 8
EPS = 1e-6
ROPE_THETA = 10000.0
D_IN = 1440
D_IN_PAD = 1536
ATTN_SCALE = (NOPE + ROPE) ** -0.5
LOG2E = math.log2(math.e)
SCALE_LOG2 = ATTN_SCALE * LOG2E
N_CHIPS = 4
N_DEV = 8

ADAM_LR = 0.001
ADAM_B1 = 0.9
ADAM_B2 = 0.999
ADAM_EPS = 1e-08
ADAM_WD = 0.01
ADAM_STEP = 10

VMEM_LIMIT = 48 * 1024 * 1024
FLAT_W = 1024
ROW_ALIGN = 256

NN = (((1,), (0,)), ((), ()))
NT = (((1,), (1,)), ((), ()))
TN = (((0,), (0,)), ((), ()))
MESH = pl.DeviceIdType.MESH

SHIFT1, SCALE1, GATE1, SHIFT2, SCALE2, GATE2 = range(6)

FSDP_SECTIONS = (("w_out", 256), ("w_in", 360), ("w_uq", 48), ("w_ukv", 32))


def _cparams(vmem=VMEM_LIMIT):
    return pltpu.CompilerParams(vmem_limit_bytes=vmem)


def _dot(a, b, dims=NN):
    return lax.dot_general(a, b, dims, preferred_element_type=F32)


def _iota(shape, axis):
    return lax.broadcasted_iota(jnp.int32, shape, axis)


def _gelu(x):
    k = math.sqrt(2.0 / math.pi)
    return 0.5 * x * (1.0 + jnp.tanh(k * (x + 0.044715 * (x * x * x))))


def _gelu_and_grad(x):
    k = math.sqrt(2.0 / math.pi)
    x2 = x * x
    t = jnp.tanh(k * (x + 0.044715 * (x2 * x)))
    half = 0.5 * (1.0 + t)
    return x * half, half + 0.5 * x * (1.0 - t * t) * (k * (1.0 + 3.0 * 0.044715 * x2))


def _rms_fwd(x, g, n):
    r = lax.rsqrt(jnp.sum(x * x, axis=-1, keepdims=True) * (1.0 / n) + EPS)
    return x * r * g


def _rms_bwd(x, g, dy, n):
    r = lax.rsqrt(jnp.sum(x * x, axis=-1, keepdims=True) * (1.0 / n) + EPS)
    xh = x * r
    dxh = dy * g
    dx = r * (dxh - xh * (jnp.sum(dxh * xh, axis=-1, keepdims=True) * (1.0 / n)))
    dg = jnp.sum(dy * xh, axis=0, keepdims=True)
    return dx, dg


def _pick_rows(rows, limit):
    if rows <= limit:
        return rows
    for t in range(limit, 7, -8):
        if rows % t == 0:
            return t
    return rows


def _mm(a, b, *, dims, name, tm=512, tn=1024, tk=1024, out_dtypes=(F32,), epilogue=None,
        extras=(), extra_specs=(), a_fn=None, weights_outer=False, side=None, b_block=None, n=None,
        out_into=None):
    if dims == "tn":
        kk, m = a.shape
    else:
        m, kk = a.shape
    if n is None:
        n = b.shape[0] if dims == "nt" else b.shape[1]
    tm, tn, tk = min(tm, m), min(tn, n), min(tk, kk)
    assert m % tm == 0 and n % tn == 0 and kk % tk == 0, (name, a.shape, b.shape, tm, tn, tk)
    ni, nj, nk = m // tm, n // tn, kk // tk

    def spec(shape, pick):
        if weights_outer:
            return pl.BlockSpec(shape, lambda j, i, k: pick(i, j, k))
        return pl.BlockSpec(shape, pick)

    if dims == "tn":
        a_spec = spec((tk, tm), lambda i, j, k: (k, i))
    else:
        a_spec = spec((tm, tk), lambda i, j, k: (i, k))
    if b_block is not None:
        b_spec = spec(*b_block)
    elif dims == "nt":
        b_spec = spec((tn, tk), lambda i, j, k: (j, k))
    else:
        b_spec = spec((tk, tn), lambda i, j, k: (k, j))
    o_spec = spec((tm, tn), lambda i, j, k: (i, j))
    out_shape = [jax.ShapeDtypeStruct((m, n), dt) for dt in out_dtypes]
    out_specs = [o_spec] * len(out_dtypes)
    prev, io_aliases = (), {}
    if out_into is not None:
        full_shape, block, index, before = out_into
        assert len(out_dtypes) == 1 and not extras
        out_shape = [jax.ShapeDtypeStruct(full_shape, out_dtypes[0])]
        out_specs = [spec(block, index)]
        if before is not None:
            prev, io_aliases = (before,), {2: 0}
    assert not (weights_outer and extra_specs)
    dn = {"nn": NN, "nt": NT, "tn": TN}[dims]
    n_ex, n_out = len(extras), len(out_dtypes)
    e_specs = [o_spec if s is None else s for s in (tuple(extra_specs) + (None,) * n_ex)[:n_ex]]

    n_prev = len(prev)

    def body(*refs):
        a_ref, b_ref = refs[0], refs[1]
        e_refs = refs[2 + n_prev:2 + n_prev + n_ex]
        o_refs = refs[2 + n_prev + n_ex:2 + n_prev + n_ex + n_out]
        av = a_ref[...]
        if a_fn is not None:
            av = a_fn(av)
        bv = b_ref[...]
        if bv.ndim == 3:
            if dims == "nt":
                bv = jnp.concatenate([bv[c] for c in range(bv.shape[0])], axis=1)
            else:
                bv = bv.reshape(-1, bv.shape[-1])
        part = _dot(av.astype(BF16), bv.astype(BF16), dn)

        def finish(acc):
            outs = (acc,) if epilogue is None else epilogue(acc, *[e[...] for e in e_refs])
            for o_ref, o in zip(o_refs, outs):
                o_ref[...] = o.astype(o_ref.dtype)

        if nk == 1:
            finish(part)
        else:
            acc_ref = refs[-1]
            k = pl.program_id(2)

            @pl.when(k == 0)
            def _():
                acc_ref[...] = part

            @pl.when(k > 0)
            def _():
                acc_ref[...] += part

            @pl.when(k == nk - 1)
            def _():
                finish(acc_ref[...])

    outs, side_outs = _hosted_call(
        body, name=name, grid=(nj, ni, nk) if weights_outer else (ni, nj, nk),
        in_specs=[a_spec, b_spec] + [ANY_SPEC] * n_prev + e_specs,
        out_specs=out_specs, out_shape=out_shape,
        scratch_shapes=[pltpu.VMEM((tm, tn), F32)] if nk > 1 else [],
        args=(a, b, *prev, *extras), side=side, io_aliases=io_aliases)
    res = outs[0] if n_out == 1 else outs
    return res if side is None else (res, side_outs)


def _mod_spec(tm, tn, seq):
    return pl.BlockSpec((1, MOD_ROWS, tn), lambda i, j, k: ((i * tm) // seq, 0, j))


def _normmod_fwd(x3, g, mod, shift_row, scale_row, *, name, tb=512, side=None):
    bsz, seq, d = x3.shape
    tb = min(tb, seq)

    def body(x_ref, g_ref, mod_ref, h_ref):
        m = mod_ref[0]
        nrm = _rms_fwd(x_ref[0], g_ref[...], d)
        h = nrm * (1.0 + m[scale_row:scale_row + 1, :]) + m[shift_row:shift_row + 1, :]
        h_ref[0] = h.astype(BF16)

    outs, side_outs = _hosted_call(
        body, name=name, grid=(bsz, seq // tb),
        in_specs=[pl.BlockSpec((1, tb, d), lambda b, i: (b, i, 0)),
                  pl.BlockSpec((1, d), lambda b, i: (0, 0)),
                  pl.BlockSpec((1, MOD_ROWS, d), lambda b, i: (b, 0, 0))],
        out_specs=[pl.BlockSpec((1, tb, d), lambda b, i: (b, i, 0))],
        out_shape=[jax.ShapeDtypeStruct((bsz, seq, d), BF16)],
        args=(x3, g, mod), side=side)
    return outs[0] if side is None else (outs[0], side_outs)


def _pair_mean_exact(x, lo):
    s_lo = jnp.sum(jnp.where(lo, x, 0.0), axis=-1, keepdims=True)
    s_hi = jnp.sum(jnp.where(lo, 0.0, x), axis=-1, keepdims=True)
    return jnp.where(lo, s_lo, s_hi) * (1.0 / GROUP_DIM)


def _gmlp_pair_fwd(gv_p, w0, w1, bias, lo):
    mu = _pair_mean_exact(gv_p, lo)
    dlt = gv_p - mu
    var = _pair_mean_exact(dlt * dlt, lo)
    rstd = lax.rsqrt(var + EPS)
    vn = dlt * rstd
    vnb = vn.astype(BF16)
    mixed = jnp.where(lo, _dot(w0, vnb), _dot(w1, vnb)) + bias
    return vn, vnb, rstd, mixed


def _tril_bf16(w):
    t = w.shape[-1]
    return jnp.where(_iota((t, t), 1) <= _iota((t, t), 0), w, 0.0).astype(BF16)


def _gmlp_fwd(z3, ws, bexp, g_out, *, name):
    bsz, seq, _ = z3.shape
    nc = seq // CHUNK

    def body(u_ref, v_ref, ws_ref, b_ref, g_ref, y_ref):
        lo = _iota((CHUNK, 128), 1) < GROUP_DIM
        gu = _gelu(u_ref[0].astype(F32))
        gv = _gelu(v_ref[0].astype(F32))
        parts = []
        for p in range(GROUPS // 2):
            sl = slice(128 * p, 128 * p + 128)
            w0 = _tril_bf16(ws_ref[2 * p])
            w1 = _tril_bf16(ws_ref[2 * p + 1])
            _, _, _, mixed = _gmlp_pair_fwd(gv[:, sl], w0, w1, b_ref[p], lo)
            parts.append(gu[:, sl] * mixed)
        yg = jnp.concatenate(parts, axis=1)
        y_ref[0] = _rms_fwd(yg, g_ref[...], D_GMLP).astype(BF16)

    return pl.pallas_call(
        body, name=name, grid=(bsz, nc),
        in_specs=[pl.BlockSpec((1, CHUNK, D_GMLP), lambda b, i: (b, i, 0)),
                  pl.BlockSpec((1, CHUNK, D_GMLP), lambda b, i: (b, i, 1)),
                  pl.BlockSpec((GROUPS, CHUNK, CHUNK), lambda b, i: (0, 0, 0)),
                  pl.BlockSpec((GROUPS // 2, CHUNK, 128), lambda b, i: (0, 0, 0)),
                  pl.BlockSpec((1, D_GMLP), lambda b, i: (0, 0))],
        out_specs=pl.BlockSpec((1, CHUNK, D_GMLP), lambda b, i: (b, i, 0)),
        out_shape=jax.ShapeDtypeStruct((bsz, seq, D_GMLP), BF16),
        compiler_params=_cparams(),
    )(z3, z3, ws, bexp, g_out)


def _gmlp_bwd(z3, dyn3, ws, wst, bexp, g_out, *, name, dy_col):
    bsz, seq, _ = z3.shape
    nc = seq // CHUNK
    npair = GROUPS // 2

    def body(u_ref, v_ref, dy_ref, ws_ref, wst_ref, b_ref, g_ref, duv_ref, dws_ref, dbs_ref, dg_ref, dbacc):
        first = jnp.logical_and(pl.program_id(0) == 0, pl.program_id(1) == 0)
        last = jnp.logical_and(pl.program_id(0) == bsz - 1, pl.program_id(1) == nc - 1)

        @pl.when(first)
        def _():
            dws_ref[...] = jnp.zeros_like(dws_ref)
            dg_ref[...] = jnp.zeros_like(dg_ref)
            dbacc[...] = jnp.zeros_like(dbacc)

        lo = _iota((CHUNK, 128), 1) < GROUP_DIM
        tril = _iota((CHUNK, CHUNK), 1) <= _iota((CHUNK, CHUNK), 0)
        u = u_ref[0].astype(F32)
        v = v_ref[0].astype(F32)
        gu, dgu = _gelu_and_grad(u)
        gv, dgv_dv = _gelu_and_grad(v)
        fwd = []
        for p in range(npair):
            sl = slice(128 * p, 128 * p + 128)
            w0 = _tril_bf16(ws_ref[2 * p])
            w1 = _tril_bf16(ws_ref[2 * p + 1])
            fwd.append(_gmlp_pair_fwd(gv[:, sl], w0, w1, b_ref[p], lo))
        yg = jnp.concatenate([gu[:, 128 * p:128 * p + 128] * fwd[p][3] for p in range(npair)], axis=1)
        dyg, dg = _rms_bwd(yg, g_ref[...], dy_ref[0].astype(F32), D_GMLP)
        dg_ref[...] += dg
        du_parts, dv_parts = [], []
        for p in range(npair):
            sl = slice(128 * p, 128 * p + 128)
            vn, vnb, rstd, mixed = fwd[p]
            dyg_p = dyg[:, sl]
            dmixed = dyg_p * gu[:, sl]
            dbacc[p] += dmixed
            dm_lo = jnp.where(lo, dmixed, 0.0).astype(BF16)
            dm_hi = jnp.where(lo, 0.0, dmixed).astype(BF16)
            dws_ref[2 * p] += jnp.where(tril, _dot(dm_lo, vnb, NT), 0.0)
            dws_ref[2 * p + 1] += jnp.where(tril, _dot(dm_hi, vnb, NT), 0.0)
            dmb = dmixed.astype(BF16)
            dvn = jnp.where(lo, _dot(wst_ref[2 * p], dmb), _dot(wst_ref[2 * p + 1], dmb))
            dgv = rstd * (dvn - _pair_mean_exact(dvn, lo) - vn * _pair_mean_exact(dvn * vn, lo))
            dv_parts.append(dgv * dgv_dv[:, sl])
            du_parts.append(dyg_p * mixed * dgu[:, sl])
        duv_ref[0] = jnp.concatenate(du_parts + dv_parts, axis=1).astype(BF16)

        @pl.when(last)
        def _():
            sel = jnp.where(_iota((8, 128), 0) == 0, (_iota((8, 128), 1) < GROUP_DIM).astype(F32),
                            jnp.where(_iota((8, 128), 0) == 1, (_iota((8, 128), 1) >= GROUP_DIM).astype(F32), 0.0))
            for p in range(npair):
                dbs_ref[p] = lax.dot_general(sel, dbacc[p], NT, precision=lax.Precision.HIGHEST,
                                             preferred_element_type=F32)

    duv, dws, dbs, dg = pl.pallas_call(
        body, name=name, grid=(bsz, nc),
        in_specs=[pl.BlockSpec((1, CHUNK, D_GMLP), lambda b, i: (b, i, 0)),
                  pl.BlockSpec((1, CHUNK, D_GMLP), lambda b, i: (b, i, 1)),
                  pl.BlockSpec((1, CHUNK, D_GMLP), lambda b, i: (b, i, dy_col)),
                  pl.BlockSpec((GROUPS, CHUNK, CHUNK), lambda b, i: (0, 0, 0)),
                  pl.BlockSpec((GROUPS, CHUNK, CHUNK), lambda b, i: (0, 0, 0)),
                  pl.BlockSpec((npair, CHUNK, 128), lambda b, i: (0, 0, 0)),
                  pl.BlockSpec((1, D_GMLP), lambda b, i: (0, 0))],
        out_specs=[pl.BlockSpec((1, CHUNK, 2 * D_GMLP), lambda b, i: (b, i, 0)),
                   pl.BlockSpec((GROUPS, CHUNK, CHUNK), lambda b, i: (0, 0, 0)),
                   pl.BlockSpec((npair, 8, CHUNK), lambda b, i: (0, 0, 0)),
                   pl.BlockSpec((1, D_GMLP), lambda b, i: (0, 0))],
        out_shape=[jax.ShapeDtypeStruct((bsz, seq, D_IN_PAD), BF16),
                   jax.ShapeDtypeStruct((GROUPS, CHUNK, CHUNK), F32),
                   jax.ShapeDtypeStruct((npair, 8, CHUNK), F32),
                   jax.ShapeDtypeStruct((1, D_GMLP), F32)],
        scratch_shapes=[pltpu.VMEM((npair, CHUNK, 128), F32)],
        compiler_params=_cparams(),
    )(z3, z3, dyn3, ws, wst, bexp, g_out)
    return duv, dws, dbs[:, :2, :].reshape(GROUPS, CHUNK), dg


def _partner(x):
    width = x.shape[-1]
    lane = _iota(x.shape, x.ndim - 1) % HEAD_PAD
    up = pltpu.roll(x, width - ROPE // 2, x.ndim - 1)
    down = pltpu.roll(x, ROPE // 2, x.ndim - 1)
    first = jnp.logical_and(lane >= NOPE, lane < NOPE + ROPE // 2)
    second = jnp.logical_and(lane >= NOPE + ROPE // 2, lane < NOPE + ROPE)
    return jnp.where(first, up, jnp.where(second, down, 0.0))


def _mla_prep_fwd(z3, g_q, g_kv, w_uq, w_ukv, ctab, stab, *, name, tb=256):
    bsz, seq, _ = z3.shape
    tb = min(tb, seq)
    hw = HEADS * HEAD_PAD

    def body(ql_ref, kvl_ref, krl_ref, gq_ref, gkv_ref, wuq_ref, wukv_ref, c_ref, s_ref, q_ref, kv_ref, kp_ref):
        cq = _rms_fwd(ql_ref[0].astype(F32), gq_ref[...], Q_RANK).astype(BF16)
        q = _dot(cq, wuq_ref[...])
        c1, s1 = c_ref[0], s_ref[0]
        c8, s8 = jnp.tile(c1, (1, HEADS)), jnp.tile(s1, (1, HEADS))
        q_ref[0] = ((q * c8 + _partner(q) * s8) * SCALE_LOG2).astype(BF16)
        ckv = _rms_fwd(kvl_ref[0].astype(F32), gkv_ref[...], KV_RANK).astype(BF16)
        kv = _dot(ckv, wukv_ref[...])
        kv_ref[0] = kv.astype(BF16)
        kr = krl_ref[0].astype(F32)
        kr = kr * c1 + _partner(kr) * s1
        lane = _iota((tb, hw), 1) % HEAD_PAD
        kp_ref[0] = jnp.where(lane < NOPE, kv, jnp.tile(kr, (1, HEADS))).astype(BF16)

    return pl.pallas_call(
        body, name=name, grid=(bsz, seq // tb),
        in_specs=[pl.BlockSpec((1, tb, Q_RANK), lambda b, i: (b, i, 4)),
                  pl.BlockSpec((1, tb, KV_RANK), lambda b, i: (b, i, 10)),
                  pl.BlockSpec((1, tb, HEAD_PAD), lambda b, i: (b, i, 11)),
                  pl.BlockSpec((1, Q_RANK), lambda b, i: (0, 0)),
                  pl.BlockSpec((1, KV_RANK), lambda b, i: (0, 0)),
                  pl.BlockSpec((Q_RANK, hw), lambda b, i: (0, 0)),
                  pl.BlockSpec((KV_RANK, hw), lambda b, i: (0, 0)),
                  pl.BlockSpec((1, tb, HEAD_PAD), lambda b, i: (b, i, 0)),
                  pl.BlockSpec((1, tb, HEAD_PAD), lambda b, i: (b, i, 0))],
        out_specs=[pl.BlockSpec((1, tb, hw), lambda b, i: (b, i, 0))] * 3,
        out_shape=[jax.ShapeDtypeStruct((bsz, seq, hw), BF16)] * 3,
        compiler_params=_cparams(),
    )(z3, z3, z3, g_q, g_kv, w_uq, w_ukv, ctab, stab)


def _mla_prep_bwd(z3, dz3, dq3, dk3, dv3, g_q, g_kv, w_uq, w_ukv, ctab, stab, *, name, tb=256):
    bsz, seq, _ = z3.shape
    tb = min(tb, seq)
    hw = HEADS * HEAD_PAD
    nb = seq // tb

    def body(ql_ref, kvl_ref, dq_ref, dk_ref, dv_ref, gq_ref, gkv_ref, wuq_ref, wukv_ref, c_ref, s_ref, dz_in,
             dz_ref, cq_ref, dqb_ref, ckv_ref, dkvb_ref, dgq_ref, dgkv_ref):
        @pl.when(jnp.logical_and(pl.program_id(0) == 0, pl.program_id(1) == 0))
        def _():
            dgq_ref[...] = jnp.zeros_like(dgq_ref)
            dgkv_ref[...] = jnp.zeros_like(dgkv_ref)

        c1, s1 = c_ref[0], s_ref[0]
        c8, s8 = jnp.tile(c1, (1, HEADS)), jnp.tile(s1, (1, HEADS))
        dqr = dq_ref[0]
        dqb = (dqr * c8 + _partner(dqr * s8)).astype(BF16)
        dqb_ref[0] = dqb
        ql = ql_ref[0].astype(F32)
        cq_ref[0] = _rms_fwd(ql, gq_ref[...], Q_RANK).astype(BF16)
        dql, dgq = _rms_bwd(ql, gq_ref[...], _dot(dqb, wuq_ref[...], NT), Q_RANK)
        dgq_ref[...] += dgq

        dk = dk_ref[0]
        lane = _iota((tb, hw), 1) % HEAD_PAD
        dkvb = jnp.where(lane < NOPE, dk, dv_ref[0]).astype(BF16)
        dkvb_ref[0] = dkvb
        kvl = kvl_ref[0].astype(F32)
        ckv_ref[0] = _rms_fwd(kvl, gkv_ref[...], KV_RANK).astype(BF16)
        dkvl, dgkv = _rms_bwd(kvl, gkv_ref[...], _dot(dkvb, wukv_ref[...], NT), KV_RANK)
        dgkv_ref[...] += dgkv

        dkr = dk[:, 0:HEAD_PAD].astype(F32)
        for h in range(1, HEADS):
            dkr = dkr + dk[:, HEAD_PAD * h:HEAD_PAD * (h + 1)].astype(F32)
        lane1 = _iota((tb, HEAD_PAD), 1)
        dkr = jnp.where(jnp.logical_and(lane1 >= NOPE, lane1 < NOPE + ROPE), dkr, 0.0)
        dkrl = dkr * c1 + _partner(dkr * s1)
        dz_ref[0] = jnp.concatenate([dql, dkvl, dkrl], axis=1).astype(BF16)

    return pl.pallas_call(
        body, name=name, grid=(bsz, nb),
        in_specs=[pl.BlockSpec((1, tb, Q_RANK), lambda b, i: (b, i, 4)),
                  pl.BlockSpec((1, tb, KV_RANK), lambda b, i: (b, i, 10)),
                  pl.BlockSpec((1, tb, hw), lambda b, i: (b, i, 0)),
                  pl.BlockSpec((1, tb, hw), lambda b, i: (b, i, 0)),
                  pl.BlockSpec((1, tb, hw), lambda b, i: (b, i, 0)),
                  pl.BlockSpec((1, Q_RANK), lambda b, i: (0, 0)),
                  pl.BlockSpec((1, KV_RANK), lambda b, i: (0, 0)),
                  pl.BlockSpec((Q_RANK, hw), lambda b, i: (0, 0)),
                  pl.BlockSpec((KV_RANK, hw), lambda b, i: (0, 0)),
                  pl.BlockSpec((1, tb, HEAD_PAD), lambda b, i: (b, i, 0)),
                  pl.BlockSpec((1, tb, HEAD_PAD), lambda b, i: (b, i, 0)),
                  ANY_SPEC],
        out_specs=[pl.BlockSpec((1, tb, 512), lambda b, i: (b, i, 2)),
                   pl.BlockSpec((1, tb, Q_RANK), lambda b, i: (b, i, 0)),
                   pl.BlockSpec((1, tb, hw), lambda b, i: (b, i, 0)),
                   pl.BlockSpec((1, tb, KV_RANK), lambda b, i: (b, i, 0)),
                   pl.BlockSpec((1, tb, hw), lambda b, i: (b, i, 0)),
                   pl.BlockSpec((1, Q_RANK), lambda b, i: (0, 0)),
                   pl.BlockSpec((1, KV_RANK), lambda b, i: (0, 0))],
        out_shape=[jax.ShapeDtypeStruct((bsz, seq, D_IN_PAD), BF16),
                   jax.ShapeDtypeStruct((bsz, seq, Q_RANK), BF16),
                   jax.ShapeDtypeStruct((bsz, seq, hw), BF16),
                   jax.ShapeDtypeStruct((bsz, seq, KV_RANK), BF16),
                   jax.ShapeDtypeStruct((bsz, seq, hw), BF16),
                   jax.ShapeDtypeStruct((1, Q_RANK), F32),
                   jax.ShapeDtypeStruct((1, KV_RANK), F32)],
        input_output_aliases={11: 0},
        compiler_params=_cparams(),
    )(z3, z3, dq3, dk3, dv3, g_q, g_kv, w_uq, w_ukv, ctab, stab, dz3)


ATTN_HEADS_PER_STEP = 4


def _attn_specs(tq, seq, hp):
    blk = pl.BlockSpec((1, tq, hp * HEAD_PAD), lambda b, h, i: (b, i, h))
    full = pl.BlockSpec((1, seq, hp * HEAD_PAD), lambda b, h, i: (b, 0, h))
    return blk, full


def _head(h):
    return slice(HEAD_PAD * h, HEAD_PAD * (h + 1))


def _attn_fwd(q3, kv3, kp3, *, name, tq=512, hp=ATTN_HEADS_PER_STEP, side=None):
    bsz, seq, hw = q3.shape
    tq = min(tq, seq)
    blk, full = _attn_specs(tq, seq, hp)

    def body(q_ref, kv_ref, kp_ref, o_ref, lse_ref):
        i = pl.program_id(2)

        def update(state, q, kp, kv, mask=None):
            m, l, acc = state
            s = _dot(q, kp, NT)
            if mask is not None:
                s = jnp.where(mask, s, -1e30)
            m_new = jnp.maximum(m, jnp.max(s, axis=1, keepdims=True))
            alpha = jnp.exp2(m - m_new)
            p = jnp.exp2(s - m_new)
            return m_new, alpha * l + jnp.sum(p, axis=1, keepdims=True), alpha * acc + _dot(p.astype(BF16), kv)

        def step(j, carry):
            st = pl.multiple_of(j * tq, tq)
            return tuple(update(carry[h], q_ref[0, :, _head(h)], kp_ref[0, pl.ds(st, tq), _head(h)],
                                kv_ref[0, pl.ds(st, tq), _head(h)]) for h in range(hp))

        init = tuple((jnp.full((tq, 1), -1e30, F32), jnp.zeros((tq, 1), F32), jnp.zeros((tq, HEAD_PAD), F32))
                     for _ in range(hp))
        carry = lax.fori_loop(0, i, step, init)

        st = pl.multiple_of(i * tq, tq)
        is_nope = _iota((tq, HEAD_PAD), 1) < NOPE
        causal = _iota((tq, tq), 1) <= _iota((tq, tq), 0)
        for h in range(hp):
            m, l, acc = update(carry[h], q_ref[0, :, _head(h)], kp_ref[0, pl.ds(st, tq), _head(h)],
                               kv_ref[0, pl.ds(st, tq), _head(h)], causal)
            o_ref[0, :, _head(h)] = jnp.where(is_nope, 0.0, acc / l).astype(BF16)
            lse_ref[0, :, _head(h)] = jnp.broadcast_to(m + jnp.log(l) * LOG2E, (tq, HEAD_PAD))

    outs, side_outs = _hosted_call(
        body, name=name, grid=(bsz, HEADS // hp, seq // tq),
        in_specs=[blk, full, full],
        out_specs=[blk, blk],
        out_shape=[jax.ShapeDtypeStruct((bsz, seq, hw), BF16), jax.ShapeDtypeStruct((bsz, seq, hw), F32)],
        args=(q3, kv3, kp3), side=side)
    return outs if side is None else (outs, side_outs)


def _attn_bwd(q3, kv3, kp3, do3, lse3, dl3, *, name, tq=512, hp=ATTN_HEADS_PER_STEP, side=None):
    bsz, seq, hw = q3.shape
    tq = min(tq, seq)
    nq = seq // tq
    blk, full = _attn_specs(tq, seq, hp)

    def body(kv_ref, kp_ref, q_ref, do_ref, lse_ref, dl_ref, dq_ref, dk_ref, dv_ref):
        j = pl.program_id(2)

        @pl.when(j == 0)
        def _():
            dq_ref[...] = jnp.zeros_like(dq_ref)

        def pair(h, row0, nrows, nkeys, mask=None):
            row0 = pl.multiple_of(row0, nrows)
            qi = q_ref[0, pl.ds(row0, nrows), _head(h)]
            do = do_ref[0, pl.ds(row0, nrows), _head(h)]
            kp = kp_ref[0, :nkeys, _head(h)]
            s = _dot(qi, kp, NT)
            if mask is not None:
                s = jnp.where(mask, s, -1e30)
            wide = nkeys // HEAD_PAD
            p = jnp.exp2(s - jnp.tile(lse_ref[0, pl.ds(row0, nrows), _head(h)], (1, wide)))
            dv = _dot(p.astype(BF16), do, TN)
            dp = _dot(do, kv_ref[0, :nkeys, _head(h)], NT)
            ds = (p * (dp - jnp.tile(dl_ref[0, pl.ds(row0, nrows), _head(h)], (1, wide)))).astype(BF16)
            dq_ref[0, pl.ds(row0, nrows), _head(h)] += _dot(ds, kp)
            return _dot(ds, qi, TN), dv

        def step(i, carry):
            st = pl.multiple_of(i * tq, tq)
            out = []
            for h in range(hp):
                dk, dv = pair(h, st, tq, tq)
                out.append((carry[h][0] + dk, carry[h][1] + dv))
            return tuple(out)

        causal = _iota((tq, tq), 1) <= _iota((tq, tq), 0)
        carry = tuple(pair(h, pl.multiple_of(j * tq, tq), tq, tq, causal) for h in range(hp))
        carry = lax.fori_loop(j + 1, nq, step, carry)
        for h in range(hp):
            dk_ref[0, :, _head(h)] = (carry[h][0] * (1.0 / LOG2E)).astype(BF16)
            dv_ref[0, :, _head(h)] = carry[h][1].astype(BF16)

        @pl.when(j == nq - 1)
        def _():
            dq_ref[...] = dq_ref[...] * ATTN_SCALE

    outs, side_outs = _hosted_call(
        body, name=name, grid=(bsz, HEADS // hp, nq),
        in_specs=[blk, blk, full, full, full, full],
        out_specs=[full, blk, blk],
        out_shape=[jax.ShapeDtypeStruct((bsz, seq, hw), F32)] + [jax.ShapeDtypeStruct((bsz, seq, hw), BF16)] * 2,
        args=(kv3, kp3, q3, do3, lse3, dl3), side=side)
    return outs if side is None else (outs, side_outs)


def _onorm_fwd(o3, yg3, g_pad, *, name, tb=512):
    bsz, seq, hw = o3.shape
    wg = yg3.shape[-1]
    tb = min(tb, seq)

    def body(o_ref, yg_ref, g_ref, y_ref):
        ya = _rms_fwd(o_ref[0].astype(F32), g_ref[...], HEADS * 64).astype(BF16)
        y_ref[0] = jnp.concatenate([ya, yg_ref[0]], axis=1)

    return pl.pallas_call(
        body, name=name, grid=(bsz, seq // tb),
        in_specs=[pl.BlockSpec((1, tb, hw), lambda b, i: (b, i, 0)),
                  pl.BlockSpec((1, tb, wg), lambda b, i: (b, i, 0)),
                  pl.BlockSpec((1, hw), lambda b, i: (0, 0))],
        out_specs=pl.BlockSpec((1, tb, hw + wg), lambda b, i: (b, i, 0)),
        out_shape=jax.ShapeDtypeStruct((bsz, seq, hw + wg), BF16),
        compiler_params=_cparams(),
    )(o3, yg3, g_pad)


def _onorm_bwd(o3, dy3, g_pad, *, name, tb=512):
    bsz, seq, hw = o3.shape
    tb = min(tb, seq)

    def body(o_ref, dy_ref, g_ref, do_ref, dl_ref, dg_ref):
        @pl.when(jnp.logical_and(pl.program_id(0) == 0, pl.program_id(1) == 0))
        def _():
            dg_ref[...] = jnp.zeros_like(dg_ref)

        o = o_ref[0].astype(F32)
        do, dg = _rms_bwd(o, g_ref[...], dy_ref[0].astype(F32), HEADS * 64)
        dg_ref[...] += dg
        do_ref[0] = do.astype(BF16)
        prod = do * o
        parts = []
        for h in range(HEADS):
            sh = jnp.sum(prod[:, HEAD_PAD * h:HEAD_PAD * (h + 1)], axis=1, keepdims=True)
            parts.append(jnp.broadcast_to(sh, (tb, HEAD_PAD)))
        dl_ref[0] = jnp.concatenate(parts, axis=1)

    return pl.pallas_call(
        body, name=name, grid=(bsz, seq // tb),
        in_specs=[pl.BlockSpec((1, tb, hw), lambda b, i: (b, i, 0)),
                  pl.BlockSpec((1, tb, hw), lambda b, i: (b, i, 0)),
                  pl.BlockSpec((1, hw), lambda b, i: (0, 0))],
        out_specs=[pl.BlockSpec((1, tb, hw), lambda b, i: (b, i, 0)),
                   pl.BlockSpec((1, tb, hw), lambda b, i: (b, i, 0)),
                   pl.BlockSpec((1, hw), lambda b, i: (0, 0))],
        out_shape=[jax.ShapeDtypeStruct((bsz, seq, hw), BF16),
                   jax.ShapeDtypeStruct((bsz, seq, hw), F32),
                   jax.ShapeDtypeStruct((1, hw), F32)],
        compiler_params=_cparams(),
    )(o3, dy3, g_pad)


def _resnode_bwd(x3, g, *, name, target3=None, dh3=None, dres3=None, mod_nm=None, rows=None,
                 branch3=None, mod_gate=None, gate_row=None, tb=512, side=None):
    bsz, seq, d = x3.shape
    tb = min(tb, seq)
    final = target3 is not None
    has_branch = branch3 is not None
    row_spec = pl.BlockSpec((1, tb, d), lambda b, i: (b, i, 0))
    vec_spec = pl.BlockSpec((1, d), lambda b, i: (0, 0))
    mod_spec = pl.BlockSpec((1, MOD_ROWS, d), lambda b, i: (b, 0, 0))

    ins, in_specs = [x3, g], [row_spec, vec_spec]
    if final:
        ins += [target3]
        in_specs += [row_spec]
    else:
        ins += [dh3, dres3, mod_nm]
        in_specs += [row_spec, row_spec, mod_spec]
    if has_branch:
        ins += [branch3, mod_gate]
        in_specs += [row_spec, mod_spec]

    out_names = ["dx", "dg"]
    out_specs = [row_spec, vec_spec]
    out_shape = [jax.ShapeDtypeStruct((bsz, seq, d), F32), jax.ShapeDtypeStruct((1, d), F32)]
    if final:
        out_names += ["loss"]
        out_specs += [pl.BlockSpec((1, 128), lambda b, i: (0, 0))]
        out_shape += [jax.ShapeDtypeStruct((1, 128), F32)]
    else:
        out_names += ["dnm"]
        out_specs += [mod_spec]
        out_shape += [jax.ShapeDtypeStruct((bsz, MOD_ROWS, d), F32)]
    if has_branch:
        out_names += ["dbr", "dgate"]
        out_specs += [row_spec, mod_spec]
        out_shape += [jax.ShapeDtypeStruct((bsz, seq, d), BF16), jax.ShapeDtypeStruct((bsz, MOD_ROWS, d), F32)]
    n_in = len(ins)

    def body(*refs):
        r = dict(zip(["x", "g"] + (["t"] if final else ["dh", "dres", "nm"]) + (["br", "gm"] if has_branch else []),
                     refs[:n_in]))
        o = dict(zip(out_names, refs[n_in:]))
        b_first = pl.program_id(1) == 0
        first = jnp.logical_and(pl.program_id(0) == 0, b_first)
        rowid = _iota((MOD_ROWS, d), 0)

        @pl.when(first)
        def _():
            o["dg"][...] = jnp.zeros_like(o["dg"])
            if final:
                o["loss"][...] = jnp.zeros_like(o["loss"])

        @pl.when(b_first)
        def _():
            if not final:
                o["dnm"][...] = jnp.zeros_like(o["dnm"])
            if has_branch:
                o["dgate"][...] = jnp.zeros_like(o["dgate"])

        x = r["x"][0]
        gv = r["g"][...]
        if final:
            e = _rms_fwd(x, gv, d) - r["t"][0]
            sq = jnp.sum(jnp.sum(e * e, axis=1, keepdims=True), axis=0, keepdims=True)
            o["loss"][...] += jnp.broadcast_to(sq * (0.5 / d), (1, 128))
            dx, dg = _rms_bwd(x, gv, e * (1.0 / d), d)
        else:
            m = r["nm"][0]
            dh = r["dh"][0].astype(F32)
            scale = m[rows[1]:rows[1] + 1, :]
            rstd = lax.rsqrt(jnp.sum(x * x, axis=-1, keepdims=True) * (1.0 / d) + EPS)
            xh = x * rstd
            nrm = xh * gv
            dshift = jnp.sum(dh, axis=0, keepdims=True)
            dscale = jnp.sum(dh * nrm, axis=0, keepdims=True)
            o["dnm"][0] += jnp.where(rowid == 0, dshift, jnp.where(rowid == 1, dscale, 0.0))
            dn = dh * (1.0 + scale)
            dg = jnp.sum(dn * xh, axis=0, keepdims=True)
            dxh = dn * gv
            dx = rstd * (dxh - xh * (jnp.sum(dxh * xh, axis=-1, keepdims=True) * (1.0 / d))) + r["dres"][0]
        o["dg"][...] += dg
        o["dx"][0] = dx
        if has_branch:
            gate = r["gm"][0][gate_row:gate_row + 1, :]
            o["dbr"][0] = (gate * dx).astype(BF16)
            dgate = jnp.sum(dx * r["br"][0], axis=0, keepdims=True)
            o["dgate"][0] += jnp.where(rowid == 0, dgate, 0.0)

    outs, side_outs = _hosted_call(
        body, name=name, grid=(bsz, seq // tb),
        in_specs=in_specs, out_specs=out_specs, out_shape=out_shape, args=tuple(ins), side=side)
    res = dict(zip(out_names, outs))
    return res if side is None else (res, side_outs)


def _adamw(w, g, m, v, *, name):
    shape = w.shape
    cols = shape[-1]
    rows = w.size // cols
    tr = _pick_rows(rows, max(8, (256 * 1024) // cols // 8 * 8))

    def body(w_ref, g_ref, m_ref, v_ref, d_ref, nm_ref, nv_ref):
        d_ref[...], nm_ref[...], nv_ref[...] = _adamw_math(w_ref[...], g_ref[...], m_ref[...], v_ref[...])

    if w.ndim == 3 and shape[1] % 8 == 0:
        tr3 = _pick_rows(shape[1], max(8, (256 * 1024) // cols // 8 * 8))
        spec3 = pl.BlockSpec((None, tr3, cols), lambda l, i: (l, i, 0))
        return tuple(pl.pallas_call(
            body, name=name, grid=(shape[0], shape[1] // tr3),
            in_specs=[spec3] * 4, out_specs=[spec3] * 3,
            out_shape=[jax.ShapeDtypeStruct(shape, F32)] * 3,
            compiler_params=_cparams(),
        )(w, g, m, v))
    spec = pl.BlockSpec((tr, cols), lambda i: (i, 0))
    outs = pl.pallas_call(
        body, name=name, grid=(rows // tr,),
        in_specs=[spec] * 4, out_specs=[spec] * 3,
        out_shape=[jax.ShapeDtypeStruct((rows, cols), F32)] * 3,
        compiler_params=_cparams(),
    )(*[t.reshape(rows, cols) for t in (w, g, m, v)])
    return tuple(o.reshape(shape) for o in outs)


def _adamw_math(w, g, m, v):
    c1 = 1.0 - ADAM_B1 ** ADAM_STEP
    c2 = 1.0 - ADAM_B2 ** ADAM_STEP
    nm = ADAM_B1 * m + (1.0 - ADAM_B1) * g
    nv = ADAM_B2 * v + (1.0 - ADAM_B2) * (g * g)
    delta = -ADAM_LR * ((nm / c1) / (jnp.sqrt(nv / c2) + ADAM_EPS) + ADAM_WD * w)
    return delta, nm, nv


def _adamw_layers(w, m, v, bufs, row_off, *, name, tr=256):
    depth, rows, cols = w.shape
    tr = min(tr, rows)
    assert rows % tr == 0 and row_off % tr == 0

    outs = None
    for l in range(depth):
        def body(w_ref, g_ref, m_ref, v_ref, *rest):
            go_ref, d_ref, nm_ref, nv_ref = rest[-4:]
            g = g_ref[...]
            go_ref[...] = g
            d_ref[...], nm_ref[...], nv_ref[...] = _adamw_math(w_ref[...], g, m_ref[...], v_ref[...])

        layer = pl.BlockSpec((None, tr, cols), lambda i, l=l: (l, i, 0))
        prev = () if outs is None else tuple(outs)
        outs = pl.pallas_call(
            body, name=f"{name}_l{l}", grid=(rows // tr,),
            in_specs=[layer, pl.BlockSpec((tr, cols), lambda i: (row_off // tr + i, 0)), layer, layer]
            + [ANY_SPEC] * len(prev),
            out_specs=[layer] * 4,
            out_shape=[jax.ShapeDtypeStruct(w.shape, F32)] * 4,
            input_output_aliases={4 + k: k for k in range(len(prev))},
            compiler_params=_cparams(),
        )(w, bufs[l], m, v, *prev)
    return tuple(outs)


def _sum_leading(x, *, name, tr=256):
    n, rows, cols = x.shape
    tr = _pick_rows(rows, tr)

    def body(x_ref, o_ref):
        acc = x_ref[0]
        for k in range(1, n):
            acc = acc + x_ref[k]
        o_ref[...] = acc

    return pl.pallas_call(
        body, name=name, grid=(rows // tr,),
        in_specs=[pl.BlockSpec((n, tr, cols), lambda i: (0, i, 0))],
        out_specs=pl.BlockSpec((tr, cols), lambda i: (i, 0)),
        out_shape=jax.ShapeDtypeStruct((rows, cols), F32),
        compiler_params=_cparams(),
    )(x)


def _position():
    return lax.axis_index("x"), lax.axis_index("y"), lax.axis_index("c")


def _allgather8(x, *, name):
    shape = x.shape

    def body(x_ref, out_ref, send_sems, recv_sems, local_sem):
        px, py, pc = _position()
        me, sibling = (px, py, pc), (px, py, 1 - pc)
        chips = [(1 - px, py), (px, 1 - py), (1 - px, 1 - py)]
        src_own = x_ref

        def slot(qx, qy, qc):
            return out_ref.at[4 * qx + 2 * qy + qc]

        def copy(k, block, to, src=None):
            return pltpu.make_async_remote_copy(
                src_ref=slot(*block) if src is None else src, dst_ref=slot(*block),
                send_sem=send_sems.at[k], recv_sem=recv_sems.at[k], device_id=to, device_id_type=MESH)

        mine = pltpu.make_async_copy(src_own, slot(*me), local_sem)
        mine.start()
        first = [copy(0, me, sibling, src=src_own)]
        first += [copy(1 + j, me, (*chip, pc), src=src_own) for j, chip in enumerate(chips)]
        for cp in first:
            cp.start()
        passed = [copy(4 + j, (*chip, pc), sibling) for j, chip in enumerate(chips)]
        for j, chip in enumerate(chips):
            copy(1 + j, (*chip, pc), me).wait_recv()
            passed[j].start()
        copy(0, sibling, me).wait_recv()
        for j, chip in enumerate(chips):
            copy(4 + j, (*chip, 1 - pc), me).wait_recv()
        for cp in first + passed:
            cp.wait_send()
        mine.wait()

    return pl.pallas_call(
        body, name=name,
        out_shape=jax.ShapeDtypeStruct((N_DEV,) + shape, x.dtype),
        in_specs=[pl.BlockSpec(memory_space=pl.ANY)],
        out_specs=pl.BlockSpec(memory_space=pl.ANY),
        scratch_shapes=[pltpu.SemaphoreType.DMA((7,)), pltpu.SemaphoreType.DMA((7,)), pltpu.SemaphoreType.DMA],
    )(x)


class _Exchange:
    def __init__(self, ins, out_shapes, n, build, aliases=None):
        self.ins, self.out_shapes, self.n, self.build = tuple(ins), tuple(out_shapes), n, build
        self.aliases = dict(aliases or {})

    def _descriptors(self, in_refs, out_refs, send_sems, recv_sems):
        sends, recvs = [], []
        for k, (src, dst, peer, landing) in enumerate(self.build(in_refs, out_refs)):
            sends.append(pltpu.make_async_remote_copy(
                src_ref=src, dst_ref=dst, send_sem=send_sems.at[k], recv_sem=recv_sems.at[k],
                device_id=peer, device_id_type=MESH))
            recvs.append(pltpu.make_async_remote_copy(
                src_ref=src, dst_ref=landing, send_sem=send_sems.at[k], recv_sem=recv_sems.at[k],
                device_id=peer, device_id_type=MESH))
        return sends, recvs

    def start(self, *refs):
        for cp in self._descriptors(*refs)[0]:
            cp.start()

    def finish(self, *refs):
        sends, recvs = self._descriptors(*refs)
        for cp in recvs:
            cp.wait_recv()
        for cp in sends:
            cp.wait_send()


ANY_SPEC = pl.BlockSpec(memory_space=pl.ANY)


def _hosted_call(body, *, name, grid, in_specs, out_specs, out_shape, args, scratch_shapes=(), side=None,
                 num_scalar_prefetch=0, io_aliases=None):
    in_specs, out_specs, out_shape = list(in_specs), list(out_specs), list(out_shape)
    n_in, n_out = len(in_specs) + num_scalar_prefetch, len(out_specs)
    kernel_body = body
    aliases = dict(io_aliases or {})
    if side is not None:
        s_in, s_out = len(side.ins), len(side.out_shapes)
        aliases.update({n_in + i: n_out + o for i, o in side.aliases.items()})

        def kernel_body(*refs):
            ins, s_ins = refs[:n_in], refs[n_in:n_in + s_in]
            outs = refs[n_in + s_in:n_in + s_in + n_out]
            s_outs = refs[n_in + s_in + n_out:n_in + s_in + n_out + s_out]
            scratch, sems = refs[n_in + s_in + n_out + s_out:-2], refs[-2:]
            first = functools.reduce(jnp.logical_and, [pl.program_id(a) == 0 for a in range(len(grid))])
            last = functools.reduce(jnp.logical_and, [pl.program_id(a) == g - 1 for a, g in enumerate(grid)])

            @pl.when(first)
            def _():
                side.start(s_ins, s_outs, *sems)

            body(*ins, *outs, *scratch)

            @pl.when(last)
            def _():
                side.finish(s_ins, s_outs, *sems)

        in_specs += [ANY_SPEC] * s_in
        out_specs += [ANY_SPEC] * s_out
        out_shape += list(side.out_shapes)
        scratch_shapes = list(scratch_shapes) + [pltpu.SemaphoreType.DMA((side.n,)),
                                                 pltpu.SemaphoreType.DMA((side.n,))]
        args = tuple(args) + side.ins
    if num_scalar_prefetch:
        grid_spec = pltpu.PrefetchScalarGridSpec(num_scalar_prefetch=num_scalar_prefetch, grid=grid,
                                                 in_specs=in_specs, out_specs=out_specs,
                                                 scratch_shapes=list(scratch_shapes))
        outs = pl.pallas_call(kernel_body, name=name, grid_spec=grid_spec, out_shape=out_shape,
                              input_output_aliases=aliases, compiler_params=_cparams())(*args)
    else:
        outs = pl.pallas_call(kernel_body, name=name, grid=grid, in_specs=in_specs, out_specs=out_specs,
                              out_shape=out_shape, scratch_shapes=list(scratch_shapes),
                              input_output_aliases=aliases, compiler_params=_cparams())(*args)
    return tuple(outs[:n_out]), tuple(outs[n_out:])


def _run_exchange(ex, *, name):
    s_in = len(ex.ins)

    def body(*refs):
        ins, outs, sems = refs[:s_in], refs[s_in:-2], refs[-2:]
        ex.start(ins, outs, *sems)
        ex.finish(ins, outs, *sems)

    outs = pl.pallas_call(
        body, name=name, out_shape=list(ex.out_shapes),
        in_specs=[ANY_SPEC] * s_in, out_specs=[ANY_SPEC] * len(ex.out_shapes),
        scratch_shapes=[pltpu.SemaphoreType.DMA((ex.n,)), pltpu.SemaphoreType.DMA((ex.n,))],
        input_output_aliases=ex.aliases,
    )(*ex.ins)
    return tuple(outs)


def _other_chips(px, py):
    return [(px, 1 - py), (1 - px, py), (1 - px, 1 - py)]


def _gather_spread(w_flat, halves=True):
    rows, w = w_flat.shape
    hr = rows // 2 if halves else rows

    def build(ins, outs):
        px, py, pc = _position()
        mine = ins[0].at[pl.ds(pc * hr, hr)] if halves else ins[0]
        me = 4 * px + 2 * py + pc
        plan = [((px, py, 1 - pc), me ^ 1)]
        plan += [((qx, qy, pc), 4 * qx + 2 * qy + pc) for qx, qy in _other_chips(px, py)]
        return [(mine, outs[0].at[me], peer, outs[0].at[their]) for peer, their in plan]

    return _Exchange([w_flat], [jax.ShapeDtypeStruct((N_DEV, hr, w), w_flat.dtype)], 4, build)


def _gather_pass_on(gath):
    def build(ins, outs):
        px, py, pc = _position()
        out = []
        for qx, qy in _other_chips(px, py):
            blk = 4 * qx + 2 * qy + pc
            out.append((outs[0].at[blk], outs[0].at[blk], (px, py, 1 - pc), outs[0].at[blk ^ 1]))
        return out

    return _Exchange([gath], [jax.ShapeDtypeStruct(gath.shape, gath.dtype)], 3, build, aliases={0: 0})


def _rs_halves(g):
    n, rows, w = g.shape
    hr = rows // 2

    def build(ins, outs):
        px, py, pc = _position()
        return [(ins[0].at[:, pl.ds((1 - pc) * hr, hr), :], outs[0], (px, py, 1 - pc), outs[0])]

    return _Exchange([g], [jax.ShapeDtypeStruct((n, hr, w), g.dtype)], 1, build)


def _rs_chips(sb):
    def build(ins, outs):
        px, py, pc = _position()
        return [(ins[0].at[j], outs[0].at[j], (qx, qy, pc), outs[0].at[j])
                for j, (qx, qy) in enumerate(_other_chips(px, py))]

    return _Exchange([sb], [jax.ShapeDtypeStruct(sb.shape, sb.dtype)], 3, build)


def _rs_complete(buf):
    def build(ins, outs):
        px, py, pc = _position()
        return [(outs[0].at[pc], outs[0].at[pc], (px, py, 1 - pc), outs[0].at[1 - pc])]

    return _Exchange([buf], [jax.ShapeDtypeStruct(buf.shape, buf.dtype)], 1, build, aliases={0: 0})


def _rs_partial(g, recv, ids, *, name, tr=128):
    _, rows, w = g.shape
    hr = rows // 2
    nb = hr // tr

    def body(ids_ref, g_ref, r_ref, o_ref):
        o_ref[0] = (g_ref[0] + r_ref[0]).astype(BF16)

    grid_spec = pltpu.PrefetchScalarGridSpec(
        num_scalar_prefetch=1, grid=(3, nb),
        in_specs=[pl.BlockSpec((1, tr, w), lambda j, i, ids: (ids[1] ^ (j + 1), ids[0] * nb + i, 0)),
                  pl.BlockSpec((1, tr, w), lambda j, i, ids: (ids[1] ^ (j + 1), i, 0))],
        out_specs=pl.BlockSpec((1, tr, w), lambda j, i, ids: (j, i, 0)))
    return pl.pallas_call(
        body, name=name, grid_spec=grid_spec,
        out_shape=jax.ShapeDtypeStruct((3, hr, w), BF16),
        compiler_params=_cparams(),
    )(ids, g, recv)


def _rs_total(g, recv, got, ids, *, name, tr=128):
    _, rows, w = g.shape
    hr = rows // 2
    nb = hr // tr

    def body(ids_ref, g_ref, r_ref, got_ref, o_ref):
        acc = g_ref[0] + r_ref[0]
        for j in range(3):
            acc = acc + got_ref[j].astype(F32)
        o_ref[0] = acc

    grid_spec = pltpu.PrefetchScalarGridSpec(
        num_scalar_prefetch=1, grid=(nb,),
        in_specs=[pl.BlockSpec((1, tr, w), lambda i, ids: (ids[1], ids[0] * nb + i, 0)),
                  pl.BlockSpec((1, tr, w), lambda i, ids: (ids[1], i, 0)),
                  pl.BlockSpec((3, tr, w), lambda i, ids: (0, i, 0))],
        out_specs=pl.BlockSpec((1, tr, w), lambda i, ids: (ids[0], i, 0)))
    return pl.pallas_call(
        body, name=name, grid_spec=grid_spec,
        out_shape=jax.ShapeDtypeStruct((2, hr, w), F32),
        compiler_params=_cparams(),
    )(ids, g, recv, got)


class _ReduceScatter:
    def __init__(self, g, ids, tag):
        self.g, self.ids, self.tag, self.stage, self.result = g, ids, tag, 0, None

    def next_exchange(self):
        if self.stage == 0:
            return _rs_halves(self.g)
        if self.stage == 1:
            return _rs_chips(self.sb)
        return _rs_complete(self.buf)

    def done(self, outs):
        if self.stage == 0:
            self.recv = outs[0]
            hr = self.recv.shape[1]
            self.tr = max(t for t in range(16, 513, 16) if hr % t == 0)
            self.sb = _rs_partial(self.g, self.recv, self.ids, name=f"{self.tag}_partial", tr=self.tr)
        elif self.stage == 1:
            self.buf = _rs_total(self.g, self.recv, outs[0], self.ids, name=f"{self.tag}_total", tr=self.tr)
        else:
            _, hr, w = outs[0].shape
            self.result = outs[0].reshape(2 * hr, w)
        self.stage += 1

    def finish_alone(self):
        names = ("halves", "chips", "complete")
        while self.stage < 3:
            self.done(_run_exchange(self.next_exchange(), name=f"{self.tag}_{names[self.stage]}"))
        return self.result


def _flat_rows():
    used = sum(r for _, r in FSDP_SECTIONS)
    return used, -(-used // ROW_ALIGN) * ROW_ALIGN


def _cols_to_chunks(full):
    rows, cols = full.shape
    t = full.reshape(rows, N_CHIPS, cols // N_CHIPS).transpose(1, 0, 2)
    return t.reshape(N_CHIPS, -1, FLAT_W)


def _chunks_to_cols(chunks, rows, cols):
    return chunks.reshape(N_CHIPS, rows, cols // N_CHIPS).transpose(1, 0, 2).reshape(rows, cols)


def _pad_heads(w, real):
    lead = w.shape[:-1]
    t = w.reshape(lead + (HEADS, real))
    t = jnp.pad(t, [(0, 0)] * len(lead) + [(0, 0), (0, HEAD_PAD - real)])
    return t.reshape(lead + (HEADS * HEAD_PAD,))


def _unpad_heads(w, real):
    lead = w.shape[:-1]
    return w.reshape(lead + (HEADS, HEAD_PAD))[..., :real].reshape(lead + (HEADS * real,))


def _pad_value_lanes(w, axis):
    w = jnp.moveaxis(w, axis, -1)
    lead = w.shape[:-1]
    t = w.reshape(lead + (HEADS, 64))
    t = jnp.pad(t, [(0, 0)] * len(lead) + [(0, 0), (HEAD_PAD - 64, 0)])
    return jnp.moveaxis(t.reshape(lead + (HEADS * HEAD_PAD,)), -1, axis)


def _unpad_value_lanes(w, axis):
    w = jnp.moveaxis(w, axis, -1)
    lead = w.shape[:-1]
    t = w.reshape(lead + (HEADS, HEAD_PAD))[..., HEAD_PAD - 64:]
    return jnp.moveaxis(t.reshape(lead + (HEADS * 64,)), -1, axis)


def _pad_w_in_t(wt):
    z = jnp.zeros((NOPE, wt.shape[1]), wt.dtype)
    z2 = jnp.zeros((HEAD_PAD - NOPE - ROPE, wt.shape[1]), wt.dtype)
    return jnp.concatenate([wt[:1408], z, wt[1408:], z2], axis=0)


def _unpad_w_in_t(wt):
    return jnp.concatenate([wt[:1408], wt[1408 + NOPE:1408 + NOPE + ROPE]], axis=0)


def _rope_tables(positions):
    freqs = ROPE_THETA ** (-jnp.arange(0, ROPE, 2, dtype=F32) / ROPE)
    ang = positions.astype(F32)[..., None] * freqs
    cos, sin = jnp.cos(ang), jnp.sin(ang)
    lead = cos.shape[:-1]
    ones = jnp.ones(lead + (NOPE,), F32)
    zeros_n = jnp.zeros(lead + (NOPE,), F32)
    zeros_p = jnp.zeros(lead + (HEAD_PAD - NOPE - ROPE,), F32)
    ctab = jnp.concatenate([ones, cos, cos, zeros_p], axis=-1)
    stab = jnp.concatenate([zeros_n, -sin, sin, zeros_p], axis=-1)
    return ctab, stab


def _mix_weights(full):
    return dict(
        w_in_t=_pad_w_in_t(full["w_in_t"]),
        w_uq=_pad_heads(full["mla_w_uq"], NOPE + ROPE),
        w_ukv=full["mla_w_ukv"],
        w_out=jnp.concatenate([_pad_value_lanes(full["w_out"][D_GMLP:], 0), full["w_out"][:D_GMLP]], axis=0),
    )


def _small_weights(p, l):
    ws = p["gmlp_ws"][l]
    tril = jnp.tril(jnp.ones((CHUNK, CHUNK), bool))
    bs = p["gmlp_bs"][l]
    bexp = jnp.repeat(bs.reshape(GROUPS // 2, 2, CHUNK).transpose(0, 2, 1), GROUP_DIM, axis=2)
    return dict(
        ws=ws,
        wst=jnp.where(tril[None], ws, 0.0).transpose(0, 2, 1).astype(BF16),
        bexp=bexp,
        g_mix=p["norm_mix_g"][l][None],
        g_ffn=p["norm_ffn_g"][l][None],
        g_q=p["mla_q_norm_g"][l][None],
        g_kv=p["mla_kv_norm_g"][l][None],
        g_og=p["out_norm_gmlp_g"][l][None],
        g_oa=_pad_value_lanes(p["out_norm_mla_g"][l], 0)[None],
    )


def _local_step(x3, target3, positions, mods, final_g, plan):
    bsz, seq, d = x3.shape
    tok = bsz * seq
    tmt = min(512, seq)
    tmk = min(1024, seq)
    tmw = min(2048, tok)
    chunk = (None, None, FLAT_W, FLAT_W)
    chunk2 = (2, None, FLAT_W, FLAT_W)
    ff_grad_shape = (N_CHIPS, 2 * FLAT_W, FLAT_W)
    ctab, stab = _rope_tables(positions)
    lw = [None] * DEPTH

    def flat(t):
        return t.reshape(tok, t.shape[-1])

    def cube(t):
        return t.reshape(bsz, seq, t.shape[-1])

    def carrying(l, tag, fn, *args, **kw):
        side = plan.host(l, tag)
        if side is None:
            return fn(*args, **kw)
        res, side_outs = fn(*args, side=side, **kw)
        plan.hosted(l, tag, side_outs)
        return res

    saved = []
    x = x3
    for l in range(DEPTH):
        lw[l] = plan.layer(l)
        w, mod = lw[l], mods[l]
        if l == 0:
            h1 = carrying(l, "fwd_normmod1", _normmod_fwd, x, w["g_mix"], mod, SHIFT1, SCALE1,
                          name=f"l{l}_normmod1")
        else:
            h1 = h1_next
        z = cube(_mm(flat(h1), w["w_in_t"], dims="nt", name=f"l{l}_w_in", tm=tmt, tn=D_IN_PAD, tk=d,
                     out_dtypes=(BF16,)))
        yg = _gmlp_fwd(z, w["ws"], w["bexp"], w["g_og"], name=f"l{l}_gmlp_fwd")
        q, kv, kp = _mla_prep_fwd(z, w["g_q"], w["g_kv"], w["w_uq"], w["w_ukv"], ctab, stab, name=f"l{l}_mla_prep")
        o, lse = carrying(l, "fwd_attn", _attn_fwd, q, kv, kp, name=f"l{l}_attn_fwd")
        y = _onorm_fwd(o, yg, w["g_oa"], name=f"l{l}_onorm_fwd")

        def normmod(xv, gv, gm, shift_row, scale_row):
            m = gm[0]
            return _rms_fwd(xv, gv, d) * (1.0 + m[scale_row:scale_row + 1, :]) + m[shift_row:shift_row + 1, :]

        def out_epi(po, xv, gm, gf):
            x_new = xv + gm[0][GATE1:GATE1 + 1, :] * po
            return po, x_new, normmod(x_new, gf, gm, SHIFT2, SCALE2)

        vec_spec = pl.BlockSpec((1, d), lambda i, j, k: (0, j))
        po, x_mid, h2 = carrying(l, "fwd_out_a", _mm, flat(y), w["w_out"], dims="nn", name=f"l{l}_w_out",
                                 tm=tmt, tn=d, tk=y.shape[-1], out_dtypes=(BF16, F32, BF16), epilogue=out_epi,
                                 extras=(flat(x), mod, w["g_ffn"]),
                                 extra_specs=(None, _mod_spec(tmt, d, seq), vec_spec))
        x_mid, h2 = cube(x_mid), cube(h2)

        def act_epi(acc):
            r = jnp.maximum(acc, 0.0)
            return (r * r,)

        r = carrying(l, "fwd_ff1", _mm, flat(h2), w["ff"], dims="nn", name=f"l{l}_w_ff1", tm=tmw, tn=FLAT_W,
                     tk=d, out_dtypes=(BF16,), epilogue=act_epi, weights_outer=True, n=D_FF,
                     b_block=(chunk, lambda i, j, k: (j, 0, 0, 0)))

        more = l + 1 < DEPTH

        def ff2_epi(acc, xv, gm, *nxt):
            x_new = xv + gm[0][GATE2:GATE2 + 1, :] * acc
            return (acc, x_new) + ((normmod(x_new, nxt[1], nxt[0], SHIFT1, SCALE1),) if more else ())

        mod_spec = _mod_spec(tmt, d, seq)
        outs = carrying(l, "fwd_ff2", _mm, r, w["ff"], dims="nn", name=f"l{l}_w_ff2", tm=tmt, tn=d, tk=2 * FLAT_W,
                        out_dtypes=(BF16, F32) + ((BF16,) if more else ()), epilogue=ff2_epi,
                        extras=(flat(x_mid), mod) + ((mods[l + 1], plan.layer(l + 1)["g_mix"]) if more else ()),
                        extra_specs=(None, mod_spec) + ((mod_spec, vec_spec) if more else ()), n=d,
                        b_block=(chunk2, lambda i, j, k: (k, 1, 0, 0)))
        f, x_out = outs[0], outs[1]
        h1_next = cube(outs[2]) if more else None
        saved.append(dict(x_in=x, h1=h1, z=z, q=q, kv=kv, kp=kp, o=o, lse=lse, y=y, po=cube(po),
                          x_mid=x_mid, h2=h2, r=r, f=cube(f)))
        x = cube(x_out)

    grads = [dict() for _ in range(DEPTH)]
    dmods = [None] * DEPTH
    top = DEPTH - 1
    node = _resnode_bwd(x, final_g[None], name="final_loss_bwd", target3=target3,
                        branch3=saved[top]["f"], mod_gate=mods[top], gate_row=GATE2)
    loss_part = node["loss"][0, 0]
    d_final_g = node["dg"][0]
    plan.scalars(loss_part, d_final_g)
    for l in range(DEPTH - 1, -1, -1):
        w, mod, s = lw[l], mods[l], saved[l]
        dx_out, dfb, dgate2 = node["dx"], flat(node["dbr"]), node["dgate"][:, 0]

        def dact_epi(acc, rv):
            return (acc * (2.0 * jnp.sqrt(rv.astype(F32))),)

        da = carrying(l, "bwd_d_r", _mm, dfb, w["ff"], dims="nt", name=f"l{l}_d_r", tm=tmw, tn=FLAT_W, tk=d,
                      out_dtypes=(BF16,), epilogue=dact_epi, extras=(s["r"],), weights_outer=True, n=D_FF,
                      b_block=(chunk, lambda i, j, k: (j, 1, 0, 0)))
        g_ff = carrying(l, "bwd_dw_ff2", _mm, s["r"], dfb, dims="tn", name=f"l{l}_dw_ff2", tm=FLAT_W, tn=d,
                        tk=2048, out_into=(ff_grad_shape, (None, FLAT_W, FLAT_W), lambda i, j, k: (i, 1, 0), None))
        g_ff = carrying(l, "bwd_dw_ff1", _mm, flat(s["h2"]), da, dims="tn", name=f"l{l}_dw_ff1", tm=d, tn=FLAT_W,
                        tk=2048, out_into=(ff_grad_shape, (None, FLAT_W, FLAT_W), lambda i, j, k: (j, 0, 0), g_ff))
        plan.ff_grads(l, g_ff)
        dh2 = carrying(l, "bwd_d_h2", _mm, da, w["ff"], dims="nt", name=f"l{l}_d_h2", tm=tmk, tn=d, tk=2 * FLAT_W,
                       n=d, b_block=(chunk2, lambda i, j, k: (k, 0, 0, 0)), out_dtypes=(BF16,))
        node = carrying(l, "bwd_resnode_ffn", _resnode_bwd, s["x_mid"], w["g_ffn"], name=f"l{l}_resnode_ffn",
                        dh3=cube(dh2), dres3=dx_out, mod_nm=mod, rows=(SHIFT2, SCALE2), branch3=s["po"],
                        mod_gate=mod, gate_row=GATE1)
        grads[l]["norm_ffn_g"] = node["dg"][0]
        dshift2, dscale2 = node["dnm"][:, 0], node["dnm"][:, 1]
        dx_mid, dpo, dgate1 = node["dx"], flat(node["dbr"]), node["dgate"][:, 0]

        wy = s["y"].shape[-1]
        dy = cube(carrying(l, "bwd_d_y", _mm, dpo, w["w_out"], dims="nt", name=f"l{l}_d_y", tm=tmt, tn=wy, tk=d,
                           out_dtypes=(BF16,)))
        dw_out = _mm(flat(s["y"]), dpo, dims="tn", name=f"l{l}_dw_out", tm=wy // 3, tn=d, tk=2048)
        hw = HEADS * HEAD_PAD
        grads[l]["w_out"] = jnp.concatenate([dw_out[hw:], _unpad_value_lanes(dw_out[:hw], 0)], axis=0)

        dz, dws, dbs, dg_og = _gmlp_bwd(s["z"], dy, w["ws"], w["wst"], w["bexp"], w["g_og"],
                                        name=f"l{l}_gmlp_bwd", dy_col=hw // D_GMLP)
        grads[l]["gmlp_ws"], grads[l]["gmlp_bs"], grads[l]["out_norm_gmlp_g"] = dws, dbs, dg_og[0]

        do, dl, dg_oa = _onorm_bwd(s["o"], dy, w["g_oa"], name=f"l{l}_onorm_bwd")
        grads[l]["out_norm_mla_g"] = _unpad_value_lanes(dg_oa[0], 0)
        dq, dk, dv = carrying(l, "bwd_attn_dkv", _attn_bwd, s["q"], s["kv"], s["kp"], do, s["lse"], dl,
                              name=f"l{l}_attn_bwd")
        dz, cq, dqb, ckv, dkvb, dg_q, dg_kv = _mla_prep_bwd(
            s["z"], dz, dq, dk, dv, w["g_q"], w["g_kv"], w["w_uq"], w["w_ukv"], ctab, stab,
            name=f"l{l}_mla_prep_bwd")
        grads[l]["mla_q_norm_g"], grads[l]["mla_kv_norm_g"] = dg_q[0], dg_kv[0]
        plan.small_ready(l, grads[l])
        dw_uq = carrying(l, "bwd_dw_uq", _mm, flat(cq), flat(dqb), dims="tn", name=f"l{l}_dw_uq", tm=Q_RANK,
                         tn=1024, tk=4096)
        grads[l]["mla_w_uq"] = _unpad_heads(dw_uq, NOPE + ROPE)
        grads[l]["w_in_t"] = _unpad_w_in_t(carrying(l, "bwd_dw_in", _mm, flat(dz), flat(s["h1"]), dims="tn",
                                                    name=f"l{l}_dw_in", tm=D_IN_PAD // 2, tn=d, tk=2048))
        grads[l]["mla_w_ukv"] = carrying(l, "bwd_dw_ukv", _mm, flat(ckv), flat(dkvb), dims="tn", name=f"l{l}_dw_ukv",
                                         tm=KV_RANK, tn=1024, tk=4096)
        plan.layer_grads(l, grads[l])
        dh1 = carrying(l, "bwd_d_h1", _mm, flat(dz), w["w_in_t"], dims="nn", name=f"l{l}_d_h1", tm=tmt, tn=d,
                       tk=D_IN_PAD, out_dtypes=(BF16,))
        below = dict(branch3=saved[l - 1]["f"], mod_gate=mods[l - 1], gate_row=GATE2) if l > 0 else {}
        node = carrying(l, "bwd_resnode_mix", _resnode_bwd, s["x_in"], w["g_mix"], name=f"l{l}_resnode_mix",
                        dh3=cube(dh1), dres3=dx_mid, mod_nm=mod, rows=(SHIFT1, SCALE1), **below)
        grads[l]["norm_mix_g"] = node["dg"][0]
        dshift1, dscale1 = node["dnm"][:, 0], node["dnm"][:, 1]
        dmods[l] = jnp.stack([dshift1, dscale1, dgate1, dshift2, dscale2, dgate2], axis=1)
        plan.layer_done(l)
    return node["dx"], dmods


W_NAMES = ("w_ada", "b_ada", "norm_mix_g", "w_in", "gmlp_ws", "gmlp_bs", "mla_q_norm_g", "mla_kv_norm_g",
           "mla_w_uq", "mla_w_ukv", "out_norm_gmlp_g", "out_norm_mla_g", "w_out", "norm_ffn_g", "w_ff1", "w_ff2",
           "final_norm_g")
FLAT_KEY = {"w_in": "w_in", "w_uq": "mla_w_uq", "w_ukv": "mla_w_ukv", "w_out": "w_out", "w_ff1": "w_ff1",
            "w_ff2": "w_ff2"}
COL_SHARDED = ("w_in", "w_uq", "w_ukv", "w_ff1")
FULL_SHAPE = {"w_in": (D_MODEL, D_IN), "w_uq": (Q_RANK, HEADS * (NOPE + ROPE)), "w_ukv": (KV_RANK, HEADS * 128),
              "w_out": (D_MODEL, D_MODEL)}
SMALL_LAYER_NAMES = ("gmlp_ws", "gmlp_bs", "mla_q_norm_g", "mla_kv_norm_g", "out_norm_gmlp_g",
                     "out_norm_mla_g", "norm_ffn_g")


def _silu(v):
    return v * (1.0 / (1.0 + jnp.exp(-v)))


class _CommPlan:
    FWD = {"fwd_attn": ("ff", 0, "spread"), "fwd_out_a": ("ff", 0, "pass"),
           "fwd_ff1": ("mix", 1, "spread"), "fwd_ff2": ("mix", 1, "pass")}
    BWD = {"bwd_d_r": ("mix", 1), "bwd_dw_ff2": ("mix", 1), "bwd_dw_ff1": ("mix", 1),
           "bwd_d_h2": ("ff", 0), "bwd_attn_dkv": ("ff", 0), "bwd_dw_uq": ("ff", 0)}
    BWD_LAST = {"bwd_d_h1": ("mix", 0), "bwd_resnode_mix": ("mix", 0)}
    SMALL = {"bwd_dw_in": "spread", "bwd_dw_ukv": "pass"}

    def __init__(self, weights, ids, dev, core):
        self.weights, self.ids, self.dev, self.core = weights, ids, dev, core
        self.used, self.rows = _flat_rows()
        self.flat = {("mix", l): self._flat_mix(l) for l in range(DEPTH)}
        self.flat.update({("ff", l): jnp.concatenate([weights["w_ff1"][l], weights["w_ff2"][l]], axis=0).astype(BF16)
                          for l in range(DEPTH)})
        self.lw, self.rs, self.grads, self.spread = {}, {}, {}, {}
        self.small_vec, self.small_sum, self.small_spread, self.extra = {}, {}, None, {}
        self.lw = {l: _small_weights(weights, l) for l in range(DEPTH)}

    def _flat_mix(self, l):
        pieces = []
        for nm, _ in FSDP_SECTIONS:
            shard = self.weights[FLAT_KEY[nm]][l]
            pieces.append(shard.T if nm == "w_in" else shard.reshape(-1, FLAT_W))
        pieces.append(jnp.zeros((self.rows - self.used, FLAT_W), F32))
        return jnp.concatenate(pieces, axis=0).astype(BF16)

    def _arrived(self, group, l, gath):
        flat = self.flat[group, l]
        hr = flat.shape[0] // 2
        mine = lax.dynamic_slice(flat, (self.core * hr, 0), (hr, FLAT_W))
        gath = lax.dynamic_update_slice(gath, mine[None], (self.dev, 0, 0))
        if group == "ff":
            self.lw[l]["ff"] = gath.reshape(N_CHIPS, 2, hr, FLAT_W)
            return
        w_gath = gath.reshape(N_CHIPS, self.rows, FLAT_W)
        full, off = {}, 0
        for nm, nrows in FSDP_SECTIONS:
            sec = w_gath[:, off:off + nrows]
            off += nrows
            rows, cols = FULL_SHAPE[nm]
            if nm == "w_in":
                full["w_in_t"] = sec.reshape(cols, rows)
            else:
                full[FLAT_KEY[nm]] = (_chunks_to_cols(sec, rows, cols) if nm in COL_SHARDED
                                      else sec.reshape(rows, cols))
        self.lw[l].update(_mix_weights(full))

    def layer(self, l):
        return self.lw[l]

    def host(self, l, tag):
        if tag == "fwd_normmod1":
            return _gather_spread(self.flat["mix", 0]) if l == 0 else None
        if tag in self.FWD:
            group, ahead, what = self.FWD[tag]
            if l + ahead >= DEPTH:
                return None
            return _gather_spread(self.flat[group, l + ahead]) if what == "spread" else _gather_pass_on(self.spread[group])
        if tag in self.SMALL:
            if self.SMALL[tag] == "spread":
                return _gather_spread(self.small_vec[l], halves=False)
            return _gather_pass_on(self.small_spread)
        rs = self._rs_for(l, tag)
        return None if rs is None or rs.stage > 2 else rs.next_exchange()

    def _rs_for(self, l, tag):
        if tag in self.BWD_LAST:
            return self.rs.get(self.BWD_LAST[tag]) if l == 0 else None
        if tag not in self.BWD:
            return None
        group, ahead = self.BWD[tag]
        return self.rs.get((group, l + ahead))

    def hosted(self, l, tag, outs):
        if tag == "fwd_normmod1":
            self._arrived("mix", 0, _run_exchange(_gather_pass_on(outs[0]), name="l0_mix_gather_pass_on")[0])
        elif tag in self.FWD:
            group, ahead, what = self.FWD[tag]
            if what == "spread":
                self.spread[group] = outs[0]
            else:
                self._arrived(group, l + ahead, outs[0])
        elif tag in self.SMALL:
            if self.SMALL[tag] == "spread":
                self.small_spread = outs[0]
            else:
                self._small_arrived(l, outs[0])
        else:
            self._rs_for(l, tag).done(outs)

    def ff_grads(self, l, g_ff):
        self.rs["ff", l] = _ReduceScatter(g_ff, self.ids, f"l{l}_ff_rs")

    def layer_grads(self, l, grads):
        self.grads[l] = grads
        pieces = []
        for nm, nrows in FSDP_SECTIONS:
            if nm == "w_in":
                pieces.append(grads["w_in_t"].reshape(N_CHIPS, nrows, FLAT_W))
                continue
            g = grads[FLAT_KEY[nm]]
            pieces.append(_cols_to_chunks(g) if nm in COL_SHARDED else g.reshape(N_CHIPS, nrows, FLAT_W))
        pieces.append(jnp.zeros((N_CHIPS, self.rows - self.used, FLAT_W), F32))
        self.rs["mix", l] = _ReduceScatter(jnp.concatenate(pieces, axis=1), self.ids, f"l{l}_mix_rs")

    def scalars(self, loss_part, d_final_g):
        self.extra = {0: [loss_part[None]]}
        self.extra.setdefault(DEPTH - 1, []).insert(0, d_final_g)

    def small_ready(self, l, grads):
        parts = [grads[nm].reshape(-1) for nm in SMALL_LAYER_NAMES] + self.extra.get(l, [])
        vec = jnp.concatenate(parts)
        rows = -(-vec.shape[0] // (8 * FLAT_W)) * 8
        self.small_vec[l] = jnp.pad(vec, (0, rows * FLAT_W - vec.shape[0])).reshape(rows, FLAT_W)

    def layer_done(self, l):
        if l == 0:
            self.rs["mix", 0].finish_alone()

    def _small_arrived(self, l, gath):
        gath = lax.dynamic_update_slice(gath, self.small_vec[l][None], (self.dev, 0, 0))
        self.small_sum[l] = _sum_leading(gath, name=f"l{l}_small_sum").reshape(-1)

    def small_grads(self):
        out = {nm: [] for nm in SMALL_LAYER_NAMES}
        for l in range(DEPTH):
            off = 0
            for nm in SMALL_LAYER_NAMES:
                size = self.weights[nm][l].size
                out[nm].append(self.small_sum[l][off:off + size].reshape(self.weights[nm].shape[1:]))
                off += size
            if l == DEPTH - 1:
                final = self.small_sum[l][off:off + self.weights["final_norm_g"].size]
                off += final.shape[0]
            if l == 0:
                loss = self.small_sum[l][off]
        res = {nm: jnp.stack(parts, axis=0) for nm, parts in out.items()}
        res["final_norm_g"] = final
        late = jnp.stack([self.grads[l]["norm_mix_g"] for l in range(DEPTH)], axis=0)
        late = _allgather8(late.reshape(-1, 128), name="gather_norm_mix_g")
        res["norm_mix_g"] = _sum_leading(late, name="norm_mix_g_sum").reshape(self.weights["norm_mix_g"].shape)
        return loss, res

    def mix_grads(self):
        per = {FLAT_KEY[nm]: [] for nm, _ in FSDP_SECTIONS}
        for l in range(DEPTH):
            shard, off = self.rs["mix", l].result, 0
            for nm, nrows in FSDP_SECTIONS:
                key = FLAT_KEY[nm]
                sec = shard[off:off + nrows]
                per[key].append(sec.T if nm == "w_in" else sec.reshape(self.weights[key].shape[1:]))
                off += nrows
        return {key: jnp.stack(parts, axis=0) for key, parts in per.items()}

    def ff_shards(self):
        return [self.rs["ff", l].result for l in range(DEPTH)]


def kernel(x, c, positions, w_ada, b_ada, norm_mix_g, w_in, gmlp_ws, gmlp_bs, mla_q_norm_g, mla_kv_norm_g, mla_w_uq, mla_w_ukv, out_norm_gmlp_g, out_norm_mla_g, w_out, norm_ffn_g, w_ff1, w_ff2, final_norm_g, loss_target, m_w_ada, m_b_ada, m_norm_mix_g, m_w_in, m_gmlp_ws, m_gmlp_bs, m_mla_q_norm_g, m_mla_kv_norm_g, m_mla_w_uq, m_mla_w_ukv, m_out_norm_gmlp_g, m_out_norm_mla_g, m_w_out, m_norm_ffn_g, m_w_ff1, m_w_ff2, m_final_norm_g, v_w_ada, v_b_ada, v_norm_mix_g, v_w_in, v_gmlp_ws, v_gmlp_bs, v_mla_q_norm_g, v_mla_kv_norm_g, v_mla_w_uq, v_mla_w_ukv, v_out_norm_gmlp_g, v_out_norm_mla_g, v_w_out, v_norm_ffn_g, v_w_ff1, v_w_ff2, v_final_norm_g):
    weights = dict(w_ada=w_ada, b_ada=b_ada, norm_mix_g=norm_mix_g, w_in=w_in, gmlp_ws=gmlp_ws, gmlp_bs=gmlp_bs,
                   mla_q_norm_g=mla_q_norm_g, mla_kv_norm_g=mla_kv_norm_g, mla_w_uq=mla_w_uq, mla_w_ukv=mla_w_ukv,
                   out_norm_gmlp_g=out_norm_gmlp_g, out_norm_mla_g=out_norm_mla_g, w_out=w_out,
                   norm_ffn_g=norm_ffn_g, w_ff1=w_ff1, w_ff2=w_ff2, final_norm_g=final_norm_g)
    mom_m = dict(zip(W_NAMES, (m_w_ada, m_b_ada, m_norm_mix_g, m_w_in, m_gmlp_ws, m_gmlp_bs, m_mla_q_norm_g,
                               m_mla_kv_norm_g, m_mla_w_uq, m_mla_w_ukv, m_out_norm_gmlp_g, m_out_norm_mla_g,
                               m_w_out, m_norm_ffn_g, m_w_ff1, m_w_ff2, m_final_norm_g)))
    mom_v = dict(zip(W_NAMES, (v_w_ada, v_b_ada, v_norm_mix_g, v_w_in, v_gmlp_ws, v_gmlp_bs, v_mla_q_norm_g,
                               v_mla_kv_norm_g, v_mla_w_uq, v_mla_w_ukv, v_out_norm_gmlp_g, v_out_norm_mla_g,
                               v_w_out, v_norm_ffn_g, v_w_ff1, v_w_ff2, v_final_norm_g)))
    bsz, seq, d = x.shape
    px, py, pc = _position()
    chip = 2 * px + py
    dev = 2 * chip + pc
    ids = jnp.stack([pc, chip]).astype(jnp.int32)
    n_ex = N_DEV * bsz
    ada_cols = w_ada.shape[-1]

    c_all = _allgather8(c.reshape(bsz * d // 128, 128), name="gather_c").reshape(n_ex, d)
    mod_parts = []
    for l in range(DEPTH):
        bias = lax.dynamic_slice(b_ada[l], (chip * ada_cols,), (ada_cols,))[None]
        mod_parts.append(_mm(c_all, w_ada, dims="nn", name=f"l{l}_mod", tm=n_ex, tn=ada_cols, tk=d, n=ada_cols,
                             b_block=((None, d, ada_cols), lambda i, j, k, l=l: (l, k, j)),
                             epilogue=lambda acc, bv: (acc + bv,), extras=(bias,),
                             extra_specs=(pl.BlockSpec((1, ada_cols), lambda i, j, k: (0, j)),), a_fn=_silu))
    mod_g = _allgather8(jnp.concatenate(mod_parts, axis=0), name="gather_mod")
    mod_g = mod_g.reshape(N_CHIPS, 2, DEPTH, n_ex, ada_cols)[:, 0]
    mod_full = mod_g.transpose(1, 2, 0, 3).reshape(DEPTH, n_ex, N_CHIPS * ada_cols)
    mod_mine = lax.dynamic_slice(mod_full, (0, dev * bsz, 0), (DEPTH, bsz, N_MOD * d))
    mod_mine = jnp.pad(mod_mine.reshape(DEPTH, bsz, N_MOD, d), ((0, 0), (0, 0), (0, MOD_ROWS - N_MOD), (0, 0)))
    mods = [mod_mine[l] for l in range(DEPTH)]

    plan = _CommPlan(weights, ids, dev, pc)
    grad_x, dmods = _local_step(x, loss_target, positions, mods, final_norm_g, plan)
    grad = plan.mix_grads()

    loss, small = plan.small_grads()
    grad.update(small)

    dmod = jnp.stack(dmods, axis=1).reshape(bsz * DEPTH * N_MOD, d)
    dmod_all = _allgather8(dmod, name="gather_dmod").reshape(n_ex, DEPTH, N_MOD * d)
    gw, gb = [], []
    for l in range(DEPTH):
        dm = dmod_all[:, l]
        dm_cols = lax.dynamic_slice(dm, (0, chip * ada_cols), (n_ex, ada_cols))
        gw.append(_mm(c_all, dm_cols, dims="tn", name=f"l{l}_dw_ada", tm=d, tn=ada_cols, tk=n_ex, a_fn=_silu,
                      out_into=(w_ada.shape, (None, d, ada_cols), lambda i, j, k, l=l: (l, i, j),
                                gw[-1] if gw else None)))
        gb.append(_sum_leading(dm.reshape(n_ex, N_MOD * d // FLAT_W, FLAT_W), name=f"l{l}_db_ada").reshape(-1))
    grad["w_ada"] = gw[-1]
    grad["b_ada"] = jnp.stack(gb, axis=0)

    delta, new_m, new_v = {}, {}, {}
    ff_bufs = plan.ff_shards()
    for nm, row_off in (("w_ff1", 0), ("w_ff2", FLAT_W)):
        grad[nm], delta[nm], new_m[nm], new_v[nm] = _adamw_layers(
            weights[nm], mom_m[nm], mom_v[nm], ff_bufs, row_off, name=f"adamw_{nm}")
    for nm in W_NAMES:
        if nm not in delta:
            delta[nm], new_m[nm], new_v[nm] = _adamw(weights[nm], grad[nm], mom_m[nm], mom_v[nm],
                                                     name=f"adamw_{nm}")
    return (loss, grad_x, *[grad[nm] for nm in W_NAMES], *[delta[nm] for nm in W_NAMES],
            *[new_m[nm] for nm in W_NAMES], *[new_v[nm] for nm in W_NAMES])
```

```python
import functools
import math

import jax
import jax.numpy as jnp
from jax import lax
from jax.experimental import pallas as pl
from jax.experimental.pallas import tpu as pltpu

F32 = jnp.float32
BF16 = jnp.bfloat16

D_MODEL = 1024
DEPTH = 2
D_GMLP = 512
GROUPS = 8
GROUP_DIM = 64
CHUNK = 128
HEADS = 8
NOPE = 64
ROPE = 32
HEAD_PAD = 128
Q_RANK = 256
KV_RANK = 128
D_FF = 4096
N_MOD = 6
MOD_ROWS = 8
EPS = 1e-6
ROPE_THETA = 10000.0
D_IN = 1440
D_IN_PAD = 1536
ATTN_SCALE = (NOPE + ROPE) ** -0.5
LOG2E = math.log2(math.e)
SCALE_LOG2 = ATTN_SCALE * LOG2E
N_CHIPS = 4
N_DEV = 8

ADAM_LR = 0.001
ADAM_B1 = 0.9
ADAM_B2 = 0.999
ADAM_EPS = 1e-08
ADAM_WD = 0.01
ADAM_STEP = 10

VMEM_LIMIT = 48 * 1024 * 1024
FLAT_W = 1024
ROW_ALIGN = 256

NN = (((1,), (0,)), ((), ()))
NT = (((1,), (1,)), ((), ()))
TN = (((0,), (0,)), ((), ()))
MESH = pl.DeviceIdType.MESH

SHIFT1, SCALE1, GATE1, SHIFT2, SCALE2, GATE2 = range(6)

FSDP_SECTIONS = (("w_out", 256), ("w_in", 360), ("w_uq", 48), ("w_ukv", 32))


def _cparams(vmem=VMEM_LIMIT):
    return pltpu.CompilerParams(vmem_limit_bytes=vmem)


def _dot(a, b, dims=NN):
    return lax.dot_general(a, b, dims, preferred_element_type=F32)


def _iota(shape, axis):
    return lax.broadcasted_iota(jnp.int32, shape, axis)


def _gelu(x):
    k = math.sqrt(2.0 / math.pi)
    return 0.5 * x * (1.0 + jnp.tanh(k * (x + 0.044715 * (x * x * x))))


def _gelu_and_grad(x):
    k = math.sqrt(2.0 / math.pi)
    x2 = x * x
    t = jnp.tanh(k * (x + 0.044715 * (x2 * x)))
    half = 0.5 * (1.0 + t)
    return x * half, half + 0.5 * x * (1.0 - t * t) * (k * (1.0 + 3.0 * 0.044715 * x2))


def _rms_fwd(x, g, n):
    r = lax.rsqrt(jnp.sum(x * x, axis=-1, keepdims=True) * (1.0 / n) + EPS)
    return x * r * g


def _rms_bwd(x, g, dy, n):
    r = lax.rsqrt(jnp.sum(x * x, axis=-1, keepdims=True) * (1.0 / n) + EPS)
    xh = x * r
    dxh = dy * g
    dx = r * (dxh - xh * (jnp.sum(dxh * xh, axis=-1, keepdims=True) * (1.0 / n)))
    dg = jnp.sum(dy * xh, axis=0, keepdims=True)
    return dx, dg


def _pick_rows(rows, limit):
    if rows <= limit:
        return rows
    for t in range(limit, 7, -8):
        if rows % t == 0:
            return t
    return rows


def _mm(a, b, *, dims, name, tm=512, tn=1024, tk=1024, out_dtypes=(F32,), epilogue=None,
        extras=(), extra_specs=(), a_fn=None, weights_outer=False, side=None, b_block=None, n=None,
        out_into=None):
    if dims == "tn":
        kk, m = a.shape
    else:
        m, kk = a.shape
    if n is None:
        n = b.shape[0] if dims == "nt" else b.shape[1]
    tm, tn, tk = min(tm, m), min(tn, n), min(tk, kk)
    assert m % tm == 0 and n % tn == 0 and kk % tk == 0, (name, a.shape, b.shape, tm, tn, tk)
    ni, nj, nk = m // tm, n // tn, kk // tk

    def spec(shape, pick):
        if weights_outer:
            return pl.BlockSpec(shape, lambda j, i, k: pick(i, j, k))
        return pl.BlockSpec(shape, pick)

    if dims == "tn":
        a_spec = spec((tk, tm), lambda i, j, k: (k, i))
    else:
        a_spec = spec((tm, tk), lambda i, j, k: (i, k))
    if b_block is not None:
        b_spec = spec(*b_block)
    elif dims == "nt":
        b_spec = spec((tn, tk), lambda i, j, k: (j, k))
    else:
        b_spec = spec((tk, tn), lambda i, j, k: (k, j))
    o_spec = spec((tm, tn), lambda i, j, k: (i, j))
    out_shape = [jax.ShapeDtypeStruct((m, n), dt) for dt in out_dtypes]
    out_specs = [o_spec] * len(out_dtypes)
    prev, io_aliases = (), {}
    if out_into is not None:
        full_shape, block, index, before = out_into
        assert len(out_dtypes) == 1 and not extras
        out_shape = [jax.ShapeDtypeStruct(full_shape, out_dtypes[0])]
        out_specs = [spec(block, index)]
        if before is not None:
            prev, io_aliases = (before,), {2: 0}
    assert not (weights_outer and extra_specs)
    dn = {"nn": NN, "nt": NT, "tn": TN}[dims]
    n_ex, n_out = len(extras), len(out_dtypes)
    e_specs = [o_spec if s is None else s for s in (tuple(extra_specs) + (None,) * n_ex)[:n_ex]]

    n_prev = len(prev)

    def body(*refs):
        a_ref, b_ref = refs[0], refs[1]
        e_refs = refs[2 + n_prev:2 + n_prev + n_ex]
        o_refs = refs[2 + n_prev + n_ex:2 + n_prev + n_ex + n_out]
        av = a_ref[...]
        if a_fn is not None:
            av = a_fn(av)
        bv = b_ref[...]
        if bv.ndim == 3:
            if dims == "nt":
                bv = jnp.concatenate([bv[c] for c in range(bv.shape[0])], axis=1)
            else:
                bv = bv.reshape(-1, bv.shape[-1])
        part = _dot(av.astype(BF16), bv.astype(BF16), dn)

        def finish(acc):
            outs = (acc,) if epilogue is None else epilogue(acc, *[e[...] for e in e_refs])
            for o_ref, o in zip(o_refs, outs):
                o_ref[...] = o.astype(o_ref.dtype)

        if nk == 1:
            finish(part)
        else:
            acc_ref = refs[-1]
            k = pl.program_id(2)

            @pl.when(k == 0)
            def _():
                acc_ref[...] = part

            @pl.when(k > 0)
            def _():
                acc_ref[...] += part

            @pl.when(k == nk - 1)
            def _():
                finish(acc_ref[...])

    outs, side_outs = _hosted_call(
        body, name=name, grid=(nj, ni, nk) if weights_outer else (ni, nj, nk),
        in_specs=[a_spec, b_spec] + [ANY_SPEC] * n_prev + e_specs,
        out_specs=out_specs, out_shape=out_shape,
        scratch_shapes=[pltpu.VMEM((tm, tn), F32)] if nk > 1 else [],
        args=(a, b, *prev, *extras), side=side, io_aliases=io_aliases)
    res = outs[0] if n_out == 1 else outs
    return res if side is None else (res, side_outs)


def _mod_spec(tm, tn, seq):
    return pl.BlockSpec((1, MOD_ROWS, tn), lambda i, j, k: ((i * tm) // seq, 0, j))


def _normmod_fwd(x3, g, mod, shift_row, scale_row, *, name, tb=512, side=None):
    bsz, seq, d = x3.shape
    tb = min(tb, seq)

    def body(x_ref, g_ref, mod_ref, h_ref):
        m = mod_ref[0]
        nrm = _rms_fwd(x_ref[0], g_ref[...], d)
        h = nrm * (1.0 + m[scale_row:scale_row + 1, :]) + m[shift_row:shift_row + 1, :]
        h_ref[0] = h.astype(BF16)

    outs, side_outs = _hosted_call(
        body, name=name, grid=(bsz, seq // tb),
        in_specs=[pl.BlockSpec((1, tb, d), lambda b, i: (b, i, 0)),
                  pl.BlockSpec((1, d), lambda b, i: (0, 0)),
                  pl.BlockSpec((1, MOD_ROWS, d), lambda b, i: (b, 0, 0))],
        out_specs=[pl.BlockSpec((1, tb, d), lambda b, i: (b, i, 0))],
        out_shape=[jax.ShapeDtypeStruct((bsz, seq, d), BF16)],
        args=(x3, g, mod), side=side)
    return outs[0] if side is None else (outs[0], side_outs)


def _pair_mean_exact(x, lo):
    s_lo = jnp.sum(jnp.where(lo, x, 0.0), axis=-1, keepdims=True)
    s_hi = jnp.sum(jnp.where(lo, 0.0, x), axis=-1, keepdims=True)
    return jnp.where(lo, s_lo, s_hi) * (1.0 / GROUP_DIM)


def _gmlp_pair_fwd(gv_p, w0, w1, bias, lo):
    mu = _pair_mean_exact(gv_p, lo)
    dlt = gv_p - mu
    var = _pair_mean_exact(dlt * dlt, lo)
    rstd = lax.rsqrt(var + EPS)
    vn = dlt * rstd
    vnb = vn.astype(BF16)
    mixed = jnp.where(lo, _dot(w0, vnb), _dot(w1, vnb)) + bias
    return vn, vnb, rstd, mixed


def _tril_bf16(w):
    t = w.shape[-1]
    return jnp.where(_iota((t, t), 1) <= _iota((t, t), 0), w, 0.0).astype(BF16)


def _gmlp_fwd(z3, ws, bexp, g_out, *, name):
    bsz, seq, _ = z3.shape
    nc = seq // CHUNK

    def body(u_ref, v_ref, ws_ref, b_ref, g_ref, y_ref):
        lo = _iota((CHUNK, 128), 1) < GROUP_DIM
        gu = _gelu(u_ref[0].astype(F32))
        gv = _gelu(v_ref[0].astype(F32))
        parts = []
        for p in range(GROUPS // 2):
            sl = slice(128 * p, 128 * p + 128)
            w0 = _tril_bf16(ws_ref[2 * p])
            w1 = _tril_bf16(ws_ref[2 * p + 1])
            _, _, _, mixed = _gmlp_pair_fwd(gv[:, sl], w0, w1, b_ref[p], lo)
            parts.append(gu[:, sl] * mixed)
        yg = jnp.concatenate(parts, axis=1)
        y_ref[0] = _rms_fwd(yg, g_ref[...], D_GMLP).astype(BF16)

    return pl.pallas_call(
        body, name=name, grid=(bsz, nc),
        in_specs=[pl.BlockSpec((1, CHUNK, D_GMLP), lambda b, i: (b, i, 0)),
                  pl.BlockSpec((1, CHUNK, D_GMLP), lambda b, i: (b, i, 1)),
                  pl.BlockSpec((GROUPS, CHUNK, CHUNK), lambda b, i: (0, 0, 0)),
                  pl.BlockSpec((GROUPS // 2, CHUNK, 128), lambda b, i: (0, 0, 0)),
                  pl.BlockSpec((1, D_GMLP), lambda b, i: (0, 0))],
        out_specs=pl.BlockSpec((1, CHUNK, D_GMLP), lambda b, i: (b, i, 0)),
        out_shape=jax.ShapeDtypeStruct((bsz, seq, D_GMLP), BF16),
        compiler_params=_cparams(),
    )(z3, z3, ws, bexp, g_out)


def _gmlp_bwd(z3, dyn3, ws, wst, bexp, g_out, *, name, dy_col):
    bsz, seq, _ = z3.shape
    nc = seq // CHUNK
    npair = GROUPS // 2

    def body(u_ref, v_ref, dy_ref, ws_ref, wst_ref, b_ref, g_ref, duv_ref, dws_ref, dbs_ref, dg_ref, dbacc):
        first = jnp.logical_and(pl.program_id(0) == 0, pl.program_id(1) == 0)
        last = jnp.logical_and(pl.program_id(0) == bsz - 1, pl.program_id(1) == nc - 1)

        @pl.when(first)
        def _():
            dws_ref[...] = jnp.zeros_like(dws_ref)
            dg_ref[...] = jnp.zeros_like(dg_ref)
            dbacc[...] = jnp.zeros_like(dbacc)

        lo = _iota((CHUNK, 128), 1) < GROUP_DIM
        tril = _iota((CHUNK, CHUNK), 1) <= _iota((CHUNK, CHUNK), 0)
        u = u_ref[0].astype(F32)
        v = v_ref[0].astype(F32)
        gu, dgu = _gelu_and_grad(u)
        gv, dgv_dv = _gelu_and_grad(v)
        fwd = []
        for p in range(npair):
            sl = slice(128 * p, 128 * p + 128)
            w0 = _tril_bf16(ws_ref[2 * p])
            w1 = _tril_bf16(ws_ref[2 * p + 1])
            fwd.append(_gmlp_pair_fwd(gv[:, sl], w0, w1, b_ref[p], lo))
        yg = jnp.concatenate([gu[:, 128 * p:128 * p + 128] * fwd[p][3] for p in range(npair)], axis=1)
        dyg, dg = _rms_bwd(yg, g_ref[...], dy_ref[0].astype(F32), D_GMLP)
        dg_ref[...] += dg
        du_parts, dv_parts = [], []
        for p in range(npair):
            sl = slice(128 * p, 128 * p + 128)
            vn, vnb, rstd, mixed = fwd[p]
            dyg_p = dyg[:, sl]
            dmixed = dyg_p * gu[:, sl]
            dbacc[p] += dmixed
            dm_lo = jnp.where(lo, dmixed, 0.0).astype(BF16)
            dm_hi = jnp.where(lo, 0.0, dmixed).astype(BF16)
            dws_ref[2 * p] += jnp.where(tril, _dot(dm_lo, vnb, NT), 0.0)
            dws_ref[2 * p + 1] += jnp.where(tril, _dot(dm_hi, vnb, NT), 0.0)
            dmb = dmixed.astype(BF16)
            dvn = jnp.where(lo, _dot(wst_ref[2 * p], dmb), _dot(wst_ref[2 * p + 1], dmb))
            dgv = rstd * (dvn - _pair_mean_exact(dvn, lo) - vn * _pair_mean_exact(dvn * vn, lo))
            dv_parts.append(dgv * dgv_dv[:, sl])
            du_parts.append(dyg_p * mixed * dgu[:, sl])
        duv_ref[0] = jnp.concatenate(du_parts + dv_parts, axis=1).astype(BF16)

        @pl.when(last)
        def _():
            sel = jnp.where(_iota((8, 128), 0) == 0, (_iota((8, 128), 1) < GROUP_DIM).astype(F32),
                            jnp.where(_iota((8, 128), 0) == 1, (_iota((8, 128), 1) >= GROUP_DIM).astype(F32), 0.0))
            for p in range(npair):
                dbs_ref[p] = lax.dot_general(sel, dbacc[p], NT, precision=lax.Precision.HIGHEST,
                                             preferred_element_type=F32)

    duv, dws, dbs, dg = pl.pallas_call(
        body, name=name, grid=(bsz, nc),
        in_specs=[pl.BlockSpec((1, CHUNK, D_GMLP), lambda b, i: (b, i, 0)),
                  pl.BlockSpec((1, CHUNK, D_GMLP), lambda b, i: (b, i, 1)),
                  pl.BlockSpec((1, CHUNK, D_GMLP), lambda b, i: (b, i, dy_col)),
                  pl.BlockSpec((GROUPS, CHUNK, CHUNK), lambda b, i: (0, 0, 0)),
                  pl.BlockSpec((GROUPS, CHUNK, CHUNK), lambda b, i: (0, 0, 0)),
                  pl.BlockSpec((npair, CHUNK, 128), lambda b, i: (0, 0, 0)),
                  pl.BlockSpec((1, D_GMLP), lambda b, i: (0, 0))],
        out_specs=[pl.BlockSpec((1, CHUNK, 2 * D_GMLP), lambda b, i: (b, i, 0)),
                   pl.BlockSpec((GROUPS, CHUNK, CHUNK), lambda b, i: (0, 0, 0)),
                   pl.BlockSpec((npair, 8, CHUNK), lambda b, i: (0, 0, 0)),
                   pl.BlockSpec((1, D_GMLP), lambda b, i: (0, 0))],
        out_shape=[jax.ShapeDtypeStruct((bsz, seq, D_IN_PAD), BF16),
                   jax.ShapeDtypeStruct((GROUPS, CHUNK, CHUNK), F32),
                   jax.ShapeDtypeStruct((npair, 8, CHUNK), F32),
                   jax.ShapeDtypeStruct((1, D_GMLP), F32)],
        scratch_shapes=[pltpu.VMEM((npair, CHUNK, 128), F32)],
        compiler_params=_cparams(),
    )(z3, z3, dyn3, ws, wst, bexp, g_out)
    return duv, dws, dbs[:, :2, :].reshape(GROUPS, CHUNK), dg


def _partner(x):
    width = x.shape[-1]
    lane = _iota(x.shape, x.ndim - 1) % HEAD_PAD
    up = pltpu.roll(x, width - ROPE // 2, x.ndim - 1)
    down = pltpu.roll(x, ROPE // 2, x.ndim - 1)
    first = jnp.logical_and(lane >= NOPE, lane < NOPE + ROPE // 2)
    second = jnp.logical_and(lane >= NOPE + ROPE // 2, lane < NOPE + ROPE)
    return jnp.where(first, up, jnp.where(second, down, 0.0))


def _mla_prep_fwd(z3, g_q, g_kv, w_uq, w_ukv, ctab, stab, *, name, tb=256):
    bsz, seq, _ = z3.shape
    tb = min(tb, seq)
    hw = HEADS * HEAD_PAD

    def body(ql_ref, kvl_ref, krl_ref, gq_ref, gkv_ref, wuq_ref, wukv_ref, c_ref, s_ref, q_ref, kv_ref, kp_ref):
        cq = _rms_fwd(ql_ref[0].astype(F32), gq_ref[...], Q_RANK).astype(BF16)
        q = _dot(cq, wuq_ref[...])
        c1, s1 = c_ref[0], s_ref[0]
        c8, s8 = jnp.tile(c1, (1, HEADS)), jnp.tile(s1, (1, HEADS))
        q_ref[0] = ((q * c8 + _partner(q) * s8) * SCALE_LOG2).astype(BF16)
        ckv = _rms_fwd(kvl_ref[0].astype(F32), gkv_ref[...], KV_RANK).astype(BF16)
        kv = _dot(ckv, wukv_ref[...])
        kv_ref[0] = kv.astype(BF16)
        kr = krl_ref[0].astype(F32)
        kr = kr * c1 + _partner(kr) * s1
        lane = _iota((tb, hw), 1) % HEAD_PAD
        kp_ref[0] = jnp.where(lane < NOPE, kv, jnp.tile(kr, (1, HEADS))).astype(BF16)

    return pl.pallas_call(
        body, name=name, grid=(bsz, seq // tb),
        in_specs=[pl.BlockSpec((1, tb, Q_RANK), lambda b, i: (b, i, 4)),
                  pl.BlockSpec((1, tb, KV_RANK), lambda b, i: (b, i, 10)),
                  pl.BlockSpec((1, tb, HEAD_PAD), lambda b, i: (b, i, 11)),
                  pl.BlockSpec((1, Q_RANK), lambda b, i: (0, 0)),
                  pl.BlockSpec((1, KV_RANK), lambda b, i: (0, 0)),
                  pl.BlockSpec((Q_RANK, hw), lambda b, i: (0, 0)),
                  pl.BlockSpec((KV_RANK, hw), lambda b, i: (0, 0)),
                  pl.BlockSpec((1, tb, HEAD_PAD), lambda b, i: (b, i, 0)),
                  pl.BlockSpec((1, tb, HEAD_PAD), lambda b, i: (b, i, 0))],
        out_specs=[pl.BlockSpec((1, tb, hw), lambda b, i: (b, i, 0))] * 3,
        out_shape=[jax.ShapeDtypeStruct((bsz, seq, hw), BF16)] * 3,
        compiler_params=_cparams(),
    )(z3, z3, z3, g_q, g_kv, w_uq, w_ukv, ctab, stab)


def _mla_prep_bwd(z3, dz3, dq3, dk3, dv3, g_q, g_kv, w_uq, w_ukv, ctab, stab, *, name, tb=256):
    bsz, seq, _ = z3.shape
    tb = min(tb, seq)
    hw = HEADS * HEAD_PAD
    nb = seq // tb

    def body(ql_ref, kvl_ref, dq_ref, dk_ref, dv_ref, gq_ref, gkv_ref, wuq_ref, wukv_ref, c_ref, s_ref, dz_in,
             dz_ref, cq_ref, dqb_ref, ckv_ref, dkvb_ref, dgq_ref, dgkv_ref):
        @pl.when(jnp.logical_and(pl.program_id(0) == 0, pl.program_id(1) == 0))
        def _():
            dgq_ref[...] = jnp.zeros_like(dgq_ref)
            dgkv_ref[...] = jnp.zeros_like(dgkv_ref)

        c1, s1 = c_ref[0], s_ref[0]
        c8, s8 = jnp.tile(c1, (1, HEADS)), jnp.tile(s1, (1, HEADS))
        dqr = dq_ref[0]
        dqb = (dqr * c8 + _partner(dqr * s8)).astype(BF16)
        dqb_ref[0] = dqb
        ql = ql_ref[0].astype(F32)
        cq_ref[0] = _rms_fwd(ql, gq_ref[...], Q_RANK).astype(BF16)
        dql, dgq = _rms_bwd(ql, gq_ref[...], _dot(dqb, wuq_ref[...], NT), Q_RANK)
        dgq_ref[...] += dgq

        dk = dk_ref[0]
        lane = _iota((tb, hw), 1) % HEAD_PAD
        dkvb = jnp.where(lane < NOPE, dk, dv_ref[0]).astype(BF16)
        dkvb_ref[0] = dkvb
        kvl = kvl_ref[0].astype(F32)
        ckv_ref[0] = _rms_fwd(kvl, gkv_ref[...], KV_RANK).astype(BF16)
        dkvl, dgkv = _rms_bwd(kvl, gkv_ref[...], _dot(dkvb, wukv_ref[...], NT), KV_RANK)
        dgkv_ref[...] += dgkv

        dkr = dk[:, 0:HEAD_PAD].astype(F32)
        for h in range(1, HEADS):
            dkr = dkr + dk[:, HEAD_PAD * h:HEAD_PAD * (h + 1)].astype(F32)
        lane1 = _iota((tb, HEAD_PAD), 1)
        dkr = jnp.where(jnp.logical_and(lane1 >= NOPE, lane1 < NOPE + ROPE), dkr, 0.0)
        dkrl = dkr * c1 + _partner(dkr * s1)
        dz_ref[0] = jnp.concatenate([dql, dkvl, dkrl], axis=1).astype(BF16)

    return pl.pallas_call(
        body, name=name, grid=(bsz, nb),
        in_specs=[pl.BlockSpec((1, tb, Q_RANK), lambda b, i: (b, i, 4)),
                  pl.BlockSpec((1, tb, KV_RANK), lambda b, i: (b, i, 10)),
                  pl.BlockSpec((1, tb, hw), lambda b, i: (b, i, 0)),
                  pl.BlockSpec((1, tb, hw), lambda b, i: (b, i, 0)),
                  pl.BlockSpec((1, tb, hw), lambda b, i: (b, i, 0)),
                  pl.BlockSpec((1, Q_RANK), lambda b, i: (0, 0)),
                  pl.BlockSpec((1, KV_RANK), lambda b, i: (0, 0)),
                  pl.BlockSpec((Q_RANK, hw), lambda b, i: (0, 0)),
                  pl.BlockSpec((KV_RANK, hw), lambda b, i: (0, 0)),
                  pl.BlockSpec((1, tb, HEAD_PAD), lambda b, i: (b, i, 0)),
                  pl.BlockSpec((1, tb, HEAD_PAD), lambda b, i: (b, i, 0)),
                  ANY_SPEC],
        out_specs=[pl.BlockSpec((1, tb, 512), lambda b, i: (b, i, 2)),
                   pl.BlockSpec((1, tb, Q_RANK), lambda b, i: (b, i, 0)),
                   pl.BlockSpec((1, tb, hw), lambda b, i: (b, i, 0)),
                   pl.BlockSpec((1, tb, KV_RANK), lambda b, i: (b, i, 0)),
                   pl.BlockSpec((1, tb, hw), lambda b, i: (b, i, 0)),
                   pl.BlockSpec((1, Q_RANK), lambda b, i: (0, 0)),
                   pl.BlockSpec((1, KV_RANK), lambda b, i: (0, 0))],
        out_shape=[jax.ShapeDtypeStruct((bsz, seq, D_IN_PAD), BF16),
                   jax.ShapeDtypeStruct((bsz, seq, Q_RANK), BF16),
                   jax.ShapeDtypeStruct((bsz, seq, hw), BF16),
                   jax.ShapeDtypeStruct((bsz, seq, KV_RANK), BF16),
                   jax.ShapeDtypeStruct((bsz, seq, hw), BF16),
                   jax.ShapeDtypeStruct((1, Q_RANK), F32),
                   jax.ShapeDtypeStruct((1, KV_RANK), F32)],
        input_output_aliases={11: 0},
        compiler_params=_cparams(),
    )(z3, z3, dq3, dk3, dv3, g_q, g_kv, w_uq, w_ukv, ctab, stab, dz3)


ATTN_HEADS_PER_STEP = 4


def _attn_specs(tq, seq, hp):
    blk = pl.BlockSpec((1, tq, hp * HEAD_PAD), lambda b, h, i: (b, i, h))
    full = pl.BlockSpec((1, seq, hp * HEAD_PAD), lambda b, h, i: (b, 0, h))
    return blk, full


def _head(h):
    return slice(HEAD_PAD * h, HEAD_PAD * (h + 1))


def _attn_fwd(q3, kv3, kp3, *, name, tq=512, hp=ATTN_HEADS_PER_STEP, side=None):
    bsz, seq, hw = q3.shape
    tq = min(tq, seq)
    blk, full = _attn_specs(tq, seq, hp)

    def body(q_ref, kv_ref, kp_ref, o_ref, lse_ref):
        i = pl.program_id(2)

        def update(state, q, kp, kv, mask=None):
            m, l, acc = state
            s = _dot(q, kp, NT)
            if mask is not None:
                s = jnp.where(mask, s, -1e30)
            m_new = jnp.maximum(m, jnp.max(s, axis=1, keepdims=True))
            alpha = jnp.exp2(m - m_new)
            p = jnp.exp2(s - m_new)
            return m_new, alpha * l + jnp.sum(p, axis=1, keepdims=True), alpha * acc + _dot(p.astype(BF16), kv)

        def step(j, carry):
            st = pl.multiple_of(j * tq, tq)
            return tuple(update(carry[h], q_ref[0, :, _head(h)], kp_ref[0, pl.ds(st, tq), _head(h)],
                                kv_ref[0, pl.ds(st, tq), _head(h)]) for h in range(hp))

        init = tuple((jnp.full((tq, 1), -1e30, F32), jnp.zeros((tq, 1), F32), jnp.zeros((tq, HEAD_PAD), F32))
                     for _ in range(hp))
        carry = lax.fori_loop(0, i, step, init)

        st = pl.multiple_of(i * tq, tq)
        is_nope = _iota((tq, HEAD_PAD), 1) < NOPE
        causal = _iota((tq, tq), 1) <= _iota((tq, tq), 0)
        for h in range(hp):
            m, l, acc = update(carry[h], q_ref[0, :, _head(h)], kp_ref[0, pl.ds(st, tq), _head(h)],
                               kv_ref[0, pl.ds(st, tq), _head(h)], causal)
            o_ref[0, :, _head(h)] = jnp.where(is_nope, 0.0, acc / l).astype(BF16)
            lse_ref[0, :, _head(h)] = jnp.broadcast_to(m + jnp.log(l) * LOG2E, (tq, HEAD_PAD))

    outs, side_outs = _hosted_call(
        body, name=name, grid=(bsz, HEADS // hp, seq // tq),
        in_specs=[blk, full, full],
        out_specs=[blk, blk],
        out_shape=[jax.ShapeDtypeStruct((bsz, seq, hw), BF16), jax.ShapeDtypeStruct((bsz, seq, hw), F32)],
        args=(q3, kv3, kp3), side=side)
    return outs if side is None else (outs, side_outs)


def _attn_bwd(q3, kv3, kp3, do3, lse3, dl3, *, name, tq=512, hp=ATTN_HEADS_PER_STEP, side=None):
    bsz, seq, hw = q3.shape
    tq = min(tq, seq)
    nq = seq // tq
    blk, full = _attn_specs(tq, seq, hp)

    def body(kv_ref, kp_ref, q_ref, do_ref, lse_ref, dl_ref, dq_ref, dk_ref, dv_ref):
        j = pl.program_id(2)

        @pl.when(j == 0)
        def _():
            dq_ref[...] = jnp.zeros_like(dq_ref)

        def pair(h, row0, nrows, nkeys, mask=None):
            row0 = pl.multiple_of(row0, nrows)
            qi = q_ref[0, pl.ds(row0, nrows), _head(h)]
            do = do_ref[0, pl.ds(row0, nrows), _head(h)]
            kp = kp_ref[0, :nkeys, _head(h)]
            s = _dot(qi, kp, NT)
            if mask is not None:
                s = jnp.where(mask, s, -1e30)
            wide = nkeys // HEAD_PAD
            p = jnp.exp2(s - jnp.tile(lse_ref[0, pl.ds(row0, nrows), _head(h)], (1, wide)))
            dv = _dot(p.astype(BF16), do, TN)
            dp = _dot(do, kv_ref[0, :nkeys, _head(h)], NT)
            ds = (p * (dp - jnp.tile(dl_ref[0, pl.ds(row0, nrows), _head(h)], (1, wide)))).astype(BF16)
            dq_ref[0, pl.ds(row0, nrows), _head(h)] += _dot(ds, kp)
            return _dot(ds, qi, TN), dv

        def step(i, carry):
            st = pl.multiple_of(i * tq, tq)
            out = []
            for h in range(hp):
                dk, dv = pair(h, st, tq, tq)
                out.append((carry[h][0] + dk, carry[h][1] + dv))
            return tuple(out)

        causal = _iota((tq, tq), 1) <= _iota((tq, tq), 0)
        carry = tuple(pair(h, pl.multiple_of(j * tq, tq), tq, tq, causal) for h in range(hp))
        carry = lax.fori_loop(j + 1, nq, step, carry)
        for h in range(hp):
            dk_ref[0, :, _head(h)] = (carry[h][0] * (1.0 / LOG2E)).astype(BF16)
            dv_ref[0, :, _head(h)] = carry[h][1].astype(BF16)

        @pl.when(j == nq - 1)
        def _():
            dq_ref[...] = dq_ref[...] * ATTN_SCALE

    outs, side_outs = _hosted_call(
        body, name=name, grid=(bsz, HEADS // hp, nq),
        in_specs=[blk, blk, full, full, full, full],
        out_specs=[full, blk, blk],
        out_shape=[jax.ShapeDtypeStruct((bsz, seq, hw), F32)] + [jax.ShapeDtypeStruct((bsz, seq, hw), BF16)] * 2,
        args=(kv3, kp3, q3, do3, lse3, dl3), side=side)
    return outs if side is None else (outs, side_outs)


def _onorm_fwd(o3, yg3, g_pad, *, name, tb=512):
    bsz, seq, hw = o3.shape
    wg = yg3.shape[-1]
    tb = min(tb, seq)

    def body(o_ref, yg_ref, g_ref, y_ref):
        ya = _rms_fwd(o_ref[0].astype(F32), g_ref[...], HEADS * 64).astype(BF16)
        y_ref[0] = jnp.concatenate([ya, yg_ref[0]], axis=1)

    return pl.pallas_call(
        body, name=name, grid=(bsz, seq // tb),
        in_specs=[pl.BlockSpec((1, tb, hw), lambda b, i: (b, i, 0)),
                  pl.BlockSpec((1, tb, wg), lambda b, i: (b, i, 0)),
                  pl.BlockSpec((1, hw), lambda b, i: (0, 0))],
        out_specs=pl.BlockSpec((1, tb, hw + wg), lambda b, i: (b, i, 0)),
        out_shape=jax.ShapeDtypeStruct((bsz, seq, hw + wg), BF16),
        compiler_params=_cparams(),
    )(o3, yg3, g_pad)


def _onorm_bwd(o3, dy3, g_pad, *, name, tb=512):
    bsz, seq, hw = o3.shape
    tb = min(tb, seq)

    def body(o_ref, dy_ref, g_ref, do_ref, dl_ref, dg_ref):
        @pl.when(jnp.logical_and(pl.program_id(0) == 0, pl.program_id(1) == 0))
        def _():
            dg_ref[...] = jnp.zeros_like(dg_ref)

        o = o_ref[0].astype(F32)
        do, dg = _rms_bwd(o, g_ref[...], dy_ref[0].astype(F32), HEADS * 64)
        dg_ref[...] += dg
        do_ref[0] = do.astype(BF16)
        prod = do * o
        parts = []
        for h in range(HEADS):
            sh = jnp.sum(prod[:, HEAD_PAD * h:HEAD_PAD * (h + 1)], axis=1, keepdims=True)
            parts.append(jnp.broadcast_to(sh, (tb, HEAD_PAD)))
        dl_ref[0] = jnp.concatenate(parts, axis=1)

    return pl.pallas_call(
        body, name=name, grid=(bsz, seq // tb),
        in_specs=[pl.BlockSpec((1, tb, hw), lambda b, i: (b, i, 0)),
                  pl.BlockSpec((1, tb, hw), lambda b, i: (b, i, 0)),
                  pl.BlockSpec((1, hw), lambda b, i: (0, 0))],
        out_specs=[pl.BlockSpec((1, tb, hw), lambda b, i: (b, i, 0)),
                   pl.BlockSpec((1, tb, hw), lambda b, i: (b, i, 0)),
                   pl.BlockSpec((1, hw), lambda b, i: (0, 0))],
        out_shape=[jax.ShapeDtypeStruct((bsz, seq, hw), BF16),
                   jax.ShapeDtypeStruct((bsz, seq, hw), F32),
                   jax.ShapeDtypeStruct((1, hw), F32)],
        compiler_params=_cparams(),
    )(o3, dy3, g_pad)


def _resnode_bwd(x3, g, *, name, target3=None, dh3=None, dres3=None, mod_nm=None, rows=None,
                 branch3=None, mod_gate=None, gate_row=None, tb=512, side=None):
    bsz, seq, d = x3.shape
    tb = min(tb, seq)
    final = target3 is not None
    has_branch = branch3 is not None
    row_spec = pl.BlockSpec((1, tb, d), lambda b, i: (b, i, 0))
    vec_spec = pl.BlockSpec((1, d), lambda b, i: (0, 0))
    mod_spec = pl.BlockSpec((1, MOD_ROWS, d), lambda b, i: (b, 0, 0))

    ins, in_specs = [x3, g], [row_spec, vec_spec]
    if final:
        ins += [target3]
        in_specs += [row_spec]
    else:
        ins += [dh3, dres3, mod_nm]
        in_specs += [row_spec, row_spec, mod_spec]
    if has_branch:
        ins += [branch3, mod_gate]
        in_specs += [row_spec, mod_spec]

    out_names = ["dx", "dg"]
    out_specs = [row_spec, vec_spec]
    out_shape = [jax.ShapeDtypeStruct((bsz, seq, d), F32), jax.ShapeDtypeStruct((1, d), F32)]
    if final:
        out_names += ["loss"]
        out_specs += [pl.BlockSpec((1, 128), lambda b, i: (0, 0))]
        out_shape += [jax.ShapeDtypeStruct((1, 128), F32)]
    else:
        out_names += ["dnm"]
        out_specs += [mod_spec]
        out_shape += [jax.ShapeDtypeStruct((bsz, MOD_ROWS, d), F32)]
    if has_branch:
        out_names += ["dbr", "dgate"]
        out_specs += [row_spec, mod_spec]
        out_shape += [jax.ShapeDtypeStruct((bsz, seq, d), BF16), jax.ShapeDtypeStruct((bsz, MOD_ROWS, d), F32)]
    n_in = len(ins)

    def body(*refs):
        r = dict(zip(["x", "g"] + (["t"] if final else ["dh", "dres", "nm"]) + (["br", "gm"] if has_branch else []),
                     refs[:n_in]))
        o = dict(zip(out_names, refs[n_in:]))
        b_first = pl.program_id(1) == 0
        first = jnp.logical_and(pl.program_id(0) == 0, b_first)
        rowid = _iota((MOD_ROWS, d), 0)

        @pl.when(first)
        def _():
            o["dg"][...] = jnp.zeros_like(o["dg"])
            if final:
                o["loss"][...] = jnp.zeros_like(o["loss"])

        @pl.when(b_first)
        def _():
            if not final:
                o["dnm"][...] = jnp.zeros_like(o["dnm"])
            if has_branch:
                o["dgate"][...] = jnp.zeros_like(o["dgate"])

        x = r["x"][0]
        gv = r["g"][...]
        if final:
            e = _rms_fwd(x, gv, d) - r["t"][0]
            sq = jnp.sum(jnp.sum(e * e, axis=1, keepdims=True), axis=0, keepdims=True)
            o["loss"][...] += jnp.broadcast_to(sq * (0.5 / d), (1, 128))
            dx, dg = _rms_bwd(x, gv, e * (1.0 / d), d)
        else:
            m = r["nm"][0]
            dh = r["dh"][0].astype(F32)
            scale = m[rows[1]:rows[1] + 1, :]
            rstd = lax.rsqrt(jnp.sum(x * x, axis=-1, keepdims=True) * (1.0 / d) + EPS)
            xh = x * rstd
            nrm = xh * gv
            dshift = jnp.sum(dh, axis=0, keepdims=True)
            dscale = jnp.sum(dh * nrm, axis=0, keepdims=True)
            o["dnm"][0] += jnp.where(rowid == 0, dshift, jnp.where(rowid == 1, dscale, 0.0))
            dn = dh * (1.0 + scale)
            dg = jnp.sum(dn * xh, axis=0, keepdims=True)
            dxh = dn * gv
            dx = rstd * (dxh - xh * (jnp.sum(dxh * xh, axis=-1, keepdims=True) * (1.0 / d))) + r["dres"][0]
        o["dg"][...] += dg
        o["dx"][0] = dx
        if has_branch:
            gate = r["gm"][0][gate_row:gate_row + 1, :]
            o["dbr"][0] = (gate * dx).astype(BF16)
            dgate = jnp.sum(dx * r["br"][0], axis=0, keepdims=True)
            o["dgate"][0] += jnp.where(rowid == 0, dgate, 0.0)

    outs, side_outs = _hosted_call(
        body, name=name, grid=(bsz, seq // tb),
        in_specs=in_specs, out_specs=out_specs, out_shape=out_shape, args=tuple(ins), side=side)
    res = dict(zip(out_names, outs))
    return res if side is None else (res, side_outs)


def _adamw(w, g, m, v, *, name):
    shape = w.shape
    cols = shape[-1]
    rows = w.size // cols
    tr = _pick_rows(rows, max(8, (256 * 1024) // cols // 8 * 8))

    def body(w_ref, g_ref, m_ref, v_ref, d_ref, nm_ref, nv_ref):
        d_ref[...], nm_ref[...], nv_ref[...] = _adamw_math(w_ref[...], g_ref[...], m_ref[...], v_ref[...])

    if w.ndim == 3 and shape[1] % 8 == 0:
        tr3 = _pick_rows(shape[1], max(8, (256 * 1024) // cols // 8 * 8))
        spec3 = pl.BlockSpec((None, tr3, cols), lambda l, i: (l, i, 0))
        return tuple(pl.pallas_call(
            body, name=name, grid=(shape[0], shape[1] // tr3),
            in_specs=[spec3] * 4, out_specs=[spec3] * 3,
            out_shape=[jax.ShapeDtypeStruct(shape, F32)] * 3,
            compiler_params=_cparams(),
        )(w, g, m, v))
    spec = pl.BlockSpec((tr, cols), lambda i: (i, 0))
    outs = pl.pallas_call(
        body, name=name, grid=(rows // tr,),
        in_specs=[spec] * 4, out_specs=[spec] * 3,
        out_shape=[jax.ShapeDtypeStruct((rows, cols), F32)] * 3,
        compiler_params=_cparams(),
    )(*[t.reshape(rows, cols) for t in (w, g, m, v)])
    return tuple(o.reshape(shape) for o in outs)


def _adamw_math(w, g, m, v):
    c1 = 1.0 - ADAM_B1 ** ADAM_STEP
    c2 = 1.0 - ADAM_B2 ** ADAM_STEP
    nm = ADAM_B1 * m + (1.0 - ADAM_B1) * g
    nv = ADAM_B2 * v + (1.0 - ADAM_B2) * (g * g)
    delta = -ADAM_LR * ((nm / c1) / (jnp.sqrt(nv / c2) + ADAM_EPS) + ADAM_WD * w)
    return delta, nm, nv


def _adamw_layers(w, m, v, bufs, row_off, *, name, tr=256):
    depth, rows, cols = w.shape
    tr = min(tr, rows)
    assert rows % tr == 0 and row_off % tr == 0

    outs = None
    for l in range(depth):
        def body(w_ref, g_ref, m_ref, v_ref, *rest):
            go_ref, d_ref, nm_ref, nv_ref = rest[-4:]
            g = g_ref[...]
            go_ref[...] = g
            d_ref[...], nm_ref[...], nv_ref[...] = _adamw_math(w_ref[...], g, m_ref[...], v_ref[...])

        layer = pl.BlockSpec((None, tr, cols), lambda i, l=l: (l, i, 0))
        prev = () if outs is None else tuple(outs)
        outs = pl.pallas_call(
            body, name=f"{name}_l{l}", grid=(rows // tr,),
            in_specs=[layer, pl.BlockSpec((tr, cols), lambda i: (row_off // tr + i, 0)), layer, layer]
            + [ANY_SPEC] * len(prev),
            out_specs=[layer] * 4,
            out_shape=[jax.ShapeDtypeStruct(w.shape, F32)] * 4,
            input_output_aliases={4 + k: k for k in range(len(prev))},
            compiler_params=_cparams(),
        )(w, bufs[l], m, v, *prev)
    return tuple(outs)


def _sum_leading(x, *, name, tr=256):
    n, rows, cols = x.shape
    tr = _pick_rows(rows, tr)

    def body(x_ref, o_ref):
        acc = x_ref[0]
        for k in range(1, n):
            acc = acc + x_ref[k]
        o_ref[...] = acc

    return pl.pallas_call(
        body, name=name, grid=(rows // tr,),
        in_specs=[pl.BlockSpec((n, tr, cols), lambda i: (0, i, 0))],
        out_specs=pl.BlockSpec((tr, cols), lambda i: (i, 0)),
        out_shape=jax.ShapeDtypeStruct((rows, cols), F32),
        compiler_params=_cparams(),
    )(x)


def _position():
    return lax.axis_index("x"), lax.axis_index("y"), lax.axis_index("c")


def _allgather8(x, *, name):
    shape = x.shape

    def body(x_ref, out_ref, send_sems, recv_sems, local_sem):
        px, py, pc = _position()
        me, sibling = (px, py, pc), (px, py, 1 - pc)
        chips = [(1 - px, py), (px, 1 - py), (1 - px, 1 - py)]
        src_own = x_ref

        def slot(qx, qy, qc):
            return out_ref.at[4 * qx + 2 * qy + qc]

        def copy(k, block, to, src=None):
            return pltpu.make_async_remote_copy(
                src_ref=slot(*block) if src is None else src, dst_ref=slot(*block),
                send_sem=send_sems.at[k], recv_sem=recv_sems.at[k], device_id=to, device_id_type=MESH)

        mine = pltpu.make_async_copy(src_own, slot(*me), local_sem)
        mine.start()
        first = [copy(0, me, sibling, src=src_own)]
        first += [copy(1 + j, me, (*chip, pc), src=src_own) for j, chip in enumerate(chips)]
        for cp in first:
            cp.start()
        passed = [copy(4 + j, (*chip, pc), sibling) for j, chip in enumerate(chips)]
        for j, chip in enumerate(chips):
            copy(1 + j, (*chip, pc), me).wait_recv()
            passed[j].start()
        copy(0, sibling, me).wait_recv()
        for j, chip in enumerate(chips):
            copy(4 + j, (*chip, 1 - pc), me).wait_recv()
        for cp in first + passed:
            cp.wait_send()
        mine.wait()

    return pl.pallas_call(
        body, name=name,
        out_shape=jax.ShapeDtypeStruct((N_DEV,) + shape, x.dtype),
        in_specs=[pl.BlockSpec(memory_space=pl.ANY)],
        out_specs=pl.BlockSpec(memory_space=pl.ANY),
        scratch_shapes=[pltpu.SemaphoreType.DMA((7,)), pltpu.SemaphoreType.DMA((7,)), pltpu.SemaphoreType.DMA],
    )(x)


class _Exchange:
    def __init__(self, ins, out_shapes, n, build, aliases=None):
        self.ins, self.out_shapes, self.n, self.build = tuple(ins), tuple(out_shapes), n, build
        self.aliases = dict(aliases or {})

    def _descriptors(self, in_refs, out_refs, send_sems, recv_sems):
        sends, recvs = [], []
        for k, (src, dst, peer, landing) in enumerate(self.build(in_refs, out_refs)):
            sends.append(pltpu.make_async_remote_copy(
                src_ref=src, dst_ref=dst, send_sem=send_sems.at[k], recv_sem=recv_sems.at[k],
                device_id=peer, device_id_type=MESH))
            recvs.append(pltpu.make_async_remote_copy(
                src_ref=src, dst_ref=landing, send_sem=send_sems.at[k], recv_sem=recv_sems.at[k],
                device_id=peer, device_id_type=MESH))
        return sends, recvs

    def start(self, *refs):
        for cp in self._descriptors(*refs)[0]:
            cp.start()

    def finish(self, *refs):
        sends, recvs = self._descriptors(*refs)
        for cp in recvs:
            cp.wait_recv()
        for cp in sends:
            cp.wait_send()


ANY_SPEC = pl.BlockSpec(memory_space=pl.ANY)


def _hosted_call(body, *, name, grid, in_specs, out_specs, out_shape, args, scratch_shapes=(), side=None,
                 num_scalar_prefetch=0, io_aliases=None):
    in_specs, out_specs, out_shape = list(in_specs), list(out_specs), list(out_shape)
    n_in, n_out = len(in_specs) + num_scalar_prefetch, len(out_specs)
    kernel_body = body
    aliases = dict(io_aliases or {})
    if side is not None:
        s_in, s_out = len(side.ins), len(side.out_shapes)
        aliases.update({n_in + i: n_out + o for i, o in side.aliases.items()})

        def kernel_body(*refs):
            ins, s_ins = refs[:n_in], refs[n_in:n_in + s_in]
            outs = refs[n_in + s_in:n_in + s_in + n_out]
            s_outs = refs[n_in + s_in + n_out:n_in + s_in + n_out + s_out]
            scratch, sems = refs[n_in + s_in + n_out + s_out:-2], refs[-2:]
            first = functools.reduce(jnp.logical_and, [pl.program_id(a) == 0 for a in range(len(grid))])
            last = functools.reduce(jnp.logical_and, [pl.program_id(a) == g - 1 for a, g in enumerate(grid)])

            @pl.when(first)
            def _():
                side.start(s_ins, s_outs, *sems)

            body(*ins, *outs, *scratch)

            @pl.when(last)
            def _():
                side.finish(s_ins, s_outs, *sems)

        in_specs += [ANY_SPEC] * s_in
        out_specs += [ANY_SPEC] * s_out
        out_shape += list(side.out_shapes)
        scratch_shapes = list(scratch_shapes) + [pltpu.SemaphoreType.DMA((side.n,)),
                                                 pltpu.SemaphoreType.DMA((side.n,))]
        args = tuple(args) + side.ins
    if num_scalar_prefetch:
        grid_spec = pltpu.PrefetchScalarGridSpec(num_scalar_prefetch=num_scalar_prefetch, grid=grid,
                                                 in_specs=in_specs, out_specs=out_specs,
                                                 scratch_shapes=list(scratch_shapes))
        outs = pl.pallas_call(kernel_body, name=name, grid_spec=grid_spec, out_shape=out_shape,
                              input_output_aliases=aliases, compiler_params=_cparams())(*args)
    else:
        outs = pl.pallas_call(kernel_body, name=name, grid=grid, in_specs=in_specs, out_specs=out_specs,
                              out_shape=out_shape, scratch_shapes=list(scratch_shapes),
                              input_output_aliases=aliases, compiler_params=_cparams())(*args)
    return tuple(outs[:n_out]), tuple(outs[n_out:])


def _run_exchange(ex, *, name):
    s_in = len(ex.ins)

    def body(*refs):
        ins, outs, sems = refs[:s_in], refs[s_in:-2], refs[-2:]
        ex.start(ins, outs, *sems)
        ex.finish(ins, outs, *sems)

    outs = pl.pallas_call(
        body, name=name, out_shape=list(ex.out_shapes),
        in_specs=[ANY_SPEC] * s_in, out_specs=[ANY_SPEC] * len(ex.out_shapes),
        scratch_shapes=[pltpu.SemaphoreType.DMA((ex.n,)), pltpu.SemaphoreType.DMA((ex.n,))],
        input_output_aliases=ex.aliases,
    )(*ex.ins)
    return tuple(outs)


def _both(a, b):
    na, oa = len(a.ins), len(a.out_shapes)

    def build(ins, outs):
        return a.build(ins[:na], outs[:oa]) + b.build(ins[na:], outs[oa:])

    aliases = dict(a.aliases)
    aliases.update({na + i: oa + o for i, o in b.aliases.items()})
    return _Exchange(a.ins + b.ins, a.out_shapes + b.out_shapes, a.n + b.n, build, aliases)


def _other_chips(px, py):
    return [(px, 1 - py), (1 - px, py), (1 - px, 1 - py)]


def _gather_spread(w_flat, halves=True):
    rows, w = w_flat.shape
    hr = rows // 2 if halves else rows

    def build(ins, outs):
        px, py, pc = _position()
        mine = ins[0].at[pl.ds(pc * hr, hr)] if halves else ins[0]
        me = 4 * px + 2 * py + pc
        plan = [((px, py, 1 - pc), me ^ 1)]
        plan += [((qx, qy, pc), 4 * qx + 2 * qy + pc) for qx, qy in _other_chips(px, py)]
        return [(mine, outs[0].at[me], peer, outs[0].at[their]) for peer, their in plan]

    return _Exchange([w_flat], [jax.ShapeDtypeStruct((N_DEV, hr, w), w_flat.dtype)], 4, build)


def _gather_pass_on(gath):
    def build(ins, outs):
        px, py, pc = _position()
        out = []
        for qx, qy in _other_chips(px, py):
            blk = 4 * qx + 2 * qy + pc
            out.append((outs[0].at[blk], outs[0].at[blk], (px, py, 1 - pc), outs[0].at[blk ^ 1]))
        return out

    return _Exchange([gath], [jax.ShapeDtypeStruct(gath.shape, gath.dtype)], 3, build, aliases={0: 0})


def _rs_halves(g):
    n, rows, w = g.shape
    hr = rows // 2

    def build(ins, outs):
        px, py, pc = _position()
        return [(ins[0].at[:, pl.ds((1 - pc) * hr, hr), :], outs[0], (px, py, 1 - pc), outs[0])]

    return _Exchange([g], [jax.ShapeDtypeStruct((n, hr, w), g.dtype)], 1, build)


def _rs_chips(sb):
    def build(ins, outs):
        px, py, pc = _position()
        return [(ins[0].at[j], outs[0].at[j], (qx, qy, pc), outs[0].at[j])
                for j, (qx, qy) in enumerate(_other_chips(px, py))]

    return _Exchange([sb], [jax.ShapeDtypeStruct(sb.shape, sb.dtype)], 3, build)


def _rs_complete(buf):
    def build(ins, outs):
        px, py, pc = _position()
        return [(outs[0].at[pc], outs[0].at[pc], (px, py, 1 - pc), outs[0].at[1 - pc])]

    return _Exchange([buf], [jax.ShapeDtypeStruct(buf.shape, buf.dtype)], 1, build, aliases={0: 0})


def _rs_partial(g, recv, ids, *, name, tr=128):
    _, rows, w = g.shape
    hr = rows // 2
    nb = hr // tr

    def body(ids_ref, g_ref, r_ref, o_ref):
        o_ref[0] = (g_ref[0] + r_ref[0]).astype(BF16)

    grid_spec = pltpu.PrefetchScalarGridSpec(
        num_scalar_prefetch=1, grid=(3, nb),
        in_specs=[pl.BlockSpec((1, tr, w), lambda j, i, ids: (ids[1] ^ (j + 1), ids[0] * nb + i, 0)),
                  pl.BlockSpec((1, tr, w), lambda j, i, ids: (ids[1] ^ (j + 1), i, 0))],
        out_specs=pl.BlockSpec((1, tr, w), lambda j, i, ids: (j, i, 0)))
    return pl.pallas_call(
        body, name=name, grid_spec=grid_spec,
        out_shape=jax.ShapeDtypeStruct((3, hr, w), BF16),
        compiler_params=_cparams(),
    )(ids, g, recv)


def _rs_total(g, recv, got, ids, *, name, tr=128):
    _, rows, w = g.shape
    hr = rows // 2
    nb = hr // tr

    def body(ids_ref, g_ref, r_ref, got_ref, o_ref):
        acc = g_ref[0] + r_ref[0]
        for j in range(3):
            acc = acc + got_ref[j].astype(F32)
        o_ref[0] = acc

    grid_spec = pltpu.PrefetchScalarGridSpec(
        num_scalar_prefetch=1, grid=(nb,),
        in_specs=[pl.BlockSpec((1, tr, w), lambda i, ids: (ids[1], ids[0] * nb + i, 0)),
                  pl.BlockSpec((1, tr, w), lambda i, ids: (ids[1], i, 0)),
                  pl.BlockSpec((3, tr, w), lambda i, ids: (0, i, 0))],
        out_specs=pl.BlockSpec((1, tr, w), lambda i, ids: (ids[0], i, 0)))
    return pl.pallas_call(
        body, name=name, grid_spec=grid_spec,
        out_shape=jax.ShapeDtypeStruct((2, hr, w), F32),
        compiler_params=_cparams(),
    )(ids, g, recv, got)


class _ReduceScatter:
    def __init__(self, g, ids, tag):
        self.g, self.ids, self.tag, self.stage, self.result = g, ids, tag, 0, None

    def next_exchange(self):
        if self.stage == 0:
            return _rs_halves(self.g)
        if self.stage == 1:
            return _rs_chips(self.sb)
        return _rs_complete(self.buf)

    def done(self, outs):
        if self.stage == 0:
            self.recv = outs[0]
            hr = self.recv.shape[1]
            self.tr = max(t for t in range(16, 513, 16) if hr % t == 0)
            self.sb = _rs_partial(self.g, self.recv, self.ids, name=f"{self.tag}_partial", tr=self.tr)
        elif self.stage == 1:
            self.buf = _rs_total(self.g, self.recv, outs[0], self.ids, name=f"{self.tag}_total", tr=self.tr)
        else:
            _, hr, w = outs[0].shape
            self.result = outs[0].reshape(2 * hr, w)
        self.stage += 1

    def finish_alone(self):
        names = ("halves", "chips", "complete")
        while self.stage < 3:
            self.done(_run_exchange(self.next_exchange(), name=f"{self.tag}_{names[self.stage]}"))
        return self.result


def _flat_rows():
    used = sum(r for _, r in FSDP_SECTIONS)
    return used, -(-used // ROW_ALIGN) * ROW_ALIGN


def _cols_to_chunks(full):
    rows, cols = full.shape
    t = full.reshape(rows, N_CHIPS, cols // N_CHIPS).transpose(1, 0, 2)
    return t.reshape(N_CHIPS, -1, FLAT_W)


def _chunks_to_cols(chunks, rows, cols):
    return chunks.reshape(N_CHIPS, rows, cols // N_CHIPS).transpose(1, 0, 2).reshape(rows, cols)


def _pad_heads(w, real):
    lead = w.shape[:-1]
    t = w.reshape(lead + (HEADS, real))
    t = jnp.pad(t, [(0, 0)] * len(lead) + [(0, 0), (0, HEAD_PAD - real)])
    return t.reshape(lead + (HEADS * HEAD_PAD,))


def _unpad_heads(w, real):
    lead = w.shape[:-1]
    return w.reshape(lead + (HEADS, HEAD_PAD))[..., :real].reshape(lead + (HEADS * real,))


def _pad_value_lanes(w, axis):
    w = jnp.moveaxis(w, axis, -1)
    lead = w.shape[:-1]
    t = w.reshape(lead + (HEADS, 64))
    t = jnp.pad(t, [(0, 0)] * len(lead) + [(0, 0), (HEAD_PAD - 64, 0)])
    return jnp.moveaxis(t.reshape(lead + (HEADS * HEAD_PAD,)), -1, axis)


def _unpad_value_lanes(w, axis):
    w = jnp.moveaxis(w, axis, -1)
    lead = w.shape[:-1]
    t = w.reshape(lead + (HEADS, HEAD_PAD))[..., HEAD_PAD - 64:]
    return jnp.moveaxis(t.reshape(lead + (HEADS * 64,)), -1, axis)


def _pad_w_in_t(wt):
    z = jnp.zeros((NOPE, wt.shape[1]), wt.dtype)
    z2 = jnp.zeros((HEAD_PAD - NOPE - ROPE, wt.shape[1]), wt.dtype)
    return jnp.concatenate([wt[:1408], z, wt[1408:], z2], axis=0)


def _unpad_w_in_t(wt):
    return jnp.concatenate([wt[:1408], wt[1408 + NOPE:1408 + NOPE + ROPE]], axis=0)


def _rope_tables(positions):
    freqs = ROPE_THETA ** (-jnp.arange(0, ROPE, 2, dtype=F32) / ROPE)
    ang = positions.astype(F32)[..., None] * freqs
    cos, sin = jnp.cos(ang), jnp.sin(ang)
    lead = cos.shape[:-1]
    ones = jnp.ones(lead + (NOPE,), F32)
    zeros_n = jnp.zeros(lead + (NOPE,), F32)
    zeros_p = jnp.zeros(lead + (HEAD_PAD - NOPE - ROPE,), F32)
    ctab = jnp.concatenate([ones, cos, cos, zeros_p], axis=-1)
    stab = jnp.concatenate([zeros_n, -sin, sin, zeros_p], axis=-1)
    return ctab, stab


def _mix_weights(full):
    return dict(
        w_in_t=_pad_w_in_t(full["w_in_t"]),
        w_uq=_pad_heads(full["mla_w_uq"], NOPE + ROPE),
        w_ukv=full["mla_w_ukv"],
        w_out=jnp.concatenate([_pad_value_lanes(full["w_out"][D_GMLP:], 0), full["w_out"][:D_GMLP]], axis=0),
    )


def _small_weights(p, l):
    ws = p["gmlp_ws"][l]
    tril = jnp.tril(jnp.ones((CHUNK, CHUNK), bool))
    bs = p["gmlp_bs"][l]
    bexp = jnp.repeat(bs.reshape(GROUPS // 2, 2, CHUNK).transpose(0, 2, 1), GROUP_DIM, axis=2)
    return dict(
        ws=ws,
        wst=jnp.where(tril[None], ws, 0.0).transpose(0, 2, 1).astype(BF16),
        bexp=bexp,
        g_mix=p["norm_mix_g"][l][None],
        g_ffn=p["norm_ffn_g"][l][None],
        g_q=p["mla_q_norm_g"][l][None],
        g_kv=p["mla_kv_norm_g"][l][None],
        g_og=p["out_norm_gmlp_g"][l][None],
        g_oa=_pad_value_lanes(p["out_norm_mla_g"][l], 0)[None],
    )


def _local_step(x3, target3, positions, mods, final_g, plan):
    bsz, seq, d = x3.shape
    tok = bsz * seq
    tmt = min(512, seq)
    tmk = min(1024, seq)
    tmw = min(2048, tok)
    chunk = (None, None, FLAT_W, FLAT_W)
    chunk2 = (2, None, FLAT_W, FLAT_W)
    ff_grad_shape = (N_CHIPS, 2 * FLAT_W, FLAT_W)
    ctab, stab = _rope_tables(positions)
    lw = [None] * DEPTH

    def flat(t):
        return t.reshape(tok, t.shape[-1])

    def cube(t):
        return t.reshape(bsz, seq, t.shape[-1])

    def carrying(l, tag, fn, *args, **kw):
        side = plan.host(l, tag)
        if side is None:
            return fn(*args, **kw)
        res, side_outs = fn(*args, side=side, **kw)
        plan.hosted(l, tag, side_outs)
        return res

    saved = []
    x = x3
    for l in range(DEPTH):
        lw[l] = plan.layer(l)
        w, mod = lw[l], mods[l]
        if l == 0:
            h1 = carrying(l, "fwd_normmod1", _normmod_fwd, x, w["g_mix"], mod, SHIFT1, SCALE1,
                          name=f"l{l}_normmod1")
        else:
            h1 = h1_next
        z = cube(_mm(flat(h1), w["w_in_t"], dims="nt", name=f"l{l}_w_in", tm=tmt, tn=D_IN_PAD, tk=d,
                     out_dtypes=(BF16,)))
        yg = _gmlp_fwd(z, w["ws"], w["bexp"], w["g_og"], name=f"l{l}_gmlp_fwd")
        q, kv, kp = _mla_prep_fwd(z, w["g_q"], w["g_kv"], w["w_uq"], w["w_ukv"], ctab, stab, name=f"l{l}_mla_prep")
        o, lse = carrying(l, "fwd_attn", _attn_fwd, q, kv, kp, name=f"l{l}_attn_fwd")
        y = _onorm_fwd(o, yg, w["g_oa"], name=f"l{l}_onorm_fwd")

        def normmod(xv, gv, gm, shift_row, scale_row):
            m = gm[0]
            return _rms_fwd(xv, gv, d) * (1.0 + m[scale_row:scale_row + 1, :]) + m[shift_row:shift_row + 1, :]

        def out_epi(po, xv, gm, gf):
            x_new = xv + gm[0][GATE1:GATE1 + 1, :] * po
            return po, x_new, normmod(x_new, gf, gm, SHIFT2, SCALE2)

        vec_spec = pl.BlockSpec((1, d), lambda i, j, k: (0, j))
        po, x_mid, h2 = carrying(l, "fwd_out_a", _mm, flat(y), w["w_out"], dims="nn", name=f"l{l}_w_out",
                                 tm=tmt, tn=d, tk=y.shape[-1], out_dtypes=(BF16, F32, BF16), epilogue=out_epi,
                                 extras=(flat(x), mod, w["g_ffn"]),
                                 extra_specs=(None, _mod_spec(tmt, d, seq), vec_spec))
        x_mid, h2 = cube(x_mid), cube(h2)

        def act_epi(acc):
            r = jnp.maximum(acc, 0.0)
            return (r * r,)

        r = carrying(l, "fwd_ff1", _mm, flat(h2), w["ff"], dims="nn", name=f"l{l}_w_ff1", tm=tmw, tn=FLAT_W,
                     tk=d, out_dtypes=(BF16,), epilogue=act_epi, weights_outer=True, n=D_FF,
                     b_block=(chunk, lambda i, j, k: (j, 0, 0, 0)))

        more = l + 1 < DEPTH

        def ff2_epi(acc, xv, gm, *nxt):
            x_new = xv + gm[0][GATE2:GATE2 + 1, :] * acc
            return (acc, x_new) + ((normmod(x_new, nxt[1], nxt[0], SHIFT1, SCALE1),) if more else ())

        mod_spec = _mod_spec(tmt, d, seq)
        outs = carrying(l, "fwd_ff2", _mm, r, w["ff"], dims="nn", name=f"l{l}_w_ff2", tm=tmt, tn=d, tk=2 * FLAT_W,
                        out_dtypes=(BF16, F32) + ((BF16,) if more else ()), epilogue=ff2_epi,
                        extras=(flat(x_mid), mod) + ((mods[l + 1], plan.layer(l + 1)["g_mix"]) if more else ()),
                        extra_specs=(None, mod_spec) + ((mod_spec, vec_spec) if more else ()), n=d,
                        b_block=(chunk2, lambda i, j, k: (k, 1, 0, 0)))
        f, x_out = outs[0], outs[1]
        h1_next = cube(outs[2]) if more else None
        saved.append(dict(x_in=x, h1=h1, z=z, q=q, kv=kv, kp=kp, o=o, lse=lse, y=y, po=cube(po),
                          x_mid=x_mid, h2=h2, r=r, f=cube(f)))
        x = cube(x_out)

    grads = [dict() for _ in range(DEPTH)]
    dmods = [None] * DEPTH
    top = DEPTH - 1
    node = _resnode_bwd(x, final_g[None], name="final_loss_bwd", target3=target3,
                        branch3=saved[top]["f"], mod_gate=mods[top], gate_row=GATE2)
    loss_part = node["loss"][0, 0]
    d_final_g = node["dg"][0]
    plan.scalars(loss_part, d_final_g)
    for l in range(DEPTH - 1, -1, -1):
        w, mod, s = lw[l], mods[l], saved[l]
        dx_out, dfb, dgate2 = node["dx"], flat(node["dbr"]), node["dgate"][:, 0]

        def dact_epi(acc, rv):
            return (acc * (2.0 * jnp.sqrt(rv.astype(F32))),)

        da = carrying(l, "bwd_d_r", _mm, dfb, w["ff"], dims="nt", name=f"l{l}_d_r", tm=tmw, tn=FLAT_W, tk=d,
                      out_dtypes=(BF16,), epilogue=dact_epi, extras=(s["r"],), weights_outer=True, n=D_FF,
                      b_block=(chunk, lambda i, j, k: (j, 1, 0, 0)))
        g_ff = carrying(l, "bwd_dw_ff2", _mm, s["r"], dfb, dims="tn", name=f"l{l}_dw_ff2", tm=FLAT_W, tn=d,
                        tk=2048, out_into=(ff_grad_shape, (None, FLAT_W, FLAT_W), lambda i, j, k: (i, 1, 0), None))
        g_ff = carrying(l, "bwd_dw_ff1", _mm, flat(s["h2"]), da, dims="tn", name=f"l{l}_dw_ff1", tm=d, tn=FLAT_W,
                        tk=2048, out_into=(ff_grad_shape, (None, FLAT_W, FLAT_W), lambda i, j, k: (j, 0, 0), g_ff))
        plan.ff_grads(l, g_ff)
        dh2 = carrying(l, "bwd_d_h2", _mm, da, w["ff"], dims="nt", name=f"l{l}_d_h2", tm=tmk, tn=d, tk=2 * FLAT_W,
                       n=d, b_block=(chunk2, lambda i, j, k: (k, 0, 0, 0)), out_dtypes=(BF16,))
        node = carrying(l, "bwd_resnode_ffn", _resnode_bwd, s["x_mid"], w["g_ffn"], name=f"l{l}_resnode_ffn",
                        dh3=cube(dh2), dres3=dx_out, mod_nm=mod, rows=(SHIFT2, SCALE2), branch3=s["po"],
                        mod_gate=mod, gate_row=GATE1)
        grads[l]["norm_ffn_g"] = node["dg"][0]
        dshift2, dscale2 = node["dnm"][:, 0], node["dnm"][:, 1]
        dx_mid, dpo, dgate1 = node["dx"], flat(node["dbr"]), node["dgate"][:, 0]

        wy = s["y"].shape[-1]
        dy = cube(carrying(l, "bwd_d_y", _mm, dpo, w["w_out"], dims="nt", name=f"l{l}_d_y", tm=tmt, tn=wy, tk=d,
                           out_dtypes=(BF16,)))
        dw_out = _mm(flat(s["y"]), dpo, dims="tn", name=f"l{l}_dw_out", tm=wy // 3, tn=d, tk=2048)
        hw = HEADS * HEAD_PAD
        grads[l]["w_out"] = jnp.concatenate([dw_out[hw:], _unpad_value_lanes(dw_out[:hw], 0)], axis=0)

        dz, dws, dbs, dg_og = _gmlp_bwd(s["z"], dy, w["ws"], w["wst"], w["bexp"], w["g_og"],
                                        name=f"l{l}_gmlp_bwd", dy_col=hw // D_GMLP)
        grads[l]["gmlp_ws"], grads[l]["gmlp_bs"], grads[l]["out_norm_gmlp_g"] = dws, dbs, dg_og[0]

        do, dl, dg_oa = _onorm_bwd(s["o"], dy, w["g_oa"], name=f"l{l}_onorm_bwd")
        grads[l]["out_norm_mla_g"] = _unpad_value_lanes(dg_oa[0], 0)
        plan.small_ready(l, grads[l])
        dq, dk, dv = carrying(l, "bwd_attn_dkv", _attn_bwd, s["q"], s["kv"], s["kp"], do, s["lse"], dl,
                              name=f"l{l}_attn_bwd")
        dz, cq, dqb, ckv, dkvb, dg_q, dg_kv = _mla_prep_bwd(
            s["z"], dz, dq, dk, dv, w["g_q"], w["g_kv"], w["w_uq"], w["w_ukv"], ctab, stab,
            name=f"l{l}_mla_prep_bwd")
        grads[l]["mla_q_norm_g"], grads[l]["mla_kv_norm_g"] = dg_q[0], dg_kv[0]
        dw_uq = carrying(l, "bwd_dw_uq", _mm, flat(cq), flat(dqb), dims="tn", name=f"l{l}_dw_uq", tm=Q_RANK,
                         tn=1024, tk=4096)
        grads[l]["mla_w_uq"] = _unpad_heads(dw_uq, NOPE + ROPE)
        grads[l]["w_in_t"] = _unpad_w_in_t(carrying(l, "bwd_dw_in", _mm, flat(dz), flat(s["h1"]), dims="tn",
                                                    name=f"l{l}_dw_in", tm=D_IN_PAD // 2, tn=d, tk=2048))
        grads[l]["mla_w_ukv"] = carrying(l, "bwd_dw_ukv", _mm, flat(ckv), flat(dkvb), dims="tn", name=f"l{l}_dw_ukv",
                                         tm=KV_RANK, tn=1024, tk=4096)
        plan.layer_grads(l, grads[l])
        dh1 = carrying(l, "bwd_d_h1", _mm, flat(dz), w["w_in_t"], dims="nn", name=f"l{l}_d_h1", tm=tmt, tn=d,
                       tk=D_IN_PAD, out_dtypes=(BF16,))
        below = dict(branch3=saved[l - 1]["f"], mod_gate=mods[l - 1], gate_row=GATE2) if l > 0 else {}
        node = carrying(l, "bwd_resnode_mix", _resnode_bwd, s["x_in"], w["g_mix"], name=f"l{l}_resnode_mix",
                        dh3=cube(dh1), dres3=dx_mid, mod_nm=mod, rows=(SHIFT1, SCALE1), **below)
        grads[l]["norm_mix_g"] = node["dg"][0]
        dshift1, dscale1 = node["dnm"][:, 0], node["dnm"][:, 1]
        dmods[l] = jnp.stack([dshift1, dscale1, dgate1, dshift2, dscale2, dgate2], axis=1)
        plan.layer_done(l)
    return node["dx"], dmods


W_NAMES = ("w_ada", "b_ada", "norm_mix_g", "w_in", "gmlp_ws", "gmlp_bs", "mla_q_norm_g", "mla_kv_norm_g",
           "mla_w_uq", "mla_w_ukv", "out_norm_gmlp_g", "out_norm_mla_g", "w_out", "norm_ffn_g", "w_ff1", "w_ff2",
           "final_norm_g")
FLAT_KEY = {"w_in": "w_in", "w_uq": "mla_w_uq", "w_ukv": "mla_w_ukv", "w_out": "w_out", "w_ff1": "w_ff1",
            "w_ff2": "w_ff2"}
COL_SHARDED = ("w_in", "w_uq", "w_ukv", "w_ff1")
FULL_SHAPE = {"w_in": (D_MODEL, D_IN), "w_uq": (Q_RANK, HEADS * (NOPE + ROPE)), "w_ukv": (KV_RANK, HEADS * 128),
              "w_out": (D_MODEL, D_MODEL)}
SMALL_LAYER_NAMES = ("gmlp_ws", "gmlp_bs", "out_norm_gmlp_g", "out_norm_mla_g", "norm_ffn_g")
LATE_SMALL_NAMES = ("norm_mix_g", "mla_q_norm_g", "mla_kv_norm_g")


def _silu(v):
    return v * (1.0 / (1.0 + jnp.exp(-v)))


class _CommPlan:
    FWD = {"fwd_attn": ("ff", 0, "spread"), "fwd_out_a": ("ff", 0, "pass"),
           "fwd_ff1": ("mix", 1, "spread"), "fwd_ff2": ("mix", 1, "pass")}
    BWD = {"bwd_d_r": ("mix", 1), "bwd_dw_ff2": ("mix", 1), "bwd_dw_ff1": ("mix", 1),
           "bwd_d_h2": ("ff", 0), "bwd_attn_dkv": ("ff", 0), "bwd_dw_uq": ("ff", 0)}
    BWD_LAST = {"bwd_d_h1": ("mix", 0), "bwd_resnode_mix": ("mix", 0)}
    SMALL = {"bwd_attn_dkv": "spread", "bwd_dw_uq": "pass"}

    def __init__(self, weights, ids, dev, core):
        self.weights, self.ids, self.dev, self.core = weights, ids, dev, core
        self.used, self.rows = _flat_rows()
        self.flat = {("mix", l): self._flat_mix(l) for l in range(DEPTH)}
        self.flat.update({("ff", l): jnp.concatenate([weights["w_ff1"][l], weights["w_ff2"][l]], axis=0).astype(BF16)
                          for l in range(DEPTH)})
        self.lw, self.rs, self.grads, self.spread = {}, {}, {}, {}
        self.small_vec, self.small_sum, self.small_spread, self.extra = {}, {}, None, {}
        self.lw = {l: _small_weights(weights, l) for l in range(DEPTH)}

    def _flat_mix(self, l):
        pieces = []
        for nm, _ in FSDP_SECTIONS:
            shard = self.weights[FLAT_KEY[nm]][l]
            pieces.append(shard.T if nm == "w_in" else shard.reshape(-1, FLAT_W))
        pieces.append(jnp.zeros((self.rows - self.used, FLAT_W), F32))
        return jnp.concatenate(pieces, axis=0).astype(BF16)

    def _arrived(self, group, l, gath):
        flat = self.flat[group, l]
        hr = flat.shape[0] // 2
        mine = lax.dynamic_slice(flat, (self.core * hr, 0), (hr, FLAT_W))
        gath = lax.dynamic_update_slice(gath, mine[None], (self.dev, 0, 0))
        if group == "ff":
            self.lw[l]["ff"] = gath.reshape(N_CHIPS, 2, hr, FLAT_W)
            return
        w_gath = gath.reshape(N_CHIPS, self.rows, FLAT_W)
        full, off = {}, 0
        for nm, nrows in FSDP_SECTIONS:
            sec = w_gath[:, off:off + nrows]
            off += nrows
            rows, cols = FULL_SHAPE[nm]
            if nm == "w_in":
                full["w_in_t"] = sec.reshape(cols, rows)
            else:
                full[FLAT_KEY[nm]] = (_chunks_to_cols(sec, rows, cols) if nm in COL_SHARDED
                                      else sec.reshape(rows, cols))
        self.lw[l].update(_mix_weights(full))

    def layer(self, l):
        return self.lw[l]

    def host(self, l, tag):
        if tag == "fwd_normmod1":
            return _gather_spread(self.flat["mix", 0]) if l == 0 else None
        if tag in self.FWD:
            group, ahead, what = self.FWD[tag]
            if l + ahead >= DEPTH:
                return None
            return _gather_spread(self.flat[group, l + ahead]) if what == "spread" else _gather_pass_on(self.spread[group])
        rs = self._rs_for(l, tag)
        ex = None if rs is None or rs.stage > 2 else rs.next_exchange()
        if tag in self.SMALL:
            small = (_gather_spread(self.small_vec[l], halves=False) if self.SMALL[tag] == "spread"
                     else _gather_pass_on(self.small_spread))
            ex = small if ex is None else _both(ex, small)
        return ex

    def _rs_for(self, l, tag):
        if tag in self.BWD_LAST:
            return self.rs.get(self.BWD_LAST[tag]) if l == 0 else None
        if tag not in self.BWD:
            return None
        group, ahead = self.BWD[tag]
        return self.rs.get((group, l + ahead))

    def hosted(self, l, tag, outs):
        if tag == "fwd_normmod1":
            self._arrived("mix", 0, _run_exchange(_gather_pass_on(outs[0]), name="l0_mix_gather_pass_on")[0])
        elif tag in self.FWD:
            group, ahead, what = self.FWD[tag]
            if what == "spread":
                self.spread[group] = outs[0]
            else:
                self._arrived(group, l + ahead, outs[0])
        else:
            rs = self._rs_for(l, tag)
            if rs is not None and rs.stage <= 2:
                rs.done(outs[:1])
                outs = outs[1:]
            if tag in self.SMALL:
                if self.SMALL[tag] == "spread":
                    self.small_spread = outs[0]
                else:
                    self._small_arrived(l, outs[0])

    def ff_grads(self, l, g_ff):
        self.rs["ff", l] = _ReduceScatter(g_ff, self.ids, f"l{l}_ff_rs")

    def layer_grads(self, l, grads):
        self.grads[l] = grads
        pieces = []
        for nm, nrows in FSDP_SECTIONS:
            if nm == "w_in":
                pieces.append(grads["w_in_t"].reshape(N_CHIPS, nrows, FLAT_W))
                continue
            g = grads[FLAT_KEY[nm]]
            pieces.append(_cols_to_chunks(g) if nm in COL_SHARDED else g.reshape(N_CHIPS, nrows, FLAT_W))
        pieces.append(jnp.zeros((N_CHIPS, self.rows - self.used, FLAT_W), F32))
        self.rs["mix", l] = _ReduceScatter(jnp.concatenate(pieces, axis=1), self.ids, f"l{l}_mix_rs")

    def scalars(self, loss_part, d_final_g):
        self.extra = {0: [loss_part[None]]}
        self.extra.setdefault(DEPTH - 1, []).insert(0, d_final_g)

    def small_ready(self, l, grads):
        parts = [grads[nm].reshape(-1) for nm in SMALL_LAYER_NAMES] + self.extra.get(l, [])
        vec = jnp.concatenate(parts)
        rows = -(-vec.shape[0] // (8 * FLAT_W)) * 8
        self.small_vec[l] = jnp.pad(vec, (0, rows * FLAT_W - vec.shape[0])).reshape(rows, FLAT_W)

    def layer_done(self, l):
        if l == 0:
            self.rs["mix", 0].finish_alone()

    def _small_arrived(self, l, gath):
        gath = lax.dynamic_update_slice(gath, self.small_vec[l][None], (self.dev, 0, 0))
        self.small_sum[l] = _sum_leading(gath, name=f"l{l}_small_sum").reshape(-1)

    def small_grads(self):
        out = {nm: [] for nm in SMALL_LAYER_NAMES}
        for l in range(DEPTH):
            off = 0
            for nm in SMALL_LAYER_NAMES:
                size = self.weights[nm][l].size
                out[nm].append(self.small_sum[l][off:off + size].reshape(self.weights[nm].shape[1:]))
                off += size
            if l == DEPTH - 1:
                final = self.small_sum[l][off:off + self.weights["final_norm_g"].size]
                off += final.shape[0]
            if l == 0:
                loss = self.small_sum[l][off]
        res = {nm: jnp.stack(parts, axis=0) for nm, parts in out.items()}
        res["final_norm_g"] = final
        late = jnp.concatenate([jnp.stack([self.grads[l][nm] for l in range(DEPTH)], axis=0).reshape(-1)
                                for nm in LATE_SMALL_NAMES])
        rows = -(-late.shape[0] // (8 * 128)) * 8
        late = jnp.pad(late, (0, rows * 128 - late.shape[0])).reshape(rows, 128)
        late = _sum_leading(_allgather8(late, name="gather_late_small"), name="late_small_sum").reshape(-1)
        off = 0
        for nm in LATE_SMALL_NAMES:
            size = self.weights[nm].size
            res[nm] = late[off:off + size].reshape(self.weights[nm].shape)
            off += size
        return loss, res

    def mix_grads(self):
        per = {FLAT_KEY[nm]: [] for nm, _ in FSDP_SECTIONS}
        for l in range(DEPTH):
            shard, off = self.rs["mix", l].result, 0
            for nm, nrows in FSDP_SECTIONS:
                key = FLAT_KEY[nm]
                sec = shard[off:off + nrows]
                per[key].append(sec.T if nm == "w_in" else sec.reshape(self.weights[key].shape[1:]))
                off += nrows
        return {key: jnp.stack(parts, axis=0) for key, parts in per.items()}

    def ff_shards(self):
        return [self.rs["ff", l].result for l in range(DEPTH)]


def kernel(x, c, positions, w_ada, b_ada, norm_mix_g, w_in, gmlp_ws, gmlp_bs, mla_q_norm_g, mla_kv_norm_g, mla_w_uq, mla_w_ukv, out_norm_gmlp_g, out_norm_mla_g, w_out, norm_ffn_g, w_ff1, w_ff2, final_norm_g, loss_target, m_w_ada, m_b_ada, m_norm_mix_g, m_w_in, m_gmlp_ws, m_gmlp_bs, m_mla_q_norm_g, m_mla_kv_norm_g, m_mla_w_uq, m_mla_w_ukv, m_out_norm_gmlp_g, m_out_norm_mla_g, m_w_out, m_norm_ffn_g, m_w_ff1, m_w_ff2, m_final_norm_g, v_w_ada, v_b_ada, v_norm_mix_g, v_w_in, v_gmlp_ws, v_gmlp_bs, v_mla_q_norm_g, v_mla_kv_norm_g, v_mla_w_uq, v_mla_w_ukv, v_out_norm_gmlp_g, v_out_norm_mla_g, v_w_out, v_norm_ffn_g, v_w_ff1, v_w_ff2, v_final_norm_g):
    weights = dict(w_ada=w_ada, b_ada=b_ada, norm_mix_g=norm_mix_g, w_in=w_in, gmlp_ws=gmlp_ws, gmlp_bs=gmlp_bs,
                   mla_q_norm_g=mla_q_norm_g, mla_kv_norm_g=mla_kv_norm_g, mla_w_uq=mla_w_uq, mla_w_ukv=mla_w_ukv,
                   out_norm_gmlp_g=out_norm_gmlp_g, out_norm_mla_g=out_norm_mla_g, w_out=w_out,
                   norm_ffn_g=norm_ffn_g, w_ff1=w_ff1, w_ff2=w_ff2, final_norm_g=final_norm_g)
    mom_m = dict(zip(W_NAMES, (m_w_ada, m_b_ada, m_norm_mix_g, m_w_in, m_gmlp_ws, m_gmlp_bs, m_mla_q_norm_g,
                               m_mla_kv_norm_g, m_mla_w_uq, m_mla_w_ukv, m_out_norm_gmlp_g, m_out_norm_mla_g,
                               m_w_out, m_norm_ffn_g, m_w_ff1, m_w_ff2, m_final_norm_g)))
    mom_v = dict(zip(W_NAMES, (v_w_ada, v_b_ada, v_norm_mix_g, v_w_in, v_gmlp_ws, v_gmlp_bs, v_mla_q_norm_g,
                               v_mla_kv_norm_g, v_mla_w_uq, v_mla_w_ukv, v_out_norm_gmlp_g, v_out_norm_mla_g,
                               v_w_out, v_norm_ffn_g, v_w_ff1, v_w_ff2, v_final_norm_g)))
    bsz, seq, d = x.shape
    px, py, pc = _position()
    chip = 2 * px + py
    dev = 2 * chip + pc
    ids = jnp.stack([pc, chip]).astype(jnp.int32)
    n_ex = N_DEV * bsz
    ada_cols = w_ada.shape[-1]

    c_all = _allgather8(c.reshape(bsz * d // 128, 128), name="gather_c").reshape(n_ex, d)
    mod_parts = []
    for l in range(DEPTH):
        bias = lax.dynamic_slice(b_ada[l], (chip * ada_cols,), (ada_cols,))[None]
        mod_parts.append(_mm(c_all, w_ada, dims="nn", name=f"l{l}_mod", tm=n_ex, tn=ada_cols, tk=d, n=ada_cols,
                             b_block=((None, d, ada_cols), lambda i, j, k, l=l: (l, k, j)),
                             epilogue=lambda acc, bv: (acc + bv,), extras=(bias,),
                             extra_specs=(pl.BlockSpec((1, ada_cols), lambda i, j, k: (0, j)),), a_fn=_silu))
    mod_g = _allgather8(jnp.concatenate(mod_parts, axis=0), name="gather_mod")
    mod_g = mod_g.reshape(N_CHIPS, 2, DEPTH, n_ex, ada_cols)[:, 0]
    mod_full = mod_g.transpose(1, 2, 0, 3).reshape(DEPTH, n_ex, N_CHIPS * ada_cols)
    mod_mine = lax.dynamic_slice(mod_full, (0, dev * bsz, 0), (DEPTH, bsz, N_MOD * d))
    mod_mine = jnp.pad(mod_mine.reshape(DEPTH, bsz, N_MOD, d), ((0, 0), (0, 0), (0, MOD_ROWS - N_MOD), (0, 0)))
    mods = [mod_mine[l] for l in range(DEPTH)]

    plan = _CommPlan(weights, ids, dev, pc)
    grad_x, dmods = _local_step(x, loss_target, positions, mods, final_norm_g, plan)
    grad = plan.mix_grads()

    loss, small = plan.small_grads()
    grad.update(small)

    dmod = jnp.stack(dmods, axis=1).reshape(bsz * DEPTH * N_MOD, d)
    dmod_all = _allgather8(dmod, name="gather_dmod").reshape(n_ex, DEPTH, N_MOD * d)
    gw, gb = [], []
    for l in range(DEPTH):
        dm = dmod_all[:, l]
        dm_cols = lax.dynamic_slice(dm, (0, chip * ada_cols), (n_ex, ada_cols))
        gw.append(_mm(c_all, dm_cols, dims="tn", name=f"l{l}_dw_ada", tm=d, tn=ada_cols, tk=n_ex, a_fn=_silu,
                      out_into=(w_ada.shape, (None, d, ada_cols), lambda i, j, k, l=l: (l, i, j),
                                gw[-1] if gw else None)))
        gb.append(_sum_leading(dm.reshape(n_ex, N_MOD * d // FLAT_W, FLAT_W), name=f"l{l}_db_ada").reshape(-1))
    grad["w_ada"] = gw[-1]
    grad["b_ada"] = jnp.stack(gb, axis=0)

    delta, new_m, new_v = {}, {}, {}
    ff_bufs = plan.ff_shards()
    for nm, row_off in (("w_ff1", 0), ("w_ff2", FLAT_W)):
        grad[nm], delta[nm], new_m[nm], new_v[nm] = _adamw_layers(
            weights[nm], mom_m[nm], mom_v[nm], ff_bufs, row_off, name=f"adamw_{nm}")
    for nm in W_NAMES:
        if nm not in delta:
            delta[nm], new_m[nm], new_v[nm] = _adamw(weights[nm], grad[nm], mom_m[nm], mom_v[nm],
                                                     name=f"adamw_{nm}")
    return (loss, grad_x, *[grad[nm] for nm in W_NAMES], *[delta[nm] for nm in W_NAMES],
            *[new_m[nm] for nm in W_NAMES], *[new_v[nm] for nm in W_NAMES])
```

```python
import functools
import math

import jax
import jax.numpy as jnp
from jax import lax
from jax.experimental import pallas as pl
from jax.experimental.pallas import tpu as pltpu

F32 = jnp.float32
BF16 = jnp.bfloat16

D_MODEL = 1024
DEPTH = 2
D_GMLP = 512
GROUPS = 8
GROUP_DIM = 64
CHUNK = 128
HEADS = 8
NOPE = 64
ROPE = 32
HEAD_PAD = 128
Q_RANK = 256
KV_RANK = 128
D_FF = 4096
N_MOD = 6
MOD_ROWS = 8
EPS = 1e-6
ROPE_THETA = 10000.0
D_IN = 1440
D_IN_PAD = 1536
ATTN_SCALE = (NOPE + ROPE) ** -0.5
LOG2E = math.log2(math.e)
SCALE_LOG2 = ATTN_SCALE * LOG2E
N_CHIPS = 4
N_DEV = 8

ADAM_LR = 0.001
ADAM_B1 = 0.9
ADAM_B2 = 0.999
ADAM_EPS = 1e-08
ADAM_WD = 0.01
ADAM_STEP = 10

VMEM_LIMIT = 48 * 1024 * 1024
FLAT_W = 1024
ROW_ALIGN = 256

NN = (((1,), (0,)), ((), ()))
NT = (((1,), (1,)), ((), ()))
TN = (((0,), (0,)), ((), ()))
MESH = pl.DeviceIdType.MESH

SHIFT1, SCALE1, GATE1, SHIFT2, SCALE2, GATE2 = range(6)

FSDP_SECTIONS = (("w_out", 256), ("w_in", 360), ("w_uq", 48), ("w_ukv", 32))


def _cparams(vmem=VMEM_LIMIT):
    return pltpu.CompilerParams(vmem_limit_bytes=vmem)


def _dot(a, b, dims=NN):
    return lax.dot_general(a, b, dims, preferred_element_type=F32)


def _iota(shape, axis):
    return lax.broadcasted_iota(jnp.int32, shape, axis)


def _gelu(x):
    k = math.sqrt(2.0 / math.pi)
    return 0.5 * x * (1.0 + jnp.tanh(k * (x + 0.044715 * (x * x * x))))


def _gelu_and_grad(x):
    k = math.sqrt(2.0 / math.pi)
    x2 = x * x
    t = jnp.tanh(k * (x + 0.044715 * (x2 * x)))
    half = 0.5 * (1.0 + t)
    return x * half, half + 0.5 * x * (1.0 - t * t) * (k * (1.0 + 3.0 * 0.044715 * x2))


def _rms_fwd(x, g, n):
    r = lax.rsqrt(jnp.sum(x * x, axis=-1, keepdims=True) * (1.0 / n) + EPS)
    return x * r * g


def _rms_bwd(x, g, dy, n):
    r = lax.rsqrt(jnp.sum(x * x, axis=-1, keepdims=True) * (1.0 / n) + EPS)
    xh = x * r
    dxh = dy * g
    dx = r * (dxh - xh * (jnp.sum(dxh * xh, axis=-1, keepdims=True) * (1.0 / n)))
    dg = jnp.sum(dy * xh, axis=0, keepdims=True)
    return dx, dg


def _pick_rows(rows, limit):
    if rows <= limit:
        return rows
    for t in range(limit, 7, -8):
        if rows % t == 0:
            return t
    return rows


def _mm(a, b, *, dims, name, tm=512, tn=1024, tk=1024, out_dtypes=(F32,), epilogue=None,
        extras=(), extra_specs=(), a_fn=None, weights_outer=False, side=None, b_block=None, n=None,
        out_into=None):
    if dims == "tn":
        kk, m = a.shape
    else:
        m, kk = a.shape
    if n is None:
        n = b.shape[0] if dims == "nt" else b.shape[1]
    tm, tn, tk = min(tm, m), min(tn, n), min(tk, kk)
    assert m % tm == 0 and n % tn == 0 and kk % tk == 0, (name, a.shape, b.shape, tm, tn, tk)
    ni, nj, nk = m // tm, n // tn, kk // tk

    def spec(shape, pick):
        if weights_outer:
            return pl.BlockSpec(shape, lambda j, i, k: pick(i, j, k))
        return pl.BlockSpec(shape, pick)

    if dims == "tn":
        a_spec = spec((tk, tm), lambda i, j, k: (k, i))
    else:
        a_spec = spec((tm, tk), lambda i, j, k: (i, k))
    if b_block is not None:
        b_spec = spec(*b_block)
    elif dims == "nt":
        b_spec = spec((tn, tk), lambda i, j, k: (j, k))
    else:
        b_spec = spec((tk, tn), lambda i, j, k: (k, j))
    o_spec = spec((tm, tn), lambda i, j, k: (i, j))
    out_shape = [jax.ShapeDtypeStruct((m, n), dt) for dt in out_dtypes]
    out_specs = [o_spec] * len(out_dtypes)
    prev, io_aliases = (), {}
    if out_into is not None:
        full_shape, block, index, before = out_into
        assert len(out_dtypes) == 1 and not extras
        out_shape = [jax.ShapeDtypeStruct(full_shape, out_dtypes[0])]
        out_specs = [spec(block, index)]
        if before is not None:
            prev, io_aliases = (before,), {2: 0}
    assert not (weights_outer and extra_specs)
    dn = {"nn": NN, "nt": NT, "tn": TN}[dims]
    n_ex, n_out = len(extras), len(out_dtypes)
    e_specs = [o_spec if s is None else s for s in (tuple(extra_specs) + (None,) * n_ex)[:n_ex]]

    n_prev = len(prev)

    def body(*refs):
        a_ref, b_ref = refs[0], refs[1]
        e_refs = refs[2 + n_prev:2 + n_prev + n_ex]
        o_refs = refs[2 + n_prev + n_ex:2 + n_prev + n_ex + n_out]
        av = a_ref[...]
        if a_fn is not None:
            av = a_fn(av)
        bv = b_ref[...]
        if bv.ndim == 3:
            if dims == "nt":
                bv = jnp.concatenate([bv[c] for c in range(bv.shape[0])], axis=1)
            else:
                bv = bv.reshape(-1, bv.shape[-1])
        part = _dot(av.astype(BF16), bv.astype(BF16), dn)

        def finish(acc):
            outs = (acc,) if epilogue is None else epilogue(acc, *[e[...] for e in e_refs])
            for o_ref, o in zip(o_refs, outs):
                o_ref[...] = o.astype(o_ref.dtype)

        if nk == 1:
            finish(part)
        else:
            acc_ref = refs[-1]
            k = pl.program_id(2)

            @pl.when(k == 0)
            def _():
                acc_ref[...] = part

            @pl.when(k > 0)
            def _():
                acc_ref[...] += part

            @pl.when(k == nk - 1)
            def _():
                finish(acc_ref[...])

    outs, side_outs = _hosted_call(
        body, name=name, grid=(nj, ni, nk) if weights_outer else (ni, nj, nk),
        in_specs=[a_spec, b_spec] + [ANY_SPEC] * n_prev + e_specs,
        out_specs=out_specs, out_shape=out_shape,
        scratch_shapes=[pltpu.VMEM((tm, tn), F32)] if nk > 1 else [],
        args=(a, b, *prev, *extras), side=side, io_aliases=io_aliases)
    res = outs[0] if n_out == 1 else outs
    return res if side is None else (res, side_outs)


def _mod_spec(tm, tn, seq):
    return pl.BlockSpec((1, MOD_ROWS, tn), lambda i, j, k: ((i * tm) // seq, 0, j))


def _normmod_fwd(x3, g, mod, shift_row, scale_row, *, name, tb=512, side=None):
    bsz, seq, d = x3.shape
    tb = min(tb, seq)

    def body(x_ref, g_ref, mod_ref, h_ref):
        m = mod_ref[0]
        nrm = _rms_fwd(x_ref[0], g_ref[...], d)
        h = nrm * (1.0 + m[scale_row:scale_row + 1, :]) + m[shift_row:shift_row + 1, :]
        h_ref[0] = h.astype(BF16)

    outs, side_outs = _hosted_call(
        body, name=name, grid=(bsz, seq // tb),
        in_specs=[pl.BlockSpec((1, tb, d), lambda b, i: (b, i, 0)),
                  pl.BlockSpec((1, d), lambda b, i: (0, 0)),
                  pl.BlockSpec((1, MOD_ROWS, d), lambda b, i: (b, 0, 0))],
        out_specs=[pl.BlockSpec((1, tb, d), lambda b, i: (b, i, 0))],
        out_shape=[jax.ShapeDtypeStruct((bsz, seq, d), BF16)],
        args=(x3, g, mod), side=side)
    return outs[0] if side is None else (outs[0], side_outs)


def _pair_mean_exact(x, lo):
    s_lo = jnp.sum(jnp.where(lo, x, 0.0), axis=-1, keepdims=True)
    s_hi = jnp.sum(jnp.where(lo, 0.0, x), axis=-1, keepdims=True)
    return jnp.where(lo, s_lo, s_hi) * (1.0 / GROUP_DIM)


def _gmlp_pair_fwd(gv_p, w0, w1, bias, lo):
    mu = _pair_mean_exact(gv_p, lo)
    dlt = gv_p - mu
    var = _pair_mean_exact(dlt * dlt, lo)
    rstd = lax.rsqrt(var + EPS)
    vn = dlt * rstd
    vnb = vn.astype(BF16)
    mixed = jnp.where(lo, _dot(w0, vnb), _dot(w1, vnb)) + bias
    return vn, vnb, rstd, mixed


def _tril_bf16(w):
    t = w.shape[-1]
    return jnp.where(_iota((t, t), 1) <= _iota((t, t), 0), w, 0.0).astype(BF16)


def _gmlp_fwd(z3, ws, bexp, g_out, *, name):
    bsz, seq, _ = z3.shape
    nc = seq // CHUNK

    def body(u_ref, v_ref, ws_ref, b_ref, g_ref, y_ref):
        lo = _iota((CHUNK, 128), 1) < GROUP_DIM
        gu = _gelu(u_ref[0].astype(F32))
        gv = _gelu(v_ref[0].astype(F32))
        parts = []
        for p in range(GROUPS // 2):
            sl = slice(128 * p, 128 * p + 128)
            w0 = _tril_bf16(ws_ref[2 * p])
            w1 = _tril_bf16(ws_ref[2 * p + 1])
            _, _, _, mixed = _gmlp_pair_fwd(gv[:, sl], w0, w1, b_ref[p], lo)
            parts.append(gu[:, sl] * mixed)
        yg = jnp.concatenate(parts, axis=1)
        y_ref[0] = _rms_fwd(yg, g_ref[...], D_GMLP).astype(BF16)

    return pl.pallas_call(
        body, name=name, grid=(bsz, nc),
        in_specs=[pl.BlockSpec((1, CHUNK, D_GMLP), lambda b, i: (b, i, 0)),
                  pl.BlockSpec((1, CHUNK, D_GMLP), lambda b, i: (b, i, 1)),
                  pl.BlockSpec((GROUPS, CHUNK, CHUNK), lambda b, i: (0, 0, 0)),
                  pl.BlockSpec((GROUPS // 2, CHUNK, 128), lambda b, i: (0, 0, 0)),
                  pl.BlockSpec((1, D_GMLP), lambda b, i: (0, 0))],
        out_specs=pl.BlockSpec((1, CHUNK, D_GMLP), lambda b, i: (b, i, 0)),
        out_shape=jax.ShapeDtypeStruct((bsz, seq, D_GMLP), BF16),
        compiler_params=_cparams(),
    )(z3, z3, ws, bexp, g_out)


def _gmlp_bwd(z3, dyn3, ws, wst, bexp, g_out, *, name, dy_col):
    bsz, seq, _ = z3.shape
    nc = seq // CHUNK
    npair = GROUPS // 2

    def body(u_ref, v_ref, dy_ref, ws_ref, wst_ref, b_ref, g_ref, duv_ref, dws_ref, dbs_ref, dg_ref, dbacc):
        first = jnp.logical_and(pl.program_id(0) == 0, pl.program_id(1) == 0)
        last = jnp.logical_and(pl.program_id(0) == bsz - 1, pl.program_id(1) == nc - 1)

        @pl.when(first)
        def _():
            dws_ref[...] = jnp.zeros_like(dws_ref)
            dg_ref[...] = jnp.zeros_like(dg_ref)
            dbacc[...] = jnp.zeros_like(dbacc)

        lo = _iota((CHUNK, 128), 1) < GROUP_DIM
        tril = _iota((CHUNK, CHUNK), 1) <= _iota((CHUNK, CHUNK), 0)
        u = u_ref[0].astype(F32)
        v = v_ref[0].astype(F32)
        gu, dgu = _gelu_and_grad(u)
        gv, dgv_dv = _gelu_and_grad(v)
        fwd = []
        for p in range(npair):
            sl = slice(128 * p, 128 * p + 128)
            w0 = _tril_bf16(ws_ref[2 * p])
            w1 = _tril_bf16(ws_ref[2 * p + 1])
            fwd.append(_gmlp_pair_fwd(gv[:, sl], w0, w1, b_ref[p], lo))
        yg = jnp.concatenate([gu[:, 128 * p:128 * p + 128] * fwd[p][3] for p in range(npair)], axis=1)
        dyg, dg = _rms_bwd(yg, g_ref[...], dy_ref[0].astype(F32), D_GMLP)
        dg_ref[...] += dg
        du_parts, dv_parts = [], []
        for p in range(npair):
            sl = slice(128 * p, 128 * p + 128)
            vn, vnb, rstd, mixed = fwd[p]
            dyg_p = dyg[:, sl]
            dmixed = dyg_p * gu[:, sl]
            dbacc[p] += dmixed
            dm_lo = jnp.where(lo, dmixed, 0.0).astype(BF16)
            dm_hi = jnp.where(lo, 0.0, dmixed).astype(BF16)
            dws_ref[2 * p] += jnp.where(tril, _dot(dm_lo, vnb, NT), 0.0)
            dws_ref[2 * p + 1] += jnp.where(tril, _dot(dm_hi, vnb, NT), 0.0)
            dmb = dmixed.astype(BF16)
            dvn = jnp.where(lo, _dot(wst_ref[2 * p], dmb), _dot(wst_ref[2 * p + 1], dmb))
            dgv = rstd * (dvn - _pair_mean_exact(dvn, lo) - vn * _pair_mean_exact(dvn * vn, lo))
            dv_parts.append(dgv * dgv_dv[:, sl])
            du_parts.append(dyg_p * mixed * dgu[:, sl])
        duv_ref[0] = jnp.concatenate(du_parts + dv_parts, axis=1).astype(BF16)

        @pl.when(last)
        def _():
            sel = jnp.where(_iota((8, 128), 0) == 0, (_iota((8, 128), 1) < GROUP_DIM).astype(F32),
                            jnp.where(_iota((8, 128), 0) == 1, (_iota((8, 128), 1) >= GROUP_DIM).astype(F32), 0.0))
            for p in range(npair):
                dbs_ref[p] = lax.dot_general(sel, dbacc[p], NT, precision=lax.Precision.HIGHEST,
                                             preferred_element_type=F32)

    duv, dws, dbs, dg = pl.pallas_call(
        body, name=name, grid=(bsz, nc),
        in_specs=[pl.BlockSpec((1, CHUNK, D_GMLP), lambda b, i: (b, i, 0)),
                  pl.BlockSpec((1, CHUNK, D_GMLP), lambda b, i: (b, i, 1)),
                  pl.BlockSpec((1, CHUNK, D_GMLP), lambda b, i: (b, i, dy_col)),
                  pl.BlockSpec((GROUPS, CHUNK, CHUNK), lambda b, i: (0, 0, 0)),
                  pl.BlockSpec((GROUPS, CHUNK, CHUNK), lambda b, i: (0, 0, 0)),
                  pl.BlockSpec((npair, CHUNK, 128), lambda b, i: (0, 0, 0)),
                  pl.BlockSpec((1, D_GMLP), lambda b, i: (0, 0))],
        out_specs=[pl.BlockSpec((1, CHUNK, 2 * D_GMLP), lambda b, i: (b, i, 0)),
                   pl.BlockSpec((GROUPS, CHUNK, CHUNK), lambda b, i: (0, 0, 0)),
                   pl.BlockSpec((npair, 8, CHUNK), lambda b, i: (0, 0, 0)),
                   pl.BlockSpec((1, D_GMLP), lambda b, i: (0, 0))],
        out_shape=[jax.ShapeDtypeStruct((bsz, seq, D_IN_PAD), BF16),
                   jax.ShapeDtypeStruct((GROUPS, CHUNK, CHUNK), F32),
                   jax.ShapeDtypeStruct((npair, 8, CHUNK), F32),
                   jax.ShapeDtypeStruct((1, D_GMLP), F32)],
        scratch_shapes=[pltpu.VMEM((npair, CHUNK, 128), F32)],
        compiler_params=_cparams(),
    )(z3, z3, dyn3, ws, wst, bexp, g_out)
    return duv, dws, dbs[:, :2, :].reshape(GROUPS, CHUNK), dg


def _partner(x):
    width = x.shape[-1]
    lane = _iota(x.shape, x.ndim - 1) % HEAD_PAD
    up = pltpu.roll(x, width - ROPE // 2, x.ndim - 1)
    down = pltpu.roll(x, ROPE // 2, x.ndim - 1)
    first = jnp.logical_and(lane >= NOPE, lane < NOPE + ROPE // 2)
    second = jnp.logical_and(lane >= NOPE + ROPE // 2, lane < NOPE + ROPE)
    return jnp.where(first, up, jnp.where(second, down, 0.0))


def _mla_prep_fwd(z3, g_q, g_kv, w_uq, w_ukv, ctab, stab, *, name, tb=256):
    bsz, seq, _ = z3.shape
    tb = min(tb, seq)
    hw = HEADS * HEAD_PAD

    def body(ql_ref, kvl_ref, krl_ref, gq_ref, gkv_ref, wuq_ref, wukv_ref, c_ref, s_ref, q_ref, kv_ref, kp_ref):
        cq = _rms_fwd(ql_ref[0].astype(F32), gq_ref[...], Q_RANK).astype(BF16)
        q = _dot(cq, wuq_ref[...])
        c1, s1 = c_ref[0], s_ref[0]
        c8, s8 = jnp.tile(c1, (1, HEADS)), jnp.tile(s1, (1, HEADS))
        q_ref[0] = ((q * c8 + _partner(q) * s8) * SCALE_LOG2).astype(BF16)
        ckv = _rms_fwd(kvl_ref[0].astype(F32), gkv_ref[...], KV_RANK).astype(BF16)
        kv = _dot(ckv, wukv_ref[...])
        kv_ref[0] = kv.astype(BF16)
        kr = krl_ref[0].astype(F32)
        kr = kr * c1 + _partner(kr) * s1
        lane = _iota((tb, hw), 1) % HEAD_PAD
        kp_ref[0] = jnp.where(lane < NOPE, kv, jnp.tile(kr, (1, HEADS))).astype(BF16)

    return pl.pallas_call(
        body, name=name, grid=(bsz, seq // tb),
        in_specs=[pl.BlockSpec((1, tb, Q_RANK), lambda b, i: (b, i, 4)),
                  pl.BlockSpec((1, tb, KV_RANK), lambda b, i: (b, i, 10)),
                  pl.BlockSpec((1, tb, HEAD_PAD), lambda b, i: (b, i, 11)),
                  pl.BlockSpec((1, Q_RANK), lambda b, i: (0, 0)),
                  pl.BlockSpec((1, KV_RANK), lambda b, i: (0, 0)),
                  pl.BlockSpec((Q_RANK, hw), lambda b, i: (0, 0)),
                  pl.BlockSpec((KV_RANK, hw), lambda b, i: (0, 0)),
                  pl.BlockSpec((1, tb, HEAD_PAD), lambda b, i: (b, i, 0)),
                  pl.BlockSpec((1, tb, HEAD_PAD), lambda b, i: (b, i, 0))],
        out_specs=[pl.BlockSpec((1, tb, hw), lambda b, i: (b, i, 0))] * 3,
        out_shape=[jax.ShapeDtypeStruct((bsz, seq, hw), BF16)] * 3,
        compiler_params=_cparams(),
    )(z3, z3, z3, g_q, g_kv, w_uq, w_ukv, ctab, stab)


def _mla_prep_bwd(z3, dz3, dq3, dk3, dv3, g_q, g_kv, w_uq, w_ukv, ctab, stab, *, name, tb=256):
    bsz, seq, _ = z3.shape
    tb = min(tb, seq)
    hw = HEADS * HEAD_PAD
    nb = seq // tb

    def body(ql_ref, kvl_ref, dq_ref, dk_ref, dv_ref, gq_ref, gkv_ref, wuq_ref, wukv_ref, c_ref, s_ref, dz_in,
             dz_ref, cq_ref, dqb_ref, ckv_ref, dkvb_ref, dgq_ref, dgkv_ref):
        @pl.when(jnp.logical_and(pl.program_id(0) == 0, pl.program_id(1) == 0))
        def _():
            dgq_ref[...] = jnp.zeros_like(dgq_ref)
            dgkv_ref[...] = jnp.zeros_like(dgkv_ref)

        c1, s1 = c_ref[0], s_ref[0]
        c8, s8 = jnp.tile(c1, (1, HEADS)), jnp.tile(s1, (1, HEADS))
        dqr = dq_ref[0]
        dqb = (dqr * c8 + _partner(dqr * s8)).astype(BF16)
        dqb_ref[0] = dqb
        ql = ql_ref[0].astype(F32)
        cq_ref[0] = _rms_fwd(ql, gq_ref[...], Q_RANK).astype(BF16)
        dql, dgq = _rms_bwd(ql, gq_ref[...], _dot(dqb, wuq_ref[...], NT), Q_RANK)
        dgq_ref[...] += dgq

        dk = dk_ref[0]
        lane = _iota((tb, hw), 1) % HEAD_PAD
        dkvb = jnp.where(lane < NOPE, dk, dv_ref[0]).astype(BF16)
        dkvb_ref[0] = dkvb
        kvl = kvl_ref[0].astype(F32)
        ckv_ref[0] = _rms_fwd(kvl, gkv_ref[...], KV_RANK).astype(BF16)
        dkvl, dgkv = _rms_bwd(kvl, gkv_ref[...], _dot(dkvb, wukv_ref[...], NT), KV_RANK)
        dgkv_ref[...] += dgkv

        dkr = dk[:, 0:HEAD_PAD].astype(F32)
        for h in range(1, HEADS):
            dkr = dkr + dk[:, HEAD_PAD * h:HEAD_PAD * (h + 1)].astype(F32)
        lane1 = _iota((tb, HEAD_PAD), 1)
        dkr = jnp.where(jnp.logical_and(lane1 >= NOPE, lane1 < NOPE + ROPE), dkr, 0.0)
        dkrl = dkr * c1 + _partner(dkr * s1)
        dz_ref[0] = jnp.concatenate([dql, dkvl, dkrl], axis=1).astype(BF16)

    return pl.pallas_call(
        body, name=name, grid=(bsz, nb),
        in_specs=[pl.BlockSpec((1, tb, Q_RANK), lambda b, i: (b, i, 4)),
                  pl.BlockSpec((1, tb, KV_RANK), lambda b, i: (b, i, 10)),
                  pl.BlockSpec((1, tb, hw), lambda b, i: (b, i, 0)),
                  pl.BlockSpec((1, tb, hw), lambda b, i: (b, i, 0)),
                  pl.BlockSpec((1, tb, hw), lambda b, i: (b, i, 0)),
                  pl.BlockSpec((1, Q_RANK), lambda b, i: (0, 0)),
                  pl.BlockSpec((1, KV_RANK), lambda b, i: (0, 0)),
                  pl.BlockSpec((Q_RANK, hw), lambda b, i: (0, 0)),
                  pl.BlockSpec((KV_RANK, hw), lambda b, i: (0, 0)),
                  pl.BlockSpec((1, tb, HEAD_PAD), lambda b, i: (b, i, 0)),
                  pl.BlockSpec((1, tb, HEAD_PAD), lambda b, i: (b, i, 0)),
                  ANY_SPEC],
        out_specs=[pl.BlockSpec((1, tb, 512), lambda b, i: (b, i, 2)),
                   pl.BlockSpec((1, tb, Q_RANK), lambda b, i: (b, i, 0)),
                   pl.BlockSpec((1, tb, hw), lambda b, i: (b, i, 0)),
                   pl.BlockSpec((1, tb, KV_RANK), lambda b, i: (b, i, 0)),
                   pl.BlockSpec((1, tb, hw), lambda b, i: (b, i, 0)),
                   pl.BlockSpec((1, Q_RANK), lambda b, i: (0, 0)),
                   pl.BlockSpec((1, KV_RANK), lambda b, i: (0, 0))],
        out_shape=[jax.ShapeDtypeStruct((bsz, seq, D_IN_PAD), BF16),
                   jax.ShapeDtypeStruct((bsz, seq, Q_RANK), BF16),
                   jax.ShapeDtypeStruct((bsz, seq, hw), BF16),
                   jax.ShapeDtypeStruct((bsz, seq, KV_RANK), BF16),
                   jax.ShapeDtypeStruct((bsz, seq, hw), BF16),
                   jax.ShapeDtypeStruct((1, Q_RANK), F32),
                   jax.ShapeDtypeStruct((1, KV_RANK), F32)],
        input_output_aliases={11: 0},
        compiler_params=_cparams(),
    )(z3, z3, dq3, dk3, dv3, g_q, g_kv, w_uq, w_ukv, ctab, stab, dz3)


ATTN_HEADS_PER_STEP = 4


def _attn_specs(tq, seq, hp):
    blk = pl.BlockSpec((1, tq, hp * HEAD_PAD), lambda b, h, i: (b, i, h))
    full = pl.BlockSpec((1, seq, hp * HEAD_PAD), lambda b, h, i: (b, 0, h))
    return blk, full


def _head(h):
    return slice(HEAD_PAD * h, HEAD_PAD * (h + 1))


def _attn_fwd(q3, kv3, kp3, *, name, tq=512, hp=ATTN_HEADS_PER_STEP, side=None):
    bsz, seq, hw = q3.shape
    tq = min(tq, seq)
    blk, full = _attn_specs(tq, seq, hp)

    def body(q_ref, kv_ref, kp_ref, o_ref, lse_ref):
        i = pl.program_id(2)

        def update(state, q, kp, kv, mask=None):
            m, l, acc = state
            s = _dot(q, kp, NT)
            if mask is not None:
                s = jnp.where(mask, s, -1e30)
            m_new = jnp.maximum(m, jnp.max(s, axis=1, keepdims=True))
            alpha = jnp.exp2(m - m_new)
            p = jnp.exp2(s - m_new)
            return m_new, alpha * l + jnp.sum(p, axis=1, keepdims=True), alpha * acc + _dot(p.astype(BF16), kv)

        def step(j, carry):
            st = pl.multiple_of(j * tq, tq)
            return tuple(update(carry[h], q_ref[0, :, _head(h)], kp_ref[0, pl.ds(st, tq), _head(h)],
                                kv_ref[0, pl.ds(st, tq), _head(h)]) for h in range(hp))

        init = tuple((jnp.full((tq, 1), -1e30, F32), jnp.zeros((tq, 1), F32), jnp.zeros((tq, HEAD_PAD), F32))
                     for _ in range(hp))
        carry = lax.fori_loop(0, i, step, init)

        st = pl.multiple_of(i * tq, tq)
        is_nope = _iota((tq, HEAD_PAD), 1) < NOPE
        causal = _iota((tq, tq), 1) <= _iota((tq, tq), 0)
        for h in range(hp):
            m, l, acc = update(carry[h], q_ref[0, :, _head(h)], kp_ref[0, pl.ds(st, tq), _head(h)],
                               kv_ref[0, pl.ds(st, tq), _head(h)], causal)
            o_ref[0, :, _head(h)] = jnp.where(is_nope, 0.0, acc / l).astype(BF16)
            lse_ref[0, :, _head(h)] = jnp.broadcast_to(m + jnp.log(l) * LOG2E, (tq, HEAD_PAD))

    outs, side_outs = _hosted_call(
        body, name=name, grid=(bsz, HEADS // hp, seq // tq),
        in_specs=[blk, full, full],
        out_specs=[blk, blk],
        out_shape=[jax.ShapeDtypeStruct((bsz, seq, hw), BF16), jax.ShapeDtypeStruct((bsz, seq, hw), F32)],
        args=(q3, kv3, kp3), side=side)
    return outs if side is None else (outs, side_outs)


def _attn_bwd(q3, kv3, kp3, do3, lse3, dl3, *, name, tq=512, hp=ATTN_HEADS_PER_STEP, side=None):
    bsz, seq, hw = q3.shape
    tq = min(tq, seq)
    nq = seq // tq
    blk, full = _attn_specs(tq, seq, hp)

    def body(kv_ref, kp_ref, q_ref, do_ref, lse_ref, dl_ref, dq_ref, dk_ref, dv_ref):
        j = pl.program_id(2)

        @pl.when(j == 0)
        def _():
            dq_ref[...] = jnp.zeros_like(dq_ref)

        def pair(h, row0, nrows, nkeys, mask=None):
            row0 = pl.multiple_of(row0, nrows)
            qi = q_ref[0, pl.ds(row0, nrows), _head(h)]
            do = do_ref[0, pl.ds(row0, nrows), _head(h)]
            kp = kp_ref[0, :nkeys, _head(h)]
            s = _dot(qi, kp, NT)
            if mask is not None:
                s = jnp.where(mask, s, -1e30)
            wide = nkeys // HEAD_PAD
            p = jnp.exp2(s - jnp.tile(lse_ref[0, pl.ds(row0, nrows), _head(h)], (1, wide)))
            dv = _dot(p.astype(BF16), do, TN)
            dp = _dot(do, kv_ref[0, :nkeys, _head(h)], NT)
            ds = (p * (dp - jnp.tile(dl_ref[0, pl.ds(row0, nrows), _head(h)], (1, wide)))).astype(BF16)
            dq_ref[0, pl.ds(row0, nrows), _head(h)] += _dot(ds, kp)
            return _dot(ds, qi, TN), dv

        def step(i, carry):
            st = pl.multiple_of(i * tq, tq)
            out = []
            for h in range(hp):
                dk, dv = pair(h, st, tq, tq)
                out.append((carry[h][0] + dk, carry[h][1] + dv))
            return tuple(out)

        causal = _iota((tq, tq), 1) <= _iota((tq, tq), 0)
        carry = tuple(pair(h, pl.multiple_of(j * tq, tq), tq, tq, causal) for h in range(hp))
        carry = lax.fori_loop(j + 1, nq, step, carry)
        for h in range(hp):
            dk_ref[0, :, _head(h)] = (carry[h][0] * (1.0 / LOG2E)).astype(BF16)
            dv_ref[0, :, _head(h)] = carry[h][1].astype(BF16)

        @pl.when(j == nq - 1)
        def _():
            dq_ref[...] = dq_ref[...] * ATTN_SCALE

    outs, side_outs = _hosted_call(
        body, name=name, grid=(bsz, HEADS // hp, nq),
        in_specs=[blk, blk, full, full, full, full],
        out_specs=[full, blk, blk],
        out_shape=[jax.ShapeDtypeStruct((bsz, seq, hw), F32)] + [jax.ShapeDtypeStruct((bsz, seq, hw), BF16)] * 2,
        args=(kv3, kp3, q3, do3, lse3, dl3), side=side)
    return outs if side is None else (outs, side_outs)


def _onorm_fwd(o3, yg3, g_pad, *, name, tb=512):
    bsz, seq, hw = o3.shape
    wg = yg3.shape[-1]
    tb = min(tb, seq)

    def body(o_ref, yg_ref, g_ref, y_ref):
        ya = _rms_fwd(o_ref[0].astype(F32), g_ref[...], HEADS * 64).astype(BF16)
        y_ref[0] = jnp.concatenate([ya, yg_ref[0]], axis=1)

    return pl.pallas_call(
        body, name=name, grid=(bsz, seq // tb),
        in_specs=[pl.BlockSpec((1, tb, hw), lambda b, i: (b, i, 0)),
                  pl.BlockSpec((1, tb, wg), lambda b, i: (b, i, 0)),
                  pl.BlockSpec((1, hw), lambda b, i: (0, 0))],
        out_specs=pl.BlockSpec((1, tb, hw + wg), lambda b, i: (b, i, 0)),
        out_shape=jax.ShapeDtypeStruct((bsz, seq, hw + wg), BF16),
        compiler_params=_cparams(),
    )(o3, yg3, g_pad)


def _onorm_bwd(o3, dy3, g_pad, *, name, tb=512):
    bsz, seq, hw = o3.shape
    tb = min(tb, seq)

    def body(o_ref, dy_ref, g_ref, do_ref, dl_ref, dg_ref):
        @pl.when(jnp.logical_and(pl.program_id(0) == 0, pl.program_id(1) == 0))
        def _():
            dg_ref[...] = jnp.zeros_like(dg_ref)

        o = o_ref[0].astype(F32)
        do, dg = _rms_bwd(o, g_ref[...], dy_ref[0].astype(F32), HEADS * 64)
        dg_ref[...] += dg
        do_ref[0] = do.astype(BF16)
        prod = do * o
        parts = []
        for h in range(HEADS):
            sh = jnp.sum(prod[:, HEAD_PAD * h:HEAD_PAD * (h + 1)], axis=1, keepdims=True)
            parts.append(jnp.broadcast_to(sh, (tb, HEAD_PAD)))
        dl_ref[0] = jnp.concatenate(parts, axis=1)

    return pl.pallas_call(
        body, name=name, grid=(bsz, seq // tb),
        in_specs=[pl.BlockSpec((1, tb, hw), lambda b, i: (b, i, 0)),
                  pl.BlockSpec((1, tb, hw), lambda b, i: (b, i, 0)),
                  pl.BlockSpec((1, hw), lambda b, i: (0, 0))],
        out_specs=[pl.BlockSpec((1, tb, hw), lambda b, i: (b, i, 0)),
                   pl.BlockSpec((1, tb, hw), lambda b, i: (b, i, 0)),
                   pl.BlockSpec((1, hw), lambda b, i: (0, 0))],
        out_shape=[jax.ShapeDtypeStruct((bsz, seq, hw), BF16),
                   jax.ShapeDtypeStruct((bsz, seq, hw), F32),
                   jax.ShapeDtypeStruct((1, hw), F32)],
        compiler_params=_cparams(),
    )(o3, dy3, g_pad)


def _resnode_bwd(x3, g, *, name, target3=None, dh3=None, dres3=None, mod_nm=None, rows=None,
                 branch3=None, mod_gate=None, gate_row=None, tb=512, side=None):
    bsz, seq, d = x3.shape
    tb = min(tb, seq)
    final = target3 is not None
    has_branch = branch3 is not None
    row_spec = pl.BlockSpec((1, tb, d), lambda b, i: (b, i, 0))
    vec_spec = pl.BlockSpec((1, d), lambda b, i: (0, 0))
    mod_spec = pl.BlockSpec((1, MOD_ROWS, d), lambda b, i: (b, 0, 0))

    ins, in_specs = [x3, g], [row_spec, vec_spec]
    if final:
        ins += [target3]
        in_specs += [row_spec]
    else:
        ins += [dh3, dres3, mod_nm]
        in_specs += [row_spec, row_spec, mod_spec]
    if has_branch:
        ins += [branch3, mod_gate]
        in_specs += [row_spec, mod_spec]

    out_names = ["dx", "dg"]
    out_specs = [row_spec, vec_spec]
    out_shape = [jax.ShapeDtypeStruct((bsz, seq, d), F32), jax.ShapeDtypeStruct((1, d), F32)]
    if final:
        out_names += ["loss"]
        out_specs += [pl.BlockSpec((1, 128), lambda b, i: (0, 0))]
        out_shape += [jax.ShapeDtypeStruct((1, 128), F32)]
    else:
        out_names += ["dnm"]
        out_specs += [mod_spec]
        out_shape += [jax.ShapeDtypeStruct((bsz, MOD_ROWS, d), F32)]
    if has_branch:
        out_names += ["dbr", "dgate"]
        out_specs += [row_spec, mod_spec]
        out_shape += [jax.ShapeDtypeStruct((bsz, seq, d), BF16), jax.ShapeDtypeStruct((bsz, MOD_ROWS, d), F32)]
    n_in = len(ins)

    def body(*refs):
        r = dict(zip(["x", "g"] + (["t"] if final else ["dh", "dres", "nm"]) + (["br", "gm"] if has_branch else []),
                     refs[:n_in]))
        o = dict(zip(out_names, refs[n_in:]))
        b_first = pl.program_id(1) == 0
        first = jnp.logical_and(pl.program_id(0) == 0, b_first)
        rowid = _iota((MOD_ROWS, d), 0)

        @pl.when(first)
        def _():
            o["dg"][...] = jnp.zeros_like(o["dg"])
            if final:
                o["loss"][...] = jnp.zeros_like(o["loss"])

        @pl.when(b_first)
        def _():
            if not final:
                o["dnm"][...] = jnp.zeros_like(o["dnm"])
            if has_branch:
                o["dgate"][...] = jnp.zeros_like(o["dgate"])

        x = r["x"][0]
        gv = r["g"][...]
        if final:
            e = _rms_fwd(x, gv, d) - r["t"][0]
            sq = jnp.sum(jnp.sum(e * e, axis=1, keepdims=True), axis=0, keepdims=True)
            o["loss"][...] += jnp.broadcast_to(sq * (0.5 / d), (1, 128))
            dx, dg = _rms_bwd(x, gv, e * (1.0 / d), d)
        else:
            m = r["nm"][0]
            dh = r["dh"][0].astype(F32)
            scale = m[rows[1]:rows[1] + 1, :]
            rstd = lax.rsqrt(jnp.sum(x * x, axis=-1, keepdims=True) * (1.0 / d) + EPS)
            xh = x * rstd
            nrm = xh * gv
            dshift = jnp.sum(dh, axis=0, keepdims=True)
            dscale = jnp.sum(dh * nrm, axis=0, keepdims=True)
            o["dnm"][0] += jnp.where(rowid == 0, dshift, jnp.where(rowid == 1, dscale, 0.0))
            dn = dh * (1.0 + scale)
            dg = jnp.sum(dn * xh, axis=0, keepdims=True)
            dxh = dn * gv
            dx = rstd * (dxh - xh * (jnp.sum(dxh * xh, axis=-1, keepdims=True) * (1.0 / d))) + r["dres"][0]
        o["dg"][...] += dg
        o["dx"][0] = dx
        if has_branch:
            gate = r["gm"][0][gate_row:gate_row + 1, :]
            o["dbr"][0] = (gate * dx).astype(BF16)
            dgate = jnp.sum(dx * r["br"][0], axis=0, keepdims=True)
            o["dgate"][0] += jnp.where(rowid == 0, dgate, 0.0)

    outs, side_outs = _hosted_call(
        body, name=name, grid=(bsz, seq // tb),
        in_specs=in_specs, out_specs=out_specs, out_shape=out_shape, args=tuple(ins), side=side)
    res = dict(zip(out_names, outs))
    return res if side is None else (res, side_outs)


def _adamw(w, g, m, v, *, name):
    shape = w.shape
    cols = shape[-1]
    rows = w.size // cols
    tr = _pick_rows(rows, max(8, (256 * 1024) // cols // 8 * 8))

    def body(w_ref, g_ref, m_ref, v_ref, d_ref, nm_ref, nv_ref):
        d_ref[...], nm_ref[...], nv_ref[...] = _adamw_math(w_ref[...], g_ref[...], m_ref[...], v_ref[...])

    if w.ndim == 3 and shape[1] % 8 == 0:
        tr3 = _pick_rows(shape[1], max(8, (256 * 1024) // cols // 8 * 8))
        spec3 = pl.BlockSpec((None, tr3, cols), lambda l, i: (l, i, 0))
        return tuple(pl.pallas_call(
            body, name=name, grid=(shape[0], shape[1] // tr3),
            in_specs=[spec3] * 4, out_specs=[spec3] * 3,
            out_shape=[jax.ShapeDtypeStruct(shape, F32)] * 3,
            compiler_params=_cparams(),
        )(w, g, m, v))
    spec = pl.BlockSpec((tr, cols), lambda i: (i, 0))
    outs = pl.pallas_call(
        body, name=name, grid=(rows // tr,),
        in_specs=[spec] * 4, out_specs=[spec] * 3,
        out_shape=[jax.ShapeDtypeStruct((rows, cols), F32)] * 3,
        compiler_params=_cparams(),
    )(*[t.reshape(rows, cols) for t in (w, g, m, v)])
    return tuple(o.reshape(shape) for o in outs)


def _adamw_math(w, g, m, v):
    c1 = 1.0 - ADAM_B1 ** ADAM_STEP
    c2 = 1.0 - ADAM_B2 ** ADAM_STEP
    nm = ADAM_B1 * m + (1.0 - ADAM_B1) * g
    nv = ADAM_B2 * v + (1.0 - ADAM_B2) * (g * g)
    delta = -ADAM_LR * ((nm / c1) / (jnp.sqrt(nv / c2) + ADAM_EPS) + ADAM_WD * w)
    return delta, nm, nv


def _adamw_layers(w, m, v, bufs, row_off, *, name, tr=256):
    depth, rows, cols = w.shape
    tr = min(tr, rows)
    assert rows % tr == 0 and row_off % tr == 0

    outs = None
    for l in range(depth):
        def body(w_ref, g_ref, m_ref, v_ref, *rest):
            go_ref, d_ref, nm_ref, nv_ref = rest[-4:]
            g = g_ref[...]
            go_ref[...] = g
            d_ref[...], nm_ref[...], nv_ref[...] = _adamw_math(w_ref[...], g, m_ref[...], v_ref[...])

        layer = pl.BlockSpec((None, tr, cols), lambda i, l=l: (l, i, 0))
        prev = () if outs is None else tuple(outs)
        outs = pl.pallas_call(
            body, name=f"{name}_l{l}", grid=(rows // tr,),
            in_specs=[layer, pl.BlockSpec((tr, cols), lambda i: (row_off // tr + i, 0)), layer, layer]
            + [ANY_SPEC] * len(prev),
            out_specs=[layer] * 4,
            out_shape=[jax.ShapeDtypeStruct(w.shape, F32)] * 4,
            input_output_aliases={4 + k: k for k in range(len(prev))},
            compiler_params=_cparams(),
        )(w, bufs[l], m, v, *prev)
    return tuple(outs)


def _sum_leading(x, *, name, tr=256):
    n, rows, cols = x.shape
    tr = _pick_rows(rows, tr)

    def body(x_ref, o_ref):
        acc = x_ref[0]
        for k in range(1, n):
            acc = acc + x_ref[k]
        o_ref[...] = acc

    return pl.pallas_call(
        body, name=name, grid=(rows // tr,),
        in_specs=[pl.BlockSpec((n, tr, cols), lambda i: (0, i, 0))],
        out_specs=pl.BlockSpec((tr, cols), lambda i: (i, 0)),
        out_shape=jax.ShapeDtypeStruct((rows, cols), F32),
        compiler_params=_cparams(),
    )(x)


def _position():
    return lax.axis_index("x"), lax.axis_index("y"), lax.axis_index("c")


def _allgather8(x, *, name):
    shape = x.shape

    def body(x_ref, out_ref, send_sems, recv_sems, local_sem):
        px, py, pc = _position()
        me, sibling = (px, py, pc), (px, py, 1 - pc)
        chips = [(1 - px, py), (px, 1 - py), (1 - px, 1 - py)]
        src_own = x_ref

        def slot(qx, qy, qc):
            return out_ref.at[4 * qx + 2 * qy + qc]

        def copy(k, block, to, src=None):
            return pltpu.make_async_remote_copy(
                src_ref=slot(*block) if src is None else src, dst_ref=slot(*block),
                send_sem=send_sems.at[k], recv_sem=recv_sems.at[k], device_id=to, device_id_type=MESH)

        mine = pltpu.make_async_copy(src_own, slot(*me), local_sem)
        mine.start()
        first = [copy(0, me, sibling, src=src_own)]
        first += [copy(1 + j, me, (*chip, pc), src=src_own) for j, chip in enumerate(chips)]
        for cp in first:
            cp.start()
        passed = [copy(4 + j, (*chip, pc), sibling) for j, chip in enumerate(chips)]
        for j, chip in enumerate(chips):
            copy(1 + j, (*chip, pc), me).wait_recv()
            passed[j].start()
        copy(0, sibling, me).wait_recv()
        for j, chip in enumerate(chips):
            copy(4 + j, (*chip, 1 - pc), me).wait_recv()
        for cp in first + passed:
            cp.wait_send()
        mine.wait()

    return pl.pallas_call(
        body, name=name,
        out_shape=jax.ShapeDtypeStruct((N_DEV,) + shape, x.dtype),
        in_specs=[pl.BlockSpec(memory_space=pl.ANY)],
        out_specs=pl.BlockSpec(memory_space=pl.ANY),
        scratch_shapes=[pltpu.SemaphoreType.DMA((7,)), pltpu.SemaphoreType.DMA((7,)), pltpu.SemaphoreType.DMA],
    )(x)


class _Exchange:
    def __init__(self, ins, out_shapes, n, build, aliases=None):
        self.ins, self.out_shapes, self.n, self.build = tuple(ins), tuple(out_shapes), n, build
        self.aliases = dict(aliases or {})

    def _descriptors(self, in_refs, out_refs, send_sems, recv_sems):
        sends, recvs = [], []
        for k, (src, dst, peer, landing) in enumerate(self.build(in_refs, out_refs)):
            sends.append(pltpu.make_async_remote_copy(
                src_ref=src, dst_ref=dst, send_sem=send_sems.at[k], recv_sem=recv_sems.at[k],
                device_id=peer, device_id_type=MESH))
            recvs.append(pltpu.make_async_remote_copy(
                src_ref=src, dst_ref=landing, send_sem=send_sems.at[k], recv_sem=recv_sems.at[k],
                device_id=peer, device_id_type=MESH))
        return sends, recvs

    def start(self, *refs):
        for cp in self._descriptors(*refs)[0]:
            cp.start()

    def finish(self, *refs):
        sends, recvs = self._descriptors(*refs)
        for cp in recvs:
            cp.wait_recv()
        for cp in sends:
            cp.wait_send()


ANY_SPEC = pl.BlockSpec(memory_space=pl.ANY)


def _hosted_call(body, *, name, grid, in_specs, out_specs, out_shape, args, scratch_shapes=(), side=None,
                 num_scalar_prefetch=0, io_aliases=None):
    in_specs, out_specs, out_shape = list(in_specs), list(out_specs), list(out_shape)
    n_in, n_out = len(in_specs) + num_scalar_prefetch, len(out_specs)
    kernel_body = body
    aliases = dict(io_aliases or {})
    if side is not None:
        s_in, s_out = len(side.ins), len(side.out_shapes)
        aliases.update({n_in + i: n_out + o for i, o in side.aliases.items()})

        def kernel_body(*refs):
            ins, s_ins = refs[:n_in], refs[n_in:n_in + s_in]
            outs = refs[n_in + s_in:n_in + s_in + n_out]
            s_outs = refs[n_in + s_in + n_out:n_in + s_in + n_out + s_out]
            scratch, sems = refs[n_in + s_in + n_out + s_out:-2], refs[-2:]
            first = functools.reduce(jnp.logical_and, [pl.program_id(a) == 0 for a in range(len(grid))])
            last = functools.reduce(jnp.logical_and, [pl.program_id(a) == g - 1 for a, g in enumerate(grid)])

            @pl.when(first)
            def _():
                side.start(s_ins, s_outs, *sems)

            body(*ins, *outs, *scratch)

            @pl.when(last)
            def _():
                side.finish(s_ins, s_outs, *sems)

        in_specs += [ANY_SPEC] * s_in
        out_specs += [ANY_SPEC] * s_out
        out_shape += list(side.out_shapes)
        scratch_shapes = list(scratch_shapes) + [pltpu.SemaphoreType.DMA((side.n,)),
                                                 pltpu.SemaphoreType.DMA((side.n,))]
        args = tuple(args) + side.ins
    if num_scalar_prefetch:
        grid_spec = pltpu.PrefetchScalarGridSpec(num_scalar_prefetch=num_scalar_prefetch, grid=grid,
                                                 in_specs=in_specs, out_specs=out_specs,
                                                 scratch_shapes=list(scratch_shapes))
        outs = pl.pallas_call(kernel_body, name=name, grid_spec=grid_spec, out_shape=out_shape,
                              input_output_aliases=aliases, compiler_params=_cparams())(*args)
    else:
        outs = pl.pallas_call(kernel_body, name=name, grid=grid, in_specs=in_specs, out_specs=out_specs,
                              out_shape=out_shape, scratch_shapes=list(scratch_shapes),
                              input_output_aliases=aliases, compiler_params=_cparams())(*args)
    return tuple(outs[:n_out]), tuple(outs[n_out:])


def _run_exchange(ex, *, name):
    s_in = len(ex.ins)

    def body(*refs):
        ins, outs, sems = refs[:s_in], refs[s_in:-2], refs[-2:]
        ex.start(ins, outs, *sems)
        ex.finish(ins, outs, *sems)

    outs = pl.pallas_call(
        body, name=name, out_shape=list(ex.out_shapes),
        in_specs=[ANY_SPEC] * s_in, out_specs=[ANY_SPEC] * len(ex.out_shapes),
        scratch_shapes=[pltpu.SemaphoreType.DMA((ex.n,)), pltpu.SemaphoreType.DMA((ex.n,))],
        input_output_aliases=ex.aliases,
    )(*ex.ins)
    return tuple(outs)


def _both(a, b):
    na, oa = len(a.ins), len(a.out_shapes)

    def build(ins, outs):
        return a.build(ins[:na], outs[:oa]) + b.build(ins[na:], outs[oa:])

    aliases = dict(a.aliases)
    aliases.update({na + i: oa + o for i, o in b.aliases.items()})
    return _Exchange(a.ins + b.ins, a.out_shapes + b.out_shapes, a.n + b.n, build, aliases)


def _other_chips(px, py):
    return [(px, 1 - py), (1 - px, py), (1 - px, 1 - py)]


def _gather_spread(w_flat, halves=True):
    rows, w = w_flat.shape
    hr = rows // 2 if halves else rows

    def build(ins, outs):
        px, py, pc = _position()
        mine = ins[0].at[pl.ds(pc * hr, hr)] if halves else ins[0]
        me = 4 * px + 2 * py + pc
        plan = [((px, py, 1 - pc), me ^ 1)]
        plan += [((qx, qy, pc), 4 * qx + 2 * qy + pc) for qx, qy in _other_chips(px, py)]
        return [(mine, outs[0].at[me], peer, outs[0].at[their]) for peer, their in plan]

    return _Exchange([w_flat], [jax.ShapeDtypeStruct((N_DEV, hr, w), w_flat.dtype)], 4, build)


def _gather_pass_on(gath):
    def build(ins, outs):
        px, py, pc = _position()
        out = []
        for qx, qy in _other_chips(px, py):
            blk = 4 * qx + 2 * qy + pc
            out.append((outs[0].at[blk], outs[0].at[blk], (px, py, 1 - pc), outs[0].at[blk ^ 1]))
        return out

    return _Exchange([gath], [jax.ShapeDtypeStruct(gath.shape, gath.dtype)], 3, build, aliases={0: 0})


def _rs_halves(g):
    n, rows, w = g.shape
    hr = rows // 2

    def build(ins, outs):
        px, py, pc = _position()
        return [(ins[0].at[:, pl.ds((1 - pc) * hr, hr), :], outs[0], (px, py, 1 - pc), outs[0])]

    return _Exchange([g], [jax.ShapeDtypeStruct((n, hr, w), g.dtype)], 1, build)


def _rs_chips(sb):
    def build(ins, outs):
        px, py, pc = _position()
        return [(ins[0].at[j], outs[0].at[j], (qx, qy, pc), outs[0].at[j])
                for j, (qx, qy) in enumerate(_other_chips(px, py))]

    return _Exchange([sb], [jax.ShapeDtypeStruct(sb.shape, sb.dtype)], 3, build)


def _rs_complete(buf):
    def build(ins, outs):
        px, py, pc = _position()
        return [(outs[0].at[pc], outs[0].at[pc], (px, py, 1 - pc), outs[0].at[1 - pc])]

    return _Exchange([buf], [jax.ShapeDtypeStruct(buf.shape, buf.dtype)], 1, build, aliases={0: 0})


def _rs_partial(g, recv, ids, *, name, tr=128):
    _, rows, w = g.shape
    hr = rows // 2
    nb = hr // tr

    def body(ids_ref, g_ref, r_ref, o_ref):
        o_ref[0] = (g_ref[0] + r_ref[0]).astype(BF16)

    grid_spec = pltpu.PrefetchScalarGridSpec(
        num_scalar_prefetch=1, grid=(3, nb),
        in_specs=[pl.BlockSpec((1, tr, w), lambda j, i, ids: (ids[1] ^ (j + 1), ids[0] * nb + i, 0)),
                  pl.BlockSpec((1, tr, w), lambda j, i, ids: (ids[1] ^ (j + 1), i, 0))],
        out_specs=pl.BlockSpec((1, tr, w), lambda j, i, ids: (j, i, 0)))
    return pl.pallas_call(
        body, name=name, grid_spec=grid_spec,
        out_shape=jax.ShapeDtypeStruct((3, hr, w), BF16),
        compiler_params=_cparams(),
    )(ids, g, recv)


def _rs_total(g, recv, got, ids, *, name, tr=128):
    _, rows, w = g.shape
    hr = rows // 2
    nb = hr // tr

    def body(ids_ref, g_ref, r_ref, got_ref, o_ref):
        acc = g_ref[0] + r_ref[0]
        for j in range(3):
            acc = acc + got_ref[j].astype(F32)
        o_ref[0] = acc

    grid_spec = pltpu.PrefetchScalarGridSpec(
        num_scalar_prefetch=1, grid=(nb,),
        in_specs=[pl.BlockSpec((1, tr, w), lambda i, ids: (ids[1], ids[0] * nb + i, 0)),
                  pl.BlockSpec((1, tr, w), lambda i, ids: (ids[1], i, 0)),
                  pl.BlockSpec((3, tr, w), lambda i, ids: (0, i, 0))],
        out_specs=pl.BlockSpec((1, tr, w), lambda i, ids: (ids[0], i, 0)))
    return pl.pallas_call(
        body, name=name, grid_spec=grid_spec,
        out_shape=jax.ShapeDtypeStruct((2, hr, w), F32),
        compiler_params=_cparams(),
    )(ids, g, recv, got)


class _ReduceScatter:
    def __init__(self, g, ids, tag):
        self.g, self.ids, self.tag, self.stage, self.result = g, ids, tag, 0, None

    def next_exchange(self):
        if self.stage == 0:
            return _rs_halves(self.g)
        if self.stage == 1:
            return _rs_chips(self.sb)
        return _rs_complete(self.buf)

    def done(self, outs):
        if self.stage == 0:
            self.recv = outs[0]
            hr = self.recv.shape[1]
            self.tr = max(t for t in range(16, 513, 16) if hr % t == 0)
            self.sb = _rs_partial(self.g, self.recv, self.ids, name=f"{self.tag}_partial", tr=self.tr)
        elif self.stage == 1:
            self.buf = _rs_total(self.g, self.recv, outs[0], self.ids, name=f"{self.tag}_total", tr=self.tr)
        else:
            _, hr, w = outs[0].shape
            self.result = outs[0].reshape(2 * hr, w)
        self.stage += 1

    def finish_alone(self):
        names = ("halves", "chips", "complete")
        while self.stage < 3:
            self.done(_run_exchange(self.next_exchange(), name=f"{self.tag}_{names[self.stage]}"))
        return self.result


def _flat_rows():
    used = sum(r for _, r in FSDP_SECTIONS)
    return used, -(-used // ROW_ALIGN) * ROW_ALIGN


def _cols_to_chunks(full):
    rows, cols = full.shape
    t = full.reshape(rows, N_CHIPS, cols // N_CHIPS).transpose(1, 0, 2)
    return t.reshape(N_CHIPS, -1, FLAT_W)


def _chunks_to_cols(chunks, rows, cols):
    return chunks.reshape(N_CHIPS, rows, cols // N_CHIPS).transpose(1, 0, 2).reshape(rows, cols)


def _pad_heads(w, real):
    lead = w.shape[:-1]
    t = w.reshape(lead + (HEADS, real))
    t = jnp.pad(t, [(0, 0)] * len(lead) + [(0, 0), (0, HEAD_PAD - real)])
    return t.reshape(lead + (HEADS * HEAD_PAD,))


def _unpad_heads(w, real):
    lead = w.shape[:-1]
    return w.reshape(lead + (HEADS, HEAD_PAD))[..., :real].reshape(lead + (HEADS * real,))


def _pad_value_lanes(w, axis):
    w = jnp.moveaxis(w, axis, -1)
    lead = w.shape[:-1]
    t = w.reshape(lead + (HEADS, 64))
    t = jnp.pad(t, [(0, 0)] * len(lead) + [(0, 0), (HEAD_PAD - 64, 0)])
    return jnp.moveaxis(t.reshape(lead + (HEADS * HEAD_PAD,)), -1, axis)


def _unpad_value_lanes(w, axis):
    w = jnp.moveaxis(w, axis, -1)
    lead = w.shape[:-1]
    t = w.reshape(lead + (HEADS, HEAD_PAD))[..., HEAD_PAD - 64:]
    return jnp.moveaxis(t.reshape(lead + (HEADS * 64,)), -1, axis)


def _pad_w_in_t(wt):
    z = jnp.zeros((NOPE, wt.shape[1]), wt.dtype)
    z2 = jnp.zeros((HEAD_PAD - NOPE - ROPE, wt.shape[1]), wt.dtype)
    return jnp.concatenate([wt[:1408], z, wt[1408:], z2], axis=0)


def _unpad_w_in_t(wt):
    return jnp.concatenate([wt[:1408], wt[1408 + NOPE:1408 + NOPE + ROPE]], axis=0)


def _rope_tables(positions):
    freqs = ROPE_THETA ** (-jnp.arange(0, ROPE, 2, dtype=F32) / ROPE)
    ang = positions.astype(F32)[..., None] * freqs
    cos, sin = jnp.cos(ang), jnp.sin(ang)
    lead = cos.shape[:-1]
    ones = jnp.ones(lead + (NOPE,), F32)
    zeros_n = jnp.zeros(lead + (NOPE,), F32)
    zeros_p = jnp.zeros(lead + (HEAD_PAD - NOPE - ROPE,), F32)
    ctab = jnp.concatenate([ones, cos, cos, zeros_p], axis=-1)
    stab = jnp.concatenate([zeros_n, -sin, sin, zeros_p], axis=-1)
    return ctab, stab


def _mix_weights(full):
    return dict(
        w_in_t=_pad_w_in_t(full["w_in_t"]),
        w_uq=_pad_heads(full["mla_w_uq"], NOPE + ROPE),
        w_ukv=full["mla_w_ukv"],
        w_out=jnp.concatenate([_pad_value_lanes(full["w_out"][D_GMLP:], 0), full["w_out"][:D_GMLP]], axis=0),
    )


def _small_weights(p, l):
    ws = p["gmlp_ws"][l]
    tril = jnp.tril(jnp.ones((CHUNK, CHUNK), bool))
    bs = p["gmlp_bs"][l]
    bexp = jnp.repeat(bs.reshape(GROUPS // 2, 2, CHUNK).transpose(0, 2, 1), GROUP_DIM, axis=2)
    return dict(
        ws=ws,
        wst=jnp.where(tril[None], ws, 0.0).transpose(0, 2, 1).astype(BF16),
        bexp=bexp,
        g_mix=p["norm_mix_g"][l][None],
        g_ffn=p["norm_ffn_g"][l][None],
        g_q=p["mla_q_norm_g"][l][None],
        g_kv=p["mla_kv_norm_g"][l][None],
        g_og=p["out_norm_gmlp_g"][l][None],
        g_oa=_pad_value_lanes(p["out_norm_mla_g"][l], 0)[None],
    )


def _local_step(x3, target3, positions, mods, final_g, plan):
    bsz, seq, d = x3.shape
    tok = bsz * seq
    tmt = min(512, seq)
    tmk = min(1024, seq)
    tmw = min(2048, tok)
    chunk = (None, None, FLAT_W, FLAT_W)
    chunk2 = (2, None, FLAT_W, FLAT_W)
    ff_grad_shape = (N_CHIPS, 2 * FLAT_W, FLAT_W)
    ctab, stab = _rope_tables(positions)
    lw = [None] * DEPTH

    def flat(t):
        return t.reshape(tok, t.shape[-1])

    def cube(t):
        return t.reshape(bsz, seq, t.shape[-1])

    def carrying(l, tag, fn, *args, **kw):
        side = plan.host(l, tag)
        if side is None:
            return fn(*args, **kw)
        res, side_outs = fn(*args, side=side, **kw)
        plan.hosted(l, tag, side_outs)
        return res

    saved = []
    x = x3
    for l in range(DEPTH):
        lw[l] = plan.layer(l)
        w, mod = lw[l], mods[l]
        if l == 0:
            h1 = carrying(l, "fwd_normmod1", _normmod_fwd, x, w["g_mix"], mod, SHIFT1, SCALE1,
                          name=f"l{l}_normmod1")
        else:
            h1 = h1_next
        z = cube(_mm(flat(h1), w["w_in_t"], dims="nt", name=f"l{l}_w_in", tm=tmt, tn=D_IN_PAD, tk=d,
                     out_dtypes=(BF16,)))
        yg = _gmlp_fwd(z, w["ws"], w["bexp"], w["g_og"], name=f"l{l}_gmlp_fwd")
        q, kv, kp = _mla_prep_fwd(z, w["g_q"], w["g_kv"], w["w_uq"], w["w_ukv"], ctab, stab, name=f"l{l}_mla_prep")
        o, lse = carrying(l, "fwd_attn", _attn_fwd, q, kv, kp, name=f"l{l}_attn_fwd")
        y = _onorm_fwd(o, yg, w["g_oa"], name=f"l{l}_onorm_fwd")

        def normmod(xv, gv, gm, shift_row, scale_row):
            m = gm[0]
            return _rms_fwd(xv, gv, d) * (1.0 + m[scale_row:scale_row + 1, :]) + m[shift_row:shift_row + 1, :]

        def out_epi(po, xv, gm, gf):
            x_new = xv + gm[0][GATE1:GATE1 + 1, :] * po
            return po, x_new, normmod(x_new, gf, gm, SHIFT2, SCALE2)

        vec_spec = pl.BlockSpec((1, d), lambda i, j, k: (0, j))
        po, x_mid, h2 = carrying(l, "fwd_out_a", _mm, flat(y), w["w_out"], dims="nn", name=f"l{l}_w_out",
                                 tm=tmt, tn=d, tk=y.shape[-1], out_dtypes=(BF16, F32, BF16), epilogue=out_epi,
                                 extras=(flat(x), mod, w["g_ffn"]),
                                 extra_specs=(None, _mod_spec(tmt, d, seq), vec_spec))
        x_mid, h2 = cube(x_mid), cube(h2)

        def act_epi(acc):
            r = jnp.maximum(acc, 0.0)
            return (r * r,)

        r = carrying(l, "fwd_ff1", _mm, flat(h2), w["ff"], dims="nn", name=f"l{l}_w_ff1", tm=tmw, tn=FLAT_W,
                     tk=d, out_dtypes=(BF16,), epilogue=act_epi, weights_outer=True, n=D_FF,
                     b_block=(chunk, lambda i, j, k: (j, 0, 0, 0)))

        more = l + 1 < DEPTH

        def ff2_epi(acc, xv, gm, *nxt):
            x_new = xv + gm[0][GATE2:GATE2 + 1, :] * acc
            return (acc, x_new) + ((normmod(x_new, nxt[1], nxt[0], SHIFT1, SCALE1),) if more else ())

        mod_spec = _mod_spec(tmt, d, seq)
        outs = carrying(l, "fwd_ff2", _mm, r, w["ff"], dims="nn", name=f"l{l}_w_ff2", tm=tmt, tn=d, tk=2 * FLAT_W,
                        out_dtypes=(BF16, F32) + ((BF16,) if more else ()), epilogue=ff2_epi,
                        extras=(flat(x_mid), mod) + ((mods[l + 1], plan.layer(l + 1)["g_mix"]) if more else ()),
                        extra_specs=(None, mod_spec) + ((mod_spec, vec_spec) if more else ()), n=d,
                        b_block=(chunk2, lambda i, j, k: (k, 1, 0, 0)))
        f, x_out = outs[0], outs[1]
        h1_next = cube(outs[2]) if more else None
        saved.append(dict(x_in=x, h1=h1, z=z, q=q, kv=kv, kp=kp, o=o, lse=lse, y=y, po=cube(po),
                          x_mid=x_mid, h2=h2, r=r, f=cube(f)))
        x = cube(x_out)

    grads = [dict() for _ in range(DEPTH)]
    dmods = [None] * DEPTH
    top = DEPTH - 1
    node = _resnode_bwd(x, final_g[None], name="final_loss_bwd", target3=target3,
                        branch3=saved[top]["f"], mod_gate=mods[top], gate_row=GATE2)
    loss_part = node["loss"][0, 0]
    d_final_g = node["dg"][0]
    plan.scalars(loss_part, d_final_g)
    for l in range(DEPTH - 1, -1, -1):
        w, mod, s = lw[l], mods[l], saved[l]
        dx_out, dfb, dgate2 = node["dx"], flat(node["dbr"]), node["dgate"][:, 0]

        def dact_epi(acc, rv):
            return (acc * (2.0 * jnp.sqrt(rv.astype(F32))),)

        da = carrying(l, "bwd_d_r", _mm, dfb, w["ff"], dims="nt", name=f"l{l}_d_r", tm=tmw, tn=FLAT_W, tk=d,
                      out_dtypes=(BF16,), epilogue=dact_epi, extras=(s["r"],), weights_outer=True, n=D_FF,
                      b_block=(chunk, lambda i, j, k: (j, 1, 0, 0)))
        g_ff = carrying(l, "bwd_dw_ff2", _mm, s["r"], dfb, dims="tn", name=f"l{l}_dw_ff2", tm=FLAT_W, tn=d,
                        tk=2048, out_into=(ff_grad_shape, (None, FLAT_W, FLAT_W), lambda i, j, k: (i, 1, 0), None))
        g_ff = carrying(l, "bwd_dw_ff1", _mm, flat(s["h2"]), da, dims="tn", name=f"l{l}_dw_ff1", tm=d, tn=FLAT_W,
                        tk=2048, out_into=(ff_grad_shape, (None, FLAT_W, FLAT_W), lambda i, j, k: (j, 0, 0), g_ff))
        plan.ff_grads(l, g_ff)
        dh2 = carrying(l, "bwd_d_h2", _mm, da, w["ff"], dims="nt", name=f"l{l}_d_h2", tm=tmk, tn=d, tk=2 * FLAT_W,
                       n=d, b_block=(chunk2, lambda i, j, k: (k, 0, 0, 0)), out_dtypes=(BF16,))
        node = carrying(l, "bwd_resnode_ffn", _resnode_bwd, s["x_mid"], w["g_ffn"], name=f"l{l}_resnode_ffn",
                        dh3=cube(dh2), dres3=dx_out, mod_nm=mod, rows=(SHIFT2, SCALE2), branch3=s["po"],
                        mod_gate=mod, gate_row=GATE1)
        grads[l]["norm_ffn_g"] = node["dg"][0]
        dshift2, dscale2 = node["dnm"][:, 0], node["dnm"][:, 1]
        dx_mid, dpo, dgate1 = node["dx"], flat(node["dbr"]), node["dgate"][:, 0]

        wy = s["y"].shape[-1]
        dy = cube(carrying(l, "bwd_d_y", _mm, dpo, w["w_out"], dims="nt", name=f"l{l}_d_y", tm=tmt, tn=wy, tk=d,
                           out_dtypes=(BF16,)))
        dw_out = _mm(flat(s["y"]), dpo, dims="tn", name=f"l{l}_dw_out", tm=wy // 3, tn=d, tk=2048)
        hw = HEADS * HEAD_PAD
        grads[l]["w_out"] = jnp.concatenate([dw_out[hw:], _unpad_value_lanes(dw_out[:hw], 0)], axis=0)

        dz, dws, dbs, dg_og = _gmlp_bwd(s["z"], dy, w["ws"], w["wst"], w["bexp"], w["g_og"],
                                        name=f"l{l}_gmlp_bwd", dy_col=hw // D_GMLP)
        grads[l]["gmlp_ws"], grads[l]["gmlp_bs"], grads[l]["out_norm_gmlp_g"] = dws, dbs, dg_og[0]

        do, dl, dg_oa = _onorm_bwd(s["o"], dy, w["g_oa"], name=f"l{l}_onorm_bwd")
        grads[l]["out_norm_mla_g"] = _unpad_value_lanes(dg_oa[0], 0)
        plan.small_ready(l, grads[l])
        dq, dk, dv = carrying(l, "bwd_attn_dkv", _attn_bwd, s["q"], s["kv"], s["kp"], do, s["lse"], dl,
                              name=f"l{l}_attn_bwd")
        dz, cq, dqb, ckv, dkvb, dg_q, dg_kv = _mla_prep_bwd(
            s["z"], dz, dq, dk, dv, w["g_q"], w["g_kv"], w["w_uq"], w["w_ukv"], ctab, stab,
            name=f"l{l}_mla_prep_bwd")
        grads[l]["mla_q_norm_g"], grads[l]["mla_kv_norm_g"] = dg_q[0], dg_kv[0]
        dw_uq = carrying(l, "bwd_dw_uq", _mm, flat(cq), flat(dqb), dims="tn", name=f"l{l}_dw_uq", tm=Q_RANK,
                         tn=1024, tk=4096)
        grads[l]["mla_w_uq"] = _unpad_heads(dw_uq, NOPE + ROPE)
        grads[l]["w_in_t"] = _unpad_w_in_t(carrying(l, "bwd_dw_in", _mm, flat(dz), flat(s["h1"]), dims="tn",
                                                    name=f"l{l}_dw_in", tm=D_IN_PAD // 2, tn=d, tk=2048))
        grads[l]["mla_w_ukv"] = carrying(l, "bwd_dw_ukv", _mm, flat(ckv), flat(dkvb), dims="tn", name=f"l{l}_dw_ukv",
                                         tm=KV_RANK, tn=1024, tk=4096)
        plan.layer_grads(l, grads[l])
        dh1 = carrying(l, "bwd_d_h1", _mm, flat(dz), w["w_in_t"], dims="nn", name=f"l{l}_d_h1", tm=tmt, tn=d,
                       tk=D_IN_PAD, out_dtypes=(BF16,))
        below = dict(branch3=saved[l - 1]["f"], mod_gate=mods[l - 1], gate_row=GATE2) if l > 0 else {}
        node = carrying(l, "bwd_resnode_mix", _resnode_bwd, s["x_in"], w["g_mix"], name=f"l{l}_resnode_mix",
                        dh3=cube(dh1), dres3=dx_mid, mod_nm=mod, rows=(SHIFT1, SCALE1), **below)
        grads[l]["norm_mix_g"] = node["dg"][0]
        dshift1, dscale1 = node["dnm"][:, 0], node["dnm"][:, 1]
        dmods[l] = jnp.stack([dshift1, dscale1, dgate1, dshift2, dscale2, dgate2], axis=1)
    return node["dx"], dmods


W_NAMES = ("w_ada", "b_ada", "norm_mix_g", "w_in", "gmlp_ws", "gmlp_bs", "mla_q_norm_g", "mla_kv_norm_g",
           "mla_w_uq", "mla_w_ukv", "out_norm_gmlp_g", "out_norm_mla_g", "w_out", "norm_ffn_g", "w_ff1", "w_ff2",
           "final_norm_g")
FLAT_KEY = {"w_in": "w_in", "w_uq": "mla_w_uq", "w_ukv": "mla_w_ukv", "w_out": "w_out", "w_ff1": "w_ff1",
            "w_ff2": "w_ff2"}
COL_SHARDED = ("w_in", "w_uq", "w_ukv", "w_ff1")
FULL_SHAPE = {"w_in": (D_MODEL, D_IN), "w_uq": (Q_RANK, HEADS * (NOPE + ROPE)), "w_ukv": (KV_RANK, HEADS * 128),
              "w_out": (D_MODEL, D_MODEL)}
SMALL_LAYER_NAMES = ("gmlp_ws", "gmlp_bs", "out_norm_gmlp_g", "out_norm_mla_g", "norm_ffn_g")
LATE_SMALL_NAMES = ("norm_mix_g", "mla_q_norm_g", "mla_kv_norm_g")


def _silu(v):
    return v * (1.0 / (1.0 + jnp.exp(-v)))


class _CommPlan:
    FWD = {"fwd_attn": ("ff", 0, "spread"), "fwd_out_a": ("ff", 0, "pass"),
           "fwd_ff1": ("mix", 1, "spread"), "fwd_ff2": ("mix", 1, "pass")}
    BWD = {"bwd_d_r": ("mix", 1), "bwd_dw_ff2": ("mix", 1), "bwd_dw_ff1": ("mix", 1),
           "bwd_d_h2": ("ff", 0), "bwd_attn_dkv": ("ff", 0), "bwd_dw_uq": ("ff", 0)}
    BWD_LAST = {"bwd_d_h1": ("mix", 0), "bwd_resnode_mix": ("mix", 0)}
    SMALL = {"bwd_attn_dkv": "spread", "bwd_dw_uq": "pass"}

    def __init__(self, weights, ids, dev, core):
        self.weights, self.ids, self.dev, self.core = weights, ids, dev, core
        self.used, self.rows = _flat_rows()
        self.flat = {("mix", l): self._flat_mix(l) for l in range(DEPTH)}
        self.flat.update({("ff", l): jnp.concatenate([weights["w_ff1"][l], weights["w_ff2"][l]], axis=0).astype(BF16)
                          for l in range(DEPTH)})
        self.lw, self.rs, self.grads, self.spread = {}, {}, {}, {}
        self.small_vec, self.small_sum, self.small_spread, self.extra = {}, {}, None, {}
        self.lw = {l: _small_weights(weights, l) for l in range(DEPTH)}

    def _flat_mix(self, l):
        pieces = []
        for nm, _ in FSDP_SECTIONS:
            shard = self.weights[FLAT_KEY[nm]][l]
            pieces.append(shard.T if nm == "w_in" else shard.reshape(-1, FLAT_W))
        pieces.append(jnp.zeros((self.rows - self.used, FLAT_W), F32))
        return jnp.concatenate(pieces, axis=0).astype(BF16)

    def _arrived(self, group, l, gath):
        flat = self.flat[group, l]
        hr = flat.shape[0] // 2
        mine = lax.dynamic_slice(flat, (self.core * hr, 0), (hr, FLAT_W))
        gath = lax.dynamic_update_slice(gath, mine[None], (self.dev, 0, 0))
        if group == "ff":
            self.lw[l]["ff"] = gath.reshape(N_CHIPS, 2, hr, FLAT_W)
            return
        w_gath = gath.reshape(N_CHIPS, self.rows, FLAT_W)
        full, off = {}, 0
        for nm, nrows in FSDP_SECTIONS:
            sec = w_gath[:, off:off + nrows]
            off += nrows
            rows, cols = FULL_SHAPE[nm]
            if nm == "w_in":
                full["w_in_t"] = sec.reshape(cols, rows)
            else:
                full[FLAT_KEY[nm]] = (_chunks_to_cols(sec, rows, cols) if nm in COL_SHARDED
                                      else sec.reshape(rows, cols))
        self.lw[l].update(_mix_weights(full))

    def layer(self, l):
        return self.lw[l]

    def host(self, l, tag):
        if tag == "fwd_normmod1":
            return _gather_spread(self.flat["mix", 0]) if l == 0 else None
        if tag in self.FWD:
            group, ahead, what = self.FWD[tag]
            if l + ahead >= DEPTH:
                return None
            return _gather_spread(self.flat[group, l + ahead]) if what == "spread" else _gather_pass_on(self.spread[group])
        rs = self._rs_for(l, tag)
        ex = None if rs is None or rs.stage > 2 else rs.next_exchange()
        if tag in self.SMALL:
            small = (_gather_spread(self.small_vec[l], halves=False) if self.SMALL[tag] == "spread"
                     else _gather_pass_on(self.small_spread))
            ex = small if ex is None else _both(ex, small)
        return ex

    def _rs_for(self, l, tag):
        if tag in self.BWD_LAST:
            return self.rs.get(self.BWD_LAST[tag]) if l == 0 else None
        if tag not in self.BWD:
            return None
        group, ahead = self.BWD[tag]
        return self.rs.get((group, l + ahead))

    def hosted(self, l, tag, outs):
        if tag == "fwd_normmod1":
            self._arrived("mix", 0, _run_exchange(_gather_pass_on(outs[0]), name="l0_mix_gather_pass_on")[0])
        elif tag in self.FWD:
            group, ahead, what = self.FWD[tag]
            if what == "spread":
                self.spread[group] = outs[0]
            else:
                self._arrived(group, l + ahead, outs[0])
        else:
            rs = self._rs_for(l, tag)
            if rs is not None and rs.stage <= 2:
                rs.done(outs[:1])
                outs = outs[1:]
            if tag in self.SMALL:
                if self.SMALL[tag] == "spread":
                    self.small_spread = outs[0]
                else:
                    self._small_arrived(l, outs[0])

    def ff_grads(self, l, g_ff):
        self.rs["ff", l] = _ReduceScatter(g_ff, self.ids, f"l{l}_ff_rs")

    def layer_grads(self, l, grads):
        self.grads[l] = grads
        pieces = []
        for nm, nrows in FSDP_SECTIONS:
            if nm == "w_in":
                pieces.append(grads["w_in_t"].reshape(N_CHIPS, nrows, FLAT_W))
                continue
            g = grads[FLAT_KEY[nm]]
            pieces.append(_cols_to_chunks(g) if nm in COL_SHARDED else g.reshape(N_CHIPS, nrows, FLAT_W))
        pieces.append(jnp.zeros((N_CHIPS, self.rows - self.used, FLAT_W), F32))
        self.rs["mix", l] = _ReduceScatter(jnp.concatenate(pieces, axis=1), self.ids, f"l{l}_mix_rs")

    def scalars(self, loss_part, d_final_g):
        self.extra = {0: [loss_part[None]]}
        self.extra.setdefault(DEPTH - 1, []).insert(0, d_final_g)

    def small_ready(self, l, grads):
        parts = [grads[nm].reshape(-1) for nm in SMALL_LAYER_NAMES] + self.extra.get(l, [])
        vec = jnp.concatenate(parts)
        rows = -(-vec.shape[0] // (8 * FLAT_W)) * 8
        self.small_vec[l] = jnp.pad(vec, (0, rows * FLAT_W - vec.shape[0])).reshape(rows, FLAT_W)

    def _small_arrived(self, l, gath):
        gath = lax.dynamic_update_slice(gath, self.small_vec[l][None], (self.dev, 0, 0))
        self.small_sum[l] = _sum_leading(gath, name=f"l{l}_small_sum").reshape(-1)

    def finish(self, dmod):
        late = jnp.concatenate([jnp.stack([self.grads[l][nm] for l in range(DEPTH)], axis=0).reshape(-1)
                                for nm in LATE_SMALL_NAMES])
        head = -(-late.shape[0] // (8 * FLAT_W)) * 8
        late = jnp.pad(late, (0, head * FLAT_W - late.shape[0])).reshape(head, FLAT_W)
        vec = jnp.concatenate([late, dmod], axis=0)
        rs = self.rs["mix", 0]
        while rs.stage < 2:
            rs.done(_run_exchange(rs.next_exchange(), name=f"l0_mix_rs_stage{rs.stage}"))
        outs = _run_exchange(_both(rs.next_exchange(), _gather_spread(vec, halves=False)), name="final_spread")
        rs.done(outs[:1])
        (gath,) = _run_exchange(_gather_pass_on(outs[1]), name="final_pass_on")
        gath = lax.dynamic_update_slice(gath, vec[None], (self.dev, 0, 0))
        late_sum = _sum_leading(gath[:, :head], name="late_small_sum").reshape(-1)
        loss, res = self._small_grads(late_sum)
        return loss, res, gath[:, head:]

    def _small_grads(self, late):
        out = {nm: [] for nm in SMALL_LAYER_NAMES}
        for l in range(DEPTH):
            off = 0
            for nm in SMALL_LAYER_NAMES:
                size = self.weights[nm][l].size
                out[nm].append(self.small_sum[l][off:off + size].reshape(self.weights[nm].shape[1:]))
                off += size
            if l == DEPTH - 1:
                final = self.small_sum[l][off:off + self.weights["final_norm_g"].size]
                off += final.shape[0]
            if l == 0:
                loss = self.small_sum[l][off]
        res = {nm: jnp.stack(parts, axis=0) for nm, parts in out.items()}
        res["final_norm_g"] = final
        off = 0
        for nm in LATE_SMALL_NAMES:
            size = self.weights[nm].size
            res[nm] = late[off:off + size].reshape(self.weights[nm].shape)
            off += size
        return loss, res

    def mix_grads(self):
        per = {FLAT_KEY[nm]: [] for nm, _ in FSDP_SECTIONS}
        for l in range(DEPTH):
            shard, off = self.rs["mix", l].result, 0
            for nm, nrows in FSDP_SECTIONS:
                key = FLAT_KEY[nm]
                sec = shard[off:off + nrows]
                per[key].append(sec.T if nm == "w_in" else sec.reshape(self.weights[key].shape[1:]))
                off += nrows
        return {key: jnp.stack(parts, axis=0) for key, parts in per.items()}

    def ff_shards(self):
        return [self.rs["ff", l].result for l in range(DEPTH)]


def kernel(x, c, positions, w_ada, b_ada, norm_mix_g, w_in, gmlp_ws, gmlp_bs, mla_q_norm_g, mla_kv_norm_g, mla_w_uq, mla_w_ukv, out_norm_gmlp_g, out_norm_mla_g, w_out, norm_ffn_g, w_ff1, w_ff2, final_norm_g, loss_target, m_w_ada, m_b_ada, m_norm_mix_g, m_w_in, m_gmlp_ws, m_gmlp_bs, m_mla_q_norm_g, m_mla_kv_norm_g, m_mla_w_uq, m_mla_w_ukv, m_out_norm_gmlp_g, m_out_norm_mla_g, m_w_out, m_norm_ffn_g, m_w_ff1, m_w_ff2, m_final_norm_g, v_w_ada, v_b_ada, v_norm_mix_g, v_w_in, v_gmlp_ws, v_gmlp_bs, v_mla_q_norm_g, v_mla_kv_norm_g, v_mla_w_uq, v_mla_w_ukv, v_out_norm_gmlp_g, v_out_norm_mla_g, v_w_out, v_norm_ffn_g, v_w_ff1, v_w_ff2, v_final_norm_g):
    weights = dict(w_ada=w_ada, b_ada=b_ada, norm_mix_g=norm_mix_g, w_in=w_in, gmlp_ws=gmlp_ws, gmlp_bs=gmlp_bs,
                   mla_q_norm_g=mla_q_norm_g, mla_kv_norm_g=mla_kv_norm_g, mla_w_uq=mla_w_uq, mla_w_ukv=mla_w_ukv,
                   out_norm_gmlp_g=out_norm_gmlp_g, out_norm_mla_g=out_norm_mla_g, w_out=w_out,
                   norm_ffn_g=norm_ffn_g, w_ff1=w_ff1, w_ff2=w_ff2, final_norm_g=final_norm_g)
    mom_m = dict(zip(W_NAMES, (m_w_ada, m_b_ada, m_norm_mix_g, m_w_in, m_gmlp_ws, m_gmlp_bs, m_mla_q_norm_g,
                               m_mla_kv_norm_g, m_mla_w_uq, m_mla_w_ukv, m_out_norm_gmlp_g, m_out_norm_mla_g,
                               m_w_out, m_norm_ffn_g, m_w_ff1, m_w_ff2, m_final_norm_g)))
    mom_v = dict(zip(W_NAMES, (v_w_ada, v_b_ada, v_norm_mix_g, v_w_in, v_gmlp_ws, v_gmlp_bs, v_mla_q_norm_g,
                               v_mla_kv_norm_g, v_mla_w_uq, v_mla_w_ukv, v_out_norm_gmlp_g, v_out_norm_mla_g,
                               v_w_out, v_norm_ffn_g, v_w_ff1, v_w_ff2, v_final_norm_g)))
    bsz, seq, d = x.shape
    px, py, pc = _position()
    chip = 2 * px + py
    dev = 2 * chip + pc
    ids = jnp.stack([pc, chip]).astype(jnp.int32)
    n_ex = N_DEV * bsz
    ada_cols = w_ada.shape[-1]

    c_all = _allgather8(c.reshape(bsz * d // 128, 128), name="gather_c").reshape(n_ex, d)
    mod_parts = []
    for l in range(DEPTH):
        bias = lax.dynamic_slice(b_ada[l], (chip * ada_cols,), (ada_cols,))[None]
        mod_parts.append(_mm(c_all, w_ada, dims="nn", name=f"l{l}_mod", tm=n_ex, tn=ada_cols, tk=d, n=ada_cols,
                             b_block=((None, d, ada_cols), lambda i, j, k, l=l: (l, k, j)),
                             epilogue=lambda acc, bv: (acc + bv,), extras=(bias,),
                             extra_specs=(pl.BlockSpec((1, ada_cols), lambda i, j, k: (0, j)),), a_fn=_silu))
    mod_g = _allgather8(jnp.concatenate(mod_parts, axis=0), name="gather_mod")
    mod_g = mod_g.reshape(N_CHIPS, 2, DEPTH, n_ex, ada_cols)[:, 0]
    mod_full = mod_g.transpose(1, 2, 0, 3).reshape(DEPTH, n_ex, N_CHIPS * ada_cols)
    mod_mine = lax.dynamic_slice(mod_full, (0, dev * bsz, 0), (DEPTH, bsz, N_MOD * d))
    mod_mine = jnp.pad(mod_mine.reshape(DEPTH, bsz, N_MOD, d), ((0, 0), (0, 0), (0, MOD_ROWS - N_MOD), (0, 0)))
    mods = [mod_mine[l] for l in range(DEPTH)]

    plan = _CommPlan(weights, ids, dev, pc)
    grad_x, dmods = _local_step(x, loss_target, positions, mods, final_norm_g, plan)

    dmod = jnp.stack(dmods, axis=1).reshape(bsz * DEPTH * N_MOD, d)
    loss, small, dmod_all = plan.finish(dmod)
    grad = plan.mix_grads()
    grad.update(small)
    dmod_all = dmod_all.reshape(n_ex, DEPTH, N_MOD * d)
    gw, gb = [], []
    for l in range(DEPTH):
        dm = dmod_all[:, l]
        dm_cols = lax.dynamic_slice(dm, (0, chip * ada_cols), (n_ex, ada_cols))
        gw.append(_mm(c_all, dm_cols, dims="tn", name=f"l{l}_dw_ada", tm=d, tn=ada_cols, tk=n_ex, a_fn=_silu,
                      out_into=(w_ada.shape, (None, d, ada_cols), lambda i, j, k, l=l: (l, i, j),
                                gw[-1] if gw else None)))
        gb.append(_sum_leading(dm.reshape(n_ex, N_MOD * d // FLAT_W, FLAT_W), name=f"l{l}_db_ada").reshape(-1))
    grad["w_ada"] = gw[-1]
    grad["b_ada"] = jnp.stack(gb, axis=0)

    delta, new_m, new_v = {}, {}, {}
    ff_bufs = plan.ff_shards()
    for nm, row_off in (("w_ff1", 0), ("w_ff2", FLAT_W)):
        grad[nm], delta[nm], new_m[nm], new_v[nm] = _adamw_layers(
            weights[nm], mom_m[nm], mom_v[nm], ff_bufs, row_off, name=f"adamw_{nm}")
    for nm in W_NAMES:
        if nm not in delta:
            delta[nm], new_m[nm], new_v[nm] = _adamw(weights[nm], grad[nm], mom_m[nm], mom_v[nm],
                                                     name=f"adamw_{nm}")
    return (loss, grad_x, *[grad[nm] for nm in W_NAMES], *[delta[nm] for nm in W_NAMES],
            *[new_m[nm] for nm in W_NAMES], *[new_v[nm] for nm in W_NAMES])
```

```python
import functools
import math

import jax
import jax.numpy as jnp
from jax import lax
from jax.experimental import pallas as pl
from jax.experimental.pallas import tpu as pltpu

F32 = jnp.float32
BF16 = jnp.bfloat16

D_MODEL = 1024
DEPTH = 2
D_GMLP = 512
GROUPS = 8
GROUP_DIM = 64
CHUNK = 128
HEADS = 8
NOPE = 64
ROPE = 32
HEAD_PAD = 128
Q_RANK = 256
KV_RANK = 128
D_FF = 4096
N_MOD = 6
MOD_ROWS = 8
EPS = 1e-6
ROPE_THETA = 10000.0
D_IN = 1440
D_IN_PAD = 1536
ATTN_SCALE = (NOPE + ROPE) ** -0.5
LOG2E = math.log2(math.e)
SCALE_LOG2 = ATTN_SCALE * LOG2E
N_CHIPS = 4
N_DEV = 8

ADAM_LR = 0.001
ADAM_B1 = 0.9
ADAM_B2 = 0.999
ADAM_EPS = 1e-08
ADAM_WD = 0.01
ADAM_STEP = 10

VMEM_LIMIT = 48 * 1024 * 1024
FLAT_W = 1024
ROW_ALIGN = 256

NN = (((1,), (0,)), ((), ()))
NT = (((1,), (1,)), ((), ()))
TN = (((0,), (0,)), ((), ()))
MESH = pl.DeviceIdType.MESH

SHIFT1, SCALE1, GATE1, SHIFT2, SCALE2, GATE2 = range(6)

FSDP_SECTIONS = (("w_out", 256), ("w_in", 360), ("w_uq", 48), ("w_ukv", 32))


def _cparams(vmem=VMEM_LIMIT):
    return pltpu.CompilerParams(vmem_limit_bytes=vmem)


def _dot(a, b, dims=NN):
    return lax.dot_general(a, b, dims, preferred_element_type=F32)


def _iota(shape, axis):
    return lax.broadcasted_iota(jnp.int32, shape, axis)


def _gelu(x):
    k = math.sqrt(2.0 / math.pi)
    return 0.5 * x * (1.0 + jnp.tanh(k * (x + 0.044715 * (x * x * x))))


def _gelu_and_grad(x):
    k = math.sqrt(2.0 / math.pi)
    x2 = x * x
    t = jnp.tanh(k * (x + 0.044715 * (x2 * x)))
    half = 0.5 * (1.0 + t)
    return x * half, half + 0.5 * x * (1.0 - t * t) * (k * (1.0 + 3.0 * 0.044715 * x2))


def _rms_fwd(x, g, n):
    r = lax.rsqrt(jnp.sum(x * x, axis=-1, keepdims=True) * (1.0 / n) + EPS)
    return x * r * g


def _rms_bwd(x, g, dy, n):
    r = lax.rsqrt(jnp.sum(x * x, axis=-1, keepdims=True) * (1.0 / n) + EPS)
    xh = x * r
    dxh = dy * g
    dx = r * (dxh - xh * (jnp.sum(dxh * xh, axis=-1, keepdims=True) * (1.0 / n)))
    dg = jnp.sum(dy * xh, axis=0, keepdims=True)
    return dx, dg


def _pick_rows(rows, limit):
    if rows <= limit:
        return rows
    for t in range(limit, 7, -8):
        if rows % t == 0:
            return t
    return rows


def _mm(a, b, *, dims, name, tm=512, tn=1024, tk=1024, out_dtypes=(F32,), epilogue=None,
        extras=(), extra_specs=(), a_fn=None, weights_outer=False, side=None, b_block=None, n=None,
        out_into=None):
    if dims == "tn":
        kk, m = a.shape
    else:
        m, kk = a.shape
    if n is None:
        n = b.shape[0] if dims == "nt" else b.shape[1]
    tm, tn, tk = min(tm, m), min(tn, n), min(tk, kk)
    assert m % tm == 0 and n % tn == 0 and kk % tk == 0, (name, a.shape, b.shape, tm, tn, tk)
    ni, nj, nk = m // tm, n // tn, kk // tk

    def spec(shape, pick):
        if weights_outer:
            return pl.BlockSpec(shape, lambda j, i, k: pick(i, j, k))
        return pl.BlockSpec(shape, pick)

    if dims == "tn":
        a_spec = spec((tk, tm), lambda i, j, k: (k, i))
    else:
        a_spec = spec((tm, tk), lambda i, j, k: (i, k))
    if b_block is not None:
        b_spec = spec(*b_block)
    elif dims == "nt":
        b_spec = spec((tn, tk), lambda i, j, k: (j, k))
    else:
        b_spec = spec((tk, tn), lambda i, j, k: (k, j))
    o_spec = spec((tm, tn), lambda i, j, k: (i, j))
    out_shape = [jax.ShapeDtypeStruct((m, n), dt) for dt in out_dtypes]
    out_specs = [o_spec] * len(out_dtypes)
    prev, io_aliases = (), {}
    if out_into is not None:
        full_shape, block, index, before = out_into
        assert len(out_dtypes) == 1 and not extras
        out_shape = [jax.ShapeDtypeStruct(full_shape, out_dtypes[0])]
        out_specs = [spec(block, index)]
        if before is not None:
            prev, io_aliases = (before,), {2: 0}
    assert not (weights_outer and extra_specs)
    dn = {"nn": NN, "nt": NT, "tn": TN}[dims]
    n_ex, n_out = len(extras), len(out_dtypes)
    e_specs = [o_spec if s is None else s for s in (tuple(extra_specs) + (None,) * n_ex)[:n_ex]]

    n_prev = len(prev)

    def body(*refs):
        a_ref, b_ref = refs[0], refs[1]
        e_refs = refs[2 + n_prev:2 + n_prev + n_ex]
        o_refs = refs[2 + n_prev + n_ex:2 + n_prev + n_ex + n_out]
        av = a_ref[...]
        if a_fn is not None:
            av = a_fn(av)
        bv = b_ref[...]
        if bv.ndim == 3:
            if dims == "nt":
                bv = jnp.concatenate([bv[c] for c in range(bv.shape[0])], axis=1)
            else:
                bv = bv.reshape(-1, bv.shape[-1])
        part = _dot(av.astype(BF16), bv.astype(BF16), dn)

        def finish(acc):
            outs = (acc,) if epilogue is None else epilogue(acc, *[e[...] for e in e_refs])
            for o_ref, o in zip(o_refs, outs):
                o_ref[...] = o.astype(o_ref.dtype)

        if nk == 1:
            finish(part)
        else:
            acc_ref = refs[-1]
            k = pl.program_id(2)

            @pl.when(k == 0)
            def _():
                acc_ref[...] = part

            @pl.when(k > 0)
            def _():
                acc_ref[...] += part

            @pl.when(k == nk - 1)
            def _():
                finish(acc_ref[...])

    outs, side_outs = _hosted_call(
        body, name=name, grid=(nj, ni, nk) if weights_outer else (ni, nj, nk),
        in_specs=[a_spec, b_spec] + [ANY_SPEC] * n_prev + e_specs,
        out_specs=out_specs, out_shape=out_shape,
        scratch_shapes=[pltpu.VMEM((tm, tn), F32)] if nk > 1 else [],
        args=(a, b, *prev, *extras), side=side, io_aliases=io_aliases)
    res = outs[0] if n_out == 1 else outs
    return res if side is None else (res, side_outs)


def _mod_spec(tm, tn, seq):
    return pl.BlockSpec((1, MOD_ROWS, tn), lambda i, j, k: ((i * tm) // seq, 0, j))


def _normmod_fwd(x3, g, mod, shift_row, scale_row, *, name, tb=512, side=None):
    bsz, seq, d = x3.shape
    tb = min(tb, seq)

    def body(x_ref, g_ref, mod_ref, h_ref):
        m = mod_ref[0]
        nrm = _rms_fwd(x_ref[0], g_ref[...], d)
        h = nrm * (1.0 + m[scale_row:scale_row + 1, :]) + m[shift_row:shift_row + 1, :]
        h_ref[0] = h.astype(BF16)

    outs, side_outs = _hosted_call(
        body, name=name, grid=(bsz, seq // tb),
        in_specs=[pl.BlockSpec((1, tb, d), lambda b, i: (b, i, 0)),
                  pl.BlockSpec((1, d), lambda b, i: (0, 0)),
                  pl.BlockSpec((1, MOD_ROWS, d), lambda b, i: (b, 0, 0))],
        out_specs=[pl.BlockSpec((1, tb, d), lambda b, i: (b, i, 0))],
        out_shape=[jax.ShapeDtypeStruct((bsz, seq, d), BF16)],
        args=(x3, g, mod), side=side)
    return outs[0] if side is None else (outs[0], side_outs)


def _pair_mean_exact(x, lo):
    s_lo = jnp.sum(jnp.where(lo, x, 0.0), axis=-1, keepdims=True)
    s_hi = jnp.sum(jnp.where(lo, 0.0, x), axis=-1, keepdims=True)
    return jnp.where(lo, s_lo, s_hi) * (1.0 / GROUP_DIM)


def _gmlp_pair_fwd(gv_p, w0, w1, bias, lo):
    mu = _pair_mean_exact(gv_p, lo)
    dlt = gv_p - mu
    var = _pair_mean_exact(dlt * dlt, lo)
    rstd = lax.rsqrt(var + EPS)
    vn = dlt * rstd
    vnb = vn.astype(BF16)
    mixed = jnp.where(lo, _dot(w0, vnb), _dot(w1, vnb)) + bias
    return vn, vnb, rstd, mixed


def _tril_bf16(w):
    t = w.shape[-1]
    return jnp.where(_iota((t, t), 1) <= _iota((t, t), 0), w, 0.0).astype(BF16)


def _gmlp_fwd(z3, ws, bexp, g_out, *, name):
    bsz, seq, _ = z3.shape
    nc = seq // CHUNK

    def body(u_ref, v_ref, ws_ref, b_ref, g_ref, y_ref):
        lo = _iota((CHUNK, 128), 1) < GROUP_DIM
        gu = _gelu(u_ref[0].astype(F32))
        gv = _gelu(v_ref[0].astype(F32))
        parts = []
        for p in range(GROUPS // 2):
            sl = slice(128 * p, 128 * p + 128)
            w0 = _tril_bf16(ws_ref[2 * p])
            w1 = _tril_bf16(ws_ref[2 * p + 1])
            _, _, _, mixed = _gmlp_pair_fwd(gv[:, sl], w0, w1, b_ref[p], lo)
            parts.append(gu[:, sl] * mixed)
        yg = jnp.concatenate(parts, axis=1)
        y_ref[0] = _rms_fwd(yg, g_ref[...], D_GMLP).astype(BF16)

    return pl.pallas_call(
        body, name=name, grid=(bsz, nc),
        in_specs=[pl.BlockSpec((1, CHUNK, D_GMLP), lambda b, i: (b, i, 0)),
                  pl.BlockSpec((1, CHUNK, D_GMLP), lambda b, i: (b, i, 1)),
                  pl.BlockSpec((GROUPS, CHUNK, CHUNK), lambda b, i: (0, 0, 0)),
                  pl.BlockSpec((GROUPS // 2, CHUNK, 128), lambda b, i: (0, 0, 0)),
                  pl.BlockSpec((1, D_GMLP), lambda b, i: (0, 0))],
        out_specs=pl.BlockSpec((1, CHUNK, D_GMLP), lambda b, i: (b, i, 0)),
        out_shape=jax.ShapeDtypeStruct((bsz, seq, D_GMLP), BF16),
        compiler_params=_cparams(),
    )(z3, z3, ws, bexp, g_out)


def _gmlp_bwd(z3, dyn3, ws, wst, bexp, g_out, *, name, dy_col):
    bsz, seq, _ = z3.shape
    nc = seq // CHUNK
    npair = GROUPS // 2

    def body(u_ref, v_ref, dy_ref, ws_ref, wst_ref, b_ref, g_ref, duv_ref, dws_ref, dbs_ref, dg_ref, dbacc):
        first = jnp.logical_and(pl.program_id(0) == 0, pl.program_id(1) == 0)
        last = jnp.logical_and(pl.program_id(0) == bsz - 1, pl.program_id(1) == nc - 1)

        @pl.when(first)
        def _():
            dws_ref[...] = jnp.zeros_like(dws_ref)
            dg_ref[...] = jnp.zeros_like(dg_ref)
            dbacc[...] = jnp.zeros_like(dbacc)

        lo = _iota((CHUNK, 128), 1) < GROUP_DIM
        tril = _iota((CHUNK, CHUNK), 1) <= _iota((CHUNK, CHUNK), 0)
        u = u_ref[0].astype(F32)
        v = v_ref[0].astype(F32)
        gu, dgu = _gelu_and_grad(u)
        gv, dgv_dv = _gelu_and_grad(v)
        fwd = []
        for p in range(npair):
            sl = slice(128 * p, 128 * p + 128)
            w0 = _tril_bf16(ws_ref[2 * p])
            w1 = _tril_bf16(ws_ref[2 * p + 1])
            fwd.append(_gmlp_pair_fwd(gv[:, sl], w0, w1, b_ref[p], lo))
        yg = jnp.concatenate([gu[:, 128 * p:128 * p + 128] * fwd[p][3] for p in range(npair)], axis=1)
        dyg, dg = _rms_bwd(yg, g_ref[...], dy_ref[0].astype(F32), D_GMLP)
        dg_ref[...] += dg
        du_parts, dv_parts = [], []
        for p in range(npair):
            sl = slice(128 * p, 128 * p + 128)
            vn, vnb, rstd, mixed = fwd[p]
            dyg_p = dyg[:, sl]
            dmixed = dyg_p * gu[:, sl]
            dbacc[p] += dmixed
            dm_lo = jnp.where(lo, dmixed, 0.0).astype(BF16)
            dm_hi = jnp.where(lo, 0.0, dmixed).astype(BF16)
            dws_ref[2 * p] += jnp.where(tril, _dot(dm_lo, vnb, NT), 0.0)
            dws_ref[2 * p + 1] += jnp.where(tril, _dot(dm_hi, vnb, NT), 0.0)
            dmb = dmixed.astype(BF16)
            dvn = jnp.where(lo, _dot(wst_ref[2 * p], dmb), _dot(wst_ref[2 * p + 1], dmb))
            dgv = rstd * (dvn - _pair_mean_exact(dvn, lo) - vn * _pair_mean_exact(dvn * vn, lo))
            dv_parts.append(dgv * dgv_dv[:, sl])
            du_parts.append(dyg_p * mixed * dgu[:, sl])
        duv_ref[0] = jnp.concatenate(du_parts + dv_parts, axis=1).astype(BF16)

        @pl.when(last)
        def _():
            sel = jnp.where(_iota((8, 128), 0) == 0, (_iota((8, 128), 1) < GROUP_DIM).astype(F32),
                            jnp.where(_iota((8, 128), 0) == 1, (_iota((8, 128), 1) >= GROUP_DIM).astype(F32), 0.0))
            for p in range(npair):
                dbs_ref[p] = lax.dot_general(sel, dbacc[p], NT, precision=lax.Precision.HIGHEST,
                                             preferred_element_type=F32)

    duv, dws, dbs, dg = pl.pallas_call(
        body, name=name, grid=(bsz, nc),
        in_specs=[pl.BlockSpec((1, CHUNK, D_GMLP), lambda b, i: (b, i, 0)),
                  pl.BlockSpec((1, CHUNK, D_GMLP), lambda b, i: (b, i, 1)),
                  pl.BlockSpec((1, CHUNK, D_GMLP), lambda b, i: (b, i, dy_col)),
                  pl.BlockSpec((GROUPS, CHUNK, CHUNK), lambda b, i: (0, 0, 0)),
                  pl.BlockSpec((GROUPS, CHUNK, CHUNK), lambda b, i: (0, 0, 0)),
                  pl.BlockSpec((npair, CHUNK, 128), lambda b, i: (0, 0, 0)),
                  pl.BlockSpec((1, D_GMLP), lambda b, i: (0, 0))],
        out_specs=[pl.BlockSpec((1, CHUNK, 2 * D_GMLP), lambda b, i: (b, i, 0)),
                   pl.BlockSpec((GROUPS, CHUNK, CHUNK), lambda b, i: (0, 0, 0)),
                   pl.BlockSpec((npair, 8, CHUNK), lambda b, i: (0, 0, 0)),
                   pl.BlockSpec((1, D_GMLP), lambda b, i: (0, 0))],
        out_shape=[jax.ShapeDtypeStruct((bsz, seq, D_IN_PAD), BF16),
                   jax.ShapeDtypeStruct((GROUPS, CHUNK, CHUNK), F32),
                   jax.ShapeDtypeStruct((npair, 8, CHUNK), F32),
                   jax.ShapeDtypeStruct((1, D_GMLP), F32)],
        scratch_shapes=[pltpu.VMEM((npair, CHUNK, 128), F32)],
        compiler_params=_cparams(),
    )(z3, z3, dyn3, ws, wst, bexp, g_out)
    return duv, dws, dbs[:, :2, :].reshape(GROUPS, CHUNK), dg


def _partner(x):
    width = x.shape[-1]
    lane = _iota(x.shape, x.ndim - 1) % HEAD_PAD
    up = pltpu.roll(x, width - ROPE // 2, x.ndim - 1)
    down = pltpu.roll(x, ROPE // 2, x.ndim - 1)
    first = jnp.logical_and(lane >= NOPE, lane < NOPE + ROPE // 2)
    second = jnp.logical_and(lane >= NOPE + ROPE // 2, lane < NOPE + ROPE)
    return jnp.where(first, up, jnp.where(second, down, 0.0))


def _mla_prep_fwd(z3, g_q, g_kv, w_uq, w_ukv, ctab, stab, *, name, tb=256):
    bsz, seq, _ = z3.shape
    tb = min(tb, seq)
    hw = HEADS * HEAD_PAD

    def body(ql_ref, kvl_ref, krl_ref, gq_ref, gkv_ref, wuq_ref, wukv_ref, c_ref, s_ref, q_ref, kv_ref, kp_ref):
        cq = _rms_fwd(ql_ref[0].astype(F32), gq_ref[...], Q_RANK).astype(BF16)
        q = _dot(cq, wuq_ref[...])
        c1, s1 = c_ref[0], s_ref[0]
        c8, s8 = jnp.tile(c1, (1, HEADS)), jnp.tile(s1, (1, HEADS))
        q_ref[0] = ((q * c8 + _partner(q) * s8) * SCALE_LOG2).astype(BF16)
        ckv = _rms_fwd(kvl_ref[0].astype(F32), gkv_ref[...], KV_RANK).astype(BF16)
        kv = _dot(ckv, wukv_ref[...])
        kv_ref[0] = kv.astype(BF16)
        kr = krl_ref[0].astype(F32)
        kr = kr * c1 + _partner(kr) * s1
        lane = _iota((tb, hw), 1) % HEAD_PAD
        kp_ref[0] = jnp.where(lane < NOPE, kv, jnp.tile(kr, (1, HEADS))).astype(BF16)

    return pl.pallas_call(
        body, name=name, grid=(bsz, seq // tb),
        in_specs=[pl.BlockSpec((1, tb, Q_RANK), lambda b, i: (b, i, 4)),
                  pl.BlockSpec((1, tb, KV_RANK), lambda b, i: (b, i, 10)),
                  pl.BlockSpec((1, tb, HEAD_PAD), lambda b, i: (b, i, 11)),
                  pl.BlockSpec((1, Q_RANK), lambda b, i: (0, 0)),
                  pl.BlockSpec((1, KV_RANK), lambda b, i: (0, 0)),
                  pl.BlockSpec((Q_RANK, hw), lambda b, i: (0, 0)),
                  pl.BlockSpec((KV_RANK, hw), lambda b, i: (0, 0)),
                  pl.BlockSpec((1, tb, HEAD_PAD), lambda b, i: (b, i, 0)),
                  pl.BlockSpec((1, tb, HEAD_PAD), lambda b, i: (b, i, 0))],
        out_specs=[pl.BlockSpec((1, tb, hw), lambda b, i: (b, i, 0))] * 3,
        out_shape=[jax.ShapeDtypeStruct((bsz, seq, hw), BF16)] * 3,
        compiler_params=_cparams(),
    )(z3, z3, z3, g_q, g_kv, w_uq, w_ukv, ctab, stab)


def _mla_prep_bwd(z3, dz3, dq3, dk3, dv3, g_q, g_kv, w_uq, w_ukv, ctab, stab, *, name, tb=256):
    bsz, seq, _ = z3.shape
    tb = min(tb, seq)
    hw = HEADS * HEAD_PAD
    nb = seq // tb

    def body(ql_ref, kvl_ref, dq_ref, dk_ref, dv_ref, gq_ref, gkv_ref, wuq_ref, wukv_ref, c_ref, s_ref, dz_in,
             dz_ref, cq_ref, dqb_ref, ckv_ref, dkvb_ref, dgq_ref, dgkv_ref):
        @pl.when(jnp.logical_and(pl.program_id(0) == 0, pl.program_id(1) == 0))
        def _():
            dgq_ref[...] = jnp.zeros_like(dgq_ref)
            dgkv_ref[...] = jnp.zeros_like(dgkv_ref)

        c1, s1 = c_ref[0], s_ref[0]
        c8, s8 = jnp.tile(c1, (1, HEADS)), jnp.tile(s1, (1, HEADS))
        dqr = dq_ref[0]
        dqb = (dqr * c8 + _partner(dqr * s8)).astype(BF16)
        dqb_ref[0] = dqb
        ql = ql_ref[0].astype(F32)
        cq_ref[0] = _rms_fwd(ql, gq_ref[...], Q_RANK).astype(BF16)
        dql, dgq = _rms_bwd(ql, gq_ref[...], _dot(dqb, wuq_ref[...], NT), Q_RANK)
        dgq_ref[...] += dgq

        dk = dk_ref[0]
        lane = _iota((tb, hw), 1) % HEAD_PAD
        dkvb = jnp.where(lane < NOPE, dk, dv_ref[0]).astype(BF16)
        dkvb_ref[0] = dkvb
        kvl = kvl_ref[0].astype(F32)
        ckv_ref[0] = _rms_fwd(kvl, gkv_ref[...], KV_RANK).astype(BF16)
        dkvl, dgkv = _rms_bwd(kvl, gkv_ref[...], _dot(dkvb, wukv_ref[...], NT), KV_RANK)
        dgkv_ref[...] += dgkv

        dkr = dk[:, 0:HEAD_PAD].astype(F32)
        for h in range(1, HEADS):
            dkr = dkr + dk[:, HEAD_PAD * h:HEAD_PAD * (h + 1)].astype(F32)
        lane1 = _iota((tb, HEAD_PAD), 1)
        dkr = jnp.where(jnp.logical_and(lane1 >= NOPE, lane1 < NOPE + ROPE), dkr, 0.0)
        dkrl = dkr * c1 + _partner(dkr * s1)
        dz_ref[0] = jnp.concatenate([dql, dkvl, dkrl], axis=1).astype(BF16)

    return pl.pallas_call(
        body, name=name, grid=(bsz, nb),
        in_specs=[pl.BlockSpec((1, tb, Q_RANK), lambda b, i: (b, i, 4)),
                  pl.BlockSpec((1, tb, KV_RANK), lambda b, i: (b, i, 10)),
                  pl.BlockSpec((1, tb, hw), lambda b, i: (b, i, 0)),
                  pl.BlockSpec((1, tb, hw), lambda b, i: (b, i, 0)),
                  pl.BlockSpec((1, tb, hw), lambda b, i: (b, i, 0)),
                  pl.BlockSpec((1, Q_RANK), lambda b, i: (0, 0)),
                  pl.BlockSpec((1, KV_RANK), lambda b, i: (0, 0)),
                  pl.BlockSpec((Q_RANK, hw), lambda b, i: (0, 0)),
                  pl.BlockSpec((KV_RANK, hw), lambda b, i: (0, 0)),
                  pl.BlockSpec((1, tb, HEAD_PAD), lambda b, i: (b, i, 0)),
                  pl.BlockSpec((1, tb, HEAD_PAD), lambda b, i: (b, i, 0)),
                  ANY_SPEC],
        out_specs=[pl.BlockSpec((1, tb, 512), lambda b, i: (b, i, 2)),
                   pl.BlockSpec((1, tb, Q_RANK), lambda b, i: (b, i, 0)),
                   pl.BlockSpec((1, tb, hw), lambda b, i: (b, i, 0)),
                   pl.BlockSpec((1, tb, KV_RANK), lambda b, i: (b, i, 0)),
                   pl.BlockSpec((1, tb, hw), lambda b, i: (b, i, 0)),
                   pl.BlockSpec((1, Q_RANK), lambda b, i: (0, 0)),
                   pl.BlockSpec((1, KV_RANK), lambda b, i: (0, 0))],
        out_shape=[jax.ShapeDtypeStruct((bsz, seq, D_IN_PAD), BF16),
                   jax.ShapeDtypeStruct((bsz, seq, Q_RANK), BF16),
                   jax.ShapeDtypeStruct((bsz, seq, hw), BF16),
                   jax.ShapeDtypeStruct((bsz, seq, KV_RANK), BF16),
                   jax.ShapeDtypeStruct((bsz, seq, hw), BF16),
                   jax.ShapeDtypeStruct((1, Q_RANK), F32),
                   jax.ShapeDtypeStruct((1, KV_RANK), F32)],
        input_output_aliases={11: 0},
        compiler_params=_cparams(),
    )(z3, z3, dq3, dk3, dv3, g_q, g_kv, w_uq, w_ukv, ctab, stab, dz3)


ATTN_HEADS_PER_STEP = 4


def _attn_specs(tq, seq, hp):
    blk = pl.BlockSpec((1, tq, hp * HEAD_PAD), lambda b, h, i: (b, i, h))
    full = pl.BlockSpec((1, seq, hp * HEAD_PAD), lambda b, h, i: (b, 0, h))
    return blk, full


def _head(h):
    return slice(HEAD_PAD * h, HEAD_PAD * (h + 1))


def _attn_fwd(q3, kv3, kp3, *, name, tq=512, hp=ATTN_HEADS_PER_STEP, side=None):
    bsz, seq, hw = q3.shape
    tq = min(tq, seq)
    blk, full = _attn_specs(tq, seq, hp)

    def body(q_ref, kv_ref, kp_ref, o_ref, lse_ref):
        i = pl.program_id(2)

        def update(state, q, kp, kv, mask=None):
            m, l, acc = state
            s = _dot(q, kp, NT)
            if mask is not None:
                s = jnp.where(mask, s, -1e30)
            m_new = jnp.maximum(m, jnp.max(s, axis=1, keepdims=True))
            alpha = jnp.exp2(m - m_new)
            p = jnp.exp2(s - m_new)
            return m_new, alpha * l + jnp.sum(p, axis=1, keepdims=True), alpha * acc + _dot(p.astype(BF16), kv)

        def step(j, carry):
            st = pl.multiple_of(j * tq, tq)
            return tuple(update(carry[h], q_ref[0, :, _head(h)], kp_ref[0, pl.ds(st, tq), _head(h)],
                                kv_ref[0, pl.ds(st, tq), _head(h)]) for h in range(hp))

        init = tuple((jnp.full((tq, 1), -1e30, F32), jnp.zeros((tq, 1), F32), jnp.zeros((tq, HEAD_PAD), F32))
                     for _ in range(hp))
        carry = lax.fori_loop(0, i, step, init)

        st = pl.multiple_of(i * tq, tq)
        is_nope = _iota((tq, HEAD_PAD), 1) < NOPE
        causal = _iota((tq, tq), 1) <= _iota((tq, tq), 0)
        for h in range(hp):
            m, l, acc = update(carry[h], q_ref[0, :, _head(h)], kp_ref[0, pl.ds(st, tq), _head(h)],
                               kv_ref[0, pl.ds(st, tq), _head(h)], causal)
            o_ref[0, :, _head(h)] = jnp.where(is_nope, 0.0, acc / l).astype(BF16)
            lse_ref[0, :, _head(h)] = jnp.broadcast_to(m + jnp.log(l) * LOG2E, (tq, HEAD_PAD))

    outs, side_outs = _hosted_call(
        body, name=name, grid=(bsz, HEADS // hp, seq // tq),
        in_specs=[blk, full, full],
        out_specs=[blk, blk],
        out_shape=[jax.ShapeDtypeStruct((bsz, seq, hw), BF16), jax.ShapeDtypeStruct((bsz, seq, hw), F32)],
        args=(q3, kv3, kp3), side=side)
    return outs if side is None else (outs, side_outs)


def _attn_bwd(q3, kv3, kp3, do3, lse3, dl3, *, name, tq=512, hp=ATTN_HEADS_PER_STEP, side=None):
    bsz, seq, hw = q3.shape
    tq = min(tq, seq)
    nq = seq // tq
    blk, full = _attn_specs(tq, seq, hp)

    def body(kv_ref, kp_ref, q_ref, do_ref, lse_ref, dl_ref, dq_ref, dk_ref, dv_ref):
        j = pl.program_id(2)

        @pl.when(j == 0)
        def _():
            dq_ref[...] = jnp.zeros_like(dq_ref)

        def pair(h, row0, nrows, nkeys, mask=None):
            row0 = pl.multiple_of(row0, nrows)
            qi = q_ref[0, pl.ds(row0, nrows), _head(h)]
            do = do_ref[0, pl.ds(row0, nrows), _head(h)]
            kp = kp_ref[0, :nkeys, _head(h)]
            s = _dot(qi, kp, NT)
            if mask is not None:
                s = jnp.where(mask, s, -1e30)
            wide = nkeys // HEAD_PAD
            p = jnp.exp2(s - jnp.tile(lse_ref[0, pl.ds(row0, nrows), _head(h)], (1, wide)))
            dv = _dot(p.astype(BF16), do, TN)
            dp = _dot(do, kv_ref[0, :nkeys, _head(h)], NT)
            ds = (p * (dp - jnp.tile(dl_ref[0, pl.ds(row0, nrows), _head(h)], (1, wide)))).astype(BF16)
            dq_ref[0, pl.ds(row0, nrows), _head(h)] += _dot(ds, kp)
            return _dot(ds, qi, TN), dv

        def step(i, carry):
            st = pl.multiple_of(i * tq, tq)
            out = []
            for h in range(hp):
                dk, dv = pair(h, st, tq, tq)
                out.append((carry[h][0] + dk, carry[h][1] + dv))
            return tuple(out)

        causal = _iota((tq, tq), 1) <= _iota((tq, tq), 0)
        carry = tuple(pair(h, pl.multiple_of(j * tq, tq), tq, tq, causal) for h in range(hp))
        carry = lax.fori_loop(j + 1, nq, step, carry)
        for h in range(hp):
            dk_ref[0, :, _head(h)] = (carry[h][0] * (1.0 / LOG2E)).astype(BF16)
            dv_ref[0, :, _head(h)] = carry[h][1].astype(BF16)

        @pl.when(j == nq - 1)
        def _():
            dq_ref[...] = dq_ref[...] * ATTN_SCALE

    outs, side_outs = _hosted_call(
        body, name=name, grid=(bsz, HEADS // hp, nq),
        in_specs=[blk, blk, full, full, full, full],
        out_specs=[full, blk, blk],
        out_shape=[jax.ShapeDtypeStruct((bsz, seq, hw), F32)] + [jax.ShapeDtypeStruct((bsz, seq, hw), BF16)] * 2,
        args=(kv3, kp3, q3, do3, lse3, dl3), side=side)
    return outs if side is None else (outs, side_outs)


def _onorm_fwd(o3, yg3, g_pad, *, name, tb=512):
    bsz, seq, hw = o3.shape
    wg = yg3.shape[-1]
    tb = min(tb, seq)

    def body(o_ref, yg_ref, g_ref, y_ref):
        ya = _rms_fwd(o_ref[0].astype(F32), g_ref[...], HEADS * 64).astype(BF16)
        y_ref[0] = jnp.concatenate([ya, yg_ref[0]], axis=1)

    return pl.pallas_call(
        body, name=name, grid=(bsz, seq // tb),
        in_specs=[pl.BlockSpec((1, tb, hw), lambda b, i: (b, i, 0)),
                  pl.BlockSpec((1, tb, wg), lambda b, i: (b, i, 0)),
                  pl.BlockSpec((1, hw), lambda b, i: (0, 0))],
        out_specs=pl.BlockSpec((1, tb, hw + wg), lambda b, i: (b, i, 0)),
        out_shape=jax.ShapeDtypeStruct((bsz, seq, hw + wg), BF16),
        compiler_params=_cparams(),
    )(o3, yg3, g_pad)


def _onorm_bwd(o3, dy3, g_pad, *, name, tb=512):
    bsz, seq, hw = o3.shape
    tb = min(tb, seq)

    def body(o_ref, dy_ref, g_ref, do_ref, dl_ref, dg_ref):
        @pl.when(jnp.logical_and(pl.program_id(0) == 0, pl.program_id(1) == 0))
        def _():
            dg_ref[...] = jnp.zeros_like(dg_ref)

        o = o_ref[0].astype(F32)
        do, dg = _rms_bwd(o, g_ref[...], dy_ref[0].astype(F32), HEADS * 64)
        dg_ref[...] += dg
        do_ref[0] = do.astype(BF16)
        prod = do * o
        parts = []
        for h in range(HEADS):
            sh = jnp.sum(prod[:, HEAD_PAD * h:HEAD_PAD * (h + 1)], axis=1, keepdims=True)
            parts.append(jnp.broadcast_to(sh, (tb, HEAD_PAD)))
        dl_ref[0] = jnp.concatenate(parts, axis=1)

    return pl.pallas_call(
        body, name=name, grid=(bsz, seq // tb),
        in_specs=[pl.BlockSpec((1, tb, hw), lambda b, i: (b, i, 0)),
                  pl.BlockSpec((1, tb, hw), lambda b, i: (b, i, 0)),
                  pl.BlockSpec((1, hw), lambda b, i: (0, 0))],
        out_specs=[pl.BlockSpec((1, tb, hw), lambda b, i: (b, i, 0)),
                   pl.BlockSpec((1, tb, hw), lambda b, i: (b, i, 0)),
                   pl.BlockSpec((1, hw), lambda b, i: (0, 0))],
        out_shape=[jax.ShapeDtypeStruct((bsz, seq, hw), BF16),
                   jax.ShapeDtypeStruct((bsz, seq, hw), F32),
                   jax.ShapeDtypeStruct((1, hw), F32)],
        compiler_params=_cparams(),
    )(o3, dy3, g_pad)


def _resnode_bwd(x3, g, *, name, target3=None, dh3=None, dres3=None, mod_nm=None, rows=None,
                 branch3=None, mod_gate=None, gate_row=None, tb=512, side=None):
    bsz, seq, d = x3.shape
    tb = min(tb, seq)
    final = target3 is not None
    has_branch = branch3 is not None
    row_spec = pl.BlockSpec((1, tb, d), lambda b, i: (b, i, 0))
    vec_spec = pl.BlockSpec((1, d), lambda b, i: (0, 0))
    mod_spec = pl.BlockSpec((1, MOD_ROWS, d), lambda b, i: (b, 0, 0))

    ins, in_specs = [x3, g], [row_spec, vec_spec]
    if final:
        ins += [target3]
        in_specs += [row_spec]
    else:
        ins += [dh3, dres3, mod_nm]
        in_specs += [row_spec, row_spec, mod_spec]
    if has_branch:
        ins += [branch3, mod_gate]
        in_specs += [row_spec, mod_spec]

    out_names = ["dx", "dg"]
    out_specs = [row_spec, vec_spec]
    out_shape = [jax.ShapeDtypeStruct((bsz, seq, d), F32), jax.ShapeDtypeStruct((1, d), F32)]
    if final:
        out_names += ["loss"]
        out_specs += [pl.BlockSpec((1, 128), lambda b, i: (0, 0))]
        out_shape += [jax.ShapeDtypeStruct((1, 128), F32)]
    else:
        out_names += ["dnm"]
        out_specs += [mod_spec]
        out_shape += [jax.ShapeDtypeStruct((bsz, MOD_ROWS, d), F32)]
    if has_branch:
        out_names += ["dbr", "dgate"]
        out_specs += [row_spec, mod_spec]
        out_shape += [jax.ShapeDtypeStruct((bsz, seq, d), BF16), jax.ShapeDtypeStruct((bsz, MOD_ROWS, d), F32)]
    n_in = len(ins)

    def body(*refs):
        r = dict(zip(["x", "g"] + (["t"] if final else ["dh", "dres", "nm"]) + (["br", "gm"] if has_branch else []),
                     refs[:n_in]))
        o = dict(zip(out_names, refs[n_in:]))
        b_first = pl.program_id(1) == 0
        first = jnp.logical_and(pl.program_id(0) == 0, b_first)
        rowid = _iota((MOD_ROWS, d), 0)

        @pl.when(first)
        def _():
            o["dg"][...] = jnp.zeros_like(o["dg"])
            if final:
                o["loss"][...] = jnp.zeros_like(o["loss"])

        @pl.when(b_first)
        def _():
            if not final:
                o["dnm"][...] = jnp.zeros_like(o["dnm"])
            if has_branch:
                o["dgate"][...] = jnp.zeros_like(o["dgate"])

        x = r["x"][0]
        gv = r["g"][...]
        if final:
            e = _rms_fwd(x, gv, d) - r["t"][0]
            sq = jnp.sum(jnp.sum(e * e, axis=1, keepdims=True), axis=0, keepdims=True)
            o["loss"][...] += jnp.broadcast_to(sq * (0.5 / d), (1, 128))
            dx, dg = _rms_bwd(x, gv, e * (1.0 / d), d)
        else:
            m = r["nm"][0]
            dh = r["dh"][0].astype(F32)
            scale = m[rows[1]:rows[1] + 1, :]
            rstd = lax.rsqrt(jnp.sum(x * x, axis=-1, keepdims=True) * (1.0 / d) + EPS)
            xh = x * rstd
            nrm = xh * gv
            dshift = jnp.sum(dh, axis=0, keepdims=True)
            dscale = jnp.sum(dh * nrm, axis=0, keepdims=True)
            o["dnm"][0] += jnp.where(rowid == 0, dshift, jnp.where(rowid == 1, dscale, 0.0))
            dn = dh * (1.0 + scale)
            dg = jnp.sum(dn * xh, axis=0, keepdims=True)
            dxh = dn * gv
            dx = rstd * (dxh - xh * (jnp.sum(dxh * xh, axis=-1, keepdims=True) * (1.0 / d))) + r["dres"][0]
        o["dg"][...] += dg
        o["dx"][0] = dx
        if has_branch:
            gate = r["gm"][0][gate_row:gate_row + 1, :]
            o["dbr"][0] = (gate * dx).astype(BF16)
            dgate = jnp.sum(dx * r["br"][0], axis=0, keepdims=True)
            o["dgate"][0] += jnp.where(rowid == 0, dgate, 0.0)

    outs, side_outs = _hosted_call(
        body, name=name, grid=(bsz, seq // tb),
        in_specs=in_specs, out_specs=out_specs, out_shape=out_shape, args=tuple(ins), side=side)
    res = dict(zip(out_names, outs))
    return res if side is None else (res, side_outs)


def _adamw(w, g, m, v, *, name):
    shape = w.shape
    cols = shape[-1]
    rows = w.size // cols
    tr = _pick_rows(rows, max(8, (256 * 1024) // cols // 8 * 8))

    def body(w_ref, g_ref, m_ref, v_ref, d_ref, nm_ref, nv_ref):
        d_ref[...], nm_ref[...], nv_ref[...] = _adamw_math(w_ref[...], g_ref[...], m_ref[...], v_ref[...])

    if w.ndim == 3 and shape[1] % 8 == 0:
        tr3 = _pick_rows(shape[1], max(8, (256 * 1024) // cols // 8 * 8))
        spec3 = pl.BlockSpec((None, tr3, cols), lambda l, i: (l, i, 0))
        return tuple(pl.pallas_call(
            body, name=name, grid=(shape[0], shape[1] // tr3),
            in_specs=[spec3] * 4, out_specs=[spec3] * 3,
            out_shape=[jax.ShapeDtypeStruct(shape, F32)] * 3,
            compiler_params=_cparams(),
        )(w, g, m, v))
    spec = pl.BlockSpec((tr, cols), lambda i: (i, 0))
    outs = pl.pallas_call(
        body, name=name, grid=(rows // tr,),
        in_specs=[spec] * 4, out_specs=[spec] * 3,
        out_shape=[jax.ShapeDtypeStruct((rows, cols), F32)] * 3,
        compiler_params=_cparams(),
    )(*[t.reshape(rows, cols) for t in (w, g, m, v)])
    return tuple(o.reshape(shape) for o in outs)


def _adamw_math(w, g, m, v):
    c1 = 1.0 - ADAM_B1 ** ADAM_STEP
    c2 = 1.0 - ADAM_B2 ** ADAM_STEP
    nm = ADAM_B1 * m + (1.0 - ADAM_B1) * g
    nv = ADAM_B2 * v + (1.0 - ADAM_B2) * (g * g)
    delta = -ADAM_LR * ((nm / c1) / (jnp.sqrt(nv / c2) + ADAM_EPS) + ADAM_WD * w)
    return delta, nm, nv


def _adamw_layers(w, m, v, bufs, row_off, *, name, tr=256):
    depth, rows, cols = w.shape
    tr = min(tr, rows)
    assert rows % tr == 0 and row_off % tr == 0

    outs = None
    for l in range(depth):
        def body(w_ref, g_ref, m_ref, v_ref, *rest):
            go_ref, d_ref, nm_ref, nv_ref = rest[-4:]
            g = g_ref[...]
            go_ref[...] = g
            d_ref[...], nm_ref[...], nv_ref[...] = _adamw_math(w_ref[...], g, m_ref[...], v_ref[...])

        layer = pl.BlockSpec((None, tr, cols), lambda i, l=l: (l, i, 0))
        prev = () if outs is None else tuple(outs)
        outs = pl.pallas_call(
            body, name=f"{name}_l{l}", grid=(rows // tr,),
            in_specs=[layer, pl.BlockSpec((tr, cols), lambda i: (row_off // tr + i, 0)), layer, layer]
            + [ANY_SPEC] * len(prev),
            out_specs=[layer] * 4,
            out_shape=[jax.ShapeDtypeStruct(w.shape, F32)] * 4,
            input_output_aliases={4 + k: k for k in range(len(prev))},
            compiler_params=_cparams(),
        )(w, bufs[l], m, v, *prev)
    return tuple(outs)


def _sum_leading(x, *, name, tr=256):
    n, rows, cols = x.shape
    tr = _pick_rows(rows, tr)

    def body(x_ref, o_ref):
        acc = x_ref[0]
        for k in range(1, n):
            acc = acc + x_ref[k]
        o_ref[...] = acc

    return pl.pallas_call(
        body, name=name, grid=(rows // tr,),
        in_specs=[pl.BlockSpec((n, tr, cols), lambda i: (0, i, 0))],
        out_specs=pl.BlockSpec((tr, cols), lambda i: (i, 0)),
        out_shape=jax.ShapeDtypeStruct((rows, cols), F32),
        compiler_params=_cparams(),
    )(x)


def _position():
    return lax.axis_index("x"), lax.axis_index("y"), lax.axis_index("c")


def _allgather8(x, *, name):
    shape = x.shape

    def body(x_ref, out_ref, send_sems, recv_sems, local_sem):
        px, py, pc = _position()
        me, sibling = (px, py, pc), (px, py, 1 - pc)
        chips = [(1 - px, py), (px, 1 - py), (1 - px, 1 - py)]
        src_own = x_ref

        def slot(qx, qy, qc):
            return out_ref.at[4 * qx + 2 * qy + qc]

        def copy(k, block, to, src=None):
            return pltpu.make_async_remote_copy(
                src_ref=slot(*block) if src is None else src, dst_ref=slot(*block),
                send_sem=send_sems.at[k], recv_sem=recv_sems.at[k], device_id=to, device_id_type=MESH)

        mine = pltpu.make_async_copy(src_own, slot(*me), local_sem)
        mine.start()
        first = [copy(0, me, sibling, src=src_own)]
        first += [copy(1 + j, me, (*chip, pc), src=src_own) for j, chip in enumerate(chips)]
        for cp in first:
            cp.start()
        passed = [copy(4 + j, (*chip, pc), sibling) for j, chip in enumerate(chips)]
        for j, chip in enumerate(chips):
            copy(1 + j, (*chip, pc), me).wait_recv()
            passed[j].start()
        copy(0, sibling, me).wait_recv()
        for j, chip in enumerate(chips):
            copy(4 + j, (*chip, 1 - pc), me).wait_recv()
        for cp in first + passed:
            cp.wait_send()
        mine.wait()

    return pl.pallas_call(
        body, name=name,
        out_shape=jax.ShapeDtypeStruct((N_DEV,) + shape, x.dtype),
        in_specs=[pl.BlockSpec(memory_space=pl.ANY)],
        out_specs=pl.BlockSpec(memory_space=pl.ANY),
        scratch_shapes=[pltpu.SemaphoreType.DMA((7,)), pltpu.SemaphoreType.DMA((7,)), pltpu.SemaphoreType.DMA],
    )(x)


class _Exchange:
    def __init__(self, ins, out_shapes, n, build, aliases=None):
        self.ins, self.out_shapes, self.n, self.build = tuple(ins), tuple(out_shapes), n, build
        self.aliases = dict(aliases or {})

    def _descriptors(self, in_refs, out_refs, send_sems, recv_sems):
        sends, recvs = [], []
        for k, (src, dst, peer, landing) in enumerate(self.build(in_refs, out_refs)):
            sends.append(pltpu.make_async_remote_copy(
                src_ref=src, dst_ref=dst, send_sem=send_sems.at[k], recv_sem=recv_sems.at[k],
                device_id=peer, device_id_type=MESH))
            recvs.append(pltpu.make_async_remote_copy(
                src_ref=src, dst_ref=landing, send_sem=send_sems.at[k], recv_sem=recv_sems.at[k],
                device_id=peer, device_id_type=MESH))
        return sends, recvs

    def start(self, *refs):
        for cp in self._descriptors(*refs)[0]:
            cp.start()

    def finish(self, *refs):
        sends, recvs = self._descriptors(*refs)
        for cp in recvs:
            cp.wait_recv()
        for cp in sends:
            cp.wait_send()


ANY_SPEC = pl.BlockSpec(memory_space=pl.ANY)


def _hosted_call(body, *, name, grid, in_specs, out_specs, out_shape, args, scratch_shapes=(), side=None,
                 num_scalar_prefetch=0, io_aliases=None):
    in_specs, out_specs, out_shape = list(in_specs), list(out_specs), list(out_shape)
    n_in, n_out = len(in_specs) + num_scalar_prefetch, len(out_specs)
    kernel_body = body
    aliases = dict(io_aliases or {})
    if side is not None:
        s_in, s_out = len(side.ins), len(side.out_shapes)
        aliases.update({n_in + i: n_out + o for i, o in side.aliases.items()})

        def kernel_body(*refs):
            ins, s_ins = refs[:n_in], refs[n_in:n_in + s_in]
            outs = refs[n_in + s_in:n_in + s_in + n_out]
            s_outs = refs[n_in + s_in + n_out:n_in + s_in + n_out + s_out]
            scratch, sems = refs[n_in + s_in + n_out + s_out:-2], refs[-2:]
            first = functools.reduce(jnp.logical_and, [pl.program_id(a) == 0 for a in range(len(grid))])
            last = functools.reduce(jnp.logical_and, [pl.program_id(a) == g - 1 for a, g in enumerate(grid)])

            @pl.when(first)
            def _():
                side.start(s_ins, s_outs, *sems)

            body(*ins, *outs, *scratch)

            @pl.when(last)
            def _():
                side.finish(s_ins, s_outs, *sems)

        in_specs += [ANY_SPEC] * s_in
        out_specs += [ANY_SPEC] * s_out
        out_shape += list(side.out_shapes)
        scratch_shapes = list(scratch_shapes) + [pltpu.SemaphoreType.DMA((side.n,)),
                                                 pltpu.SemaphoreType.DMA((side.n,))]
        args = tuple(args) + side.ins
    if num_scalar_prefetch:
        grid_spec = pltpu.PrefetchScalarGridSpec(num_scalar_prefetch=num_scalar_prefetch, grid=grid,
                                                 in_specs=in_specs, out_specs=out_specs,
                                                 scratch_shapes=list(scratch_shapes))
        outs = pl.pallas_call(kernel_body, name=name, grid_spec=grid_spec, out_shape=out_shape,
                              input_output_aliases=aliases, compiler_params=_cparams())(*args)
    else:
        outs = pl.pallas_call(kernel_body, name=name, grid=grid, in_specs=in_specs, out_specs=out_specs,
                              out_shape=out_shape, scratch_shapes=list(scratch_shapes),
                              input_output_aliases=aliases, compiler_params=_cparams())(*args)
    return tuple(outs[:n_out]), tuple(outs[n_out:])


def _run_exchange(ex, *, name):
    s_in = len(ex.ins)

    def body(*refs):
        ins, outs, sems = refs[:s_in], refs[s_in:-2], refs[-2:]
        ex.start(ins, outs, *sems)
        ex.finish(ins, outs, *sems)

    outs = pl.pallas_call(
        body, name=name, out_shape=list(ex.out_shapes),
        in_specs=[ANY_SPEC] * s_in, out_specs=[ANY_SPEC] * len(ex.out_shapes),
        scratch_shapes=[pltpu.SemaphoreType.DMA((ex.n,)), pltpu.SemaphoreType.DMA((ex.n,))],
        input_output_aliases=ex.aliases,
    )(*ex.ins)
    return tuple(outs)


def _both(a, b):
    na, oa = len(a.ins), len(a.out_shapes)

    def build(ins, outs):
        return a.build(ins[:na], outs[:oa]) + b.build(ins[na:], outs[oa:])

    aliases = dict(a.aliases)
    aliases.update({na + i: oa + o for i, o in b.aliases.items()})
    return _Exchange(a.ins + b.ins, a.out_shapes + b.out_shapes, a.n + b.n, build, aliases)


def _other_chips(px, py):
    return [(px, 1 - py), (1 - px, py), (1 - px, 1 - py)]


def _gather_spread(w_flat, halves=True):
    rows, w = w_flat.shape
    hr = rows // 2 if halves else rows

    def build(ins, outs):
        px, py, pc = _position()
        mine = ins[0].at[pl.ds(pc * hr, hr)] if halves else ins[0]
        me = 4 * px + 2 * py + pc
        plan = [((px, py, 1 - pc), me ^ 1)]
        plan += [((qx, qy, pc), 4 * qx + 2 * qy + pc) for qx, qy in _other_chips(px, py)]
        return [(mine, outs[0].at[me], peer, outs[0].at[their]) for peer, their in plan]

    return _Exchange([w_flat], [jax.ShapeDtypeStruct((N_DEV, hr, w), w_flat.dtype)], 4, build)


def _gather_pass_on(gath):
    def build(ins, outs):
        px, py, pc = _position()
        out = []
        for qx, qy in _other_chips(px, py):
            blk = 4 * qx + 2 * qy + pc
            out.append((outs[0].at[blk], outs[0].at[blk], (px, py, 1 - pc), outs[0].at[blk ^ 1]))
        return out

    return _Exchange([gath], [jax.ShapeDtypeStruct(gath.shape, gath.dtype)], 3, build, aliases={0: 0})


def _rs_halves(g):
    n, rows, w = g.shape
    hr = rows // 2

    def build(ins, outs):
        px, py, pc = _position()
        return [(ins[0].at[:, pl.ds((1 - pc) * hr, hr), :], outs[0], (px, py, 1 - pc), outs[0])]

    return _Exchange([g], [jax.ShapeDtypeStruct((n, hr, w), g.dtype)], 1, build)


def _rs_chips(sb):
    def build(ins, outs):
        px, py, pc = _position()
        return [(ins[0].at[j], outs[0].at[j], (qx, qy, pc), outs[0].at[j])
                for j, (qx, qy) in enumerate(_other_chips(px, py))]

    return _Exchange([sb], [jax.ShapeDtypeStruct(sb.shape, sb.dtype)], 3, build)


def _rs_complete(buf):
    def build(ins, outs):
        px, py, pc = _position()
        return [(outs[0].at[pc], outs[0].at[pc], (px, py, 1 - pc), outs[0].at[1 - pc])]

    return _Exchange([buf], [jax.ShapeDtypeStruct(buf.shape, buf.dtype)], 1, build, aliases={0: 0})


def _rs_partial(g, recv, ids, *, name, tr=128):
    _, rows, w = g.shape
    hr = rows // 2
    nb = hr // tr

    def body(ids_ref, g_ref, r_ref, o_ref):
        o_ref[0] = (g_ref[0] + r_ref[0]).astype(BF16)

    grid_spec = pltpu.PrefetchScalarGridSpec(
        num_scalar_prefetch=1, grid=(3, nb),
        in_specs=[pl.BlockSpec((1, tr, w), lambda j, i, ids: (ids[1] ^ (j + 1), ids[0] * nb + i, 0)),
                  pl.BlockSpec((1, tr, w), lambda j, i, ids: (ids[1] ^ (j + 1), i, 0))],
        out_specs=pl.BlockSpec((1, tr, w), lambda j, i, ids: (j, i, 0)))
    return pl.pallas_call(
        body, name=name, grid_spec=grid_spec,
        out_shape=jax.ShapeDtypeStruct((3, hr, w), BF16),
        compiler_params=_cparams(),
    )(ids, g, recv)


def _rs_total(g, recv, got, ids, *, name, tr=128):
    _, rows, w = g.shape
    hr = rows // 2
    nb = hr // tr

    def body(ids_ref, g_ref, r_ref, got_ref, o_ref):
        acc = g_ref[0] + r_ref[0]
        for j in range(3):
            acc = acc + got_ref[j].astype(F32)
        o_ref[0] = acc

    grid_spec = pltpu.PrefetchScalarGridSpec(
        num_scalar_prefetch=1, grid=(nb,),
        in_specs=[pl.BlockSpec((1, tr, w), lambda i, ids: (ids[1], ids[0] * nb + i, 0)),
                  pl.BlockSpec((1, tr, w), lambda i, ids: (ids[1], i, 0)),
                  pl.BlockSpec((3, tr, w), lambda i, ids: (0, i, 0))],
        out_specs=pl.BlockSpec((1, tr, w), lambda i, ids: (ids[0], i, 0)))
    return pl.pallas_call(
        body, name=name, grid_spec=grid_spec,
        out_shape=jax.ShapeDtypeStruct((2, hr, w), F32),
        compiler_params=_cparams(),
    )(ids, g, recv, got)


class _ReduceScatter:
    def __init__(self, g, ids, tag):
        self.g, self.ids, self.tag, self.stage, self.result = g, ids, tag, 0, None

    def next_exchange(self):
        if self.stage == 0:
            return _rs_halves(self.g)
        if self.stage == 1:
            return _rs_chips(self.sb)
        return _rs_complete(self.buf)

    def done(self, outs):
        if self.stage == 0:
            self.recv = outs[0]
            hr = self.recv.shape[1]
            self.tr = max(t for t in range(16, 513, 16) if hr % t == 0)
            self.sb = _rs_partial(self.g, self.recv, self.ids, name=f"{self.tag}_partial", tr=self.tr)
        elif self.stage == 1:
            self.buf = _rs_total(self.g, self.recv, outs[0], self.ids, name=f"{self.tag}_total", tr=self.tr)
        else:
            _, hr, w = outs[0].shape
            self.result = outs[0].reshape(2 * hr, w)
        self.stage += 1

    def finish_alone(self):
        names = ("halves", "chips", "complete")
        while self.stage < 3:
            self.done(_run_exchange(self.next_exchange(), name=f"{self.tag}_{names[self.stage]}"))
        return self.result


def _flat_rows():
    used = sum(r for _, r in FSDP_SECTIONS)
    return used, -(-used // ROW_ALIGN) * ROW_ALIGN


def _cols_to_chunks(full):
    rows, cols = full.shape
    t = full.reshape(rows, N_CHIPS, cols // N_CHIPS).transpose(1, 0, 2)
    return t.reshape(N_CHIPS, -1, FLAT_W)


def _chunks_to_cols(chunks, rows, cols):
    return chunks.reshape(N_CHIPS, rows, cols // N_CHIPS).transpose(1, 0, 2).reshape(rows, cols)


def _pad_heads(w, real):
    lead = w.shape[:-1]
    t = w.reshape(lead + (HEADS, real))
    t = jnp.pad(t, [(0, 0)] * len(lead) + [(0, 0), (0, HEAD_PAD - real)])
    return t.reshape(lead + (HEADS * HEAD_PAD,))


def _unpad_heads(w, real):
    lead = w.shape[:-1]
    return w.reshape(lead + (HEADS, HEAD_PAD))[..., :real].reshape(lead + (HEADS * real,))


def _pad_value_lanes(w, axis):
    w = jnp.moveaxis(w, axis, -1)
    lead = w.shape[:-1]
    t = w.reshape(lead + (HEADS, 64))
    t = jnp.pad(t, [(0, 0)] * len(lead) + [(0, 0), (HEAD_PAD - 64, 0)])
    return jnp.moveaxis(t.reshape(lead + (HEADS * HEAD_PAD,)), -1, axis)


def _unpad_value_lanes(w, axis):
    w = jnp.moveaxis(w, axis, -1)
    lead = w.shape[:-1]
    t = w.reshape(lead + (HEADS, HEAD_PAD))[..., HEAD_PAD - 64:]
    return jnp.moveaxis(t.reshape(lead + (HEADS * 64,)), -1, axis)


def _pad_w_in_t(wt):
    z = jnp.zeros((NOPE, wt.shape[1]), wt.dtype)
    z2 = jnp.zeros((HEAD_PAD - NOPE - ROPE, wt.shape[1]), wt.dtype)
    return jnp.concatenate([wt[:1408], z, wt[1408:], z2], axis=0)


def _unpad_w_in_t(wt):
    return jnp.concatenate([wt[:1408], wt[1408 + NOPE:1408 + NOPE + ROPE]], axis=0)


def _rope_tables(positions):
    freqs = ROPE_THETA ** (-jnp.arange(0, ROPE, 2, dtype=F32) / ROPE)
    ang = positions.astype(F32)[..., None] * freqs
    cos, sin = jnp.cos(ang), jnp.sin(ang)
    lead = cos.shape[:-1]
    ones = jnp.ones(lead + (NOPE,), F32)
    zeros_n = jnp.zeros(lead + (NOPE,), F32)
    zeros_p = jnp.zeros(lead + (HEAD_PAD - NOPE - ROPE,), F32)
    ctab = jnp.concatenate([ones, cos, cos, zeros_p], axis=-1)
    stab = jnp.concatenate([zeros_n, -sin, sin, zeros_p], axis=-1)
    return ctab, stab


def _mix_weights(full):
    return dict(
        w_in_t=_pad_w_in_t(full["w_in_t"]),
        w_uq=_pad_heads(full["mla_w_uq"], NOPE + ROPE),
        w_ukv=full["mla_w_ukv"],
        w_out=jnp.concatenate([_pad_value_lanes(full["w_out"][D_GMLP:], 0), full["w_out"][:D_GMLP]], axis=0),
    )


def _small_weights(p, l):
    ws = p["gmlp_ws"][l]
    tril = jnp.tril(jnp.ones((CHUNK, CHUNK), bool))
    bs = p["gmlp_bs"][l]
    bexp = jnp.repeat(bs.reshape(GROUPS // 2, 2, CHUNK).transpose(0, 2, 1), GROUP_DIM, axis=2)
    return dict(
        ws=ws,
        wst=jnp.where(tril[None], ws, 0.0).transpose(0, 2, 1).astype(BF16),
        bexp=bexp,
        g_mix=p["norm_mix_g"][l][None],
        g_ffn=p["norm_ffn_g"][l][None],
        g_q=p["mla_q_norm_g"][l][None],
        g_kv=p["mla_kv_norm_g"][l][None],
        g_og=p["out_norm_gmlp_g"][l][None],
        g_oa=_pad_value_lanes(p["out_norm_mla_g"][l], 0)[None],
    )


def _local_step(x3, target3, positions, mods, final_g, plan):
    bsz, seq, d = x3.shape
    tok = bsz * seq
    tmt = min(512, seq)
    tmk = min(1024, seq)
    tmw = min(2048, tok)
    chunk = (None, None, FLAT_W, FLAT_W)
    chunk2 = (2, None, FLAT_W, FLAT_W)
    ff_grad_shape = (N_CHIPS, 2 * FLAT_W, FLAT_W)
    ctab, stab = _rope_tables(positions)
    lw = [None] * DEPTH

    def flat(t):
        return t.reshape(tok, t.shape[-1])

    def cube(t):
        return t.reshape(bsz, seq, t.shape[-1])

    def carrying(l, tag, fn, *args, **kw):
        side = plan.host(l, tag)
        if side is None:
            return fn(*args, **kw)
        res, side_outs = fn(*args, side=side, **kw)
        plan.hosted(l, tag, side_outs)
        return res

    saved = []
    x = x3
    for l in range(DEPTH):
        lw[l] = plan.layer(l)
        w, mod = lw[l], mods[l]
        if l == 0:
            h1 = carrying(l, "fwd_normmod1", _normmod_fwd, x, w["g_mix"], mod, SHIFT1, SCALE1,
                          name=f"l{l}_normmod1")
        else:
            h1 = h1_next
        z = cube(_mm(flat(h1), w["w_in_t"], dims="nt", name=f"l{l}_w_in", tm=tmk, tn=D_IN_PAD, tk=d,
                     out_dtypes=(BF16,)))
        yg = _gmlp_fwd(z, w["ws"], w["bexp"], w["g_og"], name=f"l{l}_gmlp_fwd")
        q, kv, kp = _mla_prep_fwd(z, w["g_q"], w["g_kv"], w["w_uq"], w["w_ukv"], ctab, stab, name=f"l{l}_mla_prep")
        o, lse = carrying(l, "fwd_attn", _attn_fwd, q, kv, kp, name=f"l{l}_attn_fwd")
        y = _onorm_fwd(o, yg, w["g_oa"], name=f"l{l}_onorm_fwd")

        def normmod(xv, gv, gm, shift_row, scale_row):
            m = gm[0]
            return _rms_fwd(xv, gv, d) * (1.0 + m[scale_row:scale_row + 1, :]) + m[shift_row:shift_row + 1, :]

        def out_epi(po, xv, gm, gf):
            x_new = xv + gm[0][GATE1:GATE1 + 1, :] * po
            return po, x_new, normmod(x_new, gf, gm, SHIFT2, SCALE2)

        vec_spec = pl.BlockSpec((1, d), lambda i, j, k: (0, j))
        po, x_mid, h2 = carrying(l, "fwd_out_a", _mm, flat(y), w["w_out"], dims="nn", name=f"l{l}_w_out",
                                 tm=tmt, tn=d, tk=y.shape[-1], out_dtypes=(BF16, F32, BF16), epilogue=out_epi,
                                 extras=(flat(x), mod, w["g_ffn"]),
                                 extra_specs=(None, _mod_spec(tmt, d, seq), vec_spec))
        x_mid, h2 = cube(x_mid), cube(h2)

        def act_epi(acc):
            r = jnp.maximum(acc, 0.0)
            return (r * r,)

        r = carrying(l, "fwd_ff1", _mm, flat(h2), w["ff"], dims="nn", name=f"l{l}_w_ff1", tm=tmw, tn=FLAT_W,
                     tk=d, out_dtypes=(BF16,), epilogue=act_epi, weights_outer=True, n=D_FF,
                     b_block=(chunk, lambda i, j, k: (j, 0, 0, 0)))

        more = l + 1 < DEPTH

        def ff2_epi(acc, xv, gm, *nxt):
            x_new = xv + gm[0][GATE2:GATE2 + 1, :] * acc
            return (acc, x_new) + ((normmod(x_new, nxt[1], nxt[0], SHIFT1, SCALE1),) if more else ())

        mod_spec = _mod_spec(tmt, d, seq)
        outs = carrying(l, "fwd_ff2", _mm, r, w["ff"], dims="nn", name=f"l{l}_w_ff2", tm=tmt, tn=d, tk=2 * FLAT_W,
                        out_dtypes=(BF16, F32) + ((BF16,) if more else ()), epilogue=ff2_epi,
                        extras=(flat(x_mid), mod) + ((mods[l + 1], plan.layer(l + 1)["g_mix"]) if more else ()),
                        extra_specs=(None, mod_spec) + ((mod_spec, vec_spec) if more else ()), n=d,
                        b_block=(chunk2, lambda i, j, k: (k, 1, 0, 0)))
        f, x_out = outs[0], outs[1]
        h1_next = cube(outs[2]) if more else None
        saved.append(dict(x_in=x, h1=h1, z=z, q=q, kv=kv, kp=kp, o=o, lse=lse, y=y, po=cube(po),
                          x_mid=x_mid, h2=h2, r=r, f=cube(f)))
        x = cube(x_out)

    grads = [dict() for _ in range(DEPTH)]
    dmods = [None] * DEPTH
    top = DEPTH - 1
    node = _resnode_bwd(x, final_g[None], name="final_loss_bwd", target3=target3,
                        branch3=saved[top]["f"], mod_gate=mods[top], gate_row=GATE2)
    loss_part = node["loss"][0, 0]
    d_final_g = node["dg"][0]
    plan.scalars(loss_part, d_final_g)
    for l in range(DEPTH - 1, -1, -1):
        w, mod, s = lw[l], mods[l], saved[l]
        dx_out, dfb, dgate2 = node["dx"], flat(node["dbr"]), node["dgate"][:, 0]

        def dact_epi(acc, rv):
            return (acc * (2.0 * jnp.sqrt(rv.astype(F32))),)

        da = carrying(l, "bwd_d_r", _mm, dfb, w["ff"], dims="nt", name=f"l{l}_d_r", tm=tmw, tn=FLAT_W, tk=d,
                      out_dtypes=(BF16,), epilogue=dact_epi, extras=(s["r"],), weights_outer=True, n=D_FF,
                      b_block=(chunk, lambda i, j, k: (j, 1, 0, 0)))
        g_ff = carrying(l, "bwd_dw_ff2", _mm, s["r"], dfb, dims="tn", name=f"l{l}_dw_ff2", tm=FLAT_W, tn=d,
                        tk=2048, out_into=(ff_grad_shape, (None, FLAT_W, FLAT_W), lambda i, j, k: (i, 1, 0), None))
        g_ff = carrying(l, "bwd_dw_ff1", _mm, flat(s["h2"]), da, dims="tn", name=f"l{l}_dw_ff1", tm=d, tn=FLAT_W,
                        tk=2048, out_into=(ff_grad_shape, (None, FLAT_W, FLAT_W), lambda i, j, k: (j, 0, 0), g_ff))
        plan.ff_grads(l, g_ff)
        dh2 = carrying(l, "bwd_d_h2", _mm, da, w["ff"], dims="nt", name=f"l{l}_d_h2", tm=tmk, tn=d, tk=2 * FLAT_W,
                       n=d, b_block=(chunk2, lambda i, j, k: (k, 0, 0, 0)), out_dtypes=(BF16,))
        node = carrying(l, "bwd_resnode_ffn", _resnode_bwd, s["x_mid"], w["g_ffn"], name=f"l{l}_resnode_ffn",
                        dh3=cube(dh2), dres3=dx_out, mod_nm=mod, rows=(SHIFT2, SCALE2), branch3=s["po"],
                        mod_gate=mod, gate_row=GATE1)
        grads[l]["norm_ffn_g"] = node["dg"][0]
        dshift2, dscale2 = node["dnm"][:, 0], node["dnm"][:, 1]
        dx_mid, dpo, dgate1 = node["dx"], flat(node["dbr"]), node["dgate"][:, 0]

        wy = s["y"].shape[-1]
        dy = cube(carrying(l, "bwd_d_y", _mm, dpo, w["w_out"], dims="nt", name=f"l{l}_d_y", tm=tmk, tn=wy, tk=d,
                           out_dtypes=(BF16,)))
        dw_out = _mm(flat(s["y"]), dpo, dims="tn", name=f"l{l}_dw_out", tm=wy // 3, tn=d, tk=2048)
        hw = HEADS * HEAD_PAD
        grads[l]["w_out"] = jnp.concatenate([dw_out[hw:], _unpad_value_lanes(dw_out[:hw], 0)], axis=0)

        dz, dws, dbs, dg_og = _gmlp_bwd(s["z"], dy, w["ws"], w["wst"], w["bexp"], w["g_og"],
                                        name=f"l{l}_gmlp_bwd", dy_col=hw // D_GMLP)
        grads[l]["gmlp_ws"], grads[l]["gmlp_bs"], grads[l]["out_norm_gmlp_g"] = dws, dbs, dg_og[0]

        do, dl, dg_oa = _onorm_bwd(s["o"], dy, w["g_oa"], name=f"l{l}_onorm_bwd")
        grads[l]["out_norm_mla_g"] = _unpad_value_lanes(dg_oa[0], 0)
        plan.small_ready(l, grads[l])
        dq, dk, dv = carrying(l, "bwd_attn_dkv", _attn_bwd, s["q"], s["kv"], s["kp"], do, s["lse"], dl,
                              name=f"l{l}_attn_bwd")
        dz, cq, dqb, ckv, dkvb, dg_q, dg_kv = _mla_prep_bwd(
            s["z"], dz, dq, dk, dv, w["g_q"], w["g_kv"], w["w_uq"], w["w_ukv"], ctab, stab,
            name=f"l{l}_mla_prep_bwd")
        grads[l]["mla_q_norm_g"], grads[l]["mla_kv_norm_g"] = dg_q[0], dg_kv[0]
        dw_uq = carrying(l, "bwd_dw_uq", _mm, flat(cq), flat(dqb), dims="tn", name=f"l{l}_dw_uq", tm=Q_RANK,
                         tn=1024, tk=4096)
        grads[l]["mla_w_uq"] = _unpad_heads(dw_uq, NOPE + ROPE)
        grads[l]["w_in_t"] = _unpad_w_in_t(carrying(l, "bwd_dw_in", _mm, flat(dz), flat(s["h1"]), dims="tn",
                                                    name=f"l{l}_dw_in", tm=D_IN_PAD // 2, tn=d, tk=2048))
        grads[l]["mla_w_ukv"] = carrying(l, "bwd_dw_ukv", _mm, flat(ckv), flat(dkvb), dims="tn", name=f"l{l}_dw_ukv",
                                         tm=KV_RANK, tn=1024, tk=4096)
        plan.layer_grads(l, grads[l])
        dh1 = carrying(l, "bwd_d_h1", _mm, flat(dz), w["w_in_t"], dims="nn", name=f"l{l}_d_h1", tm=tmk, tn=d,
                       tk=D_IN_PAD, out_dtypes=(BF16,))
        below = dict(branch3=saved[l - 1]["f"], mod_gate=mods[l - 1], gate_row=GATE2) if l > 0 else {}
        node = carrying(l, "bwd_resnode_mix", _resnode_bwd, s["x_in"], w["g_mix"], name=f"l{l}_resnode_mix",
                        dh3=cube(dh1), dres3=dx_mid, mod_nm=mod, rows=(SHIFT1, SCALE1), **below)
        grads[l]["norm_mix_g"] = node["dg"][0]
        dshift1, dscale1 = node["dnm"][:, 0], node["dnm"][:, 1]
        dmods[l] = jnp.stack([dshift1, dscale1, dgate1, dshift2, dscale2, dgate2], axis=1)
    return node["dx"], dmods


W_NAMES = ("w_ada", "b_ada", "norm_mix_g", "w_in", "gmlp_ws", "gmlp_bs", "mla_q_norm_g", "mla_kv_norm_g",
           "mla_w_uq", "mla_w_ukv", "out_norm_gmlp_g", "out_norm_mla_g", "w_out", "norm_ffn_g", "w_ff1", "w_ff2",
           "final_norm_g")
FLAT_KEY = {"w_in": "w_in", "w_uq": "mla_w_uq", "w_ukv": "mla_w_ukv", "w_out": "w_out", "w_ff1": "w_ff1",
            "w_ff2": "w_ff2"}
COL_SHARDED = ("w_in", "w_uq", "w_ukv", "w_ff1")
FULL_SHAPE = {"w_in": (D_MODEL, D_IN), "w_uq": (Q_RANK, HEADS * (NOPE + ROPE)), "w_ukv": (KV_RANK, HEADS * 128),
              "w_out": (D_MODEL, D_MODEL)}
SMALL_LAYER_NAMES = ("gmlp_ws", "gmlp_bs", "out_norm_gmlp_g", "out_norm_mla_g", "norm_ffn_g")
LATE_SMALL_NAMES = ("norm_mix_g", "mla_q_norm_g", "mla_kv_norm_g")


def _silu(v):
    return v * (1.0 / (1.0 + jnp.exp(-v)))


class _CommPlan:
    FWD = {"fwd_attn": ("ff", 0, "spread"), "fwd_out_a": ("ff", 0, "pass"),
           "fwd_ff1": ("mix", 1, "spread"), "fwd_ff2": ("mix", 1, "pass")}
    BWD = {"bwd_d_r": ("mix", 1), "bwd_dw_ff2": ("mix", 1),
           "bwd_d_h2": ("ff", 0), "bwd_attn_dkv": ("ff", 0), "bwd_dw_uq": ("ff", 0)}
    BWD_ALSO = {"bwd_d_h2": ("mix", 1)}
    BWD_LAST = {"bwd_d_h1": ("mix", 0), "bwd_resnode_mix": ("mix", 0)}
    SMALL = {"bwd_attn_dkv": "spread", "bwd_dw_uq": "pass"}

    def __init__(self, weights, ids, dev, core):
        self.weights, self.ids, self.dev, self.core = weights, ids, dev, core
        self.used, self.rows = _flat_rows()
        self.flat = {("mix", l): self._flat_mix(l) for l in range(DEPTH)}
        self.flat.update({("ff", l): jnp.concatenate([weights["w_ff1"][l], weights["w_ff2"][l]], axis=0).astype(BF16)
                          for l in range(DEPTH)})
        self.lw, self.rs, self.grads, self.spread = {}, {}, {}, {}
        self.small_vec, self.small_sum, self.small_spread, self.extra = {}, {}, None, {}
        self.lw = {l: _small_weights(weights, l) for l in range(DEPTH)}

    def _flat_mix(self, l):
        pieces = []
        for nm, _ in FSDP_SECTIONS:
            shard = self.weights[FLAT_KEY[nm]][l]
            pieces.append(shard.T if nm == "w_in" else shard.reshape(-1, FLAT_W))
        pieces.append(jnp.zeros((self.rows - self.used, FLAT_W), F32))
        return jnp.concatenate(pieces, axis=0).astype(BF16)

    def _arrived(self, group, l, gath):
        flat = self.flat[group, l]
        hr = flat.shape[0] // 2
        mine = lax.dynamic_slice(flat, (self.core * hr, 0), (hr, FLAT_W))
        gath = lax.dynamic_update_slice(gath, mine[None], (self.dev, 0, 0))
        if group == "ff":
            self.lw[l]["ff"] = gath.reshape(N_CHIPS, 2, hr, FLAT_W)
            return
        w_gath = gath.reshape(N_CHIPS, self.rows, FLAT_W)
        full, off = {}, 0
        for nm, nrows in FSDP_SECTIONS:
            sec = w_gath[:, off:off + nrows]
            off += nrows
            rows, cols = FULL_SHAPE[nm]
            if nm == "w_in":
                full["w_in_t"] = sec.reshape(cols, rows)
            else:
                full[FLAT_KEY[nm]] = (_chunks_to_cols(sec, rows, cols) if nm in COL_SHARDED
                                      else sec.reshape(rows, cols))
        self.lw[l].update(_mix_weights(full))

    def layer(self, l):
        return self.lw[l]

    def host(self, l, tag):
        if tag == "fwd_normmod1":
            return _gather_spread(self.flat["mix", 0]) if l == 0 else None
        if tag in self.FWD:
            group, ahead, what = self.FWD[tag]
            if l + ahead >= DEPTH:
                return None
            return _gather_spread(self.flat[group, l + ahead]) if what == "spread" else _gather_pass_on(self.spread[group])
        ex = None
        for rs in self._rs_for(l, tag):
            ex = rs.next_exchange() if ex is None else _both(ex, rs.next_exchange())
        if tag in self.SMALL:
            small = (_gather_spread(self.small_vec[l], halves=False) if self.SMALL[tag] == "spread"
                     else _gather_pass_on(self.small_spread))
            ex = small if ex is None else _both(ex, small)
        return ex

    def _rs_for(self, l, tag):
        found = []
        if tag in self.BWD_LAST and l == 0:
            found.append(self.rs.get(self.BWD_LAST[tag]))
        for table in (self.BWD, self.BWD_ALSO):
            if tag in table:
                group, ahead = table[tag]
                found.append(self.rs.get((group, l + ahead)))
        return [rs for rs in found if rs is not None and rs.stage <= 2]

    def hosted(self, l, tag, outs):
        if tag == "fwd_normmod1":
            self._arrived("mix", 0, _run_exchange(_gather_pass_on(outs[0]), name="l0_mix_gather_pass_on")[0])
        elif tag in self.FWD:
            group, ahead, what = self.FWD[tag]
            if what == "spread":
                self.spread[group] = outs[0]
            else:
                self._arrived(group, l + ahead, outs[0])
        else:
            for rs in self._rs_for(l, tag):
                rs.done(outs[:1])
                outs = outs[1:]
            if tag in self.SMALL:
                if self.SMALL[tag] == "spread":
                    self.small_spread = outs[0]
                else:
                    self._small_arrived(l, outs[0])

    def ff_grads(self, l, g_ff):
        self.rs["ff", l] = _ReduceScatter(g_ff, self.ids, f"l{l}_ff_rs")

    def layer_grads(self, l, grads):
        self.grads[l] = grads
        pieces = []
        for nm, nrows in FSDP_SECTIONS:
            if nm == "w_in":
                pieces.append(grads["w_in_t"].reshape(N_CHIPS, nrows, FLAT_W))
                continue
            g = grads[FLAT_KEY[nm]]
            pieces.append(_cols_to_chunks(g) if nm in COL_SHARDED else g.reshape(N_CHIPS, nrows, FLAT_W))
        pieces.append(jnp.zeros((N_CHIPS, self.rows - self.used, FLAT_W), F32))
        self.rs["mix", l] = _ReduceScatter(jnp.concatenate(pieces, axis=1), self.ids, f"l{l}_mix_rs")

    def scalars(self, loss_part, d_final_g):
        self.extra = {0: [loss_part[None]]}
        self.extra.setdefault(DEPTH - 1, []).insert(0, d_final_g)

    def small_ready(self, l, grads):
        parts = [grads[nm].reshape(-1) for nm in SMALL_LAYER_NAMES] + self.extra.get(l, [])
        vec = jnp.concatenate(parts)
        rows = -(-vec.shape[0] // (8 * FLAT_W)) * 8
        self.small_vec[l] = jnp.pad(vec, (0, rows * FLAT_W - vec.shape[0])).reshape(rows, FLAT_W)

    def _small_arrived(self, l, gath):
        gath = lax.dynamic_update_slice(gath, self.small_vec[l][None], (self.dev, 0, 0))
        self.small_sum[l] = _sum_leading(gath, name=f"l{l}_small_sum").reshape(-1)

    def finish(self, dmod):
        late = jnp.concatenate([jnp.stack([self.grads[l][nm] for l in range(DEPTH)], axis=0).reshape(-1)
                                for nm in LATE_SMALL_NAMES])
        head = -(-late.shape[0] // (8 * FLAT_W)) * 8
        late = jnp.pad(late, (0, head * FLAT_W - late.shape[0])).reshape(head, FLAT_W)
        vec = jnp.concatenate([late, dmod], axis=0)
        rs = self.rs["mix", 0]
        while rs.stage < 2:
            rs.done(_run_exchange(rs.next_exchange(), name=f"l0_mix_rs_stage{rs.stage}"))
        outs = _run_exchange(_both(rs.next_exchange(), _gather_spread(vec, halves=False)), name="final_spread")
        rs.done(outs[:1])
        (gath,) = _run_exchange(_gather_pass_on(outs[1]), name="final_pass_on")
        gath = lax.dynamic_update_slice(gath, vec[None], (self.dev, 0, 0))
        late_sum = _sum_leading(gath[:, :head], name="late_small_sum").reshape(-1)
        loss, res = self._small_grads(late_sum)
        return loss, res, gath[:, head:]

    def _small_grads(self, late):
        out = {nm: [] for nm in SMALL_LAYER_NAMES}
        for l in range(DEPTH):
            off = 0
            for nm in SMALL_LAYER_NAMES:
                size = self.weights[nm][l].size
                out[nm].append(self.small_sum[l][off:off + size].reshape(self.weights[nm].shape[1:]))
                off += size
            if l == DEPTH - 1:
                final = self.small_sum[l][off:off + self.weights["final_norm_g"].size]
                off += final.shape[0]
            if l == 0:
                loss = self.small_sum[l][off]
        res = {nm: jnp.stack(parts, axis=0) for nm, parts in out.items()}
        res["final_norm_g"] = final
        off = 0
        for nm in LATE_SMALL_NAMES:
            size = self.weights[nm].size
            res[nm] = late[off:off + size].reshape(self.weights[nm].shape)
            off += size
        return loss, res

    def mix_grads(self):
        per = {FLAT_KEY[nm]: [] for nm, _ in FSDP_SECTIONS}
        for l in range(DEPTH):
            shard, off = self.rs["mix", l].result, 0
            for nm, nrows in FSDP_SECTIONS:
                key = FLAT_KEY[nm]
                sec = shard[off:off + nrows]
                per[key].append(sec.T if nm == "w_in" else sec.reshape(self.weights[key].shape[1:]))
                off += nrows
        return {key: jnp.stack(parts, axis=0) for key, parts in per.items()}

    def ff_shards(self):
        return [self.rs["ff", l].result for l in range(DEPTH)]


def kernel(x, c, positions, w_ada, b_ada, norm_mix_g, w_in, gmlp_ws, gmlp_bs, mla_q_norm_g, mla_kv_norm_g, mla_w_uq, mla_w_ukv, out_norm_gmlp_g, out_norm_mla_g, w_out, norm_ffn_g, w_ff1, w_ff2, final_norm_g, loss_target, m_w_ada, m_b_ada, m_norm_mix_g, m_w_in, m_gmlp_ws, m_gmlp_bs, m_mla_q_norm_g, m_mla_kv_norm_g, m_mla_w_uq, m_mla_w_ukv, m_out_norm_gmlp_g, m_out_norm_mla_g, m_w_out, m_norm_ffn_g, m_w_ff1, m_w_ff2, m_final_norm_g, v_w_ada, v_b_ada, v_norm_mix_g, v_w_in, v_gmlp_ws, v_gmlp_bs, v_mla_q_norm_g, v_mla_kv_norm_g, v_mla_w_uq, v_mla_w_ukv, v_out_norm_gmlp_g, v_out_norm_mla_g, v_w_out, v_norm_ffn_g, v_w_ff1, v_w_ff2, v_final_norm_g):
    weights = dict(w_ada=w_ada, b_ada=b_ada, norm_mix_g=norm_mix_g, w_in=w_in, gmlp_ws=gmlp_ws, gmlp_bs=gmlp_bs,
                   mla_q_norm_g=mla_q_norm_g, mla_kv_norm_g=mla_kv_norm_g, mla_w_uq=mla_w_uq, mla_w_ukv=mla_w_ukv,
                   out_norm_gmlp_g=out_norm_gmlp_g, out_norm_mla_g=out_norm_mla_g, w_out=w_out,
                   norm_ffn_g=norm_ffn_g, w_ff1=w_ff1, w_ff2=w_ff2, final_norm_g=final_norm_g)
    mom_m = dict(zip(W_NAMES, (m_w_ada, m_b_ada, m_norm_mix_g, m_w_in, m_gmlp_ws, m_gmlp_bs, m_mla_q_norm_g,
                               m_mla_kv_norm_g, m_mla_w_uq, m_mla_w_ukv, m_out_norm_gmlp_g, m_out_norm_mla_g,
                               m_w_out, m_norm_ffn_g, m_w_ff1, m_w_ff2, m_final_norm_g)))
    mom_v = dict(zip(W_NAMES, (v_w_ada, v_b_ada, v_norm_mix_g, v_w_in, v_gmlp_ws, v_gmlp_bs, v_mla_q_norm_g,
                               v_mla_kv_norm_g, v_mla_w_uq, v_mla_w_ukv, v_out_norm_gmlp_g, v_out_norm_mla_g,
                               v_w_out, v_norm_ffn_g, v_w_ff1, v_w_ff2, v_final_norm_g)))
    bsz, seq, d = x.shape
    px, py, pc = _position()
    chip = 2 * px + py
    dev = 2 * chip + pc
    ids = jnp.stack([pc, chip]).astype(jnp.int32)
    n_ex = N_DEV * bsz
    ada_cols = w_ada.shape[-1]

    c_all = _allgather8(c.reshape(bsz * d // 128, 128), name="gather_c").reshape(n_ex, d)
    mod_parts = []
    for l in range(DEPTH):
        bias = lax.dynamic_slice(b_ada[l], (chip * ada_cols,), (ada_cols,))[None]
        mod_parts.append(_mm(c_all, w_ada, dims="nn", name=f"l{l}_mod", tm=n_ex, tn=ada_cols, tk=d, n=ada_cols,
                             b_block=((None, d, ada_cols), lambda i, j, k, l=l: (l, k, j)),
                             epilogue=lambda acc, bv: (acc + bv,), extras=(bias,),
                             extra_specs=(pl.BlockSpec((1, ada_cols), lambda i, j, k: (0, j)),), a_fn=_silu))
    mod_g = _allgather8(jnp.concatenate(mod_parts, axis=0), name="gather_mod")
    mod_g = mod_g.reshape(N_CHIPS, 2, DEPTH, n_ex, ada_cols)[:, 0]
    mod_full = mod_g.transpose(1, 2, 0, 3).reshape(DEPTH, n_ex, N_CHIPS * ada_cols)
    mod_mine = lax.dynamic_slice(mod_full, (0, dev * bsz, 0), (DEPTH, bsz, N_MOD * d))
    mod_mine = jnp.pad(mod_mine.reshape(DEPTH, bsz, N_MOD, d), ((0, 0), (0, 0), (0, MOD_ROWS - N_MOD), (0, 0)))
    mods = [mod_mine[l] for l in range(DEPTH)]

    plan = _CommPlan(weights, ids, dev, pc)
    grad_x, dmods = _local_step(x, loss_target, positions, mods, final_norm_g, plan)

    dmod = jnp.stack(dmods, axis=1).reshape(bsz * DEPTH * N_MOD, d)
    loss, small, dmod_all = plan.finish(dmod)
    grad = plan.mix_grads()
    grad.update(small)
    dmod_all = dmod_all.reshape(n_ex, DEPTH, N_MOD * d)
    gw, gb = [], []
    for l in range(DEPTH):
        dm = dmod_all[:, l]
        dm_cols = lax.dynamic_slice(dm, (0, chip * ada_cols), (n_ex, ada_cols))
        gw.append(_mm(c_all, dm_cols, dims="tn", name=f"l{l}_dw_ada", tm=d, tn=ada_cols, tk=n_ex, a_fn=_silu,
                      out_into=(w_ada.shape, (None, d, ada_cols), lambda i, j, k, l=l: (l, i, j),
                                gw[-1] if gw else None)))
        gb.append(_sum_leading(dm.reshape(n_ex, N_MOD * d // FLAT_W, FLAT_W), name=f"l{l}_db_ada").reshape(-1))
    grad["w_ada"] = gw[-1]
    grad["b_ada"] = jnp.stack(gb, axis=0)

    delta, new_m, new_v = {}, {}, {}
    ff_bufs = plan.ff_shards()
    for nm, row_off in (("w_ff1", 0), ("w_ff2", FLAT_W)):
        grad[nm], delta[nm], new_m[nm], new_v[nm] = _adamw_layers(
            weights[nm], mom_m[nm], mom_v[nm], ff_bufs, row_off, name=f"adamw_{nm}")
    for nm in W_NAMES:
        if nm not in delta:
            delta[nm], new_m[nm], new_v[nm] = _adamw(weights[nm], grad[nm], mom_m[nm], mom_v[nm],
                                                     name=f"adamw_{nm}")
    return (loss, grad_x, *[grad[nm] for nm in W_NAMES], *[delta[nm] for nm in W_NAMES],
            *[new_m[nm] for nm in W_NAMES], *[new_v[nm] for nm in W_NAMES])
```

```python
import functools
import math

import jax
import jax.numpy as jnp
from jax import lax
from jax.experimental import pallas as pl
from jax.experimental.pallas import tpu as pltpu

F32 = jnp.float32
BF16 = jnp.bfloat16

D_MODEL = 1024
DEPTH = 2
D_GMLP = 512
GROUPS = 8
GROUP_DIM = 64
CHUNK = 128
HEADS = 8
NOPE = 64
ROPE = 32
HEAD_PAD = 128
Q_RANK = 256
KV_RANK = 128
D_FF = 4096
N_MOD = 6
MOD_ROWS = 8
EPS = 1e-6
ROPE_THETA = 10000.0
D_IN = 1440
D_IN_PAD = 1536
ATTN_SCALE = (NOPE + ROPE) ** -0.5
LOG2E = math.log2(math.e)
SCALE_LOG2 = ATTN_SCALE * LOG2E
N_CHIPS = 4
N_DEV = 8

ADAM_LR = 0.001
ADAM_B1 = 0.9
ADAM_B2 = 0.999
ADAM_EPS = 1e-08
ADAM_WD = 0.01
ADAM_STEP = 10

VMEM_LIMIT = 48 * 1024 * 1024
FLAT_W = 1024
ROW_ALIGN = 256

NN = (((1,), (0,)), ((), ()))
NT = (((1,), (1,)), ((), ()))
TN = (((0,), (0,)), ((), ()))
MESH = pl.DeviceIdType.MESH

SHIFT1, SCALE1, GATE1, SHIFT2, SCALE2, GATE2 = range(6)

FSDP_SECTIONS = (("w_out", 256), ("w_in", 360), ("w_uq", 48), ("w_ukv", 32))


def _cparams(vmem=VMEM_LIMIT):
    return pltpu.CompilerParams(vmem_limit_bytes=vmem)


def _dot(a, b, dims=NN):
    return lax.dot_general(a, b, dims, preferred_element_type=F32)


def _iota(shape, axis):
    return lax.broadcasted_iota(jnp.int32, shape, axis)


def _gelu(x):
    k = math.sqrt(2.0 / math.pi)
    return 0.5 * x * (1.0 + jnp.tanh(k * (x + 0.044715 * (x * x * x))))


def _gelu_and_grad(x):
    k = math.sqrt(2.0 / math.pi)
    x2 = x * x
    t = jnp.tanh(k * (x + 0.044715 * (x2 * x)))
    half = 0.5 * (1.0 + t)
    return x * half, half + 0.5 * x * (1.0 - t * t) * (k * (1.0 + 3.0 * 0.044715 * x2))


def _rms_fwd(x, g, n):
    r = lax.rsqrt(jnp.sum(x * x, axis=-1, keepdims=True) * (1.0 / n) + EPS)
    return x * r * g


def _rms_bwd(x, g, dy, n):
    r = lax.rsqrt(jnp.sum(x * x, axis=-1, keepdims=True) * (1.0 / n) + EPS)
    xh = x * r
    dxh = dy * g
    dx = r * (dxh - xh * (jnp.sum(dxh * xh, axis=-1, keepdims=True) * (1.0 / n)))
    dg = jnp.sum(dy * xh, axis=0, keepdims=True)
    return dx, dg


def _pick_rows(rows, limit):
    if rows <= limit:
        return rows
    for t in range(limit, 7, -8):
        if rows % t == 0:
            return t
    return rows


def _mm(a, b, *, dims, name, tm=512, tn=1024, tk=1024, out_dtypes=(F32,), epilogue=None,
        extras=(), extra_specs=(), a_fn=None, weights_outer=False, side=None, b_block=None, n=None,
        out_into=None):
    if dims == "tn":
        kk, m = a.shape
    else:
        m, kk = a.shape
    if n is None:
        n = b.shape[0] if dims == "nt" else b.shape[1]
    tm, tn, tk = min(tm, m), min(tn, n), min(tk, kk)
    assert m % tm == 0 and n % tn == 0 and kk % tk == 0, (name, a.shape, b.shape, tm, tn, tk)
    ni, nj, nk = m // tm, n // tn, kk // tk

    def spec(shape, pick):
        if weights_outer:
            return pl.BlockSpec(shape, lambda j, i, k: pick(i, j, k))
        return pl.BlockSpec(shape, pick)

    if dims == "tn":
        a_spec = spec((tk, tm), lambda i, j, k: (k, i))
    else:
        a_spec = spec((tm, tk), lambda i, j, k: (i, k))
    if b_block is not None:
        b_spec = spec(*b_block)
    elif dims == "nt":
        b_spec = spec((tn, tk), lambda i, j, k: (j, k))
    else:
        b_spec = spec((tk, tn), lambda i, j, k: (k, j))
    o_spec = spec((tm, tn), lambda i, j, k: (i, j))
    out_shape = [jax.ShapeDtypeStruct((m, n), dt) for dt in out_dtypes]
    out_specs = [o_spec] * len(out_dtypes)
    prev, io_aliases = (), {}
    if out_into is not None:
        full_shape, block, index, before = out_into
        assert len(out_dtypes) == 1 and not extras
        out_shape = [jax.ShapeDtypeStruct(full_shape, out_dtypes[0])]
        out_specs = [spec(block, index)]
        if before is not None:
            prev, io_aliases = (before,), {2: 0}
    assert not (weights_outer and extra_specs)
    dn = {"nn": NN, "nt": NT, "tn": TN}[dims]
    n_ex, n_out = len(extras), len(out_dtypes)
    e_specs = [o_spec if s is None else s for s in (tuple(extra_specs) + (None,) * n_ex)[:n_ex]]

    n_prev = len(prev)

    def body(*refs):
        a_ref, b_ref = refs[0], refs[1]
        e_refs = refs[2 + n_prev:2 + n_prev + n_ex]
        o_refs = refs[2 + n_prev + n_ex:2 + n_prev + n_ex + n_out]
        av = a_ref[...]
        if a_fn is not None:
            av = a_fn(av)
        bv = b_ref[...]
        if bv.ndim == 3:
            if dims == "nt":
                bv = jnp.concatenate([bv[c] for c in range(bv.shape[0])], axis=1)
            else:
                bv = bv.reshape(-1, bv.shape[-1])
        part = _dot(av.astype(BF16), bv.astype(BF16), dn)

        def finish(acc):
            outs = (acc,) if epilogue is None else epilogue(acc, *[e[...] for e in e_refs])
            for o_ref, o in zip(o_refs, outs):
                o_ref[...] = o.astype(o_ref.dtype)

        if nk == 1:
            finish(part)
        else:
            acc_ref = refs[-1]
            k = pl.program_id(2)

            @pl.when(k == 0)
            def _():
                acc_ref[...] = part

            @pl.when(k > 0)
            def _():
                acc_ref[...] += part

            @pl.when(k == nk - 1)
            def _():
                finish(acc_ref[...])

    outs, side_outs = _hosted_call(
        body, name=name, grid=(nj, ni, nk) if weights_outer else (ni, nj, nk),
        in_specs=[a_spec, b_spec] + [ANY_SPEC] * n_prev + e_specs,
        out_specs=out_specs, out_shape=out_shape,
        scratch_shapes=[pltpu.VMEM((tm, tn), F32)] if nk > 1 else [],
        args=(a, b, *prev, *extras), side=side, io_aliases=io_aliases)
    res = outs[0] if n_out == 1 else outs
    return res if side is None else (res, side_outs)


def _mod_spec(tm, tn, seq):
    return pl.BlockSpec((1, MOD_ROWS, tn), lambda i, j, k: ((i * tm) // seq, 0, j))


def _normmod_fwd(x3, g, mod, shift_row, scale_row, *, name, tb=512, side=None):
    bsz, seq, d = x3.shape
    tb = min(tb, seq)

    def body(x_ref, g_ref, mod_ref, h_ref):
        m = mod_ref[0]
        nrm = _rms_fwd(x_ref[0], g_ref[...], d)
        h = nrm * (1.0 + m[scale_row:scale_row + 1, :]) + m[shift_row:shift_row + 1, :]
        h_ref[0] = h.astype(BF16)

    outs, side_outs = _hosted_call(
        body, name=name, grid=(bsz, seq // tb),
        in_specs=[pl.BlockSpec((1, tb, d), lambda b, i: (b, i, 0)),
                  pl.BlockSpec((1, d), lambda b, i: (0, 0)),
                  pl.BlockSpec((1, MOD_ROWS, d), lambda b, i: (b, 0, 0))],
        out_specs=[pl.BlockSpec((1, tb, d), lambda b, i: (b, i, 0))],
        out_shape=[jax.ShapeDtypeStruct((bsz, seq, d), BF16)],
        args=(x3, g, mod), side=side)
    return outs[0] if side is None else (outs[0], side_outs)


def _pair_mean_exact(x, lo):
    s_lo = jnp.sum(jnp.where(lo, x, 0.0), axis=-1, keepdims=True)
    s_hi = jnp.sum(jnp.where(lo, 0.0, x), axis=-1, keepdims=True)
    return jnp.where(lo, s_lo, s_hi) * (1.0 / GROUP_DIM)


def _gmlp_pair_fwd(gv_p, w0, w1, bias, lo):
    mu = _pair_mean_exact(gv_p, lo)
    dlt = gv_p - mu
    var = _pair_mean_exact(dlt * dlt, lo)
    rstd = lax.rsqrt(var + EPS)
    vn = dlt * rstd
    vnb = vn.astype(BF16)
    mixed = jnp.where(lo, _dot(w0, vnb), _dot(w1, vnb)) + bias
    return vn, vnb, rstd, mixed


def _tril_bf16(w):
    t = w.shape[-1]
    return jnp.where(_iota((t, t), 1) <= _iota((t, t), 0), w, 0.0).astype(BF16)


def _gmlp_fwd(z3, ws, bexp, g_out, *, name):
    bsz, seq, _ = z3.shape
    cpb = 2 if (seq // CHUNK) % 2 == 0 else 1
    nc = seq // (CHUNK * cpb)
    tb = CHUNK * cpb

    def body(u_ref, v_ref, ws_ref, b_ref, g_ref, y_ref):
        lo = _iota((CHUNK, 128), 1) < GROUP_DIM
        for c in range(cpb):
            rows = slice(CHUNK * c, CHUNK * (c + 1))
            gu = _gelu(u_ref[0, rows, :].astype(F32))
            gv = _gelu(v_ref[0, rows, :].astype(F32))
            parts = []
            for p in range(GROUPS // 2):
                sl = slice(128 * p, 128 * p + 128)
                w0 = _tril_bf16(ws_ref[2 * p])
                w1 = _tril_bf16(ws_ref[2 * p + 1])
                _, _, _, mixed = _gmlp_pair_fwd(gv[:, sl], w0, w1, b_ref[p], lo)
                parts.append(gu[:, sl] * mixed)
            yg = jnp.concatenate(parts, axis=1)
            y_ref[0, rows, :] = _rms_fwd(yg, g_ref[...], D_GMLP).astype(BF16)

    return pl.pallas_call(
        body, name=name, grid=(bsz, nc),
        in_specs=[pl.BlockSpec((1, tb, D_GMLP), lambda b, i: (b, i, 0)),
                  pl.BlockSpec((1, tb, D_GMLP), lambda b, i: (b, i, 1)),
                  pl.BlockSpec((GROUPS, CHUNK, CHUNK), lambda b, i: (0, 0, 0)),
                  pl.BlockSpec((GROUPS // 2, CHUNK, 128), lambda b, i: (0, 0, 0)),
                  pl.BlockSpec((1, D_GMLP), lambda b, i: (0, 0))],
        out_specs=pl.BlockSpec((1, tb, D_GMLP), lambda b, i: (b, i, 0)),
        out_shape=jax.ShapeDtypeStruct((bsz, seq, D_GMLP), BF16),
        compiler_params=_cparams(),
    )(z3, z3, ws, bexp, g_out)


def _gmlp_bwd(z3, dyn3, ws, wst, bexp, g_out, *, name, dy_col):
    bsz, seq, _ = z3.shape
    cpb = 2 if (seq // CHUNK) % 2 == 0 else 1
    nc = seq // (CHUNK * cpb)
    tb = CHUNK * cpb
    npair = GROUPS // 2

    def body(u_ref, v_ref, dy_ref, ws_ref, wst_ref, b_ref, g_ref, duv_ref, dws_ref, dbs_ref, dg_ref, dbacc):
        first = jnp.logical_and(pl.program_id(0) == 0, pl.program_id(1) == 0)
        last = jnp.logical_and(pl.program_id(0) == bsz - 1, pl.program_id(1) == nc - 1)

        @pl.when(first)
        def _():
            dws_ref[...] = jnp.zeros_like(dws_ref)
            dg_ref[...] = jnp.zeros_like(dg_ref)
            dbacc[...] = jnp.zeros_like(dbacc)

        lo = _iota((CHUNK, 128), 1) < GROUP_DIM
        tril = _iota((CHUNK, CHUNK), 1) <= _iota((CHUNK, CHUNK), 0)
        for c in range(cpb):
            rows = slice(CHUNK * c, CHUNK * (c + 1))
            u = u_ref[0, rows, :].astype(F32)
            v = v_ref[0, rows, :].astype(F32)
            gu, dgu = _gelu_and_grad(u)
            gv, dgv_dv = _gelu_and_grad(v)
            fwd = []
            for p in range(npair):
                sl = slice(128 * p, 128 * p + 128)
                w0 = _tril_bf16(ws_ref[2 * p])
                w1 = _tril_bf16(ws_ref[2 * p + 1])
                fwd.append(_gmlp_pair_fwd(gv[:, sl], w0, w1, b_ref[p], lo))
            yg = jnp.concatenate([gu[:, 128 * p:128 * p + 128] * fwd[p][3] for p in range(npair)], axis=1)
            dyg, dg = _rms_bwd(yg, g_ref[...], dy_ref[0, rows, :].astype(F32), D_GMLP)
            dg_ref[...] += dg
            du_parts, dv_parts = [], []
            for p in range(npair):
                sl = slice(128 * p, 128 * p + 128)
                vn, vnb, rstd, mixed = fwd[p]
                dyg_p = dyg[:, sl]
                dmixed = dyg_p * gu[:, sl]
                dbacc[p] += dmixed
                dm_lo = jnp.where(lo, dmixed, 0.0).astype(BF16)
                dm_hi = jnp.where(lo, 0.0, dmixed).astype(BF16)
                dws_ref[2 * p] += jnp.where(tril, _dot(dm_lo, vnb, NT), 0.0)
                dws_ref[2 * p + 1] += jnp.where(tril, _dot(dm_hi, vnb, NT), 0.0)
                dmb = dmixed.astype(BF16)
                dvn = jnp.where(lo, _dot(wst_ref[2 * p], dmb), _dot(wst_ref[2 * p + 1], dmb))
                dgv = rstd * (dvn - _pair_mean_exact(dvn, lo) - vn * _pair_mean_exact(dvn * vn, lo))
                dv_parts.append(dgv * dgv_dv[:, sl])
                du_parts.append(dyg_p * mixed * dgu[:, sl])
            duv_ref[0, rows, :] = jnp.concatenate(du_parts + dv_parts, axis=1).astype(BF16)

        @pl.when(last)
        def _():
            sel = jnp.where(_iota((8, 128), 0) == 0, (_iota((8, 128), 1) < GROUP_DIM).astype(F32),
                            jnp.where(_iota((8, 128), 0) == 1, (_iota((8, 128), 1) >= GROUP_DIM).astype(F32), 0.0))
            for p in range(npair):
                dbs_ref[p] = lax.dot_general(sel, dbacc[p], NT, precision=lax.Precision.HIGHEST,
                                             preferred_element_type=F32)

    duv, dws, dbs, dg = pl.pallas_call(
        body, name=name, grid=(bsz, nc),
        in_specs=[pl.BlockSpec((1, tb, D_GMLP), lambda b, i: (b, i, 0)),
                  pl.BlockSpec((1, tb, D_GMLP), lambda b, i: (b, i, 1)),
                  pl.BlockSpec((1, tb, D_GMLP), lambda b, i: (b, i, dy_col)),
                  pl.BlockSpec((GROUPS, CHUNK, CHUNK), lambda b, i: (0, 0, 0)),
                  pl.BlockSpec((GROUPS, CHUNK, CHUNK), lambda b, i: (0, 0, 0)),
                  pl.BlockSpec((npair, CHUNK, 128), lambda b, i: (0, 0, 0)),
                  pl.BlockSpec((1, D_GMLP), lambda b, i: (0, 0))],
        out_specs=[pl.BlockSpec((1, tb, 2 * D_GMLP), lambda b, i: (b, i, 0)),
                   pl.BlockSpec((GROUPS, CHUNK, CHUNK), lambda b, i: (0, 0, 0)),
                   pl.BlockSpec((npair, 8, CHUNK), lambda b, i: (0, 0, 0)),
                   pl.BlockSpec((1, D_GMLP), lambda b, i: (0, 0))],
        out_shape=[jax.ShapeDtypeStruct((bsz, seq, D_IN_PAD), BF16),
                   jax.ShapeDtypeStruct((GROUPS, CHUNK, CHUNK), F32),
                   jax.ShapeDtypeStruct((npair, 8, CHUNK), F32),
                   jax.ShapeDtypeStruct((1, D_GMLP), F32)],
        scratch_shapes=[pltpu.VMEM((npair, CHUNK, 128), F32)],
        compiler_params=_cparams(),
    )(z3, z3, dyn3, ws, wst, bexp, g_out)
    return duv, dws, dbs[:, :2, :].reshape(GROUPS, CHUNK), dg


def _partner(x):
    width = x.shape[-1]
    lane = _iota(x.shape, x.ndim - 1) % HEAD_PAD
    up = pltpu.roll(x, width - ROPE // 2, x.ndim - 1)
    down = pltpu.roll(x, ROPE // 2, x.ndim - 1)
    first = jnp.logical_and(lane >= NOPE, lane < NOPE + ROPE // 2)
    second = jnp.logical_and(lane >= NOPE + ROPE // 2, lane < NOPE + ROPE)
    return jnp.where(first, up, jnp.where(second, down, 0.0))


def _mla_prep_fwd(z3, g_q, g_kv, w_uq, w_ukv, ctab, stab, *, name, tb=256):
    bsz, seq, _ = z3.shape
    tb = min(tb, seq)
    hw = HEADS * HEAD_PAD

    def body(ql_ref, kvl_ref, krl_ref, gq_ref, gkv_ref, wuq_ref, wukv_ref, c_ref, s_ref, q_ref, kv_ref, kp_ref):
        cq = _rms_fwd(ql_ref[0].astype(F32), gq_ref[...], Q_RANK).astype(BF16)
        q = _dot(cq, wuq_ref[...])
        c1, s1 = c_ref[0], s_ref[0]
        c8, s8 = jnp.tile(c1, (1, HEADS)), jnp.tile(s1, (1, HEADS))
        q_ref[0] = ((q * c8 + _partner(q) * s8) * SCALE_LOG2).astype(BF16)
        ckv = _rms_fwd(kvl_ref[0].astype(F32), gkv_ref[...], KV_RANK).astype(BF16)
        kv = _dot(ckv, wukv_ref[...])
        kv_ref[0] = kv.astype(BF16)
        kr = krl_ref[0].astype(F32)
        kr = kr * c1 + _partner(kr) * s1
        lane = _iota((tb, hw), 1) % HEAD_PAD
        kp_ref[0] = jnp.where(lane < NOPE, kv, jnp.tile(kr, (1, HEADS))).astype(BF16)

    return pl.pallas_call(
        body, name=name, grid=(bsz, seq // tb),
        in_specs=[pl.BlockSpec((1, tb, Q_RANK), lambda b, i: (b, i, 4)),
                  pl.BlockSpec((1, tb, KV_RANK), lambda b, i: (b, i, 10)),
                  pl.BlockSpec((1, tb, HEAD_PAD), lambda b, i: (b, i, 11)),
                  pl.BlockSpec((1, Q_RANK), lambda b, i: (0, 0)),
                  pl.BlockSpec((1, KV_RANK), lambda b, i: (0, 0)),
                  pl.BlockSpec((Q_RANK, hw), lambda b, i: (0, 0)),
                  pl.BlockSpec((KV_RANK, hw), lambda b, i: (0, 0)),
                  pl.BlockSpec((1, tb, HEAD_PAD), lambda b, i: (b, i, 0)),
                  pl.BlockSpec((1, tb, HEAD_PAD), lambda b, i: (b, i, 0))],
        out_specs=[pl.BlockSpec((1, tb, hw), lambda b, i: (b, i, 0))] * 3,
        out_shape=[jax.ShapeDtypeStruct((bsz, seq, hw), BF16)] * 3,
        compiler_params=_cparams(),
    )(z3, z3, z3, g_q, g_kv, w_uq, w_ukv, ctab, stab)


def _mla_prep_bwd(z3, dz3, dq3, dk3, dv3, g_q, g_kv, w_uq, w_ukv, ctab, stab, *, name, tb=256):
    bsz, seq, _ = z3.shape
    tb = min(tb, seq)
    hw = HEADS * HEAD_PAD
    nb = seq // tb

    def body(ql_ref, kvl_ref, dq_ref, dk_ref, dv_ref, gq_ref, gkv_ref, wuq_ref, wukv_ref, c_ref, s_ref, dz_in,
             dz_ref, cq_ref, dqb_ref, ckv_ref, dkvb_ref, dgq_ref, dgkv_ref):
        @pl.when(jnp.logical_and(pl.program_id(0) == 0, pl.program_id(1) == 0))
        def _():
            dgq_ref[...] = jnp.zeros_like(dgq_ref)
            dgkv_ref[...] = jnp.zeros_like(dgkv_ref)

        c1, s1 = c_ref[0], s_ref[0]
        c8, s8 = jnp.tile(c1, (1, HEADS)), jnp.tile(s1, (1, HEADS))
        dqr = dq_ref[0]
        dqb = (dqr * c8 + _partner(dqr * s8)).astype(BF16)
        dqb_ref[0] = dqb
        ql = ql_ref[0].astype(F32)
        cq_ref[0] = _rms_fwd(ql, gq_ref[...], Q_RANK).astype(BF16)
        dql, dgq = _rms_bwd(ql, gq_ref[...], _dot(dqb, wuq_ref[...], NT), Q_RANK)
        dgq_ref[...] += dgq

        dk = dk_ref[0]
        lane = _iota((tb, hw), 1) % HEAD_PAD
        dkvb = jnp.where(lane < NOPE, dk, dv_ref[0]).astype(BF16)
        dkvb_ref[0] = dkvb
        kvl = kvl_ref[0].astype(F32)
        ckv_ref[0] = _rms_fwd(kvl, gkv_ref[...], KV_RANK).astype(BF16)
        dkvl, dgkv = _rms_bwd(kvl, gkv_ref[...], _dot(dkvb, wukv_ref[...], NT), KV_RANK)
        dgkv_ref[...] += dgkv

        dkr = dk[:, 0:HEAD_PAD].astype(F32)
        for h in range(1, HEADS):
            dkr = dkr + dk[:, HEAD_PAD * h:HEAD_PAD * (h + 1)].astype(F32)
        lane1 = _iota((tb, HEAD_PAD), 1)
        dkr = jnp.where(jnp.logical_and(lane1 >= NOPE, lane1 < NOPE + ROPE), dkr, 0.0)
        dkrl = dkr * c1 + _partner(dkr * s1)
        dz_ref[0] = jnp.concatenate([dql, dkvl, dkrl], axis=1).astype(BF16)

    return pl.pallas_call(
        body, name=name, grid=(bsz, nb),
        in_specs=[pl.BlockSpec((1, tb, Q_RANK), lambda b, i: (b, i, 4)),
                  pl.BlockSpec((1, tb, KV_RANK), lambda b, i: (b, i, 10)),
                  pl.BlockSpec((1, tb, hw), lambda b, i: (b, i, 0)),
                  pl.BlockSpec((1, tb, hw), lambda b, i: (b, i, 0)),
                  pl.BlockSpec((1, tb, hw), lambda b, i: (b, i, 0)),
                  pl.BlockSpec((1, Q_RANK), lambda b, i: (0, 0)),
                  pl.BlockSpec((1, KV_RANK), lambda b, i: (0, 0)),
                  pl.BlockSpec((Q_RANK, hw), lambda b, i: (0, 0)),
                  pl.BlockSpec((KV_RANK, hw), lambda b, i: (0, 0)),
                  pl.BlockSpec((1, tb, HEAD_PAD), lambda b, i: (b, i, 0)),
                  pl.BlockSpec((1, tb, HEAD_PAD), lambda b, i: (b, i, 0)),
                  ANY_SPEC],
        out_specs=[pl.BlockSpec((1, tb, 512), lambda b, i: (b, i, 2)),
                   pl.BlockSpec((1, tb, Q_RANK), lambda b, i: (b, i, 0)),
                   pl.BlockSpec((1, tb, hw), lambda b, i: (b, i, 0)),
                   pl.BlockSpec((1, tb, KV_RANK), lambda b, i: (b, i, 0)),
                   pl.BlockSpec((1, tb, hw), lambda b, i: (b, i, 0)),
                   pl.BlockSpec((1, Q_RANK), lambda b, i: (0, 0)),
                   pl.BlockSpec((1, KV_RANK), lambda b, i: (0, 0))],
        out_shape=[jax.ShapeDtypeStruct((bsz, seq, D_IN_PAD), BF16),
                   jax.ShapeDtypeStruct((bsz, seq, Q_RANK), BF16),
                   jax.ShapeDtypeStruct((bsz, seq, hw), BF16),
                   jax.ShapeDtypeStruct((bsz, seq, KV_RANK), BF16),
                   jax.ShapeDtypeStruct((bsz, seq, hw), BF16),
                   jax.ShapeDtypeStruct((1, Q_RANK), F32),
                   jax.ShapeDtypeStruct((1, KV_RANK), F32)],
        input_output_aliases={11: 0},
        compiler_params=_cparams(),
    )(z3, z3, dq3, dk3, dv3, g_q, g_kv, w_uq, w_ukv, ctab, stab, dz3)


ATTN_HEADS_PER_STEP = 4


def _attn_specs(tq, seq, hp):
    blk = pl.BlockSpec((1, tq, hp * HEAD_PAD), lambda b, h, i: (b, i, h))
    full = pl.BlockSpec((1, seq, hp * HEAD_PAD), lambda b, h, i: (b, 0, h))
    return blk, full


def _head(h):
    return slice(HEAD_PAD * h, HEAD_PAD * (h + 1))


def _attn_fwd(q3, kv3, kp3, *, name, tq=512, hp=ATTN_HEADS_PER_STEP, side=None):
    bsz, seq, hw = q3.shape
    tq = min(tq, seq)
    blk, full = _attn_specs(tq, seq, hp)

    def body(q_ref, kv_ref, kp_ref, o_ref, lse_ref):
        i = pl.program_id(2)

        def update(state, q, kp, kv, mask=None):
            m, l, acc = state
            s = _dot(q, kp, NT)
            if mask is not None:
                s = jnp.where(mask, s, -1e30)
            m_new = jnp.maximum(m, jnp.max(s, axis=1, keepdims=True))
            alpha = jnp.exp2(m - m_new)
            p = jnp.exp2(s - m_new)
            return m_new, alpha * l + jnp.sum(p, axis=1, keepdims=True), alpha * acc + _dot(p.astype(BF16), kv)

        def step(j, carry):
            st = pl.multiple_of(j * tq, tq)
            return tuple(update(carry[h], q_ref[0, :, _head(h)], kp_ref[0, pl.ds(st, tq), _head(h)],
                                kv_ref[0, pl.ds(st, tq), _head(h)]) for h in range(hp))

        init = tuple((jnp.full((tq, 1), -1e30, F32), jnp.zeros((tq, 1), F32), jnp.zeros((tq, HEAD_PAD), F32))
                     for _ in range(hp))
        carry = lax.fori_loop(0, i, step, init)

        st = pl.multiple_of(i * tq, tq)
        is_nope = _iota((tq, HEAD_PAD), 1) < NOPE
        causal = _iota((tq, tq), 1) <= _iota((tq, tq), 0)
        for h in range(hp):
            m, l, acc = update(carry[h], q_ref[0, :, _head(h)], kp_ref[0, pl.ds(st, tq), _head(h)],
                               kv_ref[0, pl.ds(st, tq), _head(h)], causal)
            o_ref[0, :, _head(h)] = jnp.where(is_nope, 0.0, acc / l).astype(BF16)
            lse_ref[0, :, _head(h)] = jnp.broadcast_to(m + jnp.log(l) * LOG2E, (tq, HEAD_PAD))

    outs, side_outs = _hosted_call(
        body, name=name, grid=(bsz, HEADS // hp, seq // tq),
        in_specs=[blk, full, full],
        out_specs=[blk, blk],
        out_shape=[jax.ShapeDtypeStruct((bsz, seq, hw), BF16), jax.ShapeDtypeStruct((bsz, seq, hw), F32)],
        args=(q3, kv3, kp3), side=side)
    return outs if side is None else (outs, side_outs)


def _attn_bwd(q3, kv3, kp3, do3, lse3, dl3, *, name, tq=512, hp=ATTN_HEADS_PER_STEP, side=None):
    bsz, seq, hw = q3.shape
    tq = min(tq, seq)
    nq = seq // tq
    blk, full = _attn_specs(tq, seq, hp)

    def body(kv_ref, kp_ref, q_ref, do_ref, lse_ref, dl_ref, dq_ref, dk_ref, dv_ref):
        j = pl.program_id(2)

        @pl.when(j == 0)
        def _():
            dq_ref[...] = jnp.zeros_like(dq_ref)

        def pair(h, row0, nrows, nkeys, mask=None):
            row0 = pl.multiple_of(row0, nrows)
            qi = q_ref[0, pl.ds(row0, nrows), _head(h)]
            do = do_ref[0, pl.ds(row0, nrows), _head(h)]
            kp = kp_ref[0, :nkeys, _head(h)]
            s = _dot(qi, kp, NT)
            if mask is not None:
                s = jnp.where(mask, s, -1e30)
            wide = nkeys // HEAD_PAD
            p = jnp.exp2(s - jnp.tile(lse_ref[0, pl.ds(row0, nrows), _head(h)], (1, wide)))
            dv = _dot(p.astype(BF16), do, TN)
            dp = _dot(do, kv_ref[0, :nkeys, _head(h)], NT)
            ds = (p * (dp - jnp.tile(dl_ref[0, pl.ds(row0, nrows), _head(h)], (1, wide)))).astype(BF16)
            dq_ref[0, pl.ds(row0, nrows), _head(h)] += _dot(ds, kp)
            return _dot(ds, qi, TN), dv

        def step(i, carry):
            st = pl.multiple_of(i * tq, tq)
            out = []
            for h in range(hp):
                dk, dv = pair(h, st, tq, tq)
                out.append((carry[h][0] + dk, carry[h][1] + dv))
            return tuple(out)

        causal = _iota((tq, tq), 1) <= _iota((tq, tq), 0)
        carry = tuple(pair(h, pl.multiple_of(j * tq, tq), tq, tq, causal) for h in range(hp))
        carry = lax.fori_loop(j + 1, nq, step, carry)
        for h in range(hp):
            dk_ref[0, :, _head(h)] = (carry[h][0] * (1.0 / LOG2E)).astype(BF16)
            dv_ref[0, :, _head(h)] = carry[h][1].astype(BF16)

        @pl.when(j == nq - 1)
        def _():
            dq_ref[...] = dq_ref[...] * ATTN_SCALE

    outs, side_outs = _hosted_call(
        body, name=name, grid=(bsz, HEADS // hp, nq),
        in_specs=[blk, blk, full, full, full, full],
        out_specs=[full, blk, blk],
        out_shape=[jax.ShapeDtypeStruct((bsz, seq, hw), F32)] + [jax.ShapeDtypeStruct((bsz, seq, hw), BF16)] * 2,
        args=(kv3, kp3, q3, do3, lse3, dl3), side=side)
    return outs if side is None else (outs, side_outs)


def _onorm_fwd(o3, yg3, g_pad, *, name, tb=512):
    bsz, seq, hw = o3.shape
    wg = yg3.shape[-1]
    tb = min(tb, seq)

    def body(o_ref, yg_ref, g_ref, y_ref):
        ya = _rms_fwd(o_ref[0].astype(F32), g_ref[...], HEADS * 64).astype(BF16)
        y_ref[0] = jnp.concatenate([ya, yg_ref[0]], axis=1)

    return pl.pallas_call(
        body, name=name, grid=(bsz, seq // tb),
        in_specs=[pl.BlockSpec((1, tb, hw), lambda b, i: (b, i, 0)),
                  pl.BlockSpec((1, tb, wg), lambda b, i: (b, i, 0)),
                  pl.BlockSpec((1, hw), lambda b, i: (0, 0))],
        out_specs=pl.BlockSpec((1, tb, hw + wg), lambda b, i: (b, i, 0)),
        out_shape=jax.ShapeDtypeStruct((bsz, seq, hw + wg), BF16),
        compiler_params=_cparams(),
    )(o3, yg3, g_pad)


def _onorm_bwd(o3, dy3, g_pad, *, name, tb=512):
    bsz, seq, hw = o3.shape
    tb = min(tb, seq)

    def body(o_ref, dy_ref, g_ref, do_ref, dl_ref, dg_ref):
        @pl.when(jnp.logical_and(pl.program_id(0) == 0, pl.program_id(1) == 0))
        def _():
            dg_ref[...] = jnp.zeros_like(dg_ref)

        o = o_ref[0].astype(F32)
        do, dg = _rms_bwd(o, g_ref[...], dy_ref[0].astype(F32), HEADS * 64)
        dg_ref[...] += dg
        do_ref[0] = do.astype(BF16)
        prod = do * o
        parts = []
        for h in range(HEADS):
            sh = jnp.sum(prod[:, HEAD_PAD * h:HEAD_PAD * (h + 1)], axis=1, keepdims=True)
            parts.append(jnp.broadcast_to(sh, (tb, HEAD_PAD)))
        dl_ref[0] = jnp.concatenate(parts, axis=1)

    return pl.pallas_call(
        body, name=name, grid=(bsz, seq // tb),
        in_specs=[pl.BlockSpec((1, tb, hw), lambda b, i: (b, i, 0)),
                  pl.BlockSpec((1, tb, hw), lambda b, i: (b, i, 0)),
                  pl.BlockSpec((1, hw), lambda b, i: (0, 0))],
        out_specs=[pl.BlockSpec((1, tb, hw), lambda b, i: (b, i, 0)),
                   pl.BlockSpec((1, tb, hw), lambda b, i: (b, i, 0)),
                   pl.BlockSpec((1, hw), lambda b, i: (0, 0))],
        out_shape=[jax.ShapeDtypeStruct((bsz, seq, hw), BF16),
                   jax.ShapeDtypeStruct((bsz, seq, hw), F32),
                   jax.ShapeDtypeStruct((1, hw), F32)],
        compiler_params=_cparams(),
    )(o3, dy3, g_pad)


def _resnode_bwd(x3, g, *, name, target3=None, dh3=None, dres3=None, mod_nm=None, rows=None,
                 branch3=None, mod_gate=None, gate_row=None, tb=512, side=None):
    bsz, seq, d = x3.shape
    tb = min(tb, seq)
    final = target3 is not None
    has_branch = branch3 is not None
    row_spec = pl.BlockSpec((1, tb, d), lambda b, i: (b, i, 0))
    vec_spec = pl.BlockSpec((1, d), lambda b, i: (0, 0))
    mod_spec = pl.BlockSpec((1, MOD_ROWS, d), lambda b, i: (b, 0, 0))

    ins, in_specs = [x3, g], [row_spec, vec_spec]
    if final:
        ins += [target3]
        in_specs += [row_spec]
    else:
        ins += [dh3, dres3, mod_nm]
        in_specs += [row_spec, row_spec, mod_spec]
    if has_branch:
        ins += [branch3, mod_gate]
        in_specs += [row_spec, mod_spec]

    out_names = ["dx", "dg"]
    out_specs = [row_spec, vec_spec]
    out_shape = [jax.ShapeDtypeStruct((bsz, seq, d), F32), jax.ShapeDtypeStruct((1, d), F32)]
    if final:
        out_names += ["loss"]
        out_specs += [pl.BlockSpec((1, 128), lambda b, i: (0, 0))]
        out_shape += [jax.ShapeDtypeStruct((1, 128), F32)]
    else:
        out_names += ["dnm"]
        out_specs += [mod_spec]
        out_shape += [jax.ShapeDtypeStruct((bsz, MOD_ROWS, d), F32)]
    if has_branch:
        out_names += ["dbr", "dgate"]
        out_specs += [row_spec, mod_spec]
        out_shape += [jax.ShapeDtypeStruct((bsz, seq, d), BF16), jax.ShapeDtypeStruct((bsz, MOD_ROWS, d), F32)]
    n_in = len(ins)

    def body(*refs):
        r = dict(zip(["x", "g"] + (["t"] if final else ["dh", "dres", "nm"]) + (["br", "gm"] if has_branch else []),
                     refs[:n_in]))
        o = dict(zip(out_names, refs[n_in:]))
        b_first = pl.program_id(1) == 0
        first = jnp.logical_and(pl.program_id(0) == 0, b_first)
        rowid = _iota((MOD_ROWS, d), 0)

        @pl.when(first)
        def _():
            o["dg"][...] = jnp.zeros_like(o["dg"])
            if final:
                o["loss"][...] = jnp.zeros_like(o["loss"])

        @pl.when(b_first)
        def _():
            if not final:
                o["dnm"][...] = jnp.zeros_like(o["dnm"])
            if has_branch:
                o["dgate"][...] = jnp.zeros_like(o["dgate"])

        x = r["x"][0]
        gv = r["g"][...]
        if final:
            e = _rms_fwd(x, gv, d) - r["t"][0]
            sq = jnp.sum(jnp.sum(e * e, axis=1, keepdims=True), axis=0, keepdims=True)
            o["loss"][...] += jnp.broadcast_to(sq * (0.5 / d), (1, 128))
            dx, dg = _rms_bwd(x, gv, e * (1.0 / d), d)
        else:
            m = r["nm"][0]
            dh = r["dh"][0].astype(F32)
            scale = m[rows[1]:rows[1] + 1, :]
            rstd = lax.rsqrt(jnp.sum(x * x, axis=-1, keepdims=True) * (1.0 / d) + EPS)
            xh = x * rstd
            nrm = xh * gv
            dshift = jnp.sum(dh, axis=0, keepdims=True)
            dscale = jnp.sum(dh * nrm, axis=0, keepdims=True)
            o["dnm"][0] += jnp.where(rowid == 0, dshift, jnp.where(rowid == 1, dscale, 0.0))
            dn = dh * (1.0 + scale)
            dg = jnp.sum(dn * xh, axis=0, keepdims=True)
            dxh = dn * gv
            dx = rstd * (dxh - xh * (jnp.sum(dxh * xh, axis=-1, keepdims=True) * (1.0 / d))) + r["dres"][0]
        o["dg"][...] += dg
        o["dx"][0] = dx
        if has_branch:
            gate = r["gm"][0][gate_row:gate_row + 1, :]
            o["dbr"][0] = (gate * dx).astype(BF16)
            dgate = jnp.sum(dx * r["br"][0], axis=0, keepdims=True)
            o["dgate"][0] += jnp.where(rowid == 0, dgate, 0.0)

    outs, side_outs = _hosted_call(
        body, name=name, grid=(bsz, seq // tb),
        in_specs=in_specs, out_specs=out_specs, out_shape=out_shape, args=tuple(ins), side=side)
    res = dict(zip(out_names, outs))
    return res if side is None else (res, side_outs)


def _adamw(w, g, m, v, *, name):
    shape = w.shape
    cols = shape[-1]
    rows = w.size // cols
    tr = _pick_rows(rows, max(8, (256 * 1024) // cols // 8 * 8))

    def body(w_ref, g_ref, m_ref, v_ref, d_ref, nm_ref, nv_ref):
        d_ref[...], nm_ref[...], nv_ref[...] = _adamw_math(w_ref[...], g_ref[...], m_ref[...], v_ref[...])

    if w.ndim == 3 and shape[1] % 8 == 0:
        tr3 = _pick_rows(shape[1], max(8, (256 * 1024) // cols // 8 * 8))
        spec3 = pl.BlockSpec((None, tr3, cols), lambda l, i: (l, i, 0))
        return tuple(pl.pallas_call(
            body, name=name, grid=(shape[0], shape[1] // tr3),
            in_specs=[spec3] * 4, out_specs=[spec3] * 3,
            out_shape=[jax.ShapeDtypeStruct(shape, F32)] * 3,
            compiler_params=_cparams(),
        )(w, g, m, v))
    spec = pl.BlockSpec((tr, cols), lambda i: (i, 0))
    outs = pl.pallas_call(
        body, name=name, grid=(rows // tr,),
        in_specs=[spec] * 4, out_specs=[spec] * 3,
        out_shape=[jax.ShapeDtypeStruct((rows, cols), F32)] * 3,
        compiler_params=_cparams(),
    )(*[t.reshape(rows, cols) for t in (w, g, m, v)])
    return tuple(o.reshape(shape) for o in outs)


def _adamw_math(w, g, m, v):
    c1 = 1.0 - ADAM_B1 ** ADAM_STEP
    c2 = 1.0 - ADAM_B2 ** ADAM_STEP
    nm = ADAM_B1 * m + (1.0 - ADAM_B1) * g
    nv = ADAM_B2 * v + (1.0 - ADAM_B2) * (g * g)
    delta = -ADAM_LR * ((nm / c1) / (jnp.sqrt(nv / c2) + ADAM_EPS) + ADAM_WD * w)
    return delta, nm, nv


def _adamw_layers(w, m, v, bufs, row_off, *, name, tr=256):
    depth, rows, cols = w.shape
    tr = min(tr, rows)
    assert rows % tr == 0 and row_off % tr == 0

    outs = None
    for l in range(depth):
        def body(w_ref, g_ref, m_ref, v_ref, *rest):
            go_ref, d_ref, nm_ref, nv_ref = rest[-4:]
            g = g_ref[...]
            go_ref[...] = g
            d_ref[...], nm_ref[...], nv_ref[...] = _adamw_math(w_ref[...], g, m_ref[...], v_ref[...])

        layer = pl.BlockSpec((None, tr, cols), lambda i, l=l: (l, i, 0))
        prev = () if outs is None else tuple(outs)
        outs = pl.pallas_call(
            body, name=f"{name}_l{l}", grid=(rows // tr,),
            in_specs=[layer, pl.BlockSpec((tr, cols), lambda i: (row_off // tr + i, 0)), layer, layer]
            + [ANY_SPEC] * len(prev),
            out_specs=[layer] * 4,
            out_shape=[jax.ShapeDtypeStruct(w.shape, F32)] * 4,
            input_output_aliases={4 + k: k for k in range(len(prev))},
            compiler_params=_cparams(),
        )(w, bufs[l], m, v, *prev)
    return tuple(outs)


def _sum_leading(x, *, name, tr=256):
    n, rows, cols = x.shape
    tr = _pick_rows(rows, tr)

    def body(x_ref, o_ref):
        acc = x_ref[0]
        for k in range(1, n):
            acc = acc + x_ref[k]
        o_ref[...] = acc

    return pl.pallas_call(
        body, name=name, grid=(rows // tr,),
        in_specs=[pl.BlockSpec((n, tr, cols), lambda i: (0, i, 0))],
        out_specs=pl.BlockSpec((tr, cols), lambda i: (i, 0)),
        out_shape=jax.ShapeDtypeStruct((rows, cols), F32),
        compiler_params=_cparams(),
    )(x)


def _position():
    return lax.axis_index("x"), lax.axis_index("y"), lax.axis_index("c")


def _allgather8(x, *, name):
    shape = x.shape

    def body(x_ref, out_ref, send_sems, recv_sems, local_sem):
        px, py, pc = _position()
        me, sibling = (px, py, pc), (px, py, 1 - pc)
        chips = [(1 - px, py), (px, 1 - py), (1 - px, 1 - py)]
        src_own = x_ref

        def slot(qx, qy, qc):
            return out_ref.at[4 * qx + 2 * qy + qc]

        def copy(k, block, to, src=None):
            return pltpu.make_async_remote_copy(
                src_ref=slot(*block) if src is None else src, dst_ref=slot(*block),
                send_sem=send_sems.at[k], recv_sem=recv_sems.at[k], device_id=to, device_id_type=MESH)

        mine = pltpu.make_async_copy(src_own, slot(*me), local_sem)
        mine.start()
        first = [copy(0, me, sibling, src=src_own)]
        first += [copy(1 + j, me, (*chip, pc), src=src_own) for j, chip in enumerate(chips)]
        for cp in first:
            cp.start()
        passed = [copy(4 + j, (*chip, pc), sibling) for j, chip in enumerate(chips)]
        for j, chip in enumerate(chips):
            copy(1 + j, (*chip, pc), me).wait_recv()
            passed[j].start()
        copy(0, sibling, me).wait_recv()
        for j, chip in enumerate(chips):
            copy(4 + j, (*chip, 1 - pc), me).wait_recv()
        for cp in first + passed:
            cp.wait_send()
        mine.wait()

    return pl.pallas_call(
        body, name=name,
        out_shape=jax.ShapeDtypeStruct((N_DEV,) + shape, x.dtype),
        in_specs=[pl.BlockSpec(memory_space=pl.ANY)],
        out_specs=pl.BlockSpec(memory_space=pl.ANY),
        scratch_shapes=[pltpu.SemaphoreType.DMA((7,)), pltpu.SemaphoreType.DMA((7,)), pltpu.SemaphoreType.DMA],
    )(x)


class _Exchange:
    def __init__(self, ins, out_shapes, n, build, aliases=None):
        self.ins, self.out_shapes, self.n, self.build = tuple(ins), tuple(out_shapes), n, build
        self.aliases = dict(aliases or {})

    def _descriptors(self, in_refs, out_refs, send_sems, recv_sems):
        sends, recvs = [], []
        for k, (src, dst, peer, landing) in enumerate(self.build(in_refs, out_refs)):
            sends.append(pltpu.make_async_remote_copy(
                src_ref=src, dst_ref=dst, send_sem=send_sems.at[k], recv_sem=recv_sems.at[k],
                device_id=peer, device_id_type=MESH))
            recvs.append(pltpu.make_async_remote_copy(
                src_ref=src, dst_ref=landing, send_sem=send_sems.at[k], recv_sem=recv_sems.at[k],
                device_id=peer, device_id_type=MESH))
        return sends, recvs

    def start(self, *refs):
        for cp in self._descriptors(*refs)[0]:
            cp.start()

    def finish(self, *refs):
        sends, recvs = self._descriptors(*refs)
        for cp in recvs:
            cp.wait_recv()
        for cp in sends:
            cp.wait_send()


ANY_SPEC = pl.BlockSpec(memory_space=pl.ANY)


def _hosted_call(body, *, name, grid, in_specs, out_specs, out_shape, args, scratch_shapes=(), side=None,
                 num_scalar_prefetch=0, io_aliases=None):
    in_specs, out_specs, out_shape = list(in_specs), list(out_specs), list(out_shape)
    n_in, n_out = len(in_specs) + num_scalar_prefetch, len(out_specs)
    kernel_body = body
    aliases = dict(io_aliases or {})
    if side is not None:
        s_in, s_out = len(side.ins), len(side.out_shapes)
        aliases.update({n_in + i: n_out + o for i, o in side.aliases.items()})

        def kernel_body(*refs):
            ins, s_ins = refs[:n_in], refs[n_in:n_in + s_in]
            outs = refs[n_in + s_in:n_in + s_in + n_out]
            s_outs = refs[n_in + s_in + n_out:n_in + s_in + n_out + s_out]
            scratch, sems = refs[n_in + s_in + n_out + s_out:-2], refs[-2:]
            first = functools.reduce(jnp.logical_and, [pl.program_id(a) == 0 for a in range(len(grid))])
            last = functools.reduce(jnp.logical_and, [pl.program_id(a) == g - 1 for a, g in enumerate(grid)])

            @pl.when(first)
            def _():
                side.start(s_ins, s_outs, *sems)

            body(*ins, *outs, *scratch)

            @pl.when(last)
            def _():
                side.finish(s_ins, s_outs, *sems)

        in_specs += [ANY_SPEC] * s_in
        out_specs += [ANY_SPEC] * s_out
        out_shape += list(side.out_shapes)
        scratch_shapes = list(scratch_shapes) + [pltpu.SemaphoreType.DMA((side.n,)),
                                                 pltpu.SemaphoreType.DMA((side.n,))]
        args = tuple(args) + side.ins
    if num_scalar_prefetch:
        grid_spec = pltpu.PrefetchScalarGridSpec(num_scalar_prefetch=num_scalar_prefetch, grid=grid,
                                                 in_specs=in_specs, out_specs=out_specs,
                                                 scratch_shapes=list(scratch_shapes))
        outs = pl.pallas_call(kernel_body, name=name, grid_spec=grid_spec, out_shape=out_shape,
                              input_output_aliases=aliases, compiler_params=_cparams())(*args)
    else:
        outs = pl.pallas_call(kernel_body, name=name, grid=grid, in_specs=in_specs, out_specs=out_specs,
                              out_shape=out_shape, scratch_shapes=list(scratch_shapes),
                              input_output_aliases=aliases, compiler_params=_cparams())(*args)
    return tuple(outs[:n_out]), tuple(outs[n_out:])


def _run_exchange(ex, *, name):
    s_in = len(ex.ins)

    def body(*refs):
        ins, outs, sems = refs[:s_in], refs[s_in:-2], refs[-2:]
        ex.start(ins, outs, *sems)
        ex.finish(ins, outs, *sems)

    outs = pl.pallas_call(
        body, name=name, out_shape=list(ex.out_shapes),
        in_specs=[ANY_SPEC] * s_in, out_specs=[ANY_SPEC] * len(ex.out_shapes),
        scratch_shapes=[pltpu.SemaphoreType.DMA((ex.n,)), pltpu.SemaphoreType.DMA((ex.n,))],
        input_output_aliases=ex.aliases,
    )(*ex.ins)
    return tuple(outs)


def _both(a, b):
    na, oa = len(a.ins), len(a.out_shapes)

    def build(ins, outs):
        return a.build(ins[:na], outs[:oa]) + b.build(ins[na:], outs[oa:])

    aliases = dict(a.aliases)
    aliases.update({na + i: oa + o for i, o in b.aliases.items()})
    return _Exchange(a.ins + b.ins, a.out_shapes + b.out_shapes, a.n + b.n, build, aliases)


def _other_chips(px, py):
    return [(px, 1 - py), (1 - px, py), (1 - px, 1 - py)]


def _gather_spread(w_flat, halves=True):
    rows, w = w_flat.shape
    hr = rows // 2 if halves else rows

    def build(ins, outs):
        px, py, pc = _position()
        mine = ins[0].at[pl.ds(pc * hr, hr)] if halves else ins[0]
        me = 4 * px + 2 * py + pc
        plan = [((px, py, 1 - pc), me ^ 1)]
        plan += [((qx, qy, pc), 4 * qx + 2 * qy + pc) for qx, qy in _other_chips(px, py)]
        return [(mine, outs[0].at[me], peer, outs[0].at[their]) for peer, their in plan]

    return _Exchange([w_flat], [jax.ShapeDtypeStruct((N_DEV, hr, w), w_flat.dtype)], 4, build)


def _gather_pass_on(gath):
    def build(ins, outs):
        px, py, pc = _position()
        out = []
        for qx, qy in _other_chips(px, py):
            blk = 4 * qx + 2 * qy + pc
            out.append((outs[0].at[blk], outs[0].at[blk], (px, py, 1 - pc), outs[0].at[blk ^ 1]))
        return out

    return _Exchange([gath], [jax.ShapeDtypeStruct(gath.shape, gath.dtype)], 3, build, aliases={0: 0})


def _rs_halves(g):
    n, rows, w = g.shape
    hr = rows // 2

    def build(ins, outs):
        px, py, pc = _position()
        return [(ins[0].at[:, pl.ds((1 - pc) * hr, hr), :], outs[0], (px, py, 1 - pc), outs[0])]

    return _Exchange([g], [jax.ShapeDtypeStruct((n, hr, w), g.dtype)], 1, build)


def _rs_chips(sb):
    def build(ins, outs):
        px, py, pc = _position()
        return [(ins[0].at[j], outs[0].at[j], (qx, qy, pc), outs[0].at[j])
                for j, (qx, qy) in enumerate(_other_chips(px, py))]

    return _Exchange([sb], [jax.ShapeDtypeStruct(sb.shape, sb.dtype)], 3, build)


def _rs_complete(buf):
    def build(ins, outs):
        px, py, pc = _position()
        return [(outs[0].at[pc], outs[0].at[pc], (px, py, 1 - pc), outs[0].at[1 - pc])]

    return _Exchange([buf], [jax.ShapeDtypeStruct(buf.shape, buf.dtype)], 1, build, aliases={0: 0})


def _rs_partial(g, recv, ids, *, name, tr=128):
    _, rows, w = g.shape
    hr = rows // 2
    nb = hr // tr

    def body(ids_ref, g_ref, r_ref, o_ref):
        o_ref[0] = (g_ref[0] + r_ref[0]).astype(BF16)

    grid_spec = pltpu.PrefetchScalarGridSpec(
        num_scalar_prefetch=1, grid=(3, nb),
        in_specs=[pl.BlockSpec((1, tr, w), lambda j, i, ids: (ids[1] ^ (j + 1), ids[0] * nb + i, 0)),
                  pl.BlockSpec((1, tr, w), lambda j, i, ids: (ids[1] ^ (j + 1), i, 0))],
        out_specs=pl.BlockSpec((1, tr, w), lambda j, i, ids: (j, i, 0)))
    return pl.pallas_call(
        body, name=name, grid_spec=grid_spec,
        out_shape=jax.ShapeDtypeStruct((3, hr, w), BF16),
        compiler_params=_cparams(),
    )(ids, g, recv)


def _rs_total(g, recv, got, ids, *, name, tr=128):
    _, rows, w = g.shape
    hr = rows // 2
    nb = hr // tr

    def body(ids_ref, g_ref, r_ref, got_ref, o_ref):
        acc = g_ref[0] + r_ref[0]
        for j in range(3):
            acc = acc + got_ref[j].astype(F32)
        o_ref[0] = acc

    grid_spec = pltpu.PrefetchScalarGridSpec(
        num_scalar_prefetch=1, grid=(nb,),
        in_specs=[pl.BlockSpec((1, tr, w), lambda i, ids: (ids[1], ids[0] * nb + i, 0)),
                  pl.BlockSpec((1, tr, w), lambda i, ids: (ids[1], i, 0)),
                  pl.BlockSpec((3, tr, w), lambda i, ids: (0, i, 0))],
        out_specs=pl.BlockSpec((1, tr, w), lambda i, ids: (ids[0], i, 0)))
    return pl.pallas_call(
        body, name=name, grid_spec=grid_spec,
        out_shape=jax.ShapeDtypeStruct((2, hr, w), F32),
        compiler_params=_cparams(),
    )(ids, g, recv, got)


class _ReduceScatter:
    def __init__(self, g, ids, tag):
        self.g, self.ids, self.tag, self.stage, self.result = g, ids, tag, 0, None

    def next_exchange(self):
        if self.stage == 0:
            return _rs_halves(self.g)
        if self.stage == 1:
            return _rs_chips(self.sb)
        return _rs_complete(self.buf)

    def done(self, outs):
        if self.stage == 0:
            self.recv = outs[0]
            hr = self.recv.shape[1]
            self.tr = max(t for t in range(16, 513, 16) if hr % t == 0)
            self.sb = _rs_partial(self.g, self.recv, self.ids, name=f"{self.tag}_partial", tr=self.tr)
        elif self.stage == 1:
            self.buf = _rs_total(self.g, self.recv, outs[0], self.ids, name=f"{self.tag}_total", tr=self.tr)
        else:
            _, hr, w = outs[0].shape
            self.result = outs[0].reshape(2 * hr, w)
        self.stage += 1

    def finish_alone(self):
        names = ("halves", "chips", "complete")
        while self.stage < 3:
            self.done(_run_exchange(self.next_exchange(), name=f"{self.tag}_{names[self.stage]}"))
        return self.result


def _flat_rows():
    used = sum(r for _, r in FSDP_SECTIONS)
    return used, -(-used // ROW_ALIGN) * ROW_ALIGN


def _cols_to_chunks(full):
    rows, cols = full.shape
    t = full.reshape(rows, N_CHIPS, cols // N_CHIPS).transpose(1, 0, 2)
    return t.reshape(N_CHIPS, -1, FLAT_W)


def _chunks_to_cols(chunks, rows, cols):
    return chunks.reshape(N_CHIPS, rows, cols // N_CHIPS).transpose(1, 0, 2).reshape(rows, cols)


def _pad_heads(w, real):
    lead = w.shape[:-1]
    t = w.reshape(lead + (HEADS, real))
    t = jnp.pad(t, [(0, 0)] * len(lead) + [(0, 0), (0, HEAD_PAD - real)])
    return t.reshape(lead + (HEADS * HEAD_PAD,))


def _unpad_heads(w, real):
    lead = w.shape[:-1]
    return w.reshape(lead + (HEADS, HEAD_PAD))[..., :real].reshape(lead + (HEADS * real,))


def _pad_value_lanes(w, axis):
    w = jnp.moveaxis(w, axis, -1)
    lead = w.shape[:-1]
    t = w.reshape(lead + (HEADS, 64))
    t = jnp.pad(t, [(0, 0)] * len(lead) + [(0, 0), (HEAD_PAD - 64, 0)])
    return jnp.moveaxis(t.reshape(lead + (HEADS * HEAD_PAD,)), -1, axis)


def _unpad_value_lanes(w, axis):
    w = jnp.moveaxis(w, axis, -1)
    lead = w.shape[:-1]
    t = w.reshape(lead + (HEADS, HEAD_PAD))[..., HEAD_PAD - 64:]
    return jnp.moveaxis(t.reshape(lead + (HEADS * 64,)), -1, axis)


def _pad_w_in_t(wt):
    z = jnp.zeros((NOPE, wt.shape[1]), wt.dtype)
    z2 = jnp.zeros((HEAD_PAD - NOPE - ROPE, wt.shape[1]), wt.dtype)
    return jnp.concatenate([wt[:1408], z, wt[1408:], z2], axis=0)


def _unpad_w_in_t(wt):
    return jnp.concatenate([wt[:1408], wt[1408 + NOPE:1408 + NOPE + ROPE]], axis=0)


def _rope_tables(positions):
    freqs = ROPE_THETA ** (-jnp.arange(0, ROPE, 2, dtype=F32) / ROPE)
    ang = positions.astype(F32)[..., None] * freqs
    cos, sin = jnp.cos(ang), jnp.sin(ang)
    lead = cos.shape[:-1]
    ones = jnp.ones(lead + (NOPE,), F32)
    zeros_n = jnp.zeros(lead + (NOPE,), F32)
    zeros_p = jnp.zeros(lead + (HEAD_PAD - NOPE - ROPE,), F32)
    ctab = jnp.concatenate([ones, cos, cos, zeros_p], axis=-1)
    stab = jnp.concatenate([zeros_n, -sin, sin, zeros_p], axis=-1)
    return ctab, stab


def _mix_weights(full):
    return dict(
        w_in_t=_pad_w_in_t(full["w_in_t"]),
        w_uq=_pad_heads(full["mla_w_uq"], NOPE + ROPE),
        w_ukv=full["mla_w_ukv"],
        w_out=jnp.concatenate([_pad_value_lanes(full["w_out"][D_GMLP:], 0), full["w_out"][:D_GMLP]], axis=0),
    )


def _small_weights(p, l):
    ws = p["gmlp_ws"][l]
    tril = jnp.tril(jnp.ones((CHUNK, CHUNK), bool))
    bs = p["gmlp_bs"][l]
    bexp = jnp.repeat(bs.reshape(GROUPS // 2, 2, CHUNK).transpose(0, 2, 1), GROUP_DIM, axis=2)
    return dict(
        ws=ws,
        wst=jnp.where(tril[None], ws, 0.0).transpose(0, 2, 1).astype(BF16),
        bexp=bexp,
        g_mix=p["norm_mix_g"][l][None],
        g_ffn=p["norm_ffn_g"][l][None],
        g_q=p["mla_q_norm_g"][l][None],
        g_kv=p["mla_kv_norm_g"][l][None],
        g_og=p["out_norm_gmlp_g"][l][None],
        g_oa=_pad_value_lanes(p["out_norm_mla_g"][l], 0)[None],
    )


def _local_step(x3, target3, positions, mods, final_g, plan):
    bsz, seq, d = x3.shape
    tok = bsz * seq
    tmt = min(512, seq)
    tmk = min(1024, seq)
    tmw = min(2048, tok)
    chunk = (None, None, FLAT_W, FLAT_W)
    chunk2 = (2, None, FLAT_W, FLAT_W)
    ff_grad_shape = (N_CHIPS, 2 * FLAT_W, FLAT_W)
    ctab, stab = _rope_tables(positions)
    lw = [None] * DEPTH

    def flat(t):
        return t.reshape(tok, t.shape[-1])

    def cube(t):
        return t.reshape(bsz, seq, t.shape[-1])

    def carrying(l, tag, fn, *args, **kw):
        side = plan.host(l, tag)
        if side is None:
            return fn(*args, **kw)
        res, side_outs = fn(*args, side=side, **kw)
        plan.hosted(l, tag, side_outs)
        return res

    saved = []
    x = x3
    for l in range(DEPTH):
        lw[l] = plan.layer(l)
        w, mod = lw[l], mods[l]
        if l == 0:
            h1 = carrying(l, "fwd_normmod1", _normmod_fwd, x, w["g_mix"], mod, SHIFT1, SCALE1,
                          name=f"l{l}_normmod1")
        else:
            h1 = h1_next
        z = cube(_mm(flat(h1), w["w_in_t"], dims="nt", name=f"l{l}_w_in", tm=tmk, tn=D_IN_PAD, tk=d,
                     out_dtypes=(BF16,)))
        yg = _gmlp_fwd(z, w["ws"], w["bexp"], w["g_og"], name=f"l{l}_gmlp_fwd")
        q, kv, kp = _mla_prep_fwd(z, w["g_q"], w["g_kv"], w["w_uq"], w["w_ukv"], ctab, stab, name=f"l{l}_mla_prep")
        o, lse = carrying(l, "fwd_attn", _attn_fwd, q, kv, kp, name=f"l{l}_attn_fwd")
        y = _onorm_fwd(o, yg, w["g_oa"], name=f"l{l}_onorm_fwd")

        def normmod(xv, gv, gm, shift_row, scale_row):
            m = gm[0]
            return _rms_fwd(xv, gv, d) * (1.0 + m[scale_row:scale_row + 1, :]) + m[shift_row:shift_row + 1, :]

        def out_epi(po, xv, gm, gf):
            x_new = xv + gm[0][GATE1:GATE1 + 1, :] * po
            return po, x_new, normmod(x_new, gf, gm, SHIFT2, SCALE2)

        vec_spec = pl.BlockSpec((1, d), lambda i, j, k: (0, j))
        po, x_mid, h2 = carrying(l, "fwd_out_a", _mm, flat(y), w["w_out"], dims="nn", name=f"l{l}_w_out",
                                 tm=tmt, tn=d, tk=y.shape[-1], out_dtypes=(BF16, F32, BF16), epilogue=out_epi,
                                 extras=(flat(x), mod, w["g_ffn"]),
                                 extra_specs=(None, _mod_spec(tmt, d, seq), vec_spec))
        x_mid, h2 = cube(x_mid), cube(h2)

        def act_epi(acc):
            r = jnp.maximum(acc, 0.0)
            return (r * r,)

        r = carrying(l, "fwd_ff1", _mm, flat(h2), w["ff"], dims="nn", name=f"l{l}_w_ff1", tm=tmw, tn=FLAT_W,
                     tk=d, out_dtypes=(BF16,), epilogue=act_epi, weights_outer=True, n=D_FF,
                     b_block=(chunk, lambda i, j, k: (j, 0, 0, 0)))

        more = l + 1 < DEPTH

        def ff2_epi(acc, xv, gm, *nxt):
            x_new = xv + gm[0][GATE2:GATE2 + 1, :] * acc
            return (acc, x_new) + ((normmod(x_new, nxt[1], nxt[0], SHIFT1, SCALE1),) if more else ())

        mod_spec = _mod_spec(tmt, d, seq)
        outs = carrying(l, "fwd_ff2", _mm, r, w["ff"], dims="nn", name=f"l{l}_w_ff2", tm=tmt, tn=d, tk=2 * FLAT_W,
                        out_dtypes=(BF16, F32) + ((BF16,) if more else ()), epilogue=ff2_epi,
                        extras=(flat(x_mid), mod) + ((mods[l + 1], plan.layer(l + 1)["g_mix"]) if more else ()),
                        extra_specs=(None, mod_spec) + ((mod_spec, vec_spec) if more else ()), n=d,
                        b_block=(chunk2, lambda i, j, k: (k, 1, 0, 0)))
        f, x_out = outs[0], outs[1]
        h1_next = cube(outs[2]) if more else None
        saved.append(dict(x_in=x, h1=h1, z=z, q=q, kv=kv, kp=kp, o=o, lse=lse, y=y, po=cube(po),
                          x_mid=x_mid, h2=h2, r=r, f=cube(f)))
        x = cube(x_out)

    grads = [dict() for _ in range(DEPTH)]
    dmods = [None] * DEPTH
    top = DEPTH - 1
    node = _resnode_bwd(x, final_g[None], name="final_loss_bwd", target3=target3,
                        branch3=saved[top]["f"], mod_gate=mods[top], gate_row=GATE2)
    loss_part = node["loss"][0, 0]
    d_final_g = node["dg"][0]
    plan.scalars(loss_part, d_final_g)
    for l in range(DEPTH - 1, -1, -1):
        w, mod, s = lw[l], mods[l], saved[l]
        dx_out, dfb, dgate2 = node["dx"], flat(node["dbr"]), node["dgate"][:, 0]

        def dact_epi(acc, rv):
            return (acc * (2.0 * jnp.sqrt(rv.astype(F32))),)

        da = carrying(l, "bwd_d_r", _mm, dfb, w["ff"], dims="nt", name=f"l{l}_d_r", tm=tmw, tn=FLAT_W, tk=d,
                      out_dtypes=(BF16,), epilogue=dact_epi, extras=(s["r"],), weights_outer=True, n=D_FF,
                      b_block=(chunk, lambda i, j, k: (j, 1, 0, 0)))
        g_ff = carrying(l, "bwd_dw_ff2", _mm, s["r"], dfb, dims="tn", name=f"l{l}_dw_ff2", tm=FLAT_W, tn=d,
                        tk=2048, out_into=(ff_grad_shape, (None, FLAT_W, FLAT_W), lambda i, j, k: (i, 1, 0), None))
        g_ff = carrying(l, "bwd_dw_ff1", _mm, flat(s["h2"]), da, dims="tn", name=f"l{l}_dw_ff1", tm=d, tn=FLAT_W,
                        tk=2048, out_into=(ff_grad_shape, (None, FLAT_W, FLAT_W), lambda i, j, k: (j, 0, 0), g_ff))
        plan.ff_grads(l, g_ff)
        dh2 = carrying(l, "bwd_d_h2", _mm, da, w["ff"], dims="nt", name=f"l{l}_d_h2", tm=tmk, tn=d, tk=2 * FLAT_W,
                       n=d, b_block=(chunk2, lambda i, j, k: (k, 0, 0, 0)), out_dtypes=(BF16,))
        node = carrying(l, "bwd_resnode_ffn", _resnode_bwd, s["x_mid"], w["g_ffn"], name=f"l{l}_resnode_ffn",
                        dh3=cube(dh2), dres3=dx_out, mod_nm=mod, rows=(SHIFT2, SCALE2), branch3=s["po"],
                        mod_gate=mod, gate_row=GATE1)
        grads[l]["norm_ffn_g"] = node["dg"][0]
        dshift2, dscale2 = node["dnm"][:, 0], node["dnm"][:, 1]
        dx_mid, dpo, dgate1 = node["dx"], flat(node["dbr"]), node["dgate"][:, 0]

        wy = s["y"].shape[-1]
        dy = cube(carrying(l, "bwd_d_y", _mm, dpo, w["w_out"], dims="nt", name=f"l{l}_d_y", tm=tmk, tn=wy, tk=d,
                           out_dtypes=(BF16,)))
        dw_out = _mm(flat(s["y"]), dpo, dims="tn", name=f"l{l}_dw_out", tm=wy // 3, tn=d, tk=2048)
        hw = HEADS * HEAD_PAD
        grads[l]["w_out"] = jnp.concatenate([dw_out[hw:], _unpad_value_lanes(dw_out[:hw], 0)], axis=0)

        dz, dws, dbs, dg_og = _gmlp_bwd(s["z"], dy, w["ws"], w["wst"], w["bexp"], w["g_og"],
                                        name=f"l{l}_gmlp_bwd", dy_col=hw // D_GMLP)
        grads[l]["gmlp_ws"], grads[l]["gmlp_bs"], grads[l]["out_norm_gmlp_g"] = dws, dbs, dg_og[0]

        do, dl, dg_oa = _onorm_bwd(s["o"], dy, w["g_oa"], name=f"l{l}_onorm_bwd")
        grads[l]["out_norm_mla_g"] = _unpad_value_lanes(dg_oa[0], 0)
        plan.small_ready(l, grads[l])
        dq, dk, dv = carrying(l, "bwd_attn_dkv", _attn_bwd, s["q"], s["kv"], s["kp"], do, s["lse"], dl,
                              name=f"l{l}_attn_bwd")
        dz, cq, dqb, ckv, dkvb, dg_q, dg_kv = _mla_prep_bwd(
            s["z"], dz, dq, dk, dv, w["g_q"], w["g_kv"], w["w_uq"], w["w_ukv"], ctab, stab,
            name=f"l{l}_mla_prep_bwd")
        grads[l]["mla_q_norm_g"], grads[l]["mla_kv_norm_g"] = dg_q[0], dg_kv[0]
        dw_uq = carrying(l, "bwd_dw_uq", _mm, flat(cq), flat(dqb), dims="tn", name=f"l{l}_dw_uq", tm=Q_RANK,
                         tn=1024, tk=4096)
        grads[l]["mla_w_uq"] = _unpad_heads(dw_uq, NOPE + ROPE)
        grads[l]["w_in_t"] = _unpad_w_in_t(carrying(l, "bwd_dw_in", _mm, flat(dz), flat(s["h1"]), dims="tn",
                                                    name=f"l{l}_dw_in", tm=D_IN_PAD // 2, tn=d, tk=2048))
        grads[l]["mla_w_ukv"] = carrying(l, "bwd_dw_ukv", _mm, flat(ckv), flat(dkvb), dims="tn", name=f"l{l}_dw_ukv",
                                         tm=KV_RANK, tn=1024, tk=4096)
        plan.layer_grads(l, grads[l])
        dh1 = carrying(l, "bwd_d_h1", _mm, flat(dz), w["w_in_t"], dims="nn", name=f"l{l}_d_h1", tm=tmk, tn=d,
                       tk=D_IN_PAD, out_dtypes=(BF16,))
        below = dict(branch3=saved[l - 1]["f"], mod_gate=mods[l - 1], gate_row=GATE2) if l > 0 else {}
        node = carrying(l, "bwd_resnode_mix", _resnode_bwd, s["x_in"], w["g_mix"], name=f"l{l}_resnode_mix",
                        dh3=cube(dh1), dres3=dx_mid, mod_nm=mod, rows=(SHIFT1, SCALE1), **below)
        grads[l]["norm_mix_g"] = node["dg"][0]
        dshift1, dscale1 = node["dnm"][:, 0], node["dnm"][:, 1]
        dmods[l] = jnp.stack([dshift1, dscale1, dgate1, dshift2, dscale2, dgate2], axis=1)
    return node["dx"], dmods


W_NAMES = ("w_ada", "b_ada", "norm_mix_g", "w_in", "gmlp_ws", "gmlp_bs", "mla_q_norm_g", "mla_kv_norm_g",
           "mla_w_uq", "mla_w_ukv", "out_norm_gmlp_g", "out_norm_mla_g", "w_out", "norm_ffn_g", "w_ff1", "w_ff2",
           "final_norm_g")
FLAT_KEY = {"w_in": "w_in", "w_uq": "mla_w_uq", "w_ukv": "mla_w_ukv", "w_out": "w_out", "w_ff1": "w_ff1",
            "w_ff2": "w_ff2"}
COL_SHARDED = ("w_in", "w_uq", "w_ukv", "w_ff1")
FULL_SHAPE = {"w_in": (D_MODEL, D_IN), "w_uq": (Q_RANK, HEADS * (NOPE + ROPE)), "w_ukv": (KV_RANK, HEADS * 128),
              "w_out": (D_MODEL, D_MODEL)}
SMALL_LAYER_NAMES = ("gmlp_ws", "gmlp_bs", "out_norm_gmlp_g", "out_norm_mla_g", "norm_ffn_g")
LATE_SMALL_NAMES = ("norm_mix_g", "mla_q_norm_g", "mla_kv_norm_g")


def _silu(v):
    return v * (1.0 / (1.0 + jnp.exp(-v)))


class _CommPlan:
    FWD = {"fwd_attn": ("ff", 0, "spread"), "fwd_out_a": ("ff", 0, "pass"),
           "fwd_ff1": ("mix", 1, "spread"), "fwd_ff2": ("mix", 1, "pass")}
    BWD = {"bwd_d_r": ("mix", 1), "bwd_dw_ff2": ("mix", 1),
           "bwd_d_h2": ("ff", 0), "bwd_attn_dkv": ("ff", 0), "bwd_dw_uq": ("ff", 0)}
    BWD_ALSO = {"bwd_d_h2": ("mix", 1)}
    BWD_LAST = {"bwd_d_h1": ("mix", 0), "bwd_resnode_mix": ("mix", 0)}
    SMALL = {"bwd_attn_dkv": "spread", "bwd_dw_uq": "pass"}

    def __init__(self, weights, ids, dev, core):
        self.weights, self.ids, self.dev, self.core = weights, ids, dev, core
        self.used, self.rows = _flat_rows()
        self.flat = {("mix", l): self._flat_mix(l) for l in range(DEPTH)}
        self.flat.update({("ff", l): jnp.concatenate([weights["w_ff1"][l], weights["w_ff2"][l]], axis=0).astype(BF16)
                          for l in range(DEPTH)})
        self.lw, self.rs, self.grads, self.spread = {}, {}, {}, {}
        self.small_vec, self.small_sum, self.small_spread, self.extra = {}, {}, None, {}
        self.lw = {l: _small_weights(weights, l) for l in range(DEPTH)}

    def _flat_mix(self, l):
        pieces = []
        for nm, _ in FSDP_SECTIONS:
            shard = self.weights[FLAT_KEY[nm]][l]
            pieces.append(shard.T if nm == "w_in" else shard.reshape(-1, FLAT_W))
        pieces.append(jnp.zeros((self.rows - self.used, FLAT_W), F32))
        return jnp.concatenate(pieces, axis=0).astype(BF16)

    def _arrived(self, group, l, gath):
        flat = self.flat[group, l]
        hr = flat.shape[0] // 2
        mine = lax.dynamic_slice(flat, (self.core * hr, 0), (hr, FLAT_W))
        gath = lax.dynamic_update_slice(gath, mine[None], (self.dev, 0, 0))
        if group == "ff":
            self.lw[l]["ff"] = gath.reshape(N_CHIPS, 2, hr, FLAT_W)
            return
        w_gath = gath.reshape(N_CHIPS, self.rows, FLAT_W)
        full, off = {}, 0
        for nm, nrows in FSDP_SECTIONS:
            sec = w_gath[:, off:off + nrows]
            off += nrows
            rows, cols = FULL_SHAPE[nm]
            if nm == "w_in":
                full["w_in_t"] = sec.reshape(cols, rows)
            else:
                full[FLAT_KEY[nm]] = (_chunks_to_cols(sec, rows, cols) if nm in COL_SHARDED
                                      else sec.reshape(rows, cols))
        self.lw[l].update(_mix_weights(full))

    def layer(self, l):
        return self.lw[l]

    def host(self, l, tag):
        if tag == "fwd_normmod1":
            return _gather_spread(self.flat["mix", 0]) if l == 0 else None
        if tag in self.FWD:
            group, ahead, what = self.FWD[tag]
            if l + ahead >= DEPTH:
                return None
            return _gather_spread(self.flat[group, l + ahead]) if what == "spread" else _gather_pass_on(self.spread[group])
        ex = None
        for rs in self._rs_for(l, tag):
            ex = rs.next_exchange() if ex is None else _both(ex, rs.next_exchange())
        if tag in self.SMALL:
            small = (_gather_spread(self.small_vec[l], halves=False) if self.SMALL[tag] == "spread"
                     else _gather_pass_on(self.small_spread))
            ex = small if ex is None else _both(ex, small)
        return ex

    def _rs_for(self, l, tag):
        found = []
        if tag in self.BWD_LAST and l == 0:
            found.append(self.rs.get(self.BWD_LAST[tag]))
        for table in (self.BWD, self.BWD_ALSO):
            if tag in table:
                group, ahead = table[tag]
                found.append(self.rs.get((group, l + ahead)))
        return [rs for rs in found if rs is not None and rs.stage <= 2]

    def hosted(self, l, tag, outs):
        if tag == "fwd_normmod1":
            self._arrived("mix", 0, _run_exchange(_gather_pass_on(outs[0]), name="l0_mix_gather_pass_on")[0])
        elif tag in self.FWD:
            group, ahead, what = self.FWD[tag]
            if what == "spread":
                self.spread[group] = outs[0]
            else:
                self._arrived(group, l + ahead, outs[0])
        else:
            for rs in self._rs_for(l, tag):
                rs.done(outs[:1])
                outs = outs[1:]
            if tag in self.SMALL:
                if self.SMALL[tag] == "spread":
                    self.small_spread = outs[0]
                else:
                    self._small_arrived(l, outs[0])

    def ff_grads(self, l, g_ff):
        self.rs["ff", l] = _ReduceScatter(g_ff, self.ids, f"l{l}_ff_rs")

    def layer_grads(self, l, grads):
        self.grads[l] = grads
        pieces = []
        for nm, nrows in FSDP_SECTIONS:
            if nm == "w_in":
                pieces.append(grads["w_in_t"].reshape(N_CHIPS, nrows, FLAT_W))
                continue
            g = grads[FLAT_KEY[nm]]
            pieces.append(_cols_to_chunks(g) if nm in COL_SHARDED else g.reshape(N_CHIPS, nrows, FLAT_W))
        pieces.append(jnp.zeros((N_CHIPS, self.rows - self.used, FLAT_W), F32))
        self.rs["mix", l] = _ReduceScatter(jnp.concatenate(pieces, axis=1), self.ids, f"l{l}_mix_rs")

    def scalars(self, loss_part, d_final_g):
        self.extra = {0: [loss_part[None]]}
        self.extra.setdefault(DEPTH - 1, []).insert(0, d_final_g)

    def small_ready(self, l, grads):
        parts = [grads[nm].reshape(-1) for nm in SMALL_LAYER_NAMES] + self.extra.get(l, [])
        vec = jnp.concatenate(parts)
        rows = -(-vec.shape[0] // (8 * FLAT_W)) * 8
        self.small_vec[l] = jnp.pad(vec, (0, rows * FLAT_W - vec.shape[0])).reshape(rows, FLAT_W)

    def _small_arrived(self, l, gath):
        gath = lax.dynamic_update_slice(gath, self.small_vec[l][None], (self.dev, 0, 0))
        self.small_sum[l] = _sum_leading(gath, name=f"l{l}_small_sum").reshape(-1)

    def finish(self, dmod):
        late = jnp.concatenate([jnp.stack([self.grads[l][nm] for l in range(DEPTH)], axis=0).reshape(-1)
                                for nm in LATE_SMALL_NAMES])
        head = -(-late.shape[0] // (8 * FLAT_W)) * 8
        late = jnp.pad(late, (0, head * FLAT_W - late.shape[0])).reshape(head, FLAT_W)
        vec = jnp.concatenate([late, dmod], axis=0)
        rs = self.rs["mix", 0]
        while rs.stage < 2:
            rs.done(_run_exchange(rs.next_exchange(), name=f"l0_mix_rs_stage{rs.stage}"))
        outs = _run_exchange(_both(rs.next_exchange(), _gather_spread(vec, halves=False)), name="final_spread")
        rs.done(outs[:1])
        (gath,) = _run_exchange(_gather_pass_on(outs[1]), name="final_pass_on")
        gath = lax.dynamic_update_slice(gath, vec[None], (self.dev, 0, 0))
        late_sum = _sum_leading(gath[:, :head], name="late_small_sum").reshape(-1)
        loss, res = self._small_grads(late_sum)
        return loss, res, gath[:, head:]

    def _small_grads(self, late):
        out = {nm: [] for nm in SMALL_LAYER_NAMES}
        for l in range(DEPTH):
            off = 0
            for nm in SMALL_LAYER_NAMES:
                size = self.weights[nm][l].size
                out[nm].append(self.small_sum[l][off:off + size].reshape(self.weights[nm].shape[1:]))
                off += size
            if l == DEPTH - 1:
                final = self.small_sum[l][off:off + self.weights["final_norm_g"].size]
                off += final.shape[0]
            if l == 0:
                loss = self.small_sum[l][off]
        res = {nm: jnp.stack(parts, axis=0) for nm, parts in out.items()}
        res["final_norm_g"] = final
        off = 0
        for nm in LATE_SMALL_NAMES:
            size = self.weights[nm].size
            res[nm] = late[off:off + size].reshape(self.weights[nm].shape)
            off += size
        return loss, res

    def mix_grads(self):
        per = {FLAT_KEY[nm]: [] for nm, _ in FSDP_SECTIONS}
        for l in range(DEPTH):
            shard, off = self.rs["mix", l].result, 0
            for nm, nrows in FSDP_SECTIONS:
                key = FLAT_KEY[nm]
                sec = shard[off:off + nrows]
                per[key].append(sec.T if nm == "w_in" else sec.reshape(self.weights[key].shape[1:]))
                off += nrows
        return {key: jnp.stack(parts, axis=0) for key, parts in per.items()}

    def ff_shards(self):
        return [self.rs["ff", l].result for l in range(DEPTH)]


def kernel(x, c, positions, w_ada, b_ada, norm_mix_g, w_in, gmlp_ws, gmlp_bs, mla_q_norm_g, mla_kv_norm_g, mla_w_uq, mla_w_ukv, out_norm_gmlp_g, out_norm_mla_g, w_out, norm_ffn_g, w_ff1, w_ff2, final_norm_g, loss_target, m_w_ada, m_b_ada, m_norm_mix_g, m_w_in, m_gmlp_ws, m_gmlp_bs, m_mla_q_norm_g, m_mla_kv_norm_g, m_mla_w_uq, m_mla_w_ukv, m_out_norm_gmlp_g, m_out_norm_mla_g, m_w_out, m_norm_ffn_g, m_w_ff1, m_w_ff2, m_final_norm_g, v_w_ada, v_b_ada, v_norm_mix_g, v_w_in, v_gmlp_ws, v_gmlp_bs, v_mla_q_norm_g, v_mla_kv_norm_g, v_mla_w_uq, v_mla_w_ukv, v_out_norm_gmlp_g, v_out_norm_mla_g, v_w_out, v_norm_ffn_g, v_w_ff1, v_w_ff2, v_final_norm_g):
    weights = dict(w_ada=w_ada, b_ada=b_ada, norm_mix_g=norm_mix_g, w_in=w_in, gmlp_ws=gmlp_ws, gmlp_bs=gmlp_bs,
                   mla_q_norm_g=mla_q_norm_g, mla_kv_norm_g=mla_kv_norm_g, mla_w_uq=mla_w_uq, mla_w_ukv=mla_w_ukv,
                   out_norm_gmlp_g=out_norm_gmlp_g, out_norm_mla_g=out_norm_mla_g, w_out=w_out,
                   norm_ffn_g=norm_ffn_g, w_ff1=w_ff1, w_ff2=w_ff2, final_norm_g=final_norm_g)
    mom_m = dict(zip(W_NAMES, (m_w_ada, m_b_ada, m_norm_mix_g, m_w_in, m_gmlp_ws, m_gmlp_bs, m_mla_q_norm_g,
                               m_mla_kv_norm_g, m_mla_w_uq, m_mla_w_ukv, m_out_norm_gmlp_g, m_out_norm_mla_g,
                               m_w_out, m_norm_ffn_g, m_w_ff1, m_w_ff2, m_final_norm_g)))
    mom_v = dict(zip(W_NAMES, (v_w_ada, v_b_ada, v_norm_mix_g, v_w_in, v_gmlp_ws, v_gmlp_bs, v_mla_q_norm_g,
                               v_mla_kv_norm_g, v_mla_w_uq, v_mla_w_ukv, v_out_norm_gmlp_g, v_out_norm_mla_g,
                               v_w_out, v_norm_ffn_g, v_w_ff1, v_w_ff2, v_final_norm_g)))
    bsz, seq, d = x.shape
    px, py, pc = _position()
    chip = 2 * px + py
    dev = 2 * chip + pc
    ids = jnp.stack([pc, chip]).astype(jnp.int32)
    n_ex = N_DEV * bsz
    ada_cols = w_ada.shape[-1]

    c_all = _allgather8(c.reshape(bsz * d // 128, 128), name="gather_c").reshape(n_ex, d)
    mod_parts = []
    for l in range(DEPTH):
        bias = lax.dynamic_slice(b_ada[l], (chip * ada_cols,), (ada_cols,))[None]
        mod_parts.append(_mm(c_all, w_ada, dims="nn", name=f"l{l}_mod", tm=n_ex, tn=ada_cols, tk=d, n=ada_cols,
                             b_block=((None, d, ada_cols), lambda i, j, k, l=l: (l, k, j)),
                             epilogue=lambda acc, bv: (acc + bv,), extras=(bias,),
                             extra_specs=(pl.BlockSpec((1, ada_cols), lambda i, j, k: (0, j)),), a_fn=_silu))
    mod_g = _allgather8(jnp.concatenate(mod_parts, axis=0), name="gather_mod")
    mod_g = mod_g.reshape(N_CHIPS, 2, DEPTH, n_ex, ada_cols)[:, 0]
    mod_full = mod_g.transpose(1, 2, 0, 3).reshape(DEPTH, n_ex, N_CHIPS * ada_cols)
    mod_mine = lax.dynamic_slice(mod_full, (0, dev * bsz, 0), (DEPTH, bsz, N_MOD * d))
    mod_mine = jnp.pad(mod_mine.reshape(DEPTH, bsz, N_MOD, d), ((0, 0), (0, 0), (0, MOD_ROWS - N_MOD), (0, 0)))
    mods = [mod_mine[l] for l in range(DEPTH)]

    plan = _CommPlan(weights, ids, dev, pc)
    grad_x, dmods = _local_step(x, loss_target, positions, mods, final_norm_g, plan)

    dmod = jnp.stack(dmods, axis=1).reshape(bsz * DEPTH * N_MOD, d)
    loss, small, dmod_all = plan.finish(dmod)
    grad = plan.mix_grads()
    grad.update(small)
    dmod_all = dmod_all.reshape(n_ex, DEPTH, N_MOD * d)
    gw, gb = [], []
    for l in range(DEPTH):
        dm = dmod_all[:, l]
        dm_cols = lax.dynamic_slice(dm, (0, chip * ada_cols), (n_ex, ada_cols))
        gw.append(_mm(c_all, dm_cols, dims="tn", name=f"l{l}_dw_ada", tm=d, tn=ada_cols, tk=n_ex, a_fn=_silu,
                      out_into=(w_ada.shape, (None, d, ada_cols), lambda i, j, k, l=l: (l, i, j),
                                gw[-1] if gw else None)))
        gb.append(_sum_leading(dm.reshape(n_ex, N_MOD * d // FLAT_W, FLAT_W), name=f"l{l}_db_ada").reshape(-1))
    grad["w_ada"] = gw[-1]
    grad["b_ada"] = jnp.stack(gb, axis=0)

    delta, new_m, new_v = {}, {}, {}
    ff_bufs = plan.ff_shards()
    for nm, row_off in (("w_ff1", 0), ("w_ff2", FLAT_W)):
        grad[nm], delta[nm], new_m[nm], new_v[nm] = _adamw_layers(
            weights[nm], mom_m[nm], mom_v[nm], ff_bufs, row_off, name=f"adamw_{nm}")
    for nm in W_NAMES:
        if nm not in delta:
            delta[nm], new_m[nm], new_v[nm] = _adamw(weights[nm], grad[nm], mom_m[nm], mom_v[nm],
                                                     name=f"adamw_{nm}")
    return (loss, grad_x, *[grad[nm] for nm in W_NAMES], *[delta[nm] for nm in W_NAMES],
            *[new_m[nm] for nm in W_NAMES], *[new_v[nm] for nm in W_NAMES])
```

```python
import functools
import math

import jax
import jax.numpy as jnp
from jax import lax
from jax.experimental import pallas as pl
from jax.experimental.pallas import tpu as pltpu

F32 = jnp.float32
BF16 = jnp.bfloat16

D_MODEL = 1024
DEPTH = 2
D_GMLP = 512
GROUPS = 8
GROUP_DIM = 64
CHUNK = 128
HEADS = 8
NOPE = 64
ROPE = 32
HEAD_PAD = 128
Q_RANK = 256
KV_RANK = 128
D_FF = 4096
N_MOD = 6
MOD_ROWS = 8
EPS = 1e-6
ROPE_THETA = 10000.0
D_IN = 1440
D_IN_PAD = 1536
ATTN_SCALE = (NOPE + ROPE) ** -0.5
LOG2E = math.log2(math.e)
SCALE_LOG2 = ATTN_SCALE * LOG2E
N_CHIPS = 4
N_DEV = 8

ADAM_LR = 0.001
ADAM_B1 = 0.9
ADAM_B2 = 0.999
ADAM_EPS = 1e-08
ADAM_WD = 0.01
ADAM_STEP = 10

VMEM_LIMIT = 48 * 1024 * 1024
FLAT_W = 1024
ROW_ALIGN = 256

NN = (((1,), (0,)), ((), ()))
NT = (((1,), (1,)), ((), ()))
TN = (((0,), (0,)), ((), ()))
MESH = pl.DeviceIdType.MESH

SHIFT1, SCALE1, GATE1, SHIFT2, SCALE2, GATE2 = range(6)

FSDP_SECTIONS = (("w_out", 256), ("w_in", 360), ("w_uq", 48), ("w_ukv", 32))


def _cparams(vmem=VMEM_LIMIT):
    return pltpu.CompilerParams(vmem_limit_bytes=vmem)


def _dot(a, b, dims=NN):
    return lax.dot_general(a, b, dims, preferred_element_type=F32)


def _iota(shape, axis):
    return lax.broadcasted_iota(jnp.int32, shape, axis)


def _gelu(x):
    k = math.sqrt(2.0 / math.pi)
    return 0.5 * x * (1.0 + jnp.tanh(k * (x + 0.044715 * (x * x * x))))


def _gelu_and_grad(x):
    k = math.sqrt(2.0 / math.pi)
    x2 = x * x
    t = jnp.tanh(k * (x + 0.044715 * (x2 * x)))
    half = 0.5 * (1.0 + t)
    return x * half, half + 0.5 * x * (1.0 - t * t) * (k * (1.0 + 3.0 * 0.044715 * x2))


def _rms_fwd(x, g, n):
    r = lax.rsqrt(jnp.sum(x * x, axis=-1, keepdims=True) * (1.0 / n) + EPS)
    return x * r * g


def _rms_bwd(x, g, dy, n):
    r = lax.rsqrt(jnp.sum(x * x, axis=-1, keepdims=True) * (1.0 / n) + EPS)
    xh = x * r
    dxh = dy * g
    dx = r * (dxh - xh * (jnp.sum(dxh * xh, axis=-1, keepdims=True) * (1.0 / n)))
    dg = jnp.sum(dy * xh, axis=0, keepdims=True)
    return dx, dg


def _pick_rows(rows, limit):
    if rows <= limit:
        return rows
    for t in range(limit, 7, -8):
        if rows % t == 0:
            return t
    return rows


def _mm(a, b, *, dims, name, tm=512, tn=1024, tk=1024, out_dtypes=(F32,), epilogue=None,
        extras=(), extra_specs=(), a_fn=None, weights_outer=False, side=None, b_block=None, n=None,
        out_into=None):
    if dims == "tn":
        kk, m = a.shape
    else:
        m, kk = a.shape
    if n is None:
        n = b.shape[0] if dims == "nt" else b.shape[1]
    tm, tn, tk = min(tm, m), min(tn, n), min(tk, kk)
    assert m % tm == 0 and n % tn == 0 and kk % tk == 0, (name, a.shape, b.shape, tm, tn, tk)
    ni, nj, nk = m // tm, n // tn, kk // tk

    def spec(shape, pick):
        if weights_outer:
            return pl.BlockSpec(shape, lambda j, i, k: pick(i, j, k))
        return pl.BlockSpec(shape, pick)

    if dims == "tn":
        a_spec = spec((tk, tm), lambda i, j, k: (k, i))
    else:
        a_spec = spec((tm, tk), lambda i, j, k: (i, k))
    if b_block is not None:
        b_spec = spec(*b_block)
    elif dims == "nt":
        b_spec = spec((tn, tk), lambda i, j, k: (j, k))
    else:
        b_spec = spec((tk, tn), lambda i, j, k: (k, j))
    o_spec = spec((tm, tn), lambda i, j, k: (i, j))
    out_shape = [jax.ShapeDtypeStruct((m, n), dt) for dt in out_dtypes]
    out_specs = [o_spec] * len(out_dtypes)
    prev, io_aliases = (), {}
    if out_into is not None:
        full_shape, block, index, before = out_into
        assert len(out_dtypes) == 1 and not extras
        out_shape = [jax.ShapeDtypeStruct(full_shape, out_dtypes[0])]
        out_specs = [spec(block, index)]
        if before is not None:
            prev, io_aliases = (before,), {2: 0}
    assert not (weights_outer and extra_specs)
    dn = {"nn": NN, "nt": NT, "tn": TN}[dims]
    n_ex, n_out = len(extras), len(out_dtypes)
    e_specs = [o_spec if s is None else s for s in (tuple(extra_specs) + (None,) * n_ex)[:n_ex]]

    n_prev = len(prev)

    def body(*refs):
        a_ref, b_ref = refs[0], refs[1]
        e_refs = refs[2 + n_prev:2 + n_prev + n_ex]
        o_refs = refs[2 + n_prev + n_ex:2 + n_prev + n_ex + n_out]
        av = a_ref[...]
        if a_fn is not None:
            av = a_fn(av)
        bv = b_ref[...]
        if bv.ndim == 3:
            if dims == "nt":
                bv = jnp.concatenate([bv[c] for c in range(bv.shape[0])], axis=1)
            else:
                bv = bv.reshape(-1, bv.shape[-1])
        part = _dot(av.astype(BF16), bv.astype(BF16), dn)

        def finish(acc):
            outs = (acc,) if epilogue is None else epilogue(acc, *[e[...] for e in e_refs])
            for o_ref, o in zip(o_refs, outs):
                o_ref[...] = o.astype(o_ref.dtype)

        if nk == 1:
            finish(part)
        else:
            acc_ref = refs[-1]
            k = pl.program_id(2)

            @pl.when(k == 0)
            def _():
                acc_ref[...] = part

            @pl.when(k > 0)
            def _():
                acc_ref[...] += part

            @pl.when(k == nk - 1)
            def _():
                finish(acc_ref[...])

    outs, side_outs = _hosted_call(
        body, name=name, grid=(nj, ni, nk) if weights_outer else (ni, nj, nk),
        in_specs=[a_spec, b_spec] + [ANY_SPEC] * n_prev + e_specs,
        out_specs=out_specs, out_shape=out_shape,
        scratch_shapes=[pltpu.VMEM((tm, tn), F32)] if nk > 1 else [],
        args=(a, b, *prev, *extras), side=side, io_aliases=io_aliases)
    res = outs[0] if n_out == 1 else outs
    return res if side is None else (res, side_outs)


def _mod_spec(tm, tn, seq):
    return pl.BlockSpec((1, MOD_ROWS, tn), lambda i, j, k: ((i * tm) // seq, 0, j))


def _normmod_fwd(x3, g, mod, shift_row, scale_row, *, name, tb=512, side=None):
    bsz, seq, d = x3.shape
    tb = min(tb, seq)

    def body(x_ref, g_ref, mod_ref, h_ref):
        m = mod_ref[0]
        nrm = _rms_fwd(x_ref[0], g_ref[...], d)
        h = nrm * (1.0 + m[scale_row:scale_row + 1, :]) + m[shift_row:shift_row + 1, :]
        h_ref[0] = h.astype(BF16)

    outs, side_outs = _hosted_call(
        body, name=name, grid=(bsz, seq // tb),
        in_specs=[pl.BlockSpec((1, tb, d), lambda b, i: (b, i, 0)),
                  pl.BlockSpec((1, d), lambda b, i: (0, 0)),
                  pl.BlockSpec((1, MOD_ROWS, d), lambda b, i: (b, 0, 0))],
        out_specs=[pl.BlockSpec((1, tb, d), lambda b, i: (b, i, 0))],
        out_shape=[jax.ShapeDtypeStruct((bsz, seq, d), BF16)],
        args=(x3, g, mod), side=side)
    return outs[0] if side is None else (outs[0], side_outs)


def _pair_mean_exact(x, lo):
    s_lo = jnp.sum(jnp.where(lo, x, 0.0), axis=-1, keepdims=True)
    s_hi = jnp.sum(jnp.where(lo, 0.0, x), axis=-1, keepdims=True)
    return jnp.where(lo, s_lo, s_hi) * (1.0 / GROUP_DIM)


def _gmlp_pair_fwd(gv_p, w0, w1, bias, lo):
    mu = _pair_mean_exact(gv_p, lo)
    dlt = gv_p - mu
    var = _pair_mean_exact(dlt * dlt, lo)
    rstd = lax.rsqrt(var + EPS)
    vn = dlt * rstd
    vnb = vn.astype(BF16)
    mixed = jnp.where(lo, _dot(w0, vnb), _dot(w1, vnb)) + bias
    return vn, vnb, rstd, mixed


def _tril_bf16(w):
    t = w.shape[-1]
    return jnp.where(_iota((t, t), 1) <= _iota((t, t), 0), w, 0.0).astype(BF16)


def _gmlp_fwd(z3, ws, bexp, g_out, *, name):
    bsz, seq, _ = z3.shape
    cpb = max(k for k in (1, 2, 4) if (seq // CHUNK) % k == 0)
    nc = seq // (CHUNK * cpb)
    tb = CHUNK * cpb

    def body(u_ref, v_ref, ws_ref, b_ref, g_ref, y_ref):
        lo = _iota((CHUNK, 128), 1) < GROUP_DIM
        for c in range(cpb):
            rows = slice(CHUNK * c, CHUNK * (c + 1))
            gu = _gelu(u_ref[0, rows, :].astype(F32))
            gv = _gelu(v_ref[0, rows, :].astype(F32))
            parts = []
            for p in range(GROUPS // 2):
                sl = slice(128 * p, 128 * p + 128)
                w0 = _tril_bf16(ws_ref[2 * p])
                w1 = _tril_bf16(ws_ref[2 * p + 1])
                _, _, _, mixed = _gmlp_pair_fwd(gv[:, sl], w0, w1, b_ref[p], lo)
                parts.append(gu[:, sl] * mixed)
            yg = jnp.concatenate(parts, axis=1)
            y_ref[0, rows, :] = _rms_fwd(yg, g_ref[...], D_GMLP).astype(BF16)

    return pl.pallas_call(
        body, name=name, grid=(bsz, nc),
        in_specs=[pl.BlockSpec((1, tb, D_GMLP), lambda b, i: (b, i, 0)),
                  pl.BlockSpec((1, tb, D_GMLP), lambda b, i: (b, i, 1)),
                  pl.BlockSpec((GROUPS, CHUNK, CHUNK), lambda b, i: (0, 0, 0)),
                  pl.BlockSpec((GROUPS // 2, CHUNK, 128), lambda b, i: (0, 0, 0)),
                  pl.BlockSpec((1, D_GMLP), lambda b, i: (0, 0))],
        out_specs=pl.BlockSpec((1, tb, D_GMLP), lambda b, i: (b, i, 0)),
        out_shape=jax.ShapeDtypeStruct((bsz, seq, D_GMLP), BF16),
        compiler_params=_cparams(),
    )(z3, z3, ws, bexp, g_out)


def _gmlp_bwd(z3, dyn3, ws, wst, bexp, g_out, *, name, dy_col):
    bsz, seq, _ = z3.shape
    cpb = max(k for k in (1, 2, 4) if (seq // CHUNK) % k == 0)
    nc = seq // (CHUNK * cpb)
    tb = CHUNK * cpb
    npair = GROUPS // 2

    def body(u_ref, v_ref, dy_ref, ws_ref, wst_ref, b_ref, g_ref, duv_ref, dws_ref, dbs_ref, dg_ref, dbacc):
        first = jnp.logical_and(pl.program_id(0) == 0, pl.program_id(1) == 0)
        last = jnp.logical_and(pl.program_id(0) == bsz - 1, pl.program_id(1) == nc - 1)

        @pl.when(first)
        def _():
            dws_ref[...] = jnp.zeros_like(dws_ref)
            dg_ref[...] = jnp.zeros_like(dg_ref)
            dbacc[...] = jnp.zeros_like(dbacc)

        lo = _iota((CHUNK, 128), 1) < GROUP_DIM
        tril = _iota((CHUNK, CHUNK), 1) <= _iota((CHUNK, CHUNK), 0)
        for c in range(cpb):
            rows = slice(CHUNK * c, CHUNK * (c + 1))
            u = u_ref[0, rows, :].astype(F32)
            v = v_ref[0, rows, :].astype(F32)
            gu, dgu = _gelu_and_grad(u)
            gv, dgv_dv = _gelu_and_grad(v)
            fwd = []
            for p in range(npair):
                sl = slice(128 * p, 128 * p + 128)
                w0 = _tril_bf16(ws_ref[2 * p])
                w1 = _tril_bf16(ws_ref[2 * p + 1])
                fwd.append(_gmlp_pair_fwd(gv[:, sl], w0, w1, b_ref[p], lo))
            yg = jnp.concatenate([gu[:, 128 * p:128 * p + 128] * fwd[p][3] for p in range(npair)], axis=1)
            dyg, dg = _rms_bwd(yg, g_ref[...], dy_ref[0, rows, :].astype(F32), D_GMLP)
            dg_ref[...] += dg
            du_parts, dv_parts = [], []
            for p in range(npair):
                sl = slice(128 * p, 128 * p + 128)
                vn, vnb, rstd, mixed = fwd[p]
                dyg_p = dyg[:, sl]
                dmixed = dyg_p * gu[:, sl]
                dbacc[p] += dmixed
                dm_lo = jnp.where(lo, dmixed, 0.0).astype(BF16)
                dm_hi = jnp.where(lo, 0.0, dmixed).astype(BF16)
                dws_ref[2 * p] += jnp.where(tril, _dot(dm_lo, vnb, NT), 0.0)
                dws_ref[2 * p + 1] += jnp.where(tril, _dot(dm_hi, vnb, NT), 0.0)
                dmb = dmixed.astype(BF16)
                dvn = jnp.where(lo, _dot(wst_ref[2 * p], dmb), _dot(wst_ref[2 * p + 1], dmb))
                dgv = rstd * (dvn - _pair_mean_exact(dvn, lo) - vn * _pair_mean_exact(dvn * vn, lo))
                dv_parts.append(dgv * dgv_dv[:, sl])
                du_parts.append(dyg_p * mixed * dgu[:, sl])
            duv_ref[0, rows, :] = jnp.concatenate(du_parts + dv_parts, axis=1).astype(BF16)

        @pl.when(last)
        def _():
            sel = jnp.where(_iota((8, 128), 0) == 0, (_iota((8, 128), 1) < GROUP_DIM).astype(F32),
                            jnp.where(_iota((8, 128), 0) == 1, (_iota((8, 128), 1) >= GROUP_DIM).astype(F32), 0.0))
            for p in range(npair):
                dbs_ref[p] = lax.dot_general(sel, dbacc[p], NT, precision=lax.Precision.HIGHEST,
                                             preferred_element_type=F32)

    duv, dws, dbs, dg = pl.pallas_call(
        body, name=name, grid=(bsz, nc),
        in_specs=[pl.BlockSpec((1, tb, D_GMLP), lambda b, i: (b, i, 0)),
                  pl.BlockSpec((1, tb, D_GMLP), lambda b, i: (b, i, 1)),
                  pl.BlockSpec((1, tb, D_GMLP), lambda b, i: (b, i, dy_col)),
                  pl.BlockSpec((GROUPS, CHUNK, CHUNK), lambda b, i: (0, 0, 0)),
                  pl.BlockSpec((GROUPS, CHUNK, CHUNK), lambda b, i: (0, 0, 0)),
                  pl.BlockSpec((npair, CHUNK, 128), lambda b, i: (0, 0, 0)),
                  pl.BlockSpec((1, D_GMLP), lambda b, i: (0, 0))],
        out_specs=[pl.BlockSpec((1, tb, 2 * D_GMLP), lambda b, i: (b, i, 0)),
                   pl.BlockSpec((GROUPS, CHUNK, CHUNK), lambda b, i: (0, 0, 0)),
                   pl.BlockSpec((npair, 8, CHUNK), lambda b, i: (0, 0, 0)),
                   pl.BlockSpec((1, D_GMLP), lambda b, i: (0, 0))],
        out_shape=[jax.ShapeDtypeStruct((bsz, seq, D_IN_PAD), BF16),
                   jax.ShapeDtypeStruct((GROUPS, CHUNK, CHUNK), F32),
                   jax.ShapeDtypeStruct((npair, 8, CHUNK), F32),
                   jax.ShapeDtypeStruct((1, D_GMLP), F32)],
        scratch_shapes=[pltpu.VMEM((npair, CHUNK, 128), F32)],
        compiler_params=_cparams(),
    )(z3, z3, dyn3, ws, wst, bexp, g_out)
    return duv, dws, dbs[:, :2, :].reshape(GROUPS, CHUNK), dg


def _partner(x):
    width = x.shape[-1]
    lane = _iota(x.shape, x.ndim - 1) % HEAD_PAD
    up = pltpu.roll(x, width - ROPE // 2, x.ndim - 1)
    down = pltpu.roll(x, ROPE // 2, x.ndim - 1)
    first = jnp.logical_and(lane >= NOPE, lane < NOPE + ROPE // 2)
    second = jnp.logical_and(lane >= NOPE + ROPE // 2, lane < NOPE + ROPE)
    return jnp.where(first, up, jnp.where(second, down, 0.0))


def _mla_prep_fwd(z3, g_q, g_kv, w_uq, w_ukv, ctab, stab, *, name, tb=512):
    bsz, seq, _ = z3.shape
    tb = min(tb, seq)
    hw = HEADS * HEAD_PAD

    def body(ql_ref, kvl_ref, krl_ref, gq_ref, gkv_ref, wuq_ref, wukv_ref, c_ref, s_ref, q_ref, kv_ref, kp_ref):
        cq = _rms_fwd(ql_ref[0].astype(F32), gq_ref[...], Q_RANK).astype(BF16)
        q = _dot(cq, wuq_ref[...])
        c1, s1 = c_ref[0], s_ref[0]
        c8, s8 = jnp.tile(c1, (1, HEADS)), jnp.tile(s1, (1, HEADS))
        q_ref[0] = ((q * c8 + _partner(q) * s8) * SCALE_LOG2).astype(BF16)
        ckv = _rms_fwd(kvl_ref[0].astype(F32), gkv_ref[...], KV_RANK).astype(BF16)
        kv = _dot(ckv, wukv_ref[...])
        kv_ref[0] = kv.astype(BF16)
        kr = krl_ref[0].astype(F32)
        kr = kr * c1 + _partner(kr) * s1
        lane = _iota((tb, hw), 1) % HEAD_PAD
        kp_ref[0] = jnp.where(lane < NOPE, kv, jnp.tile(kr, (1, HEADS))).astype(BF16)

    return pl.pallas_call(
        body, name=name, grid=(bsz, seq // tb),
        in_specs=[pl.BlockSpec((1, tb, Q_RANK), lambda b, i: (b, i, 4)),
                  pl.BlockSpec((1, tb, KV_RANK), lambda b, i: (b, i, 10)),
                  pl.BlockSpec((1, tb, HEAD_PAD), lambda b, i: (b, i, 11)),
                  pl.BlockSpec((1, Q_RANK), lambda b, i: (0, 0)),
                  pl.BlockSpec((1, KV_RANK), lambda b, i: (0, 0)),
                  pl.BlockSpec((Q_RANK, hw), lambda b, i: (0, 0)),
                  pl.BlockSpec((KV_RANK, hw), lambda b, i: (0, 0)),
                  pl.BlockSpec((1, tb, HEAD_PAD), lambda b, i: (b, i, 0)),
                  pl.BlockSpec((1, tb, HEAD_PAD), lambda b, i: (b, i, 0))],
        out_specs=[pl.BlockSpec((1, tb, hw), lambda b, i: (b, i, 0))] * 3,
        out_shape=[jax.ShapeDtypeStruct((bsz, seq, hw), BF16)] * 3,
        compiler_params=_cparams(),
    )(z3, z3, z3, g_q, g_kv, w_uq, w_ukv, ctab, stab)


def _mla_prep_bwd(z3, dz3, dq3, dk3, dv3, g_q, g_kv, w_uq, w_ukv, ctab, stab, *, name, tb=512):
    bsz, seq, _ = z3.shape
    tb = min(tb, seq)
    hw = HEADS * HEAD_PAD
    nb = seq // tb

    def body(ql_ref, kvl_ref, dq_ref, dk_ref, dv_ref, gq_ref, gkv_ref, wuq_ref, wukv_ref, c_ref, s_ref, dz_in,
             dz_ref, cq_ref, dqb_ref, ckv_ref, dkvb_ref, dgq_ref, dgkv_ref):
        @pl.when(jnp.logical_and(pl.program_id(0) == 0, pl.program_id(1) == 0))
        def _():
            dgq_ref[...] = jnp.zeros_like(dgq_ref)
            dgkv_ref[...] = jnp.zeros_like(dgkv_ref)

        c1, s1 = c_ref[0], s_ref[0]
        c8, s8 = jnp.tile(c1, (1, HEADS)), jnp.tile(s1, (1, HEADS))
        dqr = dq_ref[0]
        dqb = (dqr * c8 + _partner(dqr * s8)).astype(BF16)
        dqb_ref[0] = dqb
        ql = ql_ref[0].astype(F32)
        cq_ref[0] = _rms_fwd(ql, gq_ref[...], Q_RANK).astype(BF16)
        dql, dgq = _rms_bwd(ql, gq_ref[...], _dot(dqb, wuq_ref[...], NT), Q_RANK)
        dgq_ref[...] += dgq

        dk = dk_ref[0]
        lane = _iota((tb, hw), 1) % HEAD_PAD
        dkvb = jnp.where(lane < NOPE, dk, dv_ref[0]).astype(BF16)
        dkvb_ref[0] = dkvb
        kvl = kvl_ref[0].astype(F32)
        ckv_ref[0] = _rms_fwd(kvl, gkv_ref[...], KV_RANK).astype(BF16)
        dkvl, dgkv = _rms_bwd(kvl, gkv_ref[...], _dot(dkvb, wukv_ref[...], NT), KV_RANK)
        dgkv_ref[...] += dgkv

        dkr = dk[:, 0:HEAD_PAD].astype(F32)
        for h in range(1, HEADS):
            dkr = dkr + dk[:, HEAD_PAD * h:HEAD_PAD * (h + 1)].astype(F32)
        lane1 = _iota((tb, HEAD_PAD), 1)
        dkr = jnp.where(jnp.logical_and(lane1 >= NOPE, lane1 < NOPE + ROPE), dkr, 0.0)
        dkrl = dkr * c1 + _partner(dkr * s1)
        dz_ref[0] = jnp.concatenate([dql, dkvl, dkrl], axis=1).astype(BF16)

    return pl.pallas_call(
        body, name=name, grid=(bsz, nb),
        in_specs=[pl.BlockSpec((1, tb, Q_RANK), lambda b, i: (b, i, 4)),
                  pl.BlockSpec((1, tb, KV_RANK), lambda b, i: (b, i, 10)),
                  pl.BlockSpec((1, tb, hw), lambda b, i: (b, i, 0)),
                  pl.BlockSpec((1, tb, hw), lambda b, i: (b, i, 0)),
                  pl.BlockSpec((1, tb, hw), lambda b, i: (b, i, 0)),
                  pl.BlockSpec((1, Q_RANK), lambda b, i: (0, 0)),
                  pl.BlockSpec((1, KV_RANK), lambda b, i: (0, 0)),
                  pl.BlockSpec((Q_RANK, hw), lambda b, i: (0, 0)),
                  pl.BlockSpec((KV_RANK, hw), lambda b, i: (0, 0)),
                  pl.BlockSpec((1, tb, HEAD_PAD), lambda b, i: (b, i, 0)),
                  pl.BlockSpec((1, tb, HEAD_PAD), lambda b, i: (b, i, 0)),
                  ANY_SPEC],
        out_specs=[pl.BlockSpec((1, tb, 512), lambda b, i: (b, i, 2)),
                   pl.BlockSpec((1, tb, Q_RANK), lambda b, i: (b, i, 0)),
                   pl.BlockSpec((1, tb, hw), lambda b, i: (b, i, 0)),
                   pl.BlockSpec((1, tb, KV_RANK), lambda b, i: (b, i, 0)),
                   pl.BlockSpec((1, tb, hw), lambda b, i: (b, i, 0)),
                   pl.BlockSpec((1, Q_RANK), lambda b, i: (0, 0)),
                   pl.BlockSpec((1, KV_RANK), lambda b, i: (0, 0))],
        out_shape=[jax.ShapeDtypeStruct((bsz, seq, D_IN_PAD), BF16),
                   jax.ShapeDtypeStruct((bsz, seq, Q_RANK), BF16),
                   jax.ShapeDtypeStruct((bsz, seq, hw), BF16),
                   jax.ShapeDtypeStruct((bsz, seq, KV_RANK), BF16),
                   jax.ShapeDtypeStruct((bsz, seq, hw), BF16),
                   jax.ShapeDtypeStruct((1, Q_RANK), F32),
                   jax.ShapeDtypeStruct((1, KV_RANK), F32)],
        input_output_aliases={11: 0},
        compiler_params=_cparams(),
    )(z3, z3, dq3, dk3, dv3, g_q, g_kv, w_uq, w_ukv, ctab, stab, dz3)


ATTN_HEADS_PER_STEP = 4


def _attn_specs(tq, seq, hp):
    blk = pl.BlockSpec((1, tq, hp * HEAD_PAD), lambda b, h, i: (b, i, h))
    full = pl.BlockSpec((1, seq, hp * HEAD_PAD), lambda b, h, i: (b, 0, h))
    return blk, full


def _head(h):
    return slice(HEAD_PAD * h, HEAD_PAD * (h + 1))


def _attn_fwd(q3, kv3, kp3, *, name, tq=512, hp=ATTN_HEADS_PER_STEP, side=None):
    bsz, seq, hw = q3.shape
    tq = min(tq, seq)
    blk, full = _attn_specs(tq, seq, hp)

    def body(q_ref, kv_ref, kp_ref, o_ref, lse_ref):
        i = pl.program_id(2)

        def update(state, q, kp, kv, mask=None):
            m, l, acc = state
            s = _dot(q, kp, NT)
            if mask is not None:
                s = jnp.where(mask, s, -1e30)
            m_new = jnp.maximum(m, jnp.max(s, axis=1, keepdims=True))
            alpha = jnp.exp2(m - m_new)
            p = jnp.exp2(s - m_new)
            return m_new, alpha * l + jnp.sum(p, axis=1, keepdims=True), alpha * acc + _dot(p.astype(BF16), kv)

        def step(j, carry):
            st = pl.multiple_of(j * tq, tq)
            return tuple(update(carry[h], q_ref[0, :, _head(h)], kp_ref[0, pl.ds(st, tq), _head(h)],
                                kv_ref[0, pl.ds(st, tq), _head(h)]) for h in range(hp))

        init = tuple((jnp.full((tq, 1), -1e30, F32), jnp.zeros((tq, 1), F32), jnp.zeros((tq, HEAD_PAD), F32))
                     for _ in range(hp))
        carry = lax.fori_loop(0, i, step, init)

        st = pl.multiple_of(i * tq, tq)
        is_nope = _iota((tq, HEAD_PAD), 1) < NOPE
        causal = _iota((tq, tq), 1) <= _iota((tq, tq), 0)
        for h in range(hp):
            m, l, acc = update(carry[h], q_ref[0, :, _head(h)], kp_ref[0, pl.ds(st, tq), _head(h)],
                               kv_ref[0, pl.ds(st, tq), _head(h)], causal)
            o_ref[0, :, _head(h)] = jnp.where(is_nope, 0.0, acc / l).astype(BF16)
            lse_ref[0, :, _head(h)] = jnp.broadcast_to(m + jnp.log(l) * LOG2E, (tq, HEAD_PAD))

    outs, side_outs = _hosted_call(
        body, name=name, grid=(bsz, HEADS // hp, seq // tq),
        in_specs=[blk, full, full],
        out_specs=[blk, blk],
        out_shape=[jax.ShapeDtypeStruct((bsz, seq, hw), BF16), jax.ShapeDtypeStruct((bsz, seq, hw), F32)],
        args=(q3, kv3, kp3), side=side)
    return outs if side is None else (outs, side_outs)


def _attn_bwd(q3, kv3, kp3, do3, lse3, dl3, *, name, tq=512, hp=ATTN_HEADS_PER_STEP, side=None):
    bsz, seq, hw = q3.shape
    tq = min(tq, seq)
    nq = seq // tq
    blk, full = _attn_specs(tq, seq, hp)

    def body(kv_ref, kp_ref, q_ref, do_ref, lse_ref, dl_ref, dq_ref, dk_ref, dv_ref):
        j = pl.program_id(2)

        @pl.when(j == 0)
        def _():
            dq_ref[...] = jnp.zeros_like(dq_ref)

        def pair(h, row0, nrows, nkeys, mask=None):
            row0 = pl.multiple_of(row0, nrows)
            qi = q_ref[0, pl.ds(row0, nrows), _head(h)]
            do = do_ref[0, pl.ds(row0, nrows), _head(h)]
            kp = kp_ref[0, :nkeys, _head(h)]
            s = _dot(qi, kp, NT)
            if mask is not None:
                s = jnp.where(mask, s, -1e30)
            wide = nkeys // HEAD_PAD
            p = jnp.exp2(s - jnp.tile(lse_ref[0, pl.ds(row0, nrows), _head(h)], (1, wide)))
            dv = _dot(p.astype(BF16), do, TN)
            dp = _dot(do, kv_ref[0, :nkeys, _head(h)], NT)
            ds = (p * (dp - jnp.tile(dl_ref[0, pl.ds(row0, nrows), _head(h)], (1, wide)))).astype(BF16)
            dq_ref[0, pl.ds(row0, nrows), _head(h)] += _dot(ds, kp)
            return _dot(ds, qi, TN), dv

        def step(i, carry):
            st = pl.multiple_of(i * tq, tq)
            out = []
            for h in range(hp):
                dk, dv = pair(h, st, tq, tq)
                out.append((carry[h][0] + dk, carry[h][1] + dv))
            return tuple(out)

        causal = _iota((tq, tq), 1) <= _iota((tq, tq), 0)
        carry = tuple(pair(h, pl.multiple_of(j * tq, tq), tq, tq, causal) for h in range(hp))
        carry = lax.fori_loop(j + 1, nq, step, carry)
        for h in range(hp):
            dk_ref[0, :, _head(h)] = (carry[h][0] * (1.0 / LOG2E)).astype(BF16)
            dv_ref[0, :, _head(h)] = carry[h][1].astype(BF16)

        @pl.when(j == nq - 1)
        def _():
            dq_ref[...] = dq_ref[...] * ATTN_SCALE

    outs, side_outs = _hosted_call(
        body, name=name, grid=(bsz, HEADS // hp, nq),
        in_specs=[blk, blk, full, full, full, full],
        out_specs=[full, blk, blk],
        out_shape=[jax.ShapeDtypeStruct((bsz, seq, hw), F32)] + [jax.ShapeDtypeStruct((bsz, seq, hw), BF16)] * 2,
        args=(kv3, kp3, q3, do3, lse3, dl3), side=side)
    return outs if side is None else (outs, side_outs)


def _onorm_fwd(o3, yg3, g_pad, *, name, tb=512):
    bsz, seq, hw = o3.shape
    wg = yg3.shape[-1]
    tb = min(tb, seq)

    def body(o_ref, yg_ref, g_ref, y_ref):
        ya = _rms_fwd(o_ref[0].astype(F32), g_ref[...], HEADS * 64).astype(BF16)
        y_ref[0] = jnp.concatenate([ya, yg_ref[0]], axis=1)

    return pl.pallas_call(
        body, name=name, grid=(bsz, seq // tb),
        in_specs=[pl.BlockSpec((1, tb, hw), lambda b, i: (b, i, 0)),
                  pl.BlockSpec((1, tb, wg), lambda b, i: (b, i, 0)),
                  pl.BlockSpec((1, hw), lambda b, i: (0, 0))],
        out_specs=pl.BlockSpec((1, tb, hw + wg), lambda b, i: (b, i, 0)),
        out_shape=jax.ShapeDtypeStruct((bsz, seq, hw + wg), BF16),
        compiler_params=_cparams(),
    )(o3, yg3, g_pad)


def _onorm_bwd(o3, dy3, g_pad, *, name, tb=512):
    bsz, seq, hw = o3.shape
    tb = min(tb, seq)

    def body(o_ref, dy_ref, g_ref, do_ref, dl_ref, dg_ref):
        @pl.when(jnp.logical_and(pl.program_id(0) == 0, pl.program_id(1) == 0))
        def _():
            dg_ref[...] = jnp.zeros_like(dg_ref)

        o = o_ref[0].astype(F32)
        do, dg = _rms_bwd(o, g_ref[...], dy_ref[0].astype(F32), HEADS * 64)
        dg_ref[...] += dg
        do_ref[0] = do.astype(BF16)
        prod = do * o
        parts = []
        for h in range(HEADS):
            sh = jnp.sum(prod[:, HEAD_PAD * h:HEAD_PAD * (h + 1)], axis=1, keepdims=True)
            parts.append(jnp.broadcast_to(sh, (tb, HEAD_PAD)))
        dl_ref[0] = jnp.concatenate(parts, axis=1)

    return pl.pallas_call(
        body, name=name, grid=(bsz, seq // tb),
        in_specs=[pl.BlockSpec((1, tb, hw), lambda b, i: (b, i, 0)),
                  pl.BlockSpec((1, tb, hw), lambda b, i: (b, i, 0)),
                  pl.BlockSpec((1, hw), lambda b, i: (0, 0))],
        out_specs=[pl.BlockSpec((1, tb, hw), lambda b, i: (b, i, 0)),
                   pl.BlockSpec((1, tb, hw), lambda b, i: (b, i, 0)),
                   pl.BlockSpec((1, hw), lambda b, i: (0, 0))],
        out_shape=[jax.ShapeDtypeStruct((bsz, seq, hw), BF16),
                   jax.ShapeDtypeStruct((bsz, seq, hw), F32),
                   jax.ShapeDtypeStruct((1, hw), F32)],
        compiler_params=_cparams(),
    )(o3, dy3, g_pad)


def _resnode_bwd(x3, g, *, name, target3=None, dh3=None, dres3=None, mod_nm=None, rows=None,
                 branch3=None, mod_gate=None, gate_row=None, tb=512, side=None):
    bsz, seq, d = x3.shape
    tb = min(tb, seq)
    final = target3 is not None
    has_branch = branch3 is not None
    row_spec = pl.BlockSpec((1, tb, d), lambda b, i: (b, i, 0))
    vec_spec = pl.BlockSpec((1, d), lambda b, i: (0, 0))
    mod_spec = pl.BlockSpec((1, MOD_ROWS, d), lambda b, i: (b, 0, 0))

    ins, in_specs = [x3, g], [row_spec, vec_spec]
    if final:
        ins += [target3]
        in_specs += [row_spec]
    else:
        ins += [dh3, dres3, mod_nm]
        in_specs += [row_spec, row_spec, mod_spec]
    if has_branch:
        ins += [branch3, mod_gate]
        in_specs += [row_spec, mod_spec]

    out_names = ["dx", "dg"]
    out_specs = [row_spec, vec_spec]
    out_shape = [jax.ShapeDtypeStruct((bsz, seq, d), F32), jax.ShapeDtypeStruct((1, d), F32)]
    if final:
        out_names += ["loss"]
        out_specs += [pl.BlockSpec((1, 128), lambda b, i: (0, 0))]
        out_shape += [jax.ShapeDtypeStruct((1, 128), F32)]
    else:
        out_names += ["dnm"]
        out_specs += [mod_spec]
        out_shape += [jax.ShapeDtypeStruct((bsz, MOD_ROWS, d), F32)]
    if has_branch:
        out_names += ["dbr", "dgate"]
        out_specs += [row_spec, mod_spec]
        out_shape += [jax.ShapeDtypeStruct((bsz, seq, d), BF16), jax.ShapeDtypeStruct((bsz, MOD_ROWS, d), F32)]
    n_in = len(ins)

    def body(*refs):
        r = dict(zip(["x", "g"] + (["t"] if final else ["dh", "dres", "nm"]) + (["br", "gm"] if has_branch else []),
                     refs[:n_in]))
        o = dict(zip(out_names, refs[n_in:]))
        b_first = pl.program_id(1) == 0
        first = jnp.logical_and(pl.program_id(0) == 0, b_first)
        rowid = _iota((MOD_ROWS, d), 0)

        @pl.when(first)
        def _():
            o["dg"][...] = jnp.zeros_like(o["dg"])
            if final:
                o["loss"][...] = jnp.zeros_like(o["loss"])

        @pl.when(b_first)
        def _():
            if not final:
                o["dnm"][...] = jnp.zeros_like(o["dnm"])
            if has_branch:
                o["dgate"][...] = jnp.zeros_like(o["dgate"])

        x = r["x"][0]
        gv = r["g"][...]
        if final:
            e = _rms_fwd(x, gv, d) - r["t"][0]
            sq = jnp.sum(jnp.sum(e * e, axis=1, keepdims=True), axis=0, keepdims=True)
            o["loss"][...] += jnp.broadcast_to(sq * (0.5 / d), (1, 128))
            dx, dg = _rms_bwd(x, gv, e * (1.0 / d), d)
        else:
            m = r["nm"][0]
            dh = r["dh"][0].astype(F32)
            scale = m[rows[1]:rows[1] + 1, :]
            rstd = lax.rsqrt(jnp.sum(x * x, axis=-1, keepdims=True) * (1.0 / d) + EPS)
            xh = x * rstd
            nrm = xh * gv
            dshift = jnp.sum(dh, axis=0, keepdims=True)
            dscale = jnp.sum(dh * nrm, axis=0, keepdims=True)
            o["dnm"][0] += jnp.where(rowid == 0, dshift, jnp.where(rowid == 1, dscale, 0.0))
            dn = dh * (1.0 + scale)
            dg = jnp.sum(dn * xh, axis=0, keepdims=True)
            dxh = dn * gv
            dx = rstd * (dxh - xh * (jnp.sum(dxh * xh, axis=-1, keepdims=True) * (1.0 / d))) + r["dres"][0]
        o["dg"][...] += dg
        o["dx"][0] = dx
        if has_branch:
            gate = r["gm"][0][gate_row:gate_row + 1, :]
            o["dbr"][0] = (gate * dx).astype(BF16)
            dgate = jnp.sum(dx * r["br"][0], axis=0, keepdims=True)
            o["dgate"][0] += jnp.where(rowid == 0, dgate, 0.0)

    outs, side_outs = _hosted_call(
        body, name=name, grid=(bsz, seq // tb),
        in_specs=in_specs, out_specs=out_specs, out_shape=out_shape, args=tuple(ins), side=side)
    res = dict(zip(out_names, outs))
    return res if side is None else (res, side_outs)


def _adamw(w, g, m, v, *, name):
    shape = w.shape
    cols = shape[-1]
    rows = w.size // cols
    tr = _pick_rows(rows, max(8, (256 * 1024) // cols // 8 * 8))

    def body(w_ref, g_ref, m_ref, v_ref, d_ref, nm_ref, nv_ref):
        d_ref[...], nm_ref[...], nv_ref[...] = _adamw_math(w_ref[...], g_ref[...], m_ref[...], v_ref[...])

    if w.ndim == 3 and shape[1] % 8 == 0:
        tr3 = _pick_rows(shape[1], max(8, (256 * 1024) // cols // 8 * 8))
        spec3 = pl.BlockSpec((None, tr3, cols), lambda l, i: (l, i, 0))
        return tuple(pl.pallas_call(
            body, name=name, grid=(shape[0], shape[1] // tr3),
            in_specs=[spec3] * 4, out_specs=[spec3] * 3,
            out_shape=[jax.ShapeDtypeStruct(shape, F32)] * 3,
            compiler_params=_cparams(),
        )(w, g, m, v))
    spec = pl.BlockSpec((tr, cols), lambda i: (i, 0))
    outs = pl.pallas_call(
        body, name=name, grid=(rows // tr,),
        in_specs=[spec] * 4, out_specs=[spec] * 3,
        out_shape=[jax.ShapeDtypeStruct((rows, cols), F32)] * 3,
        compiler_params=_cparams(),
    )(*[t.reshape(rows, cols) for t in (w, g, m, v)])
    return tuple(o.reshape(shape) for o in outs)


def _adamw_math(w, g, m, v):
    c1 = 1.0 - ADAM_B1 ** ADAM_STEP
    c2 = 1.0 - ADAM_B2 ** ADAM_STEP
    nm = ADAM_B1 * m + (1.0 - ADAM_B1) * g
    nv = ADAM_B2 * v + (1.0 - ADAM_B2) * (g * g)
    delta = -ADAM_LR * ((nm / c1) / (jnp.sqrt(nv / c2) + ADAM_EPS) + ADAM_WD * w)
    return delta, nm, nv


def _adamw_layers(w, m, v, bufs, row_off, *, name, tr=256):
    depth, rows, cols = w.shape
    tr = min(tr, rows)
    assert rows % tr == 0 and row_off % tr == 0

    outs = None
    for l in range(depth):
        def body(w_ref, g_ref, m_ref, v_ref, *rest):
            go_ref, d_ref, nm_ref, nv_ref = rest[-4:]
            g = g_ref[...]
            go_ref[...] = g
            d_ref[...], nm_ref[...], nv_ref[...] = _adamw_math(w_ref[...], g, m_ref[...], v_ref[...])

        layer = pl.BlockSpec((None, tr, cols), lambda i, l=l: (l, i, 0))
        prev = () if outs is None else tuple(outs)
        outs = pl.pallas_call(
            body, name=f"{name}_l{l}", grid=(rows // tr,),
            in_specs=[layer, pl.BlockSpec((tr, cols), lambda i: (row_off // tr + i, 0)), layer, layer]
            + [ANY_SPEC] * len(prev),
            out_specs=[layer] * 4,
            out_shape=[jax.ShapeDtypeStruct(w.shape, F32)] * 4,
            input_output_aliases={4 + k: k for k in range(len(prev))},
            compiler_params=_cparams(),
        )(w, bufs[l], m, v, *prev)
    return tuple(outs)


def _sum_leading(x, *, name, tr=256):
    n, rows, cols = x.shape
    tr = _pick_rows(rows, tr)

    def body(x_ref, o_ref):
        acc = x_ref[0]
        for k in range(1, n):
            acc = acc + x_ref[k]
        o_ref[...] = acc

    return pl.pallas_call(
        body, name=name, grid=(rows // tr,),
        in_specs=[pl.BlockSpec((n, tr, cols), lambda i: (0, i, 0))],
        out_specs=pl.BlockSpec((tr, cols), lambda i: (i, 0)),
        out_shape=jax.ShapeDtypeStruct((rows, cols), F32),
        compiler_params=_cparams(),
    )(x)


def _position():
    return lax.axis_index("x"), lax.axis_index("y"), lax.axis_index("c")


def _allgather8(x, *, name):
    shape = x.shape

    def body(x_ref, out_ref, send_sems, recv_sems, local_sem):
        px, py, pc = _position()
        me, sibling = (px, py, pc), (px, py, 1 - pc)
        chips = [(1 - px, py), (px, 1 - py), (1 - px, 1 - py)]
        src_own = x_ref

        def slot(qx, qy, qc):
            return out_ref.at[4 * qx + 2 * qy + qc]

        def copy(k, block, to, src=None):
            return pltpu.make_async_remote_copy(
                src_ref=slot(*block) if src is None else src, dst_ref=slot(*block),
                send_sem=send_sems.at[k], recv_sem=recv_sems.at[k], device_id=to, device_id_type=MESH)

        mine = pltpu.make_async_copy(src_own, slot(*me), local_sem)
        mine.start()
        first = [copy(0, me, sibling, src=src_own)]
        first += [copy(1 + j, me, (*chip, pc), src=src_own) for j, chip in enumerate(chips)]
        for cp in first:
            cp.start()
        passed = [copy(4 + j, (*chip, pc), sibling) for j, chip in enumerate(chips)]
        for j, chip in enumerate(chips):
            copy(1 + j, (*chip, pc), me).wait_recv()
            passed[j].start()
        copy(0, sibling, me).wait_recv()
        for j, chip in enumerate(chips):
            copy(4 + j, (*chip, 1 - pc), me).wait_recv()
        for cp in first + passed:
            cp.wait_send()
        mine.wait()

    return pl.pallas_call(
        body, name=name,
        out_shape=jax.ShapeDtypeStruct((N_DEV,) + shape, x.dtype),
        in_specs=[pl.BlockSpec(memory_space=pl.ANY)],
        out_specs=pl.BlockSpec(memory_space=pl.ANY),
        scratch_shapes=[pltpu.SemaphoreType.DMA((7,)), pltpu.SemaphoreType.DMA((7,)), pltpu.SemaphoreType.DMA],
    )(x)


class _Exchange:
    def __init__(self, ins, out_shapes, n, build, aliases=None):
        self.ins, self.out_shapes, self.n, self.build = tuple(ins), tuple(out_shapes), n, build
        self.aliases = dict(aliases or {})

    def _descriptors(self, in_refs, out_refs, send_sems, recv_sems):
        sends, recvs = [], []
        for k, (src, dst, peer, landing) in enumerate(self.build(in_refs, out_refs)):
            sends.append(pltpu.make_async_remote_copy(
                src_ref=src, dst_ref=dst, send_sem=send_sems.at[k], recv_sem=recv_sems.at[k],
                device_id=peer, device_id_type=MESH))
            recvs.append(pltpu.make_async_remote_copy(
                src_ref=src, dst_ref=landing, send_sem=send_sems.at[k], recv_sem=recv_sems.at[k],
                device_id=peer, device_id_type=MESH))
        return sends, recvs

    def start(self, *refs):
        for cp in self._descriptors(*refs)[0]:
            cp.start()

    def finish(self, *refs):
        sends, recvs = self._descriptors(*refs)
        for cp in recvs:
            cp.wait_recv()
        for cp in sends:
            cp.wait_send()


ANY_SPEC = pl.BlockSpec(memory_space=pl.ANY)


def _hosted_call(body, *, name, grid, in_specs, out_specs, out_shape, args, scratch_shapes=(), side=None,
                 num_scalar_prefetch=0, io_aliases=None):
    in_specs, out_specs, out_shape = list(in_specs), list(out_specs), list(out_shape)
    n_in, n_out = len(in_specs) + num_scalar_prefetch, len(out_specs)
    kernel_body = body
    aliases = dict(io_aliases or {})
    if side is not None:
        s_in, s_out = len(side.ins), len(side.out_shapes)
        aliases.update({n_in + i: n_out + o for i, o in side.aliases.items()})

        def kernel_body(*refs):
            ins, s_ins = refs[:n_in], refs[n_in:n_in + s_in]
            outs = refs[n_in + s_in:n_in + s_in + n_out]
            s_outs = refs[n_in + s_in + n_out:n_in + s_in + n_out + s_out]
            scratch, sems = refs[n_in + s_in + n_out + s_out:-2], refs[-2:]
            first = functools.reduce(jnp.logical_and, [pl.program_id(a) == 0 for a in range(len(grid))])
            last = functools.reduce(jnp.logical_and, [pl.program_id(a) == g - 1 for a, g in enumerate(grid)])

            @pl.when(first)
            def _():
                side.start(s_ins, s_outs, *sems)

            body(*ins, *outs, *scratch)

            @pl.when(last)
            def _():
                side.finish(s_ins, s_outs, *sems)

        in_specs += [ANY_SPEC] * s_in
        out_specs += [ANY_SPEC] * s_out
        out_shape += list(side.out_shapes)
        scratch_shapes = list(scratch_shapes) + [pltpu.SemaphoreType.DMA((side.n,)),
                                                 pltpu.SemaphoreType.DMA((side.n,))]
        args = tuple(args) + side.ins
    if num_scalar_prefetch:
        grid_spec = pltpu.PrefetchScalarGridSpec(num_scalar_prefetch=num_scalar_prefetch, grid=grid,
                                                 in_specs=in_specs, out_specs=out_specs,
                                                 scratch_shapes=list(scratch_shapes))
        outs = pl.pallas_call(kernel_body, name=name, grid_spec=grid_spec, out_shape=out_shape,
                              input_output_aliases=aliases, compiler_params=_cparams())(*args)
    else:
        outs = pl.pallas_call(kernel_body, name=name, grid=grid, in_specs=in_specs, out_specs=out_specs,
                              out_shape=out_shape, scratch_shapes=list(scratch_shapes),
                              input_output_aliases=aliases, compiler_params=_cparams())(*args)
    return tuple(outs[:n_out]), tuple(outs[n_out:])


def _run_exchange(ex, *, name):
    s_in = len(ex.ins)

    def body(*refs):
        ins, outs, sems = refs[:s_in], refs[s_in:-2], refs[-2:]
        ex.start(ins, outs, *sems)
        ex.finish(ins, outs, *sems)

    outs = pl.pallas_call(
        body, name=name, out_shape=list(ex.out_shapes),
        in_specs=[ANY_SPEC] * s_in, out_specs=[ANY_SPEC] * len(ex.out_shapes),
        scratch_shapes=[pltpu.SemaphoreType.DMA((ex.n,)), pltpu.SemaphoreType.DMA((ex.n,))],
        input_output_aliases=ex.aliases,
    )(*ex.ins)
    return tuple(outs)


def _both(a, b):
    na, oa = len(a.ins), len(a.out_shapes)

    def build(ins, outs):
        return a.build(ins[:na], outs[:oa]) + b.build(ins[na:], outs[oa:])

    aliases = dict(a.aliases)
    aliases.update({na + i: oa + o for i, o in b.aliases.items()})
    return _Exchange(a.ins + b.ins, a.out_shapes + b.out_shapes, a.n + b.n, build, aliases)


def _other_chips(px, py):
    return [(px, 1 - py), (1 - px, py), (1 - px, 1 - py)]


def _gather_spread(w_flat, halves=True):
    rows, w = w_flat.shape
    hr = rows // 2 if halves else rows

    def build(ins, outs):
        px, py, pc = _position()
        mine = ins[0].at[pl.ds(pc * hr, hr)] if halves else ins[0]
        me = 4 * px + 2 * py + pc
        plan = [((px, py, 1 - pc), me ^ 1)]
        plan += [((qx, qy, pc), 4 * qx + 2 * qy + pc) for qx, qy in _other_chips(px, py)]
        return [(mine, outs[0].at[me], peer, outs[0].at[their]) for peer, their in plan]

    return _Exchange([w_flat], [jax.ShapeDtypeStruct((N_DEV, hr, w), w_flat.dtype)], 4, build)


def _gather_pass_on(gath):
    def build(ins, outs):
        px, py, pc = _position()
        out = []
        for qx, qy in _other_chips(px, py):
            blk = 4 * qx + 2 * qy + pc
            out.append((outs[0].at[blk], outs[0].at[blk], (px, py, 1 - pc), outs[0].at[blk ^ 1]))
        return out

    return _Exchange([gath], [jax.ShapeDtypeStruct(gath.shape, gath.dtype)], 3, build, aliases={0: 0})


def _rs_halves(g):
    n, rows, w = g.shape
    hr = rows // 2

    def build(ins, outs):
        px, py, pc = _position()
        return [(ins[0].at[:, pl.ds((1 - pc) * hr, hr), :], outs[0], (px, py, 1 - pc), outs[0])]

    return _Exchange([g], [jax.ShapeDtypeStruct((n, hr, w), g.dtype)], 1, build)


def _rs_chips(sb):
    def build(ins, outs):
        px, py, pc = _position()
        return [(ins[0].at[j], outs[0].at[j], (qx, qy, pc), outs[0].at[j])
                for j, (qx, qy) in enumerate(_other_chips(px, py))]

    return _Exchange([sb], [jax.ShapeDtypeStruct(sb.shape, sb.dtype)], 3, build)


def _rs_complete(buf):
    def build(ins, outs):
        px, py, pc = _position()
        return [(outs[0].at[pc], outs[0].at[pc], (px, py, 1 - pc), outs[0].at[1 - pc])]

    return _Exchange([buf], [jax.ShapeDtypeStruct(buf.shape, buf.dtype)], 1, build, aliases={0: 0})


def _rs_partial(g, recv, ids, *, name, tr=128):
    _, rows, w = g.shape
    hr = rows // 2
    nb = hr // tr

    def body(ids_ref, g_ref, r_ref, o_ref):
        o_ref[0] = (g_ref[0] + r_ref[0]).astype(BF16)

    grid_spec = pltpu.PrefetchScalarGridSpec(
        num_scalar_prefetch=1, grid=(3, nb),
        in_specs=[pl.BlockSpec((1, tr, w), lambda j, i, ids: (ids[1] ^ (j + 1), ids[0] * nb + i, 0)),
                  pl.BlockSpec((1, tr, w), lambda j, i, ids: (ids[1] ^ (j + 1), i, 0))],
        out_specs=pl.BlockSpec((1, tr, w), lambda j, i, ids: (j, i, 0)))
    return pl.pallas_call(
        body, name=name, grid_spec=grid_spec,
        out_shape=jax.ShapeDtypeStruct((3, hr, w), BF16),
        compiler_params=_cparams(),
    )(ids, g, recv)


def _rs_total(g, recv, got, ids, *, name, tr=128):
    _, rows, w = g.shape
    hr = rows // 2
    nb = hr // tr

    def body(ids_ref, g_ref, r_ref, got_ref, o_ref):
        acc = g_ref[0] + r_ref[0]
        for j in range(3):
            acc = acc + got_ref[j].astype(F32)
        o_ref[0] = acc

    grid_spec = pltpu.PrefetchScalarGridSpec(
        num_scalar_prefetch=1, grid=(nb,),
        in_specs=[pl.BlockSpec((1, tr, w), lambda i, ids: (ids[1], ids[0] * nb + i, 0)),
                  pl.BlockSpec((1, tr, w), lambda i, ids: (ids[1], i, 0)),
                  pl.BlockSpec((3, tr, w), lambda i, ids: (0, i, 0))],
        out_specs=pl.BlockSpec((1, tr, w), lambda i, ids: (ids[0], i, 0)))
    return pl.pallas_call(
        body, name=name, grid_spec=grid_spec,
        out_shape=jax.ShapeDtypeStruct((2, hr, w), F32),
        compiler_params=_cparams(),
    )(ids, g, recv, got)


class _ReduceScatter:
    def __init__(self, g, ids, tag):
        self.g, self.ids, self.tag, self.stage, self.result = g, ids, tag, 0, None

    def next_exchange(self):
        if self.stage == 0:
            return _rs_halves(self.g)
        if self.stage == 1:
            return _rs_chips(self.sb)
        return _rs_complete(self.buf)

    def done(self, outs):
        if self.stage == 0:
            self.recv = outs[0]
            hr = self.recv.shape[1]
            self.tr = max(t for t in range(16, 513, 16) if hr % t == 0)
            self.sb = _rs_partial(self.g, self.recv, self.ids, name=f"{self.tag}_partial", tr=self.tr)
        elif self.stage == 1:
            self.buf = _rs_total(self.g, self.recv, outs[0], self.ids, name=f"{self.tag}_total", tr=self.tr)
        else:
            _, hr, w = outs[0].shape
            self.result = outs[0].reshape(2 * hr, w)
        self.stage += 1

    def finish_alone(self):
        names = ("halves", "chips", "complete")
        while self.stage < 3:
            self.done(_run_exchange(self.next_exchange(), name=f"{self.tag}_{names[self.stage]}"))
        return self.result


def _flat_rows():
    used = sum(r for _, r in FSDP_SECTIONS)
    return used, -(-used // ROW_ALIGN) * ROW_ALIGN


def _cols_to_chunks(full):
    rows, cols = full.shape
    t = full.reshape(rows, N_CHIPS, cols // N_CHIPS).transpose(1, 0, 2)
    return t.reshape(N_CHIPS, -1, FLAT_W)


def _chunks_to_cols(chunks, rows, cols):
    return chunks.reshape(N_CHIPS, rows, cols // N_CHIPS).transpose(1, 0, 2).reshape(rows, cols)


def _pad_heads(w, real):
    lead = w.shape[:-1]
    t = w.reshape(lead + (HEADS, real))
    t = jnp.pad(t, [(0, 0)] * len(lead) + [(0, 0), (0, HEAD_PAD - real)])
    return t.reshape(lead + (HEADS * HEAD_PAD,))


def _unpad_heads(w, real):
    lead = w.shape[:-1]
    return w.reshape(lead + (HEADS, HEAD_PAD))[..., :real].reshape(lead + (HEADS * real,))


def _pad_value_lanes(w, axis):
    w = jnp.moveaxis(w, axis, -1)
    lead = w.shape[:-1]
    t = w.reshape(lead + (HEADS, 64))
    t = jnp.pad(t, [(0, 0)] * len(lead) + [(0, 0), (HEAD_PAD - 64, 0)])
    return jnp.moveaxis(t.reshape(lead + (HEADS * HEAD_PAD,)), -1, axis)


def _unpad_value_lanes(w, axis):
    w = jnp.moveaxis(w, axis, -1)
    lead = w.shape[:-1]
    t = w.reshape(lead + (HEADS, HEAD_PAD))[..., HEAD_PAD - 64:]
    return jnp.moveaxis(t.reshape(lead + (HEADS * 64,)), -1, axis)


def _pad_w_in_t(wt):
    z = jnp.zeros((NOPE, wt.shape[1]), wt.dtype)
    z2 = jnp.zeros((HEAD_PAD - NOPE - ROPE, wt.shape[1]), wt.dtype)
    return jnp.concatenate([wt[:1408], z, wt[1408:], z2], axis=0)


def _unpad_w_in_t(wt):
    return jnp.concatenate([wt[:1408], wt[1408 + NOPE:1408 + NOPE + ROPE]], axis=0)


def _rope_tables(positions):
    freqs = ROPE_THETA ** (-jnp.arange(0, ROPE, 2, dtype=F32) / ROPE)
    ang = positions.astype(F32)[..., None] * freqs
    cos, sin = jnp.cos(ang), jnp.sin(ang)
    lead = cos.shape[:-1]
    ones = jnp.ones(lead + (NOPE,), F32)
    zeros_n = jnp.zeros(lead + (NOPE,), F32)
    zeros_p = jnp.zeros(lead + (HEAD_PAD - NOPE - ROPE,), F32)
    ctab = jnp.concatenate([ones, cos, cos, zeros_p], axis=-1)
    stab = jnp.concatenate([zeros_n, -sin, sin, zeros_p], axis=-1)
    return ctab, stab


def _mix_weights(full):
    return dict(
        w_in_t=_pad_w_in_t(full["w_in_t"]),
        w_uq=_pad_heads(full["mla_w_uq"], NOPE + ROPE),
        w_ukv=full["mla_w_ukv"],
        w_out=jnp.concatenate([_pad_value_lanes(full["w_out"][D_GMLP:], 0), full["w_out"][:D_GMLP]], axis=0),
    )


def _small_weights(p, l):
    ws = p["gmlp_ws"][l]
    tril = jnp.tril(jnp.ones((CHUNK, CHUNK), bool))
    bs = p["gmlp_bs"][l]
    bexp = jnp.repeat(bs.reshape(GROUPS // 2, 2, CHUNK).transpose(0, 2, 1), GROUP_DIM, axis=2)
    return dict(
        ws=ws,
        wst=jnp.where(tril[None], ws, 0.0).transpose(0, 2, 1).astype(BF16),
        bexp=bexp,
        g_mix=p["norm_mix_g"][l][None],
        g_ffn=p["norm_ffn_g"][l][None],
        g_q=p["mla_q_norm_g"][l][None],
        g_kv=p["mla_kv_norm_g"][l][None],
        g_og=p["out_norm_gmlp_g"][l][None],
        g_oa=_pad_value_lanes(p["out_norm_mla_g"][l], 0)[None],
    )


def _local_step(x3, target3, positions, mods, final_g, plan):
    bsz, seq, d = x3.shape
    tok = bsz * seq
    tmt = min(512, seq)
    tmk = min(1024, seq)
    tmw = min(2048, tok)
    chunk = (None, None, FLAT_W, FLAT_W)
    chunk2 = (2, None, FLAT_W, FLAT_W)
    ff_grad_shape = (N_CHIPS, 2 * FLAT_W, FLAT_W)
    ctab, stab = _rope_tables(positions)
    lw = [None] * DEPTH

    def flat(t):
        return t.reshape(tok, t.shape[-1])

    def cube(t):
        return t.reshape(bsz, seq, t.shape[-1])

    def carrying(l, tag, fn, *args, **kw):
        side = plan.host(l, tag)
        if side is None:
            return fn(*args, **kw)
        res, side_outs = fn(*args, side=side, **kw)
        plan.hosted(l, tag, side_outs)
        return res

    saved = []
    x = x3
    for l in range(DEPTH):
        lw[l] = plan.layer(l)
        w, mod = lw[l], mods[l]
        if l == 0:
            h1 = carrying(l, "fwd_normmod1", _normmod_fwd, x, w["g_mix"], mod, SHIFT1, SCALE1,
                          name=f"l{l}_normmod1")
        else:
            h1 = h1_next
        z = cube(_mm(flat(h1), w["w_in_t"], dims="nt", name=f"l{l}_w_in", tm=tmk, tn=D_IN_PAD, tk=d,
                     out_dtypes=(BF16,)))
        yg = _gmlp_fwd(z, w["ws"], w["bexp"], w["g_og"], name=f"l{l}_gmlp_fwd")
        q, kv, kp = _mla_prep_fwd(z, w["g_q"], w["g_kv"], w["w_uq"], w["w_ukv"], ctab, stab, name=f"l{l}_mla_prep")
        o, lse = carrying(l, "fwd_attn", _attn_fwd, q, kv, kp, name=f"l{l}_attn_fwd")
        y = _onorm_fwd(o, yg, w["g_oa"], name=f"l{l}_onorm_fwd")

        def normmod(xv, gv, gm, shift_row, scale_row):
            m = gm[0]
            return _rms_fwd(xv, gv, d) * (1.0 + m[scale_row:scale_row + 1, :]) + m[shift_row:shift_row + 1, :]

        def out_epi(po, xv, gm, gf):
            x_new = xv + gm[0][GATE1:GATE1 + 1, :] * po
            return po, x_new, normmod(x_new, gf, gm, SHIFT2, SCALE2)

        vec_spec = pl.BlockSpec((1, d), lambda i, j, k: (0, j))
        po, x_mid, h2 = carrying(l, "fwd_out_a", _mm, flat(y), w["w_out"], dims="nn", name=f"l{l}_w_out",
                                 tm=tmt, tn=d, tk=y.shape[-1], out_dtypes=(BF16, F32, BF16), epilogue=out_epi,
                                 extras=(flat(x), mod, w["g_ffn"]),
                                 extra_specs=(None, _mod_spec(tmt, d, seq), vec_spec))
        x_mid, h2 = cube(x_mid), cube(h2)

        def act_epi(acc):
            r = jnp.maximum(acc, 0.0)
            return (r * r,)

        r = carrying(l, "fwd_ff1", _mm, flat(h2), w["ff"], dims="nn", name=f"l{l}_w_ff1", tm=tmw, tn=FLAT_W,
                     tk=d, out_dtypes=(BF16,), epilogue=act_epi, weights_outer=True, n=D_FF,
                     b_block=(chunk, lambda i, j, k: (j, 0, 0, 0)))

        more = l + 1 < DEPTH

        def ff2_epi(acc, xv, gm, *nxt):
            x_new = xv + gm[0][GATE2:GATE2 + 1, :] * acc
            return (acc, x_new) + ((normmod(x_new, nxt[1], nxt[0], SHIFT1, SCALE1),) if more else ())

        mod_spec = _mod_spec(tmt, d, seq)
        outs = carrying(l, "fwd_ff2", _mm, r, w["ff"], dims="nn", name=f"l{l}_w_ff2", tm=tmt, tn=d, tk=2 * FLAT_W,
                        out_dtypes=(BF16, F32) + ((BF16,) if more else ()), epilogue=ff2_epi,
                        extras=(flat(x_mid), mod) + ((mods[l + 1], plan.layer(l + 1)["g_mix"]) if more else ()),
                        extra_specs=(None, mod_spec) + ((mod_spec, vec_spec) if more else ()), n=d,
                        b_block=(chunk2, lambda i, j, k: (k, 1, 0, 0)))
        f, x_out = outs[0], outs[1]
        h1_next = cube(outs[2]) if more else None
        saved.append(dict(x_in=x, h1=h1, z=z, q=q, kv=kv, kp=kp, o=o, lse=lse, y=y, po=cube(po),
                          x_mid=x_mid, h2=h2, r=r, f=cube(f)))
        x = cube(x_out)

    grads = [dict() for _ in range(DEPTH)]
    dmods = [None] * DEPTH
    top = DEPTH - 1
    node = _resnode_bwd(x, final_g[None], name="final_loss_bwd", target3=target3,
                        branch3=saved[top]["f"], mod_gate=mods[top], gate_row=GATE2)
    loss_part = node["loss"][0, 0]
    d_final_g = node["dg"][0]
    plan.scalars(loss_part, d_final_g)
    for l in range(DEPTH - 1, -1, -1):
        w, mod, s = lw[l], mods[l], saved[l]
        dx_out, dfb, dgate2 = node["dx"], flat(node["dbr"]), node["dgate"][:, 0]

        def dact_epi(acc, rv):
            return (acc * (2.0 * jnp.sqrt(rv.astype(F32))),)

        da = carrying(l, "bwd_d_r", _mm, dfb, w["ff"], dims="nt", name=f"l{l}_d_r", tm=tmw, tn=FLAT_W, tk=d,
                      out_dtypes=(BF16,), epilogue=dact_epi, extras=(s["r"],), weights_outer=True, n=D_FF,
                      b_block=(chunk, lambda i, j, k: (j, 1, 0, 0)))
        g_ff = carrying(l, "bwd_dw_ff2", _mm, s["r"], dfb, dims="tn", name=f"l{l}_dw_ff2", tm=FLAT_W, tn=d,
                        tk=2048, out_into=(ff_grad_shape, (None, FLAT_W, FLAT_W), lambda i, j, k: (i, 1, 0), None))
        g_ff = carrying(l, "bwd_dw_ff1", _mm, flat(s["h2"]), da, dims="tn", name=f"l{l}_dw_ff1", tm=d, tn=FLAT_W,
                        tk=2048, out_into=(ff_grad_shape, (None, FLAT_W, FLAT_W), lambda i, j, k: (j, 0, 0), g_ff))
        plan.ff_grads(l, g_ff)
        dh2 = carrying(l, "bwd_d_h2", _mm, da, w["ff"], dims="nt", name=f"l{l}_d_h2", tm=tmk, tn=d, tk=2 * FLAT_W,
                       n=d, b_block=(chunk2, lambda i, j, k: (k, 0, 0, 0)), out_dtypes=(BF16,))
        node = carrying(l, "bwd_resnode_ffn", _resnode_bwd, s["x_mid"], w["g_ffn"], name=f"l{l}_resnode_ffn",
                        dh3=cube(dh2), dres3=dx_out, mod_nm=mod, rows=(SHIFT2, SCALE2), branch3=s["po"],
                        mod_gate=mod, gate_row=GATE1)
        grads[l]["norm_ffn_g"] = node["dg"][0]
        dshift2, dscale2 = node["dnm"][:, 0], node["dnm"][:, 1]
        dx_mid, dpo, dgate1 = node["dx"], flat(node["dbr"]), node["dgate"][:, 0]

        wy = s["y"].shape[-1]
        dy = cube(carrying(l, "bwd_d_y", _mm, dpo, w["w_out"], dims="nt", name=f"l{l}_d_y", tm=tmk, tn=wy, tk=d,
                           out_dtypes=(BF16,)))
        dw_out = _mm(flat(s["y"]), dpo, dims="tn", name=f"l{l}_dw_out", tm=wy // 3, tn=d, tk=2048)
        hw = HEADS * HEAD_PAD
        grads[l]["w_out"] = jnp.concatenate([dw_out[hw:], _unpad_value_lanes(dw_out[:hw], 0)], axis=0)

        dz, dws, dbs, dg_og = _gmlp_bwd(s["z"], dy, w["ws"], w["wst"], w["bexp"], w["g_og"],
                                        name=f"l{l}_gmlp_bwd", dy_col=hw // D_GMLP)
        grads[l]["gmlp_ws"], grads[l]["gmlp_bs"], grads[l]["out_norm_gmlp_g"] = dws, dbs, dg_og[0]

        do, dl, dg_oa = _onorm_bwd(s["o"], dy, w["g_oa"], name=f"l{l}_onorm_bwd")
        grads[l]["out_norm_mla_g"] = _unpad_value_lanes(dg_oa[0], 0)
        plan.small_ready(l, grads[l])
        dq, dk, dv = carrying(l, "bwd_attn_dkv", _attn_bwd, s["q"], s["kv"], s["kp"], do, s["lse"], dl,
                              name=f"l{l}_attn_bwd")
        dz, cq, dqb, ckv, dkvb, dg_q, dg_kv = _mla_prep_bwd(
            s["z"], dz, dq, dk, dv, w["g_q"], w["g_kv"], w["w_uq"], w["w_ukv"], ctab, stab,
            name=f"l{l}_mla_prep_bwd")
        grads[l]["mla_q_norm_g"], grads[l]["mla_kv_norm_g"] = dg_q[0], dg_kv[0]
        dw_uq = carrying(l, "bwd_dw_uq", _mm, flat(cq), flat(dqb), dims="tn", name=f"l{l}_dw_uq", tm=Q_RANK,
                         tn=1024, tk=4096)
        grads[l]["mla_w_uq"] = _unpad_heads(dw_uq, NOPE + ROPE)
        grads[l]["w_in_t"] = _unpad_w_in_t(carrying(l, "bwd_dw_in", _mm, flat(dz), flat(s["h1"]), dims="tn",
                                                    name=f"l{l}_dw_in", tm=D_IN_PAD // 2, tn=d, tk=2048))
        grads[l]["mla_w_ukv"] = carrying(l, "bwd_dw_ukv", _mm, flat(ckv), flat(dkvb), dims="tn", name=f"l{l}_dw_ukv",
                                         tm=KV_RANK, tn=1024, tk=4096)
        plan.layer_grads(l, grads[l])
        dh1 = carrying(l, "bwd_d_h1", _mm, flat(dz), w["w_in_t"], dims="nn", name=f"l{l}_d_h1", tm=tmk, tn=d,
                       tk=D_IN_PAD, out_dtypes=(BF16,))
        below = dict(branch3=saved[l - 1]["f"], mod_gate=mods[l - 1], gate_row=GATE2) if l > 0 else {}
        node = carrying(l, "bwd_resnode_mix", _resnode_bwd, s["x_in"], w["g_mix"], name=f"l{l}_resnode_mix",
                        dh3=cube(dh1), dres3=dx_mid, mod_nm=mod, rows=(SHIFT1, SCALE1), **below)
        grads[l]["norm_mix_g"] = node["dg"][0]
        dshift1, dscale1 = node["dnm"][:, 0], node["dnm"][:, 1]
        dmods[l] = jnp.stack([dshift1, dscale1, dgate1, dshift2, dscale2, dgate2], axis=1)
    return node["dx"], dmods


W_NAMES = ("w_ada", "b_ada", "norm_mix_g", "w_in", "gmlp_ws", "gmlp_bs", "mla_q_norm_g", "mla_kv_norm_g",
           "mla_w_uq", "mla_w_ukv", "out_norm_gmlp_g", "out_norm_mla_g", "w_out", "norm_ffn_g", "w_ff1", "w_ff2",
           "final_norm_g")
FLAT_KEY = {"w_in": "w_in", "w_uq": "mla_w_uq", "w_ukv": "mla_w_ukv", "w_out": "w_out", "w_ff1": "w_ff1",
            "w_ff2": "w_ff2"}
COL_SHARDED = ("w_in", "w_uq", "w_ukv", "w_ff1")
FULL_SHAPE = {"w_in": (D_MODEL, D_IN), "w_uq": (Q_RANK, HEADS * (NOPE + ROPE)), "w_ukv": (KV_RANK, HEADS * 128),
              "w_out": (D_MODEL, D_MODEL)}
SMALL_LAYER_NAMES = ("gmlp_ws", "gmlp_bs", "out_norm_gmlp_g", "out_norm_mla_g", "norm_ffn_g")
LATE_SMALL_NAMES = ("norm_mix_g", "mla_q_norm_g", "mla_kv_norm_g")


def _silu(v):
    return v * (1.0 / (1.0 + jnp.exp(-v)))


class _CommPlan:
    FWD = {"fwd_attn": ("ff", 0, "spread"), "fwd_out_a": ("ff", 0, "pass"),
           "fwd_ff1": ("mix", 1, "spread"), "fwd_ff2": ("mix", 1, "pass")}
    BWD = {"bwd_d_r": ("mix", 1), "bwd_dw_ff2": ("mix", 1),
           "bwd_d_h2": ("ff", 0), "bwd_attn_dkv": ("ff", 0), "bwd_dw_uq": ("ff", 0)}
    BWD_ALSO = {"bwd_d_h2": ("mix", 1)}
    BWD_LAST = {"bwd_d_h1": ("mix", 0), "bwd_resnode_mix": ("mix", 0)}
    SMALL = {"bwd_attn_dkv": "spread", "bwd_dw_uq": "pass"}

    def __init__(self, weights, ids, dev, core):
        self.weights, self.ids, self.dev, self.core = weights, ids, dev, core
        self.used, self.rows = _flat_rows()
        self.flat = {("mix", l): self._flat_mix(l) for l in range(DEPTH)}
        self.flat.update({("ff", l): jnp.concatenate([weights["w_ff1"][l], weights["w_ff2"][l]], axis=0).astype(BF16)
                          for l in range(DEPTH)})
        self.lw, self.rs, self.grads, self.spread = {}, {}, {}, {}
        self.small_vec, self.small_sum, self.small_spread, self.extra = {}, {}, None, {}
        self.lw = {l: _small_weights(weights, l) for l in range(DEPTH)}

    def _flat_mix(self, l):
        pieces = []
        for nm, _ in FSDP_SECTIONS:
            shard = self.weights[FLAT_KEY[nm]][l]
            pieces.append(shard.T if nm == "w_in" else shard.reshape(-1, FLAT_W))
        pieces.append(jnp.zeros((self.rows - self.used, FLAT_W), F32))
        return jnp.concatenate(pieces, axis=0).astype(BF16)

    def _arrived(self, group, l, gath):
        flat = self.flat[group, l]
        hr = flat.shape[0] // 2
        mine = lax.dynamic_slice(flat, (self.core * hr, 0), (hr, FLAT_W))
        gath = lax.dynamic_update_slice(gath, mine[None], (self.dev, 0, 0))
        if group == "ff":
            self.lw[l]["ff"] = gath.reshape(N_CHIPS, 2, hr, FLAT_W)
            return
        w_gath = gath.reshape(N_CHIPS, self.rows, FLAT_W)
        full, off = {}, 0
        for nm, nrows in FSDP_SECTIONS:
            sec = w_gath[:, off:off + nrows]
            off += nrows
            rows, cols = FULL_SHAPE[nm]
            if nm == "w_in":
                full["w_in_t"] = sec.reshape(cols, rows)
            else:
                full[FLAT_KEY[nm]] = (_chunks_to_cols(sec, rows, cols) if nm in COL_SHARDED
                                      else sec.reshape(rows, cols))
        self.lw[l].update(_mix_weights(full))

    def layer(self, l):
        return self.lw[l]

    def host(self, l, tag):
        if tag == "fwd_normmod1":
            return _gather_spread(self.flat["mix", 0]) if l == 0 else None
        if tag in self.FWD:
            group, ahead, what = self.FWD[tag]
            if l + ahead >= DEPTH:
                return None
            return _gather_spread(self.flat[group, l + ahead]) if what == "spread" else _gather_pass_on(self.spread[group])
        ex = None
        for rs in self._rs_for(l, tag):
            ex = rs.next_exchange() if ex is None else _both(ex, rs.next_exchange())
        if tag in self.SMALL:
            small = (_gather_spread(self.small_vec[l], halves=False) if self.SMALL[tag] == "spread"
                     else _gather_pass_on(self.small_spread))
            ex = small if ex is None else _both(ex, small)
        return ex

    def _rs_for(self, l, tag):
        found = []
        if tag in self.BWD_LAST and l == 0:
            found.append(self.rs.get(self.BWD_LAST[tag]))
        for table in (self.BWD, self.BWD_ALSO):
            if tag in table:
                group, ahead = table[tag]
                found.append(self.rs.get((group, l + ahead)))
        return [rs for rs in found if rs is not None and rs.stage <= 2]

    def hosted(self, l, tag, outs):
        if tag == "fwd_normmod1":
            self._arrived("mix", 0, _run_exchange(_gather_pass_on(outs[0]), name="l0_mix_gather_pass_on")[0])
        elif tag in self.FWD:
            group, ahead, what = self.FWD[tag]
            if what == "spread":
                self.spread[group] = outs[0]
            else:
                self._arrived(group, l + ahead, outs[0])
        else:
            for rs in self._rs_for(l, tag):
                rs.done(outs[:1])
                outs = outs[1:]
            if tag in self.SMALL:
                if self.SMALL[tag] == "spread":
                    self.small_spread = outs[0]
                else:
                    self._small_arrived(l, outs[0])

    def ff_grads(self, l, g_ff):
        self.rs["ff", l] = _ReduceScatter(g_ff, self.ids, f"l{l}_ff_rs")

    def layer_grads(self, l, grads):
        self.grads[l] = grads
        pieces = []
        for nm, nrows in FSDP_SECTIONS:
            if nm == "w_in":
                pieces.append(grads["w_in_t"].reshape(N_CHIPS, nrows, FLAT_W))
                continue
            g = grads[FLAT_KEY[nm]]
            pieces.append(_cols_to_chunks(g) if nm in COL_SHARDED else g.reshape(N_CHIPS, nrows, FLAT_W))
        pieces.append(jnp.zeros((N_CHIPS, self.rows - self.used, FLAT_W), F32))
        self.rs["mix", l] = _ReduceScatter(jnp.concatenate(pieces, axis=1), self.ids, f"l{l}_mix_rs")

    def scalars(self, loss_part, d_final_g):
        self.extra = {0: [loss_part[None]]}
        self.extra.setdefault(DEPTH - 1, []).insert(0, d_final_g)

    def small_ready(self, l, grads):
        parts = [grads[nm].reshape(-1) for nm in SMALL_LAYER_NAMES] + self.extra.get(l, [])
        vec = jnp.concatenate(parts)
        rows = -(-vec.shape[0] // (8 * FLAT_W)) * 8
        self.small_vec[l] = jnp.pad(vec, (0, rows * FLAT_W - vec.shape[0])).reshape(rows, FLAT_W)

    def _small_arrived(self, l, gath):
        gath = lax.dynamic_update_slice(gath, self.small_vec[l][None], (self.dev, 0, 0))
        self.small_sum[l] = _sum_leading(gath, name=f"l{l}_small_sum").reshape(-1)

    def finish(self, dmod):
        late = jnp.concatenate([jnp.stack([self.grads[l][nm] for l in range(DEPTH)], axis=0).reshape(-1)
                                for nm in LATE_SMALL_NAMES])
        head = -(-late.shape[0] // (8 * FLAT_W)) * 8
        late = jnp.pad(late, (0, head * FLAT_W - late.shape[0])).reshape(head, FLAT_W)
        vec = jnp.concatenate([late, dmod], axis=0)
        rs = self.rs["mix", 0]
        while rs.stage < 2:
            rs.done(_run_exchange(rs.next_exchange(), name=f"l0_mix_rs_stage{rs.stage}"))
        outs = _run_exchange(_both(rs.next_exchange(), _gather_spread(vec, halves=False)), name="final_spread")
        rs.done(outs[:1])
        (gath,) = _run_exchange(_gather_pass_on(outs[1]), name="final_pass_on")
        gath = lax.dynamic_update_slice(gath, vec[None], (self.dev, 0, 0))
        late_sum = _sum_leading(gath[:, :head], name="late_small_sum").reshape(-1)
        loss, res = self._small_grads(late_sum)
        return loss, res, gath[:, head:]

    def _small_grads(self, late):
        out = {nm: [] for nm in SMALL_LAYER_NAMES}
        for l in range(DEPTH):
            off = 0
            for nm in SMALL_LAYER_NAMES:
                size = self.weights[nm][l].size
                out[nm].append(self.small_sum[l][off:off + size].reshape(self.weights[nm].shape[1:]))
                off += size
            if l == DEPTH - 1:
                final = self.small_sum[l][off:off + self.weights["final_norm_g"].size]
                off += final.shape[0]
            if l == 0:
                loss = self.small_sum[l][off]
        res = {nm: jnp.stack(parts, axis=0) for nm, parts in out.items()}
        res["final_norm_g"] = final
        off = 0
        for nm in LATE_SMALL_NAMES:
            size = self.weights[nm].size
            res[nm] = late[off:off + size].reshape(self.weights[nm].shape)
            off += size
        return loss, res

    def mix_grads(self):
        per = {FLAT_KEY[nm]: [] for nm, _ in FSDP_SECTIONS}
        for l in range(DEPTH):
            shard, off = self.rs["mix", l].result, 0
            for nm, nrows in FSDP_SECTIONS:
                key = FLAT_KEY[nm]
                sec = shard[off:off + nrows]
                per[key].append(sec.T if nm == "w_in" else sec.reshape(self.weights[key].shape[1:]))
                off += nrows
        return {key: jnp.stack(parts, axis=0) for key, parts in per.items()}

    def ff_shards(self):
        return [self.rs["ff", l].result for l in range(DEPTH)]


def kernel(x, c, positions, w_ada, b_ada, norm_mix_g, w_in, gmlp_ws, gmlp_bs, mla_q_norm_g, mla_kv_norm_g, mla_w_uq, mla_w_ukv, out_norm_gmlp_g, out_norm_mla_g, w_out, norm_ffn_g, w_ff1, w_ff2, final_norm_g, loss_target, m_w_ada, m_b_ada, m_norm_mix_g, m_w_in, m_gmlp_ws, m_gmlp_bs, m_mla_q_norm_g, m_mla_kv_norm_g, m_mla_w_uq, m_mla_w_ukv, m_out_norm_gmlp_g, m_out_norm_mla_g, m_w_out, m_norm_ffn_g, m_w_ff1, m_w_ff2, m_final_norm_g, v_w_ada, v_b_ada, v_norm_mix_g, v_w_in, v_gmlp_ws, v_gmlp_bs, v_mla_q_norm_g, v_mla_kv_norm_g, v_mla_w_uq, v_mla_w_ukv, v_out_norm_gmlp_g, v_out_norm_mla_g, v_w_out, v_norm_ffn_g, v_w_ff1, v_w_ff2, v_final_norm_g):
    weights = dict(w_ada=w_ada, b_ada=b_ada, norm_mix_g=norm_mix_g, w_in=w_in, gmlp_ws=gmlp_ws, gmlp_bs=gmlp_bs,
                   mla_q_norm_g=mla_q_norm_g, mla_kv_norm_g=mla_kv_norm_g, mla_w_uq=mla_w_uq, mla_w_ukv=mla_w_ukv,
                   out_norm_gmlp_g=out_norm_gmlp_g, out_norm_mla_g=out_norm_mla_g, w_out=w_out,
                   norm_ffn_g=norm_ffn_g, w_ff1=w_ff1, w_ff2=w_ff2, final_norm_g=final_norm_g)
    mom_m = dict(zip(W_NAMES, (m_w_ada, m_b_ada, m_norm_mix_g, m_w_in, m_gmlp_ws, m_gmlp_bs, m_mla_q_norm_g,
                               m_mla_kv_norm_g, m_mla_w_uq, m_mla_w_ukv, m_out_norm_gmlp_g, m_out_norm_mla_g,
                               m_w_out, m_norm_ffn_g, m_w_ff1, m_w_ff2, m_final_norm_g)))
    mom_v = dict(zip(W_NAMES, (v_w_ada, v_b_ada, v_norm_mix_g, v_w_in, v_gmlp_ws, v_gmlp_bs, v_mla_q_norm_g,
                               v_mla_kv_norm_g, v_mla_w_uq, v_mla_w_ukv, v_out_norm_gmlp_g, v_out_norm_mla_g,
                               v_w_out, v_norm_ffn_g, v_w_ff1, v_w_ff2, v_final_norm_g)))
    bsz, seq, d = x.shape
    px, py, pc = _position()
    chip = 2 * px + py
    dev = 2 * chip + pc
    ids = jnp.stack([pc, chip]).astype(jnp.int32)
    n_ex = N_DEV * bsz
    ada_cols = w_ada.shape[-1]

    c_all = _allgather8(c.reshape(bsz * d // 128, 128), name="gather_c").reshape(n_ex, d)
    mod_parts = []
    for l in range(DEPTH):
        bias = lax.dynamic_slice(b_ada[l], (chip * ada_cols,), (ada_cols,))[None]
        mod_parts.append(_mm(c_all, w_ada, dims="nn", name=f"l{l}_mod", tm=n_ex, tn=ada_cols, tk=d, n=ada_cols,
                             b_block=((None, d, ada_cols), lambda i, j, k, l=l: (l, k, j)),
                             epilogue=lambda acc, bv: (acc + bv,), extras=(bias,),
                             extra_specs=(pl.BlockSpec((1, ada_cols), lambda i, j, k: (0, j)),), a_fn=_silu))
    mod_g = _allgather8(jnp.concatenate(mod_parts, axis=0), name="gather_mod")
    mod_g = mod_g.reshape(N_CHIPS, 2, DEPTH, n_ex, ada_cols)[:, 0]
    mod_full = mod_g.transpose(1, 2, 0, 3).reshape(DEPTH, n_ex, N_CHIPS * ada_cols)
    mod_mine = lax.dynamic_slice(mod_full, (0, dev * bsz, 0), (DEPTH, bsz, N_MOD * d))
    mod_mine = jnp.pad(mod_mine.reshape(DEPTH, bsz, N_MOD, d), ((0, 0), (0, 0), (0, MOD_ROWS - N_MOD), (0, 0)))
    mods = [mod_mine[l] for l in range(DEPTH)]

    plan = _CommPlan(weights, ids, dev, pc)
    grad_x, dmods = _local_step(x, loss_target, positions, mods, final_norm_g, plan)

    dmod = jnp.stack(dmods, axis=1).reshape(bsz * DEPTH * N_MOD, d)
    loss, small, dmod_all = plan.finish(dmod)
    grad = plan.mix_grads()
    grad.update(small)
    dmod_all = dmod_all.reshape(n_ex, DEPTH, N_MOD * d)
    gw, gb = [], []
    for l in range(DEPTH):
        dm = dmod_all[:, l]
        dm_cols = lax.dynamic_slice(dm, (0, chip * ada_cols), (n_ex, ada_cols))
        gw.append(_mm(c_all, dm_cols, dims="tn", name=f"l{l}_dw_ada", tm=d, tn=ada_cols, tk=n_ex, a_fn=_silu,
                      out_into=(w_ada.shape, (None, d, ada_cols), lambda i, j, k, l=l: (l, i, j),
                                gw[-1] if gw else None)))
        gb.append(_sum_leading(dm.reshape(n_ex, N_MOD * d // FLAT_W, FLAT_W), name=f"l{l}_db_ada").reshape(-1))
    grad["w_ada"] = gw[-1]
    grad["b_ada"] = jnp.stack(gb, axis=0)

    delta, new_m, new_v = {}, {}, {}
    ff_bufs = plan.ff_shards()
    for nm, row_off in (("w_ff1", 0), ("w_ff2", FLAT_W)):
        grad[nm], delta[nm], new_m[nm], new_v[nm] = _adamw_layers(
            weights[nm], mom_m[nm], mom_v[nm], ff_bufs, row_off, name=f"adamw_{nm}")
    for nm in W_NAMES:
        if nm not in delta:
            delta[nm], new_m[nm], new_v[nm] = _adamw(weights[nm], grad[nm], mom_m[nm], mom_v[nm],
                                                     name=f"adamw_{nm}")
    return (loss, grad_x, *[grad[nm] for nm in W_NAMES], *[delta[nm] for nm in W_NAMES],
            *[new_m[nm] for nm in W_NAMES], *[new_v[nm] for nm in W_NAMES])
```

```python
import functools
import math

import jax
import jax.numpy as jnp
from jax import lax
from jax.experimental import pallas as pl
from jax.experimental.pallas import tpu as pltpu

F32 = jnp.float32
BF16 = jnp.bfloat16

D_MODEL = 1024
DEPTH = 2
D_GMLP = 512
GROUPS = 8
GROUP_DIM = 64
CHUNK = 128
HEADS = 8
NOPE = 64
ROPE = 32
HEAD_PAD = 128
Q_RANK = 256
KV_RANK = 128
D_FF = 4096
N_MOD = 6
MOD_ROWS = 8
EPS = 1e-6
ROPE_THETA = 10000.0
D_IN = 1440
D_IN_PAD = 1536
ATTN_SCALE = (NOPE + ROPE) ** -0.5
LOG2E = math.log2(math.e)
SCALE_LOG2 = ATTN_SCALE * LOG2E
N_CHIPS = 4
N_DEV = 8

ADAM_LR = 0.001
ADAM_B1 = 0.9
ADAM_B2 = 0.999
ADAM_EPS = 1e-08
ADAM_WD = 0.01
ADAM_STEP = 10

VMEM_LIMIT = 48 * 1024 * 1024
FLAT_W = 1024
ROW_ALIGN = 256

NN = (((1,), (0,)), ((), ()))
NT = (((1,), (1,)), ((), ()))
TN = (((0,), (0,)), ((), ()))
MESH = pl.DeviceIdType.MESH

SHIFT1, SCALE1, GATE1, SHIFT2, SCALE2, GATE2 = range(6)

FSDP_SECTIONS = (("w_out", 256), ("w_in", 360), ("w_uq", 48), ("w_ukv", 32))


def _cparams(vmem=VMEM_LIMIT):
    return pltpu.CompilerParams(vmem_limit_bytes=vmem)


def _dot(a, b, dims=NN):
    return lax.dot_general(a, b, dims, preferred_element_type=F32)


def _iota(shape, axis):
    return lax.broadcasted_iota(jnp.int32, shape, axis)


def _gelu(x):
    k = math.sqrt(2.0 / math.pi)
    return 0.5 * x * (1.0 + jnp.tanh(k * (x + 0.044715 * (x * x * x))))


def _gelu_and_grad(x):
    k = math.sqrt(2.0 / math.pi)
    x2 = x * x
    t = jnp.tanh(k * (x + 0.044715 * (x2 * x)))
    half = 0.5 * (1.0 + t)
    return x * half, half + 0.5 * x * (1.0 - t * t) * (k * (1.0 + 3.0 * 0.044715 * x2))


def _rms_fwd(x, g, n):
    r = lax.rsqrt(jnp.sum(x * x, axis=-1, keepdims=True) * (1.0 / n) + EPS)
    return x * r * g


def _rms_bwd(x, g, dy, n):
    r = lax.rsqrt(jnp.sum(x * x, axis=-1, keepdims=True) * (1.0 / n) + EPS)
    xh = x * r
    dxh = dy * g
    dx = r * (dxh - xh * (jnp.sum(dxh * xh, axis=-1, keepdims=True) * (1.0 / n)))
    dg = jnp.sum(dy * xh, axis=0, keepdims=True)
    return dx, dg


def _pick_rows(rows, limit):
    if rows <= limit:
        return rows
    for t in range(limit, 7, -8):
        if rows % t == 0:
            return t
    return rows


def _mm(a, b, *, dims, name, tm=512, tn=1024, tk=1024, out_dtypes=(F32,), epilogue=None,
        extras=(), extra_specs=(), a_fn=None, weights_outer=False, side=None, b_block=None, n=None,
        out_into=None):
    if dims == "tn":
        kk, m = a.shape
    else:
        m, kk = a.shape
    if n is None:
        n = b.shape[0] if dims == "nt" else b.shape[1]
    tm, tn, tk = min(tm, m), min(tn, n), min(tk, kk)
    assert m % tm == 0 and n % tn == 0 and kk % tk == 0, (name, a.shape, b.shape, tm, tn, tk)
    ni, nj, nk = m // tm, n // tn, kk // tk

    def spec(shape, pick):
        if weights_outer:
            return pl.BlockSpec(shape, lambda j, i, k: pick(i, j, k))
        return pl.BlockSpec(shape, pick)

    if dims == "tn":
        a_spec = spec((tk, tm), lambda i, j, k: (k, i))
    else:
        a_spec = spec((tm, tk), lambda i, j, k: (i, k))
    if b_block is not None:
        b_spec = spec(*b_block)
    elif dims == "nt":
        b_spec = spec((tn, tk), lambda i, j, k: (j, k))
    else:
        b_spec = spec((tk, tn), lambda i, j, k: (k, j))
    o_spec = spec((tm, tn), lambda i, j, k: (i, j))
    out_shape = [jax.ShapeDtypeStruct((m, n), dt) for dt in out_dtypes]
    out_specs = [o_spec] * len(out_dtypes)
    prev, io_aliases = (), {}
    if out_into is not None:
        full_shape, block, index, before = out_into
        assert len(out_dtypes) == 1 and not extras
        out_shape = [jax.ShapeDtypeStruct(full_shape, out_dtypes[0])]
        out_specs = [spec(block, index)]
        if before is not None:
            prev, io_aliases = (before,), {2: 0}
    assert not (weights_outer and extra_specs)
    dn = {"nn": NN, "nt": NT, "tn": TN}[dims]
    n_ex, n_out = len(extras), len(out_dtypes)
    e_specs = [o_spec if s is None else s for s in (tuple(extra_specs) + (None,) * n_ex)[:n_ex]]

    n_prev = len(prev)

    def body(*refs):
        a_ref, b_ref = refs[0], refs[1]
        e_refs = refs[2 + n_prev:2 + n_prev + n_ex]
        o_refs = refs[2 + n_prev + n_ex:2 + n_prev + n_ex + n_out]
        av = a_ref[...]
        if a_fn is not None:
            av = a_fn(av)
        bv = b_ref[...]
        if bv.ndim == 3:
            if dims == "nt":
                bv = jnp.concatenate([bv[c] for c in range(bv.shape[0])], axis=1)
            else:
                bv = bv.reshape(-1, bv.shape[-1])
        part = _dot(av.astype(BF16), bv.astype(BF16), dn)

        def finish(acc):
            outs = (acc,) if epilogue is None else epilogue(acc, *[e[...] for e in e_refs])
            for o_ref, o in zip(o_refs, outs):
                o_ref[...] = o.astype(o_ref.dtype)

        if nk == 1:
            finish(part)
        else:
            acc_ref = refs[-1]
            k = pl.program_id(2)

            @pl.when(k == 0)
            def _():
                acc_ref[...] = part

            @pl.when(k > 0)
            def _():
                acc_ref[...] += part

            @pl.when(k == nk - 1)
            def _():
                finish(acc_ref[...])

    outs, side_outs = _hosted_call(
        body, name=name, grid=(nj, ni, nk) if weights_outer else (ni, nj, nk),
        in_specs=[a_spec, b_spec] + [ANY_SPEC] * n_prev + e_specs,
        out_specs=out_specs, out_shape=out_shape,
        scratch_shapes=[pltpu.VMEM((tm, tn), F32)] if nk > 1 else [],
        args=(a, b, *prev, *extras), side=side, io_aliases=io_aliases)
    res = outs[0] if n_out == 1 else outs
    return res if side is None else (res, side_outs)


def _mod_spec(tm, tn, seq):
    return pl.BlockSpec((1, MOD_ROWS, tn), lambda i, j, k: ((i * tm) // seq, 0, j))


def _normmod_fwd(x3, g, mod, shift_row, scale_row, *, name, tb=512, side=None):
    bsz, seq, d = x3.shape
    tb = min(tb, seq)

    def body(x_ref, g_ref, mod_ref, h_ref):
        m = mod_ref[0]
        nrm = _rms_fwd(x_ref[0], g_ref[...], d)
        h = nrm * (1.0 + m[scale_row:scale_row + 1, :]) + m[shift_row:shift_row + 1, :]
        h_ref[0] = h.astype(BF16)

    outs, side_outs = _hosted_call(
        body, name=name, grid=(bsz, seq // tb),
        in_specs=[pl.BlockSpec((1, tb, d), lambda b, i: (b, i, 0)),
                  pl.BlockSpec((1, d), lambda b, i: (0, 0)),
                  pl.BlockSpec((1, MOD_ROWS, d), lambda b, i: (b, 0, 0))],
        out_specs=[pl.BlockSpec((1, tb, d), lambda b, i: (b, i, 0))],
        out_shape=[jax.ShapeDtypeStruct((bsz, seq, d), BF16)],
        args=(x3, g, mod), side=side)
    return outs[0] if side is None else (outs[0], side_outs)


def _pair_mean_exact(x, lo):
    s_lo = jnp.sum(jnp.where(lo, x, 0.0), axis=-1, keepdims=True)
    s_hi = jnp.sum(jnp.where(lo, 0.0, x), axis=-1, keepdims=True)
    return jnp.where(lo, s_lo, s_hi) * (1.0 / GROUP_DIM)


def _gmlp_pair_fwd(gv_p, w0, w1, bias, lo):
    mu = _pair_mean_exact(gv_p, lo)
    dlt = gv_p - mu
    var = _pair_mean_exact(dlt * dlt, lo)
    rstd = lax.rsqrt(var + EPS)
    vn = dlt * rstd
    vnb = vn.astype(BF16)
    mixed = jnp.where(lo, _dot(w0, vnb), _dot(w1, vnb)) + bias
    return vn, vnb, rstd, mixed


def _tril_bf16(w):
    t = w.shape[-1]
    return jnp.where(_iota((t, t), 1) <= _iota((t, t), 0), w, 0.0).astype(BF16)


def _gmlp_fwd(z3, ws, bexp, g_out, *, name):
    bsz, seq, _ = z3.shape
    cpb = max(k for k in (1, 2, 4) if (seq // CHUNK) % k == 0)
    nc = seq // (CHUNK * cpb)
    tb = CHUNK * cpb

    def body(u_ref, v_ref, ws_ref, b_ref, g_ref, y_ref):
        lo = _iota((CHUNK, 128), 1) < GROUP_DIM
        for c in range(cpb):
            rows = slice(CHUNK * c, CHUNK * (c + 1))
            gu = _gelu(u_ref[0, rows, :].astype(F32))
            gv = _gelu(v_ref[0, rows, :].astype(F32))
            parts = []
            for p in range(GROUPS // 2):
                sl = slice(128 * p, 128 * p + 128)
                w0 = _tril_bf16(ws_ref[2 * p])
                w1 = _tril_bf16(ws_ref[2 * p + 1])
                _, _, _, mixed = _gmlp_pair_fwd(gv[:, sl], w0, w1, b_ref[p], lo)
                parts.append(gu[:, sl] * mixed)
            yg = jnp.concatenate(parts, axis=1)
            y_ref[0, rows, :] = _rms_fwd(yg, g_ref[...], D_GMLP).astype(BF16)

    return pl.pallas_call(
        body, name=name, grid=(bsz, nc),
        in_specs=[pl.BlockSpec((1, tb, D_GMLP), lambda b, i: (b, i, 0)),
                  pl.BlockSpec((1, tb, D_GMLP), lambda b, i: (b, i, 1)),
                  pl.BlockSpec((GROUPS, CHUNK, CHUNK), lambda b, i: (0, 0, 0)),
                  pl.BlockSpec((GROUPS // 2, CHUNK, 128), lambda b, i: (0, 0, 0)),
                  pl.BlockSpec((1, D_GMLP), lambda b, i: (0, 0))],
        out_specs=pl.BlockSpec((1, tb, D_GMLP), lambda b, i: (b, i, 0)),
        out_shape=jax.ShapeDtypeStruct((bsz, seq, D_GMLP), BF16),
        compiler_params=_cparams(),
    )(z3, z3, ws, bexp, g_out)


def _gmlp_bwd(z3, dyn3, ws, wst, bexp, g_out, *, name, dy_col):
    bsz, seq, _ = z3.shape
    cpb = max(k for k in (1, 2, 4) if (seq // CHUNK) % k == 0)
    nc = seq // (CHUNK * cpb)
    tb = CHUNK * cpb
    npair = GROUPS // 2

    def body(u_ref, v_ref, dy_ref, ws_ref, wst_ref, b_ref, g_ref, duv_ref, dws_ref, dbs_ref, dg_ref, dbacc):
        first = jnp.logical_and(pl.program_id(0) == 0, pl.program_id(1) == 0)
        last = jnp.logical_and(pl.program_id(0) == bsz - 1, pl.program_id(1) == nc - 1)

        @pl.when(first)
        def _():
            dws_ref[...] = jnp.zeros_like(dws_ref)
            dg_ref[...] = jnp.zeros_like(dg_ref)
            dbacc[...] = jnp.zeros_like(dbacc)

        lo = _iota((CHUNK, 128), 1) < GROUP_DIM
        tril = _iota((CHUNK, CHUNK), 1) <= _iota((CHUNK, CHUNK), 0)
        for c in range(cpb):
            rows = slice(CHUNK * c, CHUNK * (c + 1))
            u = u_ref[0, rows, :].astype(F32)
            v = v_ref[0, rows, :].astype(F32)
            gu, dgu = _gelu_and_grad(u)
            gv, dgv_dv = _gelu_and_grad(v)
            fwd = []
            for p in range(npair):
                sl = slice(128 * p, 128 * p + 128)
                w0 = _tril_bf16(ws_ref[2 * p])
                w1 = _tril_bf16(ws_ref[2 * p + 1])
                fwd.append(_gmlp_pair_fwd(gv[:, sl], w0, w1, b_ref[p], lo))
            yg = jnp.concatenate([gu[:, 128 * p:128 * p + 128] * fwd[p][3] for p in range(npair)], axis=1)
            dyg, dg = _rms_bwd(yg, g_ref[...], dy_ref[0, rows, :].astype(F32), D_GMLP)
            dg_ref[...] += dg
            du_parts, dv_parts = [], []
            for p in range(npair):
                sl = slice(128 * p, 128 * p + 128)
                vn, vnb, rstd, mixed = fwd[p]
                dyg_p = dyg[:, sl]
                dmixed = dyg_p * gu[:, sl]
                dbacc[p] += dmixed
                dm_lo = jnp.where(lo, dmixed, 0.0).astype(BF16)
                dm_hi = jnp.where(lo, 0.0, dmixed).astype(BF16)
                dws_ref[2 * p] += jnp.where(tril, _dot(dm_lo, vnb, NT), 0.0)
                dws_ref[2 * p + 1] += jnp.where(tril, _dot(dm_hi, vnb, NT), 0.0)
                dmb = dmixed.astype(BF16)
                dvn = jnp.where(lo, _dot(wst_ref[2 * p], dmb), _dot(wst_ref[2 * p + 1], dmb))
                dgv = rstd * (dvn - _pair_mean_exact(dvn, lo) - vn * _pair_mean_exact(dvn * vn, lo))
                dv_parts.append(dgv * dgv_dv[:, sl])
                du_parts.append(dyg_p * mixed * dgu[:, sl])
            duv_ref[0, rows, :] = jnp.concatenate(du_parts + dv_parts, axis=1).astype(BF16)

        @pl.when(last)
        def _():
            sel = jnp.where(_iota((8, 128), 0) == 0, (_iota((8, 128), 1) < GROUP_DIM).astype(F32),
                            jnp.where(_iota((8, 128), 0) == 1, (_iota((8, 128), 1) >= GROUP_DIM).astype(F32), 0.0))
            for p in range(npair):
                dbs_ref[p] = lax.dot_general(sel, dbacc[p], NT, precision=lax.Precision.HIGHEST,
                                             preferred_element_type=F32)

    duv, dws, dbs, dg = pl.pallas_call(
        body, name=name, grid=(bsz, nc),
        in_specs=[pl.BlockSpec((1, tb, D_GMLP), lambda b, i: (b, i, 0)),
                  pl.BlockSpec((1, tb, D_GMLP), lambda b, i: (b, i, 1)),
                  pl.BlockSpec((1, tb, D_GMLP), lambda b, i: (b, i, dy_col)),
                  pl.BlockSpec((GROUPS, CHUNK, CHUNK), lambda b, i: (0, 0, 0)),
                  pl.BlockSpec((GROUPS, CHUNK, CHUNK), lambda b, i: (0, 0, 0)),
                  pl.BlockSpec((npair, CHUNK, 128), lambda b, i: (0, 0, 0)),
                  pl.BlockSpec((1, D_GMLP), lambda b, i: (0, 0))],
        out_specs=[pl.BlockSpec((1, tb, 2 * D_GMLP), lambda b, i: (b, i, 0)),
                   pl.BlockSpec((GROUPS, CHUNK, CHUNK), lambda b, i: (0, 0, 0)),
                   pl.BlockSpec((npair, 8, CHUNK), lambda b, i: (0, 0, 0)),
                   pl.BlockSpec((1, D_GMLP), lambda b, i: (0, 0))],
        out_shape=[jax.ShapeDtypeStruct((bsz, seq, D_IN_PAD), BF16),
                   jax.ShapeDtypeStruct((GROUPS, CHUNK, CHUNK), F32),
                   jax.ShapeDtypeStruct((npair, 8, CHUNK), F32),
                   jax.ShapeDtypeStruct((1, D_GMLP), F32)],
        scratch_shapes=[pltpu.VMEM((npair, CHUNK, 128), F32)],
        compiler_params=_cparams(),
    )(z3, z3, dyn3, ws, wst, bexp, g_out)
    return duv, dws, dbs[:, :2, :].reshape(GROUPS, CHUNK), dg


def _partner(x):
    width = x.shape[-1]
    lane = _iota(x.shape, x.ndim - 1) % HEAD_PAD
    up = pltpu.roll(x, width - ROPE // 2, x.ndim - 1)
    down = pltpu.roll(x, ROPE // 2, x.ndim - 1)
    first = jnp.logical_and(lane >= NOPE, lane < NOPE + ROPE // 2)
    second = jnp.logical_and(lane >= NOPE + ROPE // 2, lane < NOPE + ROPE)
    return jnp.where(first, up, jnp.where(second, down, 0.0))


def _mla_prep_fwd(z3, g_q, g_kv, w_uq, w_ukv, ctab, stab, *, name, tb=512):
    bsz, seq, _ = z3.shape
    tb = min(tb, seq)
    hw = HEADS * HEAD_PAD

    def body(ql_ref, kvl_ref, krl_ref, gq_ref, gkv_ref, wuq_ref, wukv_ref, c_ref, s_ref, q_ref, kv_ref, kp_ref):
        cq = _rms_fwd(ql_ref[0].astype(F32), gq_ref[...], Q_RANK).astype(BF16)
        q = _dot(cq, wuq_ref[...])
        c1, s1 = c_ref[0], s_ref[0]
        c8, s8 = jnp.tile(c1, (1, HEADS)), jnp.tile(s1, (1, HEADS))
        q_ref[0] = ((q * c8 + _partner(q) * s8) * SCALE_LOG2).astype(BF16)
        ckv = _rms_fwd(kvl_ref[0].astype(F32), gkv_ref[...], KV_RANK).astype(BF16)
        kv = _dot(ckv, wukv_ref[...])
        kv_ref[0] = kv.astype(BF16)
        kr = krl_ref[0].astype(F32)
        kr = kr * c1 + _partner(kr) * s1
        lane = _iota((tb, hw), 1) % HEAD_PAD
        kp_ref[0] = jnp.where(lane < NOPE, kv, jnp.tile(kr, (1, HEADS))).astype(BF16)

    return pl.pallas_call(
        body, name=name, grid=(bsz, seq // tb),
        in_specs=[pl.BlockSpec((1, tb, Q_RANK), lambda b, i: (b, i, 4)),
                  pl.BlockSpec((1, tb, KV_RANK), lambda b, i: (b, i, 10)),
                  pl.BlockSpec((1, tb, HEAD_PAD), lambda b, i: (b, i, 11)),
                  pl.BlockSpec((1, Q_RANK), lambda b, i: (0, 0)),
                  pl.BlockSpec((1, KV_RANK), lambda b, i: (0, 0)),
                  pl.BlockSpec((Q_RANK, hw), lambda b, i: (0, 0)),
                  pl.BlockSpec((KV_RANK, hw), lambda b, i: (0, 0)),
                  pl.BlockSpec((1, tb, HEAD_PAD), lambda b, i: (b, i, 0)),
                  pl.BlockSpec((1, tb, HEAD_PAD), lambda b, i: (b, i, 0))],
        out_specs=[pl.BlockSpec((1, tb, hw), lambda b, i: (b, i, 0))] * 3,
        out_shape=[jax.ShapeDtypeStruct((bsz, seq, hw), BF16)] * 3,
        compiler_params=_cparams(),
    )(z3, z3, z3, g_q, g_kv, w_uq, w_ukv, ctab, stab)


def _mla_prep_bwd(z3, dz3, dq3, dk3, dv3, g_q, g_kv, w_uq, w_ukv, ctab, stab, *, name, tb=512):
    bsz, seq, _ = z3.shape
    tb = min(tb, seq)
    hw = HEADS * HEAD_PAD
    nb = seq // tb

    def body(ql_ref, kvl_ref, dq_ref, dk_ref, dv_ref, gq_ref, gkv_ref, wuq_ref, wukv_ref, c_ref, s_ref, dz_in,
             dz_ref, cq_ref, dqb_ref, ckv_ref, dkvb_ref, dgq_ref, dgkv_ref):
        @pl.when(jnp.logical_and(pl.program_id(0) == 0, pl.program_id(1) == 0))
        def _():
            dgq_ref[...] = jnp.zeros_like(dgq_ref)
            dgkv_ref[...] = jnp.zeros_like(dgkv_ref)

        c1, s1 = c_ref[0], s_ref[0]
        c8, s8 = jnp.tile(c1, (1, HEADS)), jnp.tile(s1, (1, HEADS))
        dqr = dq_ref[0]
        dqb = (dqr * c8 + _partner(dqr * s8)).astype(BF16)
        dqb_ref[0] = dqb
        ql = ql_ref[0].astype(F32)
        cq_ref[0] = _rms_fwd(ql, gq_ref[...], Q_RANK).astype(BF16)
        dql, dgq = _rms_bwd(ql, gq_ref[...], _dot(dqb, wuq_ref[...], NT), Q_RANK)
        dgq_ref[...] += dgq

        dk = dk_ref[0]
        lane = _iota((tb, hw), 1) % HEAD_PAD
        dkvb = jnp.where(lane < NOPE, dk, dv_ref[0]).astype(BF16)
        dkvb_ref[0] = dkvb
        kvl = kvl_ref[0].astype(F32)
        ckv_ref[0] = _rms_fwd(kvl, gkv_ref[...], KV_RANK).astype(BF16)
        dkvl, dgkv = _rms_bwd(kvl, gkv_ref[...], _dot(dkvb, wukv_ref[...], NT), KV_RANK)
        dgkv_ref[...] += dgkv

        dkr = dk[:, 0:HEAD_PAD].astype(F32)
        for h in range(1, HEADS):
            dkr = dkr + dk[:, HEAD_PAD * h:HEAD_PAD * (h + 1)].astype(F32)
        lane1 = _iota((tb, HEAD_PAD), 1)
        dkr = jnp.where(jnp.logical_and(lane1 >= NOPE, lane1 < NOPE + ROPE), dkr, 0.0)
        dkrl = dkr * c1 + _partner(dkr * s1)
        dz_ref[0] = jnp.concatenate([dql, dkvl, dkrl], axis=1).astype(BF16)

    return pl.pallas_call(
        body, name=name, grid=(bsz, nb),
        in_specs=[pl.BlockSpec((1, tb, Q_RANK), lambda b, i: (b, i, 4)),
                  pl.BlockSpec((1, tb, KV_RANK), lambda b, i: (b, i, 10)),
                  pl.BlockSpec((1, tb, hw), lambda b, i: (b, i, 0)),
                  pl.BlockSpec((1, tb, hw), lambda b, i: (b, i, 0)),
                  pl.BlockSpec((1, tb, hw), lambda b, i: (b, i, 0)),
                  pl.BlockSpec((1, Q_RANK), lambda b, i: (0, 0)),
                  pl.BlockSpec((1, KV_RANK), lambda b, i: (0, 0)),
                  pl.BlockSpec((Q_RANK, hw), lambda b, i: (0, 0)),
                  pl.BlockSpec((KV_RANK, hw), lambda b, i: (0, 0)),
                  pl.BlockSpec((1, tb, HEAD_PAD), lambda b, i: (b, i, 0)),
                  pl.BlockSpec((1, tb, HEAD_PAD), lambda b, i: (b, i, 0)),
                  ANY_SPEC],
        out_specs=[pl.BlockSpec((1, tb, 512), lambda b, i: (b, i, 2)),
                   pl.BlockSpec((1, tb, Q_RANK), lambda b, i: (b, i, 0)),
                   pl.BlockSpec((1, tb, hw), lambda b, i: (b, i, 0)),
                   pl.BlockSpec((1, tb, KV_RANK), lambda b, i: (b, i, 0)),
                   pl.BlockSpec((1, tb, hw), lambda b, i: (b, i, 0)),
                   pl.BlockSpec((1, Q_RANK), lambda b, i: (0, 0)),
                   pl.BlockSpec((1, KV_RANK), lambda b, i: (0, 0))],
        out_shape=[jax.ShapeDtypeStruct((bsz, seq, D_IN_PAD), BF16),
                   jax.ShapeDtypeStruct((bsz, seq, Q_RANK), BF16),
                   jax.ShapeDtypeStruct((bsz, seq, hw), BF16),
                   jax.ShapeDtypeStruct((bsz, seq, KV_RANK), BF16),
                   jax.ShapeDtypeStruct((bsz, seq, hw), BF16),
                   jax.ShapeDtypeStruct((1, Q_RANK), F32),
                   jax.ShapeDtypeStruct((1, KV_RANK), F32)],
        input_output_aliases={11: 0},
        compiler_params=_cparams(),
    )(z3, z3, dq3, dk3, dv3, g_q, g_kv, w_uq, w_ukv, ctab, stab, dz3)


ATTN_HEADS_PER_STEP = 4


def _attn_specs(tq, seq, hp):
    blk = pl.BlockSpec((1, tq, hp * HEAD_PAD), lambda b, h, i: (b, i, h))
    full = pl.BlockSpec((1, seq, hp * HEAD_PAD), lambda b, h, i: (b, 0, h))
    return blk, full


def _head(h):
    return slice(HEAD_PAD * h, HEAD_PAD * (h + 1))


def _attn_fwd(q3, kv3, kp3, *, name, tq=512, hp=ATTN_HEADS_PER_STEP, side=None):
    bsz, seq, hw = q3.shape
    tq = min(tq, seq)
    blk, full = _attn_specs(tq, seq, hp)

    def body(q_ref, kv_ref, kp_ref, o_ref, lse_ref):
        i = pl.program_id(2)

        def update(state, q, kp, kv, mask=None):
            m, l, acc = state
            s = _dot(q, kp, NT)
            if mask is not None:
                s = jnp.where(mask, s, -1e30)
            m_new = jnp.maximum(m, jnp.max(s, axis=1, keepdims=True))
            alpha = jnp.exp2(m - m_new)
            p = jnp.exp2(s - m_new)
            return m_new, alpha * l + jnp.sum(p, axis=1, keepdims=True), alpha * acc + _dot(p.astype(BF16), kv)

        def step(j, carry):
            st = pl.multiple_of(j * tq, tq)
            return tuple(update(carry[h], q_ref[0, :, _head(h)], kp_ref[0, pl.ds(st, tq), _head(h)],
                                kv_ref[0, pl.ds(st, tq), _head(h)]) for h in range(hp))

        init = tuple((jnp.full((tq, 1), -1e30, F32), jnp.zeros((tq, 1), F32), jnp.zeros((tq, HEAD_PAD), F32))
                     for _ in range(hp))
        carry = lax.fori_loop(0, i, step, init)

        st = pl.multiple_of(i * tq, tq)
        is_nope = _iota((tq, HEAD_PAD), 1) < NOPE
        causal = _iota((tq, tq), 1) <= _iota((tq, tq), 0)
        for h in range(hp):
            m, l, acc = update(carry[h], q_ref[0, :, _head(h)], kp_ref[0, pl.ds(st, tq), _head(h)],
                               kv_ref[0, pl.ds(st, tq), _head(h)], causal)
            o_ref[0, :, _head(h)] = jnp.where(is_nope, 0.0, acc / l).astype(BF16)
            lse_ref[0, :, _head(h)] = jnp.broadcast_to(m + jnp.log(l) * LOG2E, (tq, HEAD_PAD))

    outs, side_outs = _hosted_call(
        body, name=name, grid=(bsz, HEADS // hp, seq // tq),
        in_specs=[blk, full, full],
        out_specs=[blk, blk],
        out_shape=[jax.ShapeDtypeStruct((bsz, seq, hw), BF16), jax.ShapeDtypeStruct((bsz, seq, hw), F32)],
        args=(q3, kv3, kp3), side=side)
    return outs if side is None else (outs, side_outs)


def _attn_bwd(q3, kv3, kp3, do3, lse3, dl3, *, name, tq=512, hp=ATTN_HEADS_PER_STEP, side=None):
    bsz, seq, hw = q3.shape
    tq = min(tq, seq)
    nq = seq // tq
    blk, full = _attn_specs(tq, seq, hp)

    def body(kv_ref, kp_ref, q_ref, do_ref, lse_ref, dl_ref, dq_ref, dk_ref, dv_ref):
        j = pl.program_id(2)

        @pl.when(j == 0)
        def _():
            dq_ref[...] = jnp.zeros_like(dq_ref)

        def pair(h, row0, nrows, nkeys, mask=None):
            row0 = pl.multiple_of(row0, nrows)
            qi = q_ref[0, pl.ds(row0, nrows), _head(h)]
            do = do_ref[0, pl.ds(row0, nrows), _head(h)]
            kp = kp_ref[0, :nkeys, _head(h)]
            s = _dot(qi, kp, NT)
            if mask is not None:
                s = jnp.where(mask, s, -1e30)
            wide = nkeys // HEAD_PAD
            p = jnp.exp2(s - jnp.tile(lse_ref[0, pl.ds(row0, nrows), _head(h)], (1, wide)))
            dv = _dot(p.astype(BF16), do, TN)
            dp = _dot(do, kv_ref[0, :nkeys, _head(h)], NT)
            ds = (p * (dp - jnp.tile(dl_ref[0, pl.ds(row0, nrows), _head(h)], (1, wide)))).astype(BF16)
            dq_ref[0, pl.ds(row0, nrows), _head(h)] += _dot(ds, kp)
            return _dot(ds, qi, TN), dv

        def step(i, carry):
            st = pl.multiple_of(i * tq, tq)
            out = []
            for h in range(hp):
                dk, dv = pair(h, st, tq, tq)
                out.append((carry[h][0] + dk, carry[h][1] + dv))
            return tuple(out)

        causal = _iota((tq, tq), 1) <= _iota((tq, tq), 0)
        carry = tuple(pair(h, pl.multiple_of(j * tq, tq), tq, tq, causal) for h in range(hp))
        carry = lax.fori_loop(j + 1, nq, step, carry)
        for h in range(hp):
            dk_ref[0, :, _head(h)] = (carry[h][0] * (1.0 / LOG2E)).astype(BF16)
            dv_ref[0, :, _head(h)] = carry[h][1].astype(BF16)

        @pl.when(j == nq - 1)
        def _():
            dq_ref[...] = dq_ref[...] * ATTN_SCALE

    outs, side_outs = _hosted_call(
        body, name=name, grid=(bsz, HEADS // hp, nq),
        in_specs=[blk, blk, full, full, full, full],
        out_specs=[full, blk, blk],
        out_shape=[jax.ShapeDtypeStruct((bsz, seq, hw), F32)] + [jax.ShapeDtypeStruct((bsz, seq, hw), BF16)] * 2,
        args=(kv3, kp3, q3, do3, lse3, dl3), side=side)
    return outs if side is None else (outs, side_outs)


def _onorm_fwd(o3, yg3, g_pad, *, name, tb=1024):
    bsz, seq, hw = o3.shape
    wg = yg3.shape[-1]
    tb = min(tb, seq)

    def body(o_ref, yg_ref, g_ref, y_ref):
        ya = _rms_fwd(o_ref[0].astype(F32), g_ref[...], HEADS * 64).astype(BF16)
        y_ref[0] = jnp.concatenate([ya, yg_ref[0]], axis=1)

    return pl.pallas_call(
        body, name=name, grid=(bsz, seq // tb),
        in_specs=[pl.BlockSpec((1, tb, hw), lambda b, i: (b, i, 0)),
                  pl.BlockSpec((1, tb, wg), lambda b, i: (b, i, 0)),
                  pl.BlockSpec((1, hw), lambda b, i: (0, 0))],
        out_specs=pl.BlockSpec((1, tb, hw + wg), lambda b, i: (b, i, 0)),
        out_shape=jax.ShapeDtypeStruct((bsz, seq, hw + wg), BF16),
        compiler_params=_cparams(),
    )(o3, yg3, g_pad)


def _onorm_bwd(o3, dy3, g_pad, *, name, tb=1024):
    bsz, seq, hw = o3.shape
    tb = min(tb, seq)

    def body(o_ref, dy_ref, g_ref, do_ref, dl_ref, dg_ref):
        @pl.when(jnp.logical_and(pl.program_id(0) == 0, pl.program_id(1) == 0))
        def _():
            dg_ref[...] = jnp.zeros_like(dg_ref)

        o = o_ref[0].astype(F32)
        do, dg = _rms_bwd(o, g_ref[...], dy_ref[0].astype(F32), HEADS * 64)
        dg_ref[...] += dg
        do_ref[0] = do.astype(BF16)
        prod = do * o
        parts = []
        for h in range(HEADS):
            sh = jnp.sum(prod[:, HEAD_PAD * h:HEAD_PAD * (h + 1)], axis=1, keepdims=True)
            parts.append(jnp.broadcast_to(sh, (tb, HEAD_PAD)))
        dl_ref[0] = jnp.concatenate(parts, axis=1)

    return pl.pallas_call(
        body, name=name, grid=(bsz, seq // tb),
        in_specs=[pl.BlockSpec((1, tb, hw), lambda b, i: (b, i, 0)),
                  pl.BlockSpec((1, tb, hw), lambda b, i: (b, i, 0)),
                  pl.BlockSpec((1, hw), lambda b, i: (0, 0))],
        out_specs=[pl.BlockSpec((1, tb, hw), lambda b, i: (b, i, 0)),
                   pl.BlockSpec((1, tb, hw), lambda b, i: (b, i, 0)),
                   pl.BlockSpec((1, hw), lambda b, i: (0, 0))],
        out_shape=[jax.ShapeDtypeStruct((bsz, seq, hw), BF16),
                   jax.ShapeDtypeStruct((bsz, seq, hw), F32),
                   jax.ShapeDtypeStruct((1, hw), F32)],
        compiler_params=_cparams(),
    )(o3, dy3, g_pad)


def _resnode_bwd(x3, g, *, name, target3=None, dh3=None, dres3=None, mod_nm=None, rows=None,
                 branch3=None, mod_gate=None, gate_row=None, tb=1024, side=None):
    bsz, seq, d = x3.shape
    tb = min(tb, seq)
    final = target3 is not None
    has_branch = branch3 is not None
    row_spec = pl.BlockSpec((1, tb, d), lambda b, i: (b, i, 0))
    vec_spec = pl.BlockSpec((1, d), lambda b, i: (0, 0))
    mod_spec = pl.BlockSpec((1, MOD_ROWS, d), lambda b, i: (b, 0, 0))

    ins, in_specs = [x3, g], [row_spec, vec_spec]
    if final:
        ins += [target3]
        in_specs += [row_spec]
    else:
        ins += [dh3, dres3, mod_nm]
        in_specs += [row_spec, row_spec, mod_spec]
    if has_branch:
        ins += [branch3, mod_gate]
        in_specs += [row_spec, mod_spec]

    out_names = ["dx", "dg"]
    out_specs = [row_spec, vec_spec]
    out_shape = [jax.ShapeDtypeStruct((bsz, seq, d), F32), jax.ShapeDtypeStruct((1, d), F32)]
    if final:
        out_names += ["loss"]
        out_specs += [pl.BlockSpec((1, 128), lambda b, i: (0, 0))]
        out_shape += [jax.ShapeDtypeStruct((1, 128), F32)]
    else:
        out_names += ["dnm"]
        out_specs += [mod_spec]
        out_shape += [jax.ShapeDtypeStruct((bsz, MOD_ROWS, d), F32)]
    if has_branch:
        out_names += ["dbr", "dgate"]
        out_specs += [row_spec, mod_spec]
        out_shape += [jax.ShapeDtypeStruct((bsz, seq, d), BF16), jax.ShapeDtypeStruct((bsz, MOD_ROWS, d), F32)]
    n_in = len(ins)

    def body(*refs):
        r = dict(zip(["x", "g"] + (["t"] if final else ["dh", "dres", "nm"]) + (["br", "gm"] if has_branch else []),
                     refs[:n_in]))
        o = dict(zip(out_names, refs[n_in:]))
        b_first = pl.program_id(1) == 0
        first = jnp.logical_and(pl.program_id(0) == 0, b_first)
        rowid = _iota((MOD_ROWS, d), 0)

        @pl.when(first)
        def _():
            o["dg"][...] = jnp.zeros_like(o["dg"])
            if final:
                o["loss"][...] = jnp.zeros_like(o["loss"])

        @pl.when(b_first)
        def _():
            if not final:
                o["dnm"][...] = jnp.zeros_like(o["dnm"])
            if has_branch:
                o["dgate"][...] = jnp.zeros_like(o["dgate"])

        x = r["x"][0]
        gv = r["g"][...]
        if final:
            e = _rms_fwd(x, gv, d) - r["t"][0]
            sq = jnp.sum(jnp.sum(e * e, axis=1, keepdims=True), axis=0, keepdims=True)
            o["loss"][...] += jnp.broadcast_to(sq * (0.5 / d), (1, 128))
            dx, dg = _rms_bwd(x, gv, e * (1.0 / d), d)
        else:
            m = r["nm"][0]
            dh = r["dh"][0].astype(F32)
            scale = m[rows[1]:rows[1] + 1, :]
            rstd = lax.rsqrt(jnp.sum(x * x, axis=-1, keepdims=True) * (1.0 / d) + EPS)
            xh = x * rstd
            nrm = xh * gv
            dshift = jnp.sum(dh, axis=0, keepdims=True)
            dscale = jnp.sum(dh * nrm, axis=0, keepdims=True)
            o["dnm"][0] += jnp.where(rowid == 0, dshift, jnp.where(rowid == 1, dscale, 0.0))
            dn = dh * (1.0 + scale)
            dg = jnp.sum(dn * xh, axis=0, keepdims=True)
            dxh = dn * gv
            dx = rstd * (dxh - xh * (jnp.sum(dxh * xh, axis=-1, keepdims=True) * (1.0 / d))) + r["dres"][0]
        o["dg"][...] += dg
        o["dx"][0] = dx
        if has_branch:
            gate = r["gm"][0][gate_row:gate_row + 1, :]
            o["dbr"][0] = (gate * dx).astype(BF16)
            dgate = jnp.sum(dx * r["br"][0], axis=0, keepdims=True)
            o["dgate"][0] += jnp.where(rowid == 0, dgate, 0.0)

    outs, side_outs = _hosted_call(
        body, name=name, grid=(bsz, seq // tb),
        in_specs=in_specs, out_specs=out_specs, out_shape=out_shape, args=tuple(ins), side=side)
    res = dict(zip(out_names, outs))
    return res if side is None else (res, side_outs)


def _adamw(w, g, m, v, *, name):
    shape = w.shape
    cols = shape[-1]
    rows = w.size // cols
    tr = _pick_rows(rows, max(8, (512 * 1024) // cols // 8 * 8))

    def body(w_ref, g_ref, m_ref, v_ref, d_ref, nm_ref, nv_ref):
        d_ref[...], nm_ref[...], nv_ref[...] = _adamw_math(w_ref[...], g_ref[...], m_ref[...], v_ref[...])

    if w.ndim == 3 and shape[1] % 8 == 0:
        tr3 = _pick_rows(shape[1], max(8, (512 * 1024) // cols // 8 * 8))
        spec3 = pl.BlockSpec((None, tr3, cols), lambda l, i: (l, i, 0))
        return tuple(pl.pallas_call(
            body, name=name, grid=(shape[0], shape[1] // tr3),
            in_specs=[spec3] * 4, out_specs=[spec3] * 3,
            out_shape=[jax.ShapeDtypeStruct(shape, F32)] * 3,
            compiler_params=_cparams(),
        )(w, g, m, v))
    spec = pl.BlockSpec((tr, cols), lambda i: (i, 0))
    outs = pl.pallas_call(
        body, name=name, grid=(rows // tr,),
        in_specs=[spec] * 4, out_specs=[spec] * 3,
        out_shape=[jax.ShapeDtypeStruct((rows, cols), F32)] * 3,
        compiler_params=_cparams(),
    )(*[t.reshape(rows, cols) for t in (w, g, m, v)])
    return tuple(o.reshape(shape) for o in outs)


def _adamw_math(w, g, m, v):
    c1 = 1.0 - ADAM_B1 ** ADAM_STEP
    c2 = 1.0 - ADAM_B2 ** ADAM_STEP
    nm = ADAM_B1 * m + (1.0 - ADAM_B1) * g
    nv = ADAM_B2 * v + (1.0 - ADAM_B2) * (g * g)
    delta = -ADAM_LR * ((nm / c1) / (jnp.sqrt(nv / c2) + ADAM_EPS) + ADAM_WD * w)
    return delta, nm, nv


def _adamw_layers(w, m, v, bufs, row_off, *, name, tr=256):
    depth, rows, cols = w.shape
    tr = min(tr, rows)
    assert rows % tr == 0 and row_off % tr == 0

    outs = None
    for l in range(depth):
        def body(w_ref, g_ref, m_ref, v_ref, *rest):
            go_ref, d_ref, nm_ref, nv_ref = rest[-4:]
            g = g_ref[...]
            go_ref[...] = g
            d_ref[...], nm_ref[...], nv_ref[...] = _adamw_math(w_ref[...], g, m_ref[...], v_ref[...])

        layer = pl.BlockSpec((None, tr, cols), lambda i, l=l: (l, i, 0))
        prev = () if outs is None else tuple(outs)
        outs = pl.pallas_call(
            body, name=f"{name}_l{l}", grid=(rows // tr,),
            in_specs=[layer, pl.BlockSpec((tr, cols), lambda i: (row_off // tr + i, 0)), layer, layer]
            + [ANY_SPEC] * len(prev),
            out_specs=[layer] * 4,
            out_shape=[jax.ShapeDtypeStruct(w.shape, F32)] * 4,
            input_output_aliases={4 + k: k for k in range(len(prev))},
            compiler_params=_cparams(),
        )(w, bufs[l], m, v, *prev)
    return tuple(outs)


def _sum_leading(x, *, name, tr=256):
    n, rows, cols = x.shape
    tr = _pick_rows(rows, tr)

    def body(x_ref, o_ref):
        acc = x_ref[0]
        for k in range(1, n):
            acc = acc + x_ref[k]
        o_ref[...] = acc

    return pl.pallas_call(
        body, name=name, grid=(rows // tr,),
        in_specs=[pl.BlockSpec((n, tr, cols), lambda i: (0, i, 0))],
        out_specs=pl.BlockSpec((tr, cols), lambda i: (i, 0)),
        out_shape=jax.ShapeDtypeStruct((rows, cols), F32),
        compiler_params=_cparams(),
    )(x)


def _position():
    return lax.axis_index("x"), lax.axis_index("y"), lax.axis_index("c")


def _allgather8(x, *, name):
    shape = x.shape

    def body(x_ref, out_ref, send_sems, recv_sems, local_sem):
        px, py, pc = _position()
        me, sibling = (px, py, pc), (px, py, 1 - pc)
        chips = [(1 - px, py), (px, 1 - py), (1 - px, 1 - py)]
        src_own = x_ref

        def slot(qx, qy, qc):
            return out_ref.at[4 * qx + 2 * qy + qc]

        def copy(k, block, to, src=None):
            return pltpu.make_async_remote_copy(
                src_ref=slot(*block) if src is None else src, dst_ref=slot(*block),
                send_sem=send_sems.at[k], recv_sem=recv_sems.at[k], device_id=to, device_id_type=MESH)

        mine = pltpu.make_async_copy(src_own, slot(*me), local_sem)
        mine.start()
        first = [copy(0, me, sibling, src=src_own)]
        first += [copy(1 + j, me, (*chip, pc), src=src_own) for j, chip in enumerate(chips)]
        for cp in first:
            cp.start()
        passed = [copy(4 + j, (*chip, pc), sibling) for j, chip in enumerate(chips)]
        for j, chip in enumerate(chips):
            copy(1 + j, (*chip, pc), me).wait_recv()
            passed[j].start()
        copy(0, sibling, me).wait_recv()
        for j, chip in enumerate(chips):
            copy(4 + j, (*chip, 1 - pc), me).wait_recv()
        for cp in first + passed:
            cp.wait_send()
        mine.wait()

    return pl.pallas_call(
        body, name=name,
        out_shape=jax.ShapeDtypeStruct((N_DEV,) + shape, x.dtype),
        in_specs=[pl.BlockSpec(memory_space=pl.ANY)],
        out_specs=pl.BlockSpec(memory_space=pl.ANY),
        scratch_shapes=[pltpu.SemaphoreType.DMA((7,)), pltpu.SemaphoreType.DMA((7,)), pltpu.SemaphoreType.DMA],
    )(x)


class _Exchange:
    def __init__(self, ins, out_shapes, n, build, aliases=None):
        self.ins, self.out_shapes, self.n, self.build = tuple(ins), tuple(out_shapes), n, build
        self.aliases = dict(aliases or {})

    def _descriptors(self, in_refs, out_refs, send_sems, recv_sems):
        sends, recvs = [], []
        for k, (src, dst, peer, landing) in enumerate(self.build(in_refs, out_refs)):
            sends.append(pltpu.make_async_remote_copy(
                src_ref=src, dst_ref=dst, send_sem=send_sems.at[k], recv_sem=recv_sems.at[k],
                device_id=peer, device_id_type=MESH))
            recvs.append(pltpu.make_async_remote_copy(
                src_ref=src, dst_ref=landing, send_sem=send_sems.at[k], recv_sem=recv_sems.at[k],
                device_id=peer, device_id_type=MESH))
        return sends, recvs

    def start(self, *refs):
        for cp in self._descriptors(*refs)[0]:
            cp.start()

    def finish(self, *refs):
        sends, recvs = self._descriptors(*refs)
        for cp in recvs:
            cp.wait_recv()
        for cp in sends:
            cp.wait_send()


ANY_SPEC = pl.BlockSpec(memory_space=pl.ANY)


def _hosted_call(body, *, name, grid, in_specs, out_specs, out_shape, args, scratch_shapes=(), side=None,
                 num_scalar_prefetch=0, io_aliases=None):
    in_specs, out_specs, out_shape = list(in_specs), list(out_specs), list(out_shape)
    n_in, n_out = len(in_specs) + num_scalar_prefetch, len(out_specs)
    kernel_body = body
    aliases = dict(io_aliases or {})
    if side is not None:
        s_in, s_out = len(side.ins), len(side.out_shapes)
        aliases.update({n_in + i: n_out + o for i, o in side.aliases.items()})

        def kernel_body(*refs):
            ins, s_ins = refs[:n_in], refs[n_in:n_in + s_in]
            outs = refs[n_in + s_in:n_in + s_in + n_out]
            s_outs = refs[n_in + s_in + n_out:n_in + s_in + n_out + s_out]
            scratch, sems = refs[n_in + s_in + n_out + s_out:-2], refs[-2:]
            first = functools.reduce(jnp.logical_and, [pl.program_id(a) == 0 for a in range(len(grid))])
            last = functools.reduce(jnp.logical_and, [pl.program_id(a) == g - 1 for a, g in enumerate(grid)])

            @pl.when(first)
            def _():
                side.start(s_ins, s_outs, *sems)

            body(*ins, *outs, *scratch)

            @pl.when(last)
            def _():
                side.finish(s_ins, s_outs, *sems)

        in_specs += [ANY_SPEC] * s_in
        out_specs += [ANY_SPEC] * s_out
        out_shape += list(side.out_shapes)
        scratch_shapes = list(scratch_shapes) + [pltpu.SemaphoreType.DMA((side.n,)),
                                                 pltpu.SemaphoreType.DMA((side.n,))]
        args = tuple(args) + side.ins
    if num_scalar_prefetch:
        grid_spec = pltpu.PrefetchScalarGridSpec(num_scalar_prefetch=num_scalar_prefetch, grid=grid,
                                                 in_specs=in_specs, out_specs=out_specs,
                                                 scratch_shapes=list(scratch_shapes))
        outs = pl.pallas_call(kernel_body, name=name, grid_spec=grid_spec, out_shape=out_shape,
                              input_output_aliases=aliases, compiler_params=_cparams())(*args)
    else:
        outs = pl.pallas_call(kernel_body, name=name, grid=grid, in_specs=in_specs, out_specs=out_specs,
                              out_shape=out_shape, scratch_shapes=list(scratch_shapes),
                              input_output_aliases=aliases, compiler_params=_cparams())(*args)
    return tuple(outs[:n_out]), tuple(outs[n_out:])


def _run_exchange(ex, *, name):
    s_in = len(ex.ins)

    def body(*refs):
        ins, outs, sems = refs[:s_in], refs[s_in:-2], refs[-2:]
        ex.start(ins, outs, *sems)
        ex.finish(ins, outs, *sems)

    outs = pl.pallas_call(
        body, name=name, out_shape=list(ex.out_shapes),
        in_specs=[ANY_SPEC] * s_in, out_specs=[ANY_SPEC] * len(ex.out_shapes),
        scratch_shapes=[pltpu.SemaphoreType.DMA((ex.n,)), pltpu.SemaphoreType.DMA((ex.n,))],
        input_output_aliases=ex.aliases,
    )(*ex.ins)
    return tuple(outs)


def _both(a, b):
    na, oa = len(a.ins), len(a.out_shapes)

    def build(ins, outs):
        return a.build(ins[:na], outs[:oa]) + b.build(ins[na:], outs[oa:])

    aliases = dict(a.aliases)
    aliases.update({na + i: oa + o for i, o in b.aliases.items()})
    return _Exchange(a.ins + b.ins, a.out_shapes + b.out_shapes, a.n + b.n, build, aliases)


def _other_chips(px, py):
    return [(px, 1 - py), (1 - px, py), (1 - px, 1 - py)]


def _gather_spread(w_flat, halves=True):
    rows, w = w_flat.shape
    hr = rows // 2 if halves else rows

    def build(ins, outs):
        px, py, pc = _position()
        mine = ins[0].at[pl.ds(pc * hr, hr)] if halves else ins[0]
        me = 4 * px + 2 * py + pc
        plan = [((px, py, 1 - pc), me ^ 1)]
        plan += [((qx, qy, pc), 4 * qx + 2 * qy + pc) for qx, qy in _other_chips(px, py)]
        return [(mine, outs[0].at[me], peer, outs[0].at[their]) for peer, their in plan]

    return _Exchange([w_flat], [jax.ShapeDtypeStruct((N_DEV, hr, w), w_flat.dtype)], 4, build)


def _gather_pass_on(gath):
    def build(ins, outs):
        px, py, pc = _position()
        out = []
        for qx, qy in _other_chips(px, py):
            blk = 4 * qx + 2 * qy + pc
            out.append((outs[0].at[blk], outs[0].at[blk], (px, py, 1 - pc), outs[0].at[blk ^ 1]))
        return out

    return _Exchange([gath], [jax.ShapeDtypeStruct(gath.shape, gath.dtype)], 3, build, aliases={0: 0})


def _rs_halves(g):
    n, rows, w = g.shape
    hr = rows // 2

    def build(ins, outs):
        px, py, pc = _position()
        return [(ins[0].at[:, pl.ds((1 - pc) * hr, hr), :], outs[0], (px, py, 1 - pc), outs[0])]

    return _Exchange([g], [jax.ShapeDtypeStruct((n, hr, w), g.dtype)], 1, build)


def _rs_chips(sb):
    def build(ins, outs):
        px, py, pc = _position()
        return [(ins[0].at[j], outs[0].at[j], (qx, qy, pc), outs[0].at[j])
                for j, (qx, qy) in enumerate(_other_chips(px, py))]

    return _Exchange([sb], [jax.ShapeDtypeStruct(sb.shape, sb.dtype)], 3, build)


def _rs_complete(buf):
    def build(ins, outs):
        px, py, pc = _position()
        return [(outs[0].at[pc], outs[0].at[pc], (px, py, 1 - pc), outs[0].at[1 - pc])]

    return _Exchange([buf], [jax.ShapeDtypeStruct(buf.shape, buf.dtype)], 1, build, aliases={0: 0})


def _rs_partial(g, recv, ids, *, name, tr=128):
    _, rows, w = g.shape
    hr = rows // 2
    nb = hr // tr

    def body(ids_ref, g_ref, r_ref, o_ref):
        o_ref[0] = (g_ref[0] + r_ref[0]).astype(BF16)

    grid_spec = pltpu.PrefetchScalarGridSpec(
        num_scalar_prefetch=1, grid=(3, nb),
        in_specs=[pl.BlockSpec((1, tr, w), lambda j, i, ids: (ids[1] ^ (j + 1), ids[0] * nb + i, 0)),
                  pl.BlockSpec((1, tr, w), lambda j, i, ids: (ids[1] ^ (j + 1), i, 0))],
        out_specs=pl.BlockSpec((1, tr, w), lambda j, i, ids: (j, i, 0)))
    return pl.pallas_call(
        body, name=name, grid_spec=grid_spec,
        out_shape=jax.ShapeDtypeStruct((3, hr, w), BF16),
        compiler_params=_cparams(),
    )(ids, g, recv)


def _rs_total(g, recv, got, ids, *, name, tr=128):
    _, rows, w = g.shape
    hr = rows // 2
    nb = hr // tr

    def body(ids_ref, g_ref, r_ref, got_ref, o_ref):
        acc = g_ref[0] + r_ref[0]
        for j in range(3):
            acc = acc + got_ref[j].astype(F32)
        o_ref[0] = acc

    grid_spec = pltpu.PrefetchScalarGridSpec(
        num_scalar_prefetch=1, grid=(nb,),
        in_specs=[pl.BlockSpec((1, tr, w), lambda i, ids: (ids[1], ids[0] * nb + i, 0)),
                  pl.BlockSpec((1, tr, w), lambda i, ids: (ids[1], i, 0)),
                  pl.BlockSpec((3, tr, w), lambda i, ids: (0, i, 0))],
        out_specs=pl.BlockSpec((1, tr, w), lambda i, ids: (ids[0], i, 0)))
    return pl.pallas_call(
        body, name=name, grid_spec=grid_spec,
        out_shape=jax.ShapeDtypeStruct((2, hr, w), F32),
        compiler_params=_cparams(),
    )(ids, g, recv, got)


class _ReduceScatter:
    def __init__(self, g, ids, tag):
        self.g, self.ids, self.tag, self.stage, self.result = g, ids, tag, 0, None

    def next_exchange(self):
        if self.stage == 0:
            return _rs_halves(self.g)
        if self.stage == 1:
            return _rs_chips(self.sb)
        return _rs_complete(self.buf)

    def done(self, outs):
        if self.stage == 0:
            self.recv = outs[0]
            hr = self.recv.shape[1]
            self.tr = max(t for t in range(16, 513, 16) if hr % t == 0)
            self.sb = _rs_partial(self.g, self.recv, self.ids, name=f"{self.tag}_partial", tr=self.tr)
        elif self.stage == 1:
            self.buf = _rs_total(self.g, self.recv, outs[0], self.ids, name=f"{self.tag}_total", tr=self.tr)
        else:
            _, hr, w = outs[0].shape
            self.result = outs[0].reshape(2 * hr, w)
        self.stage += 1

    def finish_alone(self):
        names = ("halves", "chips", "complete")
        while self.stage < 3:
            self.done(_run_exchange(self.next_exchange(), name=f"{self.tag}_{names[self.stage]}"))
        return self.result


def _flat_rows():
    used = sum(r for _, r in FSDP_SECTIONS)
    return used, -(-used // ROW_ALIGN) * ROW_ALIGN


def _cols_to_chunks(full):
    rows, cols = full.shape
    t = full.reshape(rows, N_CHIPS, cols // N_CHIPS).transpose(1, 0, 2)
    return t.reshape(N_CHIPS, -1, FLAT_W)


def _chunks_to_cols(chunks, rows, cols):
    return chunks.reshape(N_CHIPS, rows, cols // N_CHIPS).transpose(1, 0, 2).reshape(rows, cols)


def _pad_heads(w, real):
    lead = w.shape[:-1]
    t = w.reshape(lead + (HEADS, real))
    t = jnp.pad(t, [(0, 0)] * len(lead) + [(0, 0), (0, HEAD_PAD - real)])
    return t.reshape(lead + (HEADS * HEAD_PAD,))


def _unpad_heads(w, real):
    lead = w.shape[:-1]
    return w.reshape(lead + (HEADS, HEAD_PAD))[..., :real].reshape(lead + (HEADS * real,))


def _pad_value_lanes(w, axis):
    w = jnp.moveaxis(w, axis, -1)
    lead = w.shape[:-1]
    t = w.reshape(lead + (HEADS, 64))
    t = jnp.pad(t, [(0, 0)] * len(lead) + [(0, 0), (HEAD_PAD - 64, 0)])
    return jnp.moveaxis(t.reshape(lead + (HEADS * HEAD_PAD,)), -1, axis)


def _unpad_value_lanes(w, axis):
    w = jnp.moveaxis(w, axis, -1)
    lead = w.shape[:-1]
    t = w.reshape(lead + (HEADS, HEAD_PAD))[..., HEAD_PAD - 64:]
    return jnp.moveaxis(t.reshape(lead + (HEADS * 64,)), -1, axis)


def _pad_w_in_t(wt):
    z = jnp.zeros((NOPE, wt.shape[1]), wt.dtype)
    z2 = jnp.zeros((HEAD_PAD - NOPE - ROPE, wt.shape[1]), wt.dtype)
    return jnp.concatenate([wt[:1408], z, wt[1408:], z2], axis=0)


def _unpad_w_in_t(wt):
    return jnp.concatenate([wt[:1408], wt[1408 + NOPE:1408 + NOPE + ROPE]], axis=0)


def _rope_tables(positions):
    freqs = ROPE_THETA ** (-jnp.arange(0, ROPE, 2, dtype=F32) / ROPE)
    ang = positions.astype(F32)[..., None] * freqs
    cos, sin = jnp.cos(ang), jnp.sin(ang)
    lead = cos.shape[:-1]
    ones = jnp.ones(lead + (NOPE,), F32)
    zeros_n = jnp.zeros(lead + (NOPE,), F32)
    zeros_p = jnp.zeros(lead + (HEAD_PAD - NOPE - ROPE,), F32)
    ctab = jnp.concatenate([ones, cos, cos, zeros_p], axis=-1)
    stab = jnp.concatenate([zeros_n, -sin, sin, zeros_p], axis=-1)
    return ctab, stab


def _mix_weights(full):
    return dict(
        w_in_t=_pad_w_in_t(full["w_in_t"]),
        w_uq=_pad_heads(full["mla_w_uq"], NOPE + ROPE),
        w_ukv=full["mla_w_ukv"],
        w_out=jnp.concatenate([_pad_value_lanes(full["w_out"][D_GMLP:], 0), full["w_out"][:D_GMLP]], axis=0),
    )


def _small_weights(p, l):
    ws = p["gmlp_ws"][l]
    tril = jnp.tril(jnp.ones((CHUNK, CHUNK), bool))
    bs = p["gmlp_bs"][l]
    bexp = jnp.repeat(bs.reshape(GROUPS // 2, 2, CHUNK).transpose(0, 2, 1), GROUP_DIM, axis=2)
    return dict(
        ws=ws,
        wst=jnp.where(tril[None], ws, 0.0).transpose(0, 2, 1).astype(BF16),
        bexp=bexp,
        g_mix=p["norm_mix_g"][l][None],
        g_ffn=p["norm_ffn_g"][l][None],
        g_q=p["mla_q_norm_g"][l][None],
        g_kv=p["mla_kv_norm_g"][l][None],
        g_og=p["out_norm_gmlp_g"][l][None],
        g_oa=_pad_value_lanes(p["out_norm_mla_g"][l], 0)[None],
    )


def _local_step(x3, target3, positions, mods, final_g, plan):
    bsz, seq, d = x3.shape
    tok = bsz * seq
    tmt = min(512, seq)
    tmk = min(1024, seq)
    tmw = min(2048, tok)
    chunk = (None, None, FLAT_W, FLAT_W)
    chunk2 = (2, None, FLAT_W, FLAT_W)
    ff_grad_shape = (N_CHIPS, 2 * FLAT_W, FLAT_W)
    ctab, stab = _rope_tables(positions)
    lw = [None] * DEPTH

    def flat(t):
        return t.reshape(tok, t.shape[-1])

    def cube(t):
        return t.reshape(bsz, seq, t.shape[-1])

    def carrying(l, tag, fn, *args, **kw):
        side = plan.host(l, tag)
        if side is None:
            return fn(*args, **kw)
        res, side_outs = fn(*args, side=side, **kw)
        plan.hosted(l, tag, side_outs)
        return res

    saved = []
    x = x3
    for l in range(DEPTH):
        lw[l] = plan.layer(l)
        w, mod = lw[l], mods[l]
        if l == 0:
            h1 = carrying(l, "fwd_normmod1", _normmod_fwd, x, w["g_mix"], mod, SHIFT1, SCALE1,
                          name=f"l{l}_normmod1")
        else:
            h1 = h1_next
        z = cube(_mm(flat(h1), w["w_in_t"], dims="nt", name=f"l{l}_w_in", tm=tmk, tn=D_IN_PAD, tk=d,
                     out_dtypes=(BF16,)))
        yg = _gmlp_fwd(z, w["ws"], w["bexp"], w["g_og"], name=f"l{l}_gmlp_fwd")
        q, kv, kp = _mla_prep_fwd(z, w["g_q"], w["g_kv"], w["w_uq"], w["w_ukv"], ctab, stab, name=f"l{l}_mla_prep")
        o, lse = carrying(l, "fwd_attn", _attn_fwd, q, kv, kp, name=f"l{l}_attn_fwd")
        y = _onorm_fwd(o, yg, w["g_oa"], name=f"l{l}_onorm_fwd")

        def normmod(xv, gv, gm, shift_row, scale_row):
            m = gm[0]
            return _rms_fwd(xv, gv, d) * (1.0 + m[scale_row:scale_row + 1, :]) + m[shift_row:shift_row + 1, :]

        def out_epi(po, xv, gm, gf):
            x_new = xv + gm[0][GATE1:GATE1 + 1, :] * po
            return po, x_new, normmod(x_new, gf, gm, SHIFT2, SCALE2)

        vec_spec = pl.BlockSpec((1, d), lambda i, j, k: (0, j))
        po, x_mid, h2 = carrying(l, "fwd_out_a", _mm, flat(y), w["w_out"], dims="nn", name=f"l{l}_w_out",
                                 tm=tmt, tn=d, tk=y.shape[-1], out_dtypes=(BF16, F32, BF16), epilogue=out_epi,
                                 extras=(flat(x), mod, w["g_ffn"]),
                                 extra_specs=(None, _mod_spec(tmt, d, seq), vec_spec))
        x_mid, h2 = cube(x_mid), cube(h2)

        def act_epi(acc):
            r = jnp.maximum(acc, 0.0)
            return (r * r,)

        r = carrying(l, "fwd_ff1", _mm, flat(h2), w["ff"], dims="nn", name=f"l{l}_w_ff1", tm=tmw, tn=FLAT_W,
                     tk=d, out_dtypes=(BF16,), epilogue=act_epi, weights_outer=True, n=D_FF,
                     b_block=(chunk, lambda i, j, k: (j, 0, 0, 0)))

        more = l + 1 < DEPTH

        def ff2_epi(acc, xv, gm, *nxt):
            x_new = xv + gm[0][GATE2:GATE2 + 1, :] * acc
            return (acc, x_new) + ((normmod(x_new, nxt[1], nxt[0], SHIFT1, SCALE1),) if more else ())

        mod_spec = _mod_spec(tmt, d, seq)
        outs = carrying(l, "fwd_ff2", _mm, r, w["ff"], dims="nn", name=f"l{l}_w_ff2", tm=tmt, tn=d, tk=2 * FLAT_W,
                        out_dtypes=(BF16, F32) + ((BF16,) if more else ()), epilogue=ff2_epi,
                        extras=(flat(x_mid), mod) + ((mods[l + 1], plan.layer(l + 1)["g_mix"]) if more else ()),
                        extra_specs=(None, mod_spec) + ((mod_spec, vec_spec) if more else ()), n=d,
                        b_block=(chunk2, lambda i, j, k: (k, 1, 0, 0)))
        f, x_out = outs[0], outs[1]
        h1_next = cube(outs[2]) if more else None
        saved.append(dict(x_in=x, h1=h1, z=z, q=q, kv=kv, kp=kp, o=o, lse=lse, y=y, po=cube(po),
                          x_mid=x_mid, h2=h2, r=r, f=cube(f)))
        x = cube(x_out)

    grads = [dict() for _ in range(DEPTH)]
    dmods = [None] * DEPTH
    top = DEPTH - 1
    node = _resnode_bwd(x, final_g[None], name="final_loss_bwd", target3=target3,
                        branch3=saved[top]["f"], mod_gate=mods[top], gate_row=GATE2)
    loss_part = node["loss"][0, 0]
    d_final_g = node["dg"][0]
    plan.scalars(loss_part, d_final_g)
    for l in range(DEPTH - 1, -1, -1):
        w, mod, s = lw[l], mods[l], saved[l]
        dx_out, dfb, dgate2 = node["dx"], flat(node["dbr"]), node["dgate"][:, 0]

        def dact_epi(acc, rv):
            return (acc * (2.0 * jnp.sqrt(rv.astype(F32))),)

        da = carrying(l, "bwd_d_r", _mm, dfb, w["ff"], dims="nt", name=f"l{l}_d_r", tm=tmw, tn=FLAT_W, tk=d,
                      out_dtypes=(BF16,), epilogue=dact_epi, extras=(s["r"],), weights_outer=True, n=D_FF,
                      b_block=(chunk, lambda i, j, k: (j, 1, 0, 0)))
        g_ff = carrying(l, "bwd_dw_ff2", _mm, s["r"], dfb, dims="tn", name=f"l{l}_dw_ff2", tm=FLAT_W, tn=d,
                        tk=2048, out_into=(ff_grad_shape, (None, FLAT_W, FLAT_W), lambda i, j, k: (i, 1, 0), None))
        g_ff = carrying(l, "bwd_dw_ff1", _mm, flat(s["h2"]), da, dims="tn", name=f"l{l}_dw_ff1", tm=d, tn=FLAT_W,
                        tk=2048, out_into=(ff_grad_shape, (None, FLAT_W, FLAT_W), lambda i, j, k: (j, 0, 0), g_ff))
        plan.ff_grads(l, g_ff)
        dh2 = carrying(l, "bwd_d_h2", _mm, da, w["ff"], dims="nt", name=f"l{l}_d_h2", tm=tmk, tn=d, tk=2 * FLAT_W,
                       n=d, b_block=(chunk2, lambda i, j, k: (k, 0, 0, 0)), out_dtypes=(BF16,))
        node = carrying(l, "bwd_resnode_ffn", _resnode_bwd, s["x_mid"], w["g_ffn"], name=f"l{l}_resnode_ffn",
                        dh3=cube(dh2), dres3=dx_out, mod_nm=mod, rows=(SHIFT2, SCALE2), branch3=s["po"],
                        mod_gate=mod, gate_row=GATE1)
        grads[l]["norm_ffn_g"] = node["dg"][0]
        dshift2, dscale2 = node["dnm"][:, 0], node["dnm"][:, 1]
        dx_mid, dpo, dgate1 = node["dx"], flat(node["dbr"]), node["dgate"][:, 0]

        wy = s["y"].shape[-1]
        dy = cube(carrying(l, "bwd_d_y", _mm, dpo, w["w_out"], dims="nt", name=f"l{l}_d_y", tm=tmk, tn=wy, tk=d,
                           out_dtypes=(BF16,)))
        dw_out = _mm(flat(s["y"]), dpo, dims="tn", name=f"l{l}_dw_out", tm=wy // 3, tn=d, tk=2048)
        hw = HEADS * HEAD_PAD
        grads[l]["w_out"] = jnp.concatenate([dw_out[hw:], _unpad_value_lanes(dw_out[:hw], 0)], axis=0)

        dz, dws, dbs, dg_og = _gmlp_bwd(s["z"], dy, w["ws"], w["wst"], w["bexp"], w["g_og"],
                                        name=f"l{l}_gmlp_bwd", dy_col=hw // D_GMLP)
        grads[l]["gmlp_ws"], grads[l]["gmlp_bs"], grads[l]["out_norm_gmlp_g"] = dws, dbs, dg_og[0]

        do, dl, dg_oa = _onorm_bwd(s["o"], dy, w["g_oa"], name=f"l{l}_onorm_bwd")
        grads[l]["out_norm_mla_g"] = _unpad_value_lanes(dg_oa[0], 0)
        plan.small_ready(l, grads[l])
        dq, dk, dv = carrying(l, "bwd_attn_dkv", _attn_bwd, s["q"], s["kv"], s["kp"], do, s["lse"], dl,
                              name=f"l{l}_attn_bwd")
        dz, cq, dqb, ckv, dkvb, dg_q, dg_kv = _mla_prep_bwd(
            s["z"], dz, dq, dk, dv, w["g_q"], w["g_kv"], w["w_uq"], w["w_ukv"], ctab, stab,
            name=f"l{l}_mla_prep_bwd")
        grads[l]["mla_q_norm_g"], grads[l]["mla_kv_norm_g"] = dg_q[0], dg_kv[0]
        dw_uq = carrying(l, "bwd_dw_uq", _mm, flat(cq), flat(dqb), dims="tn", name=f"l{l}_dw_uq", tm=Q_RANK,
                         tn=1024, tk=4096)
        grads[l]["mla_w_uq"] = _unpad_heads(dw_uq, NOPE + ROPE)
        grads[l]["w_in_t"] = _unpad_w_in_t(carrying(l, "bwd_dw_in", _mm, flat(dz), flat(s["h1"]), dims="tn",
                                                    name=f"l{l}_dw_in", tm=D_IN_PAD // 2, tn=d, tk=2048))
        grads[l]["mla_w_ukv"] = carrying(l, "bwd_dw_ukv", _mm, flat(ckv), flat(dkvb), dims="tn", name=f"l{l}_dw_ukv",
                                         tm=KV_RANK, tn=1024, tk=4096)
        plan.layer_grads(l, grads[l])
        dh1 = carrying(l, "bwd_d_h1", _mm, flat(dz), w["w_in_t"], dims="nn", name=f"l{l}_d_h1", tm=tmk, tn=d,
                       tk=D_IN_PAD, out_dtypes=(BF16,))
        below = dict(branch3=saved[l - 1]["f"], mod_gate=mods[l - 1], gate_row=GATE2) if l > 0 else {}
        node = carrying(l, "bwd_resnode_mix", _resnode_bwd, s["x_in"], w["g_mix"], name=f"l{l}_resnode_mix",
                        dh3=cube(dh1), dres3=dx_mid, mod_nm=mod, rows=(SHIFT1, SCALE1), **below)
        grads[l]["norm_mix_g"] = node["dg"][0]
        dshift1, dscale1 = node["dnm"][:, 0], node["dnm"][:, 1]
        dmods[l] = jnp.stack([dshift1, dscale1, dgate1, dshift2, dscale2, dgate2], axis=1)
    return node["dx"], dmods


W_NAMES = ("w_ada", "b_ada", "norm_mix_g", "w_in", "gmlp_ws", "gmlp_bs", "mla_q_norm_g", "mla_kv_norm_g",
           "mla_w_uq", "mla_w_ukv", "out_norm_gmlp_g", "out_norm_mla_g", "w_out", "norm_ffn_g", "w_ff1", "w_ff2",
           "final_norm_g")
FLAT_KEY = {"w_in": "w_in", "w_uq": "mla_w_uq", "w_ukv": "mla_w_ukv", "w_out": "w_out", "w_ff1": "w_ff1",
            "w_ff2": "w_ff2"}
COL_SHARDED = ("w_in", "w_uq", "w_ukv", "w_ff1")
FULL_SHAPE = {"w_in": (D_MODEL, D_IN), "w_uq": (Q_RANK, HEADS * (NOPE + ROPE)), "w_ukv": (KV_RANK, HEADS * 128),
              "w_out": (D_MODEL, D_MODEL)}
SMALL_LAYER_NAMES = ("gmlp_ws", "gmlp_bs", "out_norm_gmlp_g", "out_norm_mla_g", "norm_ffn_g")
LATE_SMALL_NAMES = ("norm_mix_g", "mla_q_norm_g", "mla_kv_norm_g")


def _silu(v):
    return v * (1.0 / (1.0 + jnp.exp(-v)))


class _CommPlan:
    FWD = {"fwd_attn": ("ff", 0, "spread"), "fwd_out_a": ("ff", 0, "pass"),
           "fwd_ff1": ("mix", 1, "spread"), "fwd_ff2": ("mix", 1, "pass")}
    BWD = {"bwd_d_r": ("mix", 1), "bwd_dw_ff2": ("mix", 1),
           "bwd_d_h2": ("ff", 0), "bwd_attn_dkv": ("ff", 0), "bwd_dw_uq": ("ff", 0)}
    BWD_ALSO = {"bwd_d_h2": ("mix", 1)}
    BWD_LAST = {"bwd_d_h1": ("mix", 0), "bwd_resnode_mix": ("mix", 0)}
    SMALL = {"bwd_attn_dkv": "spread", "bwd_dw_uq": "pass"}

    def __init__(self, weights, ids, dev, core):
        self.weights, self.ids, self.dev, self.core = weights, ids, dev, core
        self.used, self.rows = _flat_rows()
        self.flat = {("mix", l): self._flat_mix(l) for l in range(DEPTH)}
        self.flat.update({("ff", l): jnp.concatenate([weights["w_ff1"][l], weights["w_ff2"][l]], axis=0).astype(BF16)
                          for l in range(DEPTH)})
        self.lw, self.rs, self.grads, self.spread = {}, {}, {}, {}
        self.small_vec, self.small_sum, self.small_spread, self.extra = {}, {}, None, {}
        self.lw = {l: _small_weights(weights, l) for l in range(DEPTH)}

    def _flat_mix(self, l):
        pieces = []
        for nm, _ in FSDP_SECTIONS:
            shard = self.weights[FLAT_KEY[nm]][l]
            pieces.append(shard.T if nm == "w_in" else shard.reshape(-1, FLAT_W))
        pieces.append(jnp.zeros((self.rows - self.used, FLAT_W), F32))
        return jnp.concatenate(pieces, axis=0).astype(BF16)

    def _arrived(self, group, l, gath):
        flat = self.flat[group, l]
        hr = flat.shape[0] // 2
        mine = lax.dynamic_slice(flat, (self.core * hr, 0), (hr, FLAT_W))
        gath = lax.dynamic_update_slice(gath, mine[None], (self.dev, 0, 0))
        if group == "ff":
            self.lw[l]["ff"] = gath.reshape(N_CHIPS, 2, hr, FLAT_W)
            return
        w_gath = gath.reshape(N_CHIPS, self.rows, FLAT_W)
        full, off = {}, 0
        for nm, nrows in FSDP_SECTIONS:
            sec = w_gath[:, off:off + nrows]
            off += nrows
            rows, cols = FULL_SHAPE[nm]
            if nm == "w_in":
                full["w_in_t"] = sec.reshape(cols, rows)
            else:
                full[FLAT_KEY[nm]] = (_chunks_to_cols(sec, rows, cols) if nm in COL_SHARDED
                                      else sec.reshape(rows, cols))
        self.lw[l].update(_mix_weights(full))

    def layer(self, l):
        return self.lw[l]

    def host(self, l, tag):
        if tag == "fwd_normmod1":
            return _gather_spread(self.flat["mix", 0]) if l == 0 else None
        if tag in self.FWD:
            group, ahead, what = self.FWD[tag]
            if l + ahead >= DEPTH:
                return None
            return _gather_spread(self.flat[group, l + ahead]) if what == "spread" else _gather_pass_on(self.spread[group])
        ex = None
        for rs in self._rs_for(l, tag):
            ex = rs.next_exchange() if ex is None else _both(ex, rs.next_exchange())
        if tag in self.SMALL:
            small = (_gather_spread(self.small_vec[l], halves=False) if self.SMALL[tag] == "spread"
                     else _gather_pass_on(self.small_spread))
            ex = small if ex is None else _both(ex, small)
        return ex

    def _rs_for(self, l, tag):
        found = []
        if tag in self.BWD_LAST and l == 0:
            found.append(self.rs.get(self.BWD_LAST[tag]))
        for table in (self.BWD, self.BWD_ALSO):
            if tag in table:
                group, ahead = table[tag]
                found.append(self.rs.get((group, l + ahead)))
        return [rs for rs in found if rs is not None and rs.stage <= 2]

    def hosted(self, l, tag, outs):
        if tag == "fwd_normmod1":
            self._arrived("mix", 0, _run_exchange(_gather_pass_on(outs[0]), name="l0_mix_gather_pass_on")[0])
        elif tag in self.FWD:
            group, ahead, what = self.FWD[tag]
            if what == "spread":
                self.spread[group] = outs[0]
            else:
                self._arrived(group, l + ahead, outs[0])
        else:
            for rs in self._rs_for(l, tag):
                rs.done(outs[:1])
                outs = outs[1:]
            if tag in self.SMALL:
                if self.SMALL[tag] == "spread":
                    self.small_spread = outs[0]
                else:
                    self._small_arrived(l, outs[0])

    def ff_grads(self, l, g_ff):
        self.rs["ff", l] = _ReduceScatter(g_ff, self.ids, f"l{l}_ff_rs")

    def layer_grads(self, l, grads):
        self.grads[l] = grads
        pieces = []
        for nm, nrows in FSDP_SECTIONS:
            if nm == "w_in":
                pieces.append(grads["w_in_t"].reshape(N_CHIPS, nrows, FLAT_W))
                continue
            g = grads[FLAT_KEY[nm]]
            pieces.append(_cols_to_chunks(g) if nm in COL_SHARDED else g.reshape(N_CHIPS, nrows, FLAT_W))
        pieces.append(jnp.zeros((N_CHIPS, self.rows - self.used, FLAT_W), F32))
        self.rs["mix", l] = _ReduceScatter(jnp.concatenate(pieces, axis=1), self.ids, f"l{l}_mix_rs")

    def scalars(self, loss_part, d_final_g):
        self.extra = {0: [loss_part[None]]}
        self.extra.setdefault(DEPTH - 1, []).insert(0, d_final_g)

    def small_ready(self, l, grads):
        parts = [grads[nm].reshape(-1) for nm in SMALL_LAYER_NAMES] + self.extra.get(l, [])
        vec = jnp.concatenate(parts)
        rows = -(-vec.shape[0] // (8 * FLAT_W)) * 8
        self.small_vec[l] = jnp.pad(vec, (0, rows * FLAT_W - vec.shape[0])).reshape(rows, FLAT_W)

    def _small_arrived(self, l, gath):
        gath = lax.dynamic_update_slice(gath, self.small_vec[l][None], (self.dev, 0, 0))
        self.small_sum[l] = _sum_leading(gath, name=f"l{l}_small_sum").reshape(-1)

    def finish(self, dmod):
        late = jnp.concatenate([jnp.stack([self.grads[l][nm] for l in range(DEPTH)], axis=0).reshape(-1)
                                for nm in LATE_SMALL_NAMES])
        head = -(-late.shape[0] // (8 * FLAT_W)) * 8
        late = jnp.pad(late, (0, head * FLAT_W - late.shape[0])).reshape(head, FLAT_W)
        vec = jnp.concatenate([late, dmod], axis=0)
        rs = self.rs["mix", 0]
        while rs.stage < 2:
            rs.done(_run_exchange(rs.next_exchange(), name=f"l0_mix_rs_stage{rs.stage}"))
        outs = _run_exchange(_both(rs.next_exchange(), _gather_spread(vec, halves=False)), name="final_spread")
        rs.done(outs[:1])
        (gath,) = _run_exchange(_gather_pass_on(outs[1]), name="final_pass_on")
        gath = lax.dynamic_update_slice(gath, vec[None], (self.dev, 0, 0))
        late_sum = _sum_leading(gath[:, :head], name="late_small_sum").reshape(-1)
        loss, res = self._small_grads(late_sum)
        return loss, res, gath[:, head:]

    def _small_grads(self, late):
        out = {nm: [] for nm in SMALL_LAYER_NAMES}
        for l in range(DEPTH):
            off = 0
            for nm in SMALL_LAYER_NAMES:
                size = self.weights[nm][l].size
                out[nm].append(self.small_sum[l][off:off + size].reshape(self.weights[nm].shape[1:]))
                off += size
            if l == DEPTH - 1:
                final = self.small_sum[l][off:off + self.weights["final_norm_g"].size]
                off += final.shape[0]
            if l == 0:
                loss = self.small_sum[l][off]
        res = {nm: jnp.stack(parts, axis=0) for nm, parts in out.items()}
        res["final_norm_g"] = final
        off = 0
        for nm in LATE_SMALL_NAMES:
            size = self.weights[nm].size
            res[nm] = late[off:off + size].reshape(self.weights[nm].shape)
            off += size
        return loss, res

    def mix_grads(self):
        per = {FLAT_KEY[nm]: [] for nm, _ in FSDP_SECTIONS}
        for l in range(DEPTH):
            shard, off = self.rs["mix", l].result, 0
            for nm, nrows in FSDP_SECTIONS:
                key = FLAT_KEY[nm]
                sec = shard[off:off + nrows]
                per[key].append(sec.T if nm == "w_in" else sec.reshape(self.weights[key].shape[1:]))
                off += nrows
        return {key: jnp.stack(parts, axis=0) for key, parts in per.items()}

    def ff_shards(self):
        return [self.rs["ff", l].result for l in range(DEPTH)]


def kernel(x, c, positions, w_ada, b_ada, norm_mix_g, w_in, gmlp_ws, gmlp_bs, mla_q_norm_g, mla_kv_norm_g, mla_w_uq, mla_w_ukv, out_norm_gmlp_g, out_norm_mla_g, w_out, norm_ffn_g, w_ff1, w_ff2, final_norm_g, loss_target, m_w_ada, m_b_ada, m_norm_mix_g, m_w_in, m_gmlp_ws, m_gmlp_bs, m_mla_q_norm_g, m_mla_kv_norm_g, m_mla_w_uq, m_mla_w_ukv, m_out_norm_gmlp_g, m_out_norm_mla_g, m_w_out, m_norm_ffn_g, m_w_ff1, m_w_ff2, m_final_norm_g, v_w_ada, v_b_ada, v_norm_mix_g, v_w_in, v_gmlp_ws, v_gmlp_bs, v_mla_q_norm_g, v_mla_kv_norm_g, v_mla_w_uq, v_mla_w_ukv, v_out_norm_gmlp_g, v_out_norm_mla_g, v_w_out, v_norm_ffn_g, v_w_ff1, v_w_ff2, v_final_norm_g):
    weights = dict(w_ada=w_ada, b_ada=b_ada, norm_mix_g=norm_mix_g, w_in=w_in, gmlp_ws=gmlp_ws, gmlp_bs=gmlp_bs,
                   mla_q_norm_g=mla_q_norm_g, mla_kv_norm_g=mla_kv_norm_g, mla_w_uq=mla_w_uq, mla_w_ukv=mla_w_ukv,
                   out_norm_gmlp_g=out_norm_gmlp_g, out_norm_mla_g=out_norm_mla_g, w_out=w_out,
                   norm_ffn_g=norm_ffn_g, w_ff1=w_ff1, w_ff2=w_ff2, final_norm_g=final_norm_g)
    mom_m = dict(zip(W_NAMES, (m_w_ada, m_b_ada, m_norm_mix_g, m_w_in, m_gmlp_ws, m_gmlp_bs, m_mla_q_norm_g,
                               m_mla_kv_norm_g, m_mla_w_uq, m_mla_w_ukv, m_out_norm_gmlp_g, m_out_norm_mla_g,
                               m_w_out, m_norm_ffn_g, m_w_ff1, m_w_ff2, m_final_norm_g)))
    mom_v = dict(zip(W_NAMES, (v_w_ada, v_b_ada, v_norm_mix_g, v_w_in, v_gmlp_ws, v_gmlp_bs, v_mla_q_norm_g,
                               v_mla_kv_norm_g, v_mla_w_uq, v_mla_w_ukv, v_out_norm_gmlp_g, v_out_norm_mla_g,
                               v_w_out, v_norm_ffn_g, v_w_ff1, v_w_ff2, v_final_norm_g)))
    bsz, seq, d = x.shape
    px, py, pc = _position()
    chip = 2 * px + py
    dev = 2 * chip + pc
    ids = jnp.stack([pc, chip]).astype(jnp.int32)
    n_ex = N_DEV * bsz
    ada_cols = w_ada.shape[-1]

    c_all = _allgather8(c.reshape(bsz * d // 128, 128), name="gather_c").reshape(n_ex, d)
    mod_parts = []
    for l in range(DEPTH):
        bias = lax.dynamic_slice(b_ada[l], (chip * ada_cols,), (ada_cols,))[None]
        mod_parts.append(_mm(c_all, w_ada, dims="nn", name=f"l{l}_mod", tm=n_ex, tn=ada_cols, tk=d, n=ada_cols,
                             b_block=((None, d, ada_cols), lambda i, j, k, l=l: (l, k, j)),
                             epilogue=lambda acc, bv: (acc + bv,), extras=(bias,),
                             extra_specs=(pl.BlockSpec((1, ada_cols), lambda i, j, k: (0, j)),), a_fn=_silu))
    mod_g = _allgather8(jnp.concatenate(mod_parts, axis=0), name="gather_mod")
    mod_g = mod_g.reshape(N_CHIPS, 2, DEPTH, n_ex, ada_cols)[:, 0]
    mod_full = mod_g.transpose(1, 2, 0, 3).reshape(DEPTH, n_ex, N_CHIPS * ada_cols)
    mod_mine = lax.dynamic_slice(mod_full, (0, dev * bsz, 0), (DEPTH, bsz, N_MOD * d))
    mod_mine = jnp.pad(mod_mine.reshape(DEPTH, bsz, N_MOD, d), ((0, 0), (0, 0), (0, MOD_ROWS - N_MOD), (0, 0)))
    mods = [mod_mine[l] for l in range(DEPTH)]

    plan = _CommPlan(weights, ids, dev, pc)
    grad_x, dmods = _local_step(x, loss_target, positions, mods, final_norm_g, plan)

    dmod = jnp.stack(dmods, axis=1).reshape(bsz * DEPTH * N_MOD, d)
    loss, small, dmod_all = plan.finish(dmod)
    grad = plan.mix_grads()
    grad.update(small)
    dmod_all = dmod_all.reshape(n_ex, DEPTH, N_MOD * d)
    gw, gb = [], []
    for l in range(DEPTH):
        dm = dmod_all[:, l]
        dm_cols = lax.dynamic_slice(dm, (0, chip * ada_cols), (n_ex, ada_cols))
        gw.append(_mm(c_all, dm_cols, dims="tn", name=f"l{l}_dw_ada", tm=d, tn=ada_cols, tk=n_ex, a_fn=_silu,
                      out_into=(w_ada.shape, (None, d, ada_cols), lambda i, j, k, l=l: (l, i, j),
                                gw[-1] if gw else None)))
        gb.append(_sum_leading(dm.reshape(n_ex, N_MOD * d // FLAT_W, FLAT_W), name=f"l{l}_db_ada").reshape(-1))
    grad["w_ada"] = gw[-1]
    grad["b_ada"] = jnp.stack(gb, axis=0)

    delta, new_m, new_v = {}, {}, {}
    ff_bufs = plan.ff_shards()
    for nm, row_off in (("w_ff1", 0), ("w_ff2", FLAT_W)):
        grad[nm], delta[nm], new_m[nm], new_v[nm] = _adamw_layers(
            weights[nm], mom_m[nm], mom_v[nm], ff_bufs, row_off, name=f"adamw_{nm}")
    for nm in W_NAMES:
        if nm not in delta:
            delta[nm], new_m[nm], new_v[nm] = _adamw(weights[nm], grad[nm], mom_m[nm], mom_v[nm],
                                                     name=f"adamw_{nm}")
    return (loss, grad_x, *[grad[nm] for nm in W_NAMES], *[delta[nm] for nm in W_NAMES],
            *[new_m[nm] for nm in W_NAMES], *[new_v[nm] for nm in W_NAMES])
```

```python
import functools
import math

import jax
import jax.numpy as jnp
from jax import lax
from jax.experimental import pallas as pl
from jax.experimental.pallas import tpu as pltpu

F32 = jnp.float32
BF16 = jnp.bfloat16

D_MODEL = 1024
DEPTH = 2
D_GMLP = 512
GROUPS = 8
GROUP_DIM = 64
CHUNK = 128
HEADS = 8
NOPE = 64
ROPE = 32
HEAD_PAD = 128
Q_RANK = 256
KV_RANK = 128
D_FF = 4096
N_MOD = 6
MOD_ROWS = 8
EPS = 1e-6
ROPE_THETA = 10000.0
D_IN = 1440
D_IN_PAD = 1536
ATTN_SCALE = (NOPE + ROPE) ** -0.5
LOG2E = math.log2(math.e)
SCALE_LOG2 = ATTN_SCALE * LOG2E
N_CHIPS = 4
N_DEV = 8

ADAM_LR = 0.001
ADAM_B1 = 0.9
ADAM_B2 = 0.999
ADAM_EPS = 1e-08
ADAM_WD = 0.01
ADAM_STEP = 10

VMEM_LIMIT = 48 * 1024 * 1024
FLAT_W = 1024
ROW_ALIGN = 256

NN = (((1,), (0,)), ((), ()))
NT = (((1,), (1,)), ((), ()))
TN = (((0,), (0,)), ((), ()))
MESH = pl.DeviceIdType.MESH

SHIFT1, SCALE1, GATE1, SHIFT2, SCALE2, GATE2 = range(6)

FSDP_SECTIONS = (("w_out", 256), ("w_in", 360), ("w_uq", 48), ("w_ukv", 32))


def _cparams(vmem=VMEM_LIMIT):
    return pltpu.CompilerParams(vmem_limit_bytes=vmem)


def _dot(a, b, dims=NN):
    return lax.dot_general(a, b, dims, preferred_element_type=F32)


def _iota(shape, axis):
    return lax.broadcasted_iota(jnp.int32, shape, axis)


def _gelu(x):
    k = math.sqrt(2.0 / math.pi)
    return 0.5 * x * (1.0 + jnp.tanh(k * (x + 0.044715 * (x * x * x))))


def _gelu_and_grad(x):
    k = math.sqrt(2.0 / math.pi)
    x2 = x * x
    t = jnp.tanh(k * (x + 0.044715 * (x2 * x)))
    half = 0.5 * (1.0 + t)
    return x * half, half + 0.5 * x * (1.0 - t * t) * (k * (1.0 + 3.0 * 0.044715 * x2))


def _rms_fwd(x, g, n):
    r = lax.rsqrt(jnp.sum(x * x, axis=-1, keepdims=True) * (1.0 / n) + EPS)
    return x * r * g


def _rms_bwd(x, g, dy, n):
    r = lax.rsqrt(jnp.sum(x * x, axis=-1, keepdims=True) * (1.0 / n) + EPS)
    xh = x * r
    dxh = dy * g
    dx = r * (dxh - xh * (jnp.sum(dxh * xh, axis=-1, keepdims=True) * (1.0 / n)))
    dg = jnp.sum(dy * xh, axis=0, keepdims=True)
    return dx, dg


def _pick_rows(rows, limit):
    if rows <= limit:
        return rows
    for t in range(limit, 7, -8):
        if rows % t == 0:
            return t
    return rows


def _mm(a, b, *, dims, name, tm=512, tn=1024, tk=1024, out_dtypes=(F32,), epilogue=None,
        extras=(), extra_specs=(), a_fn=None, weights_outer=False, side=None, b_block=None, n=None,
        out_into=None):
    if dims == "tn":
        kk, m = a.shape
    else:
        m, kk = a.shape
    if n is None:
        n = b.shape[0] if dims == "nt" else b.shape[1]
    tm, tn, tk = min(tm, m), min(tn, n), min(tk, kk)
    assert m % tm == 0 and n % tn == 0 and kk % tk == 0, (name, a.shape, b.shape, tm, tn, tk)
    ni, nj, nk = m // tm, n // tn, kk // tk

    def spec(shape, pick):
        if weights_outer:
            return pl.BlockSpec(shape, lambda j, i, k: pick(i, j, k))
        return pl.BlockSpec(shape, pick)

    if dims == "tn":
        a_spec = spec((tk, tm), lambda i, j, k: (k, i))
    else:
        a_spec = spec((tm, tk), lambda i, j, k: (i, k))
    if b_block is not None:
        b_spec = spec(*b_block)
    elif dims == "nt":
        b_spec = spec((tn, tk), lambda i, j, k: (j, k))
    else:
        b_spec = spec((tk, tn), lambda i, j, k: (k, j))
    o_spec = spec((tm, tn), lambda i, j, k: (i, j))
    out_shape = [jax.ShapeDtypeStruct((m, n), dt) for dt in out_dtypes]
    out_specs = [o_spec] * len(out_dtypes)
    prev, io_aliases = (), {}
    if out_into is not None:
        full_shape, block, index, before = out_into
        assert len(out_dtypes) == 1 and not extras
        out_shape = [jax.ShapeDtypeStruct(full_shape, out_dtypes[0])]
        out_specs = [spec(block, index)]
        if before is not None:
            prev, io_aliases = (before,), {2: 0}
    assert not (weights_outer and extra_specs)
    dn = {"nn": NN, "nt": NT, "tn": TN}[dims]
    n_ex, n_out = len(extras), len(out_dtypes)
    e_specs = [o_spec if s is None else s for s in (tuple(extra_specs) + (None,) * n_ex)[:n_ex]]

    n_prev = len(prev)

    def body(*refs):
        a_ref, b_ref = refs[0], refs[1]
        e_refs = refs[2 + n_prev:2 + n_prev + n_ex]
        o_refs = refs[2 + n_prev + n_ex:2 + n_prev + n_ex + n_out]
        av = a_ref[...]
        if a_fn is not None:
            av = a_fn(av)
        bv = b_ref[...]
        if bv.ndim == 3:
            if dims == "nt":
                bv = jnp.concatenate([bv[c] for c in range(bv.shape[0])], axis=1)
            else:
                bv = bv.reshape(-1, bv.shape[-1])
        part = _dot(av.astype(BF16), bv.astype(BF16), dn)

        def finish(acc):
            outs = (acc,) if epilogue is None else epilogue(acc, *[e[...] for e in e_refs])
            for o_ref, o in zip(o_refs, outs):
                o_ref[...] = o.astype(o_ref.dtype)

        if nk == 1:
            finish(part)
        else:
            acc_ref = refs[-1]
            k = pl.program_id(2)

            @pl.when(k == 0)
            def _():
                acc_ref[...] = part

            @pl.when(k > 0)
            def _():
                acc_ref[...] += part

            @pl.when(k == nk - 1)
            def _():
                finish(acc_ref[...])

    outs, side_outs = _hosted_call(
        body, name=name, grid=(nj, ni, nk) if weights_outer else (ni, nj, nk),
        in_specs=[a_spec, b_spec] + [ANY_SPEC] * n_prev + e_specs,
        out_specs=out_specs, out_shape=out_shape,
        scratch_shapes=[pltpu.VMEM((tm, tn), F32)] if nk > 1 else [],
        args=(a, b, *prev, *extras), side=side, io_aliases=io_aliases)
    res = outs[0] if n_out == 1 else outs
    return res if side is None else (res, side_outs)


def _mod_spec(tm, tn, seq):
    return pl.BlockSpec((1, MOD_ROWS, tn), lambda i, j, k: ((i * tm) // seq, 0, j))


def _normmod_fwd(x3, g, mod, shift_row, scale_row, *, name, tb=512, side=None):
    bsz, seq, d = x3.shape
    tb = min(tb, seq)

    def body(x_ref, g_ref, mod_ref, h_ref):
        m = mod_ref[0]
        nrm = _rms_fwd(x_ref[0], g_ref[...], d)
        h = nrm * (1.0 + m[scale_row:scale_row + 1, :]) + m[shift_row:shift_row + 1, :]
        h_ref[0] = h.astype(BF16)

    outs, side_outs = _hosted_call(
        body, name=name, grid=(bsz, seq // tb),
        in_specs=[pl.BlockSpec((1, tb, d), lambda b, i: (b, i, 0)),
                  pl.BlockSpec((1, d), lambda b, i: (0, 0)),
                  pl.BlockSpec((1, MOD_ROWS, d), lambda b, i: (b, 0, 0))],
        out_specs=[pl.BlockSpec((1, tb, d), lambda b, i: (b, i, 0))],
        out_shape=[jax.ShapeDtypeStruct((bsz, seq, d), BF16)],
        args=(x3, g, mod), side=side)
    return outs[0] if side is None else (outs[0], side_outs)


def _pair_mean_exact(x, lo):
    s_lo = jnp.sum(jnp.where(lo, x, 0.0), axis=-1, keepdims=True)
    s_hi = jnp.sum(jnp.where(lo, 0.0, x), axis=-1, keepdims=True)
    return jnp.where(lo, s_lo, s_hi) * (1.0 / GROUP_DIM)


def _gmlp_pair_fwd(gv_p, w0, w1, bias, lo):
    mu = _pair_mean_exact(gv_p, lo)
    dlt = gv_p - mu
    var = _pair_mean_exact(dlt * dlt, lo)
    rstd = lax.rsqrt(var + EPS)
    vn = dlt * rstd
    vnb = vn.astype(BF16)
    mixed = jnp.where(lo, _dot(w0, vnb), _dot(w1, vnb)) + bias
    return vn, vnb, rstd, mixed


def _tril_bf16(w):
    t = w.shape[-1]
    return jnp.where(_iota((t, t), 1) <= _iota((t, t), 0), w, 0.0).astype(BF16)


def _gmlp_fwd(z3, ws, bexp, g_out, *, name):
    bsz, seq, _ = z3.shape
    cpb = max(k for k in (1, 2, 4) if (seq // CHUNK) % k == 0)
    nc = seq // (CHUNK * cpb)
    tb = CHUNK * cpb

    def body(u_ref, v_ref, ws_ref, b_ref, g_ref, y_ref):
        lo = _iota((CHUNK, 128), 1) < GROUP_DIM
        for c in range(cpb):
            rows = slice(CHUNK * c, CHUNK * (c + 1))
            gu = _gelu(u_ref[0, rows, :].astype(F32))
            gv = _gelu(v_ref[0, rows, :].astype(F32))
            parts = []
            for p in range(GROUPS // 2):
                sl = slice(128 * p, 128 * p + 128)
                w0 = _tril_bf16(ws_ref[2 * p])
                w1 = _tril_bf16(ws_ref[2 * p + 1])
                _, _, _, mixed = _gmlp_pair_fwd(gv[:, sl], w0, w1, b_ref[p], lo)
                parts.append(gu[:, sl] * mixed)
            yg = jnp.concatenate(parts, axis=1)
            y_ref[0, rows, :] = _rms_fwd(yg, g_ref[...], D_GMLP).astype(BF16)

    return pl.pallas_call(
        body, name=name, grid=(bsz, nc),
        in_specs=[pl.BlockSpec((1, tb, D_GMLP), lambda b, i: (b, i, 0)),
                  pl.BlockSpec((1, tb, D_GMLP), lambda b, i: (b, i, 1)),
                  pl.BlockSpec((GROUPS, CHUNK, CHUNK), lambda b, i: (0, 0, 0)),
                  pl.BlockSpec((GROUPS // 2, CHUNK, 128), lambda b, i: (0, 0, 0)),
                  pl.BlockSpec((1, D_GMLP), lambda b, i: (0, 0))],
        out_specs=pl.BlockSpec((1, tb, D_GMLP), lambda b, i: (b, i, 0)),
        out_shape=jax.ShapeDtypeStruct((bsz, seq, D_GMLP), BF16),
        compiler_params=_cparams(),
    )(z3, z3, ws, bexp, g_out)


def _gmlp_bwd(z3, dyn3, ws, wst, bexp, g_out, *, name, dy_col):
    bsz, seq, _ = z3.shape
    cpb = max(k for k in (1, 2, 4) if (seq // CHUNK) % k == 0)
    nc = seq // (CHUNK * cpb)
    tb = CHUNK * cpb
    npair = GROUPS // 2

    def body(u_ref, v_ref, dy_ref, ws_ref, wst_ref, b_ref, g_ref, duv_ref, dws_ref, dbs_ref, dg_ref, dbacc):
        first = jnp.logical_and(pl.program_id(0) == 0, pl.program_id(1) == 0)
        last = jnp.logical_and(pl.program_id(0) == bsz - 1, pl.program_id(1) == nc - 1)

        @pl.when(first)
        def _():
            dws_ref[...] = jnp.zeros_like(dws_ref)
            dg_ref[...] = jnp.zeros_like(dg_ref)
            dbacc[...] = jnp.zeros_like(dbacc)

        lo = _iota((CHUNK, 128), 1) < GROUP_DIM
        tril = _iota((CHUNK, CHUNK), 1) <= _iota((CHUNK, CHUNK), 0)
        for c in range(cpb):
            rows = slice(CHUNK * c, CHUNK * (c + 1))
            u = u_ref[0, rows, :].astype(F32)
            v = v_ref[0, rows, :].astype(F32)
            gu, dgu = _gelu_and_grad(u)
            gv, dgv_dv = _gelu_and_grad(v)
            fwd = []
            for p in range(npair):
                sl = slice(128 * p, 128 * p + 128)
                w0 = _tril_bf16(ws_ref[2 * p])
                w1 = _tril_bf16(ws_ref[2 * p + 1])
                fwd.append(_gmlp_pair_fwd(gv[:, sl], w0, w1, b_ref[p], lo))
            yg = jnp.concatenate([gu[:, 128 * p:128 * p + 128] * fwd[p][3] for p in range(npair)], axis=1)
            dyg, dg = _rms_bwd(yg, g_ref[...], dy_ref[0, rows, :].astype(F32), D_GMLP)
            dg_ref[...] += dg
            du_parts, dv_parts = [], []
            for p in range(npair):
                sl = slice(128 * p, 128 * p + 128)
                vn, vnb, rstd, mixed = fwd[p]
                dyg_p = dyg[:, sl]
                dmixed = dyg_p * gu[:, sl]
                dbacc[p] += dmixed
                dm_lo = jnp.where(lo, dmixed, 0.0).astype(BF16)
                dm_hi = jnp.where(lo, 0.0, dmixed).astype(BF16)
                dws_ref[2 * p] += jnp.where(tril, _dot(dm_lo, vnb, NT), 0.0)
                dws_ref[2 * p + 1] += jnp.where(tril, _dot(dm_hi, vnb, NT), 0.0)
                dmb = dmixed.astype(BF16)
                dvn = jnp.where(lo, _dot(wst_ref[2 * p], dmb), _dot(wst_ref[2 * p + 1], dmb))
                dgv = rstd * (dvn - _pair_mean_exact(dvn, lo) - vn * _pair_mean_exact(dvn * vn, lo))
                dv_parts.append(dgv * dgv_dv[:, sl])
                du_parts.append(dyg_p * mixed * dgu[:, sl])
            duv_ref[0, rows, :] = jnp.concatenate(du_parts + dv_parts, axis=1).astype(BF16)

        @pl.when(last)
        def _():
            sel = jnp.where(_iota((8, 128), 0) == 0, (_iota((8, 128), 1) < GROUP_DIM).astype(F32),
                            jnp.where(_iota((8, 128), 0) == 1, (_iota((8, 128), 1) >= GROUP_DIM).astype(F32), 0.0))
            for p in range(npair):
                dbs_ref[p] = lax.dot_general(sel, dbacc[p], NT, precision=lax.Precision.HIGHEST,
                                             preferred_element_type=F32)

    duv, dws, dbs, dg = pl.pallas_call(
        body, name=name, grid=(bsz, nc),
        in_specs=[pl.BlockSpec((1, tb, D_GMLP), lambda b, i: (b, i, 0)),
                  pl.BlockSpec((1, tb, D_GMLP), lambda b, i: (b, i, 1)),
                  pl.BlockSpec((1, tb, D_GMLP), lambda b, i: (b, i, dy_col)),
                  pl.BlockSpec((GROUPS, CHUNK, CHUNK), lambda b, i: (0, 0, 0)),
                  pl.BlockSpec((GROUPS, CHUNK, CHUNK), lambda b, i: (0, 0, 0)),
                  pl.BlockSpec((npair, CHUNK, 128), lambda b, i: (0, 0, 0)),
                  pl.BlockSpec((1, D_GMLP), lambda b, i: (0, 0))],
        out_specs=[pl.BlockSpec((1, tb, 2 * D_GMLP), lambda b, i: (b, i, 0)),
                   pl.BlockSpec((GROUPS, CHUNK, CHUNK), lambda b, i: (0, 0, 0)),
                   pl.BlockSpec((npair, 8, CHUNK), lambda b, i: (0, 0, 0)),
                   pl.BlockSpec((1, D_GMLP), lambda b, i: (0, 0))],
        out_shape=[jax.ShapeDtypeStruct((bsz, seq, D_IN_PAD), BF16),
                   jax.ShapeDtypeStruct((GROUPS, CHUNK, CHUNK), F32),
                   jax.ShapeDtypeStruct((npair, 8, CHUNK), F32),
                   jax.ShapeDtypeStruct((1, D_GMLP), F32)],
        scratch_shapes=[pltpu.VMEM((npair, CHUNK, 128), F32)],
        compiler_params=_cparams(),
    )(z3, z3, dyn3, ws, wst, bexp, g_out)
    return duv, dws, dbs[:, :2, :].reshape(GROUPS, CHUNK), dg


def _partner(x):
    width = x.shape[-1]
    lane = _iota(x.shape, x.ndim - 1) % HEAD_PAD
    up = pltpu.roll(x, width - ROPE // 2, x.ndim - 1)
    down = pltpu.roll(x, ROPE // 2, x.ndim - 1)
    first = jnp.logical_and(lane >= NOPE, lane < NOPE + ROPE // 2)
    second = jnp.logical_and(lane >= NOPE + ROPE // 2, lane < NOPE + ROPE)
    return jnp.where(first, up, jnp.where(second, down, 0.0))


def _mla_prep_fwd(z3, g_q, g_kv, w_uq, w_ukv, ctab, stab, *, name, tb=512):
    bsz, seq, _ = z3.shape
    tb = min(tb, seq)
    hw = HEADS * HEAD_PAD

    def body(ql_ref, kvl_ref, krl_ref, gq_ref, gkv_ref, wuq_ref, wukv_ref, c_ref, s_ref, q_ref, kv_ref, kp_ref):
        cq = _rms_fwd(ql_ref[0].astype(F32), gq_ref[...], Q_RANK).astype(BF16)
        q = _dot(cq, wuq_ref[...])
        c1, s1 = c_ref[0], s_ref[0]
        c8, s8 = jnp.tile(c1, (1, HEADS)), jnp.tile(s1, (1, HEADS))
        q_ref[0] = ((q * c8 + _partner(q) * s8) * SCALE_LOG2).astype(BF16)
        ckv = _rms_fwd(kvl_ref[0].astype(F32), gkv_ref[...], KV_RANK).astype(BF16)
        kv = _dot(ckv, wukv_ref[...])
        kv_ref[0] = kv.astype(BF16)
        kr = krl_ref[0].astype(F32)
        kr = kr * c1 + _partner(kr) * s1
        lane = _iota((tb, hw), 1) % HEAD_PAD
        kp_ref[0] = jnp.where(lane < NOPE, kv, jnp.tile(kr, (1, HEADS))).astype(BF16)

    return pl.pallas_call(
        body, name=name, grid=(bsz, seq // tb),
        in_specs=[pl.BlockSpec((1, tb, Q_RANK), lambda b, i: (b, i, 4)),
                  pl.BlockSpec((1, tb, KV_RANK), lambda b, i: (b, i, 10)),
                  pl.BlockSpec((1, tb, HEAD_PAD), lambda b, i: (b, i, 11)),
                  pl.BlockSpec((1, Q_RANK), lambda b, i: (0, 0)),
                  pl.BlockSpec((1, KV_RANK), lambda b, i: (0, 0)),
                  pl.BlockSpec((Q_RANK, hw), lambda b, i: (0, 0)),
                  pl.BlockSpec((KV_RANK, hw), lambda b, i: (0, 0)),
                  pl.BlockSpec((1, tb, HEAD_PAD), lambda b, i: (b, i, 0)),
                  pl.BlockSpec((1, tb, HEAD_PAD), lambda b, i: (b, i, 0))],
        out_specs=[pl.BlockSpec((1, tb, hw), lambda b, i: (b, i, 0))] * 3,
        out_shape=[jax.ShapeDtypeStruct((bsz, seq, hw), BF16)] * 3,
        compiler_params=_cparams(),
    )(z3, z3, z3, g_q, g_kv, w_uq, w_ukv, ctab, stab)


def _mla_prep_bwd(z3, dz3, dq3, dk3, dv3, g_q, g_kv, w_uq, w_ukv, ctab, stab, *, name, tb=512):
    bsz, seq, _ = z3.shape
    tb = min(tb, seq)
    hw = HEADS * HEAD_PAD
    nb = seq // tb

    def body(ql_ref, kvl_ref, dq_ref, dk_ref, dv_ref, gq_ref, gkv_ref, wuq_ref, wukv_ref, c_ref, s_ref, dz_in,
             dz_ref, cq_ref, dqb_ref, ckv_ref, dkvb_ref, dgq_ref, dgkv_ref):
        @pl.when(jnp.logical_and(pl.program_id(0) == 0, pl.program_id(1) == 0))
        def _():
            dgq_ref[...] = jnp.zeros_like(dgq_ref)
            dgkv_ref[...] = jnp.zeros_like(dgkv_ref)

        c1, s1 = c_ref[0], s_ref[0]
        c8, s8 = jnp.tile(c1, (1, HEADS)), jnp.tile(s1, (1, HEADS))
        dqr = dq_ref[0]
        dqb = (dqr * c8 + _partner(dqr * s8)).astype(BF16)
        dqb_ref[0] = dqb
        ql = ql_ref[0].astype(F32)
        cq_ref[0] = _rms_fwd(ql, gq_ref[...], Q_RANK).astype(BF16)
        dql, dgq = _rms_bwd(ql, gq_ref[...], _dot(dqb, wuq_ref[...], NT), Q_RANK)
        dgq_ref[...] += dgq

        dk = dk_ref[0]
        lane = _iota((tb, hw), 1) % HEAD_PAD
        dkvb = jnp.where(lane < NOPE, dk, dv_ref[0]).astype(BF16)
        dkvb_ref[0] = dkvb
        kvl = kvl_ref[0].astype(F32)
        ckv_ref[0] = _rms_fwd(kvl, gkv_ref[...], KV_RANK).astype(BF16)
        dkvl, dgkv = _rms_bwd(kvl, gkv_ref[...], _dot(dkvb, wukv_ref[...], NT), KV_RANK)
        dgkv_ref[...] += dgkv

        dkr = dk[:, 0:HEAD_PAD].astype(F32)
        for h in range(1, HEADS):
            dkr = dkr + dk[:, HEAD_PAD * h:HEAD_PAD * (h + 1)].astype(F32)
        lane1 = _iota((tb, HEAD_PAD), 1)
        dkr = jnp.where(jnp.logical_and(lane1 >= NOPE, lane1 < NOPE + ROPE), dkr, 0.0)
        dkrl = dkr * c1 + _partner(dkr * s1)
        dz_ref[0] = jnp.concatenate([dql, dkvl, dkrl], axis=1).astype(BF16)

    return pl.pallas_call(
        body, name=name, grid=(bsz, nb),
        in_specs=[pl.BlockSpec((1, tb, Q_RANK), lambda b, i: (b, i, 4)),
                  pl.BlockSpec((1, tb, KV_RANK), lambda b, i: (b, i, 10)),
                  pl.BlockSpec((1, tb, hw), lambda b, i: (b, i, 0)),
                  pl.BlockSpec((1, tb, hw), lambda b, i: (b, i, 0)),
                  pl.BlockSpec((1, tb, hw), lambda b, i: (b, i, 0)),
                  pl.BlockSpec((1, Q_RANK), lambda b, i: (0, 0)),
                  pl.BlockSpec((1, KV_RANK), lambda b, i: (0, 0)),
                  pl.BlockSpec((Q_RANK, hw), lambda b, i: (0, 0)),
                  pl.BlockSpec((KV_RANK, hw), lambda b, i: (0, 0)),
                  pl.BlockSpec((1, tb, HEAD_PAD), lambda b, i: (b, i, 0)),
                  pl.BlockSpec((1, tb, HEAD_PAD), lambda b, i: (b, i, 0)),
                  ANY_SPEC],
        out_specs=[pl.BlockSpec((1, tb, 512), lambda b, i: (b, i, 2)),
                   pl.BlockSpec((1, tb, Q_RANK), lambda b, i: (b, i, 0)),
                   pl.BlockSpec((1, tb, hw), lambda b, i: (b, i, 0)),
                   pl.BlockSpec((1, tb, KV_RANK), lambda b, i: (b, i, 0)),
                   pl.BlockSpec((1, tb, hw), lambda b, i: (b, i, 0)),
                   pl.BlockSpec((1, Q_RANK), lambda b, i: (0, 0)),
                   pl.BlockSpec((1, KV_RANK), lambda b, i: (0, 0))],
        out_shape=[jax.ShapeDtypeStruct((bsz, seq, D_IN_PAD), BF16),
                   jax.ShapeDtypeStruct((bsz, seq, Q_RANK), BF16),
                   jax.ShapeDtypeStruct((bsz, seq, hw), BF16),
                   jax.ShapeDtypeStruct((bsz, seq, KV_RANK), BF16),
                   jax.ShapeDtypeStruct((bsz, seq, hw), BF16),
                   jax.ShapeDtypeStruct((1, Q_RANK), F32),
                   jax.ShapeDtypeStruct((1, KV_RANK), F32)],
        input_output_aliases={11: 0},
        compiler_params=_cparams(),
    )(z3, z3, dq3, dk3, dv3, g_q, g_kv, w_uq, w_ukv, ctab, stab, dz3)


ATTN_HEADS_PER_STEP = 4


def _attn_specs(tq, seq, hp):
    blk = pl.BlockSpec((1, tq, hp * HEAD_PAD), lambda b, h, i: (b, i, h))
    full = pl.BlockSpec((1, seq, hp * HEAD_PAD), lambda b, h, i: (b, 0, h))
    return blk, full


def _head(h):
    return slice(HEAD_PAD * h, HEAD_PAD * (h + 1))


def _attn_fwd(q3, kv3, kp3, *, name, tq=512, hp=ATTN_HEADS_PER_STEP, side=None):
    bsz, seq, hw = q3.shape
    tq = min(tq, seq)
    blk, full = _attn_specs(tq, seq, hp)

    def body(q_ref, kv_ref, kp_ref, o_ref, lse_ref):
        i = pl.program_id(2)

        def update(state, q, kp, kv, mask=None):
            m, l, acc = state
            s = _dot(q, kp, NT)
            if mask is not None:
                s = jnp.where(mask, s, -1e30)
            m_new = jnp.maximum(m, jnp.max(s, axis=1, keepdims=True))
            alpha = jnp.exp2(m - m_new)
            p = jnp.exp2(s - m_new)
            return m_new, alpha * l + jnp.sum(p, axis=1, keepdims=True), alpha * acc + _dot(p.astype(BF16), kv)

        def step(j, carry):
            st = pl.multiple_of(j * tq, tq)
            return tuple(update(carry[h], q_ref[0, :, _head(h)], kp_ref[0, pl.ds(st, tq), _head(h)],
                                kv_ref[0, pl.ds(st, tq), _head(h)]) for h in range(hp))

        init = tuple((jnp.full((tq, 1), -1e30, F32), jnp.zeros((tq, 1), F32), jnp.zeros((tq, HEAD_PAD), F32))
                     for _ in range(hp))
        carry = lax.fori_loop(0, i, step, init)

        st = pl.multiple_of(i * tq, tq)
        is_nope = _iota((tq, HEAD_PAD), 1) < NOPE
        causal = _iota((tq, tq), 1) <= _iota((tq, tq), 0)
        for h in range(hp):
            m, l, acc = update(carry[h], q_ref[0, :, _head(h)], kp_ref[0, pl.ds(st, tq), _head(h)],
                               kv_ref[0, pl.ds(st, tq), _head(h)], causal)
            o_ref[0, :, _head(h)] = jnp.where(is_nope, 0.0, acc / l).astype(BF16)
            lse_ref[0, :, _head(h)] = jnp.broadcast_to(m + jnp.log(l) * LOG2E, (tq, HEAD_PAD))

    outs, side_outs = _hosted_call(
        body, name=name, grid=(bsz, HEADS // hp, seq // tq),
        in_specs=[blk, full, full],
        out_specs=[blk, blk],
        out_shape=[jax.ShapeDtypeStruct((bsz, seq, hw), BF16), jax.ShapeDtypeStruct((bsz, seq, hw), F32)],
        args=(q3, kv3, kp3), side=side)
    return outs if side is None else (outs, side_outs)


def _attn_bwd(q3, kv3, kp3, do3, lse3, dl3, *, name, tq=512, hp=ATTN_HEADS_PER_STEP, side=None):
    bsz, seq, hw = q3.shape
    tq = min(tq, seq)
    nq = seq // tq
    blk, full = _attn_specs(tq, seq, hp)

    def body(kv_ref, kp_ref, q_ref, do_ref, lse_ref, dl_ref, dq_ref, dk_ref, dv_ref):
        j = pl.program_id(2)

        @pl.when(j == 0)
        def _():
            dq_ref[...] = jnp.zeros_like(dq_ref)

        def pair(h, row0, nrows, nkeys, mask=None):
            row0 = pl.multiple_of(row0, nrows)
            qi = q_ref[0, pl.ds(row0, nrows), _head(h)]
            do = do_ref[0, pl.ds(row0, nrows), _head(h)]
            kp = kp_ref[0, :nkeys, _head(h)]
            s = _dot(qi, kp, NT)
            if mask is not None:
                s = jnp.where(mask, s, -1e30)
            wide = nkeys // HEAD_PAD
            p = jnp.exp2(s - jnp.tile(lse_ref[0, pl.ds(row0, nrows), _head(h)], (1, wide)))
            dv = _dot(p.astype(BF16), do, TN)
            dp = _dot(do, kv_ref[0, :nkeys, _head(h)], NT)
            ds = (p * (dp - jnp.tile(dl_ref[0, pl.ds(row0, nrows), _head(h)], (1, wide)))).astype(BF16)
            dq_ref[0, pl.ds(row0, nrows), _head(h)] += _dot(ds, kp)
            return _dot(ds, qi, TN), dv

        def step(i, carry):
            st = pl.multiple_of(i * tq, tq)
            out = []
            for h in range(hp):
                dk, dv = pair(h, st, tq, tq)
                out.append((carry[h][0] + dk, carry[h][1] + dv))
            return tuple(out)

        causal = _iota((tq, tq), 1) <= _iota((tq, tq), 0)
        carry = tuple(pair(h, pl.multiple_of(j * tq, tq), tq, tq, causal) for h in range(hp))
        carry = lax.fori_loop(j + 1, nq, step, carry)
        for h in range(hp):
            dk_ref[0, :, _head(h)] = (carry[h][0] * (1.0 / LOG2E)).astype(BF16)
            dv_ref[0, :, _head(h)] = carry[h][1].astype(BF16)

        @pl.when(j == nq - 1)
        def _():
            dq_ref[...] = dq_ref[...] * ATTN_SCALE

    outs, side_outs = _hosted_call(
        body, name=name, grid=(bsz, HEADS // hp, nq),
        in_specs=[blk, blk, full, full, full, full],
        out_specs=[full, blk, blk],
        out_shape=[jax.ShapeDtypeStruct((bsz, seq, hw), F32)] + [jax.ShapeDtypeStruct((bsz, seq, hw), BF16)] * 2,
        args=(kv3, kp3, q3, do3, lse3, dl3), side=side)
    return outs if side is None else (outs, side_outs)


def _onorm_fwd(o3, yg3, g_pad, *, name, tb=1024):
    bsz, seq, hw = o3.shape
    wg = yg3.shape[-1]
    tb = min(tb, seq)

    def body(o_ref, yg_ref, g_ref, y_ref):
        ya = _rms_fwd(o_ref[0].astype(F32), g_ref[...], HEADS * 64).astype(BF16)
        y_ref[0] = jnp.concatenate([ya, yg_ref[0]], axis=1)

    return pl.pallas_call(
        body, name=name, grid=(bsz, seq // tb),
        in_specs=[pl.BlockSpec((1, tb, hw), lambda b, i: (b, i, 0)),
                  pl.BlockSpec((1, tb, wg), lambda b, i: (b, i, 0)),
                  pl.BlockSpec((1, hw), lambda b, i: (0, 0))],
        out_specs=pl.BlockSpec((1, tb, hw + wg), lambda b, i: (b, i, 0)),
        out_shape=jax.ShapeDtypeStruct((bsz, seq, hw + wg), BF16),
        compiler_params=_cparams(),
    )(o3, yg3, g_pad)


def _onorm_bwd(o3, dy3, g_pad, *, name, tb=1024):
    bsz, seq, hw = o3.shape
    tb = min(tb, seq)

    def body(o_ref, dy_ref, g_ref, do_ref, dl_ref, dg_ref):
        @pl.when(jnp.logical_and(pl.program_id(0) == 0, pl.program_id(1) == 0))
        def _():
            dg_ref[...] = jnp.zeros_like(dg_ref)

        o = o_ref[0].astype(F32)
        do, dg = _rms_bwd(o, g_ref[...], dy_ref[0].astype(F32), HEADS * 64)
        dg_ref[...] += dg
        do_ref[0] = do.astype(BF16)
        prod = do * o
        parts = []
        for h in range(HEADS):
            sh = jnp.sum(prod[:, HEAD_PAD * h:HEAD_PAD * (h + 1)], axis=1, keepdims=True)
            parts.append(jnp.broadcast_to(sh, (tb, HEAD_PAD)))
        dl_ref[0] = jnp.concatenate(parts, axis=1)

    return pl.pallas_call(
        body, name=name, grid=(bsz, seq // tb),
        in_specs=[pl.BlockSpec((1, tb, hw), lambda b, i: (b, i, 0)),
                  pl.BlockSpec((1, tb, hw), lambda b, i: (b, i, 0)),
                  pl.BlockSpec((1, hw), lambda b, i: (0, 0))],
        out_specs=[pl.BlockSpec((1, tb, hw), lambda b, i: (b, i, 0)),
                   pl.BlockSpec((1, tb, hw), lambda b, i: (b, i, 0)),
                   pl.BlockSpec((1, hw), lambda b, i: (0, 0))],
        out_shape=[jax.ShapeDtypeStruct((bsz, seq, hw), BF16),
                   jax.ShapeDtypeStruct((bsz, seq, hw), F32),
                   jax.ShapeDtypeStruct((1, hw), F32)],
        compiler_params=_cparams(),
    )(o3, dy3, g_pad)


def _resnode_bwd(x3, g, *, name, target3=None, dh3=None, dres3=None, mod_nm=None, rows=None,
                 branch3=None, mod_gate=None, gate_row=None, tb=1024, side=None):
    bsz, seq, d = x3.shape
    tb = min(tb, seq)
    final = target3 is not None
    has_branch = branch3 is not None
    row_spec = pl.BlockSpec((1, tb, d), lambda b, i: (b, i, 0))
    vec_spec = pl.BlockSpec((1, d), lambda b, i: (0, 0))
    mod_spec = pl.BlockSpec((1, MOD_ROWS, d), lambda b, i: (b, 0, 0))

    ins, in_specs = [x3, g], [row_spec, vec_spec]
    if final:
        ins += [target3]
        in_specs += [row_spec]
    else:
        ins += [dh3, dres3, mod_nm]
        in_specs += [row_spec, row_spec, mod_spec]
    if has_branch:
        ins += [branch3, mod_gate]
        in_specs += [row_spec, mod_spec]

    out_names = ["dx", "dg"]
    out_specs = [row_spec, vec_spec]
    out_shape = [jax.ShapeDtypeStruct((bsz, seq, d), F32), jax.ShapeDtypeStruct((1, d), F32)]
    if final:
        out_names += ["loss"]
        out_specs += [pl.BlockSpec((1, 128), lambda b, i: (0, 0))]
        out_shape += [jax.ShapeDtypeStruct((1, 128), F32)]
    else:
        out_names += ["dnm"]
        out_specs += [mod_spec]
        out_shape += [jax.ShapeDtypeStruct((bsz, MOD_ROWS, d), F32)]
    if has_branch:
        out_names += ["dbr", "dgate"]
        out_specs += [row_spec, mod_spec]
        out_shape += [jax.ShapeDtypeStruct((bsz, seq, d), BF16), jax.ShapeDtypeStruct((bsz, MOD_ROWS, d), F32)]
    n_in = len(ins)

    def body(*refs):
        r = dict(zip(["x", "g"] + (["t"] if final else ["dh", "dres", "nm"]) + (["br", "gm"] if has_branch else []),
                     refs[:n_in]))
        o = dict(zip(out_names, refs[n_in:]))
        b_first = pl.program_id(1) == 0
        first = jnp.logical_and(pl.program_id(0) == 0, b_first)
        rowid = _iota((MOD_ROWS, d), 0)

        @pl.when(first)
        def _():
            o["dg"][...] = jnp.zeros_like(o["dg"])
            if final:
                o["loss"][...] = jnp.zeros_like(o["loss"])

        @pl.when(b_first)
        def _():
            if not final:
                o["dnm"][...] = jnp.zeros_like(o["dnm"])
            if has_branch:
                o["dgate"][...] = jnp.zeros_like(o["dgate"])

        x = r["x"][0]
        gv = r["g"][...]
        if final:
            e = _rms_fwd(x, gv, d) - r["t"][0]
            sq = jnp.sum(jnp.sum(e * e, axis=1, keepdims=True), axis=0, keepdims=True)
            o["loss"][...] += jnp.broadcast_to(sq * (0.5 / d), (1, 128))
            dx, dg = _rms_bwd(x, gv, e * (1.0 / d), d)
        else:
            m = r["nm"][0]
            dh = r["dh"][0].astype(F32)
            scale = m[rows[1]:rows[1] + 1, :]
            rstd = lax.rsqrt(jnp.sum(x * x, axis=-1, keepdims=True) * (1.0 / d) + EPS)
            xh = x * rstd
            nrm = xh * gv
            dshift = jnp.sum(dh, axis=0, keepdims=True)
            dscale = jnp.sum(dh * nrm, axis=0, keepdims=True)
            o["dnm"][0] += jnp.where(rowid == 0, dshift, jnp.where(rowid == 1, dscale, 0.0))
            dn = dh * (1.0 + scale)
            dg = jnp.sum(dn * xh, axis=0, keepdims=True)
            dxh = dn * gv
            dx = rstd * (dxh - xh * (jnp.sum(dxh * xh, axis=-1, keepdims=True) * (1.0 / d))) + r["dres"][0]
        o["dg"][...] += dg
        o["dx"][0] = dx
        if has_branch:
            gate = r["gm"][0][gate_row:gate_row + 1, :]
            o["dbr"][0] = (gate * dx).astype(BF16)
            dgate = jnp.sum(dx * r["br"][0], axis=0, keepdims=True)
            o["dgate"][0] += jnp.where(rowid == 0, dgate, 0.0)

    outs, side_outs = _hosted_call(
        body, name=name, grid=(bsz, seq // tb),
        in_specs=in_specs, out_specs=out_specs, out_shape=out_shape, args=tuple(ins), side=side)
    res = dict(zip(out_names, outs))
    return res if side is None else (res, side_outs)


def _adamw(w, g, m, v, *, name):
    shape = w.shape
    cols = shape[-1]
    rows = w.size // cols
    tr = _pick_rows(rows, max(8, (512 * 1024) // cols // 8 * 8))

    def body(w_ref, g_ref, m_ref, v_ref, d_ref, nm_ref, nv_ref):
        d_ref[...], nm_ref[...], nv_ref[...] = _adamw_math(w_ref[...], g_ref[...], m_ref[...], v_ref[...])

    if w.ndim == 3 and shape[1] % 8 == 0:
        tr3 = _pick_rows(shape[1], max(8, (512 * 1024) // cols // 8 * 8))
        spec3 = pl.BlockSpec((None, tr3, cols), lambda l, i: (l, i, 0))
        return tuple(pl.pallas_call(
            body, name=name, grid=(shape[0], shape[1] // tr3),
            in_specs=[spec3] * 4, out_specs=[spec3] * 3,
            out_shape=[jax.ShapeDtypeStruct(shape, F32)] * 3,
            compiler_params=_cparams(),
        )(w, g, m, v))
    spec = pl.BlockSpec((tr, cols), lambda i: (i, 0))
    outs = pl.pallas_call(
        body, name=name, grid=(rows // tr,),
        in_specs=[spec] * 4, out_specs=[spec] * 3,
        out_shape=[jax.ShapeDtypeStruct((rows, cols), F32)] * 3,
        compiler_params=_cparams(),
    )(*[t.reshape(rows, cols) for t in (w, g, m, v)])
    return tuple(o.reshape(shape) for o in outs)


def _adamw_math(w, g, m, v):
    c1 = 1.0 - ADAM_B1 ** ADAM_STEP
    c2 = 1.0 - ADAM_B2 ** ADAM_STEP
    nm = ADAM_B1 * m + (1.0 - ADAM_B1) * g
    nv = ADAM_B2 * v + (1.0 - ADAM_B2) * (g * g)
    delta = -ADAM_LR * ((nm / c1) / (jnp.sqrt(nv / c2) + ADAM_EPS) + ADAM_WD * w)
    return delta, nm, nv


def _adamw_layers(w, m, v, bufs, row_off, *, name, tr=256):
    depth, rows, cols = w.shape
    tr = min(tr, rows)
    assert rows % tr == 0 and row_off % tr == 0

    outs = None
    for l in range(depth):
        def body(w_ref, g_ref, m_ref, v_ref, *rest):
            go_ref, d_ref, nm_ref, nv_ref = rest[-4:]
            g = g_ref[...]
            go_ref[...] = g
            d_ref[...], nm_ref[...], nv_ref[...] = _adamw_math(w_ref[...], g, m_ref[...], v_ref[...])

        layer = pl.BlockSpec((None, tr, cols), lambda i, l=l: (l, i, 0))
        prev = () if outs is None else tuple(outs)
        outs = pl.pallas_call(
            body, name=f"{name}_l{l}", grid=(rows // tr,),
            in_specs=[layer, pl.BlockSpec((tr, cols), lambda i: (row_off // tr + i, 0)), layer, layer]
            + [ANY_SPEC] * len(prev),
            out_specs=[layer] * 4,
            out_shape=[jax.ShapeDtypeStruct(w.shape, F32)] * 4,
            input_output_aliases={4 + k: k for k in range(len(prev))},
            compiler_params=_cparams(),
        )(w, bufs[l], m, v, *prev)
    return tuple(outs)


def _sum_leading(x, *, name, tr=256):
    n, rows, cols = x.shape
    tr = _pick_rows(rows, tr)

    def body(x_ref, o_ref):
        acc = x_ref[0]
        for k in range(1, n):
            acc = acc + x_ref[k]
        o_ref[...] = acc

    return pl.pallas_call(
        body, name=name, grid=(rows // tr,),
        in_specs=[pl.BlockSpec((n, tr, cols), lambda i: (0, i, 0))],
        out_specs=pl.BlockSpec((tr, cols), lambda i: (i, 0)),
        out_shape=jax.ShapeDtypeStruct((rows, cols), F32),
        compiler_params=_cparams(),
    )(x)


def _position():
    return lax.axis_index("x"), lax.axis_index("y"), lax.axis_index("c")


def _allgather8(x, *, name):
    shape = x.shape

    def body(x_ref, out_ref, send_sems, recv_sems, local_sem):
        px, py, pc = _position()
        me, sibling = (px, py, pc), (px, py, 1 - pc)
        chips = [(1 - px, py), (px, 1 - py), (1 - px, 1 - py)]
        src_own = x_ref

        def slot(qx, qy, qc):
            return out_ref.at[4 * qx + 2 * qy + qc]

        def copy(k, block, to, src=None):
            return pltpu.make_async_remote_copy(
                src_ref=slot(*block) if src is None else src, dst_ref=slot(*block),
                send_sem=send_sems.at[k], recv_sem=recv_sems.at[k], device_id=to, device_id_type=MESH)

        mine = pltpu.make_async_copy(src_own, slot(*me), local_sem)
        mine.start()
        first = [copy(0, me, sibling, src=src_own)]
        first += [copy(1 + j, me, (*chip, pc), src=src_own) for j, chip in enumerate(chips)]
        for cp in first:
            cp.start()
        passed = [copy(4 + j, (*chip, pc), sibling) for j, chip in enumerate(chips)]
        for j, chip in enumerate(chips):
            copy(1 + j, (*chip, pc), me).wait_recv()
            passed[j].start()
        copy(0, sibling, me).wait_recv()
        for j, chip in enumerate(chips):
            copy(4 + j, (*chip, 1 - pc), me).wait_recv()
        for cp in first + passed:
            cp.wait_send()
        mine.wait()

    return pl.pallas_call(
        body, name=name,
        out_shape=jax.ShapeDtypeStruct((N_DEV,) + shape, x.dtype),
        in_specs=[pl.BlockSpec(memory_space=pl.ANY)],
        out_specs=pl.BlockSpec(memory_space=pl.ANY),
        scratch_shapes=[pltpu.SemaphoreType.DMA((7,)), pltpu.SemaphoreType.DMA((7,)), pltpu.SemaphoreType.DMA],
    )(x)


class _Exchange:
    def __init__(self, ins, out_shapes, n, build, aliases=None):
        self.ins, self.out_shapes, self.n, self.build = tuple(ins), tuple(out_shapes), n, build
        self.aliases = dict(aliases or {})

    def _descriptors(self, in_refs, out_refs, send_sems, recv_sems):
        sends, recvs = [], []
        for k, (src, dst, peer, landing) in enumerate(self.build(in_refs, out_refs)):
            sends.append(pltpu.make_async_remote_copy(
                src_ref=src, dst_ref=dst, send_sem=send_sems.at[k], recv_sem=recv_sems.at[k],
                device_id=peer, device_id_type=MESH))
            recvs.append(pltpu.make_async_remote_copy(
                src_ref=src, dst_ref=landing, send_sem=send_sems.at[k], recv_sem=recv_sems.at[k],
                device_id=peer, device_id_type=MESH))
        return sends, recvs

    def start(self, *refs):
        for cp in self._descriptors(*refs)[0]:
            cp.start()

    def finish(self, *refs):
        sends, recvs = self._descriptors(*refs)
        for cp in recvs:
            cp.wait_recv()
        for cp in sends:
            cp.wait_send()


ANY_SPEC = pl.BlockSpec(memory_space=pl.ANY)


def _hosted_call(body, *, name, grid, in_specs, out_specs, out_shape, args, scratch_shapes=(), side=None,
                 num_scalar_prefetch=0, io_aliases=None):
    in_specs, out_specs, out_shape = list(in_specs), list(out_specs), list(out_shape)
    n_in, n_out = len(in_specs) + num_scalar_prefetch, len(out_specs)
    kernel_body = body
    aliases = dict(io_aliases or {})
    if side is not None:
        s_in, s_out = len(side.ins), len(side.out_shapes)
        aliases.update({n_in + i: n_out + o for i, o in side.aliases.items()})

        def kernel_body(*refs):
            ins, s_ins = refs[:n_in], refs[n_in:n_in + s_in]
            outs = refs[n_in + s_in:n_in + s_in + n_out]
            s_outs = refs[n_in + s_in + n_out:n_in + s_in + n_out + s_out]
            scratch, sems = refs[n_in + s_in + n_out + s_out:-2], refs[-2:]
            first = functools.reduce(jnp.logical_and, [pl.program_id(a) == 0 for a in range(len(grid))])
            last = functools.reduce(jnp.logical_and, [pl.program_id(a) == g - 1 for a, g in enumerate(grid)])

            @pl.when(first)
            def _():
                side.start(s_ins, s_outs, *sems)

            body(*ins, *outs, *scratch)

            @pl.when(last)
            def _():
                side.finish(s_ins, s_outs, *sems)

        in_specs += [ANY_SPEC] * s_in
        out_specs += [ANY_SPEC] * s_out
        out_shape += list(side.out_shapes)
        scratch_shapes = list(scratch_shapes) + [pltpu.SemaphoreType.DMA((side.n,)),
                                                 pltpu.SemaphoreType.DMA((side.n,))]
        args = tuple(args) + side.ins
    if num_scalar_prefetch:
        grid_spec = pltpu.PrefetchScalarGridSpec(num_scalar_prefetch=num_scalar_prefetch, grid=grid,
                                                 in_specs=in_specs, out_specs=out_specs,
                                                 scratch_shapes=list(scratch_shapes))
        outs = pl.pallas_call(kernel_body, name=name, grid_spec=grid_spec, out_shape=out_shape,
                              input_output_aliases=aliases, compiler_params=_cparams())(*args)
    else:
        outs = pl.pallas_call(kernel_body, name=name, grid=grid, in_specs=in_specs, out_specs=out_specs,
                              out_shape=out_shape, scratch_shapes=list(scratch_shapes),
                              input_output_aliases=aliases, compiler_params=_cparams())(*args)
    return tuple(outs[:n_out]), tuple(outs[n_out:])


def _run_exchange(ex, *, name):
    s_in = len(ex.ins)

    def body(*refs):
        ins, outs, sems = refs[:s_in], refs[s_in:-2], refs[-2:]
        ex.start(ins, outs, *sems)
        ex.finish(ins, outs, *sems)

    outs = pl.pallas_call(
        body, name=name, out_shape=list(ex.out_shapes),
        in_specs=[ANY_SPEC] * s_in, out_specs=[ANY_SPEC] * len(ex.out_shapes),
        scratch_shapes=[pltpu.SemaphoreType.DMA((ex.n,)), pltpu.SemaphoreType.DMA((ex.n,))],
        input_output_aliases=ex.aliases,
    )(*ex.ins)
    return tuple(outs)


def _both(a, b):
    na, oa = len(a.ins), len(a.out_shapes)

    def build(ins, outs):
        return a.build(ins[:na], outs[:oa]) + b.build(ins[na:], outs[oa:])

    aliases = dict(a.aliases)
    aliases.update({na + i: oa + o for i, o in b.aliases.items()})
    return _Exchange(a.ins + b.ins, a.out_shapes + b.out_shapes, a.n + b.n, build, aliases)


def _other_chips(px, py):
    return [(px, 1 - py), (1 - px, py), (1 - px, 1 - py)]


def _gather_spread(w_flat, halves=True):
    rows, w = w_flat.shape
    hr = rows // 2 if halves else rows

    def build(ins, outs):
        px, py, pc = _position()
        mine = ins[0].at[pl.ds(pc * hr, hr)] if halves else ins[0]
        me = 4 * px + 2 * py + pc
        plan = [((px, py, 1 - pc), me ^ 1)]
        plan += [((qx, qy, pc), 4 * qx + 2 * qy + pc) for qx, qy in _other_chips(px, py)]
        return [(mine, outs[0].at[me], peer, outs[0].at[their]) for peer, their in plan]

    return _Exchange([w_flat], [jax.ShapeDtypeStruct((N_DEV, hr, w), w_flat.dtype)], 4, build)


def _gather_pass_on(gath):
    def build(ins, outs):
        px, py, pc = _position()
        out = []
        for qx, qy in _other_chips(px, py):
            blk = 4 * qx + 2 * qy + pc
            out.append((outs[0].at[blk], outs[0].at[blk], (px, py, 1 - pc), outs[0].at[blk ^ 1]))
        return out

    return _Exchange([gath], [jax.ShapeDtypeStruct(gath.shape, gath.dtype)], 3, build, aliases={0: 0})


def _rs_halves(g):
    n, rows, w = g.shape
    hr = rows // 2

    def build(ins, outs):
        px, py, pc = _position()
        return [(ins[0].at[:, pl.ds((1 - pc) * hr, hr), :], outs[0], (px, py, 1 - pc), outs[0])]

    return _Exchange([g], [jax.ShapeDtypeStruct((n, hr, w), g.dtype)], 1, build)


def _rs_chips(sb):
    def build(ins, outs):
        px, py, pc = _position()
        return [(ins[0].at[j], outs[0].at[j], (qx, qy, pc), outs[0].at[j])
                for j, (qx, qy) in enumerate(_other_chips(px, py))]

    return _Exchange([sb], [jax.ShapeDtypeStruct(sb.shape, sb.dtype)], 3, build)


def _rs_complete(buf):
    def build(ins, outs):
        px, py, pc = _position()
        return [(outs[0].at[pc], outs[0].at[pc], (px, py, 1 - pc), outs[0].at[1 - pc])]

    return _Exchange([buf], [jax.ShapeDtypeStruct(buf.shape, buf.dtype)], 1, build, aliases={0: 0})


def _rs_partial(g, recv, ids, *, name, tr=128):
    _, rows, w = g.shape
    hr = rows // 2
    nb = hr // tr

    def body(ids_ref, g_ref, r_ref, o_ref):
        o_ref[0] = (g_ref[0] + r_ref[0]).astype(BF16)

    grid_spec = pltpu.PrefetchScalarGridSpec(
        num_scalar_prefetch=1, grid=(3, nb),
        in_specs=[pl.BlockSpec((1, tr, w), lambda j, i, ids: (ids[1] ^ (j + 1), ids[0] * nb + i, 0)),
                  pl.BlockSpec((1, tr, w), lambda j, i, ids: (ids[1] ^ (j + 1), i, 0))],
        out_specs=pl.BlockSpec((1, tr, w), lambda j, i, ids: (j, i, 0)))
    return pl.pallas_call(
        body, name=name, grid_spec=grid_spec,
        out_shape=jax.ShapeDtypeStruct((3, hr, w), BF16),
        compiler_params=_cparams(),
    )(ids, g, recv)


def _rs_total(g, recv, got, ids, *, name, tr=128):
    _, rows, w = g.shape
    hr = rows // 2
    nb = hr // tr

    def body(ids_ref, g_ref, r_ref, got_ref, o_ref):
        acc = g_ref[0] + r_ref[0]
        for j in range(3):
            acc = acc + got_ref[j].astype(F32)
        o_ref[0] = acc

    grid_spec = pltpu.PrefetchScalarGridSpec(
        num_scalar_prefetch=1, grid=(nb,),
        in_specs=[pl.BlockSpec((1, tr, w), lambda i, ids: (ids[1], ids[0] * nb + i, 0)),
                  pl.BlockSpec((1, tr, w), lambda i, ids: (ids[1], i, 0)),
                  pl.BlockSpec((3, tr, w), lambda i, ids: (0, i, 0))],
        out_specs=pl.BlockSpec((1, tr, w), lambda i, ids: (ids[0], i, 0)))
    return pl.pallas_call(
        body, name=name, grid_spec=grid_spec,
        out_shape=jax.ShapeDtypeStruct((2, hr, w), F32),
        compiler_params=_cparams(),
    )(ids, g, recv, got)


class _ReduceScatter:
    def __init__(self, g, ids, tag):
        self.g, self.ids, self.tag, self.stage, self.result = g, ids, tag, 0, None

    def next_exchange(self):
        if self.stage == 0:
            return _rs_halves(self.g)
        if self.stage == 1:
            return _rs_chips(self.sb)
        return _rs_complete(self.buf)

    def done(self, outs):
        if self.stage == 0:
            self.recv = outs[0]
            hr = self.recv.shape[1]
            self.tr = max(t for t in range(16, 513, 16) if hr % t == 0)
            self.sb = _rs_partial(self.g, self.recv, self.ids, name=f"{self.tag}_partial", tr=self.tr)
        elif self.stage == 1:
            self.buf = _rs_total(self.g, self.recv, outs[0], self.ids, name=f"{self.tag}_total", tr=self.tr)
        else:
            _, hr, w = outs[0].shape
            self.result = outs[0].reshape(2 * hr, w)
        self.stage += 1

    def finish_alone(self):
        names = ("halves", "chips", "complete")
        while self.stage < 3:
            self.done(_run_exchange(self.next_exchange(), name=f"{self.tag}_{names[self.stage]}"))
        return self.result


def _flat_rows():
    used = sum(r for _, r in FSDP_SECTIONS)
    return used, -(-used // ROW_ALIGN) * ROW_ALIGN


def _cols_to_chunks(full):
    rows, cols = full.shape
    t = full.reshape(rows, N_CHIPS, cols // N_CHIPS).transpose(1, 0, 2)
    return t.reshape(N_CHIPS, -1, FLAT_W)


def _chunks_to_cols(chunks, rows, cols):
    return chunks.reshape(N_CHIPS, rows, cols // N_CHIPS).transpose(1, 0, 2).reshape(rows, cols)


def _pad_heads(w, real):
    lead = w.shape[:-1]
    t = w.reshape(lead + (HEADS, real))
    t = jnp.pad(t, [(0, 0)] * len(lead) + [(0, 0), (0, HEAD_PAD - real)])
    return t.reshape(lead + (HEADS * HEAD_PAD,))


def _unpad_heads(w, real):
    lead = w.shape[:-1]
    return w.reshape(lead + (HEADS, HEAD_PAD))[..., :real].reshape(lead + (HEADS * real,))


def _pad_value_lanes(w, axis):
    w = jnp.moveaxis(w, axis, -1)
    lead = w.shape[:-1]
    t = w.reshape(lead + (HEADS, 64))
    t = jnp.pad(t, [(0, 0)] * len(lead) + [(0, 0), (HEAD_PAD - 64, 0)])
    return jnp.moveaxis(t.reshape(lead + (HEADS * HEAD_PAD,)), -1, axis)


def _unpad_value_lanes(w, axis):
    w = jnp.moveaxis(w, axis, -1)
    lead = w.shape[:-1]
    t = w.reshape(lead + (HEADS, HEAD_PAD))[..., HEAD_PAD - 64:]
    return jnp.moveaxis(t.reshape(lead + (HEADS * 64,)), -1, axis)


def _pad_w_in_t(wt):
    z = jnp.zeros((NOPE, wt.shape[1]), wt.dtype)
    z2 = jnp.zeros((HEAD_PAD - NOPE - ROPE, wt.shape[1]), wt.dtype)
    return jnp.concatenate([wt[:1408], z, wt[1408:], z2], axis=0)


def _unpad_w_in_t(wt):
    return jnp.concatenate([wt[:1408], wt[1408 + NOPE:1408 + NOPE + ROPE]], axis=0)


def _rope_tables(positions):
    freqs = ROPE_THETA ** (-jnp.arange(0, ROPE, 2, dtype=F32) / ROPE)
    ang = positions.astype(F32)[..., None] * freqs
    cos, sin = jnp.cos(ang), jnp.sin(ang)
    lead = cos.shape[:-1]
    ones = jnp.ones(lead + (NOPE,), F32)
    zeros_n = jnp.zeros(lead + (NOPE,), F32)
    zeros_p = jnp.zeros(lead + (HEAD_PAD - NOPE - ROPE,), F32)
    ctab = jnp.concatenate([ones, cos, cos, zeros_p], axis=-1)
    stab = jnp.concatenate([zeros_n, -sin, sin, zeros_p], axis=-1)
    return ctab, stab


def _mix_weights(full):
    return dict(
        w_in_t=_pad_w_in_t(full["w_in_t"]),
        w_uq=_pad_heads(full["mla_w_uq"], NOPE + ROPE),
        w_ukv=full["mla_w_ukv"],
        w_out=jnp.concatenate([_pad_value_lanes(full["w_out"][D_GMLP:], 0), full["w_out"][:D_GMLP]], axis=0),
    )


def _small_weights(p, l):
    ws = p["gmlp_ws"][l]
    tril = jnp.tril(jnp.ones((CHUNK, CHUNK), bool))
    bs = p["gmlp_bs"][l]
    bexp = jnp.repeat(bs.reshape(GROUPS // 2, 2, CHUNK).transpose(0, 2, 1), GROUP_DIM, axis=2)
    return dict(
        ws=ws,
        wst=jnp.where(tril[None], ws, 0.0).transpose(0, 2, 1).astype(BF16),
        bexp=bexp,
        g_mix=p["norm_mix_g"][l][None],
        g_ffn=p["norm_ffn_g"][l][None],
        g_q=p["mla_q_norm_g"][l][None],
        g_kv=p["mla_kv_norm_g"][l][None],
        g_og=p["out_norm_gmlp_g"][l][None],
        g_oa=_pad_value_lanes(p["out_norm_mla_g"][l], 0)[None],
    )


def _local_step(x3, target3, positions, mods, final_g, plan):
    bsz, seq, d = x3.shape
    tok = bsz * seq
    tmt = min(512, seq)
    tmk = min(1024, seq)
    tmw = min(2048, tok)
    chunk = (None, None, FLAT_W, FLAT_W)
    chunk2 = (2, None, FLAT_W, FLAT_W)
    ff_grad_shape = (N_CHIPS, 2 * FLAT_W, FLAT_W)
    ctab, stab = _rope_tables(positions)
    lw = [None] * DEPTH

    def flat(t):
        return t.reshape(tok, t.shape[-1])

    def cube(t):
        return t.reshape(bsz, seq, t.shape[-1])

    def carrying(l, tag, fn, *args, **kw):
        side = plan.host(l, tag)
        if side is None:
            return fn(*args, **kw)
        res, side_outs = fn(*args, side=side, **kw)
        plan.hosted(l, tag, side_outs)
        return res

    saved = []
    x = x3
    for l in range(DEPTH):
        lw[l] = plan.layer(l)
        w, mod = lw[l], mods[l]
        if l == 0:
            h1 = carrying(l, "fwd_normmod1", _normmod_fwd, x, w["g_mix"], mod, SHIFT1, SCALE1,
                          name=f"l{l}_normmod1")
        else:
            h1 = h1_next
        z = cube(_mm(flat(h1), w["w_in_t"], dims="nt", name=f"l{l}_w_in", tm=tmk, tn=D_IN_PAD, tk=d,
                     out_dtypes=(BF16,)))
        yg = _gmlp_fwd(z, w["ws"], w["bexp"], w["g_og"], name=f"l{l}_gmlp_fwd")
        q, kv, kp = _mla_prep_fwd(z, w["g_q"], w["g_kv"], w["w_uq"], w["w_ukv"], ctab, stab, name=f"l{l}_mla_prep")
        o, lse = carrying(l, "fwd_attn", _attn_fwd, q, kv, kp, name=f"l{l}_attn_fwd")
        y = _onorm_fwd(o, yg, w["g_oa"], name=f"l{l}_onorm_fwd")

        def normmod(xv, gv, gm, shift_row, scale_row):
            m = gm[0]
            return _rms_fwd(xv, gv, d) * (1.0 + m[scale_row:scale_row + 1, :]) + m[shift_row:shift_row + 1, :]

        def out_epi(po, xv, gm, gf):
            x_new = xv + gm[0][GATE1:GATE1 + 1, :] * po
            return po, x_new, normmod(x_new, gf, gm, SHIFT2, SCALE2)

        vec_spec = pl.BlockSpec((1, d), lambda i, j, k: (0, j))
        po, x_mid, h2 = carrying(l, "fwd_out_a", _mm, flat(y), w["w_out"], dims="nn", name=f"l{l}_w_out",
                                 tm=tmk, tn=d, tk=y.shape[-1], out_dtypes=(BF16, F32, BF16), epilogue=out_epi,
                                 extras=(flat(x), mod, w["g_ffn"]),
                                 extra_specs=(None, _mod_spec(tmk, d, seq), vec_spec))
        x_mid, h2 = cube(x_mid), cube(h2)

        def act_epi(acc):
            r = jnp.maximum(acc, 0.0)
            return (r * r,)

        r = carrying(l, "fwd_ff1", _mm, flat(h2), w["ff"], dims="nn", name=f"l{l}_w_ff1", tm=tmw, tn=FLAT_W,
                     tk=d, out_dtypes=(BF16,), epilogue=act_epi, weights_outer=True, n=D_FF,
                     b_block=(chunk, lambda i, j, k: (j, 0, 0, 0)))

        more = l + 1 < DEPTH

        def ff2_epi(acc, xv, gm, *nxt):
            x_new = xv + gm[0][GATE2:GATE2 + 1, :] * acc
            return (acc, x_new) + ((normmod(x_new, nxt[1], nxt[0], SHIFT1, SCALE1),) if more else ())

        mod_spec = _mod_spec(tmt, d, seq)
        outs = carrying(l, "fwd_ff2", _mm, r, w["ff"], dims="nn", name=f"l{l}_w_ff2", tm=tmt, tn=d, tk=2 * FLAT_W,
                        out_dtypes=(BF16, F32) + ((BF16,) if more else ()), epilogue=ff2_epi,
                        extras=(flat(x_mid), mod) + ((mods[l + 1], plan.layer(l + 1)["g_mix"]) if more else ()),
                        extra_specs=(None, mod_spec) + ((mod_spec, vec_spec) if more else ()), n=d,
                        b_block=(chunk2, lambda i, j, k: (k, 1, 0, 0)))
        f, x_out = outs[0], outs[1]
        h1_next = cube(outs[2]) if more else None
        saved.append(dict(x_in=x, h1=h1, z=z, q=q, kv=kv, kp=kp, o=o, lse=lse, y=y, po=cube(po),
                          x_mid=x_mid, h2=h2, r=r, f=cube(f)))
        x = cube(x_out)

    grads = [dict() for _ in range(DEPTH)]
    dmods = [None] * DEPTH
    top = DEPTH - 1
    node = _resnode_bwd(x, final_g[None], name="final_loss_bwd", target3=target3,
                        branch3=saved[top]["f"], mod_gate=mods[top], gate_row=GATE2)
    loss_part = node["loss"][0, 0]
    d_final_g = node["dg"][0]
    plan.scalars(loss_part, d_final_g)
    for l in range(DEPTH - 1, -1, -1):
        w, mod, s = lw[l], mods[l], saved[l]
        dx_out, dfb, dgate2 = node["dx"], flat(node["dbr"]), node["dgate"][:, 0]

        def dact_epi(acc, rv):
            return (acc * (2.0 * jnp.sqrt(rv.astype(F32))),)

        da = carrying(l, "bwd_d_r", _mm, dfb, w["ff"], dims="nt", name=f"l{l}_d_r", tm=tmw, tn=FLAT_W, tk=d,
                      out_dtypes=(BF16,), epilogue=dact_epi, extras=(s["r"],), weights_outer=True, n=D_FF,
                      b_block=(chunk, lambda i, j, k: (j, 1, 0, 0)))
        g_ff = carrying(l, "bwd_dw_ff2", _mm, s["r"], dfb, dims="tn", name=f"l{l}_dw_ff2", tm=FLAT_W, tn=d,
                        tk=2048, out_into=(ff_grad_shape, (None, FLAT_W, FLAT_W), lambda i, j, k: (i, 1, 0), None))
        g_ff = carrying(l, "bwd_dw_ff1", _mm, flat(s["h2"]), da, dims="tn", name=f"l{l}_dw_ff1", tm=d, tn=FLAT_W,
                        tk=2048, out_into=(ff_grad_shape, (None, FLAT_W, FLAT_W), lambda i, j, k: (j, 0, 0), g_ff))
        plan.ff_grads(l, g_ff)
        dh2 = carrying(l, "bwd_d_h2", _mm, da, w["ff"], dims="nt", name=f"l{l}_d_h2", tm=tmk, tn=d, tk=2 * FLAT_W,
                       n=d, b_block=(chunk2, lambda i, j, k: (k, 0, 0, 0)), out_dtypes=(BF16,))
        node = carrying(l, "bwd_resnode_ffn", _resnode_bwd, s["x_mid"], w["g_ffn"], name=f"l{l}_resnode_ffn",
                        dh3=cube(dh2), dres3=dx_out, mod_nm=mod, rows=(SHIFT2, SCALE2), branch3=s["po"],
                        mod_gate=mod, gate_row=GATE1)
        grads[l]["norm_ffn_g"] = node["dg"][0]
        dshift2, dscale2 = node["dnm"][:, 0], node["dnm"][:, 1]
        dx_mid, dpo, dgate1 = node["dx"], flat(node["dbr"]), node["dgate"][:, 0]

        wy = s["y"].shape[-1]
        dy = cube(carrying(l, "bwd_d_y", _mm, dpo, w["w_out"], dims="nt", name=f"l{l}_d_y", tm=tmk, tn=wy, tk=d,
                           out_dtypes=(BF16,)))
        dw_out = _mm(flat(s["y"]), dpo, dims="tn", name=f"l{l}_dw_out", tm=wy // 3, tn=d, tk=2048)
        hw = HEADS * HEAD_PAD
        grads[l]["w_out"] = jnp.concatenate([dw_out[hw:], _unpad_value_lanes(dw_out[:hw], 0)], axis=0)

        dz, dws, dbs, dg_og = _gmlp_bwd(s["z"], dy, w["ws"], w["wst"], w["bexp"], w["g_og"],
                                        name=f"l{l}_gmlp_bwd", dy_col=hw // D_GMLP)
        grads[l]["gmlp_ws"], grads[l]["gmlp_bs"], grads[l]["out_norm_gmlp_g"] = dws, dbs, dg_og[0]

        do, dl, dg_oa = _onorm_bwd(s["o"], dy, w["g_oa"], name=f"l{l}_onorm_bwd")
        grads[l]["out_norm_mla_g"] = _unpad_value_lanes(dg_oa[0], 0)
        plan.small_ready(l, grads[l])
        dq, dk, dv = carrying(l, "bwd_attn_dkv", _attn_bwd, s["q"], s["kv"], s["kp"], do, s["lse"], dl,
                              name=f"l{l}_attn_bwd")
        dz, cq, dqb, ckv, dkvb, dg_q, dg_kv = _mla_prep_bwd(
            s["z"], dz, dq, dk, dv, w["g_q"], w["g_kv"], w["w_uq"], w["w_ukv"], ctab, stab,
            name=f"l{l}_mla_prep_bwd")
        grads[l]["mla_q_norm_g"], grads[l]["mla_kv_norm_g"] = dg_q[0], dg_kv[0]
        dw_uq = carrying(l, "bwd_dw_uq", _mm, flat(cq), flat(dqb), dims="tn", name=f"l{l}_dw_uq", tm=Q_RANK,
                         tn=1024, tk=4096)
        grads[l]["mla_w_uq"] = _unpad_heads(dw_uq, NOPE + ROPE)
        grads[l]["w_in_t"] = _unpad_w_in_t(carrying(l, "bwd_dw_in", _mm, flat(dz), flat(s["h1"]), dims="tn",
                                                    name=f"l{l}_dw_in", tm=D_IN_PAD // 2, tn=d, tk=2048))
        grads[l]["mla_w_ukv"] = carrying(l, "bwd_dw_ukv", _mm, flat(ckv), flat(dkvb), dims="tn", name=f"l{l}_dw_ukv",
                                         tm=KV_RANK, tn=1024, tk=4096)
        plan.layer_grads(l, grads[l])
        dh1 = carrying(l, "bwd_d_h1", _mm, flat(dz), w["w_in_t"], dims="nn", name=f"l{l}_d_h1", tm=tmk, tn=d,
                       tk=D_IN_PAD, out_dtypes=(BF16,))
        below = dict(branch3=saved[l - 1]["f"], mod_gate=mods[l - 1], gate_row=GATE2) if l > 0 else {}
        node = carrying(l, "bwd_resnode_mix", _resnode_bwd, s["x_in"], w["g_mix"], name=f"l{l}_resnode_mix",
                        dh3=cube(dh1), dres3=dx_mid, mod_nm=mod, rows=(SHIFT1, SCALE1), **below)
        grads[l]["norm_mix_g"] = node["dg"][0]
        dshift1, dscale1 = node["dnm"][:, 0], node["dnm"][:, 1]
        dmods[l] = jnp.stack([dshift1, dscale1, dgate1, dshift2, dscale2, dgate2], axis=1)
    return node["dx"], dmods


W_NAMES = ("w_ada", "b_ada", "norm_mix_g", "w_in", "gmlp_ws", "gmlp_bs", "mla_q_norm_g", "mla_kv_norm_g",
           "mla_w_uq", "mla_w_ukv", "out_norm_gmlp_g", "out_norm_mla_g", "w_out", "norm_ffn_g", "w_ff1", "w_ff2",
           "final_norm_g")
FLAT_KEY = {"w_in": "w_in", "w_uq": "mla_w_uq", "w_ukv": "mla_w_ukv", "w_out": "w_out", "w_ff1": "w_ff1",
            "w_ff2": "w_ff2"}
COL_SHARDED = ("w_in", "w_uq", "w_ukv", "w_ff1")
FULL_SHAPE = {"w_in": (D_MODEL, D_IN), "w_uq": (Q_RANK, HEADS * (NOPE + ROPE)), "w_ukv": (KV_RANK, HEADS * 128),
              "w_out": (D_MODEL, D_MODEL)}
SMALL_LAYER_NAMES = ("gmlp_ws", "gmlp_bs", "out_norm_gmlp_g", "out_norm_mla_g", "norm_ffn_g")
LATE_SMALL_NAMES = ("norm_mix_g", "mla_q_norm_g", "mla_kv_norm_g")


def _silu(v):
    return v * (1.0 / (1.0 + jnp.exp(-v)))


class _CommPlan:
    FWD = {"fwd_attn": ("ff", 0, "spread"), "fwd_out_a": ("ff", 0, "pass"),
           "fwd_ff1": ("mix", 1, "spread"), "fwd_ff2": ("mix", 1, "pass")}
    BWD = {"bwd_d_r": ("mix", 1), "bwd_dw_ff2": ("mix", 1),
           "bwd_d_h2": ("ff", 0), "bwd_attn_dkv": ("ff", 0), "bwd_dw_uq": ("ff", 0)}
    BWD_ALSO = {"bwd_d_h2": ("mix", 1)}
    BWD_LAST = {"bwd_d_h1": ("mix", 0), "bwd_resnode_mix": ("mix", 0)}
    SMALL = {"bwd_attn_dkv": "spread", "bwd_dw_uq": "pass"}

    def __init__(self, weights, ids, dev, core):
        self.weights, self.ids, self.dev, self.core = weights, ids, dev, core
        self.used, self.rows = _flat_rows()
        self.flat = {("mix", l): self._flat_mix(l) for l in range(DEPTH)}
        self.flat.update({("ff", l): jnp.concatenate([weights["w_ff1"][l], weights["w_ff2"][l]], axis=0).astype(BF16)
                          for l in range(DEPTH)})
        self.lw, self.rs, self.grads, self.spread = {}, {}, {}, {}
        self.small_vec, self.small_sum, self.small_spread, self.extra = {}, {}, None, {}
        self.lw = {l: _small_weights(weights, l) for l in range(DEPTH)}

    def _flat_mix(self, l):
        pieces = []
        for nm, _ in FSDP_SECTIONS:
            shard = self.weights[FLAT_KEY[nm]][l]
            pieces.append(shard.T if nm == "w_in" else shard.reshape(-1, FLAT_W))
        pieces.append(jnp.zeros((self.rows - self.used, FLAT_W), F32))
        return jnp.concatenate(pieces, axis=0).astype(BF16)

    def _arrived(self, group, l, gath):
        flat = self.flat[group, l]
        hr = flat.shape[0] // 2
        mine = lax.dynamic_slice(flat, (self.core * hr, 0), (hr, FLAT_W))
        gath = lax.dynamic_update_slice(gath, mine[None], (self.dev, 0, 0))
        if group == "ff":
            self.lw[l]["ff"] = gath.reshape(N_CHIPS, 2, hr, FLAT_W)
            return
        w_gath = gath.reshape(N_CHIPS, self.rows, FLAT_W)
        full, off = {}, 0
        for nm, nrows in FSDP_SECTIONS:
            sec = w_gath[:, off:off + nrows]
            off += nrows
            rows, cols = FULL_SHAPE[nm]
            if nm == "w_in":
                full["w_in_t"] = sec.reshape(cols, rows)
            else:
                full[FLAT_KEY[nm]] = (_chunks_to_cols(sec, rows, cols) if nm in COL_SHARDED
                                      else sec.reshape(rows, cols))
        self.lw[l].update(_mix_weights(full))

    def layer(self, l):
        return self.lw[l]

    def host(self, l, tag):
        if tag == "fwd_normmod1":
            return _gather_spread(self.flat["mix", 0]) if l == 0 else None
        if tag in self.FWD:
            group, ahead, what = self.FWD[tag]
            if l + ahead >= DEPTH:
                return None
            return _gather_spread(self.flat[group, l + ahead]) if what == "spread" else _gather_pass_on(self.spread[group])
        ex = None
        for rs in self._rs_for(l, tag):
            ex = rs.next_exchange() if ex is None else _both(ex, rs.next_exchange())
        if tag in self.SMALL:
            small = (_gather_spread(self.small_vec[l], halves=False) if self.SMALL[tag] == "spread"
                     else _gather_pass_on(self.small_spread))
            ex = small if ex is None else _both(ex, small)
        return ex

    def _rs_for(self, l, tag):
        found = []
        if tag in self.BWD_LAST and l == 0:
            found.append(self.rs.get(self.BWD_LAST[tag]))
        for table in (self.BWD, self.BWD_ALSO):
            if tag in table:
                group, ahead = table[tag]
                found.append(self.rs.get((group, l + ahead)))
        return [rs for rs in found if rs is not None and rs.stage <= 2]

    def hosted(self, l, tag, outs):
        if tag == "fwd_normmod1":
            self._arrived("mix", 0, _run_exchange(_gather_pass_on(outs[0]), name="l0_mix_gather_pass_on")[0])
        elif tag in self.FWD:
            group, ahead, what = self.FWD[tag]
            if what == "spread":
                self.spread[group] = outs[0]
            else:
                self._arrived(group, l + ahead, outs[0])
        else:
            for rs in self._rs_for(l, tag):
                rs.done(outs[:1])
                outs = outs[1:]
            if tag in self.SMALL:
                if self.SMALL[tag] == "spread":
                    self.small_spread = outs[0]
                else:
                    self._small_arrived(l, outs[0])

    def ff_grads(self, l, g_ff):
        self.rs["ff", l] = _ReduceScatter(g_ff, self.ids, f"l{l}_ff_rs")

    def layer_grads(self, l, grads):
        self.grads[l] = grads
        pieces = []
        for nm, nrows in FSDP_SECTIONS:
            if nm == "w_in":
                pieces.append(grads["w_in_t"].reshape(N_CHIPS, nrows, FLAT_W))
                continue
            g = grads[FLAT_KEY[nm]]
            pieces.append(_cols_to_chunks(g) if nm in COL_SHARDED else g.reshape(N_CHIPS, nrows, FLAT_W))
        pieces.append(jnp.zeros((N_CHIPS, self.rows - self.used, FLAT_W), F32))
        self.rs["mix", l] = _ReduceScatter(jnp.concatenate(pieces, axis=1), self.ids, f"l{l}_mix_rs")

    def scalars(self, loss_part, d_final_g):
        self.extra = {0: [loss_part[None]]}
        self.extra.setdefault(DEPTH - 1, []).insert(0, d_final_g)

    def small_ready(self, l, grads):
        parts = [grads[nm].reshape(-1) for nm in SMALL_LAYER_NAMES] + self.extra.get(l, [])
        vec = jnp.concatenate(parts)
        rows = -(-vec.shape[0] // (8 * FLAT_W)) * 8
        self.small_vec[l] = jnp.pad(vec, (0, rows * FLAT_W - vec.shape[0])).reshape(rows, FLAT_W)

    def _small_arrived(self, l, gath):
        gath = lax.dynamic_update_slice(gath, self.small_vec[l][None], (self.dev, 0, 0))
        self.small_sum[l] = _sum_leading(gath, name=f"l{l}_small_sum").reshape(-1)

    def finish(self, dmod):
        late = jnp.concatenate([jnp.stack([self.grads[l][nm] for l in range(DEPTH)], axis=0).reshape(-1)
                                for nm in LATE_SMALL_NAMES])
        head = -(-late.shape[0] // (8 * FLAT_W)) * 8
        late = jnp.pad(late, (0, head * FLAT_W - late.shape[0])).reshape(head, FLAT_W)
        vec = jnp.concatenate([late, dmod], axis=0)
        rs = self.rs["mix", 0]
        while rs.stage < 2:
            rs.done(_run_exchange(rs.next_exchange(), name=f"l0_mix_rs_stage{rs.stage}"))
        outs = _run_exchange(_both(rs.next_exchange(), _gather_spread(vec, halves=False)), name="final_spread")
        rs.done(outs[:1])
        (gath,) = _run_exchange(_gather_pass_on(outs[1]), name="final_pass_on")
        gath = lax.dynamic_update_slice(gath, vec[None], (self.dev, 0, 0))
        late_sum = _sum_leading(gath[:, :head], name="late_small_sum").reshape(-1)
        loss, res = self._small_grads(late_sum)
        return loss, res, gath[:, head:]

    def _small_grads(self, late):
        out = {nm: [] for nm in SMALL_LAYER_NAMES}
        for l in range(DEPTH):
            off = 0
            for nm in SMALL_LAYER_NAMES:
                size = self.weights[nm][l].size
                out[nm].append(self.small_sum[l][off:off + size].reshape(self.weights[nm].shape[1:]))
                off += size
            if l == DEPTH - 1:
                final = self.small_sum[l][off:off + self.weights["final_norm_g"].size]
                off += final.shape[0]
            if l == 0:
                loss = self.small_sum[l][off]
        res = {nm: jnp.stack(parts, axis=0) for nm, parts in out.items()}
        res["final_norm_g"] = final
        off = 0
        for nm in LATE_SMALL_NAMES:
            size = self.weights[nm].size
            res[nm] = late[off:off + size].reshape(self.weights[nm].shape)
            off += size
        return loss, res

    def mix_grads(self):
        per = {FLAT_KEY[nm]: [] for nm, _ in FSDP_SECTIONS}
        for l in range(DEPTH):
            shard, off = self.rs["mix", l].result, 0
            for nm, nrows in FSDP_SECTIONS:
                key = FLAT_KEY[nm]
                sec = shard[off:off + nrows]
                per[key].append(sec.T if nm == "w_in" else sec.reshape(self.weights[key].shape[1:]))
                off += nrows
        return {key: jnp.stack(parts, axis=0) for key, parts in per.items()}

    def ff_shards(self):
        return [self.rs["ff", l].result for l in range(DEPTH)]


def kernel(x, c, positions, w_ada, b_ada, norm_mix_g, w_in, gmlp_ws, gmlp_bs, mla_q_norm_g, mla_kv_norm_g, mla_w_uq, mla_w_ukv, out_norm_gmlp_g, out_norm_mla_g, w_out, norm_ffn_g, w_ff1, w_ff2, final_norm_g, loss_target, m_w_ada, m_b_ada, m_norm_mix_g, m_w_in, m_gmlp_ws, m_gmlp_bs, m_mla_q_norm_g, m_mla_kv_norm_g, m_mla_w_uq, m_mla_w_ukv, m_out_norm_gmlp_g, m_out_norm_mla_g, m_w_out, m_norm_ffn_g, m_w_ff1, m_w_ff2, m_final_norm_g, v_w_ada, v_b_ada, v_norm_mix_g, v_w_in, v_gmlp_ws, v_gmlp_bs, v_mla_q_norm_g, v_mla_kv_norm_g, v_mla_w_uq, v_mla_w_ukv, v_out_norm_gmlp_g, v_out_norm_mla_g, v_w_out, v_norm_ffn_g, v_w_ff1, v_w_ff2, v_final_norm_g):
    weights = dict(w_ada=w_ada, b_ada=b_ada, norm_mix_g=norm_mix_g, w_in=w_in, gmlp_ws=gmlp_ws, gmlp_bs=gmlp_bs,
                   mla_q_norm_g=mla_q_norm_g, mla_kv_norm_g=mla_kv_norm_g, mla_w_uq=mla_w_uq, mla_w_ukv=mla_w_ukv,
                   out_norm_gmlp_g=out_norm_gmlp_g, out_norm_mla_g=out_norm_mla_g, w_out=w_out,
                   norm_ffn_g=norm_ffn_g, w_ff1=w_ff1, w_ff2=w_ff2, final_norm_g=final_norm_g)
    mom_m = dict(zip(W_NAMES, (m_w_ada, m_b_ada, m_norm_mix_g, m_w_in, m_gmlp_ws, m_gmlp_bs, m_mla_q_norm_g,
                               m_mla_kv_norm_g, m_mla_w_uq, m_mla_w_ukv, m_out_norm_gmlp_g, m_out_norm_mla_g,
                               m_w_out, m_norm_ffn_g, m_w_ff1, m_w_ff2, m_final_norm_g)))
    mom_v = dict(zip(W_NAMES, (v_w_ada, v_b_ada, v_norm_mix_g, v_w_in, v_gmlp_ws, v_gmlp_bs, v_mla_q_norm_g,
                               v_mla_kv_norm_g, v_mla_w_uq, v_mla_w_ukv, v_out_norm_gmlp_g, v_out_norm_mla_g,
                               v_w_out, v_norm_ffn_g, v_w_ff1, v_w_ff2, v_final_norm_g)))
    bsz, seq, d = x.shape
    px, py, pc = _position()
    chip = 2 * px + py
    dev = 2 * chip + pc
    ids = jnp.stack([pc, chip]).astype(jnp.int32)
    n_ex = N_DEV * bsz
    ada_cols = w_ada.shape[-1]

    c_all = _allgather8(c.reshape(bsz * d // 128, 128), name="gather_c").reshape(n_ex, d)
    mod_parts = []
    for l in range(DEPTH):
        bias = lax.dynamic_slice(b_ada[l], (chip * ada_cols,), (ada_cols,))[None]
        mod_parts.append(_mm(c_all, w_ada, dims="nn", name=f"l{l}_mod", tm=n_ex, tn=ada_cols, tk=d, n=ada_cols,
                             b_block=((None, d, ada_cols), lambda i, j, k, l=l: (l, k, j)),
                             epilogue=lambda acc, bv: (acc + bv,), extras=(bias,),
                             extra_specs=(pl.BlockSpec((1, ada_cols), lambda i, j, k: (0, j)),), a_fn=_silu))
    mod_g = _allgather8(jnp.concatenate(mod_parts, axis=0), name="gather_mod")
    mod_g = mod_g.reshape(N_CHIPS, 2, DEPTH, n_ex, ada_cols)[:, 0]
    mod_full = mod_g.transpose(1, 2, 0, 3).reshape(DEPTH, n_ex, N_CHIPS * ada_cols)
    mod_mine = lax.dynamic_slice(mod_full, (0, dev * bsz, 0), (DEPTH, bsz, N_MOD * d))
    mod_mine = jnp.pad(mod_mine.reshape(DEPTH, bsz, N_MOD, d), ((0, 0), (0, 0), (0, MOD_ROWS - N_MOD), (0, 0)))
    mods = [mod_mine[l] for l in range(DEPTH)]

    plan = _CommPlan(weights, ids, dev, pc)
    grad_x, dmods = _local_step(x, loss_target, positions, mods, final_norm_g, plan)

    dmod = jnp.stack(dmods, axis=1).reshape(bsz * DEPTH * N_MOD, d)
    loss, small, dmod_all = plan.finish(dmod)
    grad = plan.mix_grads()
    grad.update(small)
    dmod_all = dmod_all.reshape(n_ex, DEPTH, N_MOD * d)
    gw, gb = [], []
    for l in range(DEPTH):
        dm = dmod_all[:, l]
        dm_cols = lax.dynamic_slice(dm, (0, chip * ada_cols), (n_ex, ada_cols))
        gw.append(_mm(c_all, dm_cols, dims="tn", name=f"l{l}_dw_ada", tm=d, tn=ada_cols, tk=n_ex, a_fn=_silu,
                      out_into=(w_ada.shape, (None, d, ada_cols), lambda i, j, k, l=l: (l, i, j),
                                gw[-1] if gw else None)))
        gb.append(_sum_leading(dm.reshape(n_ex, N_MOD * d // FLAT_W, FLAT_W), name=f"l{l}_db_ada").reshape(-1))
    grad["w_ada"] = gw[-1]
    grad["b_ada"] = jnp.stack(gb, axis=0)

    delta, new_m, new_v = {}, {}, {}
    ff_bufs = plan.ff_shards()
    for nm, row_off in (("w_ff1", 0), ("w_ff2", FLAT_W)):
        grad[nm], delta[nm], new_m[nm], new_v[nm] = _adamw_layers(
            weights[nm], mom_m[nm], mom_v[nm], ff_bufs, row_off, name=f"adamw_{nm}")
    for nm in W_NAMES:
        if nm not in delta:
            delta[nm], new_m[nm], new_v[nm] = _adamw(weights[nm], grad[nm], mom_m[nm], mom_v[nm],
                                                     name=f"adamw_{nm}")
    return (loss, grad_x, *[grad[nm] for nm in W_NAMES], *[delta[nm] for nm in W_NAMES],
            *[new_m[nm] for nm in W_NAMES], *[new_v[nm] for nm in W_NAMES])
```
